```python
import math
import jax, jax.numpy as jnp
from jax import lax
import numpy as np

D_MODEL = 1024
BATCH = 8
SEQ = 8192
DEPTH = 1

HG_HEADS = 4
HG_DIM = 128
HG_WIDTH = HG_HEADS * HG_DIM
CHUNK = 64
ATT_Q_HEADS = 8
ATT_KV_HEADS = 2
ATT_HEAD_DIM = 64
ATT_GROUP = ATT_Q_HEADS // ATT_KV_HEADS
ATT_WIDTH = ATT_Q_HEADS * ATT_HEAD_DIM
KV_WIDTH = ATT_KV_HEADS * ATT_HEAD_DIM
WINDOW = 128
BLOCK = 128
NUM_BUCKETS = 32
MAX_DISTANCE = 128
D_FF = 2816
N_MOD = 9
EPS = 1e-6
MIX_WIDTH = HG_WIDTH + ATT_WIDTH
IN_SPLITS = (HG_WIDTH, HG_WIDTH, HG_WIDTH, HG_WIDTH, HG_WIDTH, ATT_WIDTH, KV_WIDTH, KV_WIDTH)
D_IN = sum(IN_SPLITS)

kernel_name = "hybrid_hgrn2_swa_macaron_encoder"


def rmsnorm(x, g):
    x32 = x.astype(jnp.float32)
    y = x32 * lax.rsqrt(jnp.mean(x32 * x32, axis=-1, keepdims=True) + EPS)
    return (y * g.astype(jnp.float32)).astype(x.dtype)


def modulate(h, shift, scale):
    return h * (1.0 + scale[:, None, :]) + shift[:, None, :]


def swiglu(h, w_in, w_out):
    gate, up = jnp.split(h @ w_in, 2, axis=-1)
    return (jax.nn.silu(gate) * up) @ w_out


def split_columns(z):
    outs, off = [], 0
    for n in IN_SPLITS:
        outs.append(z[..., off:off + n])
        off += n
    return outs


def t5_bucket(rel):
    nb = NUM_BUCKETS // 2
    max_exact = nb // 2
    ret = (rel > 0).astype(jnp.int32) * nb
    n = jnp.abs(rel)
    large = max_exact + (jnp.log(jnp.maximum(n, 1).astype(jnp.float32) / max_exact)
                         / math.log(MAX_DISTANCE / max_exact) * (nb - max_exact)).astype(jnp.int32)
    large = jnp.minimum(large, nb - 1)
    return ret + jnp.where(n < max_exact, n, large)


def hgrn2_direction(q, k, v, g):
    B, H, S, DK = q.shape
    DV = v.shape[-1]
    n = S // CHUNK

    def to_chunks(t):
        return jnp.moveaxis(t.reshape(B, H, n, CHUNK, t.shape[-1]), 2, 0)

    causal = jnp.tril(jnp.ones((CHUNK, CHUNK), dtype=bool))[:, :, None]

    def step(state, inp):
        q_c, k_c, v_c, g_c = inp
        G = jnp.cumsum(g_c, axis=2)
        o_inter = jnp.einsum('bhtd,bhde->bhte', q_c * jnp.exp(G), state)
        diff = G[:, :, :, None, :] - G[:, :, None, :, :]
        decay = jnp.where(causal, jnp.exp(jnp.where(causal, diff, 0.0)), 0.0)
        A = jnp.einsum('bhtd,bhsd,bhtsd->bhts', q_c, k_c, decay)
        o_intra = jnp.einsum('bhts,bhse->bhte', A, v_c)
        G_last = G[:, :, -1:, :]
        k_dec = k_c * jnp.exp(G_last - G)
        new_state = jnp.exp(G_last[:, :, 0, :])[..., None] * state + jnp.einsum('bhsd,bhse->bhde', k_dec, v_c)
        return new_state, o_inter + o_intra

    s0 = jnp.zeros((B, H, DK, DV), jnp.float32)
    _, o = lax.scan(step, s0, (to_chunks(q), to_chunks(k), to_chunks(v), to_chunks(g)))
    return jnp.moveaxis(o, 0, 2).reshape(B, H, S, DV)


def hgrn2_group(q_raw, ff_raw, fb_raw, i_raw, g_raw, lb, norm_g):
    B, S, _ = q_raw.shape

    def heads(t):
        return t.astype(jnp.float32).reshape(B, S, HG_HEADS, HG_DIM).transpose(0, 2, 1, 3)

    q = heads(jax.nn.silu(q_raw))
    v = heads(i_raw)
    lb = lb.astype(jnp.float32)
    outs = []
    for d, f_raw in enumerate((ff_raw, fb_raw)):
        fr = heads(f_raw)
        lb_d = lb[d].reshape(1, HG_HEADS, 1, HG_DIM)
        log_f = jnp.logaddexp(jnp.log(lb_d), jnp.log1p(-lb_d) + jax.nn.log_sigmoid(fr))
        k = (1.0 - lb_d) * jax.nn.sigmoid(-fr)
        if d == 0:
            outs.append(hgrn2_direction(q, k, v, log_f))
        else:
            flip = lambda t: jnp.flip(t, axis=2)
            outs.append(flip(hgrn2_direction(flip(q), flip(k), flip(v), flip(log_f))))
    o = (outs[0] + outs[1]).transpose(0, 2, 1, 3)
    o = o * lax.rsqrt(jnp.mean(o * o, axis=-1, keepdims=True) + EPS)
    o = o.reshape(B, S, HG_WIDTH) * norm_g.astype(jnp.float32)
    return (o * jax.nn.silu(g_raw.astype(jnp.float32))).astype(q_raw.dtype)


def windowed_gqa(q_raw, k_raw, v_raw, q_g, k_g, sink, rel_bias):
    B, S, _ = q_raw.shape
    nb = S // BLOCK
    q = rmsnorm(q_raw.reshape(B, S, ATT_Q_HEADS, ATT_HEAD_DIM), q_g)
    k = rmsnorm(k_raw.reshape(B, S, ATT_KV_HEADS, ATT_HEAD_DIM), k_g)
    v = v_raw.reshape(B, S, ATT_KV_HEADS, ATT_HEAD_DIM)
    qb = q.reshape(B, nb, BLOCK, ATT_KV_HEADS, ATT_GROUP, ATT_HEAD_DIM)

    def band(t):
        tp = jnp.pad(t, ((0, 0), (BLOCK, BLOCK), (0, 0), (0, 0))).reshape(B, nb + 2, BLOCK, ATT_KV_HEADS, ATT_HEAD_DIM)
        return jnp.concatenate([tp[:, :-2], tp[:, 1:-1], tp[:, 2:]], axis=2)

    kw, vw = band(k), band(v)
    logits = jnp.einsum('bnqhgd,bnkhd->bnhgqk', qb, kw).astype(jnp.float32) / math.sqrt(ATT_HEAD_DIM)
    rel = (jnp.arange(3 * BLOCK)[None, :] - BLOCK) - jnp.arange(BLOCK)[:, None]
    bias = rel_bias.astype(jnp.float32)[t5_bucket(rel)]
    bias = bias.reshape(BLOCK, 3 * BLOCK, ATT_KV_HEADS, ATT_GROUP).transpose(2, 3, 0, 1)
    key_pos = jnp.arange(nb)[:, None] * BLOCK + jnp.arange(3 * BLOCK)[None, :] - BLOCK
    valid = (jnp.abs(rel) <= WINDOW)[None] & ((key_pos >= 0) & (key_pos < S))[:, None, :]
    logits = jnp.where(valid[None, :, None, None], logits + bias, -jnp.inf)
    sink_b = sink.astype(jnp.float32).reshape(ATT_KV_HEADS, ATT_GROUP, 1, 1)
    m = jnp.maximum(jnp.max(logits, axis=-1, keepdims=True), sink_b)
    p = jnp.exp(logits - m)
    p = p / (jnp.sum(p, axis=-1, keepdims=True) + jnp.exp(sink_b - m))
    o = jnp.einsum('bnhgqk,bnkhd->bnqhgd', p.astype(vw.dtype), vw)
    return o.reshape(B, S, ATT_WIDTH)


def _fwd_setup_inputs(seed: int = 0) -> dict:
    key = jax.random.key(seed)
    ks = jax.random.split(key, 16)
    L, D, F = DEPTH, D_MODEL, D_FF
    nrm = jax.random.normal
    return {
        "x": nrm(ks[0], (BATCH, SEQ, D), jnp.float32),
        "c": nrm(ks[1], (BATCH, D), jnp.float32),
        "w_ada": nrm(ks[2], (L, D, N_MOD * D), jnp.float32) * 0.02,
        "b_ada": nrm(ks[3], (L, N_MOD * D), jnp.float32) * 0.02,
        "norm_g": 1.0 + 0.02 * nrm(ks[4], (L, 3, D), jnp.float32),
        "w_ffn1_in": nrm(ks[5], (L, D, 2 * F), jnp.float32) * D ** -0.5,
        "w_ffn1_out": nrm(ks[6], (L, F, D), jnp.float32) * F ** -0.5,
        "w_ffn2_in": nrm(ks[7], (L, D, 2 * F), jnp.float32) * D ** -0.5,
        "w_ffn2_out": nrm(ks[8], (L, F, D), jnp.float32) * F ** -0.5,
        "w_mix_in": nrm(ks[9], (L, D, D_IN), jnp.float32) * D ** -0.5,
        "w_mix_out": nrm(ks[10], (L, MIX_WIDTH, D), jnp.float32) * MIX_WIDTH ** -0.5,
        "hgrn_lb": nrm(ks[11], (2, L + 1, HG_WIDTH), jnp.float32),
        "hgrn_norm_g": 1.0 + 0.02 * nrm(ks[12], (L, HG_WIDTH), jnp.float32),
        "qk_norm_g": 1.0 + 0.02 * nrm(ks[13], (L, 2, ATT_HEAD_DIM), jnp.float32),
        "attn_sink": 0.5 * nrm(ks[14], (L, ATT_Q_HEADS), jnp.float32),
        "rel_bias": 0.5 * nrm(ks[15], (NUM_BUCKETS, ATT_Q_HEADS), jnp.float32),
    }


def _fwd_reference(x, c, w_ada, b_ada, norm_g, w_ffn1_in, w_ffn1_out, w_ffn2_in, w_ffn2_out,
              w_mix_in, w_mix_out, hgrn_lb, hgrn_norm_g, qk_norm_g, attn_sink, rel_bias):
    lb_all = jnp.cumsum(jax.nn.softmax(hgrn_lb.astype(jnp.float32), axis=1), axis=1)
    c_act = jax.nn.silu(c)
    for l in range(DEPTH):
        mods = jnp.split(c_act @ w_ada[l] + b_ada[l], N_MOD, axis=-1)
        sh1, sc1, g1, sh2, sc2, g2, sh3, sc3, g3 = mods
        h = modulate(rmsnorm(x, norm_g[l, 0]), sh1, sc1)
        x = x + 0.5 * g1[:, None, :] * swiglu(h, w_ffn1_in[l], w_ffn1_out[l])
        h = modulate(rmsnorm(x, norm_g[l, 1]), sh2, sc2)
        hq, hf_f, hf_b, hi, hg, aq, ak, av = split_columns(h @ w_mix_in[l])
        o_hgrn = hgrn2_group(hq, hf_f, hf_b, hi, hg, lb_all[:, l], hgrn_norm_g[l])
        o_attn = windowed_gqa(aq, ak, av, qk_norm_g[l, 0], qk_norm_g[l, 1], attn_sink[l], rel_bias)
        mixed = jnp.concatenate([o_hgrn, o_attn.astype(o_hgrn.dtype)], axis=-1) @ w_mix_out[l]
        x = x + g2[:, None, :] * mixed
        h = modulate(rmsnorm(x, norm_g[l, 2]), sh3, sc3)
        x = x + 0.5 * g3[:, None, :] * swiglu(h, w_ffn2_in[l], w_ffn2_out[l])
    return x


import jax as _jax
import jax.numpy as _jnp

TWIN_FORMAT = 'train_step'
FWD_PARAMS = ['x', 'c', 'w_ada', 'b_ada', 'norm_g', 'w_ffn1_in', 'w_ffn1_out', 'w_ffn2_in', 'w_ffn2_out', 'w_mix_in', 'w_mix_out', 'hgrn_lb', 'hgrn_norm_g', 'qk_norm_g', 'attn_sink', 'rel_bias']
TWIN_WEIGHTS = ['w_ada', 'b_ada', 'norm_g', 'w_ffn1_in', 'w_ffn1_out', 'w_ffn2_in', 'w_ffn2_out', 'w_mix_in', 'w_mix_out', 'hgrn_lb', 'hgrn_norm_g', 'qk_norm_g', 'attn_sink', 'rel_bias']
TWIN_DIFF_INPUT = 'x'
TWIN_INPUTS = ['x', 'c', 'w_ada', 'b_ada', 'norm_g', 'w_ffn1_in', 'w_ffn1_out', 'w_ffn2_in', 'w_ffn2_out', 'w_mix_in', 'w_mix_out', 'hgrn_lb', 'hgrn_norm_g', 'qk_norm_g', 'attn_sink', 'rel_bias', 'loss_target', 'm_w_ada', 'm_b_ada', 'm_norm_g', 'm_w_ffn1_in', 'm_w_ffn1_out', 'm_w_ffn2_in', 'm_w_ffn2_out', 'm_w_mix_in', 'm_w_mix_out', 'm_hgrn_lb', 'm_hgrn_norm_g', 'm_qk_norm_g', 'm_attn_sink', 'm_rel_bias', 'v_w_ada', 'v_b_ada', 'v_norm_g', 'v_w_ffn1_in', 'v_w_ffn1_out', 'v_w_ffn2_in', 'v_w_ffn2_out', 'v_w_mix_in', 'v_w_mix_out', 'v_hgrn_lb', 'v_hgrn_norm_g', 'v_qk_norm_g', 'v_attn_sink', 'v_rel_bias']
TWIN_OUTPUTS = ['loss', 'grad_x', 'grad_w_ada', 'grad_b_ada', 'grad_norm_g', 'grad_w_ffn1_in', 'grad_w_ffn1_out', 'grad_w_ffn2_in', 'grad_w_ffn2_out', 'grad_w_mix_in', 'grad_w_mix_out', 'grad_hgrn_lb', 'grad_hgrn_norm_g', 'grad_qk_norm_g', 'grad_attn_sink', 'grad_rel_bias', 'delta_w_ada', 'delta_b_ada', 'delta_norm_g', 'delta_w_ffn1_in', 'delta_w_ffn1_out', 'delta_w_ffn2_in', 'delta_w_ffn2_out', 'delta_w_mix_in', 'delta_w_mix_out', 'delta_hgrn_lb', 'delta_hgrn_norm_g', 'delta_qk_norm_g', 'delta_attn_sink', 'delta_rel_bias', 'new_m_w_ada', 'new_m_b_ada', 'new_m_norm_g', 'new_m_w_ffn1_in', 'new_m_w_ffn1_out', 'new_m_w_ffn2_in', 'new_m_w_ffn2_out', 'new_m_w_mix_in', 'new_m_w_mix_out', 'new_m_hgrn_lb', 'new_m_hgrn_norm_g', 'new_m_qk_norm_g', 'new_m_attn_sink', 'new_m_rel_bias', 'new_v_w_ada', 'new_v_b_ada', 'new_v_norm_g', 'new_v_w_ffn1_in', 'new_v_w_ffn1_out', 'new_v_w_ffn2_in', 'new_v_w_ffn2_out', 'new_v_w_mix_in', 'new_v_w_mix_out', 'new_v_hgrn_lb', 'new_v_hgrn_norm_g', 'new_v_qk_norm_g', 'new_v_attn_sink', 'new_v_rel_bias']
TWIN_LEAF_KINDS = {'loss': 'loss', 'grad_x': 'grad_x', 'grad_w_ada': 'grad_w', 'grad_b_ada': 'grad_w', 'grad_norm_g': 'grad_w', 'grad_w_ffn1_in': 'grad_w', 'grad_w_ffn1_out': 'grad_w', 'grad_w_ffn2_in': 'grad_w', 'grad_w_ffn2_out': 'grad_w', 'grad_w_mix_in': 'grad_w', 'grad_w_mix_out': 'grad_w', 'grad_hgrn_lb': 'grad_w', 'grad_hgrn_norm_g': 'grad_w', 'grad_qk_norm_g': 'grad_w', 'grad_attn_sink': 'grad_w', 'grad_rel_bias': 'grad_w', 'delta_w_ada': 'delta_w', 'delta_b_ada': 'delta_w', 'delta_norm_g': 'delta_w', 'delta_w_ffn1_in': 'delta_w', 'delta_w_ffn1_out': 'delta_w', 'delta_w_ffn2_in': 'delta_w', 'delta_w_ffn2_out': 'delta_w', 'delta_w_mix_in': 'delta_w', 'delta_w_mix_out': 'delta_w', 'delta_hgrn_lb': 'delta_w', 'delta_hgrn_norm_g': 'delta_w', 'delta_qk_norm_g': 'delta_w', 'delta_attn_sink': 'delta_w', 'delta_rel_bias': 'delta_w', 'new_m_w_ada': 'new_m', 'new_m_b_ada': 'new_m', 'new_m_norm_g': 'new_m', 'new_m_w_ffn1_in': 'new_m', 'new_m_w_ffn1_out': 'new_m', 'new_m_w_ffn2_in': 'new_m', 'new_m_w_ffn2_out': 'new_m', 'new_m_w_mix_in': 'new_m', 'new_m_w_mix_out': 'new_m', 'new_m_hgrn_lb': 'new_m', 'new_m_hgrn_norm_g': 'new_m', 'new_m_qk_norm_g': 'new_m', 'new_m_attn_sink': 'new_m', 'new_m_rel_bias': 'new_m', 'new_v_w_ada': 'new_v', 'new_v_b_ada': 'new_v', 'new_v_norm_g': 'new_v', 'new_v_w_ffn1_in': 'new_v', 'new_v_w_ffn1_out': 'new_v', 'new_v_w_ffn2_in': 'new_v', 'new_v_w_ffn2_out': 'new_v', 'new_v_w_mix_in': 'new_v', 'new_v_w_mix_out': 'new_v', 'new_v_hgrn_lb': 'new_v', 'new_v_hgrn_norm_g': 'new_v', 'new_v_qk_norm_g': 'new_v', 'new_v_attn_sink': 'new_v', 'new_v_rel_bias': 'new_v'}


def _forward(args):
    return _fwd_reference(*[args[k] for k in FWD_PARAMS])


def _output_shape():
    def fwd():
        inp = _fwd_setup_inputs(0)
        return _fwd_reference(*[inp[k] for k in FWD_PARAMS])
    out = _jax.eval_shape(fwd)
    return out.shape, out.dtype

N_MICROBATCH = 1
ADAM_LR = 0.001
ADAM_B1 = 0.9
ADAM_B2 = 0.999
ADAM_EPS = 1e-08
ADAM_WD = 0.01
ADAM_STEP = 10
PER_EXAMPLE_BATCH_AXIS = {'x': 0, 'c': 0, 'loss_target': 0}
SHARED_INPUTS = []
_WEIGHT_DTYPES = {'w_ada': _jnp.float32, 'b_ada': _jnp.float32, 'norm_g': _jnp.float32, 'w_ffn1_in': _jnp.float32, 'w_ffn1_out': _jnp.float32, 'w_ffn2_in': _jnp.float32, 'w_ffn2_out': _jnp.float32, 'w_mix_in': _jnp.float32, 'w_mix_out': _jnp.float32, 'hgrn_lb': _jnp.float32, 'hgrn_norm_g': _jnp.float32, 'qk_norm_g': _jnp.float32, 'attn_sink': _jnp.float32, 'rel_bias': _jnp.float32}
MOMENT_SCALE = {'w_ada': 1.091678e+00, 'b_ada': 2.615580e+00, 'norm_g': 2.826659e+00, 'w_ffn1_in': 6.446876e-02, 'w_ffn1_out': 9.873109e-02, 'w_ffn2_in': 6.302779e-02, 'w_ffn2_out': 8.713005e-02, 'w_mix_in': 6.454593e-01, 'w_mix_out': 6.126188e-01, 'hgrn_lb': 5.649213e-03, 'hgrn_norm_g': 5.140556e+00, 'qk_norm_g': 7.455636e-01, 'attn_sink': 2.760348e-01, 'rel_bias': 1.598521e-01}


def _to_microbatches(a, axis):
    t = _jnp.moveaxis(a, axis, 0)
    t = t.reshape((N_MICROBATCH, t.shape[0] // N_MICROBATCH) + t.shape[1:])
    return _jnp.moveaxis(t, 1, axis + 1)


def setup_inputs(seed: int = 0) -> dict:
    inp = _fwd_setup_inputs(seed)
    key = _jax.random.fold_in(_jax.random.key(seed), 7919)
    shape, _ = _output_shape()
    out = dict(inp)
    out["loss_target"] = _jax.random.normal(_jax.random.fold_in(key, 0), shape, _jnp.float32)
    for i, name in enumerate(TWIN_WEIGHTS):
        w = inp[name].astype(_jnp.float32)
        if MOMENT_SCALE is None:
            s = _jnp.sqrt(_jnp.mean(_jnp.square(w)) + 1e-30)
        else:
            s = MOMENT_SCALE[name]
        km, kv = _jax.random.split(_jax.random.fold_in(key, i + 1))
        out[name] = w
        out["m_" + name] = s * _jax.random.normal(km, w.shape, _jnp.float32)
        out["v_" + name] = (s * s) * _jax.random.uniform(kv, w.shape, _jnp.float32, 0.5, 1.5)
    if N_MICROBATCH > 1:
        for name, axis in PER_EXAMPLE_BATCH_AXIS.items():
            out[name] = _to_microbatches(out[name], axis)
    return {'x': out['x'], 'c': out['c'], 'w_ada': out['w_ada'], 'b_ada': out['b_ada'], 'norm_g': out['norm_g'], 'w_ffn1_in': out['w_ffn1_in'], 'w_ffn1_out': out['w_ffn1_out'], 'w_ffn2_in': out['w_ffn2_in'], 'w_ffn2_out': out['w_ffn2_out'], 'w_mix_in': out['w_mix_in'], 'w_mix_out': out['w_mix_out'], 'hgrn_lb': out['hgrn_lb'], 'hgrn_norm_g': out['hgrn_norm_g'], 'qk_norm_g': out['qk_norm_g'], 'attn_sink': out['attn_sink'], 'rel_bias': out['rel_bias'], 'loss_target': out['loss_target'], 'm_w_ada': out['m_w_ada'], 'm_b_ada': out['m_b_ada'], 'm_norm_g': out['m_norm_g'], 'm_w_ffn1_in': out['m_w_ffn1_in'], 'm_w_ffn1_out': out['m_w_ffn1_out'], 'm_w_ffn2_in': out['m_w_ffn2_in'], 'm_w_ffn2_out': out['m_w_ffn2_out'], 'm_w_mix_in': out['m_w_mix_in'], 'm_w_mix_out': out['m_w_mix_out'], 'm_hgrn_lb': out['m_hgrn_lb'], 'm_hgrn_norm_g': out['m_hgrn_norm_g'], 'm_qk_norm_g': out['m_qk_norm_g'], 'm_attn_sink': out['m_attn_sink'], 'm_rel_bias': out['m_rel_bias'], 'v_w_ada': out['v_w_ada'], 'v_b_ada': out['v_b_ada'], 'v_norm_g': out['v_norm_g'], 'v_w_ffn1_in': out['v_w_ffn1_in'], 'v_w_ffn1_out': out['v_w_ffn1_out'], 'v_w_ffn2_in': out['v_w_ffn2_in'], 'v_w_ffn2_out': out['v_w_ffn2_out'], 'v_w_mix_in': out['v_w_mix_in'], 'v_w_mix_out': out['v_w_mix_out'], 'v_hgrn_lb': out['v_hgrn_lb'], 'v_hgrn_norm_g': out['v_hgrn_norm_g'], 'v_qk_norm_g': out['v_qk_norm_g'], 'v_attn_sink': out['v_attn_sink'], 'v_rel_bias': out['v_rel_bias']}


def _loss(weights, diff, rest, loss_target):
    with _jax.named_scope("forward"):
        args = {**rest, TWIN_DIFF_INPUT: diff, **{k: w.astype(_WEIGHT_DTYPES[k]) for k, w in weights.items()}}
        y = _forward(args)
    with _jax.named_scope("loss_head"):
        err = _jnp.square(y.astype(_jnp.float32) - loss_target)
        return 0.5 * _jnp.sum(_jnp.mean(err, axis=-1)) if err.ndim else 0.5 * err


def _adamw(w, g, m, v):
    m = ADAM_B1 * m + (1.0 - ADAM_B1) * g
    v = ADAM_B2 * v + (1.0 - ADAM_B2) * _jnp.square(g)
    m_hat = m / (1.0 - ADAM_B1 ** ADAM_STEP)
    v_hat = v / (1.0 - ADAM_B2 ** ADAM_STEP)
    delta = -ADAM_LR * (m_hat / (_jnp.sqrt(v_hat) + ADAM_EPS) + ADAM_WD * w)
    return delta, m, v


def reference(x, c, w_ada, b_ada, norm_g, w_ffn1_in, w_ffn1_out, w_ffn2_in, w_ffn2_out, w_mix_in, w_mix_out, hgrn_lb, hgrn_norm_g, qk_norm_g, attn_sink, rel_bias, loss_target, m_w_ada, m_b_ada, m_norm_g, m_w_ffn1_in, m_w_ffn1_out, m_w_ffn2_in, m_w_ffn2_out, m_w_mix_in, m_w_mix_out, m_hgrn_lb, m_hgrn_norm_g, m_qk_norm_g, m_attn_sink, m_rel_bias, v_w_ada, v_b_ada, v_norm_g, v_w_ffn1_in, v_w_ffn1_out, v_w_ffn2_in, v_w_ffn2_out, v_w_mix_in, v_w_mix_out, v_hgrn_lb, v_hgrn_norm_g, v_qk_norm_g, v_attn_sink, v_rel_bias):
    given = dict(x=x, c=c, w_ada=w_ada, b_ada=b_ada, norm_g=norm_g, w_ffn1_in=w_ffn1_in, w_ffn1_out=w_ffn1_out, w_ffn2_in=w_ffn2_in, w_ffn2_out=w_ffn2_out, w_mix_in=w_mix_in, w_mix_out=w_mix_out, hgrn_lb=hgrn_lb, hgrn_norm_g=hgrn_norm_g, qk_norm_g=qk_norm_g, attn_sink=attn_sink, rel_bias=rel_bias, loss_target=loss_target, m_w_ada=m_w_ada, m_b_ada=m_b_ada, m_norm_g=m_norm_g, m_w_ffn1_in=m_w_ffn1_in, m_w_ffn1_out=m_w_ffn1_out, m_w_ffn2_in=m_w_ffn2_in, m_w_ffn2_out=m_w_ffn2_out, m_w_mix_in=m_w_mix_in, m_w_mix_out=m_w_mix_out, m_hgrn_lb=m_hgrn_lb, m_hgrn_norm_g=m_hgrn_norm_g, m_qk_norm_g=m_qk_norm_g, m_attn_sink=m_attn_sink, m_rel_bias=m_rel_bias, v_w_ada=v_w_ada, v_b_ada=v_b_ada, v_norm_g=v_norm_g, v_w_ffn1_in=v_w_ffn1_in, v_w_ffn1_out=v_w_ffn1_out, v_w_ffn2_in=v_w_ffn2_in, v_w_ffn2_out=v_w_ffn2_out, v_w_mix_in=v_w_mix_in, v_w_mix_out=v_w_mix_out, v_hgrn_lb=v_hgrn_lb, v_hgrn_norm_g=v_hgrn_norm_g, v_qk_norm_g=v_qk_norm_g, v_attn_sink=v_attn_sink, v_rel_bias=v_rel_bias)
    weights = {n: given[n] for n in TWIN_WEIGHTS}
    shared = {n: given[n] for n in SHARED_INPUTS}
    per_example = {n: given[n] for n in ['x', 'c']}
    grad_fn = _jax.value_and_grad(_loss, argnums=(0, 1))

    def one_microbatch(ex, loss_target):
        ex = dict(ex)
        diff = ex.pop(TWIN_DIFF_INPUT)
        return grad_fn(weights, diff, {**shared, **ex}, loss_target)

    if N_MICROBATCH == 1:
        loss, (grad_w, grad_x) = one_microbatch(per_example, given["loss_target"])
    else:
        def body(carry, xs):
            loss_sum, grad_sum = carry
            l_k, (gw_k, gx_k) = one_microbatch(xs[0], xs[1])
            with _jax.named_scope("update"):
                return (loss_sum + l_k, _jax.tree.map(_jnp.add, grad_sum, gw_k)), gx_k

        init = (_jnp.zeros((), _jnp.float32), _jax.tree.map(_jnp.zeros_like, weights))
        (loss, grad_w), grad_x = _jax.lax.scan(body, init, (per_example, given["loss_target"]))
    with _jax.named_scope("update"):
        delta_w, new_m, new_v = {}, {}, {}
        for n in TWIN_WEIGHTS:
            delta_w[n], new_m[n], new_v[n] = _adamw(weights[n], grad_w[n], given["m_" + n], given["v_" + n])
    return (loss, grad_x, *[grad_w[n] for n in TWIN_WEIGHTS], *[delta_w[n] for n in TWIN_WEIGHTS],
            *[new_m[n] for n in TWIN_WEIGHTS], *[new_v[n] for n in TWIN_WEIGHTS])
```

```python
import functools
import math

import numpy as np
import jax
import jax.numpy as jnp
from jax import lax
from jax.experimental import pallas as pl
from jax.experimental.pallas import tpu as pltpu

F32, BF16 = jnp.float32, jnp.bfloat16

D_MODEL = 1024
D_FF = 2816
HG_HEADS, HG_DIM = 4, 128
HG_WIDTH = HG_HEADS * HG_DIM
ATT_Q_HEADS, ATT_KV_HEADS, ATT_HEAD_DIM = 8, 2, 64
ATT_GROUP = ATT_Q_HEADS // ATT_KV_HEADS
ATT_WIDTH = ATT_Q_HEADS * ATT_HEAD_DIM
KV_WIDTH = ATT_KV_HEADS * ATT_HEAD_DIM
WINDOW, BLOCK = 128, 128
NUM_BUCKETS, MAX_DISTANCE = 32, 128
N_MOD = 9
EPS = 1e-6
D_IN = 5 * HG_WIDTH + ATT_WIDTH + 2 * KV_WIDTH
ADAM_LR, ADAM_B1, ADAM_B2, ADAM_EPS, ADAM_WD, ADAM_STEP = 0.001, 0.9, 0.999, 1e-08, 0.01, 10

N_CHIPS = 4
FF_SHARD = 2 * D_FF // N_CHIPS
NEG = -1e30

VMEM_LIMIT_BYTES = 56 << 20
ROW_TILE = 512
HG_CHUNK = 16
HG_ROWS = 256

MESH = pl.DeviceIdType.MESH
ANY = pl.BlockSpec(memory_space=pl.ANY)


def _params(*sem):
    return pltpu.CompilerParams(dimension_semantics=sem, vmem_limit_bytes=VMEM_LIMIT_BYTES)


def _resident(shape, index_map):
    return pl.BlockSpec(shape, index_map, pipeline_mode=pl.Buffered(1))


def _dot(a, b, dims, precision=None):
    return lax.dot_general(a, b, (dims, ((), ())), precision=precision, preferred_element_type=F32)


def _nn(a, b, precision=None):
    return _dot(a, b, ((1,), (0,)), precision)


def _nt(a, b):
    return _dot(a, b, ((1,), (1,)))


def _tn(a, b):
    return _dot(a, b, ((0,), (0,)))


def _sigmoid(x):
    return jax.nn.sigmoid(x)


def _rmsmod_fwd(x, g, shift, scale, name):
    S, D = x.shape
    tr = min(ROW_TILE, S)

    def body(x_ref, g_ref, sh_ref, sc_ref, h_ref):
        xv = x_ref[...]
        rstd = lax.rsqrt(jnp.mean(xv * xv, axis=-1, keepdims=True) + EPS)
        y = xv * rstd * g_ref[...]
        h_ref[...] = (y * (1.0 + sc_ref[...]) + sh_ref[...]).astype(h_ref.dtype)

    row = pl.BlockSpec((tr, D), lambda i: (i, 0))
    vec = pl.BlockSpec((1, D), lambda i: (0, 0))
    return pl.pallas_call(
        body, name=name, grid=(S // tr,), in_specs=[row, vec, vec, vec], out_specs=row,
        out_shape=jax.ShapeDtypeStruct((S, D), BF16), compiler_params=_params("parallel"),
    )(x, g, shift, scale)


def _rmsmod_bwd(dh, x, g, scale, dx_res, name, below=None):
    S, D = x.shape
    tr = min(ROW_TILE, S)
    coef = below[2] if below else None

    def body(*refs):
        if below:
            dh_ref, x_ref, g_ref, sc_ref, dxr_ref, f_ref, gate_ref, dx_ref, dsh_ref, dsc_ref, dg_ref, df_ref, dgate_ref = refs
        else:
            dh_ref, x_ref, g_ref, sc_ref, dxr_ref, dx_ref, dsh_ref, dsc_ref, dg_ref = refs

        @pl.when(pl.program_id(0) == 0)
        def _():
            dsh_ref[...] = jnp.zeros_like(dsh_ref)
            dsc_ref[...] = jnp.zeros_like(dsc_ref)
            dg_ref[...] = jnp.zeros_like(dg_ref)
            if below:
                dgate_ref[...] = jnp.zeros_like(dgate_ref)

        dhv, xv, gv = dh_ref[...], x_ref[...], g_ref[...]
        one_sc = 1.0 + sc_ref[...]
        rstd = lax.rsqrt(jnp.mean(xv * xv, axis=-1, keepdims=True) + EPS)
        n = xv * rstd
        dsh_ref[...] += jnp.sum(dhv, axis=0, keepdims=True)
        dsc_ref[...] += jnp.sum(dhv * n, axis=0, keepdims=True) * gv
        dg_ref[...] += jnp.sum(dhv * n, axis=0, keepdims=True) * one_sc
        dn = dhv * (gv * one_sc)
        dx = dxr_ref[...] + rstd * (dn - n * jnp.mean(dn * n, axis=-1, keepdims=True))
        dx_ref[...] = dx
        if below:
            df_ref[...] = (coef * gate_ref[...] * dx).astype(df_ref.dtype)
            dgate_ref[...] += coef * jnp.sum(dx * f_ref[...].astype(F32), axis=0, keepdims=True)

    row = pl.BlockSpec((tr, D), lambda i: (i, 0))
    vec = pl.BlockSpec((1, D), lambda i: (0, 0))
    vshape = jax.ShapeDtypeStruct((1, D), F32)
    ins, in_specs = [dh, x, g, scale, dx_res], [row, row, vec, vec, row]
    outs, out_specs = [jax.ShapeDtypeStruct((S, D), F32), vshape, vshape, vshape], [row, vec, vec, vec]
    if below:
        ins += [below[0], below[1]]
        in_specs += [row, vec]
        outs += [jax.ShapeDtypeStruct((S, D), BF16), vshape]
        out_specs += [row, vec]
    return pl.pallas_call(
        body, name=name, grid=(S // tr,), in_specs=in_specs, out_specs=out_specs, out_shape=outs,
        compiler_params=_params("arbitrary"),
    )(*ins)


def _loss_bwd(y, target, f, gate, coef, name):
    S, D = y.shape
    tr = min(ROW_TILE, S)

    def body(y_ref, t_ref, f_ref, gate_ref, dy_ref, df_ref, dgate_ref, sq_ref):
        @pl.when(pl.program_id(0) == 0)
        def _():
            dgate_ref[...] = jnp.zeros_like(dgate_ref)
            sq_ref[...] = jnp.zeros_like(sq_ref)

        err = y_ref[...] - t_ref[...]
        sq_ref[...] += jnp.sum(err * err, axis=0, keepdims=True)
        dy = err * (1.0 / D)
        dy_ref[...] = dy
        df_ref[...] = (coef * gate_ref[...] * dy).astype(df_ref.dtype)
        dgate_ref[...] += coef * jnp.sum(dy * f_ref[...].astype(F32), axis=0, keepdims=True)

    row = pl.BlockSpec((tr, D), lambda i: (i, 0))
    vec = pl.BlockSpec((1, D), lambda i: (0, 0))
    vshape = jax.ShapeDtypeStruct((1, D), F32)
    return pl.pallas_call(
        body, name=name, grid=(S // tr,), in_specs=[row, row, row, vec], out_specs=[row, row, vec, vec],
        out_shape=[jax.ShapeDtypeStruct((S, D), F32), jax.ShapeDtypeStruct((S, D), BF16), vshape, vshape],
        compiler_params=_params("arbitrary"),
    )(y, target, f, gate)


def _ffn_in_fwd(h, w4, name):
    S, D = h.shape
    tm = min(ROW_TILE, S)
    n = w4.shape[2]

    def body(h_ref, wg_ref, wu_ref, zg_ref, zu_ref, a_ref):
        hv = h_ref[...]
        zg = _nn(hv, wg_ref[...])
        zu = _nn(hv, wu_ref[...])
        zg_ref[...] = zg.astype(zg_ref.dtype)
        zu_ref[...] = zu.astype(zu_ref.dtype)
        a_ref[...] = (zg * _sigmoid(zg) * zu).astype(a_ref.dtype)

    out = pl.BlockSpec((tm, n), lambda j, m: (m, j))
    oshape = jax.ShapeDtypeStruct((S, 2 * n), BF16)
    return pl.pallas_call(
        body, name=name, grid=(2, S // tm),
        in_specs=[pl.BlockSpec((tm, D), lambda j, m: (m, 0)),
                  pl.BlockSpec((None, D, n), lambda j, m: (j, 0, 0)),
                  pl.BlockSpec((None, D, n), lambda j, m: (j + 2, 0, 0))],
        out_specs=[out, out, out], out_shape=[oshape, oshape, oshape],
        compiler_params=_params("parallel", "parallel"),
    )(h, w4, w4)


def _proj_out_fwd(lhs, w, x, gate, coef, name):
    S, D = x.shape
    tm = min(ROW_TILE, S)
    ks = [a.shape[1] for a in lhs]

    def body(*refs):
        lhs_refs = refs[:len(lhs)]
        w_ref, x_ref, gate_ref, xn_ref, f_ref = refs[len(lhs):]
        acc, off = None, 0
        for a_ref, k in zip(lhs_refs, ks):
            part = _nn(a_ref[...], w_ref[off:off + k, :])
            acc = part if acc is None else acc + part
            off += k
        f_ref[...] = acc.astype(f_ref.dtype)
        xn_ref[...] = x_ref[...] + coef * gate_ref[...] * acc

    row = pl.BlockSpec((tm, D), lambda m: (m, 0))
    return pl.pallas_call(
        body, name=name, grid=(S // tm,),
        in_specs=[pl.BlockSpec((tm, k), lambda m: (m, 0)) for k in ks]
        + [_resident(w.shape, lambda m: (0, 0)), row, pl.BlockSpec((1, D), lambda m: (0, 0))],
        out_specs=[row, row],
        out_shape=[jax.ShapeDtypeStruct((S, D), F32), jax.ShapeDtypeStruct((S, D), BF16)],
        compiler_params=_params("parallel"),
    )(*lhs, w, x, gate)


def _matmul_nn(a, w, out_dtype, tm, name):
    S, K = a.shape
    N = w.shape[1]
    tm = min(tm, S)

    def body(a_ref, w_ref, o_ref):
        o_ref[...] = _nn(a_ref[...], w_ref[...]).astype(o_ref.dtype)

    return pl.pallas_call(
        body, name=name, grid=(S // tm,),
        in_specs=[pl.BlockSpec((tm, K), lambda m: (m, 0)), _resident((K, N), lambda m: (0, 0))],
        out_specs=pl.BlockSpec((tm, N), lambda m: (m, 0)), out_shape=jax.ShapeDtypeStruct((S, N), out_dtype),
        compiler_params=_params("parallel"),
    )(a, w)


def _dact_bwd(df, w_out, zg, zu, name):
    S, D = df.shape
    tm = min(ROW_TILE, S)
    n = w_out.shape[0] // 2

    def body(df_ref, w_ref, zg_ref, zu_ref, dzg_ref, dzu_ref):
        da = _nt(df_ref[...], w_ref[...])
        zg_v, zu_v = zg_ref[...].astype(F32), zu_ref[...].astype(F32)
        s = _sigmoid(zg_v)
        dzu_ref[...] = (da * zg_v * s).astype(dzu_ref.dtype)
        dzg_ref[...] = (da * zu_v * (s * (1.0 + zg_v * (1.0 - s)))).astype(dzg_ref.dtype)

    blk = pl.BlockSpec((tm, n), lambda j, m: (m, j))
    oshape = jax.ShapeDtypeStruct((S, 2 * n), BF16)
    return pl.pallas_call(
        body, name=name, grid=(2, S // tm),
        in_specs=[pl.BlockSpec((tm, D), lambda j, m: (m, 0)), pl.BlockSpec((n, D), lambda j, m: (j, 0)), blk, blk],
        out_specs=[blk, blk], out_shape=[oshape, oshape], compiler_params=_params("parallel", "parallel"),
    )(df, w_out, zg, zu)


def _ffn_in_dgrad(dzg, dzu, w4, name):
    S = dzg.shape[0]
    D, n = w4.shape[1], w4.shape[2]
    tm = min(ROW_TILE, S)

    def body(dzg_ref, dzu_ref, w_ref, dh_ref):
        acc = _nt(dzg_ref[:, 0:n], w_ref[0])
        acc += _nt(dzg_ref[:, n:2 * n], w_ref[1])
        acc += _nt(dzu_ref[:, 0:n], w_ref[2])
        acc += _nt(dzu_ref[:, n:2 * n], w_ref[3])
        dh_ref[...] = acc

    blk = pl.BlockSpec((tm, 2 * n), lambda m: (m, 0))
    return pl.pallas_call(
        body, name=name, grid=(S // tm,),
        in_specs=[blk, blk, _resident(w4.shape, lambda m: (0, 0, 0))],
        out_specs=pl.BlockSpec((tm, D), lambda m: (m, 0)), out_shape=jax.ShapeDtypeStruct((S, D), F32),
        compiler_params=_params("parallel"),
    )(dzg, dzu, w4)


def _matmul_nt(a, w, tm, name):
    S, K = a.shape
    N = w.shape[0]
    tm = min(tm, S)

    def body(a_ref, w_ref, o_ref):
        o_ref[...] = _nt(a_ref[...], w_ref[...])

    return pl.pallas_call(
        body, name=name, grid=(S // tm,),
        in_specs=[pl.BlockSpec((tm, K), lambda m: (m, 0)), _resident((N, K), lambda m: (0, 0))],
        out_specs=pl.BlockSpec((tm, N), lambda m: (m, 0)), out_shape=jax.ShapeDtypeStruct((S, N), F32),
        compiler_params=_params("parallel"),
    )(a, w)


def _wgrad(a, g, tn, name):
    S, Ka = a.shape
    N = g.shape[1]
    ts = min(ROW_TILE, S)

    def body(a_ref, g_ref, o_ref):
        @pl.when(pl.program_id(1) == 0)
        def _():
            o_ref[...] = jnp.zeros_like(o_ref)

        o_ref[...] += _tn(a_ref[...], g_ref[...])

    return pl.pallas_call(
        body, name=name, grid=(N // tn, S // ts),
        in_specs=[pl.BlockSpec((ts, Ka), lambda j, s: (s, 0)), pl.BlockSpec((ts, tn), lambda j, s: (s, j))],
        out_specs=pl.BlockSpec((None, Ka, tn), lambda j, s: (j, 0, 0)),
        out_shape=jax.ShapeDtypeStruct((N // tn, Ka, tn), F32), compiler_params=_params("parallel", "arbitrary"),
    )(a, g)


def _hgrn_chunk_common(qr, fr, oml, tri, last):
    k = oml * _sigmoid(-fr)
    g = jnp.log1p(-k)
    q = qr * _sigmoid(qr)
    G = _nn(tri, g, precision=lax.Precision.HIGHEST)
    Gl = G[last:last + 1]
    return q, k, G, Gl


def _hgrn_consts(reverse):
    C = HG_CHUNK
    r = lax.broadcasted_iota(jnp.int32, (C, C), 0)
    cc = lax.broadcasted_iota(jnp.int32, (C, C), 1)
    tri = ((cc >= r) if reverse else (cc <= r)).astype(F32)
    tri_t = ((cc <= r) if reverse else (cc >= r)).astype(F32)
    rid = lax.broadcasted_iota(jnp.int32, (C, HG_DIM), 0)
    return tri, tri_t, rid, (0 if reverse else C - 1)


def _hgrn_fwd(z, lb, direction, name):
    S = z.shape[0]
    C, DK = HG_CHUNK, HG_DIM
    tb = min(HG_ROWS, S)
    n_t, n_c = S // tb, tb // C
    reverse = direction == 1
    tmap = (lambda i: n_t - 1 - i) if reverse else (lambda i: i)
    fcol = HG_HEADS * (1 + direction)

    def body(q_ref, f_ref, v_ref, lb_ref, o_ref, st_out_ref, st_ref):
        @pl.when(pl.program_id(1) == 0)
        def _():
            st_ref[...] = jnp.zeros_like(st_ref)

        oml = 1.0 - lb_ref[...]
        tri, _, rid, last = _hgrn_consts(reverse)

        def chunk(ci, carry):
            cidx = (n_c - 1 - ci) if reverse else ci
            rows = pl.ds(pl.multiple_of(cidx * C, C), C)
            v = v_ref[rows, :]
            q, k, G, Gl = _hgrn_chunk_common(q_ref[rows, :], f_ref[rows, :], oml, tri, last)
            st0 = st_ref[...]
            st_out_ref[cidx] = st0
            o = _nt((q * jnp.exp(G)).astype(BF16), st0.astype(BF16))
            for s in range(C):
                valid = (rid <= s) if reverse else (rid >= s)
                e_s = jnp.exp(jnp.where(valid, G - G[s:s + 1], NEG))
                col = jnp.sum(q * k[s:s + 1] * e_s, axis=-1, keepdims=True)
                o = o + col * v[s:s + 1]
            o_ref[rows, :] = o
            kd = k * jnp.exp(Gl - G)
            st_ref[...] = st0 * jnp.exp(Gl) + _tn(v.astype(BF16), kd.astype(BF16))
            return carry

        lax.fori_loop(0, n_c, chunk, 0)

    def col(base):
        return pl.BlockSpec((tb, DK), lambda h, i: (tmap(i), base + h))

    return pl.pallas_call(
        body, name=name, grid=(HG_HEADS, n_t),
        in_specs=[col(0), col(fcol), col(3 * HG_HEADS), pl.BlockSpec((1, DK), lambda h, i: (0, h))],
        out_specs=[pl.BlockSpec((tb, DK), lambda h, i: (tmap(i), h)),
                   pl.BlockSpec((None, n_c, DK, DK), lambda h, i: (h, tmap(i), 0, 0))],
        out_shape=[jax.ShapeDtypeStruct((S, HG_WIDTH), F32), jax.ShapeDtypeStruct((HG_HEADS, S // C, DK, DK), F32)],
        scratch_shapes=[pltpu.VMEM((DK, DK), F32)],
        compiler_params=_params("parallel", "arbitrary"),
    )(z, z, z, lb)


def _hgrn_bwd(z, lb, do, states, direction, name, acc=None):
    S = z.shape[0]
    C, DK = HG_CHUNK, HG_DIM
    tb = min(HG_ROWS, S)
    n_t, n_c = S // tb, tb // C
    reverse = direction == 1
    tmap = (lambda i: i) if reverse else (lambda i: n_t - 1 - i)
    fcol = HG_HEADS * (1 + direction)

    def body(*refs):
        if acc:
            q_ref, f_ref, v_ref, lb_ref, do_ref, st_in_ref, dqa_ref, dva_ref, dq_ref, df_ref, dv_ref, doml_ref, dst_ref = refs
        else:
            q_ref, f_ref, v_ref, lb_ref, do_ref, st_in_ref, dq_ref, df_ref, dv_ref, doml_ref, dst_ref = refs

        @pl.when(pl.program_id(1) == 0)
        def _():
            dst_ref[...] = jnp.zeros_like(dst_ref)
            doml_ref[...] = jnp.zeros_like(doml_ref)

        oml = 1.0 - lb_ref[...]
        tri, tri_t, rid, last = _hgrn_consts(reverse)

        def chunk(ci, carry):
            cidx = ci if reverse else (n_c - 1 - ci)
            rows = pl.ds(pl.multiple_of(cidx * C, C), C)
            qr, fr, v, dov = q_ref[rows, :], f_ref[rows, :], v_ref[rows, :], do_ref[rows, :]
            q, k, G, Gl = _hgrn_chunk_common(qr, fr, oml, tri, last)
            e_g, e_gl, e_kd = jnp.exp(G), jnp.exp(Gl), jnp.exp(Gl - G)
            qd, kd = q * e_g, k * e_kd
            st0, dst1 = st_in_ref[cidx], dst_ref[...]
            do_b, dst1_b = dov.astype(BF16), dst1.astype(BF16)
            dqd = _nn(do_b, st0.astype(BF16))
            dkd = _nn(v.astype(BF16), dst1_b)
            dv = _nt(kd.astype(BF16), dst1_b)
            d_gl = e_gl * jnp.sum(st0 * dst1, axis=0, keepdims=True) + jnp.sum(dkd * kd, axis=0, keepdims=True)
            dst_ref[...] = dst1 * e_gl + _tn(do_b, qd.astype(BF16))
            dq, dk = dqd * e_g, dkd * e_kd
            for s in range(C):
                valid = (rid <= s) if reverse else (rid >= s)
                e_s = jnp.exp(jnp.where(valid, G - G[s:s + 1], NEG))
                col = jnp.sum(q * k[s:s + 1] * e_s, axis=-1, keepdims=True)
                dcol = jnp.sum(dov * v[s:s + 1], axis=-1, keepdims=True)
                w_s = dcol * e_s
                dq = dq + w_s * k[s:s + 1]
                dv = dv + jnp.where(rid == s, jnp.sum(col * dov, axis=0, keepdims=True), 0.0)
                dk = dk + jnp.where(rid == s, jnp.sum(w_s * q, axis=0, keepdims=True), 0.0)
            d_big_g = dq * q - dk * k + jnp.where(rid == last, d_gl, 0.0)
            dg = _nn(tri_t, d_big_g, precision=lax.Precision.HIGHEST)
            dk_all = dk - dg / (1.0 - k)
            sig_nf = _sigmoid(-fr)
            df_ref[rows, :] = -dk_all * k * (1.0 - sig_nf)
            doml_ref[...] += jnp.sum(dk_all * sig_nf, axis=0, keepdims=True)
            sq = _sigmoid(qr)
            dqr = dq * (sq * (1.0 + qr * (1.0 - sq)))
            if acc:
                dqr = dqr + dqa_ref[rows, :]
                dv = dv + dva_ref[rows, :]
            dq_ref[rows, :] = dqr
            dv_ref[rows, :] = dv
            return carry

        lax.fori_loop(0, n_c, chunk, 0)

    def col(base):
        return pl.BlockSpec((tb, DK), lambda h, i: (tmap(i), base + h))

    vec = pl.BlockSpec((1, DK), lambda h, i: (0, h))
    ins = [z, z, z, lb, do, states]
    in_specs = [col(0), col(fcol), col(3 * HG_HEADS), vec, col(0),
                pl.BlockSpec((None, n_c, DK, DK), lambda h, i: (h, tmap(i), 0, 0))]
    if acc:
        ins += list(acc)
        in_specs += [col(0), col(0)]
    full = jax.ShapeDtypeStruct((S, HG_WIDTH), F32)
    return pl.pallas_call(
        body, name=name, grid=(HG_HEADS, n_t), in_specs=in_specs,
        out_specs=[col(0), col(0), col(0), vec],
        out_shape=[full, full, full, jax.ShapeDtypeStruct((1, HG_WIDTH), F32)],
        scratch_shapes=[pltpu.VMEM((DK, DK), F32)],
        compiler_params=_params("parallel", "arbitrary"),
    )(*ins)


def _hgrn_post_fwd(o_f, o_b, z, norm_g, name):
    S = z.shape[0]
    tr = min(ROW_TILE, S)

    def body(of_ref, ob_ref, gr_ref, ng_ref, y_ref):
        o = of_ref[...] + ob_ref[...]
        gr = gr_ref[...]
        gate = gr * _sigmoid(gr)
        ng = ng_ref[...]
        for h in range(HG_HEADS):
            sl = slice(h * HG_DIM, (h + 1) * HG_DIM)
            oh = o[:, sl]
            rstd = lax.rsqrt(jnp.mean(oh * oh, axis=-1, keepdims=True) + EPS)
            y_ref[:, sl] = (oh * rstd * ng[:, sl] * gate[:, sl]).astype(y_ref.dtype)

    row = pl.BlockSpec((tr, HG_WIDTH), lambda i: (i, 0))
    return pl.pallas_call(
        body, name=name, grid=(S // tr,),
        in_specs=[row, row, pl.BlockSpec((tr, HG_WIDTH), lambda i: (i, 4)), pl.BlockSpec((1, HG_WIDTH), lambda i: (0, 0))],
        out_specs=row, out_shape=jax.ShapeDtypeStruct((S, HG_WIDTH), BF16), compiler_params=_params("parallel"),
    )(o_f, o_b, z, norm_g)


def _hgrn_post_bwd(dy, o_f, o_b, z, norm_g, name):
    S = z.shape[0]
    tr = min(ROW_TILE, S)

    def body(dy_ref, of_ref, ob_ref, gr_ref, ng_ref, do_ref, dgr_ref, dng_ref):
        @pl.when(pl.program_id(0) == 0)
        def _():
            dng_ref[...] = jnp.zeros_like(dng_ref)

        o = of_ref[...] + ob_ref[...]
        gr, ng, dyv = gr_ref[...], ng_ref[...], dy_ref[...]
        sg = _sigmoid(gr)
        for h in range(HG_HEADS):
            sl = slice(h * HG_DIM, (h + 1) * HG_DIM)
            oh, dyh, grh, sgh, ngh = o[:, sl], dyv[:, sl], gr[:, sl], sg[:, sl], ng[:, sl]
            rstd = lax.rsqrt(jnp.mean(oh * oh, axis=-1, keepdims=True) + EPS)
            on = oh * rstd
            du = dyh * (grh * sgh)
            dgr_ref[:, sl] = dyh * (on * ngh) * (sgh * (1.0 + grh * (1.0 - sgh)))
            dng_ref[:, sl] += jnp.sum(du * on, axis=0, keepdims=True)
            don = du * ngh
            do_ref[:, sl] = rstd * (don - on * jnp.mean(don * on, axis=-1, keepdims=True))

    row = pl.BlockSpec((tr, HG_WIDTH), lambda i: (i, 0))
    vec = pl.BlockSpec((1, HG_WIDTH), lambda i: (0, 0))
    full = jax.ShapeDtypeStruct((S, HG_WIDTH), F32)
    return pl.pallas_call(
        body, name=name, grid=(S // tr,),
        in_specs=[row, row, row, pl.BlockSpec((tr, HG_WIDTH), lambda i: (i, 4)), vec],
        out_specs=[row, row, vec], out_shape=[full, full, jax.ShapeDtypeStruct((1, HG_WIDTH), F32)],
        compiler_params=_params("arbitrary"),
    )(dy, o_f, o_b, z, norm_g)


def _t5_bucket_table():
    rel = (np.arange(3 * BLOCK)[None, :] - BLOCK) - np.arange(BLOCK)[:, None]
    nb = NUM_BUCKETS // 2
    max_exact = nb // 2
    ret = (rel > 0).astype(np.int32) * nb
    n = np.abs(rel)
    ratio = np.log(np.maximum(n, 1).astype(np.float32) / np.float32(max_exact)) / np.float32(math.log(MAX_DISTANCE / max_exact))
    large = max_exact + (ratio.astype(np.float32) * np.float32(nb - max_exact)).astype(np.int32)
    large = np.minimum(large, nb - 1)
    bucket = ret + np.where(n < max_exact, n, large)
    return bucket.astype(np.int32), (n <= WINDOW)


def _bias_table(rel_bias, name):
    bucket, in_band = _t5_bucket_table()
    idx = jnp.asarray(np.where(in_band, bucket, -1))

    def body(rb_ref, idx_ref, o_ref):
        h = pl.program_id(0)
        iv = idx_ref[...]
        acc = jnp.where(iv < 0, NEG, 0.0).astype(F32)
        for b in range(NUM_BUCKETS):
            acc = acc + jnp.where(iv == b, rb_ref[b, h], 0.0)
        o_ref[...] = acc

    return pl.pallas_call(
        body, name=name, grid=(ATT_Q_HEADS,),
        in_specs=[pl.BlockSpec(memory_space=pltpu.SMEM), pl.BlockSpec((BLOCK, 3 * BLOCK), lambda h: (0, 0))],
        out_specs=pl.BlockSpec((None, BLOCK, 3 * BLOCK), lambda h: (h, 0, 0)),
        out_shape=jax.ShapeDtypeStruct((ATT_Q_HEADS, BLOCK, 3 * BLOCK), F32), compiler_params=_params("parallel"),
    )(rel_bias, idx)


def _bias_grad(ds_sum, name):
    bucket, in_band = _t5_bucket_table()
    idx = jnp.asarray(np.where(in_band, bucket, -1))

    def body(ds_ref, idx_ref, o_ref):
        iv, ds = idx_ref[...], ds_ref[...]
        for b in range(NUM_BUCKETS):
            part = jnp.sum(jnp.where(iv == b, ds, 0.0), axis=0, keepdims=True)
            o_ref[b:b + 1, :] = part[:, 0:BLOCK] + part[:, BLOCK:2 * BLOCK] + part[:, 2 * BLOCK:3 * BLOCK]

    return pl.pallas_call(
        body, name=name, grid=(ATT_Q_HEADS,),
        in_specs=[pl.BlockSpec((None, BLOCK, 3 * BLOCK), lambda h: (h, 0, 0)), pl.BlockSpec((BLOCK, 3 * BLOCK), lambda h: (0, 0))],
        out_specs=pl.BlockSpec((None, NUM_BUCKETS, BLOCK), lambda h: (h, 0, 0)),
        out_shape=jax.ShapeDtypeStruct((ATT_Q_HEADS, NUM_BUCKETS, BLOCK), F32), compiler_params=_params("parallel"),
    )(ds_sum, idx)


def _attn_specs(nb):
    G, dh = ATT_GROUP, ATT_HEAD_DIM
    qspec = pl.BlockSpec((G, BLOCK, dh), lambda j, n: (j, n, 0))

    def kv(shift):
        return pl.BlockSpec((None, BLOCK, dh), lambda j, n: (j, jnp.clip(n + shift, 0, nb - 1), 0))

    gain = pl.BlockSpec((1, dh), lambda j, n: (0, 0))
    sink = pl.BlockSpec((G, 1, BLOCK), lambda j, n: (j, 0, 0))
    bias = pl.BlockSpec((G, BLOCK, 3 * BLOCK), lambda j, n: (j, 0, 0))
    return qspec, kv, gain, sink, bias


def _attn_probs(qh, kn, bias_h, sink_h, edge_ok):
    s = _nt(qh.astype(BF16), kn.astype(BF16)) * (1.0 / math.sqrt(ATT_HEAD_DIM)) + bias_h
    s = jnp.where(edge_ok, s, NEG)
    m = jnp.maximum(jnp.max(s, axis=-1, keepdims=True), sink_h)
    p = jnp.exp(s - m)
    e_sink = jnp.exp(sink_h - m)
    inv = 1.0 / (jnp.sum(p, axis=-1, keepdims=True) + e_sink)
    return p * inv, e_sink * inv


def _rms_rows(x):
    rstd = lax.rsqrt(jnp.mean(x * x, axis=-1, keepdims=True) + EPS)
    return x * rstd, rstd


def _edge_ok(n, nb):
    colid = lax.broadcasted_iota(jnp.int32, (BLOCK, 3 * BLOCK), 1)
    return jnp.logical_and(jnp.logical_or(colid >= BLOCK, n > 0), jnp.logical_or(colid < 2 * BLOCK, n < nb - 1))


def _attn_fwd(q, k, v, q_g, k_g, sink, bias, name):
    S = q.shape[1]
    nb = S // BLOCK
    G = ATT_GROUP
    qspec, kv, gain, sink_spec, bias_spec = _attn_specs(nb)

    def body(q_ref, k0, k1, k2, v0, v1, v2, qg_ref, kg_ref, sink_ref, bias_ref, o_ref):
        n = pl.program_id(1)
        edge_ok = _edge_ok(n, nb)
        kcat = jnp.concatenate([k0[...], k1[...], k2[...]], axis=0)
        vcat = jnp.concatenate([v0[...], v1[...], v2[...]], axis=0).astype(BF16)
        kn = _rms_rows(kcat)[0] * kg_ref[...]
        for g in range(G):
            qn = _rms_rows(q_ref[g])[0] * qg_ref[...]
            p, _ = _attn_probs(qn, kn, bias_ref[g], sink_ref[g][:, 0:1], edge_ok)
            o_ref[g] = _nn(p.astype(BF16), vcat)

    return pl.pallas_call(
        body, name=name, grid=(ATT_KV_HEADS, nb),
        in_specs=[qspec, kv(-1), kv(0), kv(1), kv(-1), kv(0), kv(1), gain, gain, sink_spec, bias_spec],
        out_specs=qspec, out_shape=jax.ShapeDtypeStruct(q.shape, F32), compiler_params=_params("parallel", "parallel"),
    )(q, k, k, k, v, v, v, q_g, k_g, sink, bias)


def _attn_bwd(q, k, v, q_g, k_g, sink, bias, do, name):
    S = q.shape[1]
    nb = S // BLOCK
    G, dh = ATT_GROUP, ATT_HEAD_DIM
    scale = 1.0 / math.sqrt(dh)
    qspec, kv, gain, sink_spec, bias_spec = _attn_specs(nb)

    def body(q_ref, k0, k1, k2, v0, v1, v2, qg_ref, kg_ref, sink_ref, bias_ref, do_ref,
             dq_ref, dkw_ref, dvw_ref, ds_ref, dsink_ref, dqg_ref):
        n = pl.program_id(1)

        @pl.when(n == 0)
        def _():
            ds_ref[...] = jnp.zeros_like(ds_ref)
            dsink_ref[...] = jnp.zeros_like(dsink_ref)
            dqg_ref[...] = jnp.zeros_like(dqg_ref)

        edge_ok = _edge_ok(n, nb)
        kcat = jnp.concatenate([k0[...], k1[...], k2[...]], axis=0)
        vcat = jnp.concatenate([v0[...], v1[...], v2[...]], axis=0).astype(BF16)
        kn = _rms_rows(kcat)[0] * kg_ref[...]
        kn_b = kn.astype(BF16)
        qg = qg_ref[...]
        dkw = jnp.zeros((3 * BLOCK, dh), F32)
        dvw = jnp.zeros((3 * BLOCK, dh), F32)
        for g in range(G):
            qhat, rstd = _rms_rows(q_ref[g])
            qn = qhat * qg
            p, p_sink = _attn_probs(qn, kn, bias_ref[g], sink_ref[g][:, 0:1], edge_ok)
            do_b = do_ref[g].astype(BF16)
            dp = _nt(do_b, vcat)
            delta = jnp.sum(p * dp, axis=-1, keepdims=True)
            ds = p * (dp - delta)
            ds_ref[g] += ds
            dsink_ref[g] += jnp.zeros((1, BLOCK), F32) - jnp.sum(p_sink * delta, axis=0, keepdims=True)
            ds_b = ds.astype(BF16)
            dvw = dvw + _tn(p.astype(BF16), do_b)
            dkw = dkw + _tn(ds_b, qn.astype(BF16)) * scale
            dqn = _nn(ds_b, kn_b) * scale
            dqg_ref[...] += jnp.sum(dqn * qhat, axis=0, keepdims=True)
            dqh = dqn * qg
            dq_ref[g] = rstd * (dqh - qhat * jnp.mean(dqh * qhat, axis=-1, keepdims=True))
        dkw_ref[...] = dkw
        dvw_ref[...] = dvw

    win = pl.BlockSpec((None, None, 3 * BLOCK, dh), lambda j, n: (j, n, 0, 0))
    wshape = jax.ShapeDtypeStruct((ATT_KV_HEADS, nb, 3 * BLOCK, dh), F32)
    return pl.pallas_call(
        body, name=name, grid=(ATT_KV_HEADS, nb),
        in_specs=[qspec, kv(-1), kv(0), kv(1), kv(-1), kv(0), kv(1), gain, gain, sink_spec, bias_spec, qspec],
        out_specs=[qspec, win, win, bias_spec, sink_spec, pl.BlockSpec((None, 1, dh), lambda j, n: (j, 0, 0))],
        out_shape=[jax.ShapeDtypeStruct(q.shape, F32), wshape, wshape,
                   jax.ShapeDtypeStruct((ATT_Q_HEADS, BLOCK, 3 * BLOCK), F32),
                   jax.ShapeDtypeStruct((ATT_Q_HEADS, 1, BLOCK), F32),
                   jax.ShapeDtypeStruct((ATT_KV_HEADS, 1, dh), F32)],
        compiler_params=_params("parallel", "arbitrary"),
    )(q, k, k, k, v, v, v, q_g, k_g, sink, bias, do)


def _attn_kv_reduce(dkw, dvw, k, k_g, name):
    S = k.shape[1]
    nb = S // BLOCK
    dh = ATT_HEAD_DIM

    def body(a0, a1, a2, b0, b1, b2, k_ref, kg_ref, dk_ref, dv_ref, dkg_ref):
        n = pl.program_id(1)

        @pl.when(n == 0)
        def _():
            dkg_ref[...] = jnp.zeros_like(dkg_ref)

        lo = jnp.where(n > 0, 1.0, 0.0)
        hi = jnp.where(n < nb - 1, 1.0, 0.0)
        dkn = a1[...] + lo * a0[...] + hi * a2[...]
        dv_ref[...] = b1[...] + lo * b0[...] + hi * b2[...]
        khat, rstd = _rms_rows(k_ref[...])
        dkg_ref[...] += jnp.sum(dkn * khat, axis=0, keepdims=True)
        dkh = dkn * kg_ref[...]
        dk_ref[...] = rstd * (dkh - khat * jnp.mean(dkh * khat, axis=-1, keepdims=True))

    def win(shift, part):
        return pl.BlockSpec((None, None, BLOCK, dh), lambda j, n: (j, jnp.clip(n + shift, 0, nb - 1), part, 0))

    blk = pl.BlockSpec((None, BLOCK, dh), lambda j, n: (j, n, 0))
    return pl.pallas_call(
        body, name=name, grid=(ATT_KV_HEADS, nb),
        in_specs=[win(-1, 2), win(0, 1), win(1, 0), win(-1, 2), win(0, 1), win(1, 0), blk, pl.BlockSpec((1, dh), lambda j, n: (0, 0))],
        out_specs=[blk, blk, pl.BlockSpec((None, 1, dh), lambda j, n: (j, 0, 0))],
        out_shape=[jax.ShapeDtypeStruct(k.shape, F32), jax.ShapeDtypeStruct(k.shape, F32),
                   jax.ShapeDtypeStruct((ATT_KV_HEADS, 1, dh), F32)],
        compiler_params=_params("parallel", "arbitrary"),
    )(dkw, dkw, dkw, dvw, dvw, dvw, k, k_g)


def _ada_fwd(c_act, w, b, name):
    n = w.shape[1]

    def body(c_ref, w_ref, b_ref, o_ref):
        o_ref[...] = _nn(c_ref[...], w_ref[...], precision=lax.Precision.HIGHEST) + b_ref[...]

    tn = n // 3
    return pl.pallas_call(
        body, name=name, grid=(3,),
        in_specs=[pl.BlockSpec(c_act.shape, lambda j: (0, 0)), pl.BlockSpec((w.shape[0], tn), lambda j: (0, j)),
                  pl.BlockSpec((1, tn), lambda j: (0, j))],
        out_specs=pl.BlockSpec((c_act.shape[0], tn), lambda j: (0, j)),
        out_shape=jax.ShapeDtypeStruct((c_act.shape[0], n), F32), compiler_params=_params("parallel"),
    )(c_act, w, b)


def _ada_wgrad(c_act_t, dm, name):
    D, nbatch = c_act_t.shape
    n = dm.shape[1]
    tr = 256

    def body(c_ref, dm_ref, o_ref):
        cv, dv = c_ref[...], dm_ref[...]
        acc = cv[:, 0:1] * dv[0:1, :]
        for b in range(1, nbatch):
            acc = acc + cv[:, b:b + 1] * dv[b:b + 1, :]
        o_ref[...] = acc

    return pl.pallas_call(
        body, name=name, grid=(D // tr,),
        in_specs=[pl.BlockSpec((tr, nbatch), lambda i: (i, 0)), pl.BlockSpec((nbatch, n), lambda i: (0, 0))],
        out_specs=pl.BlockSpec((tr, n), lambda i: (i, 0)), out_shape=jax.ShapeDtypeStruct((D, n), F32),
        compiler_params=_params("parallel"),
    )(c_act_t, dm)


def _adamw(w, g, m, v, name):
    R, Cn = w.shape
    tr = R
    for cand in (256, 128, 64, 32, 16, 8):
        if R % cand == 0:
            tr = cand
            break

    def body(w_ref, g_ref, m_ref, v_ref, d_ref, nm_ref, nv_ref):
        gv = g_ref[...]
        m_new = ADAM_B1 * m_ref[...] + (1.0 - ADAM_B1) * gv
        v_new = ADAM_B2 * v_ref[...] + (1.0 - ADAM_B2) * (gv * gv)
        m_hat = m_new / (1.0 - ADAM_B1 ** ADAM_STEP)
        v_hat = v_new / (1.0 - ADAM_B2 ** ADAM_STEP)
        d_ref[...] = -ADAM_LR * (m_hat / (jnp.sqrt(v_hat) + ADAM_EPS) + ADAM_WD * w_ref[...])
        nm_ref[...] = m_new
        nv_ref[...] = v_new

    blk = pl.BlockSpec((tr, Cn), lambda i: (i, 0))
    shp = jax.ShapeDtypeStruct((R, Cn), F32)
    return pl.pallas_call(
        body, name=name, grid=(R // tr,), in_specs=[blk] * 4, out_specs=[blk] * 3, out_shape=[shp] * 3,
        compiler_params=_params("parallel"),
    )(w, g, m, v)


def _place():
    return lax.axis_index("x"), lax.axis_index("y"), lax.axis_index("c")


def _flip(place, k):
    x, y, c = place
    return (1 - x if k & 4 else x, 1 - y if k & 2 else y, 1 - c if k & 1 else c)


def _dev_index(place):
    x, y, c = place
    return 4 * x + 2 * y + c


def _chip_index(place):
    return 2 * place[0] + place[1]


def _allgather8(x, name, reduce=False):
    R, Cn = x.shape

    def body(x_ref, *rest):
        if reduce:
            out_ref, sum_ref, send_sems, recv_sems, local_sem = rest
        else:
            out_ref, send_sems, recv_sems, local_sem = rest
        me = _place()
        mine = pltpu.make_async_copy(x_ref, out_ref.at[_dev_index(me)], local_sem)
        mine.start()

        def copy(k, origin, to):
            return pltpu.make_async_remote_copy(
                src_ref=x_ref, dst_ref=out_ref.at[_dev_index(origin)], send_sem=send_sems.at[k - 1],
                recv_sem=recv_sems.at[k - 1], device_id=to, device_id_type=MESH)

        sends = [copy(k, me, _flip(me, k)) for k in range(1, 8)]
        for cp in sends:
            cp.start()
        for k in range(1, 8):
            copy(k, _flip(me, k), me).wait_recv()
        for cp in sends:
            cp.wait_send()
        mine.wait()
        if reduce:
            acc = out_ref[0]
            for i in range(1, 8):
                acc = acc + out_ref[i]
            sum_ref[...] = acc

    vm = pl.BlockSpec(memory_space=pltpu.VMEM)
    outs = [jax.ShapeDtypeStruct((8, R, Cn), F32)] + ([jax.ShapeDtypeStruct((R, Cn), F32)] if reduce else [])
    res = pl.pallas_call(
        body, name=name, in_specs=[vm], out_specs=[vm] * len(outs), out_shape=outs,
        scratch_shapes=[pltpu.SemaphoreType.DMA((7,)), pltpu.SemaphoreType.DMA((7,)), pltpu.SemaphoreType.DMA],
    )(x)
    return res if reduce else res[0]


def _weights_allgather(shards, name):
    n = len(shards)

    def body(*refs):
        in_refs, out_refs = refs[:n], refs[n:2 * n]
        send_sems, recv_sems, local_sems = refs[2 * n:]
        me = _place()
        c = me[2]
        sibling = _flip(me, 1)
        others = [_flip(me, 2 * j) for j in (1, 2, 3)]

        def copy(a, k, src, dst, to):
            return pltpu.make_async_remote_copy(
                src_ref=src, dst_ref=dst, send_sem=send_sems.at[7 * a + k], recv_sem=recv_sems.at[7 * a + k],
                device_id=to, device_id_type=MESH)

        def block(a, place, half):
            return out_refs[a].at[_chip_index(place), half]

        started, local = [], []
        for a in range(n):
            src = in_refs[a].at[c]
            mine = pltpu.make_async_copy(src, block(a, me, c), local_sems.at[a])
            mine.start()
            local.append(mine)
            for k, to in enumerate([sibling] + others):
                cp = copy(a, k, src, block(a, me, c), to)
                cp.start()
                started.append(cp)
        for a in range(n):
            for j, other in enumerate(others):
                landed = block(a, other, c)
                copy(a, 1 + j, landed, landed, me).wait_recv()
                fwd = copy(a, 4 + j, landed, landed, sibling)
                fwd.start()
                started.append(fwd)
        for a in range(n):
            for k, origin in enumerate([me] + others):
                got = block(a, origin, 1 - c)
                copy(a, 0 if k == 0 else 3 + k, got, got, me).wait_recv()
        for cp in started:
            cp.wait_send()
        for cp in local:
            cp.wait()

    return pl.pallas_call(
        body, name=name, in_specs=[ANY] * n, out_specs=[ANY] * n,
        out_shape=[jax.ShapeDtypeStruct((N_CHIPS,) + s.shape, s.dtype) for s in shards],
        scratch_shapes=[pltpu.SemaphoreType.DMA((7 * n,)), pltpu.SemaphoreType.DMA((7 * n,)), pltpu.SemaphoreType.DMA((n,))],
    )(*shards)


def _halves_exchange(grads, name):
    n = len(grads)

    def body(*refs):
        in_refs, mine_refs, got_refs = refs[:n], refs[n:2 * n], refs[2 * n:3 * n]
        send_sems, recv_sems, local_sems = refs[3 * n:]
        me = _place()
        c = me[2]
        sibling = _flip(me, 1)
        started = []
        for a in range(n):
            for kk in range(N_CHIPS):
                i = N_CHIPS * a + kk
                keep = pltpu.make_async_copy(in_refs[a].at[kk, c], mine_refs[a].at[kk], local_sems.at[i])
                keep.start()
                send = pltpu.make_async_remote_copy(
                    src_ref=in_refs[a].at[kk, 1 - c], dst_ref=got_refs[a].at[kk], send_sem=send_sems.at[i],
                    recv_sem=recv_sems.at[i], device_id=sibling, device_id_type=MESH)
                send.start()
                started.append((keep, send))
        for keep, send in started:
            send.wait_recv()
        for keep, send in started:
            send.wait_send()
            keep.wait()

    shapes = [jax.ShapeDtypeStruct((N_CHIPS,) + g.shape[2:], g.dtype) for g in grads]
    outs = pl.pallas_call(
        body, name=name, in_specs=[ANY] * n, out_specs=[ANY] * (2 * n), out_shape=shapes + shapes,
        scratch_shapes=[pltpu.SemaphoreType.DMA((N_CHIPS * n,)), pltpu.SemaphoreType.DMA((N_CHIPS * n,)),
                        pltpu.SemaphoreType.DMA((N_CHIPS * n,))],
    )(*grads)
    return list(zip(outs[:n], outs[n:]))


def _chips_exchange(parts, name):
    n = len(parts)

    def body(*refs):
        in_refs, out_refs = refs[:n], refs[n:2 * n]
        send_sems, recv_sems, local_sems = refs[2 * n:]
        me = _place()
        my_chip = _chip_index(me)
        started = []
        for a in range(n):
            keep = pltpu.make_async_copy(in_refs[a].at[my_chip], out_refs[a].at[my_chip], local_sems.at[a])
            keep.start()
            for j in (1, 2, 3):
                peer = _flip(me, 2 * j)
                send = pltpu.make_async_remote_copy(
                    src_ref=in_refs[a].at[_chip_index(peer)], dst_ref=out_refs[a].at[my_chip],
                    send_sem=send_sems.at[3 * a + j - 1], recv_sem=recv_sems.at[3 * a + j - 1],
                    device_id=peer, device_id_type=MESH)
                send.start()
                landing = out_refs[a].at[_chip_index(peer)]
                recv = pltpu.make_async_remote_copy(
                    src_ref=landing, dst_ref=landing, send_sem=send_sems.at[3 * a + j - 1],
                    recv_sem=recv_sems.at[3 * a + j - 1], device_id=me, device_id_type=MESH)
                started.append((send, recv))
            started.append((keep, None))
        for send, recv in started:
            if recv is not None:
                recv.wait_recv()
        for send, recv in started:
            if recv is not None:
                send.wait_send()
            else:
                send.wait()

    return pl.pallas_call(
        body, name=name, in_specs=[ANY] * n, out_specs=[ANY] * n,
        out_shape=[jax.ShapeDtypeStruct(p.shape, p.dtype) for p in parts],
        scratch_shapes=[pltpu.SemaphoreType.DMA((3 * n,)), pltpu.SemaphoreType.DMA((3 * n,)), pltpu.SemaphoreType.DMA((n,))],
    )(*parts)


def _siblings_exchange(halves, name):
    n = len(halves)

    def body(*refs):
        in_refs, out_refs = refs[:n], refs[n:2 * n]
        send_sems, recv_sems, local_sems = refs[2 * n:]
        me = _place()
        c = me[2]
        sibling = _flip(me, 1)
        started = []
        for a in range(n):
            keep = pltpu.make_async_copy(in_refs[a], out_refs[a].at[c], local_sems.at[a])
            keep.start()
            send = pltpu.make_async_remote_copy(
                src_ref=in_refs[a], dst_ref=out_refs[a].at[c], send_sem=send_sems.at[a], recv_sem=recv_sems.at[a],
                device_id=sibling, device_id_type=MESH)
            send.start()
            landing = out_refs[a].at[1 - c]
            recv = pltpu.make_async_remote_copy(
                src_ref=landing, dst_ref=landing, send_sem=send_sems.at[a], recv_sem=recv_sems.at[a],
                device_id=me, device_id_type=MESH)
            started.append((keep, send, recv))
        for keep, send, recv in started:
            recv.wait_recv()
        for keep, send, recv in started:
            send.wait_send()
            keep.wait()

    return pl.pallas_call(
        body, name=name, in_specs=[ANY] * n, out_specs=[ANY] * n,
        out_shape=[jax.ShapeDtypeStruct((2,) + h.shape, h.dtype) for h in halves],
        scratch_shapes=[pltpu.SemaphoreType.DMA((n,)), pltpu.SemaphoreType.DMA((n,)), pltpu.SemaphoreType.DMA((n,))],
    )(*halves)


def _add2(a, b, name):
    N, R, Cn = a.shape
    tr = R
    for cand in (256, 128, 88, 64, 32, 16, 8):
        if R % cand == 0:
            tr = cand
            break

    def body(a_ref, b_ref, o_ref):
        o_ref[...] = a_ref[...] + b_ref[...]

    blk = pl.BlockSpec((None, tr, Cn), lambda k, i: (k, i, 0))
    return pl.pallas_call(
        body, name=name, grid=(N, R // tr), in_specs=[blk, blk], out_specs=blk,
        out_shape=jax.ShapeDtypeStruct(a.shape, F32), compiler_params=_params("parallel", "parallel"),
    )(a, b)


def _sum_chips(x, name):
    _, R, Cn = x.shape
    tr = R
    for cand in (256, 128, 88, 64, 32, 16, 8):
        if R % cand == 0:
            tr = cand
            break

    def body(x_ref, o_ref):
        o_ref[...] = ((x_ref[0] + x_ref[1]) + x_ref[2]) + x_ref[3]

    return pl.pallas_call(
        body, name=name, grid=(R // tr,), in_specs=[pl.BlockSpec((N_CHIPS, tr, Cn), lambda i: (0, i, 0))],
        out_specs=pl.BlockSpec((tr, Cn), lambda i: (i, 0)), out_shape=jax.ShapeDtypeStruct((R, Cn), F32),
        compiler_params=_params("parallel"),
    )(x)


def _reduce_scatter(grads, tag):
    pairs = _halves_exchange(grads, f"{tag}_halves_exchange")
    parts = [_add2(mine, theirs, f"{tag}_pair_sum_{i}") for i, (mine, theirs) in enumerate(pairs)]
    landed = _chips_exchange(parts, f"{tag}_chips_exchange")
    halves = [_sum_chips(x, f"{tag}_chip_sum_{i}") for i, x in enumerate(landed)]
    return _siblings_exchange(halves, f"{tag}_siblings_exchange")


def _pad_row(v, width):
    v = v.reshape(1, -1)
    return jnp.pad(v, ((0, 0), (0, width - v.shape[1])))


def _ffn_forward(x, ng, shift, scale, gate, w_in4, w_out, tag):
    h = _rmsmod_fwd(x, ng, shift, scale, f"{tag}_norm")
    zg, zu, a = _ffn_in_fwd(h, w_in4, f"{tag}_in")
    x_new, f = _proj_out_fwd([a], w_out, x, gate, 0.5, f"{tag}_out")
    return x_new, (h, zg, zu, a, f)


def _ffn_backward(df, saved, w_in4, w_out, tag):
    h, zg, zu, a, _ = saved
    dzg, dzu = _dact_bwd(df, w_out, zg, zu, f"{tag}_dact")
    dw_out = _wgrad(a, df, 512, f"{tag}_dw_out")
    dw_out = jnp.concatenate([dw_out[0], dw_out[1]], axis=1)
    dh = _ffn_in_dgrad(dzg, dzu, w_in4, f"{tag}_dh")
    dw_in = jnp.concatenate([_wgrad(h, dzg, FF_SHARD, f"{tag}_dw_gate"), _wgrad(h, dzu, FF_SHARD, f"{tag}_dw_up")], axis=0)
    return dh, dw_in, dw_out


def kernel(x, c, w_ada, b_ada, norm_g, w_ffn1_in, w_ffn1_out, w_ffn2_in, w_ffn2_out, w_mix_in, w_mix_out, hgrn_lb, hgrn_norm_g, qk_norm_g, attn_sink, rel_bias, loss_target, m_w_ada, m_b_ada, m_norm_g, m_w_ffn1_in, m_w_ffn1_out, m_w_ffn2_in, m_w_ffn2_out, m_w_mix_in, m_w_mix_out, m_hgrn_lb, m_hgrn_norm_g, m_qk_norm_g, m_attn_sink, m_rel_bias, v_w_ada, v_b_ada, v_norm_g, v_w_ffn1_in, v_w_ffn1_out, v_w_ffn2_in, v_w_ffn2_out, v_w_mix_in, v_w_mix_out, v_hgrn_lb, v_hgrn_norm_g, v_qk_norm_g, v_attn_sink, v_rel_bias):
    D = D_MODEL
    S = x.shape[1]
    place = (lax.axis_index("x"), lax.axis_index("y"), lax.axis_index("c"))
    me, my_chip = _dev_index(place), _chip_index(place)
    x0 = x[0]
    target = loss_target[0]

    def halves(w):
        return w.astype(BF16).reshape(2, w.shape[0] // 2, w.shape[1])

    gathered = _weights_allgather(
        [halves(w_ffn1_in[0]), halves(w_ffn1_out[0]), halves(w_ffn2_in[0]), halves(w_ffn2_out[0]), halves(w_mix_in[0]),
         halves(w_mix_out[0])], "weights_allgather")
    w1_in = gathered[0].reshape(N_CHIPS, D, FF_SHARD)
    w1_out = gathered[1].reshape(D_FF, D)
    w2_in = gathered[2].reshape(N_CHIPS, D, FF_SHARD)
    w2_out = gathered[3].reshape(D_FF, D)
    wm_in = gathered[4].reshape(N_CHIPS, D, D_IN // N_CHIPS).transpose(1, 0, 2).reshape(D, D_IN)
    wm_out = gathered[5].reshape(D, D)

    small = jnp.concatenate([_pad_row(c, D), _pad_row(norm_g, D), _pad_row(hgrn_lb, D), jnp.zeros((5, D), F32)], axis=0)
    small_all = _allgather8(small, "small_allgather")
    c_all = small_all[:, 0, :]
    by_chip = small_all[0::2]
    norm_g_full = by_chip[:, 1, :3 * 256].reshape(N_CHIPS, 3, 256).transpose(1, 0, 2).reshape(3, D)
    lb_raw = by_chip[:, 2, :2 * 2 * 128].reshape(N_CHIPS, 2, 2, 128).transpose(1, 2, 0, 3).reshape(2, 2, HG_WIDTH)
    lb = jax.nn.sigmoid(lb_raw[:, 0, :] - lb_raw[:, 1, :])
    lb_f, lb_b = lb[0:1], lb[1:2]

    c_act_all = c_all * jax.nn.sigmoid(c_all)
    n_ada = w_ada.shape[2]
    b_mine = lax.dynamic_slice_in_dim(b_ada, my_chip * n_ada, n_ada, axis=1)
    mods_part = _ada_fwd(c_act_all, w_ada[0], b_mine, "ada_fwd")
    mods_all = _allgather8(mods_part, "mods_allgather")[0::2].transpose(1, 0, 2).reshape(8, N_MOD * D)
    mods = lax.dynamic_slice_in_dim(mods_all, me, 1, axis=0)
    sh1, sc1, g1, sh2, sc2, g2, sh3, sc3, g3 = [mods[:, i * D:(i + 1) * D] for i in range(N_MOD)]

    x1, saved1 = _ffn_forward(x0, norm_g_full[0:1], sh1, sc1, g1, w1_in, w1_out, "ffn1")

    h2 = _rmsmod_fwd(x1, norm_g_full[1:2], sh2, sc2, "mix_norm")
    z = _matmul_nn(h2, wm_in, F32, 256, "mix_in")
    of, st_f = _hgrn_fwd(z, lb_f, 0, "hgrn_fwd_f")
    ob, st_b = _hgrn_fwd(z, lb_b, 1, "hgrn_fwd_b")
    o_h = _hgrn_post_fwd(of, ob, z, hgrn_norm_g, "hgrn_post")

    def to_heads(t, nh):
        return t.reshape(S, nh, ATT_HEAD_DIM).transpose(1, 0, 2)

    aq = to_heads(z[:, 5 * HG_WIDTH:5 * HG_WIDTH + ATT_WIDTH], ATT_Q_HEADS)
    ak = to_heads(z[:, 5 * HG_WIDTH + ATT_WIDTH:5 * HG_WIDTH + ATT_WIDTH + KV_WIDTH], ATT_KV_HEADS)
    av = to_heads(z[:, 5 * HG_WIDTH + ATT_WIDTH + KV_WIDTH:], ATT_KV_HEADS)
    q_g, k_g = qk_norm_g[0, 0:1], qk_norm_g[0, 1:2]
    sink_b = jnp.broadcast_to(attn_sink.reshape(ATT_Q_HEADS, 1, 1), (ATT_Q_HEADS, 1, BLOCK))
    bias = _bias_table(rel_bias, "bias_table")
    o_attn = _attn_fwd(aq, ak, av, q_g, k_g, sink_b, bias, "attn_fwd")
    o_a = o_attn.transpose(1, 0, 2).reshape(S, ATT_WIDTH).astype(BF16)
    x2, mixed = _proj_out_fwd([o_h, o_a], wm_out, x1, g2, 1.0, "mix_out")

    x3, saved3 = _ffn_forward(x2, norm_g_full[2:3], sh3, sc3, g3, w2_in, w2_out, "ffn2")

    dx3, df3, dg3, sq_cols = _loss_bwd(x3, target, saved3[4], g3, 0.5, "loss")
    loss_mine = 0.5 * jnp.sum(sq_cols) / D

    dh3, dw2_in, dw2_out = _ffn_backward(df3, saved3, w2_in, w2_out, "ffn2")
    dx2, dsh3, dsc3, dng3, dmixed, dg2 = _rmsmod_bwd(dh3, x2, norm_g_full[2:3], sc3, dx3, "ffn2_norm_bwd", below=(mixed, g2, 1.0))

    do_cat = _matmul_nt(dmixed, wm_out, ROW_TILE, "mix_out_dgrad")
    dwm_out = jnp.concatenate([_wgrad(o_h, dmixed, 512, "mix_out_dw_h"), _wgrad(o_a, dmixed, 512, "mix_out_dw_a")], axis=1)
    dwm_out = jnp.concatenate([dwm_out[0], dwm_out[1]], axis=1)

    do_sum, dgr, d_hnorm = _hgrn_post_bwd(do_cat, of, ob, z, hgrn_norm_g, "hgrn_post_bwd")
    dq_f, dff, dv_f, doml_f = _hgrn_bwd(z, lb_f, do_sum, st_f, 0, "hgrn_bwd_f")
    dhq, dfb, dhi, doml_b = _hgrn_bwd(z, lb_b, do_sum, st_b, 1, "hgrn_bwd_b", acc=(dq_f, dv_f))

    do_a = to_heads(do_cat[:, HG_WIDTH:], ATT_Q_HEADS)
    daq, dkw, dvw, ds_sum, dsink, dqg = _attn_bwd(aq, ak, av, q_g, k_g, sink_b, bias, do_a, "attn_bwd")
    dak, dav, dkg = _attn_kv_reduce(dkw, dvw, ak, k_g, "attn_kv_reduce")
    d_rel_bias = jnp.sum(_bias_grad(ds_sum, "bias_grad"), axis=-1).T

    def from_heads(t):
        return t.transpose(1, 0, 2).reshape(S, -1)

    dz = jnp.concatenate([dhq, dff, dfb, dhi, dgr, from_heads(daq), from_heads(dak), from_heads(dav)], axis=1).astype(BF16)
    dh2 = _matmul_nt(dz, wm_in, 256, "mix_in_dgrad")
    dwm_in = _wgrad(h2, dz, D_IN // 2, "mix_in_dw")
    dwm_in = jnp.concatenate([dwm_in[0], dwm_in[1]], axis=1)
    dx1, dsh2, dsc2, dng2, df1, dg1 = _rmsmod_bwd(dh2, x1, norm_g_full[1:2], sc2, dx2, "mix_norm_bwd", below=(saved1[4], g1, 0.5))

    dh1, dw1_in, dw1_out = _ffn_backward(df1, saved1, w1_in, w1_out, "ffn1")
    dx0, dsh1, dsc1, dng1 = _rmsmod_bwd(dh1, x0, norm_g_full[0:1], sc1, dx1, "ffn1_norm_bwd")

    dlb = -jnp.concatenate([doml_f, doml_b], axis=0)
    dlb_raw = dlb * lb * (1.0 - lb)
    d_hgrn_lb = jnp.stack([dlb_raw, -dlb_raw], axis=1)
    d_qk = jnp.concatenate([jnp.sum(dqg, axis=0), jnp.sum(dkg, axis=0)], axis=0)
    dmods = jnp.concatenate([dsh1, dsc1, dg1, dsh2, dsc2, dg2, dsh3, dsc3, dg3], axis=0)
    packed = jnp.concatenate(
        [dmods, dng1, dng2, dng3, d_hgrn_lb.reshape(2, D), _pad_row(d_hnorm, D), _pad_row(d_qk, D),
         _pad_row(dsink[:, 0, 0], D), _pad_row(d_rel_bias, D), _pad_row(loss_mine, D)], axis=0)
    packed = jnp.pad(packed, ((0, 24 - packed.shape[0]), (0, 0)))
    packed_all, packed_sum = _allgather8(packed, "small_grads_allgather", reduce=True)
    dmods_all = packed_all[:, 0:N_MOD, :].reshape(8, N_MOD * D)
    g_b_ada = packed_sum[0:N_MOD].reshape(1, N_MOD * D)
    g_norm_full = packed_sum[9:12]
    g_norm_g = lax.dynamic_slice_in_dim(g_norm_full, my_chip * 256, 256, axis=1).reshape(1, 3, 256)
    g_hgrn_lb = lax.dynamic_slice_in_dim(packed_sum[12:14].reshape(2, 2, HG_WIDTH), my_chip * 128, 128, axis=2)
    g_hgrn_norm_g = packed_sum[14:15, :HG_WIDTH]
    g_qk_norm_g = packed_sum[15, :2 * ATT_HEAD_DIM].reshape(1, 2, ATT_HEAD_DIM)
    g_attn_sink = packed_sum[16:17, :ATT_Q_HEADS]
    g_rel_bias = packed_sum[17, :NUM_BUCKETS * ATT_Q_HEADS].reshape(NUM_BUCKETS, ATT_Q_HEADS)
    loss = packed_sum[18, 0]

    dm_mine = lax.dynamic_slice_in_dim(dmods_all, my_chip * n_ada, n_ada, axis=1)
    g_w_ada = _ada_wgrad(c_act_all.T, dm_mine, "ada_wgrad")[None]

    def by_chip_rows(g):
        return g.reshape(N_CHIPS, 2, g.shape[0] // (2 * N_CHIPS), g.shape[1])

    def by_chip_cols(g):
        return g.reshape(N_CHIPS, 2, g.shape[1] // 2, g.shape[2])

    wide = D_IN // N_CHIPS
    reduced = _reduce_scatter(
        [by_chip_cols(dw1_in), by_chip_rows(dw1_out), by_chip_cols(dw2_in), by_chip_rows(dw2_out),
         by_chip_cols(dwm_in.reshape(D, N_CHIPS, wide).transpose(1, 0, 2)), by_chip_rows(dwm_out)], "grads")
    g_w1_in = reduced[0].reshape(1, D, FF_SHARD)
    g_w1_out = reduced[1].reshape(1, D_FF // N_CHIPS, D)
    g_w2_in = reduced[2].reshape(1, D, FF_SHARD)
    g_w2_out = reduced[3].reshape(1, D_FF // N_CHIPS, D)
    g_wm_in = reduced[4].reshape(1, D, wide)
    g_wm_out = reduced[5].reshape(1, D // N_CHIPS, D)

    def big(w, g, m, v, name):
        d, nm, nv = _adamw(w[0], g[0], m[0], v[0], name)
        return d[None], nm[None], nv[None]

    smalls = [(b_ada, g_b_ada, m_b_ada, v_b_ada), (norm_g, g_norm_g, m_norm_g, v_norm_g), (hgrn_lb, g_hgrn_lb, m_hgrn_lb, v_hgrn_lb),
              (hgrn_norm_g, g_hgrn_norm_g, m_hgrn_norm_g, v_hgrn_norm_g), (qk_norm_g, g_qk_norm_g, m_qk_norm_g, v_qk_norm_g),
              (attn_sink, g_attn_sink, m_attn_sink, v_attn_sink), (rel_bias, g_rel_bias, m_rel_bias, v_rel_bias)]
    sizes = [t[0].size for t in smalls]
    total = sum(sizes)
    rows = -(-total // 128)
    rows = -(-rows // 8) * 8

    def pack(i):
        flat = jnp.concatenate([t[i].reshape(-1) for t in smalls])
        fill = 1.0 if i == 3 else 0.0
        return jnp.pad(flat, (0, rows * 128 - total), constant_values=fill).reshape(rows, 128)

    packed_out = _adamw(pack(0), pack(1), pack(2), pack(3), "adamw_small")

    def unpack(flat2d):
        flat = flat2d.reshape(-1)
        outs, off = [], 0
        for t, n in zip(smalls, sizes):
            outs.append(flat[off:off + n].reshape(t[0].shape))
            off += n
        return outs

    d_small, m_small, v_small = [unpack(t) for t in packed_out]

    upd = {
        "w_ada": big(w_ada, g_w_ada, m_w_ada, v_w_ada, "adamw_w_ada"),
        "w_ffn1_in": big(w_ffn1_in, g_w1_in, m_w_ffn1_in, v_w_ffn1_in, "adamw_w_ffn1_in"),
        "w_ffn1_out": big(w_ffn1_out, g_w1_out, m_w_ffn1_out, v_w_ffn1_out, "adamw_w_ffn1_out"),
        "w_ffn2_in": big(w_ffn2_in, g_w2_in, m_w_ffn2_in, v_w_ffn2_in, "adamw_w_ffn2_in"),
        "w_ffn2_out": big(w_ffn2_out, g_w2_out, m_w_ffn2_out, v_w_ffn2_out, "adamw_w_ffn2_out"),
        "w_mix_in": big(w_mix_in, g_wm_in, m_w_mix_in, v_w_mix_in, "adamw_w_mix_in"),
        "w_mix_out": big(w_mix_out, g_wm_out, m_w_mix_out, v_w_mix_out, "adamw_w_mix_out"),
    }
    small_names = ["b_ada", "norm_g", "hgrn_lb", "hgrn_norm_g", "qk_norm_g", "attn_sink", "rel_bias"]
    for i, nme in enumerate(small_names):
        upd[nme] = (d_small[i], m_small[i], v_small[i])
    grads = {
        "w_ada": g_w_ada, "b_ada": g_b_ada, "norm_g": g_norm_g, "w_ffn1_in": g_w1_in, "w_ffn1_out": g_w1_out,
        "w_ffn2_in": g_w2_in, "w_ffn2_out": g_w2_out, "w_mix_in": g_wm_in, "w_mix_out": g_wm_out, "hgrn_lb": g_hgrn_lb,
        "hgrn_norm_g": g_hgrn_norm_g, "qk_norm_g": g_qk_norm_g, "attn_sink": g_attn_sink, "rel_bias": g_rel_bias,
    }
    order = ["w_ada", "b_ada", "norm_g", "w_ffn1_in", "w_ffn1_out", "w_ffn2_in", "w_ffn2_out", "w_mix_in", "w_mix_out",
             "hgrn_lb", "hgrn_norm_g", "qk_norm_g", "attn_sink", "rel_bias"]
    return (loss, dx0[None], *[grads[k] for k in order], *[upd[k][0] for k in order], *[upd[k][1] for k in order],
            *[upd[k][2] for k in order])
```

```python
import functools
import math

import numpy as np
import jax
import jax.numpy as jnp
from jax import lax
from jax.experimental import pallas as pl
from jax.experimental.pallas import tpu as pltpu

F32, BF16 = jnp.float32, jnp.bfloat16

D_MODEL = 1024
D_FF = 2816
HG_HEADS, HG_DIM = 4, 128
HG_WIDTH = HG_HEADS * HG_DIM
ATT_Q_HEADS, ATT_KV_HEADS, ATT_HEAD_DIM = 8, 2, 64
ATT_GROUP = ATT_Q_HEADS // ATT_KV_HEADS
ATT_WIDTH = ATT_Q_HEADS * ATT_HEAD_DIM
KV_WIDTH = ATT_KV_HEADS * ATT_HEAD_DIM
WINDOW, BLOCK = 128, 128
NUM_BUCKETS, MAX_DISTANCE = 32, 128
N_MOD = 9
EPS = 1e-6
D_IN = 5 * HG_WIDTH + ATT_WIDTH + 2 * KV_WIDTH
ADAM_LR, ADAM_B1, ADAM_B2, ADAM_EPS, ADAM_WD, ADAM_STEP = 0.001, 0.9, 0.999, 1e-08, 0.01, 10

N_CHIPS = 4
FF_SHARD = 2 * D_FF // N_CHIPS
NEG = -1e30

VMEM_LIMIT_BYTES = 56 << 20
ROW_TILE = 512
HG_CHUNK = 16
HG_ROWS = 256

MESH = pl.DeviceIdType.MESH
ANY = pl.BlockSpec(memory_space=pl.ANY)


def _params(*sem):
    return pltpu.CompilerParams(dimension_semantics=sem, vmem_limit_bytes=VMEM_LIMIT_BYTES)


def _resident(shape, index_map):
    return pl.BlockSpec(shape, index_map, pipeline_mode=pl.Buffered(1))


def _dot(a, b, dims, precision=None):
    return lax.dot_general(a, b, (dims, ((), ())), precision=precision, preferred_element_type=F32)


def _nn(a, b, precision=None):
    return _dot(a, b, ((1,), (0,)), precision)


def _nt(a, b):
    return _dot(a, b, ((1,), (1,)))


def _tn(a, b):
    return _dot(a, b, ((0,), (0,)))


def _sigmoid(x):
    return jax.nn.sigmoid(x)


def _rmsmod_fwd(x, g, shift, scale, name):
    S, D = x.shape
    tr = min(ROW_TILE, S)

    def body(x_ref, g_ref, sh_ref, sc_ref, h_ref):
        xv = x_ref[...]
        rstd = lax.rsqrt(jnp.mean(xv * xv, axis=-1, keepdims=True) + EPS)
        y = xv * rstd * g_ref[...]
        h_ref[...] = (y * (1.0 + sc_ref[...]) + sh_ref[...]).astype(h_ref.dtype)

    row = pl.BlockSpec((tr, D), lambda i: (i, 0))
    vec = pl.BlockSpec((1, D), lambda i: (0, 0))
    return pl.pallas_call(
        body, name=name, grid=(S // tr,), in_specs=[row, vec, vec, vec], out_specs=row,
        out_shape=jax.ShapeDtypeStruct((S, D), BF16), compiler_params=_params("parallel"),
    )(x, g, shift, scale)


def _rmsmod_bwd(dh, x, g, scale, dx_res, name, below=None):
    S, D = x.shape
    tr = min(ROW_TILE, S)
    coef = below[2] if below else None

    def body(*refs):
        if below:
            dh_ref, x_ref, g_ref, sc_ref, dxr_ref, f_ref, gate_ref, dx_ref, dsh_ref, dsc_ref, dg_ref, df_ref, dgate_ref = refs
        else:
            dh_ref, x_ref, g_ref, sc_ref, dxr_ref, dx_ref, dsh_ref, dsc_ref, dg_ref = refs

        @pl.when(pl.program_id(0) == 0)
        def _():
            dsh_ref[...] = jnp.zeros_like(dsh_ref)
            dsc_ref[...] = jnp.zeros_like(dsc_ref)
            dg_ref[...] = jnp.zeros_like(dg_ref)
            if below:
                dgate_ref[...] = jnp.zeros_like(dgate_ref)

        dhv, xv, gv = dh_ref[...], x_ref[...], g_ref[...]
        one_sc = 1.0 + sc_ref[...]
        rstd = lax.rsqrt(jnp.mean(xv * xv, axis=-1, keepdims=True) + EPS)
        n = xv * rstd
        dsh_ref[...] += jnp.sum(dhv, axis=0, keepdims=True)
        dsc_ref[...] += jnp.sum(dhv * n, axis=0, keepdims=True) * gv
        dg_ref[...] += jnp.sum(dhv * n, axis=0, keepdims=True) * one_sc
        dn = dhv * (gv * one_sc)
        dx = dxr_ref[...] + rstd * (dn - n * jnp.mean(dn * n, axis=-1, keepdims=True))
        dx_ref[...] = dx
        if below:
            df_ref[...] = (coef * gate_ref[...] * dx).astype(df_ref.dtype)
            dgate_ref[...] += coef * jnp.sum(dx * f_ref[...].astype(F32), axis=0, keepdims=True)

    row = pl.BlockSpec((tr, D), lambda i: (i, 0))
    vec = pl.BlockSpec((1, D), lambda i: (0, 0))
    vshape = jax.ShapeDtypeStruct((1, D), F32)
    ins, in_specs = [dh, x, g, scale, dx_res], [row, row, vec, vec, row]
    outs, out_specs = [jax.ShapeDtypeStruct((S, D), F32), vshape, vshape, vshape], [row, vec, vec, vec]
    if below:
        ins += [below[0], below[1]]
        in_specs += [row, vec]
        outs += [jax.ShapeDtypeStruct((S, D), BF16), vshape]
        out_specs += [row, vec]
    return pl.pallas_call(
        body, name=name, grid=(S // tr,), in_specs=in_specs, out_specs=out_specs, out_shape=outs,
        compiler_params=_params("arbitrary"),
    )(*ins)


def _loss_bwd(y, target, f, gate, coef, name):
    S, D = y.shape
    tr = min(ROW_TILE, S)

    def body(y_ref, t_ref, f_ref, gate_ref, dy_ref, df_ref, dgate_ref, sq_ref):
        @pl.when(pl.program_id(0) == 0)
        def _():
            dgate_ref[...] = jnp.zeros_like(dgate_ref)
            sq_ref[...] = jnp.zeros_like(sq_ref)

        err = y_ref[...] - t_ref[...]
        sq_ref[...] += jnp.sum(err * err, axis=0, keepdims=True)
        dy = err * (1.0 / D)
        dy_ref[...] = dy
        df_ref[...] = (coef * gate_ref[...] * dy).astype(df_ref.dtype)
        dgate_ref[...] += coef * jnp.sum(dy * f_ref[...].astype(F32), axis=0, keepdims=True)

    row = pl.BlockSpec((tr, D), lambda i: (i, 0))
    vec = pl.BlockSpec((1, D), lambda i: (0, 0))
    vshape = jax.ShapeDtypeStruct((1, D), F32)
    return pl.pallas_call(
        body, name=name, grid=(S // tr,), in_specs=[row, row, row, vec], out_specs=[row, row, vec, vec],
        out_shape=[jax.ShapeDtypeStruct((S, D), F32), jax.ShapeDtypeStruct((S, D), BF16), vshape, vshape],
        compiler_params=_params("arbitrary"),
    )(y, target, f, gate)


def _ffn_in_fwd(h, w4, name):
    S, D = h.shape
    tm = min(ROW_TILE, S)
    n = w4.shape[2]

    def body(h_ref, wg_ref, wu_ref, zg_ref, zu_ref, a_ref):
        hv = h_ref[...]
        zg = _nn(hv, wg_ref[...])
        zu = _nn(hv, wu_ref[...])
        zg_ref[...] = zg.astype(zg_ref.dtype)
        zu_ref[...] = zu.astype(zu_ref.dtype)
        a_ref[...] = (zg * _sigmoid(zg) * zu).astype(a_ref.dtype)

    out = pl.BlockSpec((tm, n), lambda j, m: (m, j))
    oshape = jax.ShapeDtypeStruct((S, 2 * n), BF16)
    return pl.pallas_call(
        body, name=name, grid=(2, S // tm),
        in_specs=[pl.BlockSpec((tm, D), lambda j, m: (m, 0)),
                  pl.BlockSpec((None, D, n), lambda j, m: (j, 0, 0)),
                  pl.BlockSpec((None, D, n), lambda j, m: (j + 2, 0, 0))],
        out_specs=[out, out, out], out_shape=[oshape, oshape, oshape],
        compiler_params=_params("parallel", "parallel"),
    )(h, w4, w4)


def _proj_out_fwd(lhs, w, x, gate, coef, name):
    S, D = x.shape
    tm = min(ROW_TILE, S)
    ks = [a.shape[1] for a in lhs]

    def body(*refs):
        lhs_refs = refs[:len(lhs)]
        w_ref, x_ref, gate_ref, xn_ref, f_ref = refs[len(lhs):]
        acc, off = None, 0
        for a_ref, k in zip(lhs_refs, ks):
            part = _nn(a_ref[...], w_ref[off:off + k, :])
            acc = part if acc is None else acc + part
            off += k
        f_ref[...] = acc.astype(f_ref.dtype)
        xn_ref[...] = x_ref[...] + coef * gate_ref[...] * acc

    row = pl.BlockSpec((tm, D), lambda m: (m, 0))
    return pl.pallas_call(
        body, name=name, grid=(S // tm,),
        in_specs=[pl.BlockSpec((tm, k), lambda m: (m, 0)) for k in ks]
        + [_resident(w.shape, lambda m: (0, 0)), row, pl.BlockSpec((1, D), lambda m: (0, 0))],
        out_specs=[row, row],
        out_shape=[jax.ShapeDtypeStruct((S, D), F32), jax.ShapeDtypeStruct((S, D), BF16)],
        compiler_params=_params("parallel"),
    )(*lhs, w, x, gate)


def _matmul_nn(a, w, out_dtype, tm, name):
    S, K = a.shape
    N = w.shape[1]
    tm = min(tm, S)

    def body(a_ref, w_ref, o_ref):
        o_ref[...] = _nn(a_ref[...], w_ref[...]).astype(o_ref.dtype)

    return pl.pallas_call(
        body, name=name, grid=(S // tm,),
        in_specs=[pl.BlockSpec((tm, K), lambda m: (m, 0)), _resident((K, N), lambda m: (0, 0))],
        out_specs=pl.BlockSpec((tm, N), lambda m: (m, 0)), out_shape=jax.ShapeDtypeStruct((S, N), out_dtype),
        compiler_params=_params("parallel"),
    )(a, w)


def _dact_bwd(df, w_out, zg, zu, name):
    S, D = df.shape
    tm = min(ROW_TILE, S)
    n = w_out.shape[0] // 2

    def body(df_ref, w_ref, zg_ref, zu_ref, dzg_ref, dzu_ref):
        da = _nt(df_ref[...], w_ref[...])
        zg_v, zu_v = zg_ref[...].astype(F32), zu_ref[...].astype(F32)
        s = _sigmoid(zg_v)
        dzu_ref[...] = (da * zg_v * s).astype(dzu_ref.dtype)
        dzg_ref[...] = (da * zu_v * (s * (1.0 + zg_v * (1.0 - s)))).astype(dzg_ref.dtype)

    blk = pl.BlockSpec((tm, n), lambda j, m: (m, j))
    oshape = jax.ShapeDtypeStruct((S, 2 * n), BF16)
    return pl.pallas_call(
        body, name=name, grid=(2, S // tm),
        in_specs=[pl.BlockSpec((tm, D), lambda j, m: (m, 0)), pl.BlockSpec((n, D), lambda j, m: (j, 0)), blk, blk],
        out_specs=[blk, blk], out_shape=[oshape, oshape], compiler_params=_params("parallel", "parallel"),
    )(df, w_out, zg, zu)


def _ffn_in_dgrad(dzg, dzu, w4, name):
    S = dzg.shape[0]
    D, n = w4.shape[1], w4.shape[2]
    tm = min(ROW_TILE, S)

    def body(dzg_ref, dzu_ref, w_ref, dh_ref):
        acc = _nt(dzg_ref[:, 0:n], w_ref[0])
        acc += _nt(dzg_ref[:, n:2 * n], w_ref[1])
        acc += _nt(dzu_ref[:, 0:n], w_ref[2])
        acc += _nt(dzu_ref[:, n:2 * n], w_ref[3])
        dh_ref[...] = acc

    blk = pl.BlockSpec((tm, 2 * n), lambda m: (m, 0))
    return pl.pallas_call(
        body, name=name, grid=(S // tm,),
        in_specs=[blk, blk, _resident(w4.shape, lambda m: (0, 0, 0))],
        out_specs=pl.BlockSpec((tm, D), lambda m: (m, 0)), out_shape=jax.ShapeDtypeStruct((S, D), F32),
        compiler_params=_params("parallel"),
    )(dzg, dzu, w4)


def _matmul_nt(a, w, tm, name):
    S, K = a.shape
    N = w.shape[0]
    tm = min(tm, S)

    def body(a_ref, w_ref, o_ref):
        o_ref[...] = _nt(a_ref[...], w_ref[...])

    return pl.pallas_call(
        body, name=name, grid=(S // tm,),
        in_specs=[pl.BlockSpec((tm, K), lambda m: (m, 0)), _resident((N, K), lambda m: (0, 0))],
        out_specs=pl.BlockSpec((tm, N), lambda m: (m, 0)), out_shape=jax.ShapeDtypeStruct((S, N), F32),
        compiler_params=_params("parallel"),
    )(a, w)


def _wgrad(a, g, tn, name):
    S, Ka = a.shape
    N = g.shape[1]
    ts = min(ROW_TILE, S)

    def body(a_ref, g_ref, o_ref):
        @pl.when(pl.program_id(1) == 0)
        def _():
            o_ref[...] = jnp.zeros_like(o_ref)

        o_ref[...] += _tn(a_ref[...], g_ref[...])

    return pl.pallas_call(
        body, name=name, grid=(N // tn, S // ts),
        in_specs=[pl.BlockSpec((ts, Ka), lambda j, s: (s, 0)), pl.BlockSpec((ts, tn), lambda j, s: (s, j))],
        out_specs=pl.BlockSpec((None, Ka, tn), lambda j, s: (j, 0, 0)),
        out_shape=jax.ShapeDtypeStruct((N // tn, Ka, tn), F32), compiler_params=_params("parallel", "arbitrary"),
    )(a, g)


def _hgrn_chunk_common(qr, fr, oml, tri, last):
    k = oml * _sigmoid(-fr)
    g = jnp.log1p(-k)
    q = qr * _sigmoid(qr)
    G = _nn(tri, g, precision=lax.Precision.HIGHEST)
    Gl = G[last:last + 1]
    return q, k, G, Gl


def _hgrn_consts(reverse):
    C = HG_CHUNK
    r = lax.broadcasted_iota(jnp.int32, (C, C), 0)
    cc = lax.broadcasted_iota(jnp.int32, (C, C), 1)
    tri = ((cc >= r) if reverse else (cc <= r)).astype(F32)
    tri_t = ((cc <= r) if reverse else (cc >= r)).astype(F32)
    rid = lax.broadcasted_iota(jnp.int32, (C, HG_WIDTH), 0)
    return tri, tri_t, rid, (0 if reverse else C - 1)


def _head_slices():
    return [slice(h * HG_DIM, (h + 1) * HG_DIM) for h in range(HG_HEADS)]


def _per_head_lane_sum(x):
    C = x.shape[0]
    return jnp.concatenate(
        [jnp.broadcast_to(jnp.sum(x[:, sl], axis=-1, keepdims=True), (C, HG_DIM)) for sl in _head_slices()], axis=1)


def _hgrn_fwd(z, lb, direction, name):
    S = z.shape[0]
    C, DK, W = HG_CHUNK, HG_DIM, HG_WIDTH
    tb = min(HG_ROWS, S)
    n_t, n_c = S // tb, tb // C
    reverse = direction == 1
    tmap = (lambda i: n_t - 1 - i) if reverse else (lambda i: i)

    def body(q_ref, f_ref, v_ref, lb_ref, o_ref, st_out_ref, st_ref):
        @pl.when(pl.program_id(0) == 0)
        def _():
            st_ref[...] = jnp.zeros_like(st_ref)

        oml = 1.0 - lb_ref[...]
        tri, _, rid, last = _hgrn_consts(reverse)

        def chunk(ci, carry):
            cidx = (n_c - 1 - ci) if reverse else ci
            rows = pl.ds(pl.multiple_of(cidx * C, C), C)
            v = v_ref[rows, :]
            q, k, G, Gl = _hgrn_chunk_common(q_ref[rows, :], f_ref[rows, :], oml, tri, last)
            qd = (q * jnp.exp(G)).astype(BF16)
            kd = (k * jnp.exp(Gl - G)).astype(BF16)
            e_gl = jnp.exp(Gl)
            v_b = v.astype(BF16)
            inter = []
            for h, sl in enumerate(_head_slices()):
                st0 = st_ref[h]
                st_out_ref[h, cidx] = st0
                inter.append(_nt(qd[:, sl], st0.astype(BF16)))
                st_ref[h] = st0 * e_gl[:, sl] + _tn(v_b[:, sl], kd[:, sl])
            o = jnp.concatenate(inter, axis=1)
            for s in range(C):
                valid = (rid <= s) if reverse else (rid >= s)
                e_s = jnp.exp(jnp.where(valid, G - G[s:s + 1], NEG))
                o = o + _per_head_lane_sum(q * k[s:s + 1] * e_s) * v[s:s + 1]
            o_ref[rows, :] = o
            return carry

        lax.fori_loop(0, n_c, chunk, 0)

    def sec(j):
        return pl.BlockSpec((tb, W), lambda i: (tmap(i), j))

    return pl.pallas_call(
        body, name=name, grid=(n_t,),
        in_specs=[sec(0), sec(1 + direction), sec(3), pl.BlockSpec((1, W), lambda i: (0, 0))],
        out_specs=[sec(0), pl.BlockSpec((HG_HEADS, n_c, DK, DK), lambda i: (0, tmap(i), 0, 0))],
        out_shape=[jax.ShapeDtypeStruct((S, W), F32), jax.ShapeDtypeStruct((HG_HEADS, S // C, DK, DK), F32)],
        scratch_shapes=[pltpu.VMEM((HG_HEADS, DK, DK), F32)],
        compiler_params=_params("arbitrary"),
    )(z, z, z, lb)


def _hgrn_bwd(z, lb, do, states, direction, name, acc=None):
    S = z.shape[0]
    C, DK, W = HG_CHUNK, HG_DIM, HG_WIDTH
    tb = min(HG_ROWS, S)
    n_t, n_c = S // tb, tb // C
    reverse = direction == 1
    tmap = (lambda i: i) if reverse else (lambda i: n_t - 1 - i)

    def body(*refs):
        if acc:
            q_ref, f_ref, v_ref, lb_ref, do_ref, st_in_ref, dqa_ref, dva_ref, dq_ref, df_ref, dv_ref, doml_ref, dst_ref = refs
        else:
            q_ref, f_ref, v_ref, lb_ref, do_ref, st_in_ref, dq_ref, df_ref, dv_ref, doml_ref, dst_ref = refs

        @pl.when(pl.program_id(0) == 0)
        def _():
            dst_ref[...] = jnp.zeros_like(dst_ref)
            doml_ref[...] = jnp.zeros_like(doml_ref)

        oml = 1.0 - lb_ref[...]
        tri, tri_t, rid, last = _hgrn_consts(reverse)

        def chunk(ci, carry):
            cidx = ci if reverse else (n_c - 1 - ci)
            rows = pl.ds(pl.multiple_of(cidx * C, C), C)
            qr, fr, v, dov = q_ref[rows, :], f_ref[rows, :], v_ref[rows, :], do_ref[rows, :]
            q, k, G, Gl = _hgrn_chunk_common(qr, fr, oml, tri, last)
            e_g, e_gl, e_kd = jnp.exp(G), jnp.exp(Gl), jnp.exp(Gl - G)
            qd, kd = q * e_g, k * e_kd
            do_b, v_b, qd_b, kd_b = dov.astype(BF16), v.astype(BF16), qd.astype(BF16), kd.astype(BF16)
            dqd, dkd, dv, state_dot = [], [], [], []
            for h, sl in enumerate(_head_slices()):
                st0, dst1 = st_in_ref[h, cidx], dst_ref[h]
                dst1_b = dst1.astype(BF16)
                dqd.append(_nn(do_b[:, sl], st0.astype(BF16)))
                dkd.append(_nn(v_b[:, sl], dst1_b))
                dv.append(_nt(kd_b[:, sl], dst1_b))
                state_dot.append(jnp.sum(st0 * dst1, axis=0, keepdims=True))
                dst_ref[h] = dst1 * e_gl[:, sl] + _tn(do_b[:, sl], qd_b[:, sl])
            dqd, dkd, dv = [jnp.concatenate(t, axis=1) for t in (dqd, dkd, dv)]
            d_gl = e_gl * jnp.concatenate(state_dot, axis=1) + jnp.sum(dkd * kd, axis=0, keepdims=True)
            dq, dk = dqd * e_g, dkd * e_kd
            for s in range(C):
                valid = (rid <= s) if reverse else (rid >= s)
                e_s = jnp.exp(jnp.where(valid, G - G[s:s + 1], NEG))
                col = _per_head_lane_sum(q * k[s:s + 1] * e_s)
                w_s = _per_head_lane_sum(dov * v[s:s + 1]) * e_s
                dq = dq + w_s * k[s:s + 1]
                dv = dv + jnp.where(rid == s, jnp.sum(col * dov, axis=0, keepdims=True), 0.0)
                dk = dk + jnp.where(rid == s, jnp.sum(w_s * q, axis=0, keepdims=True), 0.0)
            d_big_g = dq * q - dk * k + jnp.where(rid == last, d_gl, 0.0)
            dg = _nn(tri_t, d_big_g, precision=lax.Precision.HIGHEST)
            dk_all = dk - dg / (1.0 - k)
            sig_nf = _sigmoid(-fr)
            df_ref[rows, :] = -dk_all * k * (1.0 - sig_nf)
            doml_ref[...] += jnp.sum(dk_all * sig_nf, axis=0, keepdims=True)
            sq = _sigmoid(qr)
            dqr = dq * (sq * (1.0 + qr * (1.0 - sq)))
            if acc:
                dqr = dqr + dqa_ref[rows, :]
                dv = dv + dva_ref[rows, :]
            dq_ref[rows, :] = dqr
            dv_ref[rows, :] = dv
            return carry

        lax.fori_loop(0, n_c, chunk, 0)

    def sec(j):
        return pl.BlockSpec((tb, W), lambda i: (tmap(i), j))

    vec = pl.BlockSpec((1, W), lambda i: (0, 0))
    ins = [z, z, z, lb, do, states]
    in_specs = [sec(0), sec(1 + direction), sec(3), vec, sec(0),
                pl.BlockSpec((HG_HEADS, n_c, DK, DK), lambda i: (0, tmap(i), 0, 0))]
    if acc:
        ins += list(acc)
        in_specs += [sec(0), sec(0)]
    full = jax.ShapeDtypeStruct((S, W), F32)
    return pl.pallas_call(
        body, name=name, grid=(n_t,), in_specs=in_specs,
        out_specs=[sec(0), sec(0), sec(0), vec],
        out_shape=[full, full, full, jax.ShapeDtypeStruct((1, W), F32)],
        scratch_shapes=[pltpu.VMEM((HG_HEADS, DK, DK), F32)],
        compiler_params=_params("arbitrary"),
    )(*ins)


def _hgrn_post_fwd(o_f, o_b, z, norm_g, name):
    S = z.shape[0]
    tr = min(ROW_TILE, S)

    def body(of_ref, ob_ref, gr_ref, ng_ref, y_ref):
        o = of_ref[...] + ob_ref[...]
        gr = gr_ref[...]
        gate = gr * _sigmoid(gr)
        ng = ng_ref[...]
        for h in range(HG_HEADS):
            sl = slice(h * HG_DIM, (h + 1) * HG_DIM)
            oh = o[:, sl]
            rstd = lax.rsqrt(jnp.mean(oh * oh, axis=-1, keepdims=True) + EPS)
            y_ref[:, sl] = (oh * rstd * ng[:, sl] * gate[:, sl]).astype(y_ref.dtype)

    row = pl.BlockSpec((tr, HG_WIDTH), lambda i: (i, 0))
    return pl.pallas_call(
        body, name=name, grid=(S // tr,),
        in_specs=[row, row, pl.BlockSpec((tr, HG_WIDTH), lambda i: (i, 4)), pl.BlockSpec((1, HG_WIDTH), lambda i: (0, 0))],
        out_specs=row, out_shape=jax.ShapeDtypeStruct((S, HG_WIDTH), BF16), compiler_params=_params("parallel"),
    )(o_f, o_b, z, norm_g)


def _hgrn_post_bwd(dy, o_f, o_b, z, norm_g, name):
    S = z.shape[0]
    tr = min(ROW_TILE, S)

    def body(dy_ref, of_ref, ob_ref, gr_ref, ng_ref, do_ref, dgr_ref, dng_ref):
        @pl.when(pl.program_id(0) == 0)
        def _():
            dng_ref[...] = jnp.zeros_like(dng_ref)

        o = of_ref[...] + ob_ref[...]
        gr, ng, dyv = gr_ref[...], ng_ref[...], dy_ref[...]
        sg = _sigmoid(gr)
        for h in range(HG_HEADS):
            sl = slice(h * HG_DIM, (h + 1) * HG_DIM)
            oh, dyh, grh, sgh, ngh = o[:, sl], dyv[:, sl], gr[:, sl], sg[:, sl], ng[:, sl]
            rstd = lax.rsqrt(jnp.mean(oh * oh, axis=-1, keepdims=True) + EPS)
            on = oh * rstd
            du = dyh * (grh * sgh)
            dgr_ref[:, sl] = dyh * (on * ngh) * (sgh * (1.0 + grh * (1.0 - sgh)))
            dng_ref[:, sl] += jnp.sum(du * on, axis=0, keepdims=True)
            don = du * ngh
            do_ref[:, sl] = rstd * (don - on * jnp.mean(don * on, axis=-1, keepdims=True))

    row = pl.BlockSpec((tr, HG_WIDTH), lambda i: (i, 0))
    vec = pl.BlockSpec((1, HG_WIDTH), lambda i: (0, 0))
    full = jax.ShapeDtypeStruct((S, HG_WIDTH), F32)
    return pl.pallas_call(
        body, name=name, grid=(S // tr,),
        in_specs=[row, row, row, pl.BlockSpec((tr, HG_WIDTH), lambda i: (i, 4)), vec],
        out_specs=[row, row, vec], out_shape=[full, full, jax.ShapeDtypeStruct((1, HG_WIDTH), F32)],
        compiler_params=_params("arbitrary"),
    )(dy, o_f, o_b, z, norm_g)


def _t5_bucket_table():
    rel = (np.arange(3 * BLOCK)[None, :] - BLOCK) - np.arange(BLOCK)[:, None]
    nb = NUM_BUCKETS // 2
    max_exact = nb // 2
    ret = (rel > 0).astype(np.int32) * nb
    n = np.abs(rel)
    ratio = np.log(np.maximum(n, 1).astype(np.float32) / np.float32(max_exact)) / np.float32(math.log(MAX_DISTANCE / max_exact))
    large = max_exact + (ratio.astype(np.float32) * np.float32(nb - max_exact)).astype(np.int32)
    large = np.minimum(large, nb - 1)
    bucket = ret + np.where(n < max_exact, n, large)
    return bucket.astype(np.int32), (n <= WINDOW)


def _bias_table(rel_bias, name):
    bucket, in_band = _t5_bucket_table()
    idx = jnp.asarray(np.where(in_band, bucket, -1))

    def body(rb_ref, idx_ref, o_ref):
        h = pl.program_id(0)
        iv = idx_ref[...]
        acc = jnp.where(iv < 0, NEG, 0.0).astype(F32)
        for b in range(NUM_BUCKETS):
            acc = acc + jnp.where(iv == b, rb_ref[b, h], 0.0)
        o_ref[...] = acc

    return pl.pallas_call(
        body, name=name, grid=(ATT_Q_HEADS,),
        in_specs=[pl.BlockSpec(memory_space=pltpu.SMEM), pl.BlockSpec((BLOCK, 3 * BLOCK), lambda h: (0, 0))],
        out_specs=pl.BlockSpec((None, BLOCK, 3 * BLOCK), lambda h: (h, 0, 0)),
        out_shape=jax.ShapeDtypeStruct((ATT_Q_HEADS, BLOCK, 3 * BLOCK), F32), compiler_params=_params("parallel"),
    )(rel_bias, idx)


def _bias_grad(ds_sum, name):
    bucket, in_band = _t5_bucket_table()
    idx = jnp.asarray(np.where(in_band, bucket, -1))

    def body(ds_ref, idx_ref, o_ref):
        iv, ds = idx_ref[...], ds_ref[...]
        for b in range(NUM_BUCKETS):
            part = jnp.sum(jnp.where(iv == b, ds, 0.0), axis=0, keepdims=True)
            o_ref[b:b + 1, :] = part[:, 0:BLOCK] + part[:, BLOCK:2 * BLOCK] + part[:, 2 * BLOCK:3 * BLOCK]

    return pl.pallas_call(
        body, name=name, grid=(ATT_Q_HEADS,),
        in_specs=[pl.BlockSpec((None, BLOCK, 3 * BLOCK), lambda h: (h, 0, 0)), pl.BlockSpec((BLOCK, 3 * BLOCK), lambda h: (0, 0))],
        out_specs=pl.BlockSpec((None, NUM_BUCKETS, BLOCK), lambda h: (h, 0, 0)),
        out_shape=jax.ShapeDtypeStruct((ATT_Q_HEADS, NUM_BUCKETS, BLOCK), F32), compiler_params=_params("parallel"),
    )(ds_sum, idx)


def _attn_specs(nb):
    G, dh = ATT_GROUP, ATT_HEAD_DIM
    qspec = pl.BlockSpec((G, BLOCK, dh), lambda j, n: (j, n, 0))

    def kv(shift):
        return pl.BlockSpec((None, BLOCK, dh), lambda j, n: (j, jnp.clip(n + shift, 0, nb - 1), 0))

    gain = pl.BlockSpec((1, dh), lambda j, n: (0, 0))
    sink = pl.BlockSpec((G, 1, BLOCK), lambda j, n: (j, 0, 0))
    bias = pl.BlockSpec((G, BLOCK, 3 * BLOCK), lambda j, n: (j, 0, 0))
    return qspec, kv, gain, sink, bias


def _attn_probs(qh, kn, bias_h, sink_h, edge_ok):
    s = _nt(qh.astype(BF16), kn.astype(BF16)) * (1.0 / math.sqrt(ATT_HEAD_DIM)) + bias_h
    s = jnp.where(edge_ok, s, NEG)
    m = jnp.maximum(jnp.max(s, axis=-1, keepdims=True), sink_h)
    p = jnp.exp(s - m)
    e_sink = jnp.exp(sink_h - m)
    inv = 1.0 / (jnp.sum(p, axis=-1, keepdims=True) + e_sink)
    return p * inv, e_sink * inv


def _rms_rows(x):
    rstd = lax.rsqrt(jnp.mean(x * x, axis=-1, keepdims=True) + EPS)
    return x * rstd, rstd


def _edge_ok(n, nb):
    colid = lax.broadcasted_iota(jnp.int32, (BLOCK, 3 * BLOCK), 1)
    return jnp.logical_and(jnp.logical_or(colid >= BLOCK, n > 0), jnp.logical_or(colid < 2 * BLOCK, n < nb - 1))


def _attn_fwd(q, k, v, q_g, k_g, sink, bias, name):
    S = q.shape[1]
    nb = S // BLOCK
    G = ATT_GROUP
    qspec, kv, gain, sink_spec, bias_spec = _attn_specs(nb)

    def body(q_ref, k0, k1, k2, v0, v1, v2, qg_ref, kg_ref, sink_ref, bias_ref, o_ref):
        n = pl.program_id(1)
        edge_ok = _edge_ok(n, nb)
        kcat = jnp.concatenate([k0[...], k1[...], k2[...]], axis=0)
        vcat = jnp.concatenate([v0[...], v1[...], v2[...]], axis=0).astype(BF16)
        kn = _rms_rows(kcat)[0] * kg_ref[...]
        for g in range(G):
            qn = _rms_rows(q_ref[g])[0] * qg_ref[...]
            p, _ = _attn_probs(qn, kn, bias_ref[g], sink_ref[g][:, 0:1], edge_ok)
            o_ref[g] = _nn(p.astype(BF16), vcat)

    return pl.pallas_call(
        body, name=name, grid=(ATT_KV_HEADS, nb),
        in_specs=[qspec, kv(-1), kv(0), kv(1), kv(-1), kv(0), kv(1), gain, gain, sink_spec, bias_spec],
        out_specs=qspec, out_shape=jax.ShapeDtypeStruct(q.shape, F32), compiler_params=_params("parallel", "parallel"),
    )(q, k, k, k, v, v, v, q_g, k_g, sink, bias)


def _attn_bwd(q, k, v, q_g, k_g, sink, bias, do, name):
    S = q.shape[1]
    nb = S // BLOCK
    G, dh = ATT_GROUP, ATT_HEAD_DIM
    scale = 1.0 / math.sqrt(dh)
    qspec, kv, gain, sink_spec, bias_spec = _attn_specs(nb)

    def body(q_ref, k0, k1, k2, v0, v1, v2, qg_ref, kg_ref, sink_ref, bias_ref, do_ref,
             dq_ref, dkw_ref, dvw_ref, ds_ref, dsink_ref, dqg_ref):
        n = pl.program_id(1)

        @pl.when(n == 0)
        def _():
            ds_ref[...] = jnp.zeros_like(ds_ref)
            dsink_ref[...] = jnp.zeros_like(dsink_ref)
            dqg_ref[...] = jnp.zeros_like(dqg_ref)

        edge_ok = _edge_ok(n, nb)
        kcat = jnp.concatenate([k0[...], k1[...], k2[...]], axis=0)
        vcat = jnp.concatenate([v0[...], v1[...], v2[...]], axis=0).astype(BF16)
        kn = _rms_rows(kcat)[0] * kg_ref[...]
        kn_b = kn.astype(BF16)
        qg = qg_ref[...]
        dkw = jnp.zeros((3 * BLOCK, dh), F32)
        dvw = jnp.zeros((3 * BLOCK, dh), F32)
        for g in range(G):
            qhat, rstd = _rms_rows(q_ref[g])
            qn = qhat * qg
            p, p_sink = _attn_probs(qn, kn, bias_ref[g], sink_ref[g][:, 0:1], edge_ok)
            do_b = do_ref[g].astype(BF16)
            dp = _nt(do_b, vcat)
            delta = jnp.sum(p * dp, axis=-1, keepdims=True)
            ds = p * (dp - delta)
            ds_ref[g] += ds
            dsink_ref[g] += jnp.zeros((1, BLOCK), F32) - jnp.sum(p_sink * delta, axis=0, keepdims=True)
            ds_b = ds.astype(BF16)
            dvw = dvw + _tn(p.astype(BF16), do_b)
            dkw = dkw + _tn(ds_b, qn.astype(BF16)) * scale
            dqn = _nn(ds_b, kn_b) * scale
            dqg_ref[...] += jnp.sum(dqn * qhat, axis=0, keepdims=True)
            dqh = dqn * qg
            dq_ref[g] = rstd * (dqh - qhat * jnp.mean(dqh * qhat, axis=-1, keepdims=True))
        dkw_ref[...] = dkw
        dvw_ref[...] = dvw

    win = pl.BlockSpec((None, None, 3 * BLOCK, dh), lambda j, n: (j, n, 0, 0))
    wshape = jax.ShapeDtypeStruct((ATT_KV_HEADS, nb, 3 * BLOCK, dh), F32)
    return pl.pallas_call(
        body, name=name, grid=(ATT_KV_HEADS, nb),
        in_specs=[qspec, kv(-1), kv(0), kv(1), kv(-1), kv(0), kv(1), gain, gain, sink_spec, bias_spec, qspec],
        out_specs=[qspec, win, win, bias_spec, sink_spec, pl.BlockSpec((None, 1, dh), lambda j, n: (j, 0, 0))],
        out_shape=[jax.ShapeDtypeStruct(q.shape, F32), wshape, wshape,
                   jax.ShapeDtypeStruct((ATT_Q_HEADS, BLOCK, 3 * BLOCK), F32),
                   jax.ShapeDtypeStruct((ATT_Q_HEADS, 1, BLOCK), F32),
                   jax.ShapeDtypeStruct((ATT_KV_HEADS, 1, dh), F32)],
        compiler_params=_params("parallel", "arbitrary"),
    )(q, k, k, k, v, v, v, q_g, k_g, sink, bias, do)


def _attn_kv_reduce(dkw, dvw, k, k_g, name):
    S = k.shape[1]
    nb = S // BLOCK
    dh = ATT_HEAD_DIM

    def body(a0, a1, a2, b0, b1, b2, k_ref, kg_ref, dk_ref, dv_ref, dkg_ref):
        n = pl.program_id(1)

        @pl.when(n == 0)
        def _():
            dkg_ref[...] = jnp.zeros_like(dkg_ref)

        lo = jnp.where(n > 0, 1.0, 0.0)
        hi = jnp.where(n < nb - 1, 1.0, 0.0)
        dkn = a1[...] + lo * a0[...] + hi * a2[...]
        dv_ref[...] = b1[...] + lo * b0[...] + hi * b2[...]
        khat, rstd = _rms_rows(k_ref[...])
        dkg_ref[...] += jnp.sum(dkn * khat, axis=0, keepdims=True)
        dkh = dkn * kg_ref[...]
        dk_ref[...] = rstd * (dkh - khat * jnp.mean(dkh * khat, axis=-1, keepdims=True))

    def win(shift, part):
        return pl.BlockSpec((None, None, BLOCK, dh), lambda j, n: (j, jnp.clip(n + shift, 0, nb - 1), part, 0))

    blk = pl.BlockSpec((None, BLOCK, dh), lambda j, n: (j, n, 0))
    return pl.pallas_call(
        body, name=name, grid=(ATT_KV_HEADS, nb),
        in_specs=[win(-1, 2), win(0, 1), win(1, 0), win(-1, 2), win(0, 1), win(1, 0), blk, pl.BlockSpec((1, dh), lambda j, n: (0, 0))],
        out_specs=[blk, blk, pl.BlockSpec((None, 1, dh), lambda j, n: (j, 0, 0))],
        out_shape=[jax.ShapeDtypeStruct(k.shape, F32), jax.ShapeDtypeStruct(k.shape, F32),
                   jax.ShapeDtypeStruct((ATT_KV_HEADS, 1, dh), F32)],
        compiler_params=_params("parallel", "arbitrary"),
    )(dkw, dkw, dkw, dvw, dvw, dvw, k, k_g)


def _ada_fwd(c_act, w, b, name):
    n = w.shape[1]

    def body(c_ref, w_ref, b_ref, o_ref):
        o_ref[...] = _nn(c_ref[...], w_ref[...], precision=lax.Precision.HIGHEST) + b_ref[...]

    tn = n // 3
    return pl.pallas_call(
        body, name=name, grid=(3,),
        in_specs=[pl.BlockSpec(c_act.shape, lambda j: (0, 0)), pl.BlockSpec((w.shape[0], tn), lambda j: (0, j)),
                  pl.BlockSpec((1, tn), lambda j: (0, j))],
        out_specs=pl.BlockSpec((c_act.shape[0], tn), lambda j: (0, j)),
        out_shape=jax.ShapeDtypeStruct((c_act.shape[0], n), F32), compiler_params=_params("parallel"),
    )(c_act, w, b)


def _ada_wgrad(c_act_t, dm, name):
    D, nbatch = c_act_t.shape
    n = dm.shape[1]
    tr = 256

    def body(c_ref, dm_ref, o_ref):
        cv, dv = c_ref[...], dm_ref[...]
        acc = cv[:, 0:1] * dv[0:1, :]
        for b in range(1, nbatch):
            acc = acc + cv[:, b:b + 1] * dv[b:b + 1, :]
        o_ref[...] = acc

    return pl.pallas_call(
        body, name=name, grid=(D // tr,),
        in_specs=[pl.BlockSpec((tr, nbatch), lambda i: (i, 0)), pl.BlockSpec((nbatch, n), lambda i: (0, 0))],
        out_specs=pl.BlockSpec((tr, n), lambda i: (i, 0)), out_shape=jax.ShapeDtypeStruct((D, n), F32),
        compiler_params=_params("parallel"),
    )(c_act_t, dm)


def _adamw(w, g, m, v, name):
    R, Cn = w.shape
    tr = R
    for cand in (256, 128, 64, 32, 16, 8):
        if R % cand == 0:
            tr = cand
            break

    def body(w_ref, g_ref, m_ref, v_ref, d_ref, nm_ref, nv_ref):
        gv = g_ref[...]
        m_new = ADAM_B1 * m_ref[...] + (1.0 - ADAM_B1) * gv
        v_new = ADAM_B2 * v_ref[...] + (1.0 - ADAM_B2) * (gv * gv)
        m_hat = m_new / (1.0 - ADAM_B1 ** ADAM_STEP)
        v_hat = v_new / (1.0 - ADAM_B2 ** ADAM_STEP)
        d_ref[...] = -ADAM_LR * (m_hat / (jnp.sqrt(v_hat) + ADAM_EPS) + ADAM_WD * w_ref[...])
        nm_ref[...] = m_new
        nv_ref[...] = v_new

    blk = pl.BlockSpec((tr, Cn), lambda i: (i, 0))
    shp = jax.ShapeDtypeStruct((R, Cn), F32)
    return pl.pallas_call(
        body, name=name, grid=(R // tr,), in_specs=[blk] * 4, out_specs=[blk] * 3, out_shape=[shp] * 3,
        compiler_params=_params("parallel"),
    )(w, g, m, v)


def _place():
    return lax.axis_index("x"), lax.axis_index("y"), lax.axis_index("c")


def _flip(place, k):
    x, y, c = place
    return (1 - x if k & 4 else x, 1 - y if k & 2 else y, 1 - c if k & 1 else c)


def _dev_index(place):
    x, y, c = place
    return 4 * x + 2 * y + c


def _chip_index(place):
    return 2 * place[0] + place[1]


def _allgather8(x, name, reduce=False):
    R, Cn = x.shape

    def body(x_ref, *rest):
        if reduce:
            out_ref, sum_ref, send_sems, recv_sems, local_sem = rest
        else:
            out_ref, send_sems, recv_sems, local_sem = rest
        me = _place()
        mine = pltpu.make_async_copy(x_ref, out_ref.at[_dev_index(me)], local_sem)
        mine.start()

        def copy(k, origin, to):
            return pltpu.make_async_remote_copy(
                src_ref=x_ref, dst_ref=out_ref.at[_dev_index(origin)], send_sem=send_sems.at[k - 1],
                recv_sem=recv_sems.at[k - 1], device_id=to, device_id_type=MESH)

        sends = [copy(k, me, _flip(me, k)) for k in range(1, 8)]
        for cp in sends:
            cp.start()
        for k in range(1, 8):
            copy(k, _flip(me, k), me).wait_recv()
        for cp in sends:
            cp.wait_send()
        mine.wait()
        if reduce:
            acc = out_ref[0]
            for i in range(1, 8):
                acc = acc + out_ref[i]
            sum_ref[...] = acc

    vm = pl.BlockSpec(memory_space=pltpu.VMEM)
    outs = [jax.ShapeDtypeStruct((8, R, Cn), F32)] + ([jax.ShapeDtypeStruct((R, Cn), F32)] if reduce else [])
    res = pl.pallas_call(
        body, name=name, in_specs=[vm], out_specs=[vm] * len(outs), out_shape=outs,
        scratch_shapes=[pltpu.SemaphoreType.DMA((7,)), pltpu.SemaphoreType.DMA((7,)), pltpu.SemaphoreType.DMA],
    )(x)
    return res if reduce else res[0]


def _weights_allgather(shards, name):
    n = len(shards)

    def body(*refs):
        in_refs, out_refs = refs[:n], refs[n:2 * n]
        send_sems, recv_sems, local_sems = refs[2 * n:]
        me = _place()
        c = me[2]
        sibling = _flip(me, 1)
        others = [_flip(me, 2 * j) for j in (1, 2, 3)]

        def copy(a, k, src, dst, to):
            return pltpu.make_async_remote_copy(
                src_ref=src, dst_ref=dst, send_sem=send_sems.at[7 * a + k], recv_sem=recv_sems.at[7 * a + k],
                device_id=to, device_id_type=MESH)

        def block(a, place, half):
            return out_refs[a].at[_chip_index(place), half]

        started, local = [], []
        for a in range(n):
            src = in_refs[a].at[c]
            mine = pltpu.make_async_copy(src, block(a, me, c), local_sems.at[a])
            mine.start()
            local.append(mine)
            for k, to in enumerate([sibling] + others):
                cp = copy(a, k, src, block(a, me, c), to)
                cp.start()
                started.append(cp)
        for a in range(n):
            for j, other in enumerate(others):
                landed = block(a, other, c)
                copy(a, 1 + j, landed, landed, me).wait_recv()
                fwd = copy(a, 4 + j, landed, landed, sibling)
                fwd.start()
                started.append(fwd)
        for a in range(n):
            for k, origin in enumerate([me] + others):
                got = block(a, origin, 1 - c)
                copy(a, 0 if k == 0 else 3 + k, got, got, me).wait_recv()
        for cp in started:
            cp.wait_send()
        for cp in local:
            cp.wait()

    return pl.pallas_call(
        body, name=name, in_specs=[ANY] * n, out_specs=[ANY] * n,
        out_shape=[jax.ShapeDtypeStruct((N_CHIPS,) + s.shape, s.dtype) for s in shards],
        scratch_shapes=[pltpu.SemaphoreType.DMA((7 * n,)), pltpu.SemaphoreType.DMA((7 * n,)), pltpu.SemaphoreType.DMA((n,))],
    )(*shards)


def _halves_exchange(grads, name):
    n = len(grads)

    def body(*refs):
        in_refs, got_refs = refs[:n], refs[n:2 * n]
        send_sems, recv_sems = refs[2 * n:]
        me = _place()
        c = me[2]
        sibling = _flip(me, 1)
        started = []
        for a in range(n):
            for kk in range(N_CHIPS):
                i = N_CHIPS * a + kk
                send = pltpu.make_async_remote_copy(
                    src_ref=in_refs[a].at[kk, 1 - c], dst_ref=got_refs[a].at[kk], send_sem=send_sems.at[i],
                    recv_sem=recv_sems.at[i], device_id=sibling, device_id_type=MESH)
                send.start()
                started.append(send)
        for send in started:
            send.wait_recv()
        for send in started:
            send.wait_send()

    return pl.pallas_call(
        body, name=name, in_specs=[ANY] * n, out_specs=[ANY] * n,
        out_shape=[jax.ShapeDtypeStruct((N_CHIPS,) + g.shape[2:], g.dtype) for g in grads],
        scratch_shapes=[pltpu.SemaphoreType.DMA((N_CHIPS * n,)), pltpu.SemaphoreType.DMA((N_CHIPS * n,))],
    )(*grads)


def _chips_exchange(parts, name):
    n = len(parts)

    def body(*refs):
        in_refs, out_refs = refs[:n], refs[n:2 * n]
        send_sems, recv_sems = refs[2 * n:]
        me = _place()
        started = []
        for a in range(n):
            for j in (1, 2, 3):
                peer = _flip(me, 2 * j)
                send = pltpu.make_async_remote_copy(
                    src_ref=in_refs[a].at[_chip_index(peer)], dst_ref=out_refs[a].at[j - 1],
                    send_sem=send_sems.at[3 * a + j - 1], recv_sem=recv_sems.at[3 * a + j - 1],
                    device_id=peer, device_id_type=MESH)
                send.start()
                started.append(send)
        for send in started:
            send.wait_recv()
        for send in started:
            send.wait_send()

    return pl.pallas_call(
        body, name=name, in_specs=[ANY] * n, out_specs=[ANY] * n,
        out_shape=[jax.ShapeDtypeStruct((3,) + p.shape[1:], p.dtype) for p in parts],
        scratch_shapes=[pltpu.SemaphoreType.DMA((3 * n,)), pltpu.SemaphoreType.DMA((3 * n,))],
    )(*parts)


def _siblings_exchange(halves, name):
    n = len(halves)

    def body(*refs):
        in_refs, out_refs = refs[:n], refs[n:2 * n]
        send_sems, recv_sems = refs[2 * n:]
        sibling = _flip(_place(), 1)
        started = []
        for a in range(n):
            send = pltpu.make_async_remote_copy(
                src_ref=in_refs[a], dst_ref=out_refs[a], send_sem=send_sems.at[a], recv_sem=recv_sems.at[a],
                device_id=sibling, device_id_type=MESH)
            send.start()
            started.append(send)
        for send in started:
            send.wait_recv()
        for send in started:
            send.wait_send()

    return pl.pallas_call(
        body, name=name, in_specs=[ANY] * n, out_specs=[ANY] * n,
        out_shape=[jax.ShapeDtypeStruct(h.shape, h.dtype) for h in halves],
        scratch_shapes=[pltpu.SemaphoreType.DMA((n,)), pltpu.SemaphoreType.DMA((n,))],
    )(*halves)


def _row_tile(rows):
    for cand in (256, 128, 88, 64, 32, 16, 8):
        if rows % cand == 0:
            return cand
    return rows


def _pair_sum(core, grad, theirs, name):
    N, _, R, Cn = grad.shape
    tr = _row_tile(R)

    def body(core_ref, g_ref, t_ref, o_ref):
        o_ref[...] = g_ref[...] + t_ref[...]

    return pl.pallas_call(
        body, name=name,
        grid_spec=pltpu.PrefetchScalarGridSpec(
            num_scalar_prefetch=1, grid=(N, R // tr),
            in_specs=[pl.BlockSpec((None, None, tr, Cn), lambda k, i, core_ref: (k, core_ref[0], i, 0)),
                      pl.BlockSpec((None, tr, Cn), lambda k, i, core_ref: (k, i, 0))],
            out_specs=pl.BlockSpec((None, tr, Cn), lambda k, i, core_ref: (k, i, 0))),
        out_shape=jax.ShapeDtypeStruct((N, R, Cn), F32), compiler_params=_params("parallel", "parallel"),
    )(core, grad, theirs)


def _chip_sum(chip, parts, landed, name):
    _, R, Cn = parts.shape
    tr = _row_tile(R)

    def body(chip_ref, p_ref, l_ref, o_ref):
        o_ref[...] = ((p_ref[...] + l_ref[0]) + l_ref[1]) + l_ref[2]

    return pl.pallas_call(
        body, name=name,
        grid_spec=pltpu.PrefetchScalarGridSpec(
            num_scalar_prefetch=1, grid=(R // tr,),
            in_specs=[pl.BlockSpec((None, tr, Cn), lambda i, chip_ref: (chip_ref[0], i, 0)),
                      pl.BlockSpec((3, tr, Cn), lambda i, chip_ref: (0, i, 0))],
            out_specs=pl.BlockSpec((tr, Cn), lambda i, chip_ref: (i, 0))),
        out_shape=jax.ShapeDtypeStruct((R, Cn), F32), compiler_params=_params("parallel"),
    )(chip, parts, landed)


def _reduce_scatter(grads, core, chip, tag):
    theirs = _halves_exchange(grads, f"{tag}_halves_exchange")
    parts = [_pair_sum(core, g, t, f"{tag}_pair_sum_{i}") for i, (g, t) in enumerate(zip(grads, theirs))]
    landed = _chips_exchange(parts, f"{tag}_chips_exchange")
    halves = [_chip_sum(chip, p, l, f"{tag}_chip_sum_{i}") for i, (p, l) in enumerate(zip(parts, landed))]
    return list(zip(halves, _siblings_exchange(halves, f"{tag}_siblings_exchange")))


def _adamw_halves(core, w, g_mine, g_theirs, m, v, name):
    R2, Cn = w.shape
    r = R2 // 2
    tr = _row_tile(r)
    nt = r // tr

    def body(core_ref, w_ref, gm_ref, gt_ref, m_ref, v_ref, g_ref, d_ref, nm_ref, nv_ref):
        gv = jnp.where(pl.program_id(0) == core_ref[0], gm_ref[...], gt_ref[...])
        g_ref[...] = gv
        m_new = ADAM_B1 * m_ref[...] + (1.0 - ADAM_B1) * gv
        v_new = ADAM_B2 * v_ref[...] + (1.0 - ADAM_B2) * (gv * gv)
        m_hat = m_new / (1.0 - ADAM_B1 ** ADAM_STEP)
        v_hat = v_new / (1.0 - ADAM_B2 ** ADAM_STEP)
        d_ref[...] = -ADAM_LR * (m_hat / (jnp.sqrt(v_hat) + ADAM_EPS) + ADAM_WD * w_ref[...])
        nm_ref[...] = m_new
        nv_ref[...] = v_new

    full = pl.BlockSpec((tr, Cn), lambda hf, i, core_ref: (hf * nt + i, 0))
    half = pl.BlockSpec((tr, Cn), lambda hf, i, core_ref: (i, 0))
    shp = jax.ShapeDtypeStruct((R2, Cn), F32)
    return pl.pallas_call(
        body, name=name,
        grid_spec=pltpu.PrefetchScalarGridSpec(
            num_scalar_prefetch=1, grid=(2, nt), in_specs=[full, half, half, full, full], out_specs=[full] * 4),
        out_shape=[shp] * 4, compiler_params=_params("parallel", "parallel"),
    )(core, w, g_mine, g_theirs, m, v)


def _pad_row(v, width):
    v = v.reshape(1, -1)
    return jnp.pad(v, ((0, 0), (0, width - v.shape[1])))


def _ffn_forward(x, ng, shift, scale, gate, w_in4, w_out, tag):
    h = _rmsmod_fwd(x, ng, shift, scale, f"{tag}_norm")
    zg, zu, a = _ffn_in_fwd(h, w_in4, f"{tag}_in")
    x_new, f = _proj_out_fwd([a], w_out, x, gate, 0.5, f"{tag}_out")
    return x_new, (h, zg, zu, a, f)


def _ffn_backward(df, saved, w_in4, w_out, tag):
    h, zg, zu, a, _ = saved
    dzg, dzu = _dact_bwd(df, w_out, zg, zu, f"{tag}_dact")
    dw_out = _wgrad(a, df, 512, f"{tag}_dw_out")
    dw_out = jnp.concatenate([dw_out[0], dw_out[1]], axis=1)
    dh = _ffn_in_dgrad(dzg, dzu, w_in4, f"{tag}_dh")
    dw_in = jnp.concatenate([_wgrad(h, dzg, FF_SHARD, f"{tag}_dw_gate"), _wgrad(h, dzu, FF_SHARD, f"{tag}_dw_up")], axis=0)
    return dh, dw_in, dw_out


def kernel(x, c, w_ada, b_ada, norm_g, w_ffn1_in, w_ffn1_out, w_ffn2_in, w_ffn2_out, w_mix_in, w_mix_out, hgrn_lb, hgrn_norm_g, qk_norm_g, attn_sink, rel_bias, loss_target, m_w_ada, m_b_ada, m_norm_g, m_w_ffn1_in, m_w_ffn1_out, m_w_ffn2_in, m_w_ffn2_out, m_w_mix_in, m_w_mix_out, m_hgrn_lb, m_hgrn_norm_g, m_qk_norm_g, m_attn_sink, m_rel_bias, v_w_ada, v_b_ada, v_norm_g, v_w_ffn1_in, v_w_ffn1_out, v_w_ffn2_in, v_w_ffn2_out, v_w_mix_in, v_w_mix_out, v_hgrn_lb, v_hgrn_norm_g, v_qk_norm_g, v_attn_sink, v_rel_bias):
    D = D_MODEL
    S = x.shape[1]
    place = (lax.axis_index("x"), lax.axis_index("y"), lax.axis_index("c"))
    me, my_chip = _dev_index(place), _chip_index(place)
    x0 = x[0]
    target = loss_target[0]

    def halves(w):
        return w.astype(BF16).reshape(2, w.shape[0] // 2, w.shape[1])

    gathered = _weights_allgather(
        [halves(w_ffn1_in[0]), halves(w_ffn1_out[0]), halves(w_ffn2_in[0]), halves(w_ffn2_out[0]), halves(w_mix_in[0]),
         halves(w_mix_out[0])], "weights_allgather")
    w1_in = gathered[0].reshape(N_CHIPS, D, FF_SHARD)
    w1_out = gathered[1].reshape(D_FF, D)
    w2_in = gathered[2].reshape(N_CHIPS, D, FF_SHARD)
    w2_out = gathered[3].reshape(D_FF, D)
    wm_in = gathered[4].reshape(N_CHIPS, D, D_IN // N_CHIPS).transpose(1, 0, 2).reshape(D, D_IN)
    wm_out = gathered[5].reshape(D, D)

    small = jnp.concatenate([_pad_row(c, D), _pad_row(norm_g, D), _pad_row(hgrn_lb, D), jnp.zeros((5, D), F32)], axis=0)
    small_all = _allgather8(small, "small_allgather")
    c_all = small_all[:, 0, :]
    by_chip = small_all[0::2]
    norm_g_full = by_chip[:, 1, :3 * 256].reshape(N_CHIPS, 3, 256).transpose(1, 0, 2).reshape(3, D)
    lb_raw = by_chip[:, 2, :2 * 2 * 128].reshape(N_CHIPS, 2, 2, 128).transpose(1, 2, 0, 3).reshape(2, 2, HG_WIDTH)
    lb = jax.nn.sigmoid(lb_raw[:, 0, :] - lb_raw[:, 1, :])
    lb_f, lb_b = lb[0:1], lb[1:2]

    c_act_all = c_all * jax.nn.sigmoid(c_all)
    n_ada = w_ada.shape[2]
    b_mine = lax.dynamic_slice_in_dim(b_ada, my_chip * n_ada, n_ada, axis=1)
    mods_part = _ada_fwd(c_act_all, w_ada[0], b_mine, "ada_fwd")
    mods_all = _allgather8(mods_part, "mods_allgather")[0::2].transpose(1, 0, 2).reshape(8, N_MOD * D)
    mods = lax.dynamic_slice_in_dim(mods_all, me, 1, axis=0)
    sh1, sc1, g1, sh2, sc2, g2, sh3, sc3, g3 = [mods[:, i * D:(i + 1) * D] for i in range(N_MOD)]

    x1, saved1 = _ffn_forward(x0, norm_g_full[0:1], sh1, sc1, g1, w1_in, w1_out, "ffn1")

    h2 = _rmsmod_fwd(x1, norm_g_full[1:2], sh2, sc2, "mix_norm")
    z = _matmul_nn(h2, wm_in, F32, 256, "mix_in")
    of, st_f = _hgrn_fwd(z, lb_f, 0, "hgrn_fwd_f")
    ob, st_b = _hgrn_fwd(z, lb_b, 1, "hgrn_fwd_b")
    o_h = _hgrn_post_fwd(of, ob, z, hgrn_norm_g, "hgrn_post")

    def to_heads(t, nh):
        return t.reshape(S, nh, ATT_HEAD_DIM).transpose(1, 0, 2)

    aq = to_heads(z[:, 5 * HG_WIDTH:5 * HG_WIDTH + ATT_WIDTH], ATT_Q_HEADS)
    ak = to_heads(z[:, 5 * HG_WIDTH + ATT_WIDTH:5 * HG_WIDTH + ATT_WIDTH + KV_WIDTH], ATT_KV_HEADS)
    av = to_heads(z[:, 5 * HG_WIDTH + ATT_WIDTH + KV_WIDTH:], ATT_KV_HEADS)
    q_g, k_g = qk_norm_g[0, 0:1], qk_norm_g[0, 1:2]
    sink_b = jnp.broadcast_to(attn_sink.reshape(ATT_Q_HEADS, 1, 1), (ATT_Q_HEADS, 1, BLOCK))
    bias = _bias_table(rel_bias, "bias_table")
    o_attn = _attn_fwd(aq, ak, av, q_g, k_g, sink_b, bias, "attn_fwd")
    o_a = o_attn.transpose(1, 0, 2).reshape(S, ATT_WIDTH).astype(BF16)
    x2, mixed = _proj_out_fwd([o_h, o_a], wm_out, x1, g2, 1.0, "mix_out")

    x3, saved3 = _ffn_forward(x2, norm_g_full[2:3], sh3, sc3, g3, w2_in, w2_out, "ffn2")

    dx3, df3, dg3, sq_cols = _loss_bwd(x3, target, saved3[4], g3, 0.5, "loss")
    loss_mine = 0.5 * jnp.sum(sq_cols) / D

    dh3, dw2_in, dw2_out = _ffn_backward(df3, saved3, w2_in, w2_out, "ffn2")
    dx2, dsh3, dsc3, dng3, dmixed, dg2 = _rmsmod_bwd(dh3, x2, norm_g_full[2:3], sc3, dx3, "ffn2_norm_bwd", below=(mixed, g2, 1.0))

    do_cat = _matmul_nt(dmixed, wm_out, ROW_TILE, "mix_out_dgrad")
    dwm_out = jnp.concatenate([_wgrad(o_h, dmixed, 512, "mix_out_dw_h"), _wgrad(o_a, dmixed, 512, "mix_out_dw_a")], axis=1)
    dwm_out = jnp.concatenate([dwm_out[0], dwm_out[1]], axis=1)

    do_sum, dgr, d_hnorm = _hgrn_post_bwd(do_cat, of, ob, z, hgrn_norm_g, "hgrn_post_bwd")
    dq_f, dff, dv_f, doml_f = _hgrn_bwd(z, lb_f, do_sum, st_f, 0, "hgrn_bwd_f")
    dhq, dfb, dhi, doml_b = _hgrn_bwd(z, lb_b, do_sum, st_b, 1, "hgrn_bwd_b", acc=(dq_f, dv_f))

    do_a = to_heads(do_cat[:, HG_WIDTH:], ATT_Q_HEADS)
    daq, dkw, dvw, ds_sum, dsink, dqg = _attn_bwd(aq, ak, av, q_g, k_g, sink_b, bias, do_a, "attn_bwd")
    dak, dav, dkg = _attn_kv_reduce(dkw, dvw, ak, k_g, "attn_kv_reduce")
    d_rel_bias = jnp.sum(_bias_grad(ds_sum, "bias_grad"), axis=-1).T

    def from_heads(t):
        return t.transpose(1, 0, 2).reshape(S, -1)

    dz = jnp.concatenate([dhq, dff, dfb, dhi, dgr, from_heads(daq), from_heads(dak), from_heads(dav)], axis=1).astype(BF16)
    dh2 = _matmul_nt(dz, wm_in, 256, "mix_in_dgrad")
    dwm_in = _wgrad(h2, dz, D_IN // 2, "mix_in_dw")
    dwm_in = jnp.concatenate([dwm_in[0], dwm_in[1]], axis=1)
    dx1, dsh2, dsc2, dng2, df1, dg1 = _rmsmod_bwd(dh2, x1, norm_g_full[1:2], sc2, dx2, "mix_norm_bwd", below=(saved1[4], g1, 0.5))

    dh1, dw1_in, dw1_out = _ffn_backward(df1, saved1, w1_in, w1_out, "ffn1")
    dx0, dsh1, dsc1, dng1 = _rmsmod_bwd(dh1, x0, norm_g_full[0:1], sc1, dx1, "ffn1_norm_bwd")

    dlb = -jnp.concatenate([doml_f, doml_b], axis=0)
    dlb_raw = dlb * lb * (1.0 - lb)
    d_hgrn_lb = jnp.stack([dlb_raw, -dlb_raw], axis=1)
    d_qk = jnp.concatenate([jnp.sum(dqg, axis=0), jnp.sum(dkg, axis=0)], axis=0)
    dmods = jnp.concatenate([dsh1, dsc1, dg1, dsh2, dsc2, dg2, dsh3, dsc3, dg3], axis=0)
    packed = jnp.concatenate(
        [dmods, dng1, dng2, dng3, d_hgrn_lb.reshape(2, D), _pad_row(d_hnorm, D), _pad_row(d_qk, D),
         _pad_row(dsink[:, 0, 0], D), _pad_row(d_rel_bias, D), _pad_row(loss_mine, D)], axis=0)
    packed = jnp.pad(packed, ((0, 24 - packed.shape[0]), (0, 0)))
    packed_all, packed_sum = _allgather8(packed, "small_grads_allgather", reduce=True)
    dmods_all = packed_all[:, 0:N_MOD, :].reshape(8, N_MOD * D)
    g_b_ada = packed_sum[0:N_MOD].reshape(1, N_MOD * D)
    g_norm_full = packed_sum[9:12]
    g_norm_g = lax.dynamic_slice_in_dim(g_norm_full, my_chip * 256, 256, axis=1).reshape(1, 3, 256)
    g_hgrn_lb = lax.dynamic_slice_in_dim(packed_sum[12:14].reshape(2, 2, HG_WIDTH), my_chip * 128, 128, axis=2)
    g_hgrn_norm_g = packed_sum[14:15, :HG_WIDTH]
    g_qk_norm_g = packed_sum[15, :2 * ATT_HEAD_DIM].reshape(1, 2, ATT_HEAD_DIM)
    g_attn_sink = packed_sum[16:17, :ATT_Q_HEADS]
    g_rel_bias = packed_sum[17, :NUM_BUCKETS * ATT_Q_HEADS].reshape(NUM_BUCKETS, ATT_Q_HEADS)
    loss = packed_sum[18, 0]

    dm_mine = lax.dynamic_slice_in_dim(dmods_all, my_chip * n_ada, n_ada, axis=1)
    g_w_ada = _ada_wgrad(c_act_all.T, dm_mine, "ada_wgrad")[None]

    def by_chip_rows(g):
        return g.reshape(N_CHIPS, 2, g.shape[0] // (2 * N_CHIPS), g.shape[1])

    def by_chip_cols(g):
        return g.reshape(N_CHIPS, 2, g.shape[1] // 2, g.shape[2])

    wide = D_IN // N_CHIPS
    core_arr = jnp.reshape(place[2], (1,)).astype(jnp.int32)
    chip_arr = jnp.reshape(my_chip, (1,)).astype(jnp.int32)
    reduced = _reduce_scatter(
        [by_chip_cols(dw1_in), by_chip_rows(dw1_out), by_chip_cols(dw2_in), by_chip_rows(dw2_out),
         by_chip_cols(dwm_in.reshape(D, N_CHIPS, wide).transpose(1, 0, 2)), by_chip_rows(dwm_out)], core_arr, chip_arr, "grads")

    def big(w, g, m, v, name):
        d, nm, nv = _adamw(w[0], g[0], m[0], v[0], name)
        return d[None], nm[None], nv[None]

    def big_halves(w, g_pair, m, v, name):
        g, d, nm, nv = _adamw_halves(core_arr, w[0], g_pair[0], g_pair[1], m[0], v[0], name)
        return g[None], (d[None], nm[None], nv[None])

    g_w1_in, u_w1_in = big_halves(w_ffn1_in, reduced[0], m_w_ffn1_in, v_w_ffn1_in, "adamw_w_ffn1_in")
    g_w1_out, u_w1_out = big_halves(w_ffn1_out, reduced[1], m_w_ffn1_out, v_w_ffn1_out, "adamw_w_ffn1_out")
    g_w2_in, u_w2_in = big_halves(w_ffn2_in, reduced[2], m_w_ffn2_in, v_w_ffn2_in, "adamw_w_ffn2_in")
    g_w2_out, u_w2_out = big_halves(w_ffn2_out, reduced[3], m_w_ffn2_out, v_w_ffn2_out, "adamw_w_ffn2_out")
    g_wm_in, u_wm_in = big_halves(w_mix_in, reduced[4], m_w_mix_in, v_w_mix_in, "adamw_w_mix_in")
    g_wm_out, u_wm_out = big_halves(w_mix_out, reduced[5], m_w_mix_out, v_w_mix_out, "adamw_w_mix_out")

    smalls = [(b_ada, g_b_ada, m_b_ada, v_b_ada), (norm_g, g_norm_g, m_norm_g, v_norm_g), (hgrn_lb, g_hgrn_lb, m_hgrn_lb, v_hgrn_lb),
              (hgrn_norm_g, g_hgrn_norm_g, m_hgrn_norm_g, v_hgrn_norm_g), (qk_norm_g, g_qk_norm_g, m_qk_norm_g, v_qk_norm_g),
              (attn_sink, g_attn_sink, m_attn_sink, v_attn_sink), (rel_bias, g_rel_bias, m_rel_bias, v_rel_bias)]
    sizes = [t[0].size for t in smalls]
    total = sum(sizes)
    rows = -(-total // 128)
    rows = -(-rows // 8) * 8

    def pack(i):
        flat = jnp.concatenate([t[i].reshape(-1) for t in smalls])
        fill = 1.0 if i == 3 else 0.0
        return jnp.pad(flat, (0, rows * 128 - total), constant_values=fill).reshape(rows, 128)

    packed_out = _adamw(pack(0), pack(1), pack(2), pack(3), "adamw_small")

    def unpack(flat2d):
        flat = flat2d.reshape(-1)
        outs, off = [], 0
        for t, n in zip(smalls, sizes):
            outs.append(flat[off:off + n].reshape(t[0].shape))
            off += n
        return outs

    d_small, m_small, v_small = [unpack(t) for t in packed_out]

    upd = {
        "w_ada": big(w_ada, g_w_ada, m_w_ada, v_w_ada, "adamw_w_ada"),
        "w_ffn1_in": u_w1_in, "w_ffn1_out": u_w1_out, "w_ffn2_in": u_w2_in, "w_ffn2_out": u_w2_out,
        "w_mix_in": u_wm_in, "w_mix_out": u_wm_out,
    }
    small_names = ["b_ada", "norm_g", "hgrn_lb", "hgrn_norm_g", "qk_norm_g", "attn_sink", "rel_bias"]
    for i, nme in enumerate(small_names):
        upd[nme] = (d_small[i], m_small[i], v_small[i])
    grads = {
        "w_ada": g_w_ada, "b_ada": g_b_ada, "norm_g": g_norm_g, "w_ffn1_in": g_w1_in, "w_ffn1_out": g_w1_out,
        "w_ffn2_in": g_w2_in, "w_ffn2_out": g_w2_out, "w_mix_in": g_wm_in, "w_mix_out": g_wm_out, "hgrn_lb": g_hgrn_lb,
        "hgrn_norm_g": g_hgrn_norm_g, "qk_norm_g": g_qk_norm_g, "attn_sink": g_attn_sink, "rel_bias": g_rel_bias,
    }
    order = ["w_ada", "b_ada", "norm_g", "w_ffn1_in", "w_ffn1_out", "w_ffn2_in", "w_ffn2_out", "w_mix_in", "w_mix_out",
             "hgrn_lb", "hgrn_norm_g", "qk_norm_g", "attn_sink", "rel_bias"]
    return (loss, dx0[None], *[grads[k] for k in order], *[upd[k][0] for k in order], *[upd[k][1] for k in order],
            *[upd[k][2] for k in order])
```

```python
import functools
import math

import numpy as np
import jax
import jax.numpy as jnp
from jax import lax
from jax.experimental import pallas as pl
from jax.experimental.pallas import tpu as pltpu

F32, BF16 = jnp.float32, jnp.bfloat16

D_MODEL = 1024
D_FF = 2816
HG_HEADS, HG_DIM = 4, 128
HG_WIDTH = HG_HEADS * HG_DIM
ATT_Q_HEADS, ATT_KV_HEADS, ATT_HEAD_DIM = 8, 2, 64
ATT_GROUP = ATT_Q_HEADS // ATT_KV_HEADS
ATT_WIDTH = ATT_Q_HEADS * ATT_HEAD_DIM
KV_WIDTH = ATT_KV_HEADS * ATT_HEAD_DIM
WINDOW, BLOCK = 128, 128
NUM_BUCKETS, MAX_DISTANCE = 32, 128
N_MOD = 9
EPS = 1e-6
D_IN = 5 * HG_WIDTH + ATT_WIDTH + 2 * KV_WIDTH
ADAM_LR, ADAM_B1, ADAM_B2, ADAM_EPS, ADAM_WD, ADAM_STEP = 0.001, 0.9, 0.999, 1e-08, 0.01, 10

N_CHIPS = 4
FF_SHARD = 2 * D_FF // N_CHIPS
NEG = -1e30

VMEM_LIMIT_BYTES = 56 << 20
ROW_TILE = 512
HG_CHUNK = 16
HG_ROWS = 256

MESH = pl.DeviceIdType.MESH
ANY = pl.BlockSpec(memory_space=pl.ANY)


def _params(*sem):
    return pltpu.CompilerParams(dimension_semantics=sem, vmem_limit_bytes=VMEM_LIMIT_BYTES)


def _resident(shape, index_map):
    return pl.BlockSpec(shape, index_map, pipeline_mode=pl.Buffered(1))


def _dot(a, b, dims, precision=None):
    return lax.dot_general(a, b, (dims, ((), ())), precision=precision, preferred_element_type=F32)


def _nn(a, b, precision=None):
    return _dot(a, b, ((1,), (0,)), precision)


def _nt(a, b):
    return _dot(a, b, ((1,), (1,)))


def _tn(a, b):
    return _dot(a, b, ((0,), (0,)))


def _sigmoid(x):
    return jax.nn.sigmoid(x)


def _rmsmod_fwd(x, g, shift, scale, name):
    S, D = x.shape
    tr = min(ROW_TILE, S)

    def body(x_ref, g_ref, sh_ref, sc_ref, h_ref):
        xv = x_ref[...]
        rstd = lax.rsqrt(jnp.mean(xv * xv, axis=-1, keepdims=True) + EPS)
        y = xv * rstd * g_ref[...]
        h_ref[...] = (y * (1.0 + sc_ref[...]) + sh_ref[...]).astype(h_ref.dtype)

    row = pl.BlockSpec((tr, D), lambda i: (i, 0))
    vec = pl.BlockSpec((1, D), lambda i: (0, 0))
    return pl.pallas_call(
        body, name=name, grid=(S // tr,), in_specs=[row, vec, vec, vec], out_specs=row,
        out_shape=jax.ShapeDtypeStruct((S, D), BF16), compiler_params=_params("parallel"),
    )(x, g, shift, scale)


def _rmsmod_bwd(dh, x, g, scale, dx_res, name, below=None):
    S, D = x.shape
    tr = min(ROW_TILE, S)
    coef = below[2] if below else None

    def body(*refs):
        if below:
            dh_ref, x_ref, g_ref, sc_ref, dxr_ref, f_ref, gate_ref, dx_ref, dsh_ref, dsc_ref, dg_ref, df_ref, dgate_ref = refs
        else:
            dh_ref, x_ref, g_ref, sc_ref, dxr_ref, dx_ref, dsh_ref, dsc_ref, dg_ref = refs

        @pl.when(pl.program_id(0) == 0)
        def _():
            dsh_ref[...] = jnp.zeros_like(dsh_ref)
            dsc_ref[...] = jnp.zeros_like(dsc_ref)
            dg_ref[...] = jnp.zeros_like(dg_ref)
            if below:
                dgate_ref[...] = jnp.zeros_like(dgate_ref)

        dhv, xv, gv = dh_ref[...], x_ref[...], g_ref[...]
        one_sc = 1.0 + sc_ref[...]
        rstd = lax.rsqrt(jnp.mean(xv * xv, axis=-1, keepdims=True) + EPS)
        n = xv * rstd
        dsh_ref[...] += jnp.sum(dhv, axis=0, keepdims=True)
        dsc_ref[...] += jnp.sum(dhv * n, axis=0, keepdims=True) * gv
        dg_ref[...] += jnp.sum(dhv * n, axis=0, keepdims=True) * one_sc
        dn = dhv * (gv * one_sc)
        dx = dxr_ref[...] + rstd * (dn - n * jnp.mean(dn * n, axis=-1, keepdims=True))
        dx_ref[...] = dx
        if below:
            df_ref[...] = (coef * gate_ref[...] * dx).astype(df_ref.dtype)
            dgate_ref[...] += coef * jnp.sum(dx * f_ref[...].astype(F32), axis=0, keepdims=True)

    row = pl.BlockSpec((tr, D), lambda i: (i, 0))
    vec = pl.BlockSpec((1, D), lambda i: (0, 0))
    vshape = jax.ShapeDtypeStruct((1, D), F32)
    ins, in_specs = [dh, x, g, scale, dx_res], [row, row, vec, vec, row]
    outs, out_specs = [jax.ShapeDtypeStruct((S, D), F32), vshape, vshape, vshape], [row, vec, vec, vec]
    if below:
        ins += [below[0], below[1]]
        in_specs += [row, vec]
        outs += [jax.ShapeDtypeStruct((S, D), BF16), vshape]
        out_specs += [row, vec]
    return pl.pallas_call(
        body, name=name, grid=(S // tr,), in_specs=in_specs, out_specs=out_specs, out_shape=outs,
        compiler_params=_params("arbitrary"),
    )(*ins)


def _loss_bwd(y, target, f, gate, coef, name):
    S, D = y.shape
    tr = min(ROW_TILE, S)

    def body(y_ref, t_ref, f_ref, gate_ref, dy_ref, df_ref, dgate_ref, sq_ref):
        @pl.when(pl.program_id(0) == 0)
        def _():
            dgate_ref[...] = jnp.zeros_like(dgate_ref)
            sq_ref[...] = jnp.zeros_like(sq_ref)

        err = y_ref[...] - t_ref[...]
        sq_ref[...] += jnp.sum(err * err, axis=0, keepdims=True)
        dy = err * (1.0 / D)
        dy_ref[...] = dy
        df_ref[...] = (coef * gate_ref[...] * dy).astype(df_ref.dtype)
        dgate_ref[...] += coef * jnp.sum(dy * f_ref[...].astype(F32), axis=0, keepdims=True)

    row = pl.BlockSpec((tr, D), lambda i: (i, 0))
    vec = pl.BlockSpec((1, D), lambda i: (0, 0))
    vshape = jax.ShapeDtypeStruct((1, D), F32)
    return pl.pallas_call(
        body, name=name, grid=(S // tr,), in_specs=[row, row, row, vec], out_specs=[row, row, vec, vec],
        out_shape=[jax.ShapeDtypeStruct((S, D), F32), jax.ShapeDtypeStruct((S, D), BF16), vshape, vshape],
        compiler_params=_params("arbitrary"),
    )(y, target, f, gate)


def _ffn_in_fwd(h, w4, name):
    S, D = h.shape
    tm = min(ROW_TILE, S)
    n = w4.shape[2]

    def body(h_ref, wg_ref, wu_ref, zg_ref, zu_ref, a_ref):
        hv = h_ref[...]
        zg = _nn(hv, wg_ref[...])
        zu = _nn(hv, wu_ref[...])
        zg_ref[...] = zg.astype(zg_ref.dtype)
        zu_ref[...] = zu.astype(zu_ref.dtype)
        a_ref[...] = (zg * _sigmoid(zg) * zu).astype(a_ref.dtype)

    out = pl.BlockSpec((tm, n), lambda j, m: (m, j))
    oshape = jax.ShapeDtypeStruct((S, 2 * n), BF16)
    return pl.pallas_call(
        body, name=name, grid=(2, S // tm),
        in_specs=[pl.BlockSpec((tm, D), lambda j, m: (m, 0)),
                  pl.BlockSpec((None, D, n), lambda j, m: (j, 0, 0)),
                  pl.BlockSpec((None, D, n), lambda j, m: (j + 2, 0, 0))],
        out_specs=[out, out, out], out_shape=[oshape, oshape, oshape],
        compiler_params=_params("parallel", "parallel"),
    )(h, w4, w4)


def _proj_out_fwd(lhs, w, x, gate, coef, name):
    S, D = x.shape
    tm = min(ROW_TILE, S)
    ks = [a.shape[1] for a in lhs]

    def body(*refs):
        lhs_refs = refs[:len(lhs)]
        w_ref, x_ref, gate_ref, xn_ref, f_ref = refs[len(lhs):]
        acc, off = None, 0
        for a_ref, k in zip(lhs_refs, ks):
            part = _nn(a_ref[...], w_ref[off:off + k, :])
            acc = part if acc is None else acc + part
            off += k
        f_ref[...] = acc.astype(f_ref.dtype)
        xn_ref[...] = x_ref[...] + coef * gate_ref[...] * acc

    row = pl.BlockSpec((tm, D), lambda m: (m, 0))
    return pl.pallas_call(
        body, name=name, grid=(S // tm,),
        in_specs=[pl.BlockSpec((tm, k), lambda m: (m, 0)) for k in ks]
        + [_resident(w.shape, lambda m: (0, 0)), row, pl.BlockSpec((1, D), lambda m: (0, 0))],
        out_specs=[row, row],
        out_shape=[jax.ShapeDtypeStruct((S, D), F32), jax.ShapeDtypeStruct((S, D), BF16)],
        compiler_params=_params("parallel"),
    )(*lhs, w, x, gate)


def _matmul_nn(a, w, out_dtype, tm, name):
    S, K = a.shape
    N = w.shape[1]
    tm = min(tm, S)

    def body(a_ref, w_ref, o_ref):
        o_ref[...] = _nn(a_ref[...], w_ref[...]).astype(o_ref.dtype)

    return pl.pallas_call(
        body, name=name, grid=(S // tm,),
        in_specs=[pl.BlockSpec((tm, K), lambda m: (m, 0)), _resident((K, N), lambda m: (0, 0))],
        out_specs=pl.BlockSpec((tm, N), lambda m: (m, 0)), out_shape=jax.ShapeDtypeStruct((S, N), out_dtype),
        compiler_params=_params("parallel"),
    )(a, w)


def _dact_bwd(df, w_out, zg, zu, name):
    S, D = df.shape
    tm = min(ROW_TILE, S)
    n = w_out.shape[0] // 2

    def body(df_ref, w_ref, zg_ref, zu_ref, dzg_ref, dzu_ref):
        da = _nt(df_ref[...], w_ref[...])
        zg_v, zu_v = zg_ref[...].astype(F32), zu_ref[...].astype(F32)
        s = _sigmoid(zg_v)
        dzu_ref[...] = (da * zg_v * s).astype(dzu_ref.dtype)
        dzg_ref[...] = (da * zu_v * (s * (1.0 + zg_v * (1.0 - s)))).astype(dzg_ref.dtype)

    blk = pl.BlockSpec((tm, n), lambda j, m: (m, j))
    oshape = jax.ShapeDtypeStruct((S, 2 * n), BF16)
    return pl.pallas_call(
        body, name=name, grid=(2, S // tm),
        in_specs=[pl.BlockSpec((tm, D), lambda j, m: (m, 0)), pl.BlockSpec((n, D), lambda j, m: (j, 0)), blk, blk],
        out_specs=[blk, blk], out_shape=[oshape, oshape], compiler_params=_params("parallel", "parallel"),
    )(df, w_out, zg, zu)


def _ffn_in_dgrad(dzg, dzu, w4, name):
    S = dzg.shape[0]
    D, n = w4.shape[1], w4.shape[2]
    tm = min(ROW_TILE, S)

    def body(dzg_ref, dzu_ref, w_ref, dh_ref):
        acc = _nt(dzg_ref[:, 0:n], w_ref[0])
        acc += _nt(dzg_ref[:, n:2 * n], w_ref[1])
        acc += _nt(dzu_ref[:, 0:n], w_ref[2])
        acc += _nt(dzu_ref[:, n:2 * n], w_ref[3])
        dh_ref[...] = acc

    blk = pl.BlockSpec((tm, 2 * n), lambda m: (m, 0))
    return pl.pallas_call(
        body, name=name, grid=(S // tm,),
        in_specs=[blk, blk, _resident(w4.shape, lambda m: (0, 0, 0))],
        out_specs=pl.BlockSpec((tm, D), lambda m: (m, 0)), out_shape=jax.ShapeDtypeStruct((S, D), F32),
        compiler_params=_params("parallel"),
    )(dzg, dzu, w4)


def _matmul_nt(a, w, tm, name):
    S, K = a.shape
    N = w.shape[0]
    tm = min(tm, S)

    def body(a_ref, w_ref, o_ref):
        o_ref[...] = _nt(a_ref[...], w_ref[...])

    return pl.pallas_call(
        body, name=name, grid=(S // tm,),
        in_specs=[pl.BlockSpec((tm, K), lambda m: (m, 0)), _resident((N, K), lambda m: (0, 0))],
        out_specs=pl.BlockSpec((tm, N), lambda m: (m, 0)), out_shape=jax.ShapeDtypeStruct((S, N), F32),
        compiler_params=_params("parallel"),
    )(a, w)


def _wgrad(a, g, tn, name):
    S, Ka = a.shape
    N = g.shape[1]
    ts = min(ROW_TILE, S)

    def body(a_ref, g_ref, o_ref):
        @pl.when(pl.program_id(1) == 0)
        def _():
            o_ref[...] = jnp.zeros_like(o_ref)

        o_ref[...] += _tn(a_ref[...], g_ref[...])

    return pl.pallas_call(
        body, name=name, grid=(N // tn, S // ts),
        in_specs=[pl.BlockSpec((ts, Ka), lambda j, s: (s, 0)), pl.BlockSpec((ts, tn), lambda j, s: (s, j))],
        out_specs=pl.BlockSpec((None, Ka, tn), lambda j, s: (j, 0, 0)),
        out_shape=jax.ShapeDtypeStruct((N // tn, Ka, tn), F32), compiler_params=_params("parallel", "arbitrary"),
    )(a, g)


def _hgrn_chunk_common(qr, fr, oml, tri, last):
    k = oml * _sigmoid(-fr)
    g = jnp.log1p(-k)
    q = qr * _sigmoid(qr)
    G = _nn(tri, g, precision=lax.Precision.HIGHEST)
    Gl = G[last:last + 1]
    return q, k, G, Gl


def _hgrn_consts(reverse):
    C = HG_CHUNK
    r = lax.broadcasted_iota(jnp.int32, (C, C), 0)
    cc = lax.broadcasted_iota(jnp.int32, (C, C), 1)
    tri = ((cc >= r) if reverse else (cc <= r)).astype(F32)
    tri_t = ((cc <= r) if reverse else (cc >= r)).astype(F32)
    rid = lax.broadcasted_iota(jnp.int32, (C, HG_WIDTH), 0)
    return tri, tri_t, rid, (0 if reverse else C - 1)


def _head_slices():
    return [slice(h * HG_DIM, (h + 1) * HG_DIM) for h in range(HG_HEADS)]


def _per_head_lane_sum(x):
    C = x.shape[0]
    return jnp.concatenate(
        [jnp.broadcast_to(jnp.sum(x[:, sl], axis=-1, keepdims=True), (C, HG_DIM)) for sl in _head_slices()], axis=1)


def _hgrn_fwd(z, lb, direction, name):
    S = z.shape[0]
    C, DK, W = HG_CHUNK, HG_DIM, HG_WIDTH
    tb = min(HG_ROWS, S)
    n_t, n_c = S // tb, tb // C
    reverse = direction == 1
    tmap = (lambda i: n_t - 1 - i) if reverse else (lambda i: i)

    def body(q_ref, f_ref, v_ref, lb_ref, o_ref, st_out_ref, st_ref):
        @pl.when(pl.program_id(0) == 0)
        def _():
            st_ref[...] = jnp.zeros_like(st_ref)

        oml = 1.0 - lb_ref[...]
        tri, _, rid, last = _hgrn_consts(reverse)

        def chunk(ci, carry):
            cidx = (n_c - 1 - ci) if reverse else ci
            rows = pl.ds(pl.multiple_of(cidx * C, C), C)
            v = v_ref[rows, :]
            q, k, G, Gl = _hgrn_chunk_common(q_ref[rows, :], f_ref[rows, :], oml, tri, last)
            qd = (q * jnp.exp(G)).astype(BF16)
            kd = (k * jnp.exp(Gl - G)).astype(BF16)
            e_gl = jnp.exp(Gl)
            v_b = v.astype(BF16)
            inter = []
            for h, sl in enumerate(_head_slices()):
                st0 = st_ref[h]
                st_out_ref[h, cidx] = st0
                inter.append(_nt(qd[:, sl], st0.astype(BF16)))
                st_ref[h] = st0 * e_gl[:, sl] + _tn(v_b[:, sl], kd[:, sl])
            o = jnp.concatenate(inter, axis=1)
            for s in range(C):
                valid = (rid <= s) if reverse else (rid >= s)
                e_s = jnp.exp(jnp.where(valid, G - G[s:s + 1], NEG))
                o = o + _per_head_lane_sum(q * k[s:s + 1] * e_s) * v[s:s + 1]
            o_ref[rows, :] = o
            return carry

        lax.fori_loop(0, n_c, chunk, 0)

    def sec(j):
        return pl.BlockSpec((tb, W), lambda i: (tmap(i), j))

    return pl.pallas_call(
        body, name=name, grid=(n_t,),
        in_specs=[sec(0), sec(1 + direction), sec(3), pl.BlockSpec((1, W), lambda i: (0, 0))],
        out_specs=[sec(0), pl.BlockSpec((HG_HEADS, n_c, DK, DK), lambda i: (0, tmap(i), 0, 0))],
        out_shape=[jax.ShapeDtypeStruct((S, W), F32), jax.ShapeDtypeStruct((HG_HEADS, S // C, DK, DK), F32)],
        scratch_shapes=[pltpu.VMEM((HG_HEADS, DK, DK), F32)],
        compiler_params=_params("arbitrary"),
    )(z, z, z, lb)


def _hgrn_bwd(z, lb, do, states, direction, name, acc=None):
    S = z.shape[0]
    C, DK, W = HG_CHUNK, HG_DIM, HG_WIDTH
    tb = min(HG_ROWS, S)
    n_t, n_c = S // tb, tb // C
    reverse = direction == 1
    tmap = (lambda i: i) if reverse else (lambda i: n_t - 1 - i)

    def body(*refs):
        if acc:
            q_ref, f_ref, v_ref, lb_ref, do_ref, st_in_ref, dqa_ref, dva_ref, dq_ref, df_ref, dv_ref, doml_ref, dst_ref = refs
        else:
            q_ref, f_ref, v_ref, lb_ref, do_ref, st_in_ref, dq_ref, df_ref, dv_ref, doml_ref, dst_ref = refs

        @pl.when(pl.program_id(0) == 0)
        def _():
            dst_ref[...] = jnp.zeros_like(dst_ref)
            doml_ref[...] = jnp.zeros_like(doml_ref)

        oml = 1.0 - lb_ref[...]
        tri, tri_t, rid, last = _hgrn_consts(reverse)

        def chunk(ci, carry):
            cidx = ci if reverse else (n_c - 1 - ci)
            rows = pl.ds(pl.multiple_of(cidx * C, C), C)
            qr, fr, v, dov = q_ref[rows, :], f_ref[rows, :], v_ref[rows, :], do_ref[rows, :]
            q, k, G, Gl = _hgrn_chunk_common(qr, fr, oml, tri, last)
            e_g, e_gl, e_kd = jnp.exp(G), jnp.exp(Gl), jnp.exp(Gl - G)
            qd, kd = q * e_g, k * e_kd
            do_b, v_b, qd_b, kd_b = dov.astype(BF16), v.astype(BF16), qd.astype(BF16), kd.astype(BF16)
            dqd, dkd, dv, state_dot = [], [], [], []
            for h, sl in enumerate(_head_slices()):
                st0, dst1 = st_in_ref[h, cidx], dst_ref[h]
                dst1_b = dst1.astype(BF16)
                dqd.append(_nn(do_b[:, sl], st0.astype(BF16)))
                dkd.append(_nn(v_b[:, sl], dst1_b))
                dv.append(_nt(kd_b[:, sl], dst1_b))
                state_dot.append(jnp.sum(st0 * dst1, axis=0, keepdims=True))
                dst_ref[h] = dst1 * e_gl[:, sl] + _tn(do_b[:, sl], qd_b[:, sl])
            dqd, dkd, dv = [jnp.concatenate(t, axis=1) for t in (dqd, dkd, dv)]
            d_gl = e_gl * jnp.concatenate(state_dot, axis=1) + jnp.sum(dkd * kd, axis=0, keepdims=True)
            dq, dk = dqd * e_g, dkd * e_kd
            for s in range(C):
                valid = (rid <= s) if reverse else (rid >= s)
                e_s = jnp.exp(jnp.where(valid, G - G[s:s + 1], NEG))
                col = _per_head_lane_sum(q * k[s:s + 1] * e_s)
                w_s = _per_head_lane_sum(dov * v[s:s + 1]) * e_s
                dq = dq + w_s * k[s:s + 1]
                dv = dv + jnp.where(rid == s, jnp.sum(col * dov, axis=0, keepdims=True), 0.0)
                dk = dk + jnp.where(rid == s, jnp.sum(w_s * q, axis=0, keepdims=True), 0.0)
            d_big_g = dq * q - dk * k + jnp.where(rid == last, d_gl, 0.0)
            dg = _nn(tri_t, d_big_g, precision=lax.Precision.HIGHEST)
            dk_all = dk - dg / (1.0 - k)
            sig_nf = _sigmoid(-fr)
            df_ref[rows, :] = -dk_all * k * (1.0 - sig_nf)
            doml_ref[...] += jnp.sum(dk_all * sig_nf, axis=0, keepdims=True)
            sq = _sigmoid(qr)
            dqr = dq * (sq * (1.0 + qr * (1.0 - sq)))
            if acc:
                dqr = dqr + dqa_ref[rows, :]
                dv = dv + dva_ref[rows, :]
            dq_ref[rows, :] = dqr
            dv_ref[rows, :] = dv
            return carry

        lax.fori_loop(0, n_c, chunk, 0)

    def sec(j):
        return pl.BlockSpec((tb, W), lambda i: (tmap(i), j))

    vec = pl.BlockSpec((1, W), lambda i: (0, 0))
    ins = [z, z, z, lb, do, states]
    in_specs = [sec(0), sec(1 + direction), sec(3), vec, sec(0),
                pl.BlockSpec((HG_HEADS, n_c, DK, DK), lambda i: (0, tmap(i), 0, 0))]
    if acc:
        ins += list(acc)
        in_specs += [sec(0), sec(0)]
    full = jax.ShapeDtypeStruct((S, W), F32)
    return pl.pallas_call(
        body, name=name, grid=(n_t,), in_specs=in_specs,
        out_specs=[sec(0), sec(0), sec(0), vec],
        out_shape=[full, full, full, jax.ShapeDtypeStruct((1, W), F32)],
        scratch_shapes=[pltpu.VMEM((HG_HEADS, DK, DK), F32)],
        compiler_params=_params("arbitrary"),
    )(*ins)


def _hgrn_post_fwd(o_f, o_b, z, norm_g, name):
    S = z.shape[0]
    tr = min(ROW_TILE, S)

    def body(of_ref, ob_ref, gr_ref, ng_ref, y_ref):
        o = of_ref[...] + ob_ref[...]
        gr = gr_ref[...]
        gate = gr * _sigmoid(gr)
        ng = ng_ref[...]
        for h in range(HG_HEADS):
            sl = slice(h * HG_DIM, (h + 1) * HG_DIM)
            oh = o[:, sl]
            rstd = lax.rsqrt(jnp.mean(oh * oh, axis=-1, keepdims=True) + EPS)
            y_ref[:, sl] = (oh * rstd * ng[:, sl] * gate[:, sl]).astype(y_ref.dtype)

    row = pl.BlockSpec((tr, HG_WIDTH), lambda i: (i, 0))
    return pl.pallas_call(
        body, name=name, grid=(S // tr,),
        in_specs=[row, row, pl.BlockSpec((tr, HG_WIDTH), lambda i: (i, 4)), pl.BlockSpec((1, HG_WIDTH), lambda i: (0, 0))],
        out_specs=row, out_shape=jax.ShapeDtypeStruct((S, HG_WIDTH), BF16), compiler_params=_params("parallel"),
    )(o_f, o_b, z, norm_g)


def _hgrn_post_bwd(dy, o_f, o_b, z, norm_g, name):
    S = z.shape[0]
    tr = min(ROW_TILE, S)

    def body(dy_ref, of_ref, ob_ref, gr_ref, ng_ref, do_ref, dgr_ref, dng_ref):
        @pl.when(pl.program_id(0) == 0)
        def _():
            dng_ref[...] = jnp.zeros_like(dng_ref)

        o = of_ref[...] + ob_ref[...]
        gr, ng, dyv = gr_ref[...], ng_ref[...], dy_ref[...]
        sg = _sigmoid(gr)
        for h in range(HG_HEADS):
            sl = slice(h * HG_DIM, (h + 1) * HG_DIM)
            oh, dyh, grh, sgh, ngh = o[:, sl], dyv[:, sl], gr[:, sl], sg[:, sl], ng[:, sl]
            rstd = lax.rsqrt(jnp.mean(oh * oh, axis=-1, keepdims=True) + EPS)
            on = oh * rstd
            du = dyh * (grh * sgh)
            dgr_ref[:, sl] = dyh * (on * ngh) * (sgh * (1.0 + grh * (1.0 - sgh)))
            dng_ref[:, sl] += jnp.sum(du * on, axis=0, keepdims=True)
            don = du * ngh
            do_ref[:, sl] = rstd * (don - on * jnp.mean(don * on, axis=-1, keepdims=True))

    row = pl.BlockSpec((tr, HG_WIDTH), lambda i: (i, 0))
    vec = pl.BlockSpec((1, HG_WIDTH), lambda i: (0, 0))
    full = jax.ShapeDtypeStruct((S, HG_WIDTH), F32)
    return pl.pallas_call(
        body, name=name, grid=(S // tr,),
        in_specs=[row, row, row, pl.BlockSpec((tr, HG_WIDTH), lambda i: (i, 4)), vec],
        out_specs=[row, row, vec], out_shape=[full, full, jax.ShapeDtypeStruct((1, HG_WIDTH), F32)],
        compiler_params=_params("arbitrary"),
    )(dy, o_f, o_b, z, norm_g)


def _t5_bucket_table():
    rel = (np.arange(3 * BLOCK)[None, :] - BLOCK) - np.arange(BLOCK)[:, None]
    nb = NUM_BUCKETS // 2
    max_exact = nb // 2
    ret = (rel > 0).astype(np.int32) * nb
    n = np.abs(rel)
    ratio = np.log(np.maximum(n, 1).astype(np.float32) / np.float32(max_exact)) / np.float32(math.log(MAX_DISTANCE / max_exact))
    large = max_exact + (ratio.astype(np.float32) * np.float32(nb - max_exact)).astype(np.int32)
    large = np.minimum(large, nb - 1)
    bucket = ret + np.where(n < max_exact, n, large)
    return bucket.astype(np.int32), (n <= WINDOW)


def _bias_table(rel_bias, name):
    bucket, in_band = _t5_bucket_table()
    idx = jnp.asarray(np.where(in_band, bucket, -1))

    def body(rb_ref, idx_ref, o_ref):
        h = pl.program_id(0)
        iv = idx_ref[...]
        acc = jnp.where(iv < 0, NEG, 0.0).astype(F32)
        for b in range(NUM_BUCKETS):
            acc = acc + jnp.where(iv == b, rb_ref[b, h], 0.0)
        o_ref[...] = acc

    return pl.pallas_call(
        body, name=name, grid=(ATT_Q_HEADS,),
        in_specs=[pl.BlockSpec(memory_space=pltpu.SMEM), pl.BlockSpec((BLOCK, 3 * BLOCK), lambda h: (0, 0))],
        out_specs=pl.BlockSpec((None, BLOCK, 3 * BLOCK), lambda h: (h, 0, 0)),
        out_shape=jax.ShapeDtypeStruct((ATT_Q_HEADS, BLOCK, 3 * BLOCK), F32), compiler_params=_params("parallel"),
    )(rel_bias, idx)


def _bias_grad(ds_sum, name):
    bucket, in_band = _t5_bucket_table()
    idx = jnp.asarray(np.where(in_band, bucket, -1))

    def body(ds_ref, idx_ref, o_ref):
        iv, ds = idx_ref[...], ds_ref[...]
        for b in range(NUM_BUCKETS):
            part = jnp.sum(jnp.where(iv == b, ds, 0.0), axis=0, keepdims=True)
            o_ref[b:b + 1, :] = part[:, 0:BLOCK] + part[:, BLOCK:2 * BLOCK] + part[:, 2 * BLOCK:3 * BLOCK]

    return pl.pallas_call(
        body, name=name, grid=(ATT_Q_HEADS,),
        in_specs=[pl.BlockSpec((None, BLOCK, 3 * BLOCK), lambda h: (h, 0, 0)), pl.BlockSpec((BLOCK, 3 * BLOCK), lambda h: (0, 0))],
        out_specs=pl.BlockSpec((None, NUM_BUCKETS, BLOCK), lambda h: (h, 0, 0)),
        out_shape=jax.ShapeDtypeStruct((ATT_Q_HEADS, NUM_BUCKETS, BLOCK), F32), compiler_params=_params("parallel"),
    )(ds_sum, idx)


def _attn_specs(nb):
    G, dh = ATT_GROUP, ATT_HEAD_DIM
    qspec = pl.BlockSpec((G, BLOCK, dh), lambda j, n: (j, n, 0))

    def kv(shift):
        return pl.BlockSpec((None, BLOCK, dh), lambda j, n: (j, jnp.clip(n + shift, 0, nb - 1), 0))

    gain = pl.BlockSpec((1, dh), lambda j, n: (0, 0))
    sink = pl.BlockSpec((G, 1, BLOCK), lambda j, n: (j, 0, 0))
    bias = pl.BlockSpec((G, BLOCK, 3 * BLOCK), lambda j, n: (j, 0, 0))
    return qspec, kv, gain, sink, bias


def _attn_probs(qh, kn, bias_h, sink_h, edge_ok):
    s = _nt(qh.astype(BF16), kn.astype(BF16)) * (1.0 / math.sqrt(ATT_HEAD_DIM)) + bias_h
    s = jnp.where(edge_ok, s, NEG)
    m = jnp.maximum(jnp.max(s, axis=-1, keepdims=True), sink_h)
    p = jnp.exp(s - m)
    e_sink = jnp.exp(sink_h - m)
    inv = 1.0 / (jnp.sum(p, axis=-1, keepdims=True) + e_sink)
    return p * inv, e_sink * inv


def _rms_rows(x):
    rstd = lax.rsqrt(jnp.mean(x * x, axis=-1, keepdims=True) + EPS)
    return x * rstd, rstd


def _edge_ok(n, nb):
    colid = lax.broadcasted_iota(jnp.int32, (ATT_GROUP * BLOCK, 3 * BLOCK), 1)
    return jnp.logical_and(jnp.logical_or(colid >= BLOCK, n > 0), jnp.logical_or(colid < 2 * BLOCK, n < nb - 1))


def _sink_column(sink_ref):
    return jnp.concatenate([jnp.broadcast_to(sink_ref[g][:, 0:1], (BLOCK, 1)) for g in range(ATT_GROUP)], axis=0)


def _attn_fwd(q, k, v, q_g, k_g, sink, bias, name):
    S = q.shape[1]
    nb = S // BLOCK
    G, dh = ATT_GROUP, ATT_HEAD_DIM
    qspec, kv, gain, sink_spec, bias_spec = _attn_specs(nb)

    def body(q_ref, k0, k1, k2, v0, v1, v2, qg_ref, kg_ref, sink_ref, bias_ref, o_ref):
        n = pl.program_id(1)
        kcat = jnp.concatenate([k0[...], k1[...], k2[...]], axis=0)
        vcat = jnp.concatenate([v0[...], v1[...], v2[...]], axis=0).astype(BF16)
        kn = _rms_rows(kcat)[0] * kg_ref[...]
        qn = _rms_rows(q_ref[...].reshape(G * BLOCK, dh))[0] * qg_ref[...]
        p, _ = _attn_probs(qn, kn, bias_ref[...].reshape(G * BLOCK, 3 * BLOCK), _sink_column(sink_ref), _edge_ok(n, nb))
        o_ref[...] = _nn(p.astype(BF16), vcat).reshape(G, BLOCK, dh)

    return pl.pallas_call(
        body, name=name, grid=(ATT_KV_HEADS, nb),
        in_specs=[qspec, kv(-1), kv(0), kv(1), kv(-1), kv(0), kv(1), gain, gain, sink_spec, bias_spec],
        out_specs=qspec, out_shape=jax.ShapeDtypeStruct(q.shape, F32), compiler_params=_params("parallel", "parallel"),
    )(q, k, k, k, v, v, v, q_g, k_g, sink, bias)


def _attn_bwd(q, k, v, q_g, k_g, sink, bias, do, name):
    S = q.shape[1]
    nb = S // BLOCK
    G, dh = ATT_GROUP, ATT_HEAD_DIM
    scale = 1.0 / math.sqrt(dh)
    qspec, kv, gain, sink_spec, bias_spec = _attn_specs(nb)

    def body(q_ref, k0, k1, k2, v0, v1, v2, qg_ref, kg_ref, sink_ref, bias_ref, do_ref,
             dq_ref, dkw_ref, dvw_ref, ds_ref, dsink_ref, dqg_ref):
        n = pl.program_id(1)

        @pl.when(n == 0)
        def _():
            ds_ref[...] = jnp.zeros_like(ds_ref)
            dsink_ref[...] = jnp.zeros_like(dsink_ref)
            dqg_ref[...] = jnp.zeros_like(dqg_ref)

        kcat = jnp.concatenate([k0[...], k1[...], k2[...]], axis=0)
        vcat = jnp.concatenate([v0[...], v1[...], v2[...]], axis=0).astype(BF16)
        kn = _rms_rows(kcat)[0] * kg_ref[...]
        qg = qg_ref[...]
        qhat, rstd = _rms_rows(q_ref[...].reshape(G * BLOCK, dh))
        qn = qhat * qg
        p, p_sink = _attn_probs(qn, kn, bias_ref[...].reshape(G * BLOCK, 3 * BLOCK), _sink_column(sink_ref), _edge_ok(n, nb))
        do_b = do_ref[...].reshape(G * BLOCK, dh).astype(BF16)
        dp = _nt(do_b, vcat)
        delta = jnp.sum(p * dp, axis=-1, keepdims=True)
        ds = p * (dp - delta)
        ds_ref[...] += ds.reshape(G, BLOCK, 3 * BLOCK)
        sink_term = p_sink * delta
        for g in range(G):
            dsink_ref[g] += jnp.zeros((1, BLOCK), F32) - jnp.sum(sink_term[g * BLOCK:(g + 1) * BLOCK], axis=0, keepdims=True)
        ds_b = ds.astype(BF16)
        dvw_ref[...] = _tn(p.astype(BF16), do_b)
        dkw_ref[...] = _tn(ds_b, qn.astype(BF16)) * scale
        dqn = _nn(ds_b, kn.astype(BF16)) * scale
        dqg_ref[...] += jnp.sum(dqn * qhat, axis=0, keepdims=True)
        dqh = dqn * qg
        dq_ref[...] = (rstd * (dqh - qhat * jnp.mean(dqh * qhat, axis=-1, keepdims=True))).reshape(G, BLOCK, dh)

    win = pl.BlockSpec((None, None, 3 * BLOCK, dh), lambda j, n: (j, n, 0, 0))
    wshape = jax.ShapeDtypeStruct((ATT_KV_HEADS, nb, 3 * BLOCK, dh), F32)
    return pl.pallas_call(
        body, name=name, grid=(ATT_KV_HEADS, nb),
        in_specs=[qspec, kv(-1), kv(0), kv(1), kv(-1), kv(0), kv(1), gain, gain, sink_spec, bias_spec, qspec],
        out_specs=[qspec, win, win, bias_spec, sink_spec, pl.BlockSpec((None, 1, dh), lambda j, n: (j, 0, 0))],
        out_shape=[jax.ShapeDtypeStruct(q.shape, F32), wshape, wshape,
                   jax.ShapeDtypeStruct((ATT_Q_HEADS, BLOCK, 3 * BLOCK), F32),
                   jax.ShapeDtypeStruct((ATT_Q_HEADS, 1, BLOCK), F32),
                   jax.ShapeDtypeStruct((ATT_KV_HEADS, 1, dh), F32)],
        compiler_params=_params("parallel", "arbitrary"),
    )(q, k, k, k, v, v, v, q_g, k_g, sink, bias, do)


def _attn_kv_reduce(dkw, dvw, k, k_g, name):
    S = k.shape[1]
    nb = S // BLOCK
    dh = ATT_HEAD_DIM
    kb = min(8, nb)
    steps = nb // kb

    def body(a_lo, a, a_hi, b_lo, b, b_hi, k_ref, kg_ref, dk_ref, dv_ref, dkg_ref):
        n = pl.program_id(1)

        @pl.when(n == 0)
        def _():
            dkg_ref[...] = jnp.zeros_like(dkg_ref)

        lo = jnp.where(n > 0, 1.0, 0.0)
        hi = jnp.where(n < steps - 1, 1.0, 0.0)

        def overlap_add(w, w_lo, w_hi, i):
            before = lo * w_lo[...] if i == 0 else w[i - 1, 2 * BLOCK:3 * BLOCK, :]
            after = hi * w_hi[...] if i == kb - 1 else w[i + 1, 0:BLOCK, :]
            return w[i, BLOCK:2 * BLOCK, :] + before + after

        dkg = jnp.zeros((1, dh), F32)
        for i in range(kb):
            rows = slice(i * BLOCK, (i + 1) * BLOCK)
            dkn = overlap_add(a, a_lo, a_hi, i)
            dv_ref[rows, :] = overlap_add(b, b_lo, b_hi, i)
            khat, rstd = _rms_rows(k_ref[rows, :])
            dkg = dkg + jnp.sum(dkn * khat, axis=0, keepdims=True)
            dkh = dkn * kg_ref[...]
            dk_ref[rows, :] = rstd * (dkh - khat * jnp.mean(dkh * khat, axis=-1, keepdims=True))
        dkg_ref[...] += dkg

    main = pl.BlockSpec((None, kb, 3 * BLOCK, dh), lambda j, n: (j, n, 0, 0))
    halo_lo = pl.BlockSpec((None, None, BLOCK, dh), lambda j, n: (j, jnp.maximum(n * kb - 1, 0), 2, 0))
    halo_hi = pl.BlockSpec((None, None, BLOCK, dh), lambda j, n: (j, jnp.minimum(n * kb + kb, nb - 1), 0, 0))
    blk = pl.BlockSpec((None, kb * BLOCK, dh), lambda j, n: (j, n, 0))
    return pl.pallas_call(
        body, name=name, grid=(ATT_KV_HEADS, steps),
        in_specs=[halo_lo, main, halo_hi, halo_lo, main, halo_hi, blk, pl.BlockSpec((1, dh), lambda j, n: (0, 0))],
        out_specs=[blk, blk, pl.BlockSpec((None, 1, dh), lambda j, n: (j, 0, 0))],
        out_shape=[jax.ShapeDtypeStruct(k.shape, F32), jax.ShapeDtypeStruct(k.shape, F32),
                   jax.ShapeDtypeStruct((ATT_KV_HEADS, 1, dh), F32)],
        compiler_params=_params("parallel", "arbitrary"),
    )(dkw, dkw, dkw, dvw, dvw, dvw, k, k_g)


def _ada_fwd(c_act, w, b, name):
    n = w.shape[1]

    def body(c_ref, w_ref, b_ref, o_ref):
        o_ref[...] = _nn(c_ref[...], w_ref[...], precision=lax.Precision.HIGHEST) + b_ref[...]

    tn = n // 3
    return pl.pallas_call(
        body, name=name, grid=(3,),
        in_specs=[pl.BlockSpec(c_act.shape, lambda j: (0, 0)), pl.BlockSpec((w.shape[0], tn), lambda j: (0, j)),
                  pl.BlockSpec((1, tn), lambda j: (0, j))],
        out_specs=pl.BlockSpec((c_act.shape[0], tn), lambda j: (0, j)),
        out_shape=jax.ShapeDtypeStruct((c_act.shape[0], n), F32), compiler_params=_params("parallel"),
    )(c_act, w, b)


def _ada_wgrad(c_act_t, dm, name):
    D, nbatch = c_act_t.shape
    n = dm.shape[1]
    tr = 256

    def body(c_ref, dm_ref, o_ref):
        cv, dv = c_ref[...], dm_ref[...]
        acc = cv[:, 0:1] * dv[0:1, :]
        for b in range(1, nbatch):
            acc = acc + cv[:, b:b + 1] * dv[b:b + 1, :]
        o_ref[...] = acc

    return pl.pallas_call(
        body, name=name, grid=(D // tr,),
        in_specs=[pl.BlockSpec((tr, nbatch), lambda i: (i, 0)), pl.BlockSpec((nbatch, n), lambda i: (0, 0))],
        out_specs=pl.BlockSpec((tr, n), lambda i: (i, 0)), out_shape=jax.ShapeDtypeStruct((D, n), F32),
        compiler_params=_params("parallel"),
    )(c_act_t, dm)


def _adamw(w, g, m, v, name):
    R, Cn = w.shape
    tr = R
    for cand in (256, 128, 64, 32, 16, 8):
        if R % cand == 0:
            tr = cand
            break

    def body(w_ref, g_ref, m_ref, v_ref, d_ref, nm_ref, nv_ref):
        gv = g_ref[...]
        m_new = ADAM_B1 * m_ref[...] + (1.0 - ADAM_B1) * gv
        v_new = ADAM_B2 * v_ref[...] + (1.0 - ADAM_B2) * (gv * gv)
        m_hat = m_new / (1.0 - ADAM_B1 ** ADAM_STEP)
        v_hat = v_new / (1.0 - ADAM_B2 ** ADAM_STEP)
        d_ref[...] = -ADAM_LR * (m_hat / (jnp.sqrt(v_hat) + ADAM_EPS) + ADAM_WD * w_ref[...])
        nm_ref[...] = m_new
        nv_ref[...] = v_new

    blk = pl.BlockSpec((tr, Cn), lambda i: (i, 0))
    shp = jax.ShapeDtypeStruct((R, Cn), F32)
    return pl.pallas_call(
        body, name=name, grid=(R // tr,), in_specs=[blk] * 4, out_specs=[blk] * 3, out_shape=[shp] * 3,
        compiler_params=_params("parallel"),
    )(w, g, m, v)


def _place():
    return lax.axis_index("x"), lax.axis_index("y"), lax.axis_index("c")


def _flip(place, k):
    x, y, c = place
    return (1 - x if k & 4 else x, 1 - y if k & 2 else y, 1 - c if k & 1 else c)


def _dev_index(place):
    x, y, c = place
    return 4 * x + 2 * y + c


def _chip_index(place):
    return 2 * place[0] + place[1]


def _allgather8(x, name, reduce=False):
    R, Cn = x.shape

    def body(x_ref, *rest):
        if reduce:
            out_ref, sum_ref, send_sems, recv_sems, local_sem = rest
        else:
            out_ref, send_sems, recv_sems, local_sem = rest
        me = _place()
        mine = pltpu.make_async_copy(x_ref, out_ref.at[_dev_index(me)], local_sem)
        mine.start()

        def copy(k, origin, to):
            return pltpu.make_async_remote_copy(
                src_ref=x_ref, dst_ref=out_ref.at[_dev_index(origin)], send_sem=send_sems.at[k - 1],
                recv_sem=recv_sems.at[k - 1], device_id=to, device_id_type=MESH)

        sends = [copy(k, me, _flip(me, k)) for k in range(1, 8)]
        for cp in sends:
            cp.start()
        for k in range(1, 8):
            copy(k, _flip(me, k), me).wait_recv()
        for cp in sends:
            cp.wait_send()
        mine.wait()
        if reduce:
            acc = out_ref[0]
            for i in range(1, 8):
                acc = acc + out_ref[i]
            sum_ref[...] = acc

    vm = pl.BlockSpec(memory_space=pltpu.VMEM)
    outs = [jax.ShapeDtypeStruct((8, R, Cn), F32)] + ([jax.ShapeDtypeStruct((R, Cn), F32)] if reduce else [])
    res = pl.pallas_call(
        body, name=name, in_specs=[vm], out_specs=[vm] * len(outs), out_shape=outs,
        scratch_shapes=[pltpu.SemaphoreType.DMA((7,)), pltpu.SemaphoreType.DMA((7,)), pltpu.SemaphoreType.DMA],
    )(x)
    return res if reduce else res[0]


def _weights_allgather(shards, name):
    n = len(shards)
    per = 8

    def body(*refs):
        in_refs, out_refs = refs[:n], refs[n:2 * n]
        send_sems, recv_sems = refs[2 * n:]
        me = _place()
        c = me[2]
        sibling = _flip(me, 1)
        others = [_flip(me, 2 * j) for j in (1, 2, 3)]

        def copy(a, k, src, dst, to):
            return pltpu.make_async_remote_copy(
                src_ref=src, dst_ref=dst, send_sem=send_sems.at[per * a + k], recv_sem=recv_sems.at[per * a + k],
                device_id=to, device_id_type=MESH)

        def block(a, place, half):
            return out_refs[a].at[_chip_index(place), half]

        started = []
        for a in range(n):
            sends = [copy(a, 0, in_refs[a].at[c], block(a, me, c), sibling),
                     copy(a, 7, in_refs[a].at[1 - c], block(a, me, 1 - c), sibling)]
            sends += [copy(a, 1 + j, in_refs[a].at[c], block(a, me, c), to) for j, to in enumerate(others)]
            for cp in sends:
                cp.start()
            started += sends
        for a in range(n):
            for j, other in enumerate(others):
                landed = block(a, other, c)
                copy(a, 1 + j, landed, landed, me).wait_recv()
                fwd = copy(a, 4 + j, landed, landed, sibling)
                fwd.start()
                started.append(fwd)
        for a in range(n):
            copy(a, 0, block(a, me, 1 - c), block(a, me, 1 - c), me).wait_recv()
            copy(a, 7, block(a, me, c), block(a, me, c), me).wait_recv()
            for j, other in enumerate(others):
                got = block(a, other, 1 - c)
                copy(a, 4 + j, got, got, me).wait_recv()
        for cp in started:
            cp.wait_send()

    return pl.pallas_call(
        body, name=name, in_specs=[ANY] * n, out_specs=[ANY] * n,
        out_shape=[jax.ShapeDtypeStruct((N_CHIPS,) + s.shape, s.dtype) for s in shards],
        scratch_shapes=[pltpu.SemaphoreType.DMA((per * n,)), pltpu.SemaphoreType.DMA((per * n,))],
    )(*shards)


def _halves_exchange(grads, name):
    n = len(grads)

    def body(*refs):
        in_refs, got_refs = refs[:n], refs[n:2 * n]
        send_sems, recv_sems = refs[2 * n:]
        me = _place()
        c = me[2]
        sibling = _flip(me, 1)
        started = []
        for a in range(n):
            for kk in range(N_CHIPS):
                i = N_CHIPS * a + kk
                send = pltpu.make_async_remote_copy(
                    src_ref=in_refs[a].at[kk, 1 - c], dst_ref=got_refs[a].at[kk], send_sem=send_sems.at[i],
                    recv_sem=recv_sems.at[i], device_id=sibling, device_id_type=MESH)
                send.start()
                started.append(send)
        for send in started:
            send.wait_recv()
        for send in started:
            send.wait_send()

    return pl.pallas_call(
        body, name=name, in_specs=[ANY] * n, out_specs=[ANY] * n,
        out_shape=[jax.ShapeDtypeStruct((N_CHIPS,) + g.shape[2:], g.dtype) for g in grads],
        scratch_shapes=[pltpu.SemaphoreType.DMA((N_CHIPS * n,)), pltpu.SemaphoreType.DMA((N_CHIPS * n,))],
    )(*grads)


def _chips_exchange(parts, name):
    n = len(parts)

    def body(*refs):
        in_refs, out_refs = refs[:n], refs[n:2 * n]
        send_sems, recv_sems = refs[2 * n:]
        me = _place()
        started = []
        for a in range(n):
            for j in (1, 2, 3):
                peer = _flip(me, 2 * j)
                send = pltpu.make_async_remote_copy(
                    src_ref=in_refs[a].at[_chip_index(peer)], dst_ref=out_refs[a].at[j - 1],
                    send_sem=send_sems.at[3 * a + j - 1], recv_sem=recv_sems.at[3 * a + j - 1],
                    device_id=peer, device_id_type=MESH)
                send.start()
                started.append(send)
        for send in started:
            send.wait_recv()
        for send in started:
            send.wait_send()

    return pl.pallas_call(
        body, name=name, in_specs=[ANY] * n, out_specs=[ANY] * n,
        out_shape=[jax.ShapeDtypeStruct((3,) + p.shape[1:], p.dtype) for p in parts],
        scratch_shapes=[pltpu.SemaphoreType.DMA((3 * n,)), pltpu.SemaphoreType.DMA((3 * n,))],
    )(*parts)


def _siblings_exchange(halves, name):
    n = len(halves)

    def body(*refs):
        in_refs, out_refs = refs[:n], refs[n:2 * n]
        send_sems, recv_sems = refs[2 * n:]
        sibling = _flip(_place(), 1)
        started = []
        for a in range(n):
            send = pltpu.make_async_remote_copy(
                src_ref=in_refs[a], dst_ref=out_refs[a], send_sem=send_sems.at[a], recv_sem=recv_sems.at[a],
                device_id=sibling, device_id_type=MESH)
            send.start()
            started.append(send)
        for send in started:
            send.wait_recv()
        for send in started:
            send.wait_send()

    return pl.pallas_call(
        body, name=name, in_specs=[ANY] * n, out_specs=[ANY] * n,
        out_shape=[jax.ShapeDtypeStruct(h.shape, h.dtype) for h in halves],
        scratch_shapes=[pltpu.SemaphoreType.DMA((n,)), pltpu.SemaphoreType.DMA((n,))],
    )(*halves)


def _row_tile(rows):
    for cand in (256, 176, 128, 64, 32, 16, 8):
        if rows % cand == 0:
            return cand
    return rows


def _pair_sum(core, grad, theirs, name):
    N, _, R, Cn = grad.shape
    tr = _row_tile(R)

    def body(core_ref, g_ref, t_ref, o_ref, ob_ref):
        s = g_ref[...] + t_ref[...]
        o_ref[...] = s
        ob_ref[...] = s.astype(BF16)

    out = pl.BlockSpec((None, tr, Cn), lambda k, i, core_ref: (k, i, 0))
    return pl.pallas_call(
        body, name=name,
        grid_spec=pltpu.PrefetchScalarGridSpec(
            num_scalar_prefetch=1, grid=(N, R // tr),
            in_specs=[pl.BlockSpec((None, None, tr, Cn), lambda k, i, core_ref: (k, core_ref[0], i, 0)),
                      pl.BlockSpec((None, tr, Cn), lambda k, i, core_ref: (k, i, 0))],
            out_specs=[out, out]),
        out_shape=[jax.ShapeDtypeStruct((N, R, Cn), F32), jax.ShapeDtypeStruct((N, R, Cn), BF16)],
        compiler_params=_params("parallel", "parallel"),
    )(core, grad, theirs)


def _chip_sum(chip, parts, landed, name):
    _, R, Cn = parts.shape
    tr = _row_tile(R)

    def body(chip_ref, p_ref, l_ref, o_ref):
        o_ref[...] = ((p_ref[...] + l_ref[0].astype(F32)) + l_ref[1].astype(F32)) + l_ref[2].astype(F32)

    return pl.pallas_call(
        body, name=name,
        grid_spec=pltpu.PrefetchScalarGridSpec(
            num_scalar_prefetch=1, grid=(R // tr,),
            in_specs=[pl.BlockSpec((None, tr, Cn), lambda i, chip_ref: (chip_ref[0], i, 0)),
                      pl.BlockSpec((3, tr, Cn), lambda i, chip_ref: (0, i, 0))],
            out_specs=pl.BlockSpec((tr, Cn), lambda i, chip_ref: (i, 0))),
        out_shape=jax.ShapeDtypeStruct((R, Cn), F32), compiler_params=_params("parallel"),
    )(chip, parts, landed)


def _reduce_scatter(grads, core, chip, tag):
    theirs = _halves_exchange(grads, f"{tag}_halves_exchange")
    parts = [_pair_sum(core, g, t, f"{tag}_pair_sum_{i}") for i, (g, t) in enumerate(zip(grads, theirs))]
    landed = _chips_exchange([p[1] for p in parts], f"{tag}_chips_exchange")
    halves = [_chip_sum(chip, p[0], l, f"{tag}_chip_sum_{i}") for i, (p, l) in enumerate(zip(parts, landed))]
    return list(zip(halves, _siblings_exchange(halves, f"{tag}_siblings_exchange")))


def _adamw_halves(core, w, g_mine, g_theirs, m, v, name):
    R2, Cn = w.shape
    r = R2 // 2
    tr = _row_tile(r)
    nt = r // tr

    def body(core_ref, w_ref, gm_ref, gt_ref, m_ref, v_ref, g_ref, d_ref, nm_ref, nv_ref):
        gv = jnp.where(pl.program_id(0) == core_ref[0], gm_ref[...], gt_ref[...])
        g_ref[...] = gv
        m_new = ADAM_B1 * m_ref[...] + (1.0 - ADAM_B1) * gv
        v_new = ADAM_B2 * v_ref[...] + (1.0 - ADAM_B2) * (gv * gv)
        m_hat = m_new / (1.0 - ADAM_B1 ** ADAM_STEP)
        v_hat = v_new / (1.0 - ADAM_B2 ** ADAM_STEP)
        d_ref[...] = -ADAM_LR * (m_hat / (jnp.sqrt(v_hat) + ADAM_EPS) + ADAM_WD * w_ref[...])
        nm_ref[...] = m_new
        nv_ref[...] = v_new

    full = pl.BlockSpec((tr, Cn), lambda hf, i, core_ref: (hf * nt + i, 0))
    half = pl.BlockSpec((tr, Cn), lambda hf, i, core_ref: (i, 0))
    shp = jax.ShapeDtypeStruct((R2, Cn), F32)
    return pl.pallas_call(
        body, name=name,
        grid_spec=pltpu.PrefetchScalarGridSpec(
            num_scalar_prefetch=1, grid=(2, nt), in_specs=[full, half, half, full, full], out_specs=[full] * 4),
        out_shape=[shp] * 4, compiler_params=_params("parallel", "parallel"),
    )(core, w, g_mine, g_theirs, m, v)


def _pad_row(v, width):
    v = v.reshape(1, -1)
    return jnp.pad(v, ((0, 0), (0, width - v.shape[1])))


def _ffn_forward(x, ng, shift, scale, gate, w_in4, w_out, tag):
    h = _rmsmod_fwd(x, ng, shift, scale, f"{tag}_norm")
    zg, zu, a = _ffn_in_fwd(h, w_in4, f"{tag}_in")
    x_new, f = _proj_out_fwd([a], w_out, x, gate, 0.5, f"{tag}_out")
    return x_new, (h, zg, zu, a, f)


def _ffn_backward(df, saved, w_in4, w_out, tag):
    h, zg, zu, a, _ = saved
    dzg, dzu = _dact_bwd(df, w_out, zg, zu, f"{tag}_dact")
    dw_out = _wgrad(a, df, 512, f"{tag}_dw_out")
    dw_out = jnp.concatenate([dw_out[0], dw_out[1]], axis=1)
    dh = _ffn_in_dgrad(dzg, dzu, w_in4, f"{tag}_dh")
    dw_in = jnp.concatenate([_wgrad(h, dzg, FF_SHARD, f"{tag}_dw_gate"), _wgrad(h, dzu, FF_SHARD, f"{tag}_dw_up")], axis=0)
    return dh, dw_in, dw_out


def kernel(x, c, w_ada, b_ada, norm_g, w_ffn1_in, w_ffn1_out, w_ffn2_in, w_ffn2_out, w_mix_in, w_mix_out, hgrn_lb, hgrn_norm_g, qk_norm_g, attn_sink, rel_bias, loss_target, m_w_ada, m_b_ada, m_norm_g, m_w_ffn1_in, m_w_ffn1_out, m_w_ffn2_in, m_w_ffn2_out, m_w_mix_in, m_w_mix_out, m_hgrn_lb, m_hgrn_norm_g, m_qk_norm_g, m_attn_sink, m_rel_bias, v_w_ada, v_b_ada, v_norm_g, v_w_ffn1_in, v_w_ffn1_out, v_w_ffn2_in, v_w_ffn2_out, v_w_mix_in, v_w_mix_out, v_hgrn_lb, v_hgrn_norm_g, v_qk_norm_g, v_attn_sink, v_rel_bias):
    D = D_MODEL
    S = x.shape[1]
    place = (lax.axis_index("x"), lax.axis_index("y"), lax.axis_index("c"))
    me, my_chip = _dev_index(place), _chip_index(place)
    x0 = x[0]
    target = loss_target[0]

    def halves(w):
        return w.astype(BF16).reshape(2, w.shape[0] // 2, w.shape[1])

    gathered = _weights_allgather(
        [halves(w_ffn1_in[0]), halves(w_ffn1_out[0]), halves(w_ffn2_in[0]), halves(w_ffn2_out[0]), halves(w_mix_in[0]),
         halves(w_mix_out[0])], "weights_allgather")
    w1_in = gathered[0].reshape(N_CHIPS, D, FF_SHARD)
    w1_out = gathered[1].reshape(D_FF, D)
    w2_in = gathered[2].reshape(N_CHIPS, D, FF_SHARD)
    w2_out = gathered[3].reshape(D_FF, D)
    wm_in = gathered[4].reshape(N_CHIPS, D, D_IN // N_CHIPS).transpose(1, 0, 2).reshape(D, D_IN)
    wm_out = gathered[5].reshape(D, D)

    small = jnp.concatenate([_pad_row(c, D), _pad_row(norm_g, D), _pad_row(hgrn_lb, D), jnp.zeros((5, D), F32)], axis=0)
    small_all = _allgather8(small, "small_allgather")
    c_all = small_all[:, 0, :]
    by_chip = small_all[0::2]
    norm_g_full = by_chip[:, 1, :3 * 256].reshape(N_CHIPS, 3, 256).transpose(1, 0, 2).reshape(3, D)
    lb_raw = by_chip[:, 2, :2 * 2 * 128].reshape(N_CHIPS, 2, 2, 128).transpose(1, 2, 0, 3).reshape(2, 2, HG_WIDTH)
    lb = jax.nn.sigmoid(lb_raw[:, 0, :] - lb_raw[:, 1, :])
    lb_f, lb_b = lb[0:1], lb[1:2]

    c_act_all = c_all * jax.nn.sigmoid(c_all)
    n_ada = w_ada.shape[2]
    b_mine = lax.dynamic_slice_in_dim(b_ada, my_chip * n_ada, n_ada, axis=1)
    mods_part = _ada_fwd(c_act_all, w_ada[0], b_mine, "ada_fwd")
    mods_all = _allgather8(mods_part, "mods_allgather")[0::2].transpose(1, 0, 2).reshape(8, N_MOD * D)
    mods = lax.dynamic_slice_in_dim(mods_all, me, 1, axis=0)
    sh1, sc1, g1, sh2, sc2, g2, sh3, sc3, g3 = [mods[:, i * D:(i + 1) * D] for i in range(N_MOD)]

    x1, saved1 = _ffn_forward(x0, norm_g_full[0:1], sh1, sc1, g1, w1_in, w1_out, "ffn1")

    h2 = _rmsmod_fwd(x1, norm_g_full[1:2], sh2, sc2, "mix_norm")
    z = _matmul_nn(h2, wm_in, F32, 256, "mix_in")
    of, st_f = _hgrn_fwd(z, lb_f, 0, "hgrn_fwd_f")
    ob, st_b = _hgrn_fwd(z, lb_b, 1, "hgrn_fwd_b")
    o_h = _hgrn_post_fwd(of, ob, z, hgrn_norm_g, "hgrn_post")

    def to_heads(t, nh):
        return t.reshape(S, nh, ATT_HEAD_DIM).transpose(1, 0, 2)

    aq = to_heads(z[:, 5 * HG_WIDTH:5 * HG_WIDTH + ATT_WIDTH], ATT_Q_HEADS)
    ak = to_heads(z[:, 5 * HG_WIDTH + ATT_WIDTH:5 * HG_WIDTH + ATT_WIDTH + KV_WIDTH], ATT_KV_HEADS)
    av = to_heads(z[:, 5 * HG_WIDTH + ATT_WIDTH + KV_WIDTH:], ATT_KV_HEADS)
    q_g, k_g = qk_norm_g[0, 0:1], qk_norm_g[0, 1:2]
    sink_b = jnp.broadcast_to(attn_sink.reshape(ATT_Q_HEADS, 1, 1), (ATT_Q_HEADS, 1, BLOCK))
    bias = _bias_table(rel_bias, "bias_table")
    o_attn = _attn_fwd(aq, ak, av, q_g, k_g, sink_b, bias, "attn_fwd")
    o_a = o_attn.transpose(1, 0, 2).reshape(S, ATT_WIDTH).astype(BF16)
    x2, mixed = _proj_out_fwd([o_h, o_a], wm_out, x1, g2, 1.0, "mix_out")

    x3, saved3 = _ffn_forward(x2, norm_g_full[2:3], sh3, sc3, g3, w2_in, w2_out, "ffn2")

    dx3, df3, dg3, sq_cols = _loss_bwd(x3, target, saved3[4], g3, 0.5, "loss")
    loss_mine = 0.5 * jnp.sum(sq_cols) / D

    dh3, dw2_in, dw2_out = _ffn_backward(df3, saved3, w2_in, w2_out, "ffn2")
    dx2, dsh3, dsc3, dng3, dmixed, dg2 = _rmsmod_bwd(dh3, x2, norm_g_full[2:3], sc3, dx3, "ffn2_norm_bwd", below=(mixed, g2, 1.0))

    do_cat = _matmul_nt(dmixed, wm_out, ROW_TILE, "mix_out_dgrad")
    dwm_out = jnp.concatenate([_wgrad(o_h, dmixed, 512, "mix_out_dw_h"), _wgrad(o_a, dmixed, 512, "mix_out_dw_a")], axis=1)
    dwm_out = jnp.concatenate([dwm_out[0], dwm_out[1]], axis=1)

    do_sum, dgr, d_hnorm = _hgrn_post_bwd(do_cat, of, ob, z, hgrn_norm_g, "hgrn_post_bwd")
    dq_f, dff, dv_f, doml_f = _hgrn_bwd(z, lb_f, do_sum, st_f, 0, "hgrn_bwd_f")
    dhq, dfb, dhi, doml_b = _hgrn_bwd(z, lb_b, do_sum, st_b, 1, "hgrn_bwd_b", acc=(dq_f, dv_f))

    do_a = to_heads(do_cat[:, HG_WIDTH:], ATT_Q_HEADS)
    daq, dkw, dvw, ds_sum, dsink, dqg = _attn_bwd(aq, ak, av, q_g, k_g, sink_b, bias, do_a, "attn_bwd")
    dak, dav, dkg = _attn_kv_reduce(dkw, dvw, ak, k_g, "attn_kv_reduce")
    d_rel_bias = jnp.sum(_bias_grad(ds_sum, "bias_grad"), axis=-1).T

    def from_heads(t):
        return t.transpose(1, 0, 2).reshape(S, -1)

    dz = jnp.concatenate([dhq, dff, dfb, dhi, dgr, from_heads(daq), from_heads(dak), from_heads(dav)], axis=1).astype(BF16)
    dh2 = _matmul_nt(dz, wm_in, 256, "mix_in_dgrad")
    dwm_in = _wgrad(h2, dz, D_IN // 2, "mix_in_dw")
    dwm_in = jnp.concatenate([dwm_in[0], dwm_in[1]], axis=1)
    dx1, dsh2, dsc2, dng2, df1, dg1 = _rmsmod_bwd(dh2, x1, norm_g_full[1:2], sc2, dx2, "mix_norm_bwd", below=(saved1[4], g1, 0.5))

    dh1, dw1_in, dw1_out = _ffn_backward(df1, saved1, w1_in, w1_out, "ffn1")
    dx0, dsh1, dsc1, dng1 = _rmsmod_bwd(dh1, x0, norm_g_full[0:1], sc1, dx1, "ffn1_norm_bwd")

    dlb = -jnp.concatenate([doml_f, doml_b], axis=0)
    dlb_raw = dlb * lb * (1.0 - lb)
    d_hgrn_lb = jnp.stack([dlb_raw, -dlb_raw], axis=1)
    d_qk = jnp.concatenate([jnp.sum(dqg, axis=0), jnp.sum(dkg, axis=0)], axis=0)
    dmods = jnp.concatenate([dsh1, dsc1, dg1, dsh2, dsc2, dg2, dsh3, dsc3, dg3], axis=0)
    packed = jnp.concatenate(
        [dmods, dng1, dng2, dng3, d_hgrn_lb.reshape(2, D), _pad_row(d_hnorm, D), _pad_row(d_qk, D),
         _pad_row(dsink[:, 0, 0], D), _pad_row(d_rel_bias, D), _pad_row(loss_mine, D)], axis=0)
    packed = jnp.pad(packed, ((0, 24 - packed.shape[0]), (0, 0)))
    packed_all, packed_sum = _allgather8(packed, "small_grads_allgather", reduce=True)
    dmods_all = packed_all[:, 0:N_MOD, :].reshape(8, N_MOD * D)
    g_b_ada = packed_sum[0:N_MOD].reshape(1, N_MOD * D)
    g_norm_full = packed_sum[9:12]
    g_norm_g = lax.dynamic_slice_in_dim(g_norm_full, my_chip * 256, 256, axis=1).reshape(1, 3, 256)
    g_hgrn_lb = lax.dynamic_slice_in_dim(packed_sum[12:14].reshape(2, 2, HG_WIDTH), my_chip * 128, 128, axis=2)
    g_hgrn_norm_g = packed_sum[14:15, :HG_WIDTH]
    g_qk_norm_g = packed_sum[15, :2 * ATT_HEAD_DIM].reshape(1, 2, ATT_HEAD_DIM)
    g_attn_sink = packed_sum[16:17, :ATT_Q_HEADS]
    g_rel_bias = packed_sum[17, :NUM_BUCKETS * ATT_Q_HEADS].reshape(NUM_BUCKETS, ATT_Q_HEADS)
    loss = packed_sum[18, 0]

    dm_mine = lax.dynamic_slice_in_dim(dmods_all, my_chip * n_ada, n_ada, axis=1)
    g_w_ada = _ada_wgrad(c_act_all.T, dm_mine, "ada_wgrad")[None]

    def by_chip_rows(g):
        return g.reshape(N_CHIPS, 2, g.shape[0] // (2 * N_CHIPS), g.shape[1])

    def by_chip_cols(g):
        return g.reshape(N_CHIPS, 2, g.shape[1] // 2, g.shape[2])

    wide = D_IN // N_CHIPS
    core_arr = jnp.reshape(place[2], (1,)).astype(jnp.int32)
    chip_arr = jnp.reshape(my_chip, (1,)).astype(jnp.int32)
    reduced = _reduce_scatter(
        [by_chip_cols(dw1_in), by_chip_rows(dw1_out), by_chip_cols(dw2_in), by_chip_rows(dw2_out),
         by_chip_cols(dwm_in.reshape(D, N_CHIPS, wide).transpose(1, 0, 2)), by_chip_rows(dwm_out)], core_arr, chip_arr, "grads")

    def big(w, g, m, v, name):
        d, nm, nv = _adamw(w[0], g[0], m[0], v[0], name)
        return d[None], nm[None], nv[None]

    def big_halves(w, g_pair, m, v, name):
        g, d, nm, nv = _adamw_halves(core_arr, w[0], g_pair[0], g_pair[1], m[0], v[0], name)
        return g[None], (d[None], nm[None], nv[None])

    g_w1_in, u_w1_in = big_halves(w_ffn1_in, reduced[0], m_w_ffn1_in, v_w_ffn1_in, "adamw_w_ffn1_in")
    g_w1_out, u_w1_out = big_halves(w_ffn1_out, reduced[1], m_w_ffn1_out, v_w_ffn1_out, "adamw_w_ffn1_out")
    g_w2_in, u_w2_in = big_halves(w_ffn2_in, reduced[2], m_w_ffn2_in, v_w_ffn2_in, "adamw_w_ffn2_in")
    g_w2_out, u_w2_out = big_halves(w_ffn2_out, reduced[3], m_w_ffn2_out, v_w_ffn2_out, "adamw_w_ffn2_out")
    g_wm_in, u_wm_in = big_halves(w_mix_in, reduced[4], m_w_mix_in, v_w_mix_in, "adamw_w_mix_in")
    g_wm_out, u_wm_out = big_halves(w_mix_out, reduced[5], m_w_mix_out, v_w_mix_out, "adamw_w_mix_out")

    smalls = [(b_ada, g_b_ada, m_b_ada, v_b_ada), (norm_g, g_norm_g, m_norm_g, v_norm_g), (hgrn_lb, g_hgrn_lb, m_hgrn_lb, v_hgrn_lb),
              (hgrn_norm_g, g_hgrn_norm_g, m_hgrn_norm_g, v_hgrn_norm_g), (qk_norm_g, g_qk_norm_g, m_qk_norm_g, v_qk_norm_g),
              (attn_sink, g_attn_sink, m_attn_sink, v_attn_sink), (rel_bias, g_rel_bias, m_rel_bias, v_rel_bias)]
    sizes = [t[0].size for t in smalls]
    total = sum(sizes)
    rows = -(-total // 128)
    rows = -(-rows // 8) * 8

    def pack(i):
        flat = jnp.concatenate([t[i].reshape(-1) for t in smalls])
        fill = 1.0 if i == 3 else 0.0
        return jnp.pad(flat, (0, rows * 128 - total), constant_values=fill).reshape(rows, 128)

    packed_out = _adamw(pack(0), pack(1), pack(2), pack(3), "adamw_small")

    def unpack(flat2d):
        flat = flat2d.reshape(-1)
        outs, off = [], 0
        for t, n in zip(smalls, sizes):
            outs.append(flat[off:off + n].reshape(t[0].shape))
            off += n
        return outs

    d_small, m_small, v_small = [unpack(t) for t in packed_out]

    upd = {
        "w_ada": big(w_ada, g_w_ada, m_w_ada, v_w_ada, "adamw_w_ada"),
        "w_ffn1_in": u_w1_in, "w_ffn1_out": u_w1_out, "w_ffn2_in": u_w2_in, "w_ffn2_out": u_w2_out,
        "w_mix_in": u_wm_in, "w_mix_out": u_wm_out,
    }
    small_names = ["b_ada", "norm_g", "hgrn_lb", "hgrn_norm_g", "qk_norm_g", "attn_sink", "rel_bias"]
    for i, nme in enumerate(small_names):
        upd[nme] = (d_small[i], m_small[i], v_small[i])
    grads = {
        "w_ada": g_w_ada, "b_ada": g_b_ada, "norm_g": g_norm_g, "w_ffn1_in": g_w1_in, "w_ffn1_out": g_w1_out,
        "w_ffn2_in": g_w2_in, "w_ffn2_out": g_w2_out, "w_mix_in": g_wm_in, "w_mix_out": g_wm_out, "hgrn_lb": g_hgrn_lb,
        "hgrn_norm_g": g_hgrn_norm_g, "qk_norm_g": g_qk_norm_g, "attn_sink": g_attn_sink, "rel_bias": g_rel_bias,
    }
    order = ["w_ada", "b_ada", "norm_g", "w_ffn1_in", "w_ffn1_out", "w_ffn2_in", "w_ffn2_out", "w_mix_in", "w_mix_out",
             "hgrn_lb", "hgrn_norm_g", "qk_norm_g", "attn_sink", "rel_bias"]
    return (loss, dx0[None], *[grads[k] for k in order], *[upd[k][0] for k in order], *[upd[k][1] for k in order],
            *[upd[k][2] for k in order])
```

```python
import functools
import math

import numpy as np
import jax
import jax.numpy as jnp
from jax import lax
from jax.experimental import pallas as pl
from jax.experimental.pallas import tpu as pltpu

F32, BF16 = jnp.float32, jnp.bfloat16

D_MODEL = 1024
D_FF = 2816
HG_HEADS, HG_DIM = 4, 128
HG_WIDTH = HG_HEADS * HG_DIM
ATT_Q_HEADS, ATT_KV_HEADS, ATT_HEAD_DIM = 8, 2, 64
ATT_GROUP = ATT_Q_HEADS // ATT_KV_HEADS
ATT_WIDTH = ATT_Q_HEADS * ATT_HEAD_DIM
KV_WIDTH = ATT_KV_HEADS * ATT_HEAD_DIM
WINDOW, BLOCK = 128, 128
NUM_BUCKETS, MAX_DISTANCE = 32, 128
N_MOD = 9
EPS = 1e-6
D_IN = 5 * HG_WIDTH + ATT_WIDTH + 2 * KV_WIDTH
ADAM_LR, ADAM_B1, ADAM_B2, ADAM_EPS, ADAM_WD, ADAM_STEP = 0.001, 0.9, 0.999, 1e-08, 0.01, 10

N_CHIPS = 4
FF_SHARD = 2 * D_FF // N_CHIPS
NEG = -1e30

VMEM_LIMIT_BYTES = 56 << 20
ROW_TILE = 512
HG_CHUNK = 16
HG_ROWS = 256

MESH = pl.DeviceIdType.MESH
ANY = pl.BlockSpec(memory_space=pl.ANY)


def _params(*sem):
    return pltpu.CompilerParams(dimension_semantics=sem, vmem_limit_bytes=VMEM_LIMIT_BYTES)


def _resident(shape, index_map):
    return pl.BlockSpec(shape, index_map, pipeline_mode=pl.Buffered(1))


def _dot(a, b, dims, precision=None):
    return lax.dot_general(a, b, (dims, ((), ())), precision=precision, preferred_element_type=F32)


def _nn(a, b, precision=None):
    return _dot(a, b, ((1,), (0,)), precision)


def _nt(a, b):
    return _dot(a, b, ((1,), (1,)))


def _tn(a, b):
    return _dot(a, b, ((0,), (0,)))


def _sigmoid(x):
    return jax.nn.sigmoid(x)


def _rmsmod_fwd(x, g, shift, scale, name):
    S, D = x.shape
    tr = min(ROW_TILE, S)

    def body(x_ref, g_ref, sh_ref, sc_ref, h_ref):
        xv = x_ref[...]
        rstd = lax.rsqrt(jnp.mean(xv * xv, axis=-1, keepdims=True) + EPS)
        y = xv * rstd * g_ref[...]
        h_ref[...] = (y * (1.0 + sc_ref[...]) + sh_ref[...]).astype(h_ref.dtype)

    row = pl.BlockSpec((tr, D), lambda i: (i, 0))
    vec = pl.BlockSpec((1, D), lambda i: (0, 0))
    return pl.pallas_call(
        body, name=name, grid=(S // tr,), in_specs=[row, vec, vec, vec], out_specs=row,
        out_shape=jax.ShapeDtypeStruct((S, D), BF16), compiler_params=_params("parallel"),
    )(x, g, shift, scale)


def _rmsmod_bwd(dh, x, g, scale, dx_res, name, below=None):
    S, D = x.shape
    tr = min(ROW_TILE, S)
    coef = below[2] if below else None

    def body(*refs):
        if below:
            dh_ref, x_ref, g_ref, sc_ref, dxr_ref, f_ref, gate_ref, dx_ref, dsh_ref, dsc_ref, dg_ref, df_ref, dgate_ref = refs
        else:
            dh_ref, x_ref, g_ref, sc_ref, dxr_ref, dx_ref, dsh_ref, dsc_ref, dg_ref = refs

        @pl.when(pl.program_id(0) == 0)
        def _():
            dsh_ref[...] = jnp.zeros_like(dsh_ref)
            dsc_ref[...] = jnp.zeros_like(dsc_ref)
            dg_ref[...] = jnp.zeros_like(dg_ref)
            if below:
                dgate_ref[...] = jnp.zeros_like(dgate_ref)

        dhv, xv, gv = dh_ref[...], x_ref[...], g_ref[...]
        one_sc = 1.0 + sc_ref[...]
        rstd = lax.rsqrt(jnp.mean(xv * xv, axis=-1, keepdims=True) + EPS)
        n = xv * rstd
        dsh_ref[...] += jnp.sum(dhv, axis=0, keepdims=True)
        dsc_ref[...] += jnp.sum(dhv * n, axis=0, keepdims=True) * gv
        dg_ref[...] += jnp.sum(dhv * n, axis=0, keepdims=True) * one_sc
        dn = dhv * (gv * one_sc)
        dx = dxr_ref[...] + rstd * (dn - n * jnp.mean(dn * n, axis=-1, keepdims=True))
        dx_ref[...] = dx
        if below:
            df_ref[...] = (coef * gate_ref[...] * dx).astype(df_ref.dtype)
            dgate_ref[...] += coef * jnp.sum(dx * f_ref[...].astype(F32), axis=0, keepdims=True)

    row = pl.BlockSpec((tr, D), lambda i: (i, 0))
    vec = pl.BlockSpec((1, D), lambda i: (0, 0))
    vshape = jax.ShapeDtypeStruct((1, D), F32)
    ins, in_specs = [dh, x, g, scale, dx_res], [row, row, vec, vec, row]
    outs, out_specs = [jax.ShapeDtypeStruct((S, D), F32), vshape, vshape, vshape], [row, vec, vec, vec]
    if below:
        ins += [below[0], below[1]]
        in_specs += [row, vec]
        outs += [jax.ShapeDtypeStruct((S, D), BF16), vshape]
        out_specs += [row, vec]
    return pl.pallas_call(
        body, name=name, grid=(S // tr,), in_specs=in_specs, out_specs=out_specs, out_shape=outs,
        compiler_params=_params("arbitrary"),
    )(*ins)


def _loss_bwd(y, target, f, gate, coef, name):
    S, D = y.shape
    tr = min(ROW_TILE, S)

    def body(y_ref, t_ref, f_ref, gate_ref, dy_ref, df_ref, dgate_ref, sq_ref):
        @pl.when(pl.program_id(0) == 0)
        def _():
            dgate_ref[...] = jnp.zeros_like(dgate_ref)
            sq_ref[...] = jnp.zeros_like(sq_ref)

        err = y_ref[...] - t_ref[...]
        sq_ref[...] += jnp.sum(err * err, axis=0, keepdims=True)
        dy = err * (1.0 / D)
        dy_ref[...] = dy
        df_ref[...] = (coef * gate_ref[...] * dy).astype(df_ref.dtype)
        dgate_ref[...] += coef * jnp.sum(dy * f_ref[...].astype(F32), axis=0, keepdims=True)

    row = pl.BlockSpec((tr, D), lambda i: (i, 0))
    vec = pl.BlockSpec((1, D), lambda i: (0, 0))
    vshape = jax.ShapeDtypeStruct((1, D), F32)
    return pl.pallas_call(
        body, name=name, grid=(S // tr,), in_specs=[row, row, row, vec], out_specs=[row, row, vec, vec],
        out_shape=[jax.ShapeDtypeStruct((S, D), F32), jax.ShapeDtypeStruct((S, D), BF16), vshape, vshape],
        compiler_params=_params("arbitrary"),
    )(y, target, f, gate)


def _ffn_in_fwd(h, w4, name):
    S, D = h.shape
    tm = min(ROW_TILE, S)
    n = w4.shape[2]

    def body(h_ref, wg_ref, wu_ref, zg_ref, zu_ref, a_ref):
        hv = h_ref[...]
        zg = _nn(hv, wg_ref[...])
        zu = _nn(hv, wu_ref[...])
        zg_ref[...] = zg.astype(zg_ref.dtype)
        zu_ref[...] = zu.astype(zu_ref.dtype)
        a_ref[...] = (zg * _sigmoid(zg) * zu).astype(a_ref.dtype)

    out = pl.BlockSpec((tm, n), lambda j, m: (m, j))
    oshape = jax.ShapeDtypeStruct((S, 2 * n), BF16)
    return pl.pallas_call(
        body, name=name, grid=(2, S // tm),
        in_specs=[pl.BlockSpec((tm, D), lambda j, m: (m, 0)),
                  pl.BlockSpec((None, D, n), lambda j, m: (j, 0, 0)),
                  pl.BlockSpec((None, D, n), lambda j, m: (j + 2, 0, 0))],
        out_specs=[out, out, out], out_shape=[oshape, oshape, oshape],
        compiler_params=_params("parallel", "parallel"),
    )(h, w4, w4)


def _proj_out_fwd(lhs, w, x, gate, coef, name):
    S, D = x.shape
    tm = min(ROW_TILE, S)
    ks = [a.shape[1] for a in lhs]

    def body(*refs):
        lhs_refs = refs[:len(lhs)]
        w_ref, x_ref, gate_ref, xn_ref, f_ref = refs[len(lhs):]
        acc, off = None, 0
        for a_ref, k in zip(lhs_refs, ks):
            part = _nn(a_ref[...], w_ref[off:off + k, :])
            acc = part if acc is None else acc + part
            off += k
        f_ref[...] = acc.astype(f_ref.dtype)
        xn_ref[...] = x_ref[...] + coef * gate_ref[...] * acc

    row = pl.BlockSpec((tm, D), lambda m: (m, 0))
    return pl.pallas_call(
        body, name=name, grid=(S // tm,),
        in_specs=[pl.BlockSpec((tm, k), lambda m: (m, 0)) for k in ks]
        + [_resident(w.shape, lambda m: (0, 0)), row, pl.BlockSpec((1, D), lambda m: (0, 0))],
        out_specs=[row, row],
        out_shape=[jax.ShapeDtypeStruct((S, D), F32), jax.ShapeDtypeStruct((S, D), BF16)],
        compiler_params=_params("parallel"),
    )(*lhs, w, x, gate)


def _matmul_nn(a, w, out_dtype, tm, name):
    S, K = a.shape
    N = w.shape[1]
    tm = min(tm, S)

    def body(a_ref, w_ref, o_ref):
        o_ref[...] = _nn(a_ref[...], w_ref[...]).astype(o_ref.dtype)

    return pl.pallas_call(
        body, name=name, grid=(S // tm,),
        in_specs=[pl.BlockSpec((tm, K), lambda m: (m, 0)), _resident((K, N), lambda m: (0, 0))],
        out_specs=pl.BlockSpec((tm, N), lambda m: (m, 0)), out_shape=jax.ShapeDtypeStruct((S, N), out_dtype),
        compiler_params=_params("parallel"),
    )(a, w)


def _dact_bwd(df, w_out, zg, zu, name):
    S, D = df.shape
    tm = min(ROW_TILE, S)
    n = w_out.shape[0] // 2

    def body(df_ref, w_ref, zg_ref, zu_ref, dzg_ref, dzu_ref):
        da = _nt(df_ref[...], w_ref[...])
        zg_v, zu_v = zg_ref[...].astype(F32), zu_ref[...].astype(F32)
        s = _sigmoid(zg_v)
        dzu_ref[...] = (da * zg_v * s).astype(dzu_ref.dtype)
        dzg_ref[...] = (da * zu_v * (s * (1.0 + zg_v * (1.0 - s)))).astype(dzg_ref.dtype)

    blk = pl.BlockSpec((tm, n), lambda j, m: (m, j))
    oshape = jax.ShapeDtypeStruct((S, 2 * n), BF16)
    return pl.pallas_call(
        body, name=name, grid=(2, S // tm),
        in_specs=[pl.BlockSpec((tm, D), lambda j, m: (m, 0)), pl.BlockSpec((n, D), lambda j, m: (j, 0)), blk, blk],
        out_specs=[blk, blk], out_shape=[oshape, oshape], compiler_params=_params("parallel", "parallel"),
    )(df, w_out, zg, zu)


def _ffn_in_dgrad(dzg, dzu, w4, name):
    S = dzg.shape[0]
    D, n = w4.shape[1], w4.shape[2]
    tm = min(ROW_TILE, S)

    def body(dzg_ref, dzu_ref, w_ref, dh_ref):
        acc = _nt(dzg_ref[:, 0:n], w_ref[0])
        acc += _nt(dzg_ref[:, n:2 * n], w_ref[1])
        acc += _nt(dzu_ref[:, 0:n], w_ref[2])
        acc += _nt(dzu_ref[:, n:2 * n], w_ref[3])
        dh_ref[...] = acc

    blk = pl.BlockSpec((tm, 2 * n), lambda m: (m, 0))
    return pl.pallas_call(
        body, name=name, grid=(S // tm,),
        in_specs=[blk, blk, _resident(w4.shape, lambda m: (0, 0, 0))],
        out_specs=pl.BlockSpec((tm, D), lambda m: (m, 0)), out_shape=jax.ShapeDtypeStruct((S, D), F32),
        compiler_params=_params("parallel"),
    )(dzg, dzu, w4)


def _matmul_nt(a, w, tm, name):
    S, K = a.shape
    N = w.shape[0]
    tm = min(tm, S)

    def body(a_ref, w_ref, o_ref):
        o_ref[...] = _nt(a_ref[...], w_ref[...])

    return pl.pallas_call(
        body, name=name, grid=(S // tm,),
        in_specs=[pl.BlockSpec((tm, K), lambda m: (m, 0)), _resident((N, K), lambda m: (0, 0))],
        out_specs=pl.BlockSpec((tm, N), lambda m: (m, 0)), out_shape=jax.ShapeDtypeStruct((S, N), F32),
        compiler_params=_params("parallel"),
    )(a, w)


def _wgrad(a, g, tn, name):
    S, Ka = a.shape
    N = g.shape[1]
    ts = min(ROW_TILE, S)

    def body(a_ref, g_ref, o_ref):
        @pl.when(pl.program_id(1) == 0)
        def _():
            o_ref[...] = jnp.zeros_like(o_ref)

        o_ref[...] += _tn(a_ref[...], g_ref[...])

    return pl.pallas_call(
        body, name=name, grid=(N // tn, S // ts),
        in_specs=[pl.BlockSpec((ts, Ka), lambda j, s: (s, 0)), pl.BlockSpec((ts, tn), lambda j, s: (s, j))],
        out_specs=pl.BlockSpec((None, Ka, tn), lambda j, s: (j, 0, 0)),
        out_shape=jax.ShapeDtypeStruct((N // tn, Ka, tn), F32), compiler_params=_params("parallel", "arbitrary"),
    )(a, g)


HG_SUB = 8


def _sub_row(x, j):
    C, W = x.shape
    xs = x.reshape(C // HG_SUB, HG_SUB, W)
    return jnp.broadcast_to(xs[:, j:j + 1, :], xs.shape).reshape(C, W)


def _sub_sum(x):
    C, W = x.shape
    xs = x.reshape(C // HG_SUB, HG_SUB, W)
    return jnp.broadcast_to(jnp.sum(xs, axis=1, keepdims=True), xs.shape).reshape(C, W)


def _sub_split(reverse):
    lo, hi = slice(0, HG_SUB), slice(HG_SUB, 2 * HG_SUB)
    return (lo, hi, HG_SUB) if reverse else (hi, lo, HG_SUB - 1)


def _hgrn_chunk_common(qr, fr, oml, tri, last):
    k = oml * _sigmoid(-fr)
    g = jnp.log1p(-k)
    q = qr * _sigmoid(qr)
    G = _nn(tri, g, precision=lax.Precision.HIGHEST)
    Gl = G[last:last + 1]
    return q, k, G, Gl


def _hgrn_consts(reverse):
    C = HG_CHUNK
    r = lax.broadcasted_iota(jnp.int32, (C, C), 0)
    cc = lax.broadcasted_iota(jnp.int32, (C, C), 1)
    tri = ((cc >= r) if reverse else (cc <= r)).astype(F32)
    tri_t = ((cc <= r) if reverse else (cc >= r)).astype(F32)
    rid = lax.broadcasted_iota(jnp.int32, (C, HG_WIDTH), 0)
    return tri, tri_t, rid, (0 if reverse else C - 1)


def _head_slices():
    return [slice(h * HG_DIM, (h + 1) * HG_DIM) for h in range(HG_HEADS)]


def _per_head_lane_sum(x):
    C = x.shape[0]
    return jnp.concatenate(
        [jnp.broadcast_to(jnp.sum(x[:, sl], axis=-1, keepdims=True), (C, HG_DIM)) for sl in _head_slices()], axis=1)


def _hgrn_fwd(z, lb, direction, name):
    S = z.shape[0]
    C, DK, W = HG_CHUNK, HG_DIM, HG_WIDTH
    tb = min(HG_ROWS, S)
    n_t, n_c = S // tb, tb // C
    reverse = direction == 1
    tmap = (lambda i: n_t - 1 - i) if reverse else (lambda i: i)

    def body(q_ref, f_ref, v_ref, lb_ref, o_ref, st_out_ref, st_ref):
        @pl.when(pl.program_id(0) == 0)
        def _():
            st_ref[...] = jnp.zeros_like(st_ref)

        oml = 1.0 - lb_ref[...]
        tri, _, rid, last = _hgrn_consts(reverse)
        rid8 = jnp.bitwise_and(rid, HG_SUB - 1)
        tq, ts, brow = _sub_split(reverse)

        def chunk(ci, carry):
            cidx = (n_c - 1 - ci) if reverse else ci
            rows = pl.ds(pl.multiple_of(cidx * C, C), C)
            v = v_ref[rows, :]
            q, k, G, Gl = _hgrn_chunk_common(q_ref[rows, :], f_ref[rows, :], oml, tri, last)
            qd = (q * jnp.exp(G)).astype(BF16)
            kd = (k * jnp.exp(Gl - G)).astype(BF16)
            e_gl = jnp.exp(Gl)
            v_b = v.astype(BF16)
            inter = []
            for h, sl in enumerate(_head_slices()):
                st0 = st_ref[h]
                st_out_ref[h, cidx] = st0
                inter.append(_nt(qd[:, sl], st0.astype(BF16)))
                st_ref[h] = st0 * e_gl[:, sl] + _tn(v_b[:, sl], kd[:, sl])
            o = jnp.concatenate(inter, axis=1)
            for j in range(HG_SUB):
                valid = (rid8 <= j) if reverse else (rid8 >= j)
                e_j = jnp.exp(jnp.where(valid, G - _sub_row(G, j), NEG))
                o = o + _per_head_lane_sum(q * _sub_row(k, j) * e_j) * _sub_row(v, j)
            g_b = G[brow:brow + 1]
            q_x = (q[tq] * jnp.exp(G[tq] - g_b)).astype(BF16)
            k_x = (k[ts] * jnp.exp(g_b - G[ts])).astype(BF16)
            cross = [_nn(_nt(q_x[:, sl], k_x[:, sl]).astype(BF16), v_b[ts, sl]) for sl in _head_slices()]
            o_q = o[tq] + jnp.concatenate(cross, axis=1)
            o_ref[rows, :] = jnp.concatenate([o_q, o[ts]] if reverse else [o[ts], o_q], axis=0)
            return carry

        lax.fori_loop(0, n_c, chunk, 0)

    def sec(j):
        return pl.BlockSpec((tb, W), lambda i: (tmap(i), j))

    return pl.pallas_call(
        body, name=name, grid=(n_t,),
        in_specs=[sec(0), sec(1 + direction), sec(3), pl.BlockSpec((1, W), lambda i: (0, 0))],
        out_specs=[sec(0), pl.BlockSpec((HG_HEADS, n_c, DK, DK), lambda i: (0, tmap(i), 0, 0))],
        out_shape=[jax.ShapeDtypeStruct((S, W), F32), jax.ShapeDtypeStruct((HG_HEADS, S // C, DK, DK), F32)],
        scratch_shapes=[pltpu.VMEM((HG_HEADS, DK, DK), F32)],
        compiler_params=_params("arbitrary"),
    )(z, z, z, lb)


def _hgrn_bwd(z, lb, do, states, direction, name, acc=None):
    S = z.shape[0]
    C, DK, W = HG_CHUNK, HG_DIM, HG_WIDTH
    tb = min(HG_ROWS, S)
    n_t, n_c = S // tb, tb // C
    reverse = direction == 1
    tmap = (lambda i: i) if reverse else (lambda i: n_t - 1 - i)

    def body(*refs):
        if acc:
            q_ref, f_ref, v_ref, lb_ref, do_ref, st_in_ref, dqa_ref, dva_ref, dq_ref, df_ref, dv_ref, doml_ref, dst_ref = refs
        else:
            q_ref, f_ref, v_ref, lb_ref, do_ref, st_in_ref, dq_ref, df_ref, dv_ref, doml_ref, dst_ref = refs

        @pl.when(pl.program_id(0) == 0)
        def _():
            dst_ref[...] = jnp.zeros_like(dst_ref)
            doml_ref[...] = jnp.zeros_like(doml_ref)

        oml = 1.0 - lb_ref[...]
        tri, tri_t, rid, last = _hgrn_consts(reverse)
        rid8 = jnp.bitwise_and(rid, HG_SUB - 1)
        tq, ts, brow = _sub_split(reverse)

        def chunk(ci, carry):
            cidx = ci if reverse else (n_c - 1 - ci)
            rows = pl.ds(pl.multiple_of(cidx * C, C), C)
            qr, fr, v, dov = q_ref[rows, :], f_ref[rows, :], v_ref[rows, :], do_ref[rows, :]
            q, k, G, Gl = _hgrn_chunk_common(qr, fr, oml, tri, last)
            e_g, e_gl, e_kd = jnp.exp(G), jnp.exp(Gl), jnp.exp(Gl - G)
            qd, kd = q * e_g, k * e_kd
            do_b, v_b, qd_b, kd_b = dov.astype(BF16), v.astype(BF16), qd.astype(BF16), kd.astype(BF16)
            dqd, dkd, dv, state_dot = [], [], [], []
            for h, sl in enumerate(_head_slices()):
                st0, dst1 = st_in_ref[h, cidx], dst_ref[h]
                dst1_b = dst1.astype(BF16)
                dqd.append(_nn(do_b[:, sl], st0.astype(BF16)))
                dkd.append(_nn(v_b[:, sl], dst1_b))
                dv.append(_nt(kd_b[:, sl], dst1_b))
                state_dot.append(jnp.sum(st0 * dst1, axis=0, keepdims=True))
                dst_ref[h] = dst1 * e_gl[:, sl] + _tn(do_b[:, sl], qd_b[:, sl])
            dqd, dkd, dv = [jnp.concatenate(t, axis=1) for t in (dqd, dkd, dv)]
            d_gl = e_gl * jnp.concatenate(state_dot, axis=1) + jnp.sum(dkd * kd, axis=0, keepdims=True)
            dq, dk = dqd * e_g, dkd * e_kd
            for j in range(HG_SUB):
                valid = (rid8 <= j) if reverse else (rid8 >= j)
                e_j = jnp.exp(jnp.where(valid, G - _sub_row(G, j), NEG))
                k_j = _sub_row(k, j)
                col = _per_head_lane_sum(q * k_j * e_j)
                w_j = _per_head_lane_sum(dov * _sub_row(v, j)) * e_j
                dq = dq + w_j * k_j
                dv = dv + jnp.where(rid8 == j, _sub_sum(col * dov), 0.0)
                dk = dk + jnp.where(rid8 == j, _sub_sum(w_j * q), 0.0)
            g_b = G[brow:brow + 1]
            e_q, e_k = jnp.exp(G[tq] - g_b), jnp.exp(g_b - G[ts])
            q_x, k_x = (q[tq] * e_q).astype(BF16), (k[ts] * e_k).astype(BF16)
            dq_x, dk_x, dv_x = [], [], []
            for sl in _head_slices():
                a_b = _nt(q_x[:, sl], k_x[:, sl]).astype(BF16)
                da_b = _nt(do_b[tq, sl], v_b[ts, sl]).astype(BF16)
                dv_x.append(_tn(a_b, do_b[tq, sl]))
                dq_x.append(_nn(da_b, k_x[:, sl]))
                dk_x.append(_tn(da_b, q_x[:, sl]))
            dq_q = dq[tq] + jnp.concatenate(dq_x, axis=1) * e_q
            dk_s = dk[ts] + jnp.concatenate(dk_x, axis=1) * e_k
            dv_s = dv[ts] + jnp.concatenate(dv_x, axis=1)
            if reverse:
                dq = jnp.concatenate([dq_q, dq[ts]], axis=0)
                dk = jnp.concatenate([dk[tq], dk_s], axis=0)
                dv = jnp.concatenate([dv[tq], dv_s], axis=0)
            else:
                dq = jnp.concatenate([dq[ts], dq_q], axis=0)
                dk = jnp.concatenate([dk_s, dk[tq]], axis=0)
                dv = jnp.concatenate([dv_s, dv[tq]], axis=0)
            d_big_g = dq * q - dk * k + jnp.where(rid == last, d_gl, 0.0)
            dg = _nn(tri_t, d_big_g, precision=lax.Precision.HIGHEST)
            dk_all = dk - dg / (1.0 - k)
            sig_nf = _sigmoid(-fr)
            df_ref[rows, :] = -dk_all * k * (1.0 - sig_nf)
            doml_ref[...] += jnp.sum(dk_all * sig_nf, axis=0, keepdims=True)
            sq = _sigmoid(qr)
            dqr = dq * (sq * (1.0 + qr * (1.0 - sq)))
            if acc:
                dqr = dqr + dqa_ref[rows, :]
                dv = dv + dva_ref[rows, :]
            dq_ref[rows, :] = dqr
            dv_ref[rows, :] = dv
            return carry

        lax.fori_loop(0, n_c, chunk, 0)

    def sec(j):
        return pl.BlockSpec((tb, W), lambda i: (tmap(i), j))

    vec = pl.BlockSpec((1, W), lambda i: (0, 0))
    ins = [z, z, z, lb, do, states]
    in_specs = [sec(0), sec(1 + direction), sec(3), vec, sec(0),
                pl.BlockSpec((HG_HEADS, n_c, DK, DK), lambda i: (0, tmap(i), 0, 0))]
    if acc:
        ins += list(acc)
        in_specs += [sec(0), sec(0)]
    full = jax.ShapeDtypeStruct((S, W), F32)
    return pl.pallas_call(
        body, name=name, grid=(n_t,), in_specs=in_specs,
        out_specs=[sec(0), sec(0), sec(0), vec],
        out_shape=[full, full, full, jax.ShapeDtypeStruct((1, W), F32)],
        scratch_shapes=[pltpu.VMEM((HG_HEADS, DK, DK), F32)],
        compiler_params=_params("arbitrary"),
    )(*ins)


def _hgrn_post_fwd(o_f, o_b, z, norm_g, name):
    S = z.shape[0]
    tr = min(ROW_TILE, S)

    def body(of_ref, ob_ref, gr_ref, ng_ref, y_ref):
        o = of_ref[...] + ob_ref[...]
        gr = gr_ref[...]
        gate = gr * _sigmoid(gr)
        ng = ng_ref[...]
        for h in range(HG_HEADS):
            sl = slice(h * HG_DIM, (h + 1) * HG_DIM)
            oh = o[:, sl]
            rstd = lax.rsqrt(jnp.mean(oh * oh, axis=-1, keepdims=True) + EPS)
            y_ref[:, sl] = (oh * rstd * ng[:, sl] * gate[:, sl]).astype(y_ref.dtype)

    row = pl.BlockSpec((tr, HG_WIDTH), lambda i: (i, 0))
    return pl.pallas_call(
        body, name=name, grid=(S // tr,),
        in_specs=[row, row, pl.BlockSpec((tr, HG_WIDTH), lambda i: (i, 4)), pl.BlockSpec((1, HG_WIDTH), lambda i: (0, 0))],
        out_specs=row, out_shape=jax.ShapeDtypeStruct((S, HG_WIDTH), BF16), compiler_params=_params("parallel"),
    )(o_f, o_b, z, norm_g)


def _hgrn_post_bwd(dy, o_f, o_b, z, norm_g, name):
    S = z.shape[0]
    tr = min(ROW_TILE, S)

    def body(dy_ref, of_ref, ob_ref, gr_ref, ng_ref, do_ref, dgr_ref, dng_ref):
        @pl.when(pl.program_id(0) == 0)
        def _():
            dng_ref[...] = jnp.zeros_like(dng_ref)

        o = of_ref[...] + ob_ref[...]
        gr, ng, dyv = gr_ref[...], ng_ref[...], dy_ref[...]
        sg = _sigmoid(gr)
        for h in range(HG_HEADS):
            sl = slice(h * HG_DIM, (h + 1) * HG_DIM)
            oh, dyh, grh, sgh, ngh = o[:, sl], dyv[:, sl], gr[:, sl], sg[:, sl], ng[:, sl]
            rstd = lax.rsqrt(jnp.mean(oh * oh, axis=-1, keepdims=True) + EPS)
            on = oh * rstd
            du = dyh * (grh * sgh)
            dgr_ref[:, sl] = dyh * (on * ngh) * (sgh * (1.0 + grh * (1.0 - sgh)))
            dng_ref[:, sl] += jnp.sum(du * on, axis=0, keepdims=True)
            don = du * ngh
            do_ref[:, sl] = rstd * (don - on * jnp.mean(don * on, axis=-1, keepdims=True))

    row = pl.BlockSpec((tr, HG_WIDTH), lambda i: (i, 0))
    vec = pl.BlockSpec((1, HG_WIDTH), lambda i: (0, 0))
    full = jax.ShapeDtypeStruct((S, HG_WIDTH), F32)
    return pl.pallas_call(
        body, name=name, grid=(S // tr,),
        in_specs=[row, row, row, pl.BlockSpec((tr, HG_WIDTH), lambda i: (i, 4)), vec],
        out_specs=[row, row, vec], out_shape=[full, full, jax.ShapeDtypeStruct((1, HG_WIDTH), F32)],
        compiler_params=_params("arbitrary"),
    )(dy, o_f, o_b, z, norm_g)


def _t5_bucket_table():
    rel = (np.arange(3 * BLOCK)[None, :] - BLOCK) - np.arange(BLOCK)[:, None]
    nb = NUM_BUCKETS // 2
    max_exact = nb // 2
    ret = (rel > 0).astype(np.int32) * nb
    n = np.abs(rel)
    ratio = np.log(np.maximum(n, 1).astype(np.float32) / np.float32(max_exact)) / np.float32(math.log(MAX_DISTANCE / max_exact))
    large = max_exact + (ratio.astype(np.float32) * np.float32(nb - max_exact)).astype(np.int32)
    large = np.minimum(large, nb - 1)
    bucket = ret + np.where(n < max_exact, n, large)
    return bucket.astype(np.int32), (n <= WINDOW)


def _bias_table(rel_bias, name):
    bucket, in_band = _t5_bucket_table()
    idx = jnp.asarray(np.where(in_band, bucket, -1))

    def body(rb_ref, idx_ref, o_ref):
        h = pl.program_id(0)
        iv = idx_ref[...]
        acc = jnp.where(iv < 0, NEG, 0.0).astype(F32)
        for b in range(NUM_BUCKETS):
            acc = acc + jnp.where(iv == b, rb_ref[b, h], 0.0)
        o_ref[...] = acc

    return pl.pallas_call(
        body, name=name, grid=(ATT_Q_HEADS,),
        in_specs=[pl.BlockSpec(memory_space=pltpu.SMEM), pl.BlockSpec((BLOCK, 3 * BLOCK), lambda h: (0, 0))],
        out_specs=pl.BlockSpec((None, BLOCK, 3 * BLOCK), lambda h: (h, 0, 0)),
        out_shape=jax.ShapeDtypeStruct((ATT_Q_HEADS, BLOCK, 3 * BLOCK), F32), compiler_params=_params("parallel"),
    )(rel_bias, idx)


def _bias_grad(ds_sum, name):
    bucket, in_band = _t5_bucket_table()
    idx = jnp.asarray(np.where(in_band, bucket, -1))

    def body(ds_ref, idx_ref, o_ref):
        iv, ds = idx_ref[...], ds_ref[...]
        for b in range(NUM_BUCKETS):
            part = jnp.sum(jnp.where(iv == b, ds, 0.0), axis=0, keepdims=True)
            o_ref[b:b + 1, :] = part[:, 0:BLOCK] + part[:, BLOCK:2 * BLOCK] + part[:, 2 * BLOCK:3 * BLOCK]

    return pl.pallas_call(
        body, name=name, grid=(ATT_Q_HEADS,),
        in_specs=[pl.BlockSpec((None, BLOCK, 3 * BLOCK), lambda h: (h, 0, 0)), pl.BlockSpec((BLOCK, 3 * BLOCK), lambda h: (0, 0))],
        out_specs=pl.BlockSpec((None, NUM_BUCKETS, BLOCK), lambda h: (h, 0, 0)),
        out_shape=jax.ShapeDtypeStruct((ATT_Q_HEADS, NUM_BUCKETS, BLOCK), F32), compiler_params=_params("parallel"),
    )(ds_sum, idx)


def _attn_specs(nb):
    G, dh = ATT_GROUP, ATT_HEAD_DIM
    qspec = pl.BlockSpec((G, BLOCK, dh), lambda j, n: (j, n, 0))

    def kv(shift):
        return pl.BlockSpec((None, BLOCK, dh), lambda j, n: (j, jnp.clip(n + shift, 0, nb - 1), 0))

    gain = pl.BlockSpec((1, dh), lambda j, n: (0, 0))
    sink = pl.BlockSpec((G, 1, BLOCK), lambda j, n: (j, 0, 0))
    bias = pl.BlockSpec((G, BLOCK, 3 * BLOCK), lambda j, n: (j, 0, 0))
    return qspec, kv, gain, sink, bias


def _attn_probs(qh, kn, bias_h, sink_h, edge_ok):
    s = _nt(qh.astype(BF16), kn.astype(BF16)) * (1.0 / math.sqrt(ATT_HEAD_DIM)) + bias_h
    s = jnp.where(edge_ok, s, NEG)
    m = jnp.maximum(jnp.max(s, axis=-1, keepdims=True), sink_h)
    p = jnp.exp(s - m)
    e_sink = jnp.exp(sink_h - m)
    inv = 1.0 / (jnp.sum(p, axis=-1, keepdims=True) + e_sink)
    return p * inv, e_sink * inv


def _rms_rows(x):
    rstd = lax.rsqrt(jnp.mean(x * x, axis=-1, keepdims=True) + EPS)
    return x * rstd, rstd


def _edge_ok(n, nb):
    colid = lax.broadcasted_iota(jnp.int32, (ATT_GROUP * BLOCK, 3 * BLOCK), 1)
    return jnp.logical_and(jnp.logical_or(colid >= BLOCK, n > 0), jnp.logical_or(colid < 2 * BLOCK, n < nb - 1))


def _sink_column(sink_ref):
    return jnp.concatenate([jnp.broadcast_to(sink_ref[g][:, 0:1], (BLOCK, 1)) for g in range(ATT_GROUP)], axis=0)


def _attn_fwd(q, k, v, q_g, k_g, sink, bias, name):
    S = q.shape[1]
    nb = S // BLOCK
    G, dh = ATT_GROUP, ATT_HEAD_DIM
    qspec, kv, gain, sink_spec, bias_spec = _attn_specs(nb)

    def body(q_ref, k0, k1, k2, v0, v1, v2, qg_ref, kg_ref, sink_ref, bias_ref, o_ref):
        n = pl.program_id(1)
        kcat = jnp.concatenate([k0[...], k1[...], k2[...]], axis=0)
        vcat = jnp.concatenate([v0[...], v1[...], v2[...]], axis=0).astype(BF16)
        kn = _rms_rows(kcat)[0] * kg_ref[...]
        qn = _rms_rows(q_ref[...].reshape(G * BLOCK, dh))[0] * qg_ref[...]
        p, _ = _attn_probs(qn, kn, bias_ref[...].reshape(G * BLOCK, 3 * BLOCK), _sink_column(sink_ref), _edge_ok(n, nb))
        o_ref[...] = _nn(p.astype(BF16), vcat).reshape(G, BLOCK, dh)

    return pl.pallas_call(
        body, name=name, grid=(ATT_KV_HEADS, nb),
        in_specs=[qspec, kv(-1), kv(0), kv(1), kv(-1), kv(0), kv(1), gain, gain, sink_spec, bias_spec],
        out_specs=qspec, out_shape=jax.ShapeDtypeStruct(q.shape, F32), compiler_params=_params("parallel", "parallel"),
    )(q, k, k, k, v, v, v, q_g, k_g, sink, bias)


def _attn_bwd(q, k, v, q_g, k_g, sink, bias, do, name):
    S = q.shape[1]
    nb = S // BLOCK
    G, dh = ATT_GROUP, ATT_HEAD_DIM
    scale = 1.0 / math.sqrt(dh)
    qspec, kv, gain, sink_spec, bias_spec = _attn_specs(nb)

    def body(q_ref, k0, k1, k2, v0, v1, v2, qg_ref, kg_ref, sink_ref, bias_ref, do_ref,
             dq_ref, dkw_ref, dvw_ref, ds_ref, dsink_ref, dqg_ref):
        n = pl.program_id(1)

        @pl.when(n == 0)
        def _():
            ds_ref[...] = jnp.zeros_like(ds_ref)
            dsink_ref[...] = jnp.zeros_like(dsink_ref)
            dqg_ref[...] = jnp.zeros_like(dqg_ref)

        kcat = jnp.concatenate([k0[...], k1[...], k2[...]], axis=0)
        vcat = jnp.concatenate([v0[...], v1[...], v2[...]], axis=0).astype(BF16)
        kn = _rms_rows(kcat)[0] * kg_ref[...]
        qg = qg_ref[...]
        qhat, rstd = _rms_rows(q_ref[...].reshape(G * BLOCK, dh))
        qn = qhat * qg
        p, p_sink = _attn_probs(qn, kn, bias_ref[...].reshape(G * BLOCK, 3 * BLOCK), _sink_column(sink_ref), _edge_ok(n, nb))
        do_b = do_ref[...].reshape(G * BLOCK, dh).astype(BF16)
        dp = _nt(do_b, vcat)
        delta = jnp.sum(p * dp, axis=-1, keepdims=True)
        ds = p * (dp - delta)
        ds_ref[...] += ds.reshape(G, BLOCK, 3 * BLOCK)
        sink_term = p_sink * delta
        for g in range(G):
            dsink_ref[g] += jnp.zeros((1, BLOCK), F32) - jnp.sum(sink_term[g * BLOCK:(g + 1) * BLOCK], axis=0, keepdims=True)
        ds_b = ds.astype(BF16)
        dvw_ref[...] = _tn(p.astype(BF16), do_b)
        dkw_ref[...] = _tn(ds_b, qn.astype(BF16)) * scale
        dqn = _nn(ds_b, kn.astype(BF16)) * scale
        dqg_ref[...] += jnp.sum(dqn * qhat, axis=0, keepdims=True)
        dqh = dqn * qg
        dq_ref[...] = (rstd * (dqh - qhat * jnp.mean(dqh * qhat, axis=-1, keepdims=True))).reshape(G, BLOCK, dh)

    win = pl.BlockSpec((None, None, 3 * BLOCK, dh), lambda j, n: (j, n, 0, 0))
    wshape = jax.ShapeDtypeStruct((ATT_KV_HEADS, nb, 3 * BLOCK, dh), F32)
    return pl.pallas_call(
        body, name=name, grid=(ATT_KV_HEADS, nb),
        in_specs=[qspec, kv(-1), kv(0), kv(1), kv(-1), kv(0), kv(1), gain, gain, sink_spec, bias_spec, qspec],
        out_specs=[qspec, win, win, bias_spec, sink_spec, pl.BlockSpec((None, 1, dh), lambda j, n: (j, 0, 0))],
        out_shape=[jax.ShapeDtypeStruct(q.shape, F32), wshape, wshape,
                   jax.ShapeDtypeStruct((ATT_Q_HEADS, BLOCK, 3 * BLOCK), F32),
                   jax.ShapeDtypeStruct((ATT_Q_HEADS, 1, BLOCK), F32),
                   jax.ShapeDtypeStruct((ATT_KV_HEADS, 1, dh), F32)],
        compiler_params=_params("parallel", "arbitrary"),
    )(q, k, k, k, v, v, v, q_g, k_g, sink, bias, do)


def _attn_kv_reduce(dkw, dvw, k, k_g, name):
    S = k.shape[1]
    nb = S // BLOCK
    dh = ATT_HEAD_DIM
    kb = min(8, nb)
    steps = nb // kb

    def body(a_lo, a, a_hi, b_lo, b, b_hi, k_ref, kg_ref, dk_ref, dv_ref, dkg_ref):
        n = pl.program_id(1)

        @pl.when(n == 0)
        def _():
            dkg_ref[...] = jnp.zeros_like(dkg_ref)

        lo = jnp.where(n > 0, 1.0, 0.0)
        hi = jnp.where(n < steps - 1, 1.0, 0.0)

        def overlap_add(w, w_lo, w_hi, i):
            before = lo * w_lo[...] if i == 0 else w[i - 1, 2 * BLOCK:3 * BLOCK, :]
            after = hi * w_hi[...] if i == kb - 1 else w[i + 1, 0:BLOCK, :]
            return w[i, BLOCK:2 * BLOCK, :] + before + after

        dkg = jnp.zeros((1, dh), F32)
        for i in range(kb):
            rows = slice(i * BLOCK, (i + 1) * BLOCK)
            dkn = overlap_add(a, a_lo, a_hi, i)
            dv_ref[rows, :] = overlap_add(b, b_lo, b_hi, i)
            khat, rstd = _rms_rows(k_ref[rows, :])
            dkg = dkg + jnp.sum(dkn * khat, axis=0, keepdims=True)
            dkh = dkn * kg_ref[...]
            dk_ref[rows, :] = rstd * (dkh - khat * jnp.mean(dkh * khat, axis=-1, keepdims=True))
        dkg_ref[...] += dkg

    main = pl.BlockSpec((None, kb, 3 * BLOCK, dh), lambda j, n: (j, n, 0, 0))
    halo_lo = pl.BlockSpec((None, None, BLOCK, dh), lambda j, n: (j, jnp.maximum(n * kb - 1, 0), 2, 0))
    halo_hi = pl.BlockSpec((None, None, BLOCK, dh), lambda j, n: (j, jnp.minimum(n * kb + kb, nb - 1), 0, 0))
    blk = pl.BlockSpec((None, kb * BLOCK, dh), lambda j, n: (j, n, 0))
    return pl.pallas_call(
        body, name=name, grid=(ATT_KV_HEADS, steps),
        in_specs=[halo_lo, main, halo_hi, halo_lo, main, halo_hi, blk, pl.BlockSpec((1, dh), lambda j, n: (0, 0))],
        out_specs=[blk, blk, pl.BlockSpec((None, 1, dh), lambda j, n: (j, 0, 0))],
        out_shape=[jax.ShapeDtypeStruct(k.shape, F32), jax.ShapeDtypeStruct(k.shape, F32),
                   jax.ShapeDtypeStruct((ATT_KV_HEADS, 1, dh), F32)],
        compiler_params=_params("parallel", "arbitrary"),
    )(dkw, dkw, dkw, dvw, dvw, dvw, k, k_g)


def _ada_fwd(c_act, w, b, name):
    n = w.shape[1]

    def body(c_ref, w_ref, b_ref, o_ref):
        o_ref[...] = _nn(c_ref[...], w_ref[...], precision=lax.Precision.HIGHEST) + b_ref[...]

    tn = n // 3
    return pl.pallas_call(
        body, name=name, grid=(3,),
        in_specs=[pl.BlockSpec(c_act.shape, lambda j: (0, 0)), pl.BlockSpec((w.shape[0], tn), lambda j: (0, j)),
                  pl.BlockSpec((1, tn), lambda j: (0, j))],
        out_specs=pl.BlockSpec((c_act.shape[0], tn), lambda j: (0, j)),
        out_shape=jax.ShapeDtypeStruct((c_act.shape[0], n), F32), compiler_params=_params("parallel"),
    )(c_act, w, b)


def _ada_wgrad(c_act_t, dm, name):
    D, nbatch = c_act_t.shape
    n = dm.shape[1]
    tr = 256

    def body(c_ref, dm_ref, o_ref):
        cv, dv = c_ref[...], dm_ref[...]
        acc = cv[:, 0:1] * dv[0:1, :]
        for b in range(1, nbatch):
            acc = acc + cv[:, b:b + 1] * dv[b:b + 1, :]
        o_ref[...] = acc

    return pl.pallas_call(
        body, name=name, grid=(D // tr,),
        in_specs=[pl.BlockSpec((tr, nbatch), lambda i: (i, 0)), pl.BlockSpec((nbatch, n), lambda i: (0, 0))],
        out_specs=pl.BlockSpec((tr, n), lambda i: (i, 0)), out_shape=jax.ShapeDtypeStruct((D, n), F32),
        compiler_params=_params("parallel"),
    )(c_act_t, dm)


def _adamw(w, g, m, v, name):
    R, Cn = w.shape
    tr = R
    for cand in (256, 128, 64, 32, 16, 8):
        if R % cand == 0:
            tr = cand
            break

    def body(w_ref, g_ref, m_ref, v_ref, d_ref, nm_ref, nv_ref):
        gv = g_ref[...]
        m_new = ADAM_B1 * m_ref[...] + (1.0 - ADAM_B1) * gv
        v_new = ADAM_B2 * v_ref[...] + (1.0 - ADAM_B2) * (gv * gv)
        m_hat = m_new / (1.0 - ADAM_B1 ** ADAM_STEP)
        v_hat = v_new / (1.0 - ADAM_B2 ** ADAM_STEP)
        d_ref[...] = -ADAM_LR * (m_hat / (jnp.sqrt(v_hat) + ADAM_EPS) + ADAM_WD * w_ref[...])
        nm_ref[...] = m_new
        nv_ref[...] = v_new

    blk = pl.BlockSpec((tr, Cn), lambda i: (i, 0))
    shp = jax.ShapeDtypeStruct((R, Cn), F32)
    return pl.pallas_call(
        body, name=name, grid=(R // tr,), in_specs=[blk] * 4, out_specs=[blk] * 3, out_shape=[shp] * 3,
        compiler_params=_params("parallel"),
    )(w, g, m, v)


def _place():
    return lax.axis_index("x"), lax.axis_index("y"), lax.axis_index("c")


def _flip(place, k):
    x, y, c = place
    return (1 - x if k & 4 else x, 1 - y if k & 2 else y, 1 - c if k & 1 else c)


def _dev_index(place):
    x, y, c = place
    return 4 * x + 2 * y + c


def _chip_index(place):
    return 2 * place[0] + place[1]


def _allgather8(x, name, reduce=False):
    R, Cn = x.shape

    def body(x_ref, *rest):
        if reduce:
            out_ref, sum_ref, send_sems, recv_sems, local_sem = rest
        else:
            out_ref, send_sems, recv_sems, local_sem = rest
        me = _place()
        mine = pltpu.make_async_copy(x_ref, out_ref.at[_dev_index(me)], local_sem)
        mine.start()

        def copy(k, origin, to):
            return pltpu.make_async_remote_copy(
                src_ref=x_ref, dst_ref=out_ref.at[_dev_index(origin)], send_sem=send_sems.at[k - 1],
                recv_sem=recv_sems.at[k - 1], device_id=to, device_id_type=MESH)

        sends = [copy(k, me, _flip(me, k)) for k in range(1, 8)]
        for cp in sends:
            cp.start()
        for k in range(1, 8):
            copy(k, _flip(me, k), me).wait_recv()
        for cp in sends:
            cp.wait_send()
        mine.wait()
        if reduce:
            acc = out_ref[0]
            for i in range(1, 8):
                acc = acc + out_ref[i]
            sum_ref[...] = acc

    vm = pl.BlockSpec(memory_space=pltpu.VMEM)
    outs = [jax.ShapeDtypeStruct((8, R, Cn), F32)] + ([jax.ShapeDtypeStruct((R, Cn), F32)] if reduce else [])
    res = pl.pallas_call(
        body, name=name, in_specs=[vm], out_specs=[vm] * len(outs), out_shape=outs,
        scratch_shapes=[pltpu.SemaphoreType.DMA((7,)), pltpu.SemaphoreType.DMA((7,)), pltpu.SemaphoreType.DMA],
    )(x)
    return res if reduce else res[0]


def _weights_allgather(shards, name):
    n = len(shards)
    per = 8

    def body(*refs):
        in_refs, out_refs = refs[:n], refs[n:2 * n]
        send_sems, recv_sems = refs[2 * n:]
        me = _place()
        c = me[2]
        sibling = _flip(me, 1)
        others = [_flip(me, 2 * j) for j in (1, 2, 3)]

        def copy(a, k, src, dst, to):
            return pltpu.make_async_remote_copy(
                src_ref=src, dst_ref=dst, send_sem=send_sems.at[per * a + k], recv_sem=recv_sems.at[per * a + k],
                device_id=to, device_id_type=MESH)

        def block(a, place, half):
            return out_refs[a].at[_chip_index(place), half]

        started = []
        for a in range(n):
            sends = [copy(a, 0, in_refs[a].at[c], block(a, me, c), sibling),
                     copy(a, 7, in_refs[a].at[1 - c], block(a, me, 1 - c), sibling)]
            sends += [copy(a, 1 + j, in_refs[a].at[c], block(a, me, c), to) for j, to in enumerate(others)]
            for cp in sends:
                cp.start()
            started += sends
        for a in range(n):
            for j, other in enumerate(others):
                landed = block(a, other, c)
                copy(a, 1 + j, landed, landed, me).wait_recv()
                fwd = copy(a, 4 + j, landed, landed, sibling)
                fwd.start()
                started.append(fwd)
        for a in range(n):
            copy(a, 0, block(a, me, 1 - c), block(a, me, 1 - c), me).wait_recv()
            copy(a, 7, block(a, me, c), block(a, me, c), me).wait_recv()
            for j, other in enumerate(others):
                got = block(a, other, 1 - c)
                copy(a, 4 + j, got, got, me).wait_recv()
        for cp in started:
            cp.wait_send()

    return pl.pallas_call(
        body, name=name, in_specs=[ANY] * n, out_specs=[ANY] * n,
        out_shape=[jax.ShapeDtypeStruct((N_CHIPS,) + s.shape, s.dtype) for s in shards],
        scratch_shapes=[pltpu.SemaphoreType.DMA((per * n,)), pltpu.SemaphoreType.DMA((per * n,))],
    )(*shards)


def _halves_exchange(grads, name):
    n = len(grads)

    def body(*refs):
        in_refs, got_refs = refs[:n], refs[n:2 * n]
        send_sems, recv_sems = refs[2 * n:]
        me = _place()
        c = me[2]
        sibling = _flip(me, 1)
        started = []
        for a in range(n):
            for kk in range(N_CHIPS):
                i = N_CHIPS * a + kk
                send = pltpu.make_async_remote_copy(
                    src_ref=in_refs[a].at[kk, 1 - c], dst_ref=got_refs[a].at[kk], send_sem=send_sems.at[i],
                    recv_sem=recv_sems.at[i], device_id=sibling, device_id_type=MESH)
                send.start()
                started.append(send)
        for send in started:
            send.wait_recv()
        for send in started:
            send.wait_send()

    return pl.pallas_call(
        body, name=name, in_specs=[ANY] * n, out_specs=[ANY] * n,
        out_shape=[jax.ShapeDtypeStruct((N_CHIPS,) + g.shape[2:], g.dtype) for g in grads],
        scratch_shapes=[pltpu.SemaphoreType.DMA((N_CHIPS * n,)), pltpu.SemaphoreType.DMA((N_CHIPS * n,))],
    )(*grads)


def _chips_exchange(parts, name):
    n = len(parts)

    def body(*refs):
        in_refs, out_refs = refs[:n], refs[n:2 * n]
        send_sems, recv_sems = refs[2 * n:]
        me = _place()
        started = []
        for a in range(n):
            for j in (1, 2, 3):
                peer = _flip(me, 2 * j)
                send = pltpu.make_async_remote_copy(
                    src_ref=in_refs[a].at[_chip_index(peer)], dst_ref=out_refs[a].at[j - 1],
                    send_sem=send_sems.at[3 * a + j - 1], recv_sem=recv_sems.at[3 * a + j - 1],
                    device_id=peer, device_id_type=MESH)
                send.start()
                started.append(send)
        for send in started:
            send.wait_recv()
        for send in started:
            send.wait_send()

    return pl.pallas_call(
        body, name=name, in_specs=[ANY] * n, out_specs=[ANY] * n,
        out_shape=[jax.ShapeDtypeStruct((3,) + p.shape[1:], p.dtype) for p in parts],
        scratch_shapes=[pltpu.SemaphoreType.DMA((3 * n,)), pltpu.SemaphoreType.DMA((3 * n,))],
    )(*parts)


def _siblings_exchange(halves, name):
    n = len(halves)

    def body(*refs):
        in_refs, out_refs = refs[:n], refs[n:2 * n]
        send_sems, recv_sems = refs[2 * n:]
        sibling = _flip(_place(), 1)
        started = []
        for a in range(n):
            send = pltpu.make_async_remote_copy(
                src_ref=in_refs[a], dst_ref=out_refs[a], send_sem=send_sems.at[a], recv_sem=recv_sems.at[a],
                device_id=sibling, device_id_type=MESH)
            send.start()
            started.append(send)
        for send in started:
            send.wait_recv()
        for send in started:
            send.wait_send()

    return pl.pallas_call(
        body, name=name, in_specs=[ANY] * n, out_specs=[ANY] * n,
        out_shape=[jax.ShapeDtypeStruct(h.shape, h.dtype) for h in halves],
        scratch_shapes=[pltpu.SemaphoreType.DMA((n,)), pltpu.SemaphoreType.DMA((n,))],
    )(*halves)


def _row_tile(rows):
    for cand in (256, 176, 128, 64, 32, 16, 8):
        if rows % cand == 0:
            return cand
    return rows


def _pair_sum(core, grad, theirs, name):
    N, _, R, Cn = grad.shape
    tr = _row_tile(R)

    def body(core_ref, g_ref, t_ref, o_ref, ob_ref):
        s = g_ref[...] + t_ref[...]
        o_ref[...] = s
        ob_ref[...] = s.astype(BF16)

    out = pl.BlockSpec((None, tr, Cn), lambda k, i, core_ref: (k, i, 0))
    return pl.pallas_call(
        body, name=name,
        grid_spec=pltpu.PrefetchScalarGridSpec(
            num_scalar_prefetch=1, grid=(N, R // tr),
            in_specs=[pl.BlockSpec((None, None, tr, Cn), lambda k, i, core_ref: (k, core_ref[0], i, 0)),
                      pl.BlockSpec((None, tr, Cn), lambda k, i, core_ref: (k, i, 0))],
            out_specs=[out, out]),
        out_shape=[jax.ShapeDtypeStruct((N, R, Cn), F32), jax.ShapeDtypeStruct((N, R, Cn), BF16)],
        compiler_params=_params("parallel", "parallel"),
    )(core, grad, theirs)


def _chip_sum(chip, parts, landed, name):
    _, R, Cn = parts.shape
    tr = _row_tile(R)

    def body(chip_ref, p_ref, l_ref, o_ref):
        o_ref[...] = ((p_ref[...] + l_ref[0].astype(F32)) + l_ref[1].astype(F32)) + l_ref[2].astype(F32)

    return pl.pallas_call(
        body, name=name,
        grid_spec=pltpu.PrefetchScalarGridSpec(
            num_scalar_prefetch=1, grid=(R // tr,),
            in_specs=[pl.BlockSpec((None, tr, Cn), lambda i, chip_ref: (chip_ref[0], i, 0)),
                      pl.BlockSpec((3, tr, Cn), lambda i, chip_ref: (0, i, 0))],
            out_specs=pl.BlockSpec((tr, Cn), lambda i, chip_ref: (i, 0))),
        out_shape=jax.ShapeDtypeStruct((R, Cn), F32), compiler_params=_params("parallel"),
    )(chip, parts, landed)


def _reduce_scatter(grads, core, chip, tag):
    theirs = _halves_exchange(grads, f"{tag}_halves_exchange")
    parts = [_pair_sum(core, g, t, f"{tag}_pair_sum_{i}") for i, (g, t) in enumerate(zip(grads, theirs))]
    landed = _chips_exchange([p[1] for p in parts], f"{tag}_chips_exchange")
    halves = [_chip_sum(chip, p[0], l, f"{tag}_chip_sum_{i}") for i, (p, l) in enumerate(zip(parts, landed))]
    return list(zip(halves, _siblings_exchange(halves, f"{tag}_siblings_exchange")))


def _adamw_halves(core, w, g_mine, g_theirs, m, v, name):
    R2, Cn = w.shape
    r = R2 // 2
    tr = _row_tile(r)
    nt = r // tr

    def body(core_ref, w_ref, gm_ref, gt_ref, m_ref, v_ref, g_ref, d_ref, nm_ref, nv_ref):
        gv = jnp.where(pl.program_id(0) == core_ref[0], gm_ref[...], gt_ref[...])
        g_ref[...] = gv
        m_new = ADAM_B1 * m_ref[...] + (1.0 - ADAM_B1) * gv
        v_new = ADAM_B2 * v_ref[...] + (1.0 - ADAM_B2) * (gv * gv)
        m_hat = m_new / (1.0 - ADAM_B1 ** ADAM_STEP)
        v_hat = v_new / (1.0 - ADAM_B2 ** ADAM_STEP)
        d_ref[...] = -ADAM_LR * (m_hat / (jnp.sqrt(v_hat) + ADAM_EPS) + ADAM_WD * w_ref[...])
        nm_ref[...] = m_new
        nv_ref[...] = v_new

    full = pl.BlockSpec((tr, Cn), lambda hf, i, core_ref: (hf * nt + i, 0))
    half = pl.BlockSpec((tr, Cn), lambda hf, i, core_ref: (i, 0))
    shp = jax.ShapeDtypeStruct((R2, Cn), F32)
    return pl.pallas_call(
        body, name=name,
        grid_spec=pltpu.PrefetchScalarGridSpec(
            num_scalar_prefetch=1, grid=(2, nt), in_specs=[full, half, half, full, full], out_specs=[full] * 4),
        out_shape=[shp] * 4, compiler_params=_params("parallel", "parallel"),
    )(core, w, g_mine, g_theirs, m, v)


def _pad_row(v, width):
    v = v.reshape(1, -1)
    return jnp.pad(v, ((0, 0), (0, width - v.shape[1])))


def _ffn_forward(x, ng, shift, scale, gate, w_in4, w_out, tag):
    h = _rmsmod_fwd(x, ng, shift, scale, f"{tag}_norm")
    zg, zu, a = _ffn_in_fwd(h, w_in4, f"{tag}_in")
    x_new, f = _proj_out_fwd([a], w_out, x, gate, 0.5, f"{tag}_out")
    return x_new, (h, zg, zu, a, f)


def _ffn_backward(df, saved, w_in4, w_out, tag):
    h, zg, zu, a, _ = saved
    dzg, dzu = _dact_bwd(df, w_out, zg, zu, f"{tag}_dact")
    dw_out = _wgrad(a, df, 512, f"{tag}_dw_out")
    dw_out = jnp.concatenate([dw_out[0], dw_out[1]], axis=1)
    dh = _ffn_in_dgrad(dzg, dzu, w_in4, f"{tag}_dh")
    dw_in = jnp.concatenate([_wgrad(h, dzg, FF_SHARD, f"{tag}_dw_gate"), _wgrad(h, dzu, FF_SHARD, f"{tag}_dw_up")], axis=0)
    return dh, dw_in, dw_out


def kernel(x, c, w_ada, b_ada, norm_g, w_ffn1_in, w_ffn1_out, w_ffn2_in, w_ffn2_out, w_mix_in, w_mix_out, hgrn_lb, hgrn_norm_g, qk_norm_g, attn_sink, rel_bias, loss_target, m_w_ada, m_b_ada, m_norm_g, m_w_ffn1_in, m_w_ffn1_out, m_w_ffn2_in, m_w_ffn2_out, m_w_mix_in, m_w_mix_out, m_hgrn_lb, m_hgrn_norm_g, m_qk_norm_g, m_attn_sink, m_rel_bias, v_w_ada, v_b_ada, v_norm_g, v_w_ffn1_in, v_w_ffn1_out, v_w_ffn2_in, v_w_ffn2_out, v_w_mix_in, v_w_mix_out, v_hgrn_lb, v_hgrn_norm_g, v_qk_norm_g, v_attn_sink, v_rel_bias):
    D = D_MODEL
    S = x.shape[1]
    place = (lax.axis_index("x"), lax.axis_index("y"), lax.axis_index("c"))
    me, my_chip = _dev_index(place), _chip_index(place)
    x0 = x[0]
    target = loss_target[0]

    def halves(w):
        return w.astype(BF16).reshape(2, w.shape[0] // 2, w.shape[1])

    gathered = _weights_allgather(
        [halves(w_ffn1_in[0]), halves(w_ffn1_out[0]), halves(w_ffn2_in[0]), halves(w_ffn2_out[0]), halves(w_mix_in[0]),
         halves(w_mix_out[0])], "weights_allgather")
    w1_in = gathered[0].reshape(N_CHIPS, D, FF_SHARD)
    w1_out = gathered[1].reshape(D_FF, D)
    w2_in = gathered[2].reshape(N_CHIPS, D, FF_SHARD)
    w2_out = gathered[3].reshape(D_FF, D)
    wm_in = gathered[4].reshape(N_CHIPS, D, D_IN // N_CHIPS).transpose(1, 0, 2).reshape(D, D_IN)
    wm_out = gathered[5].reshape(D, D)

    small = jnp.concatenate([_pad_row(c, D), _pad_row(norm_g, D), _pad_row(hgrn_lb, D), jnp.zeros((5, D), F32)], axis=0)
    small_all = _allgather8(small, "small_allgather")
    c_all = small_all[:, 0, :]
    by_chip = small_all[0::2]
    norm_g_full = by_chip[:, 1, :3 * 256].reshape(N_CHIPS, 3, 256).transpose(1, 0, 2).reshape(3, D)
    lb_raw = by_chip[:, 2, :2 * 2 * 128].reshape(N_CHIPS, 2, 2, 128).transpose(1, 2, 0, 3).reshape(2, 2, HG_WIDTH)
    lb = jax.nn.sigmoid(lb_raw[:, 0, :] - lb_raw[:, 1, :])
    lb_f, lb_b = lb[0:1], lb[1:2]

    c_act_all = c_all * jax.nn.sigmoid(c_all)
    n_ada = w_ada.shape[2]
    b_mine = lax.dynamic_slice_in_dim(b_ada, my_chip * n_ada, n_ada, axis=1)
    mods_part = _ada_fwd(c_act_all, w_ada[0], b_mine, "ada_fwd")
    mods_all = _allgather8(mods_part, "mods_allgather")[0::2].transpose(1, 0, 2).reshape(8, N_MOD * D)
    mods = lax.dynamic_slice_in_dim(mods_all, me, 1, axis=0)
    sh1, sc1, g1, sh2, sc2, g2, sh3, sc3, g3 = [mods[:, i * D:(i + 1) * D] for i in range(N_MOD)]

    x1, saved1 = _ffn_forward(x0, norm_g_full[0:1], sh1, sc1, g1, w1_in, w1_out, "ffn1")

    h2 = _rmsmod_fwd(x1, norm_g_full[1:2], sh2, sc2, "mix_norm")
    z = _matmul_nn(h2, wm_in, F32, 256, "mix_in")
    of, st_f = _hgrn_fwd(z, lb_f, 0, "hgrn_fwd_f")
    ob, st_b = _hgrn_fwd(z, lb_b, 1, "hgrn_fwd_b")
    o_h = _hgrn_post_fwd(of, ob, z, hgrn_norm_g, "hgrn_post")

    def to_heads(t, nh):
        return t.reshape(S, nh, ATT_HEAD_DIM).transpose(1, 0, 2)

    aq = to_heads(z[:, 5 * HG_WIDTH:5 * HG_WIDTH + ATT_WIDTH], ATT_Q_HEADS)
    ak = to_heads(z[:, 5 * HG_WIDTH + ATT_WIDTH:5 * HG_WIDTH + ATT_WIDTH + KV_WIDTH], ATT_KV_HEADS)
    av = to_heads(z[:, 5 * HG_WIDTH + ATT_WIDTH + KV_WIDTH:], ATT_KV_HEADS)
    q_g, k_g = qk_norm_g[0, 0:1], qk_norm_g[0, 1:2]
    sink_b = jnp.broadcast_to(attn_sink.reshape(ATT_Q_HEADS, 1, 1), (ATT_Q_HEADS, 1, BLOCK))
    bias = _bias_table(rel_bias, "bias_table")
    o_attn = _attn_fwd(aq, ak, av, q_g, k_g, sink_b, bias, "attn_fwd")
    o_a = o_attn.transpose(1, 0, 2).reshape(S, ATT_WIDTH).astype(BF16)
    x2, mixed = _proj_out_fwd([o_h, o_a], wm_out, x1, g2, 1.0, "mix_out")

    x3, saved3 = _ffn_forward(x2, norm_g_full[2:3], sh3, sc3, g3, w2_in, w2_out, "ffn2")

    dx3, df3, dg3, sq_cols = _loss_bwd(x3, target, saved3[4], g3, 0.5, "loss")
    loss_mine = 0.5 * jnp.sum(sq_cols) / D

    dh3, dw2_in, dw2_out = _ffn_backward(df3, saved3, w2_in, w2_out, "ffn2")
    dx2, dsh3, dsc3, dng3, dmixed, dg2 = _rmsmod_bwd(dh3, x2, norm_g_full[2:3], sc3, dx3, "ffn2_norm_bwd", below=(mixed, g2, 1.0))

    do_cat = _matmul_nt(dmixed, wm_out, ROW_TILE, "mix_out_dgrad")
    dwm_out = jnp.concatenate([_wgrad(o_h, dmixed, 512, "mix_out_dw_h"), _wgrad(o_a, dmixed, 512, "mix_out_dw_a")], axis=1)
    dwm_out = jnp.concatenate([dwm_out[0], dwm_out[1]], axis=1)

    do_sum, dgr, d_hnorm = _hgrn_post_bwd(do_cat, of, ob, z, hgrn_norm_g, "hgrn_post_bwd")
    dq_f, dff, dv_f, doml_f = _hgrn_bwd(z, lb_f, do_sum, st_f, 0, "hgrn_bwd_f")
    dhq, dfb, dhi, doml_b = _hgrn_bwd(z, lb_b, do_sum, st_b, 1, "hgrn_bwd_b", acc=(dq_f, dv_f))

    do_a = to_heads(do_cat[:, HG_WIDTH:], ATT_Q_HEADS)
    daq, dkw, dvw, ds_sum, dsink, dqg = _attn_bwd(aq, ak, av, q_g, k_g, sink_b, bias, do_a, "attn_bwd")
    dak, dav, dkg = _attn_kv_reduce(dkw, dvw, ak, k_g, "attn_kv_reduce")
    d_rel_bias = jnp.sum(_bias_grad(ds_sum, "bias_grad"), axis=-1).T

    def from_heads(t):
        return t.transpose(1, 0, 2).reshape(S, -1)

    dz = jnp.concatenate([dhq, dff, dfb, dhi, dgr, from_heads(daq), from_heads(dak), from_heads(dav)], axis=1).astype(BF16)
    dh2 = _matmul_nt(dz, wm_in, 256, "mix_in_dgrad")
    dwm_in = _wgrad(h2, dz, D_IN // 2, "mix_in_dw")
    dwm_in = jnp.concatenate([dwm_in[0], dwm_in[1]], axis=1)
    dx1, dsh2, dsc2, dng2, df1, dg1 = _rmsmod_bwd(dh2, x1, norm_g_full[1:2], sc2, dx2, "mix_norm_bwd", below=(saved1[4], g1, 0.5))

    dh1, dw1_in, dw1_out = _ffn_backward(df1, saved1, w1_in, w1_out, "ffn1")
    dx0, dsh1, dsc1, dng1 = _rmsmod_bwd(dh1, x0, norm_g_full[0:1], sc1, dx1, "ffn1_norm_bwd")

    dlb = -jnp.concatenate([doml_f, doml_b], axis=0)
    dlb_raw = dlb * lb * (1.0 - lb)
    d_hgrn_lb = jnp.stack([dlb_raw, -dlb_raw], axis=1)
    d_qk = jnp.concatenate([jnp.sum(dqg, axis=0), jnp.sum(dkg, axis=0)], axis=0)
    dmods = jnp.concatenate([dsh1, dsc1, dg1, dsh2, dsc2, dg2, dsh3, dsc3, dg3], axis=0)
    packed = jnp.concatenate(
        [dmods, dng1, dng2, dng3, d_hgrn_lb.reshape(2, D), _pad_row(d_hnorm, D), _pad_row(d_qk, D),
         _pad_row(dsink[:, 0, 0], D), _pad_row(d_rel_bias, D), _pad_row(loss_mine, D)], axis=0)
    packed = jnp.pad(packed, ((0, 24 - packed.shape[0]), (0, 0)))
    packed_all, packed_sum = _allgather8(packed, "small_grads_allgather", reduce=True)
    dmods_all = packed_all[:, 0:N_MOD, :].reshape(8, N_MOD * D)
    g_b_ada = packed_sum[0:N_MOD].reshape(1, N_MOD * D)
    g_norm_full = packed_sum[9:12]
    g_norm_g = lax.dynamic_slice_in_dim(g_norm_full, my_chip * 256, 256, axis=1).reshape(1, 3, 256)
    g_hgrn_lb = lax.dynamic_slice_in_dim(packed_sum[12:14].reshape(2, 2, HG_WIDTH), my_chip * 128, 128, axis=2)
    g_hgrn_norm_g = packed_sum[14:15, :HG_WIDTH]
    g_qk_norm_g = packed_sum[15, :2 * ATT_HEAD_DIM].reshape(1, 2, ATT_HEAD_DIM)
    g_attn_sink = packed_sum[16:17, :ATT_Q_HEADS]
    g_rel_bias = packed_sum[17, :NUM_BUCKETS * ATT_Q_HEADS].reshape(NUM_BUCKETS, ATT_Q_HEADS)
    loss = packed_sum[18, 0]

    dm_mine = lax.dynamic_slice_in_dim(dmods_all, my_chip * n_ada, n_ada, axis=1)
    g_w_ada = _ada_wgrad(c_act_all.T, dm_mine, "ada_wgrad")[None]

    def by_chip_rows(g):
        return g.reshape(N_CHIPS, 2, g.shape[0] // (2 * N_CHIPS), g.shape[1])

    def by_chip_cols(g):
        return g.reshape(N_CHIPS, 2, g.shape[1] // 2, g.shape[2])

    wide = D_IN // N_CHIPS
    core_arr = jnp.reshape(place[2], (1,)).astype(jnp.int32)
    chip_arr = jnp.reshape(my_chip, (1,)).astype(jnp.int32)
    reduced = _reduce_scatter(
        [by_chip_cols(dw1_in), by_chip_rows(dw1_out), by_chip_cols(dw2_in), by_chip_rows(dw2_out),
         by_chip_cols(dwm_in.reshape(D, N_CHIPS, wide).transpose(1, 0, 2)), by_chip_rows(dwm_out)], core_arr, chip_arr, "grads")

    def big(w, g, m, v, name):
        d, nm, nv = _adamw(w[0], g[0], m[0], v[0], name)
        return d[None], nm[None], nv[None]

    def big_halves(w, g_pair, m, v, name):
        g, d, nm, nv = _adamw_halves(core_arr, w[0], g_pair[0], g_pair[1], m[0], v[0], name)
        return g[None], (d[None], nm[None], nv[None])

    g_w1_in, u_w1_in = big_halves(w_ffn1_in, reduced[0], m_w_ffn1_in, v_w_ffn1_in, "adamw_w_ffn1_in")
    g_w1_out, u_w1_out = big_halves(w_ffn1_out, reduced[1], m_w_ffn1_out, v_w_ffn1_out, "adamw_w_ffn1_out")
    g_w2_in, u_w2_in = big_halves(w_ffn2_in, reduced[2], m_w_ffn2_in, v_w_ffn2_in, "adamw_w_ffn2_in")
    g_w2_out, u_w2_out = big_halves(w_ffn2_out, reduced[3], m_w_ffn2_out, v_w_ffn2_out, "adamw_w_ffn2_out")
    g_wm_in, u_wm_in = big_halves(w_mix_in, reduced[4], m_w_mix_in, v_w_mix_in, "adamw_w_mix_in")
    g_wm_out, u_wm_out = big_halves(w_mix_out, reduced[5], m_w_mix_out, v_w_mix_out, "adamw_w_mix_out")

    smalls = [(b_ada, g_b_ada, m_b_ada, v_b_ada), (norm_g, g_norm_g, m_norm_g, v_norm_g), (hgrn_lb, g_hgrn_lb, m_hgrn_lb, v_hgrn_lb),
              (hgrn_norm_g, g_hgrn_norm_g, m_hgrn_norm_g, v_hgrn_norm_g), (qk_norm_g, g_qk_norm_g, m_qk_norm_g, v_qk_norm_g),
              (attn_sink, g_attn_sink, m_attn_sink, v_attn_sink), (rel_bias, g_rel_bias, m_rel_bias, v_rel_bias)]
    sizes = [t[0].size for t in smalls]
    total = sum(sizes)
    rows = -(-total // 128)
    rows = -(-rows // 8) * 8

    def pack(i):
        flat = jnp.concatenate([t[i].reshape(-1) for t in smalls])
        fill = 1.0 if i == 3 else 0.0
        return jnp.pad(flat, (0, rows * 128 - total), constant_values=fill).reshape(rows, 128)

    packed_out = _adamw(pack(0), pack(1), pack(2), pack(3), "adamw_small")

    def unpack(flat2d):
        flat = flat2d.reshape(-1)
        outs, off = [], 0
        for t, n in zip(smalls, sizes):
            outs.append(flat[off:off + n].reshape(t[0].shape))
            off += n
        return outs

    d_small, m_small, v_small = [unpack(t) for t in packed_out]

    upd = {
        "w_ada": big(w_ada, g_w_ada, m_w_ada, v_w_ada, "adamw_w_ada"),
        "w_ffn1_in": u_w1_in, "w_ffn1_out": u_w1_out, "w_ffn2_in": u_w2_in, "w_ffn2_out": u_w2_out,
        "w_mix_in": u_wm_in, "w_mix_out": u_wm_out,
    }
    small_names = ["b_ada", "norm_g", "hgrn_lb", "hgrn_norm_g", "qk_norm_g", "attn_sink", "rel_bias"]
    for i, nme in enumerate(small_names):
        upd[nme] = (d_small[i], m_small[i], v_small[i])
    grads = {
        "w_ada": g_w_ada, "b_ada": g_b_ada, "norm_g": g_norm_g, "w_ffn1_in": g_w1_in, "w_ffn1_out": g_w1_out,
        "w_ffn2_in": g_w2_in, "w_ffn2_out": g_w2_out, "w_mix_in": g_wm_in, "w_mix_out": g_wm_out, "hgrn_lb": g_hgrn_lb,
        "hgrn_norm_g": g_hgrn_norm_g, "qk_norm_g": g_qk_norm_g, "attn_sink": g_attn_sink, "rel_bias": g_rel_bias,
    }
    order = ["w_ada", "b_ada", "norm_g", "w_ffn1_in", "w_ffn1_out", "w_ffn2_in", "w_ffn2_out", "w_mix_in", "w_mix_out",
             "hgrn_lb", "hgrn_norm_g", "qk_norm_g", "attn_sink", "rel_bias"]
    return (loss, dx0[None], *[grads[k] for k in order], *[upd[k][0] for k in order], *[upd[k][1] for k in order],
            *[upd[k][2] for k in order])
```

```python
import functools
import math

import numpy as np
import jax
import jax.numpy as jnp
from jax import lax
from jax.experimental import pallas as pl
from jax.experimental.pallas import tpu as pltpu

F32, BF16 = jnp.float32, jnp.bfloat16

D_MODEL = 1024
D_FF = 2816
HG_HEADS, HG_DIM = 4, 128
HG_WIDTH = HG_HEADS * HG_DIM
ATT_Q_HEADS, ATT_KV_HEADS, ATT_HEAD_DIM = 8, 2, 64
ATT_GROUP = ATT_Q_HEADS // ATT_KV_HEADS
ATT_WIDTH = ATT_Q_HEADS * ATT_HEAD_DIM
KV_WIDTH = ATT_KV_HEADS * ATT_HEAD_DIM
WINDOW, BLOCK = 128, 128
NUM_BUCKETS, MAX_DISTANCE = 32, 128
N_MOD = 9
EPS = 1e-6
D_IN = 5 * HG_WIDTH + ATT_WIDTH + 2 * KV_WIDTH
ADAM_LR, ADAM_B1, ADAM_B2, ADAM_EPS, ADAM_WD, ADAM_STEP = 0.001, 0.9, 0.999, 1e-08, 0.01, 10

N_CHIPS = 4
FF_SHARD = 2 * D_FF // N_CHIPS
NEG = -1e30

VMEM_LIMIT_BYTES = 56 << 20
ROW_TILE = 512
HG_CHUNK = 16
HG_ROWS = 256

MESH = pl.DeviceIdType.MESH
ANY = pl.BlockSpec(memory_space=pl.ANY)


def _params(*sem):
    return pltpu.CompilerParams(dimension_semantics=sem, vmem_limit_bytes=VMEM_LIMIT_BYTES)


def _resident(shape, index_map):
    return pl.BlockSpec(shape, index_map, pipeline_mode=pl.Buffered(1))


def _dot(a, b, dims, precision=None):
    return lax.dot_general(a, b, (dims, ((), ())), precision=precision, preferred_element_type=F32)


def _nn(a, b, precision=None):
    return _dot(a, b, ((1,), (0,)), precision)


def _nt(a, b):
    return _dot(a, b, ((1,), (1,)))


def _tn(a, b):
    return _dot(a, b, ((0,), (0,)))


def _sigmoid(x):
    return jax.nn.sigmoid(x)


def _rmsmod_fwd(x, g, shift, scale, name):
    S, D = x.shape
    tr = min(ROW_TILE, S)

    def body(x_ref, g_ref, sh_ref, sc_ref, h_ref):
        xv = x_ref[...]
        rstd = lax.rsqrt(jnp.mean(xv * xv, axis=-1, keepdims=True) + EPS)
        y = xv * rstd * g_ref[...]
        h_ref[...] = (y * (1.0 + sc_ref[...]) + sh_ref[...]).astype(h_ref.dtype)

    row = pl.BlockSpec((tr, D), lambda i: (i, 0))
    vec = pl.BlockSpec((1, D), lambda i: (0, 0))
    return pl.pallas_call(
        body, name=name, grid=(S // tr,), in_specs=[row, vec, vec, vec], out_specs=row,
        out_shape=jax.ShapeDtypeStruct((S, D), BF16), compiler_params=_params("parallel"),
    )(x, g, shift, scale)


def _rmsmod_bwd(dh, x, g, scale, dx_res, name, below=None):
    S, D = x.shape
    tr = min(ROW_TILE, S)
    coef = below[2] if below else None

    def body(*refs):
        if below:
            dh_ref, x_ref, g_ref, sc_ref, dxr_ref, f_ref, gate_ref, dx_ref, dsh_ref, dsc_ref, dg_ref, df_ref, dgate_ref = refs
        else:
            dh_ref, x_ref, g_ref, sc_ref, dxr_ref, dx_ref, dsh_ref, dsc_ref, dg_ref = refs

        @pl.when(pl.program_id(0) == 0)
        def _():
            dsh_ref[...] = jnp.zeros_like(dsh_ref)
            dsc_ref[...] = jnp.zeros_like(dsc_ref)
            dg_ref[...] = jnp.zeros_like(dg_ref)
            if below:
                dgate_ref[...] = jnp.zeros_like(dgate_ref)

        dhv, xv, gv = dh_ref[...], x_ref[...], g_ref[...]
        one_sc = 1.0 + sc_ref[...]
        rstd = lax.rsqrt(jnp.mean(xv * xv, axis=-1, keepdims=True) + EPS)
        n = xv * rstd
        dsh_ref[...] += jnp.sum(dhv, axis=0, keepdims=True)
        dsc_ref[...] += jnp.sum(dhv * n, axis=0, keepdims=True) * gv
        dg_ref[...] += jnp.sum(dhv * n, axis=0, keepdims=True) * one_sc
        dn = dhv * (gv * one_sc)
        dx = dxr_ref[...] + rstd * (dn - n * jnp.mean(dn * n, axis=-1, keepdims=True))
        dx_ref[...] = dx
        if below:
            df_ref[...] = (coef * gate_ref[...] * dx).astype(df_ref.dtype)
            dgate_ref[...] += coef * jnp.sum(dx * f_ref[...].astype(F32), axis=0, keepdims=True)

    row = pl.BlockSpec((tr, D), lambda i: (i, 0))
    vec = pl.BlockSpec((1, D), lambda i: (0, 0))
    vshape = jax.ShapeDtypeStruct((1, D), F32)
    ins, in_specs = [dh, x, g, scale, dx_res], [row, row, vec, vec, row]
    outs, out_specs = [jax.ShapeDtypeStruct((S, D), F32), vshape, vshape, vshape], [row, vec, vec, vec]
    if below:
        ins += [below[0], below[1]]
        in_specs += [row, vec]
        outs += [jax.ShapeDtypeStruct((S, D), BF16), vshape]
        out_specs += [row, vec]
    return pl.pallas_call(
        body, name=name, grid=(S // tr,), in_specs=in_specs, out_specs=out_specs, out_shape=outs,
        compiler_params=_params("arbitrary"),
    )(*ins)


def _loss_bwd(y, target, f, gate, coef, name):
    S, D = y.shape
    tr = min(ROW_TILE, S)

    def body(y_ref, t_ref, f_ref, gate_ref, dy_ref, df_ref, dgate_ref, sq_ref):
        @pl.when(pl.program_id(0) == 0)
        def _():
            dgate_ref[...] = jnp.zeros_like(dgate_ref)
            sq_ref[...] = jnp.zeros_like(sq_ref)

        err = y_ref[...] - t_ref[...]
        sq_ref[...] += jnp.sum(err * err, axis=0, keepdims=True)
        dy = err * (1.0 / D)
        dy_ref[...] = dy
        df_ref[...] = (coef * gate_ref[...] * dy).astype(df_ref.dtype)
        dgate_ref[...] += coef * jnp.sum(dy * f_ref[...].astype(F32), axis=0, keepdims=True)

    row = pl.BlockSpec((tr, D), lambda i: (i, 0))
    vec = pl.BlockSpec((1, D), lambda i: (0, 0))
    vshape = jax.ShapeDtypeStruct((1, D), F32)
    return pl.pallas_call(
        body, name=name, grid=(S // tr,), in_specs=[row, row, row, vec], out_specs=[row, row, vec, vec],
        out_shape=[jax.ShapeDtypeStruct((S, D), F32), jax.ShapeDtypeStruct((S, D), BF16), vshape, vshape],
        compiler_params=_params("arbitrary"),
    )(y, target, f, gate)


def _ffn_in_fwd(h, w4, name):
    S, D = h.shape
    tm = min(ROW_TILE, S)
    n = w4.shape[2]

    def body(h_ref, wg_ref, wu_ref, zg_ref, zu_ref, a_ref):
        hv = h_ref[...]
        zg = _nn(hv, wg_ref[...])
        zu = _nn(hv, wu_ref[...])
        zg_ref[...] = zg.astype(zg_ref.dtype)
        zu_ref[...] = zu.astype(zu_ref.dtype)
        a_ref[...] = (zg * _sigmoid(zg) * zu).astype(a_ref.dtype)

    out = pl.BlockSpec((tm, n), lambda j, m: (m, j))
    oshape = jax.ShapeDtypeStruct((S, 2 * n), BF16)
    return pl.pallas_call(
        body, name=name, grid=(2, S // tm),
        in_specs=[pl.BlockSpec((tm, D), lambda j, m: (m, 0)),
                  pl.BlockSpec((None, D, n), lambda j, m: (j, 0, 0)),
                  pl.BlockSpec((None, D, n), lambda j, m: (j + 2, 0, 0))],
        out_specs=[out, out, out], out_shape=[oshape, oshape, oshape],
        compiler_params=_params("parallel", "parallel"),
    )(h, w4, w4)


def _proj_out_fwd(lhs, w, x, gate, coef, name):
    S, D = x.shape
    tm = min(ROW_TILE, S)
    ks = [a.shape[1] for a in lhs]

    def body(*refs):
        lhs_refs = refs[:len(lhs)]
        w_ref, x_ref, gate_ref, xn_ref, f_ref = refs[len(lhs):]
        acc, off = None, 0
        for a_ref, k in zip(lhs_refs, ks):
            part = _nn(a_ref[...], w_ref[off:off + k, :])
            acc = part if acc is None else acc + part
            off += k
        f_ref[...] = acc.astype(f_ref.dtype)
        xn_ref[...] = x_ref[...] + coef * gate_ref[...] * acc

    row = pl.BlockSpec((tm, D), lambda m: (m, 0))
    return pl.pallas_call(
        body, name=name, grid=(S // tm,),
        in_specs=[pl.BlockSpec((tm, k), lambda m: (m, 0)) for k in ks]
        + [_resident(w.shape, lambda m: (0, 0)), row, pl.BlockSpec((1, D), lambda m: (0, 0))],
        out_specs=[row, row],
        out_shape=[jax.ShapeDtypeStruct((S, D), F32), jax.ShapeDtypeStruct((S, D), BF16)],
        compiler_params=_params("parallel"),
    )(*lhs, w, x, gate)


def _matmul_nn(a, w, out_dtype, tm, name):
    S, K = a.shape
    N = w.shape[1]
    tm = min(tm, S)

    def body(a_ref, w_ref, o_ref):
        o_ref[...] = _nn(a_ref[...], w_ref[...]).astype(o_ref.dtype)

    return pl.pallas_call(
        body, name=name, grid=(S // tm,),
        in_specs=[pl.BlockSpec((tm, K), lambda m: (m, 0)), _resident((K, N), lambda m: (0, 0))],
        out_specs=pl.BlockSpec((tm, N), lambda m: (m, 0)), out_shape=jax.ShapeDtypeStruct((S, N), out_dtype),
        compiler_params=_params("parallel"),
    )(a, w)


def _dact_bwd(df, w_out, zg, zu, name):
    S, D = df.shape
    tm = min(ROW_TILE, S)
    n = w_out.shape[0] // 2

    def body(df_ref, w_ref, zg_ref, zu_ref, dzg_ref, dzu_ref):
        da = _nt(df_ref[...], w_ref[...])
        zg_v, zu_v = zg_ref[...].astype(F32), zu_ref[...].astype(F32)
        s = _sigmoid(zg_v)
        dzu_ref[...] = (da * zg_v * s).astype(dzu_ref.dtype)
        dzg_ref[...] = (da * zu_v * (s * (1.0 + zg_v * (1.0 - s)))).astype(dzg_ref.dtype)

    blk = pl.BlockSpec((tm, n), lambda j, m: (m, j))
    oshape = jax.ShapeDtypeStruct((S, 2 * n), BF16)
    return pl.pallas_call(
        body, name=name, grid=(2, S // tm),
        in_specs=[pl.BlockSpec((tm, D), lambda j, m: (m, 0)), pl.BlockSpec((n, D), lambda j, m: (j, 0)), blk, blk],
        out_specs=[blk, blk], out_shape=[oshape, oshape], compiler_params=_params("parallel", "parallel"),
    )(df, w_out, zg, zu)


def _ffn_in_dgrad(dzg, dzu, w4, name):
    S = dzg.shape[0]
    D, n = w4.shape[1], w4.shape[2]
    tm = min(ROW_TILE, S)

    def body(dzg_ref, dzu_ref, w_ref, dh_ref):
        acc = _nt(dzg_ref[:, 0:n], w_ref[0])
        acc += _nt(dzg_ref[:, n:2 * n], w_ref[1])
        acc += _nt(dzu_ref[:, 0:n], w_ref[2])
        acc += _nt(dzu_ref[:, n:2 * n], w_ref[3])
        dh_ref[...] = acc

    blk = pl.BlockSpec((tm, 2 * n), lambda m: (m, 0))
    return pl.pallas_call(
        body, name=name, grid=(S // tm,),
        in_specs=[blk, blk, _resident(w4.shape, lambda m: (0, 0, 0))],
        out_specs=pl.BlockSpec((tm, D), lambda m: (m, 0)), out_shape=jax.ShapeDtypeStruct((S, D), F32),
        compiler_params=_params("parallel"),
    )(dzg, dzu, w4)


def _matmul_nt(a, w, tm, name):
    S, K = a.shape
    N = w.shape[0]
    tm = min(tm, S)

    def body(a_ref, w_ref, o_ref):
        o_ref[...] = _nt(a_ref[...], w_ref[...])

    return pl.pallas_call(
        body, name=name, grid=(S // tm,),
        in_specs=[pl.BlockSpec((tm, K), lambda m: (m, 0)), _resident((N, K), lambda m: (0, 0))],
        out_specs=pl.BlockSpec((tm, N), lambda m: (m, 0)), out_shape=jax.ShapeDtypeStruct((S, N), F32),
        compiler_params=_params("parallel"),
    )(a, w)


def _wgrad(a, g, tn, name):
    S, Ka = a.shape
    N = g.shape[1]
    ts = min(ROW_TILE, S)

    def body(a_ref, g_ref, o_ref):
        @pl.when(pl.program_id(1) == 0)
        def _():
            o_ref[...] = jnp.zeros_like(o_ref)

        o_ref[...] += _tn(a_ref[...], g_ref[...])

    return pl.pallas_call(
        body, name=name, grid=(N // tn, S // ts),
        in_specs=[pl.BlockSpec((ts, Ka), lambda j, s: (s, 0)), pl.BlockSpec((ts, tn), lambda j, s: (s, j))],
        out_specs=pl.BlockSpec((None, Ka, tn), lambda j, s: (j, 0, 0)),
        out_shape=jax.ShapeDtypeStruct((N // tn, Ka, tn), F32), compiler_params=_params("parallel", "arbitrary"),
    )(a, g)


def _hgrn_chunk_common(qr, fr, oml, tri, last):
    k = oml * _sigmoid(-fr)
    g = jnp.log1p(-k)
    q = qr * _sigmoid(qr)
    G = _nn(tri, g, precision=lax.Precision.HIGHEST)
    Gl = G[last:last + 1]
    return q, k, G, Gl


def _hgrn_consts(reverse):
    C = HG_CHUNK
    r = lax.broadcasted_iota(jnp.int32, (C, C), 0)
    cc = lax.broadcasted_iota(jnp.int32, (C, C), 1)
    tri = ((cc >= r) if reverse else (cc <= r)).astype(F32)
    tri_t = ((cc <= r) if reverse else (cc >= r)).astype(F32)
    rid = lax.broadcasted_iota(jnp.int32, (C, HG_WIDTH), 0)
    return tri, tri_t, rid, (0 if reverse else C - 1)


def _head_slices():
    return [slice(h * HG_DIM, (h + 1) * HG_DIM) for h in range(HG_HEADS)]


def _per_head_lane_sum(x):
    C = x.shape[0]
    return jnp.concatenate(
        [jnp.broadcast_to(jnp.sum(x[:, sl], axis=-1, keepdims=True), (C, HG_DIM)) for sl in _head_slices()], axis=1)


HG_TILE = 8


def _pair_tiles(s, reverse):
    blk, r = divmod(s, HG_TILE)
    n_tiles = HG_CHUNK // HG_TILE
    others = range(0, blk) if reverse else range(blk + 1, n_tiles)
    return [(blk, r)] + [(t, None) for t in others]


def _pair_decay(G, s, tile, r, rid8, reverse, keys=False):
    rs = slice(tile * HG_TILE, (tile + 1) * HG_TILE)
    d = (G[s:s + 1] - G[rs]) if keys else (G[rs] - G[s:s + 1])
    if r is not None:
        d = jnp.where((rid8 <= r) if reverse else (rid8 >= r), d, NEG)
    return rs, jnp.exp(d)


def _hgrn_fwd(z, lb, direction, name):
    S = z.shape[0]
    C, DK, W = HG_CHUNK, HG_DIM, HG_WIDTH
    tb = min(HG_ROWS, S)
    n_t, n_c = S // tb, tb // C
    reverse = direction == 1
    tmap = (lambda i: n_t - 1 - i) if reverse else (lambda i: i)

    def body(q_ref, f_ref, v_ref, lb_ref, o_ref, st_out_ref, st_ref):
        @pl.when(pl.program_id(0) == 0)
        def _():
            st_ref[...] = jnp.zeros_like(st_ref)

        oml = 1.0 - lb_ref[...]
        tri, _, _, last = _hgrn_consts(reverse)
        rid8 = lax.broadcasted_iota(jnp.int32, (HG_TILE, W), 0)

        def chunk(ci, carry):
            cidx = (n_c - 1 - ci) if reverse else ci
            rows = pl.ds(pl.multiple_of(cidx * C, C), C)
            v = v_ref[rows, :]
            q, k, G, Gl = _hgrn_chunk_common(q_ref[rows, :], f_ref[rows, :], oml, tri, last)
            qd = (q * jnp.exp(G)).astype(BF16)
            kd = (k * jnp.exp(Gl - G)).astype(BF16)
            e_gl = jnp.exp(Gl)
            v_b = v.astype(BF16)
            inter = []
            for h, sl in enumerate(_head_slices()):
                st0 = st_ref[h]
                st_out_ref[h, cidx] = st0
                inter.append(_nt(qd[:, sl], st0.astype(BF16)))
                st_ref[h] = st0 * e_gl[:, sl] + _tn(v_b[:, sl], kd[:, sl])
            o = jnp.concatenate(inter, axis=1)
            o_t = [o[t * HG_TILE:(t + 1) * HG_TILE] for t in range(C // HG_TILE)]
            for s in range(C):
                k_s, v_s = k[s:s + 1], v[s:s + 1]
                for tile, r in _pair_tiles(s, reverse):
                    rs, e_s = _pair_decay(G, s, tile, r, rid8, reverse)
                    o_t[tile] = o_t[tile] + _per_head_lane_sum(q[rs] * k_s * e_s) * v_s
            o_ref[rows, :] = jnp.concatenate(o_t, axis=0)
            return carry

        lax.fori_loop(0, n_c, chunk, 0, unroll=2)

    def sec(j):
        return pl.BlockSpec((tb, W), lambda i: (tmap(i), j))

    return pl.pallas_call(
        body, name=name, grid=(n_t,),
        in_specs=[sec(0), sec(1 + direction), sec(3), pl.BlockSpec((1, W), lambda i: (0, 0))],
        out_specs=[sec(0), pl.BlockSpec((HG_HEADS, n_c, DK, DK), lambda i: (0, tmap(i), 0, 0))],
        out_shape=[jax.ShapeDtypeStruct((S, W), F32), jax.ShapeDtypeStruct((HG_HEADS, S // C, DK, DK), F32)],
        scratch_shapes=[pltpu.VMEM((HG_HEADS, DK, DK), F32)],
        compiler_params=_params("arbitrary"),
    )(z, z, z, lb)


def _hgrn_bwd(z, lb, do, states, direction, name, acc=None):
    S = z.shape[0]
    C, DK, W = HG_CHUNK, HG_DIM, HG_WIDTH
    tb = min(HG_ROWS, S)
    n_t, n_c = S // tb, tb // C
    reverse = direction == 1
    tmap = (lambda i: i) if reverse else (lambda i: n_t - 1 - i)

    def body(*refs):
        if acc:
            q_ref, f_ref, v_ref, lb_ref, do_ref, st_in_ref, dqa_ref, dva_ref, dq_ref, df_ref, dv_ref, doml_ref, dst_ref = refs
        else:
            q_ref, f_ref, v_ref, lb_ref, do_ref, st_in_ref, dq_ref, df_ref, dv_ref, doml_ref, dst_ref = refs

        @pl.when(pl.program_id(0) == 0)
        def _():
            dst_ref[...] = jnp.zeros_like(dst_ref)
            doml_ref[...] = jnp.zeros_like(doml_ref)

        oml = 1.0 - lb_ref[...]
        tri, tri_t, rid, last = _hgrn_consts(reverse)
        rid8 = lax.broadcasted_iota(jnp.int32, (HG_TILE, W), 0)

        def chunk(ci, carry):
            cidx = ci if reverse else (n_c - 1 - ci)
            rows = pl.ds(pl.multiple_of(cidx * C, C), C)
            qr, fr, v, dov = q_ref[rows, :], f_ref[rows, :], v_ref[rows, :], do_ref[rows, :]
            q, k, G, Gl = _hgrn_chunk_common(qr, fr, oml, tri, last)
            e_g, e_gl, e_kd = jnp.exp(G), jnp.exp(Gl), jnp.exp(Gl - G)
            qd, kd = q * e_g, k * e_kd
            do_b, v_b, qd_b, kd_b = dov.astype(BF16), v.astype(BF16), qd.astype(BF16), kd.astype(BF16)
            dqd, dkd, dv, state_dot = [], [], [], []
            for h, sl in enumerate(_head_slices()):
                st0, dst1 = st_in_ref[h, cidx], dst_ref[h]
                dst1_b = dst1.astype(BF16)
                dqd.append(_nn(do_b[:, sl], st0.astype(BF16)))
                dkd.append(_nn(v_b[:, sl], dst1_b))
                dv.append(_nt(kd_b[:, sl], dst1_b))
                state_dot.append(jnp.sum(st0 * dst1, axis=0, keepdims=True))
                dst_ref[h] = dst1 * e_gl[:, sl] + _tn(do_b[:, sl], qd_b[:, sl])
            dqd, dkd, dv = [jnp.concatenate(t, axis=1) for t in (dqd, dkd, dv)]
            d_gl = e_gl * jnp.concatenate(state_dot, axis=1) + jnp.sum(dkd * kd, axis=0, keepdims=True)
            dq, dk = dqd * e_g, dkd * e_kd
            n_tiles = C // HG_TILE
            dq_t, dk_t, dv_t = [[x[t * HG_TILE:(t + 1) * HG_TILE] for t in range(n_tiles)] for x in (dq, dk, dv)]
            for s in range(C):
                k_s, v_s = k[s:s + 1], v[s:s + 1]
                for tile, r in _pair_tiles(s, reverse):
                    rs, e_s = _pair_decay(G, s, tile, r, rid8, reverse)
                    dq_t[tile] = dq_t[tile] + _per_head_lane_sum(dov[rs] * v_s) * e_s * k_s
            for t in range(C):
                q_t, do_t = q[t:t + 1], dov[t:t + 1]
                for tile, r in _pair_tiles(t, not reverse):
                    rs, x_t = _pair_decay(G, t, tile, r, rid8, not reverse, keys=True)
                    dv_t[tile] = dv_t[tile] + _per_head_lane_sum(k[rs] * q_t * x_t) * do_t
                    dk_t[tile] = dk_t[tile] + _per_head_lane_sum(v[rs] * do_t) * x_t * q_t
            dq, dk, dv = [jnp.concatenate(x, axis=0) for x in (dq_t, dk_t, dv_t)]
            d_big_g = dq * q - dk * k + jnp.where(rid == last, d_gl, 0.0)
            dg = _nn(tri_t, d_big_g, precision=lax.Precision.HIGHEST)
            dk_all = dk - dg / (1.0 - k)
            sig_nf = _sigmoid(-fr)
            df_ref[rows, :] = -dk_all * k * (1.0 - sig_nf)
            doml_ref[...] += jnp.sum(dk_all * sig_nf, axis=0, keepdims=True)
            sq = _sigmoid(qr)
            dqr = dq * (sq * (1.0 + qr * (1.0 - sq)))
            if acc:
                dqr = dqr + dqa_ref[rows, :]
                dv = dv + dva_ref[rows, :]
            dq_ref[rows, :] = dqr
            dv_ref[rows, :] = dv
            return carry

        lax.fori_loop(0, n_c, chunk, 0, unroll=2)

    def sec(j):
        return pl.BlockSpec((tb, W), lambda i: (tmap(i), j))

    vec = pl.BlockSpec((1, W), lambda i: (0, 0))
    ins = [z, z, z, lb, do, states]
    in_specs = [sec(0), sec(1 + direction), sec(3), vec, sec(0),
                pl.BlockSpec((HG_HEADS, n_c, DK, DK), lambda i: (0, tmap(i), 0, 0))]
    if acc:
        ins += list(acc)
        in_specs += [sec(0), sec(0)]
    full = jax.ShapeDtypeStruct((S, W), F32)
    return pl.pallas_call(
        body, name=name, grid=(n_t,), in_specs=in_specs,
        out_specs=[sec(0), sec(0), sec(0), vec],
        out_shape=[full, full, full, jax.ShapeDtypeStruct((1, W), F32)],
        scratch_shapes=[pltpu.VMEM((HG_HEADS, DK, DK), F32)],
        compiler_params=_params("arbitrary"),
    )(*ins)


def _hgrn_post_fwd(o_f, o_b, z, norm_g, name):
    S = z.shape[0]
    tr = min(ROW_TILE, S)

    def body(of_ref, ob_ref, gr_ref, ng_ref, y_ref):
        o = of_ref[...] + ob_ref[...]
        gr = gr_ref[...]
        gate = gr * _sigmoid(gr)
        ng = ng_ref[...]
        for h in range(HG_HEADS):
            sl = slice(h * HG_DIM, (h + 1) * HG_DIM)
            oh = o[:, sl]
            rstd = lax.rsqrt(jnp.mean(oh * oh, axis=-1, keepdims=True) + EPS)
            y_ref[:, sl] = (oh * rstd * ng[:, sl] * gate[:, sl]).astype(y_ref.dtype)

    row = pl.BlockSpec((tr, HG_WIDTH), lambda i: (i, 0))
    return pl.pallas_call(
        body, name=name, grid=(S // tr,),
        in_specs=[row, row, pl.BlockSpec((tr, HG_WIDTH), lambda i: (i, 4)), pl.BlockSpec((1, HG_WIDTH), lambda i: (0, 0))],
        out_specs=row, out_shape=jax.ShapeDtypeStruct((S, HG_WIDTH), BF16), compiler_params=_params("parallel"),
    )(o_f, o_b, z, norm_g)


def _hgrn_post_bwd(dy, o_f, o_b, z, norm_g, name):
    S = z.shape[0]
    tr = min(ROW_TILE, S)

    def body(dy_ref, of_ref, ob_ref, gr_ref, ng_ref, do_ref, dgr_ref, dng_ref):
        @pl.when(pl.program_id(0) == 0)
        def _():
            dng_ref[...] = jnp.zeros_like(dng_ref)

        o = of_ref[...] + ob_ref[...]
        gr, ng, dyv = gr_ref[...], ng_ref[...], dy_ref[...]
        sg = _sigmoid(gr)
        for h in range(HG_HEADS):
            sl = slice(h * HG_DIM, (h + 1) * HG_DIM)
            oh, dyh, grh, sgh, ngh = o[:, sl], dyv[:, sl], gr[:, sl], sg[:, sl], ng[:, sl]
            rstd = lax.rsqrt(jnp.mean(oh * oh, axis=-1, keepdims=True) + EPS)
            on = oh * rstd
            du = dyh * (grh * sgh)
            dgr_ref[:, sl] = dyh * (on * ngh) * (sgh * (1.0 + grh * (1.0 - sgh)))
            dng_ref[:, sl] += jnp.sum(du * on, axis=0, keepdims=True)
            don = du * ngh
            do_ref[:, sl] = rstd * (don - on * jnp.mean(don * on, axis=-1, keepdims=True))

    row = pl.BlockSpec((tr, HG_WIDTH), lambda i: (i, 0))
    vec = pl.BlockSpec((1, HG_WIDTH), lambda i: (0, 0))
    full = jax.ShapeDtypeStruct((S, HG_WIDTH), F32)
    return pl.pallas_call(
        body, name=name, grid=(S // tr,),
        in_specs=[row, row, row, pl.BlockSpec((tr, HG_WIDTH), lambda i: (i, 4)), vec],
        out_specs=[row, row, vec], out_shape=[full, full, jax.ShapeDtypeStruct((1, HG_WIDTH), F32)],
        compiler_params=_params("arbitrary"),
    )(dy, o_f, o_b, z, norm_g)


def _t5_bucket_table():
    rel = (np.arange(3 * BLOCK)[None, :] - BLOCK) - np.arange(BLOCK)[:, None]
    nb = NUM_BUCKETS // 2
    max_exact = nb // 2
    ret = (rel > 0).astype(np.int32) * nb
    n = np.abs(rel)
    ratio = np.log(np.maximum(n, 1).astype(np.float32) / np.float32(max_exact)) / np.float32(math.log(MAX_DISTANCE / max_exact))
    large = max_exact + (ratio.astype(np.float32) * np.float32(nb - max_exact)).astype(np.int32)
    large = np.minimum(large, nb - 1)
    bucket = ret + np.where(n < max_exact, n, large)
    return bucket.astype(np.int32), (n <= WINDOW)


def _bias_table(rel_bias, name):
    bucket, in_band = _t5_bucket_table()
    idx = jnp.asarray(np.where(in_band, bucket, -1))

    def body(rb_ref, idx_ref, o_ref):
        h = pl.program_id(0)
        iv = idx_ref[...]
        acc = jnp.where(iv < 0, NEG, 0.0).astype(F32)
        for b in range(NUM_BUCKETS):
            acc = acc + jnp.where(iv == b, rb_ref[b, h], 0.0)
        o_ref[...] = acc

    return pl.pallas_call(
        body, name=name, grid=(ATT_Q_HEADS,),
        in_specs=[pl.BlockSpec(memory_space=pltpu.SMEM), pl.BlockSpec((BLOCK, 3 * BLOCK), lambda h: (0, 0))],
        out_specs=pl.BlockSpec((None, BLOCK, 3 * BLOCK), lambda h: (h, 0, 0)),
        out_shape=jax.ShapeDtypeStruct((ATT_Q_HEADS, BLOCK, 3 * BLOCK), F32), compiler_params=_params("parallel"),
    )(rel_bias, idx)


def _bias_grad(ds_sum, name):
    bucket, in_band = _t5_bucket_table()
    idx = jnp.asarray(np.where(in_band, bucket, -1))

    def body(ds_ref, idx_ref, o_ref):
        iv, ds = idx_ref[...], ds_ref[...]
        for b in range(NUM_BUCKETS):
            part = jnp.sum(jnp.where(iv == b, ds, 0.0), axis=0, keepdims=True)
            o_ref[b:b + 1, :] = part[:, 0:BLOCK] + part[:, BLOCK:2 * BLOCK] + part[:, 2 * BLOCK:3 * BLOCK]

    return pl.pallas_call(
        body, name=name, grid=(ATT_Q_HEADS,),
        in_specs=[pl.BlockSpec((None, BLOCK, 3 * BLOCK), lambda h: (h, 0, 0)), pl.BlockSpec((BLOCK, 3 * BLOCK), lambda h: (0, 0))],
        out_specs=pl.BlockSpec((None, NUM_BUCKETS, BLOCK), lambda h: (h, 0, 0)),
        out_shape=jax.ShapeDtypeStruct((ATT_Q_HEADS, NUM_BUCKETS, BLOCK), F32), compiler_params=_params("parallel"),
    )(ds_sum, idx)


def _attn_specs(nb):
    G, dh = ATT_GROUP, ATT_HEAD_DIM
    qspec = pl.BlockSpec((G, BLOCK, dh), lambda j, n: (j, n, 0))

    def kv(shift):
        return pl.BlockSpec((None, BLOCK, dh), lambda j, n: (j, jnp.clip(n + shift, 0, nb - 1), 0))

    gain = pl.BlockSpec((1, dh), lambda j, n: (0, 0))
    sink = pl.BlockSpec((G, 1, BLOCK), lambda j, n: (j, 0, 0))
    bias = pl.BlockSpec((G, BLOCK, 3 * BLOCK), lambda j, n: (j, 0, 0))
    return qspec, kv, gain, sink, bias


def _attn_probs(qh, kn, bias_h, sink_h, edge_ok):
    s = _nt(qh.astype(BF16), kn.astype(BF16)) * (1.0 / math.sqrt(ATT_HEAD_DIM)) + bias_h
    s = jnp.where(edge_ok, s, NEG)
    m = jnp.maximum(jnp.max(s, axis=-1, keepdims=True), sink_h)
    p = jnp.exp(s - m)
    e_sink = jnp.exp(sink_h - m)
    inv = 1.0 / (jnp.sum(p, axis=-1, keepdims=True) + e_sink)
    return p * inv, e_sink * inv


def _rms_rows(x):
    rstd = lax.rsqrt(jnp.mean(x * x, axis=-1, keepdims=True) + EPS)
    return x * rstd, rstd


def _edge_ok(n, nb):
    colid = lax.broadcasted_iota(jnp.int32, (ATT_GROUP * BLOCK, 3 * BLOCK), 1)
    return jnp.logical_and(jnp.logical_or(colid >= BLOCK, n > 0), jnp.logical_or(colid < 2 * BLOCK, n < nb - 1))


def _sink_column(sink_ref):
    return jnp.concatenate([jnp.broadcast_to(sink_ref[g][:, 0:1], (BLOCK, 1)) for g in range(ATT_GROUP)], axis=0)


def _attn_fwd(q, k, v, q_g, k_g, sink, bias, name):
    S = q.shape[1]
    nb = S // BLOCK
    G, dh = ATT_GROUP, ATT_HEAD_DIM
    qspec, kv, gain, sink_spec, bias_spec = _attn_specs(nb)

    def body(q_ref, k0, k1, k2, v0, v1, v2, qg_ref, kg_ref, sink_ref, bias_ref, o_ref):
        n = pl.program_id(1)
        kcat = jnp.concatenate([k0[...], k1[...], k2[...]], axis=0)
        vcat = jnp.concatenate([v0[...], v1[...], v2[...]], axis=0).astype(BF16)
        kn = _rms_rows(kcat)[0] * kg_ref[...]
        qn = _rms_rows(q_ref[...].reshape(G * BLOCK, dh))[0] * qg_ref[...]
        p, _ = _attn_probs(qn, kn, bias_ref[...].reshape(G * BLOCK, 3 * BLOCK), _sink_column(sink_ref), _edge_ok(n, nb))
        o_ref[...] = _nn(p.astype(BF16), vcat).reshape(G, BLOCK, dh)

    return pl.pallas_call(
        body, name=name, grid=(ATT_KV_HEADS, nb),
        in_specs=[qspec, kv(-1), kv(0), kv(1), kv(-1), kv(0), kv(1), gain, gain, sink_spec, bias_spec],
        out_specs=qspec, out_shape=jax.ShapeDtypeStruct(q.shape, F32), compiler_params=_params("parallel", "parallel"),
    )(q, k, k, k, v, v, v, q_g, k_g, sink, bias)


def _attn_bwd(q, k, v, q_g, k_g, sink, bias, do, name):
    S = q.shape[1]
    nb = S // BLOCK
    G, dh = ATT_GROUP, ATT_HEAD_DIM
    scale = 1.0 / math.sqrt(dh)
    qspec, kv, gain, sink_spec, bias_spec = _attn_specs(nb)

    def body(q_ref, k0, k1, k2, v0, v1, v2, qg_ref, kg_ref, sink_ref, bias_ref, do_ref,
             dq_ref, dkw_ref, dvw_ref, ds_ref, dsink_ref, dqg_ref):
        n = pl.program_id(1)

        @pl.when(n == 0)
        def _():
            ds_ref[...] = jnp.zeros_like(ds_ref)
            dsink_ref[...] = jnp.zeros_like(dsink_ref)
            dqg_ref[...] = jnp.zeros_like(dqg_ref)

        kcat = jnp.concatenate([k0[...], k1[...], k2[...]], axis=0)
        vcat = jnp.concatenate([v0[...], v1[...], v2[...]], axis=0).astype(BF16)
        kn = _rms_rows(kcat)[0] * kg_ref[...]
        qg = qg_ref[...]
        qhat, rstd = _rms_rows(q_ref[...].reshape(G * BLOCK, dh))
        qn = qhat * qg
        p, p_sink = _attn_probs(qn, kn, bias_ref[...].reshape(G * BLOCK, 3 * BLOCK), _sink_column(sink_ref), _edge_ok(n, nb))
        do_b = do_ref[...].reshape(G * BLOCK, dh).astype(BF16)
        dp = _nt(do_b, vcat)
        delta = jnp.sum(p * dp, axis=-1, keepdims=True)
        ds = p * (dp - delta)
        ds_ref[...] += ds.reshape(G, BLOCK, 3 * BLOCK)
        sink_term = p_sink * delta
        for g in range(G):
            dsink_ref[g] += jnp.zeros((1, BLOCK), F32) - jnp.sum(sink_term[g * BLOCK:(g + 1) * BLOCK], axis=0, keepdims=True)
        ds_b = ds.astype(BF16)
        dvw_ref[...] = _tn(p.astype(BF16), do_b)
        dkw_ref[...] = _tn(ds_b, qn.astype(BF16)) * scale
        dqn = _nn(ds_b, kn.astype(BF16)) * scale
        dqg_ref[...] += jnp.sum(dqn * qhat, axis=0, keepdims=True)
        dqh = dqn * qg
        dq_ref[...] = (rstd * (dqh - qhat * jnp.mean(dqh * qhat, axis=-1, keepdims=True))).reshape(G, BLOCK, dh)

    win = pl.BlockSpec((None, None, 3 * BLOCK, dh), lambda j, n: (j, n, 0, 0))
    wshape = jax.ShapeDtypeStruct((ATT_KV_HEADS, nb, 3 * BLOCK, dh), F32)
    return pl.pallas_call(
        body, name=name, grid=(ATT_KV_HEADS, nb),
        in_specs=[qspec, kv(-1), kv(0), kv(1), kv(-1), kv(0), kv(1), gain, gain, sink_spec, bias_spec, qspec],
        out_specs=[qspec, win, win, bias_spec, sink_spec, pl.BlockSpec((None, 1, dh), lambda j, n: (j, 0, 0))],
        out_shape=[jax.ShapeDtypeStruct(q.shape, F32), wshape, wshape,
                   jax.ShapeDtypeStruct((ATT_Q_HEADS, BLOCK, 3 * BLOCK), F32),
                   jax.ShapeDtypeStruct((ATT_Q_HEADS, 1, BLOCK), F32),
                   jax.ShapeDtypeStruct((ATT_KV_HEADS, 1, dh), F32)],
        compiler_params=_params("parallel", "arbitrary"),
    )(q, k, k, k, v, v, v, q_g, k_g, sink, bias, do)


def _attn_kv_reduce(dkw, dvw, k, k_g, name):
    S = k.shape[1]
    nb = S // BLOCK
    dh = ATT_HEAD_DIM
    kb = min(8, nb)
    steps = nb // kb

    def body(a_lo, a, a_hi, b_lo, b, b_hi, k_ref, kg_ref, dk_ref, dv_ref, dkg_ref):
        n = pl.program_id(1)

        @pl.when(n == 0)
        def _():
            dkg_ref[...] = jnp.zeros_like(dkg_ref)

        lo = jnp.where(n > 0, 1.0, 0.0)
        hi = jnp.where(n < steps - 1, 1.0, 0.0)

        def overlap_add(w, w_lo, w_hi, i):
            before = lo * w_lo[...] if i == 0 else w[i - 1, 2 * BLOCK:3 * BLOCK, :]
            after = hi * w_hi[...] if i == kb - 1 else w[i + 1, 0:BLOCK, :]
            return w[i, BLOCK:2 * BLOCK, :] + before + after

        dkg = jnp.zeros((1, dh), F32)
        for i in range(kb):
            rows = slice(i * BLOCK, (i + 1) * BLOCK)
            dkn = overlap_add(a, a_lo, a_hi, i)
            dv_ref[rows, :] = overlap_add(b, b_lo, b_hi, i)
            khat, rstd = _rms_rows(k_ref[rows, :])
            dkg = dkg + jnp.sum(dkn * khat, axis=0, keepdims=True)
            dkh = dkn * kg_ref[...]
            dk_ref[rows, :] = rstd * (dkh - khat * jnp.mean(dkh * khat, axis=-1, keepdims=True))
        dkg_ref[...] += dkg

    main = pl.BlockSpec((None, kb, 3 * BLOCK, dh), lambda j, n: (j, n, 0, 0))
    halo_lo = pl.BlockSpec((None, None, BLOCK, dh), lambda j, n: (j, jnp.maximum(n * kb - 1, 0), 2, 0))
    halo_hi = pl.BlockSpec((None, None, BLOCK, dh), lambda j, n: (j, jnp.minimum(n * kb + kb, nb - 1), 0, 0))
    blk = pl.BlockSpec((None, kb * BLOCK, dh), lambda j, n: (j, n, 0))
    return pl.pallas_call(
        body, name=name, grid=(ATT_KV_HEADS, steps),
        in_specs=[halo_lo, main, halo_hi, halo_lo, main, halo_hi, blk, pl.BlockSpec((1, dh), lambda j, n: (0, 0))],
        out_specs=[blk, blk, pl.BlockSpec((None, 1, dh), lambda j, n: (j, 0, 0))],
        out_shape=[jax.ShapeDtypeStruct(k.shape, F32), jax.ShapeDtypeStruct(k.shape, F32),
                   jax.ShapeDtypeStruct((ATT_KV_HEADS, 1, dh), F32)],
        compiler_params=_params("parallel", "arbitrary"),
    )(dkw, dkw, dkw, dvw, dvw, dvw, k, k_g)


def _ada_fwd(c_act, w, b, name):
    n = w.shape[1]

    def body(c_ref, w_ref, b_ref, o_ref):
        o_ref[...] = _nn(c_ref[...], w_ref[...], precision=lax.Precision.HIGHEST) + b_ref[...]

    tn = n // 3
    return pl.pallas_call(
        body, name=name, grid=(3,),
        in_specs=[pl.BlockSpec(c_act.shape, lambda j: (0, 0)), pl.BlockSpec((w.shape[0], tn), lambda j: (0, j)),
                  pl.BlockSpec((1, tn), lambda j: (0, j))],
        out_specs=pl.BlockSpec((c_act.shape[0], tn), lambda j: (0, j)),
        out_shape=jax.ShapeDtypeStruct((c_act.shape[0], n), F32), compiler_params=_params("parallel"),
    )(c_act, w, b)


def _ada_wgrad(c_act_t, dm, name):
    D, nbatch = c_act_t.shape
    n = dm.shape[1]
    tr = 256

    def body(c_ref, dm_ref, o_ref):
        cv, dv = c_ref[...], dm_ref[...]
        acc = cv[:, 0:1] * dv[0:1, :]
        for b in range(1, nbatch):
            acc = acc + cv[:, b:b + 1] * dv[b:b + 1, :]
        o_ref[...] = acc

    return pl.pallas_call(
        body, name=name, grid=(D // tr,),
        in_specs=[pl.BlockSpec((tr, nbatch), lambda i: (i, 0)), pl.BlockSpec((nbatch, n), lambda i: (0, 0))],
        out_specs=pl.BlockSpec((tr, n), lambda i: (i, 0)), out_shape=jax.ShapeDtypeStruct((D, n), F32),
        compiler_params=_params("parallel"),
    )(c_act_t, dm)


def _adamw(w, g, m, v, name):
    R, Cn = w.shape
    tr = R
    for cand in (256, 128, 64, 32, 16, 8):
        if R % cand == 0:
            tr = cand
            break

    def body(w_ref, g_ref, m_ref, v_ref, d_ref, nm_ref, nv_ref):
        gv = g_ref[...]
        m_new = ADAM_B1 * m_ref[...] + (1.0 - ADAM_B1) * gv
        v_new = ADAM_B2 * v_ref[...] + (1.0 - ADAM_B2) * (gv * gv)
        m_hat = m_new / (1.0 - ADAM_B1 ** ADAM_STEP)
        v_hat = v_new / (1.0 - ADAM_B2 ** ADAM_STEP)
        d_ref[...] = -ADAM_LR * (m_hat / (jnp.sqrt(v_hat) + ADAM_EPS) + ADAM_WD * w_ref[...])
        nm_ref[...] = m_new
        nv_ref[...] = v_new

    blk = pl.BlockSpec((tr, Cn), lambda i: (i, 0))
    shp = jax.ShapeDtypeStruct((R, Cn), F32)
    return pl.pallas_call(
        body, name=name, grid=(R // tr,), in_specs=[blk] * 4, out_specs=[blk] * 3, out_shape=[shp] * 3,
        compiler_params=_params("parallel"),
    )(w, g, m, v)


def _place():
    return lax.axis_index("x"), lax.axis_index("y"), lax.axis_index("c")


def _flip(place, k):
    x, y, c = place
    return (1 - x if k & 4 else x, 1 - y if k & 2 else y, 1 - c if k & 1 else c)


def _dev_index(place):
    x, y, c = place
    return 4 * x + 2 * y + c


def _chip_index(place):
    return 2 * place[0] + place[1]


def _allgather8(x, name, reduce=False):
    R, Cn = x.shape

    def body(x_ref, *rest):
        if reduce:
            out_ref, sum_ref, send_sems, recv_sems, local_sem = rest
        else:
            out_ref, send_sems, recv_sems, local_sem = rest
        me = _place()
        mine = pltpu.make_async_copy(x_ref, out_ref.at[_dev_index(me)], local_sem)
        mine.start()

        def copy(k, origin, to):
            return pltpu.make_async_remote_copy(
                src_ref=x_ref, dst_ref=out_ref.at[_dev_index(origin)], send_sem=send_sems.at[k - 1],
                recv_sem=recv_sems.at[k - 1], device_id=to, device_id_type=MESH)

        sends = [copy(k, me, _flip(me, k)) for k in range(1, 8)]
        for cp in sends:
            cp.start()
        for k in range(1, 8):
            copy(k, _flip(me, k), me).wait_recv()
        for cp in sends:
            cp.wait_send()
        mine.wait()
        if reduce:
            acc = out_ref[0]
            for i in range(1, 8):
                acc = acc + out_ref[i]
            sum_ref[...] = acc

    vm = pl.BlockSpec(memory_space=pltpu.VMEM)
    outs = [jax.ShapeDtypeStruct((8, R, Cn), F32)] + ([jax.ShapeDtypeStruct((R, Cn), F32)] if reduce else [])
    res = pl.pallas_call(
        body, name=name, in_specs=[vm], out_specs=[vm] * len(outs), out_shape=outs,
        scratch_shapes=[pltpu.SemaphoreType.DMA((7,)), pltpu.SemaphoreType.DMA((7,)), pltpu.SemaphoreType.DMA],
    )(x)
    return res if reduce else res[0]


def _weights_allgather(shards, name):
    n = len(shards)
    per = 8

    def body(*refs):
        in_refs, out_refs = refs[:n], refs[n:2 * n]
        send_sems, recv_sems = refs[2 * n:]
        me = _place()
        c = me[2]
        sibling = _flip(me, 1)
        others = [_flip(me, 2 * j) for j in (1, 2, 3)]

        def copy(a, k, src, dst, to):
            return pltpu.make_async_remote_copy(
                src_ref=src, dst_ref=dst, send_sem=send_sems.at[per * a + k], recv_sem=recv_sems.at[per * a + k],
                device_id=to, device_id_type=MESH)

        def block(a, place, half):
            return out_refs[a].at[_chip_index(place), half]

        started = []
        for a in range(n):
            sends = [copy(a, 0, in_refs[a].at[c], block(a, me, c), sibling),
                     copy(a, 7, in_refs[a].at[1 - c], block(a, me, 1 - c), sibling)]
            sends += [copy(a, 1 + j, in_refs[a].at[c], block(a, me, c), to) for j, to in enumerate(others)]
            for cp in sends:
                cp.start()
            started += sends
        for a in range(n):
            for j, other in enumerate(others):
                landed = block(a, other, c)
                copy(a, 1 + j, landed, landed, me).wait_recv()
                fwd = copy(a, 4 + j, landed, landed, sibling)
                fwd.start()
                started.append(fwd)
        for a in range(n):
            copy(a, 0, block(a, me, 1 - c), block(a, me, 1 - c), me).wait_recv()
            copy(a, 7, block(a, me, c), block(a, me, c), me).wait_recv()
            for j, other in enumerate(others):
                got = block(a, other, 1 - c)
                copy(a, 4 + j, got, got, me).wait_recv()
        for cp in started:
            cp.wait_send()

    return pl.pallas_call(
        body, name=name, in_specs=[ANY] * n, out_specs=[ANY] * n,
        out_shape=[jax.ShapeDtypeStruct((N_CHIPS,) + s.shape, s.dtype) for s in shards],
        scratch_shapes=[pltpu.SemaphoreType.DMA((per * n,)), pltpu.SemaphoreType.DMA((per * n,))],
    )(*shards)


def _halves_exchange(grads, name):
    n = len(grads)

    def body(*refs):
        in_refs, got_refs = refs[:n], refs[n:2 * n]
        send_sems, recv_sems = refs[2 * n:]
        me = _place()
        c = me[2]
        sibling = _flip(me, 1)
        started = []
        for a in range(n):
            for kk in range(N_CHIPS):
                i = N_CHIPS * a + kk
                send = pltpu.make_async_remote_copy(
                    src_ref=in_refs[a].at[kk, 1 - c], dst_ref=got_refs[a].at[kk], send_sem=send_sems.at[i],
                    recv_sem=recv_sems.at[i], device_id=sibling, device_id_type=MESH)
                send.start()
                started.append(send)
        for send in started:
            send.wait_recv()
        for send in started:
            send.wait_send()

    return pl.pallas_call(
        body, name=name, in_specs=[ANY] * n, out_specs=[ANY] * n,
        out_shape=[jax.ShapeDtypeStruct((N_CHIPS,) + g.shape[2:], g.dtype) for g in grads],
        scratch_shapes=[pltpu.SemaphoreType.DMA((N_CHIPS * n,)), pltpu.SemaphoreType.DMA((N_CHIPS * n,))],
    )(*grads)


def _chips_exchange(parts, name):
    n = len(parts)

    def body(*refs):
        in_refs, out_refs = refs[:n], refs[n:2 * n]
        send_sems, recv_sems = refs[2 * n:]
        me = _place()
        started = []
        for a in range(n):
            for j in (1, 2, 3):
                peer = _flip(me, 2 * j)
                send = pltpu.make_async_remote_copy(
                    src_ref=in_refs[a].at[_chip_index(peer)], dst_ref=out_refs[a].at[j - 1],
                    send_sem=send_sems.at[3 * a + j - 1], recv_sem=recv_sems.at[3 * a + j - 1],
                    device_id=peer, device_id_type=MESH)
                send.start()
                started.append(send)
        for send in started:
            send.wait_recv()
        for send in started:
            send.wait_send()

    return pl.pallas_call(
        body, name=name, in_specs=[ANY] * n, out_specs=[ANY] * n,
        out_shape=[jax.ShapeDtypeStruct((3,) + p.shape[1:], p.dtype) for p in parts],
        scratch_shapes=[pltpu.SemaphoreType.DMA((3 * n,)), pltpu.SemaphoreType.DMA((3 * n,))],
    )(*parts)


def _siblings_exchange(halves, name):
    n = len(halves)

    def body(*refs):
        in_refs, out_refs = refs[:n], refs[n:2 * n]
        send_sems, recv_sems = refs[2 * n:]
        sibling = _flip(_place(), 1)
        started = []
        for a in range(n):
            send = pltpu.make_async_remote_copy(
                src_ref=in_refs[a], dst_ref=out_refs[a], send_sem=send_sems.at[a], recv_sem=recv_sems.at[a],
                device_id=sibling, device_id_type=MESH)
            send.start()
            started.append(send)
        for send in started:
            send.wait_recv()
        for send in started:
            send.wait_send()

    return pl.pallas_call(
        body, name=name, in_specs=[ANY] * n, out_specs=[ANY] * n,
        out_shape=[jax.ShapeDtypeStruct(h.shape, h.dtype) for h in halves],
        scratch_shapes=[pltpu.SemaphoreType.DMA((n,)), pltpu.SemaphoreType.DMA((n,))],
    )(*halves)


def _row_tile(rows):
    for cand in (256, 176, 128, 64, 32, 16, 8):
        if rows % cand == 0:
            return cand
    return rows


def _pair_sum(core, grad, theirs, name):
    N, _, R, Cn = grad.shape
    tr = _row_tile(R)

    def body(core_ref, g_ref, t_ref, o_ref, ob_ref):
        s = g_ref[...] + t_ref[...]
        o_ref[...] = s
        ob_ref[...] = s.astype(BF16)

    out = pl.BlockSpec((None, tr, Cn), lambda k, i, core_ref: (k, i, 0))
    return pl.pallas_call(
        body, name=name,
        grid_spec=pltpu.PrefetchScalarGridSpec(
            num_scalar_prefetch=1, grid=(N, R // tr),
            in_specs=[pl.BlockSpec((None, None, tr, Cn), lambda k, i, core_ref: (k, core_ref[0], i, 0)),
                      pl.BlockSpec((None, tr, Cn), lambda k, i, core_ref: (k, i, 0))],
            out_specs=[out, out]),
        out_shape=[jax.ShapeDtypeStruct((N, R, Cn), F32), jax.ShapeDtypeStruct((N, R, Cn), BF16)],
        compiler_params=_params("parallel", "parallel"),
    )(core, grad, theirs)


def _chip_sum(chip, parts, landed, name):
    _, R, Cn = parts.shape
    tr = _row_tile(R)

    def body(chip_ref, p_ref, l_ref, o_ref):
        o_ref[...] = ((p_ref[...] + l_ref[0].astype(F32)) + l_ref[1].astype(F32)) + l_ref[2].astype(F32)

    return pl.pallas_call(
        body, name=name,
        grid_spec=pltpu.PrefetchScalarGridSpec(
            num_scalar_prefetch=1, grid=(R // tr,),
            in_specs=[pl.BlockSpec((None, tr, Cn), lambda i, chip_ref: (chip_ref[0], i, 0)),
                      pl.BlockSpec((3, tr, Cn), lambda i, chip_ref: (0, i, 0))],
            out_specs=pl.BlockSpec((tr, Cn), lambda i, chip_ref: (i, 0))),
        out_shape=jax.ShapeDtypeStruct((R, Cn), F32), compiler_params=_params("parallel"),
    )(chip, parts, landed)


def _reduce_scatter(grads, core, chip, tag):
    theirs = _halves_exchange(grads, f"{tag}_halves_exchange")
    parts = [_pair_sum(core, g, t, f"{tag}_pair_sum_{i}") for i, (g, t) in enumerate(zip(grads, theirs))]
    landed = _chips_exchange([p[1] for p in parts], f"{tag}_chips_exchange")
    halves = [_chip_sum(chip, p[0], l, f"{tag}_chip_sum_{i}") for i, (p, l) in enumerate(zip(parts, landed))]
    return list(zip(halves, _siblings_exchange(halves, f"{tag}_siblings_exchange")))


def _adamw_halves(core, w, g_mine, g_theirs, m, v, name):
    R2, Cn = w.shape
    r = R2 // 2
    tr = _row_tile(r)
    nt = r // tr

    def body(core_ref, w_ref, gm_ref, gt_ref, m_ref, v_ref, g_ref, d_ref, nm_ref, nv_ref):
        gv = jnp.where(pl.program_id(0) == core_ref[0], gm_ref[...], gt_ref[...])
        g_ref[...] = gv
        m_new = ADAM_B1 * m_ref[...] + (1.0 - ADAM_B1) * gv
        v_new = ADAM_B2 * v_ref[...] + (1.0 - ADAM_B2) * (gv * gv)
        m_hat = m_new / (1.0 - ADAM_B1 ** ADAM_STEP)
        v_hat = v_new / (1.0 - ADAM_B2 ** ADAM_STEP)
        d_ref[...] = -ADAM_LR * (m_hat / (jnp.sqrt(v_hat) + ADAM_EPS) + ADAM_WD * w_ref[...])
        nm_ref[...] = m_new
        nv_ref[...] = v_new

    full = pl.BlockSpec((tr, Cn), lambda hf, i, core_ref: (hf * nt + i, 0))
    half = pl.BlockSpec((tr, Cn), lambda hf, i, core_ref: (i, 0))
    shp = jax.ShapeDtypeStruct((R2, Cn), F32)
    return pl.pallas_call(
        body, name=name,
        grid_spec=pltpu.PrefetchScalarGridSpec(
            num_scalar_prefetch=1, grid=(2, nt), in_specs=[full, half, half, full, full], out_specs=[full] * 4),
        out_shape=[shp] * 4, compiler_params=_params("parallel", "parallel"),
    )(core, w, g_mine, g_theirs, m, v)


def _pad_row(v, width):
    v = v.reshape(1, -1)
    return jnp.pad(v, ((0, 0), (0, width - v.shape[1])))


def _ffn_forward(x, ng, shift, scale, gate, w_in4, w_out, tag):
    h = _rmsmod_fwd(x, ng, shift, scale, f"{tag}_norm")
    zg, zu, a = _ffn_in_fwd(h, w_in4, f"{tag}_in")
    x_new, f = _proj_out_fwd([a], w_out, x, gate, 0.5, f"{tag}_out")
    return x_new, (h, zg, zu, a, f)


def _ffn_backward(df, saved, w_in4, w_out, tag):
    h, zg, zu, a, _ = saved
    dzg, dzu = _dact_bwd(df, w_out, zg, zu, f"{tag}_dact")
    dw_out = _wgrad(a, df, 512, f"{tag}_dw_out")
    dw_out = jnp.concatenate([dw_out[0], dw_out[1]], axis=1)
    dh = _ffn_in_dgrad(dzg, dzu, w_in4, f"{tag}_dh")
    dw_in = jnp.concatenate([_wgrad(h, dzg, FF_SHARD, f"{tag}_dw_gate"), _wgrad(h, dzu, FF_SHARD, f"{tag}_dw_up")], axis=0)
    return dh, dw_in, dw_out


def kernel(x, c, w_ada, b_ada, norm_g, w_ffn1_in, w_ffn1_out, w_ffn2_in, w_ffn2_out, w_mix_in, w_mix_out, hgrn_lb, hgrn_norm_g, qk_norm_g, attn_sink, rel_bias, loss_target, m_w_ada, m_b_ada, m_norm_g, m_w_ffn1_in, m_w_ffn1_out, m_w_ffn2_in, m_w_ffn2_out, m_w_mix_in, m_w_mix_out, m_hgrn_lb, m_hgrn_norm_g, m_qk_norm_g, m_attn_sink, m_rel_bias, v_w_ada, v_b_ada, v_norm_g, v_w_ffn1_in, v_w_ffn1_out, v_w_ffn2_in, v_w_ffn2_out, v_w_mix_in, v_w_mix_out, v_hgrn_lb, v_hgrn_norm_g, v_qk_norm_g, v_attn_sink, v_rel_bias):
    D = D_MODEL
    S = x.shape[1]
    place = (lax.axis_index("x"), lax.axis_index("y"), lax.axis_index("c"))
    me, my_chip = _dev_index(place), _chip_index(place)
    x0 = x[0]
    target = loss_target[0]

    def halves(w):
        return w.astype(BF16).reshape(2, w.shape[0] // 2, w.shape[1])

    gathered = _weights_allgather(
        [halves(w_ffn1_in[0]), halves(w_ffn1_out[0]), halves(w_ffn2_in[0]), halves(w_ffn2_out[0]), halves(w_mix_in[0]),
         halves(w_mix_out[0])], "weights_allgather")
    w1_in = gathered[0].reshape(N_CHIPS, D, FF_SHARD)
    w1_out = gathered[1].reshape(D_FF, D)
    w2_in = gathered[2].reshape(N_CHIPS, D, FF_SHARD)
    w2_out = gathered[3].reshape(D_FF, D)
    wm_in = gathered[4].reshape(N_CHIPS, D, D_IN // N_CHIPS).transpose(1, 0, 2).reshape(D, D_IN)
    wm_out = gathered[5].reshape(D, D)

    small = jnp.concatenate([_pad_row(c, D), _pad_row(norm_g, D), _pad_row(hgrn_lb, D), jnp.zeros((5, D), F32)], axis=0)
    small_all = _allgather8(small, "small_allgather")
    c_all = small_all[:, 0, :]
    by_chip = small_all[0::2]
    norm_g_full = by_chip[:, 1, :3 * 256].reshape(N_CHIPS, 3, 256).transpose(1, 0, 2).reshape(3, D)
    lb_raw = by_chip[:, 2, :2 * 2 * 128].reshape(N_CHIPS, 2, 2, 128).transpose(1, 2, 0, 3).reshape(2, 2, HG_WIDTH)
    lb = jax.nn.sigmoid(lb_raw[:, 0, :] - lb_raw[:, 1, :])
    lb_f, lb_b = lb[0:1], lb[1:2]

    c_act_all = c_all * jax.nn.sigmoid(c_all)
    n_ada = w_ada.shape[2]
    b_mine = lax.dynamic_slice_in_dim(b_ada, my_chip * n_ada, n_ada, axis=1)
    mods_part = _ada_fwd(c_act_all, w_ada[0], b_mine, "ada_fwd")
    mods_all = _allgather8(mods_part, "mods_allgather")[0::2].transpose(1, 0, 2).reshape(8, N_MOD * D)
    mods = lax.dynamic_slice_in_dim(mods_all, me, 1, axis=0)
    sh1, sc1, g1, sh2, sc2, g2, sh3, sc3, g3 = [mods[:, i * D:(i + 1) * D] for i in range(N_MOD)]

    x1, saved1 = _ffn_forward(x0, norm_g_full[0:1], sh1, sc1, g1, w1_in, w1_out, "ffn1")

    h2 = _rmsmod_fwd(x1, norm_g_full[1:2], sh2, sc2, "mix_norm")
    z = _matmul_nn(h2, wm_in, F32, 256, "mix_in")
    of, st_f = _hgrn_fwd(z, lb_f, 0, "hgrn_fwd_f")
    ob, st_b = _hgrn_fwd(z, lb_b, 1, "hgrn_fwd_b")
    o_h = _hgrn_post_fwd(of, ob, z, hgrn_norm_g, "hgrn_post")

    def to_heads(t, nh):
        return t.reshape(S, nh, ATT_HEAD_DIM).transpose(1, 0, 2)

    aq = to_heads(z[:, 5 * HG_WIDTH:5 * HG_WIDTH + ATT_WIDTH], ATT_Q_HEADS)
    ak = to_heads(z[:, 5 * HG_WIDTH + ATT_WIDTH:5 * HG_WIDTH + ATT_WIDTH + KV_WIDTH], ATT_KV_HEADS)
    av = to_heads(z[:, 5 * HG_WIDTH + ATT_WIDTH + KV_WIDTH:], ATT_KV_HEADS)
    q_g, k_g = qk_norm_g[0, 0:1], qk_norm_g[0, 1:2]
    sink_b = jnp.broadcast_to(attn_sink.reshape(ATT_Q_HEADS, 1, 1), (ATT_Q_HEADS, 1, BLOCK))
    bias = _bias_table(rel_bias, "bias_table")
    o_attn = _attn_fwd(aq, ak, av, q_g, k_g, sink_b, bias, "attn_fwd")
    o_a = o_attn.transpose(1, 0, 2).reshape(S, ATT_WIDTH).astype(BF16)
    x2, mixed = _proj_out_fwd([o_h, o_a], wm_out, x1, g2, 1.0, "mix_out")

    x3, saved3 = _ffn_forward(x2, norm_g_full[2:3], sh3, sc3, g3, w2_in, w2_out, "ffn2")

    dx3, df3, dg3, sq_cols = _loss_bwd(x3, target, saved3[4], g3, 0.5, "loss")
    loss_mine = 0.5 * jnp.sum(sq_cols) / D

    dh3, dw2_in, dw2_out = _ffn_backward(df3, saved3, w2_in, w2_out, "ffn2")
    dx2, dsh3, dsc3, dng3, dmixed, dg2 = _rmsmod_bwd(dh3, x2, norm_g_full[2:3], sc3, dx3, "ffn2_norm_bwd", below=(mixed, g2, 1.0))

    do_cat = _matmul_nt(dmixed, wm_out, ROW_TILE, "mix_out_dgrad")
    dwm_out = jnp.concatenate([_wgrad(o_h, dmixed, 512, "mix_out_dw_h"), _wgrad(o_a, dmixed, 512, "mix_out_dw_a")], axis=1)
    dwm_out = jnp.concatenate([dwm_out[0], dwm_out[1]], axis=1)

    do_sum, dgr, d_hnorm = _hgrn_post_bwd(do_cat, of, ob, z, hgrn_norm_g, "hgrn_post_bwd")
    dq_f, dff, dv_f, doml_f = _hgrn_bwd(z, lb_f, do_sum, st_f, 0, "hgrn_bwd_f")
    dhq, dfb, dhi, doml_b = _hgrn_bwd(z, lb_b, do_sum, st_b, 1, "hgrn_bwd_b", acc=(dq_f, dv_f))

    do_a = to_heads(do_cat[:, HG_WIDTH:], ATT_Q_HEADS)
    daq, dkw, dvw, ds_sum, dsink, dqg = _attn_bwd(aq, ak, av, q_g, k_g, sink_b, bias, do_a, "attn_bwd")
    dak, dav, dkg = _attn_kv_reduce(dkw, dvw, ak, k_g, "attn_kv_reduce")
    d_rel_bias = jnp.sum(_bias_grad(ds_sum, "bias_grad"), axis=-1).T

    def from_heads(t):
        return t.transpose(1, 0, 2).reshape(S, -1)

    dz = jnp.concatenate([dhq, dff, dfb, dhi, dgr, from_heads(daq), from_heads(dak), from_heads(dav)], axis=1).astype(BF16)
    dh2 = _matmul_nt(dz, wm_in, 256, "mix_in_dgrad")
    dwm_in = _wgrad(h2, dz, D_IN // 2, "mix_in_dw")
    dwm_in = jnp.concatenate([dwm_in[0], dwm_in[1]], axis=1)
    dx1, dsh2, dsc2, dng2, df1, dg1 = _rmsmod_bwd(dh2, x1, norm_g_full[1:2], sc2, dx2, "mix_norm_bwd", below=(saved1[4], g1, 0.5))

    dh1, dw1_in, dw1_out = _ffn_backward(df1, saved1, w1_in, w1_out, "ffn1")
    dx0, dsh1, dsc1, dng1 = _rmsmod_bwd(dh1, x0, norm_g_full[0:1], sc1, dx1, "ffn1_norm_bwd")

    dlb = -jnp.concatenate([doml_f, doml_b], axis=0)
    dlb_raw = dlb * lb * (1.0 - lb)
    d_hgrn_lb = jnp.stack([dlb_raw, -dlb_raw], axis=1)
    d_qk = jnp.concatenate([jnp.sum(dqg, axis=0), jnp.sum(dkg, axis=0)], axis=0)
    dmods = jnp.concatenate([dsh1, dsc1, dg1, dsh2, dsc2, dg2, dsh3, dsc3, dg3], axis=0)
    packed = jnp.concatenate(
        [dmods, dng1, dng2, dng3, d_hgrn_lb.reshape(2, D), _pad_row(d_hnorm, D), _pad_row(d_qk, D),
         _pad_row(dsink[:, 0, 0], D), _pad_row(d_rel_bias, D), _pad_row(loss_mine, D)], axis=0)
    packed = jnp.pad(packed, ((0, 24 - packed.shape[0]), (0, 0)))
    packed_all, packed_sum = _allgather8(packed, "small_grads_allgather", reduce=True)
    dmods_all = packed_all[:, 0:N_MOD, :].reshape(8, N_MOD * D)
    g_b_ada = packed_sum[0:N_MOD].reshape(1, N_MOD * D)
    g_norm_full = packed_sum[9:12]
    g_norm_g = lax.dynamic_slice_in_dim(g_norm_full, my_chip * 256, 256, axis=1).reshape(1, 3, 256)
    g_hgrn_lb = lax.dynamic_slice_in_dim(packed_sum[12:14].reshape(2, 2, HG_WIDTH), my_chip * 128, 128, axis=2)
    g_hgrn_norm_g = packed_sum[14:15, :HG_WIDTH]
    g_qk_norm_g = packed_sum[15, :2 * ATT_HEAD_DIM].reshape(1, 2, ATT_HEAD_DIM)
    g_attn_sink = packed_sum[16:17, :ATT_Q_HEADS]
    g_rel_bias = packed_sum[17, :NUM_BUCKETS * ATT_Q_HEADS].reshape(NUM_BUCKETS, ATT_Q_HEADS)
    loss = packed_sum[18, 0]

    dm_mine = lax.dynamic_slice_in_dim(dmods_all, my_chip * n_ada, n_ada, axis=1)
    g_w_ada = _ada_wgrad(c_act_all.T, dm_mine, "ada_wgrad")[None]

    def by_chip_rows(g):
        return g.reshape(N_CHIPS, 2, g.shape[0] // (2 * N_CHIPS), g.shape[1])

    def by_chip_cols(g):
        return g.reshape(N_CHIPS, 2, g.shape[1] // 2, g.shape[2])

    wide = D_IN // N_CHIPS
    core_arr = jnp.reshape(place[2], (1,)).astype(jnp.int32)
    chip_arr = jnp.reshape(my_chip, (1,)).astype(jnp.int32)
    reduced = _reduce_scatter(
        [by_chip_cols(dw1_in), by_chip_rows(dw1_out), by_chip_cols(dw2_in), by_chip_rows(dw2_out),
         by_chip_cols(dwm_in.reshape(D, N_CHIPS, wide).transpose(1, 0, 2)), by_chip_rows(dwm_out)], core_arr, chip_arr, "grads")

    def big(w, g, m, v, name):
        d, nm, nv = _adamw(w[0], g[0], m[0], v[0], name)
        return d[None], nm[None], nv[None]

    def big_halves(w, g_pair, m, v, name):
        g, d, nm, nv = _adamw_halves(core_arr, w[0], g_pair[0], g_pair[1], m[0], v[0], name)
        return g[None], (d[None], nm[None], nv[None])

    g_w1_in, u_w1_in = big_halves(w_ffn1_in, reduced[0], m_w_ffn1_in, v_w_ffn1_in, "adamw_w_ffn1_in")
    g_w1_out, u_w1_out = big_halves(w_ffn1_out, reduced[1], m_w_ffn1_out, v_w_ffn1_out, "adamw_w_ffn1_out")
    g_w2_in, u_w2_in = big_halves(w_ffn2_in, reduced[2], m_w_ffn2_in, v_w_ffn2_in, "adamw_w_ffn2_in")
    g_w2_out, u_w2_out = big_halves(w_ffn2_out, reduced[3], m_w_ffn2_out, v_w_ffn2_out, "adamw_w_ffn2_out")
    g_wm_in, u_wm_in = big_halves(w_mix_in, reduced[4], m_w_mix_in, v_w_mix_in, "adamw_w_mix_in")
    g_wm_out, u_wm_out = big_halves(w_mix_out, reduced[5], m_w_mix_out, v_w_mix_out, "adamw_w_mix_out")

    smalls = [(b_ada, g_b_ada, m_b_ada, v_b_ada), (norm_g, g_norm_g, m_norm_g, v_norm_g), (hgrn_lb, g_hgrn_lb, m_hgrn_lb, v_hgrn_lb),
              (hgrn_norm_g, g_hgrn_norm_g, m_hgrn_norm_g, v_hgrn_norm_g), (qk_norm_g, g_qk_norm_g, m_qk_norm_g, v_qk_norm_g),
              (attn_sink, g_attn_sink, m_attn_sink, v_attn_sink), (rel_bias, g_rel_bias, m_rel_bias, v_rel_bias)]
    sizes = [t[0].size for t in smalls]
    total = sum(sizes)
    rows = -(-total // 128)
    rows = -(-rows // 8) * 8

    def pack(i):
        flat = jnp.concatenate([t[i].reshape(-1) for t in smalls])
        fill = 1.0 if i == 3 else 0.0
        return jnp.pad(flat, (0, rows * 128 - total), constant_values=fill).reshape(rows, 128)

    packed_out = _adamw(pack(0), pack(1), pack(2), pack(3), "adamw_small")

    def unpack(flat2d):
        flat = flat2d.reshape(-1)
        outs, off = [], 0
        for t, n in zip(smalls, sizes):
            outs.append(flat[off:off + n].reshape(t[0].shape))
            off += n
        return outs

    d_small, m_small, v_small = [unpack(t) for t in packed_out]

    upd = {
        "w_ada": big(w_ada, g_w_ada, m_w_ada, v_w_ada, "adamw_w_ada"),
        "w_ffn1_in": u_w1_in, "w_ffn1_out": u_w1_out, "w_ffn2_in": u_w2_in, "w_ffn2_out": u_w2_out,
        "w_mix_in": u_wm_in, "w_mix_out": u_wm_out,
    }
    small_names = ["b_ada", "norm_g", "hgrn_lb", "hgrn_norm_g", "qk_norm_g", "attn_sink", "rel_bias"]
    for i, nme in enumerate(small_names):
        upd[nme] = (d_small[i], m_small[i], v_small[i])
    grads = {
        "w_ada": g_w_ada, "b_ada": g_b_ada, "norm_g": g_norm_g, "w_ffn1_in": g_w1_in, "w_ffn1_out": g_w1_out,
        "w_ffn2_in": g_w2_in, "w_ffn2_out": g_w2_out, "w_mix_in": g_wm_in, "w_mix_out": g_wm_out, "hgrn_lb": g_hgrn_lb,
        "hgrn_norm_g": g_hgrn_norm_g, "qk_norm_g": g_qk_norm_g, "attn_sink": g_attn_sink, "rel_bias": g_rel_bias,
    }
    order = ["w_ada", "b_ada", "norm_g", "w_ffn1_in", "w_ffn1_out", "w_ffn2_in", "w_ffn2_out", "w_mix_in", "w_mix_out",
             "hgrn_lb", "hgrn_norm_g", "qk_norm_g", "attn_sink", "rel_bias"]
    return (loss, dx0[None], *[grads[k] for k in order], *[upd[k][0] for k in order], *[upd[k][1] for k in order],
            *[upd[k][2] for k in order])
```

```python
import functools
import math

import numpy as np
import jax
import jax.numpy as jnp
from jax import lax
from jax.experimental import pallas as pl
from jax.experimental.pallas import tpu as pltpu

F32, BF16 = jnp.float32, jnp.bfloat16

D_MODEL = 1024
D_FF = 2816
HG_HEADS, HG_DIM = 4, 128
HG_WIDTH = HG_HEADS * HG_DIM
ATT_Q_HEADS, ATT_KV_HEADS, ATT_HEAD_DIM = 8, 2, 64
ATT_GROUP = ATT_Q_HEADS // ATT_KV_HEADS
ATT_WIDTH = ATT_Q_HEADS * ATT_HEAD_DIM
KV_WIDTH = ATT_KV_HEADS * ATT_HEAD_DIM
WINDOW, BLOCK = 128, 128
NUM_BUCKETS, MAX_DISTANCE = 32, 128
N_MOD = 9
EPS = 1e-6
D_IN = 5 * HG_WIDTH + ATT_WIDTH + 2 * KV_WIDTH
ADAM_LR, ADAM_B1, ADAM_B2, ADAM_EPS, ADAM_WD, ADAM_STEP = 0.001, 0.9, 0.999, 1e-08, 0.01, 10

N_CHIPS = 4
FF_SHARD = 2 * D_FF // N_CHIPS
NEG = -1e30

VMEM_LIMIT_BYTES = 56 << 20
ROW_TILE = 512
HG_CHUNK = 16
HG_ROWS = 256

MESH = pl.DeviceIdType.MESH
ANY = pl.BlockSpec(memory_space=pl.ANY)


def _params(*sem):
    return pltpu.CompilerParams(dimension_semantics=sem, vmem_limit_bytes=VMEM_LIMIT_BYTES)


def _resident(shape, index_map):
    return pl.BlockSpec(shape, index_map, pipeline_mode=pl.Buffered(1))


def _dot(a, b, dims, precision=None):
    return lax.dot_general(a, b, (dims, ((), ())), precision=precision, preferred_element_type=F32)


def _nn(a, b, precision=None):
    return _dot(a, b, ((1,), (0,)), precision)


def _nt(a, b):
    return _dot(a, b, ((1,), (1,)))


def _tn(a, b):
    return _dot(a, b, ((0,), (0,)))


def _sigmoid(x):
    return jax.nn.sigmoid(x)


class _Exchange:
    def __init__(self, inputs, out_shapes, n_sems, plan, aliases=None):
        self.inputs, self.out_shapes, self.n_sems, self.plan, self.aliases = list(inputs), list(out_shapes), n_sems, plan, aliases or {}

    def sem_shapes(self):
        return [pltpu.SemaphoreType.DMA((self.n_sems,)), pltpu.SemaphoreType.DMA((self.n_sems,))]

    def start(self, in_refs, out_refs, send_sems, recv_sems):
        for cp in self.plan(in_refs, out_refs, send_sems, recv_sems)[0]:
            cp.start()

    def finish(self, in_refs, out_refs, send_sems, recv_sems):
        sends, recvs = self.plan(in_refs, out_refs, send_sems, recv_sems)
        for cp in recvs:
            cp.wait_recv()
        for cp in sends:
            cp.wait_send()


def _run_exchange(ex, name):
    n_in, n_out = len(ex.inputs), len(ex.out_shapes)

    def body(*refs):
        in_refs, out_refs, (send_sems, recv_sems) = refs[:n_in], refs[n_in:n_in + n_out], refs[n_in + n_out:]
        ex.start(in_refs, out_refs, send_sems, recv_sems)
        ex.finish(in_refs, out_refs, send_sems, recv_sems)

    return pl.pallas_call(
        body, name=name, in_specs=[ANY] * n_in, out_specs=[ANY] * n_out, out_shape=ex.out_shapes,
        scratch_shapes=ex.sem_shapes(), input_output_aliases=dict(ex.aliases),
    )(*ex.inputs)


def _call(body, *, name, grid, in_specs, out_specs, out_shape, args, semantics, scratch_shapes=(), exchange=None):
    if exchange is None:
        return pl.pallas_call(
            body, name=name, grid=grid, in_specs=in_specs, out_specs=out_specs, out_shape=out_shape,
            scratch_shapes=list(scratch_shapes), compiler_params=_params(*semantics))(*args)
    ex = exchange
    n_in, n_out, n_scr = len(in_specs), len(out_specs), len(scratch_shapes)
    x_in, x_out = len(ex.inputs), len(ex.out_shapes)

    def carrier(*refs):
        ins, refs = refs[:n_in], refs[n_in:]
        x_ins, refs = refs[:x_in], refs[x_in:]
        outs, refs = refs[:n_out], refs[n_out:]
        x_outs, refs = refs[:x_out], refs[x_out:]
        scr, (send_sems, recv_sems) = refs[:n_scr], refs[n_scr:]
        ids = [pl.program_id(a) for a in range(len(grid))]
        first = functools.reduce(jnp.logical_and, [i == 0 for i in ids])
        last = functools.reduce(jnp.logical_and, [i == g - 1 for i, g in zip(ids, grid)])

        @pl.when(first)
        def _():
            ex.start(x_ins, x_outs, send_sems, recv_sems)

        body(*ins, *outs, *scr)

        @pl.when(last)
        def _():
            ex.finish(x_ins, x_outs, send_sems, recv_sems)

    res = pl.pallas_call(
        carrier, name=name, grid=grid, in_specs=list(in_specs) + [ANY] * x_in, out_specs=list(out_specs) + [ANY] * x_out,
        out_shape=list(out_shape) + ex.out_shapes, scratch_shapes=list(scratch_shapes) + ex.sem_shapes(),
        input_output_aliases={n_in + i: n_out + o for i, o in ex.aliases.items()},
        compiler_params=_params(*["arbitrary"] * len(grid)),
    )(*args, *ex.inputs)
    return list(res[:n_out]), list(res[n_out:])


def _rmsmod_fwd(x, g, shift, scale, name):
    S, D = x.shape
    tr = min(ROW_TILE, S)

    def body(x_ref, g_ref, sh_ref, sc_ref, h_ref):
        xv = x_ref[...]
        rstd = lax.rsqrt(jnp.mean(xv * xv, axis=-1, keepdims=True) + EPS)
        y = xv * rstd * g_ref[...]
        h_ref[...] = (y * (1.0 + sc_ref[...]) + sh_ref[...]).astype(h_ref.dtype)

    row = pl.BlockSpec((tr, D), lambda i: (i, 0))
    vec = pl.BlockSpec((1, D), lambda i: (0, 0))
    return pl.pallas_call(
        body, name=name, grid=(S // tr,), in_specs=[row, vec, vec, vec], out_specs=row,
        out_shape=jax.ShapeDtypeStruct((S, D), BF16), compiler_params=_params("parallel"),
    )(x, g, shift, scale)


def _rmsmod_bwd(dh, x, g, scale, dx_res, name, below=None):
    S, D = x.shape
    tr = min(ROW_TILE, S)
    coef = below[2] if below else None

    def body(*refs):
        if below:
            dh_ref, x_ref, g_ref, sc_ref, dxr_ref, f_ref, gate_ref, dx_ref, dsh_ref, dsc_ref, dg_ref, df_ref, dgate_ref = refs
        else:
            dh_ref, x_ref, g_ref, sc_ref, dxr_ref, dx_ref, dsh_ref, dsc_ref, dg_ref = refs

        @pl.when(pl.program_id(0) == 0)
        def _():
            dsh_ref[...] = jnp.zeros_like(dsh_ref)
            dsc_ref[...] = jnp.zeros_like(dsc_ref)
            dg_ref[...] = jnp.zeros_like(dg_ref)
            if below:
                dgate_ref[...] = jnp.zeros_like(dgate_ref)

        dhv, xv, gv = dh_ref[...], x_ref[...], g_ref[...]
        one_sc = 1.0 + sc_ref[...]
        rstd = lax.rsqrt(jnp.mean(xv * xv, axis=-1, keepdims=True) + EPS)
        n = xv * rstd
        dsh_ref[...] += jnp.sum(dhv, axis=0, keepdims=True)
        dsc_ref[...] += jnp.sum(dhv * n, axis=0, keepdims=True) * gv
        dg_ref[...] += jnp.sum(dhv * n, axis=0, keepdims=True) * one_sc
        dn = dhv * (gv * one_sc)
        dx = dxr_ref[...] + rstd * (dn - n * jnp.mean(dn * n, axis=-1, keepdims=True))
        dx_ref[...] = dx
        if below:
            df_ref[...] = (coef * gate_ref[...] * dx).astype(df_ref.dtype)
            dgate_ref[...] += coef * jnp.sum(dx * f_ref[...].astype(F32), axis=0, keepdims=True)

    row = pl.BlockSpec((tr, D), lambda i: (i, 0))
    vec = pl.BlockSpec((1, D), lambda i: (0, 0))
    vshape = jax.ShapeDtypeStruct((1, D), F32)
    ins, in_specs = [dh, x, g, scale, dx_res], [row, row, vec, vec, row]
    outs, out_specs = [jax.ShapeDtypeStruct((S, D), F32), vshape, vshape, vshape], [row, vec, vec, vec]
    if below:
        ins += [below[0], below[1]]
        in_specs += [row, vec]
        outs += [jax.ShapeDtypeStruct((S, D), BF16), vshape]
        out_specs += [row, vec]
    return pl.pallas_call(
        body, name=name, grid=(S // tr,), in_specs=in_specs, out_specs=out_specs, out_shape=outs,
        compiler_params=_params("arbitrary"),
    )(*ins)


def _loss_bwd(y, target, f, gate, coef, name):
    S, D = y.shape
    tr = min(ROW_TILE, S)

    def body(y_ref, t_ref, f_ref, gate_ref, dy_ref, df_ref, dgate_ref, sq_ref):
        @pl.when(pl.program_id(0) == 0)
        def _():
            dgate_ref[...] = jnp.zeros_like(dgate_ref)
            sq_ref[...] = jnp.zeros_like(sq_ref)

        err = y_ref[...] - t_ref[...]
        sq_ref[...] += jnp.sum(err * err, axis=0, keepdims=True)
        dy = err * (1.0 / D)
        dy_ref[...] = dy
        df_ref[...] = (coef * gate_ref[...] * dy).astype(df_ref.dtype)
        dgate_ref[...] += coef * jnp.sum(dy * f_ref[...].astype(F32), axis=0, keepdims=True)

    row = pl.BlockSpec((tr, D), lambda i: (i, 0))
    vec = pl.BlockSpec((1, D), lambda i: (0, 0))
    vshape = jax.ShapeDtypeStruct((1, D), F32)
    return pl.pallas_call(
        body, name=name, grid=(S // tr,), in_specs=[row, row, row, vec], out_specs=[row, row, vec, vec],
        out_shape=[jax.ShapeDtypeStruct((S, D), F32), jax.ShapeDtypeStruct((S, D), BF16), vshape, vshape],
        compiler_params=_params("arbitrary"),
    )(y, target, f, gate)


def _ffn_in_fwd(h, w4, name, exchange=None):
    S, D = h.shape
    tm = min(ROW_TILE, S)
    n = w4.shape[2]

    def body(h_ref, wg_ref, wu_ref, zg_ref, zu_ref, a_ref):
        hv = h_ref[...]
        zg = _nn(hv, wg_ref[...])
        zu = _nn(hv, wu_ref[...])
        zg_ref[...] = zg.astype(zg_ref.dtype)
        zu_ref[...] = zu.astype(zu_ref.dtype)
        a_ref[...] = (zg * _sigmoid(zg) * zu).astype(a_ref.dtype)

    out = pl.BlockSpec((tm, n), lambda j, m: (m, j))
    oshape = jax.ShapeDtypeStruct((S, 2 * n), BF16)
    return _call(
        body, name=name, grid=(2, S // tm),
        in_specs=[pl.BlockSpec((tm, D), lambda j, m: (m, 0)),
                  pl.BlockSpec((None, D, n), lambda j, m: (j, 0, 0)),
                  pl.BlockSpec((None, D, n), lambda j, m: (j + 2, 0, 0))],
        out_specs=[out, out, out], out_shape=[oshape, oshape, oshape], args=(h, w4, w4),
        semantics=("parallel", "parallel"), exchange=exchange)


def _proj_out_fwd(lhs, w, x, gate, coef, name, exchange=None):
    S, D = x.shape
    tm = min(ROW_TILE, S)
    ks = [a.shape[1] for a in lhs]

    def body(*refs):
        lhs_refs = refs[:len(lhs)]
        w_ref, x_ref, gate_ref, xn_ref, f_ref = refs[len(lhs):]
        acc, off = None, 0
        for a_ref, k in zip(lhs_refs, ks):
            part = _nn(a_ref[...], w_ref[off:off + k, :])
            acc = part if acc is None else acc + part
            off += k
        f_ref[...] = acc.astype(f_ref.dtype)
        xn_ref[...] = x_ref[...] + coef * gate_ref[...] * acc

    row = pl.BlockSpec((tm, D), lambda m: (m, 0))
    return _call(
        body, name=name, grid=(S // tm,),
        in_specs=[pl.BlockSpec((tm, k), lambda m: (m, 0)) for k in ks]
        + [_resident(w.shape, lambda m: (0, 0)), row, pl.BlockSpec((1, D), lambda m: (0, 0))],
        out_specs=[row, row],
        out_shape=[jax.ShapeDtypeStruct((S, D), F32), jax.ShapeDtypeStruct((S, D), BF16)],
        args=(*lhs, w, x, gate), semantics=("parallel",), exchange=exchange)


def _matmul_nn(a, w, out_dtype, tm, name):
    S, K = a.shape
    N = w.shape[1]
    tm = min(tm, S)

    def body(a_ref, w_ref, o_ref):
        o_ref[...] = _nn(a_ref[...], w_ref[...]).astype(o_ref.dtype)

    return pl.pallas_call(
        body, name=name, grid=(S // tm,),
        in_specs=[pl.BlockSpec((tm, K), lambda m: (m, 0)), _resident((K, N), lambda m: (0, 0))],
        out_specs=pl.BlockSpec((tm, N), lambda m: (m, 0)), out_shape=jax.ShapeDtypeStruct((S, N), out_dtype),
        compiler_params=_params("parallel"),
    )(a, w)


def _dact_bwd(df, w_out, zg, zu, name, exchange=None):
    S, D = df.shape
    tm = min(ROW_TILE, S)
    n = w_out.shape[0] // 2

    def body(df_ref, w_ref, zg_ref, zu_ref, dzg_ref, dzu_ref):
        da = _nt(df_ref[...], w_ref[...])
        zg_v, zu_v = zg_ref[...].astype(F32), zu_ref[...].astype(F32)
        s = _sigmoid(zg_v)
        dzu_ref[...] = (da * zg_v * s).astype(dzu_ref.dtype)
        dzg_ref[...] = (da * zu_v * (s * (1.0 + zg_v * (1.0 - s)))).astype(dzg_ref.dtype)

    blk = pl.BlockSpec((tm, n), lambda j, m: (m, j))
    oshape = jax.ShapeDtypeStruct((S, 2 * n), BF16)
    return _call(
        body, name=name, grid=(2, S // tm),
        in_specs=[pl.BlockSpec((tm, D), lambda j, m: (m, 0)), pl.BlockSpec((n, D), lambda j, m: (j, 0)), blk, blk],
        out_specs=[blk, blk], out_shape=[oshape, oshape], args=(df, w_out, zg, zu), semantics=("parallel", "parallel"),
        exchange=exchange)


def _ffn_in_dgrad(dzg, dzu, w4, name, exchange=None):
    S = dzg.shape[0]
    D, n = w4.shape[1], w4.shape[2]
    tm = min(ROW_TILE, S)

    def body(dzg_ref, dzu_ref, w_ref, dh_ref):
        acc = _nt(dzg_ref[:, 0:n], w_ref[0])
        acc += _nt(dzg_ref[:, n:2 * n], w_ref[1])
        acc += _nt(dzu_ref[:, 0:n], w_ref[2])
        acc += _nt(dzu_ref[:, n:2 * n], w_ref[3])
        dh_ref[...] = acc

    blk = pl.BlockSpec((tm, 2 * n), lambda m: (m, 0))
    return _call(
        body, name=name, grid=(S // tm,),
        in_specs=[blk, blk, _resident(w4.shape, lambda m: (0, 0, 0))],
        out_specs=[pl.BlockSpec((tm, D), lambda m: (m, 0))], out_shape=[jax.ShapeDtypeStruct((S, D), F32)],
        args=(dzg, dzu, w4), semantics=("parallel",), exchange=exchange)


def _matmul_nt(a, w, tm, name, exchange=None):
    S, K = a.shape
    N = w.shape[0]
    tm = min(tm, S)

    def body(a_ref, w_ref, o_ref):
        o_ref[...] = _nt(a_ref[...], w_ref[...])

    return _call(
        body, name=name, grid=(S // tm,),
        in_specs=[pl.BlockSpec((tm, K), lambda m: (m, 0)), _resident((N, K), lambda m: (0, 0))],
        out_specs=[pl.BlockSpec((tm, N), lambda m: (m, 0))], out_shape=[jax.ShapeDtypeStruct((S, N), F32)],
        args=(a, w), semantics=("parallel",), exchange=exchange)


def _wgrad(a, g, tn, name):
    S, Ka = a.shape
    N = g.shape[1]
    ts = min(ROW_TILE, S)

    def body(a_ref, g_ref, o_ref):
        @pl.when(pl.program_id(1) == 0)
        def _():
            o_ref[...] = jnp.zeros_like(o_ref)

        o_ref[...] += _tn(a_ref[...], g_ref[...])

    return pl.pallas_call(
        body, name=name, grid=(N // tn, S // ts),
        in_specs=[pl.BlockSpec((ts, Ka), lambda j, s: (s, 0)), pl.BlockSpec((ts, tn), lambda j, s: (s, j))],
        out_specs=pl.BlockSpec((None, Ka, tn), lambda j, s: (j, 0, 0)),
        out_shape=jax.ShapeDtypeStruct((N // tn, Ka, tn), F32), compiler_params=_params("parallel", "arbitrary"),
    )(a, g)


def _hgrn_chunk_common(qr, fr, oml, tri, last):
    k = oml * _sigmoid(-fr)
    g = jnp.log1p(-k)
    q = qr * _sigmoid(qr)
    G = _nn(tri, g, precision=lax.Precision.HIGHEST)
    Gl = G[last:last + 1]
    return q, k, G, Gl


def _hgrn_consts(reverse):
    C = HG_CHUNK
    r = lax.broadcasted_iota(jnp.int32, (C, C), 0)
    cc = lax.broadcasted_iota(jnp.int32, (C, C), 1)
    tri = ((cc >= r) if reverse else (cc <= r)).astype(F32)
    tri_t = ((cc <= r) if reverse else (cc >= r)).astype(F32)
    rid = lax.broadcasted_iota(jnp.int32, (C, HG_WIDTH), 0)
    return tri, tri_t, rid, (0 if reverse else C - 1)


def _head_slices():
    return [slice(h * HG_DIM, (h + 1) * HG_DIM) for h in range(HG_HEADS)]


def _per_head_lane_sum(x):
    C = x.shape[0]
    return jnp.concatenate(
        [jnp.broadcast_to(jnp.sum(x[:, sl], axis=-1, keepdims=True), (C, HG_DIM)) for sl in _head_slices()], axis=1)


HG_TILE = 8


def _pair_tiles(s, reverse):
    blk, r = divmod(s, HG_TILE)
    n_tiles = HG_CHUNK // HG_TILE
    others = range(0, blk) if reverse else range(blk + 1, n_tiles)
    return [(blk, r)] + [(t, None) for t in others]


def _pair_decay(G, s, tile, r, rid8, reverse, keys=False):
    rs = slice(tile * HG_TILE, (tile + 1) * HG_TILE)
    d = (G[s:s + 1] - G[rs]) if keys else (G[rs] - G[s:s + 1])
    if r is not None:
        d = jnp.where((rid8 <= r) if reverse else (rid8 >= r), d, NEG)
    return rs, jnp.exp(d)


def _hgrn_fwd(z, lb, direction, name, exchange=None):
    S = z.shape[0]
    C, DK, W = HG_CHUNK, HG_DIM, HG_WIDTH
    tb = min(HG_ROWS, S)
    n_t, n_c = S // tb, tb // C
    reverse = direction == 1
    tmap = (lambda i: n_t - 1 - i) if reverse else (lambda i: i)

    def body(q_ref, f_ref, v_ref, lb_ref, o_ref, st_out_ref, st_ref):
        @pl.when(pl.program_id(0) == 0)
        def _():
            st_ref[...] = jnp.zeros_like(st_ref)

        oml = 1.0 - lb_ref[...]
        tri, _, _, last = _hgrn_consts(reverse)
        rid8 = lax.broadcasted_iota(jnp.int32, (HG_TILE, W), 0)

        def chunk(ci, carry):
            cidx = (n_c - 1 - ci) if reverse else ci
            rows = pl.ds(pl.multiple_of(cidx * C, C), C)
            v = v_ref[rows, :]
            q, k, G, Gl = _hgrn_chunk_common(q_ref[rows, :], f_ref[rows, :], oml, tri, last)
            qd = (q * jnp.exp(G)).astype(BF16)
            kd = (k * jnp.exp(Gl - G)).astype(BF16)
            e_gl = jnp.exp(Gl)
            v_b = v.astype(BF16)
            inter = []
            for h, sl in enumerate(_head_slices()):
                st0 = st_ref[h]
                st_out_ref[h, cidx] = st0
                inter.append(_nt(qd[:, sl], st0.astype(BF16)))
                st_ref[h] = st0 * e_gl[:, sl] + _tn(v_b[:, sl], kd[:, sl])
            o = jnp.concatenate(inter, axis=1)
            o_t = [o[t * HG_TILE:(t + 1) * HG_TILE] for t in range(C // HG_TILE)]
            for s in range(C):
                k_s, v_s = k[s:s + 1], v[s:s + 1]
                for tile, r in _pair_tiles(s, reverse):
                    rs, e_s = _pair_decay(G, s, tile, r, rid8, reverse)
                    o_t[tile] = o_t[tile] + _per_head_lane_sum(q[rs] * k_s * e_s) * v_s
            o_ref[rows, :] = jnp.concatenate(o_t, axis=0)
            return carry

        lax.fori_loop(0, n_c, chunk, 0, unroll=2)

    def sec(j):
        return pl.BlockSpec((tb, W), lambda i: (tmap(i), j))

    return _call(
        body, name=name, grid=(n_t,),
        in_specs=[sec(0), sec(1 + direction), sec(3), pl.BlockSpec((1, W), lambda i: (0, 0))],
        out_specs=[sec(0), pl.BlockSpec((HG_HEADS, n_c, DK, DK), lambda i: (0, tmap(i), 0, 0))],
        out_shape=[jax.ShapeDtypeStruct((S, W), F32), jax.ShapeDtypeStruct((HG_HEADS, S // C, DK, DK), F32)],
        scratch_shapes=[pltpu.VMEM((HG_HEADS, DK, DK), F32)], args=(z, z, z, lb), semantics=("arbitrary",),
        exchange=exchange)


def _hgrn_bwd(z, lb, do, states, direction, name, acc=None, exchange=None):
    S = z.shape[0]
    C, DK, W = HG_CHUNK, HG_DIM, HG_WIDTH
    tb = min(HG_ROWS, S)
    n_t, n_c = S // tb, tb // C
    reverse = direction == 1
    tmap = (lambda i: i) if reverse else (lambda i: n_t - 1 - i)

    def body(*refs):
        if acc:
            q_ref, f_ref, v_ref, lb_ref, do_ref, st_in_ref, dqa_ref, dva_ref, dq_ref, df_ref, dv_ref, doml_ref, dst_ref = refs
        else:
            q_ref, f_ref, v_ref, lb_ref, do_ref, st_in_ref, dq_ref, df_ref, dv_ref, doml_ref, dst_ref = refs

        @pl.when(pl.program_id(0) == 0)
        def _():
            dst_ref[...] = jnp.zeros_like(dst_ref)
            doml_ref[...] = jnp.zeros_like(doml_ref)

        oml = 1.0 - lb_ref[...]
        tri, tri_t, rid, last = _hgrn_consts(reverse)
        rid8 = lax.broadcasted_iota(jnp.int32, (HG_TILE, W), 0)

        def chunk(ci, carry):
            cidx = ci if reverse else (n_c - 1 - ci)
            rows = pl.ds(pl.multiple_of(cidx * C, C), C)
            qr, fr, v, dov = q_ref[rows, :], f_ref[rows, :], v_ref[rows, :], do_ref[rows, :]
            q, k, G, Gl = _hgrn_chunk_common(qr, fr, oml, tri, last)
            e_g, e_gl, e_kd = jnp.exp(G), jnp.exp(Gl), jnp.exp(Gl - G)
            qd, kd = q * e_g, k * e_kd
            do_b, v_b, qd_b, kd_b = dov.astype(BF16), v.astype(BF16), qd.astype(BF16), kd.astype(BF16)
            dqd, dkd, dv, state_dot = [], [], [], []
            for h, sl in enumerate(_head_slices()):
                st0, dst1 = st_in_ref[h, cidx], dst_ref[h]
                dst1_b = dst1.astype(BF16)
                dqd.append(_nn(do_b[:, sl], st0.astype(BF16)))
                dkd.append(_nn(v_b[:, sl], dst1_b))
                dv.append(_nt(kd_b[:, sl], dst1_b))
                state_dot.append(jnp.sum(st0 * dst1, axis=0, keepdims=True))
                dst_ref[h] = dst1 * e_gl[:, sl] + _tn(do_b[:, sl], qd_b[:, sl])
            dqd, dkd, dv = [jnp.concatenate(t, axis=1) for t in (dqd, dkd, dv)]
            d_gl = e_gl * jnp.concatenate(state_dot, axis=1) + jnp.sum(dkd * kd, axis=0, keepdims=True)
            dq, dk = dqd * e_g, dkd * e_kd
            n_tiles = C // HG_TILE
            dq_t, dk_t, dv_t = [[x[t * HG_TILE:(t + 1) * HG_TILE] for t in range(n_tiles)] for x in (dq, dk, dv)]
            for s in range(C):
                k_s, v_s = k[s:s + 1], v[s:s + 1]
                for tile, r in _pair_tiles(s, reverse):
                    rs, e_s = _pair_decay(G, s, tile, r, rid8, reverse)
                    dq_t[tile] = dq_t[tile] + _per_head_lane_sum(dov[rs] * v_s) * e_s * k_s
            for t in range(C):
                q_t, do_t = q[t:t + 1], dov[t:t + 1]
                for tile, r in _pair_tiles(t, not reverse):
                    rs, x_t = _pair_decay(G, t, tile, r, rid8, not reverse, keys=True)
                    dv_t[tile] = dv_t[tile] + _per_head_lane_sum(k[rs] * q_t * x_t) * do_t
                    dk_t[tile] = dk_t[tile] + _per_head_lane_sum(v[rs] * do_t) * x_t * q_t
            dq, dk, dv = [jnp.concatenate(x, axis=0) for x in (dq_t, dk_t, dv_t)]
            d_big_g = dq * q - dk * k + jnp.where(rid == last, d_gl, 0.0)
            dg = _nn(tri_t, d_big_g, precision=lax.Precision.HIGHEST)
            dk_all = dk - dg / (1.0 - k)
            sig_nf = _sigmoid(-fr)
            df_ref[rows, :] = -dk_all * k * (1.0 - sig_nf)
            doml_ref[...] += jnp.sum(dk_all * sig_nf, axis=0, keepdims=True)
            sq = _sigmoid(qr)
            dqr = dq * (sq * (1.0 + qr * (1.0 - sq)))
            if acc:
                dqr = dqr + dqa_ref[rows, :]
                dv = dv + dva_ref[rows, :]
            dq_ref[rows, :] = dqr
            dv_ref[rows, :] = dv
            return carry

        lax.fori_loop(0, n_c, chunk, 0, unroll=2)

    def sec(j):
        return pl.BlockSpec((tb, W), lambda i: (tmap(i), j))

    vec = pl.BlockSpec((1, W), lambda i: (0, 0))
    ins = [z, z, z, lb, do, states]
    in_specs = [sec(0), sec(1 + direction), sec(3), vec, sec(0),
                pl.BlockSpec((HG_HEADS, n_c, DK, DK), lambda i: (0, tmap(i), 0, 0))]
    if acc:
        ins += list(acc)
        in_specs += [sec(0), sec(0)]
    full = jax.ShapeDtypeStruct((S, W), F32)
    return _call(
        body, name=name, grid=(n_t,), in_specs=in_specs,
        out_specs=[sec(0), sec(0), sec(0), vec],
        out_shape=[full, full, full, jax.ShapeDtypeStruct((1, W), F32)],
        scratch_shapes=[pltpu.VMEM((HG_HEADS, DK, DK), F32)], args=ins, semantics=("arbitrary",), exchange=exchange)


def _hgrn_post_fwd(o_f, o_b, z, norm_g, name):
    S = z.shape[0]
    tr = min(ROW_TILE, S)

    def body(of_ref, ob_ref, gr_ref, ng_ref, y_ref):
        o = of_ref[...] + ob_ref[...]
        gr = gr_ref[...]
        gate = gr * _sigmoid(gr)
        ng = ng_ref[...]
        for h in range(HG_HEADS):
            sl = slice(h * HG_DIM, (h + 1) * HG_DIM)
            oh = o[:, sl]
            rstd = lax.rsqrt(jnp.mean(oh * oh, axis=-1, keepdims=True) + EPS)
            y_ref[:, sl] = (oh * rstd * ng[:, sl] * gate[:, sl]).astype(y_ref.dtype)

    row = pl.BlockSpec((tr, HG_WIDTH), lambda i: (i, 0))
    return pl.pallas_call(
        body, name=name, grid=(S // tr,),
        in_specs=[row, row, pl.BlockSpec((tr, HG_WIDTH), lambda i: (i, 4)), pl.BlockSpec((1, HG_WIDTH), lambda i: (0, 0))],
        out_specs=row, out_shape=jax.ShapeDtypeStruct((S, HG_WIDTH), BF16), compiler_params=_params("parallel"),
    )(o_f, o_b, z, norm_g)


def _hgrn_post_bwd(dy, o_f, o_b, z, norm_g, name):
    S = z.shape[0]
    tr = min(ROW_TILE, S)

    def body(dy_ref, of_ref, ob_ref, gr_ref, ng_ref, do_ref, dgr_ref, dng_ref):
        @pl.when(pl.program_id(0) == 0)
        def _():
            dng_ref[...] = jnp.zeros_like(dng_ref)

        o = of_ref[...] + ob_ref[...]
        gr, ng, dyv = gr_ref[...], ng_ref[...], dy_ref[...]
        sg = _sigmoid(gr)
        for h in range(HG_HEADS):
            sl = slice(h * HG_DIM, (h + 1) * HG_DIM)
            oh, dyh, grh, sgh, ngh = o[:, sl], dyv[:, sl], gr[:, sl], sg[:, sl], ng[:, sl]
            rstd = lax.rsqrt(jnp.mean(oh * oh, axis=-1, keepdims=True) + EPS)
            on = oh * rstd
            du = dyh * (grh * sgh)
            dgr_ref[:, sl] = dyh * (on * ngh) * (sgh * (1.0 + grh * (1.0 - sgh)))
            dng_ref[:, sl] += jnp.sum(du * on, axis=0, keepdims=True)
            don = du * ngh
            do_ref[:, sl] = rstd * (don - on * jnp.mean(don * on, axis=-1, keepdims=True))

    row = pl.BlockSpec((tr, HG_WIDTH), lambda i: (i, 0))
    vec = pl.BlockSpec((1, HG_WIDTH), lambda i: (0, 0))
    full = jax.ShapeDtypeStruct((S, HG_WIDTH), F32)
    return pl.pallas_call(
        body, name=name, grid=(S // tr,),
        in_specs=[row, row, row, pl.BlockSpec((tr, HG_WIDTH), lambda i: (i, 4)), vec],
        out_specs=[row, row, vec], out_shape=[full, full, jax.ShapeDtypeStruct((1, HG_WIDTH), F32)],
        compiler_params=_params("arbitrary"),
    )(dy, o_f, o_b, z, norm_g)


def _t5_bucket_table():
    rel = (np.arange(3 * BLOCK)[None, :] - BLOCK) - np.arange(BLOCK)[:, None]
    nb = NUM_BUCKETS // 2
    max_exact = nb // 2
    ret = (rel > 0).astype(np.int32) * nb
    n = np.abs(rel)
    ratio = np.log(np.maximum(n, 1).astype(np.float32) / np.float32(max_exact)) / np.float32(math.log(MAX_DISTANCE / max_exact))
    large = max_exact + (ratio.astype(np.float32) * np.float32(nb - max_exact)).astype(np.int32)
    large = np.minimum(large, nb - 1)
    bucket = ret + np.where(n < max_exact, n, large)
    return bucket.astype(np.int32), (n <= WINDOW)


def _bias_table(rel_bias, name):
    bucket, in_band = _t5_bucket_table()
    idx = jnp.asarray(np.where(in_band, bucket, -1))

    def body(rb_ref, idx_ref, o_ref):
        h = pl.program_id(0)
        iv = idx_ref[...]
        acc = jnp.where(iv < 0, NEG, 0.0).astype(F32)
        for b in range(NUM_BUCKETS):
            acc = acc + jnp.where(iv == b, rb_ref[b, h], 0.0)
        o_ref[...] = acc

    return pl.pallas_call(
        body, name=name, grid=(ATT_Q_HEADS,),
        in_specs=[pl.BlockSpec(memory_space=pltpu.SMEM), pl.BlockSpec((BLOCK, 3 * BLOCK), lambda h: (0, 0))],
        out_specs=pl.BlockSpec((None, BLOCK, 3 * BLOCK), lambda h: (h, 0, 0)),
        out_shape=jax.ShapeDtypeStruct((ATT_Q_HEADS, BLOCK, 3 * BLOCK), F32), compiler_params=_params("parallel"),
    )(rel_bias, idx)


def _bias_grad(ds_sum, name):
    bucket, in_band = _t5_bucket_table()
    idx = jnp.asarray(np.where(in_band, bucket, -1))

    def body(ds_ref, idx_ref, o_ref):
        iv, ds = idx_ref[...], ds_ref[...]
        for b in range(NUM_BUCKETS):
            part = jnp.sum(jnp.where(iv == b, ds, 0.0), axis=0, keepdims=True)
            o_ref[b:b + 1, :] = part[:, 0:BLOCK] + part[:, BLOCK:2 * BLOCK] + part[:, 2 * BLOCK:3 * BLOCK]

    return pl.pallas_call(
        body, name=name, grid=(ATT_Q_HEADS,),
        in_specs=[pl.BlockSpec((None, BLOCK, 3 * BLOCK), lambda h: (h, 0, 0)), pl.BlockSpec((BLOCK, 3 * BLOCK), lambda h: (0, 0))],
        out_specs=pl.BlockSpec((None, NUM_BUCKETS, BLOCK), lambda h: (h, 0, 0)),
        out_shape=jax.ShapeDtypeStruct((ATT_Q_HEADS, NUM_BUCKETS, BLOCK), F32), compiler_params=_params("parallel"),
    )(ds_sum, idx)


def _attn_specs(nb):
    G, dh = ATT_GROUP, ATT_HEAD_DIM
    qspec = pl.BlockSpec((G, BLOCK, dh), lambda j, n: (j, n, 0))

    def kv(shift):
        return pl.BlockSpec((None, BLOCK, dh), lambda j, n: (j, jnp.clip(n + shift, 0, nb - 1), 0))

    gain = pl.BlockSpec((1, dh), lambda j, n: (0, 0))
    sink = pl.BlockSpec((G, 1, BLOCK), lambda j, n: (j, 0, 0))
    bias = pl.BlockSpec((G, BLOCK, 3 * BLOCK), lambda j, n: (j, 0, 0))
    return qspec, kv, gain, sink, bias


def _attn_probs(qh, kn, bias_h, sink_h, edge_ok):
    s = _nt(qh.astype(BF16), kn.astype(BF16)) * (1.0 / math.sqrt(ATT_HEAD_DIM)) + bias_h
    s = jnp.where(edge_ok, s, NEG)
    m = jnp.maximum(jnp.max(s, axis=-1, keepdims=True), sink_h)
    p = jnp.exp(s - m)
    e_sink = jnp.exp(sink_h - m)
    inv = 1.0 / (jnp.sum(p, axis=-1, keepdims=True) + e_sink)
    return p * inv, e_sink * inv


def _rms_rows(x):
    rstd = lax.rsqrt(jnp.mean(x * x, axis=-1, keepdims=True) + EPS)
    return x * rstd, rstd


def _edge_ok(n, nb):
    colid = lax.broadcasted_iota(jnp.int32, (ATT_GROUP * BLOCK, 3 * BLOCK), 1)
    return jnp.logical_and(jnp.logical_or(colid >= BLOCK, n > 0), jnp.logical_or(colid < 2 * BLOCK, n < nb - 1))


def _sink_column(sink_ref):
    return jnp.concatenate([jnp.broadcast_to(sink_ref[g][:, 0:1], (BLOCK, 1)) for g in range(ATT_GROUP)], axis=0)


def _attn_fwd(q, k, v, q_g, k_g, sink, bias, name):
    S = q.shape[1]
    nb = S // BLOCK
    G, dh = ATT_GROUP, ATT_HEAD_DIM
    qspec, kv, gain, sink_spec, bias_spec = _attn_specs(nb)

    def body(q_ref, k0, k1, k2, v0, v1, v2, qg_ref, kg_ref, sink_ref, bias_ref, o_ref):
        n = pl.program_id(1)
        kcat = jnp.concatenate([k0[...], k1[...], k2[...]], axis=0)
        vcat = jnp.concatenate([v0[...], v1[...], v2[...]], axis=0).astype(BF16)
        kn = _rms_rows(kcat)[0] * kg_ref[...]
        qn = _rms_rows(q_ref[...].reshape(G * BLOCK, dh))[0] * qg_ref[...]
        p, _ = _attn_probs(qn, kn, bias_ref[...].reshape(G * BLOCK, 3 * BLOCK), _sink_column(sink_ref), _edge_ok(n, nb))
        o_ref[...] = _nn(p.astype(BF16), vcat).reshape(G, BLOCK, dh)

    return pl.pallas_call(
        body, name=name, grid=(ATT_KV_HEADS, nb),
        in_specs=[qspec, kv(-1), kv(0), kv(1), kv(-1), kv(0), kv(1), gain, gain, sink_spec, bias_spec],
        out_specs=qspec, out_shape=jax.ShapeDtypeStruct(q.shape, F32), compiler_params=_params("parallel", "parallel"),
    )(q, k, k, k, v, v, v, q_g, k_g, sink, bias)


def _attn_bwd(q, k, v, q_g, k_g, sink, bias, do, name):
    S = q.shape[1]
    nb = S // BLOCK
    G, dh = ATT_GROUP, ATT_HEAD_DIM
    scale = 1.0 / math.sqrt(dh)
    qspec, kv, gain, sink_spec, bias_spec = _attn_specs(nb)

    def body(q_ref, k0, k1, k2, v0, v1, v2, qg_ref, kg_ref, sink_ref, bias_ref, do_ref,
             dq_ref, dkw_ref, dvw_ref, ds_ref, dsink_ref, dqg_ref):
        n = pl.program_id(1)

        @pl.when(n == 0)
        def _():
            ds_ref[...] = jnp.zeros_like(ds_ref)
            dsink_ref[...] = jnp.zeros_like(dsink_ref)
            dqg_ref[...] = jnp.zeros_like(dqg_ref)

        kcat = jnp.concatenate([k0[...], k1[...], k2[...]], axis=0)
        vcat = jnp.concatenate([v0[...], v1[...], v2[...]], axis=0).astype(BF16)
        kn = _rms_rows(kcat)[0] * kg_ref[...]
        qg = qg_ref[...]
        qhat, rstd = _rms_rows(q_ref[...].reshape(G * BLOCK, dh))
        qn = qhat * qg
        p, p_sink = _attn_probs(qn, kn, bias_ref[...].reshape(G * BLOCK, 3 * BLOCK), _sink_column(sink_ref), _edge_ok(n, nb))
        do_b = do_ref[...].reshape(G * BLOCK, dh).astype(BF16)
        dp = _nt(do_b, vcat)
        delta = jnp.sum(p * dp, axis=-1, keepdims=True)
        ds = p * (dp - delta)
        ds_ref[...] += ds.reshape(G, BLOCK, 3 * BLOCK)
        sink_term = p_sink * delta
        for g in range(G):
            dsink_ref[g] += jnp.zeros((1, BLOCK), F32) - jnp.sum(sink_term[g * BLOCK:(g + 1) * BLOCK], axis=0, keepdims=True)
        ds_b = ds.astype(BF16)
        dvw_ref[...] = _tn(p.astype(BF16), do_b)
        dkw_ref[...] = _tn(ds_b, qn.astype(BF16)) * scale
        dqn = _nn(ds_b, kn.astype(BF16)) * scale
        dqg_ref[...] += jnp.sum(dqn * qhat, axis=0, keepdims=True)
        dqh = dqn * qg
        dq_ref[...] = (rstd * (dqh - qhat * jnp.mean(dqh * qhat, axis=-1, keepdims=True))).reshape(G, BLOCK, dh)

    win = pl.BlockSpec((None, None, 3 * BLOCK, dh), lambda j, n: (j, n, 0, 0))
    wshape = jax.ShapeDtypeStruct((ATT_KV_HEADS, nb, 3 * BLOCK, dh), F32)
    return pl.pallas_call(
        body, name=name, grid=(ATT_KV_HEADS, nb),
        in_specs=[qspec, kv(-1), kv(0), kv(1), kv(-1), kv(0), kv(1), gain, gain, sink_spec, bias_spec, qspec],
        out_specs=[qspec, win, win, bias_spec, sink_spec, pl.BlockSpec((None, 1, dh), lambda j, n: (j, 0, 0))],
        out_shape=[jax.ShapeDtypeStruct(q.shape, F32), wshape, wshape,
                   jax.ShapeDtypeStruct((ATT_Q_HEADS, BLOCK, 3 * BLOCK), F32),
                   jax.ShapeDtypeStruct((ATT_Q_HEADS, 1, BLOCK), F32),
                   jax.ShapeDtypeStruct((ATT_KV_HEADS, 1, dh), F32)],
        compiler_params=_params("parallel", "arbitrary"),
    )(q, k, k, k, v, v, v, q_g, k_g, sink, bias, do)


def _attn_kv_reduce(dkw, dvw, k, k_g, name):
    S = k.shape[1]
    nb = S // BLOCK
    dh = ATT_HEAD_DIM
    kb = min(8, nb)
    steps = nb // kb

    def body(a_lo, a, a_hi, b_lo, b, b_hi, k_ref, kg_ref, dk_ref, dv_ref, dkg_ref):
        n = pl.program_id(1)

        @pl.when(n == 0)
        def _():
            dkg_ref[...] = jnp.zeros_like(dkg_ref)

        lo = jnp.where(n > 0, 1.0, 0.0)
        hi = jnp.where(n < steps - 1, 1.0, 0.0)

        def overlap_add(w, w_lo, w_hi, i):
            before = lo * w_lo[...] if i == 0 else w[i - 1, 2 * BLOCK:3 * BLOCK, :]
            after = hi * w_hi[...] if i == kb - 1 else w[i + 1, 0:BLOCK, :]
            return w[i, BLOCK:2 * BLOCK, :] + before + after

        dkg = jnp.zeros((1, dh), F32)
        for i in range(kb):
            rows = slice(i * BLOCK, (i + 1) * BLOCK)
            dkn = overlap_add(a, a_lo, a_hi, i)
            dv_ref[rows, :] = overlap_add(b, b_lo, b_hi, i)
            khat, rstd = _rms_rows(k_ref[rows, :])
            dkg = dkg + jnp.sum(dkn * khat, axis=0, keepdims=True)
            dkh = dkn * kg_ref[...]
            dk_ref[rows, :] = rstd * (dkh - khat * jnp.mean(dkh * khat, axis=-1, keepdims=True))
        dkg_ref[...] += dkg

    main = pl.BlockSpec((None, kb, 3 * BLOCK, dh), lambda j, n: (j, n, 0, 0))
    halo_lo = pl.BlockSpec((None, None, BLOCK, dh), lambda j, n: (j, jnp.maximum(n * kb - 1, 0), 2, 0))
    halo_hi = pl.BlockSpec((None, None, BLOCK, dh), lambda j, n: (j, jnp.minimum(n * kb + kb, nb - 1), 0, 0))
    blk = pl.BlockSpec((None, kb * BLOCK, dh), lambda j, n: (j, n, 0))
    return pl.pallas_call(
        body, name=name, grid=(ATT_KV_HEADS, steps),
        in_specs=[halo_lo, main, halo_hi, halo_lo, main, halo_hi, blk, pl.BlockSpec((1, dh), lambda j, n: (0, 0))],
        out_specs=[blk, blk, pl.BlockSpec((None, 1, dh), lambda j, n: (j, 0, 0))],
        out_shape=[jax.ShapeDtypeStruct(k.shape, F32), jax.ShapeDtypeStruct(k.shape, F32),
                   jax.ShapeDtypeStruct((ATT_KV_HEADS, 1, dh), F32)],
        compiler_params=_params("parallel", "arbitrary"),
    )(dkw, dkw, dkw, dvw, dvw, dvw, k, k_g)


def _ada_fwd(c_act, w, b, name):
    n = w.shape[1]

    def body(c_ref, w_ref, b_ref, o_ref):
        o_ref[...] = _nn(c_ref[...], w_ref[...], precision=lax.Precision.HIGHEST) + b_ref[...]

    tn = n // 3
    return pl.pallas_call(
        body, name=name, grid=(3,),
        in_specs=[pl.BlockSpec(c_act.shape, lambda j: (0, 0)), pl.BlockSpec((w.shape[0], tn), lambda j: (0, j)),
                  pl.BlockSpec((1, tn), lambda j: (0, j))],
        out_specs=pl.BlockSpec((c_act.shape[0], tn), lambda j: (0, j)),
        out_shape=jax.ShapeDtypeStruct((c_act.shape[0], n), F32), compiler_params=_params("parallel"),
    )(c_act, w, b)


def _ada_wgrad(c_act_t, dm, name):
    D, nbatch = c_act_t.shape
    n = dm.shape[1]
    tr = 256

    def body(c_ref, dm_ref, o_ref):
        cv, dv = c_ref[...], dm_ref[...]
        acc = cv[:, 0:1] * dv[0:1, :]
        for b in range(1, nbatch):
            acc = acc + cv[:, b:b + 1] * dv[b:b + 1, :]
        o_ref[...] = acc

    return pl.pallas_call(
        body, name=name, grid=(D // tr,),
        in_specs=[pl.BlockSpec((tr, nbatch), lambda i: (i, 0)), pl.BlockSpec((nbatch, n), lambda i: (0, 0))],
        out_specs=pl.BlockSpec((tr, n), lambda i: (i, 0)), out_shape=jax.ShapeDtypeStruct((D, n), F32),
        compiler_params=_params("parallel"),
    )(c_act_t, dm)


def _adamw(w, g, m, v, name):
    R, Cn = w.shape
    tr = R
    for cand in (256, 128, 64, 32, 16, 8):
        if R % cand == 0:
            tr = cand
            break

    def body(w_ref, g_ref, m_ref, v_ref, d_ref, nm_ref, nv_ref):
        gv = g_ref[...]
        m_new = ADAM_B1 * m_ref[...] + (1.0 - ADAM_B1) * gv
        v_new = ADAM_B2 * v_ref[...] + (1.0 - ADAM_B2) * (gv * gv)
        m_hat = m_new / (1.0 - ADAM_B1 ** ADAM_STEP)
        v_hat = v_new / (1.0 - ADAM_B2 ** ADAM_STEP)
        d_ref[...] = -ADAM_LR * (m_hat / (jnp.sqrt(v_hat) + ADAM_EPS) + ADAM_WD * w_ref[...])
        nm_ref[...] = m_new
        nv_ref[...] = v_new

    blk = pl.BlockSpec((tr, Cn), lambda i: (i, 0))
    shp = jax.ShapeDtypeStruct((R, Cn), F32)
    return pl.pallas_call(
        body, name=name, grid=(R // tr,), in_specs=[blk] * 4, out_specs=[blk] * 3, out_shape=[shp] * 3,
        compiler_params=_params("parallel"),
    )(w, g, m, v)


def _place():
    return lax.axis_index("x"), lax.axis_index("y"), lax.axis_index("c")


def _flip(place, k):
    x, y, c = place
    return (1 - x if k & 4 else x, 1 - y if k & 2 else y, 1 - c if k & 1 else c)


def _dev_index(place):
    x, y, c = place
    return 4 * x + 2 * y + c


def _chip_index(place):
    return 2 * place[0] + place[1]


def _allgather8(x, name, reduce=False):
    R, Cn = x.shape

    def body(x_ref, *rest):
        if reduce:
            out_ref, sum_ref, send_sems, recv_sems, local_sem = rest
        else:
            out_ref, send_sems, recv_sems, local_sem = rest
        me = _place()
        mine = pltpu.make_async_copy(x_ref, out_ref.at[_dev_index(me)], local_sem)
        mine.start()

        def copy(k, origin, to):
            return pltpu.make_async_remote_copy(
                src_ref=x_ref, dst_ref=out_ref.at[_dev_index(origin)], send_sem=send_sems.at[k - 1],
                recv_sem=recv_sems.at[k - 1], device_id=to, device_id_type=MESH)

        sends = [copy(k, me, _flip(me, k)) for k in range(1, 8)]
        for cp in sends:
            cp.start()
        for k in range(1, 8):
            copy(k, _flip(me, k), me).wait_recv()
        for cp in sends:
            cp.wait_send()
        mine.wait()
        if reduce:
            acc = out_ref[0]
            for i in range(1, 8):
                acc = acc + out_ref[i]
            sum_ref[...] = acc

    vm = pl.BlockSpec(memory_space=pltpu.VMEM)
    outs = [jax.ShapeDtypeStruct((8, R, Cn), F32)] + ([jax.ShapeDtypeStruct((R, Cn), F32)] if reduce else [])
    res = pl.pallas_call(
        body, name=name, in_specs=[vm], out_specs=[vm] * len(outs), out_shape=outs,
        scratch_shapes=[pltpu.SemaphoreType.DMA((7,)), pltpu.SemaphoreType.DMA((7,)), pltpu.SemaphoreType.DMA],
    )(x)
    return res if reduce else res[0]


def _weights_allgather(shards, name):
    n = len(shards)
    per = 8

    def body(*refs):
        in_refs, out_refs = refs[:n], refs[n:2 * n]
        send_sems, recv_sems = refs[2 * n:]
        me = _place()
        c = me[2]
        sibling = _flip(me, 1)
        others = [_flip(me, 2 * j) for j in (1, 2, 3)]

        def copy(a, k, src, dst, to):
            return pltpu.make_async_remote_copy(
                src_ref=src, dst_ref=dst, send_sem=send_sems.at[per * a + k], recv_sem=recv_sems.at[per * a + k],
                device_id=to, device_id_type=MESH)

        def block(a, place, half):
            return out_refs[a].at[_chip_index(place), half]

        started = []
        for a in range(n):
            sends = [copy(a, 0, in_refs[a].at[c], block(a, me, c), sibling),
                     copy(a, 7, in_refs[a].at[1 - c], block(a, me, 1 - c), sibling)]
            sends += [copy(a, 1 + j, in_refs[a].at[c], block(a, me, c), to) for j, to in enumerate(others)]
            for cp in sends:
                cp.start()
            started += sends
        for a in range(n):
            for j, other in enumerate(others):
                landed = block(a, other, c)
                copy(a, 1 + j, landed, landed, me).wait_recv()
                fwd = copy(a, 4 + j, landed, landed, sibling)
                fwd.start()
                started.append(fwd)
        for a in range(n):
            copy(a, 0, block(a, me, 1 - c), block(a, me, 1 - c), me).wait_recv()
            copy(a, 7, block(a, me, c), block(a, me, c), me).wait_recv()
            for j, other in enumerate(others):
                got = block(a, other, 1 - c)
                copy(a, 4 + j, got, got, me).wait_recv()
        for cp in started:
            cp.wait_send()

    return pl.pallas_call(
        body, name=name, in_specs=[ANY] * n, out_specs=[ANY] * n,
        out_shape=[jax.ShapeDtypeStruct((N_CHIPS,) + s.shape, s.dtype) for s in shards],
        scratch_shapes=[pltpu.SemaphoreType.DMA((per * n,)), pltpu.SemaphoreType.DMA((per * n,))],
    )(*shards)


def _remote(src, dst, send_sems, recv_sems, i, to):
    return pltpu.make_async_remote_copy(
        src_ref=src, dst_ref=dst, send_sem=send_sems.at[i], recv_sem=recv_sems.at[i], device_id=to, device_id_type=MESH)


def _symmetric_plan(copies):
    def plan(in_refs, out_refs, send_sems, recv_sems):
        sends = [_remote(src, dst, send_sems, recv_sems, i, to) for i, (src, dst, to) in enumerate(copies(in_refs, out_refs))]
        return sends, sends
    return plan


def _halves_exchange(grads):
    def copies(in_refs, out_refs):
        me = _place()
        return [(g.at[kk, 1 - me[2]], got.at[kk], _flip(me, 1)) for g, got in zip(in_refs, out_refs) for kk in range(N_CHIPS)]

    return _Exchange(grads, [jax.ShapeDtypeStruct((N_CHIPS,) + g.shape[2:], g.dtype) for g in grads],
                     N_CHIPS * len(grads), _symmetric_plan(copies))


def _chips_exchange(parts):
    def copies(in_refs, out_refs):
        me = _place()
        return [(p.at[_chip_index(_flip(me, 2 * j))], got.at[j - 1], _flip(me, 2 * j))
                for p, got in zip(in_refs, out_refs) for j in (1, 2, 3)]

    return _Exchange(parts, [jax.ShapeDtypeStruct((3,) + p.shape[1:], p.dtype) for p in parts], 3 * len(parts),
                     _symmetric_plan(copies))


def _siblings_exchange(halves):
    def copies(in_refs, out_refs):
        sibling = _flip(_place(), 1)
        return [(h, got, sibling) for h, got in zip(in_refs, out_refs)]

    return _Exchange(halves, [jax.ShapeDtypeStruct(h.shape, h.dtype) for h in halves], len(halves), _symmetric_plan(copies))


def _gather_over_ici(shards):
    def copies(in_refs, out_refs):
        me = _place()
        c = me[2]
        return [(w.at[c], out.at[_chip_index(me), c], _flip(me, 2 * j)) for w, out in zip(in_refs, out_refs) for j in (1, 2, 3)]

    def plan(in_refs, out_refs, send_sems, recv_sems):
        me = _place()
        sends = [_remote(src, dst, send_sems, recv_sems, i, to) for i, (src, dst, to) in enumerate(copies(in_refs, out_refs))]
        lands = [out.at[_chip_index(_flip(me, 2 * j)), me[2]] for out in out_refs for j in (1, 2, 3)]
        return sends, [_remote(z, z, send_sems, recv_sems, i, me) for i, z in enumerate(lands)]

    return _Exchange(shards, [jax.ShapeDtypeStruct((N_CHIPS,) + s.shape, s.dtype) for s in shards], 3 * len(shards), plan)


def _gather_over_d2d(shards, gathered):
    n = len(shards)

    def plan(in_refs, out_refs, send_sems, recv_sems):
        me = _place()
        c = me[2]
        sibling = _flip(me, 1)
        mine = _chip_index(me)
        sends, recvs = [], []
        for a, (w, out) in enumerate(zip(in_refs[:n], out_refs)):
            moves = [(w.at[c], (mine, c)), (w.at[1 - c], (mine, 1 - c))]
            moves += [(out.at[_chip_index(_flip(me, 2 * j)), c], (_chip_index(_flip(me, 2 * j)), c)) for j in (1, 2, 3)]
            for k, (src, (chip, half)) in enumerate(moves):
                sends.append(_remote(src, out.at[chip, half], send_sems, recv_sems, 5 * a + k, sibling))
            lands = [(mine, 1 - c), (mine, c)] + [(_chip_index(_flip(me, 2 * j)), 1 - c) for j in (1, 2, 3)]
            for k, (chip, half) in enumerate(lands):
                z = out.at[chip, half]
                recvs.append(_remote(z, z, send_sems, recv_sems, 5 * a + k, me))
        return sends, recvs

    return _Exchange(list(shards) + list(gathered), [jax.ShapeDtypeStruct(g.shape, g.dtype) for g in gathered], 5 * n, plan,
                     aliases={n + a: a for a in range(n)})


def _row_tile(rows):
    for cand in (256, 176, 128, 64, 32, 16, 8):
        if rows % cand == 0:
            return cand
    return rows


def _pair_sum(core, grad, theirs, name):
    N, _, R, Cn = grad.shape
    tr = _row_tile(R)

    def body(core_ref, g_ref, t_ref, o_ref, ob_ref):
        s = g_ref[...] + t_ref[...]
        o_ref[...] = s
        ob_ref[...] = s.astype(BF16)

    out = pl.BlockSpec((None, tr, Cn), lambda k, i, core_ref: (k, i, 0))
    return pl.pallas_call(
        body, name=name,
        grid_spec=pltpu.PrefetchScalarGridSpec(
            num_scalar_prefetch=1, grid=(N, R // tr),
            in_specs=[pl.BlockSpec((None, None, tr, Cn), lambda k, i, core_ref: (k, core_ref[0], i, 0)),
                      pl.BlockSpec((None, tr, Cn), lambda k, i, core_ref: (k, i, 0))],
            out_specs=[out, out]),
        out_shape=[jax.ShapeDtypeStruct((N, R, Cn), F32), jax.ShapeDtypeStruct((N, R, Cn), BF16)],
        compiler_params=_params("parallel", "parallel"),
    )(core, grad, theirs)


def _chip_sum(chip, parts, landed, name):
    _, R, Cn = parts.shape
    tr = _row_tile(R)

    def body(chip_ref, p_ref, l_ref, o_ref):
        o_ref[...] = ((p_ref[...] + l_ref[0].astype(F32)) + l_ref[1].astype(F32)) + l_ref[2].astype(F32)

    return pl.pallas_call(
        body, name=name,
        grid_spec=pltpu.PrefetchScalarGridSpec(
            num_scalar_prefetch=1, grid=(R // tr,),
            in_specs=[pl.BlockSpec((None, tr, Cn), lambda i, chip_ref: (chip_ref[0], i, 0)),
                      pl.BlockSpec((3, tr, Cn), lambda i, chip_ref: (0, i, 0))],
            out_specs=pl.BlockSpec((tr, Cn), lambda i, chip_ref: (i, 0))),
        out_shape=jax.ShapeDtypeStruct((R, Cn), F32), compiler_params=_params("parallel"),
    )(chip, parts, landed)


def _pair_sums(core, grads, theirs, tag):
    return [_pair_sum(core, g, t, f"{tag}_pair_sum_{i}") for i, (g, t) in enumerate(zip(grads, theirs))]


def _chip_sums(chip, parts, landed, tag):
    return [_chip_sum(chip, p[0], l, f"{tag}_chip_sum_{i}") for i, (p, l) in enumerate(zip(parts, landed))]


def _by_chip_rows(g):
    return g.reshape(N_CHIPS, 2, g.shape[0] // (2 * N_CHIPS), g.shape[1])


def _by_chip_cols(g):
    return g.reshape(N_CHIPS, 2, g.shape[1] // 2, g.shape[2])


def _adamw_halves(core, w, g_mine, g_theirs, m, v, name):
    R2, Cn = w.shape
    r = R2 // 2
    tr = _row_tile(r)
    nt = r // tr

    def body(core_ref, w_ref, gm_ref, gt_ref, m_ref, v_ref, g_ref, d_ref, nm_ref, nv_ref):
        gv = jnp.where(pl.program_id(0) == core_ref[0], gm_ref[...], gt_ref[...])
        g_ref[...] = gv
        m_new = ADAM_B1 * m_ref[...] + (1.0 - ADAM_B1) * gv
        v_new = ADAM_B2 * v_ref[...] + (1.0 - ADAM_B2) * (gv * gv)
        m_hat = m_new / (1.0 - ADAM_B1 ** ADAM_STEP)
        v_hat = v_new / (1.0 - ADAM_B2 ** ADAM_STEP)
        d_ref[...] = -ADAM_LR * (m_hat / (jnp.sqrt(v_hat) + ADAM_EPS) + ADAM_WD * w_ref[...])
        nm_ref[...] = m_new
        nv_ref[...] = v_new

    full = pl.BlockSpec((tr, Cn), lambda hf, i, core_ref: (hf * nt + i, 0))
    half = pl.BlockSpec((tr, Cn), lambda hf, i, core_ref: (i, 0))
    shp = jax.ShapeDtypeStruct((R2, Cn), F32)
    return pl.pallas_call(
        body, name=name,
        grid_spec=pltpu.PrefetchScalarGridSpec(
            num_scalar_prefetch=1, grid=(2, nt), in_specs=[full, half, half, full, full], out_specs=[full] * 4),
        out_shape=[shp] * 4, compiler_params=_params("parallel", "parallel"),
    )(core, w, g_mine, g_theirs, m, v)


def _pad_row(v, width):
    v = v.reshape(1, -1)
    return jnp.pad(v, ((0, 0), (0, width - v.shape[1])))


def _ffn_forward(x, ng, shift, scale, gate, w_in4, w_out, tag, gather=None):
    h = _rmsmod_fwd(x, ng, shift, scale, f"{tag}_norm")
    gathered = None
    if gather:
        (zg, zu, a), partly = _ffn_in_fwd(h, w_in4, f"{tag}_in", exchange=_gather_over_ici(gather))
        (x_new, f), gathered = _proj_out_fwd([a], w_out, x, gate, 0.5, f"{tag}_out", exchange=_gather_over_d2d(gather, partly))
    else:
        zg, zu, a = _ffn_in_fwd(h, w_in4, f"{tag}_in")
        x_new, f = _proj_out_fwd([a], w_out, x, gate, 0.5, f"{tag}_out")
    return x_new, (h, zg, zu, a, f), gathered


def _ffn_backward(df, saved, w_in4, w_out, core, tag, riding=None):
    h, zg, zu, a, _ = saved
    rode = None
    if riding:
        (dzg, dzu), rode = _dact_bwd(df, w_out, zg, zu, f"{tag}_dact", exchange=riding)
    else:
        dzg, dzu = _dact_bwd(df, w_out, zg, zu, f"{tag}_dact")
    dw_out = _wgrad(a, df, 512, f"{tag}_dw_out")
    dw_out = jnp.concatenate([dw_out[0], dw_out[1]], axis=1)
    dw_in = jnp.concatenate([_wgrad(h, dzg, FF_SHARD, f"{tag}_dw_gate"), _wgrad(h, dzu, FF_SHARD, f"{tag}_dw_up")], axis=0)
    grads = [_by_chip_cols(dw_in), _by_chip_rows(dw_out)]
    (dh,), theirs = _ffn_in_dgrad(dzg, dzu, w_in4, f"{tag}_dh", exchange=_halves_exchange(grads))
    return dh, _pair_sums(core, grads, theirs, tag), rode


def kernel(x, c, w_ada, b_ada, norm_g, w_ffn1_in, w_ffn1_out, w_ffn2_in, w_ffn2_out, w_mix_in, w_mix_out, hgrn_lb, hgrn_norm_g, qk_norm_g, attn_sink, rel_bias, loss_target, m_w_ada, m_b_ada, m_norm_g, m_w_ffn1_in, m_w_ffn1_out, m_w_ffn2_in, m_w_ffn2_out, m_w_mix_in, m_w_mix_out, m_hgrn_lb, m_hgrn_norm_g, m_qk_norm_g, m_attn_sink, m_rel_bias, v_w_ada, v_b_ada, v_norm_g, v_w_ffn1_in, v_w_ffn1_out, v_w_ffn2_in, v_w_ffn2_out, v_w_mix_in, v_w_mix_out, v_hgrn_lb, v_hgrn_norm_g, v_qk_norm_g, v_attn_sink, v_rel_bias):
    D = D_MODEL
    S = x.shape[1]
    place = (lax.axis_index("x"), lax.axis_index("y"), lax.axis_index("c"))
    me, my_chip = _dev_index(place), _chip_index(place)
    x0 = x[0]
    target = loss_target[0]

    def halves(w):
        return w.astype(BF16).reshape(2, w.shape[0] // 2, w.shape[1])

    gathered = _weights_allgather([halves(w_ffn1_in[0]), halves(w_ffn1_out[0])], "weights_allgather")
    w1_in = gathered[0].reshape(N_CHIPS, D, FF_SHARD)
    w1_out = gathered[1].reshape(D_FF, D)
    later = [halves(w_mix_in[0]), halves(w_mix_out[0]), halves(w_ffn2_in[0]), halves(w_ffn2_out[0])]
    core_arr = jnp.reshape(place[2], (1,)).astype(jnp.int32)
    chip_arr = jnp.reshape(my_chip, (1,)).astype(jnp.int32)

    small = jnp.concatenate([_pad_row(c, D), _pad_row(norm_g, D), _pad_row(hgrn_lb, D), jnp.zeros((5, D), F32)], axis=0)
    small_all = _allgather8(small, "small_allgather")
    c_all = small_all[:, 0, :]
    by_chip = small_all[0::2]
    norm_g_full = by_chip[:, 1, :3 * 256].reshape(N_CHIPS, 3, 256).transpose(1, 0, 2).reshape(3, D)
    lb_raw = by_chip[:, 2, :2 * 2 * 128].reshape(N_CHIPS, 2, 2, 128).transpose(1, 2, 0, 3).reshape(2, 2, HG_WIDTH)
    lb = jax.nn.sigmoid(lb_raw[:, 0, :] - lb_raw[:, 1, :])
    lb_f, lb_b = lb[0:1], lb[1:2]

    c_act_all = c_all * jax.nn.sigmoid(c_all)
    n_ada = w_ada.shape[2]
    b_mine = lax.dynamic_slice_in_dim(b_ada, my_chip * n_ada, n_ada, axis=1)
    mods_part = _ada_fwd(c_act_all, w_ada[0], b_mine, "ada_fwd")
    mods_all = _allgather8(mods_part, "mods_allgather")[0::2].transpose(1, 0, 2).reshape(8, N_MOD * D)
    mods = lax.dynamic_slice_in_dim(mods_all, me, 1, axis=0)
    sh1, sc1, g1, sh2, sc2, g2, sh3, sc3, g3 = [mods[:, i * D:(i + 1) * D] for i in range(N_MOD)]

    x1, saved1, gathered = _ffn_forward(x0, norm_g_full[0:1], sh1, sc1, g1, w1_in, w1_out, "ffn1", gather=later)
    wm_in = gathered[0].reshape(N_CHIPS, D, D_IN // N_CHIPS).transpose(1, 0, 2).reshape(D, D_IN)
    wm_out = gathered[1].reshape(D, D)
    w2_in = gathered[2].reshape(N_CHIPS, D, FF_SHARD)
    w2_out = gathered[3].reshape(D_FF, D)

    h2 = _rmsmod_fwd(x1, norm_g_full[1:2], sh2, sc2, "mix_norm")
    z = _matmul_nn(h2, wm_in, F32, 256, "mix_in")
    of, st_f = _hgrn_fwd(z, lb_f, 0, "hgrn_fwd_f")
    ob, st_b = _hgrn_fwd(z, lb_b, 1, "hgrn_fwd_b")
    o_h = _hgrn_post_fwd(of, ob, z, hgrn_norm_g, "hgrn_post")

    def to_heads(t, nh):
        return t.reshape(S, nh, ATT_HEAD_DIM).transpose(1, 0, 2)

    aq = to_heads(z[:, 5 * HG_WIDTH:5 * HG_WIDTH + ATT_WIDTH], ATT_Q_HEADS)
    ak = to_heads(z[:, 5 * HG_WIDTH + ATT_WIDTH:5 * HG_WIDTH + ATT_WIDTH + KV_WIDTH], ATT_KV_HEADS)
    av = to_heads(z[:, 5 * HG_WIDTH + ATT_WIDTH + KV_WIDTH:], ATT_KV_HEADS)
    q_g, k_g = qk_norm_g[0, 0:1], qk_norm_g[0, 1:2]
    sink_b = jnp.broadcast_to(attn_sink.reshape(ATT_Q_HEADS, 1, 1), (ATT_Q_HEADS, 1, BLOCK))
    bias = _bias_table(rel_bias, "bias_table")
    o_attn = _attn_fwd(aq, ak, av, q_g, k_g, sink_b, bias, "attn_fwd")
    o_a = o_attn.transpose(1, 0, 2).reshape(S, ATT_WIDTH).astype(BF16)
    x2, mixed = _proj_out_fwd([o_h, o_a], wm_out, x1, g2, 1.0, "mix_out")

    x3, saved3, _ = _ffn_forward(x2, norm_g_full[2:3], sh3, sc3, g3, w2_in, w2_out, "ffn2")

    dx3, df3, dg3, sq_cols = _loss_bwd(x3, target, saved3[4], g3, 0.5, "loss")
    loss_mine = 0.5 * jnp.sum(sq_cols) / D

    dh3, parts2, _ = _ffn_backward(df3, saved3, w2_in, w2_out, core_arr, "ffn2")
    dx2, dsh3, dsc3, dng3, dmixed, dg2 = _rmsmod_bwd(dh3, x2, norm_g_full[2:3], sc3, dx3, "ffn2_norm_bwd", below=(mixed, g2, 1.0))

    (do_cat,) = _matmul_nt(dmixed, wm_out, ROW_TILE, "mix_out_dgrad")
    dwm_out = jnp.concatenate([_wgrad(o_h, dmixed, 512, "mix_out_dw_h"), _wgrad(o_a, dmixed, 512, "mix_out_dw_a")], axis=1)
    dwm_out = jnp.concatenate([dwm_out[0], dwm_out[1]], axis=1)

    do_sum, dgr, d_hnorm = _hgrn_post_bwd(do_cat, of, ob, z, hgrn_norm_g, "hgrn_post_bwd")
    (dq_f, dff, dv_f, doml_f), landed2 = _hgrn_bwd(z, lb_f, do_sum, st_f, 0, "hgrn_bwd_f",
                                                   exchange=_chips_exchange([p[1] for p in parts2]))
    mine2 = _chip_sums(chip_arr, parts2, landed2, "ffn2")
    (dhq, dfb, dhi, doml_b), theirs2 = _hgrn_bwd(z, lb_b, do_sum, st_b, 1, "hgrn_bwd_b", acc=(dq_f, dv_f),
                                                 exchange=_siblings_exchange(mine2))

    do_a = to_heads(do_cat[:, HG_WIDTH:], ATT_Q_HEADS)
    daq, dkw, dvw, ds_sum, dsink, dqg = _attn_bwd(aq, ak, av, q_g, k_g, sink_b, bias, do_a, "attn_bwd")
    dak, dav, dkg = _attn_kv_reduce(dkw, dvw, ak, k_g, "attn_kv_reduce")
    d_rel_bias = jnp.sum(_bias_grad(ds_sum, "bias_grad"), axis=-1).T

    def from_heads(t):
        return t.transpose(1, 0, 2).reshape(S, -1)

    dz = jnp.concatenate([dhq, dff, dfb, dhi, dgr, from_heads(daq), from_heads(dak), from_heads(dav)], axis=1).astype(BF16)
    dwm_in = _wgrad(h2, dz, D_IN // 2, "mix_in_dw")
    dwm_in = jnp.concatenate([dwm_in[0], dwm_in[1]], axis=1)
    wide = D_IN // N_CHIPS
    grads_m = [_by_chip_cols(dwm_in.reshape(D, N_CHIPS, wide).transpose(1, 0, 2)), _by_chip_rows(dwm_out)]
    (dh2,), theirs_m = _matmul_nt(dz, wm_in, 256, "mix_in_dgrad", exchange=_halves_exchange(grads_m))
    parts_m = _pair_sums(core_arr, grads_m, theirs_m, "mix")
    dx1, dsh2, dsc2, dng2, df1, dg1 = _rmsmod_bwd(dh2, x1, norm_g_full[1:2], sc2, dx2, "mix_norm_bwd", below=(saved1[4], g1, 0.5))

    dh1, parts1, landed_m = _ffn_backward(df1, saved1, w1_in, w1_out, core_arr, "ffn1",
                                          riding=_chips_exchange([p[1] for p in parts_m]))
    mine_m = _chip_sums(chip_arr, parts_m, landed_m, "mix")
    dx0, dsh1, dsc1, dng1 = _rmsmod_bwd(dh1, x0, norm_g_full[0:1], sc1, dx1, "ffn1_norm_bwd")
    landed1 = _run_exchange(_chips_exchange([p[1] for p in parts1]), "ffn1_chips_exchange")
    mine1 = _chip_sums(chip_arr, parts1, landed1, "ffn1")
    theirs_1m = list(_run_exchange(_siblings_exchange(mine1 + mine_m), "siblings_exchange"))
    reduced = list(zip(mine1 + mine2 + mine_m, theirs_1m[:2] + list(theirs2) + theirs_1m[2:]))

    dlb = -jnp.concatenate([doml_f, doml_b], axis=0)
    dlb_raw = dlb * lb * (1.0 - lb)
    d_hgrn_lb = jnp.stack([dlb_raw, -dlb_raw], axis=1)
    d_qk = jnp.concatenate([jnp.sum(dqg, axis=0), jnp.sum(dkg, axis=0)], axis=0)
    dmods = jnp.concatenate([dsh1, dsc1, dg1, dsh2, dsc2, dg2, dsh3, dsc3, dg3], axis=0)
    packed = jnp.concatenate(
        [dmods, dng1, dng2, dng3, d_hgrn_lb.reshape(2, D), _pad_row(d_hnorm, D), _pad_row(d_qk, D),
         _pad_row(dsink[:, 0, 0], D), _pad_row(d_rel_bias, D), _pad_row(loss_mine, D)], axis=0)
    packed = jnp.pad(packed, ((0, 24 - packed.shape[0]), (0, 0)))
    packed_all, packed_sum = _allgather8(packed, "small_grads_allgather", reduce=True)
    dmods_all = packed_all[:, 0:N_MOD, :].reshape(8, N_MOD * D)
    g_b_ada = packed_sum[0:N_MOD].reshape(1, N_MOD * D)
    g_norm_full = packed_sum[9:12]
    g_norm_g = lax.dynamic_slice_in_dim(g_norm_full, my_chip * 256, 256, axis=1).reshape(1, 3, 256)
    g_hgrn_lb = lax.dynamic_slice_in_dim(packed_sum[12:14].reshape(2, 2, HG_WIDTH), my_chip * 128, 128, axis=2)
    g_hgrn_norm_g = packed_sum[14:15, :HG_WIDTH]
    g_qk_norm_g = packed_sum[15, :2 * ATT_HEAD_DIM].reshape(1, 2, ATT_HEAD_DIM)
    g_attn_sink = packed_sum[16:17, :ATT_Q_HEADS]
    g_rel_bias = packed_sum[17, :NUM_BUCKETS * ATT_Q_HEADS].reshape(NUM_BUCKETS, ATT_Q_HEADS)
    loss = packed_sum[18, 0]

    dm_mine = lax.dynamic_slice_in_dim(dmods_all, my_chip * n_ada, n_ada, axis=1)
    g_w_ada = _ada_wgrad(c_act_all.T, dm_mine, "ada_wgrad")[None]

    def big(w, g, m, v, name):
        d, nm, nv = _adamw(w[0], g[0], m[0], v[0], name)
        return d[None], nm[None], nv[None]

    def big_halves(w, g_pair, m, v, name):
        g, d, nm, nv = _adamw_halves(core_arr, w[0], g_pair[0], g_pair[1], m[0], v[0], name)
        return g[None], (d[None], nm[None], nv[None])

    g_w1_in, u_w1_in = big_halves(w_ffn1_in, reduced[0], m_w_ffn1_in, v_w_ffn1_in, "adamw_w_ffn1_in")
    g_w1_out, u_w1_out = big_halves(w_ffn1_out, reduced[1], m_w_ffn1_out, v_w_ffn1_out, "adamw_w_ffn1_out")
    g_w2_in, u_w2_in = big_halves(w_ffn2_in, reduced[2], m_w_ffn2_in, v_w_ffn2_in, "adamw_w_ffn2_in")
    g_w2_out, u_w2_out = big_halves(w_ffn2_out, reduced[3], m_w_ffn2_out, v_w_ffn2_out, "adamw_w_ffn2_out")
    g_wm_in, u_wm_in = big_halves(w_mix_in, reduced[4], m_w_mix_in, v_w_mix_in, "adamw_w_mix_in")
    g_wm_out, u_wm_out = big_halves(w_mix_out, reduced[5], m_w_mix_out, v_w_mix_out, "adamw_w_mix_out")

    smalls = [(b_ada, g_b_ada, m_b_ada, v_b_ada), (norm_g, g_norm_g, m_norm_g, v_norm_g), (hgrn_lb, g_hgrn_lb, m_hgrn_lb, v_hgrn_lb),
              (hgrn_norm_g, g_hgrn_norm_g, m_hgrn_norm_g, v_hgrn_norm_g), (qk_norm_g, g_qk_norm_g, m_qk_norm_g, v_qk_norm_g),
              (attn_sink, g_attn_sink, m_attn_sink, v_attn_sink), (rel_bias, g_rel_bias, m_rel_bias, v_rel_bias)]
    sizes = [t[0].size for t in smalls]
    total = sum(sizes)
    rows = -(-total // 128)
    rows = -(-rows // 8) * 8

    def pack(i):
        flat = jnp.concatenate([t[i].reshape(-1) for t in smalls])
        fill = 1.0 if i == 3 else 0.0
        return jnp.pad(flat, (0, rows * 128 - total), constant_values=fill).reshape(rows, 128)

    packed_out = _adamw(pack(0), pack(1), pack(2), pack(3), "adamw_small")

    def unpack(flat2d):
        flat = flat2d.reshape(-1)
        outs, off = [], 0
        for t, n in zip(smalls, sizes):
            outs.append(flat[off:off + n].reshape(t[0].shape))
            off += n
        return outs

    d_small, m_small, v_small = [unpack(t) for t in packed_out]

    upd = {
        "w_ada": big(w_ada, g_w_ada, m_w_ada, v_w_ada, "adamw_w_ada"),
        "w_ffn1_in": u_w1_in, "w_ffn1_out": u_w1_out, "w_ffn2_in": u_w2_in, "w_ffn2_out": u_w2_out,
        "w_mix_in": u_wm_in, "w_mix_out": u_wm_out,
    }
    small_names = ["b_ada", "norm_g", "hgrn_lb", "hgrn_norm_g", "qk_norm_g", "attn_sink", "rel_bias"]
    for i, nme in enumerate(small_names):
        upd[nme] = (d_small[i], m_small[i], v_small[i])
    grads = {
        "w_ada": g_w_ada, "b_ada": g_b_ada, "norm_g": g_norm_g, "w_ffn1_in": g_w1_in, "w_ffn1_out": g_w1_out,
        "w_ffn2_in": g_w2_in, "w_ffn2_out": g_w2_out, "w_mix_in": g_wm_in, "w_mix_out": g_wm_out, "hgrn_lb": g_hgrn_lb,
        "hgrn_norm_g": g_hgrn_norm_g, "qk_norm_g": g_qk_norm_g, "attn_sink": g_attn_sink, "rel_bias": g_rel_bias,
    }
    order = ["w_ada", "b_ada", "norm_g", "w_ffn1_in", "w_ffn1_out", "w_ffn2_in", "w_ffn2_out", "w_mix_in", "w_mix_out",
             "hgrn_lb", "hgrn_norm_g", "qk_norm_g", "attn_sink", "rel_bias"]
    return (loss, dx0[None], *[grads[k] for k in order], *[upd[k][0] for k in order], *[upd[k][1] for k in order],
            *[upd[k][2] for k in order])
```

```python
import functools
import math

import numpy as np
import jax
import jax.numpy as jnp
from jax import lax
from jax.experimental import pallas as pl
from jax.experimental.pallas import tpu as pltpu

F32, BF16 = jnp.float32, jnp.bfloat16

D_MODEL = 1024
D_FF = 2816
HG_HEADS, HG_DIM = 4, 128
HG_WIDTH = HG_HEADS * HG_DIM
ATT_Q_HEADS, ATT_KV_HEADS, ATT_HEAD_DIM = 8, 2, 64
ATT_GROUP = ATT_Q_HEADS // ATT_KV_HEADS
ATT_WIDTH = ATT_Q_HEADS * ATT_HEAD_DIM
KV_WIDTH = ATT_KV_HEADS * ATT_HEAD_DIM
WINDOW, BLOCK = 128, 128
NUM_BUCKETS, MAX_DISTANCE = 32, 128
N_MOD = 9
EPS = 1e-6
D_IN = 5 * HG_WIDTH + ATT_WIDTH + 2 * KV_WIDTH
ADAM_LR, ADAM_B1, ADAM_B2, ADAM_EPS, ADAM_WD, ADAM_STEP = 0.001, 0.9, 0.999, 1e-08, 0.01, 10

N_CHIPS = 4
FF_SHARD = 2 * D_FF // N_CHIPS
NEG = -1e30

VMEM_LIMIT_BYTES = 56 << 20
ROW_TILE = 512
HG_CHUNK = 16
HG_ROWS = 256

MESH = pl.DeviceIdType.MESH
ANY = pl.BlockSpec(memory_space=pl.ANY)


def _params(*sem):
    return pltpu.CompilerParams(dimension_semantics=sem, vmem_limit_bytes=VMEM_LIMIT_BYTES)


def _resident(shape, index_map):
    return pl.BlockSpec(shape, index_map, pipeline_mode=pl.Buffered(1))


def _dot(a, b, dims, precision=None):
    return lax.dot_general(a, b, (dims, ((), ())), precision=precision, preferred_element_type=F32)


def _nn(a, b, precision=None):
    return _dot(a, b, ((1,), (0,)), precision)


def _nt(a, b):
    return _dot(a, b, ((1,), (1,)))


def _tn(a, b):
    return _dot(a, b, ((0,), (0,)))


def _sigmoid(x):
    return jax.nn.sigmoid(x)


class _Exchange:
    def __init__(self, inputs, out_shapes, n_sems, plan, aliases=None):
        self.inputs, self.out_shapes, self.n_sems, self.plan, self.aliases = list(inputs), list(out_shapes), n_sems, plan, aliases or {}

    def sem_shapes(self):
        return [pltpu.SemaphoreType.DMA((self.n_sems,)), pltpu.SemaphoreType.DMA((self.n_sems,))]

    def start(self, in_refs, out_refs, send_sems, recv_sems):
        for cp in self.plan(in_refs, out_refs, send_sems, recv_sems)[0]:
            cp.start()

    def finish(self, in_refs, out_refs, send_sems, recv_sems):
        sends, recvs = self.plan(in_refs, out_refs, send_sems, recv_sems)
        for cp in recvs:
            cp.wait_recv()
        for cp in sends:
            cp.wait_send()


def _run_exchange(ex, name):
    n_in, n_out = len(ex.inputs), len(ex.out_shapes)

    def body(*refs):
        in_refs, out_refs, (send_sems, recv_sems) = refs[:n_in], refs[n_in:n_in + n_out], refs[n_in + n_out:]
        ex.start(in_refs, out_refs, send_sems, recv_sems)
        ex.finish(in_refs, out_refs, send_sems, recv_sems)

    return pl.pallas_call(
        body, name=name, in_specs=[ANY] * n_in, out_specs=[ANY] * n_out, out_shape=ex.out_shapes,
        scratch_shapes=ex.sem_shapes(), input_output_aliases=dict(ex.aliases),
    )(*ex.inputs)


def _call(body, *, name, grid, in_specs, out_specs, out_shape, args, semantics, scratch_shapes=(), exchange=None):
    if exchange is None:
        return pl.pallas_call(
            body, name=name, grid=grid, in_specs=in_specs, out_specs=out_specs, out_shape=out_shape,
            scratch_shapes=list(scratch_shapes), compiler_params=_params(*semantics))(*args)
    exs = exchange if isinstance(exchange, (list, tuple)) else [exchange]
    n_in, n_out, n_scr = len(in_specs), len(out_specs), len(scratch_shapes)
    x_in, x_out = [len(ex.inputs) for ex in exs], [len(ex.out_shapes) for ex in exs]

    def take(refs, counts):
        groups = []
        for n in counts:
            groups.append(refs[:n])
            refs = refs[n:]
        return groups, refs

    def carrier(*refs):
        ins, refs = refs[:n_in], refs[n_in:]
        x_ins, refs = take(refs, x_in)
        outs, refs = refs[:n_out], refs[n_out:]
        x_outs, refs = take(refs, x_out)
        scr, refs = refs[:n_scr], refs[n_scr:]
        sems, _ = take(refs, [2] * len(exs))
        ids = [pl.program_id(a) for a in range(len(grid))]
        first = functools.reduce(jnp.logical_and, [i == 0 for i in ids])
        last = functools.reduce(jnp.logical_and, [i == g - 1 for i, g in zip(ids, grid)])

        @pl.when(first)
        def _():
            for ex, xi, xo, (send_sems, recv_sems) in zip(exs, x_ins, x_outs, sems):
                ex.start(xi, xo, send_sems, recv_sems)

        body(*ins, *outs, *scr)

        @pl.when(last)
        def _():
            for ex, xi, xo, (send_sems, recv_sems) in zip(exs, x_ins, x_outs, sems):
                ex.finish(xi, xo, send_sems, recv_sems)

    aliases, i0, o0 = {}, n_in, n_out
    for ex in exs:
        aliases.update({i0 + i: o0 + o for i, o in ex.aliases.items()})
        i0, o0 = i0 + len(ex.inputs), o0 + len(ex.out_shapes)
    res = pl.pallas_call(
        carrier, name=name, grid=grid, in_specs=list(in_specs) + [ANY] * sum(x_in),
        out_specs=list(out_specs) + [ANY] * sum(x_out),
        out_shape=list(out_shape) + [s for ex in exs for s in ex.out_shapes],
        scratch_shapes=list(scratch_shapes) + [s for ex in exs for s in ex.sem_shapes()],
        input_output_aliases=aliases, compiler_params=_params(*["arbitrary"] * len(grid)),
    )(*args, *[a for ex in exs for a in ex.inputs])
    x_res, _ = take(list(res[n_out:]), x_out)
    return list(res[:n_out]), (x_res if isinstance(exchange, (list, tuple)) else x_res[0])


def _rmsmod_fwd(x, g, shift, scale, name):
    S, D = x.shape
    tr = min(ROW_TILE, S)

    def body(x_ref, g_ref, sh_ref, sc_ref, h_ref):
        xv = x_ref[...]
        rstd = lax.rsqrt(jnp.mean(xv * xv, axis=-1, keepdims=True) + EPS)
        y = xv * rstd * g_ref[...]
        h_ref[...] = (y * (1.0 + sc_ref[...]) + sh_ref[...]).astype(h_ref.dtype)

    row = pl.BlockSpec((tr, D), lambda i: (i, 0))
    vec = pl.BlockSpec((1, D), lambda i: (0, 0))
    return pl.pallas_call(
        body, name=name, grid=(S // tr,), in_specs=[row, vec, vec, vec], out_specs=row,
        out_shape=jax.ShapeDtypeStruct((S, D), BF16), compiler_params=_params("parallel"),
    )(x, g, shift, scale)


def _rmsmod_bwd(dh, x, g, scale, dx_res, name, below=None, exchange=None):
    S, D = x.shape
    tr = min(ROW_TILE, S)
    coef = below[2] if below else None

    def body(*refs):
        if below:
            dh_ref, x_ref, g_ref, sc_ref, dxr_ref, f_ref, gate_ref, dx_ref, dsh_ref, dsc_ref, dg_ref, df_ref, dgate_ref = refs
        else:
            dh_ref, x_ref, g_ref, sc_ref, dxr_ref, dx_ref, dsh_ref, dsc_ref, dg_ref = refs

        @pl.when(pl.program_id(0) == 0)
        def _():
            dsh_ref[...] = jnp.zeros_like(dsh_ref)
            dsc_ref[...] = jnp.zeros_like(dsc_ref)
            dg_ref[...] = jnp.zeros_like(dg_ref)
            if below:
                dgate_ref[...] = jnp.zeros_like(dgate_ref)

        dhv, xv, gv = dh_ref[...], x_ref[...], g_ref[...]
        one_sc = 1.0 + sc_ref[...]
        rstd = lax.rsqrt(jnp.mean(xv * xv, axis=-1, keepdims=True) + EPS)
        n = xv * rstd
        dsh_ref[...] += jnp.sum(dhv, axis=0, keepdims=True)
        dsc_ref[...] += jnp.sum(dhv * n, axis=0, keepdims=True) * gv
        dg_ref[...] += jnp.sum(dhv * n, axis=0, keepdims=True) * one_sc
        dn = dhv * (gv * one_sc)
        dx = dxr_ref[...] + rstd * (dn - n * jnp.mean(dn * n, axis=-1, keepdims=True))
        dx_ref[...] = dx
        if below:
            df_ref[...] = (coef * gate_ref[...] * dx).astype(df_ref.dtype)
            dgate_ref[...] += coef * jnp.sum(dx * f_ref[...].astype(F32), axis=0, keepdims=True)

    row = pl.BlockSpec((tr, D), lambda i: (i, 0))
    vec = pl.BlockSpec((1, D), lambda i: (0, 0))
    vshape = jax.ShapeDtypeStruct((1, D), F32)
    ins, in_specs = [dh, x, g, scale, dx_res], [row, row, vec, vec, row]
    outs, out_specs = [jax.ShapeDtypeStruct((S, D), F32), vshape, vshape, vshape], [row, vec, vec, vec]
    if below:
        ins += [below[0], below[1]]
        in_specs += [row, vec]
        outs += [jax.ShapeDtypeStruct((S, D), BF16), vshape]
        out_specs += [row, vec]
    return _call(body, name=name, grid=(S // tr,), in_specs=in_specs, out_specs=out_specs, out_shape=outs, args=ins,
                 semantics=("arbitrary",), exchange=exchange)


def _loss_bwd(y, target, f, gate, coef, name):
    S, D = y.shape
    tr = min(ROW_TILE, S)

    def body(y_ref, t_ref, f_ref, gate_ref, dy_ref, df_ref, dgate_ref, sq_ref):
        @pl.when(pl.program_id(0) == 0)
        def _():
            dgate_ref[...] = jnp.zeros_like(dgate_ref)
            sq_ref[...] = jnp.zeros_like(sq_ref)

        err = y_ref[...] - t_ref[...]
        sq_ref[...] += jnp.sum(err * err, axis=0, keepdims=True)
        dy = err * (1.0 / D)
        dy_ref[...] = dy
        df_ref[...] = (coef * gate_ref[...] * dy).astype(df_ref.dtype)
        dgate_ref[...] += coef * jnp.sum(dy * f_ref[...].astype(F32), axis=0, keepdims=True)

    row = pl.BlockSpec((tr, D), lambda i: (i, 0))
    vec = pl.BlockSpec((1, D), lambda i: (0, 0))
    vshape = jax.ShapeDtypeStruct((1, D), F32)
    return pl.pallas_call(
        body, name=name, grid=(S // tr,), in_specs=[row, row, row, vec], out_specs=[row, row, vec, vec],
        out_shape=[jax.ShapeDtypeStruct((S, D), F32), jax.ShapeDtypeStruct((S, D), BF16), vshape, vshape],
        compiler_params=_params("arbitrary"),
    )(y, target, f, gate)


def _ffn_in_fwd(h, w4, name, exchange=None):
    S, D = h.shape
    tm = min(ROW_TILE, S)
    n = w4.shape[2]

    def body(h_ref, wg_ref, wu_ref, zg_ref, zu_ref, a_ref):
        hv = h_ref[...]
        zg = _nn(hv, wg_ref[...])
        zu = _nn(hv, wu_ref[...])
        zg_ref[...] = zg.astype(zg_ref.dtype)
        zu_ref[...] = zu.astype(zu_ref.dtype)
        a_ref[...] = (zg * _sigmoid(zg) * zu).astype(a_ref.dtype)

    out = pl.BlockSpec((tm, n), lambda j, m: (m, j))
    oshape = jax.ShapeDtypeStruct((S, 2 * n), BF16)
    return _call(
        body, name=name, grid=(2, S // tm),
        in_specs=[pl.BlockSpec((tm, D), lambda j, m: (m, 0)),
                  pl.BlockSpec((None, D, n), lambda j, m: (j, 0, 0)),
                  pl.BlockSpec((None, D, n), lambda j, m: (j + 2, 0, 0))],
        out_specs=[out, out, out], out_shape=[oshape, oshape, oshape], args=(h, w4, w4),
        semantics=("parallel", "parallel"), exchange=exchange)


def _proj_out_fwd(lhs, w, x, gate, coef, name, exchange=None):
    S, D = x.shape
    tm = min(ROW_TILE, S)
    ks = [a.shape[1] for a in lhs]

    def body(*refs):
        lhs_refs = refs[:len(lhs)]
        w_ref, x_ref, gate_ref, xn_ref, f_ref = refs[len(lhs):]
        acc, off = None, 0
        for a_ref, k in zip(lhs_refs, ks):
            part = _nn(a_ref[...], w_ref[off:off + k, :])
            acc = part if acc is None else acc + part
            off += k
        f_ref[...] = acc.astype(f_ref.dtype)
        xn_ref[...] = x_ref[...] + coef * gate_ref[...] * acc

    row = pl.BlockSpec((tm, D), lambda m: (m, 0))
    return _call(
        body, name=name, grid=(S // tm,),
        in_specs=[pl.BlockSpec((tm, k), lambda m: (m, 0)) for k in ks]
        + [_resident(w.shape, lambda m: (0, 0)), row, pl.BlockSpec((1, D), lambda m: (0, 0))],
        out_specs=[row, row],
        out_shape=[jax.ShapeDtypeStruct((S, D), F32), jax.ShapeDtypeStruct((S, D), BF16)],
        args=(*lhs, w, x, gate), semantics=("parallel",), exchange=exchange)


def _matmul_nn(a, w, out_dtype, tm, name):
    S, K = a.shape
    N = w.shape[1]
    tm = min(tm, S)

    def body(a_ref, w_ref, o_ref):
        o_ref[...] = _nn(a_ref[...], w_ref[...]).astype(o_ref.dtype)

    return pl.pallas_call(
        body, name=name, grid=(S // tm,),
        in_specs=[pl.BlockSpec((tm, K), lambda m: (m, 0)), _resident((K, N), lambda m: (0, 0))],
        out_specs=pl.BlockSpec((tm, N), lambda m: (m, 0)), out_shape=jax.ShapeDtypeStruct((S, N), out_dtype),
        compiler_params=_params("parallel"),
    )(a, w)


def _dact_bwd(df, w_out, zg, zu, name, exchange=None):
    S, D = df.shape
    tm = min(ROW_TILE, S)
    n = w_out.shape[0] // 2

    def body(df_ref, w_ref, zg_ref, zu_ref, dzg_ref, dzu_ref):
        da = _nt(df_ref[...], w_ref[...])
        zg_v, zu_v = zg_ref[...].astype(F32), zu_ref[...].astype(F32)
        s = _sigmoid(zg_v)
        dzu_ref[...] = (da * zg_v * s).astype(dzu_ref.dtype)
        dzg_ref[...] = (da * zu_v * (s * (1.0 + zg_v * (1.0 - s)))).astype(dzg_ref.dtype)

    blk = pl.BlockSpec((tm, n), lambda j, m: (m, j))
    oshape = jax.ShapeDtypeStruct((S, 2 * n), BF16)
    return _call(
        body, name=name, grid=(2, S // tm),
        in_specs=[pl.BlockSpec((tm, D), lambda j, m: (m, 0)), pl.BlockSpec((n, D), lambda j, m: (j, 0)), blk, blk],
        out_specs=[blk, blk], out_shape=[oshape, oshape], args=(df, w_out, zg, zu), semantics=("parallel", "parallel"),
        exchange=exchange)


def _ffn_in_dgrad(dzg, dzu, w4, name, exchange=None):
    S = dzg.shape[0]
    D, n = w4.shape[1], w4.shape[2]
    tm = min(ROW_TILE, S)

    def body(dzg_ref, dzu_ref, w_ref, dh_ref):
        acc = _nt(dzg_ref[:, 0:n], w_ref[0])
        acc += _nt(dzg_ref[:, n:2 * n], w_ref[1])
        acc += _nt(dzu_ref[:, 0:n], w_ref[2])
        acc += _nt(dzu_ref[:, n:2 * n], w_ref[3])
        dh_ref[...] = acc

    blk = pl.BlockSpec((tm, 2 * n), lambda m: (m, 0))
    return _call(
        body, name=name, grid=(S // tm,),
        in_specs=[blk, blk, _resident(w4.shape, lambda m: (0, 0, 0))],
        out_specs=[pl.BlockSpec((tm, D), lambda m: (m, 0))], out_shape=[jax.ShapeDtypeStruct((S, D), F32)],
        args=(dzg, dzu, w4), semantics=("parallel",), exchange=exchange)


def _matmul_nt(a, w, tm, name, exchange=None):
    S, K = a.shape
    N = w.shape[0]
    tm = min(tm, S)

    def body(a_ref, w_ref, o_ref):
        o_ref[...] = _nt(a_ref[...], w_ref[...])

    return _call(
        body, name=name, grid=(S // tm,),
        in_specs=[pl.BlockSpec((tm, K), lambda m: (m, 0)), _resident((N, K), lambda m: (0, 0))],
        out_specs=[pl.BlockSpec((tm, N), lambda m: (m, 0))], out_shape=[jax.ShapeDtypeStruct((S, N), F32)],
        args=(a, w), semantics=("parallel",), exchange=exchange)


def _wgrad(a, gs, tn, name, exchange=None):
    S, Ka = a.shape
    N = gs[0].shape[1]
    ts = min(ROW_TILE, S)

    def body(a_ref, *refs):
        g_refs, o_ref = refs[:-1], refs[-1]

        @pl.when(pl.program_id(1) == 0)
        def _():
            o_ref[...] = jnp.zeros_like(o_ref)

        a_t = a_ref[...].T
        for i, g_ref in enumerate(g_refs):
            o_ref[i] += _nn(a_t, g_ref[...])

    return _call(
        body, name=name, grid=(N // tn, S // ts),
        in_specs=[pl.BlockSpec((ts, Ka), lambda j, s: (s, 0))] + [pl.BlockSpec((ts, tn), lambda j, s: (s, j))] * len(gs),
        out_specs=[pl.BlockSpec((len(gs), None, Ka, tn), lambda j, s: (0, j, 0, 0))],
        out_shape=[jax.ShapeDtypeStruct((len(gs), N // tn, Ka, tn), F32)], args=(a, *gs),
        semantics=("parallel", "arbitrary"), exchange=exchange)


def _wgrad_rows(lhs, g, name):
    S, Ka = lhs[0].shape
    N = g.shape[1]
    ts = min(ROW_TILE, S)

    def body(*refs):
        a_refs, g_ref, o_ref = refs[:-2], refs[-2], refs[-1]

        @pl.when(pl.program_id(0) == 0)
        def _():
            o_ref[...] = jnp.zeros_like(o_ref)

        gv = g_ref[...]
        for i, a_ref in enumerate(a_refs):
            o_ref[i] += _tn(a_ref[...], gv)

    return pl.pallas_call(
        body, name=name, grid=(S // ts,),
        in_specs=[pl.BlockSpec((ts, Ka), lambda s: (s, 0))] * len(lhs) + [pl.BlockSpec((ts, N), lambda s: (s, 0))],
        out_specs=pl.BlockSpec((len(lhs), Ka, N), lambda s: (0, 0, 0)),
        out_shape=jax.ShapeDtypeStruct((len(lhs), Ka, N), F32), compiler_params=_params("arbitrary"),
    )(*lhs, g)


def _hgrn_chunk_common(qr, fr, oml, tri, last):
    k = oml * _sigmoid(-fr)
    g = jnp.log1p(-k)
    q = qr * _sigmoid(qr)
    G = _nn(tri, g, precision=lax.Precision.HIGHEST)
    Gl = G[last:last + 1]
    return q, k, G, Gl


def _hgrn_consts(reverse):
    C = HG_CHUNK
    r = lax.broadcasted_iota(jnp.int32, (C, C), 0)
    cc = lax.broadcasted_iota(jnp.int32, (C, C), 1)
    tri = ((cc >= r) if reverse else (cc <= r)).astype(F32)
    tri_t = ((cc <= r) if reverse else (cc >= r)).astype(F32)
    rid = lax.broadcasted_iota(jnp.int32, (C, HG_WIDTH), 0)
    return tri, tri_t, rid, (0 if reverse else C - 1)


def _head_slices():
    return [slice(h * HG_DIM, (h + 1) * HG_DIM) for h in range(HG_HEADS)]


def _per_head_lane_sum(x):
    C = x.shape[0]
    return jnp.concatenate(
        [jnp.broadcast_to(jnp.sum(x[:, sl], axis=-1, keepdims=True), (C, HG_DIM)) for sl in _head_slices()], axis=1)


HG_TILE = 8


def _pair_tiles(s, reverse):
    blk, r = divmod(s, HG_TILE)
    n_tiles = HG_CHUNK // HG_TILE
    others = range(0, blk) if reverse else range(blk + 1, n_tiles)
    return [(blk, r)] + [(t, None) for t in others]


def _pair_decay(G, s, tile, r, rid8, reverse, keys=False):
    rs = slice(tile * HG_TILE, (tile + 1) * HG_TILE)
    d = (G[s:s + 1] - G[rs]) if keys else (G[rs] - G[s:s + 1])
    if r is not None:
        d = jnp.where((rid8 <= r) if reverse else (rid8 >= r), d, NEG)
    return rs, jnp.exp(d)


def _hgrn_fwd(z, lb, direction, name, exchange=None):
    S = z.shape[0]
    C, DK, W = HG_CHUNK, HG_DIM, HG_WIDTH
    tb = min(HG_ROWS, S)
    n_t, n_c = S // tb, tb // C
    reverse = direction == 1
    tmap = (lambda i: n_t - 1 - i) if reverse else (lambda i: i)

    def body(q_ref, f_ref, v_ref, lb_ref, o_ref, st_out_ref, st_ref):
        @pl.when(pl.program_id(0) == 0)
        def _():
            st_ref[...] = jnp.zeros_like(st_ref)

        oml = 1.0 - lb_ref[...]
        tri, _, _, last = _hgrn_consts(reverse)
        rid8 = lax.broadcasted_iota(jnp.int32, (HG_TILE, W), 0)

        def chunk(ci, carry):
            cidx = (n_c - 1 - ci) if reverse else ci
            rows = pl.ds(pl.multiple_of(cidx * C, C), C)
            v = v_ref[rows, :]
            q, k, G, Gl = _hgrn_chunk_common(q_ref[rows, :], f_ref[rows, :], oml, tri, last)
            qd = (q * jnp.exp(G)).astype(BF16)
            kd = (k * jnp.exp(Gl - G)).astype(BF16)
            e_gl = jnp.exp(Gl)
            v_b = v.astype(BF16)
            inter = []
            for h, sl in enumerate(_head_slices()):
                st0 = st_ref[h]
                st_out_ref[h, cidx] = st0
                inter.append(_nt(qd[:, sl], st0.astype(BF16)))
                st_ref[h] = st0 * e_gl[:, sl] + _tn(v_b[:, sl], kd[:, sl])
            o = jnp.concatenate(inter, axis=1)
            o_t = [o[t * HG_TILE:(t + 1) * HG_TILE] for t in range(C // HG_TILE)]
            for s in range(C):
                k_s, v_s = k[s:s + 1], v[s:s + 1]
                for tile, r in _pair_tiles(s, reverse):
                    rs, e_s = _pair_decay(G, s, tile, r, rid8, reverse)
                    o_t[tile] = o_t[tile] + _per_head_lane_sum(q[rs] * k_s * e_s) * v_s
            o_ref[rows, :] = jnp.concatenate(o_t, axis=0)
            return carry

        lax.fori_loop(0, n_c, chunk, 0, unroll=2)

    def sec(j):
        return pl.BlockSpec((tb, W), lambda i: (tmap(i), j))

    return _call(
        body, name=name, grid=(n_t,),
        in_specs=[sec(0), sec(1 + direction), sec(3), pl.BlockSpec((1, W), lambda i: (0, 0))],
        out_specs=[sec(0), pl.BlockSpec((HG_HEADS, n_c, DK, DK), lambda i: (0, tmap(i), 0, 0))],
        out_shape=[jax.ShapeDtypeStruct((S, W), F32), jax.ShapeDtypeStruct((HG_HEADS, S // C, DK, DK), F32)],
        scratch_shapes=[pltpu.VMEM((HG_HEADS, DK, DK), F32)], args=(z, z, z, lb), semantics=("arbitrary",),
        exchange=exchange)


def _hgrn_bwd(z, lb, do, states, direction, name, acc=None, exchange=None):
    S = z.shape[0]
    C, DK, W = HG_CHUNK, HG_DIM, HG_WIDTH
    tb = min(HG_ROWS, S)
    n_t, n_c = S // tb, tb // C
    reverse = direction == 1
    tmap = (lambda i: i) if reverse else (lambda i: n_t - 1 - i)

    def body(*refs):
        if acc:
            q_ref, f_ref, v_ref, lb_ref, do_ref, st_in_ref, dqa_ref, dva_ref, dq_ref, df_ref, dv_ref, doml_ref, dst_ref = refs
        else:
            q_ref, f_ref, v_ref, lb_ref, do_ref, st_in_ref, dq_ref, df_ref, dv_ref, doml_ref, dst_ref = refs

        @pl.when(pl.program_id(0) == 0)
        def _():
            dst_ref[...] = jnp.zeros_like(dst_ref)
            doml_ref[...] = jnp.zeros_like(doml_ref)

        oml = 1.0 - lb_ref[...]
        tri, tri_t, rid, last = _hgrn_consts(reverse)
        rid8 = lax.broadcasted_iota(jnp.int32, (HG_TILE, W), 0)

        def chunk(ci, carry):
            cidx = ci if reverse else (n_c - 1 - ci)
            rows = pl.ds(pl.multiple_of(cidx * C, C), C)
            qr, fr, v, dov = q_ref[rows, :], f_ref[rows, :], v_ref[rows, :], do_ref[rows, :]
            q, k, G, Gl = _hgrn_chunk_common(qr, fr, oml, tri, last)
            e_g, e_gl, e_kd = jnp.exp(G), jnp.exp(Gl), jnp.exp(Gl - G)
            qd, kd = q * e_g, k * e_kd
            do_b, v_b, qd_b, kd_b = dov.astype(BF16), v.astype(BF16), qd.astype(BF16), kd.astype(BF16)
            dqd, dkd, dv, state_dot = [], [], [], []
            for h, sl in enumerate(_head_slices()):
                st0, dst1 = st_in_ref[h, cidx], dst_ref[h]
                dst1_b = dst1.astype(BF16)
                dqd.append(_nn(do_b[:, sl], st0.astype(BF16)))
                dkd.append(_nn(v_b[:, sl], dst1_b))
                dv.append(_nt(kd_b[:, sl], dst1_b))
                state_dot.append(jnp.sum(st0 * dst1, axis=0, keepdims=True))
                dst_ref[h] = dst1 * e_gl[:, sl] + _tn(do_b[:, sl], qd_b[:, sl])
            dqd, dkd, dv = [jnp.concatenate(t, axis=1) for t in (dqd, dkd, dv)]
            d_gl = e_gl * jnp.concatenate(state_dot, axis=1) + jnp.sum(dkd * kd, axis=0, keepdims=True)
            dq, dk = dqd * e_g, dkd * e_kd
            n_tiles = C // HG_TILE
            dq_t, dk_t, dv_t = [[x[t * HG_TILE:(t + 1) * HG_TILE] for t in range(n_tiles)] for x in (dq, dk, dv)]
            for s in range(C):
                k_s, v_s = k[s:s + 1], v[s:s + 1]
                for tile, r in _pair_tiles(s, reverse):
                    rs, e_s = _pair_decay(G, s, tile, r, rid8, reverse)
                    dq_t[tile] = dq_t[tile] + _per_head_lane_sum(dov[rs] * v_s) * e_s * k_s
            for t in range(C):
                q_t, do_t = q[t:t + 1], dov[t:t + 1]
                for tile, r in _pair_tiles(t, not reverse):
                    rs, x_t = _pair_decay(G, t, tile, r, rid8, not reverse, keys=True)
                    dv_t[tile] = dv_t[tile] + _per_head_lane_sum(k[rs] * q_t * x_t) * do_t
                    dk_t[tile] = dk_t[tile] + _per_head_lane_sum(v[rs] * do_t) * x_t * q_t
            dq, dk, dv = [jnp.concatenate(x, axis=0) for x in (dq_t, dk_t, dv_t)]
            d_big_g = dq * q - dk * k + jnp.where(rid == last, d_gl, 0.0)
            dg = _nn(tri_t, d_big_g, precision=lax.Precision.HIGHEST)
            dk_all = dk - dg / (1.0 - k)
            sig_nf = _sigmoid(-fr)
            df_ref[rows, :] = -dk_all * k * (1.0 - sig_nf)
            doml_ref[...] += jnp.sum(dk_all * sig_nf, axis=0, keepdims=True)
            sq = _sigmoid(qr)
            dqr = dq * (sq * (1.0 + qr * (1.0 - sq)))
            if acc:
                dqr = dqr + dqa_ref[rows, :]
                dv = dv + dva_ref[rows, :]
            dq_ref[rows, :] = dqr
            dv_ref[rows, :] = dv
            return carry

        lax.fori_loop(0, n_c, chunk, 0, unroll=2)

    def sec(j):
        return pl.BlockSpec((tb, W), lambda i: (tmap(i), j))

    vec = pl.BlockSpec((1, W), lambda i: (0, 0))
    ins = [z, z, z, lb, do, states]
    in_specs = [sec(0), sec(1 + direction), sec(3), vec, sec(0),
                pl.BlockSpec((HG_HEADS, n_c, DK, DK), lambda i: (0, tmap(i), 0, 0))]
    if acc:
        ins += list(acc)
        in_specs += [sec(0), sec(0)]
    full = jax.ShapeDtypeStruct((S, W), F32)
    return _call(
        body, name=name, grid=(n_t,), in_specs=in_specs,
        out_specs=[sec(0), sec(0), sec(0), vec],
        out_shape=[full, full, full, jax.ShapeDtypeStruct((1, W), F32)],
        scratch_shapes=[pltpu.VMEM((HG_HEADS, DK, DK), F32)], args=ins, semantics=("arbitrary",), exchange=exchange)


def _hgrn_post_fwd(o_f, o_b, z, norm_g, name):
    S = z.shape[0]
    tr = min(ROW_TILE, S)

    def body(of_ref, ob_ref, gr_ref, ng_ref, y_ref):
        o = of_ref[...] + ob_ref[...]
        gr = gr_ref[...]
        gate = gr * _sigmoid(gr)
        ng = ng_ref[...]
        for h in range(HG_HEADS):
            sl = slice(h * HG_DIM, (h + 1) * HG_DIM)
            oh = o[:, sl]
            rstd = lax.rsqrt(jnp.mean(oh * oh, axis=-1, keepdims=True) + EPS)
            y_ref[:, sl] = (oh * rstd * ng[:, sl] * gate[:, sl]).astype(y_ref.dtype)

    row = pl.BlockSpec((tr, HG_WIDTH), lambda i: (i, 0))
    return pl.pallas_call(
        body, name=name, grid=(S // tr,),
        in_specs=[row, row, pl.BlockSpec((tr, HG_WIDTH), lambda i: (i, 4)), pl.BlockSpec((1, HG_WIDTH), lambda i: (0, 0))],
        out_specs=row, out_shape=jax.ShapeDtypeStruct((S, HG_WIDTH), BF16), compiler_params=_params("parallel"),
    )(o_f, o_b, z, norm_g)


def _hgrn_post_bwd(dy, o_f, o_b, z, norm_g, name):
    S = z.shape[0]
    tr = min(ROW_TILE, S)

    def body(dy_ref, of_ref, ob_ref, gr_ref, ng_ref, do_ref, dgr_ref, dng_ref):
        @pl.when(pl.program_id(0) == 0)
        def _():
            dng_ref[...] = jnp.zeros_like(dng_ref)

        o = of_ref[...] + ob_ref[...]
        gr, ng, dyv = gr_ref[...], ng_ref[...], dy_ref[...]
        sg = _sigmoid(gr)
        for h in range(HG_HEADS):
            sl = slice(h * HG_DIM, (h + 1) * HG_DIM)
            oh, dyh, grh, sgh, ngh = o[:, sl], dyv[:, sl], gr[:, sl], sg[:, sl], ng[:, sl]
            rstd = lax.rsqrt(jnp.mean(oh * oh, axis=-1, keepdims=True) + EPS)
            on = oh * rstd
            du = dyh * (grh * sgh)
            dgr_ref[:, sl] = dyh * (on * ngh) * (sgh * (1.0 + grh * (1.0 - sgh)))
            dng_ref[:, sl] += jnp.sum(du * on, axis=0, keepdims=True)
            don = du * ngh
            do_ref[:, sl] = rstd * (don - on * jnp.mean(don * on, axis=-1, keepdims=True))

    row = pl.BlockSpec((tr, HG_WIDTH), lambda i: (i, 0))
    vec = pl.BlockSpec((1, HG_WIDTH), lambda i: (0, 0))
    full = jax.ShapeDtypeStruct((S, HG_WIDTH), F32)
    return pl.pallas_call(
        body, name=name, grid=(S // tr,),
        in_specs=[row, row, row, pl.BlockSpec((tr, HG_WIDTH), lambda i: (i, 4)), vec],
        out_specs=[row, row, vec], out_shape=[full, full, jax.ShapeDtypeStruct((1, HG_WIDTH), F32)],
        compiler_params=_params("arbitrary"),
    )(dy, o_f, o_b, z, norm_g)


def _t5_bucket_table():
    rel = (np.arange(3 * BLOCK)[None, :] - BLOCK) - np.arange(BLOCK)[:, None]
    nb = NUM_BUCKETS // 2
    max_exact = nb // 2
    ret = (rel > 0).astype(np.int32) * nb
    n = np.abs(rel)
    ratio = np.log(np.maximum(n, 1).astype(np.float32) / np.float32(max_exact)) / np.float32(math.log(MAX_DISTANCE / max_exact))
    large = max_exact + (ratio.astype(np.float32) * np.float32(nb - max_exact)).astype(np.int32)
    large = np.minimum(large, nb - 1)
    bucket = ret + np.where(n < max_exact, n, large)
    return bucket.astype(np.int32), (n <= WINDOW)


def _bias_table(rel_bias, name):
    bucket, in_band = _t5_bucket_table()
    idx = jnp.asarray(np.where(in_band, bucket, -1))

    def body(rb_ref, idx_ref, o_ref):
        h = pl.program_id(0)
        iv = idx_ref[...]
        acc = jnp.where(iv < 0, NEG, 0.0).astype(F32)
        for b in range(NUM_BUCKETS):
            acc = acc + jnp.where(iv == b, rb_ref[b, h], 0.0)
        o_ref[...] = acc

    return pl.pallas_call(
        body, name=name, grid=(ATT_Q_HEADS,),
        in_specs=[pl.BlockSpec(memory_space=pltpu.SMEM), pl.BlockSpec((BLOCK, 3 * BLOCK), lambda h: (0, 0))],
        out_specs=pl.BlockSpec((None, BLOCK, 3 * BLOCK), lambda h: (h, 0, 0)),
        out_shape=jax.ShapeDtypeStruct((ATT_Q_HEADS, BLOCK, 3 * BLOCK), F32), compiler_params=_params("parallel"),
    )(rel_bias, idx)


def _bias_grad(ds_sum, name):
    bucket, in_band = _t5_bucket_table()
    idx = jnp.asarray(np.where(in_band, bucket, -1))

    def body(ds_ref, idx_ref, o_ref):
        iv, ds = idx_ref[...], ds_ref[...]
        for b in range(NUM_BUCKETS):
            part = jnp.sum(jnp.where(iv == b, ds, 0.0), axis=0, keepdims=True)
            o_ref[b:b + 1, :] = part[:, 0:BLOCK] + part[:, BLOCK:2 * BLOCK] + part[:, 2 * BLOCK:3 * BLOCK]

    return pl.pallas_call(
        body, name=name, grid=(ATT_Q_HEADS,),
        in_specs=[pl.BlockSpec((None, BLOCK, 3 * BLOCK), lambda h: (h, 0, 0)), pl.BlockSpec((BLOCK, 3 * BLOCK), lambda h: (0, 0))],
        out_specs=pl.BlockSpec((None, NUM_BUCKETS, BLOCK), lambda h: (h, 0, 0)),
        out_shape=jax.ShapeDtypeStruct((ATT_Q_HEADS, NUM_BUCKETS, BLOCK), F32), compiler_params=_params("parallel"),
    )(ds_sum, idx)


def _attn_specs(nb):
    G, dh = ATT_GROUP, ATT_HEAD_DIM
    qspec = pl.BlockSpec((G, BLOCK, dh), lambda j, n: (j, n, 0))

    def kv(shift):
        return pl.BlockSpec((None, BLOCK, dh), lambda j, n: (j, jnp.clip(n + shift, 0, nb - 1), 0))

    gain = pl.BlockSpec((1, dh), lambda j, n: (0, 0))
    sink = pl.BlockSpec((G, 1, BLOCK), lambda j, n: (j, 0, 0))
    bias = pl.BlockSpec((G, BLOCK, 3 * BLOCK), lambda j, n: (j, 0, 0))
    return qspec, kv, gain, sink, bias


def _attn_probs(qh, kn, bias_h, sink_h, edge_ok):
    s = _nt(qh.astype(BF16), kn.astype(BF16)) * (1.0 / math.sqrt(ATT_HEAD_DIM)) + bias_h
    s = jnp.where(edge_ok, s, NEG)
    m = jnp.maximum(jnp.max(s, axis=-1, keepdims=True), sink_h)
    p = jnp.exp(s - m)
    e_sink = jnp.exp(sink_h - m)
    inv = 1.0 / (jnp.sum(p, axis=-1, keepdims=True) + e_sink)
    return p * inv, e_sink * inv


def _rms_rows(x):
    rstd = lax.rsqrt(jnp.mean(x * x, axis=-1, keepdims=True) + EPS)
    return x * rstd, rstd


def _edge_ok(n, nb):
    colid = lax.broadcasted_iota(jnp.int32, (ATT_GROUP * BLOCK, 3 * BLOCK), 1)
    return jnp.logical_and(jnp.logical_or(colid >= BLOCK, n > 0), jnp.logical_or(colid < 2 * BLOCK, n < nb - 1))


def _sink_column(sink_ref):
    return jnp.concatenate([jnp.broadcast_to(sink_ref[g][:, 0:1], (BLOCK, 1)) for g in range(ATT_GROUP)], axis=0)


def _attn_fwd(q, k, v, q_g, k_g, sink, bias, name):
    S = q.shape[1]
    nb = S // BLOCK
    G, dh = ATT_GROUP, ATT_HEAD_DIM
    qspec, kv, gain, sink_spec, bias_spec = _attn_specs(nb)

    def body(q_ref, k0, k1, k2, v0, v1, v2, qg_ref, kg_ref, sink_ref, bias_ref, o_ref):
        n = pl.program_id(1)
        kcat = jnp.concatenate([k0[...], k1[...], k2[...]], axis=0)
        vcat = jnp.concatenate([v0[...], v1[...], v2[...]], axis=0).astype(BF16)
        kn = _rms_rows(kcat)[0] * kg_ref[...]
        qn = _rms_rows(q_ref[...].reshape(G * BLOCK, dh))[0] * qg_ref[...]
        p, _ = _attn_probs(qn, kn, bias_ref[...].reshape(G * BLOCK, 3 * BLOCK), _sink_column(sink_ref), _edge_ok(n, nb))
        o_ref[...] = _nn(p.astype(BF16), vcat).reshape(G, BLOCK, dh)

    return pl.pallas_call(
        body, name=name, grid=(ATT_KV_HEADS, nb),
        in_specs=[qspec, kv(-1), kv(0), kv(1), kv(-1), kv(0), kv(1), gain, gain, sink_spec, bias_spec],
        out_specs=qspec, out_shape=jax.ShapeDtypeStruct(q.shape, F32), compiler_params=_params("parallel", "parallel"),
    )(q, k, k, k, v, v, v, q_g, k_g, sink, bias)


def _attn_bwd(q, k, v, q_g, k_g, sink, bias, do, name):
    S = q.shape[1]
    nb = S // BLOCK
    G, dh = ATT_GROUP, ATT_HEAD_DIM
    scale = 1.0 / math.sqrt(dh)
    qspec, kv, gain, sink_spec, bias_spec = _attn_specs(nb)

    def body(q_ref, k0, k1, k2, v0, v1, v2, qg_ref, kg_ref, sink_ref, bias_ref, do_ref,
             dq_ref, dkw_ref, dvw_ref, ds_ref, dsink_ref, dqg_ref):
        n = pl.program_id(1)

        @pl.when(n == 0)
        def _():
            ds_ref[...] = jnp.zeros_like(ds_ref)
            dsink_ref[...] = jnp.zeros_like(dsink_ref)
            dqg_ref[...] = jnp.zeros_like(dqg_ref)

        kcat = jnp.concatenate([k0[...], k1[...], k2[...]], axis=0)
        vcat = jnp.concatenate([v0[...], v1[...], v2[...]], axis=0).astype(BF16)
        kn = _rms_rows(kcat)[0] * kg_ref[...]
        qg = qg_ref[...]
        qhat, rstd = _rms_rows(q_ref[...].reshape(G * BLOCK, dh))
        qn = qhat * qg
        p, p_sink = _attn_probs(qn, kn, bias_ref[...].reshape(G * BLOCK, 3 * BLOCK), _sink_column(sink_ref), _edge_ok(n, nb))
        do_b = do_ref[...].reshape(G * BLOCK, dh).astype(BF16)
        dp = _nt(do_b, vcat)
        delta = jnp.sum(p * dp, axis=-1, keepdims=True)
        ds = p * (dp - delta)
        ds_ref[...] += ds.reshape(G, BLOCK, 3 * BLOCK)
        sink_term = p_sink * delta
        for g in range(G):
            dsink_ref[g] += jnp.zeros((1, BLOCK), F32) - jnp.sum(sink_term[g * BLOCK:(g + 1) * BLOCK], axis=0, keepdims=True)
        ds_b = ds.astype(BF16)
        dvw_ref[...] = _tn(p.astype(BF16), do_b)
        dkw_ref[...] = _tn(ds_b, qn.astype(BF16)) * scale
        dqn = _nn(ds_b, kn.astype(BF16)) * scale
        dqg_ref[...] += jnp.sum(dqn * qhat, axis=0, keepdims=True)
        dqh = dqn * qg
        dq_ref[...] = (rstd * (dqh - qhat * jnp.mean(dqh * qhat, axis=-1, keepdims=True))).reshape(G, BLOCK, dh)

    win = pl.BlockSpec((None, None, 3 * BLOCK, dh), lambda j, n: (j, n, 0, 0))
    wshape = jax.ShapeDtypeStruct((ATT_KV_HEADS, nb, 3 * BLOCK, dh), F32)
    return pl.pallas_call(
        body, name=name, grid=(ATT_KV_HEADS, nb),
        in_specs=[qspec, kv(-1), kv(0), kv(1), kv(-1), kv(0), kv(1), gain, gain, sink_spec, bias_spec, qspec],
        out_specs=[qspec, win, win, bias_spec, sink_spec, pl.BlockSpec((None, 1, dh), lambda j, n: (j, 0, 0))],
        out_shape=[jax.ShapeDtypeStruct(q.shape, F32), wshape, wshape,
                   jax.ShapeDtypeStruct((ATT_Q_HEADS, BLOCK, 3 * BLOCK), F32),
                   jax.ShapeDtypeStruct((ATT_Q_HEADS, 1, BLOCK), F32),
                   jax.ShapeDtypeStruct((ATT_KV_HEADS, 1, dh), F32)],
        compiler_params=_params("parallel", "arbitrary"),
    )(q, k, k, k, v, v, v, q_g, k_g, sink, bias, do)


def _attn_kv_reduce(dkw, dvw, k, k_g, name):
    S = k.shape[1]
    nb = S // BLOCK
    dh = ATT_HEAD_DIM
    kb = min(8, nb)
    steps = nb // kb

    def body(a_lo, a, a_hi, b_lo, b, b_hi, k_ref, kg_ref, dk_ref, dv_ref, dkg_ref):
        n = pl.program_id(1)

        @pl.when(n == 0)
        def _():
            dkg_ref[...] = jnp.zeros_like(dkg_ref)

        lo = jnp.where(n > 0, 1.0, 0.0)
        hi = jnp.where(n < steps - 1, 1.0, 0.0)

        def overlap_add(w, w_lo, w_hi, i):
            before = lo * w_lo[...] if i == 0 else w[i - 1, 2 * BLOCK:3 * BLOCK, :]
            after = hi * w_hi[...] if i == kb - 1 else w[i + 1, 0:BLOCK, :]
            return w[i, BLOCK:2 * BLOCK, :] + before + after

        dkg = jnp.zeros((1, dh), F32)
        for i in range(kb):
            rows = slice(i * BLOCK, (i + 1) * BLOCK)
            dkn = overlap_add(a, a_lo, a_hi, i)
            dv_ref[rows, :] = overlap_add(b, b_lo, b_hi, i)
            khat, rstd = _rms_rows(k_ref[rows, :])
            dkg = dkg + jnp.sum(dkn * khat, axis=0, keepdims=True)
            dkh = dkn * kg_ref[...]
            dk_ref[rows, :] = rstd * (dkh - khat * jnp.mean(dkh * khat, axis=-1, keepdims=True))
        dkg_ref[...] += dkg

    main = pl.BlockSpec((None, kb, 3 * BLOCK, dh), lambda j, n: (j, n, 0, 0))
    halo_lo = pl.BlockSpec((None, None, BLOCK, dh), lambda j, n: (j, jnp.maximum(n * kb - 1, 0), 2, 0))
    halo_hi = pl.BlockSpec((None, None, BLOCK, dh), lambda j, n: (j, jnp.minimum(n * kb + kb, nb - 1), 0, 0))
    blk = pl.BlockSpec((None, kb * BLOCK, dh), lambda j, n: (j, n, 0))
    return pl.pallas_call(
        body, name=name, grid=(ATT_KV_HEADS, steps),
        in_specs=[halo_lo, main, halo_hi, halo_lo, main, halo_hi, blk, pl.BlockSpec((1, dh), lambda j, n: (0, 0))],
        out_specs=[blk, blk, pl.BlockSpec((None, 1, dh), lambda j, n: (j, 0, 0))],
        out_shape=[jax.ShapeDtypeStruct(k.shape, F32), jax.ShapeDtypeStruct(k.shape, F32),
                   jax.ShapeDtypeStruct((ATT_KV_HEADS, 1, dh), F32)],
        compiler_params=_params("parallel", "arbitrary"),
    )(dkw, dkw, dkw, dvw, dvw, dvw, k, k_g)


def _ada_fwd(c_act, w, b, name):
    n = w.shape[1]

    def body(c_ref, w_ref, b_ref, o_ref):
        o_ref[...] = _nn(c_ref[...], w_ref[...], precision=lax.Precision.HIGHEST) + b_ref[...]

    tn = n // 3
    return pl.pallas_call(
        body, name=name, grid=(3,),
        in_specs=[pl.BlockSpec(c_act.shape, lambda j: (0, 0)), pl.BlockSpec((w.shape[0], tn), lambda j: (0, j)),
                  pl.BlockSpec((1, tn), lambda j: (0, j))],
        out_specs=pl.BlockSpec((c_act.shape[0], tn), lambda j: (0, j)),
        out_shape=jax.ShapeDtypeStruct((c_act.shape[0], n), F32), compiler_params=_params("parallel"),
    )(c_act, w, b)


def _ada_wgrad(c_act_t, dm, name):
    D, nbatch = c_act_t.shape
    n = dm.shape[1]
    tr = 256

    def body(c_ref, dm_ref, o_ref):
        cv, dv = c_ref[...], dm_ref[...]
        acc = cv[:, 0:1] * dv[0:1, :]
        for b in range(1, nbatch):
            acc = acc + cv[:, b:b + 1] * dv[b:b + 1, :]
        o_ref[...] = acc

    return pl.pallas_call(
        body, name=name, grid=(D // tr,),
        in_specs=[pl.BlockSpec((tr, nbatch), lambda i: (i, 0)), pl.BlockSpec((nbatch, n), lambda i: (0, 0))],
        out_specs=pl.BlockSpec((tr, n), lambda i: (i, 0)), out_shape=jax.ShapeDtypeStruct((D, n), F32),
        compiler_params=_params("parallel"),
    )(c_act_t, dm)


def _adamw(w, g, m, v, name):
    R, Cn = w.shape
    tr = R
    for cand in (256, 128, 64, 32, 16, 8):
        if R % cand == 0:
            tr = cand
            break

    def body(w_ref, g_ref, m_ref, v_ref, d_ref, nm_ref, nv_ref):
        gv = g_ref[...]
        m_new = ADAM_B1 * m_ref[...] + (1.0 - ADAM_B1) * gv
        v_new = ADAM_B2 * v_ref[...] + (1.0 - ADAM_B2) * (gv * gv)
        m_hat = m_new / (1.0 - ADAM_B1 ** ADAM_STEP)
        v_hat = v_new / (1.0 - ADAM_B2 ** ADAM_STEP)
        d_ref[...] = -ADAM_LR * (m_hat / (jnp.sqrt(v_hat) + ADAM_EPS) + ADAM_WD * w_ref[...])
        nm_ref[...] = m_new
        nv_ref[...] = v_new

    blk = pl.BlockSpec((tr, Cn), lambda i: (i, 0))
    shp = jax.ShapeDtypeStruct((R, Cn), F32)
    return pl.pallas_call(
        body, name=name, grid=(R // tr,), in_specs=[blk] * 4, out_specs=[blk] * 3, out_shape=[shp] * 3,
        compiler_params=_params("parallel"),
    )(w, g, m, v)


def _place():
    return lax.axis_index("x"), lax.axis_index("y"), lax.axis_index("c")


def _flip(place, k):
    x, y, c = place
    return (1 - x if k & 4 else x, 1 - y if k & 2 else y, 1 - c if k & 1 else c)


def _dev_index(place):
    x, y, c = place
    return 4 * x + 2 * y + c


def _chip_index(place):
    return 2 * place[0] + place[1]


def _allgather8(x, name, reduce=False):
    R, Cn = x.shape

    def body(x_ref, *rest):
        if reduce:
            out_ref, sum_ref, send_sems, recv_sems, local_sem = rest
        else:
            out_ref, send_sems, recv_sems, local_sem = rest
        me = _place()
        mine = pltpu.make_async_copy(x_ref, out_ref.at[_dev_index(me)], local_sem)
        mine.start()

        def copy(k, origin, to):
            return pltpu.make_async_remote_copy(
                src_ref=x_ref, dst_ref=out_ref.at[_dev_index(origin)], send_sem=send_sems.at[k - 1],
                recv_sem=recv_sems.at[k - 1], device_id=to, device_id_type=MESH)

        sends = [copy(k, me, _flip(me, k)) for k in range(1, 8)]
        for cp in sends:
            cp.start()
        for k in range(1, 8):
            copy(k, _flip(me, k), me).wait_recv()
        for cp in sends:
            cp.wait_send()
        mine.wait()
        if reduce:
            acc = out_ref[0]
            for i in range(1, 8):
                acc = acc + out_ref[i]
            sum_ref[...] = acc

    vm = pl.BlockSpec(memory_space=pltpu.VMEM)
    outs = [jax.ShapeDtypeStruct((8, R, Cn), F32)] + ([jax.ShapeDtypeStruct((R, Cn), F32)] if reduce else [])
    res = pl.pallas_call(
        body, name=name, in_specs=[vm], out_specs=[vm] * len(outs), out_shape=outs,
        scratch_shapes=[pltpu.SemaphoreType.DMA((7,)), pltpu.SemaphoreType.DMA((7,)), pltpu.SemaphoreType.DMA],
    )(x)
    return res if reduce else res[0]


def _weights_allgather(shards, name):
    n = len(shards)
    per = 8

    def body(*refs):
        in_refs, out_refs = refs[:n], refs[n:2 * n]
        send_sems, recv_sems = refs[2 * n:]
        me = _place()
        c = me[2]
        sibling = _flip(me, 1)
        others = [_flip(me, 2 * j) for j in (1, 2, 3)]

        def copy(a, k, src, dst, to):
            return pltpu.make_async_remote_copy(
                src_ref=src, dst_ref=dst, send_sem=send_sems.at[per * a + k], recv_sem=recv_sems.at[per * a + k],
                device_id=to, device_id_type=MESH)

        def block(a, place, half):
            return out_refs[a].at[_chip_index(place), half]

        started = []
        for a in range(n):
            sends = [copy(a, 0, in_refs[a].at[c], block(a, me, c), sibling),
                     copy(a, 7, in_refs[a].at[1 - c], block(a, me, 1 - c), sibling)]
            sends += [copy(a, 1 + j, in_refs[a].at[c], block(a, me, c), to) for j, to in enumerate(others)]
            for cp in sends:
                cp.start()
            started += sends
        for a in range(n):
            for j, other in enumerate(others):
                landed = block(a, other, c)
                copy(a, 1 + j, landed, landed, me).wait_recv()
                fwd = copy(a, 4 + j, landed, landed, sibling)
                fwd.start()
                started.append(fwd)
        for a in range(n):
            copy(a, 0, block(a, me, 1 - c), block(a, me, 1 - c), me).wait_recv()
            copy(a, 7, block(a, me, c), block(a, me, c), me).wait_recv()
            for j, other in enumerate(others):
                got = block(a, other, 1 - c)
                copy(a, 4 + j, got, got, me).wait_recv()
        for cp in started:
            cp.wait_send()

    return pl.pallas_call(
        body, name=name, in_specs=[ANY] * n, out_specs=[ANY] * n,
        out_shape=[jax.ShapeDtypeStruct((N_CHIPS,) + s.shape, s.dtype) for s in shards],
        scratch_shapes=[pltpu.SemaphoreType.DMA((per * n,)), pltpu.SemaphoreType.DMA((per * n,))],
    )(*shards)


def _remote(src, dst, send_sems, recv_sems, i, to):
    return pltpu.make_async_remote_copy(
        src_ref=src, dst_ref=dst, send_sem=send_sems.at[i], recv_sem=recv_sems.at[i], device_id=to, device_id_type=MESH)


def _symmetric_plan(copies):
    def plan(in_refs, out_refs, send_sems, recv_sems):
        sends = [_remote(src, dst, send_sems, recv_sems, i, to) for i, (src, dst, to) in enumerate(copies(in_refs, out_refs))]
        return sends, sends
    return plan


def _halves_exchange(grads):
    def copies(in_refs, out_refs):
        me = _place()
        return [(g.at[kk, 1 - me[2]], got.at[kk], _flip(me, 1)) for g, got in zip(in_refs, out_refs) for kk in range(N_CHIPS)]

    return _Exchange(grads, [jax.ShapeDtypeStruct((N_CHIPS,) + g.shape[2:], g.dtype) for g in grads],
                     N_CHIPS * len(grads), _symmetric_plan(copies))


def _chips_exchange(parts):
    def copies(in_refs, out_refs):
        me = _place()
        return [(p.at[_chip_index(_flip(me, 2 * j))], got.at[j - 1], _flip(me, 2 * j))
                for p, got in zip(in_refs, out_refs) for j in (1, 2, 3)]

    return _Exchange(parts, [jax.ShapeDtypeStruct((3,) + p.shape[1:], p.dtype) for p in parts], 3 * len(parts),
                     _symmetric_plan(copies))


def _siblings_exchange(halves):
    def copies(in_refs, out_refs):
        sibling = _flip(_place(), 1)
        return [(h, got, sibling) for h, got in zip(in_refs, out_refs)]

    return _Exchange(halves, [jax.ShapeDtypeStruct(h.shape, h.dtype) for h in halves], len(halves), _symmetric_plan(copies))


def _gather_over_ici(shards):
    def copies(in_refs, out_refs):
        me = _place()
        c = me[2]
        return [(w.at[c], out.at[_chip_index(me), c], _flip(me, 2 * j)) for w, out in zip(in_refs, out_refs) for j in (1, 2, 3)]

    def plan(in_refs, out_refs, send_sems, recv_sems):
        me = _place()
        sends = [_remote(src, dst, send_sems, recv_sems, i, to) for i, (src, dst, to) in enumerate(copies(in_refs, out_refs))]
        lands = [out.at[_chip_index(_flip(me, 2 * j)), me[2]] for out in out_refs for j in (1, 2, 3)]
        return sends, [_remote(z, z, send_sems, recv_sems, i, me) for i, z in enumerate(lands)]

    return _Exchange(shards, [jax.ShapeDtypeStruct((N_CHIPS,) + s.shape, s.dtype) for s in shards], 3 * len(shards), plan)


def _gather_over_d2d(shards, gathered):
    n = len(shards)

    def plan(in_refs, out_refs, send_sems, recv_sems):
        me = _place()
        c = me[2]
        sibling = _flip(me, 1)
        mine = _chip_index(me)
        sends, recvs = [], []
        for a, (w, out) in enumerate(zip(in_refs[:n], out_refs)):
            moves = [(w.at[c], (mine, c)), (w.at[1 - c], (mine, 1 - c))]
            moves += [(out.at[_chip_index(_flip(me, 2 * j)), c], (_chip_index(_flip(me, 2 * j)), c)) for j in (1, 2, 3)]
            for k, (src, (chip, half)) in enumerate(moves):
                sends.append(_remote(src, out.at[chip, half], send_sems, recv_sems, 5 * a + k, sibling))
            lands = [(mine, 1 - c), (mine, c)] + [(_chip_index(_flip(me, 2 * j)), 1 - c) for j in (1, 2, 3)]
            for k, (chip, half) in enumerate(lands):
                z = out.at[chip, half]
                recvs.append(_remote(z, z, send_sems, recv_sems, 5 * a + k, me))
        return sends, recvs

    return _Exchange(list(shards) + list(gathered), [jax.ShapeDtypeStruct(g.shape, g.dtype) for g in gathered], 5 * n, plan,
                     aliases={n + a: a for a in range(n)})


def _row_tile(rows):
    for cand in (256, 176, 128, 64, 32, 16, 8):
        if rows % cand == 0:
            return cand
    return rows


def _pair_sum(core, grad, theirs, name):
    N, _, R, Cn = grad.shape
    tr = _row_tile(R)

    def body(core_ref, g_ref, t_ref, o_ref, ob_ref):
        s = g_ref[...] + t_ref[...]
        o_ref[...] = s
        ob_ref[...] = s.astype(BF16)

    out = pl.BlockSpec((None, tr, Cn), lambda k, i, core_ref: (k, i, 0))
    return pl.pallas_call(
        body, name=name,
        grid_spec=pltpu.PrefetchScalarGridSpec(
            num_scalar_prefetch=1, grid=(N, R // tr),
            in_specs=[pl.BlockSpec((None, None, tr, Cn), lambda k, i, core_ref: (k, core_ref[0], i, 0)),
                      pl.BlockSpec((None, tr, Cn), lambda k, i, core_ref: (k, i, 0))],
            out_specs=[out, out]),
        out_shape=[jax.ShapeDtypeStruct((N, R, Cn), F32), jax.ShapeDtypeStruct((N, R, Cn), BF16)],
        compiler_params=_params("parallel", "parallel"),
    )(core, grad, theirs)


def _chip_sum(chip, parts, landed, name):
    _, R, Cn = parts.shape
    tr = _row_tile(R)

    def body(chip_ref, p_ref, l_ref, o_ref):
        o_ref[...] = ((p_ref[...] + l_ref[0].astype(F32)) + l_ref[1].astype(F32)) + l_ref[2].astype(F32)

    return pl.pallas_call(
        body, name=name,
        grid_spec=pltpu.PrefetchScalarGridSpec(
            num_scalar_prefetch=1, grid=(R // tr,),
            in_specs=[pl.BlockSpec((None, tr, Cn), lambda i, chip_ref: (chip_ref[0], i, 0)),
                      pl.BlockSpec((3, tr, Cn), lambda i, chip_ref: (0, i, 0))],
            out_specs=pl.BlockSpec((tr, Cn), lambda i, chip_ref: (i, 0))),
        out_shape=jax.ShapeDtypeStruct((R, Cn), F32), compiler_params=_params("parallel"),
    )(chip, parts, landed)


def _pair_sums(core, grads, theirs, tag):
    return [_pair_sum(core, g, t, f"{tag}_pair_sum_{i}") for i, (g, t) in enumerate(zip(grads, theirs))]


def _chip_sums(chip, parts, landed, tag):
    return [_chip_sum(chip, p[0], l, f"{tag}_chip_sum_{i}") for i, (p, l) in enumerate(zip(parts, landed))]


def _by_chip_rows(g):
    return g.reshape(N_CHIPS, 2, g.shape[0] // (2 * N_CHIPS), g.shape[1])


def _by_chip_cols(g):
    return g.reshape(N_CHIPS, 2, g.shape[1] // 2, g.shape[2])


def _adamw_halves(core, w, g_mine, g_theirs, m, v, name):
    R2, Cn = w.shape
    r = R2 // 2
    tr = _row_tile(r)
    nt = r // tr

    def body(core_ref, w_ref, gm_ref, gt_ref, m_ref, v_ref, g_ref, d_ref, nm_ref, nv_ref):
        gv = jnp.where(pl.program_id(0) == core_ref[0], gm_ref[...], gt_ref[...])
        g_ref[...] = gv
        m_new = ADAM_B1 * m_ref[...] + (1.0 - ADAM_B1) * gv
        v_new = ADAM_B2 * v_ref[...] + (1.0 - ADAM_B2) * (gv * gv)
        m_hat = m_new / (1.0 - ADAM_B1 ** ADAM_STEP)
        v_hat = v_new / (1.0 - ADAM_B2 ** ADAM_STEP)
        d_ref[...] = -ADAM_LR * (m_hat / (jnp.sqrt(v_hat) + ADAM_EPS) + ADAM_WD * w_ref[...])
        nm_ref[...] = m_new
        nv_ref[...] = v_new

    full = pl.BlockSpec((tr, Cn), lambda hf, i, core_ref: (hf * nt + i, 0))
    half = pl.BlockSpec((tr, Cn), lambda hf, i, core_ref: (i, 0))
    shp = jax.ShapeDtypeStruct((R2, Cn), F32)
    return pl.pallas_call(
        body, name=name,
        grid_spec=pltpu.PrefetchScalarGridSpec(
            num_scalar_prefetch=1, grid=(2, nt), in_specs=[full, half, half, full, full], out_specs=[full] * 4),
        out_shape=[shp] * 4, compiler_params=_params("parallel", "parallel"),
    )(core, w, g_mine, g_theirs, m, v)


def _pad_row(v, width):
    v = v.reshape(1, -1)
    return jnp.pad(v, ((0, 0), (0, width - v.shape[1])))


def _ffn_forward(x, ng, shift, scale, gate, w_in4, w_out, tag, gather=None):
    h = _rmsmod_fwd(x, ng, shift, scale, f"{tag}_norm")
    gathered = None
    if gather:
        (zg, zu, a), partly = _ffn_in_fwd(h, w_in4, f"{tag}_in", exchange=_gather_over_ici(gather))
        (x_new, f), gathered = _proj_out_fwd([a], w_out, x, gate, 0.5, f"{tag}_out", exchange=_gather_over_d2d(gather, partly))
    else:
        zg, zu, a = _ffn_in_fwd(h, w_in4, f"{tag}_in")
        x_new, f = _proj_out_fwd([a], w_out, x, gate, 0.5, f"{tag}_out")
    return x_new, (h, zg, zu, a, f), gathered


def _ffn_backward(df, saved, w_in4, w_out, core, chip, tag, riding=None):
    h, zg, zu, a, _ = saved
    rode = None
    if riding:
        (dzg, dzu), rode = _dact_bwd(df, w_out, zg, zu, f"{tag}_dact", exchange=riding)
    else:
        dzg, dzu = _dact_bwd(df, w_out, zg, zu, f"{tag}_dact")
    g_out = [_by_chip_rows(_wgrad(a, [df], df.shape[1], f"{tag}_dw_out")[0].reshape(a.shape[1], df.shape[1]))]
    (dw_in,), theirs_out = _wgrad(h, [dzg, dzu], FF_SHARD, f"{tag}_dw_in", exchange=_halves_exchange(g_out))
    g_in = [_by_chip_cols(dw_in.reshape(N_CHIPS, h.shape[1], FF_SHARD))]
    parts_out = _pair_sums(core, g_out, theirs_out, f"{tag}_out")
    (dh,), (theirs_in, landed_out) = _ffn_in_dgrad(
        dzg, dzu, w_in4, f"{tag}_dh", exchange=[_halves_exchange(g_in), _chips_exchange([parts_out[0][1]])])
    parts_in = _pair_sums(core, g_in, theirs_in, f"{tag}_in")
    return dh, parts_in, _chip_sums(chip, parts_out, landed_out, f"{tag}_out"), rode


def kernel(x, c, w_ada, b_ada, norm_g, w_ffn1_in, w_ffn1_out, w_ffn2_in, w_ffn2_out, w_mix_in, w_mix_out, hgrn_lb, hgrn_norm_g, qk_norm_g, attn_sink, rel_bias, loss_target, m_w_ada, m_b_ada, m_norm_g, m_w_ffn1_in, m_w_ffn1_out, m_w_ffn2_in, m_w_ffn2_out, m_w_mix_in, m_w_mix_out, m_hgrn_lb, m_hgrn_norm_g, m_qk_norm_g, m_attn_sink, m_rel_bias, v_w_ada, v_b_ada, v_norm_g, v_w_ffn1_in, v_w_ffn1_out, v_w_ffn2_in, v_w_ffn2_out, v_w_mix_in, v_w_mix_out, v_hgrn_lb, v_hgrn_norm_g, v_qk_norm_g, v_attn_sink, v_rel_bias):
    D = D_MODEL
    S = x.shape[1]
    place = (lax.axis_index("x"), lax.axis_index("y"), lax.axis_index("c"))
    me, my_chip = _dev_index(place), _chip_index(place)
    x0 = x[0]
    target = loss_target[0]

    def halves(w):
        return w.astype(BF16).reshape(2, w.shape[0] // 2, w.shape[1])

    gathered = _weights_allgather([halves(w_ffn1_in[0]), halves(w_ffn1_out[0])], "weights_allgather")
    w1_in = gathered[0].reshape(N_CHIPS, D, FF_SHARD)
    w1_out = gathered[1].reshape(D_FF, D)
    later = [halves(w_mix_in[0]), halves(w_mix_out[0]), halves(w_ffn2_in[0]), halves(w_ffn2_out[0])]
    core_arr = jnp.reshape(place[2], (1,)).astype(jnp.int32)
    chip_arr = jnp.reshape(my_chip, (1,)).astype(jnp.int32)

    small = jnp.concatenate([_pad_row(c, D), _pad_row(norm_g, D), _pad_row(hgrn_lb, D), jnp.zeros((5, D), F32)], axis=0)
    small_all = _allgather8(small, "small_allgather")
    c_all = small_all[:, 0, :]
    by_chip = small_all[0::2]
    norm_g_full = by_chip[:, 1, :3 * 256].reshape(N_CHIPS, 3, 256).transpose(1, 0, 2).reshape(3, D)
    lb_raw = by_chip[:, 2, :2 * 2 * 128].reshape(N_CHIPS, 2, 2, 128).transpose(1, 2, 0, 3).reshape(2, 2, HG_WIDTH)
    lb = jax.nn.sigmoid(lb_raw[:, 0, :] - lb_raw[:, 1, :])
    lb_f, lb_b = lb[0:1], lb[1:2]

    c_act_all = c_all * jax.nn.sigmoid(c_all)
    n_ada = w_ada.shape[2]
    b_mine = lax.dynamic_slice_in_dim(b_ada, my_chip * n_ada, n_ada, axis=1)
    mods_part = _ada_fwd(c_act_all, w_ada[0], b_mine, "ada_fwd")
    mods_all = _allgather8(mods_part, "mods_allgather")[0::2].transpose(1, 0, 2).reshape(8, N_MOD * D)
    mods = lax.dynamic_slice_in_dim(mods_all, me, 1, axis=0)
    sh1, sc1, g1, sh2, sc2, g2, sh3, sc3, g3 = [mods[:, i * D:(i + 1) * D] for i in range(N_MOD)]

    x1, saved1, gathered = _ffn_forward(x0, norm_g_full[0:1], sh1, sc1, g1, w1_in, w1_out, "ffn1", gather=later)
    wm_in = gathered[0].reshape(N_CHIPS, D, D_IN // N_CHIPS).transpose(1, 0, 2).reshape(D, D_IN)
    wm_out = gathered[1].reshape(D, D)
    w2_in = gathered[2].reshape(N_CHIPS, D, FF_SHARD)
    w2_out = gathered[3].reshape(D_FF, D)

    h2 = _rmsmod_fwd(x1, norm_g_full[1:2], sh2, sc2, "mix_norm")
    z = _matmul_nn(h2, wm_in, F32, 256, "mix_in")
    of, st_f = _hgrn_fwd(z, lb_f, 0, "hgrn_fwd_f")
    ob, st_b = _hgrn_fwd(z, lb_b, 1, "hgrn_fwd_b")
    o_h = _hgrn_post_fwd(of, ob, z, hgrn_norm_g, "hgrn_post")

    def to_heads(t, nh):
        return t.reshape(S, nh, ATT_HEAD_DIM).transpose(1, 0, 2)

    aq = to_heads(z[:, 5 * HG_WIDTH:5 * HG_WIDTH + ATT_WIDTH], ATT_Q_HEADS)
    ak = to_heads(z[:, 5 * HG_WIDTH + ATT_WIDTH:5 * HG_WIDTH + ATT_WIDTH + KV_WIDTH], ATT_KV_HEADS)
    av = to_heads(z[:, 5 * HG_WIDTH + ATT_WIDTH + KV_WIDTH:], ATT_KV_HEADS)
    q_g, k_g = qk_norm_g[0, 0:1], qk_norm_g[0, 1:2]
    sink_b = jnp.broadcast_to(attn_sink.reshape(ATT_Q_HEADS, 1, 1), (ATT_Q_HEADS, 1, BLOCK))
    bias = _bias_table(rel_bias, "bias_table")
    o_attn = _attn_fwd(aq, ak, av, q_g, k_g, sink_b, bias, "attn_fwd")
    o_a = o_attn.transpose(1, 0, 2).reshape(S, ATT_WIDTH).astype(BF16)
    x2, mixed = _proj_out_fwd([o_h, o_a], wm_out, x1, g2, 1.0, "mix_out")

    x3, saved3, _ = _ffn_forward(x2, norm_g_full[2:3], sh3, sc3, g3, w2_in, w2_out, "ffn2")

    dx3, df3, dg3, sq_cols = _loss_bwd(x3, target, saved3[4], g3, 0.5, "loss")
    loss_mine = 0.5 * jnp.sum(sq_cols) / D

    dh3, parts2, mine2_out, _ = _ffn_backward(df3, saved3, w2_in, w2_out, core_arr, chip_arr, "ffn2")
    dx2, dsh3, dsc3, dng3, dmixed, dg2 = _rmsmod_bwd(dh3, x2, norm_g_full[2:3], sc3, dx3, "ffn2_norm_bwd", below=(mixed, g2, 1.0))

    (do_cat,) = _matmul_nt(dmixed, wm_out, ROW_TILE, "mix_out_dgrad")
    dwm_out = _wgrad_rows([o_h, o_a], dmixed, "mix_out_dw").reshape(D, D)

    do_sum, dgr, d_hnorm = _hgrn_post_bwd(do_cat, of, ob, z, hgrn_norm_g, "hgrn_post_bwd")
    (dq_f, dff, dv_f, doml_f), landed2 = _hgrn_bwd(z, lb_f, do_sum, st_f, 0, "hgrn_bwd_f",
                                                   exchange=_chips_exchange([p[1] for p in parts2]))
    mine2 = _chip_sums(chip_arr, parts2, landed2, "ffn2_in") + mine2_out
    (dhq, dfb, dhi, doml_b), theirs2 = _hgrn_bwd(z, lb_b, do_sum, st_b, 1, "hgrn_bwd_b", acc=(dq_f, dv_f),
                                                 exchange=_siblings_exchange(mine2))

    do_a = to_heads(do_cat[:, HG_WIDTH:], ATT_Q_HEADS)
    daq, dkw, dvw, ds_sum, dsink, dqg = _attn_bwd(aq, ak, av, q_g, k_g, sink_b, bias, do_a, "attn_bwd")
    dak, dav, dkg = _attn_kv_reduce(dkw, dvw, ak, k_g, "attn_kv_reduce")
    d_rel_bias = jnp.sum(_bias_grad(ds_sum, "bias_grad"), axis=-1).T

    def from_heads(t):
        return t.transpose(1, 0, 2).reshape(S, -1)

    dz = jnp.concatenate([dhq, dff, dfb, dhi, dgr, from_heads(daq), from_heads(dak), from_heads(dav)], axis=1).astype(BF16)
    dwm_in = _wgrad(h2, [dz], D_IN // 2, "mix_in_dw")[0][0]
    dwm_in = jnp.concatenate([dwm_in[0], dwm_in[1]], axis=1)
    wide = D_IN // N_CHIPS
    grads_m = [_by_chip_cols(dwm_in.reshape(D, N_CHIPS, wide).transpose(1, 0, 2)), _by_chip_rows(dwm_out)]
    (dh2,), theirs_m = _matmul_nt(dz, wm_in, 256, "mix_in_dgrad", exchange=_halves_exchange(grads_m))
    parts_m = _pair_sums(core_arr, grads_m, theirs_m, "mix")
    dx1, dsh2, dsc2, dng2, df1, dg1 = _rmsmod_bwd(dh2, x1, norm_g_full[1:2], sc2, dx2, "mix_norm_bwd", below=(saved1[4], g1, 0.5))

    dh1, parts1, mine1_out, landed_m = _ffn_backward(df1, saved1, w1_in, w1_out, core_arr, chip_arr, "ffn1",
                                                     riding=_chips_exchange([p[1] for p in parts_m]))
    mine_m = _chip_sums(chip_arr, parts_m, landed_m, "mix")
    (dx0, dsh1, dsc1, dng1), landed1 = _rmsmod_bwd(dh1, x0, norm_g_full[0:1], sc1, dx1, "ffn1_norm_bwd",
                                                   exchange=_chips_exchange([p[1] for p in parts1]))
    mine1 = _chip_sums(chip_arr, parts1, landed1, "ffn1_in") + mine1_out
    theirs_1m = list(_run_exchange(_siblings_exchange(mine1 + mine_m), "siblings_exchange"))
    reduced = list(zip(mine1 + mine2 + mine_m, theirs_1m[:2] + list(theirs2) + theirs_1m[2:]))

    dlb = -jnp.concatenate([doml_f, doml_b], axis=0)
    dlb_raw = dlb * lb * (1.0 - lb)
    d_hgrn_lb = jnp.stack([dlb_raw, -dlb_raw], axis=1)
    d_qk = jnp.concatenate([jnp.sum(dqg, axis=0), jnp.sum(dkg, axis=0)], axis=0)
    dmods = jnp.concatenate([dsh1, dsc1, dg1, dsh2, dsc2, dg2, dsh3, dsc3, dg3], axis=0)
    packed = jnp.concatenate(
        [dmods, dng1, dng2, dng3, d_hgrn_lb.reshape(2, D), _pad_row(d_hnorm, D), _pad_row(d_qk, D),
         _pad_row(dsink[:, 0, 0], D), _pad_row(d_rel_bias, D), _pad_row(loss_mine, D)], axis=0)
    packed = jnp.pad(packed, ((0, 24 - packed.shape[0]), (0, 0)))
    packed_all, packed_sum = _allgather8(packed, "small_grads_allgather", reduce=True)
    dmods_all = packed_all[:, 0:N_MOD, :].reshape(8, N_MOD * D)
    g_b_ada = packed_sum[0:N_MOD].reshape(1, N_MOD * D)
    g_norm_full = packed_sum[9:12]
    g_norm_g = lax.dynamic_slice_in_dim(g_norm_full, my_chip * 256, 256, axis=1).reshape(1, 3, 256)
    g_hgrn_lb = lax.dynamic_slice_in_dim(packed_sum[12:14].reshape(2, 2, HG_WIDTH), my_chip * 128, 128, axis=2)
    g_hgrn_norm_g = packed_sum[14:15, :HG_WIDTH]
    g_qk_norm_g = packed_sum[15, :2 * ATT_HEAD_DIM].reshape(1, 2, ATT_HEAD_DIM)
    g_attn_sink = packed_sum[16:17, :ATT_Q_HEADS]
    g_rel_bias = packed_sum[17, :NUM_BUCKETS * ATT_Q_HEADS].reshape(NUM_BUCKETS, ATT_Q_HEADS)
    loss = packed_sum[18, 0]

    dm_mine = lax.dynamic_slice_in_dim(dmods_all, my_chip * n_ada, n_ada, axis=1)
    g_w_ada = _ada_wgrad(c_act_all.T, dm_mine, "ada_wgrad")[None]

    def big(w, g, m, v, name):
        d, nm, nv = _adamw(w[0], g[0], m[0], v[0], name)
        return d[None], nm[None], nv[None]

    def big_halves(w, g_pair, m, v, name):
        g, d, nm, nv = _adamw_halves(core_arr, w[0], g_pair[0], g_pair[1], m[0], v[0], name)
        return g[None], (d[None], nm[None], nv[None])

    g_w1_in, u_w1_in = big_halves(w_ffn1_in, reduced[0], m_w_ffn1_in, v_w_ffn1_in, "adamw_w_ffn1_in")
    g_w1_out, u_w1_out = big_halves(w_ffn1_out, reduced[1], m_w_ffn1_out, v_w_ffn1_out, "adamw_w_ffn1_out")
    g_w2_in, u_w2_in = big_halves(w_ffn2_in, reduced[2], m_w_ffn2_in, v_w_ffn2_in, "adamw_w_ffn2_in")
    g_w2_out, u_w2_out = big_halves(w_ffn2_out, reduced[3], m_w_ffn2_out, v_w_ffn2_out, "adamw_w_ffn2_out")
    g_wm_in, u_wm_in = big_halves(w_mix_in, reduced[4], m_w_mix_in, v_w_mix_in, "adamw_w_mix_in")
    g_wm_out, u_wm_out = big_halves(w_mix_out, reduced[5], m_w_mix_out, v_w_mix_out, "adamw_w_mix_out")

    smalls = [(b_ada, g_b_ada, m_b_ada, v_b_ada), (norm_g, g_norm_g, m_norm_g, v_norm_g), (hgrn_lb, g_hgrn_lb, m_hgrn_lb, v_hgrn_lb),
              (hgrn_norm_g, g_hgrn_norm_g, m_hgrn_norm_g, v_hgrn_norm_g), (qk_norm_g, g_qk_norm_g, m_qk_norm_g, v_qk_norm_g),
              (attn_sink, g_attn_sink, m_attn_sink, v_attn_sink), (rel_bias, g_rel_bias, m_rel_bias, v_rel_bias)]
    sizes = [t[0].size for t in smalls]
    total = sum(sizes)
    rows = -(-total // 128)
    rows = -(-rows // 8) * 8

    def pack(i):
        flat = jnp.concatenate([t[i].reshape(-1) for t in smalls])
        fill = 1.0 if i == 3 else 0.0
        return jnp.pad(flat, (0, rows * 128 - total), constant_values=fill).reshape(rows, 128)

    packed_out = _adamw(pack(0), pack(1), pack(2), pack(3), "adamw_small")

    def unpack(flat2d):
        flat = flat2d.reshape(-1)
        outs, off = [], 0
        for t, n in zip(smalls, sizes):
            outs.append(flat[off:off + n].reshape(t[0].shape))
            off += n
        return outs

    d_small, m_small, v_small = [unpack(t) for t in packed_out]

    upd = {
        "w_ada": big(w_ada, g_w_ada, m_w_ada, v_w_ada, "adamw_w_ada"),
        "w_ffn1_in": u_w1_in, "w_ffn1_out": u_w1_out, "w_ffn2_in": u_w2_in, "w_ffn2_out": u_w2_out,
        "w_mix_in": u_wm_in, "w_mix_out": u_wm_out,
    }
    small_names = ["b_ada", "norm_g", "hgrn_lb", "hgrn_norm_g", "qk_norm_g", "attn_sink", "rel_bias"]
    for i, nme in enumerate(small_names):
        upd[nme] = (d_small[i], m_small[i], v_small[i])
    grads = {
        "w_ada": g_w_ada, "b_ada": g_b_ada, "norm_g": g_norm_g, "w_ffn1_in": g_w1_in, "w_ffn1_out": g_w1_out,
        "w_ffn2_in": g_w2_in, "w_ffn2_out": g_w2_out, "w_mix_in": g_wm_in, "w_mix_out": g_wm_out, "hgrn_lb": g_hgrn_lb,
        "hgrn_norm_g": g_hgrn_norm_g, "qk_norm_g": g_qk_norm_g, "attn_sink": g_attn_sink, "rel_bias": g_rel_bias,
    }
    order = ["w_ada", "b_ada", "norm_g", "w_ffn1_in", "w_ffn1_out", "w_ffn2_in", "w_ffn2_out", "w_mix_in", "w_mix_out",
             "hgrn_lb", "hgrn_norm_g", "qk_norm_g", "attn_sink", "rel_bias"]
    return (loss, dx0[None], *[grads[k] for k in order], *[upd[k][0] for k in order], *[upd[k][1] for k in order],
            *[upd[k][2] for k in order])
```

```python
import functools
import math

import numpy as np
import jax
import jax.numpy as jnp
from jax import lax
from jax.experimental import pallas as pl
from jax.experimental.pallas import tpu as pltpu

F32, BF16 = jnp.float32, jnp.bfloat16

D_MODEL = 1024
D_FF = 2816
HG_HEADS, HG_DIM = 4, 128
HG_WIDTH = HG_HEADS * HG_DIM
ATT_Q_HEADS, ATT_KV_HEADS, ATT_HEAD_DIM = 8, 2, 64
ATT_GROUP = ATT_Q_HEADS // ATT_KV_HEADS
ATT_WIDTH = ATT_Q_HEADS * ATT_HEAD_DIM
KV_WIDTH = ATT_KV_HEADS * ATT_HEAD_DIM
WINDOW, BLOCK = 128, 128
NUM_BUCKETS, MAX_DISTANCE = 32, 128
N_MOD = 9
EPS = 1e-6
D_IN = 5 * HG_WIDTH + ATT_WIDTH + 2 * KV_WIDTH
ADAM_LR, ADAM_B1, ADAM_B2, ADAM_EPS, ADAM_WD, ADAM_STEP = 0.001, 0.9, 0.999, 1e-08, 0.01, 10

N_CHIPS = 4
FF_SHARD = 2 * D_FF // N_CHIPS
NEG = -1e30

VMEM_LIMIT_BYTES = 56 << 20
ROW_TILE = 512
HG_CHUNK = 16
HG_ROWS = 256

MESH = pl.DeviceIdType.MESH
ANY = pl.BlockSpec(memory_space=pl.ANY)


def _params(*sem):
    return pltpu.CompilerParams(dimension_semantics=sem, vmem_limit_bytes=VMEM_LIMIT_BYTES)


def _resident(shape, index_map):
    return pl.BlockSpec(shape, index_map, pipeline_mode=pl.Buffered(1))


def _dot(a, b, dims, precision=None):
    return lax.dot_general(a, b, (dims, ((), ())), precision=precision, preferred_element_type=F32)


def _nn(a, b, precision=None):
    return _dot(a, b, ((1,), (0,)), precision)


def _nt(a, b):
    return _dot(a, b, ((1,), (1,)))


def _tn(a, b):
    return _dot(a, b, ((0,), (0,)))


def _sigmoid(x):
    return jax.nn.sigmoid(x)


class _Exchange:
    def __init__(self, inputs, out_shapes, n_sems, plan, aliases=None):
        self.inputs, self.out_shapes, self.n_sems, self.plan, self.aliases = list(inputs), list(out_shapes), n_sems, plan, aliases or {}

    def sem_shapes(self):
        return [pltpu.SemaphoreType.DMA((self.n_sems,)), pltpu.SemaphoreType.DMA((self.n_sems,))]

    def start(self, in_refs, out_refs, send_sems, recv_sems):
        for cp in self.plan(in_refs, out_refs, send_sems, recv_sems)[0]:
            cp.start()

    def finish(self, in_refs, out_refs, send_sems, recv_sems):
        sends, recvs = self.plan(in_refs, out_refs, send_sems, recv_sems)
        for cp in recvs:
            cp.wait_recv()
        for cp in sends:
            cp.wait_send()


def _run_exchange(ex, name):
    n_in, n_out = len(ex.inputs), len(ex.out_shapes)

    def body(*refs):
        in_refs, out_refs, (send_sems, recv_sems) = refs[:n_in], refs[n_in:n_in + n_out], refs[n_in + n_out:]
        ex.start(in_refs, out_refs, send_sems, recv_sems)
        ex.finish(in_refs, out_refs, send_sems, recv_sems)

    return pl.pallas_call(
        body, name=name, in_specs=[ANY] * n_in, out_specs=[ANY] * n_out, out_shape=ex.out_shapes,
        scratch_shapes=ex.sem_shapes(), input_output_aliases=dict(ex.aliases),
    )(*ex.inputs)


def _call(body, *, name, grid, in_specs, out_specs, out_shape, args, semantics, scratch_shapes=(), exchange=None):
    if exchange is None:
        return pl.pallas_call(
            body, name=name, grid=grid, in_specs=in_specs, out_specs=out_specs, out_shape=out_shape,
            scratch_shapes=list(scratch_shapes), compiler_params=_params(*semantics))(*args)
    exs = exchange if isinstance(exchange, (list, tuple)) else [exchange]
    n_in, n_out, n_scr = len(in_specs), len(out_specs), len(scratch_shapes)
    x_in, x_out = [len(ex.inputs) for ex in exs], [len(ex.out_shapes) for ex in exs]

    def take(refs, counts):
        groups = []
        for n in counts:
            groups.append(refs[:n])
            refs = refs[n:]
        return groups, refs

    def carrier(*refs):
        ins, refs = refs[:n_in], refs[n_in:]
        x_ins, refs = take(refs, x_in)
        outs, refs = refs[:n_out], refs[n_out:]
        x_outs, refs = take(refs, x_out)
        scr, refs = refs[:n_scr], refs[n_scr:]
        sems, _ = take(refs, [2] * len(exs))
        ids = [pl.program_id(a) for a in range(len(grid))]
        first = functools.reduce(jnp.logical_and, [i == 0 for i in ids])
        last = functools.reduce(jnp.logical_and, [i == g - 1 for i, g in zip(ids, grid)])

        @pl.when(first)
        def _():
            for ex, xi, xo, (send_sems, recv_sems) in zip(exs, x_ins, x_outs, sems):
                ex.start(xi, xo, send_sems, recv_sems)

        body(*ins, *outs, *scr)

        @pl.when(last)
        def _():
            for ex, xi, xo, (send_sems, recv_sems) in zip(exs, x_ins, x_outs, sems):
                ex.finish(xi, xo, send_sems, recv_sems)

    aliases, i0, o0 = {}, n_in, n_out
    for ex in exs:
        aliases.update({i0 + i: o0 + o for i, o in ex.aliases.items()})
        i0, o0 = i0 + len(ex.inputs), o0 + len(ex.out_shapes)
    res = pl.pallas_call(
        carrier, name=name, grid=grid, in_specs=list(in_specs) + [ANY] * sum(x_in),
        out_specs=list(out_specs) + [ANY] * sum(x_out),
        out_shape=list(out_shape) + [s for ex in exs for s in ex.out_shapes],
        scratch_shapes=list(scratch_shapes) + [s for ex in exs for s in ex.sem_shapes()],
        input_output_aliases=aliases, compiler_params=_params(*["arbitrary"] * len(grid)),
    )(*args, *[a for ex in exs for a in ex.inputs])
    x_res, _ = take(list(res[n_out:]), x_out)
    return list(res[:n_out]), (x_res if isinstance(exchange, (list, tuple)) else x_res[0])


def _rmsmod_fwd(x, g, shift, scale, name):
    S, D = x.shape
    tr = min(ROW_TILE, S)

    def body(x_ref, g_ref, sh_ref, sc_ref, h_ref):
        xv = x_ref[...]
        rstd = lax.rsqrt(jnp.mean(xv * xv, axis=-1, keepdims=True) + EPS)
        y = xv * rstd * g_ref[...]
        h_ref[...] = (y * (1.0 + sc_ref[...]) + sh_ref[...]).astype(h_ref.dtype)

    row = pl.BlockSpec((tr, D), lambda i: (i, 0))
    vec = pl.BlockSpec((1, D), lambda i: (0, 0))
    return pl.pallas_call(
        body, name=name, grid=(S // tr,), in_specs=[row, vec, vec, vec], out_specs=row,
        out_shape=jax.ShapeDtypeStruct((S, D), BF16), compiler_params=_params("parallel"),
    )(x, g, shift, scale)


class _NormBwd:
    def __init__(self, x, g, scale, dx_res, below=None):
        S, D = x.shape
        self.below, self.coef = below, (below[2] if below else None)
        self.inputs = [x, g, scale, dx_res] + ([below[0], below[1]] if below else [])
        vshape = jax.ShapeDtypeStruct((1, D), F32)
        self.out_shape = [jax.ShapeDtypeStruct((S, D), F32), vshape, vshape, vshape]
        if below:
            self.out_shape += [jax.ShapeDtypeStruct((S, D), BF16), vshape]

    def specs(self, tr, D):
        row = pl.BlockSpec((tr, D), lambda i: (i, 0))
        vec = pl.BlockSpec((1, D), lambda i: (0, 0))
        return ([row, vec, vec, row] + ([row, vec] if self.below else []),
                [row, vec, vec, vec] + ([row, vec] if self.below else []))

    def step(self, dhv, in_refs, out_refs):
        if self.below:
            x_ref, g_ref, sc_ref, dxr_ref, f_ref, gate_ref = in_refs
            dx_ref, dsh_ref, dsc_ref, dg_ref, df_ref, dgate_ref = out_refs
            sums = (dsh_ref, dsc_ref, dg_ref, dgate_ref)
        else:
            x_ref, g_ref, sc_ref, dxr_ref = in_refs
            dx_ref, dsh_ref, dsc_ref, dg_ref = out_refs
            sums = (dsh_ref, dsc_ref, dg_ref)

        @pl.when(pl.program_id(0) == 0)
        def _():
            for ref in sums:
                ref[...] = jnp.zeros_like(ref)

        xv, gv = x_ref[...], g_ref[...]
        one_sc = 1.0 + sc_ref[...]
        rstd = lax.rsqrt(jnp.mean(xv * xv, axis=-1, keepdims=True) + EPS)
        n = xv * rstd
        dsh_ref[...] += jnp.sum(dhv, axis=0, keepdims=True)
        dsc_ref[...] += jnp.sum(dhv * n, axis=0, keepdims=True) * gv
        dg_ref[...] += jnp.sum(dhv * n, axis=0, keepdims=True) * one_sc
        dn = dhv * (gv * one_sc)
        dx = dxr_ref[...] + rstd * (dn - n * jnp.mean(dn * n, axis=-1, keepdims=True))
        dx_ref[...] = dx
        if self.below:
            df_ref[...] = (self.coef * gate_ref[...] * dx).astype(df_ref.dtype)
            dgate_ref[...] += self.coef * jnp.sum(dx * f_ref[...].astype(F32), axis=0, keepdims=True)


def _rmsmod_bwd(dh, norm, name, exchange=None):
    S, D = dh.shape
    tr = min(ROW_TILE, S)
    n_in = len(norm.inputs)

    def body(dh_ref, *refs):
        norm.step(dh_ref[...], refs[:n_in], refs[n_in:])

    in_specs, out_specs = norm.specs(tr, D)
    return _call(body, name=name, grid=(S // tr,), in_specs=[pl.BlockSpec((tr, D), lambda i: (i, 0))] + in_specs,
                 out_specs=out_specs, out_shape=norm.out_shape, args=[dh] + norm.inputs, semantics=("arbitrary",),
                 exchange=exchange)


def _ffn_in_fwd(h, w4, name, exchange=None):
    S, D = h.shape
    tm = min(ROW_TILE, S)
    n = w4.shape[2]

    def body(h_ref, wg_ref, wu_ref, zg_ref, zu_ref, a_ref):
        hv = h_ref[...]
        zg = _nn(hv, wg_ref[...])
        zu = _nn(hv, wu_ref[...])
        zg_ref[...] = zg.astype(zg_ref.dtype)
        zu_ref[...] = zu.astype(zu_ref.dtype)
        a_ref[...] = (zg * _sigmoid(zg) * zu).astype(a_ref.dtype)

    out = pl.BlockSpec((tm, n), lambda j, m: (m, j))
    oshape = jax.ShapeDtypeStruct((S, 2 * n), BF16)
    return _call(
        body, name=name, grid=(2, S // tm),
        in_specs=[pl.BlockSpec((tm, D), lambda j, m: (m, 0)),
                  pl.BlockSpec((None, D, n), lambda j, m: (j, 0, 0)),
                  pl.BlockSpec((None, D, n), lambda j, m: (j + 2, 0, 0))],
        out_specs=[out, out, out], out_shape=[oshape, oshape, oshape], args=(h, w4, w4),
        semantics=("parallel", "parallel"), exchange=exchange)


def _proj_out_fwd(lhs, w, x, gate, coef, name, exchange=None, next_norm=None):
    S, D = x.shape
    tm = min(ROW_TILE, S)
    ks = [a.shape[1] for a in lhs]

    def body(*refs):
        lhs_refs, refs = refs[:len(lhs)], refs[len(lhs):]
        if next_norm:
            w_ref, x_ref, gate_ref, g_ref, sh_ref, sc_ref, xn_ref, f_ref, h_ref = refs
        else:
            w_ref, x_ref, gate_ref, xn_ref, f_ref = refs
        acc, off = None, 0
        for a_ref, k in zip(lhs_refs, ks):
            part = _nn(a_ref[...], w_ref[off:off + k, :])
            acc = part if acc is None else acc + part
            off += k
        f_ref[...] = acc.astype(f_ref.dtype)
        xn = x_ref[...] + coef * gate_ref[...] * acc
        xn_ref[...] = xn
        if next_norm:
            rstd = lax.rsqrt(jnp.mean(xn * xn, axis=-1, keepdims=True) + EPS)
            h_ref[...] = (xn * rstd * g_ref[...] * (1.0 + sc_ref[...]) + sh_ref[...]).astype(h_ref.dtype)

    row = pl.BlockSpec((tm, D), lambda m: (m, 0))
    vec = pl.BlockSpec((1, D), lambda m: (0, 0))
    extra = list(next_norm) if next_norm else []
    return _call(
        body, name=name, grid=(S // tm,),
        in_specs=[pl.BlockSpec((tm, k), lambda m: (m, 0)) for k in ks]
        + [_resident(w.shape, lambda m: (0, 0)), row, vec] + [vec] * len(extra),
        out_specs=[row, row] + ([row] if next_norm else []),
        out_shape=[jax.ShapeDtypeStruct((S, D), F32), jax.ShapeDtypeStruct((S, D), BF16)]
        + ([jax.ShapeDtypeStruct((S, D), BF16)] if next_norm else []),
        args=(*lhs, w, x, gate, *extra), semantics=("parallel",), exchange=exchange)


def _proj_out_loss(lhs, w, x, gate, coef, target, name):
    S, D = x.shape
    tm = min(ROW_TILE, S)

    def body(a_ref, w_ref, x_ref, gate_ref, t_ref, dy_ref, df_ref, dgate_ref, sq_ref):
        @pl.when(pl.program_id(0) == 0)
        def _():
            dgate_ref[...] = jnp.zeros_like(dgate_ref)
            sq_ref[...] = jnp.zeros_like(sq_ref)

        f = _nn(a_ref[...], w_ref[...])
        gate = coef * gate_ref[...]
        err = x_ref[...] + gate * f - t_ref[...]
        sq_ref[...] += jnp.sum(err * err, axis=0, keepdims=True)
        dy = err * (1.0 / D)
        dy_ref[...] = dy
        df_ref[...] = (gate * dy).astype(df_ref.dtype)
        dgate_ref[...] += coef * jnp.sum(dy * f, axis=0, keepdims=True)

    row = pl.BlockSpec((tm, D), lambda m: (m, 0))
    vec = pl.BlockSpec((1, D), lambda m: (0, 0))
    vshape = jax.ShapeDtypeStruct((1, D), F32)
    return pl.pallas_call(
        body, name=name, grid=(S // tm,),
        in_specs=[pl.BlockSpec((tm, lhs.shape[1]), lambda m: (m, 0)), _resident(w.shape, lambda m: (0, 0)), row, vec, row],
        out_specs=[row, row, vec, vec],
        out_shape=[jax.ShapeDtypeStruct((S, D), F32), jax.ShapeDtypeStruct((S, D), BF16), vshape, vshape],
        compiler_params=_params("arbitrary"),
    )(lhs, w, x, gate, target)


def _matmul_nn(a, w, out_dtype, tm, name):
    S, K = a.shape
    N = w.shape[1]
    tm = min(tm, S)

    def body(a_ref, w_ref, o_ref):
        o_ref[...] = _nn(a_ref[...], w_ref[...]).astype(o_ref.dtype)

    return pl.pallas_call(
        body, name=name, grid=(S // tm,),
        in_specs=[pl.BlockSpec((tm, K), lambda m: (m, 0)), _resident((K, N), lambda m: (0, 0))],
        out_specs=pl.BlockSpec((tm, N), lambda m: (m, 0)), out_shape=jax.ShapeDtypeStruct((S, N), out_dtype),
        compiler_params=_params("parallel"),
    )(a, w)


def _dact_bwd(df, w_out, zg, zu, name, exchange=None):
    S, D = df.shape
    tm = min(ROW_TILE, S)
    n = w_out.shape[0] // 2

    def body(df_ref, w_ref, zg_ref, zu_ref, dzg_ref, dzu_ref):
        da = _nt(df_ref[...], w_ref[...])
        zg_v, zu_v = zg_ref[...].astype(F32), zu_ref[...].astype(F32)
        s = _sigmoid(zg_v)
        dzu_ref[...] = (da * zg_v * s).astype(dzu_ref.dtype)
        dzg_ref[...] = (da * zu_v * (s * (1.0 + zg_v * (1.0 - s)))).astype(dzg_ref.dtype)

    blk = pl.BlockSpec((tm, n), lambda j, m: (m, j))
    oshape = jax.ShapeDtypeStruct((S, 2 * n), BF16)
    return _call(
        body, name=name, grid=(2, S // tm),
        in_specs=[pl.BlockSpec((tm, D), lambda j, m: (m, 0)), pl.BlockSpec((n, D), lambda j, m: (j, 0)), blk, blk],
        out_specs=[blk, blk], out_shape=[oshape, oshape], args=(df, w_out, zg, zu), semantics=("parallel", "parallel"),
        exchange=exchange)


def _ffn_in_dgrad(dzg, dzu, w4, name, exchange=None, norm=None):
    S = dzg.shape[0]
    D, n = w4.shape[1], w4.shape[2]
    tm = min(ROW_TILE, S)
    n_norm = len(norm.inputs) if norm else 0

    def body(dzg_ref, dzu_ref, w_ref, *refs):
        acc = _nt(dzg_ref[:, 0:n], w_ref[0])
        acc += _nt(dzg_ref[:, n:2 * n], w_ref[1])
        acc += _nt(dzu_ref[:, 0:n], w_ref[2])
        acc += _nt(dzu_ref[:, n:2 * n], w_ref[3])
        if norm:
            norm.step(acc, refs[:n_norm], refs[n_norm:])
        else:
            refs[0][...] = acc

    blk = pl.BlockSpec((tm, 2 * n), lambda m: (m, 0))
    in_specs, args = [blk, blk, _resident(w4.shape, lambda m: (0, 0, 0))], [dzg, dzu, w4]
    out_specs, out_shape = [pl.BlockSpec((tm, D), lambda m: (m, 0))], [jax.ShapeDtypeStruct((S, D), F32)]
    if norm:
        norm_in, out_specs = norm.specs(tm, D)
        in_specs, args, out_shape = in_specs + norm_in, args + norm.inputs, norm.out_shape
    return _call(body, name=name, grid=(S // tm,), in_specs=in_specs, out_specs=out_specs, out_shape=out_shape, args=args,
                 semantics=("arbitrary",) if norm else ("parallel",), exchange=exchange)


def _matmul_nt(a, w, tm, name, exchange=None, norm=None):
    S, K = a.shape
    N = w.shape[0]
    tm = min(tm, S)
    n_norm = len(norm.inputs) if norm else 0

    def body(a_ref, w_ref, *refs):
        acc = _nt(a_ref[...], w_ref[...])
        if norm:
            norm.step(acc, refs[:n_norm], refs[n_norm:])
        else:
            refs[0][...] = acc

    in_specs, args = [pl.BlockSpec((tm, K), lambda m: (m, 0)), _resident((N, K), lambda m: (0, 0))], [a, w]
    out_specs, out_shape = [pl.BlockSpec((tm, N), lambda m: (m, 0))], [jax.ShapeDtypeStruct((S, N), F32)]
    if norm:
        norm_in, out_specs = norm.specs(tm, N)
        in_specs, args, out_shape = in_specs + norm_in, args + norm.inputs, norm.out_shape
    return _call(body, name=name, grid=(S // tm,), in_specs=in_specs, out_specs=out_specs, out_shape=out_shape, args=args,
                 semantics=("arbitrary",) if norm else ("parallel",), exchange=exchange)


def _wgrad(a, gs, tn, name, exchange=None):
    S, Ka = a.shape
    N = gs[0].shape[1]
    ts = min(ROW_TILE, S)

    def body(a_ref, *refs):
        g_refs, o_ref = refs[:-1], refs[-1]

        @pl.when(pl.program_id(1) == 0)
        def _():
            o_ref[...] = jnp.zeros_like(o_ref)

        a_t = a_ref[...].T
        for i, g_ref in enumerate(g_refs):
            o_ref[i] += _nn(a_t, g_ref[...])

    return _call(
        body, name=name, grid=(N // tn, S // ts),
        in_specs=[pl.BlockSpec((ts, Ka), lambda j, s: (s, 0))] + [pl.BlockSpec((ts, tn), lambda j, s: (s, j))] * len(gs),
        out_specs=[pl.BlockSpec((len(gs), None, Ka, tn), lambda j, s: (0, j, 0, 0))],
        out_shape=[jax.ShapeDtypeStruct((len(gs), N // tn, Ka, tn), F32)], args=(a, *gs),
        semantics=("parallel", "arbitrary"), exchange=exchange)


def _wgrad_rows(lhs, g, name):
    S, Ka = lhs[0].shape
    N = g.shape[1]
    ts = min(ROW_TILE, S)

    def body(*refs):
        a_refs, g_ref, o_ref = refs[:-2], refs[-2], refs[-1]

        @pl.when(pl.program_id(0) == 0)
        def _():
            o_ref[...] = jnp.zeros_like(o_ref)

        gv = g_ref[...]
        for i, a_ref in enumerate(a_refs):
            o_ref[i] += _tn(a_ref[...], gv)

    return pl.pallas_call(
        body, name=name, grid=(S // ts,),
        in_specs=[pl.BlockSpec((ts, Ka), lambda s: (s, 0))] * len(lhs) + [pl.BlockSpec((ts, N), lambda s: (s, 0))],
        out_specs=pl.BlockSpec((len(lhs), Ka, N), lambda s: (0, 0, 0)),
        out_shape=jax.ShapeDtypeStruct((len(lhs), Ka, N), F32), compiler_params=_params("arbitrary"),
    )(*lhs, g)


def _hgrn_chunk_common(qr, fr, oml, tri, last):
    k = oml * _sigmoid(-fr)
    g = jnp.log1p(-k) * math.log2(math.e)
    q = qr * _sigmoid(qr)
    G = _nn(tri, g, precision=lax.Precision.HIGHEST)
    Gl = G[last:last + 1]
    return q, k, G, Gl


def _hgrn_consts(reverse):
    C = HG_CHUNK
    r = lax.broadcasted_iota(jnp.int32, (C, C), 0)
    cc = lax.broadcasted_iota(jnp.int32, (C, C), 1)
    tri = ((cc >= r) if reverse else (cc <= r)).astype(F32)
    tri_t = ((cc <= r) if reverse else (cc >= r)).astype(F32)
    rid = lax.broadcasted_iota(jnp.int32, (C, HG_WIDTH), 0)
    return tri, tri_t, rid, (0 if reverse else C - 1)


def _head_slices():
    return [slice(h * HG_DIM, (h + 1) * HG_DIM) for h in range(HG_HEADS)]


def _per_head_lane_sum(x):
    C = x.shape[0]
    return jnp.concatenate(
        [jnp.broadcast_to(jnp.sum(x[:, sl], axis=-1, keepdims=True), (C, HG_DIM)) for sl in _head_slices()], axis=1)


HG_TILE = 8


def _pair_tiles(s, reverse):
    blk, r = divmod(s, HG_TILE)
    n_tiles = HG_CHUNK // HG_TILE
    others = range(0, blk) if reverse else range(blk + 1, n_tiles)
    return [(blk, r)] + [(t, None) for t in others]


def _pair_decay(G, s, tile, r, rid8, reverse, keys=False):
    rs = slice(tile * HG_TILE, (tile + 1) * HG_TILE)
    d = (G[s:s + 1] - G[rs]) if keys else (G[rs] - G[s:s + 1])
    if r is not None:
        d = jnp.where((rid8 <= r) if reverse else (rid8 >= r), d, NEG)
    return rs, jnp.exp2(d)


def _hgrn_fwd(z, lb, direction, name, exchange=None):
    S = z.shape[0]
    C, DK, W = HG_CHUNK, HG_DIM, HG_WIDTH
    tb = min(HG_ROWS, S)
    n_t, n_c = S // tb, tb // C
    reverse = direction == 1
    tmap = (lambda i: n_t - 1 - i) if reverse else (lambda i: i)

    def body(q_ref, f_ref, v_ref, lb_ref, o_ref, st_out_ref, st_ref):
        @pl.when(pl.program_id(0) == 0)
        def _():
            st_ref[...] = jnp.zeros_like(st_ref)

        oml = 1.0 - lb_ref[...]
        tri, _, _, last = _hgrn_consts(reverse)
        rid8 = lax.broadcasted_iota(jnp.int32, (HG_TILE, W), 0)

        def chunk(ci, carry):
            cidx = (n_c - 1 - ci) if reverse else ci
            rows = pl.ds(pl.multiple_of(cidx * C, C), C)
            v = v_ref[rows, :]
            q, k, G, Gl = _hgrn_chunk_common(q_ref[rows, :], f_ref[rows, :], oml, tri, last)
            qd = (q * jnp.exp2(G)).astype(BF16)
            kd = (k * jnp.exp2(Gl - G)).astype(BF16)
            e_gl = jnp.exp2(Gl)
            v_b = v.astype(BF16)
            inter = []
            for h, sl in enumerate(_head_slices()):
                st0 = st_ref[h]
                st_out_ref[h, cidx] = st0
                inter.append(_nt(qd[:, sl], st0.astype(BF16)))
                st_ref[h] = st0 * e_gl[:, sl] + _tn(v_b[:, sl], kd[:, sl])
            o = jnp.concatenate(inter, axis=1)
            o_t = [o[t * HG_TILE:(t + 1) * HG_TILE] for t in range(C // HG_TILE)]
            for s in range(C):
                k_s, v_s = k[s:s + 1], v[s:s + 1]
                for tile, r in _pair_tiles(s, reverse):
                    rs, e_s = _pair_decay(G, s, tile, r, rid8, reverse)
                    o_t[tile] = o_t[tile] + _per_head_lane_sum(q[rs] * k_s * e_s) * v_s
            o_ref[rows, :] = jnp.concatenate(o_t, axis=0)
            return carry

        lax.fori_loop(0, n_c, chunk, 0, unroll=2)

    def sec(j):
        return pl.BlockSpec((tb, W), lambda i: (tmap(i), j))

    return _call(
        body, name=name, grid=(n_t,),
        in_specs=[sec(0), sec(1 + direction), sec(3), pl.BlockSpec((1, W), lambda i: (0, 0))],
        out_specs=[sec(0), pl.BlockSpec((HG_HEADS, n_c, DK, DK), lambda i: (0, tmap(i), 0, 0))],
        out_shape=[jax.ShapeDtypeStruct((S, W), F32), jax.ShapeDtypeStruct((HG_HEADS, S // C, DK, DK), F32)],
        scratch_shapes=[pltpu.VMEM((HG_HEADS, DK, DK), F32)], args=(z, z, z, lb), semantics=("arbitrary",),
        exchange=exchange)


def _hgrn_bwd(z, lb, do, states, direction, name, acc=None, exchange=None):
    S = z.shape[0]
    C, DK, W = HG_CHUNK, HG_DIM, HG_WIDTH
    tb = min(HG_ROWS, S)
    n_t, n_c = S // tb, tb // C
    reverse = direction == 1
    tmap = (lambda i: i) if reverse else (lambda i: n_t - 1 - i)

    def body(*refs):
        if acc:
            q_ref, f_ref, v_ref, lb_ref, do_ref, st_in_ref, dqa_ref, dva_ref, dq_ref, df_ref, dv_ref, doml_ref, dst_ref = refs
        else:
            q_ref, f_ref, v_ref, lb_ref, do_ref, st_in_ref, dq_ref, df_ref, dv_ref, doml_ref, dst_ref = refs

        @pl.when(pl.program_id(0) == 0)
        def _():
            dst_ref[...] = jnp.zeros_like(dst_ref)
            doml_ref[...] = jnp.zeros_like(doml_ref)

        oml = 1.0 - lb_ref[...]
        tri, tri_t, rid, last = _hgrn_consts(reverse)
        rid8 = lax.broadcasted_iota(jnp.int32, (HG_TILE, W), 0)

        def chunk(ci, carry):
            cidx = ci if reverse else (n_c - 1 - ci)
            rows = pl.ds(pl.multiple_of(cidx * C, C), C)
            qr, fr, v, dov = q_ref[rows, :], f_ref[rows, :], v_ref[rows, :], do_ref[rows, :]
            q, k, G, Gl = _hgrn_chunk_common(qr, fr, oml, tri, last)
            e_g, e_gl, e_kd = jnp.exp2(G), jnp.exp2(Gl), jnp.exp2(Gl - G)
            qd, kd = q * e_g, k * e_kd
            do_b, v_b, qd_b, kd_b = dov.astype(BF16), v.astype(BF16), qd.astype(BF16), kd.astype(BF16)
            dqd, dkd, dv, state_dot = [], [], [], []
            for h, sl in enumerate(_head_slices()):
                st0, dst1 = st_in_ref[h, cidx], dst_ref[h]
                dst1_b = dst1.astype(BF16)
                dqd.append(_nn(do_b[:, sl], st0.astype(BF16)))
                dkd.append(_nn(v_b[:, sl], dst1_b))
                dv.append(_nt(kd_b[:, sl], dst1_b))
                state_dot.append(jnp.sum(st0 * dst1, axis=0, keepdims=True))
                dst_ref[h] = dst1 * e_gl[:, sl] + _tn(do_b[:, sl], qd_b[:, sl])
            dqd, dkd, dv = [jnp.concatenate(t, axis=1) for t in (dqd, dkd, dv)]
            d_gl = e_gl * jnp.concatenate(state_dot, axis=1) + jnp.sum(dkd * kd, axis=0, keepdims=True)
            dq, dk = dqd * e_g, dkd * e_kd
            n_tiles = C // HG_TILE
            dq_t, dk_t, dv_t = [[x[t * HG_TILE:(t + 1) * HG_TILE] for t in range(n_tiles)] for x in (dq, dk, dv)]
            for s in range(C):
                k_s, v_s = k[s:s + 1], v[s:s + 1]
                for tile, r in _pair_tiles(s, reverse):
                    rs, e_s = _pair_decay(G, s, tile, r, rid8, reverse)
                    dq_t[tile] = dq_t[tile] + _per_head_lane_sum(dov[rs] * v_s) * e_s * k_s
            for t in range(C):
                q_t, do_t = q[t:t + 1], dov[t:t + 1]
                for tile, r in _pair_tiles(t, not reverse):
                    rs, x_t = _pair_decay(G, t, tile, r, rid8, not reverse, keys=True)
                    qx = q_t * x_t
                    dv_t[tile] = dv_t[tile] + _per_head_lane_sum(k[rs] * qx) * do_t
                    dk_t[tile] = dk_t[tile] + _per_head_lane_sum(v[rs] * do_t) * qx
            dq, dk, dv = [jnp.concatenate(x, axis=0) for x in (dq_t, dk_t, dv_t)]
            d_big_g = dq * q - dk * k + jnp.where(rid == last, d_gl, 0.0)
            dg = _nn(tri_t, d_big_g, precision=lax.Precision.HIGHEST)
            dk_all = dk - dg / (1.0 - k)
            sig_nf = _sigmoid(-fr)
            df_ref[rows, :] = -dk_all * k * (1.0 - sig_nf)
            doml_ref[...] += jnp.sum(dk_all * sig_nf, axis=0, keepdims=True)
            sq = _sigmoid(qr)
            dqr = dq * (sq * (1.0 + qr * (1.0 - sq)))
            if acc:
                dqr = dqr + dqa_ref[rows, :]
                dv = dv + dva_ref[rows, :]
            dq_ref[rows, :] = dqr
            dv_ref[rows, :] = dv
            return carry

        lax.fori_loop(0, n_c, chunk, 0, unroll=2)

    def sec(j):
        return pl.BlockSpec((tb, W), lambda i: (tmap(i), j))

    vec = pl.BlockSpec((1, W), lambda i: (0, 0))
    ins = [z, z, z, lb, do, states]
    in_specs = [sec(0), sec(1 + direction), sec(3), vec, sec(0),
                pl.BlockSpec((HG_HEADS, n_c, DK, DK), lambda i: (0, tmap(i), 0, 0))]
    if acc:
        ins += list(acc)
        in_specs += [sec(0), sec(0)]
    full = jax.ShapeDtypeStruct((S, W), F32)
    return _call(
        body, name=name, grid=(n_t,), in_specs=in_specs,
        out_specs=[sec(0), sec(0), sec(0), vec],
        out_shape=[full, full, full, jax.ShapeDtypeStruct((1, W), F32)],
        scratch_shapes=[pltpu.VMEM((HG_HEADS, DK, DK), F32)], args=ins, semantics=("arbitrary",), exchange=exchange)


def _hgrn_post_fwd(o_f, o_b, z, norm_g, name):
    S = z.shape[0]
    tr = min(ROW_TILE, S)

    def body(of_ref, ob_ref, gr_ref, ng_ref, y_ref):
        o = of_ref[...] + ob_ref[...]
        gr = gr_ref[...]
        gate = gr * _sigmoid(gr)
        ng = ng_ref[...]
        for h in range(HG_HEADS):
            sl = slice(h * HG_DIM, (h + 1) * HG_DIM)
            oh = o[:, sl]
            rstd = lax.rsqrt(jnp.mean(oh * oh, axis=-1, keepdims=True) + EPS)
            y_ref[:, sl] = (oh * rstd * ng[:, sl] * gate[:, sl]).astype(y_ref.dtype)

    row = pl.BlockSpec((tr, HG_WIDTH), lambda i: (i, 0))
    return pl.pallas_call(
        body, name=name, grid=(S // tr,),
        in_specs=[row, row, pl.BlockSpec((tr, HG_WIDTH), lambda i: (i, 4)), pl.BlockSpec((1, HG_WIDTH), lambda i: (0, 0))],
        out_specs=row, out_shape=jax.ShapeDtypeStruct((S, HG_WIDTH), BF16), compiler_params=_params("parallel"),
    )(o_f, o_b, z, norm_g)


def _hgrn_post_bwd(dy, o_f, o_b, z, norm_g, name):
    S = z.shape[0]
    tr = min(ROW_TILE, S)

    def body(dy_ref, of_ref, ob_ref, gr_ref, ng_ref, do_ref, dgr_ref, dng_ref):
        @pl.when(pl.program_id(0) == 0)
        def _():
            dng_ref[...] = jnp.zeros_like(dng_ref)

        o = of_ref[...] + ob_ref[...]
        gr, ng, dyv = gr_ref[...], ng_ref[...], dy_ref[...]
        sg = _sigmoid(gr)
        for h in range(HG_HEADS):
            sl = slice(h * HG_DIM, (h + 1) * HG_DIM)
            oh, dyh, grh, sgh, ngh = o[:, sl], dyv[:, sl], gr[:, sl], sg[:, sl], ng[:, sl]
            rstd = lax.rsqrt(jnp.mean(oh * oh, axis=-1, keepdims=True) + EPS)
            on = oh * rstd
            du = dyh * (grh * sgh)
            dgr_ref[:, sl] = dyh * (on * ngh) * (sgh * (1.0 + grh * (1.0 - sgh)))
            dng_ref[:, sl] += jnp.sum(du * on, axis=0, keepdims=True)
            don = du * ngh
            do_ref[:, sl] = rstd * (don - on * jnp.mean(don * on, axis=-1, keepdims=True))

    row = pl.BlockSpec((tr, HG_WIDTH), lambda i: (i, 0))
    vec = pl.BlockSpec((1, HG_WIDTH), lambda i: (0, 0))
    full = jax.ShapeDtypeStruct((S, HG_WIDTH), F32)
    return pl.pallas_call(
        body, name=name, grid=(S // tr,),
        in_specs=[row, row, row, pl.BlockSpec((tr, HG_WIDTH), lambda i: (i, 4)), vec],
        out_specs=[row, row, vec], out_shape=[full, full, jax.ShapeDtypeStruct((1, HG_WIDTH), F32)],
        compiler_params=_params("arbitrary"),
    )(dy, o_f, o_b, z, norm_g)


def _t5_bucket_table():
    rel = (np.arange(3 * BLOCK)[None, :] - BLOCK) - np.arange(BLOCK)[:, None]
    nb = NUM_BUCKETS // 2
    max_exact = nb // 2
    ret = (rel > 0).astype(np.int32) * nb
    n = np.abs(rel)
    ratio = np.log(np.maximum(n, 1).astype(np.float32) / np.float32(max_exact)) / np.float32(math.log(MAX_DISTANCE / max_exact))
    large = max_exact + (ratio.astype(np.float32) * np.float32(nb - max_exact)).astype(np.int32)
    large = np.minimum(large, nb - 1)
    bucket = ret + np.where(n < max_exact, n, large)
    return bucket.astype(np.int32), (n <= WINDOW)


def _bias_table(rel_bias, name):
    bucket, in_band = _t5_bucket_table()
    idx = jnp.asarray(np.where(in_band, bucket, -1))

    def body(rb_ref, idx_ref, o_ref):
        h = pl.program_id(0)
        iv = idx_ref[...]
        acc = jnp.where(iv < 0, NEG, 0.0).astype(F32)
        for b in range(NUM_BUCKETS):
            acc = acc + jnp.where(iv == b, rb_ref[b, h], 0.0)
        o_ref[...] = acc

    return pl.pallas_call(
        body, name=name, grid=(ATT_Q_HEADS,),
        in_specs=[pl.BlockSpec(memory_space=pltpu.SMEM), pl.BlockSpec((BLOCK, 3 * BLOCK), lambda h: (0, 0))],
        out_specs=pl.BlockSpec((None, BLOCK, 3 * BLOCK), lambda h: (h, 0, 0)),
        out_shape=jax.ShapeDtypeStruct((ATT_Q_HEADS, BLOCK, 3 * BLOCK), F32), compiler_params=_params("parallel"),
    )(rel_bias, idx)


def _bias_grad(ds_sum, name):
    bucket, in_band = _t5_bucket_table()
    idx = jnp.asarray(np.where(in_band, bucket, -1))

    def body(ds_ref, idx_ref, o_ref):
        iv, ds = idx_ref[...], ds_ref[...]
        for b in range(NUM_BUCKETS):
            part = jnp.sum(jnp.where(iv == b, ds, 0.0), axis=0, keepdims=True)
            o_ref[b:b + 1, :] = part[:, 0:BLOCK] + part[:, BLOCK:2 * BLOCK] + part[:, 2 * BLOCK:3 * BLOCK]

    return pl.pallas_call(
        body, name=name, grid=(ATT_Q_HEADS,),
        in_specs=[pl.BlockSpec((None, BLOCK, 3 * BLOCK), lambda h: (h, 0, 0)), pl.BlockSpec((BLOCK, 3 * BLOCK), lambda h: (0, 0))],
        out_specs=pl.BlockSpec((None, NUM_BUCKETS, BLOCK), lambda h: (h, 0, 0)),
        out_shape=jax.ShapeDtypeStruct((ATT_Q_HEADS, NUM_BUCKETS, BLOCK), F32), compiler_params=_params("parallel"),
    )(ds_sum, idx)


def _attn_specs(nb):
    G, dh = ATT_GROUP, ATT_HEAD_DIM
    qspec = pl.BlockSpec((G, BLOCK, dh), lambda j, n: (j, n, 0))

    def kv(shift):
        return pl.BlockSpec((None, BLOCK, dh), lambda j, n: (j, jnp.clip(n + shift, 0, nb - 1), 0))

    gain = pl.BlockSpec((1, dh), lambda j, n: (0, 0))
    sink = pl.BlockSpec((G, 1, BLOCK), lambda j, n: (j, 0, 0))
    bias = pl.BlockSpec((G, BLOCK, 3 * BLOCK), lambda j, n: (j, 0, 0))
    return qspec, kv, gain, sink, bias


def _attn_probs(qh, kn, bias_h, sink_h, edge_ok):
    s = _nt(qh.astype(BF16), kn.astype(BF16)) * (1.0 / math.sqrt(ATT_HEAD_DIM)) + bias_h
    s = jnp.where(edge_ok, s, NEG)
    m = jnp.maximum(jnp.max(s, axis=-1, keepdims=True), sink_h)
    p = jnp.exp(s - m)
    e_sink = jnp.exp(sink_h - m)
    inv = 1.0 / (jnp.sum(p, axis=-1, keepdims=True) + e_sink)
    return p * inv, e_sink * inv


def _rms_rows(x):
    rstd = lax.rsqrt(jnp.mean(x * x, axis=-1, keepdims=True) + EPS)
    return x * rstd, rstd


def _edge_ok(n, nb):
    colid = lax.broadcasted_iota(jnp.int32, (ATT_GROUP * BLOCK, 3 * BLOCK), 1)
    return jnp.logical_and(jnp.logical_or(colid >= BLOCK, n > 0), jnp.logical_or(colid < 2 * BLOCK, n < nb - 1))


def _sink_column(sink_ref):
    return jnp.concatenate([jnp.broadcast_to(sink_ref[g][:, 0:1], (BLOCK, 1)) for g in range(ATT_GROUP)], axis=0)


def _attn_fwd(q, k, v, q_g, k_g, sink, bias, name):
    S = q.shape[1]
    nb = S // BLOCK
    G, dh = ATT_GROUP, ATT_HEAD_DIM
    qspec, kv, gain, sink_spec, bias_spec = _attn_specs(nb)

    def body(q_ref, k0, k1, k2, v0, v1, v2, qg_ref, kg_ref, sink_ref, bias_ref, o_ref):
        n = pl.program_id(1)
        kcat = jnp.concatenate([k0[...], k1[...], k2[...]], axis=0)
        vcat = jnp.concatenate([v0[...], v1[...], v2[...]], axis=0).astype(BF16)
        kn = _rms_rows(kcat)[0] * kg_ref[...]
        qn = _rms_rows(q_ref[...].reshape(G * BLOCK, dh))[0] * qg_ref[...]
        p, _ = _attn_probs(qn, kn, bias_ref[...].reshape(G * BLOCK, 3 * BLOCK), _sink_column(sink_ref), _edge_ok(n, nb))
        o_ref[...] = _nn(p.astype(BF16), vcat).reshape(G, BLOCK, dh)

    return pl.pallas_call(
        body, name=name, grid=(ATT_KV_HEADS, nb),
        in_specs=[qspec, kv(-1), kv(0), kv(1), kv(-1), kv(0), kv(1), gain, gain, sink_spec, bias_spec],
        out_specs=qspec, out_shape=jax.ShapeDtypeStruct(q.shape, F32), compiler_params=_params("parallel", "parallel"),
    )(q, k, k, k, v, v, v, q_g, k_g, sink, bias)


def _attn_bwd(q, k, v, q_g, k_g, sink, bias, do, name):
    S = q.shape[1]
    nb = S // BLOCK
    G, dh = ATT_GROUP, ATT_HEAD_DIM
    scale = 1.0 / math.sqrt(dh)
    qspec, kv, gain, sink_spec, bias_spec = _attn_specs(nb)

    def body(q_ref, k0, k1, k2, v0, v1, v2, qg_ref, kg_ref, sink_ref, bias_ref, do_ref,
             dq_ref, dkw_ref, dvw_ref, ds_ref, dsink_ref, dqg_ref):
        n = pl.program_id(1)

        @pl.when(n == 0)
        def _():
            ds_ref[...] = jnp.zeros_like(ds_ref)
            dsink_ref[...] = jnp.zeros_like(dsink_ref)
            dqg_ref[...] = jnp.zeros_like(dqg_ref)

        kcat = jnp.concatenate([k0[...], k1[...], k2[...]], axis=0)
        vcat = jnp.concatenate([v0[...], v1[...], v2[...]], axis=0).astype(BF16)
        kn = _rms_rows(kcat)[0] * kg_ref[...]
        qg = qg_ref[...]
        qhat, rstd = _rms_rows(q_ref[...].reshape(G * BLOCK, dh))
        qn = qhat * qg
        p, p_sink = _attn_probs(qn, kn, bias_ref[...].reshape(G * BLOCK, 3 * BLOCK), _sink_column(sink_ref), _edge_ok(n, nb))
        do_b = do_ref[...].reshape(G * BLOCK, dh).astype(BF16)
        dp = _nt(do_b, vcat)
        delta = jnp.sum(p * dp, axis=-1, keepdims=True)
        ds = p * (dp - delta)
        ds_ref[...] += ds.reshape(G, BLOCK, 3 * BLOCK)
        sink_term = p_sink * delta
        for g in range(G):
            dsink_ref[g] += jnp.zeros((1, BLOCK), F32) - jnp.sum(sink_term[g * BLOCK:(g + 1) * BLOCK], axis=0, keepdims=True)
        ds_b = ds.astype(BF16)
        dvw_ref[...] = _tn(p.astype(BF16), do_b)
        dkw_ref[...] = _tn(ds_b, qn.astype(BF16)) * scale
        dqn = _nn(ds_b, kn.astype(BF16)) * scale
        dqg_ref[...] += jnp.sum(dqn * qhat, axis=0, keepdims=True)
        dqh = dqn * qg
        dq_ref[...] = (rstd * (dqh - qhat * jnp.mean(dqh * qhat, axis=-1, keepdims=True))).reshape(G, BLOCK, dh)

    win = pl.BlockSpec((None, None, 3 * BLOCK, dh), lambda j, n: (j, n, 0, 0))
    wshape = jax.ShapeDtypeStruct((ATT_KV_HEADS, nb, 3 * BLOCK, dh), F32)
    return pl.pallas_call(
        body, name=name, grid=(ATT_KV_HEADS, nb),
        in_specs=[qspec, kv(-1), kv(0), kv(1), kv(-1), kv(0), kv(1), gain, gain, sink_spec, bias_spec, qspec],
        out_specs=[qspec, win, win, bias_spec, sink_spec, pl.BlockSpec((None, 1, dh), lambda j, n: (j, 0, 0))],
        out_shape=[jax.ShapeDtypeStruct(q.shape, F32), wshape, wshape,
                   jax.ShapeDtypeStruct((ATT_Q_HEADS, BLOCK, 3 * BLOCK), F32),
                   jax.ShapeDtypeStruct((ATT_Q_HEADS, 1, BLOCK), F32),
                   jax.ShapeDtypeStruct((ATT_KV_HEADS, 1, dh), F32)],
        compiler_params=_params("parallel", "arbitrary"),
    )(q, k, k, k, v, v, v, q_g, k_g, sink, bias, do)


def _attn_kv_reduce(dkw, dvw, k, k_g, name):
    S = k.shape[1]
    nb = S // BLOCK
    dh = ATT_HEAD_DIM
    kb = min(8, nb)
    steps = nb // kb

    def body(a_lo, a, a_hi, b_lo, b, b_hi, k_ref, kg_ref, dk_ref, dv_ref, dkg_ref):
        n = pl.program_id(1)

        @pl.when(n == 0)
        def _():
            dkg_ref[...] = jnp.zeros_like(dkg_ref)

        lo = jnp.where(n > 0, 1.0, 0.0)
        hi = jnp.where(n < steps - 1, 1.0, 0.0)

        def overlap_add(w, w_lo, w_hi, i):
            before = lo * w_lo[...] if i == 0 else w[i - 1, 2 * BLOCK:3 * BLOCK, :]
            after = hi * w_hi[...] if i == kb - 1 else w[i + 1, 0:BLOCK, :]
            return w[i, BLOCK:2 * BLOCK, :] + before + after

        dkg = jnp.zeros((1, dh), F32)
        for i in range(kb):
            rows = slice(i * BLOCK, (i + 1) * BLOCK)
            dkn = overlap_add(a, a_lo, a_hi, i)
            dv_ref[rows, :] = overlap_add(b, b_lo, b_hi, i)
            khat, rstd = _rms_rows(k_ref[rows, :])
            dkg = dkg + jnp.sum(dkn * khat, axis=0, keepdims=True)
            dkh = dkn * kg_ref[...]
            dk_ref[rows, :] = rstd * (dkh - khat * jnp.mean(dkh * khat, axis=-1, keepdims=True))
        dkg_ref[...] += dkg

    main = pl.BlockSpec((None, kb, 3 * BLOCK, dh), lambda j, n: (j, n, 0, 0))
    halo_lo = pl.BlockSpec((None, None, BLOCK, dh), lambda j, n: (j, jnp.maximum(n * kb - 1, 0), 2, 0))
    halo_hi = pl.BlockSpec((None, None, BLOCK, dh), lambda j, n: (j, jnp.minimum(n * kb + kb, nb - 1), 0, 0))
    blk = pl.BlockSpec((None, kb * BLOCK, dh), lambda j, n: (j, n, 0))
    return pl.pallas_call(
        body, name=name, grid=(ATT_KV_HEADS, steps),
        in_specs=[halo_lo, main, halo_hi, halo_lo, main, halo_hi, blk, pl.BlockSpec((1, dh), lambda j, n: (0, 0))],
        out_specs=[blk, blk, pl.BlockSpec((None, 1, dh), lambda j, n: (j, 0, 0))],
        out_shape=[jax.ShapeDtypeStruct(k.shape, F32), jax.ShapeDtypeStruct(k.shape, F32),
                   jax.ShapeDtypeStruct((ATT_KV_HEADS, 1, dh), F32)],
        compiler_params=_params("parallel", "arbitrary"),
    )(dkw, dkw, dkw, dvw, dvw, dvw, k, k_g)


def _ada_fwd(c_act, w, b, name):
    n = w.shape[1]

    def body(c_ref, w_ref, b_ref, o_ref):
        o_ref[...] = _nn(c_ref[...], w_ref[...], precision=lax.Precision.HIGHEST) + b_ref[...]

    tn = n // 3
    return pl.pallas_call(
        body, name=name, grid=(3,),
        in_specs=[pl.BlockSpec(c_act.shape, lambda j: (0, 0)), pl.BlockSpec((w.shape[0], tn), lambda j: (0, j)),
                  pl.BlockSpec((1, tn), lambda j: (0, j))],
        out_specs=pl.BlockSpec((c_act.shape[0], tn), lambda j: (0, j)),
        out_shape=jax.ShapeDtypeStruct((c_act.shape[0], n), F32), compiler_params=_params("parallel"),
    )(c_act, w, b)


def _ada_wgrad(c_act_t, dm, name):
    D, nbatch = c_act_t.shape
    n = dm.shape[1]
    tr = 256

    def body(c_ref, dm_ref, o_ref):
        cv, dv = c_ref[...], dm_ref[...]
        acc = cv[:, 0:1] * dv[0:1, :]
        for b in range(1, nbatch):
            acc = acc + cv[:, b:b + 1] * dv[b:b + 1, :]
        o_ref[...] = acc

    return pl.pallas_call(
        body, name=name, grid=(D // tr,),
        in_specs=[pl.BlockSpec((tr, nbatch), lambda i: (i, 0)), pl.BlockSpec((nbatch, n), lambda i: (0, 0))],
        out_specs=pl.BlockSpec((tr, n), lambda i: (i, 0)), out_shape=jax.ShapeDtypeStruct((D, n), F32),
        compiler_params=_params("parallel"),
    )(c_act_t, dm)


def _adamw(w, g, m, v, name):
    R, Cn = w.shape
    tr = R
    for cand in (256, 128, 64, 32, 16, 8):
        if R % cand == 0:
            tr = cand
            break

    def body(w_ref, g_ref, m_ref, v_ref, d_ref, nm_ref, nv_ref):
        gv = g_ref[...]
        m_new = ADAM_B1 * m_ref[...] + (1.0 - ADAM_B1) * gv
        v_new = ADAM_B2 * v_ref[...] + (1.0 - ADAM_B2) * (gv * gv)
        m_hat = m_new / (1.0 - ADAM_B1 ** ADAM_STEP)
        v_hat = v_new / (1.0 - ADAM_B2 ** ADAM_STEP)
        d_ref[...] = -ADAM_LR * (m_hat / (jnp.sqrt(v_hat) + ADAM_EPS) + ADAM_WD * w_ref[...])
        nm_ref[...] = m_new
        nv_ref[...] = v_new

    blk = pl.BlockSpec((tr, Cn), lambda i: (i, 0))
    shp = jax.ShapeDtypeStruct((R, Cn), F32)
    return pl.pallas_call(
        body, name=name, grid=(R // tr,), in_specs=[blk] * 4, out_specs=[blk] * 3, out_shape=[shp] * 3,
        compiler_params=_params("parallel"),
    )(w, g, m, v)


def _place():
    return lax.axis_index("x"), lax.axis_index("y"), lax.axis_index("c")


def _flip(place, k):
    x, y, c = place
    return (1 - x if k & 4 else x, 1 - y if k & 2 else y, 1 - c if k & 1 else c)


def _dev_index(place):
    x, y, c = place
    return 4 * x + 2 * y + c


def _chip_index(place):
    return 2 * place[0] + place[1]


def _allgather8(x, name, reduce=False):
    R, Cn = x.shape

    def body(x_ref, *rest):
        if reduce:
            out_ref, sum_ref, send_sems, recv_sems, local_sem = rest
        else:
            out_ref, send_sems, recv_sems, local_sem = rest
        me = _place()
        mine = pltpu.make_async_copy(x_ref, out_ref.at[_dev_index(me)], local_sem)
        mine.start()

        def copy(k, origin, to):
            return pltpu.make_async_remote_copy(
                src_ref=x_ref, dst_ref=out_ref.at[_dev_index(origin)], send_sem=send_sems.at[k - 1],
                recv_sem=recv_sems.at[k - 1], device_id=to, device_id_type=MESH)

        sends = [copy(k, me, _flip(me, k)) for k in range(1, 8)]
        for cp in sends:
            cp.start()
        for k in range(1, 8):
            copy(k, _flip(me, k), me).wait_recv()
        for cp in sends:
            cp.wait_send()
        mine.wait()
        if reduce:
            acc = out_ref[0]
            for i in range(1, 8):
                acc = acc + out_ref[i]
            sum_ref[...] = acc

    vm = pl.BlockSpec(memory_space=pltpu.VMEM)
    outs = [jax.ShapeDtypeStruct((8, R, Cn), F32)] + ([jax.ShapeDtypeStruct((R, Cn), F32)] if reduce else [])
    res = pl.pallas_call(
        body, name=name, in_specs=[vm], out_specs=[vm] * len(outs), out_shape=outs,
        scratch_shapes=[pltpu.SemaphoreType.DMA((7,)), pltpu.SemaphoreType.DMA((7,)), pltpu.SemaphoreType.DMA],
    )(x)
    return res if reduce else res[0]


def _weights_allgather(shards, name):
    n = len(shards)
    per = 8

    def body(*refs):
        in_refs, out_refs = refs[:n], refs[n:2 * n]
        send_sems, recv_sems = refs[2 * n:]
        me = _place()
        c = me[2]
        sibling = _flip(me, 1)
        others = [_flip(me, 2 * j) for j in (1, 2, 3)]

        def copy(a, k, src, dst, to):
            return pltpu.make_async_remote_copy(
                src_ref=src, dst_ref=dst, send_sem=send_sems.at[per * a + k], recv_sem=recv_sems.at[per * a + k],
                device_id=to, device_id_type=MESH)

        def block(a, place, half):
            return out_refs[a].at[_chip_index(place), half]

        started = []
        for a in range(n):
            sends = [copy(a, 0, in_refs[a].at[c], block(a, me, c), sibling),
                     copy(a, 7, in_refs[a].at[1 - c], block(a, me, 1 - c), sibling)]
            sends += [copy(a, 1 + j, in_refs[a].at[c], block(a, me, c), to) for j, to in enumerate(others)]
            for cp in sends:
                cp.start()
            started += sends
        for a in range(n):
            for j, other in enumerate(others):
                landed = block(a, other, c)
                copy(a, 1 + j, landed, landed, me).wait_recv()
                fwd = copy(a, 4 + j, landed, landed, sibling)
                fwd.start()
                started.append(fwd)
        for a in range(n):
            copy(a, 0, block(a, me, 1 - c), block(a, me, 1 - c), me).wait_recv()
            copy(a, 7, block(a, me, c), block(a, me, c), me).wait_recv()
            for j, other in enumerate(others):
                got = block(a, other, 1 - c)
                copy(a, 4 + j, got, got, me).wait_recv()
        for cp in started:
            cp.wait_send()

    return pl.pallas_call(
        body, name=name, in_specs=[ANY] * n, out_specs=[ANY] * n,
        out_shape=[jax.ShapeDtypeStruct((N_CHIPS,) + s.shape, s.dtype) for s in shards],
        scratch_shapes=[pltpu.SemaphoreType.DMA((per * n,)), pltpu.SemaphoreType.DMA((per * n,))],
    )(*shards)


def _remote(src, dst, send_sems, recv_sems, i, to):
    return pltpu.make_async_remote_copy(
        src_ref=src, dst_ref=dst, send_sem=send_sems.at[i], recv_sem=recv_sems.at[i], device_id=to, device_id_type=MESH)


def _symmetric_plan(copies):
    def plan(in_refs, out_refs, send_sems, recv_sems):
        sends = [_remote(src, dst, send_sems, recv_sems, i, to) for i, (src, dst, to) in enumerate(copies(in_refs, out_refs))]
        return sends, sends
    return plan


def _halves_exchange(grads):
    def copies(in_refs, out_refs):
        me = _place()
        return [(g.at[kk, 1 - me[2]], got.at[kk], _flip(me, 1)) for g, got in zip(in_refs, out_refs) for kk in range(N_CHIPS)]

    return _Exchange(grads, [jax.ShapeDtypeStruct((N_CHIPS,) + g.shape[2:], g.dtype) for g in grads],
                     N_CHIPS * len(grads), _symmetric_plan(copies))


def _chips_exchange(parts):
    def copies(in_refs, out_refs):
        me = _place()
        return [(p.at[_chip_index(_flip(me, 2 * j))], got.at[j - 1], _flip(me, 2 * j))
                for p, got in zip(in_refs, out_refs) for j in (1, 2, 3)]

    return _Exchange(parts, [jax.ShapeDtypeStruct((3,) + p.shape[1:], p.dtype) for p in parts], 3 * len(parts),
                     _symmetric_plan(copies))


def _siblings_exchange(halves):
    def copies(in_refs, out_refs):
        sibling = _flip(_place(), 1)
        return [(h, got, sibling) for h, got in zip(in_refs, out_refs)]

    return _Exchange(halves, [jax.ShapeDtypeStruct(h.shape, h.dtype) for h in halves], len(halves), _symmetric_plan(copies))


def _gather_over_ici(shards):
    def copies(in_refs, out_refs):
        me = _place()
        c = me[2]
        return [(w.at[c], out.at[_chip_index(me), c], _flip(me, 2 * j)) for w, out in zip(in_refs, out_refs) for j in (1, 2, 3)]

    def plan(in_refs, out_refs, send_sems, recv_sems):
        me = _place()
        sends = [_remote(src, dst, send_sems, recv_sems, i, to) for i, (src, dst, to) in enumerate(copies(in_refs, out_refs))]
        lands = [out.at[_chip_index(_flip(me, 2 * j)), me[2]] for out in out_refs for j in (1, 2, 3)]
        return sends, [_remote(z, z, send_sems, recv_sems, i, me) for i, z in enumerate(lands)]

    return _Exchange(shards, [jax.ShapeDtypeStruct((N_CHIPS,) + s.shape, s.dtype) for s in shards], 3 * len(shards), plan)


def _gather_over_d2d(shards, gathered):
    n = len(shards)

    def plan(in_refs, out_refs, send_sems, recv_sems):
        me = _place()
        c = me[2]
        sibling = _flip(me, 1)
        mine = _chip_index(me)
        sends, recvs = [], []
        for a, (w, out) in enumerate(zip(in_refs[:n], out_refs)):
            moves = [(w.at[c], (mine, c)), (w.at[1 - c], (mine, 1 - c))]
            moves += [(out.at[_chip_index(_flip(me, 2 * j)), c], (_chip_index(_flip(me, 2 * j)), c)) for j in (1, 2, 3)]
            for k, (src, (chip, half)) in enumerate(moves):
                sends.append(_remote(src, out.at[chip, half], send_sems, recv_sems, 5 * a + k, sibling))
            lands = [(mine, 1 - c), (mine, c)] + [(_chip_index(_flip(me, 2 * j)), 1 - c) for j in (1, 2, 3)]
            for k, (chip, half) in enumerate(lands):
                z = out.at[chip, half]
                recvs.append(_remote(z, z, send_sems, recv_sems, 5 * a + k, me))
        return sends, recvs

    return _Exchange(list(shards) + list(gathered), [jax.ShapeDtypeStruct(g.shape, g.dtype) for g in gathered], 5 * n, plan,
                     aliases={n + a: a for a in range(n)})


def _row_tile(rows):
    for cand in (256, 176, 128, 64, 32, 16, 8):
        if rows % cand == 0:
            return cand
    return rows


def _pair_sum(core, grad, theirs, name):
    N, _, R, Cn = grad.shape
    tr = _row_tile(R)

    def body(core_ref, g_ref, t_ref, o_ref, ob_ref):
        s = g_ref[...] + t_ref[...]
        o_ref[...] = s
        ob_ref[...] = s.astype(BF16)

    out = pl.BlockSpec((None, tr, Cn), lambda k, i, core_ref: (k, i, 0))
    return pl.pallas_call(
        body, name=name,
        grid_spec=pltpu.PrefetchScalarGridSpec(
            num_scalar_prefetch=1, grid=(N, R // tr),
            in_specs=[pl.BlockSpec((None, None, tr, Cn), lambda k, i, core_ref: (k, core_ref[0], i, 0)),
                      pl.BlockSpec((None, tr, Cn), lambda k, i, core_ref: (k, i, 0))],
            out_specs=[out, out]),
        out_shape=[jax.ShapeDtypeStruct((N, R, Cn), F32), jax.ShapeDtypeStruct((N, R, Cn), BF16)],
        compiler_params=_params("parallel", "parallel"),
    )(core, grad, theirs)


def _chip_sum(chip, parts, landed, name):
    _, R, Cn = parts.shape
    tr = _row_tile(R)

    def body(chip_ref, p_ref, l_ref, o_ref):
        o_ref[...] = ((p_ref[...] + l_ref[0].astype(F32)) + l_ref[1].astype(F32)) + l_ref[2].astype(F32)

    return pl.pallas_call(
        body, name=name,
        grid_spec=pltpu.PrefetchScalarGridSpec(
            num_scalar_prefetch=1, grid=(R // tr,),
            in_specs=[pl.BlockSpec((None, tr, Cn), lambda i, chip_ref: (chip_ref[0], i, 0)),
                      pl.BlockSpec((3, tr, Cn), lambda i, chip_ref: (0, i, 0))],
            out_specs=pl.BlockSpec((tr, Cn), lambda i, chip_ref: (i, 0))),
        out_shape=jax.ShapeDtypeStruct((R, Cn), F32), compiler_params=_params("parallel"),
    )(chip, parts, landed)


def _pair_sums(core, grads, theirs, tag):
    return [_pair_sum(core, g, t, f"{tag}_pair_sum_{i}") for i, (g, t) in enumerate(zip(grads, theirs))]


def _chip_sums(chip, parts, landed, tag):
    return [_chip_sum(chip, p[0], l, f"{tag}_chip_sum_{i}") for i, (p, l) in enumerate(zip(parts, landed))]


def _by_chip_rows(g):
    return g.reshape(N_CHIPS, 2, g.shape[0] // (2 * N_CHIPS), g.shape[1])


def _by_chip_cols(g):
    return g.reshape(N_CHIPS, 2, g.shape[1] // 2, g.shape[2])


def _adamw_halves(core, w, g_mine, g_theirs, m, v, name):
    R2, Cn = w.shape
    r = R2 // 2
    tr = _row_tile(r)
    nt = r // tr

    def body(core_ref, w_ref, gm_ref, gt_ref, m_ref, v_ref, g_ref, d_ref, nm_ref, nv_ref):
        gv = jnp.where(pl.program_id(0) == core_ref[0], gm_ref[...], gt_ref[...])
        g_ref[...] = gv
        m_new = ADAM_B1 * m_ref[...] + (1.0 - ADAM_B1) * gv
        v_new = ADAM_B2 * v_ref[...] + (1.0 - ADAM_B2) * (gv * gv)
        m_hat = m_new / (1.0 - ADAM_B1 ** ADAM_STEP)
        v_hat = v_new / (1.0 - ADAM_B2 ** ADAM_STEP)
        d_ref[...] = -ADAM_LR * (m_hat / (jnp.sqrt(v_hat) + ADAM_EPS) + ADAM_WD * w_ref[...])
        nm_ref[...] = m_new
        nv_ref[...] = v_new

    full = pl.BlockSpec((tr, Cn), lambda hf, i, core_ref: (hf * nt + i, 0))
    half = pl.BlockSpec((tr, Cn), lambda hf, i, core_ref: (i, 0))
    shp = jax.ShapeDtypeStruct((R2, Cn), F32)
    return pl.pallas_call(
        body, name=name,
        grid_spec=pltpu.PrefetchScalarGridSpec(
            num_scalar_prefetch=1, grid=(2, nt), in_specs=[full, half, half, full, full], out_specs=[full] * 4),
        out_shape=[shp] * 4, compiler_params=_params("parallel", "parallel"),
    )(core, w, g_mine, g_theirs, m, v)


def _pad_row(v, width):
    v = v.reshape(1, -1)
    return jnp.pad(v, ((0, 0), (0, width - v.shape[1])))


def _ffn1_forward(x, ng, shift, scale, gate, w_in4, w_out, gather, next_norm):
    h = _rmsmod_fwd(x, ng, shift, scale, "ffn1_norm")
    (zg, zu, a), partly = _ffn_in_fwd(h, w_in4, "ffn1_in", exchange=_gather_over_ici(gather))
    (x_new, f, h_next), gathered = _proj_out_fwd([a], w_out, x, gate, 0.5, "ffn1_out", next_norm=next_norm,
                                                 exchange=_gather_over_d2d(gather, partly))
    return x_new, (h, zg, zu, a, f), gathered, h_next


def _ffn_backward(df, saved, w_in4, w_out, core, chip, tag, riding=None, norm=None):
    h, zg, zu, a = saved[:4]
    rode = None
    if riding:
        (dzg, dzu), rode = _dact_bwd(df, w_out, zg, zu, f"{tag}_dact", exchange=riding)
    else:
        dzg, dzu = _dact_bwd(df, w_out, zg, zu, f"{tag}_dact")
    g_out = [_by_chip_rows(_wgrad(a, [df], df.shape[1], f"{tag}_dw_out")[0].reshape(a.shape[1], df.shape[1]))]
    (dw_in,), theirs_out = _wgrad(h, [dzg, dzu], FF_SHARD, f"{tag}_dw_in", exchange=_halves_exchange(g_out))
    g_in = [_by_chip_cols(dw_in.reshape(N_CHIPS, h.shape[1], FF_SHARD))]
    parts_out = _pair_sums(core, g_out, theirs_out, f"{tag}_out")
    dh_outs, (theirs_in, landed_out) = _ffn_in_dgrad(
        dzg, dzu, w_in4, f"{tag}_dh", norm=norm, exchange=[_halves_exchange(g_in), _chips_exchange([parts_out[0][1]])])
    parts_in = _pair_sums(core, g_in, theirs_in, f"{tag}_in")
    return dh_outs, parts_in, _chip_sums(chip, parts_out, landed_out, f"{tag}_out"), rode


def kernel(x, c, w_ada, b_ada, norm_g, w_ffn1_in, w_ffn1_out, w_ffn2_in, w_ffn2_out, w_mix_in, w_mix_out, hgrn_lb, hgrn_norm_g, qk_norm_g, attn_sink, rel_bias, loss_target, m_w_ada, m_b_ada, m_norm_g, m_w_ffn1_in, m_w_ffn1_out, m_w_ffn2_in, m_w_ffn2_out, m_w_mix_in, m_w_mix_out, m_hgrn_lb, m_hgrn_norm_g, m_qk_norm_g, m_attn_sink, m_rel_bias, v_w_ada, v_b_ada, v_norm_g, v_w_ffn1_in, v_w_ffn1_out, v_w_ffn2_in, v_w_ffn2_out, v_w_mix_in, v_w_mix_out, v_hgrn_lb, v_hgrn_norm_g, v_qk_norm_g, v_attn_sink, v_rel_bias):
    D = D_MODEL
    S = x.shape[1]
    place = (lax.axis_index("x"), lax.axis_index("y"), lax.axis_index("c"))
    me, my_chip = _dev_index(place), _chip_index(place)
    x0 = x[0]
    target = loss_target[0]

    def halves(w):
        return w.astype(BF16).reshape(2, w.shape[0] // 2, w.shape[1])

    gathered = _weights_allgather([halves(w_ffn1_in[0]), halves(w_ffn1_out[0])], "weights_allgather")
    w1_in = gathered[0].reshape(N_CHIPS, D, FF_SHARD)
    w1_out = gathered[1].reshape(D_FF, D)
    later = [halves(w_mix_in[0]), halves(w_mix_out[0]), halves(w_ffn2_in[0]), halves(w_ffn2_out[0])]
    core_arr = jnp.reshape(place[2], (1,)).astype(jnp.int32)
    chip_arr = jnp.reshape(my_chip, (1,)).astype(jnp.int32)

    small = jnp.concatenate([_pad_row(c, D), _pad_row(norm_g, D), _pad_row(hgrn_lb, D), jnp.zeros((5, D), F32)], axis=0)
    small_all = _allgather8(small, "small_allgather")
    c_all = small_all[:, 0, :]
    by_chip = small_all[0::2]
    norm_g_full = by_chip[:, 1, :3 * 256].reshape(N_CHIPS, 3, 256).transpose(1, 0, 2).reshape(3, D)
    lb_raw = by_chip[:, 2, :2 * 2 * 128].reshape(N_CHIPS, 2, 2, 128).transpose(1, 2, 0, 3).reshape(2, 2, HG_WIDTH)
    lb = jax.nn.sigmoid(lb_raw[:, 0, :] - lb_raw[:, 1, :])
    lb_f, lb_b = lb[0:1], lb[1:2]

    c_act_all = c_all * jax.nn.sigmoid(c_all)
    n_ada = w_ada.shape[2]
    b_mine = lax.dynamic_slice_in_dim(b_ada, my_chip * n_ada, n_ada, axis=1)
    mods_part = _ada_fwd(c_act_all, w_ada[0], b_mine, "ada_fwd")
    mods_all = _allgather8(mods_part, "mods_allgather")[0::2].transpose(1, 0, 2).reshape(8, N_MOD * D)
    mods = lax.dynamic_slice_in_dim(mods_all, me, 1, axis=0)
    sh1, sc1, g1, sh2, sc2, g2, sh3, sc3, g3 = [mods[:, i * D:(i + 1) * D] for i in range(N_MOD)]

    x1, saved1, gathered, h2 = _ffn1_forward(x0, norm_g_full[0:1], sh1, sc1, g1, w1_in, w1_out, later,
                                             (norm_g_full[1:2], sh2, sc2))
    wm_in = gathered[0].reshape(N_CHIPS, D, D_IN // N_CHIPS).transpose(1, 0, 2).reshape(D, D_IN)
    wm_out = gathered[1].reshape(D, D)
    w2_in = gathered[2].reshape(N_CHIPS, D, FF_SHARD)
    w2_out = gathered[3].reshape(D_FF, D)

    z = _matmul_nn(h2, wm_in, F32, 256, "mix_in")
    of, st_f = _hgrn_fwd(z, lb_f, 0, "hgrn_fwd_f")
    ob, st_b = _hgrn_fwd(z, lb_b, 1, "hgrn_fwd_b")
    o_h = _hgrn_post_fwd(of, ob, z, hgrn_norm_g, "hgrn_post")

    def to_heads(t, nh):
        return t.reshape(S, nh, ATT_HEAD_DIM).transpose(1, 0, 2)

    aq = to_heads(z[:, 5 * HG_WIDTH:5 * HG_WIDTH + ATT_WIDTH], ATT_Q_HEADS)
    ak = to_heads(z[:, 5 * HG_WIDTH + ATT_WIDTH:5 * HG_WIDTH + ATT_WIDTH + KV_WIDTH], ATT_KV_HEADS)
    av = to_heads(z[:, 5 * HG_WIDTH + ATT_WIDTH + KV_WIDTH:], ATT_KV_HEADS)
    q_g, k_g = qk_norm_g[0, 0:1], qk_norm_g[0, 1:2]
    sink_b = jnp.broadcast_to(attn_sink.reshape(ATT_Q_HEADS, 1, 1), (ATT_Q_HEADS, 1, BLOCK))
    bias = _bias_table(rel_bias, "bias_table")
    o_attn = _attn_fwd(aq, ak, av, q_g, k_g, sink_b, bias, "attn_fwd")
    o_a = o_attn.transpose(1, 0, 2).reshape(S, ATT_WIDTH).astype(BF16)
    x2, mixed, h3 = _proj_out_fwd([o_h, o_a], wm_out, x1, g2, 1.0, "mix_out", next_norm=(norm_g_full[2:3], sh3, sc3))

    zg3, zu3, a3 = _ffn_in_fwd(h3, w2_in, "ffn2_in")
    dx3, df3, dg3, sq_cols = _proj_out_loss(a3, w2_out, x2, g3, 0.5, target, "ffn2_out_loss")
    loss_mine = 0.5 * jnp.sum(sq_cols) / D

    (dx2, dsh3, dsc3, dng3, dmixed, dg2), parts2, mine2_out, _ = _ffn_backward(
        df3, (h3, zg3, zu3, a3), w2_in, w2_out, core_arr, chip_arr, "ffn2",
        norm=_NormBwd(x2, norm_g_full[2:3], sc3, dx3, below=(mixed, g2, 1.0)))

    (do_cat,) = _matmul_nt(dmixed, wm_out, ROW_TILE, "mix_out_dgrad")
    dwm_out = _wgrad_rows([o_h, o_a], dmixed, "mix_out_dw").reshape(D, D)

    do_sum, dgr, d_hnorm = _hgrn_post_bwd(do_cat, of, ob, z, hgrn_norm_g, "hgrn_post_bwd")
    (dq_f, dff, dv_f, doml_f), landed2 = _hgrn_bwd(z, lb_f, do_sum, st_f, 0, "hgrn_bwd_f",
                                                   exchange=_chips_exchange([p[1] for p in parts2]))
    mine2 = _chip_sums(chip_arr, parts2, landed2, "ffn2_in") + mine2_out
    (dhq, dfb, dhi, doml_b), theirs2 = _hgrn_bwd(z, lb_b, do_sum, st_b, 1, "hgrn_bwd_b", acc=(dq_f, dv_f),
                                                 exchange=_siblings_exchange(mine2))

    do_a = to_heads(do_cat[:, HG_WIDTH:], ATT_Q_HEADS)
    daq, dkw, dvw, ds_sum, dsink, dqg = _attn_bwd(aq, ak, av, q_g, k_g, sink_b, bias, do_a, "attn_bwd")
    dak, dav, dkg = _attn_kv_reduce(dkw, dvw, ak, k_g, "attn_kv_reduce")
    d_rel_bias = jnp.sum(_bias_grad(ds_sum, "bias_grad"), axis=-1).T

    def from_heads(t):
        return t.transpose(1, 0, 2).reshape(S, -1)

    dz = jnp.concatenate([dhq, dff, dfb, dhi, dgr, from_heads(daq), from_heads(dak), from_heads(dav)], axis=1).astype(BF16)
    dwm_in = _wgrad(h2, [dz], D_IN // 2, "mix_in_dw")[0][0]
    dwm_in = jnp.concatenate([dwm_in[0], dwm_in[1]], axis=1)
    wide = D_IN // N_CHIPS
    grads_m = [_by_chip_cols(dwm_in.reshape(D, N_CHIPS, wide).transpose(1, 0, 2)), _by_chip_rows(dwm_out)]
    (dx1, dsh2, dsc2, dng2, df1, dg1), theirs_m = _matmul_nt(
        dz, wm_in, 256, "mix_in_dgrad", exchange=_halves_exchange(grads_m),
        norm=_NormBwd(x1, norm_g_full[1:2], sc2, dx2, below=(saved1[4], g1, 0.5)))
    parts_m = _pair_sums(core_arr, grads_m, theirs_m, "mix")

    (dh1,), parts1, mine1_out, landed_m = _ffn_backward(df1, saved1, w1_in, w1_out, core_arr, chip_arr, "ffn1",
                                                        riding=_chips_exchange([p[1] for p in parts_m]))
    mine_m = _chip_sums(chip_arr, parts_m, landed_m, "mix")
    (dx0, dsh1, dsc1, dng1), landed1 = _rmsmod_bwd(dh1, _NormBwd(x0, norm_g_full[0:1], sc1, dx1), "ffn1_norm_bwd",
                                                   exchange=_chips_exchange([p[1] for p in parts1]))
    mine1 = _chip_sums(chip_arr, parts1, landed1, "ffn1_in") + mine1_out
    theirs_1m = list(_run_exchange(_siblings_exchange(mine1 + mine_m), "siblings_exchange"))
    reduced = list(zip(mine1 + mine2 + mine_m, theirs_1m[:2] + list(theirs2) + theirs_1m[2:]))

    dlb = -jnp.concatenate([doml_f, doml_b], axis=0)
    dlb_raw = dlb * lb * (1.0 - lb)
    d_hgrn_lb = jnp.stack([dlb_raw, -dlb_raw], axis=1)
    d_qk = jnp.concatenate([jnp.sum(dqg, axis=0), jnp.sum(dkg, axis=0)], axis=0)
    dmods = jnp.concatenate([dsh1, dsc1, dg1, dsh2, dsc2, dg2, dsh3, dsc3, dg3], axis=0)
    packed = jnp.concatenate(
        [dmods, dng1, dng2, dng3, d_hgrn_lb.reshape(2, D), _pad_row(d_hnorm, D), _pad_row(d_qk, D),
         _pad_row(dsink[:, 0, 0], D), _pad_row(d_rel_bias, D), _pad_row(loss_mine, D)], axis=0)
    packed = jnp.pad(packed, ((0, 24 - packed.shape[0]), (0, 0)))
    packed_all, packed_sum = _allgather8(packed, "small_grads_allgather", reduce=True)
    dmods_all = packed_all[:, 0:N_MOD, :].reshape(8, N_MOD * D)
    g_b_ada = packed_sum[0:N_MOD].reshape(1, N_MOD * D)
    g_norm_full = packed_sum[9:12]
    g_norm_g = lax.dynamic_slice_in_dim(g_norm_full, my_chip * 256, 256, axis=1).reshape(1, 3, 256)
    g_hgrn_lb = lax.dynamic_slice_in_dim(packed_sum[12:14].reshape(2, 2, HG_WIDTH), my_chip * 128, 128, axis=2)
    g_hgrn_norm_g = packed_sum[14:15, :HG_WIDTH]
    g_qk_norm_g = packed_sum[15, :2 * ATT_HEAD_DIM].reshape(1, 2, ATT_HEAD_DIM)
    g_attn_sink = packed_sum[16:17, :ATT_Q_HEADS]
    g_rel_bias = packed_sum[17, :NUM_BUCKETS * ATT_Q_HEADS].reshape(NUM_BUCKETS, ATT_Q_HEADS)
    loss = packed_sum[18, 0]

    dm_mine = lax.dynamic_slice_in_dim(dmods_all, my_chip * n_ada, n_ada, axis=1)
    g_w_ada = _ada_wgrad(c_act_all.T, dm_mine, "ada_wgrad")[None]

    def big(w, g, m, v, name):
        d, nm, nv = _adamw(w[0], g[0], m[0], v[0], name)
        return d[None], nm[None], nv[None]

    def big_halves(w, g_pair, m, v, name):
        g, d, nm, nv = _adamw_halves(core_arr, w[0], g_pair[0], g_pair[1], m[0], v[0], name)
        return g[None], (d[None], nm[None], nv[None])

    g_w1_in, u_w1_in = big_halves(w_ffn1_in, reduced[0], m_w_ffn1_in, v_w_ffn1_in, "adamw_w_ffn1_in")
    g_w1_out, u_w1_out = big_halves(w_ffn1_out, reduced[1], m_w_ffn1_out, v_w_ffn1_out, "adamw_w_ffn1_out")
    g_w2_in, u_w2_in = big_halves(w_ffn2_in, reduced[2], m_w_ffn2_in, v_w_ffn2_in, "adamw_w_ffn2_in")
    g_w2_out, u_w2_out = big_halves(w_ffn2_out, reduced[3], m_w_ffn2_out, v_w_ffn2_out, "adamw_w_ffn2_out")
    g_wm_in, u_wm_in = big_halves(w_mix_in, reduced[4], m_w_mix_in, v_w_mix_in, "adamw_w_mix_in")
    g_wm_out, u_wm_out = big_halves(w_mix_out, reduced[5], m_w_mix_out, v_w_mix_out, "adamw_w_mix_out")

    smalls = [(b_ada, g_b_ada, m_b_ada, v_b_ada), (norm_g, g_norm_g, m_norm_g, v_norm_g), (hgrn_lb, g_hgrn_lb, m_hgrn_lb, v_hgrn_lb),
              (hgrn_norm_g, g_hgrn_norm_g, m_hgrn_norm_g, v_hgrn_norm_g), (qk_norm_g, g_qk_norm_g, m_qk_norm_g, v_qk_norm_g),
              (attn_sink, g_attn_sink, m_attn_sink, v_attn_sink), (rel_bias, g_rel_bias, m_rel_bias, v_rel_bias)]
    sizes = [t[0].size for t in smalls]
    total = sum(sizes)
    rows = -(-total // 128)
    rows = -(-rows // 8) * 8

    def pack(i):
        flat = jnp.concatenate([t[i].reshape(-1) for t in smalls])
        fill = 1.0 if i == 3 else 0.0
        return jnp.pad(flat, (0, rows * 128 - total), constant_values=fill).reshape(rows, 128)

    packed_out = _adamw(pack(0), pack(1), pack(2), pack(3), "adamw_small")

    def unpack(flat2d):
        flat = flat2d.reshape(-1)
        outs, off = [], 0
        for t, n in zip(smalls, sizes):
            outs.append(flat[off:off + n].reshape(t[0].shape))
            off += n
        return outs

    d_small, m_small, v_small = [unpack(t) for t in packed_out]

    upd = {
        "w_ada": big(w_ada, g_w_ada, m_w_ada, v_w_ada, "adamw_w_ada"),
        "w_ffn1_in": u_w1_in, "w_ffn1_out": u_w1_out, "w_ffn2_in": u_w2_in, "w_ffn2_out": u_w2_out,
        "w_mix_in": u_wm_in, "w_mix_out": u_wm_out,
    }
    small_names = ["b_ada", "norm_g", "hgrn_lb", "hgrn_norm_g", "qk_norm_g", "attn_sink", "rel_bias"]
    for i, nme in enumerate(small_names):
        upd[nme] = (d_small[i], m_small[i], v_small[i])
    grads = {
        "w_ada": g_w_ada, "b_ada": g_b_ada, "norm_g": g_norm_g, "w_ffn1_in": g_w1_in, "w_ffn1_out": g_w1_out,
        "w_ffn2_in": g_w2_in, "w_ffn2_out": g_w2_out, "w_mix_in": g_wm_in, "w_mix_out": g_wm_out, "hgrn_lb": g_hgrn_lb,
        "hgrn_norm_g": g_hgrn_norm_g, "qk_norm_g": g_qk_norm_g, "attn_sink": g_attn_sink, "rel_bias": g_rel_bias,
    }
    order = ["w_ada", "b_ada", "norm_g", "w_ffn1_in", "w_ffn1_out", "w_ffn2_in", "w_ffn2_out", "w_mix_in", "w_mix_out",
             "hgrn_lb", "hgrn_norm_g", "qk_norm_g", "attn_sink", "rel_bias"]
    return (loss, dx0[None], *[grads[k] for k in order], *[upd[k][0] for k in order], *[upd[k][1] for k in order],
            *[upd[k][2] for k in order])
```

```python
import functools
import math

import numpy as np
import jax
import jax.numpy as jnp
from jax import lax
from jax.experimental import pallas as pl
from jax.experimental.pallas import tpu as pltpu

F32, BF16 = jnp.float32, jnp.bfloat16

D_MODEL = 1024
D_FF = 2816
HG_HEADS, HG_DIM = 4, 128
HG_WIDTH = HG_HEADS * HG_DIM
ATT_Q_HEADS, ATT_KV_HEADS, ATT_HEAD_DIM = 8, 2, 64
ATT_GROUP = ATT_Q_HEADS // ATT_KV_HEADS
ATT_WIDTH = ATT_Q_HEADS * ATT_HEAD_DIM
KV_WIDTH = ATT_KV_HEADS * ATT_HEAD_DIM
WINDOW, BLOCK = 128, 128
NUM_BUCKETS, MAX_DISTANCE = 32, 128
N_MOD = 9
EPS = 1e-6
D_IN = 5 * HG_WIDTH + ATT_WIDTH + 2 * KV_WIDTH
ADAM_LR, ADAM_B1, ADAM_B2, ADAM_EPS, ADAM_WD, ADAM_STEP = 0.001, 0.9, 0.999, 1e-08, 0.01, 10

N_CHIPS = 4
FF_SHARD = 2 * D_FF // N_CHIPS
NEG = -1e30

VMEM_LIMIT_BYTES = 56 << 20
ROW_TILE = 512
HG_CHUNK = 16
HG_ROWS = 256

MESH = pl.DeviceIdType.MESH
ANY = pl.BlockSpec(memory_space=pl.ANY)


def _params(*sem):
    return pltpu.CompilerParams(dimension_semantics=sem, vmem_limit_bytes=VMEM_LIMIT_BYTES)


def _resident(shape, index_map):
    return pl.BlockSpec(shape, index_map, pipeline_mode=pl.Buffered(1))


def _dot(a, b, dims, precision=None):
    return lax.dot_general(a, b, (dims, ((), ())), precision=precision, preferred_element_type=F32)


def _nn(a, b, precision=None):
    return _dot(a, b, ((1,), (0,)), precision)


def _nt(a, b):
    return _dot(a, b, ((1,), (1,)))


def _tn(a, b):
    return _dot(a, b, ((0,), (0,)))


def _sigmoid(x):
    return jax.nn.sigmoid(x)


class _Exchange:
    def __init__(self, inputs, out_shapes, n_sems, plan, aliases=None):
        self.inputs, self.out_shapes, self.n_sems, self.plan, self.aliases = list(inputs), list(out_shapes), n_sems, plan, aliases or {}

    def sem_shapes(self):
        return [pltpu.SemaphoreType.DMA((self.n_sems,)), pltpu.SemaphoreType.DMA((self.n_sems,))]

    def start(self, in_refs, out_refs, send_sems, recv_sems):
        for cp in self.plan(in_refs, out_refs, send_sems, recv_sems)[0]:
            cp.start()

    def finish(self, in_refs, out_refs, send_sems, recv_sems):
        sends, recvs = self.plan(in_refs, out_refs, send_sems, recv_sems)
        for cp in recvs:
            cp.wait_recv()
        for cp in sends:
            cp.wait_send()


def _run_exchange(ex, name):
    n_in, n_out = len(ex.inputs), len(ex.out_shapes)

    def body(*refs):
        in_refs, out_refs, (send_sems, recv_sems) = refs[:n_in], refs[n_in:n_in + n_out], refs[n_in + n_out:]
        ex.start(in_refs, out_refs, send_sems, recv_sems)
        ex.finish(in_refs, out_refs, send_sems, recv_sems)

    return pl.pallas_call(
        body, name=name, in_specs=[ANY] * n_in, out_specs=[ANY] * n_out, out_shape=ex.out_shapes,
        scratch_shapes=ex.sem_shapes(), input_output_aliases=dict(ex.aliases),
    )(*ex.inputs)


def _call(body, *, name, grid, in_specs, out_specs, out_shape, args, semantics, scratch_shapes=(), exchange=None):
    if exchange is None:
        return pl.pallas_call(
            body, name=name, grid=grid, in_specs=in_specs, out_specs=out_specs, out_shape=out_shape,
            scratch_shapes=list(scratch_shapes), compiler_params=_params(*semantics))(*args)
    exs = exchange if isinstance(exchange, (list, tuple)) else [exchange]
    n_in, n_out, n_scr = len(in_specs), len(out_specs), len(scratch_shapes)
    x_in, x_out = [len(ex.inputs) for ex in exs], [len(ex.out_shapes) for ex in exs]

    def take(refs, counts):
        groups = []
        for n in counts:
            groups.append(refs[:n])
            refs = refs[n:]
        return groups, refs

    def carrier(*refs):
        ins, refs = refs[:n_in], refs[n_in:]
        x_ins, refs = take(refs, x_in)
        outs, refs = refs[:n_out], refs[n_out:]
        x_outs, refs = take(refs, x_out)
        scr, refs = refs[:n_scr], refs[n_scr:]
        sems, _ = take(refs, [2] * len(exs))
        ids = [pl.program_id(a) for a in range(len(grid))]
        first = functools.reduce(jnp.logical_and, [i == 0 for i in ids])
        last = functools.reduce(jnp.logical_and, [i == g - 1 for i, g in zip(ids, grid)])

        @pl.when(first)
        def _():
            for ex, xi, xo, (send_sems, recv_sems) in zip(exs, x_ins, x_outs, sems):
                ex.start(xi, xo, send_sems, recv_sems)

        body(*ins, *outs, *scr)

        @pl.when(last)
        def _():
            for ex, xi, xo, (send_sems, recv_sems) in zip(exs, x_ins, x_outs, sems):
                ex.finish(xi, xo, send_sems, recv_sems)

    aliases, i0, o0 = {}, n_in, n_out
    for ex in exs:
        aliases.update({i0 + i: o0 + o for i, o in ex.aliases.items()})
        i0, o0 = i0 + len(ex.inputs), o0 + len(ex.out_shapes)
    res = pl.pallas_call(
        carrier, name=name, grid=grid, in_specs=list(in_specs) + [ANY] * sum(x_in),
        out_specs=list(out_specs) + [ANY] * sum(x_out),
        out_shape=list(out_shape) + [s for ex in exs for s in ex.out_shapes],
        scratch_shapes=list(scratch_shapes) + [s for ex in exs for s in ex.sem_shapes()],
        input_output_aliases=aliases, compiler_params=_params(*["arbitrary"] * len(grid)),
    )(*args, *[a for ex in exs for a in ex.inputs])
    x_res, _ = take(list(res[n_out:]), x_out)
    return list(res[:n_out]), (x_res if isinstance(exchange, (list, tuple)) else x_res[0])


def _rmsmod_fwd(x, g, shift, scale, name):
    S, D = x.shape
    tr = min(ROW_TILE, S)

    def body(x_ref, g_ref, sh_ref, sc_ref, h_ref):
        xv = x_ref[...]
        rstd = lax.rsqrt(jnp.mean(xv * xv, axis=-1, keepdims=True) + EPS)
        y = xv * rstd * g_ref[...]
        h_ref[...] = (y * (1.0 + sc_ref[...]) + sh_ref[...]).astype(h_ref.dtype)

    row = pl.BlockSpec((tr, D), lambda i: (i, 0))
    vec = pl.BlockSpec((1, D), lambda i: (0, 0))
    return pl.pallas_call(
        body, name=name, grid=(S // tr,), in_specs=[row, vec, vec, vec], out_specs=row,
        out_shape=jax.ShapeDtypeStruct((S, D), BF16), compiler_params=_params("parallel"),
    )(x, g, shift, scale)


class _NormBwd:
    def __init__(self, x, g, scale, dx_res, below=None):
        S, D = x.shape
        self.below, self.coef = below, (below[2] if below else None)
        self.inputs = [x, g, scale, dx_res] + ([below[0], below[1]] if below else [])
        vshape = jax.ShapeDtypeStruct((1, D), F32)
        self.out_shape = [jax.ShapeDtypeStruct((S, D), F32), vshape, vshape, vshape]
        if below:
            self.out_shape += [jax.ShapeDtypeStruct((S, D), BF16), vshape]

    def specs(self, tr, D):
        row = pl.BlockSpec((tr, D), lambda i: (i, 0))
        vec = pl.BlockSpec((1, D), lambda i: (0, 0))
        return ([row, vec, vec, row] + ([row, vec] if self.below else []),
                [row, vec, vec, vec] + ([row, vec] if self.below else []))

    def step(self, dhv, in_refs, out_refs):
        if self.below:
            x_ref, g_ref, sc_ref, dxr_ref, f_ref, gate_ref = in_refs
            dx_ref, dsh_ref, dsc_ref, dg_ref, df_ref, dgate_ref = out_refs
            sums = (dsh_ref, dsc_ref, dg_ref, dgate_ref)
        else:
            x_ref, g_ref, sc_ref, dxr_ref = in_refs
            dx_ref, dsh_ref, dsc_ref, dg_ref = out_refs
            sums = (dsh_ref, dsc_ref, dg_ref)

        @pl.when(pl.program_id(0) == 0)
        def _():
            for ref in sums:
                ref[...] = jnp.zeros_like(ref)

        xv, gv = x_ref[...], g_ref[...]
        one_sc = 1.0 + sc_ref[...]
        rstd = lax.rsqrt(jnp.mean(xv * xv, axis=-1, keepdims=True) + EPS)
        n = xv * rstd
        dsh_ref[...] += jnp.sum(dhv, axis=0, keepdims=True)
        dsc_ref[...] += jnp.sum(dhv * n, axis=0, keepdims=True) * gv
        dg_ref[...] += jnp.sum(dhv * n, axis=0, keepdims=True) * one_sc
        dn = dhv * (gv * one_sc)
        dx = dxr_ref[...] + rstd * (dn - n * jnp.mean(dn * n, axis=-1, keepdims=True))
        dx_ref[...] = dx
        if self.below:
            df_ref[...] = (self.coef * gate_ref[...] * dx).astype(df_ref.dtype)
            dgate_ref[...] += self.coef * jnp.sum(dx * f_ref[...].astype(F32), axis=0, keepdims=True)


def _rmsmod_bwd(dh, norm, name, exchange=None):
    S, D = dh.shape
    tr = min(ROW_TILE, S)
    n_in = len(norm.inputs)

    def body(dh_ref, *refs):
        norm.step(dh_ref[...], refs[:n_in], refs[n_in:])

    in_specs, out_specs = norm.specs(tr, D)
    return _call(body, name=name, grid=(S // tr,), in_specs=[pl.BlockSpec((tr, D), lambda i: (i, 0))] + in_specs,
                 out_specs=out_specs, out_shape=norm.out_shape, args=[dh] + norm.inputs, semantics=("arbitrary",),
                 exchange=exchange)


def _ffn_in_fwd(h, w4, name, exchange=None):
    S, D = h.shape
    tm = min(ROW_TILE, S)
    n = w4.shape[2]

    def body(h_ref, wg_ref, wu_ref, zg_ref, zu_ref, a_ref):
        hv = h_ref[...]
        zg = _nn(hv, wg_ref[...])
        zu = _nn(hv, wu_ref[...])
        zg_ref[...] = zg.astype(zg_ref.dtype)
        zu_ref[...] = zu.astype(zu_ref.dtype)
        a_ref[...] = (zg * _sigmoid(zg) * zu).astype(a_ref.dtype)

    out = pl.BlockSpec((tm, n), lambda j, m: (m, j))
    oshape = jax.ShapeDtypeStruct((S, 2 * n), BF16)
    return _call(
        body, name=name, grid=(2, S // tm),
        in_specs=[pl.BlockSpec((tm, D), lambda j, m: (m, 0)),
                  pl.BlockSpec((None, D, n), lambda j, m: (j, 0, 0)),
                  pl.BlockSpec((None, D, n), lambda j, m: (j + 2, 0, 0))],
        out_specs=[out, out, out], out_shape=[oshape, oshape, oshape], args=(h, w4, w4),
        semantics=("parallel", "parallel"), exchange=exchange)


def _proj_out_fwd(lhs, w, x, gate, coef, name, exchange=None, next_norm=None):
    S, D = x.shape
    tm = min(ROW_TILE, S)
    ks = [a.shape[1] for a in lhs]

    def body(*refs):
        lhs_refs, refs = refs[:len(lhs)], refs[len(lhs):]
        if next_norm:
            w_ref, x_ref, gate_ref, g_ref, sh_ref, sc_ref, xn_ref, f_ref, h_ref = refs
        else:
            w_ref, x_ref, gate_ref, xn_ref, f_ref = refs
        acc, off = None, 0
        for a_ref, k in zip(lhs_refs, ks):
            part = _nn(a_ref[...], w_ref[off:off + k, :])
            acc = part if acc is None else acc + part
            off += k
        f_ref[...] = acc.astype(f_ref.dtype)
        xn = x_ref[...] + coef * gate_ref[...] * acc
        xn_ref[...] = xn
        if next_norm:
            rstd = lax.rsqrt(jnp.mean(xn * xn, axis=-1, keepdims=True) + EPS)
            h_ref[...] = (xn * rstd * g_ref[...] * (1.0 + sc_ref[...]) + sh_ref[...]).astype(h_ref.dtype)

    row = pl.BlockSpec((tm, D), lambda m: (m, 0))
    vec = pl.BlockSpec((1, D), lambda m: (0, 0))
    extra = list(next_norm) if next_norm else []
    return _call(
        body, name=name, grid=(S // tm,),
        in_specs=[pl.BlockSpec((tm, k), lambda m: (m, 0)) for k in ks]
        + [_resident(w.shape, lambda m: (0, 0)), row, vec] + [vec] * len(extra),
        out_specs=[row, row] + ([row] if next_norm else []),
        out_shape=[jax.ShapeDtypeStruct((S, D), F32), jax.ShapeDtypeStruct((S, D), BF16)]
        + ([jax.ShapeDtypeStruct((S, D), BF16)] if next_norm else []),
        args=(*lhs, w, x, gate, *extra), semantics=("parallel",), exchange=exchange)


def _proj_out_loss(lhs, w, x, gate, coef, target, name):
    S, D = x.shape
    tm = min(ROW_TILE, S)

    def body(a_ref, w_ref, x_ref, gate_ref, t_ref, dy_ref, df_ref, dgate_ref, sq_ref):
        @pl.when(pl.program_id(0) == 0)
        def _():
            dgate_ref[...] = jnp.zeros_like(dgate_ref)
            sq_ref[...] = jnp.zeros_like(sq_ref)

        f = _nn(a_ref[...], w_ref[...])
        gate = coef * gate_ref[...]
        err = x_ref[...] + gate * f - t_ref[...]
        sq_ref[...] += jnp.sum(err * err, axis=0, keepdims=True)
        dy = err * (1.0 / D)
        dy_ref[...] = dy
        df_ref[...] = (gate * dy).astype(df_ref.dtype)
        dgate_ref[...] += coef * jnp.sum(dy * f, axis=0, keepdims=True)

    row = pl.BlockSpec((tm, D), lambda m: (m, 0))
    vec = pl.BlockSpec((1, D), lambda m: (0, 0))
    vshape = jax.ShapeDtypeStruct((1, D), F32)
    return pl.pallas_call(
        body, name=name, grid=(S // tm,),
        in_specs=[pl.BlockSpec((tm, lhs.shape[1]), lambda m: (m, 0)), _resident(w.shape, lambda m: (0, 0)), row, vec, row],
        out_specs=[row, row, vec, vec],
        out_shape=[jax.ShapeDtypeStruct((S, D), F32), jax.ShapeDtypeStruct((S, D), BF16), vshape, vshape],
        compiler_params=_params("arbitrary"),
    )(lhs, w, x, gate, target)


def _matmul_nn(a, w, out_dtype, tm, name):
    S, K = a.shape
    N = w.shape[1]
    tm = min(tm, S)

    def body(a_ref, w_ref, o_ref):
        o_ref[...] = _nn(a_ref[...], w_ref[...]).astype(o_ref.dtype)

    return pl.pallas_call(
        body, name=name, grid=(S // tm,),
        in_specs=[pl.BlockSpec((tm, K), lambda m: (m, 0)), _resident((K, N), lambda m: (0, 0))],
        out_specs=pl.BlockSpec((tm, N), lambda m: (m, 0)), out_shape=jax.ShapeDtypeStruct((S, N), out_dtype),
        compiler_params=_params("parallel"),
    )(a, w)


def _dact_bwd(df, w_out, zg, zu, name, exchange=None):
    S, D = df.shape
    tm = min(ROW_TILE, S)
    n = w_out.shape[0] // 2

    def body(df_ref, w_ref, zg_ref, zu_ref, dzg_ref, dzu_ref):
        da = _nt(df_ref[...], w_ref[...])
        zg_v, zu_v = zg_ref[...].astype(F32), zu_ref[...].astype(F32)
        s = _sigmoid(zg_v)
        dzu_ref[...] = (da * zg_v * s).astype(dzu_ref.dtype)
        dzg_ref[...] = (da * zu_v * (s * (1.0 + zg_v * (1.0 - s)))).astype(dzg_ref.dtype)

    blk = pl.BlockSpec((tm, n), lambda j, m: (m, j))
    oshape = jax.ShapeDtypeStruct((S, 2 * n), BF16)
    return _call(
        body, name=name, grid=(2, S // tm),
        in_specs=[pl.BlockSpec((tm, D), lambda j, m: (m, 0)), pl.BlockSpec((n, D), lambda j, m: (j, 0)), blk, blk],
        out_specs=[blk, blk], out_shape=[oshape, oshape], args=(df, w_out, zg, zu), semantics=("parallel", "parallel"),
        exchange=exchange)


def _ffn_in_dgrad(dzg, dzu, w4, name, exchange=None, norm=None):
    S = dzg.shape[0]
    D, n = w4.shape[1], w4.shape[2]
    tm = min(ROW_TILE, S)
    n_norm = len(norm.inputs) if norm else 0

    def body(dzg_ref, dzu_ref, w_ref, *refs):
        acc = _nt(dzg_ref[:, 0:n], w_ref[0])
        acc += _nt(dzg_ref[:, n:2 * n], w_ref[1])
        acc += _nt(dzu_ref[:, 0:n], w_ref[2])
        acc += _nt(dzu_ref[:, n:2 * n], w_ref[3])
        if norm:
            norm.step(acc, refs[:n_norm], refs[n_norm:])
        else:
            refs[0][...] = acc

    blk = pl.BlockSpec((tm, 2 * n), lambda m: (m, 0))
    in_specs, args = [blk, blk, _resident(w4.shape, lambda m: (0, 0, 0))], [dzg, dzu, w4]
    out_specs, out_shape = [pl.BlockSpec((tm, D), lambda m: (m, 0))], [jax.ShapeDtypeStruct((S, D), F32)]
    if norm:
        norm_in, out_specs = norm.specs(tm, D)
        in_specs, args, out_shape = in_specs + norm_in, args + norm.inputs, norm.out_shape
    return _call(body, name=name, grid=(S // tm,), in_specs=in_specs, out_specs=out_specs, out_shape=out_shape, args=args,
                 semantics=("arbitrary",) if norm else ("parallel",), exchange=exchange)


def _matmul_nt(a, w, tm, name, exchange=None, norm=None):
    S, K = a.shape
    N = w.shape[0]
    tm = min(tm, S)
    n_norm = len(norm.inputs) if norm else 0

    def body(a_ref, w_ref, *refs):
        acc = _nt(a_ref[...], w_ref[...])
        if norm:
            norm.step(acc, refs[:n_norm], refs[n_norm:])
        else:
            refs[0][...] = acc

    in_specs, args = [pl.BlockSpec((tm, K), lambda m: (m, 0)), _resident((N, K), lambda m: (0, 0))], [a, w]
    out_specs, out_shape = [pl.BlockSpec((tm, N), lambda m: (m, 0))], [jax.ShapeDtypeStruct((S, N), F32)]
    if norm:
        norm_in, out_specs = norm.specs(tm, N)
        in_specs, args, out_shape = in_specs + norm_in, args + norm.inputs, norm.out_shape
    return _call(body, name=name, grid=(S // tm,), in_specs=in_specs, out_specs=out_specs, out_shape=out_shape, args=args,
                 semantics=("arbitrary",) if norm else ("parallel",), exchange=exchange)


def _wgrad(a, gs, tn, name, exchange=None):
    S, Ka = a.shape
    N = gs[0].shape[1]
    ts = min(ROW_TILE, S)

    def body(a_ref, *refs):
        g_refs, o_ref = refs[:-1], refs[-1]

        @pl.when(pl.program_id(1) == 0)
        def _():
            o_ref[...] = jnp.zeros_like(o_ref)

        a_t = a_ref[...].T
        for i, g_ref in enumerate(g_refs):
            o_ref[i] += _nn(a_t, g_ref[...])

    return _call(
        body, name=name, grid=(N // tn, S // ts),
        in_specs=[pl.BlockSpec((ts, Ka), lambda j, s: (s, 0))] + [pl.BlockSpec((ts, tn), lambda j, s: (s, j))] * len(gs),
        out_specs=[pl.BlockSpec((len(gs), None, Ka, tn), lambda j, s: (0, j, 0, 0))],
        out_shape=[jax.ShapeDtypeStruct((len(gs), N // tn, Ka, tn), F32)], args=(a, *gs),
        semantics=("parallel", "arbitrary"), exchange=exchange)


def _wgrad_rows(lhs, g, name):
    S, Ka = lhs[0].shape
    N = g.shape[1]
    ts = min(ROW_TILE, S)

    def body(*refs):
        a_refs, g_ref, o_ref = refs[:-2], refs[-2], refs[-1]

        @pl.when(pl.program_id(0) == 0)
        def _():
            o_ref[...] = jnp.zeros_like(o_ref)

        gv = g_ref[...]
        for i, a_ref in enumerate(a_refs):
            o_ref[i] += _tn(a_ref[...], gv)

    return pl.pallas_call(
        body, name=name, grid=(S // ts,),
        in_specs=[pl.BlockSpec((ts, Ka), lambda s: (s, 0))] * len(lhs) + [pl.BlockSpec((ts, N), lambda s: (s, 0))],
        out_specs=pl.BlockSpec((len(lhs), Ka, N), lambda s: (0, 0, 0)),
        out_shape=jax.ShapeDtypeStruct((len(lhs), Ka, N), F32), compiler_params=_params("arbitrary"),
    )(*lhs, g)


def _hgrn_chunk_common(qr, fr, oml, tri, last):
    k = oml * _sigmoid(-fr)
    g = jnp.log1p(-k) * math.log2(math.e)
    q = qr * _sigmoid(qr)
    G = _nn(tri, g, precision=lax.Precision.HIGHEST)
    Gl = G[last:last + 1]
    return q, k, G, Gl


def _hgrn_consts(reverse):
    C = HG_CHUNK
    r = lax.broadcasted_iota(jnp.int32, (C, C), 0)
    cc = lax.broadcasted_iota(jnp.int32, (C, C), 1)
    tri = ((cc >= r) if reverse else (cc <= r)).astype(F32)
    tri_t = ((cc <= r) if reverse else (cc >= r)).astype(F32)
    rid = lax.broadcasted_iota(jnp.int32, (C, HG_WIDTH), 0)
    return tri, tri_t, rid, (0 if reverse else C - 1)


def _head_slices():
    return [slice(h * HG_DIM, (h + 1) * HG_DIM) for h in range(HG_HEADS)]


def _per_head_lane_sum(x):
    C = x.shape[0]
    return jnp.concatenate(
        [jnp.broadcast_to(jnp.sum(x[:, sl], axis=-1, keepdims=True), (C, HG_DIM)) for sl in _head_slices()], axis=1)


HG_TILE = 8


def _pair_tiles(s, reverse):
    blk, r = divmod(s, HG_TILE)
    n_tiles = HG_CHUNK // HG_TILE
    others = range(0, blk) if reverse else range(blk + 1, n_tiles)
    return [(blk, r)] + [(t, None) for t in others]


def _pair_decay(G, s, tile, r, rid8, reverse, keys=False):
    rs = slice(tile * HG_TILE, (tile + 1) * HG_TILE)
    d = (G[s:s + 1] - G[rs]) if keys else (G[rs] - G[s:s + 1])
    if r is not None:
        d = jnp.where((rid8 <= r) if reverse else (rid8 >= r), d, NEG)
    return rs, jnp.exp2(d)


def _hgrn_fwd(z, lb, direction, name, exchange=None):
    S = z.shape[0]
    C, DK, W = HG_CHUNK, HG_DIM, HG_WIDTH
    tb = min(HG_ROWS, S)
    n_t, n_c = S // tb, tb // C
    reverse = direction == 1
    tmap = (lambda i: n_t - 1 - i) if reverse else (lambda i: i)

    def body(q_ref, f_ref, v_ref, lb_ref, o_ref, st_out_ref, st_ref):
        @pl.when(pl.program_id(0) == 0)
        def _():
            st_ref[...] = jnp.zeros_like(st_ref)

        oml = 1.0 - lb_ref[...]
        tri, _, _, last = _hgrn_consts(reverse)
        rid8 = lax.broadcasted_iota(jnp.int32, (HG_TILE, W), 0)

        def chunk(ci, carry):
            cidx = (n_c - 1 - ci) if reverse else ci
            rows = pl.ds(pl.multiple_of(cidx * C, C), C)
            v = v_ref[rows, :]
            q, k, G, Gl = _hgrn_chunk_common(q_ref[rows, :], f_ref[rows, :], oml, tri, last)
            qd = (q * jnp.exp2(G)).astype(BF16)
            kd = (k * jnp.exp2(Gl - G)).astype(BF16)
            e_gl = jnp.exp2(Gl)
            v_b = v.astype(BF16)
            inter = []
            for h, sl in enumerate(_head_slices()):
                st0 = st_ref[h]
                st_out_ref[h, cidx] = st0
                inter.append(_nt(qd[:, sl], st0.astype(BF16)))
                st_ref[h] = st0 * e_gl[:, sl] + _tn(v_b[:, sl], kd[:, sl])
            o = jnp.concatenate(inter, axis=1)
            o_t = [o[t * HG_TILE:(t + 1) * HG_TILE] for t in range(C // HG_TILE)]
            for s in range(C):
                k_s, v_s = k[s:s + 1], v[s:s + 1]
                for tile, r in _pair_tiles(s, reverse):
                    rs, e_s = _pair_decay(G, s, tile, r, rid8, reverse)
                    o_t[tile] = o_t[tile] + _per_head_lane_sum(q[rs] * k_s * e_s) * v_s
            o_ref[rows, :] = jnp.concatenate(o_t, axis=0)
            return carry

        lax.fori_loop(0, n_c, chunk, 0, unroll=8)

    def sec(j):
        return pl.BlockSpec((tb, W), lambda i: (tmap(i), j))

    return _call(
        body, name=name, grid=(n_t,),
        in_specs=[sec(0), sec(1 + direction), sec(3), pl.BlockSpec((1, W), lambda i: (0, 0))],
        out_specs=[sec(0), pl.BlockSpec((HG_HEADS, n_c, DK, DK), lambda i: (0, tmap(i), 0, 0))],
        out_shape=[jax.ShapeDtypeStruct((S, W), F32), jax.ShapeDtypeStruct((HG_HEADS, S // C, DK, DK), F32)],
        scratch_shapes=[pltpu.VMEM((HG_HEADS, DK, DK), F32)], args=(z, z, z, lb), semantics=("arbitrary",),
        exchange=exchange)


def _hgrn_bwd(z, lb, do, states, direction, name, acc=None, exchange=None):
    S = z.shape[0]
    C, DK, W = HG_CHUNK, HG_DIM, HG_WIDTH
    tb = min(HG_ROWS, S)
    n_t, n_c = S // tb, tb // C
    reverse = direction == 1
    tmap = (lambda i: i) if reverse else (lambda i: n_t - 1 - i)

    def body(*refs):
        if acc:
            q_ref, f_ref, v_ref, lb_ref, do_ref, st_in_ref, dqa_ref, dva_ref, dq_ref, df_ref, dv_ref, doml_ref, dst_ref = refs
        else:
            q_ref, f_ref, v_ref, lb_ref, do_ref, st_in_ref, dq_ref, df_ref, dv_ref, doml_ref, dst_ref = refs

        @pl.when(pl.program_id(0) == 0)
        def _():
            dst_ref[...] = jnp.zeros_like(dst_ref)
            doml_ref[...] = jnp.zeros_like(doml_ref)

        oml = 1.0 - lb_ref[...]
        tri, tri_t, rid, last = _hgrn_consts(reverse)
        rid8 = lax.broadcasted_iota(jnp.int32, (HG_TILE, W), 0)

        def chunk(ci, carry):
            cidx = ci if reverse else (n_c - 1 - ci)
            rows = pl.ds(pl.multiple_of(cidx * C, C), C)
            qr, fr, v, dov = q_ref[rows, :], f_ref[rows, :], v_ref[rows, :], do_ref[rows, :]
            q, k, G, Gl = _hgrn_chunk_common(qr, fr, oml, tri, last)
            e_g, e_gl, e_kd = jnp.exp2(G), jnp.exp2(Gl), jnp.exp2(Gl - G)
            qd, kd = q * e_g, k * e_kd
            do_b, v_b, qd_b, kd_b = dov.astype(BF16), v.astype(BF16), qd.astype(BF16), kd.astype(BF16)
            dqd, dkd, dv, state_dot = [], [], [], []
            for h, sl in enumerate(_head_slices()):
                st0, dst1 = st_in_ref[h, cidx], dst_ref[h]
                dst1_b = dst1.astype(BF16)
                dqd.append(_nn(do_b[:, sl], st0.astype(BF16)))
                dkd.append(_nn(v_b[:, sl], dst1_b))
                dv.append(_nt(kd_b[:, sl], dst1_b))
                state_dot.append(jnp.sum(st0 * dst1, axis=0, keepdims=True))
                dst_ref[h] = dst1 * e_gl[:, sl] + _tn(do_b[:, sl], qd_b[:, sl])
            dqd, dkd, dv = [jnp.concatenate(t, axis=1) for t in (dqd, dkd, dv)]
            d_gl = e_gl * jnp.concatenate(state_dot, axis=1) + jnp.sum(dkd * kd, axis=0, keepdims=True)
            dq, dk = dqd * e_g, dkd * e_kd
            n_tiles = C // HG_TILE
            dq_t, dk_t, dv_t = [[x[t * HG_TILE:(t + 1) * HG_TILE] for t in range(n_tiles)] for x in (dq, dk, dv)]
            for s in range(C):
                k_s, v_s = k[s:s + 1], v[s:s + 1]
                for tile, r in _pair_tiles(s, reverse):
                    rs, e_s = _pair_decay(G, s, tile, r, rid8, reverse)
                    dq_t[tile] = dq_t[tile] + _per_head_lane_sum(dov[rs] * v_s) * e_s * k_s
            for t in range(C):
                q_t, do_t = q[t:t + 1], dov[t:t + 1]
                for tile, r in _pair_tiles(t, not reverse):
                    rs, x_t = _pair_decay(G, t, tile, r, rid8, not reverse, keys=True)
                    qx = q_t * x_t
                    dv_t[tile] = dv_t[tile] + _per_head_lane_sum(k[rs] * qx) * do_t
                    dk_t[tile] = dk_t[tile] + _per_head_lane_sum(v[rs] * do_t) * qx
            dq, dk, dv = [jnp.concatenate(x, axis=0) for x in (dq_t, dk_t, dv_t)]
            d_big_g = dq * q - dk * k + jnp.where(rid == last, d_gl, 0.0)
            dg = _nn(tri_t, d_big_g, precision=lax.Precision.HIGHEST)
            dk_all = dk - dg / (1.0 - k)
            sig_nf = _sigmoid(-fr)
            df_ref[rows, :] = -dk_all * k * (1.0 - sig_nf)
            doml_ref[...] += jnp.sum(dk_all * sig_nf, axis=0, keepdims=True)
            sq = _sigmoid(qr)
            dqr = dq * (sq * (1.0 + qr * (1.0 - sq)))
            if acc:
                dqr = dqr + dqa_ref[rows, :]
                dv = dv + dva_ref[rows, :]
            dq_ref[rows, :] = dqr
            dv_ref[rows, :] = dv
            return carry

        lax.fori_loop(0, n_c, chunk, 0, unroll=8)

    def sec(j):
        return pl.BlockSpec((tb, W), lambda i: (tmap(i), j))

    vec = pl.BlockSpec((1, W), lambda i: (0, 0))
    ins = [z, z, z, lb, do, states]
    in_specs = [sec(0), sec(1 + direction), sec(3), vec, sec(0),
                pl.BlockSpec((HG_HEADS, n_c, DK, DK), lambda i: (0, tmap(i), 0, 0))]
    if acc:
        ins += list(acc)
        in_specs += [sec(0), sec(0)]
    full = jax.ShapeDtypeStruct((S, W), F32)
    return _call(
        body, name=name, grid=(n_t,), in_specs=in_specs,
        out_specs=[sec(0), sec(0), sec(0), vec],
        out_shape=[full, full, full, jax.ShapeDtypeStruct((1, W), F32)],
        scratch_shapes=[pltpu.VMEM((HG_HEADS, DK, DK), F32)], args=ins, semantics=("arbitrary",), exchange=exchange)


def _hgrn_post_fwd(o_f, o_b, z, norm_g, name):
    S = z.shape[0]
    tr = min(ROW_TILE, S)

    def body(of_ref, ob_ref, gr_ref, ng_ref, y_ref):
        o = of_ref[...] + ob_ref[...]
        gr = gr_ref[...]
        gate = gr * _sigmoid(gr)
        ng = ng_ref[...]
        for h in range(HG_HEADS):
            sl = slice(h * HG_DIM, (h + 1) * HG_DIM)
            oh = o[:, sl]
            rstd = lax.rsqrt(jnp.mean(oh * oh, axis=-1, keepdims=True) + EPS)
            y_ref[:, sl] = (oh * rstd * ng[:, sl] * gate[:, sl]).astype(y_ref.dtype)

    row = pl.BlockSpec((tr, HG_WIDTH), lambda i: (i, 0))
    return pl.pallas_call(
        body, name=name, grid=(S // tr,),
        in_specs=[row, row, pl.BlockSpec((tr, HG_WIDTH), lambda i: (i, 4)), pl.BlockSpec((1, HG_WIDTH), lambda i: (0, 0))],
        out_specs=row, out_shape=jax.ShapeDtypeStruct((S, HG_WIDTH), BF16), compiler_params=_params("parallel"),
    )(o_f, o_b, z, norm_g)


def _hgrn_post_bwd(dy, o_f, o_b, z, norm_g, name):
    S = z.shape[0]
    tr = min(ROW_TILE, S)

    def body(dy_ref, of_ref, ob_ref, gr_ref, ng_ref, do_ref, dgr_ref, dng_ref):
        @pl.when(pl.program_id(0) == 0)
        def _():
            dng_ref[...] = jnp.zeros_like(dng_ref)

        o = of_ref[...] + ob_ref[...]
        gr, ng, dyv = gr_ref[...], ng_ref[...], dy_ref[...]
        sg = _sigmoid(gr)
        for h in range(HG_HEADS):
            sl = slice(h * HG_DIM, (h + 1) * HG_DIM)
            oh, dyh, grh, sgh, ngh = o[:, sl], dyv[:, sl], gr[:, sl], sg[:, sl], ng[:, sl]
            rstd = lax.rsqrt(jnp.mean(oh * oh, axis=-1, keepdims=True) + EPS)
            on = oh * rstd
            du = dyh * (grh * sgh)
            dgr_ref[:, sl] = dyh * (on * ngh) * (sgh * (1.0 + grh * (1.0 - sgh)))
            dng_ref[:, sl] += jnp.sum(du * on, axis=0, keepdims=True)
            don = du * ngh
            do_ref[:, sl] = rstd * (don - on * jnp.mean(don * on, axis=-1, keepdims=True))

    row = pl.BlockSpec((tr, HG_WIDTH), lambda i: (i, 0))
    vec = pl.BlockSpec((1, HG_WIDTH), lambda i: (0, 0))
    full = jax.ShapeDtypeStruct((S, HG_WIDTH), F32)
    return pl.pallas_call(
        body, name=name, grid=(S // tr,),
        in_specs=[row, row, row, pl.BlockSpec((tr, HG_WIDTH), lambda i: (i, 4)), vec],
        out_specs=[row, row, vec], out_shape=[full, full, jax.ShapeDtypeStruct((1, HG_WIDTH), F32)],
        compiler_params=_params("arbitrary"),
    )(dy, o_f, o_b, z, norm_g)


def _t5_bucket_table():
    rel = (np.arange(3 * BLOCK)[None, :] - BLOCK) - np.arange(BLOCK)[:, None]
    nb = NUM_BUCKETS // 2
    max_exact = nb // 2
    ret = (rel > 0).astype(np.int32) * nb
    n = np.abs(rel)
    ratio = np.log(np.maximum(n, 1).astype(np.float32) / np.float32(max_exact)) / np.float32(math.log(MAX_DISTANCE / max_exact))
    large = max_exact + (ratio.astype(np.float32) * np.float32(nb - max_exact)).astype(np.int32)
    large = np.minimum(large, nb - 1)
    bucket = ret + np.where(n < max_exact, n, large)
    return bucket.astype(np.int32), (n <= WINDOW)


def _bias_table(rel_bias, name):
    bucket, in_band = _t5_bucket_table()
    idx = jnp.asarray(np.where(in_band, bucket, -1))

    def body(rb_ref, idx_ref, o_ref):
        h = pl.program_id(0)
        iv = idx_ref[...]
        acc = jnp.where(iv < 0, NEG, 0.0).astype(F32)
        for b in range(NUM_BUCKETS):
            acc = acc + jnp.where(iv == b, rb_ref[b, h], 0.0)
        o_ref[...] = acc

    return pl.pallas_call(
        body, name=name, grid=(ATT_Q_HEADS,),
        in_specs=[pl.BlockSpec(memory_space=pltpu.SMEM), pl.BlockSpec((BLOCK, 3 * BLOCK), lambda h: (0, 0))],
        out_specs=pl.BlockSpec((None, BLOCK, 3 * BLOCK), lambda h: (h, 0, 0)),
        out_shape=jax.ShapeDtypeStruct((ATT_Q_HEADS, BLOCK, 3 * BLOCK), F32), compiler_params=_params("parallel"),
    )(rel_bias, idx)


def _bias_grad(ds_sum, name):
    bucket, in_band = _t5_bucket_table()
    idx = jnp.asarray(np.where(in_band, bucket, -1))

    def body(ds_ref, idx_ref, o_ref):
        iv, ds = idx_ref[...], ds_ref[...]
        for b in range(NUM_BUCKETS):
            part = jnp.sum(jnp.where(iv == b, ds, 0.0), axis=0, keepdims=True)
            o_ref[b:b + 1, :] = part[:, 0:BLOCK] + part[:, BLOCK:2 * BLOCK] + part[:, 2 * BLOCK:3 * BLOCK]

    return pl.pallas_call(
        body, name=name, grid=(ATT_Q_HEADS,),
        in_specs=[pl.BlockSpec((None, BLOCK, 3 * BLOCK), lambda h: (h, 0, 0)), pl.BlockSpec((BLOCK, 3 * BLOCK), lambda h: (0, 0))],
        out_specs=pl.BlockSpec((None, NUM_BUCKETS, BLOCK), lambda h: (h, 0, 0)),
        out_shape=jax.ShapeDtypeStruct((ATT_Q_HEADS, NUM_BUCKETS, BLOCK), F32), compiler_params=_params("parallel"),
    )(ds_sum, idx)


def _attn_specs(nb):
    G, dh = ATT_GROUP, ATT_HEAD_DIM
    qspec = pl.BlockSpec((G, BLOCK, dh), lambda j, n: (j, n, 0))

    def kv(shift):
        return pl.BlockSpec((None, BLOCK, dh), lambda j, n: (j, jnp.clip(n + shift, 0, nb - 1), 0))

    gain = pl.BlockSpec((1, dh), lambda j, n: (0, 0))
    sink = pl.BlockSpec((G, 1, BLOCK), lambda j, n: (j, 0, 0))
    bias = pl.BlockSpec((G, BLOCK, 3 * BLOCK), lambda j, n: (j, 0, 0))
    return qspec, kv, gain, sink, bias


def _attn_probs(qh, kn, bias_h, sink_h, edge_ok):
    s = _nt(qh.astype(BF16), kn.astype(BF16)) * (1.0 / math.sqrt(ATT_HEAD_DIM)) + bias_h
    s = jnp.where(edge_ok, s, NEG)
    m = jnp.maximum(jnp.max(s, axis=-1, keepdims=True), sink_h)
    p = jnp.exp(s - m)
    e_sink = jnp.exp(sink_h - m)
    inv = 1.0 / (jnp.sum(p, axis=-1, keepdims=True) + e_sink)
    return p * inv, e_sink * inv


def _rms_rows(x):
    rstd = lax.rsqrt(jnp.mean(x * x, axis=-1, keepdims=True) + EPS)
    return x * rstd, rstd


def _edge_ok(n, nb):
    colid = lax.broadcasted_iota(jnp.int32, (ATT_GROUP * BLOCK, 3 * BLOCK), 1)
    return jnp.logical_and(jnp.logical_or(colid >= BLOCK, n > 0), jnp.logical_or(colid < 2 * BLOCK, n < nb - 1))


def _sink_column(sink_ref):
    return jnp.concatenate([jnp.broadcast_to(sink_ref[g][:, 0:1], (BLOCK, 1)) for g in range(ATT_GROUP)], axis=0)


def _attn_fwd(q, k, v, q_g, k_g, sink, bias, name):
    S = q.shape[1]
    nb = S // BLOCK
    G, dh = ATT_GROUP, ATT_HEAD_DIM
    qspec, kv, gain, sink_spec, bias_spec = _attn_specs(nb)

    def body(q_ref, k0, k1, k2, v0, v1, v2, qg_ref, kg_ref, sink_ref, bias_ref, o_ref):
        n = pl.program_id(1)
        kcat = jnp.concatenate([k0[...], k1[...], k2[...]], axis=0)
        vcat = jnp.concatenate([v0[...], v1[...], v2[...]], axis=0).astype(BF16)
        kn = _rms_rows(kcat)[0] * kg_ref[...]
        qn = _rms_rows(q_ref[...].reshape(G * BLOCK, dh))[0] * qg_ref[...]
        p, _ = _attn_probs(qn, kn, bias_ref[...].reshape(G * BLOCK, 3 * BLOCK), _sink_column(sink_ref), _edge_ok(n, nb))
        o_ref[...] = _nn(p.astype(BF16), vcat).reshape(G, BLOCK, dh)

    return pl.pallas_call(
        body, name=name, grid=(ATT_KV_HEADS, nb),
        in_specs=[qspec, kv(-1), kv(0), kv(1), kv(-1), kv(0), kv(1), gain, gain, sink_spec, bias_spec],
        out_specs=qspec, out_shape=jax.ShapeDtypeStruct(q.shape, F32), compiler_params=_params("parallel", "parallel"),
    )(q, k, k, k, v, v, v, q_g, k_g, sink, bias)


def _attn_bwd(q, k, v, q_g, k_g, sink, bias, do, name):
    S = q.shape[1]
    nb = S // BLOCK
    G, dh = ATT_GROUP, ATT_HEAD_DIM
    scale = 1.0 / math.sqrt(dh)
    qspec, kv, gain, sink_spec, bias_spec = _attn_specs(nb)

    def body(q_ref, k0, k1, k2, v0, v1, v2, qg_ref, kg_ref, sink_ref, bias_ref, do_ref,
             dq_ref, dkw_ref, dvw_ref, ds_ref, dsink_ref, dqg_ref):
        n = pl.program_id(1)

        @pl.when(n == 0)
        def _():
            ds_ref[...] = jnp.zeros_like(ds_ref)
            dsink_ref[...] = jnp.zeros_like(dsink_ref)
            dqg_ref[...] = jnp.zeros_like(dqg_ref)

        kcat = jnp.concatenate([k0[...], k1[...], k2[...]], axis=0)
        vcat = jnp.concatenate([v0[...], v1[...], v2[...]], axis=0).astype(BF16)
        kn = _rms_rows(kcat)[0] * kg_ref[...]
        qg = qg_ref[...]
        qhat, rstd = _rms_rows(q_ref[...].reshape(G * BLOCK, dh))
        qn = qhat * qg
        p, p_sink = _attn_probs(qn, kn, bias_ref[...].reshape(G * BLOCK, 3 * BLOCK), _sink_column(sink_ref), _edge_ok(n, nb))
        do_b = do_ref[...].reshape(G * BLOCK, dh).astype(BF16)
        dp = _nt(do_b, vcat)
        delta = jnp.sum(p * dp, axis=-1, keepdims=True)
        ds = p * (dp - delta)
        ds_ref[...] += ds.reshape(G, BLOCK, 3 * BLOCK)
        sink_term = p_sink * delta
        for g in range(G):
            dsink_ref[g] += jnp.zeros((1, BLOCK), F32) - jnp.sum(sink_term[g * BLOCK:(g + 1) * BLOCK], axis=0, keepdims=True)
        ds_b = ds.astype(BF16)
        dvw_ref[...] = _tn(p.astype(BF16), do_b)
        dkw_ref[...] = _tn(ds_b, qn.astype(BF16)) * scale
        dqn = _nn(ds_b, kn.astype(BF16)) * scale
        dqg_ref[...] += jnp.sum(dqn * qhat, axis=0, keepdims=True)
        dqh = dqn * qg
        dq_ref[...] = (rstd * (dqh - qhat * jnp.mean(dqh * qhat, axis=-1, keepdims=True))).reshape(G, BLOCK, dh)

    win = pl.BlockSpec((None, None, 3 * BLOCK, dh), lambda j, n: (j, n, 0, 0))
    wshape = jax.ShapeDtypeStruct((ATT_KV_HEADS, nb, 3 * BLOCK, dh), F32)
    return pl.pallas_call(
        body, name=name, grid=(ATT_KV_HEADS, nb),
        in_specs=[qspec, kv(-1), kv(0), kv(1), kv(-1), kv(0), kv(1), gain, gain, sink_spec, bias_spec, qspec],
        out_specs=[qspec, win, win, bias_spec, sink_spec, pl.BlockSpec((None, 1, dh), lambda j, n: (j, 0, 0))],
        out_shape=[jax.ShapeDtypeStruct(q.shape, F32), wshape, wshape,
                   jax.ShapeDtypeStruct((ATT_Q_HEADS, BLOCK, 3 * BLOCK), F32),
                   jax.ShapeDtypeStruct((ATT_Q_HEADS, 1, BLOCK), F32),
                   jax.ShapeDtypeStruct((ATT_KV_HEADS, 1, dh), F32)],
        compiler_params=_params("parallel", "arbitrary"),
    )(q, k, k, k, v, v, v, q_g, k_g, sink, bias, do)


def _attn_kv_reduce(dkw, dvw, k, k_g, name):
    S = k.shape[1]
    nb = S // BLOCK
    dh = ATT_HEAD_DIM
    kb = min(8, nb)
    steps = nb // kb

    def body(a_lo, a, a_hi, b_lo, b, b_hi, k_ref, kg_ref, dk_ref, dv_ref, dkg_ref):
        n = pl.program_id(1)

        @pl.when(n == 0)
        def _():
            dkg_ref[...] = jnp.zeros_like(dkg_ref)

        lo = jnp.where(n > 0, 1.0, 0.0)
        hi = jnp.where(n < steps - 1, 1.0, 0.0)

        def overlap_add(w, w_lo, w_hi, i):
            before = lo * w_lo[...] if i == 0 else w[i - 1, 2 * BLOCK:3 * BLOCK, :]
            after = hi * w_hi[...] if i == kb - 1 else w[i + 1, 0:BLOCK, :]
            return w[i, BLOCK:2 * BLOCK, :] + before + after

        dkg = jnp.zeros((1, dh), F32)
        for i in range(kb):
            rows = slice(i * BLOCK, (i + 1) * BLOCK)
            dkn = overlap_add(a, a_lo, a_hi, i)
            dv_ref[rows, :] = overlap_add(b, b_lo, b_hi, i)
            khat, rstd = _rms_rows(k_ref[rows, :])
            dkg = dkg + jnp.sum(dkn * khat, axis=0, keepdims=True)
            dkh = dkn * kg_ref[...]
            dk_ref[rows, :] = rstd * (dkh - khat * jnp.mean(dkh * khat, axis=-1, keepdims=True))
        dkg_ref[...] += dkg

    main = pl.BlockSpec((None, kb, 3 * BLOCK, dh), lambda j, n: (j, n, 0, 0))
    halo_lo = pl.BlockSpec((None, None, BLOCK, dh), lambda j, n: (j, jnp.maximum(n * kb - 1, 0), 2, 0))
    halo_hi = pl.BlockSpec((None, None, BLOCK, dh), lambda j, n: (j, jnp.minimum(n * kb + kb, nb - 1), 0, 0))
    blk = pl.BlockSpec((None, kb * BLOCK, dh), lambda j, n: (j, n, 0))
    return pl.pallas_call(
        body, name=name, grid=(ATT_KV_HEADS, steps),
        in_specs=[halo_lo, main, halo_hi, halo_lo, main, halo_hi, blk, pl.BlockSpec((1, dh), lambda j, n: (0, 0))],
        out_specs=[blk, blk, pl.BlockSpec((None, 1, dh), lambda j, n: (j, 0, 0))],
        out_shape=[jax.ShapeDtypeStruct(k.shape, F32), jax.ShapeDtypeStruct(k.shape, F32),
                   jax.ShapeDtypeStruct((ATT_KV_HEADS, 1, dh), F32)],
        compiler_params=_params("parallel", "arbitrary"),
    )(dkw, dkw, dkw, dvw, dvw, dvw, k, k_g)


def _ada_fwd(c_act, w, b, name):
    n = w.shape[1]

    def body(c_ref, w_ref, b_ref, o_ref):
        o_ref[...] = _nn(c_ref[...], w_ref[...], precision=lax.Precision.HIGHEST) + b_ref[...]

    tn = n // 3
    return pl.pallas_call(
        body, name=name, grid=(3,),
        in_specs=[pl.BlockSpec(c_act.shape, lambda j: (0, 0)), pl.BlockSpec((w.shape[0], tn), lambda j: (0, j)),
                  pl.BlockSpec((1, tn), lambda j: (0, j))],
        out_specs=pl.BlockSpec((c_act.shape[0], tn), lambda j: (0, j)),
        out_shape=jax.ShapeDtypeStruct((c_act.shape[0], n), F32), compiler_params=_params("parallel"),
    )(c_act, w, b)


def _ada_wgrad(c_act_t, dm, name):
    D, nbatch = c_act_t.shape
    n = dm.shape[1]
    tr = 256

    def body(c_ref, dm_ref, o_ref):
        cv, dv = c_ref[...], dm_ref[...]
        acc = cv[:, 0:1] * dv[0:1, :]
        for b in range(1, nbatch):
            acc = acc + cv[:, b:b + 1] * dv[b:b + 1, :]
        o_ref[...] = acc

    return pl.pallas_call(
        body, name=name, grid=(D // tr,),
        in_specs=[pl.BlockSpec((tr, nbatch), lambda i: (i, 0)), pl.BlockSpec((nbatch, n), lambda i: (0, 0))],
        out_specs=pl.BlockSpec((tr, n), lambda i: (i, 0)), out_shape=jax.ShapeDtypeStruct((D, n), F32),
        compiler_params=_params("parallel"),
    )(c_act_t, dm)


def _to_bf16(w, name):
    R, Cn = w.shape
    tr = _row_tile(R)

    def body(w_ref, o_ref):
        o_ref[...] = w_ref[...].astype(BF16)

    blk = pl.BlockSpec((tr, Cn), lambda i: (i, 0))
    return pl.pallas_call(
        body, name=name, grid=(R // tr,), in_specs=[blk], out_specs=blk, out_shape=jax.ShapeDtypeStruct((R, Cn), BF16),
        compiler_params=_params("parallel"),
    )(w)


def _adamw(w, g, m, v, name):
    R, Cn = w.shape
    tr = R
    for cand in (256, 128, 64, 32, 16, 8):
        if R % cand == 0:
            tr = cand
            break

    def body(w_ref, g_ref, m_ref, v_ref, d_ref, nm_ref, nv_ref):
        gv = g_ref[...]
        m_new = ADAM_B1 * m_ref[...] + (1.0 - ADAM_B1) * gv
        v_new = ADAM_B2 * v_ref[...] + (1.0 - ADAM_B2) * (gv * gv)
        m_hat = m_new / (1.0 - ADAM_B1 ** ADAM_STEP)
        v_hat = v_new / (1.0 - ADAM_B2 ** ADAM_STEP)
        d_ref[...] = -ADAM_LR * (m_hat / (jnp.sqrt(v_hat) + ADAM_EPS) + ADAM_WD * w_ref[...])
        nm_ref[...] = m_new
        nv_ref[...] = v_new

    blk = pl.BlockSpec((tr, Cn), lambda i: (i, 0))
    shp = jax.ShapeDtypeStruct((R, Cn), F32)
    return pl.pallas_call(
        body, name=name, grid=(R // tr,), in_specs=[blk] * 4, out_specs=[blk] * 3, out_shape=[shp] * 3,
        compiler_params=_params("parallel"),
    )(w, g, m, v)


def _place():
    return lax.axis_index("x"), lax.axis_index("y"), lax.axis_index("c")


def _flip(place, k):
    x, y, c = place
    return (1 - x if k & 4 else x, 1 - y if k & 2 else y, 1 - c if k & 1 else c)


def _dev_index(place):
    x, y, c = place
    return 4 * x + 2 * y + c


def _chip_index(place):
    return 2 * place[0] + place[1]


def _allgather8(x, name, reduce=False):
    R, Cn = x.shape

    def body(x_ref, *rest):
        if reduce:
            out_ref, sum_ref, send_sems, recv_sems, local_sem = rest
        else:
            out_ref, send_sems, recv_sems, local_sem = rest
        me = _place()
        mine = pltpu.make_async_copy(x_ref, out_ref.at[_dev_index(me)], local_sem)
        mine.start()

        def copy(k, origin, to):
            return pltpu.make_async_remote_copy(
                src_ref=x_ref, dst_ref=out_ref.at[_dev_index(origin)], send_sem=send_sems.at[k - 1],
                recv_sem=recv_sems.at[k - 1], device_id=to, device_id_type=MESH)

        sends = [copy(k, me, _flip(me, k)) for k in range(1, 8)]
        for cp in sends:
            cp.start()
        for k in range(1, 8):
            copy(k, _flip(me, k), me).wait_recv()
        for cp in sends:
            cp.wait_send()
        mine.wait()
        if reduce:
            acc = out_ref[0]
            for i in range(1, 8):
                acc = acc + out_ref[i]
            sum_ref[...] = acc

    vm = pl.BlockSpec(memory_space=pltpu.VMEM)
    outs = [jax.ShapeDtypeStruct((8, R, Cn), F32)] + ([jax.ShapeDtypeStruct((R, Cn), F32)] if reduce else [])
    res = pl.pallas_call(
        body, name=name, in_specs=[vm], out_specs=[vm] * len(outs), out_shape=outs,
        scratch_shapes=[pltpu.SemaphoreType.DMA((7,)), pltpu.SemaphoreType.DMA((7,)), pltpu.SemaphoreType.DMA],
    )(x)
    return res if reduce else res[0]


def _weights_allgather(shards, name):
    n = len(shards)
    per = 8

    def body(*refs):
        in_refs, out_refs = refs[:n], refs[n:2 * n]
        send_sems, recv_sems = refs[2 * n:]
        me = _place()
        c = me[2]
        sibling = _flip(me, 1)
        others = [_flip(me, 2 * j) for j in (1, 2, 3)]

        def copy(a, k, src, dst, to):
            return pltpu.make_async_remote_copy(
                src_ref=src, dst_ref=dst, send_sem=send_sems.at[per * a + k], recv_sem=recv_sems.at[per * a + k],
                device_id=to, device_id_type=MESH)

        def block(a, place, half):
            return out_refs[a].at[_chip_index(place), half]

        started = []
        for a in range(n):
            sends = [copy(a, 0, in_refs[a].at[c], block(a, me, c), sibling),
                     copy(a, 7, in_refs[a].at[1 - c], block(a, me, 1 - c), sibling)]
            sends += [copy(a, 1 + j, in_refs[a].at[c], block(a, me, c), to) for j, to in enumerate(others)]
            for cp in sends:
                cp.start()
            started += sends
        for a in range(n):
            for j, other in enumerate(others):
                landed = block(a, other, c)
                copy(a, 1 + j, landed, landed, me).wait_recv()
                fwd = copy(a, 4 + j, landed, landed, sibling)
                fwd.start()
                started.append(fwd)
        for a in range(n):
            copy(a, 0, block(a, me, 1 - c), block(a, me, 1 - c), me).wait_recv()
            copy(a, 7, block(a, me, c), block(a, me, c), me).wait_recv()
            for j, other in enumerate(others):
                got = block(a, other, 1 - c)
                copy(a, 4 + j, got, got, me).wait_recv()
        for cp in started:
            cp.wait_send()

    return pl.pallas_call(
        body, name=name, in_specs=[ANY] * n, out_specs=[ANY] * n,
        out_shape=[jax.ShapeDtypeStruct((N_CHIPS,) + s.shape, s.dtype) for s in shards],
        scratch_shapes=[pltpu.SemaphoreType.DMA((per * n,)), pltpu.SemaphoreType.DMA((per * n,))],
    )(*shards)


def _remote(src, dst, send_sems, recv_sems, i, to):
    return pltpu.make_async_remote_copy(
        src_ref=src, dst_ref=dst, send_sem=send_sems.at[i], recv_sem=recv_sems.at[i], device_id=to, device_id_type=MESH)


def _symmetric_plan(copies):
    def plan(in_refs, out_refs, send_sems, recv_sems):
        sends = [_remote(src, dst, send_sems, recv_sems, i, to) for i, (src, dst, to) in enumerate(copies(in_refs, out_refs))]
        return sends, sends
    return plan


def _halves_exchange(grads):
    def copies(in_refs, out_refs):
        me = _place()
        return [(g.at[kk, 1 - me[2]], got.at[kk], _flip(me, 1)) for g, got in zip(in_refs, out_refs) for kk in range(N_CHIPS)]

    return _Exchange(grads, [jax.ShapeDtypeStruct((N_CHIPS,) + g.shape[2:], g.dtype) for g in grads],
                     N_CHIPS * len(grads), _symmetric_plan(copies))


def _chips_exchange(parts):
    def copies(in_refs, out_refs):
        me = _place()
        return [(p.at[_chip_index(_flip(me, 2 * j))], got.at[j - 1], _flip(me, 2 * j))
                for p, got in zip(in_refs, out_refs) for j in (1, 2, 3)]

    return _Exchange(parts, [jax.ShapeDtypeStruct((3,) + p.shape[1:], p.dtype) for p in parts], 3 * len(parts),
                     _symmetric_plan(copies))


def _siblings_exchange(halves):
    def copies(in_refs, out_refs):
        sibling = _flip(_place(), 1)
        return [(h, got, sibling) for h, got in zip(in_refs, out_refs)]

    return _Exchange(halves, [jax.ShapeDtypeStruct(h.shape, h.dtype) for h in halves], len(halves), _symmetric_plan(copies))


def _gather_over_ici(shards):
    def copies(in_refs, out_refs):
        me = _place()
        c = me[2]
        return [(w.at[c], out.at[_chip_index(me), c], _flip(me, 2 * j)) for w, out in zip(in_refs, out_refs) for j in (1, 2, 3)]

    def plan(in_refs, out_refs, send_sems, recv_sems):
        me = _place()
        sends = [_remote(src, dst, send_sems, recv_sems, i, to) for i, (src, dst, to) in enumerate(copies(in_refs, out_refs))]
        lands = [out.at[_chip_index(_flip(me, 2 * j)), me[2]] for out in out_refs for j in (1, 2, 3)]
        return sends, [_remote(z, z, send_sems, recv_sems, i, me) for i, z in enumerate(lands)]

    return _Exchange(shards, [jax.ShapeDtypeStruct((N_CHIPS,) + s.shape, s.dtype) for s in shards], 3 * len(shards), plan)


def _gather_over_d2d(shards, gathered):
    n = len(shards)

    def plan(in_refs, out_refs, send_sems, recv_sems):
        me = _place()
        c = me[2]
        sibling = _flip(me, 1)
        mine = _chip_index(me)
        sends, recvs = [], []
        for a, (w, out) in enumerate(zip(in_refs[:n], out_refs)):
            moves = [(w.at[c], (mine, c)), (w.at[1 - c], (mine, 1 - c))]
            moves += [(out.at[_chip_index(_flip(me, 2 * j)), c], (_chip_index(_flip(me, 2 * j)), c)) for j in (1, 2, 3)]
            for k, (src, (chip, half)) in enumerate(moves):
                sends.append(_remote(src, out.at[chip, half], send_sems, recv_sems, 5 * a + k, sibling))
            lands = [(mine, 1 - c), (mine, c)] + [(_chip_index(_flip(me, 2 * j)), 1 - c) for j in (1, 2, 3)]
            for k, (chip, half) in enumerate(lands):
                z = out.at[chip, half]
                recvs.append(_remote(z, z, send_sems, recv_sems, 5 * a + k, me))
        return sends, recvs

    return _Exchange(list(shards) + list(gathered), [jax.ShapeDtypeStruct(g.shape, g.dtype) for g in gathered], 5 * n, plan,
                     aliases={n + a: a for a in range(n)})


def _row_tile(rows):
    for cand in (256, 176, 128, 64, 32, 16, 8):
        if rows % cand == 0:
            return cand
    return rows


def _pair_sum(core, grad, theirs, name):
    N, _, R, Cn = grad.shape
    tr = _row_tile(R)

    def body(core_ref, g_ref, t_ref, o_ref, ob_ref):
        s = g_ref[...] + t_ref[...]
        o_ref[...] = s
        ob_ref[...] = s.astype(BF16)

    out = pl.BlockSpec((None, tr, Cn), lambda k, i, core_ref: (k, i, 0))
    return pl.pallas_call(
        body, name=name,
        grid_spec=pltpu.PrefetchScalarGridSpec(
            num_scalar_prefetch=1, grid=(N, R // tr),
            in_specs=[pl.BlockSpec((None, None, tr, Cn), lambda k, i, core_ref: (k, core_ref[0], i, 0)),
                      pl.BlockSpec((None, tr, Cn), lambda k, i, core_ref: (k, i, 0))],
            out_specs=[out, out]),
        out_shape=[jax.ShapeDtypeStruct((N, R, Cn), F32), jax.ShapeDtypeStruct((N, R, Cn), BF16)],
        compiler_params=_params("parallel", "parallel"),
    )(core, grad, theirs)


def _chip_sum(chip, parts, landed, name):
    _, R, Cn = parts.shape
    tr = _row_tile(R)

    def body(chip_ref, p_ref, l_ref, o_ref):
        o_ref[...] = ((p_ref[...] + l_ref[0].astype(F32)) + l_ref[1].astype(F32)) + l_ref[2].astype(F32)

    return pl.pallas_call(
        body, name=name,
        grid_spec=pltpu.PrefetchScalarGridSpec(
            num_scalar_prefetch=1, grid=(R // tr,),
            in_specs=[pl.BlockSpec((None, tr, Cn), lambda i, chip_ref: (chip_ref[0], i, 0)),
                      pl.BlockSpec((3, tr, Cn), lambda i, chip_ref: (0, i, 0))],
            out_specs=pl.BlockSpec((tr, Cn), lambda i, chip_ref: (i, 0))),
        out_shape=jax.ShapeDtypeStruct((R, Cn), F32), compiler_params=_params("parallel"),
    )(chip, parts, landed)


def _pair_sums(core, grads, theirs, tag):
    return [_pair_sum(core, g, t, f"{tag}_pair_sum_{i}") for i, (g, t) in enumerate(zip(grads, theirs))]


def _chip_sums(chip, parts, landed, tag):
    return [_chip_sum(chip, p[0], l, f"{tag}_chip_sum_{i}") for i, (p, l) in enumerate(zip(parts, landed))]


def _by_chip_rows(g):
    return g.reshape(N_CHIPS, 2, g.shape[0] // (2 * N_CHIPS), g.shape[1])


def _by_chip_cols(g):
    return g.reshape(N_CHIPS, 2, g.shape[1] // 2, g.shape[2])


def _adamw_halves(core, w, g_mine, g_theirs, m, v, name):
    R2, Cn = w.shape
    r = R2 // 2
    tr = _row_tile(r)
    nt = r // tr

    def body(core_ref, w_ref, gm_ref, gt_ref, m_ref, v_ref, g_ref, d_ref, nm_ref, nv_ref):
        gv = jnp.where(pl.program_id(0) == core_ref[0], gm_ref[...], gt_ref[...])
        g_ref[...] = gv
        m_new = ADAM_B1 * m_ref[...] + (1.0 - ADAM_B1) * gv
        v_new = ADAM_B2 * v_ref[...] + (1.0 - ADAM_B2) * (gv * gv)
        m_hat = m_new / (1.0 - ADAM_B1 ** ADAM_STEP)
        v_hat = v_new / (1.0 - ADAM_B2 ** ADAM_STEP)
        d_ref[...] = -ADAM_LR * (m_hat / (jnp.sqrt(v_hat) + ADAM_EPS) + ADAM_WD * w_ref[...])
        nm_ref[...] = m_new
        nv_ref[...] = v_new

    full = pl.BlockSpec((tr, Cn), lambda hf, i, core_ref: (hf * nt + i, 0))
    half = pl.BlockSpec((tr, Cn), lambda hf, i, core_ref: (i, 0))
    shp = jax.ShapeDtypeStruct((R2, Cn), F32)
    return pl.pallas_call(
        body, name=name,
        grid_spec=pltpu.PrefetchScalarGridSpec(
            num_scalar_prefetch=1, grid=(2, nt), in_specs=[full, half, half, full, full], out_specs=[full] * 4),
        out_shape=[shp] * 4, compiler_params=_params("parallel", "parallel"),
    )(core, w, g_mine, g_theirs, m, v)


def _pad_row(v, width):
    v = v.reshape(1, -1)
    return jnp.pad(v, ((0, 0), (0, width - v.shape[1])))


def _ffn1_forward(x, ng, shift, scale, gate, w_in4, w_out, gather, next_norm):
    h = _rmsmod_fwd(x, ng, shift, scale, "ffn1_norm")
    (zg, zu, a), partly = _ffn_in_fwd(h, w_in4, "ffn1_in", exchange=_gather_over_ici(gather))
    (x_new, f, h_next), gathered = _proj_out_fwd([a], w_out, x, gate, 0.5, "ffn1_out", next_norm=next_norm,
                                                 exchange=_gather_over_d2d(gather, partly))
    return x_new, (h, zg, zu, a, f), gathered, h_next


def _ffn_backward(df, saved, w_in4, w_out, core, chip, tag, riding=None, norm=None):
    h, zg, zu, a = saved[:4]
    rode = None
    if riding:
        (dzg, dzu), rode = _dact_bwd(df, w_out, zg, zu, f"{tag}_dact", exchange=riding)
    else:
        dzg, dzu = _dact_bwd(df, w_out, zg, zu, f"{tag}_dact")
    g_out = [_by_chip_rows(_wgrad(a, [df], df.shape[1], f"{tag}_dw_out")[0].reshape(a.shape[1], df.shape[1]))]
    (dw_in,), theirs_out = _wgrad(h, [dzg, dzu], FF_SHARD, f"{tag}_dw_in", exchange=_halves_exchange(g_out))
    g_in = [_by_chip_cols(dw_in.reshape(N_CHIPS, h.shape[1], FF_SHARD))]
    parts_out = _pair_sums(core, g_out, theirs_out, f"{tag}_out")
    dh_outs, (theirs_in, landed_out) = _ffn_in_dgrad(
        dzg, dzu, w_in4, f"{tag}_dh", norm=norm, exchange=[_halves_exchange(g_in), _chips_exchange([parts_out[0][1]])])
    parts_in = _pair_sums(core, g_in, theirs_in, f"{tag}_in")
    return dh_outs, parts_in, _chip_sums(chip, parts_out, landed_out, f"{tag}_out"), rode


def kernel(x, c, w_ada, b_ada, norm_g, w_ffn1_in, w_ffn1_out, w_ffn2_in, w_ffn2_out, w_mix_in, w_mix_out, hgrn_lb, hgrn_norm_g, qk_norm_g, attn_sink, rel_bias, loss_target, m_w_ada, m_b_ada, m_norm_g, m_w_ffn1_in, m_w_ffn1_out, m_w_ffn2_in, m_w_ffn2_out, m_w_mix_in, m_w_mix_out, m_hgrn_lb, m_hgrn_norm_g, m_qk_norm_g, m_attn_sink, m_rel_bias, v_w_ada, v_b_ada, v_norm_g, v_w_ffn1_in, v_w_ffn1_out, v_w_ffn2_in, v_w_ffn2_out, v_w_mix_in, v_w_mix_out, v_hgrn_lb, v_hgrn_norm_g, v_qk_norm_g, v_attn_sink, v_rel_bias):
    D = D_MODEL
    S = x.shape[1]
    place = (lax.axis_index("x"), lax.axis_index("y"), lax.axis_index("c"))
    me, my_chip = _dev_index(place), _chip_index(place)
    x0 = x[0]
    target = loss_target[0]

    def halves(w, tag):
        return _to_bf16(w[0], f"{tag}_to_bf16").reshape(2, w.shape[1] // 2, w.shape[2])

    gathered = _weights_allgather([halves(w_ffn1_in, "w_ffn1_in"), halves(w_ffn1_out, "w_ffn1_out")], "weights_allgather")
    w1_in = gathered[0].reshape(N_CHIPS, D, FF_SHARD)
    w1_out = gathered[1].reshape(D_FF, D)
    mix_shards = [halves(w_mix_in, "w_mix_in"), halves(w_mix_out, "w_mix_out")]
    ffn2_shards = [halves(w_ffn2_in, "w_ffn2_in"), halves(w_ffn2_out, "w_ffn2_out")]
    core_arr = jnp.reshape(place[2], (1,)).astype(jnp.int32)
    chip_arr = jnp.reshape(my_chip, (1,)).astype(jnp.int32)

    small = jnp.concatenate([_pad_row(c, D), _pad_row(norm_g, D), _pad_row(hgrn_lb, D), jnp.zeros((5, D), F32)], axis=0)
    small_all = _allgather8(small, "small_allgather")
    c_all = small_all[:, 0, :]
    by_chip = small_all[0::2]
    norm_g_full = by_chip[:, 1, :3 * 256].reshape(N_CHIPS, 3, 256).transpose(1, 0, 2).reshape(3, D)
    lb_raw = by_chip[:, 2, :2 * 2 * 128].reshape(N_CHIPS, 2, 2, 128).transpose(1, 2, 0, 3).reshape(2, 2, HG_WIDTH)
    lb = jax.nn.sigmoid(lb_raw[:, 0, :] - lb_raw[:, 1, :])
    lb_f, lb_b = lb[0:1], lb[1:2]

    c_act_all = c_all * jax.nn.sigmoid(c_all)
    n_ada = w_ada.shape[2]
    b_mine = lax.dynamic_slice_in_dim(b_ada, my_chip * n_ada, n_ada, axis=1)
    mods_part = _ada_fwd(c_act_all, w_ada[0], b_mine, "ada_fwd")
    mods_all = _allgather8(mods_part, "mods_allgather")[0::2].transpose(1, 0, 2).reshape(8, N_MOD * D)
    mods = lax.dynamic_slice_in_dim(mods_all, me, 1, axis=0)
    sh1, sc1, g1, sh2, sc2, g2, sh3, sc3, g3 = [mods[:, i * D:(i + 1) * D] for i in range(N_MOD)]

    x1, saved1, gathered, h2 = _ffn1_forward(x0, norm_g_full[0:1], sh1, sc1, g1, w1_in, w1_out, mix_shards,
                                             (norm_g_full[1:2], sh2, sc2))
    wm_in = gathered[0].reshape(N_CHIPS, D, D_IN // N_CHIPS).transpose(1, 0, 2).reshape(D, D_IN)
    wm_out = gathered[1].reshape(D, D)

    z = _matmul_nn(h2, wm_in, F32, 256, "mix_in")
    (of, st_f), partly = _hgrn_fwd(z, lb_f, 0, "hgrn_fwd_f", exchange=_gather_over_ici(ffn2_shards))
    (ob, st_b), gathered = _hgrn_fwd(z, lb_b, 1, "hgrn_fwd_b", exchange=_gather_over_d2d(ffn2_shards, partly))
    w2_in = gathered[0].reshape(N_CHIPS, D, FF_SHARD)
    w2_out = gathered[1].reshape(D_FF, D)
    o_h = _hgrn_post_fwd(of, ob, z, hgrn_norm_g, "hgrn_post")

    def to_heads(t, nh):
        return t.reshape(S, nh, ATT_HEAD_DIM).transpose(1, 0, 2)

    aq = to_heads(z[:, 5 * HG_WIDTH:5 * HG_WIDTH + ATT_WIDTH], ATT_Q_HEADS)
    ak = to_heads(z[:, 5 * HG_WIDTH + ATT_WIDTH:5 * HG_WIDTH + ATT_WIDTH + KV_WIDTH], ATT_KV_HEADS)
    av = to_heads(z[:, 5 * HG_WIDTH + ATT_WIDTH + KV_WIDTH:], ATT_KV_HEADS)
    q_g, k_g = qk_norm_g[0, 0:1], qk_norm_g[0, 1:2]
    sink_b = jnp.broadcast_to(attn_sink.reshape(ATT_Q_HEADS, 1, 1), (ATT_Q_HEADS, 1, BLOCK))
    bias = _bias_table(rel_bias, "bias_table")
    o_attn = _attn_fwd(aq, ak, av, q_g, k_g, sink_b, bias, "attn_fwd")
    o_a = o_attn.transpose(1, 0, 2).reshape(S, ATT_WIDTH).astype(BF16)
    x2, mixed, h3 = _proj_out_fwd([o_h, o_a], wm_out, x1, g2, 1.0, "mix_out", next_norm=(norm_g_full[2:3], sh3, sc3))

    zg3, zu3, a3 = _ffn_in_fwd(h3, w2_in, "ffn2_in")
    dx3, df3, dg3, sq_cols = _proj_out_loss(a3, w2_out, x2, g3, 0.5, target, "ffn2_out_loss")
    loss_mine = 0.5 * jnp.sum(sq_cols) / D

    (dx2, dsh3, dsc3, dng3, dmixed, dg2), parts2, mine2_out, _ = _ffn_backward(
        df3, (h3, zg3, zu3, a3), w2_in, w2_out, core_arr, chip_arr, "ffn2",
        norm=_NormBwd(x2, norm_g_full[2:3], sc3, dx3, below=(mixed, g2, 1.0)))

    (do_cat,) = _matmul_nt(dmixed, wm_out, ROW_TILE, "mix_out_dgrad")
    dwm_out = _wgrad_rows([o_h, o_a], dmixed, "mix_out_dw").reshape(D, D)

    do_sum, dgr, d_hnorm = _hgrn_post_bwd(do_cat, of, ob, z, hgrn_norm_g, "hgrn_post_bwd")
    (dq_f, dff, dv_f, doml_f), landed2 = _hgrn_bwd(z, lb_f, do_sum, st_f, 0, "hgrn_bwd_f",
                                                   exchange=_chips_exchange([p[1] for p in parts2]))
    mine2 = _chip_sums(chip_arr, parts2, landed2, "ffn2_in") + mine2_out
    (dhq, dfb, dhi, doml_b), theirs2 = _hgrn_bwd(z, lb_b, do_sum, st_b, 1, "hgrn_bwd_b", acc=(dq_f, dv_f),
                                                 exchange=_siblings_exchange(mine2))

    do_a = to_heads(do_cat[:, HG_WIDTH:], ATT_Q_HEADS)
    daq, dkw, dvw, ds_sum, dsink, dqg = _attn_bwd(aq, ak, av, q_g, k_g, sink_b, bias, do_a, "attn_bwd")
    dak, dav, dkg = _attn_kv_reduce(dkw, dvw, ak, k_g, "attn_kv_reduce")
    d_rel_bias = jnp.sum(_bias_grad(ds_sum, "bias_grad"), axis=-1).T

    def from_heads(t):
        return t.transpose(1, 0, 2).reshape(S, -1)

    dz = jnp.concatenate([dhq, dff, dfb, dhi, dgr, from_heads(daq), from_heads(dak), from_heads(dav)], axis=1).astype(BF16)
    dwm_in = _wgrad(h2, [dz], D_IN // 2, "mix_in_dw")[0][0]
    dwm_in = jnp.concatenate([dwm_in[0], dwm_in[1]], axis=1)
    wide = D_IN // N_CHIPS
    grads_m = [_by_chip_cols(dwm_in.reshape(D, N_CHIPS, wide).transpose(1, 0, 2)), _by_chip_rows(dwm_out)]
    (dx1, dsh2, dsc2, dng2, df1, dg1), theirs_m = _matmul_nt(
        dz, wm_in, 256, "mix_in_dgrad", exchange=_halves_exchange(grads_m),
        norm=_NormBwd(x1, norm_g_full[1:2], sc2, dx2, below=(saved1[4], g1, 0.5)))
    parts_m = _pair_sums(core_arr, grads_m, theirs_m, "mix")

    (dh1,), parts1, mine1_out, landed_m = _ffn_backward(df1, saved1, w1_in, w1_out, core_arr, chip_arr, "ffn1",
                                                        riding=_chips_exchange([p[1] for p in parts_m]))
    mine_m = _chip_sums(chip_arr, parts_m, landed_m, "mix")
    (dx0, dsh1, dsc1, dng1), landed1 = _rmsmod_bwd(dh1, _NormBwd(x0, norm_g_full[0:1], sc1, dx1), "ffn1_norm_bwd",
                                                   exchange=_chips_exchange([p[1] for p in parts1]))
    mine1 = _chip_sums(chip_arr, parts1, landed1, "ffn1_in") + mine1_out
    theirs_1m = list(_run_exchange(_siblings_exchange(mine1 + mine_m), "siblings_exchange"))
    reduced = list(zip(mine1 + mine2 + mine_m, theirs_1m[:2] + list(theirs2) + theirs_1m[2:]))

    dlb = -jnp.concatenate([doml_f, doml_b], axis=0)
    dlb_raw = dlb * lb * (1.0 - lb)
    d_hgrn_lb = jnp.stack([dlb_raw, -dlb_raw], axis=1)
    d_qk = jnp.concatenate([jnp.sum(dqg, axis=0), jnp.sum(dkg, axis=0)], axis=0)
    dmods = jnp.concatenate([dsh1, dsc1, dg1, dsh2, dsc2, dg2, dsh3, dsc3, dg3], axis=0)
    packed = jnp.concatenate(
        [dmods, dng1, dng2, dng3, d_hgrn_lb.reshape(2, D), _pad_row(d_hnorm, D), _pad_row(d_qk, D),
         _pad_row(dsink[:, 0, 0], D), _pad_row(d_rel_bias, D), _pad_row(loss_mine, D)], axis=0)
    packed = jnp.pad(packed, ((0, 24 - packed.shape[0]), (0, 0)))
    packed_all, packed_sum = _allgather8(packed, "small_grads_allgather", reduce=True)
    dmods_all = packed_all[:, 0:N_MOD, :].reshape(8, N_MOD * D)
    g_b_ada = packed_sum[0:N_MOD].reshape(1, N_MOD * D)
    g_norm_full = packed_sum[9:12]
    g_norm_g = lax.dynamic_slice_in_dim(g_norm_full, my_chip * 256, 256, axis=1).reshape(1, 3, 256)
    g_hgrn_lb = lax.dynamic_slice_in_dim(packed_sum[12:14].reshape(2, 2, HG_WIDTH), my_chip * 128, 128, axis=2)
    g_hgrn_norm_g = packed_sum[14:15, :HG_WIDTH]
    g_qk_norm_g = packed_sum[15, :2 * ATT_HEAD_DIM].reshape(1, 2, ATT_HEAD_DIM)
    g_attn_sink = packed_sum[16:17, :ATT_Q_HEADS]
    g_rel_bias = packed_sum[17, :NUM_BUCKETS * ATT_Q_HEADS].reshape(NUM_BUCKETS, ATT_Q_HEADS)
    loss = packed_sum[18, 0]

    dm_mine = lax.dynamic_slice_in_dim(dmods_all, my_chip * n_ada, n_ada, axis=1)
    g_w_ada = _ada_wgrad(c_act_all.T, dm_mine, "ada_wgrad")[None]

    def big(w, g, m, v, name):
        d, nm, nv = _adamw(w[0], g[0], m[0], v[0], name)
        return d[None], nm[None], nv[None]

    def big_halves(w, g_pair, m, v, name):
        g, d, nm, nv = _adamw_halves(core_arr, w[0], g_pair[0], g_pair[1], m[0], v[0], name)
        return g[None], (d[None], nm[None], nv[None])

    g_w1_in, u_w1_in = big_halves(w_ffn1_in, reduced[0], m_w_ffn1_in, v_w_ffn1_in, "adamw_w_ffn1_in")
    g_w1_out, u_w1_out = big_halves(w_ffn1_out, reduced[1], m_w_ffn1_out, v_w_ffn1_out, "adamw_w_ffn1_out")
    g_w2_in, u_w2_in = big_halves(w_ffn2_in, reduced[2], m_w_ffn2_in, v_w_ffn2_in, "adamw_w_ffn2_in")
    g_w2_out, u_w2_out = big_halves(w_ffn2_out, reduced[3], m_w_ffn2_out, v_w_ffn2_out, "adamw_w_ffn2_out")
    g_wm_in, u_wm_in = big_halves(w_mix_in, reduced[4], m_w_mix_in, v_w_mix_in, "adamw_w_mix_in")
    g_wm_out, u_wm_out = big_halves(w_mix_out, reduced[5], m_w_mix_out, v_w_mix_out, "adamw_w_mix_out")

    smalls = [(b_ada, g_b_ada, m_b_ada, v_b_ada), (norm_g, g_norm_g, m_norm_g, v_norm_g), (hgrn_lb, g_hgrn_lb, m_hgrn_lb, v_hgrn_lb),
              (hgrn_norm_g, g_hgrn_norm_g, m_hgrn_norm_g, v_hgrn_norm_g), (qk_norm_g, g_qk_norm_g, m_qk_norm_g, v_qk_norm_g),
              (attn_sink, g_attn_sink, m_attn_sink, v_attn_sink), (rel_bias, g_rel_bias, m_rel_bias, v_rel_bias)]
    sizes = [t[0].size for t in smalls]
    total = sum(sizes)
    rows = -(-total // 128)
    rows = -(-rows // 8) * 8

    def pack(i):
        flat = jnp.concatenate([t[i].reshape(-1) for t in smalls])
        fill = 1.0 if i == 3 else 0.0
        return jnp.pad(flat, (0, rows * 128 - total), constant_values=fill).reshape(rows, 128)

    packed_out = _adamw(pack(0), pack(1), pack(2), pack(3), "adamw_small")

    def unpack(flat2d):
        flat = flat2d.reshape(-1)
        outs, off = [], 0
        for t, n in zip(smalls, sizes):
            outs.append(flat[off:off + n].reshape(t[0].shape))
            off += n
        return outs

    d_small, m_small, v_small = [unpack(t) for t in packed_out]

    upd = {
        "w_ada": big(w_ada, g_w_ada, m_w_ada, v_w_ada, "adamw_w_ada"),
        "w_ffn1_in": u_w1_in, "w_ffn1_out": u_w1_out, "w_ffn2_in": u_w2_in, "w_ffn2_out": u_w2_out,
        "w_mix_in": u_wm_in, "w_mix_out": u_wm_out,
    }
    small_names = ["b_ada", "norm_g", "hgrn_lb", "hgrn_norm_g", "qk_norm_g", "attn_sink", "rel_bias"]
    for i, nme in enumerate(small_names):
        upd[nme] = (d_small[i], m_small[i], v_small[i])
    grads = {
        "w_ada": g_w_ada, "b_ada": g_b_ada, "norm_g": g_norm_g, "w_ffn1_in": g_w1_in, "w_ffn1_out": g_w1_out,
        "w_ffn2_in": g_w2_in, "w_ffn2_out": g_w2_out, "w_mix_in": g_wm_in, "w_mix_out": g_wm_out, "hgrn_lb": g_hgrn_lb,
        "hgrn_norm_g": g_hgrn_norm_g, "qk_norm_g": g_qk_norm_g, "attn_sink": g_attn_sink, "rel_bias": g_rel_bias,
    }
    order = ["w_ada", "b_ada", "norm_g", "w_ffn1_in", "w_ffn1_out", "w_ffn2_in", "w_ffn2_out", "w_mix_in", "w_mix_out",
             "hgrn_lb", "hgrn_norm_g", "qk_norm_g", "attn_sink", "rel_bias"]
    return (loss, dx0[None], *[grads[k] for k in order], *[upd[k][0] for k in order], *[upd[k][1] for k in order],
            *[upd[k][2] for k in order])
```

```python
import functools
import math

import numpy as np
import jax
import jax.numpy as jnp
from jax import lax
from jax.experimental import pallas as pl
from jax.experimental.pallas import tpu as pltpu

F32, BF16 = jnp.float32, jnp.bfloat16

D_MODEL = 1024
D_FF = 2816
HG_HEADS, HG_DIM = 4, 128
HG_WIDTH = HG_HEADS * HG_DIM
ATT_Q_HEADS, ATT_KV_HEADS, ATT_HEAD_DIM = 8, 2, 64
ATT_GROUP = ATT_Q_HEADS // ATT_KV_HEADS
ATT_WIDTH = ATT_Q_HEADS * ATT_HEAD_DIM
KV_WIDTH = ATT_KV_HEADS * ATT_HEAD_DIM
WINDOW, BLOCK = 128, 128
NUM_BUCKETS, MAX_DISTANCE = 32, 128
N_MOD = 9
EPS = 1e-6
D_IN = 5 * HG_WIDTH + ATT_WIDTH + 2 * KV_WIDTH
ADAM_LR, ADAM_B1, ADAM_B2, ADAM_EPS, ADAM_WD, ADAM_STEP = 0.001, 0.9, 0.999, 1e-08, 0.01, 10

N_CHIPS = 4
FF_SHARD = 2 * D_FF // N_CHIPS
NEG = -1e30

VMEM_LIMIT_BYTES = 56 << 20
ROW_TILE = 512
HG_CHUNK = 16
HG_ROWS = 256

MESH = pl.DeviceIdType.MESH
ANY = pl.BlockSpec(memory_space=pl.ANY)


def _params(*sem):
    return pltpu.CompilerParams(dimension_semantics=sem, vmem_limit_bytes=VMEM_LIMIT_BYTES)


def _resident(shape, index_map):
    return pl.BlockSpec(shape, index_map, pipeline_mode=pl.Buffered(1))


def _dot(a, b, dims, precision=None):
    return lax.dot_general(a, b, (dims, ((), ())), precision=precision, preferred_element_type=F32)


def _nn(a, b, precision=None):
    return _dot(a, b, ((1,), (0,)), precision)


def _nt(a, b):
    return _dot(a, b, ((1,), (1,)))


def _tn(a, b):
    return _dot(a, b, ((0,), (0,)))


def _sigmoid(x):
    return jax.nn.sigmoid(x)


class _Exchange:
    def __init__(self, inputs, out_shapes, n_sems, plan, aliases=None):
        self.inputs, self.out_shapes, self.n_sems, self.plan, self.aliases = list(inputs), list(out_shapes), n_sems, plan, aliases or {}

    def sem_shapes(self):
        return [pltpu.SemaphoreType.DMA((self.n_sems,)), pltpu.SemaphoreType.DMA((self.n_sems,))]

    def start(self, in_refs, out_refs, send_sems, recv_sems):
        for cp in self.plan(in_refs, out_refs, send_sems, recv_sems)[0]:
            cp.start()

    def finish(self, in_refs, out_refs, send_sems, recv_sems):
        sends, recvs = self.plan(in_refs, out_refs, send_sems, recv_sems)
        for cp in recvs:
            cp.wait_recv()
        for cp in sends:
            cp.wait_send()


def _run_exchange(ex, name):
    n_in, n_out = len(ex.inputs), len(ex.out_shapes)

    def body(*refs):
        in_refs, out_refs, (send_sems, recv_sems) = refs[:n_in], refs[n_in:n_in + n_out], refs[n_in + n_out:]
        ex.start(in_refs, out_refs, send_sems, recv_sems)
        ex.finish(in_refs, out_refs, send_sems, recv_sems)

    return pl.pallas_call(
        body, name=name, in_specs=[ANY] * n_in, out_specs=[ANY] * n_out, out_shape=ex.out_shapes,
        scratch_shapes=ex.sem_shapes(), input_output_aliases=dict(ex.aliases),
    )(*ex.inputs)


def _call(body, *, name, grid, in_specs, out_specs, out_shape, args, semantics, scratch_shapes=(), exchange=None):
    if exchange is None:
        return pl.pallas_call(
            body, name=name, grid=grid, in_specs=in_specs, out_specs=out_specs, out_shape=out_shape,
            scratch_shapes=list(scratch_shapes), compiler_params=_params(*semantics))(*args)
    exs = exchange if isinstance(exchange, (list, tuple)) else [exchange]
    n_in, n_out, n_scr = len(in_specs), len(out_specs), len(scratch_shapes)
    x_in, x_out = [len(ex.inputs) for ex in exs], [len(ex.out_shapes) for ex in exs]

    def take(refs, counts):
        groups = []
        for n in counts:
            groups.append(refs[:n])
            refs = refs[n:]
        return groups, refs

    def carrier(*refs):
        ins, refs = refs[:n_in], refs[n_in:]
        x_ins, refs = take(refs, x_in)
        outs, refs = refs[:n_out], refs[n_out:]
        x_outs, refs = take(refs, x_out)
        scr, refs = refs[:n_scr], refs[n_scr:]
        sems, _ = take(refs, [2] * len(exs))
        ids = [pl.program_id(a) for a in range(len(grid))]
        first = functools.reduce(jnp.logical_and, [i == 0 for i in ids])
        last = functools.reduce(jnp.logical_and, [i == g - 1 for i, g in zip(ids, grid)])

        @pl.when(first)
        def _():
            for ex, xi, xo, (send_sems, recv_sems) in zip(exs, x_ins, x_outs, sems):
                ex.start(xi, xo, send_sems, recv_sems)

        body(*ins, *outs, *scr)

        @pl.when(last)
        def _():
            for ex, xi, xo, (send_sems, recv_sems) in zip(exs, x_ins, x_outs, sems):
                ex.finish(xi, xo, send_sems, recv_sems)

    aliases, i0, o0 = {}, n_in, n_out
    for ex in exs:
        aliases.update({i0 + i: o0 + o for i, o in ex.aliases.items()})
        i0, o0 = i0 + len(ex.inputs), o0 + len(ex.out_shapes)
    res = pl.pallas_call(
        carrier, name=name, grid=grid, in_specs=list(in_specs) + [ANY] * sum(x_in),
        out_specs=list(out_specs) + [ANY] * sum(x_out),
        out_shape=list(out_shape) + [s for ex in exs for s in ex.out_shapes],
        scratch_shapes=list(scratch_shapes) + [s for ex in exs for s in ex.sem_shapes()],
        input_output_aliases=aliases, compiler_params=_params(*["arbitrary"] * len(grid)),
    )(*args, *[a for ex in exs for a in ex.inputs])
    x_res, _ = take(list(res[n_out:]), x_out)
    return list(res[:n_out]), (x_res if isinstance(exchange, (list, tuple)) else x_res[0])


def _rmsmod_fwd(x, g, shift, scale, name):
    S, D = x.shape
    tr = min(ROW_TILE, S)

    def body(x_ref, g_ref, sh_ref, sc_ref, h_ref):
        xv = x_ref[...]
        rstd = lax.rsqrt(jnp.mean(xv * xv, axis=-1, keepdims=True) + EPS)
        y = xv * rstd * g_ref[...]
        h_ref[...] = (y * (1.0 + sc_ref[...]) + sh_ref[...]).astype(h_ref.dtype)

    row = pl.BlockSpec((tr, D), lambda i: (i, 0))
    vec = pl.BlockSpec((1, D), lambda i: (0, 0))
    return pl.pallas_call(
        body, name=name, grid=(S // tr,), in_specs=[row, vec, vec, vec], out_specs=row,
        out_shape=jax.ShapeDtypeStruct((S, D), BF16), compiler_params=_params("parallel"),
    )(x, g, shift, scale)


class _NormBwd:
    def __init__(self, x, g, scale, dx_res, below=None):
        S, D = x.shape
        self.below, self.coef = below, (below[2] if below else None)
        self.inputs = [x, g, scale, dx_res] + ([below[0], below[1]] if below else [])
        vshape = jax.ShapeDtypeStruct((1, D), F32)
        self.out_shape = [jax.ShapeDtypeStruct((S, D), F32), vshape, vshape, vshape]
        if below:
            self.out_shape += [jax.ShapeDtypeStruct((S, D), BF16), vshape]

    def specs(self, tr, D):
        row = pl.BlockSpec((tr, D), lambda i: (i, 0))
        vec = pl.BlockSpec((1, D), lambda i: (0, 0))
        return ([row, vec, vec, row] + ([row, vec] if self.below else []),
                [row, vec, vec, vec] + ([row, vec] if self.below else []))

    def step(self, dhv, in_refs, out_refs):
        if self.below:
            x_ref, g_ref, sc_ref, dxr_ref, f_ref, gate_ref = in_refs
            dx_ref, dsh_ref, dsc_ref, dg_ref, df_ref, dgate_ref = out_refs
            sums = (dsh_ref, dsc_ref, dg_ref, dgate_ref)
        else:
            x_ref, g_ref, sc_ref, dxr_ref = in_refs
            dx_ref, dsh_ref, dsc_ref, dg_ref = out_refs
            sums = (dsh_ref, dsc_ref, dg_ref)

        @pl.when(pl.program_id(0) == 0)
        def _():
            for ref in sums:
                ref[...] = jnp.zeros_like(ref)

        xv, gv = x_ref[...], g_ref[...]
        one_sc = 1.0 + sc_ref[...]
        rstd = lax.rsqrt(jnp.mean(xv * xv, axis=-1, keepdims=True) + EPS)
        n = xv * rstd
        dsh_ref[...] += jnp.sum(dhv, axis=0, keepdims=True)
        dsc_ref[...] += jnp.sum(dhv * n, axis=0, keepdims=True) * gv
        dg_ref[...] += jnp.sum(dhv * n, axis=0, keepdims=True) * one_sc
        dn = dhv * (gv * one_sc)
        dx = dxr_ref[...] + rstd * (dn - n * jnp.mean(dn * n, axis=-1, keepdims=True))
        dx_ref[...] = dx
        if self.below:
            df_ref[...] = (self.coef * gate_ref[...] * dx).astype(df_ref.dtype)
            dgate_ref[...] += self.coef * jnp.sum(dx * f_ref[...].astype(F32), axis=0, keepdims=True)


def _rmsmod_bwd(dh, norm, name, exchange=None):
    S, D = dh.shape
    tr = min(ROW_TILE, S)
    n_in = len(norm.inputs)

    def body(dh_ref, *refs):
        norm.step(dh_ref[...], refs[:n_in], refs[n_in:])

    in_specs, out_specs = norm.specs(tr, D)
    return _call(body, name=name, grid=(S // tr,), in_specs=[pl.BlockSpec((tr, D), lambda i: (i, 0))] + in_specs,
                 out_specs=out_specs, out_shape=norm.out_shape, args=[dh] + norm.inputs, semantics=("arbitrary",),
                 exchange=exchange)


def _ffn_in_fwd(h, w4, name, exchange=None):
    S, D = h.shape
    tm = min(ROW_TILE, S)
    n = w4.shape[2]

    def body(h_ref, wg_ref, wu_ref, zg_ref, zu_ref, a_ref):
        hv = h_ref[...]
        zg = _nn(hv, wg_ref[...])
        zu = _nn(hv, wu_ref[...])
        zg_ref[...] = zg.astype(zg_ref.dtype)
        zu_ref[...] = zu.astype(zu_ref.dtype)
        a_ref[...] = (zg * _sigmoid(zg) * zu).astype(a_ref.dtype)

    out = pl.BlockSpec((tm, n), lambda j, m: (m, j))
    oshape = jax.ShapeDtypeStruct((S, 2 * n), BF16)
    return _call(
        body, name=name, grid=(2, S // tm),
        in_specs=[pl.BlockSpec((tm, D), lambda j, m: (m, 0)),
                  pl.BlockSpec((None, D, n), lambda j, m: (j, 0, 0)),
                  pl.BlockSpec((None, D, n), lambda j, m: (j + 2, 0, 0))],
        out_specs=[out, out, out], out_shape=[oshape, oshape, oshape], args=(h, w4, w4),
        semantics=("parallel", "parallel"), exchange=exchange)


def _proj_out_fwd(lhs, w, x, gate, coef, name, exchange=None, next_norm=None):
    S, D = x.shape
    tm = min(ROW_TILE, S)
    ks = [a.shape[1] for a in lhs]

    def body(*refs):
        lhs_refs, refs = refs[:len(lhs)], refs[len(lhs):]
        if next_norm:
            w_ref, x_ref, gate_ref, g_ref, sh_ref, sc_ref, xn_ref, f_ref, h_ref = refs
        else:
            w_ref, x_ref, gate_ref, xn_ref, f_ref = refs
        acc, off = None, 0
        for a_ref, k in zip(lhs_refs, ks):
            part = _nn(a_ref[...], w_ref[off:off + k, :])
            acc = part if acc is None else acc + part
            off += k
        f_ref[...] = acc.astype(f_ref.dtype)
        xn = x_ref[...] + coef * gate_ref[...] * acc
        xn_ref[...] = xn
        if next_norm:
            rstd = lax.rsqrt(jnp.mean(xn * xn, axis=-1, keepdims=True) + EPS)
            h_ref[...] = (xn * rstd * g_ref[...] * (1.0 + sc_ref[...]) + sh_ref[...]).astype(h_ref.dtype)

    row = pl.BlockSpec((tm, D), lambda m: (m, 0))
    vec = pl.BlockSpec((1, D), lambda m: (0, 0))
    extra = list(next_norm) if next_norm else []
    return _call(
        body, name=name, grid=(S // tm,),
        in_specs=[pl.BlockSpec((tm, k), lambda m: (m, 0)) for k in ks]
        + [_resident(w.shape, lambda m: (0, 0)), row, vec] + [vec] * len(extra),
        out_specs=[row, row] + ([row] if next_norm else []),
        out_shape=[jax.ShapeDtypeStruct((S, D), F32), jax.ShapeDtypeStruct((S, D), BF16)]
        + ([jax.ShapeDtypeStruct((S, D), BF16)] if next_norm else []),
        args=(*lhs, w, x, gate, *extra), semantics=("parallel",), exchange=exchange)


def _proj_out_loss(lhs, w, x, gate, coef, target, name):
    S, D = x.shape
    tm = min(ROW_TILE, S)

    def body(a_ref, w_ref, x_ref, gate_ref, t_ref, dy_ref, df_ref, dgate_ref, sq_ref):
        @pl.when(pl.program_id(0) == 0)
        def _():
            dgate_ref[...] = jnp.zeros_like(dgate_ref)
            sq_ref[...] = jnp.zeros_like(sq_ref)

        f = _nn(a_ref[...], w_ref[...])
        gate = coef * gate_ref[...]
        err = x_ref[...] + gate * f - t_ref[...]
        sq_ref[...] += jnp.sum(err * err, axis=0, keepdims=True)
        dy = err * (1.0 / D)
        dy_ref[...] = dy
        df_ref[...] = (gate * dy).astype(df_ref.dtype)
        dgate_ref[...] += coef * jnp.sum(dy * f, axis=0, keepdims=True)

    row = pl.BlockSpec((tm, D), lambda m: (m, 0))
    vec = pl.BlockSpec((1, D), lambda m: (0, 0))
    vshape = jax.ShapeDtypeStruct((1, D), F32)
    return pl.pallas_call(
        body, name=name, grid=(S // tm,),
        in_specs=[pl.BlockSpec((tm, lhs.shape[1]), lambda m: (m, 0)), _resident(w.shape, lambda m: (0, 0)), row, vec, row],
        out_specs=[row, row, vec, vec],
        out_shape=[jax.ShapeDtypeStruct((S, D), F32), jax.ShapeDtypeStruct((S, D), BF16), vshape, vshape],
        compiler_params=_params("arbitrary"),
    )(lhs, w, x, gate, target)


def _matmul_nn(a, w, out_dtype, tm, name):
    S, K = a.shape
    N = w.shape[1]
    tm = min(tm, S)

    def body(a_ref, w_ref, o_ref):
        o_ref[...] = _nn(a_ref[...], w_ref[...]).astype(o_ref.dtype)

    return pl.pallas_call(
        body, name=name, grid=(S // tm,),
        in_specs=[pl.BlockSpec((tm, K), lambda m: (m, 0)), _resident((K, N), lambda m: (0, 0))],
        out_specs=pl.BlockSpec((tm, N), lambda m: (m, 0)), out_shape=jax.ShapeDtypeStruct((S, N), out_dtype),
        compiler_params=_params("parallel"),
    )(a, w)


def _dact_bwd(df, w_out, zg, zu, name, exchange=None):
    S, D = df.shape
    tm = min(ROW_TILE, S)
    n = w_out.shape[0] // 2

    def body(df_ref, w_ref, zg_ref, zu_ref, dzg_ref, dzu_ref):
        da = _nt(df_ref[...], w_ref[...])
        zg_v, zu_v = zg_ref[...].astype(F32), zu_ref[...].astype(F32)
        s = _sigmoid(zg_v)
        dzu_ref[...] = (da * zg_v * s).astype(dzu_ref.dtype)
        dzg_ref[...] = (da * zu_v * (s * (1.0 + zg_v * (1.0 - s)))).astype(dzg_ref.dtype)

    blk = pl.BlockSpec((tm, n), lambda j, m: (m, j))
    oshape = jax.ShapeDtypeStruct((S, 2 * n), BF16)
    return _call(
        body, name=name, grid=(2, S // tm),
        in_specs=[pl.BlockSpec((tm, D), lambda j, m: (m, 0)), pl.BlockSpec((n, D), lambda j, m: (j, 0)), blk, blk],
        out_specs=[blk, blk], out_shape=[oshape, oshape], args=(df, w_out, zg, zu), semantics=("parallel", "parallel"),
        exchange=exchange)


def _ffn_in_dgrad(dzg, dzu, w4, name, exchange=None, norm=None):
    S = dzg.shape[0]
    D, n = w4.shape[1], w4.shape[2]
    tm = min(ROW_TILE, S)
    n_norm = len(norm.inputs) if norm else 0

    def body(dzg_ref, dzu_ref, w_ref, *refs):
        acc = _nt(dzg_ref[:, 0:n], w_ref[0])
        acc += _nt(dzg_ref[:, n:2 * n], w_ref[1])
        acc += _nt(dzu_ref[:, 0:n], w_ref[2])
        acc += _nt(dzu_ref[:, n:2 * n], w_ref[3])
        if norm:
            norm.step(acc, refs[:n_norm], refs[n_norm:])
        else:
            refs[0][...] = acc

    blk = pl.BlockSpec((tm, 2 * n), lambda m: (m, 0))
    in_specs, args = [blk, blk, _resident(w4.shape, lambda m: (0, 0, 0))], [dzg, dzu, w4]
    out_specs, out_shape = [pl.BlockSpec((tm, D), lambda m: (m, 0))], [jax.ShapeDtypeStruct((S, D), F32)]
    if norm:
        norm_in, out_specs = norm.specs(tm, D)
        in_specs, args, out_shape = in_specs + norm_in, args + norm.inputs, norm.out_shape
    return _call(body, name=name, grid=(S // tm,), in_specs=in_specs, out_specs=out_specs, out_shape=out_shape, args=args,
                 semantics=("arbitrary",) if norm else ("parallel",), exchange=exchange)


def _matmul_nt(a, w, tm, name, exchange=None, norm=None):
    S, K = a.shape
    N = w.shape[0]
    tm = min(tm, S)
    n_norm = len(norm.inputs) if norm else 0

    def body(a_ref, w_ref, *refs):
        acc = _nt(a_ref[...], w_ref[...])
        if norm:
            norm.step(acc, refs[:n_norm], refs[n_norm:])
        else:
            refs[0][...] = acc

    in_specs, args = [pl.BlockSpec((tm, K), lambda m: (m, 0)), _resident((N, K), lambda m: (0, 0))], [a, w]
    out_specs, out_shape = [pl.BlockSpec((tm, N), lambda m: (m, 0))], [jax.ShapeDtypeStruct((S, N), F32)]
    if norm:
        norm_in, out_specs = norm.specs(tm, N)
        in_specs, args, out_shape = in_specs + norm_in, args + norm.inputs, norm.out_shape
    return _call(body, name=name, grid=(S // tm,), in_specs=in_specs, out_specs=out_specs, out_shape=out_shape, args=args,
                 semantics=("arbitrary",) if norm else ("parallel",), exchange=exchange)


def _wgrad(a, gs, tn, name, exchange=None):
    S, Ka = a.shape
    N = gs[0].shape[1]
    ts = min(ROW_TILE, S)

    def body(a_ref, *refs):
        g_refs, o_ref = refs[:-1], refs[-1]

        @pl.when(pl.program_id(1) == 0)
        def _():
            o_ref[...] = jnp.zeros_like(o_ref)

        a_t = a_ref[...].T
        for i, g_ref in enumerate(g_refs):
            o_ref[i] += _nn(a_t, g_ref[...])

    return _call(
        body, name=name, grid=(N // tn, S // ts),
        in_specs=[pl.BlockSpec((ts, Ka), lambda j, s: (s, 0))] + [pl.BlockSpec((ts, tn), lambda j, s: (s, j))] * len(gs),
        out_specs=[pl.BlockSpec((len(gs), None, Ka, tn), lambda j, s: (0, j, 0, 0))],
        out_shape=[jax.ShapeDtypeStruct((len(gs), N // tn, Ka, tn), F32)], args=(a, *gs),
        semantics=("parallel", "arbitrary"), exchange=exchange)


def _wgrad_rows(lhs, g, name):
    S, Ka = lhs[0].shape
    N = g.shape[1]
    ts = min(ROW_TILE, S)

    def body(*refs):
        a_refs, g_ref, o_ref = refs[:-2], refs[-2], refs[-1]

        @pl.when(pl.program_id(0) == 0)
        def _():
            o_ref[...] = jnp.zeros_like(o_ref)

        gv = g_ref[...]
        for i, a_ref in enumerate(a_refs):
            o_ref[i] += _tn(a_ref[...], gv)

    return pl.pallas_call(
        body, name=name, grid=(S // ts,),
        in_specs=[pl.BlockSpec((ts, Ka), lambda s: (s, 0))] * len(lhs) + [pl.BlockSpec((ts, N), lambda s: (s, 0))],
        out_specs=pl.BlockSpec((len(lhs), Ka, N), lambda s: (0, 0, 0)),
        out_shape=jax.ShapeDtypeStruct((len(lhs), Ka, N), F32), compiler_params=_params("arbitrary"),
    )(*lhs, g)


def _hgrn_chunk_common(qr, fr, oml, tri, last):
    k = oml * _sigmoid(-fr)
    g = jnp.log1p(-k) * math.log2(math.e)
    q = qr * _sigmoid(qr)
    G = _nn(tri, g, precision=lax.Precision.HIGHEST)
    Gl = G[last:last + 1]
    return q, k, G, Gl


def _hgrn_consts(reverse):
    C = HG_CHUNK
    r = lax.broadcasted_iota(jnp.int32, (C, C), 0)
    cc = lax.broadcasted_iota(jnp.int32, (C, C), 1)
    tri = ((cc >= r) if reverse else (cc <= r)).astype(F32)
    tri_t = ((cc <= r) if reverse else (cc >= r)).astype(F32)
    rid = lax.broadcasted_iota(jnp.int32, (C, HG_WIDTH), 0)
    return tri, tri_t, rid, (0 if reverse else C - 1)


def _head_slices():
    return [slice(h * HG_DIM, (h + 1) * HG_DIM) for h in range(HG_HEADS)]


def _per_head_lane_sum(x):
    C = x.shape[0]
    return jnp.concatenate(
        [jnp.broadcast_to(jnp.sum(x[:, sl], axis=-1, keepdims=True), (C, HG_DIM)) for sl in _head_slices()], axis=1)


HG_TILE = 8


def _pair_tiles(s, reverse):
    blk, r = divmod(s, HG_TILE)
    n_tiles = HG_CHUNK // HG_TILE
    others = range(0, blk) if reverse else range(blk + 1, n_tiles)
    return [(blk, r)] + [(t, None) for t in others]


def _pair_decay(G, s, tile, r, rid8, reverse, keys=False):
    rs = slice(tile * HG_TILE, (tile + 1) * HG_TILE)
    d = (G[s:s + 1] - G[rs]) if keys else (G[rs] - G[s:s + 1])
    if r is not None:
        d = jnp.where((rid8 <= r) if reverse else (rid8 >= r), d, NEG)
    return rs, jnp.exp2(d)


def _hgrn_fwd(z, lb, direction, name, exchange=None):
    S = z.shape[0]
    C, DK, W = HG_CHUNK, HG_DIM, HG_WIDTH
    tb = min(HG_ROWS, S)
    n_t, n_c = S // tb, tb // C
    reverse = direction == 1
    tmap = (lambda i: n_t - 1 - i) if reverse else (lambda i: i)

    def body(q_ref, f_ref, v_ref, lb_ref, o_ref, st_out_ref, st_ref):
        @pl.when(pl.program_id(0) == 0)
        def _():
            st_ref[...] = jnp.zeros_like(st_ref)

        oml = 1.0 - lb_ref[...]
        tri, _, _, last = _hgrn_consts(reverse)
        rid8 = lax.broadcasted_iota(jnp.int32, (HG_TILE, W), 0)

        def chunk(ci, carry):
            cidx = (n_c - 1 - ci) if reverse else ci
            rows = pl.ds(pl.multiple_of(cidx * C, C), C)
            v = v_ref[rows, :]
            q, k, G, Gl = _hgrn_chunk_common(q_ref[rows, :], f_ref[rows, :], oml, tri, last)
            qd = (q * jnp.exp2(G)).astype(BF16)
            kd = (k * jnp.exp2(Gl - G)).astype(BF16)
            e_gl = jnp.exp2(Gl)
            v_b = v.astype(BF16)
            inter = []
            for h, sl in enumerate(_head_slices()):
                st0 = st_ref[h]
                st_out_ref[h, cidx] = st0
                inter.append(_nt(qd[:, sl], st0.astype(BF16)))
                st_ref[h] = st0 * e_gl[:, sl] + _tn(v_b[:, sl], kd[:, sl])
            o = jnp.concatenate(inter, axis=1)
            o_t = [o[t * HG_TILE:(t + 1) * HG_TILE] for t in range(C // HG_TILE)]
            for s in range(C):
                k_s, v_s = k[s:s + 1], v[s:s + 1]
                for tile, r in _pair_tiles(s, reverse):
                    rs, e_s = _pair_decay(G, s, tile, r, rid8, reverse)
                    o_t[tile] = o_t[tile] + _per_head_lane_sum(q[rs] * k_s * e_s) * v_s
            o_ref[rows, :] = jnp.concatenate(o_t, axis=0)
            return carry

        lax.fori_loop(0, n_c, chunk, 0, unroll=8)

    def sec(j):
        return pl.BlockSpec((tb, W), lambda i: (tmap(i), j))

    return _call(
        body, name=name, grid=(n_t,),
        in_specs=[sec(0), sec(1 + direction), sec(3), pl.BlockSpec((1, W), lambda i: (0, 0))],
        out_specs=[sec(0), pl.BlockSpec((HG_HEADS, n_c, DK, DK), lambda i: (0, tmap(i), 0, 0))],
        out_shape=[jax.ShapeDtypeStruct((S, W), F32), jax.ShapeDtypeStruct((HG_HEADS, S // C, DK, DK), F32)],
        scratch_shapes=[pltpu.VMEM((HG_HEADS, DK, DK), F32)], args=(z, z, z, lb), semantics=("arbitrary",),
        exchange=exchange)


def _hgrn_bwd(z, lb, do, states, direction, name, acc=None, exchange=None):
    S = z.shape[0]
    C, DK, W = HG_CHUNK, HG_DIM, HG_WIDTH
    tb = min(HG_ROWS, S)
    n_t, n_c = S // tb, tb // C
    reverse = direction == 1
    tmap = (lambda i: i) if reverse else (lambda i: n_t - 1 - i)

    def body(*refs):
        if acc:
            q_ref, f_ref, v_ref, lb_ref, do_ref, st_in_ref, dqa_ref, dva_ref, dq_ref, df_ref, dv_ref, doml_ref, dst_ref = refs
        else:
            q_ref, f_ref, v_ref, lb_ref, do_ref, st_in_ref, dq_ref, df_ref, dv_ref, doml_ref, dst_ref = refs

        @pl.when(pl.program_id(0) == 0)
        def _():
            dst_ref[...] = jnp.zeros_like(dst_ref)
            doml_ref[...] = jnp.zeros_like(doml_ref)

        oml = 1.0 - lb_ref[...]
        tri, tri_t, rid, last = _hgrn_consts(reverse)
        rid8 = lax.broadcasted_iota(jnp.int32, (HG_TILE, W), 0)

        def chunk(ci, carry):
            cidx = ci if reverse else (n_c - 1 - ci)
            rows = pl.ds(pl.multiple_of(cidx * C, C), C)
            qr, fr, v, dov = q_ref[rows, :], f_ref[rows, :], v_ref[rows, :], do_ref[rows, :]
            q, k, G, Gl = _hgrn_chunk_common(qr, fr, oml, tri, last)
            e_g, e_gl, e_kd = jnp.exp2(G), jnp.exp2(Gl), jnp.exp2(Gl - G)
            qd, kd = q * e_g, k * e_kd
            do_b, v_b, qd_b, kd_b = dov.astype(BF16), v.astype(BF16), qd.astype(BF16), kd.astype(BF16)
            dqd, dkd, dv, state_dot = [], [], [], []
            for h, sl in enumerate(_head_slices()):
                st0, dst1 = st_in_ref[h, cidx], dst_ref[h]
                dst1_b = dst1.astype(BF16)
                dqd.append(_nn(do_b[:, sl], st0.astype(BF16)))
                dkd.append(_nn(v_b[:, sl], dst1_b))
                dv.append(_nt(kd_b[:, sl], dst1_b))
                state_dot.append(jnp.sum(st0 * dst1, axis=0, keepdims=True))
                dst_ref[h] = dst1 * e_gl[:, sl] + _tn(do_b[:, sl], qd_b[:, sl])
            dqd, dkd, dv = [jnp.concatenate(t, axis=1) for t in (dqd, dkd, dv)]
            d_gl = e_gl * jnp.concatenate(state_dot, axis=1) + jnp.sum(dkd * kd, axis=0, keepdims=True)
            dq, dk = dqd * e_g, dkd * e_kd
            n_tiles = C // HG_TILE
            dq_t, dk_t, dv_t = [[x[t * HG_TILE:(t + 1) * HG_TILE] for t in range(n_tiles)] for x in (dq, dk, dv)]
            for s in range(C):
                k_s, v_s = k[s:s + 1], v[s:s + 1]
                for tile, r in _pair_tiles(s, reverse):
                    rs, e_s = _pair_decay(G, s, tile, r, rid8, reverse)
                    dq_t[tile] = dq_t[tile] + _per_head_lane_sum(dov[rs] * v_s) * e_s * k_s
            for t in range(C):
                q_t, do_t = q[t:t + 1], dov[t:t + 1]
                for tile, r in _pair_tiles(t, not reverse):
                    rs, x_t = _pair_decay(G, t, tile, r, rid8, not reverse, keys=True)
                    qx = q_t * x_t
                    dv_t[tile] = dv_t[tile] + _per_head_lane_sum(k[rs] * qx) * do_t
                    dk_t[tile] = dk_t[tile] + _per_head_lane_sum(v[rs] * do_t) * qx
            dq, dk, dv = [jnp.concatenate(x, axis=0) for x in (dq_t, dk_t, dv_t)]
            d_big_g = dq * q - dk * k + jnp.where(rid == last, d_gl, 0.0)
            dg = _nn(tri_t, d_big_g, precision=lax.Precision.HIGHEST)
            dk_all = dk - dg / (1.0 - k)
            sig_nf = _sigmoid(-fr)
            df_ref[rows, :] = -dk_all * k * (1.0 - sig_nf)
            doml_ref[...] += jnp.sum(dk_all * sig_nf, axis=0, keepdims=True)
            sq = _sigmoid(qr)
            dqr = dq * (sq * (1.0 + qr * (1.0 - sq)))
            if acc:
                dqr = dqr + dqa_ref[rows, :]
                dv = dv + dva_ref[rows, :]
            dq_ref[rows, :] = dqr
            dv_ref[rows, :] = dv
            return carry

        lax.fori_loop(0, n_c, chunk, 0, unroll=8)

    def sec(j):
        return pl.BlockSpec((tb, W), lambda i: (tmap(i), j))

    vec = pl.BlockSpec((1, W), lambda i: (0, 0))
    ins = [z, z, z, lb, do, states]
    in_specs = [sec(0), sec(1 + direction), sec(3), vec, sec(0),
                pl.BlockSpec((HG_HEADS, n_c, DK, DK), lambda i: (0, tmap(i), 0, 0))]
    if acc:
        ins += list(acc)
        in_specs += [sec(0), sec(0)]
    full = jax.ShapeDtypeStruct((S, W), F32)
    return _call(
        body, name=name, grid=(n_t,), in_specs=in_specs,
        out_specs=[sec(0), sec(0), sec(0), vec],
        out_shape=[full, full, full, jax.ShapeDtypeStruct((1, W), F32)],
        scratch_shapes=[pltpu.VMEM((HG_HEADS, DK, DK), F32)], args=ins, semantics=("arbitrary",), exchange=exchange)


def _hgrn_post_fwd(o_f, o_b, z, norm_g, name):
    S = z.shape[0]
    tr = min(ROW_TILE, S)

    def body(of_ref, ob_ref, gr_ref, ng_ref, y_ref):
        o = of_ref[...] + ob_ref[...]
        gr = gr_ref[...]
        gate = gr * _sigmoid(gr)
        ng = ng_ref[...]
        for h in range(HG_HEADS):
            sl = slice(h * HG_DIM, (h + 1) * HG_DIM)
            oh = o[:, sl]
            rstd = lax.rsqrt(jnp.mean(oh * oh, axis=-1, keepdims=True) + EPS)
            y_ref[:, sl] = (oh * rstd * ng[:, sl] * gate[:, sl]).astype(y_ref.dtype)

    row = pl.BlockSpec((tr, HG_WIDTH), lambda i: (i, 0))
    return pl.pallas_call(
        body, name=name, grid=(S // tr,),
        in_specs=[row, row, pl.BlockSpec((tr, HG_WIDTH), lambda i: (i, 4)), pl.BlockSpec((1, HG_WIDTH), lambda i: (0, 0))],
        out_specs=row, out_shape=jax.ShapeDtypeStruct((S, HG_WIDTH), BF16), compiler_params=_params("parallel"),
    )(o_f, o_b, z, norm_g)


def _hgrn_post_bwd(dy, o_f, o_b, z, norm_g, name):
    S = z.shape[0]
    tr = min(ROW_TILE, S)

    def body(dy_ref, of_ref, ob_ref, gr_ref, ng_ref, do_ref, dgr_ref, dng_ref):
        @pl.when(pl.program_id(0) == 0)
        def _():
            dng_ref[...] = jnp.zeros_like(dng_ref)

        o = of_ref[...] + ob_ref[...]
        gr, ng, dyv = gr_ref[...], ng_ref[...], dy_ref[...]
        sg = _sigmoid(gr)
        for h in range(HG_HEADS):
            sl = slice(h * HG_DIM, (h + 1) * HG_DIM)
            oh, dyh, grh, sgh, ngh = o[:, sl], dyv[:, sl], gr[:, sl], sg[:, sl], ng[:, sl]
            rstd = lax.rsqrt(jnp.mean(oh * oh, axis=-1, keepdims=True) + EPS)
            on = oh * rstd
            du = dyh * (grh * sgh)
            dgr_ref[:, sl] = dyh * (on * ngh) * (sgh * (1.0 + grh * (1.0 - sgh)))
            dng_ref[:, sl] += jnp.sum(du * on, axis=0, keepdims=True)
            don = du * ngh
            do_ref[:, sl] = rstd * (don - on * jnp.mean(don * on, axis=-1, keepdims=True))

    row = pl.BlockSpec((tr, HG_WIDTH), lambda i: (i, 0))
    vec = pl.BlockSpec((1, HG_WIDTH), lambda i: (0, 0))
    full = jax.ShapeDtypeStruct((S, HG_WIDTH), F32)
    return pl.pallas_call(
        body, name=name, grid=(S // tr,),
        in_specs=[row, row, row, pl.BlockSpec((tr, HG_WIDTH), lambda i: (i, 4)), vec],
        out_specs=[row, row, vec], out_shape=[full, full, jax.ShapeDtypeStruct((1, HG_WIDTH), F32)],
        compiler_params=_params("arbitrary"),
    )(dy, o_f, o_b, z, norm_g)


def _t5_bucket_table():
    rel = (np.arange(3 * BLOCK)[None, :] - BLOCK) - np.arange(BLOCK)[:, None]
    nb = NUM_BUCKETS // 2
    max_exact = nb // 2
    ret = (rel > 0).astype(np.int32) * nb
    n = np.abs(rel)
    ratio = np.log(np.maximum(n, 1).astype(np.float32) / np.float32(max_exact)) / np.float32(math.log(MAX_DISTANCE / max_exact))
    large = max_exact + (ratio.astype(np.float32) * np.float32(nb - max_exact)).astype(np.int32)
    large = np.minimum(large, nb - 1)
    bucket = ret + np.where(n < max_exact, n, large)
    return bucket.astype(np.int32), (n <= WINDOW)


def _bias_table(rel_bias, name):
    bucket, in_band = _t5_bucket_table()
    idx = jnp.asarray(np.where(in_band, bucket, -1))

    def body(rb_ref, idx_ref, o_ref):
        h = pl.program_id(0)
        iv = idx_ref[...]
        acc = jnp.where(iv < 0, NEG, 0.0).astype(F32)
        for b in range(NUM_BUCKETS):
            acc = acc + jnp.where(iv == b, rb_ref[b, h], 0.0)
        o_ref[...] = acc

    return pl.pallas_call(
        body, name=name, grid=(ATT_Q_HEADS,),
        in_specs=[pl.BlockSpec(memory_space=pltpu.SMEM), pl.BlockSpec((BLOCK, 3 * BLOCK), lambda h: (0, 0))],
        out_specs=pl.BlockSpec((None, BLOCK, 3 * BLOCK), lambda h: (h, 0, 0)),
        out_shape=jax.ShapeDtypeStruct((ATT_Q_HEADS, BLOCK, 3 * BLOCK), F32), compiler_params=_params("parallel"),
    )(rel_bias, idx)


def _bias_grad(ds_sum, name):
    bucket, in_band = _t5_bucket_table()
    idx = jnp.asarray(np.where(in_band, bucket, -1))

    def body(ds_ref, idx_ref, o_ref):
        iv, ds = idx_ref[...], ds_ref[...]
        for b in range(NUM_BUCKETS):
            part = jnp.sum(jnp.where(iv == b, ds, 0.0), axis=0, keepdims=True)
            o_ref[b:b + 1, :] = part[:, 0:BLOCK] + part[:, BLOCK:2 * BLOCK] + part[:, 2 * BLOCK:3 * BLOCK]

    return pl.pallas_call(
        body, name=name, grid=(ATT_Q_HEADS,),
        in_specs=[pl.BlockSpec((None, BLOCK, 3 * BLOCK), lambda h: (h, 0, 0)), pl.BlockSpec((BLOCK, 3 * BLOCK), lambda h: (0, 0))],
        out_specs=pl.BlockSpec((None, NUM_BUCKETS, BLOCK), lambda h: (h, 0, 0)),
        out_shape=jax.ShapeDtypeStruct((ATT_Q_HEADS, NUM_BUCKETS, BLOCK), F32), compiler_params=_params("parallel"),
    )(ds_sum, idx)


Q_COL = 5 * HG_WIDTH
KV_COL = Q_COL + ATT_WIDTH
GROUP_WIDTH = ATT_GROUP * ATT_HEAD_DIM


def _attn_specs(nb):
    G, dh = ATT_GROUP, ATT_HEAD_DIM
    qspec = pl.BlockSpec((BLOCK, GROUP_WIDTH), lambda j, n: (n, Q_COL // GROUP_WIDTH + j))

    def kv(shift):
        return pl.BlockSpec((BLOCK, 2 * KV_WIDTH), lambda j, n: (jnp.clip(n + shift, 0, nb - 1), KV_COL // (2 * KV_WIDTH)))

    gain = pl.BlockSpec((1, dh), lambda j, n: (0, 0))
    sink = pl.BlockSpec((G, 1, BLOCK), lambda j, n: (j, 0, 0))
    bias = pl.BlockSpec((G, BLOCK, 3 * BLOCK), lambda j, n: (j, 0, 0))
    return qspec, kv, gain, sink, bias


def _stack_heads(blk):
    dh = ATT_HEAD_DIM
    return jnp.concatenate([blk[:, g * dh:(g + 1) * dh] for g in range(ATT_GROUP)], axis=0)


def _unstack_heads(st):
    return jnp.concatenate([st[g * BLOCK:(g + 1) * BLOCK] for g in range(ATT_GROUP)], axis=1)


def _window(kv0, kv1, kv2, j):
    dh = ATT_HEAD_DIM
    cat = jnp.concatenate([kv0[...], kv1[...], kv2[...]], axis=0)
    return (jnp.where(j == 0, cat[:, 0:dh], cat[:, dh:2 * dh]),
            jnp.where(j == 0, cat[:, 2 * dh:3 * dh], cat[:, 3 * dh:4 * dh]))


def _attn_probs(qh, kn, bias_h, sink_h, edge_ok):
    s = _nt(qh.astype(BF16), kn.astype(BF16)) * (1.0 / math.sqrt(ATT_HEAD_DIM)) + bias_h
    s = jnp.where(edge_ok, s, NEG)
    m = jnp.maximum(jnp.max(s, axis=-1, keepdims=True), sink_h)
    p = jnp.exp(s - m)
    e_sink = jnp.exp(sink_h - m)
    inv = 1.0 / (jnp.sum(p, axis=-1, keepdims=True) + e_sink)
    return p * inv, e_sink * inv


def _rms_rows(x):
    rstd = lax.rsqrt(jnp.mean(x * x, axis=-1, keepdims=True) + EPS)
    return x * rstd, rstd


def _edge_ok(n, nb):
    colid = lax.broadcasted_iota(jnp.int32, (ATT_GROUP * BLOCK, 3 * BLOCK), 1)
    return jnp.logical_and(jnp.logical_or(colid >= BLOCK, n > 0), jnp.logical_or(colid < 2 * BLOCK, n < nb - 1))


def _sink_column(sink_ref):
    return jnp.concatenate([jnp.broadcast_to(sink_ref[g][:, 0:1], (BLOCK, 1)) for g in range(ATT_GROUP)], axis=0)


def _attn_fwd(z, q_g, k_g, sink, bias, name):
    S = z.shape[0]
    nb = S // BLOCK
    G, dh = ATT_GROUP, ATT_HEAD_DIM
    qspec, kv, gain, sink_spec, bias_spec = _attn_specs(nb)

    def body(q_ref, kv0, kv1, kv2, qg_ref, kg_ref, sink_ref, bias_ref, o_ref):
        j, n = pl.program_id(0), pl.program_id(1)
        kraw, v = _window(kv0, kv1, kv2, j)
        kn = _rms_rows(kraw)[0] * kg_ref[...]
        qn = _rms_rows(_stack_heads(q_ref[...]))[0] * qg_ref[...]
        p, _ = _attn_probs(qn, kn, bias_ref[...].reshape(G * BLOCK, 3 * BLOCK), _sink_column(sink_ref), _edge_ok(n, nb))
        o_ref[...] = _unstack_heads(_nn(p.astype(BF16), v.astype(BF16))).astype(o_ref.dtype)

    return pl.pallas_call(
        body, name=name, grid=(ATT_KV_HEADS, nb),
        in_specs=[qspec, kv(-1), kv(0), kv(1), gain, gain, sink_spec, bias_spec],
        out_specs=pl.BlockSpec((BLOCK, GROUP_WIDTH), lambda j, n: (n, j)),
        out_shape=jax.ShapeDtypeStruct((S, ATT_WIDTH), BF16), compiler_params=_params("parallel", "parallel"),
    )(z, z, z, z, q_g, k_g, sink, bias)


def _attn_bwd(z, q_g, k_g, sink, bias, do, name):
    S = z.shape[0]
    nb = S // BLOCK
    G, dh = ATT_GROUP, ATT_HEAD_DIM
    scale = 1.0 / math.sqrt(dh)
    qspec, kv, gain, sink_spec, bias_spec = _attn_specs(nb)

    def body(q_ref, kv0, kv1, kv2, qg_ref, kg_ref, sink_ref, bias_ref, do_ref,
             dq_ref, dkw_ref, dvw_ref, ds_ref, dsink_ref, dqg_ref):
        j, n = pl.program_id(0), pl.program_id(1)

        @pl.when(n == 0)
        def _():
            ds_ref[...] = jnp.zeros_like(ds_ref)
            dsink_ref[...] = jnp.zeros_like(dsink_ref)
            dqg_ref[...] = jnp.zeros_like(dqg_ref)

        kraw, v = _window(kv0, kv1, kv2, j)
        vcat = v.astype(BF16)
        kn = _rms_rows(kraw)[0] * kg_ref[...]
        qg = qg_ref[...]
        qhat, rstd = _rms_rows(_stack_heads(q_ref[...]))
        qn = qhat * qg
        p, p_sink = _attn_probs(qn, kn, bias_ref[...].reshape(G * BLOCK, 3 * BLOCK), _sink_column(sink_ref), _edge_ok(n, nb))
        do_b = _stack_heads(do_ref[...]).astype(BF16)
        dp = _nt(do_b, vcat)
        delta = jnp.sum(p * dp, axis=-1, keepdims=True)
        ds = p * (dp - delta)
        ds_ref[...] += ds.reshape(G, BLOCK, 3 * BLOCK)
        sink_term = p_sink * delta
        for g in range(G):
            dsink_ref[g] += jnp.zeros((1, BLOCK), F32) - jnp.sum(sink_term[g * BLOCK:(g + 1) * BLOCK], axis=0, keepdims=True)
        ds_b = ds.astype(BF16)
        dvw_ref[...] = _tn(p.astype(BF16), do_b)
        dkw_ref[...] = _tn(ds_b, qn.astype(BF16)) * scale
        dqn = _nn(ds_b, kn.astype(BF16)) * scale
        dqg_ref[...] += jnp.sum(dqn * qhat, axis=0, keepdims=True)
        dqh = dqn * qg
        dq = rstd * (dqh - qhat * jnp.mean(dqh * qhat, axis=-1, keepdims=True))
        dq_ref[...] = _unstack_heads(dq).astype(dq_ref.dtype)

    win = pl.BlockSpec((None, None, 3 * BLOCK, dh), lambda j, n: (j, n, 0, 0))
    wshape = jax.ShapeDtypeStruct((ATT_KV_HEADS, nb, 3 * BLOCK, dh), F32)
    do_spec = pl.BlockSpec((BLOCK, GROUP_WIDTH), lambda j, n: (n, HG_WIDTH // GROUP_WIDTH + j))
    return pl.pallas_call(
        body, name=name, grid=(ATT_KV_HEADS, nb),
        in_specs=[qspec, kv(-1), kv(0), kv(1), gain, gain, sink_spec, bias_spec, do_spec],
        out_specs=[pl.BlockSpec((BLOCK, GROUP_WIDTH), lambda j, n: (n, j)), win, win, bias_spec, sink_spec,
                   pl.BlockSpec((None, 1, dh), lambda j, n: (j, 0, 0))],
        out_shape=[jax.ShapeDtypeStruct((S, ATT_WIDTH), BF16), wshape, wshape,
                   jax.ShapeDtypeStruct((ATT_Q_HEADS, BLOCK, 3 * BLOCK), F32),
                   jax.ShapeDtypeStruct((ATT_Q_HEADS, 1, BLOCK), F32),
                   jax.ShapeDtypeStruct((ATT_KV_HEADS, 1, dh), F32)],
        compiler_params=_params("parallel", "arbitrary"),
    )(z, z, z, z, q_g, k_g, sink, bias, do)


def _attn_kv_reduce(dkw, dvw, z, k_g, name):
    S = z.shape[0]
    nb = S // BLOCK
    dh = ATT_HEAD_DIM
    kb = min(8, nb)
    steps = nb // kb

    def body(a_lo, a, a_hi, b_lo, b, b_hi, kv_ref, kg_ref, dkv_ref, dkg_ref):
        n = pl.program_id(0)

        @pl.when(n == 0)
        def _():
            dkg_ref[...] = jnp.zeros_like(dkg_ref)

        lo = jnp.where(n > 0, 1.0, 0.0)
        hi = jnp.where(n < steps - 1, 1.0, 0.0)

        def overlap_add(w, w_lo, w_hi, j, i):
            before = lo * w_lo[j] if i == 0 else w[j, i - 1, 2 * BLOCK:3 * BLOCK, :]
            after = hi * w_hi[j] if i == kb - 1 else w[j, i + 1, 0:BLOCK, :]
            return w[j, i, BLOCK:2 * BLOCK, :] + before + after

        dkg = [jnp.zeros((1, dh), F32) for _ in range(ATT_KV_HEADS)]
        for i in range(kb):
            rows = slice(i * BLOCK, (i + 1) * BLOCK)
            dks, dvs = [], []
            for j in range(ATT_KV_HEADS):
                dkn = overlap_add(a, a_lo, a_hi, j, i)
                dvs.append(overlap_add(b, b_lo, b_hi, j, i))
                khat, rstd = _rms_rows(kv_ref[rows, j * dh:(j + 1) * dh])
                dkg[j] = dkg[j] + jnp.sum(dkn * khat, axis=0, keepdims=True)
                dkh = dkn * kg_ref[...]
                dks.append(rstd * (dkh - khat * jnp.mean(dkh * khat, axis=-1, keepdims=True)))
            dkv_ref[rows, :] = jnp.concatenate(dks + dvs, axis=1).astype(dkv_ref.dtype)
        for j in range(ATT_KV_HEADS):
            dkg_ref[j] += dkg[j]

    main = pl.BlockSpec((ATT_KV_HEADS, kb, 3 * BLOCK, dh), lambda n: (0, n, 0, 0))
    halo_lo = pl.BlockSpec((ATT_KV_HEADS, None, BLOCK, dh), lambda n: (0, jnp.maximum(n * kb - 1, 0), 2, 0))
    halo_hi = pl.BlockSpec((ATT_KV_HEADS, None, BLOCK, dh), lambda n: (0, jnp.minimum(n * kb + kb, nb - 1), 0, 0))
    return pl.pallas_call(
        body, name=name, grid=(steps,),
        in_specs=[halo_lo, main, halo_hi, halo_lo, main, halo_hi,
                  pl.BlockSpec((kb * BLOCK, 2 * KV_WIDTH), lambda n: (n, KV_COL // (2 * KV_WIDTH))),
                  pl.BlockSpec((1, dh), lambda n: (0, 0))],
        out_specs=[pl.BlockSpec((kb * BLOCK, 2 * KV_WIDTH), lambda n: (n, 0)),
                   pl.BlockSpec((ATT_KV_HEADS, 1, dh), lambda n: (0, 0, 0))],
        out_shape=[jax.ShapeDtypeStruct((S, 2 * KV_WIDTH), BF16), jax.ShapeDtypeStruct((ATT_KV_HEADS, 1, dh), F32)],
        compiler_params=_params("arbitrary"),
    )(dkw, dkw, dkw, dvw, dvw, dvw, z, k_g)


def _ada_fwd(c_act, w, b, name):
    n = w.shape[1]

    def body(c_ref, w_ref, b_ref, o_ref):
        o_ref[...] = _nn(c_ref[...], w_ref[...], precision=lax.Precision.HIGHEST) + b_ref[...]

    tn = n // 3
    return pl.pallas_call(
        body, name=name, grid=(3,),
        in_specs=[pl.BlockSpec(c_act.shape, lambda j: (0, 0)), pl.BlockSpec((w.shape[0], tn), lambda j: (0, j)),
                  pl.BlockSpec((1, tn), lambda j: (0, j))],
        out_specs=pl.BlockSpec((c_act.shape[0], tn), lambda j: (0, j)),
        out_shape=jax.ShapeDtypeStruct((c_act.shape[0], n), F32), compiler_params=_params("parallel"),
    )(c_act, w, b)


def _ada_wgrad(c_act_t, dm, name):
    D, nbatch = c_act_t.shape
    n = dm.shape[1]
    tr = 256

    def body(c_ref, dm_ref, o_ref):
        cv, dv = c_ref[...], dm_ref[...]
        acc = cv[:, 0:1] * dv[0:1, :]
        for b in range(1, nbatch):
            acc = acc + cv[:, b:b + 1] * dv[b:b + 1, :]
        o_ref[...] = acc

    return pl.pallas_call(
        body, name=name, grid=(D // tr,),
        in_specs=[pl.BlockSpec((tr, nbatch), lambda i: (i, 0)), pl.BlockSpec((nbatch, n), lambda i: (0, 0))],
        out_specs=pl.BlockSpec((tr, n), lambda i: (i, 0)), out_shape=jax.ShapeDtypeStruct((D, n), F32),
        compiler_params=_params("parallel"),
    )(c_act_t, dm)


def _to_bf16(w, name):
    R, Cn = w.shape
    tr = _row_tile(R)

    def body(w_ref, o_ref):
        o_ref[...] = w_ref[...].astype(BF16)

    blk = pl.BlockSpec((tr, Cn), lambda i: (i, 0))
    return pl.pallas_call(
        body, name=name, grid=(R // tr,), in_specs=[blk], out_specs=blk, out_shape=jax.ShapeDtypeStruct((R, Cn), BF16),
        compiler_params=_params("parallel"),
    )(w)


def _adamw(w, g, m, v, name):
    R, Cn = w.shape
    tr = R
    for cand in (256, 128, 64, 32, 16, 8):
        if R % cand == 0:
            tr = cand
            break

    def body(w_ref, g_ref, m_ref, v_ref, d_ref, nm_ref, nv_ref):
        gv = g_ref[...]
        m_new = ADAM_B1 * m_ref[...] + (1.0 - ADAM_B1) * gv
        v_new = ADAM_B2 * v_ref[...] + (1.0 - ADAM_B2) * (gv * gv)
        m_hat = m_new / (1.0 - ADAM_B1 ** ADAM_STEP)
        v_hat = v_new / (1.0 - ADAM_B2 ** ADAM_STEP)
        d_ref[...] = -ADAM_LR * (m_hat / (jnp.sqrt(v_hat) + ADAM_EPS) + ADAM_WD * w_ref[...])
        nm_ref[...] = m_new
        nv_ref[...] = v_new

    blk = pl.BlockSpec((tr, Cn), lambda i: (i, 0))
    shp = jax.ShapeDtypeStruct((R, Cn), F32)
    return pl.pallas_call(
        body, name=name, grid=(R // tr,), in_specs=[blk] * 4, out_specs=[blk] * 3, out_shape=[shp] * 3,
        compiler_params=_params("parallel"),
    )(w, g, m, v)


def _place():
    return lax.axis_index("x"), lax.axis_index("y"), lax.axis_index("c")


def _flip(place, k):
    x, y, c = place
    return (1 - x if k & 4 else x, 1 - y if k & 2 else y, 1 - c if k & 1 else c)


def _dev_index(place):
    x, y, c = place
    return 4 * x + 2 * y + c


def _chip_index(place):
    return 2 * place[0] + place[1]


def _allgather8(x, name, reduce=False):
    R, Cn = x.shape

    def body(x_ref, *rest):
        if reduce:
            out_ref, sum_ref, send_sems, recv_sems, local_sem = rest
        else:
            out_ref, send_sems, recv_sems, local_sem = rest
        me = _place()
        mine = pltpu.make_async_copy(x_ref, out_ref.at[_dev_index(me)], local_sem)
        mine.start()

        def copy(k, origin, to):
            return pltpu.make_async_remote_copy(
                src_ref=x_ref, dst_ref=out_ref.at[_dev_index(origin)], send_sem=send_sems.at[k - 1],
                recv_sem=recv_sems.at[k - 1], device_id=to, device_id_type=MESH)

        sends = [copy(k, me, _flip(me, k)) for k in range(1, 8)]
        for cp in sends:
            cp.start()
        for k in range(1, 8):
            copy(k, _flip(me, k), me).wait_recv()
        for cp in sends:
            cp.wait_send()
        mine.wait()
        if reduce:
            acc = out_ref[0]
            for i in range(1, 8):
                acc = acc + out_ref[i]
            sum_ref[...] = acc

    vm = pl.BlockSpec(memory_space=pltpu.VMEM)
    outs = [jax.ShapeDtypeStruct((8, R, Cn), F32)] + ([jax.ShapeDtypeStruct((R, Cn), F32)] if reduce else [])
    res = pl.pallas_call(
        body, name=name, in_specs=[vm], out_specs=[vm] * len(outs), out_shape=outs,
        scratch_shapes=[pltpu.SemaphoreType.DMA((7,)), pltpu.SemaphoreType.DMA((7,)), pltpu.SemaphoreType.DMA],
    )(x)
    return res if reduce else res[0]


def _weights_allgather(shards, name):
    n = len(shards)
    per = 8

    def body(*refs):
        in_refs, out_refs = refs[:n], refs[n:2 * n]
        send_sems, recv_sems = refs[2 * n:]
        me = _place()
        c = me[2]
        sibling = _flip(me, 1)
        others = [_flip(me, 2 * j) for j in (1, 2, 3)]

        def copy(a, k, src, dst, to):
            return pltpu.make_async_remote_copy(
                src_ref=src, dst_ref=dst, send_sem=send_sems.at[per * a + k], recv_sem=recv_sems.at[per * a + k],
                device_id=to, device_id_type=MESH)

        def block(a, place, half):
            return out_refs[a].at[_chip_index(place), half]

        started = []
        for a in range(n):
            sends = [copy(a, 0, in_refs[a].at[c], block(a, me, c), sibling),
                     copy(a, 7, in_refs[a].at[1 - c], block(a, me, 1 - c), sibling)]
            sends += [copy(a, 1 + j, in_refs[a].at[c], block(a, me, c), to) for j, to in enumerate(others)]
            for cp in sends:
                cp.start()
            started += sends
        for a in range(n):
            for j, other in enumerate(others):
                landed = block(a, other, c)
                copy(a, 1 + j, landed, landed, me).wait_recv()
                fwd = copy(a, 4 + j, landed, landed, sibling)
                fwd.start()
                started.append(fwd)
        for a in range(n):
            copy(a, 0, block(a, me, 1 - c), block(a, me, 1 - c), me).wait_recv()
            copy(a, 7, block(a, me, c), block(a, me, c), me).wait_recv()
            for j, other in enumerate(others):
                got = block(a, other, 1 - c)
                copy(a, 4 + j, got, got, me).wait_recv()
        for cp in started:
            cp.wait_send()

    return pl.pallas_call(
        body, name=name, in_specs=[ANY] * n, out_specs=[ANY] * n,
        out_shape=[jax.ShapeDtypeStruct((N_CHIPS,) + s.shape, s.dtype) for s in shards],
        scratch_shapes=[pltpu.SemaphoreType.DMA((per * n,)), pltpu.SemaphoreType.DMA((per * n,))],
    )(*shards)


def _remote(src, dst, send_sems, recv_sems, i, to):
    return pltpu.make_async_remote_copy(
        src_ref=src, dst_ref=dst, send_sem=send_sems.at[i], recv_sem=recv_sems.at[i], device_id=to, device_id_type=MESH)


def _symmetric_plan(copies):
    def plan(in_refs, out_refs, send_sems, recv_sems):
        sends = [_remote(src, dst, send_sems, recv_sems, i, to) for i, (src, dst, to) in enumerate(copies(in_refs, out_refs))]
        return sends, sends
    return plan


def _halves_exchange(grads):
    def copies(in_refs, out_refs):
        me = _place()
        return [(g.at[kk, 1 - me[2]], got.at[kk], _flip(me, 1)) for g, got in zip(in_refs, out_refs) for kk in range(N_CHIPS)]

    return _Exchange(grads, [jax.ShapeDtypeStruct((N_CHIPS,) + g.shape[2:], g.dtype) for g in grads],
                     N_CHIPS * len(grads), _symmetric_plan(copies))


def _chips_exchange(parts):
    def copies(in_refs, out_refs):
        me = _place()
        return [(p.at[_chip_index(_flip(me, 2 * j))], got.at[j - 1], _flip(me, 2 * j))
                for p, got in zip(in_refs, out_refs) for j in (1, 2, 3)]

    return _Exchange(parts, [jax.ShapeDtypeStruct((3,) + p.shape[1:], p.dtype) for p in parts], 3 * len(parts),
                     _symmetric_plan(copies))


def _siblings_exchange(halves):
    def copies(in_refs, out_refs):
        sibling = _flip(_place(), 1)
        return [(h, got, sibling) for h, got in zip(in_refs, out_refs)]

    return _Exchange(halves, [jax.ShapeDtypeStruct(h.shape, h.dtype) for h in halves], len(halves), _symmetric_plan(copies))


def _gather_over_ici(shards):
    def copies(in_refs, out_refs):
        me = _place()
        c = me[2]
        return [(w.at[c], out.at[_chip_index(me), c], _flip(me, 2 * j)) for w, out in zip(in_refs, out_refs) for j in (1, 2, 3)]

    def plan(in_refs, out_refs, send_sems, recv_sems):
        me = _place()
        sends = [_remote(src, dst, send_sems, recv_sems, i, to) for i, (src, dst, to) in enumerate(copies(in_refs, out_refs))]
        lands = [out.at[_chip_index(_flip(me, 2 * j)), me[2]] for out in out_refs for j in (1, 2, 3)]
        return sends, [_remote(z, z, send_sems, recv_sems, i, me) for i, z in enumerate(lands)]

    return _Exchange(shards, [jax.ShapeDtypeStruct((N_CHIPS,) + s.shape, s.dtype) for s in shards], 3 * len(shards), plan)


def _gather_over_d2d(shards, gathered):
    n = len(shards)

    def plan(in_refs, out_refs, send_sems, recv_sems):
        me = _place()
        c = me[2]
        sibling = _flip(me, 1)
        mine = _chip_index(me)
        sends, recvs = [], []
        for a, (w, out) in enumerate(zip(in_refs[:n], out_refs)):
            moves = [(w.at[c], (mine, c)), (w.at[1 - c], (mine, 1 - c))]
            moves += [(out.at[_chip_index(_flip(me, 2 * j)), c], (_chip_index(_flip(me, 2 * j)), c)) for j in (1, 2, 3)]
            for k, (src, (chip, half)) in enumerate(moves):
                sends.append(_remote(src, out.at[chip, half], send_sems, recv_sems, 5 * a + k, sibling))
            lands = [(mine, 1 - c), (mine, c)] + [(_chip_index(_flip(me, 2 * j)), 1 - c) for j in (1, 2, 3)]
            for k, (chip, half) in enumerate(lands):
                z = out.at[chip, half]
                recvs.append(_remote(z, z, send_sems, recv_sems, 5 * a + k, me))
        return sends, recvs

    return _Exchange(list(shards) + list(gathered), [jax.ShapeDtypeStruct(g.shape, g.dtype) for g in gathered], 5 * n, plan,
                     aliases={n + a: a for a in range(n)})


def _row_tile(rows):
    for cand in (256, 176, 128, 64, 32, 16, 8):
        if rows % cand == 0:
            return cand
    return rows


def _pair_sum(core, grad, theirs, name):
    N, _, R, Cn = grad.shape
    tr = _row_tile(R)

    def body(core_ref, g_ref, t_ref, o_ref, ob_ref):
        s = g_ref[...] + t_ref[...]
        o_ref[...] = s
        ob_ref[...] = s.astype(BF16)

    out = pl.BlockSpec((None, tr, Cn), lambda k, i, core_ref: (k, i, 0))
    return pl.pallas_call(
        body, name=name,
        grid_spec=pltpu.PrefetchScalarGridSpec(
            num_scalar_prefetch=1, grid=(N, R // tr),
            in_specs=[pl.BlockSpec((None, None, tr, Cn), lambda k, i, core_ref: (k, core_ref[0], i, 0)),
                      pl.BlockSpec((None, tr, Cn), lambda k, i, core_ref: (k, i, 0))],
            out_specs=[out, out]),
        out_shape=[jax.ShapeDtypeStruct((N, R, Cn), F32), jax.ShapeDtypeStruct((N, R, Cn), BF16)],
        compiler_params=_params("parallel", "parallel"),
    )(core, grad, theirs)


def _chip_sum(chip, parts, landed, name):
    _, R, Cn = parts.shape
    tr = _row_tile(R)

    def body(chip_ref, p_ref, l_ref, o_ref):
        o_ref[...] = ((p_ref[...] + l_ref[0].astype(F32)) + l_ref[1].astype(F32)) + l_ref[2].astype(F32)

    return pl.pallas_call(
        body, name=name,
        grid_spec=pltpu.PrefetchScalarGridSpec(
            num_scalar_prefetch=1, grid=(R // tr,),
            in_specs=[pl.BlockSpec((None, tr, Cn), lambda i, chip_ref: (chip_ref[0], i, 0)),
                      pl.BlockSpec((3, tr, Cn), lambda i, chip_ref: (0, i, 0))],
            out_specs=pl.BlockSpec((tr, Cn), lambda i, chip_ref: (i, 0))),
        out_shape=jax.ShapeDtypeStruct((R, Cn), F32), compiler_params=_params("parallel"),
    )(chip, parts, landed)


def _pair_sums(core, grads, theirs, tag):
    return [_pair_sum(core, g, t, f"{tag}_pair_sum_{i}") for i, (g, t) in enumerate(zip(grads, theirs))]


def _chip_sums(chip, parts, landed, tag):
    return [_chip_sum(chip, p[0], l, f"{tag}_chip_sum_{i}") for i, (p, l) in enumerate(zip(parts, landed))]


def _by_chip_rows(g):
    return g.reshape(N_CHIPS, 2, g.shape[0] // (2 * N_CHIPS), g.shape[1])


def _by_chip_cols(g):
    return g.reshape(N_CHIPS, 2, g.shape[1] // 2, g.shape[2])


def _adamw_halves(core, w, g_mine, g_theirs, m, v, name):
    R2, Cn = w.shape
    r = R2 // 2
    tr = _row_tile(r)
    nt = r // tr

    def body(core_ref, w_ref, gm_ref, gt_ref, m_ref, v_ref, g_ref, d_ref, nm_ref, nv_ref):
        gv = jnp.where(pl.program_id(0) == core_ref[0], gm_ref[...], gt_ref[...])
        g_ref[...] = gv
        m_new = ADAM_B1 * m_ref[...] + (1.0 - ADAM_B1) * gv
        v_new = ADAM_B2 * v_ref[...] + (1.0 - ADAM_B2) * (gv * gv)
        m_hat = m_new / (1.0 - ADAM_B1 ** ADAM_STEP)
        v_hat = v_new / (1.0 - ADAM_B2 ** ADAM_STEP)
        d_ref[...] = -ADAM_LR * (m_hat / (jnp.sqrt(v_hat) + ADAM_EPS) + ADAM_WD * w_ref[...])
        nm_ref[...] = m_new
        nv_ref[...] = v_new

    full = pl.BlockSpec((tr, Cn), lambda hf, i, core_ref: (hf * nt + i, 0))
    half = pl.BlockSpec((tr, Cn), lambda hf, i, core_ref: (i, 0))
    shp = jax.ShapeDtypeStruct((R2, Cn), F32)
    return pl.pallas_call(
        body, name=name,
        grid_spec=pltpu.PrefetchScalarGridSpec(
            num_scalar_prefetch=1, grid=(2, nt), in_specs=[full, half, half, full, full], out_specs=[full] * 4),
        out_shape=[shp] * 4, compiler_params=_params("parallel", "parallel"),
    )(core, w, g_mine, g_theirs, m, v)


def _pad_row(v, width):
    v = v.reshape(1, -1)
    return jnp.pad(v, ((0, 0), (0, width - v.shape[1])))


def _ffn1_forward(x, ng, shift, scale, gate, w_in4, w_out, gather, next_norm):
    h = _rmsmod_fwd(x, ng, shift, scale, "ffn1_norm")
    (zg, zu, a), partly = _ffn_in_fwd(h, w_in4, "ffn1_in", exchange=_gather_over_ici(gather))
    (x_new, f, h_next), gathered = _proj_out_fwd([a], w_out, x, gate, 0.5, "ffn1_out", next_norm=next_norm,
                                                 exchange=_gather_over_d2d(gather, partly))
    return x_new, (h, zg, zu, a, f), gathered, h_next


def _ffn_backward(df, saved, w_in4, w_out, core, chip, tag, riding=None, norm=None):
    h, zg, zu, a = saved[:4]
    rode = None
    if riding:
        (dzg, dzu), rode = _dact_bwd(df, w_out, zg, zu, f"{tag}_dact", exchange=riding)
    else:
        dzg, dzu = _dact_bwd(df, w_out, zg, zu, f"{tag}_dact")
    g_out = [_by_chip_rows(_wgrad(a, [df], df.shape[1], f"{tag}_dw_out")[0].reshape(a.shape[1], df.shape[1]))]
    (dw_in,), theirs_out = _wgrad(h, [dzg, dzu], FF_SHARD, f"{tag}_dw_in", exchange=_halves_exchange(g_out))
    g_in = [_by_chip_cols(dw_in.reshape(N_CHIPS, h.shape[1], FF_SHARD))]
    parts_out = _pair_sums(core, g_out, theirs_out, f"{tag}_out")
    dh_outs, (theirs_in, landed_out) = _ffn_in_dgrad(
        dzg, dzu, w_in4, f"{tag}_dh", norm=norm, exchange=[_halves_exchange(g_in), _chips_exchange([parts_out[0][1]])])
    parts_in = _pair_sums(core, g_in, theirs_in, f"{tag}_in")
    return dh_outs, parts_in, _chip_sums(chip, parts_out, landed_out, f"{tag}_out"), rode


def kernel(x, c, w_ada, b_ada, norm_g, w_ffn1_in, w_ffn1_out, w_ffn2_in, w_ffn2_out, w_mix_in, w_mix_out, hgrn_lb, hgrn_norm_g, qk_norm_g, attn_sink, rel_bias, loss_target, m_w_ada, m_b_ada, m_norm_g, m_w_ffn1_in, m_w_ffn1_out, m_w_ffn2_in, m_w_ffn2_out, m_w_mix_in, m_w_mix_out, m_hgrn_lb, m_hgrn_norm_g, m_qk_norm_g, m_attn_sink, m_rel_bias, v_w_ada, v_b_ada, v_norm_g, v_w_ffn1_in, v_w_ffn1_out, v_w_ffn2_in, v_w_ffn2_out, v_w_mix_in, v_w_mix_out, v_hgrn_lb, v_hgrn_norm_g, v_qk_norm_g, v_attn_sink, v_rel_bias):
    D = D_MODEL
    S = x.shape[1]
    place = (lax.axis_index("x"), lax.axis_index("y"), lax.axis_index("c"))
    me, my_chip = _dev_index(place), _chip_index(place)
    x0 = x[0]
    target = loss_target[0]

    def halves(w, tag):
        return _to_bf16(w[0], f"{tag}_to_bf16").reshape(2, w.shape[1] // 2, w.shape[2])

    gathered = _weights_allgather([halves(w_ffn1_in, "w_ffn1_in"), halves(w_ffn1_out, "w_ffn1_out")], "weights_allgather")
    w1_in = gathered[0].reshape(N_CHIPS, D, FF_SHARD)
    w1_out = gathered[1].reshape(D_FF, D)
    mix_shards = [halves(w_mix_in, "w_mix_in"), halves(w_mix_out, "w_mix_out")]
    ffn2_shards = [halves(w_ffn2_in, "w_ffn2_in"), halves(w_ffn2_out, "w_ffn2_out")]
    core_arr = jnp.reshape(place[2], (1,)).astype(jnp.int32)
    chip_arr = jnp.reshape(my_chip, (1,)).astype(jnp.int32)

    small = jnp.concatenate([_pad_row(c, D), _pad_row(norm_g, D), _pad_row(hgrn_lb, D), jnp.zeros((5, D), F32)], axis=0)
    small_all = _allgather8(small, "small_allgather")
    c_all = small_all[:, 0, :]
    by_chip = small_all[0::2]
    norm_g_full = by_chip[:, 1, :3 * 256].reshape(N_CHIPS, 3, 256).transpose(1, 0, 2).reshape(3, D)
    lb_raw = by_chip[:, 2, :2 * 2 * 128].reshape(N_CHIPS, 2, 2, 128).transpose(1, 2, 0, 3).reshape(2, 2, HG_WIDTH)
    lb = jax.nn.sigmoid(lb_raw[:, 0, :] - lb_raw[:, 1, :])
    lb_f, lb_b = lb[0:1], lb[1:2]

    c_act_all = c_all * jax.nn.sigmoid(c_all)
    n_ada = w_ada.shape[2]
    b_mine = lax.dynamic_slice_in_dim(b_ada, my_chip * n_ada, n_ada, axis=1)
    mods_part = _ada_fwd(c_act_all, w_ada[0], b_mine, "ada_fwd")
    mods_all = _allgather8(mods_part, "mods_allgather")[0::2].transpose(1, 0, 2).reshape(8, N_MOD * D)
    mods = lax.dynamic_slice_in_dim(mods_all, me, 1, axis=0)
    sh1, sc1, g1, sh2, sc2, g2, sh3, sc3, g3 = [mods[:, i * D:(i + 1) * D] for i in range(N_MOD)]

    x1, saved1, gathered, h2 = _ffn1_forward(x0, norm_g_full[0:1], sh1, sc1, g1, w1_in, w1_out, mix_shards,
                                             (norm_g_full[1:2], sh2, sc2))
    wm_in = gathered[0].reshape(N_CHIPS, D, D_IN // N_CHIPS).transpose(1, 0, 2).reshape(D, D_IN)
    wm_out = gathered[1].reshape(D, D)

    z = _matmul_nn(h2, wm_in, F32, 256, "mix_in")
    (of, st_f), partly = _hgrn_fwd(z, lb_f, 0, "hgrn_fwd_f", exchange=_gather_over_ici(ffn2_shards))
    (ob, st_b), gathered = _hgrn_fwd(z, lb_b, 1, "hgrn_fwd_b", exchange=_gather_over_d2d(ffn2_shards, partly))
    w2_in = gathered[0].reshape(N_CHIPS, D, FF_SHARD)
    w2_out = gathered[1].reshape(D_FF, D)
    o_h = _hgrn_post_fwd(of, ob, z, hgrn_norm_g, "hgrn_post")

    q_g, k_g = qk_norm_g[0, 0:1], qk_norm_g[0, 1:2]
    sink_b = jnp.broadcast_to(attn_sink.reshape(ATT_Q_HEADS, 1, 1), (ATT_Q_HEADS, 1, BLOCK))
    bias = _bias_table(rel_bias, "bias_table")
    o_a = _attn_fwd(z, q_g, k_g, sink_b, bias, "attn_fwd")
    x2, mixed, h3 = _proj_out_fwd([o_h, o_a], wm_out, x1, g2, 1.0, "mix_out", next_norm=(norm_g_full[2:3], sh3, sc3))

    zg3, zu3, a3 = _ffn_in_fwd(h3, w2_in, "ffn2_in")
    dx3, df3, dg3, sq_cols = _proj_out_loss(a3, w2_out, x2, g3, 0.5, target, "ffn2_out_loss")
    loss_mine = 0.5 * jnp.sum(sq_cols) / D

    (dx2, dsh3, dsc3, dng3, dmixed, dg2), parts2, mine2_out, _ = _ffn_backward(
        df3, (h3, zg3, zu3, a3), w2_in, w2_out, core_arr, chip_arr, "ffn2",
        norm=_NormBwd(x2, norm_g_full[2:3], sc3, dx3, below=(mixed, g2, 1.0)))

    (do_cat,) = _matmul_nt(dmixed, wm_out, ROW_TILE, "mix_out_dgrad")
    dwm_out = _wgrad_rows([o_h, o_a], dmixed, "mix_out_dw").reshape(D, D)

    do_sum, dgr, d_hnorm = _hgrn_post_bwd(do_cat, of, ob, z, hgrn_norm_g, "hgrn_post_bwd")
    (dq_f, dff, dv_f, doml_f), landed2 = _hgrn_bwd(z, lb_f, do_sum, st_f, 0, "hgrn_bwd_f",
                                                   exchange=_chips_exchange([p[1] for p in parts2]))
    mine2 = _chip_sums(chip_arr, parts2, landed2, "ffn2_in") + mine2_out
    (dhq, dfb, dhi, doml_b), theirs2 = _hgrn_bwd(z, lb_b, do_sum, st_b, 1, "hgrn_bwd_b", acc=(dq_f, dv_f),
                                                 exchange=_siblings_exchange(mine2))

    daq, dkw, dvw, ds_sum, dsink, dqg = _attn_bwd(z, q_g, k_g, sink_b, bias, do_cat, "attn_bwd")
    dkv, dkg = _attn_kv_reduce(dkw, dvw, z, k_g, "attn_kv_reduce")
    d_rel_bias = jnp.sum(_bias_grad(ds_sum, "bias_grad"), axis=-1).T
    dz = jnp.concatenate([t.astype(BF16) for t in (dhq, dff, dfb, dhi, dgr)] + [daq, dkv], axis=1)
    dwm_in = _wgrad(h2, [dz], D_IN // 2, "mix_in_dw")[0][0]
    dwm_in = jnp.concatenate([dwm_in[0], dwm_in[1]], axis=1)
    wide = D_IN // N_CHIPS
    grads_m = [_by_chip_cols(dwm_in.reshape(D, N_CHIPS, wide).transpose(1, 0, 2)), _by_chip_rows(dwm_out)]
    (dx1, dsh2, dsc2, dng2, df1, dg1), theirs_m = _matmul_nt(
        dz, wm_in, 256, "mix_in_dgrad", exchange=_halves_exchange(grads_m),
        norm=_NormBwd(x1, norm_g_full[1:2], sc2, dx2, below=(saved1[4], g1, 0.5)))
    parts_m = _pair_sums(core_arr, grads_m, theirs_m, "mix")

    (dh1,), parts1, mine1_out, landed_m = _ffn_backward(df1, saved1, w1_in, w1_out, core_arr, chip_arr, "ffn1",
                                                        riding=_chips_exchange([p[1] for p in parts_m]))
    mine_m = _chip_sums(chip_arr, parts_m, landed_m, "mix")
    (dx0, dsh1, dsc1, dng1), landed1 = _rmsmod_bwd(dh1, _NormBwd(x0, norm_g_full[0:1], sc1, dx1), "ffn1_norm_bwd",
                                                   exchange=_chips_exchange([p[1] for p in parts1]))
    mine1 = _chip_sums(chip_arr, parts1, landed1, "ffn1_in") + mine1_out
    theirs_1m = list(_run_exchange(_siblings_exchange(mine1 + mine_m), "siblings_exchange"))
    reduced = list(zip(mine1 + mine2 + mine_m, theirs_1m[:2] + list(theirs2) + theirs_1m[2:]))

    dlb = -jnp.concatenate([doml_f, doml_b], axis=0)
    dlb_raw = dlb * lb * (1.0 - lb)
    d_hgrn_lb = jnp.stack([dlb_raw, -dlb_raw], axis=1)
    d_qk = jnp.concatenate([jnp.sum(dqg, axis=0), jnp.sum(dkg, axis=0)], axis=0)
    dmods = jnp.concatenate([dsh1, dsc1, dg1, dsh2, dsc2, dg2, dsh3, dsc3, dg3], axis=0)
    packed = jnp.concatenate(
        [dmods, dng1, dng2, dng3, d_hgrn_lb.reshape(2, D), _pad_row(d_hnorm, D), _pad_row(d_qk, D),
         _pad_row(dsink[:, 0, 0], D), _pad_row(d_rel_bias, D), _pad_row(loss_mine, D)], axis=0)
    packed = jnp.pad(packed, ((0, 24 - packed.shape[0]), (0, 0)))
    packed_all, packed_sum = _allgather8(packed, "small_grads_allgather", reduce=True)
    dmods_all = packed_all[:, 0:N_MOD, :].reshape(8, N_MOD * D)
    g_b_ada = packed_sum[0:N_MOD].reshape(1, N_MOD * D)
    g_norm_full = packed_sum[9:12]
    g_norm_g = lax.dynamic_slice_in_dim(g_norm_full, my_chip * 256, 256, axis=1).reshape(1, 3, 256)
    g_hgrn_lb = lax.dynamic_slice_in_dim(packed_sum[12:14].reshape(2, 2, HG_WIDTH), my_chip * 128, 128, axis=2)
    g_hgrn_norm_g = packed_sum[14:15, :HG_WIDTH]
    g_qk_norm_g = packed_sum[15, :2 * ATT_HEAD_DIM].reshape(1, 2, ATT_HEAD_DIM)
    g_attn_sink = packed_sum[16:17, :ATT_Q_HEADS]
    g_rel_bias = packed_sum[17, :NUM_BUCKETS * ATT_Q_HEADS].reshape(NUM_BUCKETS, ATT_Q_HEADS)
    loss = packed_sum[18, 0]

    dm_mine = lax.dynamic_slice_in_dim(dmods_all, my_chip * n_ada, n_ada, axis=1)
    g_w_ada = _ada_wgrad(c_act_all.T, dm_mine, "ada_wgrad")[None]

    def big(w, g, m, v, name):
        d, nm, nv = _adamw(w[0], g[0], m[0], v[0], name)
        return d[None], nm[None], nv[None]

    def big_halves(w, g_pair, m, v, name):
        g, d, nm, nv = _adamw_halves(core_arr, w[0], g_pair[0], g_pair[1], m[0], v[0], name)
        return g[None], (d[None], nm[None], nv[None])

    g_w1_in, u_w1_in = big_halves(w_ffn1_in, reduced[0], m_w_ffn1_in, v_w_ffn1_in, "adamw_w_ffn1_in")
    g_w1_out, u_w1_out = big_halves(w_ffn1_out, reduced[1], m_w_ffn1_out, v_w_ffn1_out, "adamw_w_ffn1_out")
    g_w2_in, u_w2_in = big_halves(w_ffn2_in, reduced[2], m_w_ffn2_in, v_w_ffn2_in, "adamw_w_ffn2_in")
    g_w2_out, u_w2_out = big_halves(w_ffn2_out, reduced[3], m_w_ffn2_out, v_w_ffn2_out, "adamw_w_ffn2_out")
    g_wm_in, u_wm_in = big_halves(w_mix_in, reduced[4], m_w_mix_in, v_w_mix_in, "adamw_w_mix_in")
    g_wm_out, u_wm_out = big_halves(w_mix_out, reduced[5], m_w_mix_out, v_w_mix_out, "adamw_w_mix_out")

    smalls = [(b_ada, g_b_ada, m_b_ada, v_b_ada), (norm_g, g_norm_g, m_norm_g, v_norm_g), (hgrn_lb, g_hgrn_lb, m_hgrn_lb, v_hgrn_lb),
              (hgrn_norm_g, g_hgrn_norm_g, m_hgrn_norm_g, v_hgrn_norm_g), (qk_norm_g, g_qk_norm_g, m_qk_norm_g, v_qk_norm_g),
              (attn_sink, g_attn_sink, m_attn_sink, v_attn_sink), (rel_bias, g_rel_bias, m_rel_bias, v_rel_bias)]
    sizes = [t[0].size for t in smalls]
    total = sum(sizes)
    rows = -(-total // 128)
    rows = -(-rows // 8) * 8

    def pack(i):
        flat = jnp.concatenate([t[i].reshape(-1) for t in smalls])
        fill = 1.0 if i == 3 else 0.0
        return jnp.pad(flat, (0, rows * 128 - total), constant_values=fill).reshape(rows, 128)

    packed_out = _adamw(pack(0), pack(1), pack(2), pack(3), "adamw_small")

    def unpack(flat2d):
        flat = flat2d.reshape(-1)
        outs, off = [], 0
        for t, n in zip(smalls, sizes):
            outs.append(flat[off:off + n].reshape(t[0].shape))
            off += n
        return outs

    d_small, m_small, v_small = [unpack(t) for t in packed_out]

    upd = {
        "w_ada": big(w_ada, g_w_ada, m_w_ada, v_w_ada, "adamw_w_ada"),
        "w_ffn1_in": u_w1_in, "w_ffn1_out": u_w1_out, "w_ffn2_in": u_w2_in, "w_ffn2_out": u_w2_out,
        "w_mix_in": u_wm_in, "w_mix_out": u_wm_out,
    }
    small_names = ["b_ada", "norm_g", "hgrn_lb", "hgrn_norm_g", "qk_norm_g", "attn_sink", "rel_bias"]
    for i, nme in enumerate(small_names):
        upd[nme] = (d_small[i], m_small[i], v_small[i])
    grads = {
        "w_ada": g_w_ada, "b_ada": g_b_ada, "norm_g": g_norm_g, "w_ffn1_in": g_w1_in, "w_ffn1_out": g_w1_out,
        "w_ffn2_in": g_w2_in, "w_ffn2_out": g_w2_out, "w_mix_in": g_wm_in, "w_mix_out": g_wm_out, "hgrn_lb": g_hgrn_lb,
        "hgrn_norm_g": g_hgrn_norm_g, "qk_norm_g": g_qk_norm_g, "attn_sink": g_attn_sink, "rel_bias": g_rel_bias,
    }
    order = ["w_ada", "b_ada", "norm_g", "w_ffn1_in", "w_ffn1_out", "w_ffn2_in", "w_ffn2_out", "w_mix_in", "w_mix_out",
             "hgrn_lb", "hgrn_norm_g", "qk_norm_g", "attn_sink", "rel_bias"]
    return (loss, dx0[None], *[grads[k] for k in order], *[upd[k][0] for k in order], *[upd[k][1] for k in order],
            *[upd[k][2] for k in order])
```

```python
import functools
import math

import numpy as np
import jax
import jax.numpy as jnp
from jax import lax
from jax.experimental import pallas as pl
from jax.experimental.pallas import tpu as pltpu

F32, BF16 = jnp.float32, jnp.bfloat16

D_MODEL = 1024
D_FF = 2816
HG_HEADS, HG_DIM = 4, 128
HG_WIDTH = HG_HEADS * HG_DIM
ATT_Q_HEADS, ATT_KV_HEADS, ATT_HEAD_DIM = 8, 2, 64
ATT_GROUP = ATT_Q_HEADS // ATT_KV_HEADS
ATT_WIDTH = ATT_Q_HEADS * ATT_HEAD_DIM
KV_WIDTH = ATT_KV_HEADS * ATT_HEAD_DIM
WINDOW, BLOCK = 128, 128
NUM_BUCKETS, MAX_DISTANCE = 32, 128
N_MOD = 9
EPS = 1e-6
D_IN = 5 * HG_WIDTH + ATT_WIDTH + 2 * KV_WIDTH
ADAM_LR, ADAM_B1, ADAM_B2, ADAM_EPS, ADAM_WD, ADAM_STEP = 0.001, 0.9, 0.999, 1e-08, 0.01, 10

N_CHIPS = 4
FF_SHARD = 2 * D_FF // N_CHIPS
NEG = -1e30

VMEM_LIMIT_BYTES = 56 << 20
ROW_TILE = 512
HG_CHUNK = 16
HG_ROWS = 256

MESH = pl.DeviceIdType.MESH
ANY = pl.BlockSpec(memory_space=pl.ANY)


def _params(*sem):
    return pltpu.CompilerParams(dimension_semantics=sem, vmem_limit_bytes=VMEM_LIMIT_BYTES)


def _resident(shape, index_map):
    return pl.BlockSpec(shape, index_map, pipeline_mode=pl.Buffered(1))


def _dot(a, b, dims, precision=None):
    return lax.dot_general(a, b, (dims, ((), ())), precision=precision, preferred_element_type=F32)


def _nn(a, b, precision=None):
    return _dot(a, b, ((1,), (0,)), precision)


def _nt(a, b):
    return _dot(a, b, ((1,), (1,)))


def _tn(a, b):
    return _dot(a, b, ((0,), (0,)))


def _sigmoid(x):
    return jax.nn.sigmoid(x)


class _Exchange:
    def __init__(self, inputs, out_shapes, n_sems, plan, aliases=None):
        self.inputs, self.out_shapes, self.n_sems, self.plan, self.aliases = list(inputs), list(out_shapes), n_sems, plan, aliases or {}

    def sem_shapes(self):
        return [pltpu.SemaphoreType.DMA((self.n_sems,)), pltpu.SemaphoreType.DMA((self.n_sems,))]

    def start(self, in_refs, out_refs, send_sems, recv_sems):
        for cp in self.plan(in_refs, out_refs, send_sems, recv_sems)[0]:
            cp.start()

    def finish(self, in_refs, out_refs, send_sems, recv_sems):
        sends, recvs = self.plan(in_refs, out_refs, send_sems, recv_sems)
        for cp in recvs:
            cp.wait_recv()
        for cp in sends:
            cp.wait_send()


def _run_exchange(ex, name):
    n_in, n_out = len(ex.inputs), len(ex.out_shapes)

    def body(*refs):
        in_refs, out_refs, (send_sems, recv_sems) = refs[:n_in], refs[n_in:n_in + n_out], refs[n_in + n_out:]
        ex.start(in_refs, out_refs, send_sems, recv_sems)
        ex.finish(in_refs, out_refs, send_sems, recv_sems)

    return pl.pallas_call(
        body, name=name, in_specs=[ANY] * n_in, out_specs=[ANY] * n_out, out_shape=ex.out_shapes,
        scratch_shapes=ex.sem_shapes(), input_output_aliases=dict(ex.aliases),
    )(*ex.inputs)


def _call(body, *, name, grid, in_specs, out_specs, out_shape, args, semantics, scratch_shapes=(), exchange=None):
    if exchange is None:
        return pl.pallas_call(
            body, name=name, grid=grid, in_specs=in_specs, out_specs=out_specs, out_shape=out_shape,
            scratch_shapes=list(scratch_shapes), compiler_params=_params(*semantics))(*args)
    exs = exchange if isinstance(exchange, (list, tuple)) else [exchange]
    n_in, n_out, n_scr = len(in_specs), len(out_specs), len(scratch_shapes)
    x_in, x_out = [len(ex.inputs) for ex in exs], [len(ex.out_shapes) for ex in exs]

    def take(refs, counts):
        groups = []
        for n in counts:
            groups.append(refs[:n])
            refs = refs[n:]
        return groups, refs

    def carrier(*refs):
        ins, refs = refs[:n_in], refs[n_in:]
        x_ins, refs = take(refs, x_in)
        outs, refs = refs[:n_out], refs[n_out:]
        x_outs, refs = take(refs, x_out)
        scr, refs = refs[:n_scr], refs[n_scr:]
        sems, _ = take(refs, [2] * len(exs))
        ids = [pl.program_id(a) for a in range(len(grid))]
        first = functools.reduce(jnp.logical_and, [i == 0 for i in ids])
        last = functools.reduce(jnp.logical_and, [i == g - 1 for i, g in zip(ids, grid)])

        @pl.when(first)
        def _():
            for ex, xi, xo, (send_sems, recv_sems) in zip(exs, x_ins, x_outs, sems):
                ex.start(xi, xo, send_sems, recv_sems)

        body(*ins, *outs, *scr)

        @pl.when(last)
        def _():
            for ex, xi, xo, (send_sems, recv_sems) in zip(exs, x_ins, x_outs, sems):
                ex.finish(xi, xo, send_sems, recv_sems)

    aliases, i0, o0 = {}, n_in, n_out
    for ex in exs:
        aliases.update({i0 + i: o0 + o for i, o in ex.aliases.items()})
        i0, o0 = i0 + len(ex.inputs), o0 + len(ex.out_shapes)
    res = pl.pallas_call(
        carrier, name=name, grid=grid, in_specs=list(in_specs) + [ANY] * sum(x_in),
        out_specs=list(out_specs) + [ANY] * sum(x_out),
        out_shape=list(out_shape) + [s for ex in exs for s in ex.out_shapes],
        scratch_shapes=list(scratch_shapes) + [s for ex in exs for s in ex.sem_shapes()],
        input_output_aliases=aliases, compiler_params=_params(*["arbitrary"] * len(grid)),
    )(*args, *[a for ex in exs for a in ex.inputs])
    x_res, _ = take(list(res[n_out:]), x_out)
    return list(res[:n_out]), (x_res if isinstance(exchange, (list, tuple)) else x_res[0])


def _rmsmod_fwd(x, g, shift, scale, name):
    S, D = x.shape
    tr = min(ROW_TILE, S)

    def body(x_ref, g_ref, sh_ref, sc_ref, h_ref):
        xv = x_ref[...]
        rstd = lax.rsqrt(jnp.mean(xv * xv, axis=-1, keepdims=True) + EPS)
        y = xv * rstd * g_ref[...]
        h_ref[...] = (y * (1.0 + sc_ref[...]) + sh_ref[...]).astype(h_ref.dtype)

    row = pl.BlockSpec((tr, D), lambda i: (i, 0))
    vec = pl.BlockSpec((1, D), lambda i: (0, 0))
    return pl.pallas_call(
        body, name=name, grid=(S // tr,), in_specs=[row, vec, vec, vec], out_specs=row,
        out_shape=jax.ShapeDtypeStruct((S, D), BF16), compiler_params=_params("parallel"),
    )(x, g, shift, scale)


class _NormBwd:
    def __init__(self, x, g, scale, dx_res, below=None):
        S, D = x.shape
        self.below, self.coef = below, (below[2] if below else None)
        self.inputs = [x, g, scale, dx_res] + ([below[0], below[1]] if below else [])
        vshape = jax.ShapeDtypeStruct((1, D), F32)
        self.out_shape = [jax.ShapeDtypeStruct((S, D), F32), vshape, vshape, vshape]
        if below:
            self.out_shape += [jax.ShapeDtypeStruct((S, D), BF16), vshape]

    def specs(self, tr, D):
        row = pl.BlockSpec((tr, D), lambda i: (i, 0))
        vec = pl.BlockSpec((1, D), lambda i: (0, 0))
        return ([row, vec, vec, row] + ([row, vec] if self.below else []),
                [row, vec, vec, vec] + ([row, vec] if self.below else []))

    def step(self, dhv, in_refs, out_refs):
        if self.below:
            x_ref, g_ref, sc_ref, dxr_ref, f_ref, gate_ref = in_refs
            dx_ref, dsh_ref, dsc_ref, dg_ref, df_ref, dgate_ref = out_refs
            sums = (dsh_ref, dsc_ref, dg_ref, dgate_ref)
        else:
            x_ref, g_ref, sc_ref, dxr_ref = in_refs
            dx_ref, dsh_ref, dsc_ref, dg_ref = out_refs
            sums = (dsh_ref, dsc_ref, dg_ref)

        @pl.when(pl.program_id(0) == 0)
        def _():
            for ref in sums:
                ref[...] = jnp.zeros_like(ref)

        xv, gv = x_ref[...], g_ref[...]
        one_sc = 1.0 + sc_ref[...]
        rstd = lax.rsqrt(jnp.mean(xv * xv, axis=-1, keepdims=True) + EPS)
        n = xv * rstd
        dsh_ref[...] += jnp.sum(dhv, axis=0, keepdims=True)
        dsc_ref[...] += jnp.sum(dhv * n, axis=0, keepdims=True) * gv
        dg_ref[...] += jnp.sum(dhv * n, axis=0, keepdims=True) * one_sc
        dn = dhv * (gv * one_sc)
        dx = dxr_ref[...] + rstd * (dn - n * jnp.mean(dn * n, axis=-1, keepdims=True))
        dx_ref[...] = dx
        if self.below:
            df_ref[...] = (self.coef * gate_ref[...] * dx).astype(df_ref.dtype)
            dgate_ref[...] += self.coef * jnp.sum(dx * f_ref[...].astype(F32), axis=0, keepdims=True)


def _rmsmod_bwd(dh, norm, name, exchange=None):
    S, D = dh.shape
    tr = min(ROW_TILE, S)
    n_in = len(norm.inputs)

    def body(dh_ref, *refs):
        norm.step(dh_ref[...], refs[:n_in], refs[n_in:])

    in_specs, out_specs = norm.specs(tr, D)
    return _call(body, name=name, grid=(S // tr,), in_specs=[pl.BlockSpec((tr, D), lambda i: (i, 0))] + in_specs,
                 out_specs=out_specs, out_shape=norm.out_shape, args=[dh] + norm.inputs, semantics=("arbitrary",),
                 exchange=exchange)


def _ffn_in_fwd(h, w4, name, exchange=None):
    S, D = h.shape
    tm = min(ROW_TILE, S)
    n = w4.shape[2]

    def body(h_ref, wg_ref, wu_ref, zg_ref, zu_ref, a_ref):
        hv = h_ref[...]
        zg = _nn(hv, wg_ref[...])
        zu = _nn(hv, wu_ref[...])
        zg_ref[...] = zg.astype(zg_ref.dtype)
        zu_ref[...] = zu.astype(zu_ref.dtype)
        a_ref[...] = (zg * _sigmoid(zg) * zu).astype(a_ref.dtype)

    out = pl.BlockSpec((tm, n), lambda j, m: (m, j))
    oshape = jax.ShapeDtypeStruct((S, 2 * n), BF16)
    return _call(
        body, name=name, grid=(2, S // tm),
        in_specs=[pl.BlockSpec((tm, D), lambda j, m: (m, 0)),
                  pl.BlockSpec((None, D, n), lambda j, m: (j, 0, 0)),
                  pl.BlockSpec((None, D, n), lambda j, m: (j + 2, 0, 0))],
        out_specs=[out, out, out], out_shape=[oshape, oshape, oshape], args=(h, w4, w4),
        semantics=("parallel", "parallel"), exchange=exchange)


def _proj_out_fwd(lhs, w, x, gate, coef, name, exchange=None, next_norm=None):
    S, D = x.shape
    tm = min(ROW_TILE, S)
    ks = [a.shape[1] for a in lhs]

    def body(*refs):
        lhs_refs, refs = refs[:len(lhs)], refs[len(lhs):]
        if next_norm:
            w_ref, x_ref, gate_ref, g_ref, sh_ref, sc_ref, xn_ref, f_ref, h_ref = refs
        else:
            w_ref, x_ref, gate_ref, xn_ref, f_ref = refs
        acc, off = None, 0
        for a_ref, k in zip(lhs_refs, ks):
            part = _nn(a_ref[...], w_ref[off:off + k, :])
            acc = part if acc is None else acc + part
            off += k
        f_ref[...] = acc.astype(f_ref.dtype)
        xn = x_ref[...] + coef * gate_ref[...] * acc
        xn_ref[...] = xn
        if next_norm:
            rstd = lax.rsqrt(jnp.mean(xn * xn, axis=-1, keepdims=True) + EPS)
            h_ref[...] = (xn * rstd * g_ref[...] * (1.0 + sc_ref[...]) + sh_ref[...]).astype(h_ref.dtype)

    row = pl.BlockSpec((tm, D), lambda m: (m, 0))
    vec = pl.BlockSpec((1, D), lambda m: (0, 0))
    extra = list(next_norm) if next_norm else []
    return _call(
        body, name=name, grid=(S // tm,),
        in_specs=[pl.BlockSpec((tm, k), lambda m: (m, 0)) for k in ks]
        + [_resident(w.shape, lambda m: (0, 0)), row, vec] + [vec] * len(extra),
        out_specs=[row, row] + ([row] if next_norm else []),
        out_shape=[jax.ShapeDtypeStruct((S, D), F32), jax.ShapeDtypeStruct((S, D), BF16)]
        + ([jax.ShapeDtypeStruct((S, D), BF16)] if next_norm else []),
        args=(*lhs, w, x, gate, *extra), semantics=("parallel",), exchange=exchange)


def _proj_out_loss(lhs, w, x, gate, coef, target, name):
    S, D = x.shape
    tm = min(ROW_TILE, S)

    def body(a_ref, w_ref, x_ref, gate_ref, t_ref, dy_ref, df_ref, dgate_ref, sq_ref):
        @pl.when(pl.program_id(0) == 0)
        def _():
            dgate_ref[...] = jnp.zeros_like(dgate_ref)
            sq_ref[...] = jnp.zeros_like(sq_ref)

        f = _nn(a_ref[...], w_ref[...])
        gate = coef * gate_ref[...]
        err = x_ref[...] + gate * f - t_ref[...]
        sq_ref[...] += jnp.sum(err * err, axis=0, keepdims=True)
        dy = err * (1.0 / D)
        dy_ref[...] = dy
        df_ref[...] = (gate * dy).astype(df_ref.dtype)
        dgate_ref[...] += coef * jnp.sum(dy * f, axis=0, keepdims=True)

    row = pl.BlockSpec((tm, D), lambda m: (m, 0))
    vec = pl.BlockSpec((1, D), lambda m: (0, 0))
    vshape = jax.ShapeDtypeStruct((1, D), F32)
    return pl.pallas_call(
        body, name=name, grid=(S // tm,),
        in_specs=[pl.BlockSpec((tm, lhs.shape[1]), lambda m: (m, 0)), _resident(w.shape, lambda m: (0, 0)), row, vec, row],
        out_specs=[row, row, vec, vec],
        out_shape=[jax.ShapeDtypeStruct((S, D), F32), jax.ShapeDtypeStruct((S, D), BF16), vshape, vshape],
        compiler_params=_params("arbitrary"),
    )(lhs, w, x, gate, target)


def _matmul_nn(a, w, out_dtype, tm, name):
    S, K = a.shape
    N = w.shape[1]
    tm = min(tm, S)

    def body(a_ref, w_ref, o_ref):
        o_ref[...] = _nn(a_ref[...], w_ref[...]).astype(o_ref.dtype)

    return pl.pallas_call(
        body, name=name, grid=(S // tm,),
        in_specs=[pl.BlockSpec((tm, K), lambda m: (m, 0)), _resident((K, N), lambda m: (0, 0))],
        out_specs=pl.BlockSpec((tm, N), lambda m: (m, 0)), out_shape=jax.ShapeDtypeStruct((S, N), out_dtype),
        compiler_params=_params("parallel"),
    )(a, w)


def _dact_bwd(df, w_out, zg, zu, name, exchange=None):
    S, D = df.shape
    tm = min(ROW_TILE, S)
    n = w_out.shape[0] // 2

    def body(df_ref, w_ref, zg_ref, zu_ref, dzg_ref, dzu_ref):
        da = _nt(df_ref[...], w_ref[...])
        zg_v, zu_v = zg_ref[...].astype(F32), zu_ref[...].astype(F32)
        s = _sigmoid(zg_v)
        dzu_ref[...] = (da * zg_v * s).astype(dzu_ref.dtype)
        dzg_ref[...] = (da * zu_v * (s * (1.0 + zg_v * (1.0 - s)))).astype(dzg_ref.dtype)

    blk = pl.BlockSpec((tm, n), lambda j, m: (m, j))
    oshape = jax.ShapeDtypeStruct((S, 2 * n), BF16)
    return _call(
        body, name=name, grid=(2, S // tm),
        in_specs=[pl.BlockSpec((tm, D), lambda j, m: (m, 0)), pl.BlockSpec((n, D), lambda j, m: (j, 0)), blk, blk],
        out_specs=[blk, blk], out_shape=[oshape, oshape], args=(df, w_out, zg, zu), semantics=("parallel", "parallel"),
        exchange=exchange)


def _ffn_in_dgrad(dzg, dzu, w4, name, exchange=None, norm=None):
    S = dzg.shape[0]
    D, n = w4.shape[1], w4.shape[2]
    tm = min(ROW_TILE, S)
    n_norm = len(norm.inputs) if norm else 0

    def body(dzg_ref, dzu_ref, w_ref, *refs):
        acc = _nt(dzg_ref[:, 0:n], w_ref[0])
        acc += _nt(dzg_ref[:, n:2 * n], w_ref[1])
        acc += _nt(dzu_ref[:, 0:n], w_ref[2])
        acc += _nt(dzu_ref[:, n:2 * n], w_ref[3])
        if norm:
            norm.step(acc, refs[:n_norm], refs[n_norm:])
        else:
            refs[0][...] = acc

    blk = pl.BlockSpec((tm, 2 * n), lambda m: (m, 0))
    in_specs, args = [blk, blk, _resident(w4.shape, lambda m: (0, 0, 0))], [dzg, dzu, w4]
    out_specs, out_shape = [pl.BlockSpec((tm, D), lambda m: (m, 0))], [jax.ShapeDtypeStruct((S, D), F32)]
    if norm:
        norm_in, out_specs = norm.specs(tm, D)
        in_specs, args, out_shape = in_specs + norm_in, args + norm.inputs, norm.out_shape
    return _call(body, name=name, grid=(S // tm,), in_specs=in_specs, out_specs=out_specs, out_shape=out_shape, args=args,
                 semantics=("arbitrary",) if norm else ("parallel",), exchange=exchange)


def _matmul_nt(a, w, tm, name, exchange=None, norm=None):
    S, K = a.shape
    N = w.shape[0]
    tm = min(tm, S)
    n_norm = len(norm.inputs) if norm else 0

    def body(a_ref, w_ref, *refs):
        acc = _nt(a_ref[...], w_ref[...])
        if norm:
            norm.step(acc, refs[:n_norm], refs[n_norm:])
        else:
            refs[0][...] = acc

    in_specs, args = [pl.BlockSpec((tm, K), lambda m: (m, 0)), _resident((N, K), lambda m: (0, 0))], [a, w]
    out_specs, out_shape = [pl.BlockSpec((tm, N), lambda m: (m, 0))], [jax.ShapeDtypeStruct((S, N), F32)]
    if norm:
        norm_in, out_specs = norm.specs(tm, N)
        in_specs, args, out_shape = in_specs + norm_in, args + norm.inputs, norm.out_shape
    return _call(body, name=name, grid=(S // tm,), in_specs=in_specs, out_specs=out_specs, out_shape=out_shape, args=args,
                 semantics=("arbitrary",) if norm else ("parallel",), exchange=exchange)


def _wgrad(a, gs, tn, name, exchange=None):
    S, Ka = a.shape
    N = gs[0].shape[1]
    ts = min(ROW_TILE, S)

    def body(a_ref, *refs):
        g_refs, o_ref = refs[:-1], refs[-1]

        @pl.when(pl.program_id(1) == 0)
        def _():
            o_ref[...] = jnp.zeros_like(o_ref)

        a_t = a_ref[...].T
        for i, g_ref in enumerate(g_refs):
            o_ref[i] += _nn(a_t, g_ref[...])

    return _call(
        body, name=name, grid=(N // tn, S // ts),
        in_specs=[pl.BlockSpec((ts, Ka), lambda j, s: (s, 0))] + [pl.BlockSpec((ts, tn), lambda j, s: (s, j))] * len(gs),
        out_specs=[pl.BlockSpec((len(gs), None, Ka, tn), lambda j, s: (0, j, 0, 0))],
        out_shape=[jax.ShapeDtypeStruct((len(gs), N // tn, Ka, tn), F32)], args=(a, *gs),
        semantics=("parallel", "arbitrary"), exchange=exchange)


def _wgrad_rows(lhs, g, name):
    S, Ka = lhs[0].shape
    N = g.shape[1]
    ts = min(ROW_TILE, S)

    def body(*refs):
        a_refs, g_ref, o_ref = refs[:-2], refs[-2], refs[-1]

        @pl.when(pl.program_id(0) == 0)
        def _():
            o_ref[...] = jnp.zeros_like(o_ref)

        gv = g_ref[...]
        for i, a_ref in enumerate(a_refs):
            o_ref[i] += _tn(a_ref[...], gv)

    return pl.pallas_call(
        body, name=name, grid=(S // ts,),
        in_specs=[pl.BlockSpec((ts, Ka), lambda s: (s, 0))] * len(lhs) + [pl.BlockSpec((ts, N), lambda s: (s, 0))],
        out_specs=pl.BlockSpec((len(lhs), Ka, N), lambda s: (0, 0, 0)),
        out_shape=jax.ShapeDtypeStruct((len(lhs), Ka, N), F32), compiler_params=_params("arbitrary"),
    )(*lhs, g)


def _hgrn_chunk_common(qr, fr, oml, tri, last):
    k = oml * _sigmoid(-fr)
    g = jnp.log1p(-k) * math.log2(math.e)
    q = qr * _sigmoid(qr)
    G = _nn(tri, g, precision=lax.Precision.HIGHEST)
    Gl = G[last:last + 1]
    return q, k, G, Gl


def _hgrn_consts(reverse):
    C = HG_CHUNK
    r = lax.broadcasted_iota(jnp.int32, (C, C), 0)
    cc = lax.broadcasted_iota(jnp.int32, (C, C), 1)
    tri = ((cc >= r) if reverse else (cc <= r)).astype(F32)
    tri_t = ((cc <= r) if reverse else (cc >= r)).astype(F32)
    rid = lax.broadcasted_iota(jnp.int32, (C, HG_WIDTH), 0)
    return tri, tri_t, rid, (0 if reverse else C - 1)


def _head_slices():
    return [slice(h * HG_DIM, (h + 1) * HG_DIM) for h in range(HG_HEADS)]


def _per_head_lane_sum(x):
    C = x.shape[0]
    return jnp.concatenate(
        [jnp.broadcast_to(jnp.sum(x[:, sl], axis=-1, keepdims=True), (C, HG_DIM)) for sl in _head_slices()], axis=1)


HG_TILE = 8


def _pair_tiles(s, reverse):
    blk, r = divmod(s, HG_TILE)
    n_tiles = HG_CHUNK // HG_TILE
    others = range(0, blk) if reverse else range(blk + 1, n_tiles)
    return [(blk, r)] + [(t, None) for t in others]


def _pair_decay(G, s, tile, r, rid8, reverse, keys=False):
    rs = slice(tile * HG_TILE, (tile + 1) * HG_TILE)
    d = (G[s:s + 1] - G[rs]) if keys else (G[rs] - G[s:s + 1])
    if r is not None:
        d = jnp.where((rid8 <= r) if reverse else (rid8 >= r), d, NEG)
    return rs, jnp.exp2(d)


def _hgrn_fwd(z, lb, direction, name, exchange=None):
    S = z.shape[0]
    C, DK, W = HG_CHUNK, HG_DIM, HG_WIDTH
    tb = min(HG_ROWS, S)
    n_t, n_c = S // tb, tb // C
    reverse = direction == 1
    tmap = (lambda i: n_t - 1 - i) if reverse else (lambda i: i)

    def body(q_ref, f_ref, v_ref, lb_ref, o_ref, st_out_ref, st_ref):
        @pl.when(pl.program_id(0) == 0)
        def _():
            st_ref[...] = jnp.zeros_like(st_ref)

        oml = 1.0 - lb_ref[...]
        tri, _, _, last = _hgrn_consts(reverse)
        rid8 = lax.broadcasted_iota(jnp.int32, (HG_TILE, W), 0)

        def chunk(ci, carry):
            cidx = (n_c - 1 - ci) if reverse else ci
            rows = pl.ds(pl.multiple_of(cidx * C, C), C)
            v = v_ref[rows, :]
            q, k, G, Gl = _hgrn_chunk_common(q_ref[rows, :], f_ref[rows, :], oml, tri, last)
            qd = (q * jnp.exp2(G)).astype(BF16)
            kd = (k * jnp.exp2(Gl - G)).astype(BF16)
            e_gl = jnp.exp2(Gl)
            v_b = v.astype(BF16)
            inter = []
            for h, sl in enumerate(_head_slices()):
                st0 = st_ref[h]
                st_out_ref[h, cidx] = st0
                inter.append(_nt(qd[:, sl], st0.astype(BF16)))
                st_ref[h] = st0 * e_gl[:, sl] + _tn(v_b[:, sl], kd[:, sl])
            o = jnp.concatenate(inter, axis=1)
            o_t = [o[t * HG_TILE:(t + 1) * HG_TILE] for t in range(C // HG_TILE)]
            for s in range(C):
                k_s, v_s = k[s:s + 1], v[s:s + 1]
                for tile, r in _pair_tiles(s, reverse):
                    rs, e_s = _pair_decay(G, s, tile, r, rid8, reverse)
                    o_t[tile] = o_t[tile] + _per_head_lane_sum(q[rs] * k_s * e_s) * v_s
            o_ref[rows, :] = jnp.concatenate(o_t, axis=0)
            return carry

        lax.fori_loop(0, n_c, chunk, 0, unroll=8)

    def sec(j):
        return pl.BlockSpec((tb, W), lambda i: (tmap(i), j))

    return _call(
        body, name=name, grid=(n_t,),
        in_specs=[sec(0), sec(1 + direction), sec(3), pl.BlockSpec((1, W), lambda i: (0, 0))],
        out_specs=[sec(0), pl.BlockSpec((HG_HEADS, n_c, DK, DK), lambda i: (0, tmap(i), 0, 0))],
        out_shape=[jax.ShapeDtypeStruct((S, W), F32), jax.ShapeDtypeStruct((HG_HEADS, S // C, DK, DK), F32)],
        scratch_shapes=[pltpu.VMEM((HG_HEADS, DK, DK), F32)], args=(z, z, z, lb), semantics=("arbitrary",),
        exchange=exchange)


def _hgrn_bwd(z, lb, do, states, direction, name, acc=None, exchange=None):
    S = z.shape[0]
    C, DK, W = HG_CHUNK, HG_DIM, HG_WIDTH
    tb = min(HG_ROWS, S)
    n_t, n_c = S // tb, tb // C
    reverse = direction == 1
    tmap = (lambda i: i) if reverse else (lambda i: n_t - 1 - i)

    def body(*refs):
        if acc:
            q_ref, f_ref, v_ref, lb_ref, do_ref, st_in_ref, dqa_ref, dva_ref, dq_ref, df_ref, dv_ref, doml_ref, dst_ref = refs
        else:
            q_ref, f_ref, v_ref, lb_ref, do_ref, st_in_ref, dq_ref, df_ref, dv_ref, doml_ref, dst_ref = refs

        @pl.when(pl.program_id(0) == 0)
        def _():
            dst_ref[...] = jnp.zeros_like(dst_ref)
            doml_ref[...] = jnp.zeros_like(doml_ref)

        oml = 1.0 - lb_ref[...]
        tri, tri_t, rid, last = _hgrn_consts(reverse)
        rid8 = lax.broadcasted_iota(jnp.int32, (HG_TILE, W), 0)

        def chunk(ci, carry):
            cidx = ci if reverse else (n_c - 1 - ci)
            rows = pl.ds(pl.multiple_of(cidx * C, C), C)
            qr, fr, v, dov = q_ref[rows, :], f_ref[rows, :], v_ref[rows, :], do_ref[rows, :]
            q, k, G, Gl = _hgrn_chunk_common(qr, fr, oml, tri, last)
            e_g, e_gl, e_kd = jnp.exp2(G), jnp.exp2(Gl), jnp.exp2(Gl - G)
            qd, kd = q * e_g, k * e_kd
            do_b, v_b, qd_b, kd_b = dov.astype(BF16), v.astype(BF16), qd.astype(BF16), kd.astype(BF16)
            dqd, dkd, dv, state_dot = [], [], [], []
            for h, sl in enumerate(_head_slices()):
                st0, dst1 = st_in_ref[h, cidx], dst_ref[h]
                dst1_b = dst1.astype(BF16)
                dqd.append(_nn(do_b[:, sl], st0.astype(BF16)))
                dkd.append(_nn(v_b[:, sl], dst1_b))
                dv.append(_nt(kd_b[:, sl], dst1_b))
                state_dot.append(jnp.sum(st0 * dst1, axis=0, keepdims=True))
                dst_ref[h] = dst1 * e_gl[:, sl] + _tn(do_b[:, sl], qd_b[:, sl])
            dqd, dkd, dv = [jnp.concatenate(t, axis=1) for t in (dqd, dkd, dv)]
            d_gl = e_gl * jnp.concatenate(state_dot, axis=1) + jnp.sum(dkd * kd, axis=0, keepdims=True)
            dq, dk = dqd * e_g, dkd * e_kd
            n_tiles = C // HG_TILE
            dq_t, dk_t, dv_t = [[x[t * HG_TILE:(t + 1) * HG_TILE] for t in range(n_tiles)] for x in (dq, dk, dv)]
            for s in range(C):
                k_s, v_s = k[s:s + 1], v[s:s + 1]
                for tile, r in _pair_tiles(s, reverse):
                    rs, e_s = _pair_decay(G, s, tile, r, rid8, reverse)
                    dq_t[tile] = dq_t[tile] + _per_head_lane_sum(dov[rs] * v_s) * e_s * k_s
            for t in range(C):
                q_t, do_t = q[t:t + 1], dov[t:t + 1]
                for tile, r in _pair_tiles(t, not reverse):
                    rs, x_t = _pair_decay(G, t, tile, r, rid8, not reverse, keys=True)
                    qx = q_t * x_t
                    dv_t[tile] = dv_t[tile] + _per_head_lane_sum(k[rs] * qx) * do_t
                    dk_t[tile] = dk_t[tile] + _per_head_lane_sum(v[rs] * do_t) * qx
            dq, dk, dv = [jnp.concatenate(x, axis=0) for x in (dq_t, dk_t, dv_t)]
            d_big_g = dq * q - dk * k + jnp.where(rid == last, d_gl, 0.0)
            dg = _nn(tri_t, d_big_g, precision=lax.Precision.HIGHEST)
            dk_all = dk - dg / (1.0 - k)
            sig_nf = _sigmoid(-fr)
            df_ref[rows, :] = -dk_all * k * (1.0 - sig_nf)
            doml_ref[...] += jnp.sum(dk_all * sig_nf, axis=0, keepdims=True)
            sq = _sigmoid(qr)
            dqr = dq * (sq * (1.0 + qr * (1.0 - sq)))
            if acc:
                dqr = dqr + dqa_ref[rows, :]
                dv = dv + dva_ref[rows, :]
            dq_ref[rows, :] = dqr
            dv_ref[rows, :] = dv
            return carry

        lax.fori_loop(0, n_c, chunk, 0, unroll=8)

    def sec(j):
        return pl.BlockSpec((tb, W), lambda i: (tmap(i), j))

    vec = pl.BlockSpec((1, W), lambda i: (0, 0))
    ins = [z, z, z, lb, do, states]
    in_specs = [sec(0), sec(1 + direction), sec(3), vec, sec(0),
                pl.BlockSpec((HG_HEADS, n_c, DK, DK), lambda i: (0, tmap(i), 0, 0))]
    if acc:
        ins += list(acc)
        in_specs += [sec(0), sec(0)]
    full = jax.ShapeDtypeStruct((S, W), F32)
    return _call(
        body, name=name, grid=(n_t,), in_specs=in_specs,
        out_specs=[sec(0), sec(0), sec(0), vec],
        out_shape=[full, full, full, jax.ShapeDtypeStruct((1, W), F32)],
        scratch_shapes=[pltpu.VMEM((HG_HEADS, DK, DK), F32)], args=ins, semantics=("arbitrary",), exchange=exchange)


def _hgrn_post_fwd(o_f, o_b, z, norm_g, name):
    S = z.shape[0]
    tr = min(ROW_TILE, S)

    def body(of_ref, ob_ref, gr_ref, ng_ref, y_ref):
        o = of_ref[...] + ob_ref[...]
        gr = gr_ref[...]
        gate = gr * _sigmoid(gr)
        ng = ng_ref[...]
        for h in range(HG_HEADS):
            sl = slice(h * HG_DIM, (h + 1) * HG_DIM)
            oh = o[:, sl]
            rstd = lax.rsqrt(jnp.mean(oh * oh, axis=-1, keepdims=True) + EPS)
            y_ref[:, sl] = (oh * rstd * ng[:, sl] * gate[:, sl]).astype(y_ref.dtype)

    row = pl.BlockSpec((tr, HG_WIDTH), lambda i: (i, 0))
    return pl.pallas_call(
        body, name=name, grid=(S // tr,),
        in_specs=[row, row, pl.BlockSpec((tr, HG_WIDTH), lambda i: (i, 4)), pl.BlockSpec((1, HG_WIDTH), lambda i: (0, 0))],
        out_specs=row, out_shape=jax.ShapeDtypeStruct((S, HG_WIDTH), BF16), compiler_params=_params("parallel"),
    )(o_f, o_b, z, norm_g)


def _hgrn_post_bwd(dy, o_f, o_b, z, norm_g, name):
    S = z.shape[0]
    tr = min(ROW_TILE, S)

    def body(dy_ref, of_ref, ob_ref, gr_ref, ng_ref, do_ref, dgr_ref, dng_ref):
        @pl.when(pl.program_id(0) == 0)
        def _():
            dng_ref[...] = jnp.zeros_like(dng_ref)

        o = of_ref[...] + ob_ref[...]
        gr, ng, dyv = gr_ref[...], ng_ref[...], dy_ref[...]
        sg = _sigmoid(gr)
        for h in range(HG_HEADS):
            sl = slice(h * HG_DIM, (h + 1) * HG_DIM)
            oh, dyh, grh, sgh, ngh = o[:, sl], dyv[:, sl], gr[:, sl], sg[:, sl], ng[:, sl]
            rstd = lax.rsqrt(jnp.mean(oh * oh, axis=-1, keepdims=True) + EPS)
            on = oh * rstd
            du = dyh * (grh * sgh)
            dgr_ref[:, sl] = dyh * (on * ngh) * (sgh * (1.0 + grh * (1.0 - sgh)))
            dng_ref[:, sl] += jnp.sum(du * on, axis=0, keepdims=True)
            don = du * ngh
            do_ref[:, sl] = rstd * (don - on * jnp.mean(don * on, axis=-1, keepdims=True))

    row = pl.BlockSpec((tr, HG_WIDTH), lambda i: (i, 0))
    vec = pl.BlockSpec((1, HG_WIDTH), lambda i: (0, 0))
    full = jax.ShapeDtypeStruct((S, HG_WIDTH), F32)
    return pl.pallas_call(
        body, name=name, grid=(S // tr,),
        in_specs=[row, row, row, pl.BlockSpec((tr, HG_WIDTH), lambda i: (i, 4)), vec],
        out_specs=[row, row, vec], out_shape=[full, full, jax.ShapeDtypeStruct((1, HG_WIDTH), F32)],
        compiler_params=_params("arbitrary"),
    )(dy, o_f, o_b, z, norm_g)


def _t5_bucket_table():
    rel = (np.arange(3 * BLOCK)[None, :] - BLOCK) - np.arange(BLOCK)[:, None]
    nb = NUM_BUCKETS // 2
    max_exact = nb // 2
    ret = (rel > 0).astype(np.int32) * nb
    n = np.abs(rel)
    ratio = np.log(np.maximum(n, 1).astype(np.float32) / np.float32(max_exact)) / np.float32(math.log(MAX_DISTANCE / max_exact))
    large = max_exact + (ratio.astype(np.float32) * np.float32(nb - max_exact)).astype(np.int32)
    large = np.minimum(large, nb - 1)
    bucket = ret + np.where(n < max_exact, n, large)
    return bucket.astype(np.int32), (n <= WINDOW)


def _bias_table(rel_bias, name):
    bucket, in_band = _t5_bucket_table()
    idx = jnp.asarray(np.where(in_band, bucket, -1))

    def body(rb_ref, idx_ref, o_ref):
        h = pl.program_id(0)
        iv = idx_ref[...]
        acc = jnp.where(iv < 0, NEG, 0.0).astype(F32)
        for b in range(NUM_BUCKETS):
            acc = acc + jnp.where(iv == b, rb_ref[b, h], 0.0)
        o_ref[...] = acc

    return pl.pallas_call(
        body, name=name, grid=(ATT_Q_HEADS,),
        in_specs=[pl.BlockSpec(memory_space=pltpu.SMEM), pl.BlockSpec((BLOCK, 3 * BLOCK), lambda h: (0, 0))],
        out_specs=pl.BlockSpec((None, BLOCK, 3 * BLOCK), lambda h: (h, 0, 0)),
        out_shape=jax.ShapeDtypeStruct((ATT_Q_HEADS, BLOCK, 3 * BLOCK), F32), compiler_params=_params("parallel"),
    )(rel_bias, idx)


def _bias_grad(ds_sum, name):
    bucket, in_band = _t5_bucket_table()
    idx = jnp.asarray(np.where(in_band, bucket, -1))

    def body(ds_ref, idx_ref, o_ref):
        iv, ds = idx_ref[...], ds_ref[...]
        for b in range(NUM_BUCKETS):
            part = jnp.sum(jnp.where(iv == b, ds, 0.0), axis=0, keepdims=True)
            o_ref[b:b + 1, :] = part[:, 0:BLOCK] + part[:, BLOCK:2 * BLOCK] + part[:, 2 * BLOCK:3 * BLOCK]

    return pl.pallas_call(
        body, name=name, grid=(ATT_Q_HEADS,),
        in_specs=[pl.BlockSpec((None, BLOCK, 3 * BLOCK), lambda h: (h, 0, 0)), pl.BlockSpec((BLOCK, 3 * BLOCK), lambda h: (0, 0))],
        out_specs=pl.BlockSpec((None, NUM_BUCKETS, BLOCK), lambda h: (h, 0, 0)),
        out_shape=jax.ShapeDtypeStruct((ATT_Q_HEADS, NUM_BUCKETS, BLOCK), F32), compiler_params=_params("parallel"),
    )(ds_sum, idx)


Q_COL = 5 * HG_WIDTH
KV_COL = Q_COL + ATT_WIDTH
GROUP_WIDTH = ATT_GROUP * ATT_HEAD_DIM


def _stack_heads(blk):
    dh = ATT_HEAD_DIM
    return jnp.concatenate([blk[:, g * dh:(g + 1) * dh] for g in range(ATT_GROUP)], axis=0)


def _unstack_heads(st):
    return jnp.concatenate([st[g * BLOCK:(g + 1) * BLOCK] for g in range(ATT_GROUP)], axis=1)


def _rms_rows(x):
    rstd = lax.rsqrt(jnp.mean(x * x, axis=-1, keepdims=True) + EPS)
    return x * rstd, rstd


def _edge_ok(n, nb):
    colid = lax.broadcasted_iota(jnp.int32, (ATT_GROUP * BLOCK, 3 * BLOCK), 1)
    return jnp.logical_and(jnp.logical_or(colid >= BLOCK, n > 0), jnp.logical_or(colid < 2 * BLOCK, n < nb - 1))


def _sink_column(sink_ref, j=0):
    heads = range(j * ATT_GROUP, (j + 1) * ATT_GROUP)
    return jnp.concatenate([jnp.broadcast_to(sink_ref[h][:, 0:1], (BLOCK, 1)) for h in heads], axis=0)


def _attn_fwd(z, q_g, k_g, sink, bias, name):
    S = z.shape[0]
    nb = S // BLOCK
    G, dh, KV = ATT_GROUP, ATT_HEAD_DIM, ATT_KV_HEADS
    scale = 1.0 / math.sqrt(dh)

    def body(q_ref, kv0, kv1, kv2, qg_ref, kg_ref, sink_ref, bias_ref, o_ref):
        n = pl.program_id(0)
        edge_ok = _edge_ok(n, nb)
        cat = jnp.concatenate([kv0[...], kv1[...], kv2[...]], axis=0)
        qblk = q_ref[...]
        kn = [(_rms_rows(cat[:, j * dh:(j + 1) * dh])[0] * kg_ref[...]).astype(BF16) for j in range(KV)]
        vb = [cat[:, (KV + j) * dh:(KV + j + 1) * dh].astype(BF16) for j in range(KV)]
        qn = [(_rms_rows(_stack_heads(qblk[:, j * GROUP_WIDTH:(j + 1) * GROUP_WIDTH]))[0] * (qg_ref[...] * scale)).astype(BF16)
              for j in range(KV)]
        s = [_nt(qn[j], kn[j]) + bias_ref[j * G:(j + 1) * G].reshape(G * BLOCK, 3 * BLOCK) for j in range(KV)]
        s = [jnp.where(edge_ok, sj, NEG) for sj in s]
        sinks = [_sink_column(sink_ref, j) for j in range(KV)]
        m = [jnp.maximum(jnp.max(s[j], axis=-1, keepdims=True), sinks[j]) for j in range(KV)]
        e = [jnp.exp(s[j] - m[j]) for j in range(KV)]
        den = [jnp.sum(e[j], axis=-1, keepdims=True) + jnp.exp(sinks[j] - m[j]) for j in range(KV)]
        o = [_nn(e[j].astype(BF16), vb[j]) * (1.0 / den[j]) for j in range(KV)]
        o_ref[...] = jnp.concatenate([_unstack_heads(oj) for oj in o], axis=1).astype(o_ref.dtype)

    def kv(shift):
        return pl.BlockSpec((BLOCK, 2 * KV_WIDTH), lambda n: (jnp.clip(n + shift, 0, nb - 1), KV_COL // (2 * KV_WIDTH)))

    gain = pl.BlockSpec((1, dh), lambda n: (0, 0))
    return pl.pallas_call(
        body, name=name, grid=(nb,),
        in_specs=[pl.BlockSpec((BLOCK, ATT_WIDTH), lambda n: (n, Q_COL // ATT_WIDTH)), kv(-1), kv(0), kv(1), gain, gain,
                  pl.BlockSpec((ATT_Q_HEADS, 1, BLOCK), lambda n: (0, 0, 0)),
                  pl.BlockSpec((ATT_Q_HEADS, BLOCK, 3 * BLOCK), lambda n: (0, 0, 0))],
        out_specs=pl.BlockSpec((BLOCK, ATT_WIDTH), lambda n: (n, 0)),
        out_shape=jax.ShapeDtypeStruct((S, ATT_WIDTH), BF16), compiler_params=_params("parallel"),
    )(z, z, z, z, q_g, k_g, sink, bias)


def _attn_bwd(z, q_g, k_g, sink, bias, do, name):
    S = z.shape[0]
    nb = S // BLOCK
    G, dh, KV = ATT_GROUP, ATT_HEAD_DIM, ATT_KV_HEADS
    scale = 1.0 / math.sqrt(dh)
    both = range(KV)

    def body(q_ref, kv0, kv1, kv2, qg_ref, kg_ref, sink_ref, bias_ref, do_ref,
             dq_ref, dkw_ref, dvw_ref, ds_ref, dsink_ref, dqg_ref):
        n = pl.program_id(0)

        @pl.when(n == 0)
        def _():
            ds_ref[...] = jnp.zeros_like(ds_ref)
            dsink_ref[...] = jnp.zeros_like(dsink_ref)
            dqg_ref[...] = jnp.zeros_like(dqg_ref)

        edge_ok = _edge_ok(n, nb)
        qg = qg_ref[...]
        cat = jnp.concatenate([kv0[...], kv1[...], kv2[...]], axis=0)
        qblk, doblk = q_ref[...], do_ref[...]
        kn = [(_rms_rows(cat[:, j * dh:(j + 1) * dh])[0] * kg_ref[...]).astype(BF16) for j in both]
        vb = [cat[:, (KV + j) * dh:(KV + j + 1) * dh].astype(BF16) for j in both]
        norm = [_rms_rows(_stack_heads(qblk[:, j * GROUP_WIDTH:(j + 1) * GROUP_WIDTH])) for j in both]
        qn = [(norm[j][0] * (qg * scale)).astype(BF16) for j in both]
        do_b = [_stack_heads(doblk[:, j * GROUP_WIDTH:(j + 1) * GROUP_WIDTH]).astype(BF16) for j in both]
        s = [_nt(qn[j], kn[j]) + bias_ref[j * G:(j + 1) * G].reshape(G * BLOCK, 3 * BLOCK) for j in both]
        dp = [_nt(do_b[j], vb[j]) for j in both]
        s = [jnp.where(edge_ok, sj, NEG) for sj in s]
        sinks = [_sink_column(sink_ref, j) for j in both]
        m = [jnp.maximum(jnp.max(s[j], axis=-1, keepdims=True), sinks[j]) for j in both]
        e = [jnp.exp(s[j] - m[j]) for j in both]
        e_sink = [jnp.exp(sinks[j] - m[j]) for j in both]
        inv = [1.0 / (jnp.sum(e[j], axis=-1, keepdims=True) + e_sink[j]) for j in both]
        p = [e[j] * inv[j] for j in both]
        delta = [jnp.sum(p[j] * dp[j], axis=-1, keepdims=True) for j in both]
        ds = [p[j] * (dp[j] - delta[j]) for j in both]
        ds_b = [dsj.astype(BF16) for dsj in ds]
        dqn = [_nn(ds_b[j], kn[j]) * scale for j in both]
        for j in both:
            dvw_ref[j] = _tn(p[j].astype(BF16), do_b[j])
            dkw_ref[j] = _tn(ds_b[j], qn[j])
        for j in both:
            ds_ref[j * G:(j + 1) * G] += ds[j].reshape(G, BLOCK, 3 * BLOCK)
            sink_term = e_sink[j] * inv[j] * delta[j]
            for g in range(G):
                dsink_ref[j * G + g] += (jnp.zeros((1, BLOCK), F32)
                                         - jnp.sum(sink_term[g * BLOCK:(g + 1) * BLOCK], axis=0, keepdims=True))
        dq = []
        for j in both:
            qhat, rstd = norm[j]
            dqg_ref[j] += jnp.sum(dqn[j] * qhat, axis=0, keepdims=True)
            dqh = dqn[j] * qg
            dq.append(_unstack_heads(rstd * (dqh - qhat * jnp.mean(dqh * qhat, axis=-1, keepdims=True))))
        dq_ref[...] = jnp.concatenate(dq, axis=1).astype(dq_ref.dtype)

    def kv(shift):
        return pl.BlockSpec((BLOCK, 2 * KV_WIDTH), lambda n: (jnp.clip(n + shift, 0, nb - 1), KV_COL // (2 * KV_WIDTH)))

    gain = pl.BlockSpec((1, dh), lambda n: (0, 0))
    sink_spec = pl.BlockSpec((ATT_Q_HEADS, 1, BLOCK), lambda n: (0, 0, 0))
    bias_spec = pl.BlockSpec((ATT_Q_HEADS, BLOCK, 3 * BLOCK), lambda n: (0, 0, 0))
    win = pl.BlockSpec((KV, None, 3 * BLOCK, dh), lambda n: (0, n, 0, 0))
    wshape = jax.ShapeDtypeStruct((KV, nb, 3 * BLOCK, dh), F32)
    return pl.pallas_call(
        body, name=name, grid=(nb,),
        in_specs=[pl.BlockSpec((BLOCK, ATT_WIDTH), lambda n: (n, Q_COL // ATT_WIDTH)), kv(-1), kv(0), kv(1), gain, gain,
                  sink_spec, bias_spec, pl.BlockSpec((BLOCK, ATT_WIDTH), lambda n: (n, HG_WIDTH // ATT_WIDTH))],
        out_specs=[pl.BlockSpec((BLOCK, ATT_WIDTH), lambda n: (n, 0)), win, win, bias_spec, sink_spec,
                   pl.BlockSpec((KV, 1, dh), lambda n: (0, 0, 0))],
        out_shape=[jax.ShapeDtypeStruct((S, ATT_WIDTH), BF16), wshape, wshape,
                   jax.ShapeDtypeStruct((ATT_Q_HEADS, BLOCK, 3 * BLOCK), F32),
                   jax.ShapeDtypeStruct((ATT_Q_HEADS, 1, BLOCK), F32),
                   jax.ShapeDtypeStruct((KV, 1, dh), F32)],
        compiler_params=_params("arbitrary"),
    )(z, z, z, z, q_g, k_g, sink, bias, do)


def _attn_kv_reduce(dkw, dvw, z, k_g, name):
    S = z.shape[0]
    nb = S // BLOCK
    dh = ATT_HEAD_DIM
    kb = min(8, nb)
    steps = nb // kb

    def body(a_lo, a, a_hi, b_lo, b, b_hi, kv_ref, kg_ref, dkv_ref, dkg_ref):
        n = pl.program_id(0)

        @pl.when(n == 0)
        def _():
            dkg_ref[...] = jnp.zeros_like(dkg_ref)

        lo = jnp.where(n > 0, 1.0, 0.0)
        hi = jnp.where(n < steps - 1, 1.0, 0.0)

        def overlap_add(w, w_lo, w_hi, j, i):
            before = lo * w_lo[j] if i == 0 else w[j, i - 1, 2 * BLOCK:3 * BLOCK, :]
            after = hi * w_hi[j] if i == kb - 1 else w[j, i + 1, 0:BLOCK, :]
            return w[j, i, BLOCK:2 * BLOCK, :] + before + after

        dkg = [jnp.zeros((1, dh), F32) for _ in range(ATT_KV_HEADS)]
        for i in range(kb):
            rows = slice(i * BLOCK, (i + 1) * BLOCK)
            dks, dvs = [], []
            for j in range(ATT_KV_HEADS):
                dkn = overlap_add(a, a_lo, a_hi, j, i)
                dvs.append(overlap_add(b, b_lo, b_hi, j, i))
                khat, rstd = _rms_rows(kv_ref[rows, j * dh:(j + 1) * dh])
                dkg[j] = dkg[j] + jnp.sum(dkn * khat, axis=0, keepdims=True)
                dkh = dkn * kg_ref[...]
                dks.append(rstd * (dkh - khat * jnp.mean(dkh * khat, axis=-1, keepdims=True)))
            dkv_ref[rows, :] = jnp.concatenate(dks + dvs, axis=1).astype(dkv_ref.dtype)
        for j in range(ATT_KV_HEADS):
            dkg_ref[j] += dkg[j]

    main = pl.BlockSpec((ATT_KV_HEADS, kb, 3 * BLOCK, dh), lambda n: (0, n, 0, 0))
    halo_lo = pl.BlockSpec((ATT_KV_HEADS, None, BLOCK, dh), lambda n: (0, jnp.maximum(n * kb - 1, 0), 2, 0))
    halo_hi = pl.BlockSpec((ATT_KV_HEADS, None, BLOCK, dh), lambda n: (0, jnp.minimum(n * kb + kb, nb - 1), 0, 0))
    return pl.pallas_call(
        body, name=name, grid=(steps,),
        in_specs=[halo_lo, main, halo_hi, halo_lo, main, halo_hi,
                  pl.BlockSpec((kb * BLOCK, 2 * KV_WIDTH), lambda n: (n, KV_COL // (2 * KV_WIDTH))),
                  pl.BlockSpec((1, dh), lambda n: (0, 0))],
        out_specs=[pl.BlockSpec((kb * BLOCK, 2 * KV_WIDTH), lambda n: (n, 0)),
                   pl.BlockSpec((ATT_KV_HEADS, 1, dh), lambda n: (0, 0, 0))],
        out_shape=[jax.ShapeDtypeStruct((S, 2 * KV_WIDTH), BF16), jax.ShapeDtypeStruct((ATT_KV_HEADS, 1, dh), F32)],
        compiler_params=_params("arbitrary"),
    )(dkw, dkw, dkw, dvw, dvw, dvw, z, k_g)


def _ada_fwd(c_act, w, b, name):
    n = w.shape[1]

    def body(c_ref, w_ref, b_ref, o_ref):
        o_ref[...] = _nn(c_ref[...], w_ref[...], precision=lax.Precision.HIGHEST) + b_ref[...]

    tn = n // 3
    return pl.pallas_call(
        body, name=name, grid=(3,),
        in_specs=[pl.BlockSpec(c_act.shape, lambda j: (0, 0)), pl.BlockSpec((w.shape[0], tn), lambda j: (0, j)),
                  pl.BlockSpec((1, tn), lambda j: (0, j))],
        out_specs=pl.BlockSpec((c_act.shape[0], tn), lambda j: (0, j)),
        out_shape=jax.ShapeDtypeStruct((c_act.shape[0], n), F32), compiler_params=_params("parallel"),
    )(c_act, w, b)


def _ada_wgrad(c_act_t, dm, name):
    D, nbatch = c_act_t.shape
    n = dm.shape[1]
    tr = 256

    def body(c_ref, dm_ref, o_ref):
        cv, dv = c_ref[...], dm_ref[...]
        acc = cv[:, 0:1] * dv[0:1, :]
        for b in range(1, nbatch):
            acc = acc + cv[:, b:b + 1] * dv[b:b + 1, :]
        o_ref[...] = acc

    return pl.pallas_call(
        body, name=name, grid=(D // tr,),
        in_specs=[pl.BlockSpec((tr, nbatch), lambda i: (i, 0)), pl.BlockSpec((nbatch, n), lambda i: (0, 0))],
        out_specs=pl.BlockSpec((tr, n), lambda i: (i, 0)), out_shape=jax.ShapeDtypeStruct((D, n), F32),
        compiler_params=_params("parallel"),
    )(c_act_t, dm)


def _to_bf16(w, name):
    R, Cn = w.shape
    tr = _row_tile(R)

    def body(w_ref, o_ref):
        o_ref[...] = w_ref[...].astype(BF16)

    blk = pl.BlockSpec((tr, Cn), lambda i: (i, 0))
    return pl.pallas_call(
        body, name=name, grid=(R // tr,), in_specs=[blk], out_specs=blk, out_shape=jax.ShapeDtypeStruct((R, Cn), BF16),
        compiler_params=_params("parallel"),
    )(w)


def _adamw(w, g, m, v, name):
    R, Cn = w.shape
    tr = R
    for cand in (256, 128, 64, 32, 16, 8):
        if R % cand == 0:
            tr = cand
            break

    def body(w_ref, g_ref, m_ref, v_ref, d_ref, nm_ref, nv_ref):
        gv = g_ref[...]
        m_new = ADAM_B1 * m_ref[...] + (1.0 - ADAM_B1) * gv
        v_new = ADAM_B2 * v_ref[...] + (1.0 - ADAM_B2) * (gv * gv)
        m_hat = m_new / (1.0 - ADAM_B1 ** ADAM_STEP)
        v_hat = v_new / (1.0 - ADAM_B2 ** ADAM_STEP)
        d_ref[...] = -ADAM_LR * (m_hat / (jnp.sqrt(v_hat) + ADAM_EPS) + ADAM_WD * w_ref[...])
        nm_ref[...] = m_new
        nv_ref[...] = v_new

    blk = pl.BlockSpec((tr, Cn), lambda i: (i, 0))
    shp = jax.ShapeDtypeStruct((R, Cn), F32)
    return pl.pallas_call(
        body, name=name, grid=(R // tr,), in_specs=[blk] * 4, out_specs=[blk] * 3, out_shape=[shp] * 3,
        compiler_params=_params("parallel"),
    )(w, g, m, v)


def _place():
    return lax.axis_index("x"), lax.axis_index("y"), lax.axis_index("c")


def _flip(place, k):
    x, y, c = place
    return (1 - x if k & 4 else x, 1 - y if k & 2 else y, 1 - c if k & 1 else c)


def _dev_index(place):
    x, y, c = place
    return 4 * x + 2 * y + c


def _chip_index(place):
    return 2 * place[0] + place[1]


def _allgather8(x, name, reduce=False):
    R, Cn = x.shape

    def body(x_ref, *rest):
        if reduce:
            out_ref, sum_ref, send_sems, recv_sems, local_sem = rest
        else:
            out_ref, send_sems, recv_sems, local_sem = rest
        me = _place()
        mine = pltpu.make_async_copy(x_ref, out_ref.at[_dev_index(me)], local_sem)
        mine.start()

        def copy(k, origin, to):
            return pltpu.make_async_remote_copy(
                src_ref=x_ref, dst_ref=out_ref.at[_dev_index(origin)], send_sem=send_sems.at[k - 1],
                recv_sem=recv_sems.at[k - 1], device_id=to, device_id_type=MESH)

        sends = [copy(k, me, _flip(me, k)) for k in range(1, 8)]
        for cp in sends:
            cp.start()
        for k in range(1, 8):
            copy(k, _flip(me, k), me).wait_recv()
        for cp in sends:
            cp.wait_send()
        mine.wait()
        if reduce:
            acc = out_ref[0]
            for i in range(1, 8):
                acc = acc + out_ref[i]
            sum_ref[...] = acc

    vm = pl.BlockSpec(memory_space=pltpu.VMEM)
    outs = [jax.ShapeDtypeStruct((8, R, Cn), F32)] + ([jax.ShapeDtypeStruct((R, Cn), F32)] if reduce else [])
    res = pl.pallas_call(
        body, name=name, in_specs=[vm], out_specs=[vm] * len(outs), out_shape=outs,
        scratch_shapes=[pltpu.SemaphoreType.DMA((7,)), pltpu.SemaphoreType.DMA((7,)), pltpu.SemaphoreType.DMA],
    )(x)
    return res if reduce else res[0]


def _weights_allgather(shards, name):
    n = len(shards)
    per = 8

    def body(*refs):
        in_refs, out_refs = refs[:n], refs[n:2 * n]
        send_sems, recv_sems = refs[2 * n:]
        me = _place()
        c = me[2]
        sibling = _flip(me, 1)
        others = [_flip(me, 2 * j) for j in (1, 2, 3)]

        def copy(a, k, src, dst, to):
            return pltpu.make_async_remote_copy(
                src_ref=src, dst_ref=dst, send_sem=send_sems.at[per * a + k], recv_sem=recv_sems.at[per * a + k],
                device_id=to, device_id_type=MESH)

        def block(a, place, half):
            return out_refs[a].at[_chip_index(place), half]

        started = []
        for a in range(n):
            sends = [copy(a, 0, in_refs[a].at[c], block(a, me, c), sibling),
                     copy(a, 7, in_refs[a].at[1 - c], block(a, me, 1 - c), sibling)]
            sends += [copy(a, 1 + j, in_refs[a].at[c], block(a, me, c), to) for j, to in enumerate(others)]
            for cp in sends:
                cp.start()
            started += sends
        for a in range(n):
            for j, other in enumerate(others):
                landed = block(a, other, c)
                copy(a, 1 + j, landed, landed, me).wait_recv()
                fwd = copy(a, 4 + j, landed, landed, sibling)
                fwd.start()
                started.append(fwd)
        for a in range(n):
            copy(a, 0, block(a, me, 1 - c), block(a, me, 1 - c), me).wait_recv()
            copy(a, 7, block(a, me, c), block(a, me, c), me).wait_recv()
            for j, other in enumerate(others):
                got = block(a, other, 1 - c)
                copy(a, 4 + j, got, got, me).wait_recv()
        for cp in started:
            cp.wait_send()

    return pl.pallas_call(
        body, name=name, in_specs=[ANY] * n, out_specs=[ANY] * n,
        out_shape=[jax.ShapeDtypeStruct((N_CHIPS,) + s.shape, s.dtype) for s in shards],
        scratch_shapes=[pltpu.SemaphoreType.DMA((per * n,)), pltpu.SemaphoreType.DMA((per * n,))],
    )(*shards)


def _remote(src, dst, send_sems, recv_sems, i, to):
    return pltpu.make_async_remote_copy(
        src_ref=src, dst_ref=dst, send_sem=send_sems.at[i], recv_sem=recv_sems.at[i], device_id=to, device_id_type=MESH)


def _symmetric_plan(copies):
    def plan(in_refs, out_refs, send_sems, recv_sems):
        sends = [_remote(src, dst, send_sems, recv_sems, i, to) for i, (src, dst, to) in enumerate(copies(in_refs, out_refs))]
        return sends, sends
    return plan


def _halves_exchange(grads):
    def copies(in_refs, out_refs):
        me = _place()
        return [(g.at[kk, 1 - me[2]], got.at[kk], _flip(me, 1)) for g, got in zip(in_refs, out_refs) for kk in range(N_CHIPS)]

    return _Exchange(grads, [jax.ShapeDtypeStruct((N_CHIPS,) + g.shape[2:], g.dtype) for g in grads],
                     N_CHIPS * len(grads), _symmetric_plan(copies))


def _chips_exchange(parts):
    def copies(in_refs, out_refs):
        me = _place()
        return [(p.at[_chip_index(_flip(me, 2 * j))], got.at[j - 1], _flip(me, 2 * j))
                for p, got in zip(in_refs, out_refs) for j in (1, 2, 3)]

    return _Exchange(parts, [jax.ShapeDtypeStruct((3,) + p.shape[1:], p.dtype) for p in parts], 3 * len(parts),
                     _symmetric_plan(copies))


def _siblings_exchange(halves):
    def copies(in_refs, out_refs):
        sibling = _flip(_place(), 1)
        return [(h, got, sibling) for h, got in zip(in_refs, out_refs)]

    return _Exchange(halves, [jax.ShapeDtypeStruct(h.shape, h.dtype) for h in halves], len(halves), _symmetric_plan(copies))


def _gather_over_ici(shards):
    def copies(in_refs, out_refs):
        me = _place()
        c = me[2]
        return [(w.at[c], out.at[_chip_index(me), c], _flip(me, 2 * j)) for w, out in zip(in_refs, out_refs) for j in (1, 2, 3)]

    def plan(in_refs, out_refs, send_sems, recv_sems):
        me = _place()
        sends = [_remote(src, dst, send_sems, recv_sems, i, to) for i, (src, dst, to) in enumerate(copies(in_refs, out_refs))]
        lands = [out.at[_chip_index(_flip(me, 2 * j)), me[2]] for out in out_refs for j in (1, 2, 3)]
        return sends, [_remote(z, z, send_sems, recv_sems, i, me) for i, z in enumerate(lands)]

    return _Exchange(shards, [jax.ShapeDtypeStruct((N_CHIPS,) + s.shape, s.dtype) for s in shards], 3 * len(shards), plan)


def _gather_over_d2d(shards, gathered):
    n = len(shards)

    def plan(in_refs, out_refs, send_sems, recv_sems):
        me = _place()
        c = me[2]
        sibling = _flip(me, 1)
        mine = _chip_index(me)
        sends, recvs = [], []
        for a, (w, out) in enumerate(zip(in_refs[:n], out_refs)):
            moves = [(w.at[c], (mine, c)), (w.at[1 - c], (mine, 1 - c))]
            moves += [(out.at[_chip_index(_flip(me, 2 * j)), c], (_chip_index(_flip(me, 2 * j)), c)) for j in (1, 2, 3)]
            for k, (src, (chip, half)) in enumerate(moves):
                sends.append(_remote(src, out.at[chip, half], send_sems, recv_sems, 5 * a + k, sibling))
            lands = [(mine, 1 - c), (mine, c)] + [(_chip_index(_flip(me, 2 * j)), 1 - c) for j in (1, 2, 3)]
            for k, (chip, half) in enumerate(lands):
                z = out.at[chip, half]
                recvs.append(_remote(z, z, send_sems, recv_sems, 5 * a + k, me))
        return sends, recvs

    return _Exchange(list(shards) + list(gathered), [jax.ShapeDtypeStruct(g.shape, g.dtype) for g in gathered], 5 * n, plan,
                     aliases={n + a: a for a in range(n)})


def _row_tile(rows):
    for cand in (256, 176, 128, 64, 32, 16, 8):
        if rows % cand == 0:
            return cand
    return rows


def _pair_sum(core, grad, theirs, name):
    N, _, R, Cn = grad.shape
    tr = _row_tile(R)

    def body(core_ref, g_ref, t_ref, o_ref, ob_ref):
        s = g_ref[...] + t_ref[...]
        o_ref[...] = s
        ob_ref[...] = s.astype(BF16)

    out = pl.BlockSpec((None, tr, Cn), lambda k, i, core_ref: (k, i, 0))
    return pl.pallas_call(
        body, name=name,
        grid_spec=pltpu.PrefetchScalarGridSpec(
            num_scalar_prefetch=1, grid=(N, R // tr),
            in_specs=[pl.BlockSpec((None, None, tr, Cn), lambda k, i, core_ref: (k, core_ref[0], i, 0)),
                      pl.BlockSpec((None, tr, Cn), lambda k, i, core_ref: (k, i, 0))],
            out_specs=[out, out]),
        out_shape=[jax.ShapeDtypeStruct((N, R, Cn), F32), jax.ShapeDtypeStruct((N, R, Cn), BF16)],
        compiler_params=_params("parallel", "parallel"),
    )(core, grad, theirs)


def _chip_sum(chip, parts, landed, name):
    _, R, Cn = parts.shape
    tr = _row_tile(R)

    def body(chip_ref, p_ref, l_ref, o_ref):
        o_ref[...] = ((p_ref[...] + l_ref[0].astype(F32)) + l_ref[1].astype(F32)) + l_ref[2].astype(F32)

    return pl.pallas_call(
        body, name=name,
        grid_spec=pltpu.PrefetchScalarGridSpec(
            num_scalar_prefetch=1, grid=(R // tr,),
            in_specs=[pl.BlockSpec((None, tr, Cn), lambda i, chip_ref: (chip_ref[0], i, 0)),
                      pl.BlockSpec((3, tr, Cn), lambda i, chip_ref: (0, i, 0))],
            out_specs=pl.BlockSpec((tr, Cn), lambda i, chip_ref: (i, 0))),
        out_shape=jax.ShapeDtypeStruct((R, Cn), F32), compiler_params=_params("parallel"),
    )(chip, parts, landed)


def _pair_sums(core, grads, theirs, tag):
    return [_pair_sum(core, g, t, f"{tag}_pair_sum_{i}") for i, (g, t) in enumerate(zip(grads, theirs))]


def _chip_sums(chip, parts, landed, tag):
    return [_chip_sum(chip, p[0], l, f"{tag}_chip_sum_{i}") for i, (p, l) in enumerate(zip(parts, landed))]


def _by_chip_rows(g):
    return g.reshape(N_CHIPS, 2, g.shape[0] // (2 * N_CHIPS), g.shape[1])


def _by_chip_cols(g):
    return g.reshape(N_CHIPS, 2, g.shape[1] // 2, g.shape[2])


def _adamw_halves(core, w, g_mine, g_theirs, m, v, name):
    R2, Cn = w.shape
    r = R2 // 2
    tr = _row_tile(r)
    nt = r // tr

    def body(core_ref, w_ref, gm_ref, gt_ref, m_ref, v_ref, g_ref, d_ref, nm_ref, nv_ref):
        gv = jnp.where(pl.program_id(0) == core_ref[0], gm_ref[...], gt_ref[...])
        g_ref[...] = gv
        m_new = ADAM_B1 * m_ref[...] + (1.0 - ADAM_B1) * gv
        v_new = ADAM_B2 * v_ref[...] + (1.0 - ADAM_B2) * (gv * gv)
        m_hat = m_new / (1.0 - ADAM_B1 ** ADAM_STEP)
        v_hat = v_new / (1.0 - ADAM_B2 ** ADAM_STEP)
        d_ref[...] = -ADAM_LR * (m_hat / (jnp.sqrt(v_hat) + ADAM_EPS) + ADAM_WD * w_ref[...])
        nm_ref[...] = m_new
        nv_ref[...] = v_new

    full = pl.BlockSpec((tr, Cn), lambda hf, i, core_ref: (hf * nt + i, 0))
    half = pl.BlockSpec((tr, Cn), lambda hf, i, core_ref: (i, 0))
    shp = jax.ShapeDtypeStruct((R2, Cn), F32)
    return pl.pallas_call(
        body, name=name,
        grid_spec=pltpu.PrefetchScalarGridSpec(
            num_scalar_prefetch=1, grid=(2, nt), in_specs=[full, half, half, full, full], out_specs=[full] * 4),
        out_shape=[shp] * 4, compiler_params=_params("parallel", "parallel"),
    )(core, w, g_mine, g_theirs, m, v)


def _pad_row(v, width):
    v = v.reshape(1, -1)
    return jnp.pad(v, ((0, 0), (0, width - v.shape[1])))


def _ffn1_forward(x, ng, shift, scale, gate, w_in4, w_out, gather, next_norm):
    h = _rmsmod_fwd(x, ng, shift, scale, "ffn1_norm")
    (zg, zu, a), partly = _ffn_in_fwd(h, w_in4, "ffn1_in", exchange=_gather_over_ici(gather))
    (x_new, f, h_next), gathered = _proj_out_fwd([a], w_out, x, gate, 0.5, "ffn1_out", next_norm=next_norm,
                                                 exchange=_gather_over_d2d(gather, partly))
    return x_new, (h, zg, zu, a, f), gathered, h_next


def _ffn_backward(df, saved, w_in4, w_out, core, chip, tag, riding=None, norm=None):
    h, zg, zu, a = saved[:4]
    rode = None
    if riding:
        (dzg, dzu), rode = _dact_bwd(df, w_out, zg, zu, f"{tag}_dact", exchange=riding)
    else:
        dzg, dzu = _dact_bwd(df, w_out, zg, zu, f"{tag}_dact")
    g_out = [_by_chip_rows(_wgrad(a, [df], df.shape[1], f"{tag}_dw_out")[0].reshape(a.shape[1], df.shape[1]))]
    (dw_in,), theirs_out = _wgrad(h, [dzg, dzu], FF_SHARD, f"{tag}_dw_in", exchange=_halves_exchange(g_out))
    g_in = [_by_chip_cols(dw_in.reshape(N_CHIPS, h.shape[1], FF_SHARD))]
    parts_out = _pair_sums(core, g_out, theirs_out, f"{tag}_out")
    dh_outs, (theirs_in, landed_out) = _ffn_in_dgrad(
        dzg, dzu, w_in4, f"{tag}_dh", norm=norm, exchange=[_halves_exchange(g_in), _chips_exchange([parts_out[0][1]])])
    parts_in = _pair_sums(core, g_in, theirs_in, f"{tag}_in")
    return dh_outs, parts_in, _chip_sums(chip, parts_out, landed_out, f"{tag}_out"), rode


def kernel(x, c, w_ada, b_ada, norm_g, w_ffn1_in, w_ffn1_out, w_ffn2_in, w_ffn2_out, w_mix_in, w_mix_out, hgrn_lb, hgrn_norm_g, qk_norm_g, attn_sink, rel_bias, loss_target, m_w_ada, m_b_ada, m_norm_g, m_w_ffn1_in, m_w_ffn1_out, m_w_ffn2_in, m_w_ffn2_out, m_w_mix_in, m_w_mix_out, m_hgrn_lb, m_hgrn_norm_g, m_qk_norm_g, m_attn_sink, m_rel_bias, v_w_ada, v_b_ada, v_norm_g, v_w_ffn1_in, v_w_ffn1_out, v_w_ffn2_in, v_w_ffn2_out, v_w_mix_in, v_w_mix_out, v_hgrn_lb, v_hgrn_norm_g, v_qk_norm_g, v_attn_sink, v_rel_bias):
    D = D_MODEL
    S = x.shape[1]
    place = (lax.axis_index("x"), lax.axis_index("y"), lax.axis_index("c"))
    me, my_chip = _dev_index(place), _chip_index(place)
    x0 = x[0]
    target = loss_target[0]

    def halves(w, tag):
        return _to_bf16(w[0], f"{tag}_to_bf16").reshape(2, w.shape[1] // 2, w.shape[2])

    gathered = _weights_allgather([halves(w_ffn1_in, "w_ffn1_in"), halves(w_ffn1_out, "w_ffn1_out")], "weights_allgather")
    w1_in = gathered[0].reshape(N_CHIPS, D, FF_SHARD)
    w1_out = gathered[1].reshape(D_FF, D)
    mix_shards = [halves(w_mix_in, "w_mix_in"), halves(w_mix_out, "w_mix_out")]
    ffn2_shards = [halves(w_ffn2_in, "w_ffn2_in"), halves(w_ffn2_out, "w_ffn2_out")]
    core_arr = jnp.reshape(place[2], (1,)).astype(jnp.int32)
    chip_arr = jnp.reshape(my_chip, (1,)).astype(jnp.int32)

    small = jnp.concatenate([_pad_row(c, D), _pad_row(norm_g, D), _pad_row(hgrn_lb, D), jnp.zeros((5, D), F32)], axis=0)
    small_all = _allgather8(small, "small_allgather")
    c_all = small_all[:, 0, :]
    by_chip = small_all[0::2]
    norm_g_full = by_chip[:, 1, :3 * 256].reshape(N_CHIPS, 3, 256).transpose(1, 0, 2).reshape(3, D)
    lb_raw = by_chip[:, 2, :2 * 2 * 128].reshape(N_CHIPS, 2, 2, 128).transpose(1, 2, 0, 3).reshape(2, 2, HG_WIDTH)
    lb = jax.nn.sigmoid(lb_raw[:, 0, :] - lb_raw[:, 1, :])
    lb_f, lb_b = lb[0:1], lb[1:2]

    c_act_all = c_all * jax.nn.sigmoid(c_all)
    n_ada = w_ada.shape[2]
    b_mine = lax.dynamic_slice_in_dim(b_ada, my_chip * n_ada, n_ada, axis=1)
    mods_part = _ada_fwd(c_act_all, w_ada[0], b_mine, "ada_fwd")
    mods_all = _allgather8(mods_part, "mods_allgather")[0::2].transpose(1, 0, 2).reshape(8, N_MOD * D)
    mods = lax.dynamic_slice_in_dim(mods_all, me, 1, axis=0)
    sh1, sc1, g1, sh2, sc2, g2, sh3, sc3, g3 = [mods[:, i * D:(i + 1) * D] for i in range(N_MOD)]

    x1, saved1, gathered, h2 = _ffn1_forward(x0, norm_g_full[0:1], sh1, sc1, g1, w1_in, w1_out, mix_shards,
                                             (norm_g_full[1:2], sh2, sc2))
    wm_in = gathered[0].reshape(N_CHIPS, D, D_IN // N_CHIPS).transpose(1, 0, 2).reshape(D, D_IN)
    wm_out = gathered[1].reshape(D, D)

    z = _matmul_nn(h2, wm_in, F32, 256, "mix_in")
    (of, st_f), partly = _hgrn_fwd(z, lb_f, 0, "hgrn_fwd_f", exchange=_gather_over_ici(ffn2_shards))
    (ob, st_b), gathered = _hgrn_fwd(z, lb_b, 1, "hgrn_fwd_b", exchange=_gather_over_d2d(ffn2_shards, partly))
    w2_in = gathered[0].reshape(N_CHIPS, D, FF_SHARD)
    w2_out = gathered[1].reshape(D_FF, D)
    o_h = _hgrn_post_fwd(of, ob, z, hgrn_norm_g, "hgrn_post")

    q_g, k_g = qk_norm_g[0, 0:1], qk_norm_g[0, 1:2]
    sink_b = jnp.broadcast_to(attn_sink.reshape(ATT_Q_HEADS, 1, 1), (ATT_Q_HEADS, 1, BLOCK))
    bias = _bias_table(rel_bias, "bias_table")
    o_a = _attn_fwd(z, q_g, k_g, sink_b, bias, "attn_fwd")
    x2, mixed, h3 = _proj_out_fwd([o_h, o_a], wm_out, x1, g2, 1.0, "mix_out", next_norm=(norm_g_full[2:3], sh3, sc3))

    zg3, zu3, a3 = _ffn_in_fwd(h3, w2_in, "ffn2_in")
    dx3, df3, dg3, sq_cols = _proj_out_loss(a3, w2_out, x2, g3, 0.5, target, "ffn2_out_loss")
    loss_mine = 0.5 * jnp.sum(sq_cols) / D

    (dx2, dsh3, dsc3, dng3, dmixed, dg2), parts2, mine2_out, _ = _ffn_backward(
        df3, (h3, zg3, zu3, a3), w2_in, w2_out, core_arr, chip_arr, "ffn2",
        norm=_NormBwd(x2, norm_g_full[2:3], sc3, dx3, below=(mixed, g2, 1.0)))

    (do_cat,) = _matmul_nt(dmixed, wm_out, ROW_TILE, "mix_out_dgrad")
    dwm_out = _wgrad_rows([o_h, o_a], dmixed, "mix_out_dw").reshape(D, D)

    do_sum, dgr, d_hnorm = _hgrn_post_bwd(do_cat, of, ob, z, hgrn_norm_g, "hgrn_post_bwd")
    (dq_f, dff, dv_f, doml_f), landed2 = _hgrn_bwd(z, lb_f, do_sum, st_f, 0, "hgrn_bwd_f",
                                                   exchange=_chips_exchange([p[1] for p in parts2]))
    mine2 = _chip_sums(chip_arr, parts2, landed2, "ffn2_in") + mine2_out
    (dhq, dfb, dhi, doml_b), theirs2 = _hgrn_bwd(z, lb_b, do_sum, st_b, 1, "hgrn_bwd_b", acc=(dq_f, dv_f),
                                                 exchange=_siblings_exchange(mine2))

    daq, dkw, dvw, ds_sum, dsink, dqg = _attn_bwd(z, q_g, k_g, sink_b, bias, do_cat, "attn_bwd")
    dkv, dkg = _attn_kv_reduce(dkw, dvw, z, k_g, "attn_kv_reduce")
    d_rel_bias = jnp.sum(_bias_grad(ds_sum, "bias_grad"), axis=-1).T
    dz = jnp.concatenate([t.astype(BF16) for t in (dhq, dff, dfb, dhi, dgr)] + [daq, dkv], axis=1)
    dwm_in = _wgrad(h2, [dz], D_IN // 2, "mix_in_dw")[0][0]
    dwm_in = jnp.concatenate([dwm_in[0], dwm_in[1]], axis=1)
    wide = D_IN // N_CHIPS
    grads_m = [_by_chip_cols(dwm_in.reshape(D, N_CHIPS, wide).transpose(1, 0, 2)), _by_chip_rows(dwm_out)]
    (dx1, dsh2, dsc2, dng2, df1, dg1), theirs_m = _matmul_nt(
        dz, wm_in, 256, "mix_in_dgrad", exchange=_halves_exchange(grads_m),
        norm=_NormBwd(x1, norm_g_full[1:2], sc2, dx2, below=(saved1[4], g1, 0.5)))
    parts_m = _pair_sums(core_arr, grads_m, theirs_m, "mix")

    (dh1,), parts1, mine1_out, landed_m = _ffn_backward(df1, saved1, w1_in, w1_out, core_arr, chip_arr, "ffn1",
                                                        riding=_chips_exchange([p[1] for p in parts_m]))
    mine_m = _chip_sums(chip_arr, parts_m, landed_m, "mix")
    (dx0, dsh1, dsc1, dng1), landed1 = _rmsmod_bwd(dh1, _NormBwd(x0, norm_g_full[0:1], sc1, dx1), "ffn1_norm_bwd",
                                                   exchange=_chips_exchange([p[1] for p in parts1]))
    mine1 = _chip_sums(chip_arr, parts1, landed1, "ffn1_in") + mine1_out
    theirs_1m = list(_run_exchange(_siblings_exchange(mine1 + mine_m), "siblings_exchange"))
    reduced = list(zip(mine1 + mine2 + mine_m, theirs_1m[:2] + list(theirs2) + theirs_1m[2:]))

    dlb = -jnp.concatenate([doml_f, doml_b], axis=0)
    dlb_raw = dlb * lb * (1.0 - lb)
    d_hgrn_lb = jnp.stack([dlb_raw, -dlb_raw], axis=1)
    d_qk = jnp.concatenate([jnp.sum(dqg, axis=0), jnp.sum(dkg, axis=0)], axis=0)
    dmods = jnp.concatenate([dsh1, dsc1, dg1, dsh2, dsc2, dg2, dsh3, dsc3, dg3], axis=0)
    packed = jnp.concatenate(
        [dmods, dng1, dng2, dng3, d_hgrn_lb.reshape(2, D), _pad_row(d_hnorm, D), _pad_row(d_qk, D),
         _pad_row(dsink[:, 0, 0], D), _pad_row(d_rel_bias, D), _pad_row(loss_mine, D)], axis=0)
    packed = jnp.pad(packed, ((0, 24 - packed.shape[0]), (0, 0)))
    packed_all, packed_sum = _allgather8(packed, "small_grads_allgather", reduce=True)
    dmods_all = packed_all[:, 0:N_MOD, :].reshape(8, N_MOD * D)
    g_b_ada = packed_sum[0:N_MOD].reshape(1, N_MOD * D)
    g_norm_full = packed_sum[9:12]
    g_norm_g = lax.dynamic_slice_in_dim(g_norm_full, my_chip * 256, 256, axis=1).reshape(1, 3, 256)
    g_hgrn_lb = lax.dynamic_slice_in_dim(packed_sum[12:14].reshape(2, 2, HG_WIDTH), my_chip * 128, 128, axis=2)
    g_hgrn_norm_g = packed_sum[14:15, :HG_WIDTH]
    g_qk_norm_g = packed_sum[15, :2 * ATT_HEAD_DIM].reshape(1, 2, ATT_HEAD_DIM)
    g_attn_sink = packed_sum[16:17, :ATT_Q_HEADS]
    g_rel_bias = packed_sum[17, :NUM_BUCKETS * ATT_Q_HEADS].reshape(NUM_BUCKETS, ATT_Q_HEADS)
    loss = packed_sum[18, 0]

    dm_mine = lax.dynamic_slice_in_dim(dmods_all, my_chip * n_ada, n_ada, axis=1)
    g_w_ada = _ada_wgrad(c_act_all.T, dm_mine, "ada_wgrad")[None]

    def big(w, g, m, v, name):
        d, nm, nv = _adamw(w[0], g[0], m[0], v[0], name)
        return d[None], nm[None], nv[None]

    def big_halves(w, g_pair, m, v, name):
        g, d, nm, nv = _adamw_halves(core_arr, w[0], g_pair[0], g_pair[1], m[0], v[0], name)
        return g[None], (d[None], nm[None], nv[None])

    g_w1_in, u_w1_in = big_halves(w_ffn1_in, reduced[0], m_w_ffn1_in, v_w_ffn1_in, "adamw_w_ffn1_in")
    g_w1_out, u_w1_out = big_halves(w_ffn1_out, reduced[1], m_w_ffn1_out, v_w_ffn1_out, "adamw_w_ffn1_out")
    g_w2_in, u_w2_in = big_halves(w_ffn2_in, reduced[2], m_w_ffn2_in, v_w_ffn2_in, "adamw_w_ffn2_in")
    g_w2_out, u_w2_out = big_halves(w_ffn2_out, reduced[3], m_w_ffn2_out, v_w_ffn2_out, "adamw_w_ffn2_out")
    g_wm_in, u_wm_in = big_halves(w_mix_in, reduced[4], m_w_mix_in, v_w_mix_in, "adamw_w_mix_in")
    g_wm_out, u_wm_out = big_halves(w_mix_out, reduced[5], m_w_mix_out, v_w_mix_out, "adamw_w_mix_out")

    smalls = [(b_ada, g_b_ada, m_b_ada, v_b_ada), (norm_g, g_norm_g, m_norm_g, v_norm_g), (hgrn_lb, g_hgrn_lb, m_hgrn_lb, v_hgrn_lb),
              (hgrn_norm_g, g_hgrn_norm_g, m_hgrn_norm_g, v_hgrn_norm_g), (qk_norm_g, g_qk_norm_g, m_qk_norm_g, v_qk_norm_g),
              (attn_sink, g_attn_sink, m_attn_sink, v_attn_sink), (rel_bias, g_rel_bias, m_rel_bias, v_rel_bias)]
    sizes = [t[0].size for t in smalls]
    total = sum(sizes)
    rows = -(-total // 128)
    rows = -(-rows // 8) * 8

    def pack(i):
        flat = jnp.concatenate([t[i].reshape(-1) for t in smalls])
        fill = 1.0 if i == 3 else 0.0
        return jnp.pad(flat, (0, rows * 128 - total), constant_values=fill).reshape(rows, 128)

    packed_out = _adamw(pack(0), pack(1), pack(2), pack(3), "adamw_small")

    def unpack(flat2d):
        flat = flat2d.reshape(-1)
        outs, off = [], 0
        for t, n in zip(smalls, sizes):
            outs.append(flat[off:off + n].reshape(t[0].shape))
            off += n
        return outs

    d_small, m_small, v_small = [unpack(t) for t in packed_out]

    upd = {
        "w_ada": big(w_ada, g_w_ada, m_w_ada, v_w_ada, "adamw_w_ada"),
        "w_ffn1_in": u_w1_in, "w_ffn1_out": u_w1_out, "w_ffn2_in": u_w2_in, "w_ffn2_out": u_w2_out,
        "w_mix_in": u_wm_in, "w_mix_out": u_wm_out,
    }
    small_names = ["b_ada", "norm_g", "hgrn_lb", "hgrn_norm_g", "qk_norm_g", "attn_sink", "rel_bias"]
    for i, nme in enumerate(small_names):
        upd[nme] = (d_small[i], m_small[i], v_small[i])
    grads = {
        "w_ada": g_w_ada, "b_ada": g_b_ada, "norm_g": g_norm_g, "w_ffn1_in": g_w1_in, "w_ffn1_out": g_w1_out,
        "w_ffn2_in": g_w2_in, "w_ffn2_out": g_w2_out, "w_mix_in": g_wm_in, "w_mix_out": g_wm_out, "hgrn_lb": g_hgrn_lb,
        "hgrn_norm_g": g_hgrn_norm_g, "qk_norm_g": g_qk_norm_g, "attn_sink": g_attn_sink, "rel_bias": g_rel_bias,
    }
    order = ["w_ada", "b_ada", "norm_g", "w_ffn1_in", "w_ffn1_out", "w_ffn2_in", "w_ffn2_out", "w_mix_in", "w_mix_out",
             "hgrn_lb", "hgrn_norm_g", "qk_norm_g", "attn_sink", "rel_bias"]
    return (loss, dx0[None], *[grads[k] for k in order], *[upd[k][0] for k in order], *[upd[k][1] for k in order],
            *[upd[k][2] for k in order])
```

```python
import functools
import math

import numpy as np
import jax
import jax.numpy as jnp
from jax import lax
from jax.experimental import pallas as pl
from jax.experimental.pallas import tpu as pltpu

F32, BF16 = jnp.float32, jnp.bfloat16

D_MODEL = 1024
D_FF = 2816
HG_HEADS, HG_DIM = 4, 128
HG_WIDTH = HG_HEADS * HG_DIM
ATT_Q_HEADS, ATT_KV_HEADS, ATT_HEAD_DIM = 8, 2, 64
ATT_GROUP = ATT_Q_HEADS // ATT_KV_HEADS
ATT_WIDTH = ATT_Q_HEADS * ATT_HEAD_DIM
KV_WIDTH = ATT_KV_HEADS * ATT_HEAD_DIM
WINDOW, BLOCK = 128, 128
NUM_BUCKETS, MAX_DISTANCE = 32, 128
N_MOD = 9
EPS = 1e-6
D_IN = 5 * HG_WIDTH + ATT_WIDTH + 2 * KV_WIDTH
ADAM_LR, ADAM_B1, ADAM_B2, ADAM_EPS, ADAM_WD, ADAM_STEP = 0.001, 0.9, 0.999, 1e-08, 0.01, 10

N_CHIPS = 4
FF_SHARD = 2 * D_FF // N_CHIPS
NEG = -1e30

VMEM_LIMIT_BYTES = 56 << 20
ROW_TILE = 512
HG_CHUNK = 16
HG_ROWS = 256

MESH = pl.DeviceIdType.MESH
ANY = pl.BlockSpec(memory_space=pl.ANY)


def _params(*sem):
    return pltpu.CompilerParams(dimension_semantics=sem, vmem_limit_bytes=VMEM_LIMIT_BYTES)


def _resident(shape, index_map):
    return pl.BlockSpec(shape, index_map, pipeline_mode=pl.Buffered(1))


def _dot(a, b, dims, precision=None):
    return lax.dot_general(a, b, (dims, ((), ())), precision=precision, preferred_element_type=F32)


def _nn(a, b, precision=None):
    return _dot(a, b, ((1,), (0,)), precision)


def _nt(a, b):
    return _dot(a, b, ((1,), (1,)))


def _tn(a, b):
    return _dot(a, b, ((0,), (0,)))


def _sigmoid(x):
    return jax.nn.sigmoid(x)


class _Exchange:
    def __init__(self, inputs, out_shapes, n_sems, plan, aliases=None):
        self.inputs, self.out_shapes, self.n_sems, self.plan, self.aliases = list(inputs), list(out_shapes), n_sems, plan, aliases or {}

    def sem_shapes(self):
        return [pltpu.SemaphoreType.DMA((self.n_sems,)), pltpu.SemaphoreType.DMA((self.n_sems,))]

    def start(self, in_refs, out_refs, send_sems, recv_sems):
        for cp in self.plan(in_refs, out_refs, send_sems, recv_sems)[0]:
            cp.start()

    def finish(self, in_refs, out_refs, send_sems, recv_sems):
        sends, recvs = self.plan(in_refs, out_refs, send_sems, recv_sems)
        for cp in recvs:
            cp.wait_recv()
        for cp in sends:
            cp.wait_send()


def _run_exchange(ex, name):
    n_in, n_out = len(ex.inputs), len(ex.out_shapes)

    def body(*refs):
        in_refs, out_refs, (send_sems, recv_sems) = refs[:n_in], refs[n_in:n_in + n_out], refs[n_in + n_out:]
        ex.start(in_refs, out_refs, send_sems, recv_sems)
        ex.finish(in_refs, out_refs, send_sems, recv_sems)

    return pl.pallas_call(
        body, name=name, in_specs=[ANY] * n_in, out_specs=[ANY] * n_out, out_shape=ex.out_shapes,
        scratch_shapes=ex.sem_shapes(), input_output_aliases=dict(ex.aliases),
    )(*ex.inputs)


def _call(body, *, name, grid, in_specs, out_specs, out_shape, args, semantics, scratch_shapes=(), exchange=None):
    if exchange is None:
        return pl.pallas_call(
            body, name=name, grid=grid, in_specs=in_specs, out_specs=out_specs, out_shape=out_shape,
            scratch_shapes=list(scratch_shapes), compiler_params=_params(*semantics))(*args)
    exs = exchange if isinstance(exchange, (list, tuple)) else [exchange]
    n_in, n_out, n_scr = len(in_specs), len(out_specs), len(scratch_shapes)
    x_in, x_out = [len(ex.inputs) for ex in exs], [len(ex.out_shapes) for ex in exs]

    def take(refs, counts):
        groups = []
        for n in counts:
            groups.append(refs[:n])
            refs = refs[n:]
        return groups, refs

    def carrier(*refs):
        ins, refs = refs[:n_in], refs[n_in:]
        x_ins, refs = take(refs, x_in)
        outs, refs = refs[:n_out], refs[n_out:]
        x_outs, refs = take(refs, x_out)
        scr, refs = refs[:n_scr], refs[n_scr:]
        sems, _ = take(refs, [2] * len(exs))
        ids = [pl.program_id(a) for a in range(len(grid))]
        first = functools.reduce(jnp.logical_and, [i == 0 for i in ids])
        last = functools.reduce(jnp.logical_and, [i == g - 1 for i, g in zip(ids, grid)])

        @pl.when(first)
        def _():
            for ex, xi, xo, (send_sems, recv_sems) in zip(exs, x_ins, x_outs, sems):
                ex.start(xi, xo, send_sems, recv_sems)

        body(*ins, *outs, *scr)

        @pl.when(last)
        def _():
            for ex, xi, xo, (send_sems, recv_sems) in zip(exs, x_ins, x_outs, sems):
                ex.finish(xi, xo, send_sems, recv_sems)

    aliases, i0, o0 = {}, n_in, n_out
    for ex in exs:
        aliases.update({i0 + i: o0 + o for i, o in ex.aliases.items()})
        i0, o0 = i0 + len(ex.inputs), o0 + len(ex.out_shapes)
    res = pl.pallas_call(
        carrier, name=name, grid=grid, in_specs=list(in_specs) + [ANY] * sum(x_in),
        out_specs=list(out_specs) + [ANY] * sum(x_out),
        out_shape=list(out_shape) + [s for ex in exs for s in ex.out_shapes],
        scratch_shapes=list(scratch_shapes) + [s for ex in exs for s in ex.sem_shapes()],
        input_output_aliases=aliases, compiler_params=_params(*["arbitrary"] * len(grid)),
    )(*args, *[a for ex in exs for a in ex.inputs])
    x_res, _ = take(list(res[n_out:]), x_out)
    return list(res[:n_out]), (x_res if isinstance(exchange, (list, tuple)) else x_res[0])


def _rmsmod_fwd(x, g, shift, scale, name):
    S, D = x.shape
    tr = min(ROW_TILE, S)

    def body(x_ref, g_ref, sh_ref, sc_ref, h_ref):
        xv = x_ref[...]
        rstd = lax.rsqrt(jnp.mean(xv * xv, axis=-1, keepdims=True) + EPS)
        y = xv * rstd * g_ref[...]
        h_ref[...] = (y * (1.0 + sc_ref[...]) + sh_ref[...]).astype(h_ref.dtype)

    row = pl.BlockSpec((tr, D), lambda i: (i, 0))
    vec = pl.BlockSpec((1, D), lambda i: (0, 0))
    return pl.pallas_call(
        body, name=name, grid=(S // tr,), in_specs=[row, vec, vec, vec], out_specs=row,
        out_shape=jax.ShapeDtypeStruct((S, D), BF16), compiler_params=_params("parallel"),
    )(x, g, shift, scale)


class _NormBwd:
    def __init__(self, x, g, scale, dx_res, below=None):
        S, D = x.shape
        self.below, self.coef = below, (below[2] if below else None)
        self.inputs = [x, g, scale, dx_res] + ([below[0], below[1]] if below else [])
        vshape = jax.ShapeDtypeStruct((1, D), F32)
        self.out_shape = [jax.ShapeDtypeStruct((S, D), F32), vshape, vshape, vshape]
        if below:
            self.out_shape += [jax.ShapeDtypeStruct((S, D), BF16), vshape]

    def specs(self, tr, D):
        row = pl.BlockSpec((tr, D), lambda i: (i, 0))
        vec = pl.BlockSpec((1, D), lambda i: (0, 0))
        return ([row, vec, vec, row] + ([row, vec] if self.below else []),
                [row, vec, vec, vec] + ([row, vec] if self.below else []))

    def step(self, dhv, in_refs, out_refs):
        if self.below:
            x_ref, g_ref, sc_ref, dxr_ref, f_ref, gate_ref = in_refs
            dx_ref, dsh_ref, dsc_ref, dg_ref, df_ref, dgate_ref = out_refs
            sums = (dsh_ref, dsc_ref, dg_ref, dgate_ref)
        else:
            x_ref, g_ref, sc_ref, dxr_ref = in_refs
            dx_ref, dsh_ref, dsc_ref, dg_ref = out_refs
            sums = (dsh_ref, dsc_ref, dg_ref)

        @pl.when(pl.program_id(0) == 0)
        def _():
            for ref in sums:
                ref[...] = jnp.zeros_like(ref)

        xv, gv = x_ref[...], g_ref[...]
        one_sc = 1.0 + sc_ref[...]
        rstd = lax.rsqrt(jnp.mean(xv * xv, axis=-1, keepdims=True) + EPS)
        n = xv * rstd
        dsh_ref[...] += jnp.sum(dhv, axis=0, keepdims=True)
        dsc_ref[...] += jnp.sum(dhv * n, axis=0, keepdims=True) * gv
        dg_ref[...] += jnp.sum(dhv * n, axis=0, keepdims=True) * one_sc
        dn = dhv * (gv * one_sc)
        dx = dxr_ref[...] + rstd * (dn - n * jnp.mean(dn * n, axis=-1, keepdims=True))
        dx_ref[...] = dx
        if self.below:
            df_ref[...] = (self.coef * gate_ref[...] * dx).astype(df_ref.dtype)
            dgate_ref[...] += self.coef * jnp.sum(dx * f_ref[...].astype(F32), axis=0, keepdims=True)


def _rmsmod_bwd(dh, norm, name, exchange=None):
    S, D = dh.shape
    tr = min(ROW_TILE, S)
    n_in = len(norm.inputs)

    def body(dh_ref, *refs):
        norm.step(dh_ref[...], refs[:n_in], refs[n_in:])

    in_specs, out_specs = norm.specs(tr, D)
    return _call(body, name=name, grid=(S // tr,), in_specs=[pl.BlockSpec((tr, D), lambda i: (i, 0))] + in_specs,
                 out_specs=out_specs, out_shape=norm.out_shape, args=[dh] + norm.inputs, semantics=("arbitrary",),
                 exchange=exchange)


def _ffn_in_fwd(h, w4, name, exchange=None):
    S, D = h.shape
    tm = min(ROW_TILE, S)
    n = w4.shape[2]

    def body(h_ref, wg_ref, wu_ref, zg_ref, zu_ref, a_ref):
        hv = h_ref[...]
        zg = _nn(hv, wg_ref[...])
        zu = _nn(hv, wu_ref[...])
        zg_ref[...] = zg.astype(zg_ref.dtype)
        zu_ref[...] = zu.astype(zu_ref.dtype)
        a_ref[...] = (zg * _sigmoid(zg) * zu).astype(a_ref.dtype)

    out = pl.BlockSpec((tm, n), lambda j, m: (m, j))
    oshape = jax.ShapeDtypeStruct((S, 2 * n), BF16)
    return _call(
        body, name=name, grid=(2, S // tm),
        in_specs=[pl.BlockSpec((tm, D), lambda j, m: (m, 0)),
                  pl.BlockSpec((None, D, n), lambda j, m: (j, 0, 0)),
                  pl.BlockSpec((None, D, n), lambda j, m: (j + 2, 0, 0))],
        out_specs=[out, out, out], out_shape=[oshape, oshape, oshape], args=(h, w4, w4),
        semantics=("parallel", "parallel"), exchange=exchange)


def _proj_out_fwd(lhs, w, x, gate, coef, name, exchange=None, next_norm=None):
    S, D = x.shape
    tm = min(ROW_TILE, S)
    ks = [a.shape[1] for a in lhs]

    def body(*refs):
        lhs_refs, refs = refs[:len(lhs)], refs[len(lhs):]
        if next_norm:
            w_ref, x_ref, gate_ref, g_ref, sh_ref, sc_ref, xn_ref, f_ref, h_ref = refs
        else:
            w_ref, x_ref, gate_ref, xn_ref, f_ref = refs
        acc, off = None, 0
        for a_ref, k in zip(lhs_refs, ks):
            part = _nn(a_ref[...], w_ref[off:off + k, :])
            acc = part if acc is None else acc + part
            off += k
        f_ref[...] = acc.astype(f_ref.dtype)
        xn = x_ref[...] + coef * gate_ref[...] * acc
        xn_ref[...] = xn
        if next_norm:
            rstd = lax.rsqrt(jnp.mean(xn * xn, axis=-1, keepdims=True) + EPS)
            h_ref[...] = (xn * rstd * g_ref[...] * (1.0 + sc_ref[...]) + sh_ref[...]).astype(h_ref.dtype)

    row = pl.BlockSpec((tm, D), lambda m: (m, 0))
    vec = pl.BlockSpec((1, D), lambda m: (0, 0))
    extra = list(next_norm) if next_norm else []
    return _call(
        body, name=name, grid=(S // tm,),
        in_specs=[pl.BlockSpec((tm, k), lambda m: (m, 0)) for k in ks]
        + [_resident(w.shape, lambda m: (0, 0)), row, vec] + [vec] * len(extra),
        out_specs=[row, row] + ([row] if next_norm else []),
        out_shape=[jax.ShapeDtypeStruct((S, D), F32), jax.ShapeDtypeStruct((S, D), BF16)]
        + ([jax.ShapeDtypeStruct((S, D), BF16)] if next_norm else []),
        args=(*lhs, w, x, gate, *extra), semantics=("parallel",), exchange=exchange)


def _proj_out_loss(lhs, w, x, gate, coef, target, name):
    S, D = x.shape
    tm = min(ROW_TILE, S)

    def body(a_ref, w_ref, x_ref, gate_ref, t_ref, dy_ref, df_ref, dgate_ref, sq_ref):
        @pl.when(pl.program_id(0) == 0)
        def _():
            dgate_ref[...] = jnp.zeros_like(dgate_ref)
            sq_ref[...] = jnp.zeros_like(sq_ref)

        f = _nn(a_ref[...], w_ref[...])
        gate = coef * gate_ref[...]
        err = x_ref[...] + gate * f - t_ref[...]
        sq_ref[...] += jnp.sum(err * err, axis=0, keepdims=True)
        dy = err * (1.0 / D)
        dy_ref[...] = dy
        df_ref[...] = (gate * dy).astype(df_ref.dtype)
        dgate_ref[...] += coef * jnp.sum(dy * f, axis=0, keepdims=True)

    row = pl.BlockSpec((tm, D), lambda m: (m, 0))
    vec = pl.BlockSpec((1, D), lambda m: (0, 0))
    vshape = jax.ShapeDtypeStruct((1, D), F32)
    return pl.pallas_call(
        body, name=name, grid=(S // tm,),
        in_specs=[pl.BlockSpec((tm, lhs.shape[1]), lambda m: (m, 0)), _resident(w.shape, lambda m: (0, 0)), row, vec, row],
        out_specs=[row, row, vec, vec],
        out_shape=[jax.ShapeDtypeStruct((S, D), F32), jax.ShapeDtypeStruct((S, D), BF16), vshape, vshape],
        compiler_params=_params("arbitrary"),
    )(lhs, w, x, gate, target)


def _matmul_nn(a, w, out_dtype, tm, name):
    S, K = a.shape
    N = w.shape[1]
    tm = min(tm, S)

    def body(a_ref, w_ref, o_ref):
        o_ref[...] = _nn(a_ref[...], w_ref[...]).astype(o_ref.dtype)

    return pl.pallas_call(
        body, name=name, grid=(S // tm,),
        in_specs=[pl.BlockSpec((tm, K), lambda m: (m, 0)), _resident((K, N), lambda m: (0, 0))],
        out_specs=pl.BlockSpec((tm, N), lambda m: (m, 0)), out_shape=jax.ShapeDtypeStruct((S, N), out_dtype),
        compiler_params=_params("parallel"),
    )(a, w)


def _dact_bwd(df, w_out, zg, zu, name, exchange=None):
    S, D = df.shape
    tm = min(ROW_TILE, S)
    n = w_out.shape[0] // 2

    def body(df_ref, w_ref, zg_ref, zu_ref, dzg_ref, dzu_ref):
        da = _nt(df_ref[...], w_ref[...])
        zg_v, zu_v = zg_ref[...].astype(F32), zu_ref[...].astype(F32)
        s = _sigmoid(zg_v)
        dzu_ref[...] = (da * zg_v * s).astype(dzu_ref.dtype)
        dzg_ref[...] = (da * zu_v * (s * (1.0 + zg_v * (1.0 - s)))).astype(dzg_ref.dtype)

    blk = pl.BlockSpec((tm, n), lambda j, m: (m, j))
    oshape = jax.ShapeDtypeStruct((S, 2 * n), BF16)
    return _call(
        body, name=name, grid=(2, S // tm),
        in_specs=[pl.BlockSpec((tm, D), lambda j, m: (m, 0)), pl.BlockSpec((n, D), lambda j, m: (j, 0)), blk, blk],
        out_specs=[blk, blk], out_shape=[oshape, oshape], args=(df, w_out, zg, zu), semantics=("parallel", "parallel"),
        exchange=exchange)


def _ffn_in_dgrad(dzg, dzu, w4, name, exchange=None, norm=None):
    S = dzg.shape[0]
    D, n = w4.shape[1], w4.shape[2]
    tm = min(ROW_TILE, S)
    n_norm = len(norm.inputs) if norm else 0

    def body(dzg_ref, dzu_ref, w_ref, *refs):
        acc = _nt(dzg_ref[:, 0:n], w_ref[0])
        acc += _nt(dzg_ref[:, n:2 * n], w_ref[1])
        acc += _nt(dzu_ref[:, 0:n], w_ref[2])
        acc += _nt(dzu_ref[:, n:2 * n], w_ref[3])
        if norm:
            norm.step(acc, refs[:n_norm], refs[n_norm:])
        else:
            refs[0][...] = acc

    blk = pl.BlockSpec((tm, 2 * n), lambda m: (m, 0))
    in_specs, args = [blk, blk, _resident(w4.shape, lambda m: (0, 0, 0))], [dzg, dzu, w4]
    out_specs, out_shape = [pl.BlockSpec((tm, D), lambda m: (m, 0))], [jax.ShapeDtypeStruct((S, D), F32)]
    if norm:
        norm_in, out_specs = norm.specs(tm, D)
        in_specs, args, out_shape = in_specs + norm_in, args + norm.inputs, norm.out_shape
    return _call(body, name=name, grid=(S // tm,), in_specs=in_specs, out_specs=out_specs, out_shape=out_shape, args=args,
                 semantics=("arbitrary",) if norm else ("parallel",), exchange=exchange)


def _matmul_nt(pieces, w, tm, name, exchange=None, norm=None):
    S = pieces[0].shape[0]
    ks = [p.shape[1] for p in pieces]
    N = w.shape[0]
    tm = min(tm, S)
    n_norm = len(norm.inputs) if norm else 0

    def body(*refs):
        p_refs, w_ref, refs = refs[:len(ks)], refs[len(ks)], refs[len(ks) + 1:]
        acc, off = None, 0
        for p_ref, k in zip(p_refs, ks):
            part = _nt(p_ref[...], w_ref[:, off:off + k])
            acc = part if acc is None else acc + part
            off += k
        if norm:
            norm.step(acc, refs[:n_norm], refs[n_norm:])
        else:
            refs[0][...] = acc

    in_specs = [pl.BlockSpec((tm, k), lambda m: (m, 0)) for k in ks] + [_resident(w.shape, lambda m: (0, 0))]
    args = list(pieces) + [w]
    out_specs, out_shape = [pl.BlockSpec((tm, N), lambda m: (m, 0))], [jax.ShapeDtypeStruct((S, N), F32)]
    if norm:
        norm_in, out_specs = norm.specs(tm, N)
        in_specs, args, out_shape = in_specs + norm_in, args + norm.inputs, norm.out_shape
    return _call(body, name=name, grid=(S // tm,), in_specs=in_specs, out_specs=out_specs, out_shape=out_shape, args=args,
                 semantics=("arbitrary",) if norm else ("parallel",), exchange=exchange)


def _wgrad(a, gs, tn, name, exchange=None):
    S, Ka = a.shape
    N = gs[0].shape[1]
    ts = min(ROW_TILE * (2 if Ka <= D_MODEL else 1), S)

    def body(a_ref, *refs):
        g_refs, o_ref = refs[:-1], refs[-1]

        @pl.when(pl.program_id(1) == 0)
        def _():
            o_ref[...] = jnp.zeros_like(o_ref)

        a_t = a_ref[...].T
        for i, g_ref in enumerate(g_refs):
            o_ref[i] += _nn(a_t, g_ref[...])

    return _call(
        body, name=name, grid=(N // tn, S // ts),
        in_specs=[pl.BlockSpec((ts, Ka), lambda j, s: (s, 0))] + [pl.BlockSpec((ts, tn), lambda j, s: (s, j))] * len(gs),
        out_specs=[pl.BlockSpec((len(gs), None, Ka, tn), lambda j, s: (0, j, 0, 0))],
        out_shape=[jax.ShapeDtypeStruct((len(gs), N // tn, Ka, tn), F32)], args=(a, *gs),
        semantics=("parallel", "arbitrary"), exchange=exchange)


def _wgrad_pieces(a, pieces, tn, name):
    S, Ka = a.shape
    ts = min(ROW_TILE, S)
    blocks = [(i, j) for i, p in enumerate(pieces) for j in range(p.shape[1] // tn)]

    def body(a_ref, *refs):
        g_refs, o_ref = refs[:-1], refs[-1]

        @pl.when(pl.program_id(0) == 0)
        def _():
            o_ref[...] = jnp.zeros_like(o_ref)

        a_t = a_ref[...].T
        for b, g_ref in enumerate(g_refs):
            o_ref[b] += _nn(a_t, g_ref[...])

    return pl.pallas_call(
        body, name=name, grid=(S // ts,),
        in_specs=[pl.BlockSpec((ts, Ka), lambda s: (s, 0))] + [pl.BlockSpec((ts, tn), lambda s, j=j: (s, j)) for _, j in blocks],
        out_specs=pl.BlockSpec((len(blocks), Ka, tn), lambda s: (0, 0, 0)),
        out_shape=jax.ShapeDtypeStruct((len(blocks), Ka, tn), F32), compiler_params=_params("arbitrary"),
    )(a, *[pieces[i] for i, _ in blocks])


def _wgrad_rows(lhs, g, name):
    S, Ka = lhs[0].shape
    N = g.shape[1]
    ts = min(ROW_TILE, S)

    def body(*refs):
        a_refs, g_ref, o_ref = refs[:-2], refs[-2], refs[-1]

        @pl.when(pl.program_id(0) == 0)
        def _():
            o_ref[...] = jnp.zeros_like(o_ref)

        gv = g_ref[...]
        for i, a_ref in enumerate(a_refs):
            o_ref[i] += _tn(a_ref[...], gv)

    return pl.pallas_call(
        body, name=name, grid=(S // ts,),
        in_specs=[pl.BlockSpec((ts, Ka), lambda s: (s, 0))] * len(lhs) + [pl.BlockSpec((ts, N), lambda s: (s, 0))],
        out_specs=pl.BlockSpec((len(lhs), Ka, N), lambda s: (0, 0, 0)),
        out_shape=jax.ShapeDtypeStruct((len(lhs), Ka, N), F32), compiler_params=_params("arbitrary"),
    )(*lhs, g)


def _hgrn_chunk_common(qr, fr, oml, tri, last):
    k = oml * _sigmoid(-fr)
    g = jnp.log1p(-k) * math.log2(math.e)
    q = qr * _sigmoid(qr)
    G = _nn(tri, g, precision=lax.Precision.HIGHEST)
    Gl = G[last:last + 1]
    return q, k, G, Gl


def _hgrn_consts(reverse):
    C = HG_CHUNK
    r = lax.broadcasted_iota(jnp.int32, (C, C), 0)
    cc = lax.broadcasted_iota(jnp.int32, (C, C), 1)
    tri = ((cc >= r) if reverse else (cc <= r)).astype(F32)
    tri_t = ((cc <= r) if reverse else (cc >= r)).astype(F32)
    rid = lax.broadcasted_iota(jnp.int32, (C, HG_WIDTH), 0)
    return tri, tri_t, rid, (0 if reverse else C - 1)


def _head_slices():
    return [slice(h * HG_DIM, (h + 1) * HG_DIM) for h in range(HG_HEADS)]


def _per_head_lane_sum(x):
    C = x.shape[0]
    return jnp.concatenate(
        [jnp.broadcast_to(jnp.sum(x[:, sl], axis=-1, keepdims=True), (C, HG_DIM)) for sl in _head_slices()], axis=1)


HG_TILE = 8


def _pair_tiles(s, reverse):
    blk, r = divmod(s, HG_TILE)
    n_tiles = HG_CHUNK // HG_TILE
    others = range(0, blk) if reverse else range(blk + 1, n_tiles)
    return [(blk, r)] + [(t, None) for t in others]


def _pair_decay(G, s, tile, r, rid8, reverse, keys=False):
    rs = slice(tile * HG_TILE, (tile + 1) * HG_TILE)
    d = (G[s:s + 1] - G[rs]) if keys else (G[rs] - G[s:s + 1])
    if r is not None:
        d = jnp.where((rid8 <= r) if reverse else (rid8 >= r), d, NEG)
    return rs, jnp.exp2(d)


def _hgrn_fwd(z, lb, direction, name, exchange=None):
    S = z.shape[0]
    C, DK, W = HG_CHUNK, HG_DIM, HG_WIDTH
    tb = min(HG_ROWS, S)
    n_t, n_c = S // tb, tb // C
    reverse = direction == 1
    tmap = (lambda i: n_t - 1 - i) if reverse else (lambda i: i)

    def body(q_ref, f_ref, v_ref, lb_ref, o_ref, st_out_ref, st_ref):
        @pl.when(pl.program_id(0) == 0)
        def _():
            st_ref[...] = jnp.zeros_like(st_ref)

        oml = 1.0 - lb_ref[...]
        tri, _, _, last = _hgrn_consts(reverse)
        rid8 = lax.broadcasted_iota(jnp.int32, (HG_TILE, W), 0)

        def chunk(ci, carry):
            cidx = (n_c - 1 - ci) if reverse else ci
            rows = pl.ds(pl.multiple_of(cidx * C, C), C)
            v = v_ref[rows, :]
            q, k, G, Gl = _hgrn_chunk_common(q_ref[rows, :], f_ref[rows, :], oml, tri, last)
            qd = (q * jnp.exp2(G)).astype(BF16)
            kd = (k * jnp.exp2(Gl - G)).astype(BF16)
            e_gl = jnp.exp2(Gl)
            v_b = v.astype(BF16)
            inter = []
            for h, sl in enumerate(_head_slices()):
                st0 = st_ref[h]
                st_out_ref[h, cidx] = st0
                inter.append(_nt(qd[:, sl], st0.astype(BF16)))
                st_ref[h] = st0 * e_gl[:, sl] + _tn(v_b[:, sl], kd[:, sl])
            o = jnp.concatenate(inter, axis=1)
            o_t = [o[t * HG_TILE:(t + 1) * HG_TILE] for t in range(C // HG_TILE)]
            for s in range(C):
                k_s, v_s = k[s:s + 1], v[s:s + 1]
                for tile, r in _pair_tiles(s, reverse):
                    rs, e_s = _pair_decay(G, s, tile, r, rid8, reverse)
                    o_t[tile] = o_t[tile] + _per_head_lane_sum(q[rs] * k_s * e_s) * v_s
            o_ref[rows, :] = jnp.concatenate(o_t, axis=0)
            return carry

        lax.fori_loop(0, n_c, chunk, 0, unroll=8)

    def sec(j):
        return pl.BlockSpec((tb, W), lambda i: (tmap(i), j))

    return _call(
        body, name=name, grid=(n_t,),
        in_specs=[sec(0), sec(1 + direction), sec(3), pl.BlockSpec((1, W), lambda i: (0, 0))],
        out_specs=[sec(0), pl.BlockSpec((HG_HEADS, n_c, DK, DK), lambda i: (0, tmap(i), 0, 0))],
        out_shape=[jax.ShapeDtypeStruct((S, W), F32), jax.ShapeDtypeStruct((HG_HEADS, S // C, DK, DK), F32)],
        scratch_shapes=[pltpu.VMEM((HG_HEADS, DK, DK), F32)], args=(z, z, z, lb), semantics=("arbitrary",),
        exchange=exchange)


def _hgrn_bwd(z, lb, do, states, direction, name, acc=None, exchange=None):
    S = z.shape[0]
    C, DK, W = HG_CHUNK, HG_DIM, HG_WIDTH
    tb = min(HG_ROWS, S)
    n_t, n_c = S // tb, tb // C
    reverse = direction == 1
    tmap = (lambda i: i) if reverse else (lambda i: n_t - 1 - i)

    def body(*refs):
        if acc:
            q_ref, f_ref, v_ref, lb_ref, do_ref, st_in_ref, dqa_ref, dva_ref, dq_ref, df_ref, dv_ref, doml_ref, dst_ref = refs
        else:
            q_ref, f_ref, v_ref, lb_ref, do_ref, st_in_ref, dq_ref, df_ref, dv_ref, doml_ref, dst_ref = refs

        @pl.when(pl.program_id(0) == 0)
        def _():
            dst_ref[...] = jnp.zeros_like(dst_ref)
            doml_ref[...] = jnp.zeros_like(doml_ref)

        oml = 1.0 - lb_ref[...]
        tri, tri_t, rid, last = _hgrn_consts(reverse)
        rid8 = lax.broadcasted_iota(jnp.int32, (HG_TILE, W), 0)

        def chunk(ci, carry):
            cidx = ci if reverse else (n_c - 1 - ci)
            rows = pl.ds(pl.multiple_of(cidx * C, C), C)
            qr, fr, v, dov = q_ref[rows, :], f_ref[rows, :], v_ref[rows, :], do_ref[rows, :]
            q, k, G, Gl = _hgrn_chunk_common(qr, fr, oml, tri, last)
            e_g, e_gl, e_kd = jnp.exp2(G), jnp.exp2(Gl), jnp.exp2(Gl - G)
            qd, kd = q * e_g, k * e_kd
            do_b, v_b, qd_b, kd_b = dov.astype(BF16), v.astype(BF16), qd.astype(BF16), kd.astype(BF16)
            dqd, dkd, dv, state_dot = [], [], [], []
            for h, sl in enumerate(_head_slices()):
                st0, dst1 = st_in_ref[h, cidx], dst_ref[h]
                dst1_b = dst1.astype(BF16)
                dqd.append(_nn(do_b[:, sl], st0.astype(BF16)))
                dkd.append(_nn(v_b[:, sl], dst1_b))
                dv.append(_nt(kd_b[:, sl], dst1_b))
                state_dot.append(jnp.sum(st0 * dst1, axis=0, keepdims=True))
                dst_ref[h] = dst1 * e_gl[:, sl] + _tn(do_b[:, sl], qd_b[:, sl])
            dqd, dkd, dv = [jnp.concatenate(t, axis=1) for t in (dqd, dkd, dv)]
            d_gl = e_gl * jnp.concatenate(state_dot, axis=1) + jnp.sum(dkd * kd, axis=0, keepdims=True)
            dq, dk = dqd * e_g, dkd * e_kd
            n_tiles = C // HG_TILE
            dq_t, dk_t, dv_t = [[x[t * HG_TILE:(t + 1) * HG_TILE] for t in range(n_tiles)] for x in (dq, dk, dv)]
            for s in range(C):
                k_s, v_s = k[s:s + 1], v[s:s + 1]
                for tile, r in _pair_tiles(s, reverse):
                    rs, e_s = _pair_decay(G, s, tile, r, rid8, reverse)
                    dq_t[tile] = dq_t[tile] + _per_head_lane_sum(dov[rs] * v_s) * e_s * k_s
            for t in range(C):
                q_t, do_t = q[t:t + 1], dov[t:t + 1]
                for tile, r in _pair_tiles(t, not reverse):
                    rs, x_t = _pair_decay(G, t, tile, r, rid8, not reverse, keys=True)
                    qx = q_t * x_t
                    dv_t[tile] = dv_t[tile] + _per_head_lane_sum(k[rs] * qx) * do_t
                    dk_t[tile] = dk_t[tile] + _per_head_lane_sum(v[rs] * do_t) * qx
            dq, dk, dv = [jnp.concatenate(x, axis=0) for x in (dq_t, dk_t, dv_t)]
            d_big_g = dq * q - dk * k + jnp.where(rid == last, d_gl, 0.0)
            dg = _nn(tri_t, d_big_g, precision=lax.Precision.HIGHEST)
            dk_all = dk - dg / (1.0 - k)
            sig_nf = _sigmoid(-fr)
            df_ref[rows, :] = (-dk_all * k * (1.0 - sig_nf)).astype(df_ref.dtype)
            doml_ref[...] += jnp.sum(dk_all * sig_nf, axis=0, keepdims=True)
            sq = _sigmoid(qr)
            dqr = dq * (sq * (1.0 + qr * (1.0 - sq)))
            if acc:
                dqr = dqr + dqa_ref[rows, :]
                dv = dv + dva_ref[rows, :]
            dq_ref[rows, :] = dqr.astype(dq_ref.dtype)
            dv_ref[rows, :] = dv.astype(dv_ref.dtype)
            return carry

        lax.fori_loop(0, n_c, chunk, 0, unroll=8)

    def sec(j):
        return pl.BlockSpec((tb, W), lambda i: (tmap(i), j))

    vec = pl.BlockSpec((1, W), lambda i: (0, 0))
    ins = [z, z, z, lb, do, states]
    in_specs = [sec(0), sec(1 + direction), sec(3), vec, sec(0),
                pl.BlockSpec((HG_HEADS, n_c, DK, DK), lambda i: (0, tmap(i), 0, 0))]
    if acc:
        ins += list(acc)
        in_specs += [sec(0), sec(0)]
    final = jax.ShapeDtypeStruct((S, W), BF16)
    partial = final if acc else jax.ShapeDtypeStruct((S, W), F32)
    return _call(
        body, name=name, grid=(n_t,), in_specs=in_specs,
        out_specs=[sec(0), sec(0), sec(0), vec],
        out_shape=[partial, final, partial, jax.ShapeDtypeStruct((1, W), F32)],
        scratch_shapes=[pltpu.VMEM((HG_HEADS, DK, DK), F32)], args=ins, semantics=("arbitrary",), exchange=exchange)


def _hgrn_post_fwd(o_f, o_b, z, norm_g, name):
    S = z.shape[0]
    tr = min(ROW_TILE, S)

    def body(of_ref, ob_ref, gr_ref, ng_ref, y_ref):
        o = of_ref[...] + ob_ref[...]
        gr = gr_ref[...]
        gate = gr * _sigmoid(gr)
        ng = ng_ref[...]
        for h in range(HG_HEADS):
            sl = slice(h * HG_DIM, (h + 1) * HG_DIM)
            oh = o[:, sl]
            rstd = lax.rsqrt(jnp.mean(oh * oh, axis=-1, keepdims=True) + EPS)
            y_ref[:, sl] = (oh * rstd * ng[:, sl] * gate[:, sl]).astype(y_ref.dtype)

    row = pl.BlockSpec((tr, HG_WIDTH), lambda i: (i, 0))
    return pl.pallas_call(
        body, name=name, grid=(S // tr,),
        in_specs=[row, row, pl.BlockSpec((tr, HG_WIDTH), lambda i: (i, 4)), pl.BlockSpec((1, HG_WIDTH), lambda i: (0, 0))],
        out_specs=row, out_shape=jax.ShapeDtypeStruct((S, HG_WIDTH), BF16), compiler_params=_params("parallel"),
    )(o_f, o_b, z, norm_g)


def _hgrn_post_bwd(dy, o_f, o_b, z, norm_g, name):
    S = z.shape[0]
    tr = min(ROW_TILE, S)

    def body(dy_ref, of_ref, ob_ref, gr_ref, ng_ref, do_ref, dgr_ref, dng_ref):
        @pl.when(pl.program_id(0) == 0)
        def _():
            dng_ref[...] = jnp.zeros_like(dng_ref)

        o = of_ref[...] + ob_ref[...]
        gr, ng, dyv = gr_ref[...], ng_ref[...], dy_ref[...]
        sg = _sigmoid(gr)
        for h in range(HG_HEADS):
            sl = slice(h * HG_DIM, (h + 1) * HG_DIM)
            oh, dyh, grh, sgh, ngh = o[:, sl], dyv[:, sl], gr[:, sl], sg[:, sl], ng[:, sl]
            rstd = lax.rsqrt(jnp.mean(oh * oh, axis=-1, keepdims=True) + EPS)
            on = oh * rstd
            du = dyh * (grh * sgh)
            dgr_ref[:, sl] = (dyh * (on * ngh) * (sgh * (1.0 + grh * (1.0 - sgh)))).astype(dgr_ref.dtype)
            dng_ref[:, sl] += jnp.sum(du * on, axis=0, keepdims=True)
            don = du * ngh
            do_ref[:, sl] = rstd * (don - on * jnp.mean(don * on, axis=-1, keepdims=True))

    row = pl.BlockSpec((tr, HG_WIDTH), lambda i: (i, 0))
    vec = pl.BlockSpec((1, HG_WIDTH), lambda i: (0, 0))
    full = jax.ShapeDtypeStruct((S, HG_WIDTH), F32)
    return pl.pallas_call(
        body, name=name, grid=(S // tr,),
        in_specs=[row, row, row, pl.BlockSpec((tr, HG_WIDTH), lambda i: (i, 4)), vec],
        out_specs=[row, row, vec],
        out_shape=[full, jax.ShapeDtypeStruct((S, HG_WIDTH), BF16), jax.ShapeDtypeStruct((1, HG_WIDTH), F32)],
        compiler_params=_params("arbitrary"),
    )(dy, o_f, o_b, z, norm_g)


def _t5_bucket_table():
    rel = (np.arange(3 * BLOCK)[None, :] - BLOCK) - np.arange(BLOCK)[:, None]
    nb = NUM_BUCKETS // 2
    max_exact = nb // 2
    ret = (rel > 0).astype(np.int32) * nb
    n = np.abs(rel)
    ratio = np.log(np.maximum(n, 1).astype(np.float32) / np.float32(max_exact)) / np.float32(math.log(MAX_DISTANCE / max_exact))
    large = max_exact + (ratio.astype(np.float32) * np.float32(nb - max_exact)).astype(np.int32)
    large = np.minimum(large, nb - 1)
    bucket = ret + np.where(n < max_exact, n, large)
    return bucket.astype(np.int32), (n <= WINDOW)


def _bias_table(rel_bias, name):
    bucket, in_band = _t5_bucket_table()
    idx = jnp.asarray(np.where(in_band, bucket, -1))

    def body(rb_ref, idx_ref, o_ref):
        h = pl.program_id(0)
        iv = idx_ref[...]
        acc = jnp.where(iv < 0, NEG, 0.0).astype(F32)
        for b in range(NUM_BUCKETS):
            acc = acc + jnp.where(iv == b, rb_ref[b, h], 0.0)
        o_ref[...] = acc

    return pl.pallas_call(
        body, name=name, grid=(ATT_Q_HEADS,),
        in_specs=[pl.BlockSpec(memory_space=pltpu.SMEM), pl.BlockSpec((BLOCK, 3 * BLOCK), lambda h: (0, 0))],
        out_specs=pl.BlockSpec((None, BLOCK, 3 * BLOCK), lambda h: (h, 0, 0)),
        out_shape=jax.ShapeDtypeStruct((ATT_Q_HEADS, BLOCK, 3 * BLOCK), F32), compiler_params=_params("parallel"),
    )(rel_bias, idx)


def _bias_grad(ds_sum, name):
    bucket, in_band = _t5_bucket_table()
    idx = jnp.asarray(np.where(in_band, bucket, -1))

    def body(ds_ref, idx_ref, o_ref):
        iv, ds = idx_ref[...], ds_ref[...]
        for b in range(NUM_BUCKETS):
            part = jnp.sum(jnp.where(iv == b, ds, 0.0), axis=0, keepdims=True)
            o_ref[b:b + 1, :] = part[:, 0:BLOCK] + part[:, BLOCK:2 * BLOCK] + part[:, 2 * BLOCK:3 * BLOCK]

    return pl.pallas_call(
        body, name=name, grid=(ATT_Q_HEADS,),
        in_specs=[pl.BlockSpec((None, BLOCK, 3 * BLOCK), lambda h: (h, 0, 0)), pl.BlockSpec((BLOCK, 3 * BLOCK), lambda h: (0, 0))],
        out_specs=pl.BlockSpec((None, NUM_BUCKETS, BLOCK), lambda h: (h, 0, 0)),
        out_shape=jax.ShapeDtypeStruct((ATT_Q_HEADS, NUM_BUCKETS, BLOCK), F32), compiler_params=_params("parallel"),
    )(ds_sum, idx)


Q_COL = 5 * HG_WIDTH
KV_COL = Q_COL + ATT_WIDTH
GROUP_WIDTH = ATT_GROUP * ATT_HEAD_DIM


def _stack_heads(blk):
    dh = ATT_HEAD_DIM
    return jnp.concatenate([blk[:, g * dh:(g + 1) * dh] for g in range(ATT_GROUP)], axis=0)


def _unstack_heads(st):
    return jnp.concatenate([st[g * BLOCK:(g + 1) * BLOCK] for g in range(ATT_GROUP)], axis=1)


def _rms_rows(x):
    rstd = lax.rsqrt(jnp.mean(x * x, axis=-1, keepdims=True) + EPS)
    return x * rstd, rstd


def _edge_ok(n, nb):
    colid = lax.broadcasted_iota(jnp.int32, (ATT_GROUP * BLOCK, 3 * BLOCK), 1)
    return jnp.logical_and(jnp.logical_or(colid >= BLOCK, n > 0), jnp.logical_or(colid < 2 * BLOCK, n < nb - 1))


def _sink_column(sink_ref, j=0):
    heads = range(j * ATT_GROUP, (j + 1) * ATT_GROUP)
    return jnp.concatenate([jnp.broadcast_to(sink_ref[h][:, 0:1], (BLOCK, 1)) for h in heads], axis=0)


def _attn_fwd(z, q_g, k_g, sink, bias, name):
    S = z.shape[0]
    nb = S // BLOCK
    G, dh, KV = ATT_GROUP, ATT_HEAD_DIM, ATT_KV_HEADS
    scale = 1.0 / math.sqrt(dh)

    def body(q_ref, kv0, kv1, kv2, qg_ref, kg_ref, sink_ref, bias_ref, o_ref):
        n = pl.program_id(0)
        edge_ok = _edge_ok(n, nb)
        cat = jnp.concatenate([kv0[...], kv1[...], kv2[...]], axis=0)
        qblk = q_ref[...]
        kn = [(_rms_rows(cat[:, j * dh:(j + 1) * dh])[0] * kg_ref[...]).astype(BF16) for j in range(KV)]
        vb = [cat[:, (KV + j) * dh:(KV + j + 1) * dh].astype(BF16) for j in range(KV)]
        qn = [(_rms_rows(_stack_heads(qblk[:, j * GROUP_WIDTH:(j + 1) * GROUP_WIDTH]))[0] * (qg_ref[...] * scale)).astype(BF16)
              for j in range(KV)]
        s = [_nt(qn[j], kn[j]) + bias_ref[j * G:(j + 1) * G].reshape(G * BLOCK, 3 * BLOCK) for j in range(KV)]
        s = [jnp.where(edge_ok, sj, NEG) for sj in s]
        sinks = [_sink_column(sink_ref, j) for j in range(KV)]
        m = [jnp.maximum(jnp.max(s[j], axis=-1, keepdims=True), sinks[j]) for j in range(KV)]
        e = [jnp.exp(s[j] - m[j]) for j in range(KV)]
        den = [jnp.sum(e[j], axis=-1, keepdims=True) + jnp.exp(sinks[j] - m[j]) for j in range(KV)]
        o = [_nn(e[j].astype(BF16), vb[j]) * (1.0 / den[j]) for j in range(KV)]
        o_ref[...] = jnp.concatenate([_unstack_heads(oj) for oj in o], axis=1).astype(o_ref.dtype)

    def kv(shift):
        return pl.BlockSpec((BLOCK, 2 * KV_WIDTH), lambda n: (jnp.clip(n + shift, 0, nb - 1), KV_COL // (2 * KV_WIDTH)))

    gain = pl.BlockSpec((1, dh), lambda n: (0, 0))
    return pl.pallas_call(
        body, name=name, grid=(nb,),
        in_specs=[pl.BlockSpec((BLOCK, ATT_WIDTH), lambda n: (n, Q_COL // ATT_WIDTH)), kv(-1), kv(0), kv(1), gain, gain,
                  pl.BlockSpec((ATT_Q_HEADS, 1, BLOCK), lambda n: (0, 0, 0)),
                  pl.BlockSpec((ATT_Q_HEADS, BLOCK, 3 * BLOCK), lambda n: (0, 0, 0))],
        out_specs=pl.BlockSpec((BLOCK, ATT_WIDTH), lambda n: (n, 0)),
        out_shape=jax.ShapeDtypeStruct((S, ATT_WIDTH), BF16), compiler_params=_params("parallel"),
    )(z, z, z, z, q_g, k_g, sink, bias)


def _attn_bwd(z, q_g, k_g, sink, bias, do, name):
    S = z.shape[0]
    nb = S // BLOCK
    G, dh, KV = ATT_GROUP, ATT_HEAD_DIM, ATT_KV_HEADS
    scale = 1.0 / math.sqrt(dh)
    both = range(KV)

    def body(q_ref, kv0, kv1, kv2, qg_ref, kg_ref, sink_ref, bias_ref, do_ref,
             dq_ref, dkw_ref, dvw_ref, ds_ref, dsink_ref, dqg_ref):
        n = pl.program_id(0)

        @pl.when(n == 0)
        def _():
            ds_ref[...] = jnp.zeros_like(ds_ref)
            dsink_ref[...] = jnp.zeros_like(dsink_ref)
            dqg_ref[...] = jnp.zeros_like(dqg_ref)

        edge_ok = _edge_ok(n, nb)
        qg = qg_ref[...]
        cat = jnp.concatenate([kv0[...], kv1[...], kv2[...]], axis=0)
        qblk, doblk = q_ref[...], do_ref[...]
        kn = [(_rms_rows(cat[:, j * dh:(j + 1) * dh])[0] * kg_ref[...]).astype(BF16) for j in both]
        vb = [cat[:, (KV + j) * dh:(KV + j + 1) * dh].astype(BF16) for j in both]
        norm = [_rms_rows(_stack_heads(qblk[:, j * GROUP_WIDTH:(j + 1) * GROUP_WIDTH])) for j in both]
        qn = [(norm[j][0] * (qg * scale)).astype(BF16) for j in both]
        do_b = [_stack_heads(doblk[:, j * GROUP_WIDTH:(j + 1) * GROUP_WIDTH]).astype(BF16) for j in both]
        s = [_nt(qn[j], kn[j]) + bias_ref[j * G:(j + 1) * G].reshape(G * BLOCK, 3 * BLOCK) for j in both]
        dp = [_nt(do_b[j], vb[j]) for j in both]
        s = [jnp.where(edge_ok, sj, NEG) for sj in s]
        sinks = [_sink_column(sink_ref, j) for j in both]
        m = [jnp.maximum(jnp.max(s[j], axis=-1, keepdims=True), sinks[j]) for j in both]
        e = [jnp.exp(s[j] - m[j]) for j in both]
        e_sink = [jnp.exp(sinks[j] - m[j]) for j in both]
        inv = [1.0 / (jnp.sum(e[j], axis=-1, keepdims=True) + e_sink[j]) for j in both]
        p = [e[j] * inv[j] for j in both]
        delta = [jnp.sum(p[j] * dp[j], axis=-1, keepdims=True) for j in both]
        ds = [p[j] * (dp[j] - delta[j]) for j in both]
        ds_b = [dsj.astype(BF16) for dsj in ds]
        dqn = [_nn(ds_b[j], kn[j]) * scale for j in both]
        for j in both:
            dvw_ref[j] = _tn(p[j].astype(BF16), do_b[j])
            dkw_ref[j] = _tn(ds_b[j], qn[j])
        for j in both:
            ds_ref[j * G:(j + 1) * G] += ds[j].reshape(G, BLOCK, 3 * BLOCK)
            sink_term = e_sink[j] * inv[j] * delta[j]
            for g in range(G):
                dsink_ref[j * G + g] += (jnp.zeros((1, BLOCK), F32)
                                         - jnp.sum(sink_term[g * BLOCK:(g + 1) * BLOCK], axis=0, keepdims=True))
        dq = []
        for j in both:
            qhat, rstd = norm[j]
            dqg_ref[j] += jnp.sum(dqn[j] * qhat, axis=0, keepdims=True)
            dqh = dqn[j] * qg
            dq.append(_unstack_heads(rstd * (dqh - qhat * jnp.mean(dqh * qhat, axis=-1, keepdims=True))))
        dq_ref[...] = jnp.concatenate(dq, axis=1).astype(dq_ref.dtype)

    def kv(shift):
        return pl.BlockSpec((BLOCK, 2 * KV_WIDTH), lambda n: (jnp.clip(n + shift, 0, nb - 1), KV_COL // (2 * KV_WIDTH)))

    gain = pl.BlockSpec((1, dh), lambda n: (0, 0))
    sink_spec = pl.BlockSpec((ATT_Q_HEADS, 1, BLOCK), lambda n: (0, 0, 0))
    bias_spec = pl.BlockSpec((ATT_Q_HEADS, BLOCK, 3 * BLOCK), lambda n: (0, 0, 0))
    win = pl.BlockSpec((KV, None, 3 * BLOCK, dh), lambda n: (0, n, 0, 0))
    wshape = jax.ShapeDtypeStruct((KV, nb, 3 * BLOCK, dh), F32)
    return pl.pallas_call(
        body, name=name, grid=(nb,),
        in_specs=[pl.BlockSpec((BLOCK, ATT_WIDTH), lambda n: (n, Q_COL // ATT_WIDTH)), kv(-1), kv(0), kv(1), gain, gain,
                  sink_spec, bias_spec, pl.BlockSpec((BLOCK, ATT_WIDTH), lambda n: (n, HG_WIDTH // ATT_WIDTH))],
        out_specs=[pl.BlockSpec((BLOCK, ATT_WIDTH), lambda n: (n, 0)), win, win, bias_spec, sink_spec,
                   pl.BlockSpec((KV, 1, dh), lambda n: (0, 0, 0))],
        out_shape=[jax.ShapeDtypeStruct((S, ATT_WIDTH), BF16), wshape, wshape,
                   jax.ShapeDtypeStruct((ATT_Q_HEADS, BLOCK, 3 * BLOCK), F32),
                   jax.ShapeDtypeStruct((ATT_Q_HEADS, 1, BLOCK), F32),
                   jax.ShapeDtypeStruct((KV, 1, dh), F32)],
        compiler_params=_params("arbitrary"),
    )(z, z, z, z, q_g, k_g, sink, bias, do)


def _attn_kv_reduce(dkw, dvw, z, k_g, name):
    S = z.shape[0]
    nb = S // BLOCK
    dh = ATT_HEAD_DIM
    kb = min(8, nb)
    steps = nb // kb

    def body(a_lo, a, a_hi, b_lo, b, b_hi, kv_ref, kg_ref, dkv_ref, dkg_ref):
        n = pl.program_id(0)

        @pl.when(n == 0)
        def _():
            dkg_ref[...] = jnp.zeros_like(dkg_ref)

        lo = jnp.where(n > 0, 1.0, 0.0)
        hi = jnp.where(n < steps - 1, 1.0, 0.0)

        def overlap_add(w, w_lo, w_hi, j, i):
            before = lo * w_lo[j] if i == 0 else w[j, i - 1, 2 * BLOCK:3 * BLOCK, :]
            after = hi * w_hi[j] if i == kb - 1 else w[j, i + 1, 0:BLOCK, :]
            return w[j, i, BLOCK:2 * BLOCK, :] + before + after

        dkg = [jnp.zeros((1, dh), F32) for _ in range(ATT_KV_HEADS)]
        for i in range(kb):
            rows = slice(i * BLOCK, (i + 1) * BLOCK)
            dks, dvs = [], []
            for j in range(ATT_KV_HEADS):
                dkn = overlap_add(a, a_lo, a_hi, j, i)
                dvs.append(overlap_add(b, b_lo, b_hi, j, i))
                khat, rstd = _rms_rows(kv_ref[rows, j * dh:(j + 1) * dh])
                dkg[j] = dkg[j] + jnp.sum(dkn * khat, axis=0, keepdims=True)
                dkh = dkn * kg_ref[...]
                dks.append(rstd * (dkh - khat * jnp.mean(dkh * khat, axis=-1, keepdims=True)))
            dkv_ref[rows, :] = jnp.concatenate(dks + dvs, axis=1).astype(dkv_ref.dtype)
        for j in range(ATT_KV_HEADS):
            dkg_ref[j] += dkg[j]

    main = pl.BlockSpec((ATT_KV_HEADS, kb, 3 * BLOCK, dh), lambda n: (0, n, 0, 0))
    halo_lo = pl.BlockSpec((ATT_KV_HEADS, None, BLOCK, dh), lambda n: (0, jnp.maximum(n * kb - 1, 0), 2, 0))
    halo_hi = pl.BlockSpec((ATT_KV_HEADS, None, BLOCK, dh), lambda n: (0, jnp.minimum(n * kb + kb, nb - 1), 0, 0))
    return pl.pallas_call(
        body, name=name, grid=(steps,),
        in_specs=[halo_lo, main, halo_hi, halo_lo, main, halo_hi,
                  pl.BlockSpec((kb * BLOCK, 2 * KV_WIDTH), lambda n: (n, KV_COL // (2 * KV_WIDTH))),
                  pl.BlockSpec((1, dh), lambda n: (0, 0))],
        out_specs=[pl.BlockSpec((kb * BLOCK, 2 * KV_WIDTH), lambda n: (n, 0)),
                   pl.BlockSpec((ATT_KV_HEADS, 1, dh), lambda n: (0, 0, 0))],
        out_shape=[jax.ShapeDtypeStruct((S, 2 * KV_WIDTH), BF16), jax.ShapeDtypeStruct((ATT_KV_HEADS, 1, dh), F32)],
        compiler_params=_params("arbitrary"),
    )(dkw, dkw, dkw, dvw, dvw, dvw, z, k_g)


def _ada_fwd(c_act, w, b, name):
    n = w.shape[1]

    def body(c_ref, w_ref, b_ref, o_ref):
        o_ref[...] = _nn(c_ref[...], w_ref[...], precision=lax.Precision.HIGHEST) + b_ref[...]

    tn = n // 3
    return pl.pallas_call(
        body, name=name, grid=(3,),
        in_specs=[pl.BlockSpec(c_act.shape, lambda j: (0, 0)), pl.BlockSpec((w.shape[0], tn), lambda j: (0, j)),
                  pl.BlockSpec((1, tn), lambda j: (0, j))],
        out_specs=pl.BlockSpec((c_act.shape[0], tn), lambda j: (0, j)),
        out_shape=jax.ShapeDtypeStruct((c_act.shape[0], n), F32), compiler_params=_params("parallel"),
    )(c_act, w, b)


def _ada_wgrad(c_act_t, dm, name):
    D, nbatch = c_act_t.shape
    n = dm.shape[1]
    tr = 256

    def body(c_ref, dm_ref, o_ref):
        cv, dv = c_ref[...], dm_ref[...]
        acc = cv[:, 0:1] * dv[0:1, :]
        for b in range(1, nbatch):
            acc = acc + cv[:, b:b + 1] * dv[b:b + 1, :]
        o_ref[...] = acc

    return pl.pallas_call(
        body, name=name, grid=(D // tr,),
        in_specs=[pl.BlockSpec((tr, nbatch), lambda i: (i, 0)), pl.BlockSpec((nbatch, n), lambda i: (0, 0))],
        out_specs=pl.BlockSpec((tr, n), lambda i: (i, 0)), out_shape=jax.ShapeDtypeStruct((D, n), F32),
        compiler_params=_params("parallel"),
    )(c_act_t, dm)


def _to_bf16(w, name):
    R, Cn = w.shape
    tr = _row_tile(R)

    def body(w_ref, o_ref):
        o_ref[...] = w_ref[...].astype(BF16)

    blk = pl.BlockSpec((tr, Cn), lambda i: (i, 0))
    return pl.pallas_call(
        body, name=name, grid=(R // tr,), in_specs=[blk], out_specs=blk, out_shape=jax.ShapeDtypeStruct((R, Cn), BF16),
        compiler_params=_params("parallel"),
    )(w)


def _adamw(w, g, m, v, name):
    R, Cn = w.shape
    tr = R
    for cand in (256, 128, 64, 32, 16, 8):
        if R % cand == 0:
            tr = cand
            break

    def body(w_ref, g_ref, m_ref, v_ref, d_ref, nm_ref, nv_ref):
        gv = g_ref[...]
        m_new = ADAM_B1 * m_ref[...] + (1.0 - ADAM_B1) * gv
        v_new = ADAM_B2 * v_ref[...] + (1.0 - ADAM_B2) * (gv * gv)
        m_hat = m_new / (1.0 - ADAM_B1 ** ADAM_STEP)
        v_hat = v_new / (1.0 - ADAM_B2 ** ADAM_STEP)
        d_ref[...] = -ADAM_LR * (m_hat / (jnp.sqrt(v_hat) + ADAM_EPS) + ADAM_WD * w_ref[...])
        nm_ref[...] = m_new
        nv_ref[...] = v_new

    blk = pl.BlockSpec((tr, Cn), lambda i: (i, 0))
    shp = jax.ShapeDtypeStruct((R, Cn), F32)
    return pl.pallas_call(
        body, name=name, grid=(R // tr,), in_specs=[blk] * 4, out_specs=[blk] * 3, out_shape=[shp] * 3,
        compiler_params=_params("parallel"),
    )(w, g, m, v)


def _place():
    return lax.axis_index("x"), lax.axis_index("y"), lax.axis_index("c")


def _flip(place, k):
    x, y, c = place
    return (1 - x if k & 4 else x, 1 - y if k & 2 else y, 1 - c if k & 1 else c)


def _dev_index(place):
    x, y, c = place
    return 4 * x + 2 * y + c


def _chip_index(place):
    return 2 * place[0] + place[1]


def _allgather8(x, name, reduce=False):
    R, Cn = x.shape

    def body(x_ref, *rest):
        if reduce:
            out_ref, sum_ref, send_sems, recv_sems, local_sem = rest
        else:
            out_ref, send_sems, recv_sems, local_sem = rest
        me = _place()
        mine = pltpu.make_async_copy(x_ref, out_ref.at[_dev_index(me)], local_sem)
        mine.start()

        def copy(k, origin, to):
            return pltpu.make_async_remote_copy(
                src_ref=x_ref, dst_ref=out_ref.at[_dev_index(origin)], send_sem=send_sems.at[k - 1],
                recv_sem=recv_sems.at[k - 1], device_id=to, device_id_type=MESH)

        sends = [copy(k, me, _flip(me, k)) for k in range(1, 8)]
        for cp in sends:
            cp.start()
        for k in range(1, 8):
            copy(k, _flip(me, k), me).wait_recv()
        for cp in sends:
            cp.wait_send()
        mine.wait()
        if reduce:
            acc = out_ref[0]
            for i in range(1, 8):
                acc = acc + out_ref[i]
            sum_ref[...] = acc

    vm = pl.BlockSpec(memory_space=pltpu.VMEM)
    outs = [jax.ShapeDtypeStruct((8, R, Cn), F32)] + ([jax.ShapeDtypeStruct((R, Cn), F32)] if reduce else [])
    res = pl.pallas_call(
        body, name=name, in_specs=[vm], out_specs=[vm] * len(outs), out_shape=outs,
        scratch_shapes=[pltpu.SemaphoreType.DMA((7,)), pltpu.SemaphoreType.DMA((7,)), pltpu.SemaphoreType.DMA],
    )(x)
    return res if reduce else res[0]


def _weights_allgather(shards, name):
    n = len(shards)
    per = 8

    def body(*refs):
        in_refs, out_refs = refs[:n], refs[n:2 * n]
        send_sems, recv_sems = refs[2 * n:]
        me = _place()
        c = me[2]
        sibling = _flip(me, 1)
        others = [_flip(me, 2 * j) for j in (1, 2, 3)]

        def copy(a, k, src, dst, to):
            return pltpu.make_async_remote_copy(
                src_ref=src, dst_ref=dst, send_sem=send_sems.at[per * a + k], recv_sem=recv_sems.at[per * a + k],
                device_id=to, device_id_type=MESH)

        def block(a, place, half):
            return out_refs[a].at[_chip_index(place), half]

        started = []
        for a in range(n):
            sends = [copy(a, 0, in_refs[a].at[c], block(a, me, c), sibling),
                     copy(a, 7, in_refs[a].at[1 - c], block(a, me, 1 - c), sibling)]
            sends += [copy(a, 1 + j, in_refs[a].at[c], block(a, me, c), to) for j, to in enumerate(others)]
            for cp in sends:
                cp.start()
            started += sends
        for a in range(n):
            for j, other in enumerate(others):
                landed = block(a, other, c)
                copy(a, 1 + j, landed, landed, me).wait_recv()
                fwd = copy(a, 4 + j, landed, landed, sibling)
                fwd.start()
                started.append(fwd)
        for a in range(n):
            copy(a, 0, block(a, me, 1 - c), block(a, me, 1 - c), me).wait_recv()
            copy(a, 7, block(a, me, c), block(a, me, c), me).wait_recv()
            for j, other in enumerate(others):
                got = block(a, other, 1 - c)
                copy(a, 4 + j, got, got, me).wait_recv()
        for cp in started:
            cp.wait_send()

    return pl.pallas_call(
        body, name=name, in_specs=[ANY] * n, out_specs=[ANY] * n,
        out_shape=[jax.ShapeDtypeStruct((N_CHIPS,) + s.shape, s.dtype) for s in shards],
        scratch_shapes=[pltpu.SemaphoreType.DMA((per * n,)), pltpu.SemaphoreType.DMA((per * n,))],
    )(*shards)


def _remote(src, dst, send_sems, recv_sems, i, to):
    return pltpu.make_async_remote_copy(
        src_ref=src, dst_ref=dst, send_sem=send_sems.at[i], recv_sem=recv_sems.at[i], device_id=to, device_id_type=MESH)


def _symmetric_plan(copies):
    def plan(in_refs, out_refs, send_sems, recv_sems):
        sends = [_remote(src, dst, send_sems, recv_sems, i, to) for i, (src, dst, to) in enumerate(copies(in_refs, out_refs))]
        return sends, sends
    return plan


def _halves_exchange(grads):
    def copies(in_refs, out_refs):
        me = _place()
        return [(g.at[kk, 1 - me[2]], got.at[kk], _flip(me, 1)) for g, got in zip(in_refs, out_refs) for kk in range(N_CHIPS)]

    return _Exchange(grads, [jax.ShapeDtypeStruct((N_CHIPS,) + g.shape[2:], g.dtype) for g in grads],
                     N_CHIPS * len(grads), _symmetric_plan(copies))


def _chips_exchange(parts):
    def copies(in_refs, out_refs):
        me = _place()
        return [(p.at[_chip_index(_flip(me, 2 * j))], got.at[j - 1], _flip(me, 2 * j))
                for p, got in zip(in_refs, out_refs) for j in (1, 2, 3)]

    return _Exchange(parts, [jax.ShapeDtypeStruct((3,) + p.shape[1:], p.dtype) for p in parts], 3 * len(parts),
                     _symmetric_plan(copies))


def _siblings_exchange(halves):
    def copies(in_refs, out_refs):
        sibling = _flip(_place(), 1)
        return [(h, got, sibling) for h, got in zip(in_refs, out_refs)]

    return _Exchange(halves, [jax.ShapeDtypeStruct(h.shape, h.dtype) for h in halves], len(halves), _symmetric_plan(copies))


def _gather_over_ici(shards):
    def copies(in_refs, out_refs):
        me = _place()
        c = me[2]
        return [(w.at[c], out.at[_chip_index(me), c], _flip(me, 2 * j)) for w, out in zip(in_refs, out_refs) for j in (1, 2, 3)]

    def plan(in_refs, out_refs, send_sems, recv_sems):
        me = _place()
        sends = [_remote(src, dst, send_sems, recv_sems, i, to) for i, (src, dst, to) in enumerate(copies(in_refs, out_refs))]
        lands = [out.at[_chip_index(_flip(me, 2 * j)), me[2]] for out in out_refs for j in (1, 2, 3)]
        return sends, [_remote(z, z, send_sems, recv_sems, i, me) for i, z in enumerate(lands)]

    return _Exchange(shards, [jax.ShapeDtypeStruct((N_CHIPS,) + s.shape, s.dtype) for s in shards], 3 * len(shards), plan)


def _gather_over_d2d(shards, gathered):
    n = len(shards)

    def plan(in_refs, out_refs, send_sems, recv_sems):
        me = _place()
        c = me[2]
        sibling = _flip(me, 1)
        mine = _chip_index(me)
        sends, recvs = [], []
        for a, (w, out) in enumerate(zip(in_refs[:n], out_refs)):
            moves = [(w.at[c], (mine, c)), (w.at[1 - c], (mine, 1 - c))]
            moves += [(out.at[_chip_index(_flip(me, 2 * j)), c], (_chip_index(_flip(me, 2 * j)), c)) for j in (1, 2, 3)]
            for k, (src, (chip, half)) in enumerate(moves):
                sends.append(_remote(src, out.at[chip, half], send_sems, recv_sems, 5 * a + k, sibling))
            lands = [(mine, 1 - c), (mine, c)] + [(_chip_index(_flip(me, 2 * j)), 1 - c) for j in (1, 2, 3)]
            for k, (chip, half) in enumerate(lands):
                z = out.at[chip, half]
                recvs.append(_remote(z, z, send_sems, recv_sems, 5 * a + k, me))
        return sends, recvs

    return _Exchange(list(shards) + list(gathered), [jax.ShapeDtypeStruct(g.shape, g.dtype) for g in gathered], 5 * n, plan,
                     aliases={n + a: a for a in range(n)})


def _row_tile(rows):
    for cand in (256, 176, 128, 64, 32, 16, 8):
        if rows % cand == 0:
            return cand
    return rows


def _pair_sum(core, grad, theirs, name):
    N, _, R, Cn = grad.shape
    tr = R

    def body(core_ref, g_ref, t_ref, o_ref, ob_ref):
        s = g_ref[...] + t_ref[...]
        o_ref[...] = s
        ob_ref[...] = s.astype(BF16)

    out = pl.BlockSpec((None, tr, Cn), lambda k, i, core_ref: (k, i, 0))
    return pl.pallas_call(
        body, name=name,
        grid_spec=pltpu.PrefetchScalarGridSpec(
            num_scalar_prefetch=1, grid=(N, R // tr),
            in_specs=[pl.BlockSpec((None, None, tr, Cn), lambda k, i, core_ref: (k, core_ref[0], i, 0)),
                      pl.BlockSpec((None, tr, Cn), lambda k, i, core_ref: (k, i, 0))],
            out_specs=[out, out]),
        out_shape=[jax.ShapeDtypeStruct((N, R, Cn), F32), jax.ShapeDtypeStruct((N, R, Cn), BF16)],
        compiler_params=_params("parallel", "parallel"),
    )(core, grad, theirs)


def _chip_sum(chip, parts, landed, name):
    _, R, Cn = parts.shape
    tr = R

    def body(chip_ref, p_ref, l_ref, o_ref):
        o_ref[...] = ((p_ref[...] + l_ref[0].astype(F32)) + l_ref[1].astype(F32)) + l_ref[2].astype(F32)

    return pl.pallas_call(
        body, name=name,
        grid_spec=pltpu.PrefetchScalarGridSpec(
            num_scalar_prefetch=1, grid=(R // tr,),
            in_specs=[pl.BlockSpec((None, tr, Cn), lambda i, chip_ref: (chip_ref[0], i, 0)),
                      pl.BlockSpec((3, tr, Cn), lambda i, chip_ref: (0, i, 0))],
            out_specs=pl.BlockSpec((tr, Cn), lambda i, chip_ref: (i, 0))),
        out_shape=jax.ShapeDtypeStruct((R, Cn), F32), compiler_params=_params("parallel"),
    )(chip, parts, landed)


def _pair_sums(core, grads, theirs, tag):
    return [_pair_sum(core, g, t, f"{tag}_pair_sum_{i}") for i, (g, t) in enumerate(zip(grads, theirs))]


def _chip_sums(chip, parts, landed, tag):
    return [_chip_sum(chip, p[0], l, f"{tag}_chip_sum_{i}") for i, (p, l) in enumerate(zip(parts, landed))]


def _by_chip_rows(g):
    return g.reshape(N_CHIPS, 2, g.shape[0] // (2 * N_CHIPS), g.shape[1])


def _by_chip_cols(g):
    return g.reshape(N_CHIPS, 2, g.shape[1] // 2, g.shape[2])


def _adamw_halves(core, w, g_mine, g_theirs, m, v, name):
    R2, Cn = w.shape
    r = R2 // 2
    tr = _row_tile(r)
    nt = r // tr

    def body(core_ref, w_ref, gm_ref, gt_ref, m_ref, v_ref, g_ref, d_ref, nm_ref, nv_ref):
        gv = jnp.where(pl.program_id(0) == core_ref[0], gm_ref[...], gt_ref[...])
        g_ref[...] = gv
        m_new = ADAM_B1 * m_ref[...] + (1.0 - ADAM_B1) * gv
        v_new = ADAM_B2 * v_ref[...] + (1.0 - ADAM_B2) * (gv * gv)
        m_hat = m_new / (1.0 - ADAM_B1 ** ADAM_STEP)
        v_hat = v_new / (1.0 - ADAM_B2 ** ADAM_STEP)
        d_ref[...] = -ADAM_LR * (m_hat / (jnp.sqrt(v_hat) + ADAM_EPS) + ADAM_WD * w_ref[...])
        nm_ref[...] = m_new
        nv_ref[...] = v_new

    full = pl.BlockSpec((tr, Cn), lambda hf, i, core_ref: (hf * nt + i, 0))
    half = pl.BlockSpec((tr, Cn), lambda hf, i, core_ref: (i, 0))
    shp = jax.ShapeDtypeStruct((R2, Cn), F32)
    return pl.pallas_call(
        body, name=name,
        grid_spec=pltpu.PrefetchScalarGridSpec(
            num_scalar_prefetch=1, grid=(2, nt), in_specs=[full, half, half, full, full], out_specs=[full] * 4),
        out_shape=[shp] * 4, compiler_params=_params("parallel", "parallel"),
    )(core, w, g_mine, g_theirs, m, v)


def _pad_row(v, width):
    v = v.reshape(1, -1)
    return jnp.pad(v, ((0, 0), (0, width - v.shape[1])))


def _ffn1_forward(x, ng, shift, scale, gate, w_in4, w_out, gather, next_norm):
    h = _rmsmod_fwd(x, ng, shift, scale, "ffn1_norm")
    (zg, zu, a), partly = _ffn_in_fwd(h, w_in4, "ffn1_in", exchange=_gather_over_ici(gather))
    (x_new, f, h_next), gathered = _proj_out_fwd([a], w_out, x, gate, 0.5, "ffn1_out", next_norm=next_norm,
                                                 exchange=_gather_over_d2d(gather, partly))
    return x_new, (h, zg, zu, a, f), gathered, h_next


def _ffn_backward(df, saved, w_in4, w_out, core, chip, tag, riding=None, norm=None):
    h, zg, zu, a = saved[:4]
    rode = None
    if riding:
        (dzg, dzu), rode = _dact_bwd(df, w_out, zg, zu, f"{tag}_dact", exchange=riding)
    else:
        dzg, dzu = _dact_bwd(df, w_out, zg, zu, f"{tag}_dact")
    g_out = [_by_chip_rows(_wgrad(a, [df], df.shape[1], f"{tag}_dw_out")[0].reshape(a.shape[1], df.shape[1]))]
    (dw_in,), theirs_out = _wgrad(h, [dzg, dzu], FF_SHARD, f"{tag}_dw_in", exchange=_halves_exchange(g_out))
    g_in = [_by_chip_cols(dw_in.reshape(N_CHIPS, h.shape[1], FF_SHARD))]
    parts_out = _pair_sums(core, g_out, theirs_out, f"{tag}_out")
    dh_outs, (theirs_in, landed_out) = _ffn_in_dgrad(
        dzg, dzu, w_in4, f"{tag}_dh", norm=norm, exchange=[_halves_exchange(g_in), _chips_exchange([parts_out[0][1]])])
    parts_in = _pair_sums(core, g_in, theirs_in, f"{tag}_in")
    return dh_outs, parts_in, _chip_sums(chip, parts_out, landed_out, f"{tag}_out"), rode


def kernel(x, c, w_ada, b_ada, norm_g, w_ffn1_in, w_ffn1_out, w_ffn2_in, w_ffn2_out, w_mix_in, w_mix_out, hgrn_lb, hgrn_norm_g, qk_norm_g, attn_sink, rel_bias, loss_target, m_w_ada, m_b_ada, m_norm_g, m_w_ffn1_in, m_w_ffn1_out, m_w_ffn2_in, m_w_ffn2_out, m_w_mix_in, m_w_mix_out, m_hgrn_lb, m_hgrn_norm_g, m_qk_norm_g, m_attn_sink, m_rel_bias, v_w_ada, v_b_ada, v_norm_g, v_w_ffn1_in, v_w_ffn1_out, v_w_ffn2_in, v_w_ffn2_out, v_w_mix_in, v_w_mix_out, v_hgrn_lb, v_hgrn_norm_g, v_qk_norm_g, v_attn_sink, v_rel_bias):
    D = D_MODEL
    S = x.shape[1]
    place = (lax.axis_index("x"), lax.axis_index("y"), lax.axis_index("c"))
    me, my_chip = _dev_index(place), _chip_index(place)
    x0 = x[0]
    target = loss_target[0]

    def halves(w, tag):
        return _to_bf16(w[0], f"{tag}_to_bf16").reshape(2, w.shape[1] // 2, w.shape[2])

    gathered = _weights_allgather([halves(w_ffn1_in, "w_ffn1_in"), halves(w_ffn1_out, "w_ffn1_out")], "weights_allgather")
    w1_in = gathered[0].reshape(N_CHIPS, D, FF_SHARD)
    w1_out = gathered[1].reshape(D_FF, D)
    mix_shards = [halves(w_mix_in, "w_mix_in"), halves(w_mix_out, "w_mix_out")]
    ffn2_shards = [halves(w_ffn2_in, "w_ffn2_in"), halves(w_ffn2_out, "w_ffn2_out")]
    core_arr = jnp.reshape(place[2], (1,)).astype(jnp.int32)
    chip_arr = jnp.reshape(my_chip, (1,)).astype(jnp.int32)

    small = jnp.concatenate([_pad_row(c, D), _pad_row(norm_g, D), _pad_row(hgrn_lb, D), jnp.zeros((5, D), F32)], axis=0)
    small_all = _allgather8(small, "small_allgather")
    c_all = small_all[:, 0, :]
    by_chip = small_all[0::2]
    norm_g_full = by_chip[:, 1, :3 * 256].reshape(N_CHIPS, 3, 256).transpose(1, 0, 2).reshape(3, D)
    lb_raw = by_chip[:, 2, :2 * 2 * 128].reshape(N_CHIPS, 2, 2, 128).transpose(1, 2, 0, 3).reshape(2, 2, HG_WIDTH)
    lb = jax.nn.sigmoid(lb_raw[:, 0, :] - lb_raw[:, 1, :])
    lb_f, lb_b = lb[0:1], lb[1:2]

    c_act_all = c_all * jax.nn.sigmoid(c_all)
    n_ada = w_ada.shape[2]
    b_mine = lax.dynamic_slice_in_dim(b_ada, my_chip * n_ada, n_ada, axis=1)
    mods_part = _ada_fwd(c_act_all, w_ada[0], b_mine, "ada_fwd")
    mods_all = _allgather8(mods_part, "mods_allgather")[0::2].transpose(1, 0, 2).reshape(8, N_MOD * D)
    mods = lax.dynamic_slice_in_dim(mods_all, me, 1, axis=0)
    sh1, sc1, g1, sh2, sc2, g2, sh3, sc3, g3 = [mods[:, i * D:(i + 1) * D] for i in range(N_MOD)]

    x1, saved1, gathered, h2 = _ffn1_forward(x0, norm_g_full[0:1], sh1, sc1, g1, w1_in, w1_out, mix_shards,
                                             (norm_g_full[1:2], sh2, sc2))
    wm_in = gathered[0].reshape(N_CHIPS, D, D_IN // N_CHIPS).transpose(1, 0, 2).reshape(D, D_IN)
    wm_out = gathered[1].reshape(D, D)

    z = _matmul_nn(h2, wm_in, F32, 256, "mix_in")
    (of, st_f), partly = _hgrn_fwd(z, lb_f, 0, "hgrn_fwd_f", exchange=_gather_over_ici(ffn2_shards))
    (ob, st_b), gathered = _hgrn_fwd(z, lb_b, 1, "hgrn_fwd_b", exchange=_gather_over_d2d(ffn2_shards, partly))
    w2_in = gathered[0].reshape(N_CHIPS, D, FF_SHARD)
    w2_out = gathered[1].reshape(D_FF, D)
    o_h = _hgrn_post_fwd(of, ob, z, hgrn_norm_g, "hgrn_post")

    q_g, k_g = qk_norm_g[0, 0:1], qk_norm_g[0, 1:2]
    sink_b = jnp.broadcast_to(attn_sink.reshape(ATT_Q_HEADS, 1, 1), (ATT_Q_HEADS, 1, BLOCK))
    bias = _bias_table(rel_bias, "bias_table")
    o_a = _attn_fwd(z, q_g, k_g, sink_b, bias, "attn_fwd")
    x2, mixed, h3 = _proj_out_fwd([o_h, o_a], wm_out, x1, g2, 1.0, "mix_out", next_norm=(norm_g_full[2:3], sh3, sc3))

    zg3, zu3, a3 = _ffn_in_fwd(h3, w2_in, "ffn2_in")
    dx3, df3, dg3, sq_cols = _proj_out_loss(a3, w2_out, x2, g3, 0.5, target, "ffn2_out_loss")
    loss_mine = 0.5 * jnp.sum(sq_cols) / D

    (dx2, dsh3, dsc3, dng3, dmixed, dg2), parts2, mine2_out, _ = _ffn_backward(
        df3, (h3, zg3, zu3, a3), w2_in, w2_out, core_arr, chip_arr, "ffn2",
        norm=_NormBwd(x2, norm_g_full[2:3], sc3, dx3, below=(mixed, g2, 1.0)))

    (do_cat,) = _matmul_nt([dmixed], wm_out, ROW_TILE, "mix_out_dgrad")
    dwm_out = _wgrad_rows([o_h, o_a], dmixed, "mix_out_dw").reshape(D, D)

    do_sum, dgr, d_hnorm = _hgrn_post_bwd(do_cat, of, ob, z, hgrn_norm_g, "hgrn_post_bwd")
    (dq_f, dff, dv_f, doml_f), landed2 = _hgrn_bwd(z, lb_f, do_sum, st_f, 0, "hgrn_bwd_f",
                                                   exchange=_chips_exchange([p[1] for p in parts2]))
    mine2 = _chip_sums(chip_arr, parts2, landed2, "ffn2_in") + mine2_out
    (dhq, dfb, dhi, doml_b), theirs2 = _hgrn_bwd(z, lb_b, do_sum, st_b, 1, "hgrn_bwd_b", acc=(dq_f, dv_f),
                                                 exchange=_siblings_exchange(mine2))

    daq, dkw, dvw, ds_sum, dsink, dqg = _attn_bwd(z, q_g, k_g, sink_b, bias, do_cat, "attn_bwd")
    dkv, dkg = _attn_kv_reduce(dkw, dvw, z, k_g, "attn_kv_reduce")
    d_rel_bias = jnp.sum(_bias_grad(ds_sum, "bias_grad"), axis=-1).T
    dz = [dhq, dff, dfb, dhi, dgr, daq, dkv]
    dwm_in = _wgrad_pieces(h2, dz, 2 * KV_WIDTH, "mix_in_dw").transpose(1, 0, 2).reshape(D, D_IN)
    wide = D_IN // N_CHIPS
    grads_m = [_by_chip_cols(dwm_in.reshape(D, N_CHIPS, wide).transpose(1, 0, 2)), _by_chip_rows(dwm_out)]
    (dx1, dsh2, dsc2, dng2, df1, dg1), theirs_m = _matmul_nt(
        dz, wm_in, 256, "mix_in_dgrad", exchange=_halves_exchange(grads_m),
        norm=_NormBwd(x1, norm_g_full[1:2], sc2, dx2, below=(saved1[4], g1, 0.5)))
    parts_m = _pair_sums(core_arr, grads_m, theirs_m, "mix")

    (dh1,), parts1, mine1_out, landed_m = _ffn_backward(df1, saved1, w1_in, w1_out, core_arr, chip_arr, "ffn1",
                                                        riding=_chips_exchange([p[1] for p in parts_m]))
    mine_m = _chip_sums(chip_arr, parts_m, landed_m, "mix")
    (dx0, dsh1, dsc1, dng1), landed1 = _rmsmod_bwd(dh1, _NormBwd(x0, norm_g_full[0:1], sc1, dx1), "ffn1_norm_bwd",
                                                   exchange=_chips_exchange([p[1] for p in parts1]))
    mine1 = _chip_sums(chip_arr, parts1, landed1, "ffn1_in") + mine1_out
    theirs_1m = list(_run_exchange(_siblings_exchange(mine1 + mine_m), "siblings_exchange"))
    reduced = list(zip(mine1 + mine2 + mine_m, theirs_1m[:2] + list(theirs2) + theirs_1m[2:]))

    dlb = -jnp.concatenate([doml_f, doml_b], axis=0)
    dlb_raw = dlb * lb * (1.0 - lb)
    d_hgrn_lb = jnp.stack([dlb_raw, -dlb_raw], axis=1)
    d_qk = jnp.concatenate([jnp.sum(dqg, axis=0), jnp.sum(dkg, axis=0)], axis=0)
    dmods = jnp.concatenate([dsh1, dsc1, dg1, dsh2, dsc2, dg2, dsh3, dsc3, dg3], axis=0)
    packed = jnp.concatenate(
        [dmods, dng1, dng2, dng3, d_hgrn_lb.reshape(2, D), _pad_row(d_hnorm, D), _pad_row(d_qk, D),
         _pad_row(dsink[:, 0, 0], D), _pad_row(d_rel_bias, D), _pad_row(loss_mine, D)], axis=0)
    packed = jnp.pad(packed, ((0, 24 - packed.shape[0]), (0, 0)))
    packed_all, packed_sum = _allgather8(packed, "small_grads_allgather", reduce=True)
    dmods_all = packed_all[:, 0:N_MOD, :].reshape(8, N_MOD * D)
    g_b_ada = packed_sum[0:N_MOD].reshape(1, N_MOD * D)
    g_norm_full = packed_sum[9:12]
    g_norm_g = lax.dynamic_slice_in_dim(g_norm_full, my_chip * 256, 256, axis=1).reshape(1, 3, 256)
    g_hgrn_lb = lax.dynamic_slice_in_dim(packed_sum[12:14].reshape(2, 2, HG_WIDTH), my_chip * 128, 128, axis=2)
    g_hgrn_norm_g = packed_sum[14:15, :HG_WIDTH]
    g_qk_norm_g = packed_sum[15, :2 * ATT_HEAD_DIM].reshape(1, 2, ATT_HEAD_DIM)
    g_attn_sink = packed_sum[16:17, :ATT_Q_HEADS]
    g_rel_bias = packed_sum[17, :NUM_BUCKETS * ATT_Q_HEADS].reshape(NUM_BUCKETS, ATT_Q_HEADS)
    loss = packed_sum[18, 0]

    dm_mine = lax.dynamic_slice_in_dim(dmods_all, my_chip * n_ada, n_ada, axis=1)
    g_w_ada = _ada_wgrad(c_act_all.T, dm_mine, "ada_wgrad")[None]

    def big(w, g, m, v, name):
        d, nm, nv = _adamw(w[0], g[0], m[0], v[0], name)
        return d[None], nm[None], nv[None]

    def big_halves(w, g_pair, m, v, name):
        g, d, nm, nv = _adamw_halves(core_arr, w[0], g_pair[0], g_pair[1], m[0], v[0], name)
        return g[None], (d[None], nm[None], nv[None])

    g_w1_in, u_w1_in = big_halves(w_ffn1_in, reduced[0], m_w_ffn1_in, v_w_ffn1_in, "adamw_w_ffn1_in")
    g_w1_out, u_w1_out = big_halves(w_ffn1_out, reduced[1], m_w_ffn1_out, v_w_ffn1_out, "adamw_w_ffn1_out")
    g_w2_in, u_w2_in = big_halves(w_ffn2_in, reduced[2], m_w_ffn2_in, v_w_ffn2_in, "adamw_w_ffn2_in")
    g_w2_out, u_w2_out = big_halves(w_ffn2_out, reduced[3], m_w_ffn2_out, v_w_ffn2_out, "adamw_w_ffn2_out")
    g_wm_in, u_wm_in = big_halves(w_mix_in, reduced[4], m_w_mix_in, v_w_mix_in, "adamw_w_mix_in")
    g_wm_out, u_wm_out = big_halves(w_mix_out, reduced[5], m_w_mix_out, v_w_mix_out, "adamw_w_mix_out")

    smalls = [(b_ada, g_b_ada, m_b_ada, v_b_ada), (norm_g, g_norm_g, m_norm_g, v_norm_g), (hgrn_lb, g_hgrn_lb, m_hgrn_lb, v_hgrn_lb),
              (hgrn_norm_g, g_hgrn_norm_g, m_hgrn_norm_g, v_hgrn_norm_g), (qk_norm_g, g_qk_norm_g, m_qk_norm_g, v_qk_norm_g),
              (attn_sink, g_attn_sink, m_attn_sink, v_attn_sink), (rel_bias, g_rel_bias, m_rel_bias, v_rel_bias)]
    sizes = [t[0].size for t in smalls]
    total = sum(sizes)
    rows = -(-total // 128)
    rows = -(-rows // 8) * 8

    def pack(i):
        flat = jnp.concatenate([t[i].reshape(-1) for t in smalls])
        fill = 1.0 if i == 3 else 0.0
        return jnp.pad(flat, (0, rows * 128 - total), constant_values=fill).reshape(rows, 128)

    packed_out = _adamw(pack(0), pack(1), pack(2), pack(3), "adamw_small")

    def unpack(flat2d):
        flat = flat2d.reshape(-1)
        outs, off = [], 0
        for t, n in zip(smalls, sizes):
            outs.append(flat[off:off + n].reshape(t[0].shape))
            off += n
        return outs

    d_small, m_small, v_small = [unpack(t) for t in packed_out]

    upd = {
        "w_ada": big(w_ada, g_w_ada, m_w_ada, v_w_ada, "adamw_w_ada"),
        "w_ffn1_in": u_w1_in, "w_ffn1_out": u_w1_out, "w_ffn2_in": u_w2_in, "w_ffn2_out": u_w2_out,
        "w_mix_in": u_wm_in, "w_mix_out": u_wm_out,
    }
    small_names = ["b_ada", "norm_g", "hgrn_lb", "hgrn_norm_g", "qk_norm_g", "attn_sink", "rel_bias"]
    for i, nme in enumerate(small_names):
        upd[nme] = (d_small[i], m_small[i], v_small[i])
    grads = {
        "w_ada": g_w_ada, "b_ada": g_b_ada, "norm_g": g_norm_g, "w_ffn1_in": g_w1_in, "w_ffn1_out": g_w1_out,
        "w_ffn2_in": g_w2_in, "w_ffn2_out": g_w2_out, "w_mix_in": g_wm_in, "w_mix_out": g_wm_out, "hgrn_lb": g_hgrn_lb,
        "hgrn_norm_g": g_hgrn_norm_g, "qk_norm_g": g_qk_norm_g, "attn_sink": g_attn_sink, "rel_bias": g_rel_bias,
    }
    order = ["w_ada", "b_ada", "norm_g", "w_ffn1_in", "w_ffn1_out", "w_ffn2_in", "w_ffn2_out", "w_mix_in", "w_mix_out",
             "hgrn_lb", "hgrn_norm_g", "qk_norm_g", "attn_sink", "rel_bias"]
    return (loss, dx0[None], *[grads[k] for k in order], *[upd[k][0] for k in order], *[upd[k][1] for k in order],
            *[upd[k][2] for k in order])
```

```python
import functools
import math

import numpy as np
import jax
import jax.numpy as jnp
from jax import lax
from jax.experimental import pallas as pl
from jax.experimental.pallas import tpu as pltpu

F32, BF16 = jnp.float32, jnp.bfloat16

D_MODEL = 1024
D_FF = 2816
HG_HEADS, HG_DIM = 4, 128
HG_WIDTH = HG_HEADS * HG_DIM
ATT_Q_HEADS, ATT_KV_HEADS, ATT_HEAD_DIM = 8, 2, 64
ATT_GROUP = ATT_Q_HEADS // ATT_KV_HEADS
ATT_WIDTH = ATT_Q_HEADS * ATT_HEAD_DIM
KV_WIDTH = ATT_KV_HEADS * ATT_HEAD_DIM
WINDOW, BLOCK = 128, 128
NUM_BUCKETS, MAX_DISTANCE = 32, 128
N_MOD = 9
EPS = 1e-6
D_IN = 5 * HG_WIDTH + ATT_WIDTH + 2 * KV_WIDTH
ADAM_LR, ADAM_B1, ADAM_B2, ADAM_EPS, ADAM_WD, ADAM_STEP = 0.001, 0.9, 0.999, 1e-08, 0.01, 10

N_CHIPS = 4
FF_SHARD = 2 * D_FF // N_CHIPS
NEG = -1e30

VMEM_LIMIT_BYTES = 56 << 20
ROW_TILE = 512
HG_CHUNK = 16
HG_ROWS = 256

MESH = pl.DeviceIdType.MESH
ANY = pl.BlockSpec(memory_space=pl.ANY)


def _params(*sem):
    return pltpu.CompilerParams(dimension_semantics=sem, vmem_limit_bytes=VMEM_LIMIT_BYTES)


def _resident(shape, index_map):
    return pl.BlockSpec(shape, index_map, pipeline_mode=pl.Buffered(1))


def _dot(a, b, dims, precision=None):
    return lax.dot_general(a, b, (dims, ((), ())), precision=precision, preferred_element_type=F32)


def _nn(a, b, precision=None):
    return _dot(a, b, ((1,), (0,)), precision)


def _nt(a, b):
    return _dot(a, b, ((1,), (1,)))


def _tn(a, b):
    return _dot(a, b, ((0,), (0,)))


def _sigmoid(x):
    return jax.nn.sigmoid(x)


class _Exchange:
    def __init__(self, inputs, out_shapes, n_sems, plan, aliases=None, then=None):
        self.inputs, self.out_shapes, self.n_sems, self.plan, self.aliases = list(inputs), list(out_shapes), n_sems, plan, aliases or {}
        self.then = then

    def sem_shapes(self):
        return [pltpu.SemaphoreType.DMA((self.n_sems,)), pltpu.SemaphoreType.DMA((self.n_sems,))]

    def start(self, in_refs, out_refs, send_sems, recv_sems):
        for cp in self.plan(in_refs, out_refs, send_sems, recv_sems)[0]:
            cp.start()

    @staticmethod
    def _wait(sends, recvs):
        for cp in recvs:
            cp.wait_recv()
        for cp in sends:
            cp.wait_send()

    def switch(self, in_refs, out_refs, send_sems, recv_sems):
        if self.then:
            self._wait(*self.plan(in_refs, out_refs, send_sems, recv_sems))
            for cp in self.then(in_refs, out_refs, send_sems, recv_sems)[0]:
                cp.start()

    def finish(self, in_refs, out_refs, send_sems, recv_sems):
        self._wait(*(self.then or self.plan)(in_refs, out_refs, send_sems, recv_sems))


def _run_exchange(ex, name):
    n_in, n_out = len(ex.inputs), len(ex.out_shapes)

    def body(*refs):
        in_refs, out_refs, (send_sems, recv_sems) = refs[:n_in], refs[n_in:n_in + n_out], refs[n_in + n_out:]
        ex.start(in_refs, out_refs, send_sems, recv_sems)
        ex.switch(in_refs, out_refs, send_sems, recv_sems)
        ex.finish(in_refs, out_refs, send_sems, recv_sems)

    return pl.pallas_call(
        body, name=name, in_specs=[ANY] * n_in, out_specs=[ANY] * n_out, out_shape=ex.out_shapes,
        scratch_shapes=ex.sem_shapes(), input_output_aliases=dict(ex.aliases),
    )(*ex.inputs)


def _call(body, *, name, grid, in_specs, out_specs, out_shape, args, semantics, scratch_shapes=(), exchange=None):
    if exchange is None:
        return pl.pallas_call(
            body, name=name, grid=grid, in_specs=in_specs, out_specs=out_specs, out_shape=out_shape,
            scratch_shapes=list(scratch_shapes), compiler_params=_params(*semantics))(*args)
    exs = exchange if isinstance(exchange, (list, tuple)) else [exchange]
    n_in, n_out, n_scr = len(in_specs), len(out_specs), len(scratch_shapes)
    x_in, x_out = [len(ex.inputs) for ex in exs], [len(ex.out_shapes) for ex in exs]

    def take(refs, counts):
        groups = []
        for n in counts:
            groups.append(refs[:n])
            refs = refs[n:]
        return groups, refs

    def carrier(*refs):
        ins, refs = refs[:n_in], refs[n_in:]
        x_ins, refs = take(refs, x_in)
        outs, refs = refs[:n_out], refs[n_out:]
        x_outs, refs = take(refs, x_out)
        scr, refs = refs[:n_scr], refs[n_scr:]
        sems, _ = take(refs, [2] * len(exs))
        ids = [pl.program_id(a) for a in range(len(grid))]
        first = functools.reduce(jnp.logical_and, [i == 0 for i in ids])
        last = functools.reduce(jnp.logical_and, [i == g - 1 for i, g in zip(ids, grid)])
        step = functools.reduce(lambda acc, ig: acc * ig[1] + ig[0], zip(ids, grid), 0)

        @pl.when(first)
        def _():
            for ex, xi, xo, (send_sems, recv_sems) in zip(exs, x_ins, x_outs, sems):
                ex.start(xi, xo, send_sems, recv_sems)

        if any(ex.then for ex in exs):
            @pl.when(step == (3 * math.prod(grid)) // 4)
            def _():
                for ex, xi, xo, (send_sems, recv_sems) in zip(exs, x_ins, x_outs, sems):
                    ex.switch(xi, xo, send_sems, recv_sems)

        body(*ins, *outs, *scr)

        @pl.when(last)
        def _():
            for ex, xi, xo, (send_sems, recv_sems) in zip(exs, x_ins, x_outs, sems):
                ex.finish(xi, xo, send_sems, recv_sems)

    aliases, i0, o0 = {}, n_in, n_out
    for ex in exs:
        aliases.update({i0 + i: o0 + o for i, o in ex.aliases.items()})
        i0, o0 = i0 + len(ex.inputs), o0 + len(ex.out_shapes)
    res = pl.pallas_call(
        carrier, name=name, grid=grid, in_specs=list(in_specs) + [ANY] * sum(x_in),
        out_specs=list(out_specs) + [ANY] * sum(x_out),
        out_shape=list(out_shape) + [s for ex in exs for s in ex.out_shapes],
        scratch_shapes=list(scratch_shapes) + [s for ex in exs for s in ex.sem_shapes()],
        input_output_aliases=aliases, compiler_params=_params(*["arbitrary"] * len(grid)),
    )(*args, *[a for ex in exs for a in ex.inputs])
    x_res, _ = take(list(res[n_out:]), x_out)
    return list(res[:n_out]), (x_res if isinstance(exchange, (list, tuple)) else x_res[0])


def _rmsmod_fwd(x, g, shift, scale, name):
    S, D = x.shape
    tr = min(ROW_TILE, S)

    def body(x_ref, g_ref, sh_ref, sc_ref, h_ref):
        xv = x_ref[...]
        rstd = lax.rsqrt(jnp.mean(xv * xv, axis=-1, keepdims=True) + EPS)
        y = xv * rstd * g_ref[...]
        h_ref[...] = (y * (1.0 + sc_ref[...]) + sh_ref[...]).astype(h_ref.dtype)

    row = pl.BlockSpec((tr, D), lambda i: (i, 0))
    vec = pl.BlockSpec((1, D), lambda i: (0, 0))
    return pl.pallas_call(
        body, name=name, grid=(S // tr,), in_specs=[row, vec, vec, vec], out_specs=row,
        out_shape=jax.ShapeDtypeStruct((S, D), BF16), compiler_params=_params("parallel"),
    )(x, g, shift, scale)


class _NormBwd:
    def __init__(self, x, g, scale, dx_res, below=None):
        S, D = x.shape
        self.below, self.coef = below, (below[2] if below else None)
        self.inputs = [x, g, scale, dx_res] + ([below[0], below[1]] if below else [])
        vshape = jax.ShapeDtypeStruct((1, D), F32)
        self.out_shape = [jax.ShapeDtypeStruct((S, D), F32), vshape, vshape, vshape]
        if below:
            self.out_shape += [jax.ShapeDtypeStruct((S, D), BF16), vshape]

    def specs(self, tr, D):
        row = pl.BlockSpec((tr, D), lambda i: (i, 0))
        vec = pl.BlockSpec((1, D), lambda i: (0, 0))
        return ([row, vec, vec, row] + ([row, vec] if self.below else []),
                [row, vec, vec, vec] + ([row, vec] if self.below else []))

    def step(self, dhv, in_refs, out_refs):
        if self.below:
            x_ref, g_ref, sc_ref, dxr_ref, f_ref, gate_ref = in_refs
            dx_ref, dsh_ref, dsc_ref, dg_ref, df_ref, dgate_ref = out_refs
            sums = (dsh_ref, dsc_ref, dg_ref, dgate_ref)
        else:
            x_ref, g_ref, sc_ref, dxr_ref = in_refs
            dx_ref, dsh_ref, dsc_ref, dg_ref = out_refs
            sums = (dsh_ref, dsc_ref, dg_ref)

        @pl.when(pl.program_id(0) == 0)
        def _():
            for ref in sums:
                ref[...] = jnp.zeros_like(ref)

        xv, gv = x_ref[...], g_ref[...]
        one_sc = 1.0 + sc_ref[...]
        rstd = lax.rsqrt(jnp.mean(xv * xv, axis=-1, keepdims=True) + EPS)
        n = xv * rstd
        dsh_ref[...] += jnp.sum(dhv, axis=0, keepdims=True)
        dsc_ref[...] += jnp.sum(dhv * n, axis=0, keepdims=True) * gv
        dg_ref[...] += jnp.sum(dhv * n, axis=0, keepdims=True) * one_sc
        dn = dhv * (gv * one_sc)
        dx = dxr_ref[...] + rstd * (dn - n * jnp.mean(dn * n, axis=-1, keepdims=True))
        dx_ref[...] = dx
        if self.below:
            df_ref[...] = (self.coef * gate_ref[...] * dx).astype(df_ref.dtype)
            dgate_ref[...] += self.coef * jnp.sum(dx * f_ref[...].astype(F32), axis=0, keepdims=True)


def _rmsmod_bwd(dh, norm, name, exchange=None):
    S, D = dh.shape
    tr = min(ROW_TILE, S)
    n_in = len(norm.inputs)

    def body(dh_ref, *refs):
        norm.step(dh_ref[...], refs[:n_in], refs[n_in:])

    in_specs, out_specs = norm.specs(tr, D)
    return _call(body, name=name, grid=(S // tr,), in_specs=[pl.BlockSpec((tr, D), lambda i: (i, 0))] + in_specs,
                 out_specs=out_specs, out_shape=norm.out_shape, args=[dh] + norm.inputs, semantics=("arbitrary",),
                 exchange=exchange)


def _ffn_in_fwd(h, w4, name, exchange=None):
    S, D = h.shape
    tm = min(ROW_TILE, S)
    n = w4.shape[2]

    def body(h_ref, wg_ref, wu_ref, zg_ref, zu_ref, a_ref):
        hv = h_ref[...]
        zg = _nn(hv, wg_ref[...])
        zu = _nn(hv, wu_ref[...])
        zg_ref[...] = zg.astype(zg_ref.dtype)
        zu_ref[...] = zu.astype(zu_ref.dtype)
        a_ref[...] = (zg * _sigmoid(zg) * zu).astype(a_ref.dtype)

    out = pl.BlockSpec((tm, n), lambda j, m: (m, j))
    oshape = jax.ShapeDtypeStruct((S, 2 * n), BF16)
    return _call(
        body, name=name, grid=(2, S // tm),
        in_specs=[pl.BlockSpec((tm, D), lambda j, m: (m, 0)),
                  pl.BlockSpec((None, D, n), lambda j, m: (j, 0, 0)),
                  pl.BlockSpec((None, D, n), lambda j, m: (j + 2, 0, 0))],
        out_specs=[out, out, out], out_shape=[oshape, oshape, oshape], args=(h, w4, w4),
        semantics=("parallel", "parallel"), exchange=exchange)


def _proj_out_fwd(lhs, w, x, gate, coef, name, exchange=None, next_norm=None):
    S, D = x.shape
    tm = min(ROW_TILE, S)
    ks = [a.shape[1] for a in lhs]

    def body(*refs):
        lhs_refs, refs = refs[:len(lhs)], refs[len(lhs):]
        if next_norm:
            w_ref, x_ref, gate_ref, g_ref, sh_ref, sc_ref, xn_ref, f_ref, h_ref = refs
        else:
            w_ref, x_ref, gate_ref, xn_ref, f_ref = refs
        acc, off = None, 0
        for a_ref, k in zip(lhs_refs, ks):
            part = _nn(a_ref[...], w_ref[off:off + k, :])
            acc = part if acc is None else acc + part
            off += k
        f_ref[...] = acc.astype(f_ref.dtype)
        xn = x_ref[...] + coef * gate_ref[...] * acc
        xn_ref[...] = xn
        if next_norm:
            rstd = lax.rsqrt(jnp.mean(xn * xn, axis=-1, keepdims=True) + EPS)
            h_ref[...] = (xn * rstd * g_ref[...] * (1.0 + sc_ref[...]) + sh_ref[...]).astype(h_ref.dtype)

    row = pl.BlockSpec((tm, D), lambda m: (m, 0))
    vec = pl.BlockSpec((1, D), lambda m: (0, 0))
    extra = list(next_norm) if next_norm else []
    return _call(
        body, name=name, grid=(S // tm,),
        in_specs=[pl.BlockSpec((tm, k), lambda m: (m, 0)) for k in ks]
        + [_resident(w.shape, lambda m: (0, 0)), row, vec] + [vec] * len(extra),
        out_specs=[row, row] + ([row] if next_norm else []),
        out_shape=[jax.ShapeDtypeStruct((S, D), F32), jax.ShapeDtypeStruct((S, D), BF16)]
        + ([jax.ShapeDtypeStruct((S, D), BF16)] if next_norm else []),
        args=(*lhs, w, x, gate, *extra), semantics=("parallel",), exchange=exchange)


def _proj_out_loss(lhs, w, x, gate, coef, target, name):
    S, D = x.shape
    tm = min(ROW_TILE, S)

    def body(a_ref, w_ref, x_ref, gate_ref, t_ref, dy_ref, df_ref, dgate_ref, sq_ref):
        @pl.when(pl.program_id(0) == 0)
        def _():
            dgate_ref[...] = jnp.zeros_like(dgate_ref)
            sq_ref[...] = jnp.zeros_like(sq_ref)

        f = _nn(a_ref[...], w_ref[...])
        gate = coef * gate_ref[...]
        err = x_ref[...] + gate * f - t_ref[...]
        sq_ref[...] += jnp.sum(err * err, axis=0, keepdims=True)
        dy = err * (1.0 / D)
        dy_ref[...] = dy
        df_ref[...] = (gate * dy).astype(df_ref.dtype)
        dgate_ref[...] += coef * jnp.sum(dy * f, axis=0, keepdims=True)

    row = pl.BlockSpec((tm, D), lambda m: (m, 0))
    vec = pl.BlockSpec((1, D), lambda m: (0, 0))
    vshape = jax.ShapeDtypeStruct((1, D), F32)
    return pl.pallas_call(
        body, name=name, grid=(S // tm,),
        in_specs=[pl.BlockSpec((tm, lhs.shape[1]), lambda m: (m, 0)), _resident(w.shape, lambda m: (0, 0)), row, vec, row],
        out_specs=[row, row, vec, vec],
        out_shape=[jax.ShapeDtypeStruct((S, D), F32), jax.ShapeDtypeStruct((S, D), BF16), vshape, vshape],
        compiler_params=_params("arbitrary"),
    )(lhs, w, x, gate, target)


def _matmul_nn(a, w, out_dtype, tm, name):
    S, K = a.shape
    N = w.shape[1]
    tm = min(tm, S)

    def body(a_ref, w_ref, o_ref):
        o_ref[...] = _nn(a_ref[...], w_ref[...]).astype(o_ref.dtype)

    return pl.pallas_call(
        body, name=name, grid=(S // tm,),
        in_specs=[pl.BlockSpec((tm, K), lambda m: (m, 0)), _resident((K, N), lambda m: (0, 0))],
        out_specs=pl.BlockSpec((tm, N), lambda m: (m, 0)), out_shape=jax.ShapeDtypeStruct((S, N), out_dtype),
        compiler_params=_params("parallel"),
    )(a, w)


def _dact_bwd(df, w_out, zg, zu, name, exchange=None):
    S, D = df.shape
    tm = min(ROW_TILE, S)
    n = w_out.shape[0] // 2

    def body(df_ref, w_ref, zg_ref, zu_ref, dzg_ref, dzu_ref):
        da = _nt(df_ref[...], w_ref[...]).astype(BF16)
        zg_v, zu_v = zg_ref[...], zu_ref[...]
        s = _sigmoid(zg_v)
        dzu_ref[...] = da * zg_v * s
        dzg_ref[...] = da * zu_v * (s * (1.0 + zg_v * (1.0 - s)))

    blk = pl.BlockSpec((tm, n), lambda j, m: (m, j))
    oshape = jax.ShapeDtypeStruct((S, 2 * n), BF16)
    return _call(
        body, name=name, grid=(2, S // tm),
        in_specs=[pl.BlockSpec((tm, D), lambda j, m: (m, 0)), pl.BlockSpec((n, D), lambda j, m: (j, 0)), blk, blk],
        out_specs=[blk, blk], out_shape=[oshape, oshape], args=(df, w_out, zg, zu), semantics=("parallel", "parallel"),
        exchange=exchange)


def _ffn_in_dgrad(dzg, dzu, w4, name, exchange=None, norm=None):
    S = dzg.shape[0]
    D, n = w4.shape[1], w4.shape[2]
    tm = min(ROW_TILE, S)
    n_norm = len(norm.inputs) if norm else 0

    def body(dzg_ref, dzu_ref, w_ref, *refs):
        acc = _nt(dzg_ref[:, 0:n], w_ref[0])
        acc += _nt(dzg_ref[:, n:2 * n], w_ref[1])
        acc += _nt(dzu_ref[:, 0:n], w_ref[2])
        acc += _nt(dzu_ref[:, n:2 * n], w_ref[3])
        if norm:
            norm.step(acc, refs[:n_norm], refs[n_norm:])
        else:
            refs[0][...] = acc

    blk = pl.BlockSpec((tm, 2 * n), lambda m: (m, 0))
    in_specs, args = [blk, blk, _resident(w4.shape, lambda m: (0, 0, 0))], [dzg, dzu, w4]
    out_specs, out_shape = [pl.BlockSpec((tm, D), lambda m: (m, 0))], [jax.ShapeDtypeStruct((S, D), F32)]
    if norm:
        norm_in, out_specs = norm.specs(tm, D)
        in_specs, args, out_shape = in_specs + norm_in, args + norm.inputs, norm.out_shape
    return _call(body, name=name, grid=(S // tm,), in_specs=in_specs, out_specs=out_specs, out_shape=out_shape, args=args,
                 semantics=("arbitrary",) if norm else ("parallel",), exchange=exchange)


def _matmul_nt(pieces, w, tm, name, exchange=None, norm=None):
    S = pieces[0].shape[0]
    ks = [p.shape[1] for p in pieces]
    N = w.shape[0]
    tm = min(tm, S)
    n_norm = len(norm.inputs) if norm else 0

    def body(*refs):
        p_refs, w_ref, refs = refs[:len(ks)], refs[len(ks)], refs[len(ks) + 1:]
        acc, off = None, 0
        for p_ref, k in zip(p_refs, ks):
            part = _nt(p_ref[...], w_ref[:, off:off + k])
            acc = part if acc is None else acc + part
            off += k
        if norm:
            norm.step(acc, refs[:n_norm], refs[n_norm:])
        else:
            refs[0][...] = acc

    in_specs = [pl.BlockSpec((tm, k), lambda m: (m, 0)) for k in ks] + [_resident(w.shape, lambda m: (0, 0))]
    args = list(pieces) + [w]
    out_specs, out_shape = [pl.BlockSpec((tm, N), lambda m: (m, 0))], [jax.ShapeDtypeStruct((S, N), F32)]
    if norm:
        norm_in, out_specs = norm.specs(tm, N)
        in_specs, args, out_shape = in_specs + norm_in, args + norm.inputs, norm.out_shape
    return _call(body, name=name, grid=(S // tm,), in_specs=in_specs, out_specs=out_specs, out_shape=out_shape, args=args,
                 semantics=("arbitrary",) if norm else ("parallel",), exchange=exchange)


def _wgrad(a, gs, tn, name, exchange=None):
    S, Ka = a.shape
    N = gs[0].shape[1]
    ts = min(ROW_TILE * (2 if Ka <= D_MODEL else 1), S)

    def body(a_ref, *refs):
        g_refs, o_ref = refs[:-1], refs[-1]

        @pl.when(pl.program_id(1) == 0)
        def _():
            o_ref[...] = jnp.zeros_like(o_ref)

        a_t = a_ref[...].T
        for i, g_ref in enumerate(g_refs):
            o_ref[i] += _nn(a_t, g_ref[...])

    return _call(
        body, name=name, grid=(N // tn, S // ts),
        in_specs=[pl.BlockSpec((ts, Ka), lambda j, s: (s, 0))] + [pl.BlockSpec((ts, tn), lambda j, s: (s, j))] * len(gs),
        out_specs=[pl.BlockSpec((len(gs), None, Ka, tn), lambda j, s: (0, j, 0, 0))],
        out_shape=[jax.ShapeDtypeStruct((len(gs), N // tn, Ka, tn), F32)], args=(a, *gs),
        semantics=("parallel", "arbitrary"), exchange=exchange)


def _wgrad_pieces(a, pieces, tn, name):
    S, Ka = a.shape
    ts = min(ROW_TILE, S)
    blocks = [(i, j) for i, p in enumerate(pieces) for j in range(p.shape[1] // tn)]

    def body(a_ref, *refs):
        g_refs, o_ref = refs[:-1], refs[-1]

        @pl.when(pl.program_id(0) == 0)
        def _():
            o_ref[...] = jnp.zeros_like(o_ref)

        a_t = a_ref[...].T
        for b, g_ref in enumerate(g_refs):
            o_ref[b] += _nn(a_t, g_ref[...])

    return pl.pallas_call(
        body, name=name, grid=(S // ts,),
        in_specs=[pl.BlockSpec((ts, Ka), lambda s: (s, 0))] + [pl.BlockSpec((ts, tn), lambda s, j=j: (s, j)) for _, j in blocks],
        out_specs=pl.BlockSpec((len(blocks), Ka, tn), lambda s: (0, 0, 0)),
        out_shape=jax.ShapeDtypeStruct((len(blocks), Ka, tn), F32), compiler_params=_params("arbitrary"),
    )(a, *[pieces[i] for i, _ in blocks])


def _wgrad_rows(lhs, g, name):
    S, Ka = lhs[0].shape
    N = g.shape[1]
    ts = min(ROW_TILE, S)

    def body(*refs):
        a_refs, g_ref, o_ref = refs[:-2], refs[-2], refs[-1]

        @pl.when(pl.program_id(0) == 0)
        def _():
            o_ref[...] = jnp.zeros_like(o_ref)

        gv = g_ref[...]
        for i, a_ref in enumerate(a_refs):
            o_ref[i] += _tn(a_ref[...], gv)

    return pl.pallas_call(
        body, name=name, grid=(S // ts,),
        in_specs=[pl.BlockSpec((ts, Ka), lambda s: (s, 0))] * len(lhs) + [pl.BlockSpec((ts, N), lambda s: (s, 0))],
        out_specs=pl.BlockSpec((len(lhs), Ka, N), lambda s: (0, 0, 0)),
        out_shape=jax.ShapeDtypeStruct((len(lhs), Ka, N), F32), compiler_params=_params("arbitrary"),
    )(*lhs, g)


def _hgrn_chunk_common(qr, fr, oml, tri, last):
    k = oml * _sigmoid(-fr)
    g = jnp.log1p(-k) * math.log2(math.e)
    q = qr * _sigmoid(qr)
    G = _nn(tri, g, precision=lax.Precision.HIGHEST)
    Gl = G[last:last + 1]
    return q, k, G, Gl


def _hgrn_consts(reverse):
    C = HG_CHUNK
    r = lax.broadcasted_iota(jnp.int32, (C, C), 0)
    cc = lax.broadcasted_iota(jnp.int32, (C, C), 1)
    tri = ((cc >= r) if reverse else (cc <= r)).astype(F32)
    tri_t = ((cc <= r) if reverse else (cc >= r)).astype(F32)
    rid = lax.broadcasted_iota(jnp.int32, (C, HG_WIDTH), 0)
    return tri, tri_t, rid, (0 if reverse else C - 1)


def _head_slices():
    return [slice(h * HG_DIM, (h + 1) * HG_DIM) for h in range(HG_HEADS)]


def _per_head_lane_sum(x):
    C = x.shape[0]
    return jnp.concatenate(
        [jnp.broadcast_to(jnp.sum(x[:, sl], axis=-1, keepdims=True), (C, HG_DIM)) for sl in _head_slices()], axis=1)


HG_TILE = 8


def _pair_tiles(s, reverse):
    blk, r = divmod(s, HG_TILE)
    n_tiles = HG_CHUNK // HG_TILE
    others = range(0, blk) if reverse else range(blk + 1, n_tiles)
    return [(blk, r)] + [(t, None) for t in others]


def _pair_decay(G, s, tile, r, rid8, reverse, keys=False):
    rs = slice(tile * HG_TILE, (tile + 1) * HG_TILE)
    d = (G[s:s + 1] - G[rs]) if keys else (G[rs] - G[s:s + 1])
    if r is not None:
        d = jnp.where((rid8 <= r) if reverse else (rid8 >= r), d, NEG)
    return rs, jnp.exp2(d)


def _hgrn_fwd(z, lb, direction, name, exchange=None):
    S = z.shape[0]
    C, DK, W = HG_CHUNK, HG_DIM, HG_WIDTH
    tb = min(HG_ROWS, S)
    n_t, n_c = S // tb, tb // C
    reverse = direction == 1
    tmap = (lambda i: n_t - 1 - i) if reverse else (lambda i: i)

    def body(q_ref, f_ref, v_ref, lb_ref, o_ref, st_out_ref, st_ref):
        @pl.when(pl.program_id(0) == 0)
        def _():
            st_ref[...] = jnp.zeros_like(st_ref)

        oml = 1.0 - lb_ref[...]
        tri, _, _, last = _hgrn_consts(reverse)
        rid8 = lax.broadcasted_iota(jnp.int32, (HG_TILE, W), 0)

        def chunk(ci, carry):
            cidx = (n_c - 1 - ci) if reverse else ci
            rows = pl.ds(pl.multiple_of(cidx * C, C), C)
            v = v_ref[rows, :]
            q, k, G, Gl = _hgrn_chunk_common(q_ref[rows, :], f_ref[rows, :], oml, tri, last)
            qd = (q * jnp.exp2(G)).astype(BF16)
            kd = (k * jnp.exp2(Gl - G)).astype(BF16)
            e_gl = jnp.exp2(Gl)
            v_b = v.astype(BF16)
            inter = []
            for h, sl in enumerate(_head_slices()):
                st0 = st_ref[h]
                st_out_ref[h, cidx] = st0
                inter.append(_nt(qd[:, sl], st0.astype(BF16)))
                st_ref[h] = st0 * e_gl[:, sl] + _tn(v_b[:, sl], kd[:, sl])
            o = jnp.concatenate(inter, axis=1)
            o_t = [o[t * HG_TILE:(t + 1) * HG_TILE] for t in range(C // HG_TILE)]
            for s in range(C):
                k_s, v_s = k[s:s + 1], v[s:s + 1]
                for tile, r in _pair_tiles(s, reverse):
                    rs, e_s = _pair_decay(G, s, tile, r, rid8, reverse)
                    o_t[tile] = o_t[tile] + _per_head_lane_sum(q[rs] * k_s * e_s) * v_s
            o_ref[rows, :] = jnp.concatenate(o_t, axis=0)
            return carry

        lax.fori_loop(0, n_c, chunk, 0, unroll=8)

    def sec(j):
        return pl.BlockSpec((tb, W), lambda i: (tmap(i), j))

    return _call(
        body, name=name, grid=(n_t,),
        in_specs=[sec(0), sec(1 + direction), sec(3), pl.BlockSpec((1, W), lambda i: (0, 0))],
        out_specs=[sec(0), pl.BlockSpec((HG_HEADS, n_c, DK, DK), lambda i: (0, tmap(i), 0, 0))],
        out_shape=[jax.ShapeDtypeStruct((S, W), F32), jax.ShapeDtypeStruct((HG_HEADS, S // C, DK, DK), F32)],
        scratch_shapes=[pltpu.VMEM((HG_HEADS, DK, DK), F32)], args=(z, z, z, lb), semantics=("arbitrary",),
        exchange=exchange)


def _hgrn_bwd(z, lb, do, states, direction, name, acc=None, exchange=None):
    S = z.shape[0]
    C, DK, W = HG_CHUNK, HG_DIM, HG_WIDTH
    tb = min(HG_ROWS, S)
    n_t, n_c = S // tb, tb // C
    reverse = direction == 1
    tmap = (lambda i: i) if reverse else (lambda i: n_t - 1 - i)

    def body(*refs):
        if acc:
            q_ref, f_ref, v_ref, lb_ref, do_ref, st_in_ref, dqa_ref, dva_ref, dq_ref, df_ref, dv_ref, doml_ref, dst_ref = refs
        else:
            q_ref, f_ref, v_ref, lb_ref, do_ref, st_in_ref, dq_ref, df_ref, dv_ref, doml_ref, dst_ref = refs

        @pl.when(pl.program_id(0) == 0)
        def _():
            dst_ref[...] = jnp.zeros_like(dst_ref)
            doml_ref[...] = jnp.zeros_like(doml_ref)

        oml = 1.0 - lb_ref[...]
        tri, tri_t, rid, last = _hgrn_consts(reverse)
        rid8 = lax.broadcasted_iota(jnp.int32, (HG_TILE, W), 0)

        def chunk(ci, carry):
            cidx = ci if reverse else (n_c - 1 - ci)
            rows = pl.ds(pl.multiple_of(cidx * C, C), C)
            qr, fr, v, dov = q_ref[rows, :], f_ref[rows, :], v_ref[rows, :], do_ref[rows, :]
            q, k, G, Gl = _hgrn_chunk_common(qr, fr, oml, tri, last)
            e_g, e_gl, e_kd = jnp.exp2(G), jnp.exp2(Gl), jnp.exp2(Gl - G)
            qd, kd = q * e_g, k * e_kd
            do_b, v_b, qd_b, kd_b = dov.astype(BF16), v.astype(BF16), qd.astype(BF16), kd.astype(BF16)
            dqd, dkd, dv, state_dot = [], [], [], []
            for h, sl in enumerate(_head_slices()):
                st0, dst1 = st_in_ref[h, cidx], dst_ref[h]
                dst1_b = dst1.astype(BF16)
                dqd.append(_nn(do_b[:, sl], st0.astype(BF16)))
                dkd.append(_nn(v_b[:, sl], dst1_b))
                dv.append(_nt(kd_b[:, sl], dst1_b))
                state_dot.append(jnp.sum(st0 * dst1, axis=0, keepdims=True))
                dst_ref[h] = dst1 * e_gl[:, sl] + _tn(do_b[:, sl], qd_b[:, sl])
            dqd, dkd, dv = [jnp.concatenate(t, axis=1) for t in (dqd, dkd, dv)]
            d_gl = e_gl * jnp.concatenate(state_dot, axis=1) + jnp.sum(dkd * kd, axis=0, keepdims=True)
            dq, dk = dqd * e_g, dkd * e_kd
            n_tiles = C // HG_TILE
            dq_t, dk_t, dv_t = [[x[t * HG_TILE:(t + 1) * HG_TILE] for t in range(n_tiles)] for x in (dq, dk, dv)]
            for s in range(C):
                k_s, v_s = k[s:s + 1], v[s:s + 1]
                for tile, r in _pair_tiles(s, reverse):
                    rs, e_s = _pair_decay(G, s, tile, r, rid8, reverse)
                    dq_t[tile] = dq_t[tile] + _per_head_lane_sum(dov[rs] * v_s) * e_s * k_s
            for t in range(C):
                q_t, do_t = q[t:t + 1], dov[t:t + 1]
                for tile, r in _pair_tiles(t, not reverse):
                    rs, x_t = _pair_decay(G, t, tile, r, rid8, not reverse, keys=True)
                    qx = q_t * x_t
                    dv_t[tile] = dv_t[tile] + _per_head_lane_sum(k[rs] * qx) * do_t
                    dk_t[tile] = dk_t[tile] + _per_head_lane_sum(v[rs] * do_t) * qx
            dq, dk, dv = [jnp.concatenate(x, axis=0) for x in (dq_t, dk_t, dv_t)]
            d_big_g = dq * q - dk * k + jnp.where(rid == last, d_gl, 0.0)
            dg = _nn(tri_t, d_big_g, precision=lax.Precision.HIGHEST)
            dk_all = dk - dg / (1.0 - k)
            sig_nf = _sigmoid(-fr)
            df_ref[rows, :] = (-dk_all * k * (1.0 - sig_nf)).astype(df_ref.dtype)
            doml_ref[...] += jnp.sum(dk_all * sig_nf, axis=0, keepdims=True)
            sq = _sigmoid(qr)
            dqr = dq * (sq * (1.0 + qr * (1.0 - sq)))
            if acc:
                dqr = dqr + dqa_ref[rows, :]
                dv = dv + dva_ref[rows, :]
            dq_ref[rows, :] = dqr.astype(dq_ref.dtype)
            dv_ref[rows, :] = dv.astype(dv_ref.dtype)
            return carry

        lax.fori_loop(0, n_c, chunk, 0, unroll=8)

    def sec(j):
        return pl.BlockSpec((tb, W), lambda i: (tmap(i), j))

    vec = pl.BlockSpec((1, W), lambda i: (0, 0))
    ins = [z, z, z, lb, do, states]
    in_specs = [sec(0), sec(1 + direction), sec(3), vec, sec(0),
                pl.BlockSpec((HG_HEADS, n_c, DK, DK), lambda i: (0, tmap(i), 0, 0))]
    if acc:
        ins += list(acc)
        in_specs += [sec(0), sec(0)]
    final = jax.ShapeDtypeStruct((S, W), BF16)
    partial = final if acc else jax.ShapeDtypeStruct((S, W), F32)
    return _call(
        body, name=name, grid=(n_t,), in_specs=in_specs,
        out_specs=[sec(0), sec(0), sec(0), vec],
        out_shape=[partial, final, partial, jax.ShapeDtypeStruct((1, W), F32)],
        scratch_shapes=[pltpu.VMEM((HG_HEADS, DK, DK), F32)], args=ins, semantics=("arbitrary",), exchange=exchange)


def _hgrn_post_fwd(o_f, o_b, z, norm_g, name):
    S = z.shape[0]
    tr = min(ROW_TILE, S)

    def body(of_ref, ob_ref, gr_ref, ng_ref, y_ref):
        o = of_ref[...] + ob_ref[...]
        gr = gr_ref[...]
        gate = gr * _sigmoid(gr)
        ng = ng_ref[...]
        for h in range(HG_HEADS):
            sl = slice(h * HG_DIM, (h + 1) * HG_DIM)
            oh = o[:, sl]
            rstd = lax.rsqrt(jnp.mean(oh * oh, axis=-1, keepdims=True) + EPS)
            y_ref[:, sl] = (oh * rstd * ng[:, sl] * gate[:, sl]).astype(y_ref.dtype)

    row = pl.BlockSpec((tr, HG_WIDTH), lambda i: (i, 0))
    return pl.pallas_call(
        body, name=name, grid=(S // tr,),
        in_specs=[row, row, pl.BlockSpec((tr, HG_WIDTH), lambda i: (i, 4)), pl.BlockSpec((1, HG_WIDTH), lambda i: (0, 0))],
        out_specs=row, out_shape=jax.ShapeDtypeStruct((S, HG_WIDTH), BF16), compiler_params=_params("parallel"),
    )(o_f, o_b, z, norm_g)


def _hgrn_post_bwd(dy, o_f, o_b, z, norm_g, name):
    S = z.shape[0]
    tr = min(ROW_TILE, S)

    def body(dy_ref, of_ref, ob_ref, gr_ref, ng_ref, do_ref, dgr_ref, dng_ref):
        @pl.when(pl.program_id(0) == 0)
        def _():
            dng_ref[...] = jnp.zeros_like(dng_ref)

        o = of_ref[...] + ob_ref[...]
        gr, ng, dyv = gr_ref[...], ng_ref[...], dy_ref[...]
        sg = _sigmoid(gr)
        for h in range(HG_HEADS):
            sl = slice(h * HG_DIM, (h + 1) * HG_DIM)
            oh, dyh, grh, sgh, ngh = o[:, sl], dyv[:, sl], gr[:, sl], sg[:, sl], ng[:, sl]
            rstd = lax.rsqrt(jnp.mean(oh * oh, axis=-1, keepdims=True) + EPS)
            on = oh * rstd
            du = dyh * (grh * sgh)
            dgr_ref[:, sl] = (dyh * (on * ngh) * (sgh * (1.0 + grh * (1.0 - sgh)))).astype(dgr_ref.dtype)
            dng_ref[:, sl] += jnp.sum(du * on, axis=0, keepdims=True)
            don = du * ngh
            do_ref[:, sl] = rstd * (don - on * jnp.mean(don * on, axis=-1, keepdims=True))

    row = pl.BlockSpec((tr, HG_WIDTH), lambda i: (i, 0))
    vec = pl.BlockSpec((1, HG_WIDTH), lambda i: (0, 0))
    full = jax.ShapeDtypeStruct((S, HG_WIDTH), F32)
    return pl.pallas_call(
        body, name=name, grid=(S // tr,),
        in_specs=[row, row, row, pl.BlockSpec((tr, HG_WIDTH), lambda i: (i, 4)), vec],
        out_specs=[row, row, vec],
        out_shape=[full, jax.ShapeDtypeStruct((S, HG_WIDTH), BF16), jax.ShapeDtypeStruct((1, HG_WIDTH), F32)],
        compiler_params=_params("arbitrary"),
    )(dy, o_f, o_b, z, norm_g)


def _t5_bucket_table():
    rel = (np.arange(3 * BLOCK)[None, :] - BLOCK) - np.arange(BLOCK)[:, None]
    nb = NUM_BUCKETS // 2
    max_exact = nb // 2
    ret = (rel > 0).astype(np.int32) * nb
    n = np.abs(rel)
    ratio = np.log(np.maximum(n, 1).astype(np.float32) / np.float32(max_exact)) / np.float32(math.log(MAX_DISTANCE / max_exact))
    large = max_exact + (ratio.astype(np.float32) * np.float32(nb - max_exact)).astype(np.int32)
    large = np.minimum(large, nb - 1)
    bucket = ret + np.where(n < max_exact, n, large)
    return bucket.astype(np.int32), (n <= WINDOW)


def _bias_table(rel_bias, name):
    bucket, in_band = _t5_bucket_table()
    idx = jnp.asarray(np.where(in_band, bucket, -1))

    def body(rb_ref, idx_ref, o_ref):
        h = pl.program_id(0)
        iv = idx_ref[...]
        acc = jnp.where(iv < 0, NEG, 0.0).astype(F32)
        for b in range(NUM_BUCKETS):
            acc = acc + jnp.where(iv == b, rb_ref[b, h], 0.0)
        o_ref[...] = acc

    return pl.pallas_call(
        body, name=name, grid=(ATT_Q_HEADS,),
        in_specs=[pl.BlockSpec(memory_space=pltpu.SMEM), pl.BlockSpec((BLOCK, 3 * BLOCK), lambda h: (0, 0))],
        out_specs=pl.BlockSpec((None, BLOCK, 3 * BLOCK), lambda h: (h, 0, 0)),
        out_shape=jax.ShapeDtypeStruct((ATT_Q_HEADS, BLOCK, 3 * BLOCK), F32), compiler_params=_params("parallel"),
    )(rel_bias, idx)


def _bias_grad(ds_sum, name):
    bucket, in_band = _t5_bucket_table()
    idx = jnp.asarray(np.where(in_band, bucket, -1))

    def body(ds_ref, idx_ref, o_ref):
        iv, ds = idx_ref[...], ds_ref[...]
        for b in range(NUM_BUCKETS):
            part = jnp.sum(jnp.where(iv == b, ds, 0.0), axis=0, keepdims=True)
            o_ref[b:b + 1, :] = part[:, 0:BLOCK] + part[:, BLOCK:2 * BLOCK] + part[:, 2 * BLOCK:3 * BLOCK]

    return pl.pallas_call(
        body, name=name, grid=(ATT_Q_HEADS,),
        in_specs=[pl.BlockSpec((None, BLOCK, 3 * BLOCK), lambda h: (h, 0, 0)), pl.BlockSpec((BLOCK, 3 * BLOCK), lambda h: (0, 0))],
        out_specs=pl.BlockSpec((None, NUM_BUCKETS, BLOCK), lambda h: (h, 0, 0)),
        out_shape=jax.ShapeDtypeStruct((ATT_Q_HEADS, NUM_BUCKETS, BLOCK), F32), compiler_params=_params("parallel"),
    )(ds_sum, idx)


Q_COL = 5 * HG_WIDTH
KV_COL = Q_COL + ATT_WIDTH
GROUP_WIDTH = ATT_GROUP * ATT_HEAD_DIM


def _stack_heads(blk):
    dh = ATT_HEAD_DIM
    return jnp.concatenate([blk[:, g * dh:(g + 1) * dh] for g in range(ATT_GROUP)], axis=0)


def _unstack_heads(st):
    return jnp.concatenate([st[g * BLOCK:(g + 1) * BLOCK] for g in range(ATT_GROUP)], axis=1)


def _rms_rows(x):
    rstd = lax.rsqrt(jnp.mean(x * x, axis=-1, keepdims=True) + EPS)
    return x * rstd, rstd


def _edge_ok(n, nb):
    colid = lax.broadcasted_iota(jnp.int32, (ATT_GROUP * BLOCK, 3 * BLOCK), 1)
    return jnp.logical_and(jnp.logical_or(colid >= BLOCK, n > 0), jnp.logical_or(colid < 2 * BLOCK, n < nb - 1))


def _sink_column(sink_ref, j=0):
    heads = range(j * ATT_GROUP, (j + 1) * ATT_GROUP)
    return jnp.concatenate([jnp.broadcast_to(sink_ref[h][:, 0:1], (BLOCK, 1)) for h in heads], axis=0)


def _attn_fwd(z, q_g, k_g, sink, bias, name):
    S = z.shape[0]
    nb = S // BLOCK
    G, dh, KV = ATT_GROUP, ATT_HEAD_DIM, ATT_KV_HEADS
    scale = 1.0 / math.sqrt(dh)

    def body(q_ref, kv0, kv1, kv2, qg_ref, kg_ref, sink_ref, bias_ref, o_ref):
        n = pl.program_id(0)
        edge_ok = _edge_ok(n, nb)
        cat = jnp.concatenate([kv0[...], kv1[...], kv2[...]], axis=0)
        qblk = q_ref[...]
        kn = [(_rms_rows(cat[:, j * dh:(j + 1) * dh])[0] * kg_ref[...]).astype(BF16) for j in range(KV)]
        vb = [cat[:, (KV + j) * dh:(KV + j + 1) * dh].astype(BF16) for j in range(KV)]
        qn = [(_rms_rows(_stack_heads(qblk[:, j * GROUP_WIDTH:(j + 1) * GROUP_WIDTH]))[0] * (qg_ref[...] * scale)).astype(BF16)
              for j in range(KV)]
        s = [_nt(qn[j], kn[j]) + bias_ref[j * G:(j + 1) * G].reshape(G * BLOCK, 3 * BLOCK) for j in range(KV)]
        s = [jnp.where(edge_ok, sj, NEG) for sj in s]
        sinks = [_sink_column(sink_ref, j) for j in range(KV)]
        m = [jnp.maximum(jnp.max(s[j], axis=-1, keepdims=True), sinks[j]) for j in range(KV)]
        e = [jnp.exp(s[j] - m[j]) for j in range(KV)]
        den = [jnp.sum(e[j], axis=-1, keepdims=True) + jnp.exp(sinks[j] - m[j]) for j in range(KV)]
        o = [_nn(e[j].astype(BF16), vb[j]) * (1.0 / den[j]) for j in range(KV)]
        o_ref[...] = jnp.concatenate([_unstack_heads(oj) for oj in o], axis=1).astype(o_ref.dtype)

    def kv(shift):
        return pl.BlockSpec((BLOCK, 2 * KV_WIDTH), lambda n: (jnp.clip(n + shift, 0, nb - 1), KV_COL // (2 * KV_WIDTH)))

    gain = pl.BlockSpec((1, dh), lambda n: (0, 0))
    return pl.pallas_call(
        body, name=name, grid=(nb,),
        in_specs=[pl.BlockSpec((BLOCK, ATT_WIDTH), lambda n: (n, Q_COL // ATT_WIDTH)), kv(-1), kv(0), kv(1), gain, gain,
                  pl.BlockSpec((ATT_Q_HEADS, 1, BLOCK), lambda n: (0, 0, 0)),
                  pl.BlockSpec((ATT_Q_HEADS, BLOCK, 3 * BLOCK), lambda n: (0, 0, 0))],
        out_specs=pl.BlockSpec((BLOCK, ATT_WIDTH), lambda n: (n, 0)),
        out_shape=jax.ShapeDtypeStruct((S, ATT_WIDTH), BF16), compiler_params=_params("parallel"),
    )(z, z, z, z, q_g, k_g, sink, bias)


def _attn_bwd(z, q_g, k_g, sink, bias, do, name):
    S = z.shape[0]
    nb = S // BLOCK
    G, dh, KV = ATT_GROUP, ATT_HEAD_DIM, ATT_KV_HEADS
    scale = 1.0 / math.sqrt(dh)
    both = range(KV)

    def body(q_ref, kv0, kv1, kv2, qg_ref, kg_ref, sink_ref, bias_ref, do_ref,
             dq_ref, dkw_ref, dvw_ref, ds_ref, dsink_ref, dqg_ref):
        n = pl.program_id(0)

        @pl.when(n == 0)
        def _():
            ds_ref[...] = jnp.zeros_like(ds_ref)
            dsink_ref[...] = jnp.zeros_like(dsink_ref)
            dqg_ref[...] = jnp.zeros_like(dqg_ref)

        edge_ok = _edge_ok(n, nb)
        qg = qg_ref[...]
        cat = jnp.concatenate([kv0[...], kv1[...], kv2[...]], axis=0)
        qblk, doblk = q_ref[...], do_ref[...]
        kn = [(_rms_rows(cat[:, j * dh:(j + 1) * dh])[0] * kg_ref[...]).astype(BF16) for j in both]
        vb = [cat[:, (KV + j) * dh:(KV + j + 1) * dh].astype(BF16) for j in both]
        norm = [_rms_rows(_stack_heads(qblk[:, j * GROUP_WIDTH:(j + 1) * GROUP_WIDTH])) for j in both]
        qn = [(norm[j][0] * (qg * scale)).astype(BF16) for j in both]
        do_b = [_stack_heads(doblk[:, j * GROUP_WIDTH:(j + 1) * GROUP_WIDTH]).astype(BF16) for j in both]
        s = [_nt(qn[j], kn[j]) + bias_ref[j * G:(j + 1) * G].reshape(G * BLOCK, 3 * BLOCK) for j in both]
        dp = [_nt(do_b[j], vb[j]) for j in both]
        s = [jnp.where(edge_ok, sj, NEG) for sj in s]
        sinks = [_sink_column(sink_ref, j) for j in both]
        m = [jnp.maximum(jnp.max(s[j], axis=-1, keepdims=True), sinks[j]) for j in both]
        e = [jnp.exp(s[j] - m[j]) for j in both]
        e_sink = [jnp.exp(sinks[j] - m[j]) for j in both]
        inv = [1.0 / (jnp.sum(e[j], axis=-1, keepdims=True) + e_sink[j]) for j in both]
        p = [e[j] * inv[j] for j in both]
        delta = [jnp.sum(p[j] * dp[j], axis=-1, keepdims=True) for j in both]
        ds = [p[j] * (dp[j] - delta[j]) for j in both]
        ds_b = [dsj.astype(BF16) for dsj in ds]
        dqn = [_nn(ds_b[j], kn[j]) * scale for j in both]
        for j in both:
            dvw_ref[j] = _tn(p[j].astype(BF16), do_b[j])
            dkw_ref[j] = _tn(ds_b[j], qn[j])
        for j in both:
            ds_ref[j * G:(j + 1) * G] += ds[j].reshape(G, BLOCK, 3 * BLOCK)
            sink_term = e_sink[j] * inv[j] * delta[j]
            for g in range(G):
                dsink_ref[j * G + g] += (jnp.zeros((1, BLOCK), F32)
                                         - jnp.sum(sink_term[g * BLOCK:(g + 1) * BLOCK], axis=0, keepdims=True))
        dq = []
        for j in both:
            qhat, rstd = norm[j]
            dqg_ref[j] += jnp.sum(dqn[j] * qhat, axis=0, keepdims=True)
            dqh = dqn[j] * qg
            dq.append(_unstack_heads(rstd * (dqh - qhat * jnp.mean(dqh * qhat, axis=-1, keepdims=True))))
        dq_ref[...] = jnp.concatenate(dq, axis=1).astype(dq_ref.dtype)

    def kv(shift):
        return pl.BlockSpec((BLOCK, 2 * KV_WIDTH), lambda n: (jnp.clip(n + shift, 0, nb - 1), KV_COL // (2 * KV_WIDTH)))

    gain = pl.BlockSpec((1, dh), lambda n: (0, 0))
    sink_spec = pl.BlockSpec((ATT_Q_HEADS, 1, BLOCK), lambda n: (0, 0, 0))
    bias_spec = pl.BlockSpec((ATT_Q_HEADS, BLOCK, 3 * BLOCK), lambda n: (0, 0, 0))
    win = pl.BlockSpec((KV, None, 3 * BLOCK, dh), lambda n: (0, n, 0, 0))
    wshape = jax.ShapeDtypeStruct((KV, nb, 3 * BLOCK, dh), F32)
    return pl.pallas_call(
        body, name=name, grid=(nb,),
        in_specs=[pl.BlockSpec((BLOCK, ATT_WIDTH), lambda n: (n, Q_COL // ATT_WIDTH)), kv(-1), kv(0), kv(1), gain, gain,
                  sink_spec, bias_spec, pl.BlockSpec((BLOCK, ATT_WIDTH), lambda n: (n, HG_WIDTH // ATT_WIDTH))],
        out_specs=[pl.BlockSpec((BLOCK, ATT_WIDTH), lambda n: (n, 0)), win, win, bias_spec, sink_spec,
                   pl.BlockSpec((KV, 1, dh), lambda n: (0, 0, 0))],
        out_shape=[jax.ShapeDtypeStruct((S, ATT_WIDTH), BF16), wshape, wshape,
                   jax.ShapeDtypeStruct((ATT_Q_HEADS, BLOCK, 3 * BLOCK), F32),
                   jax.ShapeDtypeStruct((ATT_Q_HEADS, 1, BLOCK), F32),
                   jax.ShapeDtypeStruct((KV, 1, dh), F32)],
        compiler_params=_params("arbitrary"),
    )(z, z, z, z, q_g, k_g, sink, bias, do)


def _attn_kv_reduce(dkw, dvw, z, k_g, name):
    S = z.shape[0]
    nb = S // BLOCK
    dh = ATT_HEAD_DIM
    kb = min(8, nb)
    steps = nb // kb

    def body(a_lo, a, a_hi, b_lo, b, b_hi, kv_ref, kg_ref, dkv_ref, dkg_ref):
        n = pl.program_id(0)

        @pl.when(n == 0)
        def _():
            dkg_ref[...] = jnp.zeros_like(dkg_ref)

        lo = jnp.where(n > 0, 1.0, 0.0)
        hi = jnp.where(n < steps - 1, 1.0, 0.0)

        def overlap_add(w, w_lo, w_hi, j, i):
            before = lo * w_lo[j] if i == 0 else w[j, i - 1, 2 * BLOCK:3 * BLOCK, :]
            after = hi * w_hi[j] if i == kb - 1 else w[j, i + 1, 0:BLOCK, :]
            return w[j, i, BLOCK:2 * BLOCK, :] + before + after

        dkg = [jnp.zeros((1, dh), F32) for _ in range(ATT_KV_HEADS)]
        for i in range(kb):
            rows = slice(i * BLOCK, (i + 1) * BLOCK)
            dks, dvs = [], []
            for j in range(ATT_KV_HEADS):
                dkn = overlap_add(a, a_lo, a_hi, j, i)
                dvs.append(overlap_add(b, b_lo, b_hi, j, i))
                khat, rstd = _rms_rows(kv_ref[rows, j * dh:(j + 1) * dh])
                dkg[j] = dkg[j] + jnp.sum(dkn * khat, axis=0, keepdims=True)
                dkh = dkn * kg_ref[...]
                dks.append(rstd * (dkh - khat * jnp.mean(dkh * khat, axis=-1, keepdims=True)))
            dkv_ref[rows, :] = jnp.concatenate(dks + dvs, axis=1).astype(dkv_ref.dtype)
        for j in range(ATT_KV_HEADS):
            dkg_ref[j] += dkg[j]

    main = pl.BlockSpec((ATT_KV_HEADS, kb, 3 * BLOCK, dh), lambda n: (0, n, 0, 0))
    halo_lo = pl.BlockSpec((ATT_KV_HEADS, None, BLOCK, dh), lambda n: (0, jnp.maximum(n * kb - 1, 0), 2, 0))
    halo_hi = pl.BlockSpec((ATT_KV_HEADS, None, BLOCK, dh), lambda n: (0, jnp.minimum(n * kb + kb, nb - 1), 0, 0))
    return pl.pallas_call(
        body, name=name, grid=(steps,),
        in_specs=[halo_lo, main, halo_hi, halo_lo, main, halo_hi,
                  pl.BlockSpec((kb * BLOCK, 2 * KV_WIDTH), lambda n: (n, KV_COL // (2 * KV_WIDTH))),
                  pl.BlockSpec((1, dh), lambda n: (0, 0))],
        out_specs=[pl.BlockSpec((kb * BLOCK, 2 * KV_WIDTH), lambda n: (n, 0)),
                   pl.BlockSpec((ATT_KV_HEADS, 1, dh), lambda n: (0, 0, 0))],
        out_shape=[jax.ShapeDtypeStruct((S, 2 * KV_WIDTH), BF16), jax.ShapeDtypeStruct((ATT_KV_HEADS, 1, dh), F32)],
        compiler_params=_params("arbitrary"),
    )(dkw, dkw, dkw, dvw, dvw, dvw, z, k_g)


def _ada_fwd(c_act, w, b, name):
    n = w.shape[1]

    def body(c_ref, w_ref, b_ref, o_ref):
        o_ref[...] = _nn(c_ref[...], w_ref[...], precision=lax.Precision.HIGHEST) + b_ref[...]

    tn = n // 3
    return pl.pallas_call(
        body, name=name, grid=(3,),
        in_specs=[pl.BlockSpec(c_act.shape, lambda j: (0, 0)), pl.BlockSpec((w.shape[0], tn), lambda j: (0, j)),
                  pl.BlockSpec((1, tn), lambda j: (0, j))],
        out_specs=pl.BlockSpec((c_act.shape[0], tn), lambda j: (0, j)),
        out_shape=jax.ShapeDtypeStruct((c_act.shape[0], n), F32), compiler_params=_params("parallel"),
    )(c_act, w, b)


def _ada_wgrad(c_act_t, dm, name):
    D, nbatch = c_act_t.shape
    n = dm.shape[1]
    tr = 256

    def body(c_ref, dm_ref, o_ref):
        cv, dv = c_ref[...], dm_ref[...]
        acc = cv[:, 0:1] * dv[0:1, :]
        for b in range(1, nbatch):
            acc = acc + cv[:, b:b + 1] * dv[b:b + 1, :]
        o_ref[...] = acc

    return pl.pallas_call(
        body, name=name, grid=(D // tr,),
        in_specs=[pl.BlockSpec((tr, nbatch), lambda i: (i, 0)), pl.BlockSpec((nbatch, n), lambda i: (0, 0))],
        out_specs=pl.BlockSpec((tr, n), lambda i: (i, 0)), out_shape=jax.ShapeDtypeStruct((D, n), F32),
        compiler_params=_params("parallel"),
    )(c_act_t, dm)


def _to_bf16(w, name):
    R, Cn = w.shape
    tr = _row_tile(R)

    def body(w_ref, o_ref):
        o_ref[...] = w_ref[...].astype(BF16)

    blk = pl.BlockSpec((tr, Cn), lambda i: (i, 0))
    return pl.pallas_call(
        body, name=name, grid=(R // tr,), in_specs=[blk], out_specs=blk, out_shape=jax.ShapeDtypeStruct((R, Cn), BF16),
        compiler_params=_params("parallel"),
    )(w)


def _adamw(w, g, m, v, name):
    R, Cn = w.shape
    tr = R
    for cand in (256, 128, 64, 32, 16, 8):
        if R % cand == 0:
            tr = cand
            break

    def body(w_ref, g_ref, m_ref, v_ref, d_ref, nm_ref, nv_ref):
        gv = g_ref[...]
        m_new = ADAM_B1 * m_ref[...] + (1.0 - ADAM_B1) * gv
        v_new = ADAM_B2 * v_ref[...] + (1.0 - ADAM_B2) * (gv * gv)
        m_hat = m_new / (1.0 - ADAM_B1 ** ADAM_STEP)
        v_hat = v_new / (1.0 - ADAM_B2 ** ADAM_STEP)
        d_ref[...] = -ADAM_LR * (m_hat / (jnp.sqrt(v_hat) + ADAM_EPS) + ADAM_WD * w_ref[...])
        nm_ref[...] = m_new
        nv_ref[...] = v_new

    blk = pl.BlockSpec((tr, Cn), lambda i: (i, 0))
    shp = jax.ShapeDtypeStruct((R, Cn), F32)
    return pl.pallas_call(
        body, name=name, grid=(R // tr,), in_specs=[blk] * 4, out_specs=[blk] * 3, out_shape=[shp] * 3,
        compiler_params=_params("parallel"),
    )(w, g, m, v)


def _place():
    return lax.axis_index("x"), lax.axis_index("y"), lax.axis_index("c")


def _flip(place, k):
    x, y, c = place
    return (1 - x if k & 4 else x, 1 - y if k & 2 else y, 1 - c if k & 1 else c)


def _dev_index(place):
    x, y, c = place
    return 4 * x + 2 * y + c


def _chip_index(place):
    return 2 * place[0] + place[1]


def _allgather8(x, name, reduce=False):
    R, Cn = x.shape

    def body(x_ref, *rest):
        if reduce:
            out_ref, sum_ref, send_sems, recv_sems, local_sem = rest
        else:
            out_ref, send_sems, recv_sems, local_sem = rest
        me = _place()
        mine = pltpu.make_async_copy(x_ref, out_ref.at[_dev_index(me)], local_sem)
        mine.start()

        def copy(k, origin, to):
            return pltpu.make_async_remote_copy(
                src_ref=x_ref, dst_ref=out_ref.at[_dev_index(origin)], send_sem=send_sems.at[k - 1],
                recv_sem=recv_sems.at[k - 1], device_id=to, device_id_type=MESH)

        sends = [copy(k, me, _flip(me, k)) for k in range(1, 8)]
        for cp in sends:
            cp.start()
        for k in range(1, 8):
            copy(k, _flip(me, k), me).wait_recv()
        for cp in sends:
            cp.wait_send()
        mine.wait()
        if reduce:
            acc = out_ref[0]
            for i in range(1, 8):
                acc = acc + out_ref[i]
            sum_ref[...] = acc

    vm = pl.BlockSpec(memory_space=pltpu.VMEM)
    outs = [jax.ShapeDtypeStruct((8, R, Cn), F32)] + ([jax.ShapeDtypeStruct((R, Cn), F32)] if reduce else [])
    res = pl.pallas_call(
        body, name=name, in_specs=[vm], out_specs=[vm] * len(outs), out_shape=outs,
        scratch_shapes=[pltpu.SemaphoreType.DMA((7,)), pltpu.SemaphoreType.DMA((7,)), pltpu.SemaphoreType.DMA],
    )(x)
    return res if reduce else res[0]


def _weights_allgather(shards, name):
    n = len(shards)
    per = 8

    def body(*refs):
        in_refs, out_refs = refs[:n], refs[n:2 * n]
        send_sems, recv_sems = refs[2 * n:]
        me = _place()
        c = me[2]
        sibling = _flip(me, 1)
        others = [_flip(me, 2 * j) for j in (1, 2, 3)]

        def copy(a, k, src, dst, to):
            return pltpu.make_async_remote_copy(
                src_ref=src, dst_ref=dst, send_sem=send_sems.at[per * a + k], recv_sem=recv_sems.at[per * a + k],
                device_id=to, device_id_type=MESH)

        def block(a, place, half):
            return out_refs[a].at[_chip_index(place), half]

        started = []
        for a in range(n):
            sends = [copy(a, 0, in_refs[a].at[c], block(a, me, c), sibling),
                     copy(a, 7, in_refs[a].at[1 - c], block(a, me, 1 - c), sibling)]
            sends += [copy(a, 1 + j, in_refs[a].at[c], block(a, me, c), to) for j, to in enumerate(others)]
            for cp in sends:
                cp.start()
            started += sends
        for a in range(n):
            for j, other in enumerate(others):
                landed = block(a, other, c)
                copy(a, 1 + j, landed, landed, me).wait_recv()
                fwd = copy(a, 4 + j, landed, landed, sibling)
                fwd.start()
                started.append(fwd)
        for a in range(n):
            copy(a, 0, block(a, me, 1 - c), block(a, me, 1 - c), me).wait_recv()
            copy(a, 7, block(a, me, c), block(a, me, c), me).wait_recv()
            for j, other in enumerate(others):
                got = block(a, other, 1 - c)
                copy(a, 4 + j, got, got, me).wait_recv()
        for cp in started:
            cp.wait_send()

    return pl.pallas_call(
        body, name=name, in_specs=[ANY] * n, out_specs=[ANY] * n,
        out_shape=[jax.ShapeDtypeStruct((N_CHIPS,) + s.shape, s.dtype) for s in shards],
        scratch_shapes=[pltpu.SemaphoreType.DMA((per * n,)), pltpu.SemaphoreType.DMA((per * n,))],
    )(*shards)


def _remote(src, dst, send_sems, recv_sems, i, to):
    return pltpu.make_async_remote_copy(
        src_ref=src, dst_ref=dst, send_sem=send_sems.at[i], recv_sem=recv_sems.at[i], device_id=to, device_id_type=MESH)


def _symmetric_plan(copies):
    def plan(in_refs, out_refs, send_sems, recv_sems):
        sends = [_remote(src, dst, send_sems, recv_sems, i, to) for i, (src, dst, to) in enumerate(copies(in_refs, out_refs))]
        return sends, sends
    return plan


def _halves_exchange(grads):
    def copies(in_refs, out_refs):
        me = _place()
        return [(g.at[kk, 1 - me[2]], got.at[kk], _flip(me, 1)) for g, got in zip(in_refs, out_refs) for kk in range(N_CHIPS)]

    return _Exchange(grads, [jax.ShapeDtypeStruct((N_CHIPS,) + g.shape[2:], g.dtype) for g in grads],
                     N_CHIPS * len(grads), _symmetric_plan(copies))


def _chips_exchange(parts):
    def copies(in_refs, out_refs):
        me = _place()
        return [(p.at[_chip_index(_flip(me, 2 * j))], got.at[j - 1], _flip(me, 2 * j))
                for p, got in zip(in_refs, out_refs) for j in (1, 2, 3)]

    return _Exchange(parts, [jax.ShapeDtypeStruct((3,) + p.shape[1:], p.dtype) for p in parts], 3 * len(parts),
                     _symmetric_plan(copies))


def _siblings_exchange(halves):
    def copies(in_refs, out_refs):
        sibling = _flip(_place(), 1)
        return [(h, got, sibling) for h, got in zip(in_refs, out_refs)]

    return _Exchange(halves, [jax.ShapeDtypeStruct(h.shape, h.dtype) for h in halves], len(halves), _symmetric_plan(copies))


def _ici_gather_plan(n, base=0):
    def plan(in_refs, out_refs, send_sems, recv_sems):
        me = _place()
        c = me[2]
        sends, recvs = [], []
        for a, (w, out) in enumerate(zip(in_refs[:n], out_refs)):
            for j in (1, 2, 3):
                i = base + 3 * a + j - 1
                sends.append(_remote(w.at[c], out.at[_chip_index(me), c], send_sems, recv_sems, i, _flip(me, 2 * j)))
                z = out.at[_chip_index(_flip(me, 2 * j)), c]
                recvs.append(_remote(z, z, send_sems, recv_sems, i, me))
        return sends, recvs
    return plan


def _d2d_gather_plan(n, base=0):
    def plan(in_refs, out_refs, send_sems, recv_sems):
        me = _place()
        c = me[2]
        sibling = _flip(me, 1)
        mine = _chip_index(me)
        sends, recvs = [], []
        for a, (w, out) in enumerate(zip(in_refs[:n], out_refs)):
            moves = [(w.at[c], (mine, c)), (w.at[1 - c], (mine, 1 - c))]
            moves += [(out.at[_chip_index(_flip(me, 2 * j)), c], (_chip_index(_flip(me, 2 * j)), c)) for j in (1, 2, 3)]
            for k, (src, (chip, half)) in enumerate(moves):
                sends.append(_remote(src, out.at[chip, half], send_sems, recv_sems, base + 5 * a + k, sibling))
            lands = [(mine, 1 - c), (mine, c)] + [(_chip_index(_flip(me, 2 * j)), 1 - c) for j in (1, 2, 3)]
            for k, (chip, half) in enumerate(lands):
                z = out.at[chip, half]
                recvs.append(_remote(z, z, send_sems, recv_sems, base + 5 * a + k, me))
        return sends, recvs
    return plan


def _gathered_shapes(shards):
    return [jax.ShapeDtypeStruct((N_CHIPS,) + s.shape, s.dtype) for s in shards]


def _gather_over_ici(shards):
    return _Exchange(shards, _gathered_shapes(shards), 3 * len(shards), _ici_gather_plan(len(shards)))


def _gather_over_d2d(shards, gathered):
    n = len(shards)
    return _Exchange(list(shards) + list(gathered), [jax.ShapeDtypeStruct(g.shape, g.dtype) for g in gathered], 5 * n,
                     _d2d_gather_plan(n), aliases={n + a: a for a in range(n)})


def _gather_in_one(shards):
    n = len(shards)
    return _Exchange(shards, _gathered_shapes(shards), 8 * n, _ici_gather_plan(n), then=_d2d_gather_plan(n, base=3 * n))


def _row_tile(rows):
    for cand in (256, 176, 128, 64, 32, 16, 8):
        if rows % cand == 0:
            return cand
    return rows


def _pair_sum(core, grad, theirs, name):
    N, _, R, Cn = grad.shape
    tr = R

    def body(core_ref, g_ref, t_ref, o_ref, ob_ref):
        s = g_ref[...] + t_ref[...]
        o_ref[...] = s
        ob_ref[...] = s.astype(BF16)

    out = pl.BlockSpec((None, tr, Cn), lambda k, i, core_ref: (k, i, 0))
    return pl.pallas_call(
        body, name=name,
        grid_spec=pltpu.PrefetchScalarGridSpec(
            num_scalar_prefetch=1, grid=(N, R // tr),
            in_specs=[pl.BlockSpec((None, None, tr, Cn), lambda k, i, core_ref: (k, core_ref[0], i, 0)),
                      pl.BlockSpec((None, tr, Cn), lambda k, i, core_ref: (k, i, 0))],
            out_specs=[out, out]),
        out_shape=[jax.ShapeDtypeStruct((N, R, Cn), F32), jax.ShapeDtypeStruct((N, R, Cn), BF16)],
        compiler_params=_params("parallel", "parallel"),
    )(core, grad, theirs)


def _chip_sum(chip, parts, landed, name):
    _, R, Cn = parts.shape
    tr = R

    def body(chip_ref, p_ref, l_ref, o_ref):
        o_ref[...] = ((p_ref[...] + l_ref[0].astype(F32)) + l_ref[1].astype(F32)) + l_ref[2].astype(F32)

    return pl.pallas_call(
        body, name=name,
        grid_spec=pltpu.PrefetchScalarGridSpec(
            num_scalar_prefetch=1, grid=(R // tr,),
            in_specs=[pl.BlockSpec((None, tr, Cn), lambda i, chip_ref: (chip_ref[0], i, 0)),
                      pl.BlockSpec((3, tr, Cn), lambda i, chip_ref: (0, i, 0))],
            out_specs=pl.BlockSpec((tr, Cn), lambda i, chip_ref: (i, 0))),
        out_shape=jax.ShapeDtypeStruct((R, Cn), F32), compiler_params=_params("parallel"),
    )(chip, parts, landed)


def _pair_sums(core, grads, theirs, tag):
    return [_pair_sum(core, g, t, f"{tag}_pair_sum_{i}") for i, (g, t) in enumerate(zip(grads, theirs))]


def _chip_sums(chip, parts, landed, tag):
    return [_chip_sum(chip, p[0], l, f"{tag}_chip_sum_{i}") for i, (p, l) in enumerate(zip(parts, landed))]


def _by_chip_rows(g):
    return g.reshape(N_CHIPS, 2, g.shape[0] // (2 * N_CHIPS), g.shape[1])


def _by_chip_cols(g):
    return g.reshape(N_CHIPS, 2, g.shape[1] // 2, g.shape[2])


def _adamw_halves(core, w, g_mine, g_theirs, m, v, name):
    R2, Cn = w.shape
    r = R2 // 2
    tr = _row_tile(r)
    nt = r // tr

    def body(core_ref, w_ref, gm_ref, gt_ref, m_ref, v_ref, g_ref, d_ref, nm_ref, nv_ref):
        gv = jnp.where(pl.program_id(0) == core_ref[0], gm_ref[...], gt_ref[...])
        g_ref[...] = gv
        m_new = ADAM_B1 * m_ref[...] + (1.0 - ADAM_B1) * gv
        v_new = ADAM_B2 * v_ref[...] + (1.0 - ADAM_B2) * (gv * gv)
        m_hat = m_new / (1.0 - ADAM_B1 ** ADAM_STEP)
        v_hat = v_new / (1.0 - ADAM_B2 ** ADAM_STEP)
        d_ref[...] = -ADAM_LR * (m_hat / (jnp.sqrt(v_hat) + ADAM_EPS) + ADAM_WD * w_ref[...])
        nm_ref[...] = m_new
        nv_ref[...] = v_new

    full = pl.BlockSpec((tr, Cn), lambda hf, i, core_ref: (hf * nt + i, 0))
    half = pl.BlockSpec((tr, Cn), lambda hf, i, core_ref: (i, 0))
    shp = jax.ShapeDtypeStruct((R2, Cn), F32)
    return pl.pallas_call(
        body, name=name,
        grid_spec=pltpu.PrefetchScalarGridSpec(
            num_scalar_prefetch=1, grid=(2, nt), in_specs=[full, half, half, full, full], out_specs=[full] * 4),
        out_shape=[shp] * 4, compiler_params=_params("parallel", "parallel"),
    )(core, w, g_mine, g_theirs, m, v)


def _pad_row(v, width):
    v = v.reshape(1, -1)
    return jnp.pad(v, ((0, 0), (0, width - v.shape[1])))


def _ffn1_forward(x, ng, shift, scale, gate, w_in4, w_out_shard, gather, next_norm):
    h = _rmsmod_fwd(x, ng, shift, scale, "ffn1_norm")
    (zg, zu, a), (partly, (w_out4,)) = _ffn_in_fwd(
        h, w_in4, "ffn1_in", exchange=[_gather_over_ici(gather), _gather_in_one([w_out_shard])])
    w_out = w_out4.reshape(D_FF, D_MODEL)
    (x_new, f, h_next), gathered = _proj_out_fwd([a], w_out, x, gate, 0.5, "ffn1_out", next_norm=next_norm,
                                                 exchange=_gather_over_d2d(gather, partly))
    return x_new, (h, zg, zu, a, f), w_out, gathered, h_next


def _ffn_backward(df, saved, w_in4, w_out, core, chip, tag, riding=None, norm=None):
    h, zg, zu, a = saved[:4]
    rode = None
    if riding:
        (dzg, dzu), rode = _dact_bwd(df, w_out, zg, zu, f"{tag}_dact", exchange=riding)
    else:
        dzg, dzu = _dact_bwd(df, w_out, zg, zu, f"{tag}_dact")
    g_out = [_by_chip_rows(_wgrad(a, [df], df.shape[1], f"{tag}_dw_out")[0].reshape(a.shape[1], df.shape[1]))]
    (dw_in,), theirs_out = _wgrad(h, [dzg, dzu], FF_SHARD, f"{tag}_dw_in", exchange=_halves_exchange(g_out))
    g_in = [_by_chip_cols(dw_in.reshape(N_CHIPS, h.shape[1], FF_SHARD))]
    parts_out = _pair_sums(core, g_out, theirs_out, f"{tag}_out")
    dh_outs, (theirs_in, landed_out) = _ffn_in_dgrad(
        dzg, dzu, w_in4, f"{tag}_dh", norm=norm, exchange=[_halves_exchange(g_in), _chips_exchange([parts_out[0][1]])])
    parts_in = _pair_sums(core, g_in, theirs_in, f"{tag}_in")
    return dh_outs, parts_in, _chip_sums(chip, parts_out, landed_out, f"{tag}_out"), rode


def kernel(x, c, w_ada, b_ada, norm_g, w_ffn1_in, w_ffn1_out, w_ffn2_in, w_ffn2_out, w_mix_in, w_mix_out, hgrn_lb, hgrn_norm_g, qk_norm_g, attn_sink, rel_bias, loss_target, m_w_ada, m_b_ada, m_norm_g, m_w_ffn1_in, m_w_ffn1_out, m_w_ffn2_in, m_w_ffn2_out, m_w_mix_in, m_w_mix_out, m_hgrn_lb, m_hgrn_norm_g, m_qk_norm_g, m_attn_sink, m_rel_bias, v_w_ada, v_b_ada, v_norm_g, v_w_ffn1_in, v_w_ffn1_out, v_w_ffn2_in, v_w_ffn2_out, v_w_mix_in, v_w_mix_out, v_hgrn_lb, v_hgrn_norm_g, v_qk_norm_g, v_attn_sink, v_rel_bias):
    D = D_MODEL
    S = x.shape[1]
    place = (lax.axis_index("x"), lax.axis_index("y"), lax.axis_index("c"))
    me, my_chip = _dev_index(place), _chip_index(place)
    x0 = x[0]
    target = loss_target[0]

    def halves(w, tag):
        return _to_bf16(w[0], f"{tag}_to_bf16").reshape(2, w.shape[1] // 2, w.shape[2])

    gathered = _weights_allgather([halves(w_ffn1_in, "w_ffn1_in")], "weights_allgather")
    w1_in = gathered[0].reshape(N_CHIPS, D, FF_SHARD)
    w1_out_shard = halves(w_ffn1_out, "w_ffn1_out")
    mix_shards = [halves(w_mix_in, "w_mix_in"), halves(w_mix_out, "w_mix_out")]
    ffn2_shards = [halves(w_ffn2_in, "w_ffn2_in"), halves(w_ffn2_out, "w_ffn2_out")]
    core_arr = jnp.reshape(place[2], (1,)).astype(jnp.int32)
    chip_arr = jnp.reshape(my_chip, (1,)).astype(jnp.int32)

    small = jnp.concatenate([_pad_row(c, D), _pad_row(norm_g, D), _pad_row(hgrn_lb, D), jnp.zeros((5, D), F32)], axis=0)
    small_all = _allgather8(small, "small_allgather")
    c_all = small_all[:, 0, :]
    by_chip = small_all[0::2]
    norm_g_full = by_chip[:, 1, :3 * 256].reshape(N_CHIPS, 3, 256).transpose(1, 0, 2).reshape(3, D)
    lb_raw = by_chip[:, 2, :2 * 2 * 128].reshape(N_CHIPS, 2, 2, 128).transpose(1, 2, 0, 3).reshape(2, 2, HG_WIDTH)
    lb = jax.nn.sigmoid(lb_raw[:, 0, :] - lb_raw[:, 1, :])
    lb_f, lb_b = lb[0:1], lb[1:2]

    c_act_all = c_all * jax.nn.sigmoid(c_all)
    n_ada = w_ada.shape[2]
    b_mine = lax.dynamic_slice_in_dim(b_ada, my_chip * n_ada, n_ada, axis=1)
    mods_part = _ada_fwd(c_act_all, w_ada[0], b_mine, "ada_fwd")
    mods_all = _allgather8(mods_part, "mods_allgather")[0::2].transpose(1, 0, 2).reshape(8, N_MOD * D)
    mods = lax.dynamic_slice_in_dim(mods_all, me, 1, axis=0)
    sh1, sc1, g1, sh2, sc2, g2, sh3, sc3, g3 = [mods[:, i * D:(i + 1) * D] for i in range(N_MOD)]

    x1, saved1, w1_out, gathered, h2 = _ffn1_forward(x0, norm_g_full[0:1], sh1, sc1, g1, w1_in, w1_out_shard, mix_shards,
                                                     (norm_g_full[1:2], sh2, sc2))
    wm_in = gathered[0].reshape(N_CHIPS, D, D_IN // N_CHIPS).transpose(1, 0, 2).reshape(D, D_IN)
    wm_out = gathered[1].reshape(D, D)

    z = _matmul_nn(h2, wm_in, F32, 256, "mix_in")
    (of, st_f), partly = _hgrn_fwd(z, lb_f, 0, "hgrn_fwd_f", exchange=_gather_over_ici(ffn2_shards))
    (ob, st_b), gathered = _hgrn_fwd(z, lb_b, 1, "hgrn_fwd_b", exchange=_gather_over_d2d(ffn2_shards, partly))
    w2_in = gathered[0].reshape(N_CHIPS, D, FF_SHARD)
    w2_out = gathered[1].reshape(D_FF, D)
    o_h = _hgrn_post_fwd(of, ob, z, hgrn_norm_g, "hgrn_post")

    q_g, k_g = qk_norm_g[0, 0:1], qk_norm_g[0, 1:2]
    sink_b = jnp.broadcast_to(attn_sink.reshape(ATT_Q_HEADS, 1, 1), (ATT_Q_HEADS, 1, BLOCK))
    bias = _bias_table(rel_bias, "bias_table")
    o_a = _attn_fwd(z, q_g, k_g, sink_b, bias, "attn_fwd")
    x2, mixed, h3 = _proj_out_fwd([o_h, o_a], wm_out, x1, g2, 1.0, "mix_out", next_norm=(norm_g_full[2:3], sh3, sc3))

    zg3, zu3, a3 = _ffn_in_fwd(h3, w2_in, "ffn2_in")
    dx3, df3, dg3, sq_cols = _proj_out_loss(a3, w2_out, x2, g3, 0.5, target, "ffn2_out_loss")
    loss_mine = 0.5 * jnp.sum(sq_cols) / D

    (dx2, dsh3, dsc3, dng3, dmixed, dg2), parts2, mine2_out, _ = _ffn_backward(
        df3, (h3, zg3, zu3, a3), w2_in, w2_out, core_arr, chip_arr, "ffn2",
        norm=_NormBwd(x2, norm_g_full[2:3], sc3, dx3, below=(mixed, g2, 1.0)))

    (do_cat,) = _matmul_nt([dmixed], wm_out, ROW_TILE, "mix_out_dgrad")
    dwm_out = _wgrad_rows([o_h, o_a], dmixed, "mix_out_dw").reshape(D, D)

    do_sum, dgr, d_hnorm = _hgrn_post_bwd(do_cat, of, ob, z, hgrn_norm_g, "hgrn_post_bwd")
    (dq_f, dff, dv_f, doml_f), landed2 = _hgrn_bwd(z, lb_f, do_sum, st_f, 0, "hgrn_bwd_f",
                                                   exchange=_chips_exchange([p[1] for p in parts2]))
    mine2 = _chip_sums(chip_arr, parts2, landed2, "ffn2_in") + mine2_out
    (dhq, dfb, dhi, doml_b), theirs2 = _hgrn_bwd(z, lb_b, do_sum, st_b, 1, "hgrn_bwd_b", acc=(dq_f, dv_f),
                                                 exchange=_siblings_exchange(mine2))

    daq, dkw, dvw, ds_sum, dsink, dqg = _attn_bwd(z, q_g, k_g, sink_b, bias, do_cat, "attn_bwd")
    dkv, dkg = _attn_kv_reduce(dkw, dvw, z, k_g, "attn_kv_reduce")
    d_rel_bias = jnp.sum(_bias_grad(ds_sum, "bias_grad"), axis=-1).T
    dz = [dhq, dff, dfb, dhi, dgr, daq, dkv]
    dwm_in = _wgrad_pieces(h2, dz, 2 * KV_WIDTH, "mix_in_dw").transpose(1, 0, 2).reshape(D, D_IN)
    wide = D_IN // N_CHIPS
    grads_m = [_by_chip_cols(dwm_in.reshape(D, N_CHIPS, wide).transpose(1, 0, 2)), _by_chip_rows(dwm_out)]
    (dx1, dsh2, dsc2, dng2, df1, dg1), theirs_m = _matmul_nt(
        dz, wm_in, 256, "mix_in_dgrad", exchange=_halves_exchange(grads_m),
        norm=_NormBwd(x1, norm_g_full[1:2], sc2, dx2, below=(saved1[4], g1, 0.5)))
    parts_m = _pair_sums(core_arr, grads_m, theirs_m, "mix")

    (dh1,), parts1, mine1_out, landed_m = _ffn_backward(df1, saved1, w1_in, w1_out, core_arr, chip_arr, "ffn1",
                                                        riding=_chips_exchange([p[1] for p in parts_m]))
    mine_m = _chip_sums(chip_arr, parts_m, landed_m, "mix")
    (dx0, dsh1, dsc1, dng1), landed1 = _rmsmod_bwd(dh1, _NormBwd(x0, norm_g_full[0:1], sc1, dx1), "ffn1_norm_bwd",
                                                   exchange=_chips_exchange([p[1] for p in parts1]))
    mine1 = _chip_sums(chip_arr, parts1, landed1, "ffn1_in") + mine1_out
    theirs_1m = list(_run_exchange(_siblings_exchange(mine1 + mine_m), "siblings_exchange"))
    reduced = list(zip(mine1 + mine2 + mine_m, theirs_1m[:2] + list(theirs2) + theirs_1m[2:]))

    dlb = -jnp.concatenate([doml_f, doml_b], axis=0)
    dlb_raw = dlb * lb * (1.0 - lb)
    d_hgrn_lb = jnp.stack([dlb_raw, -dlb_raw], axis=1)
    d_qk = jnp.concatenate([jnp.sum(dqg, axis=0), jnp.sum(dkg, axis=0)], axis=0)
    dmods = jnp.concatenate([dsh1, dsc1, dg1, dsh2, dsc2, dg2, dsh3, dsc3, dg3], axis=0)
    packed = jnp.concatenate(
        [dmods, dng1, dng2, dng3, d_hgrn_lb.reshape(2, D), _pad_row(d_hnorm, D), _pad_row(d_qk, D),
         _pad_row(dsink[:, 0, 0], D), _pad_row(d_rel_bias, D), _pad_row(loss_mine, D)], axis=0)
    packed = jnp.pad(packed, ((0, 24 - packed.shape[0]), (0, 0)))
    packed_all, packed_sum = _allgather8(packed, "small_grads_allgather", reduce=True)
    dmods_all = packed_all[:, 0:N_MOD, :].reshape(8, N_MOD * D)
    g_b_ada = packed_sum[0:N_MOD].reshape(1, N_MOD * D)
    g_norm_full = packed_sum[9:12]
    g_norm_g = lax.dynamic_slice_in_dim(g_norm_full, my_chip * 256, 256, axis=1).reshape(1, 3, 256)
    g_hgrn_lb = lax.dynamic_slice_in_dim(packed_sum[12:14].reshape(2, 2, HG_WIDTH), my_chip * 128, 128, axis=2)
    g_hgrn_norm_g = packed_sum[14:15, :HG_WIDTH]
    g_qk_norm_g = packed_sum[15, :2 * ATT_HEAD_DIM].reshape(1, 2, ATT_HEAD_DIM)
    g_attn_sink = packed_sum[16:17, :ATT_Q_HEADS]
    g_rel_bias = packed_sum[17, :NUM_BUCKETS * ATT_Q_HEADS].reshape(NUM_BUCKETS, ATT_Q_HEADS)
    loss = packed_sum[18, 0]

    dm_mine = lax.dynamic_slice_in_dim(dmods_all, my_chip * n_ada, n_ada, axis=1)
    g_w_ada = _ada_wgrad(c_act_all.T, dm_mine, "ada_wgrad")[None]

    def big(w, g, m, v, name):
        d, nm, nv = _adamw(w[0], g[0], m[0], v[0], name)
        return d[None], nm[None], nv[None]

    def big_halves(w, g_pair, m, v, name):
        g, d, nm, nv = _adamw_halves(core_arr, w[0], g_pair[0], g_pair[1], m[0], v[0], name)
        return g[None], (d[None], nm[None], nv[None])

    g_w1_in, u_w1_in = big_halves(w_ffn1_in, reduced[0], m_w_ffn1_in, v_w_ffn1_in, "adamw_w_ffn1_in")
    g_w1_out, u_w1_out = big_halves(w_ffn1_out, reduced[1], m_w_ffn1_out, v_w_ffn1_out, "adamw_w_ffn1_out")
    g_w2_in, u_w2_in = big_halves(w_ffn2_in, reduced[2], m_w_ffn2_in, v_w_ffn2_in, "adamw_w_ffn2_in")
    g_w2_out, u_w2_out = big_halves(w_ffn2_out, reduced[3], m_w_ffn2_out, v_w_ffn2_out, "adamw_w_ffn2_out")
    g_wm_in, u_wm_in = big_halves(w_mix_in, reduced[4], m_w_mix_in, v_w_mix_in, "adamw_w_mix_in")
    g_wm_out, u_wm_out = big_halves(w_mix_out, reduced[5], m_w_mix_out, v_w_mix_out, "adamw_w_mix_out")

    smalls = [(b_ada, g_b_ada, m_b_ada, v_b_ada), (norm_g, g_norm_g, m_norm_g, v_norm_g), (hgrn_lb, g_hgrn_lb, m_hgrn_lb, v_hgrn_lb),
              (hgrn_norm_g, g_hgrn_norm_g, m_hgrn_norm_g, v_hgrn_norm_g), (qk_norm_g, g_qk_norm_g, m_qk_norm_g, v_qk_norm_g),
              (attn_sink, g_attn_sink, m_attn_sink, v_attn_sink), (rel_bias, g_rel_bias, m_rel_bias, v_rel_bias)]
    sizes = [t[0].size for t in smalls]
    total = sum(sizes)
    rows = -(-total // 128)
    rows = -(-rows // 8) * 8

    def pack(i):
        flat = jnp.concatenate([t[i].reshape(-1) for t in smalls])
        fill = 1.0 if i == 3 else 0.0
        return jnp.pad(flat, (0, rows * 128 - total), constant_values=fill).reshape(rows, 128)

    packed_out = _adamw(pack(0), pack(1), pack(2), pack(3), "adamw_small")

    def unpack(flat2d):
        flat = flat2d.reshape(-1)
        outs, off = [], 0
        for t, n in zip(smalls, sizes):
            outs.append(flat[off:off + n].reshape(t[0].shape))
            off += n
        return outs

    d_small, m_small, v_small = [unpack(t) for t in packed_out]

    upd = {
        "w_ada": big(w_ada, g_w_ada, m_w_ada, v_w_ada, "adamw_w_ada"),
        "w_ffn1_in": u_w1_in, "w_ffn1_out": u_w1_out, "w_ffn2_in": u_w2_in, "w_ffn2_out": u_w2_out,
        "w_mix_in": u_wm_in, "w_mix_out": u_wm_out,
    }
    small_names = ["b_ada", "norm_g", "hgrn_lb", "hgrn_norm_g", "qk_norm_g", "attn_sink", "rel_bias"]
    for i, nme in enumerate(small_names):
        upd[nme] = (d_small[i], m_small[i], v_small[i])
    grads = {
        "w_ada": g_w_ada, "b_ada": g_b_ada, "norm_g": g_norm_g, "w_ffn1_in": g_w1_in, "w_ffn1_out": g_w1_out,
        "w_ffn2_in": g_w2_in, "w_ffn2_out": g_w2_out, "w_mix_in": g_wm_in, "w_mix_out": g_wm_out, "hgrn_lb": g_hgrn_lb,
        "hgrn_norm_g": g_hgrn_norm_g, "qk_norm_g": g_qk_norm_g, "attn_sink": g_attn_sink, "rel_bias": g_rel_bias,
    }
    order = ["w_ada", "b_ada", "norm_g", "w_ffn1_in", "w_ffn1_out", "w_ffn2_in", "w_ffn2_out", "w_mix_in", "w_mix_out",
             "hgrn_lb", "hgrn_norm_g", "qk_norm_g", "attn_sink", "rel_bias"]
    return (loss, dx0[None], *[grads[k] for k in order], *[upd[k][0] for k in order], *[upd[k][1] for k in order],
            *[upd[k][2] for k in order])
```

```python
import functools
import math

import numpy as np
import jax
import jax.numpy as jnp
from jax import lax
from jax.experimental import pallas as pl
from jax.experimental.pallas import tpu as pltpu

F32, BF16 = jnp.float32, jnp.bfloat16

D_MODEL = 1024
D_FF = 2816
HG_HEADS, HG_DIM = 4, 128
HG_WIDTH = HG_HEADS * HG_DIM
ATT_Q_HEADS, ATT_KV_HEADS, ATT_HEAD_DIM = 8, 2, 64
ATT_GROUP = ATT_Q_HEADS // ATT_KV_HEADS
ATT_WIDTH = ATT_Q_HEADS * ATT_HEAD_DIM
KV_WIDTH = ATT_KV_HEADS * ATT_HEAD_DIM
WINDOW, BLOCK = 128, 128
NUM_BUCKETS, MAX_DISTANCE = 32, 128
N_MOD = 9
EPS = 1e-6
D_IN = 5 * HG_WIDTH + ATT_WIDTH + 2 * KV_WIDTH
ADAM_LR, ADAM_B1, ADAM_B2, ADAM_EPS, ADAM_WD, ADAM_STEP = 0.001, 0.9, 0.999, 1e-08, 0.01, 10

N_CHIPS = 4
FF_SHARD = 2 * D_FF // N_CHIPS
NEG = -1e30

VMEM_LIMIT_BYTES = 56 << 20
ROW_TILE = 512
HG_CHUNK = 16
HG_ROWS = 512

MESH = pl.DeviceIdType.MESH
ANY = pl.BlockSpec(memory_space=pl.ANY)


def _params(*sem):
    return pltpu.CompilerParams(dimension_semantics=sem, vmem_limit_bytes=VMEM_LIMIT_BYTES)


def _resident(shape, index_map):
    return pl.BlockSpec(shape, index_map, pipeline_mode=pl.Buffered(1))


def _dot(a, b, dims, precision=None):
    return lax.dot_general(a, b, (dims, ((), ())), precision=precision, preferred_element_type=F32)


def _nn(a, b, precision=None):
    return _dot(a, b, ((1,), (0,)), precision)


def _nt(a, b):
    return _dot(a, b, ((1,), (1,)))


def _tn(a, b):
    return _dot(a, b, ((0,), (0,)))


def _sigmoid(x):
    return jax.nn.sigmoid(x)


class _Exchange:
    def __init__(self, inputs, out_shapes, n_sems, plan, aliases=None, then=None):
        self.inputs, self.out_shapes, self.n_sems, self.plan, self.aliases = list(inputs), list(out_shapes), n_sems, plan, aliases or {}
        self.then = then

    def sem_shapes(self):
        return [pltpu.SemaphoreType.DMA((self.n_sems,)), pltpu.SemaphoreType.DMA((self.n_sems,))]

    def start(self, in_refs, out_refs, send_sems, recv_sems):
        for cp in self.plan(in_refs, out_refs, send_sems, recv_sems)[0]:
            cp.start()

    @staticmethod
    def _wait(sends, recvs):
        for cp in recvs:
            cp.wait_recv()
        for cp in sends:
            cp.wait_send()

    def switch(self, in_refs, out_refs, send_sems, recv_sems):
        if self.then:
            self._wait(*self.plan(in_refs, out_refs, send_sems, recv_sems))
            for cp in self.then(in_refs, out_refs, send_sems, recv_sems)[0]:
                cp.start()

    def finish(self, in_refs, out_refs, send_sems, recv_sems):
        self._wait(*(self.then or self.plan)(in_refs, out_refs, send_sems, recv_sems))


def _run_exchange(ex, name):
    n_in, n_out = len(ex.inputs), len(ex.out_shapes)

    def body(*refs):
        in_refs, out_refs, (send_sems, recv_sems) = refs[:n_in], refs[n_in:n_in + n_out], refs[n_in + n_out:]
        ex.start(in_refs, out_refs, send_sems, recv_sems)
        ex.switch(in_refs, out_refs, send_sems, recv_sems)
        ex.finish(in_refs, out_refs, send_sems, recv_sems)

    return pl.pallas_call(
        body, name=name, in_specs=[ANY] * n_in, out_specs=[ANY] * n_out, out_shape=ex.out_shapes,
        scratch_shapes=ex.sem_shapes(), input_output_aliases=dict(ex.aliases),
    )(*ex.inputs)


def _call(body, *, name, grid, in_specs, out_specs, out_shape, args, semantics, scratch_shapes=(), exchange=None):
    if exchange is None:
        return pl.pallas_call(
            body, name=name, grid=grid, in_specs=in_specs, out_specs=out_specs, out_shape=out_shape,
            scratch_shapes=list(scratch_shapes), compiler_params=_params(*semantics))(*args)
    exs = exchange if isinstance(exchange, (list, tuple)) else [exchange]
    n_in, n_out, n_scr = len(in_specs), len(out_specs), len(scratch_shapes)
    x_in, x_out = [len(ex.inputs) for ex in exs], [len(ex.out_shapes) for ex in exs]

    def take(refs, counts):
        groups = []
        for n in counts:
            groups.append(refs[:n])
            refs = refs[n:]
        return groups, refs

    def carrier(*refs):
        ins, refs = refs[:n_in], refs[n_in:]
        x_ins, refs = take(refs, x_in)
        outs, refs = refs[:n_out], refs[n_out:]
        x_outs, refs = take(refs, x_out)
        scr, refs = refs[:n_scr], refs[n_scr:]
        sems, _ = take(refs, [2] * len(exs))
        ids = [pl.program_id(a) for a in range(len(grid))]
        first = functools.reduce(jnp.logical_and, [i == 0 for i in ids])
        last = functools.reduce(jnp.logical_and, [i == g - 1 for i, g in zip(ids, grid)])
        step = functools.reduce(lambda acc, ig: acc * ig[1] + ig[0], zip(ids, grid), 0)

        @pl.when(first)
        def _():
            for ex, xi, xo, (send_sems, recv_sems) in zip(exs, x_ins, x_outs, sems):
                ex.start(xi, xo, send_sems, recv_sems)

        if any(ex.then for ex in exs):
            @pl.when(step == (3 * math.prod(grid)) // 4)
            def _():
                for ex, xi, xo, (send_sems, recv_sems) in zip(exs, x_ins, x_outs, sems):
                    ex.switch(xi, xo, send_sems, recv_sems)

        body(*ins, *outs, *scr)

        @pl.when(last)
        def _():
            for ex, xi, xo, (send_sems, recv_sems) in zip(exs, x_ins, x_outs, sems):
                ex.finish(xi, xo, send_sems, recv_sems)

    aliases, i0, o0 = {}, n_in, n_out
    for ex in exs:
        aliases.update({i0 + i: o0 + o for i, o in ex.aliases.items()})
        i0, o0 = i0 + len(ex.inputs), o0 + len(ex.out_shapes)
    res = pl.pallas_call(
        carrier, name=name, grid=grid, in_specs=list(in_specs) + [ANY] * sum(x_in),
        out_specs=list(out_specs) + [ANY] * sum(x_out),
        out_shape=list(out_shape) + [s for ex in exs for s in ex.out_shapes],
        scratch_shapes=list(scratch_shapes) + [s for ex in exs for s in ex.sem_shapes()],
        input_output_aliases=aliases, compiler_params=_params(*["arbitrary"] * len(grid)),
    )(*args, *[a for ex in exs for a in ex.inputs])
    x_res, _ = take(list(res[n_out:]), x_out)
    return list(res[:n_out]), (x_res if isinstance(exchange, (list, tuple)) else x_res[0])


def _rmsmod_fwd(x, g, shift, scale, name):
    S, D = x.shape
    tr = min(ROW_TILE, S)

    def body(x_ref, g_ref, sh_ref, sc_ref, h_ref):
        xv = x_ref[...]
        rstd = lax.rsqrt(jnp.mean(xv * xv, axis=-1, keepdims=True) + EPS)
        y = xv * rstd * g_ref[...]
        h_ref[...] = (y * (1.0 + sc_ref[...]) + sh_ref[...]).astype(h_ref.dtype)

    row = pl.BlockSpec((tr, D), lambda i: (i, 0))
    vec = pl.BlockSpec((1, D), lambda i: (0, 0))
    return pl.pallas_call(
        body, name=name, grid=(S // tr,), in_specs=[row, vec, vec, vec], out_specs=row,
        out_shape=jax.ShapeDtypeStruct((S, D), BF16), compiler_params=_params("parallel"),
    )(x, g, shift, scale)


class _NormBwd:
    def __init__(self, x, g, scale, dx_res, below=None):
        S, D = x.shape
        self.below, self.coef = below, (below[2] if below else None)
        self.inputs = [x, g, scale, dx_res] + ([below[0], below[1]] if below else [])
        vshape = jax.ShapeDtypeStruct((1, D), F32)
        self.out_shape = [jax.ShapeDtypeStruct((S, D), F32), vshape, vshape, vshape]
        if below:
            self.out_shape += [jax.ShapeDtypeStruct((S, D), BF16), vshape]

    def specs(self, tr, D):
        row = pl.BlockSpec((tr, D), lambda i: (i, 0))
        vec = pl.BlockSpec((1, D), lambda i: (0, 0))
        return ([row, vec, vec, row] + ([row, vec] if self.below else []),
                [row, vec, vec, vec] + ([row, vec] if self.below else []))

    def step(self, dhv, in_refs, out_refs):
        if self.below:
            x_ref, g_ref, sc_ref, dxr_ref, f_ref, gate_ref = in_refs
            dx_ref, dsh_ref, dsc_ref, dg_ref, df_ref, dgate_ref = out_refs
            sums = (dsh_ref, dsc_ref, dg_ref, dgate_ref)
        else:
            x_ref, g_ref, sc_ref, dxr_ref = in_refs
            dx_ref, dsh_ref, dsc_ref, dg_ref = out_refs
            sums = (dsh_ref, dsc_ref, dg_ref)

        @pl.when(pl.program_id(0) == 0)
        def _():
            for ref in sums:
                ref[...] = jnp.zeros_like(ref)

        xv, gv = x_ref[...], g_ref[...]
        one_sc = 1.0 + sc_ref[...]
        rstd = lax.rsqrt(jnp.mean(xv * xv, axis=-1, keepdims=True) + EPS)
        n = xv * rstd
        dsh_ref[...] += jnp.sum(dhv, axis=0, keepdims=True)
        dsc_ref[...] += jnp.sum(dhv * n, axis=0, keepdims=True) * gv
        dg_ref[...] += jnp.sum(dhv * n, axis=0, keepdims=True) * one_sc
        dn = dhv * (gv * one_sc)
        dx = dxr_ref[...] + rstd * (dn - n * jnp.mean(dn * n, axis=-1, keepdims=True))
        dx_ref[...] = dx
        if self.below:
            df_ref[...] = (self.coef * gate_ref[...] * dx).astype(df_ref.dtype)
            dgate_ref[...] += self.coef * jnp.sum(dx * f_ref[...].astype(F32), axis=0, keepdims=True)


def _rmsmod_bwd(dh, norm, name, exchange=None):
    S, D = dh.shape
    tr = min(ROW_TILE, S)
    n_in = len(norm.inputs)

    def body(dh_ref, *refs):
        norm.step(dh_ref[...], refs[:n_in], refs[n_in:])

    in_specs, out_specs = norm.specs(tr, D)
    return _call(body, name=name, grid=(S // tr,), in_specs=[pl.BlockSpec((tr, D), lambda i: (i, 0))] + in_specs,
                 out_specs=out_specs, out_shape=norm.out_shape, args=[dh] + norm.inputs, semantics=("arbitrary",),
                 exchange=exchange)


def _ffn_in_fwd(h, w4, name, exchange=None):
    S, D = h.shape
    tm = min(2 * ROW_TILE, S)
    n = w4.shape[2]

    def body(h_ref, wg_ref, wu_ref, zg_ref, zu_ref, a_ref):
        hv = h_ref[...]
        zg = _nn(hv, wg_ref[...])
        zu = _nn(hv, wu_ref[...])
        zg_ref[...] = zg.astype(zg_ref.dtype)
        zu_ref[...] = zu.astype(zu_ref.dtype)
        a_ref[...] = (zg * _sigmoid(zg) * zu).astype(a_ref.dtype)

    out = pl.BlockSpec((tm, n), lambda j, m: (m, j))
    oshape = jax.ShapeDtypeStruct((S, 2 * n), BF16)
    return _call(
        body, name=name, grid=(2, S // tm),
        in_specs=[pl.BlockSpec((tm, D), lambda j, m: (m, 0)),
                  pl.BlockSpec((None, D, n), lambda j, m: (j, 0, 0)),
                  pl.BlockSpec((None, D, n), lambda j, m: (j + 2, 0, 0))],
        out_specs=[out, out, out], out_shape=[oshape, oshape, oshape], args=(h, w4, w4),
        semantics=("parallel", "parallel"), exchange=exchange)


def _proj_out_fwd(lhs, w, x, gate, coef, name, exchange=None, next_norm=None):
    S, D = x.shape
    tm = min(ROW_TILE, S)
    ks = [a.shape[1] for a in lhs]

    def body(*refs):
        lhs_refs, refs = refs[:len(lhs)], refs[len(lhs):]
        if next_norm:
            w_ref, x_ref, gate_ref, g_ref, sh_ref, sc_ref, xn_ref, f_ref, h_ref = refs
        else:
            w_ref, x_ref, gate_ref, xn_ref, f_ref = refs
        acc, off = None, 0
        for a_ref, k in zip(lhs_refs, ks):
            part = _nn(a_ref[...], w_ref[off:off + k, :])
            acc = part if acc is None else acc + part
            off += k
        f_ref[...] = acc.astype(f_ref.dtype)
        xn = x_ref[...] + coef * gate_ref[...] * acc
        xn_ref[...] = xn
        if next_norm:
            rstd = lax.rsqrt(jnp.mean(xn * xn, axis=-1, keepdims=True) + EPS)
            h_ref[...] = (xn * rstd * g_ref[...] * (1.0 + sc_ref[...]) + sh_ref[...]).astype(h_ref.dtype)

    row = pl.BlockSpec((tm, D), lambda m: (m, 0))
    vec = pl.BlockSpec((1, D), lambda m: (0, 0))
    extra = list(next_norm) if next_norm else []
    return _call(
        body, name=name, grid=(S // tm,),
        in_specs=[pl.BlockSpec((tm, k), lambda m: (m, 0)) for k in ks]
        + [_resident(w.shape, lambda m: (0, 0)), row, vec] + [vec] * len(extra),
        out_specs=[row, row] + ([row] if next_norm else []),
        out_shape=[jax.ShapeDtypeStruct((S, D), F32), jax.ShapeDtypeStruct((S, D), BF16)]
        + ([jax.ShapeDtypeStruct((S, D), BF16)] if next_norm else []),
        args=(*lhs, w, x, gate, *extra), semantics=("parallel",), exchange=exchange)


def _proj_out_loss(lhs, w, x, gate, coef, target, name):
    S, D = x.shape
    tm = min(ROW_TILE, S)

    def body(a_ref, w_ref, x_ref, gate_ref, t_ref, dy_ref, df_ref, dgate_ref, sq_ref):
        @pl.when(pl.program_id(0) == 0)
        def _():
            dgate_ref[...] = jnp.zeros_like(dgate_ref)
            sq_ref[...] = jnp.zeros_like(sq_ref)

        f = _nn(a_ref[...], w_ref[...])
        gate = coef * gate_ref[...]
        err = x_ref[...] + gate * f - t_ref[...]
        sq_ref[...] += jnp.sum(err * err, axis=0, keepdims=True)
        dy = err * (1.0 / D)
        dy_ref[...] = dy
        df_ref[...] = (gate * dy).astype(df_ref.dtype)
        dgate_ref[...] += coef * jnp.sum(dy * f, axis=0, keepdims=True)

    row = pl.BlockSpec((tm, D), lambda m: (m, 0))
    vec = pl.BlockSpec((1, D), lambda m: (0, 0))
    vshape = jax.ShapeDtypeStruct((1, D), F32)
    return pl.pallas_call(
        body, name=name, grid=(S // tm,),
        in_specs=[pl.BlockSpec((tm, lhs.shape[1]), lambda m: (m, 0)), _resident(w.shape, lambda m: (0, 0)), row, vec, row],
        out_specs=[row, row, vec, vec],
        out_shape=[jax.ShapeDtypeStruct((S, D), F32), jax.ShapeDtypeStruct((S, D), BF16), vshape, vshape],
        compiler_params=_params("arbitrary"),
    )(lhs, w, x, gate, target)


def _matmul_nn(a, w, out_dtype, tm, name):
    S, K = a.shape
    N = w.shape[1]
    tm = min(tm, S)

    def body(a_ref, w_ref, o_ref):
        o_ref[...] = _nn(a_ref[...], w_ref[...]).astype(o_ref.dtype)

    return pl.pallas_call(
        body, name=name, grid=(S // tm,),
        in_specs=[pl.BlockSpec((tm, K), lambda m: (m, 0)), _resident((K, N), lambda m: (0, 0))],
        out_specs=pl.BlockSpec((tm, N), lambda m: (m, 0)), out_shape=jax.ShapeDtypeStruct((S, N), out_dtype),
        compiler_params=_params("parallel"),
    )(a, w)


def _dact_bwd(df, w_out, zg, zu, name, exchange=None):
    S, D = df.shape
    tm = min(ROW_TILE, S)
    n = w_out.shape[0] // 2

    def body(df_ref, w_ref, zg_ref, zu_ref, dzg_ref, dzu_ref):
        da = _nt(df_ref[...], w_ref[...]).astype(BF16)
        zg_v, zu_v = zg_ref[...], zu_ref[...]
        s = _sigmoid(zg_v)
        dzu_ref[...] = da * zg_v * s
        dzg_ref[...] = da * zu_v * (s * (1.0 + zg_v * (1.0 - s)))

    blk = pl.BlockSpec((tm, n), lambda j, m: (m, j))
    oshape = jax.ShapeDtypeStruct((S, 2 * n), BF16)
    return _call(
        body, name=name, grid=(2, S // tm),
        in_specs=[pl.BlockSpec((tm, D), lambda j, m: (m, 0)), pl.BlockSpec((n, D), lambda j, m: (j, 0)), blk, blk],
        out_specs=[blk, blk], out_shape=[oshape, oshape], args=(df, w_out, zg, zu), semantics=("parallel", "parallel"),
        exchange=exchange)


def _ffn_in_dgrad(dzg, dzu, w4, name, exchange=None, norm=None):
    S = dzg.shape[0]
    D, n = w4.shape[1], w4.shape[2]
    tm = min(ROW_TILE, S)
    n_norm = len(norm.inputs) if norm else 0

    def body(dzg_ref, dzu_ref, w_ref, *refs):
        acc = _nt(dzg_ref[:, 0:n], w_ref[0])
        acc += _nt(dzg_ref[:, n:2 * n], w_ref[1])
        acc += _nt(dzu_ref[:, 0:n], w_ref[2])
        acc += _nt(dzu_ref[:, n:2 * n], w_ref[3])
        if norm:
            norm.step(acc, refs[:n_norm], refs[n_norm:])
        else:
            refs[0][...] = acc

    blk = pl.BlockSpec((tm, 2 * n), lambda m: (m, 0))
    in_specs, args = [blk, blk, _resident(w4.shape, lambda m: (0, 0, 0))], [dzg, dzu, w4]
    out_specs, out_shape = [pl.BlockSpec((tm, D), lambda m: (m, 0))], [jax.ShapeDtypeStruct((S, D), F32)]
    if norm:
        norm_in, out_specs = norm.specs(tm, D)
        in_specs, args, out_shape = in_specs + norm_in, args + norm.inputs, norm.out_shape
    return _call(body, name=name, grid=(S // tm,), in_specs=in_specs, out_specs=out_specs, out_shape=out_shape, args=args,
                 semantics=("arbitrary",) if norm else ("parallel",), exchange=exchange)


def _matmul_nt(pieces, w, tm, name, exchange=None, norm=None):
    S = pieces[0].shape[0]
    ks = [p.shape[1] for p in pieces]
    N = w.shape[0]
    tm = min(tm, S)
    n_norm = len(norm.inputs) if norm else 0

    def body(*refs):
        p_refs, w_ref, refs = refs[:len(ks)], refs[len(ks)], refs[len(ks) + 1:]
        acc, off = None, 0
        for p_ref, k in zip(p_refs, ks):
            part = _nt(p_ref[...], w_ref[:, off:off + k])
            acc = part if acc is None else acc + part
            off += k
        if norm:
            norm.step(acc, refs[:n_norm], refs[n_norm:])
        else:
            refs[0][...] = acc

    in_specs = [pl.BlockSpec((tm, k), lambda m: (m, 0)) for k in ks] + [_resident(w.shape, lambda m: (0, 0))]
    args = list(pieces) + [w]
    out_specs, out_shape = [pl.BlockSpec((tm, N), lambda m: (m, 0))], [jax.ShapeDtypeStruct((S, N), F32)]
    if norm:
        norm_in, out_specs = norm.specs(tm, N)
        in_specs, args, out_shape = in_specs + norm_in, args + norm.inputs, norm.out_shape
    return _call(body, name=name, grid=(S // tm,), in_specs=in_specs, out_specs=out_specs, out_shape=out_shape, args=args,
                 semantics=("arbitrary",) if norm else ("parallel",), exchange=exchange)


def _wgrad(a, gs, tn, name, exchange=None):
    S, Ka = a.shape
    N = gs[0].shape[1]
    ts = min(ROW_TILE * (2 if Ka <= D_MODEL else 1), S)

    def body(a_ref, *refs):
        g_refs, o_ref = refs[:-1], refs[-1]

        @pl.when(pl.program_id(1) == 0)
        def _():
            o_ref[...] = jnp.zeros_like(o_ref)

        a_t = a_ref[...].T
        for i, g_ref in enumerate(g_refs):
            o_ref[i] += _nn(a_t, g_ref[...])

    return _call(
        body, name=name, grid=(N // tn, S // ts),
        in_specs=[pl.BlockSpec((ts, Ka), lambda j, s: (s, 0))] + [pl.BlockSpec((ts, tn), lambda j, s: (s, j))] * len(gs),
        out_specs=[pl.BlockSpec((len(gs), None, Ka, tn), lambda j, s: (0, j, 0, 0))],
        out_shape=[jax.ShapeDtypeStruct((len(gs), N // tn, Ka, tn), F32)], args=(a, *gs),
        semantics=("parallel", "arbitrary"), exchange=exchange)


def _wgrad_pieces(a, pieces, tn, name):
    S, Ka = a.shape
    ts = min(ROW_TILE, S)
    blocks = [(i, j) for i, p in enumerate(pieces) for j in range(p.shape[1] // tn)]

    def body(a_ref, *refs):
        g_refs, o_ref = refs[:-1], refs[-1]

        @pl.when(pl.program_id(0) == 0)
        def _():
            o_ref[...] = jnp.zeros_like(o_ref)

        a_t = a_ref[...].T
        for b, g_ref in enumerate(g_refs):
            o_ref[b] += _nn(a_t, g_ref[...])

    return pl.pallas_call(
        body, name=name, grid=(S // ts,),
        in_specs=[pl.BlockSpec((ts, Ka), lambda s: (s, 0))] + [pl.BlockSpec((ts, tn), lambda s, j=j: (s, j)) for _, j in blocks],
        out_specs=pl.BlockSpec((len(blocks), Ka, tn), lambda s: (0, 0, 0)),
        out_shape=jax.ShapeDtypeStruct((len(blocks), Ka, tn), F32), compiler_params=_params("arbitrary"),
    )(a, *[pieces[i] for i, _ in blocks])


def _wgrad_rows(lhs, g, name):
    S, Ka = lhs[0].shape
    N = g.shape[1]
    ts = min(ROW_TILE, S)

    def body(*refs):
        a_refs, g_ref, o_ref = refs[:-2], refs[-2], refs[-1]

        @pl.when(pl.program_id(0) == 0)
        def _():
            o_ref[...] = jnp.zeros_like(o_ref)

        gv = g_ref[...]
        for i, a_ref in enumerate(a_refs):
            o_ref[i] += _tn(a_ref[...], gv)

    return pl.pallas_call(
        body, name=name, grid=(S // ts,),
        in_specs=[pl.BlockSpec((ts, Ka), lambda s: (s, 0))] * len(lhs) + [pl.BlockSpec((ts, N), lambda s: (s, 0))],
        out_specs=pl.BlockSpec((len(lhs), Ka, N), lambda s: (0, 0, 0)),
        out_shape=jax.ShapeDtypeStruct((len(lhs), Ka, N), F32), compiler_params=_params("arbitrary"),
    )(*lhs, g)


def _hgrn_chunk_common(qr, fr, oml, tri, last):
    k = oml * _sigmoid(-fr)
    g = jnp.log1p(-k) * math.log2(math.e)
    q = qr * _sigmoid(qr)
    G = _nn(tri, g, precision=lax.Precision.HIGHEST)
    Gl = G[last:last + 1]
    return q, k, G, Gl


def _hgrn_consts(reverse):
    C = HG_CHUNK
    r = lax.broadcasted_iota(jnp.int32, (C, C), 0)
    cc = lax.broadcasted_iota(jnp.int32, (C, C), 1)
    tri = ((cc >= r) if reverse else (cc <= r)).astype(F32)
    tri_t = ((cc <= r) if reverse else (cc >= r)).astype(F32)
    rid = lax.broadcasted_iota(jnp.int32, (C, HG_WIDTH), 0)
    return tri, tri_t, rid, (0 if reverse else C - 1)


def _head_slices():
    return [slice(h * HG_DIM, (h + 1) * HG_DIM) for h in range(HG_HEADS)]


def _per_head_lane_sum(x):
    C = x.shape[0]
    return jnp.concatenate(
        [jnp.broadcast_to(jnp.sum(x[:, sl], axis=-1, keepdims=True), (C, HG_DIM)) for sl in _head_slices()], axis=1)


HG_TILE = 8


def _pair_tiles(s, reverse):
    blk, r = divmod(s, HG_TILE)
    n_tiles = HG_CHUNK // HG_TILE
    others = range(0, blk) if reverse else range(blk + 1, n_tiles)
    return [(blk, r)] + [(t, None) for t in others]


def _pair_decay(G, s, tile, r, rid8, reverse, keys=False):
    rs = slice(tile * HG_TILE, (tile + 1) * HG_TILE)
    d = (G[s:s + 1] - G[rs]) if keys else (G[rs] - G[s:s + 1])
    if r is not None:
        d = jnp.where((rid8 <= r) if reverse else (rid8 >= r), d, NEG)
    return rs, jnp.exp2(d)


def _hgrn_fwd(z, lb, direction, name, exchange=None):
    S = z.shape[0]
    C, DK, W = HG_CHUNK, HG_DIM, HG_WIDTH
    tb = min(HG_ROWS, S)
    n_t, n_c = S // tb, tb // C
    reverse = direction == 1
    tmap = (lambda i: n_t - 1 - i) if reverse else (lambda i: i)

    def body(q_ref, f_ref, v_ref, lb_ref, o_ref, st_out_ref, st_ref):
        @pl.when(pl.program_id(0) == 0)
        def _():
            st_ref[...] = jnp.zeros_like(st_ref)

        oml = 1.0 - lb_ref[...]
        tri, _, _, last = _hgrn_consts(reverse)
        rid8 = lax.broadcasted_iota(jnp.int32, (HG_TILE, W), 0)

        def chunk(ci, carry):
            cidx = (n_c - 1 - ci) if reverse else ci
            rows = pl.ds(pl.multiple_of(cidx * C, C), C)
            v = v_ref[rows, :]
            q, k, G, Gl = _hgrn_chunk_common(q_ref[rows, :], f_ref[rows, :], oml, tri, last)
            qd = (q * jnp.exp2(G)).astype(BF16)
            kd = (k * jnp.exp2(Gl - G)).astype(BF16)
            e_gl = jnp.exp2(Gl)
            v_b = v.astype(BF16)
            inter = []
            for h, sl in enumerate(_head_slices()):
                st0 = st_ref[h]
                st_out_ref[h, cidx] = st0
                inter.append(_nt(qd[:, sl], st0.astype(BF16)))
                st_ref[h] = st0 * e_gl[:, sl] + _tn(v_b[:, sl], kd[:, sl])
            o = jnp.concatenate(inter, axis=1)
            o_t = [o[t * HG_TILE:(t + 1) * HG_TILE] for t in range(C // HG_TILE)]
            for s in range(C):
                k_s, v_s = k[s:s + 1], v[s:s + 1]
                for tile, r in _pair_tiles(s, reverse):
                    rs, e_s = _pair_decay(G, s, tile, r, rid8, reverse)
                    o_t[tile] = o_t[tile] + _per_head_lane_sum(q[rs] * k_s * e_s) * v_s
            o_ref[rows, :] = jnp.concatenate(o_t, axis=0)
            return carry

        lax.fori_loop(0, n_c, chunk, 0, unroll=8)

    def sec(j):
        return pl.BlockSpec((tb, W), lambda i: (tmap(i), j))

    return _call(
        body, name=name, grid=(n_t,),
        in_specs=[sec(0), sec(1 + direction), sec(3), pl.BlockSpec((1, W), lambda i: (0, 0))],
        out_specs=[sec(0), pl.BlockSpec((HG_HEADS, n_c, DK, DK), lambda i: (0, tmap(i), 0, 0))],
        out_shape=[jax.ShapeDtypeStruct((S, W), F32), jax.ShapeDtypeStruct((HG_HEADS, S // C, DK, DK), F32)],
        scratch_shapes=[pltpu.VMEM((HG_HEADS, DK, DK), F32)], args=(z, z, z, lb), semantics=("arbitrary",),
        exchange=exchange)


def _hgrn_bwd(z, lb, do, states, direction, name, acc=None, exchange=None):
    S = z.shape[0]
    C, DK, W = HG_CHUNK, HG_DIM, HG_WIDTH
    tb = min(HG_ROWS, S)
    n_t, n_c = S // tb, tb // C
    reverse = direction == 1
    tmap = (lambda i: i) if reverse else (lambda i: n_t - 1 - i)

    def body(*refs):
        if acc:
            q_ref, f_ref, v_ref, lb_ref, do_ref, st_in_ref, dqa_ref, dva_ref, dq_ref, df_ref, dv_ref, doml_ref, dst_ref = refs
        else:
            q_ref, f_ref, v_ref, lb_ref, do_ref, st_in_ref, dq_ref, df_ref, dv_ref, doml_ref, dst_ref = refs

        @pl.when(pl.program_id(0) == 0)
        def _():
            dst_ref[...] = jnp.zeros_like(dst_ref)
            doml_ref[...] = jnp.zeros_like(doml_ref)

        oml = 1.0 - lb_ref[...]
        tri, tri_t, rid, last = _hgrn_consts(reverse)
        rid8 = lax.broadcasted_iota(jnp.int32, (HG_TILE, W), 0)

        def chunk(ci, carry):
            cidx = ci if reverse else (n_c - 1 - ci)
            rows = pl.ds(pl.multiple_of(cidx * C, C), C)
            qr, fr, v, dov = q_ref[rows, :], f_ref[rows, :], v_ref[rows, :], do_ref[rows, :]
            q, k, G, Gl = _hgrn_chunk_common(qr, fr, oml, tri, last)
            e_g, e_gl, e_kd = jnp.exp2(G), jnp.exp2(Gl), jnp.exp2(Gl - G)
            qd, kd = q * e_g, k * e_kd
            do_b, v_b, qd_b, kd_b = dov.astype(BF16), v.astype(BF16), qd.astype(BF16), kd.astype(BF16)
            dqd, dkd, dv, state_dot = [], [], [], []
            for h, sl in enumerate(_head_slices()):
                st0, dst1 = st_in_ref[h, cidx], dst_ref[h]
                dst1_b = dst1.astype(BF16)
                dqd.append(_nn(do_b[:, sl], st0.astype(BF16)))
                dkd.append(_nn(v_b[:, sl], dst1_b))
                dv.append(_nt(kd_b[:, sl], dst1_b))
                state_dot.append(jnp.sum(st0 * dst1, axis=0, keepdims=True))
                dst_ref[h] = dst1 * e_gl[:, sl] + _tn(do_b[:, sl], qd_b[:, sl])
            dqd, dkd, dv = [jnp.concatenate(t, axis=1) for t in (dqd, dkd, dv)]
            d_gl = e_gl * jnp.concatenate(state_dot, axis=1) + jnp.sum(dkd * kd, axis=0, keepdims=True)
            dq, dk = dqd * e_g, dkd * e_kd
            n_tiles = C // HG_TILE
            dq_t, dk_t, dv_t = [[x[t * HG_TILE:(t + 1) * HG_TILE] for t in range(n_tiles)] for x in (dq, dk, dv)]
            for s in range(C):
                k_s, v_s = k[s:s + 1], v[s:s + 1]
                for tile, r in _pair_tiles(s, reverse):
                    rs, e_s = _pair_decay(G, s, tile, r, rid8, reverse)
                    dq_t[tile] = dq_t[tile] + _per_head_lane_sum(dov[rs] * v_s) * e_s * k_s
            for t in range(C):
                q_t, do_t = q[t:t + 1], dov[t:t + 1]
                for tile, r in _pair_tiles(t, not reverse):
                    rs, x_t = _pair_decay(G, t, tile, r, rid8, not reverse, keys=True)
                    qx = q_t * x_t
                    dv_t[tile] = dv_t[tile] + _per_head_lane_sum(k[rs] * qx) * do_t
                    dk_t[tile] = dk_t[tile] + _per_head_lane_sum(v[rs] * do_t) * qx
            dq, dk, dv = [jnp.concatenate(x, axis=0) for x in (dq_t, dk_t, dv_t)]
            d_big_g = dq * q - dk * k + jnp.where(rid == last, d_gl, 0.0)
            dg = _nn(tri_t, d_big_g, precision=lax.Precision.HIGHEST)
            dk_all = dk - dg / (1.0 - k)
            sig_nf = _sigmoid(-fr)
            df_ref[rows, :] = (-dk_all * k * (1.0 - sig_nf)).astype(df_ref.dtype)
            doml_ref[...] += jnp.sum(dk_all * sig_nf, axis=0, keepdims=True)
            sq = _sigmoid(qr)
            dqr = dq * (sq * (1.0 + qr * (1.0 - sq)))
            if acc:
                dqr = dqr + dqa_ref[rows, :]
                dv = dv + dva_ref[rows, :]
            dq_ref[rows, :] = dqr.astype(dq_ref.dtype)
            dv_ref[rows, :] = dv.astype(dv_ref.dtype)
            return carry

        lax.fori_loop(0, n_c, chunk, 0, unroll=8)

    def sec(j):
        return pl.BlockSpec((tb, W), lambda i: (tmap(i), j))

    vec = pl.BlockSpec((1, W), lambda i: (0, 0))
    ins = [z, z, z, lb, do, states]
    in_specs = [sec(0), sec(1 + direction), sec(3), vec, sec(0),
                pl.BlockSpec((HG_HEADS, n_c, DK, DK), lambda i: (0, tmap(i), 0, 0))]
    if acc:
        ins += list(acc)
        in_specs += [sec(0), sec(0)]
    final = jax.ShapeDtypeStruct((S, W), BF16)
    partial = final if acc else jax.ShapeDtypeStruct((S, W), F32)
    return _call(
        body, name=name, grid=(n_t,), in_specs=in_specs,
        out_specs=[sec(0), sec(0), sec(0), vec],
        out_shape=[partial, final, partial, jax.ShapeDtypeStruct((1, W), F32)],
        scratch_shapes=[pltpu.VMEM((HG_HEADS, DK, DK), F32)], args=ins, semantics=("arbitrary",), exchange=exchange)


def _hgrn_post_fwd(o_f, o_b, z, norm_g, name):
    S = z.shape[0]
    tr = min(ROW_TILE, S)

    def body(of_ref, ob_ref, gr_ref, ng_ref, y_ref):
        o = of_ref[...] + ob_ref[...]
        gr = gr_ref[...]
        gate = gr * _sigmoid(gr)
        ng = ng_ref[...]
        for h in range(HG_HEADS):
            sl = slice(h * HG_DIM, (h + 1) * HG_DIM)
            oh = o[:, sl]
            rstd = lax.rsqrt(jnp.mean(oh * oh, axis=-1, keepdims=True) + EPS)
            y_ref[:, sl] = (oh * rstd * ng[:, sl] * gate[:, sl]).astype(y_ref.dtype)

    row = pl.BlockSpec((tr, HG_WIDTH), lambda i: (i, 0))
    return pl.pallas_call(
        body, name=name, grid=(S // tr,),
        in_specs=[row, row, pl.BlockSpec((tr, HG_WIDTH), lambda i: (i, 4)), pl.BlockSpec((1, HG_WIDTH), lambda i: (0, 0))],
        out_specs=row, out_shape=jax.ShapeDtypeStruct((S, HG_WIDTH), BF16), compiler_params=_params("parallel"),
    )(o_f, o_b, z, norm_g)


def _hgrn_post_bwd(dy, o_f, o_b, z, norm_g, name):
    S = z.shape[0]
    tr = min(ROW_TILE, S)

    def body(dy_ref, of_ref, ob_ref, gr_ref, ng_ref, do_ref, dgr_ref, dng_ref):
        @pl.when(pl.program_id(0) == 0)
        def _():
            dng_ref[...] = jnp.zeros_like(dng_ref)

        o = of_ref[...] + ob_ref[...]
        gr, ng, dyv = gr_ref[...], ng_ref[...], dy_ref[...]
        sg = _sigmoid(gr)
        for h in range(HG_HEADS):
            sl = slice(h * HG_DIM, (h + 1) * HG_DIM)
            oh, dyh, grh, sgh, ngh = o[:, sl], dyv[:, sl], gr[:, sl], sg[:, sl], ng[:, sl]
            rstd = lax.rsqrt(jnp.mean(oh * oh, axis=-1, keepdims=True) + EPS)
            on = oh * rstd
            du = dyh * (grh * sgh)
            dgr_ref[:, sl] = (dyh * (on * ngh) * (sgh * (1.0 + grh * (1.0 - sgh)))).astype(dgr_ref.dtype)
            dng_ref[:, sl] += jnp.sum(du * on, axis=0, keepdims=True)
            don = du * ngh
            do_ref[:, sl] = rstd * (don - on * jnp.mean(don * on, axis=-1, keepdims=True))

    row = pl.BlockSpec((tr, HG_WIDTH), lambda i: (i, 0))
    vec = pl.BlockSpec((1, HG_WIDTH), lambda i: (0, 0))
    full = jax.ShapeDtypeStruct((S, HG_WIDTH), F32)
    return pl.pallas_call(
        body, name=name, grid=(S // tr,),
        in_specs=[row, row, row, pl.BlockSpec((tr, HG_WIDTH), lambda i: (i, 4)), vec],
        out_specs=[row, row, vec],
        out_shape=[full, jax.ShapeDtypeStruct((S, HG_WIDTH), BF16), jax.ShapeDtypeStruct((1, HG_WIDTH), F32)],
        compiler_params=_params("arbitrary"),
    )(dy, o_f, o_b, z, norm_g)


def _t5_bucket_table():
    rel = (np.arange(3 * BLOCK)[None, :] - BLOCK) - np.arange(BLOCK)[:, None]
    nb = NUM_BUCKETS // 2
    max_exact = nb // 2
    ret = (rel > 0).astype(np.int32) * nb
    n = np.abs(rel)
    ratio = np.log(np.maximum(n, 1).astype(np.float32) / np.float32(max_exact)) / np.float32(math.log(MAX_DISTANCE / max_exact))
    large = max_exact + (ratio.astype(np.float32) * np.float32(nb - max_exact)).astype(np.int32)
    large = np.minimum(large, nb - 1)
    bucket = ret + np.where(n < max_exact, n, large)
    return bucket.astype(np.int32), (n <= WINDOW)


def _bias_table(rel_bias, name):
    bucket, in_band = _t5_bucket_table()
    idx = jnp.asarray(np.where(in_band, bucket, -1))

    def body(rb_ref, idx_ref, o_ref):
        h = pl.program_id(0)
        iv = idx_ref[...]
        acc = jnp.where(iv < 0, NEG, 0.0).astype(F32)
        for b in range(NUM_BUCKETS):
            acc = acc + jnp.where(iv == b, rb_ref[b, h], 0.0)
        o_ref[...] = acc

    return pl.pallas_call(
        body, name=name, grid=(ATT_Q_HEADS,),
        in_specs=[pl.BlockSpec(memory_space=pltpu.SMEM), pl.BlockSpec((BLOCK, 3 * BLOCK), lambda h: (0, 0))],
        out_specs=pl.BlockSpec((None, BLOCK, 3 * BLOCK), lambda h: (h, 0, 0)),
        out_shape=jax.ShapeDtypeStruct((ATT_Q_HEADS, BLOCK, 3 * BLOCK), F32), compiler_params=_params("parallel"),
    )(rel_bias, idx)


def _bias_grad(ds_sum, name):
    bucket, in_band = _t5_bucket_table()
    idx = jnp.asarray(np.where(in_band, bucket, -1))

    def body(ds_ref, idx_ref, o_ref):
        iv, ds = idx_ref[...], ds_ref[...]
        for b in range(NUM_BUCKETS):
            part = jnp.sum(jnp.where(iv == b, ds, 0.0), axis=0, keepdims=True)
            o_ref[b:b + 1, :] = part[:, 0:BLOCK] + part[:, BLOCK:2 * BLOCK] + part[:, 2 * BLOCK:3 * BLOCK]

    return pl.pallas_call(
        body, name=name, grid=(ATT_Q_HEADS,),
        in_specs=[pl.BlockSpec((None, BLOCK, 3 * BLOCK), lambda h: (h, 0, 0)), pl.BlockSpec((BLOCK, 3 * BLOCK), lambda h: (0, 0))],
        out_specs=pl.BlockSpec((None, NUM_BUCKETS, BLOCK), lambda h: (h, 0, 0)),
        out_shape=jax.ShapeDtypeStruct((ATT_Q_HEADS, NUM_BUCKETS, BLOCK), F32), compiler_params=_params("parallel"),
    )(ds_sum, idx)


Q_COL = 5 * HG_WIDTH
KV_COL = Q_COL + ATT_WIDTH
GROUP_WIDTH = ATT_GROUP * ATT_HEAD_DIM


def _stack_heads(blk):
    dh = ATT_HEAD_DIM
    return jnp.concatenate([blk[:, g * dh:(g + 1) * dh] for g in range(ATT_GROUP)], axis=0)


def _unstack_heads(st):
    return jnp.concatenate([st[g * BLOCK:(g + 1) * BLOCK] for g in range(ATT_GROUP)], axis=1)


def _rms_rows(x):
    rstd = lax.rsqrt(jnp.mean(x * x, axis=-1, keepdims=True) + EPS)
    return x * rstd, rstd


def _edge_ok(n, nb):
    colid = lax.broadcasted_iota(jnp.int32, (ATT_GROUP * BLOCK, 3 * BLOCK), 1)
    return jnp.logical_and(jnp.logical_or(colid >= BLOCK, n > 0), jnp.logical_or(colid < 2 * BLOCK, n < nb - 1))


def _sink_column(sink_ref, j=0):
    heads = range(j * ATT_GROUP, (j + 1) * ATT_GROUP)
    return jnp.concatenate([jnp.broadcast_to(sink_ref[h][:, 0:1], (BLOCK, 1)) for h in heads], axis=0)


def _attn_fwd(z, q_g, k_g, sink, bias, name):
    S = z.shape[0]
    nb = S // BLOCK
    G, dh, KV = ATT_GROUP, ATT_HEAD_DIM, ATT_KV_HEADS
    scale = 1.0 / math.sqrt(dh)

    def body(q_ref, kv0, kv1, kv2, qg_ref, kg_ref, sink_ref, bias_ref, o_ref):
        n = pl.program_id(0)
        edge_ok = _edge_ok(n, nb)
        cat = jnp.concatenate([kv0[...], kv1[...], kv2[...]], axis=0)
        qblk = q_ref[...]
        kn = [(_rms_rows(cat[:, j * dh:(j + 1) * dh])[0] * kg_ref[...]).astype(BF16) for j in range(KV)]
        vb = [cat[:, (KV + j) * dh:(KV + j + 1) * dh].astype(BF16) for j in range(KV)]
        qn = [(_rms_rows(_stack_heads(qblk[:, j * GROUP_WIDTH:(j + 1) * GROUP_WIDTH]))[0] * (qg_ref[...] * scale)).astype(BF16)
              for j in range(KV)]
        s = [_nt(qn[j], kn[j]) + bias_ref[j * G:(j + 1) * G].reshape(G * BLOCK, 3 * BLOCK) for j in range(KV)]
        s = [jnp.where(edge_ok, sj, NEG) for sj in s]
        sinks = [_sink_column(sink_ref, j) for j in range(KV)]
        m = [jnp.maximum(jnp.max(s[j], axis=-1, keepdims=True), sinks[j]) for j in range(KV)]
        e = [jnp.exp(s[j] - m[j]) for j in range(KV)]
        den = [jnp.sum(e[j], axis=-1, keepdims=True) + jnp.exp(sinks[j] - m[j]) for j in range(KV)]
        o = [_nn(e[j].astype(BF16), vb[j]) * (1.0 / den[j]) for j in range(KV)]
        o_ref[...] = jnp.concatenate([_unstack_heads(oj) for oj in o], axis=1).astype(o_ref.dtype)

    def kv(shift):
        return pl.BlockSpec((BLOCK, 2 * KV_WIDTH), lambda n: (jnp.clip(n + shift, 0, nb - 1), KV_COL // (2 * KV_WIDTH)))

    gain = pl.BlockSpec((1, dh), lambda n: (0, 0))
    return pl.pallas_call(
        body, name=name, grid=(nb,),
        in_specs=[pl.BlockSpec((BLOCK, ATT_WIDTH), lambda n: (n, Q_COL // ATT_WIDTH)), kv(-1), kv(0), kv(1), gain, gain,
                  pl.BlockSpec((ATT_Q_HEADS, 1, BLOCK), lambda n: (0, 0, 0)),
                  pl.BlockSpec((ATT_Q_HEADS, BLOCK, 3 * BLOCK), lambda n: (0, 0, 0))],
        out_specs=pl.BlockSpec((BLOCK, ATT_WIDTH), lambda n: (n, 0)),
        out_shape=jax.ShapeDtypeStruct((S, ATT_WIDTH), BF16), compiler_params=_params("parallel"),
    )(z, z, z, z, q_g, k_g, sink, bias)


def _attn_bwd(z, q_g, k_g, sink, bias, do, name):
    S = z.shape[0]
    nb = S // BLOCK
    G, dh, KV = ATT_GROUP, ATT_HEAD_DIM, ATT_KV_HEADS
    scale = 1.0 / math.sqrt(dh)
    both = range(KV)

    def body(q_ref, kv0, kv1, kv2, qg_ref, kg_ref, sink_ref, bias_ref, do_ref,
             dq_ref, dkw_ref, dvw_ref, ds_ref, dsink_ref, dqg_ref):
        n = pl.program_id(0)

        @pl.when(n == 0)
        def _():
            ds_ref[...] = jnp.zeros_like(ds_ref)
            dsink_ref[...] = jnp.zeros_like(dsink_ref)
            dqg_ref[...] = jnp.zeros_like(dqg_ref)

        edge_ok = _edge_ok(n, nb)
        qg = qg_ref[...]
        cat = jnp.concatenate([kv0[...], kv1[...], kv2[...]], axis=0)
        qblk, doblk = q_ref[...], do_ref[...]
        kn = [(_rms_rows(cat[:, j * dh:(j + 1) * dh])[0] * kg_ref[...]).astype(BF16) for j in both]
        vb = [cat[:, (KV + j) * dh:(KV + j + 1) * dh].astype(BF16) for j in both]
        norm = [_rms_rows(_stack_heads(qblk[:, j * GROUP_WIDTH:(j + 1) * GROUP_WIDTH])) for j in both]
        qn = [(norm[j][0] * (qg * scale)).astype(BF16) for j in both]
        do_b = [_stack_heads(doblk[:, j * GROUP_WIDTH:(j + 1) * GROUP_WIDTH]).astype(BF16) for j in both]
        s = [_nt(qn[j], kn[j]) + bias_ref[j * G:(j + 1) * G].reshape(G * BLOCK, 3 * BLOCK) for j in both]
        dp = [_nt(do_b[j], vb[j]) for j in both]
        s = [jnp.where(edge_ok, sj, NEG) for sj in s]
        sinks = [_sink_column(sink_ref, j) for j in both]
        m = [jnp.maximum(jnp.max(s[j], axis=-1, keepdims=True), sinks[j]) for j in both]
        e = [jnp.exp(s[j] - m[j]) for j in both]
        e_sink = [jnp.exp(sinks[j] - m[j]) for j in both]
        inv = [1.0 / (jnp.sum(e[j], axis=-1, keepdims=True) + e_sink[j]) for j in both]
        p = [e[j] * inv[j] for j in both]
        delta = [jnp.sum(p[j] * dp[j], axis=-1, keepdims=True) for j in both]
        ds = [p[j] * (dp[j] - delta[j]) for j in both]
        ds_b = [dsj.astype(BF16) for dsj in ds]
        dqn = [_nn(ds_b[j], kn[j]) * scale for j in both]
        for j in both:
            dvw_ref[j] = _tn(p[j].astype(BF16), do_b[j])
            dkw_ref[j] = _tn(ds_b[j], qn[j])
        for j in both:
            ds_ref[j * G:(j + 1) * G] += ds[j].reshape(G, BLOCK, 3 * BLOCK)
            sink_term = e_sink[j] * inv[j] * delta[j]
            for g in range(G):
                dsink_ref[j * G + g] += (jnp.zeros((1, BLOCK), F32)
                                         - jnp.sum(sink_term[g * BLOCK:(g + 1) * BLOCK], axis=0, keepdims=True))
        dq = []
        for j in both:
            qhat, rstd = norm[j]
            dqg_ref[j] += jnp.sum(dqn[j] * qhat, axis=0, keepdims=True)
            dqh = dqn[j] * qg
            dq.append(_unstack_heads(rstd * (dqh - qhat * jnp.mean(dqh * qhat, axis=-1, keepdims=True))))
        dq_ref[...] = jnp.concatenate(dq, axis=1).astype(dq_ref.dtype)

    def kv(shift):
        return pl.BlockSpec((BLOCK, 2 * KV_WIDTH), lambda n: (jnp.clip(n + shift, 0, nb - 1), KV_COL // (2 * KV_WIDTH)))

    gain = pl.BlockSpec((1, dh), lambda n: (0, 0))
    sink_spec = pl.BlockSpec((ATT_Q_HEADS, 1, BLOCK), lambda n: (0, 0, 0))
    bias_spec = pl.BlockSpec((ATT_Q_HEADS, BLOCK, 3 * BLOCK), lambda n: (0, 0, 0))
    win = pl.BlockSpec((KV, None, 3 * BLOCK, dh), lambda n: (0, n, 0, 0))
    wshape = jax.ShapeDtypeStruct((KV, nb, 3 * BLOCK, dh), F32)
    return pl.pallas_call(
        body, name=name, grid=(nb,),
        in_specs=[pl.BlockSpec((BLOCK, ATT_WIDTH), lambda n: (n, Q_COL // ATT_WIDTH)), kv(-1), kv(0), kv(1), gain, gain,
                  sink_spec, bias_spec, pl.BlockSpec((BLOCK, ATT_WIDTH), lambda n: (n, HG_WIDTH // ATT_WIDTH))],
        out_specs=[pl.BlockSpec((BLOCK, ATT_WIDTH), lambda n: (n, 0)), win, win, bias_spec, sink_spec,
                   pl.BlockSpec((KV, 1, dh), lambda n: (0, 0, 0))],
        out_shape=[jax.ShapeDtypeStruct((S, ATT_WIDTH), BF16), wshape, wshape,
                   jax.ShapeDtypeStruct((ATT_Q_HEADS, BLOCK, 3 * BLOCK), F32),
                   jax.ShapeDtypeStruct((ATT_Q_HEADS, 1, BLOCK), F32),
                   jax.ShapeDtypeStruct((KV, 1, dh), F32)],
        compiler_params=_params("arbitrary"),
    )(z, z, z, z, q_g, k_g, sink, bias, do)


def _attn_kv_reduce(dkw, dvw, z, k_g, name):
    S = z.shape[0]
    nb = S // BLOCK
    dh = ATT_HEAD_DIM
    kb = min(8, nb)
    steps = nb // kb

    def body(a_lo, a, a_hi, b_lo, b, b_hi, kv_ref, kg_ref, dkv_ref, dkg_ref):
        n = pl.program_id(0)

        @pl.when(n == 0)
        def _():
            dkg_ref[...] = jnp.zeros_like(dkg_ref)

        lo = jnp.where(n > 0, 1.0, 0.0)
        hi = jnp.where(n < steps - 1, 1.0, 0.0)

        def overlap_add(w, w_lo, w_hi, j, i):
            before = lo * w_lo[j] if i == 0 else w[j, i - 1, 2 * BLOCK:3 * BLOCK, :]
            after = hi * w_hi[j] if i == kb - 1 else w[j, i + 1, 0:BLOCK, :]
            return w[j, i, BLOCK:2 * BLOCK, :] + before + after

        dkg = [jnp.zeros((1, dh), F32) for _ in range(ATT_KV_HEADS)]
        for i in range(kb):
            rows = slice(i * BLOCK, (i + 1) * BLOCK)
            dks, dvs = [], []
            for j in range(ATT_KV_HEADS):
                dkn = overlap_add(a, a_lo, a_hi, j, i)
                dvs.append(overlap_add(b, b_lo, b_hi, j, i))
                khat, rstd = _rms_rows(kv_ref[rows, j * dh:(j + 1) * dh])
                dkg[j] = dkg[j] + jnp.sum(dkn * khat, axis=0, keepdims=True)
                dkh = dkn * kg_ref[...]
                dks.append(rstd * (dkh - khat * jnp.mean(dkh * khat, axis=-1, keepdims=True)))
            dkv_ref[rows, :] = jnp.concatenate(dks + dvs, axis=1).astype(dkv_ref.dtype)
        for j in range(ATT_KV_HEADS):
            dkg_ref[j] += dkg[j]

    main = pl.BlockSpec((ATT_KV_HEADS, kb, 3 * BLOCK, dh), lambda n: (0, n, 0, 0))
    halo_lo = pl.BlockSpec((ATT_KV_HEADS, None, BLOCK, dh), lambda n: (0, jnp.maximum(n * kb - 1, 0), 2, 0))
    halo_hi = pl.BlockSpec((ATT_KV_HEADS, None, BLOCK, dh), lambda n: (0, jnp.minimum(n * kb + kb, nb - 1), 0, 0))
    return pl.pallas_call(
        body, name=name, grid=(steps,),
        in_specs=[halo_lo, main, halo_hi, halo_lo, main, halo_hi,
                  pl.BlockSpec((kb * BLOCK, 2 * KV_WIDTH), lambda n: (n, KV_COL // (2 * KV_WIDTH))),
                  pl.BlockSpec((1, dh), lambda n: (0, 0))],
        out_specs=[pl.BlockSpec((kb * BLOCK, 2 * KV_WIDTH), lambda n: (n, 0)),
                   pl.BlockSpec((ATT_KV_HEADS, 1, dh), lambda n: (0, 0, 0))],
        out_shape=[jax.ShapeDtypeStruct((S, 2 * KV_WIDTH), BF16), jax.ShapeDtypeStruct((ATT_KV_HEADS, 1, dh), F32)],
        compiler_params=_params("arbitrary"),
    )(dkw, dkw, dkw, dvw, dvw, dvw, z, k_g)


def _ada_fwd(c_act, w, b, name):
    n = w.shape[1]

    def body(c_ref, w_ref, b_ref, o_ref):
        o_ref[...] = _nn(c_ref[...], w_ref[...], precision=lax.Precision.HIGHEST) + b_ref[...]

    tn = n // 3
    return pl.pallas_call(
        body, name=name, grid=(3,),
        in_specs=[pl.BlockSpec(c_act.shape, lambda j: (0, 0)), pl.BlockSpec((w.shape[0], tn), lambda j: (0, j)),
                  pl.BlockSpec((1, tn), lambda j: (0, j))],
        out_specs=pl.BlockSpec((c_act.shape[0], tn), lambda j: (0, j)),
        out_shape=jax.ShapeDtypeStruct((c_act.shape[0], n), F32), compiler_params=_params("parallel"),
    )(c_act, w, b)


def _ada_wgrad(c_act_t, dm, name):
    D, nbatch = c_act_t.shape
    n = dm.shape[1]
    tr = 256

    def body(c_ref, dm_ref, o_ref):
        cv, dv = c_ref[...], dm_ref[...]
        acc = cv[:, 0:1] * dv[0:1, :]
        for b in range(1, nbatch):
            acc = acc + cv[:, b:b + 1] * dv[b:b + 1, :]
        o_ref[...] = acc

    return pl.pallas_call(
        body, name=name, grid=(D // tr,),
        in_specs=[pl.BlockSpec((tr, nbatch), lambda i: (i, 0)), pl.BlockSpec((nbatch, n), lambda i: (0, 0))],
        out_specs=pl.BlockSpec((tr, n), lambda i: (i, 0)), out_shape=jax.ShapeDtypeStruct((D, n), F32),
        compiler_params=_params("parallel"),
    )(c_act_t, dm)


def _to_bf16(w, name):
    R, Cn = w.shape
    tr = _row_tile(R)

    def body(w_ref, o_ref):
        o_ref[...] = w_ref[...].astype(BF16)

    blk = pl.BlockSpec((tr, Cn), lambda i: (i, 0))
    return pl.pallas_call(
        body, name=name, grid=(R // tr,), in_specs=[blk], out_specs=blk, out_shape=jax.ShapeDtypeStruct((R, Cn), BF16),
        compiler_params=_params("parallel"),
    )(w)


def _adamw(w, g, m, v, name):
    R, Cn = w.shape
    tr = R
    for cand in (256, 128, 64, 32, 16, 8):
        if R % cand == 0:
            tr = cand
            break

    def body(w_ref, g_ref, m_ref, v_ref, d_ref, nm_ref, nv_ref):
        gv = g_ref[...]
        m_new = ADAM_B1 * m_ref[...] + (1.0 - ADAM_B1) * gv
        v_new = ADAM_B2 * v_ref[...] + (1.0 - ADAM_B2) * (gv * gv)
        m_hat = m_new / (1.0 - ADAM_B1 ** ADAM_STEP)
        v_hat = v_new / (1.0 - ADAM_B2 ** ADAM_STEP)
        d_ref[...] = -ADAM_LR * (m_hat / (jnp.sqrt(v_hat) + ADAM_EPS) + ADAM_WD * w_ref[...])
        nm_ref[...] = m_new
        nv_ref[...] = v_new

    blk = pl.BlockSpec((tr, Cn), lambda i: (i, 0))
    shp = jax.ShapeDtypeStruct((R, Cn), F32)
    return pl.pallas_call(
        body, name=name, grid=(R // tr,), in_specs=[blk] * 4, out_specs=[blk] * 3, out_shape=[shp] * 3,
        compiler_params=_params("parallel"),
    )(w, g, m, v)


def _place():
    return lax.axis_index("x"), lax.axis_index("y"), lax.axis_index("c")


def _flip(place, k):
    x, y, c = place
    return (1 - x if k & 4 else x, 1 - y if k & 2 else y, 1 - c if k & 1 else c)


def _dev_index(place):
    x, y, c = place
    return 4 * x + 2 * y + c


def _chip_index(place):
    return 2 * place[0] + place[1]


def _allgather8(x, name, reduce=False):
    R, Cn = x.shape

    def body(x_ref, *rest):
        if reduce:
            out_ref, sum_ref, send_sems, recv_sems, local_sem = rest
        else:
            out_ref, send_sems, recv_sems, local_sem = rest
        me = _place()
        mine = pltpu.make_async_copy(x_ref, out_ref.at[_dev_index(me)], local_sem)
        mine.start()

        def copy(k, origin, to):
            return pltpu.make_async_remote_copy(
                src_ref=x_ref, dst_ref=out_ref.at[_dev_index(origin)], send_sem=send_sems.at[k - 1],
                recv_sem=recv_sems.at[k - 1], device_id=to, device_id_type=MESH)

        sends = [copy(k, me, _flip(me, k)) for k in range(1, 8)]
        for cp in sends:
            cp.start()
        for k in range(1, 8):
            copy(k, _flip(me, k), me).wait_recv()
        for cp in sends:
            cp.wait_send()
        mine.wait()
        if reduce:
            acc = out_ref[0]
            for i in range(1, 8):
                acc = acc + out_ref[i]
            sum_ref[...] = acc

    vm = pl.BlockSpec(memory_space=pltpu.VMEM)
    outs = [jax.ShapeDtypeStruct((8, R, Cn), F32)] + ([jax.ShapeDtypeStruct((R, Cn), F32)] if reduce else [])
    res = pl.pallas_call(
        body, name=name, in_specs=[vm], out_specs=[vm] * len(outs), out_shape=outs,
        scratch_shapes=[pltpu.SemaphoreType.DMA((7,)), pltpu.SemaphoreType.DMA((7,)), pltpu.SemaphoreType.DMA],
    )(x)
    return res if reduce else res[0]


def _weights_allgather(shards, name):
    n = len(shards)
    per = 8

    def body(*refs):
        in_refs, out_refs = refs[:n], refs[n:2 * n]
        send_sems, recv_sems = refs[2 * n:]
        me = _place()
        c = me[2]
        sibling = _flip(me, 1)
        others = [_flip(me, 2 * j) for j in (1, 2, 3)]

        def copy(a, k, src, dst, to):
            return pltpu.make_async_remote_copy(
                src_ref=src, dst_ref=dst, send_sem=send_sems.at[per * a + k], recv_sem=recv_sems.at[per * a + k],
                device_id=to, device_id_type=MESH)

        def block(a, place, half):
            return out_refs[a].at[_chip_index(place), half]

        started = []
        for a in range(n):
            sends = [copy(a, 0, in_refs[a].at[c], block(a, me, c), sibling),
                     copy(a, 7, in_refs[a].at[1 - c], block(a, me, 1 - c), sibling)]
            sends += [copy(a, 1 + j, in_refs[a].at[c], block(a, me, c), to) for j, to in enumerate(others)]
            for cp in sends:
                cp.start()
            started += sends
        for a in range(n):
            for j, other in enumerate(others):
                landed = block(a, other, c)
                copy(a, 1 + j, landed, landed, me).wait_recv()
                fwd = copy(a, 4 + j, landed, landed, sibling)
                fwd.start()
                started.append(fwd)
        for a in range(n):
            copy(a, 0, block(a, me, 1 - c), block(a, me, 1 - c), me).wait_recv()
            copy(a, 7, block(a, me, c), block(a, me, c), me).wait_recv()
            for j, other in enumerate(others):
                got = block(a, other, 1 - c)
                copy(a, 4 + j, got, got, me).wait_recv()
        for cp in started:
            cp.wait_send()

    return pl.pallas_call(
        body, name=name, in_specs=[ANY] * n, out_specs=[ANY] * n,
        out_shape=[jax.ShapeDtypeStruct((N_CHIPS,) + s.shape, s.dtype) for s in shards],
        scratch_shapes=[pltpu.SemaphoreType.DMA((per * n,)), pltpu.SemaphoreType.DMA((per * n,))],
    )(*shards)


def _remote(src, dst, send_sems, recv_sems, i, to):
    return pltpu.make_async_remote_copy(
        src_ref=src, dst_ref=dst, send_sem=send_sems.at[i], recv_sem=recv_sems.at[i], device_id=to, device_id_type=MESH)


def _symmetric_plan(copies):
    def plan(in_refs, out_refs, send_sems, recv_sems):
        sends = [_remote(src, dst, send_sems, recv_sems, i, to) for i, (src, dst, to) in enumerate(copies(in_refs, out_refs))]
        return sends, sends
    return plan


def _halves_exchange(grads):
    def copies(in_refs, out_refs):
        me = _place()
        return [(g.at[kk, 1 - me[2]], got.at[kk], _flip(me, 1)) for g, got in zip(in_refs, out_refs) for kk in range(N_CHIPS)]

    return _Exchange(grads, [jax.ShapeDtypeStruct((N_CHIPS,) + g.shape[2:], g.dtype) for g in grads],
                     N_CHIPS * len(grads), _symmetric_plan(copies))


def _chips_exchange(parts):
    def copies(in_refs, out_refs):
        me = _place()
        return [(p.at[_chip_index(_flip(me, 2 * j))], got.at[j - 1], _flip(me, 2 * j))
                for p, got in zip(in_refs, out_refs) for j in (1, 2, 3)]

    return _Exchange(parts, [jax.ShapeDtypeStruct((3,) + p.shape[1:], p.dtype) for p in parts], 3 * len(parts),
                     _symmetric_plan(copies))


def _siblings_exchange(halves):
    def copies(in_refs, out_refs):
        sibling = _flip(_place(), 1)
        return [(h, got, sibling) for h, got in zip(in_refs, out_refs)]

    return _Exchange(halves, [jax.ShapeDtypeStruct(h.shape, h.dtype) for h in halves], len(halves), _symmetric_plan(copies))


def _ici_gather_plan(n, base=0):
    def plan(in_refs, out_refs, send_sems, recv_sems):
        me = _place()
        c = me[2]
        sends, recvs = [], []
        for a, (w, out) in enumerate(zip(in_refs[:n], out_refs)):
            for j in (1, 2, 3):
                i = base + 3 * a + j - 1
                sends.append(_remote(w.at[c], out.at[_chip_index(me), c], send_sems, recv_sems, i, _flip(me, 2 * j)))
                z = out.at[_chip_index(_flip(me, 2 * j)), c]
                recvs.append(_remote(z, z, send_sems, recv_sems, i, me))
        return sends, recvs
    return plan


def _d2d_gather_plan(n, base=0):
    def plan(in_refs, out_refs, send_sems, recv_sems):
        me = _place()
        c = me[2]
        sibling = _flip(me, 1)
        mine = _chip_index(me)
        sends, recvs = [], []
        for a, (w, out) in enumerate(zip(in_refs[:n], out_refs)):
            moves = [(w.at[c], (mine, c)), (w.at[1 - c], (mine, 1 - c))]
            moves += [(out.at[_chip_index(_flip(me, 2 * j)), c], (_chip_index(_flip(me, 2 * j)), c)) for j in (1, 2, 3)]
            for k, (src, (chip, half)) in enumerate(moves):
                sends.append(_remote(src, out.at[chip, half], send_sems, recv_sems, base + 5 * a + k, sibling))
            lands = [(mine, 1 - c), (mine, c)] + [(_chip_index(_flip(me, 2 * j)), 1 - c) for j in (1, 2, 3)]
            for k, (chip, half) in enumerate(lands):
                z = out.at[chip, half]
                recvs.append(_remote(z, z, send_sems, recv_sems, base + 5 * a + k, me))
        return sends, recvs
    return plan


def _gathered_shapes(shards):
    return [jax.ShapeDtypeStruct((N_CHIPS,) + s.shape, s.dtype) for s in shards]


def _gather_over_ici(shards):
    return _Exchange(shards, _gathered_shapes(shards), 3 * len(shards), _ici_gather_plan(len(shards)))


def _gather_over_d2d(shards, gathered):
    n = len(shards)
    return _Exchange(list(shards) + list(gathered), [jax.ShapeDtypeStruct(g.shape, g.dtype) for g in gathered], 5 * n,
                     _d2d_gather_plan(n), aliases={n + a: a for a in range(n)})


def _gather_in_one(shards):
    n = len(shards)
    return _Exchange(shards, _gathered_shapes(shards), 8 * n, _ici_gather_plan(n), then=_d2d_gather_plan(n, base=3 * n))


def _row_tile(rows):
    for cand in (256, 176, 128, 64, 32, 16, 8):
        if rows % cand == 0:
            return cand
    return rows


def _pair_sum(core, grad, theirs, name):
    N, _, R, Cn = grad.shape
    tr = R

    def body(core_ref, g_ref, t_ref, o_ref, ob_ref):
        s = g_ref[...] + t_ref[...]
        o_ref[...] = s
        ob_ref[...] = s.astype(BF16)

    out = pl.BlockSpec((None, tr, Cn), lambda k, i, core_ref: (k, i, 0))
    return pl.pallas_call(
        body, name=name,
        grid_spec=pltpu.PrefetchScalarGridSpec(
            num_scalar_prefetch=1, grid=(N, R // tr),
            in_specs=[pl.BlockSpec((None, None, tr, Cn), lambda k, i, core_ref: (k, core_ref[0], i, 0)),
                      pl.BlockSpec((None, tr, Cn), lambda k, i, core_ref: (k, i, 0))],
            out_specs=[out, out]),
        out_shape=[jax.ShapeDtypeStruct((N, R, Cn), F32), jax.ShapeDtypeStruct((N, R, Cn), BF16)],
        compiler_params=_params("parallel", "parallel"),
    )(core, grad, theirs)


def _chip_sum(chip, parts, landed, name):
    _, R, Cn = parts.shape
    tr = R

    def body(chip_ref, p_ref, l_ref, o_ref):
        o_ref[...] = ((p_ref[...] + l_ref[0].astype(F32)) + l_ref[1].astype(F32)) + l_ref[2].astype(F32)

    return pl.pallas_call(
        body, name=name,
        grid_spec=pltpu.PrefetchScalarGridSpec(
            num_scalar_prefetch=1, grid=(R // tr,),
            in_specs=[pl.BlockSpec((None, tr, Cn), lambda i, chip_ref: (chip_ref[0], i, 0)),
                      pl.BlockSpec((3, tr, Cn), lambda i, chip_ref: (0, i, 0))],
            out_specs=pl.BlockSpec((tr, Cn), lambda i, chip_ref: (i, 0))),
        out_shape=jax.ShapeDtypeStruct((R, Cn), F32), compiler_params=_params("parallel"),
    )(chip, parts, landed)


def _pair_sums(core, grads, theirs, tag):
    return [_pair_sum(core, g, t, f"{tag}_pair_sum_{i}") for i, (g, t) in enumerate(zip(grads, theirs))]


def _chip_sums(chip, parts, landed, tag):
    return [_chip_sum(chip, p[0], l, f"{tag}_chip_sum_{i}") for i, (p, l) in enumerate(zip(parts, landed))]


def _by_chip_rows(g):
    return g.reshape(N_CHIPS, 2, g.shape[0] // (2 * N_CHIPS), g.shape[1])


def _by_chip_cols(g):
    return g.reshape(N_CHIPS, 2, g.shape[1] // 2, g.shape[2])


def _adamw_halves(core, w, g_mine, g_theirs, m, v, name):
    R2, Cn = w.shape
    r = R2 // 2
    tr = _row_tile(r)
    nt = r // tr

    def body(core_ref, w_ref, gm_ref, gt_ref, m_ref, v_ref, g_ref, d_ref, nm_ref, nv_ref):
        gv = jnp.where(pl.program_id(0) == core_ref[0], gm_ref[...], gt_ref[...])
        g_ref[...] = gv
        m_new = ADAM_B1 * m_ref[...] + (1.0 - ADAM_B1) * gv
        v_new = ADAM_B2 * v_ref[...] + (1.0 - ADAM_B2) * (gv * gv)
        m_hat = m_new / (1.0 - ADAM_B1 ** ADAM_STEP)
        v_hat = v_new / (1.0 - ADAM_B2 ** ADAM_STEP)
        d_ref[...] = -ADAM_LR * (m_hat / (jnp.sqrt(v_hat) + ADAM_EPS) + ADAM_WD * w_ref[...])
        nm_ref[...] = m_new
        nv_ref[...] = v_new

    full = pl.BlockSpec((tr, Cn), lambda hf, i, core_ref: (hf * nt + i, 0))
    half = pl.BlockSpec((tr, Cn), lambda hf, i, core_ref: (i, 0))
    shp = jax.ShapeDtypeStruct((R2, Cn), F32)
    return pl.pallas_call(
        body, name=name,
        grid_spec=pltpu.PrefetchScalarGridSpec(
            num_scalar_prefetch=1, grid=(2, nt), in_specs=[full, half, half, full, full], out_specs=[full] * 4),
        out_shape=[shp] * 4, compiler_params=_params("parallel", "parallel"),
    )(core, w, g_mine, g_theirs, m, v)


def _pad_row(v, width):
    v = v.reshape(1, -1)
    return jnp.pad(v, ((0, 0), (0, width - v.shape[1])))


def _ffn1_forward(x, ng, shift, scale, gate, w_in4, w_out_shard, gather, next_norm):
    h = _rmsmod_fwd(x, ng, shift, scale, "ffn1_norm")
    (zg, zu, a), (partly, (w_out4,)) = _ffn_in_fwd(
        h, w_in4, "ffn1_in", exchange=[_gather_over_ici(gather), _gather_in_one([w_out_shard])])
    w_out = w_out4.reshape(D_FF, D_MODEL)
    (x_new, f, h_next), gathered = _proj_out_fwd([a], w_out, x, gate, 0.5, "ffn1_out", next_norm=next_norm,
                                                 exchange=_gather_over_d2d(gather, partly))
    return x_new, (h, zg, zu, a, f), w_out, gathered, h_next


def _ffn_backward(df, saved, w_in4, w_out, core, chip, tag, riding=None, norm=None):
    h, zg, zu, a = saved[:4]
    rode = None
    if riding:
        (dzg, dzu), rode = _dact_bwd(df, w_out, zg, zu, f"{tag}_dact", exchange=riding)
    else:
        dzg, dzu = _dact_bwd(df, w_out, zg, zu, f"{tag}_dact")
    g_out = [_by_chip_rows(_wgrad(a, [df], df.shape[1], f"{tag}_dw_out")[0].reshape(a.shape[1], df.shape[1]))]
    (dw_in,), theirs_out = _wgrad(h, [dzg, dzu], FF_SHARD, f"{tag}_dw_in", exchange=_halves_exchange(g_out))
    g_in = [_by_chip_cols(dw_in.reshape(N_CHIPS, h.shape[1], FF_SHARD))]
    parts_out = _pair_sums(core, g_out, theirs_out, f"{tag}_out")
    dh_outs, (theirs_in, landed_out) = _ffn_in_dgrad(
        dzg, dzu, w_in4, f"{tag}_dh", norm=norm, exchange=[_halves_exchange(g_in), _chips_exchange([parts_out[0][1]])])
    parts_in = _pair_sums(core, g_in, theirs_in, f"{tag}_in")
    return dh_outs, parts_in, _chip_sums(chip, parts_out, landed_out, f"{tag}_out"), rode


def kernel(x, c, w_ada, b_ada, norm_g, w_ffn1_in, w_ffn1_out, w_ffn2_in, w_ffn2_out, w_mix_in, w_mix_out, hgrn_lb, hgrn_norm_g, qk_norm_g, attn_sink, rel_bias, loss_target, m_w_ada, m_b_ada, m_norm_g, m_w_ffn1_in, m_w_ffn1_out, m_w_ffn2_in, m_w_ffn2_out, m_w_mix_in, m_w_mix_out, m_hgrn_lb, m_hgrn_norm_g, m_qk_norm_g, m_attn_sink, m_rel_bias, v_w_ada, v_b_ada, v_norm_g, v_w_ffn1_in, v_w_ffn1_out, v_w_ffn2_in, v_w_ffn2_out, v_w_mix_in, v_w_mix_out, v_hgrn_lb, v_hgrn_norm_g, v_qk_norm_g, v_attn_sink, v_rel_bias):
    D = D_MODEL
    S = x.shape[1]
    place = (lax.axis_index("x"), lax.axis_index("y"), lax.axis_index("c"))
    me, my_chip = _dev_index(place), _chip_index(place)
    x0 = x[0]
    target = loss_target[0]

    def halves(w, tag):
        return _to_bf16(w[0], f"{tag}_to_bf16").reshape(2, w.shape[1] // 2, w.shape[2])

    gathered = _weights_allgather([halves(w_ffn1_in, "w_ffn1_in")], "weights_allgather")
    w1_in = gathered[0].reshape(N_CHIPS, D, FF_SHARD)
    w1_out_shard = halves(w_ffn1_out, "w_ffn1_out")
    mix_shards = [halves(w_mix_in, "w_mix_in"), halves(w_mix_out, "w_mix_out")]
    ffn2_shards = [halves(w_ffn2_in, "w_ffn2_in"), halves(w_ffn2_out, "w_ffn2_out")]
    core_arr = jnp.reshape(place[2], (1,)).astype(jnp.int32)
    chip_arr = jnp.reshape(my_chip, (1,)).astype(jnp.int32)

    small = jnp.concatenate([_pad_row(c, D), _pad_row(norm_g, D), _pad_row(hgrn_lb, D), jnp.zeros((5, D), F32)], axis=0)
    small_all = _allgather8(small, "small_allgather")
    c_all = small_all[:, 0, :]
    by_chip = small_all[0::2]
    norm_g_full = by_chip[:, 1, :3 * 256].reshape(N_CHIPS, 3, 256).transpose(1, 0, 2).reshape(3, D)
    lb_raw = by_chip[:, 2, :2 * 2 * 128].reshape(N_CHIPS, 2, 2, 128).transpose(1, 2, 0, 3).reshape(2, 2, HG_WIDTH)
    lb = jax.nn.sigmoid(lb_raw[:, 0, :] - lb_raw[:, 1, :])
    lb_f, lb_b = lb[0:1], lb[1:2]

    c_act_all = c_all * jax.nn.sigmoid(c_all)
    n_ada = w_ada.shape[2]
    b_mine = lax.dynamic_slice_in_dim(b_ada, my_chip * n_ada, n_ada, axis=1)
    mods_part = _ada_fwd(c_act_all, w_ada[0], b_mine, "ada_fwd")
    mods_all = _allgather8(mods_part, "mods_allgather")[0::2].transpose(1, 0, 2).reshape(8, N_MOD * D)
    mods = lax.dynamic_slice_in_dim(mods_all, me, 1, axis=0)
    sh1, sc1, g1, sh2, sc2, g2, sh3, sc3, g3 = [mods[:, i * D:(i + 1) * D] for i in range(N_MOD)]

    x1, saved1, w1_out, gathered, h2 = _ffn1_forward(x0, norm_g_full[0:1], sh1, sc1, g1, w1_in, w1_out_shard, mix_shards,
                                                     (norm_g_full[1:2], sh2, sc2))
    wm_in = gathered[0].reshape(N_CHIPS, D, D_IN // N_CHIPS).transpose(1, 0, 2).reshape(D, D_IN)
    wm_out = gathered[1].reshape(D, D)

    z = _matmul_nn(h2, wm_in, F32, 256, "mix_in")
    (of, st_f), partly = _hgrn_fwd(z, lb_f, 0, "hgrn_fwd_f", exchange=_gather_over_ici(ffn2_shards))
    (ob, st_b), gathered = _hgrn_fwd(z, lb_b, 1, "hgrn_fwd_b", exchange=_gather_over_d2d(ffn2_shards, partly))
    w2_in = gathered[0].reshape(N_CHIPS, D, FF_SHARD)
    w2_out = gathered[1].reshape(D_FF, D)
    o_h = _hgrn_post_fwd(of, ob, z, hgrn_norm_g, "hgrn_post")

    q_g, k_g = qk_norm_g[0, 0:1], qk_norm_g[0, 1:2]
    sink_b = jnp.broadcast_to(attn_sink.reshape(ATT_Q_HEADS, 1, 1), (ATT_Q_HEADS, 1, BLOCK))
    bias = _bias_table(rel_bias, "bias_table")
    o_a = _attn_fwd(z, q_g, k_g, sink_b, bias, "attn_fwd")
    x2, mixed, h3 = _proj_out_fwd([o_h, o_a], wm_out, x1, g2, 1.0, "mix_out", next_norm=(norm_g_full[2:3], sh3, sc3))

    zg3, zu3, a3 = _ffn_in_fwd(h3, w2_in, "ffn2_in")
    dx3, df3, dg3, sq_cols = _proj_out_loss(a3, w2_out, x2, g3, 0.5, target, "ffn2_out_loss")
    loss_mine = 0.5 * jnp.sum(sq_cols) / D

    (dx2, dsh3, dsc3, dng3, dmixed, dg2), parts2, mine2_out, _ = _ffn_backward(
        df3, (h3, zg3, zu3, a3), w2_in, w2_out, core_arr, chip_arr, "ffn2",
        norm=_NormBwd(x2, norm_g_full[2:3], sc3, dx3, below=(mixed, g2, 1.0)))

    (do_cat,) = _matmul_nt([dmixed], wm_out, ROW_TILE, "mix_out_dgrad")
    dwm_out = _wgrad_rows([o_h, o_a], dmixed, "mix_out_dw").reshape(D, D)

    do_sum, dgr, d_hnorm = _hgrn_post_bwd(do_cat, of, ob, z, hgrn_norm_g, "hgrn_post_bwd")
    (dq_f, dff, dv_f, doml_f), landed2 = _hgrn_bwd(z, lb_f, do_sum, st_f, 0, "hgrn_bwd_f",
                                                   exchange=_chips_exchange([p[1] for p in parts2]))
    mine2 = _chip_sums(chip_arr, parts2, landed2, "ffn2_in") + mine2_out
    (dhq, dfb, dhi, doml_b), theirs2 = _hgrn_bwd(z, lb_b, do_sum, st_b, 1, "hgrn_bwd_b", acc=(dq_f, dv_f),
                                                 exchange=_siblings_exchange(mine2))

    daq, dkw, dvw, ds_sum, dsink, dqg = _attn_bwd(z, q_g, k_g, sink_b, bias, do_cat, "attn_bwd")
    dkv, dkg = _attn_kv_reduce(dkw, dvw, z, k_g, "attn_kv_reduce")
    d_rel_bias = jnp.sum(_bias_grad(ds_sum, "bias_grad"), axis=-1).T
    dz = [dhq, dff, dfb, dhi, dgr, daq, dkv]
    dwm_in = _wgrad_pieces(h2, dz, 2 * KV_WIDTH, "mix_in_dw").transpose(1, 0, 2).reshape(D, D_IN)
    wide = D_IN // N_CHIPS
    grads_m = [_by_chip_cols(dwm_in.reshape(D, N_CHIPS, wide).transpose(1, 0, 2)), _by_chip_rows(dwm_out)]
    (dx1, dsh2, dsc2, dng2, df1, dg1), theirs_m = _matmul_nt(
        dz, wm_in, 256, "mix_in_dgrad", exchange=_halves_exchange(grads_m),
        norm=_NormBwd(x1, norm_g_full[1:2], sc2, dx2, below=(saved1[4], g1, 0.5)))
    parts_m = _pair_sums(core_arr, grads_m, theirs_m, "mix")

    (dh1,), parts1, mine1_out, landed_m = _ffn_backward(df1, saved1, w1_in, w1_out, core_arr, chip_arr, "ffn1",
                                                        riding=_chips_exchange([p[1] for p in parts_m]))
    mine_m = _chip_sums(chip_arr, parts_m, landed_m, "mix")
    (dx0, dsh1, dsc1, dng1), landed1 = _rmsmod_bwd(dh1, _NormBwd(x0, norm_g_full[0:1], sc1, dx1), "ffn1_norm_bwd",
                                                   exchange=_chips_exchange([p[1] for p in parts1]))
    mine1 = _chip_sums(chip_arr, parts1, landed1, "ffn1_in") + mine1_out
    theirs_1m = list(_run_exchange(_siblings_exchange(mine1 + mine_m), "siblings_exchange"))
    reduced = list(zip(mine1 + mine2 + mine_m, theirs_1m[:2] + list(theirs2) + theirs_1m[2:]))

    dlb = -jnp.concatenate([doml_f, doml_b], axis=0)
    dlb_raw = dlb * lb * (1.0 - lb)
    d_hgrn_lb = jnp.stack([dlb_raw, -dlb_raw], axis=1)
    d_qk = jnp.concatenate([jnp.sum(dqg, axis=0), jnp.sum(dkg, axis=0)], axis=0)
    dmods = jnp.concatenate([dsh1, dsc1, dg1, dsh2, dsc2, dg2, dsh3, dsc3, dg3], axis=0)
    packed = jnp.concatenate(
        [dmods, dng1, dng2, dng3, d_hgrn_lb.reshape(2, D), _pad_row(d_hnorm, D), _pad_row(d_qk, D),
         _pad_row(dsink[:, 0, 0], D), _pad_row(d_rel_bias, D), _pad_row(loss_mine, D)], axis=0)
    packed = jnp.pad(packed, ((0, 24 - packed.shape[0]), (0, 0)))
    packed_all, packed_sum = _allgather8(packed, "small_grads_allgather", reduce=True)
    dmods_all = packed_all[:, 0:N_MOD, :].reshape(8, N_MOD * D)
    g_b_ada = packed_sum[0:N_MOD].reshape(1, N_MOD * D)
    g_norm_full = packed_sum[9:12]
    g_norm_g = lax.dynamic_slice_in_dim(g_norm_full, my_chip * 256, 256, axis=1).reshape(1, 3, 256)
    g_hgrn_lb = lax.dynamic_slice_in_dim(packed_sum[12:14].reshape(2, 2, HG_WIDTH), my_chip * 128, 128, axis=2)
    g_hgrn_norm_g = packed_sum[14:15, :HG_WIDTH]
    g_qk_norm_g = packed_sum[15, :2 * ATT_HEAD_DIM].reshape(1, 2, ATT_HEAD_DIM)
    g_attn_sink = packed_sum[16:17, :ATT_Q_HEADS]
    g_rel_bias = packed_sum[17, :NUM_BUCKETS * ATT_Q_HEADS].reshape(NUM_BUCKETS, ATT_Q_HEADS)
    loss = packed_sum[18, 0]

    dm_mine = lax.dynamic_slice_in_dim(dmods_all, my_chip * n_ada, n_ada, axis=1)
    g_w_ada = _ada_wgrad(c_act_all.T, dm_mine, "ada_wgrad")[None]

    def big(w, g, m, v, name):
        d, nm, nv = _adamw(w[0], g[0], m[0], v[0], name)
        return d[None], nm[None], nv[None]

    def big_halves(w, g_pair, m, v, name):
        g, d, nm, nv = _adamw_halves(core_arr, w[0], g_pair[0], g_pair[1], m[0], v[0], name)
        return g[None], (d[None], nm[None], nv[None])

    g_w1_in, u_w1_in = big_halves(w_ffn1_in, reduced[0], m_w_ffn1_in, v_w_ffn1_in, "adamw_w_ffn1_in")
    g_w1_out, u_w1_out = big_halves(w_ffn1_out, reduced[1], m_w_ffn1_out, v_w_ffn1_out, "adamw_w_ffn1_out")
    g_w2_in, u_w2_in = big_halves(w_ffn2_in, reduced[2], m_w_ffn2_in, v_w_ffn2_in, "adamw_w_ffn2_in")
    g_w2_out, u_w2_out = big_halves(w_ffn2_out, reduced[3], m_w_ffn2_out, v_w_ffn2_out, "adamw_w_ffn2_out")
    g_wm_in, u_wm_in = big_halves(w_mix_in, reduced[4], m_w_mix_in, v_w_mix_in, "adamw_w_mix_in")
    g_wm_out, u_wm_out = big_halves(w_mix_out, reduced[5], m_w_mix_out, v_w_mix_out, "adamw_w_mix_out")

    smalls = [(b_ada, g_b_ada, m_b_ada, v_b_ada), (norm_g, g_norm_g, m_norm_g, v_norm_g), (hgrn_lb, g_hgrn_lb, m_hgrn_lb, v_hgrn_lb),
              (hgrn_norm_g, g_hgrn_norm_g, m_hgrn_norm_g, v_hgrn_norm_g), (qk_norm_g, g_qk_norm_g, m_qk_norm_g, v_qk_norm_g),
              (attn_sink, g_attn_sink, m_attn_sink, v_attn_sink), (rel_bias, g_rel_bias, m_rel_bias, v_rel_bias)]
    sizes = [t[0].size for t in smalls]
    total = sum(sizes)
    rows = -(-total // 128)
    rows = -(-rows // 8) * 8

    def pack(i):
        flat = jnp.concatenate([t[i].reshape(-1) for t in smalls])
        fill = 1.0 if i == 3 else 0.0
        return jnp.pad(flat, (0, rows * 128 - total), constant_values=fill).reshape(rows, 128)

    packed_out = _adamw(pack(0), pack(1), pack(2), pack(3), "adamw_small")

    def unpack(flat2d):
        flat = flat2d.reshape(-1)
        outs, off = [], 0
        for t, n in zip(smalls, sizes):
            outs.append(flat[off:off + n].reshape(t[0].shape))
            off += n
        return outs

    d_small, m_small, v_small = [unpack(t) for t in packed_out]

    upd = {
        "w_ada": big(w_ada, g_w_ada, m_w_ada, v_w_ada, "adamw_w_ada"),
        "w_ffn1_in": u_w1_in, "w_ffn1_out": u_w1_out, "w_ffn2_in": u_w2_in, "w_ffn2_out": u_w2_out,
        "w_mix_in": u_wm_in, "w_mix_out": u_wm_out,
    }
    small_names = ["b_ada", "norm_g", "hgrn_lb", "hgrn_norm_g", "qk_norm_g", "attn_sink", "rel_bias"]
    for i, nme in enumerate(small_names):
        upd[nme] = (d_small[i], m_small[i], v_small[i])
    grads = {
        "w_ada": g_w_ada, "b_ada": g_b_ada, "norm_g": g_norm_g, "w_ffn1_in": g_w1_in, "w_ffn1_out": g_w1_out,
        "w_ffn2_in": g_w2_in, "w_ffn2_out": g_w2_out, "w_mix_in": g_wm_in, "w_mix_out": g_wm_out, "hgrn_lb": g_hgrn_lb,
        "hgrn_norm_g": g_hgrn_norm_g, "qk_norm_g": g_qk_norm_g, "attn_sink": g_attn_sink, "rel_bias": g_rel_bias,
    }
    order = ["w_ada", "b_ada", "norm_g", "w_ffn1_in", "w_ffn1_out", "w_ffn2_in", "w_ffn2_out", "w_mix_in", "w_mix_out",
             "hgrn_lb", "hgrn_norm_g", "qk_norm_g", "attn_sink", "rel_bias"]
    return (loss, dx0[None], *[grads[k] for k in order], *[upd[k][0] for k in order], *[upd[k][1] for k in order],
            *[upd[k][2] for k in order])
```

```python
import functools
import math

import numpy as np
import jax
import jax.numpy as jnp
from jax import lax
from jax.experimental import pallas as pl
from jax.experimental.pallas import tpu as pltpu

F32, BF16 = jnp.float32, jnp.bfloat16

D_MODEL = 1024
D_FF = 2816
HG_HEADS, HG_DIM = 4, 128
HG_WIDTH = HG_HEADS * HG_DIM
ATT_Q_HEADS, ATT_KV_HEADS, ATT_HEAD_DIM = 8, 2, 64
ATT_GROUP = ATT_Q_HEADS // ATT_KV_HEADS
ATT_WIDTH = ATT_Q_HEADS * ATT_HEAD_DIM
KV_WIDTH = ATT_KV_HEADS * ATT_HEAD_DIM
WINDOW, BLOCK = 128, 128
NUM_BUCKETS, MAX_DISTANCE = 32, 128
N_MOD = 9
EPS = 1e-6
D_IN = 5 * HG_WIDTH + ATT_WIDTH + 2 * KV_WIDTH
ADAM_LR, ADAM_B1, ADAM_B2, ADAM_EPS, ADAM_WD, ADAM_STEP = 0.001, 0.9, 0.999, 1e-08, 0.01, 10

N_CHIPS = 4
FF_SHARD = 2 * D_FF // N_CHIPS
NEG = -1e30

VMEM_LIMIT_BYTES = 56 << 20
ROW_TILE = 512
HG_CHUNK = 16
HG_ROWS = 512

MESH = pl.DeviceIdType.MESH
ANY = pl.BlockSpec(memory_space=pl.ANY)


def _params(*sem):
    return pltpu.CompilerParams(dimension_semantics=sem, vmem_limit_bytes=VMEM_LIMIT_BYTES)


def _resident(shape, index_map):
    return pl.BlockSpec(shape, index_map, pipeline_mode=pl.Buffered(1))


def _dot(a, b, dims, precision=None):
    return lax.dot_general(a, b, (dims, ((), ())), precision=precision, preferred_element_type=F32)


def _nn(a, b, precision=None):
    return _dot(a, b, ((1,), (0,)), precision)


def _nt(a, b):
    return _dot(a, b, ((1,), (1,)))


def _tn(a, b):
    return _dot(a, b, ((0,), (0,)))


def _sigmoid(x):
    return jax.nn.sigmoid(x)


class _Exchange:
    def __init__(self, inputs, out_shapes, n_sems, plan, aliases=None, then=None):
        self.inputs, self.out_shapes, self.n_sems, self.plan, self.aliases = list(inputs), list(out_shapes), n_sems, plan, aliases or {}
        self.then = then

    def sem_shapes(self):
        return [pltpu.SemaphoreType.DMA((self.n_sems,)), pltpu.SemaphoreType.DMA((self.n_sems,))]

    def start(self, in_refs, out_refs, send_sems, recv_sems):
        for cp in self.plan(in_refs, out_refs, send_sems, recv_sems)[0]:
            cp.start()

    @staticmethod
    def _wait(sends, recvs):
        for cp in recvs:
            cp.wait_recv()
        for cp in sends:
            cp.wait_send()

    def switch(self, in_refs, out_refs, send_sems, recv_sems):
        if self.then:
            self._wait(*self.plan(in_refs, out_refs, send_sems, recv_sems))
            for cp in self.then(in_refs, out_refs, send_sems, recv_sems)[0]:
                cp.start()

    def finish(self, in_refs, out_refs, send_sems, recv_sems):
        self._wait(*(self.then or self.plan)(in_refs, out_refs, send_sems, recv_sems))


def _run_exchange(ex, name):
    n_in, n_out = len(ex.inputs), len(ex.out_shapes)

    def body(*refs):
        in_refs, out_refs, (send_sems, recv_sems) = refs[:n_in], refs[n_in:n_in + n_out], refs[n_in + n_out:]
        ex.start(in_refs, out_refs, send_sems, recv_sems)
        ex.switch(in_refs, out_refs, send_sems, recv_sems)
        ex.finish(in_refs, out_refs, send_sems, recv_sems)

    return pl.pallas_call(
        body, name=name, in_specs=[ANY] * n_in, out_specs=[ANY] * n_out, out_shape=ex.out_shapes,
        scratch_shapes=ex.sem_shapes(), input_output_aliases=dict(ex.aliases),
    )(*ex.inputs)


def _call(body, *, name, grid, in_specs, out_specs, out_shape, args, semantics, scratch_shapes=(), exchange=None):
    if exchange is None:
        return pl.pallas_call(
            body, name=name, grid=grid, in_specs=in_specs, out_specs=out_specs, out_shape=out_shape,
            scratch_shapes=list(scratch_shapes), compiler_params=_params(*semantics))(*args)
    exs = exchange if isinstance(exchange, (list, tuple)) else [exchange]
    n_in, n_out, n_scr = len(in_specs), len(out_specs), len(scratch_shapes)
    x_in, x_out = [len(ex.inputs) for ex in exs], [len(ex.out_shapes) for ex in exs]

    def take(refs, counts):
        groups = []
        for n in counts:
            groups.append(refs[:n])
            refs = refs[n:]
        return groups, refs

    def carrier(*refs):
        ins, refs = refs[:n_in], refs[n_in:]
        x_ins, refs = take(refs, x_in)
        outs, refs = refs[:n_out], refs[n_out:]
        x_outs, refs = take(refs, x_out)
        scr, refs = refs[:n_scr], refs[n_scr:]
        sems, _ = take(refs, [2] * len(exs))
        ids = [pl.program_id(a) for a in range(len(grid))]
        first = functools.reduce(jnp.logical_and, [i == 0 for i in ids])
        last = functools.reduce(jnp.logical_and, [i == g - 1 for i, g in zip(ids, grid)])
        step = functools.reduce(lambda acc, ig: acc * ig[1] + ig[0], zip(ids, grid), 0)

        @pl.when(first)
        def _():
            for ex, xi, xo, (send_sems, recv_sems) in zip(exs, x_ins, x_outs, sems):
                ex.start(xi, xo, send_sems, recv_sems)

        if any(ex.then for ex in exs):
            @pl.when(step == (3 * math.prod(grid)) // 4)
            def _():
                for ex, xi, xo, (send_sems, recv_sems) in zip(exs, x_ins, x_outs, sems):
                    ex.switch(xi, xo, send_sems, recv_sems)

        body(*ins, *outs, *scr)

        @pl.when(last)
        def _():
            for ex, xi, xo, (send_sems, recv_sems) in zip(exs, x_ins, x_outs, sems):
                ex.finish(xi, xo, send_sems, recv_sems)

    aliases, i0, o0 = {}, n_in, n_out
    for ex in exs:
        aliases.update({i0 + i: o0 + o for i, o in ex.aliases.items()})
        i0, o0 = i0 + len(ex.inputs), o0 + len(ex.out_shapes)
    res = pl.pallas_call(
        carrier, name=name, grid=grid, in_specs=list(in_specs) + [ANY] * sum(x_in),
        out_specs=list(out_specs) + [ANY] * sum(x_out),
        out_shape=list(out_shape) + [s for ex in exs for s in ex.out_shapes],
        scratch_shapes=list(scratch_shapes) + [s for ex in exs for s in ex.sem_shapes()],
        input_output_aliases=aliases, compiler_params=_params(*["arbitrary"] * len(grid)),
    )(*args, *[a for ex in exs for a in ex.inputs])
    x_res, _ = take(list(res[n_out:]), x_out)
    return list(res[:n_out]), (x_res if isinstance(exchange, (list, tuple)) else x_res[0])


def _rmsmod_fwd(x, g, shift, scale, name):
    S, D = x.shape
    tr = min(ROW_TILE, S)

    def body(x_ref, g_ref, sh_ref, sc_ref, h_ref):
        xv = x_ref[...]
        rstd = lax.rsqrt(jnp.mean(xv * xv, axis=-1, keepdims=True) + EPS)
        y = xv * rstd * g_ref[...]
        h_ref[...] = (y * (1.0 + sc_ref[...]) + sh_ref[...]).astype(h_ref.dtype)

    row = pl.BlockSpec((tr, D), lambda i: (i, 0))
    vec = pl.BlockSpec((1, D), lambda i: (0, 0))
    return pl.pallas_call(
        body, name=name, grid=(S // tr,), in_specs=[row, vec, vec, vec], out_specs=row,
        out_shape=jax.ShapeDtypeStruct((S, D), BF16), compiler_params=_params("parallel"),
    )(x, g, shift, scale)


class _NormBwd:
    def __init__(self, x, g, scale, dx_res, below=None):
        S, D = x.shape
        self.below, self.coef = below, (below[2] if below else None)
        self.inputs = [x, g, scale, dx_res] + ([below[0], below[1]] if below else [])
        vshape = jax.ShapeDtypeStruct((1, D), F32)
        self.out_shape = [jax.ShapeDtypeStruct((S, D), F32), vshape, vshape, vshape]
        if below:
            self.out_shape += [jax.ShapeDtypeStruct((S, D), BF16), vshape]

    def specs(self, tr, D):
        row = pl.BlockSpec((tr, D), lambda i: (i, 0))
        vec = pl.BlockSpec((1, D), lambda i: (0, 0))
        return ([row, vec, vec, row] + ([row, vec] if self.below else []),
                [row, vec, vec, vec] + ([row, vec] if self.below else []))

    def step(self, dhv, in_refs, out_refs):
        if self.below:
            x_ref, g_ref, sc_ref, dxr_ref, f_ref, gate_ref = in_refs
            dx_ref, dsh_ref, dsc_ref, dg_ref, df_ref, dgate_ref = out_refs
            sums = (dsh_ref, dsc_ref, dg_ref, dgate_ref)
        else:
            x_ref, g_ref, sc_ref, dxr_ref = in_refs
            dx_ref, dsh_ref, dsc_ref, dg_ref = out_refs
            sums = (dsh_ref, dsc_ref, dg_ref)

        @pl.when(pl.program_id(0) == 0)
        def _():
            for ref in sums:
                ref[...] = jnp.zeros_like(ref)

        xv, gv = x_ref[...], g_ref[...]
        one_sc = 1.0 + sc_ref[...]
        rstd = lax.rsqrt(jnp.mean(xv * xv, axis=-1, keepdims=True) + EPS)
        n = xv * rstd
        dsh_ref[...] += jnp.sum(dhv, axis=0, keepdims=True)
        dsc_ref[...] += jnp.sum(dhv * n, axis=0, keepdims=True) * gv
        dg_ref[...] += jnp.sum(dhv * n, axis=0, keepdims=True) * one_sc
        dn = dhv * (gv * one_sc)
        dx = dxr_ref[...] + rstd * (dn - n * jnp.mean(dn * n, axis=-1, keepdims=True))
        dx_ref[...] = dx
        if self.below:
            df_ref[...] = (self.coef * gate_ref[...] * dx).astype(df_ref.dtype)
            dgate_ref[...] += self.coef * jnp.sum(dx * f_ref[...].astype(F32), axis=0, keepdims=True)


def _rmsmod_bwd(dh, norm, name, exchange=None):
    S, D = dh.shape
    tr = min(ROW_TILE, S)
    n_in = len(norm.inputs)

    def body(dh_ref, *refs):
        norm.step(dh_ref[...], refs[:n_in], refs[n_in:])

    in_specs, out_specs = norm.specs(tr, D)
    return _call(body, name=name, grid=(S // tr,), in_specs=[pl.BlockSpec((tr, D), lambda i: (i, 0))] + in_specs,
                 out_specs=out_specs, out_shape=norm.out_shape, args=[dh] + norm.inputs, semantics=("arbitrary",),
                 exchange=exchange)


def _ffn_in_fwd(h, w4, name, exchange=None):
    S, D = h.shape
    tm = min(2 * ROW_TILE, S)
    n = w4.shape[2]

    def body(h_ref, wg_ref, wu_ref, zg_ref, zu_ref, a_ref):
        hv = h_ref[...]
        zg = _nn(hv, wg_ref[...])
        zu = _nn(hv, wu_ref[...])
        zg_ref[...] = zg.astype(zg_ref.dtype)
        zu_ref[...] = zu.astype(zu_ref.dtype)
        a_ref[...] = (zg * _sigmoid(zg) * zu).astype(a_ref.dtype)

    out = pl.BlockSpec((tm, n), lambda j, m: (m, j))
    oshape = jax.ShapeDtypeStruct((S, 2 * n), BF16)
    return _call(
        body, name=name, grid=(2, S // tm),
        in_specs=[pl.BlockSpec((tm, D), lambda j, m: (m, 0)),
                  pl.BlockSpec((None, D, n), lambda j, m: (j, 0, 0)),
                  pl.BlockSpec((None, D, n), lambda j, m: (j + 2, 0, 0))],
        out_specs=[out, out, out], out_shape=[oshape, oshape, oshape], args=(h, w4, w4),
        semantics=("parallel", "parallel"), exchange=exchange)


def _proj_out_fwd(lhs, w, x, gate, coef, name, exchange=None, next_norm=None):
    S, D = x.shape
    tm = min(ROW_TILE, S)
    ks = [a.shape[1] for a in lhs]

    def body(*refs):
        lhs_refs, refs = refs[:len(lhs)], refs[len(lhs):]
        if next_norm:
            w_ref, x_ref, gate_ref, g_ref, sh_ref, sc_ref, xn_ref, f_ref, h_ref = refs
        else:
            w_ref, x_ref, gate_ref, xn_ref, f_ref = refs
        acc, off = None, 0
        for a_ref, k in zip(lhs_refs, ks):
            part = _nn(a_ref[...], w_ref[off:off + k, :])
            acc = part if acc is None else acc + part
            off += k
        f_ref[...] = acc.astype(f_ref.dtype)
        xn = x_ref[...] + coef * gate_ref[...] * acc
        xn_ref[...] = xn
        if next_norm:
            rstd = lax.rsqrt(jnp.mean(xn * xn, axis=-1, keepdims=True) + EPS)
            h_ref[...] = (xn * rstd * g_ref[...] * (1.0 + sc_ref[...]) + sh_ref[...]).astype(h_ref.dtype)

    row = pl.BlockSpec((tm, D), lambda m: (m, 0))
    vec = pl.BlockSpec((1, D), lambda m: (0, 0))
    extra = list(next_norm) if next_norm else []
    return _call(
        body, name=name, grid=(S // tm,),
        in_specs=[pl.BlockSpec((tm, k), lambda m: (m, 0)) for k in ks]
        + [_resident(w.shape, lambda m: (0, 0)), row, vec] + [vec] * len(extra),
        out_specs=[row, row] + ([row] if next_norm else []),
        out_shape=[jax.ShapeDtypeStruct((S, D), F32), jax.ShapeDtypeStruct((S, D), BF16)]
        + ([jax.ShapeDtypeStruct((S, D), BF16)] if next_norm else []),
        args=(*lhs, w, x, gate, *extra), semantics=("parallel",), exchange=exchange)


def _proj_out_loss(lhs, w, x, gate, coef, target, name):
    S, D = x.shape
    tm = min(ROW_TILE, S)

    def body(a_ref, w_ref, x_ref, gate_ref, t_ref, dy_ref, df_ref, dgate_ref, sq_ref):
        @pl.when(pl.program_id(0) == 0)
        def _():
            dgate_ref[...] = jnp.zeros_like(dgate_ref)
            sq_ref[...] = jnp.zeros_like(sq_ref)

        f = _nn(a_ref[...], w_ref[...])
        gate = coef * gate_ref[...]
        err = x_ref[...] + gate * f - t_ref[...]
        sq_ref[...] += jnp.sum(err * err, axis=0, keepdims=True)
        dy = err * (1.0 / D)
        dy_ref[...] = dy
        df_ref[...] = (gate * dy).astype(df_ref.dtype)
        dgate_ref[...] += coef * jnp.sum(dy * f, axis=0, keepdims=True)

    row = pl.BlockSpec((tm, D), lambda m: (m, 0))
    vec = pl.BlockSpec((1, D), lambda m: (0, 0))
    vshape = jax.ShapeDtypeStruct((1, D), F32)
    return pl.pallas_call(
        body, name=name, grid=(S // tm,),
        in_specs=[pl.BlockSpec((tm, lhs.shape[1]), lambda m: (m, 0)), _resident(w.shape, lambda m: (0, 0)), row, vec, row],
        out_specs=[row, row, vec, vec],
        out_shape=[jax.ShapeDtypeStruct((S, D), F32), jax.ShapeDtypeStruct((S, D), BF16), vshape, vshape],
        compiler_params=_params("arbitrary"),
    )(lhs, w, x, gate, target)


def _matmul_nn(a, w, out_dtype, tm, name):
    S, K = a.shape
    N = w.shape[1]
    tm = min(tm, S)

    def body(a_ref, w_ref, o_ref):
        o_ref[...] = _nn(a_ref[...], w_ref[...]).astype(o_ref.dtype)

    return pl.pallas_call(
        body, name=name, grid=(S // tm,),
        in_specs=[pl.BlockSpec((tm, K), lambda m: (m, 0)), _resident((K, N), lambda m: (0, 0))],
        out_specs=pl.BlockSpec((tm, N), lambda m: (m, 0)), out_shape=jax.ShapeDtypeStruct((S, N), out_dtype),
        compiler_params=_params("parallel"),
    )(a, w)


def _dact_bwd(df, w_out, zg, zu, name, exchange=None):
    S, D = df.shape
    tm = min(ROW_TILE, S)
    n = w_out.shape[0] // 2

    def body(df_ref, w_ref, zg_ref, zu_ref, dzg_ref, dzu_ref):
        da = _nt(df_ref[...], w_ref[...]).astype(BF16)
        zg_v, zu_v = zg_ref[...], zu_ref[...]
        s = _sigmoid(zg_v)
        dzu_ref[...] = da * zg_v * s
        dzg_ref[...] = da * zu_v * (s * (1.0 + zg_v * (1.0 - s)))

    blk = pl.BlockSpec((tm, n), lambda j, m: (m, j))
    oshape = jax.ShapeDtypeStruct((S, 2 * n), BF16)
    return _call(
        body, name=name, grid=(2, S // tm),
        in_specs=[pl.BlockSpec((tm, D), lambda j, m: (m, 0)), pl.BlockSpec((n, D), lambda j, m: (j, 0)), blk, blk],
        out_specs=[blk, blk], out_shape=[oshape, oshape], args=(df, w_out, zg, zu), semantics=("parallel", "parallel"),
        exchange=exchange)


def _ffn_in_dgrad(dzg, dzu, w4, name, exchange=None, norm=None):
    S = dzg.shape[0]
    D, n = w4.shape[1], w4.shape[2]
    tm = min(ROW_TILE, S)
    n_norm = len(norm.inputs) if norm else 0

    def body(dzg_ref, dzu_ref, w_ref, *refs):
        acc = _nt(dzg_ref[:, 0:n], w_ref[0])
        acc += _nt(dzg_ref[:, n:2 * n], w_ref[1])
        acc += _nt(dzu_ref[:, 0:n], w_ref[2])
        acc += _nt(dzu_ref[:, n:2 * n], w_ref[3])
        if norm:
            norm.step(acc, refs[:n_norm], refs[n_norm:])
        else:
            refs[0][...] = acc

    blk = pl.BlockSpec((tm, 2 * n), lambda m: (m, 0))
    in_specs, args = [blk, blk, _resident(w4.shape, lambda m: (0, 0, 0))], [dzg, dzu, w4]
    out_specs, out_shape = [pl.BlockSpec((tm, D), lambda m: (m, 0))], [jax.ShapeDtypeStruct((S, D), F32)]
    if norm:
        norm_in, out_specs = norm.specs(tm, D)
        in_specs, args, out_shape = in_specs + norm_in, args + norm.inputs, norm.out_shape
    return _call(body, name=name, grid=(S // tm,), in_specs=in_specs, out_specs=out_specs, out_shape=out_shape, args=args,
                 semantics=("arbitrary",) if norm else ("parallel",), exchange=exchange)


def _matmul_nt(pieces, w, tm, name, exchange=None, norm=None):
    S = pieces[0].shape[0]
    ks = [p.shape[1] for p in pieces]
    N = w.shape[0]
    tm = min(tm, S)
    n_norm = len(norm.inputs) if norm else 0

    def body(*refs):
        p_refs, w_ref, refs = refs[:len(ks)], refs[len(ks)], refs[len(ks) + 1:]
        acc, off = None, 0
        for p_ref, k in zip(p_refs, ks):
            part = _nt(p_ref[...], w_ref[:, off:off + k])
            acc = part if acc is None else acc + part
            off += k
        if norm:
            norm.step(acc, refs[:n_norm], refs[n_norm:])
        else:
            refs[0][...] = acc

    in_specs = [pl.BlockSpec((tm, k), lambda m: (m, 0)) for k in ks] + [_resident(w.shape, lambda m: (0, 0))]
    args = list(pieces) + [w]
    out_specs, out_shape = [pl.BlockSpec((tm, N), lambda m: (m, 0))], [jax.ShapeDtypeStruct((S, N), F32)]
    if norm:
        norm_in, out_specs = norm.specs(tm, N)
        in_specs, args, out_shape = in_specs + norm_in, args + norm.inputs, norm.out_shape
    return _call(body, name=name, grid=(S // tm,), in_specs=in_specs, out_specs=out_specs, out_shape=out_shape, args=args,
                 semantics=("arbitrary",) if norm else ("parallel",), exchange=exchange)


def _wgrad(a, gs, tn, name, exchange=None):
    S, Ka = a.shape
    N = gs[0].shape[1]
    ts = min(ROW_TILE * (2 if Ka <= D_MODEL else 1), S)

    def body(a_ref, *refs):
        g_refs, o_ref = refs[:-1], refs[-1]

        @pl.when(pl.program_id(1) == 0)
        def _():
            o_ref[...] = jnp.zeros_like(o_ref)

        a_t = a_ref[...].T
        for i, g_ref in enumerate(g_refs):
            o_ref[i] += _nn(a_t, g_ref[...])

    return _call(
        body, name=name, grid=(N // tn, S // ts),
        in_specs=[pl.BlockSpec((ts, Ka), lambda j, s: (s, 0))] + [pl.BlockSpec((ts, tn), lambda j, s: (s, j))] * len(gs),
        out_specs=[pl.BlockSpec((len(gs), None, Ka, tn), lambda j, s: (0, j, 0, 0))],
        out_shape=[jax.ShapeDtypeStruct((len(gs), N // tn, Ka, tn), F32)], args=(a, *gs),
        semantics=("parallel", "arbitrary"), exchange=exchange)


def _wgrad_pieces(a, pieces, tn, name):
    S, Ka = a.shape
    ts = min(ROW_TILE, S)
    blocks = [(i, j) for i, p in enumerate(pieces) for j in range(p.shape[1] // tn)]

    def body(a_ref, *refs):
        g_refs, o_ref = refs[:-1], refs[-1]

        @pl.when(pl.program_id(0) == 0)
        def _():
            o_ref[...] = jnp.zeros_like(o_ref)

        a_t = a_ref[...].T
        for b, g_ref in enumerate(g_refs):
            o_ref[b] += _nn(a_t, g_ref[...])

    return pl.pallas_call(
        body, name=name, grid=(S // ts,),
        in_specs=[pl.BlockSpec((ts, Ka), lambda s: (s, 0))] + [pl.BlockSpec((ts, tn), lambda s, j=j: (s, j)) for _, j in blocks],
        out_specs=pl.BlockSpec((len(blocks), Ka, tn), lambda s: (0, 0, 0)),
        out_shape=jax.ShapeDtypeStruct((len(blocks), Ka, tn), F32), compiler_params=_params("arbitrary"),
    )(a, *[pieces[i] for i, _ in blocks])


def _wgrad_rows(lhs, g, name):
    S, Ka = lhs[0].shape
    N = g.shape[1]
    ts = min(ROW_TILE, S)

    def body(*refs):
        a_refs, g_ref, o_ref = refs[:-2], refs[-2], refs[-1]

        @pl.when(pl.program_id(0) == 0)
        def _():
            o_ref[...] = jnp.zeros_like(o_ref)

        gv = g_ref[...]
        for i, a_ref in enumerate(a_refs):
            o_ref[i] += _tn(a_ref[...], gv)

    return pl.pallas_call(
        body, name=name, grid=(S // ts,),
        in_specs=[pl.BlockSpec((ts, Ka), lambda s: (s, 0))] * len(lhs) + [pl.BlockSpec((ts, N), lambda s: (s, 0))],
        out_specs=pl.BlockSpec((len(lhs), Ka, N), lambda s: (0, 0, 0)),
        out_shape=jax.ShapeDtypeStruct((len(lhs), Ka, N), F32), compiler_params=_params("arbitrary"),
    )(*lhs, g)


def _hgrn_chunk_common(qr, fr, oml, tri, last):
    k = oml * _sigmoid(-fr)
    g = jnp.log1p(-k) * math.log2(math.e)
    q = qr * _sigmoid(qr)
    G = _nn(tri, g, precision=lax.Precision.HIGHEST)
    Gl = G[last:last + 1]
    return q, k, G, Gl


def _hgrn_consts(reverse):
    C = HG_CHUNK
    r = lax.broadcasted_iota(jnp.int32, (C, C), 0)
    cc = lax.broadcasted_iota(jnp.int32, (C, C), 1)
    tri = ((cc >= r) if reverse else (cc <= r)).astype(F32)
    tri_t = ((cc <= r) if reverse else (cc >= r)).astype(F32)
    rid = lax.broadcasted_iota(jnp.int32, (C, HG_WIDTH), 0)
    return tri, tri_t, rid, (0 if reverse else C - 1)


def _head_slices():
    return [slice(h * HG_DIM, (h + 1) * HG_DIM) for h in range(HG_HEADS)]


def _per_head_lane_sum(x):
    C = x.shape[0]
    return jnp.concatenate(
        [jnp.broadcast_to(jnp.sum(x[:, sl], axis=-1, keepdims=True), (C, HG_DIM)) for sl in _head_slices()], axis=1)


HG_TILE = 8


def _pair_tiles(s, reverse):
    blk, r = divmod(s, HG_TILE)
    n_tiles = HG_CHUNK // HG_TILE
    others = range(0, blk) if reverse else range(blk + 1, n_tiles)
    return [(blk, r)] + [(t, None) for t in others]


def _pair_decay(G, s, tile, r, rid8, reverse, keys=False):
    rs = slice(tile * HG_TILE, (tile + 1) * HG_TILE)
    d = (G[s:s + 1] - G[rs]) if keys else (G[rs] - G[s:s + 1])
    if r is not None:
        d = jnp.where((rid8 <= r) if reverse else (rid8 >= r), d, NEG)
    return rs, jnp.exp2(d)


def _hgrn_fwd(z, lb, direction, name, exchange=None):
    S = z.shape[0]
    C, DK, W = HG_CHUNK, HG_DIM, HG_WIDTH
    tb = min(HG_ROWS, S)
    n_t, n_c = S // tb, tb // C
    reverse = direction == 1
    tmap = (lambda i: n_t - 1 - i) if reverse else (lambda i: i)

    def body(q_ref, f_ref, v_ref, lb_ref, o_ref, st_out_ref, st_ref):
        @pl.when(pl.program_id(0) == 0)
        def _():
            st_ref[...] = jnp.zeros_like(st_ref)

        oml = 1.0 - lb_ref[...]
        tri, _, _, last = _hgrn_consts(reverse)
        rid8 = lax.broadcasted_iota(jnp.int32, (HG_TILE, W), 0)

        def chunk(ci, carry):
            cidx = (n_c - 1 - ci) if reverse else ci
            rows = pl.ds(pl.multiple_of(cidx * C, C), C)
            v = v_ref[rows, :]
            q, k, G, Gl = _hgrn_chunk_common(q_ref[rows, :], f_ref[rows, :], oml, tri, last)
            qd = (q * jnp.exp2(G)).astype(BF16)
            kd = (k * jnp.exp2(Gl - G)).astype(BF16)
            e_gl = jnp.exp2(Gl)
            v_b = v.astype(BF16)
            inter = []
            for h, sl in enumerate(_head_slices()):
                st0 = st_ref[h]
                st_out_ref[h, cidx] = st0
                inter.append(_nt(qd[:, sl], st0.astype(BF16)))
                st_ref[h] = st0 * e_gl[:, sl] + _tn(v_b[:, sl], kd[:, sl])
            o = jnp.concatenate(inter, axis=1)
            o_t = [o[t * HG_TILE:(t + 1) * HG_TILE] for t in range(C // HG_TILE)]
            for s in range(C):
                k_s, v_s = k[s:s + 1], v[s:s + 1]
                for tile, r in _pair_tiles(s, reverse):
                    rs, e_s = _pair_decay(G, s, tile, r, rid8, reverse)
                    o_t[tile] = o_t[tile] + _per_head_lane_sum(q[rs] * k_s * e_s) * v_s
            o_ref[rows, :] = jnp.concatenate(o_t, axis=0)
            return carry

        lax.fori_loop(0, n_c, chunk, 0, unroll=8)

    def sec(j):
        return pl.BlockSpec((tb, W), lambda i: (tmap(i), j))

    return _call(
        body, name=name, grid=(n_t,),
        in_specs=[sec(0), sec(1 + direction), sec(3), pl.BlockSpec((1, W), lambda i: (0, 0))],
        out_specs=[sec(0), pl.BlockSpec((HG_HEADS, n_c, DK, DK), lambda i: (0, tmap(i), 0, 0))],
        out_shape=[jax.ShapeDtypeStruct((S, W), F32), jax.ShapeDtypeStruct((HG_HEADS, S // C, DK, DK), F32)],
        scratch_shapes=[pltpu.VMEM((HG_HEADS, DK, DK), F32)], args=(z, z, z, lb), semantics=("arbitrary",),
        exchange=exchange)


def _hgrn_bwd(z, lb, do, states, direction, name, acc=None, exchange=None):
    S = z.shape[0]
    C, DK, W = HG_CHUNK, HG_DIM, HG_WIDTH
    tb = min(HG_ROWS, S)
    n_t, n_c = S // tb, tb // C
    reverse = direction == 1
    tmap = (lambda i: i) if reverse else (lambda i: n_t - 1 - i)

    def body(*refs):
        if acc:
            q_ref, f_ref, v_ref, lb_ref, do_ref, st_in_ref, dqa_ref, dva_ref, dq_ref, df_ref, dv_ref, doml_ref, dst_ref = refs
        else:
            q_ref, f_ref, v_ref, lb_ref, do_ref, st_in_ref, dq_ref, df_ref, dv_ref, doml_ref, dst_ref = refs

        @pl.when(pl.program_id(0) == 0)
        def _():
            dst_ref[...] = jnp.zeros_like(dst_ref)
            doml_ref[...] = jnp.zeros_like(doml_ref)

        oml = 1.0 - lb_ref[...]
        tri, tri_t, rid, last = _hgrn_consts(reverse)
        rid8 = lax.broadcasted_iota(jnp.int32, (HG_TILE, W), 0)

        def chunk(ci, carry):
            cidx = ci if reverse else (n_c - 1 - ci)
            rows = pl.ds(pl.multiple_of(cidx * C, C), C)
            qr, fr, v, dov = q_ref[rows, :], f_ref[rows, :], v_ref[rows, :], do_ref[rows, :]
            q, k, G, Gl = _hgrn_chunk_common(qr, fr, oml, tri, last)
            e_g, e_gl, e_kd = jnp.exp2(G), jnp.exp2(Gl), jnp.exp2(Gl - G)
            qd, kd = q * e_g, k * e_kd
            do_b, v_b, qd_b, kd_b = dov.astype(BF16), v.astype(BF16), qd.astype(BF16), kd.astype(BF16)
            dqd, dkd, dv, state_dot = [], [], [], []
            for h, sl in enumerate(_head_slices()):
                st0, dst1 = st_in_ref[h, cidx], dst_ref[h]
                dst1_b = dst1.astype(BF16)
                dqd.append(_nn(do_b[:, sl], st0.astype(BF16)))
                dkd.append(_nn(v_b[:, sl], dst1_b))
                dv.append(_nt(kd_b[:, sl], dst1_b))
                state_dot.append(jnp.sum(st0 * dst1, axis=0, keepdims=True))
                dst_ref[h] = dst1 * e_gl[:, sl] + _tn(do_b[:, sl], qd_b[:, sl])
            dqd, dkd, dv = [jnp.concatenate(t, axis=1) for t in (dqd, dkd, dv)]
            d_gl = e_gl * jnp.concatenate(state_dot, axis=1) + jnp.sum(dkd * kd, axis=0, keepdims=True)
            dq, dk = dqd * e_g, dkd * e_kd
            n_tiles = C // HG_TILE
            dq_t, dk_t, dv_t = [[x[t * HG_TILE:(t + 1) * HG_TILE] for t in range(n_tiles)] for x in (dq, dk, dv)]
            for s in range(C):
                k_s, v_s = k[s:s + 1], v[s:s + 1]
                for tile, r in _pair_tiles(s, reverse):
                    rs, e_s = _pair_decay(G, s, tile, r, rid8, reverse)
                    dq_t[tile] = dq_t[tile] + _per_head_lane_sum(dov[rs] * v_s) * e_s * k_s
            for t in range(C):
                q_t, do_t = q[t:t + 1], dov[t:t + 1]
                for tile, r in _pair_tiles(t, not reverse):
                    rs, x_t = _pair_decay(G, t, tile, r, rid8, not reverse, keys=True)
                    qx = q_t * x_t
                    dv_t[tile] = dv_t[tile] + _per_head_lane_sum(k[rs] * qx) * do_t
                    dk_t[tile] = dk_t[tile] + _per_head_lane_sum(v[rs] * do_t) * qx
            dq, dk, dv = [jnp.concatenate(x, axis=0) for x in (dq_t, dk_t, dv_t)]
            d_big_g = dq * q - dk * k + jnp.where(rid == last, d_gl, 0.0)
            dg = _nn(tri_t, d_big_g, precision=lax.Precision.HIGHEST)
            dk_all = dk - dg / (1.0 - k)
            sig_nf = _sigmoid(-fr)
            df_ref[rows, :] = (-dk_all * k * (1.0 - sig_nf)).astype(df_ref.dtype)
            doml_ref[...] += jnp.sum(dk_all * sig_nf, axis=0, keepdims=True)
            sq = _sigmoid(qr)
            dqr = dq * (sq * (1.0 + qr * (1.0 - sq)))
            if acc:
                dqr = dqr + dqa_ref[rows, :]
                dv = dv + dva_ref[rows, :]
            dq_ref[rows, :] = dqr.astype(dq_ref.dtype)
            dv_ref[rows, :] = dv.astype(dv_ref.dtype)
            return carry

        lax.fori_loop(0, n_c, chunk, 0, unroll=8)

    def sec(j):
        return pl.BlockSpec((tb, W), lambda i: (tmap(i), j))

    vec = pl.BlockSpec((1, W), lambda i: (0, 0))
    ins = [z, z, z, lb, do, states]
    in_specs = [sec(0), sec(1 + direction), sec(3), vec, sec(0),
                pl.BlockSpec((HG_HEADS, n_c, DK, DK), lambda i: (0, tmap(i), 0, 0))]
    if acc:
        ins += list(acc)
        in_specs += [sec(0), sec(0)]
    final = jax.ShapeDtypeStruct((S, W), BF16)
    partial = final if acc else jax.ShapeDtypeStruct((S, W), F32)
    return _call(
        body, name=name, grid=(n_t,), in_specs=in_specs,
        out_specs=[sec(0), sec(0), sec(0), vec],
        out_shape=[partial, final, partial, jax.ShapeDtypeStruct((1, W), F32)],
        scratch_shapes=[pltpu.VMEM((HG_HEADS, DK, DK), F32)], args=ins, semantics=("arbitrary",), exchange=exchange)


def _hgrn_post_fwd(o_f, o_b, z, norm_g, name):
    S = z.shape[0]
    tr = min(ROW_TILE, S)

    def body(of_ref, ob_ref, gr_ref, ng_ref, y_ref):
        o = of_ref[...] + ob_ref[...]
        gr = gr_ref[...]
        gate = gr * _sigmoid(gr)
        ng = ng_ref[...]
        for h in range(HG_HEADS):
            sl = slice(h * HG_DIM, (h + 1) * HG_DIM)
            oh = o[:, sl]
            rstd = lax.rsqrt(jnp.mean(oh * oh, axis=-1, keepdims=True) + EPS)
            y_ref[:, sl] = (oh * rstd * ng[:, sl] * gate[:, sl]).astype(y_ref.dtype)

    row = pl.BlockSpec((tr, HG_WIDTH), lambda i: (i, 0))
    return pl.pallas_call(
        body, name=name, grid=(S // tr,),
        in_specs=[row, row, pl.BlockSpec((tr, HG_WIDTH), lambda i: (i, 4)), pl.BlockSpec((1, HG_WIDTH), lambda i: (0, 0))],
        out_specs=row, out_shape=jax.ShapeDtypeStruct((S, HG_WIDTH), BF16), compiler_params=_params("parallel"),
    )(o_f, o_b, z, norm_g)


def _hgrn_post_bwd(dy, o_f, o_b, z, norm_g, name):
    S = z.shape[0]
    tr = min(ROW_TILE, S)

    def body(dy_ref, of_ref, ob_ref, gr_ref, ng_ref, do_ref, dgr_ref, dng_ref):
        @pl.when(pl.program_id(0) == 0)
        def _():
            dng_ref[...] = jnp.zeros_like(dng_ref)

        o = of_ref[...] + ob_ref[...]
        gr, ng, dyv = gr_ref[...], ng_ref[...], dy_ref[...]
        sg = _sigmoid(gr)
        for h in range(HG_HEADS):
            sl = slice(h * HG_DIM, (h + 1) * HG_DIM)
            oh, dyh, grh, sgh, ngh = o[:, sl], dyv[:, sl], gr[:, sl], sg[:, sl], ng[:, sl]
            rstd = lax.rsqrt(jnp.mean(oh * oh, axis=-1, keepdims=True) + EPS)
            on = oh * rstd
            du = dyh * (grh * sgh)
            dgr_ref[:, sl] = (dyh * (on * ngh) * (sgh * (1.0 + grh * (1.0 - sgh)))).astype(dgr_ref.dtype)
            dng_ref[:, sl] += jnp.sum(du * on, axis=0, keepdims=True)
            don = du * ngh
            do_ref[:, sl] = rstd * (don - on * jnp.mean(don * on, axis=-1, keepdims=True))

    row = pl.BlockSpec((tr, HG_WIDTH), lambda i: (i, 0))
    vec = pl.BlockSpec((1, HG_WIDTH), lambda i: (0, 0))
    full = jax.ShapeDtypeStruct((S, HG_WIDTH), F32)
    return pl.pallas_call(
        body, name=name, grid=(S // tr,),
        in_specs=[row, row, row, pl.BlockSpec((tr, HG_WIDTH), lambda i: (i, 4)), vec],
        out_specs=[row, row, vec],
        out_shape=[full, jax.ShapeDtypeStruct((S, HG_WIDTH), BF16), jax.ShapeDtypeStruct((1, HG_WIDTH), F32)],
        compiler_params=_params("arbitrary"),
    )(dy, o_f, o_b, z, norm_g)


def _t5_bucket_table():
    rel = (np.arange(3 * BLOCK)[None, :] - BLOCK) - np.arange(BLOCK)[:, None]
    nb = NUM_BUCKETS // 2
    max_exact = nb // 2
    ret = (rel > 0).astype(np.int32) * nb
    n = np.abs(rel)
    ratio = np.log(np.maximum(n, 1).astype(np.float32) / np.float32(max_exact)) / np.float32(math.log(MAX_DISTANCE / max_exact))
    large = max_exact + (ratio.astype(np.float32) * np.float32(nb - max_exact)).astype(np.int32)
    large = np.minimum(large, nb - 1)
    bucket = ret + np.where(n < max_exact, n, large)
    return bucket.astype(np.int32), (n <= WINDOW)


def _bias_table(rel_bias, name):
    bucket, in_band = _t5_bucket_table()
    idx = jnp.asarray(np.where(in_band, bucket, -1))

    def body(rb_ref, idx_ref, o_ref):
        h = pl.program_id(0)
        iv = idx_ref[...]
        acc = jnp.where(iv < 0, NEG, 0.0).astype(F32)
        for b in range(NUM_BUCKETS):
            acc = acc + jnp.where(iv == b, rb_ref[b, h], 0.0)
        o_ref[...] = acc

    return pl.pallas_call(
        body, name=name, grid=(ATT_Q_HEADS,),
        in_specs=[pl.BlockSpec(memory_space=pltpu.SMEM), pl.BlockSpec((BLOCK, 3 * BLOCK), lambda h: (0, 0))],
        out_specs=pl.BlockSpec((None, BLOCK, 3 * BLOCK), lambda h: (h, 0, 0)),
        out_shape=jax.ShapeDtypeStruct((ATT_Q_HEADS, BLOCK, 3 * BLOCK), F32), compiler_params=_params("parallel"),
    )(rel_bias, idx)


def _bias_grad(ds_sum_t, name):
    bucket, in_band = _t5_bucket_table()
    idx_t = jnp.asarray(np.where(in_band, bucket, -1).T)

    def body(ds_ref, idx_ref, o_ref):
        iv, ds = idx_ref[...], ds_ref[...]
        for b in range(NUM_BUCKETS):
            o_ref[b:b + 1, :] = jnp.sum(jnp.where(iv == b, ds, 0.0), axis=0, keepdims=True)

    return pl.pallas_call(
        body, name=name, grid=(ATT_Q_HEADS,),
        in_specs=[pl.BlockSpec((None, 3 * BLOCK, BLOCK), lambda h: (h // ATT_GROUP, 0, h % ATT_GROUP)),
                  pl.BlockSpec((3 * BLOCK, BLOCK), lambda h: (0, 0))],
        out_specs=pl.BlockSpec((None, NUM_BUCKETS, BLOCK), lambda h: (h, 0, 0)),
        out_shape=jax.ShapeDtypeStruct((ATT_Q_HEADS, NUM_BUCKETS, BLOCK), F32), compiler_params=_params("parallel"),
    )(ds_sum_t, idx_t)


Q_COL = 5 * HG_WIDTH
KV_COL = Q_COL + ATT_WIDTH
GROUP_WIDTH = ATT_GROUP * ATT_HEAD_DIM


def _stack_heads(blk):
    dh = ATT_HEAD_DIM
    return jnp.concatenate([blk[:, g * dh:(g + 1) * dh] for g in range(ATT_GROUP)], axis=0)


def _unstack_heads(st):
    return jnp.concatenate([st[g * BLOCK:(g + 1) * BLOCK] for g in range(ATT_GROUP)], axis=1)


def _rms_rows(x):
    rstd = lax.rsqrt(jnp.mean(x * x, axis=-1, keepdims=True) + EPS)
    return x * rstd, rstd


def _edge_ok(n, nb):
    colid = lax.broadcasted_iota(jnp.int32, (ATT_GROUP * BLOCK, 3 * BLOCK), 1)
    return jnp.logical_and(jnp.logical_or(colid >= BLOCK, n > 0), jnp.logical_or(colid < 2 * BLOCK, n < nb - 1))


def _sink_column(sink_ref, j=0):
    heads = range(j * ATT_GROUP, (j + 1) * ATT_GROUP)
    return jnp.concatenate([jnp.broadcast_to(sink_ref[h][:, 0:1], (BLOCK, 1)) for h in heads], axis=0)


def _attn_fwd(z, q_g, k_g, sink, bias, name):
    S = z.shape[0]
    nb = S // BLOCK
    G, dh, KV = ATT_GROUP, ATT_HEAD_DIM, ATT_KV_HEADS
    scale = 1.0 / math.sqrt(dh)

    def body(q_ref, kv0, kv1, kv2, qg_ref, kg_ref, sink_ref, bias_ref, o_ref):
        n = pl.program_id(0)
        edge_ok = _edge_ok(n, nb)
        cat = jnp.concatenate([kv0[...], kv1[...], kv2[...]], axis=0)
        qblk = q_ref[...]
        kn = [(_rms_rows(cat[:, j * dh:(j + 1) * dh])[0] * kg_ref[...]).astype(BF16) for j in range(KV)]
        vb = [cat[:, (KV + j) * dh:(KV + j + 1) * dh].astype(BF16) for j in range(KV)]
        qn = [(_rms_rows(_stack_heads(qblk[:, j * GROUP_WIDTH:(j + 1) * GROUP_WIDTH]))[0] * (qg_ref[...] * scale)).astype(BF16)
              for j in range(KV)]
        s = [_nt(qn[j], kn[j]) + bias_ref[j * G:(j + 1) * G].reshape(G * BLOCK, 3 * BLOCK) for j in range(KV)]
        s = [jnp.where(edge_ok, sj, NEG) for sj in s]
        sinks = [_sink_column(sink_ref, j) for j in range(KV)]
        m = [jnp.maximum(jnp.max(s[j], axis=-1, keepdims=True), sinks[j]) for j in range(KV)]
        e = [jnp.exp(s[j] - m[j]) for j in range(KV)]
        den = [jnp.sum(e[j], axis=-1, keepdims=True) + jnp.exp(sinks[j] - m[j]) for j in range(KV)]
        o = [_nn(e[j].astype(BF16), vb[j]) * (1.0 / den[j]) for j in range(KV)]
        o_ref[...] = jnp.concatenate([_unstack_heads(oj) for oj in o], axis=1).astype(o_ref.dtype)

    def kv(shift):
        return pl.BlockSpec((BLOCK, 2 * KV_WIDTH), lambda n: (jnp.clip(n + shift, 0, nb - 1), KV_COL // (2 * KV_WIDTH)))

    gain = pl.BlockSpec((1, dh), lambda n: (0, 0))
    return pl.pallas_call(
        body, name=name, grid=(nb,),
        in_specs=[pl.BlockSpec((BLOCK, ATT_WIDTH), lambda n: (n, Q_COL // ATT_WIDTH)), kv(-1), kv(0), kv(1), gain, gain,
                  pl.BlockSpec((ATT_Q_HEADS, 1, BLOCK), lambda n: (0, 0, 0)),
                  pl.BlockSpec((ATT_Q_HEADS, BLOCK, 3 * BLOCK), lambda n: (0, 0, 0))],
        out_specs=pl.BlockSpec((BLOCK, ATT_WIDTH), lambda n: (n, 0)),
        out_shape=jax.ShapeDtypeStruct((S, ATT_WIDTH), BF16), compiler_params=_params("parallel"),
    )(z, z, z, z, q_g, k_g, sink, bias)


def _attn_bwd(z, q_g, k_g, sink, bias, do, name):
    S = z.shape[0]
    nb = S // BLOCK
    G, dh, KV = ATT_GROUP, ATT_HEAD_DIM, ATT_KV_HEADS
    scale = 1.0 / math.sqrt(dh)
    both = range(KV)
    bias_t = bias.reshape(KV, G, BLOCK, 3 * BLOCK).transpose(0, 3, 1, 2).reshape(KV, 3 * BLOCK, G * BLOCK)

    def body(q_ref, kv0, kv1, kv2, qg_ref, kg_ref, sink_ref, bias_ref, do_ref,
             dq_ref, dkw_ref, dvw_ref, ds_ref, dsink_ref, dqg_ref):
        n = pl.program_id(0)

        @pl.when(n == 0)
        def _():
            ds_ref[...] = jnp.zeros_like(ds_ref)
            dsink_ref[...] = jnp.zeros_like(dsink_ref)
            dqg_ref[...] = jnp.zeros_like(dqg_ref)

        rowid = lax.broadcasted_iota(jnp.int32, (3 * BLOCK, G * BLOCK), 0)
        edge_ok = jnp.logical_and(jnp.logical_or(rowid >= BLOCK, n > 0), jnp.logical_or(rowid < 2 * BLOCK, n < nb - 1))
        qg = qg_ref[...]
        cat = jnp.concatenate([kv0[...], kv1[...], kv2[...]], axis=0)
        qblk, doblk = q_ref[...], do_ref[...]
        kn = [(_rms_rows(cat[:, j * dh:(j + 1) * dh])[0] * kg_ref[...]).astype(BF16) for j in both]
        vb = [cat[:, (KV + j) * dh:(KV + j + 1) * dh].astype(BF16) for j in both]
        norm = [_rms_rows(_stack_heads(qblk[:, j * GROUP_WIDTH:(j + 1) * GROUP_WIDTH])) for j in both]
        qn = [(norm[j][0] * (qg * scale)).astype(BF16) for j in both]
        do_b = [_stack_heads(doblk[:, j * GROUP_WIDTH:(j + 1) * GROUP_WIDTH]).astype(BF16) for j in both]
        s = [_nt(kn[j], qn[j]) + bias_ref[j] for j in both]
        dp = [_nt(vb[j], do_b[j]) for j in both]
        s = [jnp.where(edge_ok, sj, NEG) for sj in s]
        sinks = [jnp.concatenate([sink_ref[j * G + g] for g in range(G)], axis=1) for j in both]
        m = [jnp.maximum(jnp.max(s[j], axis=0, keepdims=True), sinks[j]) for j in both]
        e = [jnp.exp(s[j] - m[j]) for j in both]
        e_sink = [jnp.exp(sinks[j] - m[j]) for j in both]
        inv = [1.0 / (jnp.sum(e[j], axis=0, keepdims=True) + e_sink[j]) for j in both]
        p = [e[j] * inv[j] for j in both]
        delta = [jnp.sum(p[j] * dp[j], axis=0, keepdims=True) for j in both]
        ds = [p[j] * (dp[j] - delta[j]) for j in both]
        ds_b = [dsj.astype(BF16) for dsj in ds]
        dqn = [_tn(kn[j], ds_b[j]).T * scale for j in both]
        for j in both:
            dvw_ref[j] = _nn(p[j].astype(BF16), do_b[j])
            dkw_ref[j] = _nn(ds_b[j], qn[j])
        for j in both:
            ds_ref[j] += ds[j]
            sink_term = e_sink[j] * inv[j] * delta[j]
            for g in range(G):
                dsink_ref[j * G + g] += (jnp.zeros((1, BLOCK), F32)
                                         - jnp.sum(sink_term[:, g * BLOCK:(g + 1) * BLOCK], axis=1, keepdims=True))
        dq = []
        for j in both:
            qhat, rstd = norm[j]
            dqg_ref[j] += jnp.sum(dqn[j] * qhat, axis=0, keepdims=True)
            dqh = dqn[j] * qg
            dq.append(_unstack_heads(rstd * (dqh - qhat * jnp.mean(dqh * qhat, axis=-1, keepdims=True))))
        dq_ref[...] = jnp.concatenate(dq, axis=1).astype(dq_ref.dtype)

    def kv(shift):
        return pl.BlockSpec((BLOCK, 2 * KV_WIDTH), lambda n: (jnp.clip(n + shift, 0, nb - 1), KV_COL // (2 * KV_WIDTH)))

    gain = pl.BlockSpec((1, dh), lambda n: (0, 0))
    sink_spec = pl.BlockSpec((ATT_Q_HEADS, 1, BLOCK), lambda n: (0, 0, 0))
    bias_spec = pl.BlockSpec((KV, 3 * BLOCK, G * BLOCK), lambda n: (0, 0, 0))
    win = pl.BlockSpec((KV, None, 3 * BLOCK, dh), lambda n: (0, n, 0, 0))
    wshape = jax.ShapeDtypeStruct((KV, nb, 3 * BLOCK, dh), F32)
    return pl.pallas_call(
        body, name=name, grid=(nb,),
        in_specs=[pl.BlockSpec((BLOCK, ATT_WIDTH), lambda n: (n, Q_COL // ATT_WIDTH)), kv(-1), kv(0), kv(1), gain, gain,
                  sink_spec, bias_spec, pl.BlockSpec((BLOCK, ATT_WIDTH), lambda n: (n, HG_WIDTH // ATT_WIDTH))],
        out_specs=[pl.BlockSpec((BLOCK, ATT_WIDTH), lambda n: (n, 0)), win, win, bias_spec, sink_spec,
                   pl.BlockSpec((KV, 1, dh), lambda n: (0, 0, 0))],
        out_shape=[jax.ShapeDtypeStruct((S, ATT_WIDTH), BF16), wshape, wshape,
                   jax.ShapeDtypeStruct((KV, 3 * BLOCK, G * BLOCK), F32),
                   jax.ShapeDtypeStruct((ATT_Q_HEADS, 1, BLOCK), F32),
                   jax.ShapeDtypeStruct((KV, 1, dh), F32)],
        compiler_params=_params("arbitrary"),
    )(z, z, z, z, q_g, k_g, sink, bias_t, do)


def _attn_kv_reduce(dkw, dvw, z, k_g, name):
    S = z.shape[0]
    nb = S // BLOCK
    dh = ATT_HEAD_DIM
    kb = min(8, nb)
    steps = nb // kb

    def body(a_lo, a, a_hi, b_lo, b, b_hi, kv_ref, kg_ref, dkv_ref, dkg_ref):
        n = pl.program_id(0)

        @pl.when(n == 0)
        def _():
            dkg_ref[...] = jnp.zeros_like(dkg_ref)

        lo = jnp.where(n > 0, 1.0, 0.0)
        hi = jnp.where(n < steps - 1, 1.0, 0.0)

        def overlap_add(w, w_lo, w_hi, j, i):
            before = lo * w_lo[j] if i == 0 else w[j, i - 1, 2 * BLOCK:3 * BLOCK, :]
            after = hi * w_hi[j] if i == kb - 1 else w[j, i + 1, 0:BLOCK, :]
            return w[j, i, BLOCK:2 * BLOCK, :] + before + after

        dkg = [jnp.zeros((1, dh), F32) for _ in range(ATT_KV_HEADS)]
        for i in range(kb):
            rows = slice(i * BLOCK, (i + 1) * BLOCK)
            dks, dvs = [], []
            for j in range(ATT_KV_HEADS):
                dkn = overlap_add(a, a_lo, a_hi, j, i)
                dvs.append(overlap_add(b, b_lo, b_hi, j, i))
                khat, rstd = _rms_rows(kv_ref[rows, j * dh:(j + 1) * dh])
                dkg[j] = dkg[j] + jnp.sum(dkn * khat, axis=0, keepdims=True)
                dkh = dkn * kg_ref[...]
                dks.append(rstd * (dkh - khat * jnp.mean(dkh * khat, axis=-1, keepdims=True)))
            dkv_ref[rows, :] = jnp.concatenate(dks + dvs, axis=1).astype(dkv_ref.dtype)
        for j in range(ATT_KV_HEADS):
            dkg_ref[j] += dkg[j]

    main = pl.BlockSpec((ATT_KV_HEADS, kb, 3 * BLOCK, dh), lambda n: (0, n, 0, 0))
    halo_lo = pl.BlockSpec((ATT_KV_HEADS, None, BLOCK, dh), lambda n: (0, jnp.maximum(n * kb - 1, 0), 2, 0))
    halo_hi = pl.BlockSpec((ATT_KV_HEADS, None, BLOCK, dh), lambda n: (0, jnp.minimum(n * kb + kb, nb - 1), 0, 0))
    return pl.pallas_call(
        body, name=name, grid=(steps,),
        in_specs=[halo_lo, main, halo_hi, halo_lo, main, halo_hi,
                  pl.BlockSpec((kb * BLOCK, 2 * KV_WIDTH), lambda n: (n, KV_COL // (2 * KV_WIDTH))),
                  pl.BlockSpec((1, dh), lambda n: (0, 0))],
        out_specs=[pl.BlockSpec((kb * BLOCK, 2 * KV_WIDTH), lambda n: (n, 0)),
                   pl.BlockSpec((ATT_KV_HEADS, 1, dh), lambda n: (0, 0, 0))],
        out_shape=[jax.ShapeDtypeStruct((S, 2 * KV_WIDTH), BF16), jax.ShapeDtypeStruct((ATT_KV_HEADS, 1, dh), F32)],
        compiler_params=_params("arbitrary"),
    )(dkw, dkw, dkw, dvw, dvw, dvw, z, k_g)


def _ada_wgrad(c_act_t, dm, name):
    D, nbatch = c_act_t.shape
    n = dm.shape[1]
    tr = 256

    def body(c_ref, dm_ref, o_ref):
        cv, dv = c_ref[...], dm_ref[...]
        acc = cv[:, 0:1] * dv[0:1, :]
        for b in range(1, nbatch):
            acc = acc + cv[:, b:b + 1] * dv[b:b + 1, :]
        o_ref[...] = acc

    return pl.pallas_call(
        body, name=name, grid=(D // tr,),
        in_specs=[pl.BlockSpec((tr, nbatch), lambda i: (i, 0)), pl.BlockSpec((nbatch, n), lambda i: (0, 0))],
        out_specs=pl.BlockSpec((tr, n), lambda i: (i, 0)), out_shape=jax.ShapeDtypeStruct((D, n), F32),
        compiler_params=_params("parallel"),
    )(c_act_t, dm)


def _to_bf16(w, name):
    R, Cn = w.shape
    tr = _row_tile(R)

    def body(w_ref, o_ref):
        o_ref[...] = w_ref[...].astype(BF16)

    blk = pl.BlockSpec((tr, Cn), lambda i: (i, 0))
    return pl.pallas_call(
        body, name=name, grid=(R // tr,), in_specs=[blk], out_specs=blk, out_shape=jax.ShapeDtypeStruct((R, Cn), BF16),
        compiler_params=_params("parallel"),
    )(w)


def _adamw(w, g, m, v, name):
    R, Cn = w.shape
    tr = R
    for cand in (256, 128, 64, 32, 16, 8):
        if R % cand == 0:
            tr = cand
            break

    def body(w_ref, g_ref, m_ref, v_ref, d_ref, nm_ref, nv_ref):
        gv = g_ref[...]
        m_new = ADAM_B1 * m_ref[...] + (1.0 - ADAM_B1) * gv
        v_new = ADAM_B2 * v_ref[...] + (1.0 - ADAM_B2) * (gv * gv)
        m_hat = m_new / (1.0 - ADAM_B1 ** ADAM_STEP)
        v_hat = v_new / (1.0 - ADAM_B2 ** ADAM_STEP)
        d_ref[...] = -ADAM_LR * (m_hat / (jnp.sqrt(v_hat) + ADAM_EPS) + ADAM_WD * w_ref[...])
        nm_ref[...] = m_new
        nv_ref[...] = v_new

    blk = pl.BlockSpec((tr, Cn), lambda i: (i, 0))
    shp = jax.ShapeDtypeStruct((R, Cn), F32)
    return pl.pallas_call(
        body, name=name, grid=(R // tr,), in_specs=[blk] * 4, out_specs=[blk] * 3, out_shape=[shp] * 3,
        compiler_params=_params("parallel"),
    )(w, g, m, v)


def _place():
    return lax.axis_index("x"), lax.axis_index("y"), lax.axis_index("c")


def _flip(place, k):
    x, y, c = place
    return (1 - x if k & 4 else x, 1 - y if k & 2 else y, 1 - c if k & 1 else c)


def _dev_index(place):
    x, y, c = place
    return 4 * x + 2 * y + c


def _chip_index(place):
    return 2 * place[0] + place[1]


def _gather8(x_ref, out_ref, send_sems, recv_sems, local_sem):
    me = _place()
    mine = pltpu.make_async_copy(x_ref, out_ref.at[_dev_index(me)], local_sem)
    mine.start()

    def copy(k, origin, to):
        return pltpu.make_async_remote_copy(
            src_ref=x_ref, dst_ref=out_ref.at[_dev_index(origin)], send_sem=send_sems.at[k - 1],
            recv_sem=recv_sems.at[k - 1], device_id=to, device_id_type=MESH)

    sends = [copy(k, me, _flip(me, k)) for k in range(1, 8)]
    for cp in sends:
        cp.start()
    for k in range(1, 8):
        copy(k, _flip(me, k), me).wait_recv()
    for cp in sends:
        cp.wait_send()
    mine.wait()


def _allgather8(x, name, reduce=False):
    R, Cn = x.shape

    def body(x_ref, *rest):
        if reduce:
            out_ref, sum_ref, send_sems, recv_sems, local_sem = rest
        else:
            out_ref, send_sems, recv_sems, local_sem = rest
        _gather8(x_ref, out_ref, send_sems, recv_sems, local_sem)
        if reduce:
            acc = out_ref[0]
            for i in range(1, 8):
                acc = acc + out_ref[i]
            sum_ref[...] = acc

    vm = pl.BlockSpec(memory_space=pltpu.VMEM)
    outs = [jax.ShapeDtypeStruct((8, R, Cn), F32)] + ([jax.ShapeDtypeStruct((R, Cn), F32)] if reduce else [])
    res = pl.pallas_call(
        body, name=name, in_specs=[vm], out_specs=[vm] * len(outs), out_shape=outs,
        scratch_shapes=[pltpu.SemaphoreType.DMA((7,)), pltpu.SemaphoreType.DMA((7,)), pltpu.SemaphoreType.DMA],
    )(x)
    return res if reduce else res[0]


def _prologue(small, w_ada, b_ada, w_shard, name):
    R, Cn = small.shape
    n_mod = w_ada.shape[1]
    big = _gather_in_one([w_shard])

    def body(small_ref, wada_ref, b_ref, shard_ref, small_all_ref, mods_all_ref, gathered_ref, mods_ref,
             send1, recv1, send2, recv2, local_sems, big_send, big_recv):
        big.start([shard_ref], [gathered_ref], big_send, big_recv)
        _gather8(small_ref, small_all_ref, send1, recv1, local_sems.at[0])
        c_all = jnp.concatenate([small_all_ref[d, 0:1, :] for d in range(8)], axis=0)
        c_act = c_all * _sigmoid(c_all)
        mods_ref[...] = _nn(c_act, wada_ref[...], precision=lax.Precision.HIGHEST) + b_ref[...]
        _gather8(mods_ref, mods_all_ref, send2, recv2, local_sems.at[1])
        big.switch([shard_ref], [gathered_ref], big_send, big_recv)
        big.finish([shard_ref], [gathered_ref], big_send, big_recv)

    vm = pl.BlockSpec(memory_space=pltpu.VMEM)
    seven = pltpu.SemaphoreType.DMA((7,))
    return pl.pallas_call(
        body, name=name, in_specs=[vm, vm, vm, ANY], out_specs=[vm, vm, ANY],
        out_shape=[jax.ShapeDtypeStruct((8, R, Cn), F32), jax.ShapeDtypeStruct((8, 8, n_mod), F32)] + big.out_shapes,
        scratch_shapes=[pltpu.VMEM((8, n_mod), F32), seven, seven, seven, seven, pltpu.SemaphoreType.DMA((2,))]
        + big.sem_shapes(),
        compiler_params=pltpu.CompilerParams(vmem_limit_bytes=VMEM_LIMIT_BYTES),
    )(small, w_ada, b_ada, w_shard)


def _remote(src, dst, send_sems, recv_sems, i, to):
    return pltpu.make_async_remote_copy(
        src_ref=src, dst_ref=dst, send_sem=send_sems.at[i], recv_sem=recv_sems.at[i], device_id=to, device_id_type=MESH)


def _symmetric_plan(copies):
    def plan(in_refs, out_refs, send_sems, recv_sems):
        sends = [_remote(src, dst, send_sems, recv_sems, i, to) for i, (src, dst, to) in enumerate(copies(in_refs, out_refs))]
        return sends, sends
    return plan


def _halves_exchange(grads):
    def copies(in_refs, out_refs):
        me = _place()
        return [(g.at[kk, 1 - me[2]], got.at[kk], _flip(me, 1)) for g, got in zip(in_refs, out_refs) for kk in range(N_CHIPS)]

    return _Exchange(grads, [jax.ShapeDtypeStruct((N_CHIPS,) + g.shape[2:], g.dtype) for g in grads],
                     N_CHIPS * len(grads), _symmetric_plan(copies))


def _chips_exchange(parts):
    def copies(in_refs, out_refs):
        me = _place()
        return [(p.at[_chip_index(_flip(me, 2 * j))], got.at[j - 1], _flip(me, 2 * j))
                for p, got in zip(in_refs, out_refs) for j in (1, 2, 3)]

    return _Exchange(parts, [jax.ShapeDtypeStruct((3,) + p.shape[1:], p.dtype) for p in parts], 3 * len(parts),
                     _symmetric_plan(copies))


def _siblings_exchange(halves):
    def copies(in_refs, out_refs):
        sibling = _flip(_place(), 1)
        return [(h, got, sibling) for h, got in zip(in_refs, out_refs)]

    return _Exchange(halves, [jax.ShapeDtypeStruct(h.shape, h.dtype) for h in halves], len(halves), _symmetric_plan(copies))


def _ici_gather_plan(n, base=0):
    def plan(in_refs, out_refs, send_sems, recv_sems):
        me = _place()
        c = me[2]
        sends, recvs = [], []
        for a, (w, out) in enumerate(zip(in_refs[:n], out_refs)):
            for j in (1, 2, 3):
                i = base + 3 * a + j - 1
                sends.append(_remote(w.at[c], out.at[_chip_index(me), c], send_sems, recv_sems, i, _flip(me, 2 * j)))
                z = out.at[_chip_index(_flip(me, 2 * j)), c]
                recvs.append(_remote(z, z, send_sems, recv_sems, i, me))
        return sends, recvs
    return plan


def _d2d_gather_plan(n, base=0):
    def plan(in_refs, out_refs, send_sems, recv_sems):
        me = _place()
        c = me[2]
        sibling = _flip(me, 1)
        mine = _chip_index(me)
        sends, recvs = [], []
        for a, (w, out) in enumerate(zip(in_refs[:n], out_refs)):
            moves = [(w.at[c], (mine, c)), (w.at[1 - c], (mine, 1 - c))]
            moves += [(out.at[_chip_index(_flip(me, 2 * j)), c], (_chip_index(_flip(me, 2 * j)), c)) for j in (1, 2, 3)]
            for k, (src, (chip, half)) in enumerate(moves):
                sends.append(_remote(src, out.at[chip, half], send_sems, recv_sems, base + 5 * a + k, sibling))
            lands = [(mine, 1 - c), (mine, c)] + [(_chip_index(_flip(me, 2 * j)), 1 - c) for j in (1, 2, 3)]
            for k, (chip, half) in enumerate(lands):
                z = out.at[chip, half]
                recvs.append(_remote(z, z, send_sems, recv_sems, base + 5 * a + k, me))
        return sends, recvs
    return plan


def _gathered_shapes(shards):
    return [jax.ShapeDtypeStruct((N_CHIPS,) + s.shape, s.dtype) for s in shards]


def _gather_over_ici(shards):
    return _Exchange(shards, _gathered_shapes(shards), 3 * len(shards), _ici_gather_plan(len(shards)))


def _gather_over_d2d(shards, gathered):
    n = len(shards)
    return _Exchange(list(shards) + list(gathered), [jax.ShapeDtypeStruct(g.shape, g.dtype) for g in gathered], 5 * n,
                     _d2d_gather_plan(n), aliases={n + a: a for a in range(n)})


def _gather_in_one(shards):
    n = len(shards)
    return _Exchange(shards, _gathered_shapes(shards), 8 * n, _ici_gather_plan(n), then=_d2d_gather_plan(n, base=3 * n))


def _row_tile(rows):
    for cand in (256, 176, 128, 64, 32, 16, 8):
        if rows % cand == 0:
            return cand
    return rows


def _pair_sum(core, grad, theirs, name):
    N, _, R, Cn = grad.shape
    tr = R

    def body(core_ref, g_ref, t_ref, o_ref, ob_ref):
        s = g_ref[...] + t_ref[...]
        o_ref[...] = s
        ob_ref[...] = s.astype(BF16)

    out = pl.BlockSpec((None, tr, Cn), lambda k, i, core_ref: (k, i, 0))
    return pl.pallas_call(
        body, name=name,
        grid_spec=pltpu.PrefetchScalarGridSpec(
            num_scalar_prefetch=1, grid=(N, R // tr),
            in_specs=[pl.BlockSpec((None, None, tr, Cn), lambda k, i, core_ref: (k, core_ref[0], i, 0)),
                      pl.BlockSpec((None, tr, Cn), lambda k, i, core_ref: (k, i, 0))],
            out_specs=[out, out]),
        out_shape=[jax.ShapeDtypeStruct((N, R, Cn), F32), jax.ShapeDtypeStruct((N, R, Cn), BF16)],
        compiler_params=_params("parallel", "parallel"),
    )(core, grad, theirs)


def _chip_sum(chip, parts, landed, name):
    _, R, Cn = parts.shape
    tr = R

    def body(chip_ref, p_ref, l_ref, o_ref):
        o_ref[...] = ((p_ref[...] + l_ref[0].astype(F32)) + l_ref[1].astype(F32)) + l_ref[2].astype(F32)

    return pl.pallas_call(
        body, name=name,
        grid_spec=pltpu.PrefetchScalarGridSpec(
            num_scalar_prefetch=1, grid=(R // tr,),
            in_specs=[pl.BlockSpec((None, tr, Cn), lambda i, chip_ref: (chip_ref[0], i, 0)),
                      pl.BlockSpec((3, tr, Cn), lambda i, chip_ref: (0, i, 0))],
            out_specs=pl.BlockSpec((tr, Cn), lambda i, chip_ref: (i, 0))),
        out_shape=jax.ShapeDtypeStruct((R, Cn), F32), compiler_params=_params("parallel"),
    )(chip, parts, landed)


def _pair_sums(core, grads, theirs, tag):
    return [_pair_sum(core, g, t, f"{tag}_pair_sum_{i}") for i, (g, t) in enumerate(zip(grads, theirs))]


def _chip_sums(chip, parts, landed, tag):
    return [_chip_sum(chip, p[0], l, f"{tag}_chip_sum_{i}") for i, (p, l) in enumerate(zip(parts, landed))]


def _by_chip_rows(g):
    return g.reshape(N_CHIPS, 2, g.shape[0] // (2 * N_CHIPS), g.shape[1])


def _by_chip_cols(g):
    return g.reshape(N_CHIPS, 2, g.shape[1] // 2, g.shape[2])


def _adamw_halves(core, w, g_mine, g_theirs, m, v, name):
    R2, Cn = w.shape
    r = R2 // 2
    tr = _row_tile(r)
    nt = r // tr

    def body(core_ref, w_ref, gm_ref, gt_ref, m_ref, v_ref, g_ref, d_ref, nm_ref, nv_ref):
        gv = jnp.where(pl.program_id(0) == core_ref[0], gm_ref[...], gt_ref[...])
        g_ref[...] = gv
        m_new = ADAM_B1 * m_ref[...] + (1.0 - ADAM_B1) * gv
        v_new = ADAM_B2 * v_ref[...] + (1.0 - ADAM_B2) * (gv * gv)
        m_hat = m_new / (1.0 - ADAM_B1 ** ADAM_STEP)
        v_hat = v_new / (1.0 - ADAM_B2 ** ADAM_STEP)
        d_ref[...] = -ADAM_LR * (m_hat / (jnp.sqrt(v_hat) + ADAM_EPS) + ADAM_WD * w_ref[...])
        nm_ref[...] = m_new
        nv_ref[...] = v_new

    full = pl.BlockSpec((tr, Cn), lambda hf, i, core_ref: (hf * nt + i, 0))
    half = pl.BlockSpec((tr, Cn), lambda hf, i, core_ref: (i, 0))
    shp = jax.ShapeDtypeStruct((R2, Cn), F32)
    return pl.pallas_call(
        body, name=name,
        grid_spec=pltpu.PrefetchScalarGridSpec(
            num_scalar_prefetch=1, grid=(2, nt), in_specs=[full, half, half, full, full], out_specs=[full] * 4),
        out_shape=[shp] * 4, compiler_params=_params("parallel", "parallel"),
    )(core, w, g_mine, g_theirs, m, v)


def _pad_row(v, width):
    v = v.reshape(1, -1)
    return jnp.pad(v, ((0, 0), (0, width - v.shape[1])))


def _ffn1_forward(x, ng, shift, scale, gate, w_in4, w_out_shard, gather, next_norm):
    h = _rmsmod_fwd(x, ng, shift, scale, "ffn1_norm")
    (zg, zu, a), (partly, (w_out4,)) = _ffn_in_fwd(
        h, w_in4, "ffn1_in", exchange=[_gather_over_ici(gather), _gather_in_one([w_out_shard])])
    w_out = w_out4.reshape(D_FF, D_MODEL)
    (x_new, f, h_next), gathered = _proj_out_fwd([a], w_out, x, gate, 0.5, "ffn1_out", next_norm=next_norm,
                                                 exchange=_gather_over_d2d(gather, partly))
    return x_new, (h, zg, zu, a, f), w_out, gathered, h_next


def _ffn_backward(df, saved, w_in4, w_out, core, chip, tag, riding=None, norm=None):
    h, zg, zu, a = saved[:4]
    rode = None
    if riding:
        (dzg, dzu), rode = _dact_bwd(df, w_out, zg, zu, f"{tag}_dact", exchange=riding)
    else:
        dzg, dzu = _dact_bwd(df, w_out, zg, zu, f"{tag}_dact")
    g_out = [_by_chip_rows(_wgrad(a, [df], df.shape[1], f"{tag}_dw_out")[0].reshape(a.shape[1], df.shape[1]))]
    (dw_in,), theirs_out = _wgrad(h, [dzg, dzu], FF_SHARD, f"{tag}_dw_in", exchange=_halves_exchange(g_out))
    g_in = [_by_chip_cols(dw_in.reshape(N_CHIPS, h.shape[1], FF_SHARD))]
    parts_out = _pair_sums(core, g_out, theirs_out, f"{tag}_out")
    dh_outs, (theirs_in, landed_out) = _ffn_in_dgrad(
        dzg, dzu, w_in4, f"{tag}_dh", norm=norm, exchange=[_halves_exchange(g_in), _chips_exchange([parts_out[0][1]])])
    parts_in = _pair_sums(core, g_in, theirs_in, f"{tag}_in")
    return dh_outs, parts_in, _chip_sums(chip, parts_out, landed_out, f"{tag}_out"), rode


def kernel(x, c, w_ada, b_ada, norm_g, w_ffn1_in, w_ffn1_out, w_ffn2_in, w_ffn2_out, w_mix_in, w_mix_out, hgrn_lb, hgrn_norm_g, qk_norm_g, attn_sink, rel_bias, loss_target, m_w_ada, m_b_ada, m_norm_g, m_w_ffn1_in, m_w_ffn1_out, m_w_ffn2_in, m_w_ffn2_out, m_w_mix_in, m_w_mix_out, m_hgrn_lb, m_hgrn_norm_g, m_qk_norm_g, m_attn_sink, m_rel_bias, v_w_ada, v_b_ada, v_norm_g, v_w_ffn1_in, v_w_ffn1_out, v_w_ffn2_in, v_w_ffn2_out, v_w_mix_in, v_w_mix_out, v_hgrn_lb, v_hgrn_norm_g, v_qk_norm_g, v_attn_sink, v_rel_bias):
    D = D_MODEL
    S = x.shape[1]
    place = (lax.axis_index("x"), lax.axis_index("y"), lax.axis_index("c"))
    me, my_chip = _dev_index(place), _chip_index(place)
    x0 = x[0]
    target = loss_target[0]

    def halves(w, tag):
        return _to_bf16(w[0], f"{tag}_to_bf16").reshape(2, w.shape[1] // 2, w.shape[2])

    w1_out_shard = halves(w_ffn1_out, "w_ffn1_out")
    mix_shards = [halves(w_mix_in, "w_mix_in"), halves(w_mix_out, "w_mix_out")]
    ffn2_shards = [halves(w_ffn2_in, "w_ffn2_in"), halves(w_ffn2_out, "w_ffn2_out")]
    core_arr = jnp.reshape(place[2], (1,)).astype(jnp.int32)
    chip_arr = jnp.reshape(my_chip, (1,)).astype(jnp.int32)

    small = jnp.concatenate([_pad_row(c, D), _pad_row(norm_g, D), _pad_row(hgrn_lb, D), jnp.zeros((5, D), F32)], axis=0)
    n_ada = w_ada.shape[2]
    b_mine = lax.dynamic_slice_in_dim(b_ada, my_chip * n_ada, n_ada, axis=1)
    small_all, mods_parts, w1_in4 = _prologue(small, w_ada[0], b_mine, halves(w_ffn1_in, "w_ffn1_in"), "prologue")
    w1_in = w1_in4.reshape(N_CHIPS, D, FF_SHARD)
    c_all = small_all[:, 0, :]
    by_chip = small_all[0::2]
    norm_g_full = by_chip[:, 1, :3 * 256].reshape(N_CHIPS, 3, 256).transpose(1, 0, 2).reshape(3, D)
    lb_raw = by_chip[:, 2, :2 * 2 * 128].reshape(N_CHIPS, 2, 2, 128).transpose(1, 2, 0, 3).reshape(2, 2, HG_WIDTH)
    lb = jax.nn.sigmoid(lb_raw[:, 0, :] - lb_raw[:, 1, :])
    lb_f, lb_b = lb[0:1], lb[1:2]

    c_act_all = c_all * jax.nn.sigmoid(c_all)
    mods_all = mods_parts[0::2].transpose(1, 0, 2).reshape(8, N_MOD * D)
    mods = lax.dynamic_slice_in_dim(mods_all, me, 1, axis=0)
    sh1, sc1, g1, sh2, sc2, g2, sh3, sc3, g3 = [mods[:, i * D:(i + 1) * D] for i in range(N_MOD)]

    x1, saved1, w1_out, gathered, h2 = _ffn1_forward(x0, norm_g_full[0:1], sh1, sc1, g1, w1_in, w1_out_shard, mix_shards,
                                                     (norm_g_full[1:2], sh2, sc2))
    wm_in = gathered[0].reshape(N_CHIPS, D, D_IN // N_CHIPS).transpose(1, 0, 2).reshape(D, D_IN)
    wm_out = gathered[1].reshape(D, D)

    z = _matmul_nn(h2, wm_in, F32, 256, "mix_in")
    (of, st_f), partly = _hgrn_fwd(z, lb_f, 0, "hgrn_fwd_f", exchange=_gather_over_ici(ffn2_shards))
    (ob, st_b), gathered = _hgrn_fwd(z, lb_b, 1, "hgrn_fwd_b", exchange=_gather_over_d2d(ffn2_shards, partly))
    w2_in = gathered[0].reshape(N_CHIPS, D, FF_SHARD)
    w2_out = gathered[1].reshape(D_FF, D)
    o_h = _hgrn_post_fwd(of, ob, z, hgrn_norm_g, "hgrn_post")

    q_g, k_g = qk_norm_g[0, 0:1], qk_norm_g[0, 1:2]
    sink_b = jnp.broadcast_to(attn_sink.reshape(ATT_Q_HEADS, 1, 1), (ATT_Q_HEADS, 1, BLOCK))
    bias = _bias_table(rel_bias, "bias_table")
    o_a = _attn_fwd(z, q_g, k_g, sink_b, bias, "attn_fwd")
    x2, mixed, h3 = _proj_out_fwd([o_h, o_a], wm_out, x1, g2, 1.0, "mix_out", next_norm=(norm_g_full[2:3], sh3, sc3))

    zg3, zu3, a3 = _ffn_in_fwd(h3, w2_in, "ffn2_in")
    dx3, df3, dg3, sq_cols = _proj_out_loss(a3, w2_out, x2, g3, 0.5, target, "ffn2_out_loss")
    loss_mine = 0.5 * jnp.sum(sq_cols) / D

    (dx2, dsh3, dsc3, dng3, dmixed, dg2), parts2, mine2_out, _ = _ffn_backward(
        df3, (h3, zg3, zu3, a3), w2_in, w2_out, core_arr, chip_arr, "ffn2",
        norm=_NormBwd(x2, norm_g_full[2:3], sc3, dx3, below=(mixed, g2, 1.0)))

    (do_cat,) = _matmul_nt([dmixed], wm_out, ROW_TILE, "mix_out_dgrad")
    dwm_out = _wgrad_rows([o_h, o_a], dmixed, "mix_out_dw").reshape(D, D)

    do_sum, dgr, d_hnorm = _hgrn_post_bwd(do_cat, of, ob, z, hgrn_norm_g, "hgrn_post_bwd")
    (dq_f, dff, dv_f, doml_f), landed2 = _hgrn_bwd(z, lb_f, do_sum, st_f, 0, "hgrn_bwd_f",
                                                   exchange=_chips_exchange([p[1] for p in parts2]))
    mine2 = _chip_sums(chip_arr, parts2, landed2, "ffn2_in") + mine2_out
    (dhq, dfb, dhi, doml_b), theirs2 = _hgrn_bwd(z, lb_b, do_sum, st_b, 1, "hgrn_bwd_b", acc=(dq_f, dv_f),
                                                 exchange=_siblings_exchange(mine2))

    daq, dkw, dvw, ds_sum, dsink, dqg = _attn_bwd(z, q_g, k_g, sink_b, bias, do_cat, "attn_bwd")
    dkv, dkg = _attn_kv_reduce(dkw, dvw, z, k_g, "attn_kv_reduce")
    d_rel_bias = jnp.sum(_bias_grad(ds_sum, "bias_grad"), axis=-1).T
    dz = [dhq, dff, dfb, dhi, dgr, daq, dkv]
    dwm_in = _wgrad_pieces(h2, dz, 2 * KV_WIDTH, "mix_in_dw").transpose(1, 0, 2).reshape(D, D_IN)
    wide = D_IN // N_CHIPS
    grads_m = [_by_chip_cols(dwm_in.reshape(D, N_CHIPS, wide).transpose(1, 0, 2)), _by_chip_rows(dwm_out)]
    (dx1, dsh2, dsc2, dng2, df1, dg1), theirs_m = _matmul_nt(
        dz, wm_in, 256, "mix_in_dgrad", exchange=_halves_exchange(grads_m),
        norm=_NormBwd(x1, norm_g_full[1:2], sc2, dx2, below=(saved1[4], g1, 0.5)))
    parts_m = _pair_sums(core_arr, grads_m, theirs_m, "mix")

    (dh1,), parts1, mine1_out, landed_m = _ffn_backward(df1, saved1, w1_in, w1_out, core_arr, chip_arr, "ffn1",
                                                        riding=_chips_exchange([p[1] for p in parts_m]))
    mine_m = _chip_sums(chip_arr, parts_m, landed_m, "mix")
    (dx0, dsh1, dsc1, dng1), landed1 = _rmsmod_bwd(dh1, _NormBwd(x0, norm_g_full[0:1], sc1, dx1), "ffn1_norm_bwd",
                                                   exchange=_chips_exchange([p[1] for p in parts1]))
    mine1 = _chip_sums(chip_arr, parts1, landed1, "ffn1_in") + mine1_out
    theirs_1m = list(_run_exchange(_siblings_exchange(mine1 + mine_m), "siblings_exchange"))
    reduced = list(zip(mine1 + mine2 + mine_m, theirs_1m[:2] + list(theirs2) + theirs_1m[2:]))

    dlb = -jnp.concatenate([doml_f, doml_b], axis=0)
    dlb_raw = dlb * lb * (1.0 - lb)
    d_hgrn_lb = jnp.stack([dlb_raw, -dlb_raw], axis=1)
    d_qk = jnp.concatenate([jnp.sum(dqg, axis=0), jnp.sum(dkg, axis=0)], axis=0)
    dmods = jnp.concatenate([dsh1, dsc1, dg1, dsh2, dsc2, dg2, dsh3, dsc3, dg3], axis=0)
    packed = jnp.concatenate(
        [dmods, dng1, dng2, dng3, d_hgrn_lb.reshape(2, D), _pad_row(d_hnorm, D), _pad_row(d_qk, D),
         _pad_row(dsink[:, 0, 0], D), _pad_row(d_rel_bias, D), _pad_row(loss_mine, D)], axis=0)
    packed = jnp.pad(packed, ((0, 24 - packed.shape[0]), (0, 0)))
    packed_all, packed_sum = _allgather8(packed, "small_grads_allgather", reduce=True)
    dmods_all = packed_all[:, 0:N_MOD, :].reshape(8, N_MOD * D)
    g_b_ada = packed_sum[0:N_MOD].reshape(1, N_MOD * D)
    g_norm_full = packed_sum[9:12]
    g_norm_g = lax.dynamic_slice_in_dim(g_norm_full, my_chip * 256, 256, axis=1).reshape(1, 3, 256)
    g_hgrn_lb = lax.dynamic_slice_in_dim(packed_sum[12:14].reshape(2, 2, HG_WIDTH), my_chip * 128, 128, axis=2)
    g_hgrn_norm_g = packed_sum[14:15, :HG_WIDTH]
    g_qk_norm_g = packed_sum[15, :2 * ATT_HEAD_DIM].reshape(1, 2, ATT_HEAD_DIM)
    g_attn_sink = packed_sum[16:17, :ATT_Q_HEADS]
    g_rel_bias = packed_sum[17, :NUM_BUCKETS * ATT_Q_HEADS].reshape(NUM_BUCKETS, ATT_Q_HEADS)
    loss = packed_sum[18, 0]

    dm_mine = lax.dynamic_slice_in_dim(dmods_all, my_chip * n_ada, n_ada, axis=1)
    g_w_ada = _ada_wgrad(c_act_all.T, dm_mine, "ada_wgrad")[None]

    def big(w, g, m, v, name):
        d, nm, nv = _adamw(w[0], g[0], m[0], v[0], name)
        return d[None], nm[None], nv[None]

    def big_halves(w, g_pair, m, v, name):
        g, d, nm, nv = _adamw_halves(core_arr, w[0], g_pair[0], g_pair[1], m[0], v[0], name)
        return g[None], (d[None], nm[None], nv[None])

    g_w1_in, u_w1_in = big_halves(w_ffn1_in, reduced[0], m_w_ffn1_in, v_w_ffn1_in, "adamw_w_ffn1_in")
    g_w1_out, u_w1_out = big_halves(w_ffn1_out, reduced[1], m_w_ffn1_out, v_w_ffn1_out, "adamw_w_ffn1_out")
    g_w2_in, u_w2_in = big_halves(w_ffn2_in, reduced[2], m_w_ffn2_in, v_w_ffn2_in, "adamw_w_ffn2_in")
    g_w2_out, u_w2_out = big_halves(w_ffn2_out, reduced[3], m_w_ffn2_out, v_w_ffn2_out, "adamw_w_ffn2_out")
    g_wm_in, u_wm_in = big_halves(w_mix_in, reduced[4], m_w_mix_in, v_w_mix_in, "adamw_w_mix_in")
    g_wm_out, u_wm_out = big_halves(w_mix_out, reduced[5], m_w_mix_out, v_w_mix_out, "adamw_w_mix_out")

    smalls = [(b_ada, g_b_ada, m_b_ada, v_b_ada), (norm_g, g_norm_g, m_norm_g, v_norm_g), (hgrn_lb, g_hgrn_lb, m_hgrn_lb, v_hgrn_lb),
              (hgrn_norm_g, g_hgrn_norm_g, m_hgrn_norm_g, v_hgrn_norm_g), (qk_norm_g, g_qk_norm_g, m_qk_norm_g, v_qk_norm_g),
              (attn_sink, g_attn_sink, m_attn_sink, v_attn_sink), (rel_bias, g_rel_bias, m_rel_bias, v_rel_bias)]
    sizes = [t[0].size for t in smalls]
    total = sum(sizes)
    rows = -(-total // 128)
    rows = -(-rows // 8) * 8

    def pack(i):
        flat = jnp.concatenate([t[i].reshape(-1) for t in smalls])
        fill = 1.0 if i == 3 else 0.0
        return jnp.pad(flat, (0, rows * 128 - total), constant_values=fill).reshape(rows, 128)

    packed_out = _adamw(pack(0), pack(1), pack(2), pack(3), "adamw_small")

    def unpack(flat2d):
        flat = flat2d.reshape(-1)
        outs, off = [], 0
        for t, n in zip(smalls, sizes):
            outs.append(flat[off:off + n].reshape(t[0].shape))
            off += n
        return outs

    d_small, m_small, v_small = [unpack(t) for t in packed_out]

    upd = {
        "w_ada": big(w_ada, g_w_ada, m_w_ada, v_w_ada, "adamw_w_ada"),
        "w_ffn1_in": u_w1_in, "w_ffn1_out": u_w1_out, "w_ffn2_in": u_w2_in, "w_ffn2_out": u_w2_out,
        "w_mix_in": u_wm_in, "w_mix_out": u_wm_out,
    }
    small_names = ["b_ada", "norm_g", "hgrn_lb", "hgrn_norm_g", "qk_norm_g", "attn_sink", "rel_bias"]
    for i, nme in enumerate(small_names):
        upd[nme] = (d_small[i], m_small[i], v_small[i])
    grads = {
        "w_ada": g_w_ada, "b_ada": g_b_ada, "norm_g": g_norm_g, "w_ffn1_in": g_w1_in, "w_ffn1_out": g_w1_out,
        "w_ffn2_in": g_w2_in, "w_ffn2_out": g_w2_out, "w_mix_in": g_wm_in, "w_mix_out": g_wm_out, "hgrn_lb": g_hgrn_lb,
        "hgrn_norm_g": g_hgrn_norm_g, "qk_norm_g": g_qk_norm_g, "attn_sink": g_attn_sink, "rel_bias": g_rel_bias,
    }
    order = ["w_ada", "b_ada", "norm_g", "w_ffn1_in", "w_ffn1_out", "w_ffn2_in", "w_ffn2_out", "w_mix_in", "w_mix_out",
             "hgrn_lb", "hgrn_norm_g", "qk_norm_g", "attn_sink", "rel_bias"]
    return (loss, dx0[None], *[grads[k] for k in order], *[upd[k][0] for k in order], *[upd[k][1] for k in order],
            *[upd[k][2] for k in order])
```

```python
import functools
import math

import numpy as np
import jax
import jax.numpy as jnp
from jax import lax
from jax.experimental import pallas as pl
from jax.experimental.pallas import tpu as pltpu

F32, BF16 = jnp.float32, jnp.bfloat16

D_MODEL = 1024
D_FF = 2816
HG_HEADS, HG_DIM = 4, 128
HG_WIDTH = HG_HEADS * HG_DIM
ATT_Q_HEADS, ATT_KV_HEADS, ATT_HEAD_DIM = 8, 2, 64
ATT_GROUP = ATT_Q_HEADS // ATT_KV_HEADS
ATT_WIDTH = ATT_Q_HEADS * ATT_HEAD_DIM
KV_WIDTH = ATT_KV_HEADS * ATT_HEAD_DIM
WINDOW, BLOCK = 128, 128
NUM_BUCKETS, MAX_DISTANCE = 32, 128
N_MOD = 9
EPS = 1e-6
D_IN = 5 * HG_WIDTH + ATT_WIDTH + 2 * KV_WIDTH
ADAM_LR, ADAM_B1, ADAM_B2, ADAM_EPS, ADAM_WD, ADAM_STEP = 0.001, 0.9, 0.999, 1e-08, 0.01, 10

N_CHIPS = 4
FF_SHARD = 2 * D_FF // N_CHIPS
NEG = -1e30

VMEM_LIMIT_BYTES = 56 << 20
ROW_TILE = 512
HG_CHUNK = 16
HG_ROWS = 512

MESH = pl.DeviceIdType.MESH
ANY = pl.BlockSpec(memory_space=pl.ANY)


def _params(*sem):
    return pltpu.CompilerParams(dimension_semantics=sem, vmem_limit_bytes=VMEM_LIMIT_BYTES)


def _resident(shape, index_map):
    return pl.BlockSpec(shape, index_map, pipeline_mode=pl.Buffered(1))


def _dot(a, b, dims, precision=None):
    return lax.dot_general(a, b, (dims, ((), ())), precision=precision, preferred_element_type=F32)


def _nn(a, b, precision=None):
    return _dot(a, b, ((1,), (0,)), precision)


def _nt(a, b):
    return _dot(a, b, ((1,), (1,)))


def _tn(a, b):
    return _dot(a, b, ((0,), (0,)))


def _sigmoid(x):
    return jax.nn.sigmoid(x)


class _Exchange:
    def __init__(self, inputs, out_shapes, n_sems, plan, aliases=None, then=None):
        self.inputs, self.out_shapes, self.n_sems, self.plan, self.aliases = list(inputs), list(out_shapes), n_sems, plan, aliases or {}
        self.then = then

    def sem_shapes(self):
        return [pltpu.SemaphoreType.DMA((self.n_sems,)), pltpu.SemaphoreType.DMA((self.n_sems,))]

    def start(self, in_refs, out_refs, send_sems, recv_sems):
        for cp in self.plan(in_refs, out_refs, send_sems, recv_sems)[0]:
            cp.start()

    @staticmethod
    def _wait(sends, recvs):
        for cp in recvs:
            cp.wait_recv()
        for cp in sends:
            cp.wait_send()

    def switch(self, in_refs, out_refs, send_sems, recv_sems):
        if self.then:
            self._wait(*self.plan(in_refs, out_refs, send_sems, recv_sems))
            for cp in self.then(in_refs, out_refs, send_sems, recv_sems)[0]:
                cp.start()

    def finish(self, in_refs, out_refs, send_sems, recv_sems):
        self._wait(*(self.then or self.plan)(in_refs, out_refs, send_sems, recv_sems))


def _run_exchange(ex, name):
    n_in, n_out = len(ex.inputs), len(ex.out_shapes)

    def body(*refs):
        in_refs, out_refs, (send_sems, recv_sems) = refs[:n_in], refs[n_in:n_in + n_out], refs[n_in + n_out:]
        ex.start(in_refs, out_refs, send_sems, recv_sems)
        ex.switch(in_refs, out_refs, send_sems, recv_sems)
        ex.finish(in_refs, out_refs, send_sems, recv_sems)

    return pl.pallas_call(
        body, name=name, in_specs=[ANY] * n_in, out_specs=[ANY] * n_out, out_shape=ex.out_shapes,
        scratch_shapes=ex.sem_shapes(), input_output_aliases=dict(ex.aliases),
    )(*ex.inputs)


def _call(body, *, name, grid, in_specs, out_specs, out_shape, args, semantics, scratch_shapes=(), exchange=None):
    if exchange is None:
        return pl.pallas_call(
            body, name=name, grid=grid, in_specs=in_specs, out_specs=out_specs, out_shape=out_shape,
            scratch_shapes=list(scratch_shapes), compiler_params=_params(*semantics))(*args)
    exs = exchange if isinstance(exchange, (list, tuple)) else [exchange]
    n_in, n_out, n_scr = len(in_specs), len(out_specs), len(scratch_shapes)
    x_in, x_out = [len(ex.inputs) for ex in exs], [len(ex.out_shapes) for ex in exs]

    def take(refs, counts):
        groups = []
        for n in counts:
            groups.append(refs[:n])
            refs = refs[n:]
        return groups, refs

    def carrier(*refs):
        ins, refs = refs[:n_in], refs[n_in:]
        x_ins, refs = take(refs, x_in)
        outs, refs = refs[:n_out], refs[n_out:]
        x_outs, refs = take(refs, x_out)
        scr, refs = refs[:n_scr], refs[n_scr:]
        sems, _ = take(refs, [2] * len(exs))
        ids = [pl.program_id(a) for a in range(len(grid))]
        first = functools.reduce(jnp.logical_and, [i == 0 for i in ids])
        last = functools.reduce(jnp.logical_and, [i == g - 1 for i, g in zip(ids, grid)])
        step = functools.reduce(lambda acc, ig: acc * ig[1] + ig[0], zip(ids, grid), 0)

        @pl.when(first)
        def _():
            for ex, xi, xo, (send_sems, recv_sems) in zip(exs, x_ins, x_outs, sems):
                ex.start(xi, xo, send_sems, recv_sems)

        if any(ex.then for ex in exs):
            @pl.when(step == (3 * math.prod(grid)) // 4)
            def _():
                for ex, xi, xo, (send_sems, recv_sems) in zip(exs, x_ins, x_outs, sems):
                    ex.switch(xi, xo, send_sems, recv_sems)

        body(*ins, *outs, *scr)

        @pl.when(last)
        def _():
            for ex, xi, xo, (send_sems, recv_sems) in zip(exs, x_ins, x_outs, sems):
                ex.finish(xi, xo, send_sems, recv_sems)

    aliases, i0, o0 = {}, n_in, n_out
    for ex in exs:
        aliases.update({i0 + i: o0 + o for i, o in ex.aliases.items()})
        i0, o0 = i0 + len(ex.inputs), o0 + len(ex.out_shapes)
    res = pl.pallas_call(
        carrier, name=name, grid=grid, in_specs=list(in_specs) + [ANY] * sum(x_in),
        out_specs=list(out_specs) + [ANY] * sum(x_out),
        out_shape=list(out_shape) + [s for ex in exs for s in ex.out_shapes],
        scratch_shapes=list(scratch_shapes) + [s for ex in exs for s in ex.sem_shapes()],
        input_output_aliases=aliases, compiler_params=_params(*["arbitrary"] * len(grid)),
    )(*args, *[a for ex in exs for a in ex.inputs])
    x_res, _ = take(list(res[n_out:]), x_out)
    return list(res[:n_out]), (x_res if isinstance(exchange, (list, tuple)) else x_res[0])


def _rmsmod_fwd(x, g, shift, scale, name):
    S, D = x.shape
    tr = min(ROW_TILE, S)

    def body(x_ref, g_ref, sh_ref, sc_ref, h_ref):
        xv = x_ref[...]
        rstd = lax.rsqrt(jnp.mean(xv * xv, axis=-1, keepdims=True) + EPS)
        y = xv * rstd * g_ref[...]
        h_ref[...] = (y * (1.0 + sc_ref[...]) + sh_ref[...]).astype(h_ref.dtype)

    row = pl.BlockSpec((tr, D), lambda i: (i, 0))
    vec = pl.BlockSpec((1, D), lambda i: (0, 0))
    return pl.pallas_call(
        body, name=name, grid=(S // tr,), in_specs=[row, vec, vec, vec], out_specs=row,
        out_shape=jax.ShapeDtypeStruct((S, D), BF16), compiler_params=_params("parallel"),
    )(x, g, shift, scale)


class _NormBwd:
    def __init__(self, x, g, scale, dx_res, below=None):
        S, D = x.shape
        self.below, self.coef = below, (below[2] if below else None)
        self.inputs = [x, g, scale, dx_res] + ([below[0], below[1]] if below else [])
        vshape = jax.ShapeDtypeStruct((1, D), F32)
        self.out_shape = [jax.ShapeDtypeStruct((S, D), F32), vshape, vshape, vshape]
        if below:
            self.out_shape += [jax.ShapeDtypeStruct((S, D), BF16), vshape]

    def specs(self, tr, D):
        row = pl.BlockSpec((tr, D), lambda i: (i, 0))
        vec = pl.BlockSpec((1, D), lambda i: (0, 0))
        return ([row, vec, vec, row] + ([row, vec] if self.below else []),
                [row, vec, vec, vec] + ([row, vec] if self.below else []))

    def step(self, dhv, in_refs, out_refs):
        if self.below:
            x_ref, g_ref, sc_ref, dxr_ref, f_ref, gate_ref = in_refs
            dx_ref, dsh_ref, dsc_ref, dg_ref, df_ref, dgate_ref = out_refs
            sums = (dsh_ref, dsc_ref, dg_ref, dgate_ref)
        else:
            x_ref, g_ref, sc_ref, dxr_ref = in_refs
            dx_ref, dsh_ref, dsc_ref, dg_ref = out_refs
            sums = (dsh_ref, dsc_ref, dg_ref)

        @pl.when(pl.program_id(0) == 0)
        def _():
            for ref in sums:
                ref[...] = jnp.zeros_like(ref)

        xv, gv = x_ref[...], g_ref[...]
        one_sc = 1.0 + sc_ref[...]
        rstd = lax.rsqrt(jnp.mean(xv * xv, axis=-1, keepdims=True) + EPS)
        n = xv * rstd
        dsh_ref[...] += jnp.sum(dhv, axis=0, keepdims=True)
        dsc_ref[...] += jnp.sum(dhv * n, axis=0, keepdims=True) * gv
        dg_ref[...] += jnp.sum(dhv * n, axis=0, keepdims=True) * one_sc
        dn = dhv * (gv * one_sc)
        dx = dxr_ref[...] + rstd * (dn - n * jnp.mean(dn * n, axis=-1, keepdims=True))
        dx_ref[...] = dx
        if self.below:
            df_ref[...] = (self.coef * gate_ref[...] * dx).astype(df_ref.dtype)
            dgate_ref[...] += self.coef * jnp.sum(dx * f_ref[...].astype(F32), axis=0, keepdims=True)


def _rmsmod_bwd(dh, norm, name, exchange=None):
    S, D = dh.shape
    tr = min(ROW_TILE, S)
    n_in = len(norm.inputs)

    def body(dh_ref, *refs):
        norm.step(dh_ref[...], refs[:n_in], refs[n_in:])

    in_specs, out_specs = norm.specs(tr, D)
    return _call(body, name=name, grid=(S // tr,), in_specs=[pl.BlockSpec((tr, D), lambda i: (i, 0))] + in_specs,
                 out_specs=out_specs, out_shape=norm.out_shape, args=[dh] + norm.inputs, semantics=("arbitrary",),
                 exchange=exchange)


def _ffn_in_fwd(h, w4, name, exchange=None):
    S, D = h.shape
    tm = min(2 * ROW_TILE, S)
    n = w4.shape[2]

    def body(h_ref, wg_ref, wu_ref, zg_ref, zu_ref, a_ref):
        hv = h_ref[...]
        zg = _nn(hv, wg_ref[...])
        zu = _nn(hv, wu_ref[...])
        zg_ref[...] = zg.astype(zg_ref.dtype)
        zu_ref[...] = zu.astype(zu_ref.dtype)
        a_ref[...] = (zg * _sigmoid(zg) * zu).astype(a_ref.dtype)

    out = pl.BlockSpec((tm, n), lambda j, m: (m, j))
    oshape = jax.ShapeDtypeStruct((S, 2 * n), BF16)
    return _call(
        body, name=name, grid=(2, S // tm),
        in_specs=[pl.BlockSpec((tm, D), lambda j, m: (m, 0)),
                  pl.BlockSpec((None, D, n), lambda j, m: (j, 0, 0)),
                  pl.BlockSpec((None, D, n), lambda j, m: (j + 2, 0, 0))],
        out_specs=[out, out, out], out_shape=[oshape, oshape, oshape], args=(h, w4, w4),
        semantics=("parallel", "parallel"), exchange=exchange)


def _proj_out_fwd(lhs, w, x, gate, coef, name, exchange=None, next_norm=None):
    S, D = x.shape
    tm = min(ROW_TILE, S)
    ks = [a.shape[1] for a in lhs]

    def body(*refs):
        lhs_refs, refs = refs[:len(lhs)], refs[len(lhs):]
        if next_norm:
            w_ref, x_ref, gate_ref, g_ref, sh_ref, sc_ref, xn_ref, f_ref, h_ref = refs
        else:
            w_ref, x_ref, gate_ref, xn_ref, f_ref = refs
        acc, off = None, 0
        for a_ref, k in zip(lhs_refs, ks):
            part = _nn(a_ref[...], w_ref[off:off + k, :])
            acc = part if acc is None else acc + part
            off += k
        f_ref[...] = acc.astype(f_ref.dtype)
        xn = x_ref[...] + coef * gate_ref[...] * acc
        xn_ref[...] = xn
        if next_norm:
            rstd = lax.rsqrt(jnp.mean(xn * xn, axis=-1, keepdims=True) + EPS)
            h_ref[...] = (xn * rstd * g_ref[...] * (1.0 + sc_ref[...]) + sh_ref[...]).astype(h_ref.dtype)

    row = pl.BlockSpec((tm, D), lambda m: (m, 0))
    vec = pl.BlockSpec((1, D), lambda m: (0, 0))
    extra = list(next_norm) if next_norm else []
    return _call(
        body, name=name, grid=(S // tm,),
        in_specs=[pl.BlockSpec((tm, k), lambda m: (m, 0)) for k in ks]
        + [_resident(w.shape, lambda m: (0, 0)), row, vec] + [vec] * len(extra),
        out_specs=[row, row] + ([row] if next_norm else []),
        out_shape=[jax.ShapeDtypeStruct((S, D), F32), jax.ShapeDtypeStruct((S, D), BF16)]
        + ([jax.ShapeDtypeStruct((S, D), BF16)] if next_norm else []),
        args=(*lhs, w, x, gate, *extra), semantics=("parallel",), exchange=exchange)


def _proj_out_loss(lhs, w, x, gate, coef, target, name):
    S, D = x.shape
    tm = min(ROW_TILE, S)

    def body(a_ref, w_ref, x_ref, gate_ref, t_ref, dy_ref, df_ref, dgate_ref, sq_ref):
        @pl.when(pl.program_id(0) == 0)
        def _():
            dgate_ref[...] = jnp.zeros_like(dgate_ref)
            sq_ref[...] = jnp.zeros_like(sq_ref)

        f = _nn(a_ref[...], w_ref[...])
        gate = coef * gate_ref[...]
        err = x_ref[...] + gate * f - t_ref[...]
        sq_ref[...] += jnp.sum(err * err, axis=0, keepdims=True)
        dy = err * (1.0 / D)
        dy_ref[...] = dy
        df_ref[...] = (gate * dy).astype(df_ref.dtype)
        dgate_ref[...] += coef * jnp.sum(dy * f, axis=0, keepdims=True)

    row = pl.BlockSpec((tm, D), lambda m: (m, 0))
    vec = pl.BlockSpec((1, D), lambda m: (0, 0))
    vshape = jax.ShapeDtypeStruct((1, D), F32)
    return pl.pallas_call(
        body, name=name, grid=(S // tm,),
        in_specs=[pl.BlockSpec((tm, lhs.shape[1]), lambda m: (m, 0)), _resident(w.shape, lambda m: (0, 0)), row, vec, row],
        out_specs=[row, row, vec, vec],
        out_shape=[jax.ShapeDtypeStruct((S, D), F32), jax.ShapeDtypeStruct((S, D), BF16), vshape, vshape],
        compiler_params=_params("arbitrary"),
    )(lhs, w, x, gate, target)


def _matmul_nn(a, w, out_dtype, tm, name):
    S, K = a.shape
    N = w.shape[1]
    tm = min(tm, S)

    def body(a_ref, w_ref, o_ref):
        o_ref[...] = _nn(a_ref[...], w_ref[...]).astype(o_ref.dtype)

    return pl.pallas_call(
        body, name=name, grid=(S // tm,),
        in_specs=[pl.BlockSpec((tm, K), lambda m: (m, 0)), _resident((K, N), lambda m: (0, 0))],
        out_specs=pl.BlockSpec((tm, N), lambda m: (m, 0)), out_shape=jax.ShapeDtypeStruct((S, N), out_dtype),
        compiler_params=_params("parallel"),
    )(a, w)


def _dact_bwd(df, w_out, zg, zu, name, exchange=None):
    S, D = df.shape
    tm = min(ROW_TILE, S)
    n = w_out.shape[0] // 2

    def body(df_ref, w_ref, zg_ref, zu_ref, dzg_ref, dzu_ref):
        da = _nt(df_ref[...], w_ref[...]).astype(BF16)
        zg_v, zu_v = zg_ref[...], zu_ref[...]
        s = _sigmoid(zg_v)
        dzu_ref[...] = da * zg_v * s
        dzg_ref[...] = da * zu_v * (s * (1.0 + zg_v * (1.0 - s)))

    blk = pl.BlockSpec((tm, n), lambda j, m: (m, j))
    oshape = jax.ShapeDtypeStruct((S, 2 * n), BF16)
    return _call(
        body, name=name, grid=(2, S // tm),
        in_specs=[pl.BlockSpec((tm, D), lambda j, m: (m, 0)), pl.BlockSpec((n, D), lambda j, m: (j, 0)), blk, blk],
        out_specs=[blk, blk], out_shape=[oshape, oshape], args=(df, w_out, zg, zu), semantics=("parallel", "parallel"),
        exchange=exchange)


def _ffn_in_dgrad(dzg, dzu, w4, name, exchange=None, norm=None):
    S = dzg.shape[0]
    D, n = w4.shape[1], w4.shape[2]
    tm = min(ROW_TILE, S)
    n_norm = len(norm.inputs) if norm else 0

    def body(dzg_ref, dzu_ref, w_ref, *refs):
        acc = _nt(dzg_ref[:, 0:n], w_ref[0])
        acc += _nt(dzg_ref[:, n:2 * n], w_ref[1])
        acc += _nt(dzu_ref[:, 0:n], w_ref[2])
        acc += _nt(dzu_ref[:, n:2 * n], w_ref[3])
        if norm:
            norm.step(acc, refs[:n_norm], refs[n_norm:])
        else:
            refs[0][...] = acc

    blk = pl.BlockSpec((tm, 2 * n), lambda m: (m, 0))
    in_specs, args = [blk, blk, _resident(w4.shape, lambda m: (0, 0, 0))], [dzg, dzu, w4]
    out_specs, out_shape = [pl.BlockSpec((tm, D), lambda m: (m, 0))], [jax.ShapeDtypeStruct((S, D), F32)]
    if norm:
        norm_in, out_specs = norm.specs(tm, D)
        in_specs, args, out_shape = in_specs + norm_in, args + norm.inputs, norm.out_shape
    return _call(body, name=name, grid=(S // tm,), in_specs=in_specs, out_specs=out_specs, out_shape=out_shape, args=args,
                 semantics=("arbitrary",) if norm else ("parallel",), exchange=exchange)


def _matmul_nt(pieces, w, tm, name, exchange=None, norm=None):
    S = pieces[0].shape[0]
    ks = [p.shape[1] for p in pieces]
    N = w.shape[0]
    tm = min(tm, S)
    n_norm = len(norm.inputs) if norm else 0

    def body(*refs):
        p_refs, w_ref, refs = refs[:len(ks)], refs[len(ks)], refs[len(ks) + 1:]
        acc, off = None, 0
        for p_ref, k in zip(p_refs, ks):
            part = _nt(p_ref[...], w_ref[:, off:off + k])
            acc = part if acc is None else acc + part
            off += k
        if norm:
            norm.step(acc, refs[:n_norm], refs[n_norm:])
        else:
            refs[0][...] = acc

    in_specs = [pl.BlockSpec((tm, k), lambda m: (m, 0)) for k in ks] + [_resident(w.shape, lambda m: (0, 0))]
    args = list(pieces) + [w]
    out_specs, out_shape = [pl.BlockSpec((tm, N), lambda m: (m, 0))], [jax.ShapeDtypeStruct((S, N), F32)]
    if norm:
        norm_in, out_specs = norm.specs(tm, N)
        in_specs, args, out_shape = in_specs + norm_in, args + norm.inputs, norm.out_shape
    return _call(body, name=name, grid=(S // tm,), in_specs=in_specs, out_specs=out_specs, out_shape=out_shape, args=args,
                 semantics=("arbitrary",) if norm else ("parallel",), exchange=exchange)


def _wgrad(a, gs, tn, name, exchange=None):
    S, Ka = a.shape
    N = gs[0].shape[1]
    ts = min(ROW_TILE * (2 if Ka <= D_MODEL else 1), S)

    def body(a_ref, *refs):
        g_refs, o_ref = refs[:-1], refs[-1]

        @pl.when(pl.program_id(1) == 0)
        def _():
            o_ref[...] = jnp.zeros_like(o_ref)

        a_t = a_ref[...].T
        for i, g_ref in enumerate(g_refs):
            o_ref[i] += _nn(a_t, g_ref[...])

    return _call(
        body, name=name, grid=(N // tn, S // ts),
        in_specs=[pl.BlockSpec((ts, Ka), lambda j, s: (s, 0))] + [pl.BlockSpec((ts, tn), lambda j, s: (s, j))] * len(gs),
        out_specs=[pl.BlockSpec((len(gs), None, Ka, tn), lambda j, s: (0, j, 0, 0))],
        out_shape=[jax.ShapeDtypeStruct((len(gs), N // tn, Ka, tn), F32)], args=(a, *gs),
        semantics=("parallel", "arbitrary"), exchange=exchange)


def _wgrad_pieces(a, pieces, tn, name):
    S, Ka = a.shape
    ts = min(ROW_TILE, S)
    blocks = [(i, j) for i, p in enumerate(pieces) for j in range(p.shape[1] // tn)]

    def body(a_ref, *refs):
        g_refs, o_ref = refs[:-1], refs[-1]

        @pl.when(pl.program_id(0) == 0)
        def _():
            o_ref[...] = jnp.zeros_like(o_ref)

        a_t = a_ref[...].T
        for b, g_ref in enumerate(g_refs):
            o_ref[b] += _nn(a_t, g_ref[...])

    return pl.pallas_call(
        body, name=name, grid=(S // ts,),
        in_specs=[pl.BlockSpec((ts, Ka), lambda s: (s, 0))] + [pl.BlockSpec((ts, tn), lambda s, j=j: (s, j)) for _, j in blocks],
        out_specs=pl.BlockSpec((len(blocks), Ka, tn), lambda s: (0, 0, 0)),
        out_shape=jax.ShapeDtypeStruct((len(blocks), Ka, tn), F32), compiler_params=_params("arbitrary"),
    )(a, *[pieces[i] for i, _ in blocks])


def _wgrad_rows(lhs, g, name):
    S, Ka = lhs[0].shape
    N = g.shape[1]
    ts = min(ROW_TILE, S)

    def body(*refs):
        a_refs, g_ref, o_ref = refs[:-2], refs[-2], refs[-1]

        @pl.when(pl.program_id(0) == 0)
        def _():
            o_ref[...] = jnp.zeros_like(o_ref)

        gv = g_ref[...]
        for i, a_ref in enumerate(a_refs):
            o_ref[i] += _tn(a_ref[...], gv)

    return pl.pallas_call(
        body, name=name, grid=(S // ts,),
        in_specs=[pl.BlockSpec((ts, Ka), lambda s: (s, 0))] * len(lhs) + [pl.BlockSpec((ts, N), lambda s: (s, 0))],
        out_specs=pl.BlockSpec((len(lhs), Ka, N), lambda s: (0, 0, 0)),
        out_shape=jax.ShapeDtypeStruct((len(lhs), Ka, N), F32), compiler_params=_params("arbitrary"),
    )(*lhs, g)


def _hgrn_chunk_common(qr, fr, lb, oml, tri, last):
    sig_nf = _sigmoid(-fr)
    k = oml * sig_nf
    f_small = lb + oml * (jnp.exp(jnp.minimum(fr, 0.0)) * sig_nf)
    use_k = k < 0.5
    f = jnp.where(use_k, 1.0 - k, f_small)
    g = jnp.where(use_k, jnp.log1p(-k), jnp.log(f_small)) * math.log2(math.e)
    q = qr * _sigmoid(qr)
    G = _nn(tri, g, precision=lax.Precision.HIGHEST)
    Gl = G[last:last + 1]
    return q, k, f, G, Gl


def _hgrn_consts(reverse):
    C = HG_CHUNK
    r = lax.broadcasted_iota(jnp.int32, (C, C), 0)
    cc = lax.broadcasted_iota(jnp.int32, (C, C), 1)
    tri = ((cc >= r) if reverse else (cc <= r)).astype(F32)
    tri_t = ((cc <= r) if reverse else (cc >= r)).astype(F32)
    rid = lax.broadcasted_iota(jnp.int32, (C, HG_WIDTH), 0)
    return tri, tri_t, rid, (0 if reverse else C - 1)


def _head_slices():
    return [slice(h * HG_DIM, (h + 1) * HG_DIM) for h in range(HG_HEADS)]


def _per_head_lane_sum(x):
    C = x.shape[0]
    return jnp.concatenate(
        [jnp.broadcast_to(jnp.sum(x[:, sl], axis=-1, keepdims=True), (C, HG_DIM)) for sl in _head_slices()], axis=1)


HG_TILE = 8


def _pair_tiles(s, reverse):
    blk, r = divmod(s, HG_TILE)
    n_tiles = HG_CHUNK // HG_TILE
    others = range(0, blk) if reverse else range(blk + 1, n_tiles)
    return [(blk, r)] + [(t, None) for t in others]


def _pair_decay(G, s, tile, r, rid8, reverse, keys=False):
    rs = slice(tile * HG_TILE, (tile + 1) * HG_TILE)
    d = (G[s:s + 1] - G[rs]) if keys else (G[rs] - G[s:s + 1])
    if r is not None:
        d = jnp.where((rid8 <= r) if reverse else (rid8 >= r), d, NEG)
    return rs, jnp.exp2(d)


def _hgrn_fwd(z, lb, direction, name, exchange=None):
    S = z.shape[0]
    C, DK, W = HG_CHUNK, HG_DIM, HG_WIDTH
    tb = min(HG_ROWS, S)
    n_t, n_c = S // tb, tb // C
    reverse = direction == 1
    tmap = (lambda i: n_t - 1 - i) if reverse else (lambda i: i)

    def body(q_ref, f_ref, v_ref, lb_ref, o_ref, st_out_ref, st_ref):
        @pl.when(pl.program_id(0) == 0)
        def _():
            st_ref[...] = jnp.zeros_like(st_ref)

        lbv, oml = lb_ref[0:1, :], lb_ref[1:2, :]
        tri, _, _, last = _hgrn_consts(reverse)
        rid8 = lax.broadcasted_iota(jnp.int32, (HG_TILE, W), 0)

        def chunk(ci, carry):
            cidx = (n_c - 1 - ci) if reverse else ci
            rows = pl.ds(pl.multiple_of(cidx * C, C), C)
            v = v_ref[rows, :]
            q, k, _, G, Gl = _hgrn_chunk_common(q_ref[rows, :], f_ref[rows, :], lbv, oml, tri, last)
            qd = (q * jnp.exp2(G)).astype(BF16)
            kd = (k * jnp.exp2(Gl - G)).astype(BF16)
            e_gl = jnp.exp2(Gl)
            v_b = v.astype(BF16)
            inter = []
            for h, sl in enumerate(_head_slices()):
                st0 = st_ref[h]
                st_out_ref[h, cidx] = st0
                inter.append(_nt(qd[:, sl], st0.astype(BF16)))
                st_ref[h] = st0 * e_gl[:, sl] + _tn(v_b[:, sl], kd[:, sl])
            o = jnp.concatenate(inter, axis=1)
            o_t = [o[t * HG_TILE:(t + 1) * HG_TILE] for t in range(C // HG_TILE)]
            for s in range(C):
                k_s, v_s = k[s:s + 1], v[s:s + 1]
                for tile, r in _pair_tiles(s, reverse):
                    rs, e_s = _pair_decay(G, s, tile, r, rid8, reverse)
                    o_t[tile] = o_t[tile] + _per_head_lane_sum(q[rs] * k_s * e_s) * v_s
            o_ref[rows, :] = jnp.concatenate(o_t, axis=0)
            return carry

        lax.fori_loop(0, n_c, chunk, 0, unroll=8)

    def sec(j):
        return pl.BlockSpec((tb, W), lambda i: (tmap(i), j))

    return _call(
        body, name=name, grid=(n_t,),
        in_specs=[sec(0), sec(1 + direction), sec(3), pl.BlockSpec((2, W), lambda i: (0, 0))],
        out_specs=[sec(0), pl.BlockSpec((HG_HEADS, n_c, DK, DK), lambda i: (0, tmap(i), 0, 0))],
        out_shape=[jax.ShapeDtypeStruct((S, W), F32), jax.ShapeDtypeStruct((HG_HEADS, S // C, DK, DK), F32)],
        scratch_shapes=[pltpu.VMEM((HG_HEADS, DK, DK), F32)], args=(z, z, z, lb), semantics=("arbitrary",),
        exchange=exchange)


def _hgrn_bwd(z, lb, do, states, direction, name, acc=None, exchange=None):
    S = z.shape[0]
    C, DK, W = HG_CHUNK, HG_DIM, HG_WIDTH
    tb = min(HG_ROWS, S)
    n_t, n_c = S // tb, tb // C
    reverse = direction == 1
    tmap = (lambda i: i) if reverse else (lambda i: n_t - 1 - i)

    def body(*refs):
        if acc:
            q_ref, f_ref, v_ref, lb_ref, do_ref, st_in_ref, dqa_ref, dva_ref, dq_ref, df_ref, dv_ref, doml_ref, dst_ref = refs
        else:
            q_ref, f_ref, v_ref, lb_ref, do_ref, st_in_ref, dq_ref, df_ref, dv_ref, doml_ref, dst_ref = refs

        @pl.when(pl.program_id(0) == 0)
        def _():
            dst_ref[...] = jnp.zeros_like(dst_ref)
            doml_ref[...] = jnp.zeros_like(doml_ref)

        lbv, oml = lb_ref[0:1, :], lb_ref[1:2, :]
        tri, tri_t, rid, last = _hgrn_consts(reverse)
        rid8 = lax.broadcasted_iota(jnp.int32, (HG_TILE, W), 0)

        def chunk(ci, carry):
            cidx = ci if reverse else (n_c - 1 - ci)
            rows = pl.ds(pl.multiple_of(cidx * C, C), C)
            qr, fr, v, dov = q_ref[rows, :], f_ref[rows, :], v_ref[rows, :], do_ref[rows, :]
            q, k, f, G, Gl = _hgrn_chunk_common(qr, fr, lbv, oml, tri, last)
            e_g, e_gl, e_kd = jnp.exp2(G), jnp.exp2(Gl), jnp.exp2(Gl - G)
            qd, kd = q * e_g, k * e_kd
            do_b, v_b, qd_b, kd_b = dov.astype(BF16), v.astype(BF16), qd.astype(BF16), kd.astype(BF16)
            dqd, dkd, dv, state_dot = [], [], [], []
            for h, sl in enumerate(_head_slices()):
                st0, dst1 = st_in_ref[h, cidx], dst_ref[h]
                dst1_b = dst1.astype(BF16)
                dqd.append(_nn(do_b[:, sl], st0.astype(BF16)))
                dkd.append(_nn(v_b[:, sl], dst1_b))
                dv.append(_nt(kd_b[:, sl], dst1_b))
                state_dot.append(jnp.sum(st0 * dst1, axis=0, keepdims=True))
                dst_ref[h] = dst1 * e_gl[:, sl] + _tn(do_b[:, sl], qd_b[:, sl])
            dqd, dkd, dv = [jnp.concatenate(t, axis=1) for t in (dqd, dkd, dv)]
            d_gl = e_gl * jnp.concatenate(state_dot, axis=1) + jnp.sum(dkd * kd, axis=0, keepdims=True)
            dq, dk = dqd * e_g, dkd * e_kd
            n_tiles = C // HG_TILE
            dq_t, dk_t, dv_t = [[x[t * HG_TILE:(t + 1) * HG_TILE] for t in range(n_tiles)] for x in (dq, dk, dv)]
            for s in range(C):
                k_s, v_s = k[s:s + 1], v[s:s + 1]
                for tile, r in _pair_tiles(s, reverse):
                    rs, e_s = _pair_decay(G, s, tile, r, rid8, reverse)
                    dq_t[tile] = dq_t[tile] + _per_head_lane_sum(dov[rs] * v_s) * e_s * k_s
            for t in range(C):
                q_t, do_t = q[t:t + 1], dov[t:t + 1]
                for tile, r in _pair_tiles(t, not reverse):
                    rs, x_t = _pair_decay(G, t, tile, r, rid8, not reverse, keys=True)
                    qx = q_t * x_t
                    dv_t[tile] = dv_t[tile] + _per_head_lane_sum(k[rs] * qx) * do_t
                    dk_t[tile] = dk_t[tile] + _per_head_lane_sum(v[rs] * do_t) * qx
            dq, dk, dv = [jnp.concatenate(x, axis=0) for x in (dq_t, dk_t, dv_t)]
            d_big_g = dq * q - dk * k + jnp.where(rid == last, d_gl, 0.0)
            dg = _nn(tri_t, d_big_g, precision=lax.Precision.HIGHEST)
            dk_all = dk - dg / f
            sig_nf = _sigmoid(-fr)
            df_ref[rows, :] = (-dk_all * k * (1.0 - sig_nf)).astype(df_ref.dtype)
            doml_ref[...] += jnp.sum(dk_all * sig_nf, axis=0, keepdims=True)
            sq = _sigmoid(qr)
            dqr = dq * (sq * (1.0 + qr * (1.0 - sq)))
            if acc:
                dqr = dqr + dqa_ref[rows, :]
                dv = dv + dva_ref[rows, :]
            dq_ref[rows, :] = dqr.astype(dq_ref.dtype)
            dv_ref[rows, :] = dv.astype(dv_ref.dtype)
            return carry

        lax.fori_loop(0, n_c, chunk, 0, unroll=8)

    def sec(j):
        return pl.BlockSpec((tb, W), lambda i: (tmap(i), j))

    vec = pl.BlockSpec((1, W), lambda i: (0, 0))
    ins = [z, z, z, lb, do, states]
    in_specs = [sec(0), sec(1 + direction), sec(3), pl.BlockSpec((2, W), lambda i: (0, 0)), sec(0),
                pl.BlockSpec((HG_HEADS, n_c, DK, DK), lambda i: (0, tmap(i), 0, 0))]
    if acc:
        ins += list(acc)
        in_specs += [sec(0), sec(0)]
    final = jax.ShapeDtypeStruct((S, W), BF16)
    partial = final if acc else jax.ShapeDtypeStruct((S, W), F32)
    return _call(
        body, name=name, grid=(n_t,), in_specs=in_specs,
        out_specs=[sec(0), sec(0), sec(0), vec],
        out_shape=[partial, final, partial, jax.ShapeDtypeStruct((1, W), F32)],
        scratch_shapes=[pltpu.VMEM((HG_HEADS, DK, DK), F32)], args=ins, semantics=("arbitrary",), exchange=exchange)


def _hgrn_post_fwd(o_f, o_b, z, norm_g, name):
    S = z.shape[0]
    tr = min(ROW_TILE, S)

    def body(of_ref, ob_ref, gr_ref, ng_ref, y_ref):
        o = of_ref[...] + ob_ref[...]
        gr = gr_ref[...]
        gate = gr * _sigmoid(gr)
        ng = ng_ref[...]
        for h in range(HG_HEADS):
            sl = slice(h * HG_DIM, (h + 1) * HG_DIM)
            oh = o[:, sl]
            rstd = lax.rsqrt(jnp.mean(oh * oh, axis=-1, keepdims=True) + EPS)
            y_ref[:, sl] = (oh * rstd * ng[:, sl] * gate[:, sl]).astype(y_ref.dtype)

    row = pl.BlockSpec((tr, HG_WIDTH), lambda i: (i, 0))
    return pl.pallas_call(
        body, name=name, grid=(S // tr,),
        in_specs=[row, row, pl.BlockSpec((tr, HG_WIDTH), lambda i: (i, 4)), pl.BlockSpec((1, HG_WIDTH), lambda i: (0, 0))],
        out_specs=row, out_shape=jax.ShapeDtypeStruct((S, HG_WIDTH), BF16), compiler_params=_params("parallel"),
    )(o_f, o_b, z, norm_g)


def _hgrn_post_bwd(dy, o_f, o_b, z, norm_g, name):
    S = z.shape[0]
    tr = min(ROW_TILE, S)

    def body(dy_ref, of_ref, ob_ref, gr_ref, ng_ref, do_ref, dgr_ref, dng_ref):
        @pl.when(pl.program_id(0) == 0)
        def _():
            dng_ref[...] = jnp.zeros_like(dng_ref)

        o = of_ref[...] + ob_ref[...]
        gr, ng, dyv = gr_ref[...], ng_ref[...], dy_ref[...]
        sg = _sigmoid(gr)
        for h in range(HG_HEADS):
            sl = slice(h * HG_DIM, (h + 1) * HG_DIM)
            oh, dyh, grh, sgh, ngh = o[:, sl], dyv[:, sl], gr[:, sl], sg[:, sl], ng[:, sl]
            rstd = lax.rsqrt(jnp.mean(oh * oh, axis=-1, keepdims=True) + EPS)
            on = oh * rstd
            du = dyh * (grh * sgh)
            dgr_ref[:, sl] = (dyh * (on * ngh) * (sgh * (1.0 + grh * (1.0 - sgh)))).astype(dgr_ref.dtype)
            dng_ref[:, sl] += jnp.sum(du * on, axis=0, keepdims=True)
            don = du * ngh
            do_ref[:, sl] = rstd * (don - on * jnp.mean(don * on, axis=-1, keepdims=True))

    row = pl.BlockSpec((tr, HG_WIDTH), lambda i: (i, 0))
    vec = pl.BlockSpec((1, HG_WIDTH), lambda i: (0, 0))
    full = jax.ShapeDtypeStruct((S, HG_WIDTH), F32)
    return pl.pallas_call(
        body, name=name, grid=(S // tr,),
        in_specs=[row, row, row, pl.BlockSpec((tr, HG_WIDTH), lambda i: (i, 4)), vec],
        out_specs=[row, row, vec],
        out_shape=[full, jax.ShapeDtypeStruct((S, HG_WIDTH), BF16), jax.ShapeDtypeStruct((1, HG_WIDTH), F32)],
        compiler_params=_params("arbitrary"),
    )(dy, o_f, o_b, z, norm_g)


def _t5_bucket_table():
    rel = (np.arange(3 * BLOCK)[None, :] - BLOCK) - np.arange(BLOCK)[:, None]
    nb = NUM_BUCKETS // 2
    max_exact = nb // 2
    ret = (rel > 0).astype(np.int32) * nb
    n = np.abs(rel)
    ratio = np.log(np.maximum(n, 1).astype(np.float32) / np.float32(max_exact)) / np.float32(math.log(MAX_DISTANCE / max_exact))
    large = max_exact + (ratio.astype(np.float32) * np.float32(nb - max_exact)).astype(np.int32)
    large = np.minimum(large, nb - 1)
    bucket = ret + np.where(n < max_exact, n, large)
    return bucket.astype(np.int32), (n <= WINDOW)


def _bias_table(rel_bias, name):
    bucket, in_band = _t5_bucket_table()
    idx = jnp.asarray(np.where(in_band, bucket, -1))

    def body(rb_ref, idx_ref, o_ref):
        h = pl.program_id(0)
        iv = idx_ref[...]
        acc = jnp.where(iv < 0, NEG, 0.0).astype(F32)
        for b in range(NUM_BUCKETS):
            acc = acc + jnp.where(iv == b, rb_ref[b, h], 0.0)
        o_ref[...] = acc

    return pl.pallas_call(
        body, name=name, grid=(ATT_Q_HEADS,),
        in_specs=[pl.BlockSpec(memory_space=pltpu.SMEM), pl.BlockSpec((BLOCK, 3 * BLOCK), lambda h: (0, 0))],
        out_specs=pl.BlockSpec((None, BLOCK, 3 * BLOCK), lambda h: (h, 0, 0)),
        out_shape=jax.ShapeDtypeStruct((ATT_Q_HEADS, BLOCK, 3 * BLOCK), F32), compiler_params=_params("parallel"),
    )(rel_bias, idx)


def _bias_grad(ds_sum_t, name):
    bucket, in_band = _t5_bucket_table()
    idx_t = jnp.asarray(np.where(in_band, bucket, -1).T)

    def body(ds_ref, idx_ref, o_ref):
        iv, ds = idx_ref[...], ds_ref[...]
        for b in range(NUM_BUCKETS):
            o_ref[b:b + 1, :] = jnp.sum(jnp.where(iv == b, ds, 0.0), axis=0, keepdims=True)

    return pl.pallas_call(
        body, name=name, grid=(ATT_Q_HEADS,),
        in_specs=[pl.BlockSpec((None, 3 * BLOCK, BLOCK), lambda h: (h // ATT_GROUP, 0, h % ATT_GROUP)),
                  pl.BlockSpec((3 * BLOCK, BLOCK), lambda h: (0, 0))],
        out_specs=pl.BlockSpec((None, NUM_BUCKETS, BLOCK), lambda h: (h, 0, 0)),
        out_shape=jax.ShapeDtypeStruct((ATT_Q_HEADS, NUM_BUCKETS, BLOCK), F32), compiler_params=_params("parallel"),
    )(ds_sum_t, idx_t)


Q_COL = 5 * HG_WIDTH
KV_COL = Q_COL + ATT_WIDTH
GROUP_WIDTH = ATT_GROUP * ATT_HEAD_DIM


def _stack_heads(blk):
    dh = ATT_HEAD_DIM
    return jnp.concatenate([blk[:, g * dh:(g + 1) * dh] for g in range(ATT_GROUP)], axis=0)


def _unstack_heads(st):
    return jnp.concatenate([st[g * BLOCK:(g + 1) * BLOCK] for g in range(ATT_GROUP)], axis=1)


def _rms_rows(x):
    rstd = lax.rsqrt(jnp.mean(x * x, axis=-1, keepdims=True) + EPS)
    return x * rstd, rstd


def _edge_ok(n, nb):
    colid = lax.broadcasted_iota(jnp.int32, (ATT_GROUP * BLOCK, 3 * BLOCK), 1)
    return jnp.logical_and(jnp.logical_or(colid >= BLOCK, n > 0), jnp.logical_or(colid < 2 * BLOCK, n < nb - 1))


def _sink_column(sink_ref, j=0):
    heads = range(j * ATT_GROUP, (j + 1) * ATT_GROUP)
    return jnp.concatenate([jnp.broadcast_to(sink_ref[h][:, 0:1], (BLOCK, 1)) for h in heads], axis=0)


def _attn_fwd(z, q_g, k_g, sink, bias, name):
    S = z.shape[0]
    nb = S // BLOCK
    G, dh, KV = ATT_GROUP, ATT_HEAD_DIM, ATT_KV_HEADS
    scale = 1.0 / math.sqrt(dh)

    def body(q_ref, kv0, kv1, kv2, qg_ref, kg_ref, sink_ref, bias_ref, o_ref):
        n = pl.program_id(0)
        edge_ok = _edge_ok(n, nb)
        cat = jnp.concatenate([kv0[...], kv1[...], kv2[...]], axis=0)
        qblk = q_ref[...]
        kn = [(_rms_rows(cat[:, j * dh:(j + 1) * dh])[0] * kg_ref[...]).astype(BF16) for j in range(KV)]
        vb = [cat[:, (KV + j) * dh:(KV + j + 1) * dh].astype(BF16) for j in range(KV)]
        qn = [(_rms_rows(_stack_heads(qblk[:, j * GROUP_WIDTH:(j + 1) * GROUP_WIDTH]))[0] * (qg_ref[...] * scale)).astype(BF16)
              for j in range(KV)]
        s = [_nt(qn[j], kn[j]) + bias_ref[j * G:(j + 1) * G].reshape(G * BLOCK, 3 * BLOCK) for j in range(KV)]
        s = [jnp.where(edge_ok, sj, NEG) for sj in s]
        sinks = [_sink_column(sink_ref, j) for j in range(KV)]
        m = [jnp.maximum(jnp.max(s[j], axis=-1, keepdims=True), sinks[j]) for j in range(KV)]
        e = [jnp.exp(s[j] - m[j]) for j in range(KV)]
        den = [jnp.sum(e[j], axis=-1, keepdims=True) + jnp.exp(sinks[j] - m[j]) for j in range(KV)]
        o = [_nn(e[j].astype(BF16), vb[j]) * (1.0 / den[j]) for j in range(KV)]
        o_ref[...] = jnp.concatenate([_unstack_heads(oj) for oj in o], axis=1).astype(o_ref.dtype)

    def kv(shift):
        return pl.BlockSpec((BLOCK, 2 * KV_WIDTH), lambda n: (jnp.clip(n + shift, 0, nb - 1), KV_COL // (2 * KV_WIDTH)))

    gain = pl.BlockSpec((1, dh), lambda n: (0, 0))
    return pl.pallas_call(
        body, name=name, grid=(nb,),
        in_specs=[pl.BlockSpec((BLOCK, ATT_WIDTH), lambda n: (n, Q_COL // ATT_WIDTH)), kv(-1), kv(0), kv(1), gain, gain,
                  pl.BlockSpec((ATT_Q_HEADS, 1, BLOCK), lambda n: (0, 0, 0)),
                  pl.BlockSpec((ATT_Q_HEADS, BLOCK, 3 * BLOCK), lambda n: (0, 0, 0))],
        out_specs=pl.BlockSpec((BLOCK, ATT_WIDTH), lambda n: (n, 0)),
        out_shape=jax.ShapeDtypeStruct((S, ATT_WIDTH), BF16), compiler_params=_params("parallel"),
    )(z, z, z, z, q_g, k_g, sink, bias)


def _attn_bwd(z, q_g, k_g, sink, bias, do, name):
    S = z.shape[0]
    nb = S // BLOCK
    G, dh, KV = ATT_GROUP, ATT_HEAD_DIM, ATT_KV_HEADS
    scale = 1.0 / math.sqrt(dh)
    both = range(KV)
    bias_t = bias.reshape(KV, G, BLOCK, 3 * BLOCK).transpose(0, 3, 1, 2).reshape(KV, 3 * BLOCK, G * BLOCK)

    def body(q_ref, kv0, kv1, kv2, qg_ref, kg_ref, sink_ref, bias_ref, do_ref,
             dq_ref, dkw_ref, dvw_ref, ds_ref, dsink_ref, dqg_ref):
        n = pl.program_id(0)

        @pl.when(n == 0)
        def _():
            ds_ref[...] = jnp.zeros_like(ds_ref)
            dsink_ref[...] = jnp.zeros_like(dsink_ref)
            dqg_ref[...] = jnp.zeros_like(dqg_ref)

        rowid = lax.broadcasted_iota(jnp.int32, (3 * BLOCK, G * BLOCK), 0)
        edge_ok = jnp.logical_and(jnp.logical_or(rowid >= BLOCK, n > 0), jnp.logical_or(rowid < 2 * BLOCK, n < nb - 1))
        qg = qg_ref[...]
        cat = jnp.concatenate([kv0[...], kv1[...], kv2[...]], axis=0)
        qblk, doblk = q_ref[...], do_ref[...]
        kn = [(_rms_rows(cat[:, j * dh:(j + 1) * dh])[0] * kg_ref[...]).astype(BF16) for j in both]
        vb = [cat[:, (KV + j) * dh:(KV + j + 1) * dh].astype(BF16) for j in both]
        norm = [_rms_rows(_stack_heads(qblk[:, j * GROUP_WIDTH:(j + 1) * GROUP_WIDTH])) for j in both]
        qn = [(norm[j][0] * (qg * scale)).astype(BF16) for j in both]
        do_b = [_stack_heads(doblk[:, j * GROUP_WIDTH:(j + 1) * GROUP_WIDTH]).astype(BF16) for j in both]
        s = [_nt(kn[j], qn[j]) + bias_ref[j] for j in both]
        dp = [_nt(vb[j], do_b[j]) for j in both]
        s = [jnp.where(edge_ok, sj, NEG) for sj in s]
        sinks = [jnp.concatenate([sink_ref[j * G + g] for g in range(G)], axis=1) for j in both]
        m = [jnp.maximum(jnp.max(s[j], axis=0, keepdims=True), sinks[j]) for j in both]
        e = [jnp.exp(s[j] - m[j]) for j in both]
        e_sink = [jnp.exp(sinks[j] - m[j]) for j in both]
        inv = [1.0 / (jnp.sum(e[j], axis=0, keepdims=True) + e_sink[j]) for j in both]
        p = [e[j] * inv[j] for j in both]
        delta = [jnp.sum(p[j] * dp[j], axis=0, keepdims=True) for j in both]
        ds = [p[j] * (dp[j] - delta[j]) for j in both]
        ds_b = [dsj.astype(BF16) for dsj in ds]
        dqn = [_tn(kn[j], ds_b[j]).T * scale for j in both]
        for j in both:
            dvw_ref[j] = _nn(p[j].astype(BF16), do_b[j])
            dkw_ref[j] = _nn(ds_b[j], qn[j])
        for j in both:
            ds_ref[j] += ds[j]
            sink_term = e_sink[j] * inv[j] * delta[j]
            for g in range(G):
                dsink_ref[j * G + g] += (jnp.zeros((1, BLOCK), F32)
                                         - jnp.sum(sink_term[:, g * BLOCK:(g + 1) * BLOCK], axis=1, keepdims=True))
        dq = []
        for j in both:
            qhat, rstd = norm[j]
            dqg_ref[j] += jnp.sum(dqn[j] * qhat, axis=0, keepdims=True)
            dqh = dqn[j] * qg
            dq.append(_unstack_heads(rstd * (dqh - qhat * jnp.mean(dqh * qhat, axis=-1, keepdims=True))))
        dq_ref[...] = jnp.concatenate(dq, axis=1).astype(dq_ref.dtype)

    def kv(shift):
        return pl.BlockSpec((BLOCK, 2 * KV_WIDTH), lambda n: (jnp.clip(n + shift, 0, nb - 1), KV_COL // (2 * KV_WIDTH)))

    gain = pl.BlockSpec((1, dh), lambda n: (0, 0))
    sink_spec = pl.BlockSpec((ATT_Q_HEADS, 1, BLOCK), lambda n: (0, 0, 0))
    bias_spec = pl.BlockSpec((KV, 3 * BLOCK, G * BLOCK), lambda n: (0, 0, 0))
    win = pl.BlockSpec((KV, None, 3 * BLOCK, dh), lambda n: (0, n, 0, 0))
    wshape = jax.ShapeDtypeStruct((KV, nb, 3 * BLOCK, dh), F32)
    return pl.pallas_call(
        body, name=name, grid=(nb,),
        in_specs=[pl.BlockSpec((BLOCK, ATT_WIDTH), lambda n: (n, Q_COL // ATT_WIDTH)), kv(-1), kv(0), kv(1), gain, gain,
                  sink_spec, bias_spec, pl.BlockSpec((BLOCK, ATT_WIDTH), lambda n: (n, HG_WIDTH // ATT_WIDTH))],
        out_specs=[pl.BlockSpec((BLOCK, ATT_WIDTH), lambda n: (n, 0)), win, win, bias_spec, sink_spec,
                   pl.BlockSpec((KV, 1, dh), lambda n: (0, 0, 0))],
        out_shape=[jax.ShapeDtypeStruct((S, ATT_WIDTH), BF16), wshape, wshape,
                   jax.ShapeDtypeStruct((KV, 3 * BLOCK, G * BLOCK), F32),
                   jax.ShapeDtypeStruct((ATT_Q_HEADS, 1, BLOCK), F32),
                   jax.ShapeDtypeStruct((KV, 1, dh), F32)],
        compiler_params=_params("arbitrary"),
    )(z, z, z, z, q_g, k_g, sink, bias_t, do)


def _attn_kv_reduce(dkw, dvw, z, k_g, name):
    S = z.shape[0]
    nb = S // BLOCK
    dh = ATT_HEAD_DIM
    kb = min(8, nb)
    steps = nb // kb

    def body(a_lo, a, a_hi, b_lo, b, b_hi, kv_ref, kg_ref, dkv_ref, dkg_ref):
        n = pl.program_id(0)

        @pl.when(n == 0)
        def _():
            dkg_ref[...] = jnp.zeros_like(dkg_ref)

        lo = jnp.where(n > 0, 1.0, 0.0)
        hi = jnp.where(n < steps - 1, 1.0, 0.0)

        def overlap_add(w, w_lo, w_hi, j, i):
            before = lo * w_lo[j] if i == 0 else w[j, i - 1, 2 * BLOCK:3 * BLOCK, :]
            after = hi * w_hi[j] if i == kb - 1 else w[j, i + 1, 0:BLOCK, :]
            return w[j, i, BLOCK:2 * BLOCK, :] + before + after

        dkg = [jnp.zeros((1, dh), F32) for _ in range(ATT_KV_HEADS)]
        for i in range(kb):
            rows = slice(i * BLOCK, (i + 1) * BLOCK)
            dks, dvs = [], []
            for j in range(ATT_KV_HEADS):
                dkn = overlap_add(a, a_lo, a_hi, j, i)
                dvs.append(overlap_add(b, b_lo, b_hi, j, i))
                khat, rstd = _rms_rows(kv_ref[rows, j * dh:(j + 1) * dh])
                dkg[j] = dkg[j] + jnp.sum(dkn * khat, axis=0, keepdims=True)
                dkh = dkn * kg_ref[...]
                dks.append(rstd * (dkh - khat * jnp.mean(dkh * khat, axis=-1, keepdims=True)))
            dkv_ref[rows, :] = jnp.concatenate(dks + dvs, axis=1).astype(dkv_ref.dtype)
        for j in range(ATT_KV_HEADS):
            dkg_ref[j] += dkg[j]

    main = pl.BlockSpec((ATT_KV_HEADS, kb, 3 * BLOCK, dh), lambda n: (0, n, 0, 0))
    halo_lo = pl.BlockSpec((ATT_KV_HEADS, None, BLOCK, dh), lambda n: (0, jnp.maximum(n * kb - 1, 0), 2, 0))
    halo_hi = pl.BlockSpec((ATT_KV_HEADS, None, BLOCK, dh), lambda n: (0, jnp.minimum(n * kb + kb, nb - 1), 0, 0))
    return pl.pallas_call(
        body, name=name, grid=(steps,),
        in_specs=[halo_lo, main, halo_hi, halo_lo, main, halo_hi,
                  pl.BlockSpec((kb * BLOCK, 2 * KV_WIDTH), lambda n: (n, KV_COL // (2 * KV_WIDTH))),
                  pl.BlockSpec((1, dh), lambda n: (0, 0))],
        out_specs=[pl.BlockSpec((kb * BLOCK, 2 * KV_WIDTH), lambda n: (n, 0)),
                   pl.BlockSpec((ATT_KV_HEADS, 1, dh), lambda n: (0, 0, 0))],
        out_shape=[jax.ShapeDtypeStruct((S, 2 * KV_WIDTH), BF16), jax.ShapeDtypeStruct((ATT_KV_HEADS, 1, dh), F32)],
        compiler_params=_params("arbitrary"),
    )(dkw, dkw, dkw, dvw, dvw, dvw, z, k_g)


def _ada_wgrad(c_act_t, dm, name):
    D, nbatch = c_act_t.shape
    n = dm.shape[1]
    tr = 256

    def body(c_ref, dm_ref, o_ref):
        cv, dv = c_ref[...], dm_ref[...]
        acc = cv[:, 0:1] * dv[0:1, :]
        for b in range(1, nbatch):
            acc = acc + cv[:, b:b + 1] * dv[b:b + 1, :]
        o_ref[...] = acc

    return pl.pallas_call(
        body, name=name, grid=(D // tr,),
        in_specs=[pl.BlockSpec((tr, nbatch), lambda i: (i, 0)), pl.BlockSpec((nbatch, n), lambda i: (0, 0))],
        out_specs=pl.BlockSpec((tr, n), lambda i: (i, 0)), out_shape=jax.ShapeDtypeStruct((D, n), F32),
        compiler_params=_params("parallel"),
    )(c_act_t, dm)


def _to_bf16(w, name):
    R, Cn = w.shape
    tr = _row_tile(R)

    def body(w_ref, o_ref):
        o_ref[...] = w_ref[...].astype(BF16)

    blk = pl.BlockSpec((tr, Cn), lambda i: (i, 0))
    return pl.pallas_call(
        body, name=name, grid=(R // tr,), in_specs=[blk], out_specs=blk, out_shape=jax.ShapeDtypeStruct((R, Cn), BF16),
        compiler_params=_params("parallel"),
    )(w)


def _adamw(w, g, m, v, name):
    R, Cn = w.shape
    tr = R
    for cand in (256, 128, 64, 32, 16, 8):
        if R % cand == 0:
            tr = cand
            break

    def body(w_ref, g_ref, m_ref, v_ref, d_ref, nm_ref, nv_ref):
        gv = g_ref[...]
        m_new = ADAM_B1 * m_ref[...] + (1.0 - ADAM_B1) * gv
        v_new = ADAM_B2 * v_ref[...] + (1.0 - ADAM_B2) * (gv * gv)
        m_hat = m_new / (1.0 - ADAM_B1 ** ADAM_STEP)
        v_hat = v_new / (1.0 - ADAM_B2 ** ADAM_STEP)
        d_ref[...] = -ADAM_LR * (m_hat / (jnp.sqrt(v_hat) + ADAM_EPS) + ADAM_WD * w_ref[...])
        nm_ref[...] = m_new
        nv_ref[...] = v_new

    blk = pl.BlockSpec((tr, Cn), lambda i: (i, 0))
    shp = jax.ShapeDtypeStruct((R, Cn), F32)
    return pl.pallas_call(
        body, name=name, grid=(R // tr,), in_specs=[blk] * 4, out_specs=[blk] * 3, out_shape=[shp] * 3,
        compiler_params=_params("parallel"),
    )(w, g, m, v)


def _place():
    return lax.axis_index("x"), lax.axis_index("y"), lax.axis_index("c")


def _flip(place, k):
    x, y, c = place
    return (1 - x if k & 4 else x, 1 - y if k & 2 else y, 1 - c if k & 1 else c)


def _dev_index(place):
    x, y, c = place
    return 4 * x + 2 * y + c


def _chip_index(place):
    return 2 * place[0] + place[1]


def _gather8(x_ref, out_ref, send_sems, recv_sems, local_sem):
    me = _place()
    mine = pltpu.make_async_copy(x_ref, out_ref.at[_dev_index(me)], local_sem)
    mine.start()

    def copy(k, origin, to):
        return pltpu.make_async_remote_copy(
            src_ref=x_ref, dst_ref=out_ref.at[_dev_index(origin)], send_sem=send_sems.at[k - 1],
            recv_sem=recv_sems.at[k - 1], device_id=to, device_id_type=MESH)

    sends = [copy(k, me, _flip(me, k)) for k in range(1, 8)]
    for cp in sends:
        cp.start()
    for k in range(1, 8):
        copy(k, _flip(me, k), me).wait_recv()
    for cp in sends:
        cp.wait_send()
    mine.wait()


def _allgather8(x, name, reduce=False):
    R, Cn = x.shape

    def body(x_ref, *rest):
        if reduce:
            out_ref, sum_ref, send_sems, recv_sems, local_sem = rest
        else:
            out_ref, send_sems, recv_sems, local_sem = rest
        _gather8(x_ref, out_ref, send_sems, recv_sems, local_sem)
        if reduce:
            acc = out_ref[0]
            for i in range(1, 8):
                acc = acc + out_ref[i]
            sum_ref[...] = acc

    vm = pl.BlockSpec(memory_space=pltpu.VMEM)
    outs = [jax.ShapeDtypeStruct((8, R, Cn), F32)] + ([jax.ShapeDtypeStruct((R, Cn), F32)] if reduce else [])
    res = pl.pallas_call(
        body, name=name, in_specs=[vm], out_specs=[vm] * len(outs), out_shape=outs,
        scratch_shapes=[pltpu.SemaphoreType.DMA((7,)), pltpu.SemaphoreType.DMA((7,)), pltpu.SemaphoreType.DMA],
    )(x)
    return res if reduce else res[0]


def _prologue(small, w_ada, b_ada, w_shard, name):
    R, Cn = small.shape
    n_mod = w_ada.shape[1]
    big = _gather_in_one([w_shard])

    def body(small_ref, wada_ref, b_ref, shard_ref, small_all_ref, mods_all_ref, gathered_ref, mods_ref,
             send1, recv1, send2, recv2, local_sems, big_send, big_recv):
        big.start([shard_ref], [gathered_ref], big_send, big_recv)
        _gather8(small_ref, small_all_ref, send1, recv1, local_sems.at[0])
        c_all = jnp.concatenate([small_all_ref[d, 0:1, :] for d in range(8)], axis=0)
        c_act = c_all * _sigmoid(c_all)
        mods_ref[...] = _nn(c_act, wada_ref[...], precision=lax.Precision.HIGHEST) + b_ref[...]
        _gather8(mods_ref, mods_all_ref, send2, recv2, local_sems.at[1])
        big.switch([shard_ref], [gathered_ref], big_send, big_recv)
        big.finish([shard_ref], [gathered_ref], big_send, big_recv)

    vm = pl.BlockSpec(memory_space=pltpu.VMEM)
    seven = pltpu.SemaphoreType.DMA((7,))
    return pl.pallas_call(
        body, name=name, in_specs=[vm, vm, vm, ANY], out_specs=[vm, vm, ANY],
        out_shape=[jax.ShapeDtypeStruct((8, R, Cn), F32), jax.ShapeDtypeStruct((8, 8, n_mod), F32)] + big.out_shapes,
        scratch_shapes=[pltpu.VMEM((8, n_mod), F32), seven, seven, seven, seven, pltpu.SemaphoreType.DMA((2,))]
        + big.sem_shapes(),
        compiler_params=pltpu.CompilerParams(vmem_limit_bytes=VMEM_LIMIT_BYTES),
    )(small, w_ada, b_ada, w_shard)


def _remote(src, dst, send_sems, recv_sems, i, to):
    return pltpu.make_async_remote_copy(
        src_ref=src, dst_ref=dst, send_sem=send_sems.at[i], recv_sem=recv_sems.at[i], device_id=to, device_id_type=MESH)


def _symmetric_plan(copies):
    def plan(in_refs, out_refs, send_sems, recv_sems):
        sends = [_remote(src, dst, send_sems, recv_sems, i, to) for i, (src, dst, to) in enumerate(copies(in_refs, out_refs))]
        return sends, sends
    return plan


def _halves_exchange(grads):
    def copies(in_refs, out_refs):
        me = _place()
        return [(g.at[kk, 1 - me[2]], got.at[kk], _flip(me, 1)) for g, got in zip(in_refs, out_refs) for kk in range(N_CHIPS)]

    return _Exchange(grads, [jax.ShapeDtypeStruct((N_CHIPS,) + g.shape[2:], g.dtype) for g in grads],
                     N_CHIPS * len(grads), _symmetric_plan(copies))


def _chips_exchange(parts):
    def copies(in_refs, out_refs):
        me = _place()
        return [(p.at[_chip_index(_flip(me, 2 * j))], got.at[j - 1], _flip(me, 2 * j))
                for p, got in zip(in_refs, out_refs) for j in (1, 2, 3)]

    return _Exchange(parts, [jax.ShapeDtypeStruct((3,) + p.shape[1:], p.dtype) for p in parts], 3 * len(parts),
                     _symmetric_plan(copies))


def _siblings_exchange(halves):
    def copies(in_refs, out_refs):
        sibling = _flip(_place(), 1)
        return [(h, got, sibling) for h, got in zip(in_refs, out_refs)]

    return _Exchange(halves, [jax.ShapeDtypeStruct(h.shape, h.dtype) for h in halves], len(halves), _symmetric_plan(copies))


def _ici_gather_plan(n, base=0):
    def plan(in_refs, out_refs, send_sems, recv_sems):
        me = _place()
        c = me[2]
        sends, recvs = [], []
        for a, (w, out) in enumerate(zip(in_refs[:n], out_refs)):
            for j in (1, 2, 3):
                i = base + 3 * a + j - 1
                sends.append(_remote(w.at[c], out.at[_chip_index(me), c], send_sems, recv_sems, i, _flip(me, 2 * j)))
                z = out.at[_chip_index(_flip(me, 2 * j)), c]
                recvs.append(_remote(z, z, send_sems, recv_sems, i, me))
        return sends, recvs
    return plan


def _d2d_gather_plan(n, base=0):
    def plan(in_refs, out_refs, send_sems, recv_sems):
        me = _place()
        c = me[2]
        sibling = _flip(me, 1)
        mine = _chip_index(me)
        sends, recvs = [], []
        for a, (w, out) in enumerate(zip(in_refs[:n], out_refs)):
            moves = [(w.at[c], (mine, c)), (w.at[1 - c], (mine, 1 - c))]
            moves += [(out.at[_chip_index(_flip(me, 2 * j)), c], (_chip_index(_flip(me, 2 * j)), c)) for j in (1, 2, 3)]
            for k, (src, (chip, half)) in enumerate(moves):
                sends.append(_remote(src, out.at[chip, half], send_sems, recv_sems, base + 5 * a + k, sibling))
            lands = [(mine, 1 - c), (mine, c)] + [(_chip_index(_flip(me, 2 * j)), 1 - c) for j in (1, 2, 3)]
            for k, (chip, half) in enumerate(lands):
                z = out.at[chip, half]
                recvs.append(_remote(z, z, send_sems, recv_sems, base + 5 * a + k, me))
        return sends, recvs
    return plan


def _gathered_shapes(shards):
    return [jax.ShapeDtypeStruct((N_CHIPS,) + s.shape, s.dtype) for s in shards]


def _gather_over_ici(shards):
    return _Exchange(shards, _gathered_shapes(shards), 3 * len(shards), _ici_gather_plan(len(shards)))


def _gather_over_d2d(shards, gathered):
    n = len(shards)
    return _Exchange(list(shards) + list(gathered), [jax.ShapeDtypeStruct(g.shape, g.dtype) for g in gathered], 5 * n,
                     _d2d_gather_plan(n), aliases={n + a: a for a in range(n)})


def _gather_in_one(shards):
    n = len(shards)
    return _Exchange(shards, _gathered_shapes(shards), 8 * n, _ici_gather_plan(n), then=_d2d_gather_plan(n, base=3 * n))


def _row_tile(rows):
    for cand in (256, 176, 128, 64, 32, 16, 8):
        if rows % cand == 0:
            return cand
    return rows


def _pair_sum(core, grad, theirs, name):
    N, _, R, Cn = grad.shape
    tr = R

    def body(core_ref, g_ref, t_ref, o_ref, ob_ref):
        s = g_ref[...] + t_ref[...]
        o_ref[...] = s
        ob_ref[...] = s.astype(BF16)

    out = pl.BlockSpec((None, tr, Cn), lambda k, i, core_ref: (k, i, 0))
    return pl.pallas_call(
        body, name=name,
        grid_spec=pltpu.PrefetchScalarGridSpec(
            num_scalar_prefetch=1, grid=(N, R // tr),
            in_specs=[pl.BlockSpec((None, None, tr, Cn), lambda k, i, core_ref: (k, core_ref[0], i, 0)),
                      pl.BlockSpec((None, tr, Cn), lambda k, i, core_ref: (k, i, 0))],
            out_specs=[out, out]),
        out_shape=[jax.ShapeDtypeStruct((N, R, Cn), F32), jax.ShapeDtypeStruct((N, R, Cn), BF16)],
        compiler_params=_params("parallel", "parallel"),
    )(core, grad, theirs)


def _chip_sum(chip, parts, landed, name):
    _, R, Cn = parts.shape
    tr = R

    def body(chip_ref, p_ref, l_ref, o_ref):
        o_ref[...] = ((p_ref[...] + l_ref[0].astype(F32)) + l_ref[1].astype(F32)) + l_ref[2].astype(F32)

    return pl.pallas_call(
        body, name=name,
        grid_spec=pltpu.PrefetchScalarGridSpec(
            num_scalar_prefetch=1, grid=(R // tr,),
            in_specs=[pl.BlockSpec((None, tr, Cn), lambda i, chip_ref: (chip_ref[0], i, 0)),
                      pl.BlockSpec((3, tr, Cn), lambda i, chip_ref: (0, i, 0))],
            out_specs=pl.BlockSpec((tr, Cn), lambda i, chip_ref: (i, 0))),
        out_shape=jax.ShapeDtypeStruct((R, Cn), F32), compiler_params=_params("parallel"),
    )(chip, parts, landed)


def _pair_sums(core, grads, theirs, tag):
    return [_pair_sum(core, g, t, f"{tag}_pair_sum_{i}") for i, (g, t) in enumerate(zip(grads, theirs))]


def _chip_sums(chip, parts, landed, tag):
    return [_chip_sum(chip, p[0], l, f"{tag}_chip_sum_{i}") for i, (p, l) in enumerate(zip(parts, landed))]


def _by_chip_rows(g):
    return g.reshape(N_CHIPS, 2, g.shape[0] // (2 * N_CHIPS), g.shape[1])


def _by_chip_cols(g):
    return g.reshape(N_CHIPS, 2, g.shape[1] // 2, g.shape[2])


def _adamw_halves(core, w, g_mine, g_theirs, m, v, name):
    R2, Cn = w.shape
    r = R2 // 2
    tr = _row_tile(r)
    nt = r // tr

    def body(core_ref, w_ref, gm_ref, gt_ref, m_ref, v_ref, g_ref, d_ref, nm_ref, nv_ref):
        gv = jnp.where(pl.program_id(0) == core_ref[0], gm_ref[...], gt_ref[...])
        g_ref[...] = gv
        m_new = ADAM_B1 * m_ref[...] + (1.0 - ADAM_B1) * gv
        v_new = ADAM_B2 * v_ref[...] + (1.0 - ADAM_B2) * (gv * gv)
        m_hat = m_new / (1.0 - ADAM_B1 ** ADAM_STEP)
        v_hat = v_new / (1.0 - ADAM_B2 ** ADAM_STEP)
        d_ref[...] = -ADAM_LR * (m_hat / (jnp.sqrt(v_hat) + ADAM_EPS) + ADAM_WD * w_ref[...])
        nm_ref[...] = m_new
        nv_ref[...] = v_new

    full = pl.BlockSpec((tr, Cn), lambda hf, i, core_ref: (hf * nt + i, 0))
    half = pl.BlockSpec((tr, Cn), lambda hf, i, core_ref: (i, 0))
    shp = jax.ShapeDtypeStruct((R2, Cn), F32)
    return pl.pallas_call(
        body, name=name,
        grid_spec=pltpu.PrefetchScalarGridSpec(
            num_scalar_prefetch=1, grid=(2, nt), in_specs=[full, half, half, full, full], out_specs=[full] * 4),
        out_shape=[shp] * 4, compiler_params=_params("parallel", "parallel"),
    )(core, w, g_mine, g_theirs, m, v)


def _pad_row(v, width):
    v = v.reshape(1, -1)
    return jnp.pad(v, ((0, 0), (0, width - v.shape[1])))


def _ffn1_forward(x, ng, shift, scale, gate, w_in4, w_out_shard, gather, next_norm):
    h = _rmsmod_fwd(x, ng, shift, scale, "ffn1_norm")
    (zg, zu, a), (partly, (w_out4,)) = _ffn_in_fwd(
        h, w_in4, "ffn1_in", exchange=[_gather_over_ici(gather), _gather_in_one([w_out_shard])])
    w_out = w_out4.reshape(D_FF, D_MODEL)
    (x_new, f, h_next), gathered = _proj_out_fwd([a], w_out, x, gate, 0.5, "ffn1_out", next_norm=next_norm,
                                                 exchange=_gather_over_d2d(gather, partly))
    return x_new, (h, zg, zu, a, f), w_out, gathered, h_next


def _ffn_backward(df, saved, w_in4, w_out, core, chip, tag, riding=None, norm=None):
    h, zg, zu, a = saved[:4]
    rode = None
    if riding:
        (dzg, dzu), rode = _dact_bwd(df, w_out, zg, zu, f"{tag}_dact", exchange=riding)
    else:
        dzg, dzu = _dact_bwd(df, w_out, zg, zu, f"{tag}_dact")
    g_out = [_by_chip_rows(_wgrad(a, [df], df.shape[1], f"{tag}_dw_out")[0].reshape(a.shape[1], df.shape[1]))]
    (dw_in,), theirs_out = _wgrad(h, [dzg, dzu], FF_SHARD, f"{tag}_dw_in", exchange=_halves_exchange(g_out))
    g_in = [_by_chip_cols(dw_in.reshape(N_CHIPS, h.shape[1], FF_SHARD))]
    parts_out = _pair_sums(core, g_out, theirs_out, f"{tag}_out")
    dh_outs, (theirs_in, landed_out) = _ffn_in_dgrad(
        dzg, dzu, w_in4, f"{tag}_dh", norm=norm, exchange=[_halves_exchange(g_in), _chips_exchange([parts_out[0][1]])])
    parts_in = _pair_sums(core, g_in, theirs_in, f"{tag}_in")
    return dh_outs, parts_in, _chip_sums(chip, parts_out, landed_out, f"{tag}_out"), rode


def kernel(x, c, w_ada, b_ada, norm_g, w_ffn1_in, w_ffn1_out, w_ffn2_in, w_ffn2_out, w_mix_in, w_mix_out, hgrn_lb, hgrn_norm_g, qk_norm_g, attn_sink, rel_bias, loss_target, m_w_ada, m_b_ada, m_norm_g, m_w_ffn1_in, m_w_ffn1_out, m_w_ffn2_in, m_w_ffn2_out, m_w_mix_in, m_w_mix_out, m_hgrn_lb, m_hgrn_norm_g, m_qk_norm_g, m_attn_sink, m_rel_bias, v_w_ada, v_b_ada, v_norm_g, v_w_ffn1_in, v_w_ffn1_out, v_w_ffn2_in, v_w_ffn2_out, v_w_mix_in, v_w_mix_out, v_hgrn_lb, v_hgrn_norm_g, v_qk_norm_g, v_attn_sink, v_rel_bias):
    D = D_MODEL
    S = x.shape[1]
    place = (lax.axis_index("x"), lax.axis_index("y"), lax.axis_index("c"))
    me, my_chip = _dev_index(place), _chip_index(place)
    x0 = x[0]
    target = loss_target[0]

    def halves(w, tag):
        return _to_bf16(w[0], f"{tag}_to_bf16").reshape(2, w.shape[1] // 2, w.shape[2])

    w1_out_shard = halves(w_ffn1_out, "w_ffn1_out")
    mix_shards = [halves(w_mix_in, "w_mix_in"), halves(w_mix_out, "w_mix_out")]
    ffn2_shards = [halves(w_ffn2_in, "w_ffn2_in"), halves(w_ffn2_out, "w_ffn2_out")]
    core_arr = jnp.reshape(place[2], (1,)).astype(jnp.int32)
    chip_arr = jnp.reshape(my_chip, (1,)).astype(jnp.int32)

    small = jnp.concatenate([_pad_row(c, D), _pad_row(norm_g, D), _pad_row(hgrn_lb, D), jnp.zeros((5, D), F32)], axis=0)
    n_ada = w_ada.shape[2]
    b_mine = lax.dynamic_slice_in_dim(b_ada, my_chip * n_ada, n_ada, axis=1)
    small_all, mods_parts, w1_in4 = _prologue(small, w_ada[0], b_mine, halves(w_ffn1_in, "w_ffn1_in"), "prologue")
    w1_in = w1_in4.reshape(N_CHIPS, D, FF_SHARD)
    c_all = small_all[:, 0, :]
    by_chip = small_all[0::2]
    norm_g_full = by_chip[:, 1, :3 * 256].reshape(N_CHIPS, 3, 256).transpose(1, 0, 2).reshape(3, D)
    lb_raw = by_chip[:, 2, :2 * 2 * 128].reshape(N_CHIPS, 2, 2, 128).transpose(1, 2, 0, 3).reshape(2, 2, HG_WIDTH)
    lb_logit = lb_raw[:, 0, :] - lb_raw[:, 1, :]
    lb = jax.nn.sigmoid(lb_logit)
    one_minus_lb = jax.nn.sigmoid(-lb_logit)
    lb_f = jnp.stack([lb[0], one_minus_lb[0]])
    lb_b = jnp.stack([lb[1], one_minus_lb[1]])

    c_act_all = c_all * jax.nn.sigmoid(c_all)
    mods_all = mods_parts[0::2].transpose(1, 0, 2).reshape(8, N_MOD * D)
    mods = lax.dynamic_slice_in_dim(mods_all, me, 1, axis=0)
    sh1, sc1, g1, sh2, sc2, g2, sh3, sc3, g3 = [mods[:, i * D:(i + 1) * D] for i in range(N_MOD)]

    x1, saved1, w1_out, gathered, h2 = _ffn1_forward(x0, norm_g_full[0:1], sh1, sc1, g1, w1_in, w1_out_shard, mix_shards,
                                                     (norm_g_full[1:2], sh2, sc2))
    wm_in = gathered[0].reshape(N_CHIPS, D, D_IN // N_CHIPS).transpose(1, 0, 2).reshape(D, D_IN)
    wm_out = gathered[1].reshape(D, D)

    z = _matmul_nn(h2, wm_in, F32, 256, "mix_in")
    (of, st_f), partly = _hgrn_fwd(z, lb_f, 0, "hgrn_fwd_f", exchange=_gather_over_ici(ffn2_shards))
    (ob, st_b), gathered = _hgrn_fwd(z, lb_b, 1, "hgrn_fwd_b", exchange=_gather_over_d2d(ffn2_shards, partly))
    w2_in = gathered[0].reshape(N_CHIPS, D, FF_SHARD)
    w2_out = gathered[1].reshape(D_FF, D)
    o_h = _hgrn_post_fwd(of, ob, z, hgrn_norm_g, "hgrn_post")

    q_g, k_g = qk_norm_g[0, 0:1], qk_norm_g[0, 1:2]
    sink_b = jnp.broadcast_to(attn_sink.reshape(ATT_Q_HEADS, 1, 1), (ATT_Q_HEADS, 1, BLOCK))
    bias = _bias_table(rel_bias, "bias_table")
    o_a = _attn_fwd(z, q_g, k_g, sink_b, bias, "attn_fwd")
    x2, mixed, h3 = _proj_out_fwd([o_h, o_a], wm_out, x1, g2, 1.0, "mix_out", next_norm=(norm_g_full[2:3], sh3, sc3))

    zg3, zu3, a3 = _ffn_in_fwd(h3, w2_in, "ffn2_in")
    dx3, df3, dg3, sq_cols = _proj_out_loss(a3, w2_out, x2, g3, 0.5, target, "ffn2_out_loss")
    loss_mine = 0.5 * jnp.sum(sq_cols) / D

    (dx2, dsh3, dsc3, dng3, dmixed, dg2), parts2, mine2_out, _ = _ffn_backward(
        df3, (h3, zg3, zu3, a3), w2_in, w2_out, core_arr, chip_arr, "ffn2",
        norm=_NormBwd(x2, norm_g_full[2:3], sc3, dx3, below=(mixed, g2, 1.0)))

    (do_cat,) = _matmul_nt([dmixed], wm_out, ROW_TILE, "mix_out_dgrad")
    dwm_out = _wgrad_rows([o_h, o_a], dmixed, "mix_out_dw").reshape(D, D)

    do_sum, dgr, d_hnorm = _hgrn_post_bwd(do_cat, of, ob, z, hgrn_norm_g, "hgrn_post_bwd")
    (dq_f, dff, dv_f, doml_f), landed2 = _hgrn_bwd(z, lb_f, do_sum, st_f, 0, "hgrn_bwd_f",
                                                   exchange=_chips_exchange([p[1] for p in parts2]))
    mine2 = _chip_sums(chip_arr, parts2, landed2, "ffn2_in") + mine2_out
    (dhq, dfb, dhi, doml_b), theirs2 = _hgrn_bwd(z, lb_b, do_sum, st_b, 1, "hgrn_bwd_b", acc=(dq_f, dv_f),
                                                 exchange=_siblings_exchange(mine2))

    daq, dkw, dvw, ds_sum, dsink, dqg = _attn_bwd(z, q_g, k_g, sink_b, bias, do_cat, "attn_bwd")
    dkv, dkg = _attn_kv_reduce(dkw, dvw, z, k_g, "attn_kv_reduce")
    d_rel_bias = jnp.sum(_bias_grad(ds_sum, "bias_grad"), axis=-1).T
    dz = [dhq, dff, dfb, dhi, dgr, daq, dkv]
    dwm_in = _wgrad_pieces(h2, dz, 2 * KV_WIDTH, "mix_in_dw").transpose(1, 0, 2).reshape(D, D_IN)
    wide = D_IN // N_CHIPS
    grads_m = [_by_chip_cols(dwm_in.reshape(D, N_CHIPS, wide).transpose(1, 0, 2)), _by_chip_rows(dwm_out)]
    (dx1, dsh2, dsc2, dng2, df1, dg1), theirs_m = _matmul_nt(
        dz, wm_in, 256, "mix_in_dgrad", exchange=_halves_exchange(grads_m),
        norm=_NormBwd(x1, norm_g_full[1:2], sc2, dx2, below=(saved1[4], g1, 0.5)))
    parts_m = _pair_sums(core_arr, grads_m, theirs_m, "mix")

    (dh1,), parts1, mine1_out, landed_m = _ffn_backward(df1, saved1, w1_in, w1_out, core_arr, chip_arr, "ffn1",
                                                        riding=_chips_exchange([p[1] for p in parts_m]))
    mine_m = _chip_sums(chip_arr, parts_m, landed_m, "mix")
    (dx0, dsh1, dsc1, dng1), landed1 = _rmsmod_bwd(dh1, _NormBwd(x0, norm_g_full[0:1], sc1, dx1), "ffn1_norm_bwd",
                                                   exchange=_chips_exchange([p[1] for p in parts1]))
    mine1 = _chip_sums(chip_arr, parts1, landed1, "ffn1_in") + mine1_out
    theirs_1m = list(_run_exchange(_siblings_exchange(mine1 + mine_m), "siblings_exchange"))
    reduced = list(zip(mine1 + mine2 + mine_m, theirs_1m[:2] + list(theirs2) + theirs_1m[2:]))

    dlb = -jnp.concatenate([doml_f, doml_b], axis=0)
    dlb_raw = dlb * lb * one_minus_lb
    d_hgrn_lb = jnp.stack([dlb_raw, -dlb_raw], axis=1)
    d_qk = jnp.concatenate([jnp.sum(dqg, axis=0), jnp.sum(dkg, axis=0)], axis=0)
    dmods = jnp.concatenate([dsh1, dsc1, dg1, dsh2, dsc2, dg2, dsh3, dsc3, dg3], axis=0)
    packed = jnp.concatenate(
        [dmods, dng1, dng2, dng3, d_hgrn_lb.reshape(2, D), _pad_row(d_hnorm, D), _pad_row(d_qk, D),
         _pad_row(dsink[:, 0, 0], D), _pad_row(d_rel_bias, D), _pad_row(loss_mine, D)], axis=0)
    packed = jnp.pad(packed, ((0, 24 - packed.shape[0]), (0, 0)))
    packed_all, packed_sum = _allgather8(packed, "small_grads_allgather", reduce=True)
    dmods_all = packed_all[:, 0:N_MOD, :].reshape(8, N_MOD * D)
    g_b_ada = packed_sum[0:N_MOD].reshape(1, N_MOD * D)
    g_norm_full = packed_sum[9:12]
    g_norm_g = lax.dynamic_slice_in_dim(g_norm_full, my_chip * 256, 256, axis=1).reshape(1, 3, 256)
    g_hgrn_lb = lax.dynamic_slice_in_dim(packed_sum[12:14].reshape(2, 2, HG_WIDTH), my_chip * 128, 128, axis=2)
    g_hgrn_norm_g = packed_sum[14:15, :HG_WIDTH]
    g_qk_norm_g = packed_sum[15, :2 * ATT_HEAD_DIM].reshape(1, 2, ATT_HEAD_DIM)
    g_attn_sink = packed_sum[16:17, :ATT_Q_HEADS]
    g_rel_bias = packed_sum[17, :NUM_BUCKETS * ATT_Q_HEADS].reshape(NUM_BUCKETS, ATT_Q_HEADS)
    loss = packed_sum[18, 0]

    dm_mine = lax.dynamic_slice_in_dim(dmods_all, my_chip * n_ada, n_ada, axis=1)
    g_w_ada = _ada_wgrad(c_act_all.T, dm_mine, "ada_wgrad")[None]

    def big(w, g, m, v, name):
        d, nm, nv = _adamw(w[0], g[0], m[0], v[0], name)
        return d[None], nm[None], nv[None]

    def big_halves(w, g_pair, m, v, name):
        g, d, nm, nv = _adamw_halves(core_arr, w[0], g_pair[0], g_pair[1], m[0], v[0], name)
        return g[None], (d[None], nm[None], nv[None])

    g_w1_in, u_w1_in = big_halves(w_ffn1_in, reduced[0], m_w_ffn1_in, v_w_ffn1_in, "adamw_w_ffn1_in")
    g_w1_out, u_w1_out = big_halves(w_ffn1_out, reduced[1], m_w_ffn1_out, v_w_ffn1_out, "adamw_w_ffn1_out")
    g_w2_in, u_w2_in = big_halves(w_ffn2_in, reduced[2], m_w_ffn2_in, v_w_ffn2_in, "adamw_w_ffn2_in")
    g_w2_out, u_w2_out = big_halves(w_ffn2_out, reduced[3], m_w_ffn2_out, v_w_ffn2_out, "adamw_w_ffn2_out")
    g_wm_in, u_wm_in = big_halves(w_mix_in, reduced[4], m_w_mix_in, v_w_mix_in, "adamw_w_mix_in")
    g_wm_out, u_wm_out = big_halves(w_mix_out, reduced[5], m_w_mix_out, v_w_mix_out, "adamw_w_mix_out")

    smalls = [(b_ada, g_b_ada, m_b_ada, v_b_ada), (norm_g, g_norm_g, m_norm_g, v_norm_g), (hgrn_lb, g_hgrn_lb, m_hgrn_lb, v_hgrn_lb),
              (hgrn_norm_g, g_hgrn_norm_g, m_hgrn_norm_g, v_hgrn_norm_g), (qk_norm_g, g_qk_norm_g, m_qk_norm_g, v_qk_norm_g),
              (attn_sink, g_attn_sink, m_attn_sink, v_attn_sink), (rel_bias, g_rel_bias, m_rel_bias, v_rel_bias)]
    sizes = [t[0].size for t in smalls]
    total = sum(sizes)
    rows = -(-total // 128)
    rows = -(-rows // 8) * 8

    def pack(i):
        flat = jnp.concatenate([t[i].reshape(-1) for t in smalls])
        fill = 1.0 if i == 3 else 0.0
        return jnp.pad(flat, (0, rows * 128 - total), constant_values=fill).reshape(rows, 128)

    packed_out = _adamw(pack(0), pack(1), pack(2), pack(3), "adamw_small")

    def unpack(flat2d):
        flat = flat2d.reshape(-1)
        outs, off = [], 0
        for t, n in zip(smalls, sizes):
            outs.append(flat[off:off + n].reshape(t[0].shape))
            off += n
        return outs

    d_small, m_small, v_small = [unpack(t) for t in packed_out]

    upd = {
        "w_ada": big(w_ada, g_w_ada, m_w_ada, v_w_ada, "adamw_w_ada"),
        "w_ffn1_in": u_w1_in, "w_ffn1_out": u_w1_out, "w_ffn2_in": u_w2_in, "w_ffn2_out": u_w2_out,
        "w_mix_in": u_wm_in, "w_mix_out": u_wm_out,
    }
    small_names = ["b_ada", "norm_g", "hgrn_lb", "hgrn_norm_g", "qk_norm_g", "attn_sink", "rel_bias"]
    for i, nme in enumerate(small_names):
        upd[nme] = (d_small[i], m_small[i], v_small[i])
    grads = {
        "w_ada": g_w_ada, "b_ada": g_b_ada, "norm_g": g_norm_g, "w_ffn1_in": g_w1_in, "w_ffn1_out": g_w1_out,
        "w_ffn2_in": g_w2_in, "w_ffn2_out": g_w2_out, "w_mix_in": g_wm_in, "w_mix_out": g_wm_out, "hgrn_lb": g_hgrn_lb,
        "hgrn_norm_g": g_hgrn_norm_g, "qk_norm_g": g_qk_norm_g, "attn_sink": g_attn_sink, "rel_bias": g_rel_bias,
    }
    order = ["w_ada", "b_ada", "norm_g", "w_ffn1_in", "w_ffn1_out", "w_ffn2_in", "w_ffn2_out", "w_mix_in", "w_mix_out",
             "hgrn_lb", "hgrn_norm_g", "qk_norm_g", "attn_sink", "rel_bias"]
    return (loss, dx0[None], *[grads[k] for k in order], *[upd[k][0] for k in order], *[upd[k][1] for k in order],
            *[upd[k][2] for k in order])
```

```python
import functools
import math

import numpy as np
import jax
import jax.numpy as jnp
from jax import lax
from jax.experimental import pallas as pl
from jax.experimental.pallas import tpu as pltpu

F32, BF16 = jnp.float32, jnp.bfloat16

D_MODEL = 1024
D_FF = 2816
HG_HEADS, HG_DIM = 4, 128
HG_WIDTH = HG_HEADS * HG_DIM
ATT_Q_HEADS, ATT_KV_HEADS, ATT_HEAD_DIM = 8, 2, 64
ATT_GROUP = ATT_Q_HEADS // ATT_KV_HEADS
ATT_WIDTH = ATT_Q_HEADS * ATT_HEAD_DIM
KV_WIDTH = ATT_KV_HEADS * ATT_HEAD_DIM
WINDOW, BLOCK = 128, 128
NUM_BUCKETS, MAX_DISTANCE = 32, 128
N_MOD = 9
EPS = 1e-6
D_IN = 5 * HG_WIDTH + ATT_WIDTH + 2 * KV_WIDTH
ADAM_LR, ADAM_B1, ADAM_B2, ADAM_EPS, ADAM_WD, ADAM_STEP = 0.001, 0.9, 0.999, 1e-08, 0.01, 10

N_CHIPS = 4
FF_SHARD = 2 * D_FF // N_CHIPS
NEG = -1e30

VMEM_LIMIT_BYTES = 56 << 20
ROW_TILE = 512
HG_CHUNK = 16
HG_ROWS = 512

MESH = pl.DeviceIdType.MESH
ANY = pl.BlockSpec(memory_space=pl.ANY)


def _params(*sem):
    return pltpu.CompilerParams(dimension_semantics=sem, vmem_limit_bytes=VMEM_LIMIT_BYTES)


def _resident(shape, index_map):
    return pl.BlockSpec(shape, index_map, pipeline_mode=pl.Buffered(1))


def _dot(a, b, dims, precision=None):
    return lax.dot_general(a, b, (dims, ((), ())), precision=precision, preferred_element_type=F32)


def _nn(a, b, precision=None):
    return _dot(a, b, ((1,), (0,)), precision)


def _nt(a, b):
    return _dot(a, b, ((1,), (1,)))


def _tn(a, b):
    return _dot(a, b, ((0,), (0,)))


def _sigmoid(x):
    return jax.nn.sigmoid(x)


class _Exchange:
    def __init__(self, inputs, out_shapes, n_sems, plan, aliases=None, then=None):
        self.inputs, self.out_shapes, self.n_sems, self.plan, self.aliases = list(inputs), list(out_shapes), n_sems, plan, aliases or {}
        self.then = then

    def sem_shapes(self):
        return [pltpu.SemaphoreType.DMA((self.n_sems,)), pltpu.SemaphoreType.DMA((self.n_sems,))]

    @staticmethod
    def _copy(src, dst, i, to, send_sems, recv_sems):
        return pltpu.make_async_remote_copy(
            src_ref=src, dst_ref=dst, send_sem=send_sems.at[i], recv_sem=recv_sems.at[i], device_id=to, device_id_type=MESH)

    def _start(self, plan, in_refs, out_refs, send_sems, recv_sems):
        for src, dst, i, to in plan(in_refs, out_refs)[0]:
            self._copy(src, dst, i, to, send_sems, recv_sems).start()

    def _wait(self, plan, in_refs, out_refs, send_sems, recv_sems):
        sends, lands = plan(in_refs, out_refs)
        for zone, i in lands:
            self._copy(zone, zone, i, _place(), send_sems, recv_sems).wait_recv()
        for src, dst, i, to in sends:
            self._copy(src, dst, i, to, send_sems, recv_sems).wait_send()

    def start(self, *refs):
        self._start(self.plan, *refs)

    def switch(self, *refs):
        if self.then:
            self._wait(self.plan, *refs)
            self._start(self.then, *refs)

    def finish(self, *refs):
        self._wait(self.then or self.plan, *refs)


def _run_exchange(ex, name):
    n_in, n_out = len(ex.inputs), len(ex.out_shapes)

    def body(*refs):
        in_refs, out_refs, (send_sems, recv_sems) = refs[:n_in], refs[n_in:n_in + n_out], refs[n_in + n_out:]
        ex.start(in_refs, out_refs, send_sems, recv_sems)
        ex.switch(in_refs, out_refs, send_sems, recv_sems)
        ex.finish(in_refs, out_refs, send_sems, recv_sems)

    return pl.pallas_call(
        body, name=name, in_specs=[ANY] * n_in, out_specs=[ANY] * n_out, out_shape=ex.out_shapes,
        scratch_shapes=ex.sem_shapes(), input_output_aliases=dict(ex.aliases),
    )(*ex.inputs)


def _call(body, *, name, grid, in_specs, out_specs, out_shape, args, semantics, scratch_shapes=(), exchange=None):
    if exchange is None:
        return pl.pallas_call(
            body, name=name, grid=grid, in_specs=in_specs, out_specs=out_specs, out_shape=out_shape,
            scratch_shapes=list(scratch_shapes), compiler_params=_params(*semantics))(*args)
    exs = exchange if isinstance(exchange, (list, tuple)) else [exchange]
    n_in, n_out, n_scr = len(in_specs), len(out_specs), len(scratch_shapes)
    x_in, x_out = [len(ex.inputs) for ex in exs], [len(ex.out_shapes) for ex in exs]

    def take(refs, counts):
        groups = []
        for n in counts:
            groups.append(refs[:n])
            refs = refs[n:]
        return groups, refs

    def carrier(*refs):
        ins, refs = refs[:n_in], refs[n_in:]
        x_ins, refs = take(refs, x_in)
        outs, refs = refs[:n_out], refs[n_out:]
        x_outs, refs = take(refs, x_out)
        scr, refs = refs[:n_scr], refs[n_scr:]
        sems, _ = take(refs, [2] * len(exs))
        ids = [pl.program_id(a) for a in range(len(grid))]
        first = functools.reduce(jnp.logical_and, [i == 0 for i in ids])
        last = functools.reduce(jnp.logical_and, [i == g - 1 for i, g in zip(ids, grid)])
        step = functools.reduce(lambda acc, ig: acc * ig[1] + ig[0], zip(ids, grid), 0)

        @pl.when(first)
        def _():
            for ex, xi, xo, (send_sems, recv_sems) in zip(exs, x_ins, x_outs, sems):
                ex.start(xi, xo, send_sems, recv_sems)

        if any(ex.then for ex in exs):
            @pl.when(step == (3 * math.prod(grid)) // 4)
            def _():
                for ex, xi, xo, (send_sems, recv_sems) in zip(exs, x_ins, x_outs, sems):
                    ex.switch(xi, xo, send_sems, recv_sems)

        body(*ins, *outs, *scr)

        @pl.when(last)
        def _():
            for ex, xi, xo, (send_sems, recv_sems) in zip(exs, x_ins, x_outs, sems):
                ex.finish(xi, xo, send_sems, recv_sems)

    aliases, i0, o0 = {}, n_in, n_out
    for ex in exs:
        aliases.update({i0 + i: o0 + o for i, o in ex.aliases.items()})
        i0, o0 = i0 + len(ex.inputs), o0 + len(ex.out_shapes)
    res = pl.pallas_call(
        carrier, name=name, grid=grid, in_specs=list(in_specs) + [ANY] * sum(x_in),
        out_specs=list(out_specs) + [ANY] * sum(x_out),
        out_shape=list(out_shape) + [s for ex in exs for s in ex.out_shapes],
        scratch_shapes=list(scratch_shapes) + [s for ex in exs for s in ex.sem_shapes()],
        input_output_aliases=aliases, compiler_params=_params(*["arbitrary"] * len(grid)),
    )(*args, *[a for ex in exs for a in ex.inputs])
    x_res, _ = take(list(res[n_out:]), x_out)
    return list(res[:n_out]), (x_res if isinstance(exchange, (list, tuple)) else x_res[0])


def _rmsmod_fwd(x, g, shift, scale, name):
    S, D = x.shape
    tr = min(ROW_TILE, S)

    def body(x_ref, g_ref, sh_ref, sc_ref, h_ref):
        xv = x_ref[...]
        rstd = lax.rsqrt(jnp.mean(xv * xv, axis=-1, keepdims=True) + EPS)
        y = xv * rstd * g_ref[...]
        h_ref[...] = (y * (1.0 + sc_ref[...]) + sh_ref[...]).astype(h_ref.dtype)

    row = pl.BlockSpec((tr, D), lambda i: (i, 0))
    vec = pl.BlockSpec((1, D), lambda i: (0, 0))
    return pl.pallas_call(
        body, name=name, grid=(S // tr,), in_specs=[row, vec, vec, vec], out_specs=row,
        out_shape=jax.ShapeDtypeStruct((S, D), BF16), compiler_params=_params("parallel"),
    )(x, g, shift, scale)


class _NormBwd:
    def __init__(self, x, g, scale, dx_res, below=None):
        S, D = x.shape
        self.below, self.coef = below, (below[2] if below else None)
        self.inputs = [x, g, scale, dx_res] + ([below[0], below[1]] if below else [])
        vshape = jax.ShapeDtypeStruct((1, D), F32)
        self.out_shape = [jax.ShapeDtypeStruct((S, D), F32), vshape, vshape, vshape]
        if below:
            self.out_shape += [jax.ShapeDtypeStruct((S, D), BF16), vshape]

    def specs(self, tr, D):
        row = pl.BlockSpec((tr, D), lambda i: (i, 0))
        vec = pl.BlockSpec((1, D), lambda i: (0, 0))
        return ([row, vec, vec, row] + ([row, vec] if self.below else []),
                [row, vec, vec, vec] + ([row, vec] if self.below else []))

    def step(self, dhv, in_refs, out_refs):
        if self.below:
            x_ref, g_ref, sc_ref, dxr_ref, f_ref, gate_ref = in_refs
            dx_ref, dsh_ref, dsc_ref, dg_ref, df_ref, dgate_ref = out_refs
            sums = (dsh_ref, dsc_ref, dg_ref, dgate_ref)
        else:
            x_ref, g_ref, sc_ref, dxr_ref = in_refs
            dx_ref, dsh_ref, dsc_ref, dg_ref = out_refs
            sums = (dsh_ref, dsc_ref, dg_ref)

        @pl.when(pl.program_id(0) == 0)
        def _():
            for ref in sums:
                ref[...] = jnp.zeros_like(ref)

        xv, gv = x_ref[...], g_ref[...]
        one_sc = 1.0 + sc_ref[...]
        rstd = lax.rsqrt(jnp.mean(xv * xv, axis=-1, keepdims=True) + EPS)
        n = xv * rstd
        dsh_ref[...] += jnp.sum(dhv, axis=0, keepdims=True)
        dsc_ref[...] += jnp.sum(dhv * n, axis=0, keepdims=True) * gv
        dg_ref[...] += jnp.sum(dhv * n, axis=0, keepdims=True) * one_sc
        dn = dhv * (gv * one_sc)
        dx = dxr_ref[...] + rstd * (dn - n * jnp.mean(dn * n, axis=-1, keepdims=True))
        dx_ref[...] = dx
        if self.below:
            df_ref[...] = (self.coef * gate_ref[...] * dx).astype(df_ref.dtype)
            dgate_ref[...] += self.coef * jnp.sum(dx * f_ref[...].astype(F32), axis=0, keepdims=True)


def _rmsmod_bwd(dh, norm, name, exchange=None):
    S, D = dh.shape
    tr = min(ROW_TILE, S)
    n_in = len(norm.inputs)

    def body(dh_ref, *refs):
        norm.step(dh_ref[...], refs[:n_in], refs[n_in:])

    in_specs, out_specs = norm.specs(tr, D)
    return _call(body, name=name, grid=(S // tr,), in_specs=[pl.BlockSpec((tr, D), lambda i: (i, 0))] + in_specs,
                 out_specs=out_specs, out_shape=norm.out_shape, args=[dh] + norm.inputs, semantics=("arbitrary",),
                 exchange=exchange)


def _ffn_in_fwd(h, w4, name, exchange=None):
    S, D = h.shape
    tm = min(2 * ROW_TILE, S)
    n = w4.shape[2]

    def body(h_ref, wg_ref, wu_ref, zg_ref, zu_ref, a_ref):
        hv = h_ref[...]
        zg = _nn(hv, wg_ref[...])
        zu = _nn(hv, wu_ref[...])
        zg_ref[...] = zg.astype(zg_ref.dtype)
        zu_ref[...] = zu.astype(zu_ref.dtype)
        a_ref[...] = (zg * _sigmoid(zg) * zu).astype(a_ref.dtype)

    out = pl.BlockSpec((tm, n), lambda j, m: (m, j))
    oshape = jax.ShapeDtypeStruct((S, 2 * n), BF16)
    return _call(
        body, name=name, grid=(2, S // tm),
        in_specs=[pl.BlockSpec((tm, D), lambda j, m: (m, 0)),
                  pl.BlockSpec((None, D, n), lambda j, m: (j, 0, 0)),
                  pl.BlockSpec((None, D, n), lambda j, m: (j + 2, 0, 0))],
        out_specs=[out, out, out], out_shape=[oshape, oshape, oshape], args=(h, w4, w4),
        semantics=("parallel", "parallel"), exchange=exchange)


def _proj_out_fwd(lhs, w, x, gate, coef, name, exchange=None, next_norm=None):
    S, D = x.shape
    tm = min(ROW_TILE, S)
    ks = [a.shape[1] for a in lhs]

    def body(*refs):
        lhs_refs, refs = refs[:len(lhs)], refs[len(lhs):]
        if next_norm:
            w_ref, x_ref, gate_ref, g_ref, sh_ref, sc_ref, xn_ref, f_ref, h_ref = refs
        else:
            w_ref, x_ref, gate_ref, xn_ref, f_ref = refs
        acc, off = None, 0
        for a_ref, k in zip(lhs_refs, ks):
            part = _nn(a_ref[...], w_ref[off:off + k, :])
            acc = part if acc is None else acc + part
            off += k
        f_ref[...] = acc.astype(f_ref.dtype)
        xn = x_ref[...] + coef * gate_ref[...] * acc
        xn_ref[...] = xn
        if next_norm:
            rstd = lax.rsqrt(jnp.mean(xn * xn, axis=-1, keepdims=True) + EPS)
            h_ref[...] = (xn * rstd * g_ref[...] * (1.0 + sc_ref[...]) + sh_ref[...]).astype(h_ref.dtype)

    row = pl.BlockSpec((tm, D), lambda m: (m, 0))
    vec = pl.BlockSpec((1, D), lambda m: (0, 0))
    extra = list(next_norm) if next_norm else []
    return _call(
        body, name=name, grid=(S // tm,),
        in_specs=[pl.BlockSpec((tm, k), lambda m: (m, 0)) for k in ks]
        + [_resident(w.shape, lambda m: (0, 0)), row, vec] + [vec] * len(extra),
        out_specs=[row, row] + ([row] if next_norm else []),
        out_shape=[jax.ShapeDtypeStruct((S, D), F32), jax.ShapeDtypeStruct((S, D), BF16)]
        + ([jax.ShapeDtypeStruct((S, D), BF16)] if next_norm else []),
        args=(*lhs, w, x, gate, *extra), semantics=("parallel",), exchange=exchange)


def _proj_out_loss(lhs, w, x, gate, coef, target, name):
    S, D = x.shape
    tm = min(ROW_TILE, S)

    def body(a_ref, w_ref, x_ref, gate_ref, t_ref, dy_ref, df_ref, dgate_ref, sq_ref):
        @pl.when(pl.program_id(0) == 0)
        def _():
            dgate_ref[...] = jnp.zeros_like(dgate_ref)
            sq_ref[...] = jnp.zeros_like(sq_ref)

        f = _nn(a_ref[...], w_ref[...])
        gate = coef * gate_ref[...]
        err = x_ref[...] + gate * f - t_ref[...]
        sq_ref[...] += jnp.sum(err * err, axis=0, keepdims=True)
        dy = err * (1.0 / D)
        dy_ref[...] = dy
        df_ref[...] = (gate * dy).astype(df_ref.dtype)
        dgate_ref[...] += coef * jnp.sum(dy * f, axis=0, keepdims=True)

    row = pl.BlockSpec((tm, D), lambda m: (m, 0))
    vec = pl.BlockSpec((1, D), lambda m: (0, 0))
    vshape = jax.ShapeDtypeStruct((1, D), F32)
    return pl.pallas_call(
        body, name=name, grid=(S // tm,),
        in_specs=[pl.BlockSpec((tm, lhs.shape[1]), lambda m: (m, 0)), _resident(w.shape, lambda m: (0, 0)), row, vec, row],
        out_specs=[row, row, vec, vec],
        out_shape=[jax.ShapeDtypeStruct((S, D), F32), jax.ShapeDtypeStruct((S, D), BF16), vshape, vshape],
        compiler_params=_params("arbitrary"),
    )(lhs, w, x, gate, target)


def _matmul_nn(a, w, out_dtype, tm, name):
    S, K = a.shape
    N = w.shape[1]
    tm = min(tm, S)

    def body(a_ref, w_ref, o_ref):
        o_ref[...] = _nn(a_ref[...], w_ref[...]).astype(o_ref.dtype)

    return pl.pallas_call(
        body, name=name, grid=(S // tm,),
        in_specs=[pl.BlockSpec((tm, K), lambda m: (m, 0)), _resident((K, N), lambda m: (0, 0))],
        out_specs=pl.BlockSpec((tm, N), lambda m: (m, 0)), out_shape=jax.ShapeDtypeStruct((S, N), out_dtype),
        compiler_params=_params("parallel"),
    )(a, w)


def _dact_bwd(df, w_out, zg, zu, name, exchange=None):
    S, D = df.shape
    tm = min(ROW_TILE, S)
    n = w_out.shape[0] // 2

    def body(df_ref, w_ref, zg_ref, zu_ref, dzg_ref, dzu_ref):
        da = _nt(df_ref[...], w_ref[...]).astype(BF16)
        zg_v, zu_v = zg_ref[...], zu_ref[...]
        s = _sigmoid(zg_v)
        dzu_ref[...] = da * zg_v * s
        dzg_ref[...] = da * zu_v * (s * (1.0 + zg_v * (1.0 - s)))

    blk = pl.BlockSpec((tm, n), lambda j, m: (m, j))
    oshape = jax.ShapeDtypeStruct((S, 2 * n), BF16)
    return _call(
        body, name=name, grid=(2, S // tm),
        in_specs=[pl.BlockSpec((tm, D), lambda j, m: (m, 0)), pl.BlockSpec((n, D), lambda j, m: (j, 0)), blk, blk],
        out_specs=[blk, blk], out_shape=[oshape, oshape], args=(df, w_out, zg, zu), semantics=("parallel", "parallel"),
        exchange=exchange)


def _ffn_in_dgrad(dzg, dzu, w4, name, exchange=None, norm=None):
    S = dzg.shape[0]
    D, n = w4.shape[1], w4.shape[2]
    tm = min(ROW_TILE, S)
    n_norm = len(norm.inputs) if norm else 0

    def body(dzg_ref, dzu_ref, w_ref, *refs):
        acc = _nt(dzg_ref[:, 0:n], w_ref[0])
        acc += _nt(dzg_ref[:, n:2 * n], w_ref[1])
        acc += _nt(dzu_ref[:, 0:n], w_ref[2])
        acc += _nt(dzu_ref[:, n:2 * n], w_ref[3])
        if norm:
            norm.step(acc, refs[:n_norm], refs[n_norm:])
        else:
            refs[0][...] = acc

    blk = pl.BlockSpec((tm, 2 * n), lambda m: (m, 0))
    in_specs, args = [blk, blk, _resident(w4.shape, lambda m: (0, 0, 0))], [dzg, dzu, w4]
    out_specs, out_shape = [pl.BlockSpec((tm, D), lambda m: (m, 0))], [jax.ShapeDtypeStruct((S, D), F32)]
    if norm:
        norm_in, out_specs = norm.specs(tm, D)
        in_specs, args, out_shape = in_specs + norm_in, args + norm.inputs, norm.out_shape
    return _call(body, name=name, grid=(S // tm,), in_specs=in_specs, out_specs=out_specs, out_shape=out_shape, args=args,
                 semantics=("arbitrary",) if norm else ("parallel",), exchange=exchange)


def _matmul_nt(pieces, w, tm, name, exchange=None, norm=None):
    S = pieces[0].shape[0]
    ks = [p.shape[1] for p in pieces]
    N = w.shape[0]
    tm = min(tm, S)
    n_norm = len(norm.inputs) if norm else 0

    def body(*refs):
        p_refs, w_ref, refs = refs[:len(ks)], refs[len(ks)], refs[len(ks) + 1:]
        acc, off = None, 0
        for p_ref, k in zip(p_refs, ks):
            part = _nt(p_ref[...], w_ref[:, off:off + k])
            acc = part if acc is None else acc + part
            off += k
        if norm:
            norm.step(acc, refs[:n_norm], refs[n_norm:])
        else:
            refs[0][...] = acc

    in_specs = [pl.BlockSpec((tm, k), lambda m: (m, 0)) for k in ks] + [_resident(w.shape, lambda m: (0, 0))]
    args = list(pieces) + [w]
    out_specs, out_shape = [pl.BlockSpec((tm, N), lambda m: (m, 0))], [jax.ShapeDtypeStruct((S, N), F32)]
    if norm:
        norm_in, out_specs = norm.specs(tm, N)
        in_specs, args, out_shape = in_specs + norm_in, args + norm.inputs, norm.out_shape
    return _call(body, name=name, grid=(S // tm,), in_specs=in_specs, out_specs=out_specs, out_shape=out_shape, args=args,
                 semantics=("arbitrary",) if norm else ("parallel",), exchange=exchange)


def _wgrad(a, gs, tn, name, exchange=None):
    S, Ka = a.shape
    N = gs[0].shape[1]
    ts = min(ROW_TILE * (2 if Ka <= D_MODEL else 1), S)

    def body(a_ref, *refs):
        g_refs, o_ref = refs[:-1], refs[-1]

        @pl.when(pl.program_id(1) == 0)
        def _():
            o_ref[...] = jnp.zeros_like(o_ref)

        a_t = a_ref[...].T
        for i, g_ref in enumerate(g_refs):
            o_ref[i] += _nn(a_t, g_ref[...])

    return _call(
        body, name=name, grid=(N // tn, S // ts),
        in_specs=[pl.BlockSpec((ts, Ka), lambda j, s: (s, 0))] + [pl.BlockSpec((ts, tn), lambda j, s: (s, j))] * len(gs),
        out_specs=[pl.BlockSpec((len(gs), None, Ka, tn), lambda j, s: (0, j, 0, 0))],
        out_shape=[jax.ShapeDtypeStruct((len(gs), N // tn, Ka, tn), F32)], args=(a, *gs),
        semantics=("parallel", "arbitrary"), exchange=exchange)


def _wgrad_pieces(a, pieces, tn, name):
    S, Ka = a.shape
    ts = min(ROW_TILE, S)
    blocks = [(i, j) for i, p in enumerate(pieces) for j in range(p.shape[1] // tn)]

    def body(a_ref, *refs):
        g_refs, o_ref = refs[:-1], refs[-1]

        @pl.when(pl.program_id(0) == 0)
        def _():
            o_ref[...] = jnp.zeros_like(o_ref)

        a_t = a_ref[...].T
        for b, g_ref in enumerate(g_refs):
            o_ref[b] += _nn(a_t, g_ref[...])

    return pl.pallas_call(
        body, name=name, grid=(S // ts,),
        in_specs=[pl.BlockSpec((ts, Ka), lambda s: (s, 0))] + [pl.BlockSpec((ts, tn), lambda s, j=j: (s, j)) for _, j in blocks],
        out_specs=pl.BlockSpec((len(blocks), Ka, tn), lambda s: (0, 0, 0)),
        out_shape=jax.ShapeDtypeStruct((len(blocks), Ka, tn), F32), compiler_params=_params("arbitrary"),
    )(a, *[pieces[i] for i, _ in blocks])


def _wgrad_rows(lhs, g, name):
    S, Ka = lhs[0].shape
    N = g.shape[1]
    ts = min(ROW_TILE, S)

    def body(*refs):
        a_refs, g_ref, o_ref = refs[:-2], refs[-2], refs[-1]

        @pl.when(pl.program_id(0) == 0)
        def _():
            o_ref[...] = jnp.zeros_like(o_ref)

        gv = g_ref[...]
        for i, a_ref in enumerate(a_refs):
            o_ref[i] += _tn(a_ref[...], gv)

    return pl.pallas_call(
        body, name=name, grid=(S // ts,),
        in_specs=[pl.BlockSpec((ts, Ka), lambda s: (s, 0))] * len(lhs) + [pl.BlockSpec((ts, N), lambda s: (s, 0))],
        out_specs=pl.BlockSpec((len(lhs), Ka, N), lambda s: (0, 0, 0)),
        out_shape=jax.ShapeDtypeStruct((len(lhs), Ka, N), F32), compiler_params=_params("arbitrary"),
    )(*lhs, g)


def _hgrn_chunk_common(qr, fr, lb, oml, tri, last):
    sig_nf = _sigmoid(-fr)
    k = oml * sig_nf
    f_small = lb + oml * (jnp.exp(jnp.minimum(fr, 0.0)) * sig_nf)
    use_k = k < 0.5
    f = jnp.where(use_k, 1.0 - k, f_small)
    g = jnp.where(use_k, jnp.log1p(-k), jnp.log(f_small)) * math.log2(math.e)
    q = qr * _sigmoid(qr)
    G = _nn(tri, g, precision=lax.Precision.HIGHEST)
    Gl = G[last:last + 1]
    return q, k, f, G, Gl


def _hgrn_consts(reverse):
    C = HG_CHUNK
    r = lax.broadcasted_iota(jnp.int32, (C, C), 0)
    cc = lax.broadcasted_iota(jnp.int32, (C, C), 1)
    tri = ((cc >= r) if reverse else (cc <= r)).astype(F32)
    tri_t = ((cc <= r) if reverse else (cc >= r)).astype(F32)
    rid = lax.broadcasted_iota(jnp.int32, (C, HG_WIDTH), 0)
    return tri, tri_t, rid, (0 if reverse else C - 1)


def _head_slices():
    return [slice(h * HG_DIM, (h + 1) * HG_DIM) for h in range(HG_HEADS)]


def _per_head_lane_sum(x):
    C = x.shape[0]
    return jnp.concatenate(
        [jnp.broadcast_to(jnp.sum(x[:, sl], axis=-1, keepdims=True), (C, HG_DIM)) for sl in _head_slices()], axis=1)


HG_TILE = 8


def _pair_tiles(s, reverse):
    blk, r = divmod(s, HG_TILE)
    n_tiles = HG_CHUNK // HG_TILE
    others = range(0, blk) if reverse else range(blk + 1, n_tiles)
    return [(blk, r)] + [(t, None) for t in others]


def _pair_decay(G, s, tile, r, rid8, reverse, keys=False):
    rs = slice(tile * HG_TILE, (tile + 1) * HG_TILE)
    d = (G[s:s + 1] - G[rs]) if keys else (G[rs] - G[s:s + 1])
    if r is not None:
        d = jnp.where((rid8 <= r) if reverse else (rid8 >= r), d, NEG)
    return rs, jnp.exp2(d)


def _hgrn_fwd(z, lb, direction, name, exchange=None):
    S = z.shape[0]
    C, DK, W = HG_CHUNK, HG_DIM, HG_WIDTH
    tb = min(HG_ROWS, S)
    n_t, n_c = S // tb, tb // C
    reverse = direction == 1
    tmap = (lambda i: n_t - 1 - i) if reverse else (lambda i: i)

    def body(q_ref, f_ref, v_ref, lb_ref, o_ref, st_out_ref, st_ref):
        @pl.when(pl.program_id(0) == 0)
        def _():
            st_ref[...] = jnp.zeros_like(st_ref)

        lbv, oml = lb_ref[0:1, :], lb_ref[1:2, :]
        tri, _, _, last = _hgrn_consts(reverse)
        rid8 = lax.broadcasted_iota(jnp.int32, (HG_TILE, W), 0)

        def chunk(ci, carry):
            cidx = (n_c - 1 - ci) if reverse else ci
            rows = pl.ds(pl.multiple_of(cidx * C, C), C)
            v = v_ref[rows, :]
            q, k, _, G, Gl = _hgrn_chunk_common(q_ref[rows, :], f_ref[rows, :], lbv, oml, tri, last)
            qd = (q * jnp.exp2(G)).astype(BF16)
            kd = (k * jnp.exp2(Gl - G)).astype(BF16)
            e_gl = jnp.exp2(Gl)
            v_b = v.astype(BF16)
            inter = []
            for h, sl in enumerate(_head_slices()):
                st0 = st_ref[h]
                st_out_ref[h, cidx] = st0
                inter.append(_nt(qd[:, sl], st0.astype(BF16)))
                st_ref[h] = st0 * e_gl[:, sl] + _tn(v_b[:, sl], kd[:, sl])
            o = jnp.concatenate(inter, axis=1)
            o_t = [o[t * HG_TILE:(t + 1) * HG_TILE] for t in range(C // HG_TILE)]
            for s in range(C):
                k_s, v_s = k[s:s + 1], v[s:s + 1]
                for tile, r in _pair_tiles(s, reverse):
                    rs, e_s = _pair_decay(G, s, tile, r, rid8, reverse)
                    o_t[tile] = o_t[tile] + _per_head_lane_sum(q[rs] * k_s * e_s) * v_s
            o_ref[rows, :] = jnp.concatenate(o_t, axis=0)
            return carry

        lax.fori_loop(0, n_c, chunk, 0, unroll=8)

    def sec(j):
        return pl.BlockSpec((tb, W), lambda i: (tmap(i), j))

    return _call(
        body, name=name, grid=(n_t,),
        in_specs=[sec(0), sec(1 + direction), sec(3), pl.BlockSpec((2, W), lambda i: (0, 0))],
        out_specs=[sec(0), pl.BlockSpec((HG_HEADS, n_c, DK, DK), lambda i: (0, tmap(i), 0, 0))],
        out_shape=[jax.ShapeDtypeStruct((S, W), F32), jax.ShapeDtypeStruct((HG_HEADS, S // C, DK, DK), F32)],
        scratch_shapes=[pltpu.VMEM((HG_HEADS, DK, DK), F32)], args=(z, z, z, lb), semantics=("arbitrary",),
        exchange=exchange)


def _hgrn_bwd(z, lb, do, states, direction, name, acc=None, exchange=None):
    S = z.shape[0]
    C, DK, W = HG_CHUNK, HG_DIM, HG_WIDTH
    tb = min(HG_ROWS, S)
    n_t, n_c = S // tb, tb // C
    reverse = direction == 1
    tmap = (lambda i: i) if reverse else (lambda i: n_t - 1 - i)

    def body(*refs):
        if acc:
            q_ref, f_ref, v_ref, lb_ref, do_ref, st_in_ref, dqa_ref, dva_ref, dq_ref, df_ref, dv_ref, doml_ref, dst_ref = refs
        else:
            q_ref, f_ref, v_ref, lb_ref, do_ref, st_in_ref, dq_ref, df_ref, dv_ref, doml_ref, dst_ref = refs

        @pl.when(pl.program_id(0) == 0)
        def _():
            dst_ref[...] = jnp.zeros_like(dst_ref)
            doml_ref[...] = jnp.zeros_like(doml_ref)

        lbv, oml = lb_ref[0:1, :], lb_ref[1:2, :]
        tri, tri_t, rid, last = _hgrn_consts(reverse)
        rid8 = lax.broadcasted_iota(jnp.int32, (HG_TILE, W), 0)

        def chunk(ci, carry):
            cidx = ci if reverse else (n_c - 1 - ci)
            rows = pl.ds(pl.multiple_of(cidx * C, C), C)
            qr, fr, v, dov = q_ref[rows, :], f_ref[rows, :], v_ref[rows, :], do_ref[rows, :]
            q, k, f, G, Gl = _hgrn_chunk_common(qr, fr, lbv, oml, tri, last)
            e_g, e_gl, e_kd = jnp.exp2(G), jnp.exp2(Gl), jnp.exp2(Gl - G)
            qd, kd = q * e_g, k * e_kd
            do_b, v_b, qd_b, kd_b = dov.astype(BF16), v.astype(BF16), qd.astype(BF16), kd.astype(BF16)
            dqd, dkd, dv, state_dot = [], [], [], []
            for h, sl in enumerate(_head_slices()):
                st0, dst1 = st_in_ref[h, cidx], dst_ref[h]
                dst1_b = dst1.astype(BF16)
                dqd.append(_nn(do_b[:, sl], st0.astype(BF16)))
                dkd.append(_nn(v_b[:, sl], dst1_b))
                dv.append(_nt(kd_b[:, sl], dst1_b))
                state_dot.append(jnp.sum(st0 * dst1, axis=0, keepdims=True))
                dst_ref[h] = dst1 * e_gl[:, sl] + _tn(do_b[:, sl], qd_b[:, sl])
            dqd, dkd, dv = [jnp.concatenate(t, axis=1) for t in (dqd, dkd, dv)]
            d_gl = e_gl * jnp.concatenate(state_dot, axis=1) + jnp.sum(dkd * kd, axis=0, keepdims=True)
            dq, dk = dqd * e_g, dkd * e_kd
            n_tiles = C // HG_TILE
            dq_t, dk_t, dv_t = [[x[t * HG_TILE:(t + 1) * HG_TILE] for t in range(n_tiles)] for x in (dq, dk, dv)]
            for s in range(C):
                k_s, v_s = k[s:s + 1], v[s:s + 1]
                for tile, r in _pair_tiles(s, reverse):
                    rs, e_s = _pair_decay(G, s, tile, r, rid8, reverse)
                    dq_t[tile] = dq_t[tile] + _per_head_lane_sum(dov[rs] * v_s) * e_s * k_s
            for t in range(C):
                q_t, do_t = q[t:t + 1], dov[t:t + 1]
                for tile, r in _pair_tiles(t, not reverse):
                    rs, x_t = _pair_decay(G, t, tile, r, rid8, not reverse, keys=True)
                    qx = q_t * x_t
                    dv_t[tile] = dv_t[tile] + _per_head_lane_sum(k[rs] * qx) * do_t
                    dk_t[tile] = dk_t[tile] + _per_head_lane_sum(v[rs] * do_t) * qx
            dq, dk, dv = [jnp.concatenate(x, axis=0) for x in (dq_t, dk_t, dv_t)]
            d_big_g = dq * q - dk * k + jnp.where(rid == last, d_gl, 0.0)
            dg = _nn(tri_t, d_big_g, precision=lax.Precision.HIGHEST)
            dk_all = dk - dg / f
            sig_nf = _sigmoid(-fr)
            df_ref[rows, :] = (-dk_all * k * (1.0 - sig_nf)).astype(df_ref.dtype)
            doml_ref[...] += jnp.sum(dk_all * sig_nf, axis=0, keepdims=True)
            sq = _sigmoid(qr)
            dqr = dq * (sq * (1.0 + qr * (1.0 - sq)))
            if acc:
                dqr = dqr + dqa_ref[rows, :]
                dv = dv + dva_ref[rows, :]
            dq_ref[rows, :] = dqr.astype(dq_ref.dtype)
            dv_ref[rows, :] = dv.astype(dv_ref.dtype)
            return carry

        lax.fori_loop(0, n_c, chunk, 0, unroll=8)

    def sec(j):
        return pl.BlockSpec((tb, W), lambda i: (tmap(i), j))

    vec = pl.BlockSpec((1, W), lambda i: (0, 0))
    ins = [z, z, z, lb, do, states]
    in_specs = [sec(0), sec(1 + direction), sec(3), pl.BlockSpec((2, W), lambda i: (0, 0)), sec(0),
                pl.BlockSpec((HG_HEADS, n_c, DK, DK), lambda i: (0, tmap(i), 0, 0))]
    if acc:
        ins += list(acc)
        in_specs += [sec(0), sec(0)]
    final = jax.ShapeDtypeStruct((S, W), BF16)
    partial = final if acc else jax.ShapeDtypeStruct((S, W), F32)
    return _call(
        body, name=name, grid=(n_t,), in_specs=in_specs,
        out_specs=[sec(0), sec(0), sec(0), vec],
        out_shape=[partial, final, partial, jax.ShapeDtypeStruct((1, W), F32)],
        scratch_shapes=[pltpu.VMEM((HG_HEADS, DK, DK), F32)], args=ins, semantics=("arbitrary",), exchange=exchange)


def _hgrn_post_fwd(o_f, o_b, z, norm_g, name):
    S = z.shape[0]
    tr = min(ROW_TILE, S)

    def body(of_ref, ob_ref, gr_ref, ng_ref, y_ref):
        o = of_ref[...] + ob_ref[...]
        gr = gr_ref[...]
        gate = gr * _sigmoid(gr)
        ng = ng_ref[...]
        for h in range(HG_HEADS):
            sl = slice(h * HG_DIM, (h + 1) * HG_DIM)
            oh = o[:, sl]
            rstd = lax.rsqrt(jnp.mean(oh * oh, axis=-1, keepdims=True) + EPS)
            y_ref[:, sl] = (oh * rstd * ng[:, sl] * gate[:, sl]).astype(y_ref.dtype)

    row = pl.BlockSpec((tr, HG_WIDTH), lambda i: (i, 0))
    return pl.pallas_call(
        body, name=name, grid=(S // tr,),
        in_specs=[row, row, pl.BlockSpec((tr, HG_WIDTH), lambda i: (i, 4)), pl.BlockSpec((1, HG_WIDTH), lambda i: (0, 0))],
        out_specs=row, out_shape=jax.ShapeDtypeStruct((S, HG_WIDTH), BF16), compiler_params=_params("parallel"),
    )(o_f, o_b, z, norm_g)


def _hgrn_post_bwd(dy, o_f, o_b, z, norm_g, name):
    S = z.shape[0]
    tr = min(ROW_TILE, S)

    def body(dy_ref, of_ref, ob_ref, gr_ref, ng_ref, do_ref, dgr_ref, dng_ref):
        @pl.when(pl.program_id(0) == 0)
        def _():
            dng_ref[...] = jnp.zeros_like(dng_ref)

        o = of_ref[...] + ob_ref[...]
        gr, ng, dyv = gr_ref[...], ng_ref[...], dy_ref[...]
        sg = _sigmoid(gr)
        for h in range(HG_HEADS):
            sl = slice(h * HG_DIM, (h + 1) * HG_DIM)
            oh, dyh, grh, sgh, ngh = o[:, sl], dyv[:, sl], gr[:, sl], sg[:, sl], ng[:, sl]
            rstd = lax.rsqrt(jnp.mean(oh * oh, axis=-1, keepdims=True) + EPS)
            on = oh * rstd
            du = dyh * (grh * sgh)
            dgr_ref[:, sl] = (dyh * (on * ngh) * (sgh * (1.0 + grh * (1.0 - sgh)))).astype(dgr_ref.dtype)
            dng_ref[:, sl] += jnp.sum(du * on, axis=0, keepdims=True)
            don = du * ngh
            do_ref[:, sl] = rstd * (don - on * jnp.mean(don * on, axis=-1, keepdims=True))

    row = pl.BlockSpec((tr, HG_WIDTH), lambda i: (i, 0))
    vec = pl.BlockSpec((1, HG_WIDTH), lambda i: (0, 0))
    full = jax.ShapeDtypeStruct((S, HG_WIDTH), F32)
    return pl.pallas_call(
        body, name=name, grid=(S // tr,),
        in_specs=[row, row, row, pl.BlockSpec((tr, HG_WIDTH), lambda i: (i, 4)), vec],
        out_specs=[row, row, vec],
        out_shape=[full, jax.ShapeDtypeStruct((S, HG_WIDTH), BF16), jax.ShapeDtypeStruct((1, HG_WIDTH), F32)],
        compiler_params=_params("arbitrary"),
    )(dy, o_f, o_b, z, norm_g)


def _t5_bucket_table():
    rel = (np.arange(3 * BLOCK)[None, :] - BLOCK) - np.arange(BLOCK)[:, None]
    nb = NUM_BUCKETS // 2
    max_exact = nb // 2
    ret = (rel > 0).astype(np.int32) * nb
    n = np.abs(rel)
    ratio = np.log(np.maximum(n, 1).astype(np.float32) / np.float32(max_exact)) / np.float32(math.log(MAX_DISTANCE / max_exact))
    large = max_exact + (ratio.astype(np.float32) * np.float32(nb - max_exact)).astype(np.int32)
    large = np.minimum(large, nb - 1)
    bucket = ret + np.where(n < max_exact, n, large)
    return bucket.astype(np.int32), (n <= WINDOW)


def _bias_table(rel_bias, name):
    bucket, in_band = _t5_bucket_table()
    idx = jnp.asarray(np.where(in_band, bucket, -1))

    def body(rb_ref, idx_ref, o_ref):
        h = pl.program_id(0)
        iv = idx_ref[...]
        acc = jnp.where(iv < 0, NEG, 0.0).astype(F32)
        for b in range(NUM_BUCKETS):
            acc = acc + jnp.where(iv == b, rb_ref[b, h], 0.0)
        o_ref[...] = acc

    return pl.pallas_call(
        body, name=name, grid=(ATT_Q_HEADS,),
        in_specs=[pl.BlockSpec(memory_space=pltpu.SMEM), pl.BlockSpec((BLOCK, 3 * BLOCK), lambda h: (0, 0))],
        out_specs=pl.BlockSpec((None, BLOCK, 3 * BLOCK), lambda h: (h, 0, 0)),
        out_shape=jax.ShapeDtypeStruct((ATT_Q_HEADS, BLOCK, 3 * BLOCK), F32), compiler_params=_params("parallel"),
    )(rel_bias, idx)


def _bias_grad(ds_sum_t, name):
    bucket, in_band = _t5_bucket_table()
    idx_t = jnp.asarray(np.where(in_band, bucket, -1).T)

    def body(ds_ref, idx_ref, o_ref):
        iv, ds = idx_ref[...], ds_ref[...]
        for b in range(NUM_BUCKETS):
            o_ref[b:b + 1, :] = jnp.sum(jnp.where(iv == b, ds, 0.0), axis=0, keepdims=True)

    return pl.pallas_call(
        body, name=name, grid=(ATT_Q_HEADS,),
        in_specs=[pl.BlockSpec((None, 3 * BLOCK, BLOCK), lambda h: (h // ATT_GROUP, 0, h % ATT_GROUP)),
                  pl.BlockSpec((3 * BLOCK, BLOCK), lambda h: (0, 0))],
        out_specs=pl.BlockSpec((None, NUM_BUCKETS, BLOCK), lambda h: (h, 0, 0)),
        out_shape=jax.ShapeDtypeStruct((ATT_Q_HEADS, NUM_BUCKETS, BLOCK), F32), compiler_params=_params("parallel"),
    )(ds_sum_t, idx_t)


Q_COL = 5 * HG_WIDTH
KV_COL = Q_COL + ATT_WIDTH
GROUP_WIDTH = ATT_GROUP * ATT_HEAD_DIM


def _stack_heads(blk):
    dh = ATT_HEAD_DIM
    return jnp.concatenate([blk[:, g * dh:(g + 1) * dh] for g in range(ATT_GROUP)], axis=0)


def _unstack_heads(st):
    return jnp.concatenate([st[g * BLOCK:(g + 1) * BLOCK] for g in range(ATT_GROUP)], axis=1)


def _rms_rows(x):
    rstd = lax.rsqrt(jnp.mean(x * x, axis=-1, keepdims=True) + EPS)
    return x * rstd, rstd


def _edge_ok(n, nb):
    colid = lax.broadcasted_iota(jnp.int32, (ATT_GROUP * BLOCK, 3 * BLOCK), 1)
    return jnp.logical_and(jnp.logical_or(colid >= BLOCK, n > 0), jnp.logical_or(colid < 2 * BLOCK, n < nb - 1))


def _sink_column(sink_ref, j=0):
    heads = range(j * ATT_GROUP, (j + 1) * ATT_GROUP)
    return jnp.concatenate([jnp.broadcast_to(sink_ref[h][:, 0:1], (BLOCK, 1)) for h in heads], axis=0)


def _attn_fwd(z, q_g, k_g, sink, bias, name):
    S = z.shape[0]
    nb = S // BLOCK
    G, dh, KV = ATT_GROUP, ATT_HEAD_DIM, ATT_KV_HEADS
    scale = 1.0 / math.sqrt(dh)

    def body(q_ref, kv0, kv1, kv2, qg_ref, kg_ref, sink_ref, bias_ref, o_ref):
        n = pl.program_id(0)
        edge_ok = _edge_ok(n, nb)
        cat = jnp.concatenate([kv0[...], kv1[...], kv2[...]], axis=0)
        qblk = q_ref[...]
        kn = [(_rms_rows(cat[:, j * dh:(j + 1) * dh])[0] * kg_ref[...]).astype(BF16) for j in range(KV)]
        vb = [cat[:, (KV + j) * dh:(KV + j + 1) * dh].astype(BF16) for j in range(KV)]
        qn = [(_rms_rows(_stack_heads(qblk[:, j * GROUP_WIDTH:(j + 1) * GROUP_WIDTH]))[0] * (qg_ref[...] * scale)).astype(BF16)
              for j in range(KV)]
        s = [_nt(qn[j], kn[j]) + bias_ref[j * G:(j + 1) * G].reshape(G * BLOCK, 3 * BLOCK) for j in range(KV)]
        s = [jnp.where(edge_ok, sj, NEG) for sj in s]
        sinks = [_sink_column(sink_ref, j) for j in range(KV)]
        m = [jnp.maximum(jnp.max(s[j], axis=-1, keepdims=True), sinks[j]) for j in range(KV)]
        e = [jnp.exp(s[j] - m[j]) for j in range(KV)]
        den = [jnp.sum(e[j], axis=-1, keepdims=True) + jnp.exp(sinks[j] - m[j]) for j in range(KV)]
        o = [_nn(e[j].astype(BF16), vb[j]) * (1.0 / den[j]) for j in range(KV)]
        o_ref[...] = jnp.concatenate([_unstack_heads(oj) for oj in o], axis=1).astype(o_ref.dtype)

    def kv(shift):
        return pl.BlockSpec((BLOCK, 2 * KV_WIDTH), lambda n: (jnp.clip(n + shift, 0, nb - 1), KV_COL // (2 * KV_WIDTH)))

    gain = pl.BlockSpec((1, dh), lambda n: (0, 0))
    return pl.pallas_call(
        body, name=name, grid=(nb,),
        in_specs=[pl.BlockSpec((BLOCK, ATT_WIDTH), lambda n: (n, Q_COL // ATT_WIDTH)), kv(-1), kv(0), kv(1), gain, gain,
                  pl.BlockSpec((ATT_Q_HEADS, 1, BLOCK), lambda n: (0, 0, 0)),
                  pl.BlockSpec((ATT_Q_HEADS, BLOCK, 3 * BLOCK), lambda n: (0, 0, 0))],
        out_specs=pl.BlockSpec((BLOCK, ATT_WIDTH), lambda n: (n, 0)),
        out_shape=jax.ShapeDtypeStruct((S, ATT_WIDTH), BF16), compiler_params=_params("parallel"),
    )(z, z, z, z, q_g, k_g, sink, bias)


def _attn_bwd(z, q_g, k_g, sink, bias, do, name):
    S = z.shape[0]
    nb = S // BLOCK
    G, dh, KV = ATT_GROUP, ATT_HEAD_DIM, ATT_KV_HEADS
    scale = 1.0 / math.sqrt(dh)
    both = range(KV)
    bias_t = bias.reshape(KV, G, BLOCK, 3 * BLOCK).transpose(0, 3, 1, 2).reshape(KV, 3 * BLOCK, G * BLOCK)

    def body(q_ref, kv0, kv1, kv2, qg_ref, kg_ref, sink_ref, bias_ref, do_ref,
             dq_ref, dkw_ref, dvw_ref, ds_ref, dsink_ref, dqg_ref):
        n = pl.program_id(0)

        @pl.when(n == 0)
        def _():
            ds_ref[...] = jnp.zeros_like(ds_ref)
            dsink_ref[...] = jnp.zeros_like(dsink_ref)
            dqg_ref[...] = jnp.zeros_like(dqg_ref)

        rowid = lax.broadcasted_iota(jnp.int32, (3 * BLOCK, G * BLOCK), 0)
        edge_ok = jnp.logical_and(jnp.logical_or(rowid >= BLOCK, n > 0), jnp.logical_or(rowid < 2 * BLOCK, n < nb - 1))
        qg = qg_ref[...]
        cat = jnp.concatenate([kv0[...], kv1[...], kv2[...]], axis=0)
        qblk, doblk = q_ref[...], do_ref[...]
        kn = [(_rms_rows(cat[:, j * dh:(j + 1) * dh])[0] * kg_ref[...]).astype(BF16) for j in both]
        vb = [cat[:, (KV + j) * dh:(KV + j + 1) * dh].astype(BF16) for j in both]
        norm = [_rms_rows(_stack_heads(qblk[:, j * GROUP_WIDTH:(j + 1) * GROUP_WIDTH])) for j in both]
        qn = [(norm[j][0] * (qg * scale)).astype(BF16) for j in both]
        do_b = [_stack_heads(doblk[:, j * GROUP_WIDTH:(j + 1) * GROUP_WIDTH]).astype(BF16) for j in both]
        s = [_nt(kn[j], qn[j]) + bias_ref[j] for j in both]
        dp = [_nt(vb[j], do_b[j]) for j in both]
        s = [jnp.where(edge_ok, sj, NEG) for sj in s]
        sinks = [jnp.concatenate([sink_ref[j * G + g] for g in range(G)], axis=1) for j in both]
        m = [jnp.maximum(jnp.max(s[j], axis=0, keepdims=True), sinks[j]) for j in both]
        e = [jnp.exp(s[j] - m[j]) for j in both]
        e_sink = [jnp.exp(sinks[j] - m[j]) for j in both]
        inv = [1.0 / (jnp.sum(e[j], axis=0, keepdims=True) + e_sink[j]) for j in both]
        p = [e[j] * inv[j] for j in both]
        delta = [jnp.sum(p[j] * dp[j], axis=0, keepdims=True) for j in both]
        ds = [p[j] * (dp[j] - delta[j]) for j in both]
        ds_b = [dsj.astype(BF16) for dsj in ds]
        dqn = [_tn(kn[j], ds_b[j]).T * scale for j in both]
        for j in both:
            dvw_ref[j] = _nn(p[j].astype(BF16), do_b[j])
            dkw_ref[j] = _nn(ds_b[j], qn[j])
        for j in both:
            ds_ref[j] += ds[j]
            sink_term = e_sink[j] * inv[j] * delta[j]
            for g in range(G):
                dsink_ref[j * G + g] += (jnp.zeros((1, BLOCK), F32)
                                         - jnp.sum(sink_term[:, g * BLOCK:(g + 1) * BLOCK], axis=1, keepdims=True))
        dq = []
        for j in both:
            qhat, rstd = norm[j]
            dqg_ref[j] += jnp.sum(dqn[j] * qhat, axis=0, keepdims=True)
            dqh = dqn[j] * qg
            dq.append(_unstack_heads(rstd * (dqh - qhat * jnp.mean(dqh * qhat, axis=-1, keepdims=True))))
        dq_ref[...] = jnp.concatenate(dq, axis=1).astype(dq_ref.dtype)

    def kv(shift):
        return pl.BlockSpec((BLOCK, 2 * KV_WIDTH), lambda n: (jnp.clip(n + shift, 0, nb - 1), KV_COL // (2 * KV_WIDTH)))

    gain = pl.BlockSpec((1, dh), lambda n: (0, 0))
    sink_spec = pl.BlockSpec((ATT_Q_HEADS, 1, BLOCK), lambda n: (0, 0, 0))
    bias_spec = pl.BlockSpec((KV, 3 * BLOCK, G * BLOCK), lambda n: (0, 0, 0))
    win = pl.BlockSpec((KV, None, 3 * BLOCK, dh), lambda n: (0, n, 0, 0))
    wshape = jax.ShapeDtypeStruct((KV, nb, 3 * BLOCK, dh), F32)
    return pl.pallas_call(
        body, name=name, grid=(nb,),
        in_specs=[pl.BlockSpec((BLOCK, ATT_WIDTH), lambda n: (n, Q_COL // ATT_WIDTH)), kv(-1), kv(0), kv(1), gain, gain,
                  sink_spec, bias_spec, pl.BlockSpec((BLOCK, ATT_WIDTH), lambda n: (n, HG_WIDTH // ATT_WIDTH))],
        out_specs=[pl.BlockSpec((BLOCK, ATT_WIDTH), lambda n: (n, 0)), win, win, bias_spec, sink_spec,
                   pl.BlockSpec((KV, 1, dh), lambda n: (0, 0, 0))],
        out_shape=[jax.ShapeDtypeStruct((S, ATT_WIDTH), BF16), wshape, wshape,
                   jax.ShapeDtypeStruct((KV, 3 * BLOCK, G * BLOCK), F32),
                   jax.ShapeDtypeStruct((ATT_Q_HEADS, 1, BLOCK), F32),
                   jax.ShapeDtypeStruct((KV, 1, dh), F32)],
        compiler_params=_params("arbitrary"),
    )(z, z, z, z, q_g, k_g, sink, bias_t, do)


def _attn_kv_reduce(dkw, dvw, z, k_g, name):
    S = z.shape[0]
    nb = S // BLOCK
    dh = ATT_HEAD_DIM
    kb = min(8, nb)
    steps = nb // kb

    def body(a_lo, a, a_hi, b_lo, b, b_hi, kv_ref, kg_ref, dkv_ref, dkg_ref):
        n = pl.program_id(0)

        @pl.when(n == 0)
        def _():
            dkg_ref[...] = jnp.zeros_like(dkg_ref)

        lo = jnp.where(n > 0, 1.0, 0.0)
        hi = jnp.where(n < steps - 1, 1.0, 0.0)

        def overlap_add(w, w_lo, w_hi, j, i):
            before = lo * w_lo[j] if i == 0 else w[j, i - 1, 2 * BLOCK:3 * BLOCK, :]
            after = hi * w_hi[j] if i == kb - 1 else w[j, i + 1, 0:BLOCK, :]
            return w[j, i, BLOCK:2 * BLOCK, :] + before + after

        dkg = [jnp.zeros((1, dh), F32) for _ in range(ATT_KV_HEADS)]
        for i in range(kb):
            rows = slice(i * BLOCK, (i + 1) * BLOCK)
            dks, dvs = [], []
            for j in range(ATT_KV_HEADS):
                dkn = overlap_add(a, a_lo, a_hi, j, i)
                dvs.append(overlap_add(b, b_lo, b_hi, j, i))
                khat, rstd = _rms_rows(kv_ref[rows, j * dh:(j + 1) * dh])
                dkg[j] = dkg[j] + jnp.sum(dkn * khat, axis=0, keepdims=True)
                dkh = dkn * kg_ref[...]
                dks.append(rstd * (dkh - khat * jnp.mean(dkh * khat, axis=-1, keepdims=True)))
            dkv_ref[rows, :] = jnp.concatenate(dks + dvs, axis=1).astype(dkv_ref.dtype)
        for j in range(ATT_KV_HEADS):
            dkg_ref[j] += dkg[j]

    main = pl.BlockSpec((ATT_KV_HEADS, kb, 3 * BLOCK, dh), lambda n: (0, n, 0, 0))
    halo_lo = pl.BlockSpec((ATT_KV_HEADS, None, BLOCK, dh), lambda n: (0, jnp.maximum(n * kb - 1, 0), 2, 0))
    halo_hi = pl.BlockSpec((ATT_KV_HEADS, None, BLOCK, dh), lambda n: (0, jnp.minimum(n * kb + kb, nb - 1), 0, 0))
    return pl.pallas_call(
        body, name=name, grid=(steps,),
        in_specs=[halo_lo, main, halo_hi, halo_lo, main, halo_hi,
                  pl.BlockSpec((kb * BLOCK, 2 * KV_WIDTH), lambda n: (n, KV_COL // (2 * KV_WIDTH))),
                  pl.BlockSpec((1, dh), lambda n: (0, 0))],
        out_specs=[pl.BlockSpec((kb * BLOCK, 2 * KV_WIDTH), lambda n: (n, 0)),
                   pl.BlockSpec((ATT_KV_HEADS, 1, dh), lambda n: (0, 0, 0))],
        out_shape=[jax.ShapeDtypeStruct((S, 2 * KV_WIDTH), BF16), jax.ShapeDtypeStruct((ATT_KV_HEADS, 1, dh), F32)],
        compiler_params=_params("arbitrary"),
    )(dkw, dkw, dkw, dvw, dvw, dvw, z, k_g)


def _ada_wgrad(c_act_t, dm, name):
    D, nbatch = c_act_t.shape
    n = dm.shape[1]
    tr = 256

    def body(c_ref, dm_ref, o_ref):
        cv, dv = c_ref[...], dm_ref[...]
        acc = cv[:, 0:1] * dv[0:1, :]
        for b in range(1, nbatch):
            acc = acc + cv[:, b:b + 1] * dv[b:b + 1, :]
        o_ref[...] = acc

    return pl.pallas_call(
        body, name=name, grid=(D // tr,),
        in_specs=[pl.BlockSpec((tr, nbatch), lambda i: (i, 0)), pl.BlockSpec((nbatch, n), lambda i: (0, 0))],
        out_specs=pl.BlockSpec((tr, n), lambda i: (i, 0)), out_shape=jax.ShapeDtypeStruct((D, n), F32),
        compiler_params=_params("parallel"),
    )(c_act_t, dm)


def _to_bf16(w, name):
    R, Cn = w.shape
    tr = _row_tile(R)

    def body(w_ref, o_ref):
        o_ref[...] = w_ref[...].astype(BF16)

    blk = pl.BlockSpec((tr, Cn), lambda i: (i, 0))
    return pl.pallas_call(
        body, name=name, grid=(R // tr,), in_specs=[blk], out_specs=blk, out_shape=jax.ShapeDtypeStruct((R, Cn), BF16),
        compiler_params=_params("parallel"),
    )(w)


def _adamw(w, g, m, v, name):
    R, Cn = w.shape
    tr = R
    for cand in (256, 128, 64, 32, 16, 8):
        if R % cand == 0:
            tr = cand
            break

    def body(w_ref, g_ref, m_ref, v_ref, d_ref, nm_ref, nv_ref):
        gv = g_ref[...]
        m_new = ADAM_B1 * m_ref[...] + (1.0 - ADAM_B1) * gv
        v_new = ADAM_B2 * v_ref[...] + (1.0 - ADAM_B2) * (gv * gv)
        m_hat = m_new / (1.0 - ADAM_B1 ** ADAM_STEP)
        v_hat = v_new / (1.0 - ADAM_B2 ** ADAM_STEP)
        d_ref[...] = -ADAM_LR * (m_hat / (jnp.sqrt(v_hat) + ADAM_EPS) + ADAM_WD * w_ref[...])
        nm_ref[...] = m_new
        nv_ref[...] = v_new

    blk = pl.BlockSpec((tr, Cn), lambda i: (i, 0))
    shp = jax.ShapeDtypeStruct((R, Cn), F32)
    return pl.pallas_call(
        body, name=name, grid=(R // tr,), in_specs=[blk] * 4, out_specs=[blk] * 3, out_shape=[shp] * 3,
        compiler_params=_params("parallel"),
    )(w, g, m, v)


def _place():
    return lax.axis_index("x"), lax.axis_index("y"), lax.axis_index("c")


def _flip(place, k):
    x, y, c = place
    return (1 - x if k & 4 else x, 1 - y if k & 2 else y, 1 - c if k & 1 else c)


def _dev_index(place):
    x, y, c = place
    return 4 * x + 2 * y + c


def _chip_index(place):
    return 2 * place[0] + place[1]


def _gather8(x_ref, out_ref, send_sems, recv_sems, local_sem):
    me = _place()
    mine = pltpu.make_async_copy(x_ref, out_ref.at[_dev_index(me)], local_sem)
    mine.start()

    def copy(k, origin, to):
        return pltpu.make_async_remote_copy(
            src_ref=x_ref, dst_ref=out_ref.at[_dev_index(origin)], send_sem=send_sems.at[k - 1],
            recv_sem=recv_sems.at[k - 1], device_id=to, device_id_type=MESH)

    sends = [copy(k, me, _flip(me, k)) for k in range(1, 8)]
    for cp in sends:
        cp.start()
    for k in range(1, 8):
        copy(k, _flip(me, k), me).wait_recv()
    for cp in sends:
        cp.wait_send()
    mine.wait()


def _allgather8(x, name, reduce=False):
    R, Cn = x.shape

    def body(x_ref, *rest):
        if reduce:
            out_ref, sum_ref, send_sems, recv_sems, local_sem = rest
        else:
            out_ref, send_sems, recv_sems, local_sem = rest
        _gather8(x_ref, out_ref, send_sems, recv_sems, local_sem)
        if reduce:
            acc = out_ref[0]
            for i in range(1, 8):
                acc = acc + out_ref[i]
            sum_ref[...] = acc

    vm = pl.BlockSpec(memory_space=pltpu.VMEM)
    outs = [jax.ShapeDtypeStruct((8, R, Cn), F32)] + ([jax.ShapeDtypeStruct((R, Cn), F32)] if reduce else [])
    res = pl.pallas_call(
        body, name=name, in_specs=[vm], out_specs=[vm] * len(outs), out_shape=outs,
        scratch_shapes=[pltpu.SemaphoreType.DMA((7,)), pltpu.SemaphoreType.DMA((7,)), pltpu.SemaphoreType.DMA],
    )(x)
    return res if reduce else res[0]


def _prologue(small, w_ada, b_ada, w_shard, name):
    R, Cn = small.shape
    n_mod = w_ada.shape[1]
    big = _gather_in_one([w_shard])

    def body(small_ref, wada_ref, b_ref, shard_ref, small_all_ref, mods_all_ref, gathered_ref, mods_ref,
             send1, recv1, send2, recv2, local_sems, big_send, big_recv):
        big.start([shard_ref], [gathered_ref], big_send, big_recv)
        _gather8(small_ref, small_all_ref, send1, recv1, local_sems.at[0])
        c_all = jnp.concatenate([small_all_ref[d, 0:1, :] for d in range(8)], axis=0)
        c_act = c_all * _sigmoid(c_all)
        mods_ref[...] = _nn(c_act, wada_ref[...], precision=lax.Precision.HIGHEST) + b_ref[...]
        _gather8(mods_ref, mods_all_ref, send2, recv2, local_sems.at[1])
        big.switch([shard_ref], [gathered_ref], big_send, big_recv)
        big.finish([shard_ref], [gathered_ref], big_send, big_recv)

    vm = pl.BlockSpec(memory_space=pltpu.VMEM)
    seven = pltpu.SemaphoreType.DMA((7,))
    return pl.pallas_call(
        body, name=name, in_specs=[vm, vm, vm, ANY], out_specs=[vm, vm, ANY],
        out_shape=[jax.ShapeDtypeStruct((8, R, Cn), F32), jax.ShapeDtypeStruct((8, 8, n_mod), F32)] + big.out_shapes,
        scratch_shapes=[pltpu.VMEM((8, n_mod), F32), seven, seven, seven, seven, pltpu.SemaphoreType.DMA((2,))]
        + big.sem_shapes(),
        compiler_params=pltpu.CompilerParams(vmem_limit_bytes=VMEM_LIMIT_BYTES),
    )(small, w_ada, b_ada, w_shard)


def _symmetric_plan(copies):
    def plan(in_refs, out_refs):
        sends = [(src, dst, i, to) for i, (src, dst, to) in enumerate(copies(in_refs, out_refs))]
        return sends, [(dst, i) for _, dst, i, _ in sends]
    return plan


def _halves_exchange(grads):
    def copies(in_refs, out_refs):
        me = _place()
        return [(g.at[kk, 1 - me[2]], got.at[kk], _flip(me, 1)) for g, got in zip(in_refs, out_refs) for kk in range(N_CHIPS)]

    return _Exchange(grads, [jax.ShapeDtypeStruct((N_CHIPS,) + g.shape[2:], g.dtype) for g in grads],
                     N_CHIPS * len(grads), _symmetric_plan(copies))


def _chips_exchange(parts):
    def copies(in_refs, out_refs):
        me = _place()
        return [(p.at[_chip_index(_flip(me, 2 * j))], got.at[j - 1], _flip(me, 2 * j))
                for p, got in zip(in_refs, out_refs) for j in (1, 2, 3)]

    return _Exchange(parts, [jax.ShapeDtypeStruct((3,) + p.shape[1:], p.dtype) for p in parts], 3 * len(parts),
                     _symmetric_plan(copies))


def _siblings_exchange(halves):
    def copies(in_refs, out_refs):
        sibling = _flip(_place(), 1)
        return [(h, got, sibling) for h, got in zip(in_refs, out_refs)]

    return _Exchange(halves, [jax.ShapeDtypeStruct(h.shape, h.dtype) for h in halves], len(halves), _symmetric_plan(copies))


def _ici_gather_plan(n, base=0):
    def plan(in_refs, out_refs):
        me = _place()
        c = me[2]
        sends, lands = [], []
        for a, (w, out) in enumerate(zip(in_refs[:n], out_refs)):
            for j in (1, 2, 3):
                i = base + 3 * a + j - 1
                sends.append((w.at[c], out.at[_chip_index(me), c], i, _flip(me, 2 * j)))
                lands.append((out.at[_chip_index(_flip(me, 2 * j)), c], i))
        return sends, lands
    return plan


def _d2d_gather_plan(n, base=0):
    def plan(in_refs, out_refs):
        me = _place()
        c = me[2]
        sibling = _flip(me, 1)
        mine = _chip_index(me)
        sends, lands = [], []
        for a, (w, out) in enumerate(zip(in_refs[:n], out_refs)):
            moves = [(w.at[c], (mine, c)), (w.at[1 - c], (mine, 1 - c))]
            moves += [(out.at[_chip_index(_flip(me, 2 * j)), c], (_chip_index(_flip(me, 2 * j)), c)) for j in (1, 2, 3)]
            for k, (src, (chip, half)) in enumerate(moves):
                sends.append((src, out.at[chip, half], base + 5 * a + k, sibling))
            blocks = [(mine, 1 - c), (mine, c)] + [(_chip_index(_flip(me, 2 * j)), 1 - c) for j in (1, 2, 3)]
            lands += [(out.at[chip, half], base + 5 * a + k) for k, (chip, half) in enumerate(blocks)]
        return sends, lands
    return plan


def _gathered_shapes(shards):
    return [jax.ShapeDtypeStruct((N_CHIPS,) + s.shape, s.dtype) for s in shards]


def _gather_over_ici(shards):
    return _Exchange(shards, _gathered_shapes(shards), 3 * len(shards), _ici_gather_plan(len(shards)))


def _gather_over_d2d(shards, gathered):
    n = len(shards)
    return _Exchange(list(shards) + list(gathered), [jax.ShapeDtypeStruct(g.shape, g.dtype) for g in gathered], 5 * n,
                     _d2d_gather_plan(n), aliases={n + a: a for a in range(n)})


def _gather_in_one(shards):
    n = len(shards)
    return _Exchange(shards, _gathered_shapes(shards), 8 * n, _ici_gather_plan(n), then=_d2d_gather_plan(n, base=3 * n))


def _row_tile(rows):
    for cand in (256, 176, 128, 64, 32, 16, 8):
        if rows % cand == 0:
            return cand
    return rows


def _pair_sum(core, grad, theirs, name):
    N, _, R, Cn = grad.shape
    tr = R

    def body(core_ref, g_ref, t_ref, o_ref, ob_ref):
        s = g_ref[...] + t_ref[...]
        o_ref[...] = s
        ob_ref[...] = s.astype(BF16)

    out = pl.BlockSpec((None, tr, Cn), lambda k, i, core_ref: (k, i, 0))
    return pl.pallas_call(
        body, name=name,
        grid_spec=pltpu.PrefetchScalarGridSpec(
            num_scalar_prefetch=1, grid=(N, R // tr),
            in_specs=[pl.BlockSpec((None, None, tr, Cn), lambda k, i, core_ref: (k, core_ref[0], i, 0)),
                      pl.BlockSpec((None, tr, Cn), lambda k, i, core_ref: (k, i, 0))],
            out_specs=[out, out]),
        out_shape=[jax.ShapeDtypeStruct((N, R, Cn), F32), jax.ShapeDtypeStruct((N, R, Cn), BF16)],
        compiler_params=_params("parallel", "parallel"),
    )(core, grad, theirs)


def _chip_sum(chip, parts, landed, name):
    _, R, Cn = parts.shape
    tr = R

    def body(chip_ref, p_ref, l_ref, o_ref):
        o_ref[...] = ((p_ref[...] + l_ref[0].astype(F32)) + l_ref[1].astype(F32)) + l_ref[2].astype(F32)

    return pl.pallas_call(
        body, name=name,
        grid_spec=pltpu.PrefetchScalarGridSpec(
            num_scalar_prefetch=1, grid=(R // tr,),
            in_specs=[pl.BlockSpec((None, tr, Cn), lambda i, chip_ref: (chip_ref[0], i, 0)),
                      pl.BlockSpec((3, tr, Cn), lambda i, chip_ref: (0, i, 0))],
            out_specs=pl.BlockSpec((tr, Cn), lambda i, chip_ref: (i, 0))),
        out_shape=jax.ShapeDtypeStruct((R, Cn), F32), compiler_params=_params("parallel"),
    )(chip, parts, landed)


def _pair_sums(core, grads, theirs, tag):
    return [_pair_sum(core, g, t, f"{tag}_pair_sum_{i}") for i, (g, t) in enumerate(zip(grads, theirs))]


def _chip_sums(chip, parts, landed, tag):
    return [_chip_sum(chip, p[0], l, f"{tag}_chip_sum_{i}") for i, (p, l) in enumerate(zip(parts, landed))]


def _by_chip_rows(g):
    return g.reshape(N_CHIPS, 2, g.shape[0] // (2 * N_CHIPS), g.shape[1])


def _by_chip_cols(g):
    return g.reshape(N_CHIPS, 2, g.shape[1] // 2, g.shape[2])


def _adamw_halves(core, w, g_mine, g_theirs, m, v, name):
    R2, Cn = w.shape
    r = R2 // 2
    tr = _row_tile(r)
    nt = r // tr

    def body(core_ref, w_ref, gm_ref, gt_ref, m_ref, v_ref, g_ref, d_ref, nm_ref, nv_ref):
        gv = jnp.where(pl.program_id(0) == core_ref[0], gm_ref[...], gt_ref[...])
        g_ref[...] = gv
        m_new = ADAM_B1 * m_ref[...] + (1.0 - ADAM_B1) * gv
        v_new = ADAM_B2 * v_ref[...] + (1.0 - ADAM_B2) * (gv * gv)
        m_hat = m_new / (1.0 - ADAM_B1 ** ADAM_STEP)
        v_hat = v_new / (1.0 - ADAM_B2 ** ADAM_STEP)
        d_ref[...] = -ADAM_LR * (m_hat / (jnp.sqrt(v_hat) + ADAM_EPS) + ADAM_WD * w_ref[...])
        nm_ref[...] = m_new
        nv_ref[...] = v_new

    full = pl.BlockSpec((tr, Cn), lambda hf, i, core_ref: (hf * nt + i, 0))
    half = pl.BlockSpec((tr, Cn), lambda hf, i, core_ref: (i, 0))
    shp = jax.ShapeDtypeStruct((R2, Cn), F32)
    return pl.pallas_call(
        body, name=name,
        grid_spec=pltpu.PrefetchScalarGridSpec(
            num_scalar_prefetch=1, grid=(2, nt), in_specs=[full, half, half, full, full], out_specs=[full] * 4),
        out_shape=[shp] * 4, compiler_params=_params("parallel", "parallel"),
    )(core, w, g_mine, g_theirs, m, v)


def _pad_row(v, width):
    v = v.reshape(1, -1)
    return jnp.pad(v, ((0, 0), (0, width - v.shape[1])))


def _ffn1_forward(x, ng, shift, scale, gate, w_in4, w_out_shard, gather, next_norm):
    h = _rmsmod_fwd(x, ng, shift, scale, "ffn1_norm")
    (zg, zu, a), (partly, (w_out4,)) = _ffn_in_fwd(
        h, w_in4, "ffn1_in", exchange=[_gather_over_ici(gather), _gather_in_one([w_out_shard])])
    w_out = w_out4.reshape(D_FF, D_MODEL)
    (x_new, f, h_next), gathered = _proj_out_fwd([a], w_out, x, gate, 0.5, "ffn1_out", next_norm=next_norm,
                                                 exchange=_gather_over_d2d(gather, partly))
    return x_new, (h, zg, zu, a, f), w_out, gathered, h_next


def _ffn_backward(df, saved, w_in4, w_out, core, chip, tag, riding=None, norm=None):
    h, zg, zu, a = saved[:4]
    rode = None
    if riding:
        (dzg, dzu), rode = _dact_bwd(df, w_out, zg, zu, f"{tag}_dact", exchange=riding)
    else:
        dzg, dzu = _dact_bwd(df, w_out, zg, zu, f"{tag}_dact")
    g_out = [_by_chip_rows(_wgrad(a, [df], df.shape[1], f"{tag}_dw_out")[0].reshape(a.shape[1], df.shape[1]))]
    (dw_in,), theirs_out = _wgrad(h, [dzg, dzu], FF_SHARD, f"{tag}_dw_in", exchange=_halves_exchange(g_out))
    g_in = [_by_chip_cols(dw_in.reshape(N_CHIPS, h.shape[1], FF_SHARD))]
    parts_out = _pair_sums(core, g_out, theirs_out, f"{tag}_out")
    dh_outs, (theirs_in, landed_out) = _ffn_in_dgrad(
        dzg, dzu, w_in4, f"{tag}_dh", norm=norm, exchange=[_halves_exchange(g_in), _chips_exchange([parts_out[0][1]])])
    parts_in = _pair_sums(core, g_in, theirs_in, f"{tag}_in")
    return dh_outs, parts_in, _chip_sums(chip, parts_out, landed_out, f"{tag}_out"), rode


def kernel(x, c, w_ada, b_ada, norm_g, w_ffn1_in, w_ffn1_out, w_ffn2_in, w_ffn2_out, w_mix_in, w_mix_out, hgrn_lb, hgrn_norm_g, qk_norm_g, attn_sink, rel_bias, loss_target, m_w_ada, m_b_ada, m_norm_g, m_w_ffn1_in, m_w_ffn1_out, m_w_ffn2_in, m_w_ffn2_out, m_w_mix_in, m_w_mix_out, m_hgrn_lb, m_hgrn_norm_g, m_qk_norm_g, m_attn_sink, m_rel_bias, v_w_ada, v_b_ada, v_norm_g, v_w_ffn1_in, v_w_ffn1_out, v_w_ffn2_in, v_w_ffn2_out, v_w_mix_in, v_w_mix_out, v_hgrn_lb, v_hgrn_norm_g, v_qk_norm_g, v_attn_sink, v_rel_bias):
    D = D_MODEL
    S = x.shape[1]
    place = (lax.axis_index("x"), lax.axis_index("y"), lax.axis_index("c"))
    me, my_chip = _dev_index(place), _chip_index(place)
    x0 = x[0]
    target = loss_target[0]

    def halves(w, tag):
        return _to_bf16(w[0], f"{tag}_to_bf16").reshape(2, w.shape[1] // 2, w.shape[2])

    w1_out_shard = halves(w_ffn1_out, "w_ffn1_out")
    mix_shards = [halves(w_mix_in, "w_mix_in"), halves(w_mix_out, "w_mix_out")]
    ffn2_shards = [halves(w_ffn2_in, "w_ffn2_in"), halves(w_ffn2_out, "w_ffn2_out")]
    core_arr = jnp.reshape(place[2], (1,)).astype(jnp.int32)
    chip_arr = jnp.reshape(my_chip, (1,)).astype(jnp.int32)

    small = jnp.concatenate([_pad_row(c, D), _pad_row(norm_g, D), _pad_row(hgrn_lb, D), jnp.zeros((5, D), F32)], axis=0)
    n_ada = w_ada.shape[2]
    b_mine = lax.dynamic_slice_in_dim(b_ada, my_chip * n_ada, n_ada, axis=1)
    small_all, mods_parts, w1_in4 = _prologue(small, w_ada[0], b_mine, halves(w_ffn1_in, "w_ffn1_in"), "prologue")
    w1_in = w1_in4.reshape(N_CHIPS, D, FF_SHARD)
    c_all = small_all[:, 0, :]
    by_chip = small_all[0::2]
    norm_g_full = by_chip[:, 1, :3 * 256].reshape(N_CHIPS, 3, 256).transpose(1, 0, 2).reshape(3, D)
    lb_raw = by_chip[:, 2, :2 * 2 * 128].reshape(N_CHIPS, 2, 2, 128).transpose(1, 2, 0, 3).reshape(2, 2, HG_WIDTH)
    lb_logit = lb_raw[:, 0, :] - lb_raw[:, 1, :]
    lb = jax.nn.sigmoid(lb_logit)
    one_minus_lb = jax.nn.sigmoid(-lb_logit)
    lb_f = jnp.stack([lb[0], one_minus_lb[0]])
    lb_b = jnp.stack([lb[1], one_minus_lb[1]])

    c_act_all = c_all * jax.nn.sigmoid(c_all)
    mods_all = mods_parts[0::2].transpose(1, 0, 2).reshape(8, N_MOD * D)
    mods = lax.dynamic_slice_in_dim(mods_all, me, 1, axis=0)
    sh1, sc1, g1, sh2, sc2, g2, sh3, sc3, g3 = [mods[:, i * D:(i + 1) * D] for i in range(N_MOD)]

    x1, saved1, w1_out, gathered, h2 = _ffn1_forward(x0, norm_g_full[0:1], sh1, sc1, g1, w1_in, w1_out_shard, mix_shards,
                                                     (norm_g_full[1:2], sh2, sc2))
    wm_in = gathered[0].reshape(N_CHIPS, D, D_IN // N_CHIPS).transpose(1, 0, 2).reshape(D, D_IN)
    wm_out = gathered[1].reshape(D, D)

    z = _matmul_nn(h2, wm_in, F32, 256, "mix_in")
    (of, st_f), partly = _hgrn_fwd(z, lb_f, 0, "hgrn_fwd_f", exchange=_gather_over_ici(ffn2_shards))
    (ob, st_b), gathered = _hgrn_fwd(z, lb_b, 1, "hgrn_fwd_b", exchange=_gather_over_d2d(ffn2_shards, partly))
    w2_in = gathered[0].reshape(N_CHIPS, D, FF_SHARD)
    w2_out = gathered[1].reshape(D_FF, D)
    o_h = _hgrn_post_fwd(of, ob, z, hgrn_norm_g, "hgrn_post")

    q_g, k_g = qk_norm_g[0, 0:1], qk_norm_g[0, 1:2]
    sink_b = jnp.broadcast_to(attn_sink.reshape(ATT_Q_HEADS, 1, 1), (ATT_Q_HEADS, 1, BLOCK))
    bias = _bias_table(rel_bias, "bias_table")
    o_a = _attn_fwd(z, q_g, k_g, sink_b, bias, "attn_fwd")
    x2, mixed, h3 = _proj_out_fwd([o_h, o_a], wm_out, x1, g2, 1.0, "mix_out", next_norm=(norm_g_full[2:3], sh3, sc3))

    zg3, zu3, a3 = _ffn_in_fwd(h3, w2_in, "ffn2_in")
    dx3, df3, dg3, sq_cols = _proj_out_loss(a3, w2_out, x2, g3, 0.5, target, "ffn2_out_loss")
    loss_mine = 0.5 * jnp.sum(sq_cols) / D

    (dx2, dsh3, dsc3, dng3, dmixed, dg2), parts2, mine2_out, _ = _ffn_backward(
        df3, (h3, zg3, zu3, a3), w2_in, w2_out, core_arr, chip_arr, "ffn2",
        norm=_NormBwd(x2, norm_g_full[2:3], sc3, dx3, below=(mixed, g2, 1.0)))

    (do_cat,) = _matmul_nt([dmixed], wm_out, ROW_TILE, "mix_out_dgrad")
    dwm_out = _wgrad_rows([o_h, o_a], dmixed, "mix_out_dw").reshape(D, D)

    do_sum, dgr, d_hnorm = _hgrn_post_bwd(do_cat, of, ob, z, hgrn_norm_g, "hgrn_post_bwd")
    (dq_f, dff, dv_f, doml_f), landed2 = _hgrn_bwd(z, lb_f, do_sum, st_f, 0, "hgrn_bwd_f",
                                                   exchange=_chips_exchange([p[1] for p in parts2]))
    mine2 = _chip_sums(chip_arr, parts2, landed2, "ffn2_in") + mine2_out
    (dhq, dfb, dhi, doml_b), theirs2 = _hgrn_bwd(z, lb_b, do_sum, st_b, 1, "hgrn_bwd_b", acc=(dq_f, dv_f),
                                                 exchange=_siblings_exchange(mine2))

    daq, dkw, dvw, ds_sum, dsink, dqg = _attn_bwd(z, q_g, k_g, sink_b, bias, do_cat, "attn_bwd")
    dkv, dkg = _attn_kv_reduce(dkw, dvw, z, k_g, "attn_kv_reduce")
    d_rel_bias = jnp.sum(_bias_grad(ds_sum, "bias_grad"), axis=-1).T
    dz = [dhq, dff, dfb, dhi, dgr, daq, dkv]
    dwm_in = _wgrad_pieces(h2, dz, 2 * KV_WIDTH, "mix_in_dw").transpose(1, 0, 2).reshape(D, D_IN)
    wide = D_IN // N_CHIPS
    grads_m = [_by_chip_cols(dwm_in.reshape(D, N_CHIPS, wide).transpose(1, 0, 2)), _by_chip_rows(dwm_out)]
    (dx1, dsh2, dsc2, dng2, df1, dg1), theirs_m = _matmul_nt(
        dz, wm_in, 256, "mix_in_dgrad", exchange=_halves_exchange(grads_m),
        norm=_NormBwd(x1, norm_g_full[1:2], sc2, dx2, below=(saved1[4], g1, 0.5)))
    parts_m = _pair_sums(core_arr, grads_m, theirs_m, "mix")

    (dh1,), parts1, mine1_out, landed_m = _ffn_backward(df1, saved1, w1_in, w1_out, core_arr, chip_arr, "ffn1",
                                                        riding=_chips_exchange([p[1] for p in parts_m]))
    mine_m = _chip_sums(chip_arr, parts_m, landed_m, "mix")
    (dx0, dsh1, dsc1, dng1), landed1 = _rmsmod_bwd(dh1, _NormBwd(x0, norm_g_full[0:1], sc1, dx1), "ffn1_norm_bwd",
                                                   exchange=_chips_exchange([p[1] for p in parts1]))
    mine1 = _chip_sums(chip_arr, parts1, landed1, "ffn1_in") + mine1_out
    theirs_1m = list(_run_exchange(_siblings_exchange(mine1 + mine_m), "siblings_exchange"))
    reduced = list(zip(mine1 + mine2 + mine_m, theirs_1m[:2] + list(theirs2) + theirs_1m[2:]))

    dlb = -jnp.concatenate([doml_f, doml_b], axis=0)
    dlb_raw = dlb * lb * one_minus_lb
    d_hgrn_lb = jnp.stack([dlb_raw, -dlb_raw], axis=1)
    d_qk = jnp.concatenate([jnp.sum(dqg, axis=0), jnp.sum(dkg, axis=0)], axis=0)
    dmods = jnp.concatenate([dsh1, dsc1, dg1, dsh2, dsc2, dg2, dsh3, dsc3, dg3], axis=0)
    packed = jnp.concatenate(
        [dmods, dng1, dng2, dng3, d_hgrn_lb.reshape(2, D), _pad_row(d_hnorm, D), _pad_row(d_qk, D),
         _pad_row(dsink[:, 0, 0], D), _pad_row(d_rel_bias, D), _pad_row(loss_mine, D)], axis=0)
    packed = jnp.pad(packed, ((0, 24 - packed.shape[0]), (0, 0)))
    packed_all, packed_sum = _allgather8(packed, "small_grads_allgather", reduce=True)
    dmods_all = packed_all[:, 0:N_MOD, :].reshape(8, N_MOD * D)
    g_b_ada = packed_sum[0:N_MOD].reshape(1, N_MOD * D)
    g_norm_full = packed_sum[9:12]
    g_norm_g = lax.dynamic_slice_in_dim(g_norm_full, my_chip * 256, 256, axis=1).reshape(1, 3, 256)
    g_hgrn_lb = lax.dynamic_slice_in_dim(packed_sum[12:14].reshape(2, 2, HG_WIDTH), my_chip * 128, 128, axis=2)
    g_hgrn_norm_g = packed_sum[14:15, :HG_WIDTH]
    g_qk_norm_g = packed_sum[15, :2 * ATT_HEAD_DIM].reshape(1, 2, ATT_HEAD_DIM)
    g_attn_sink = packed_sum[16:17, :ATT_Q_HEADS]
    g_rel_bias = packed_sum[17, :NUM_BUCKETS * ATT_Q_HEADS].reshape(NUM_BUCKETS, ATT_Q_HEADS)
    loss = packed_sum[18, 0]

    dm_mine = lax.dynamic_slice_in_dim(dmods_all, my_chip * n_ada, n_ada, axis=1)
    g_w_ada = _ada_wgrad(c_act_all.T, dm_mine, "ada_wgrad")[None]

    def big(w, g, m, v, name):
        d, nm, nv = _adamw(w[0], g[0], m[0], v[0], name)
        return d[None], nm[None], nv[None]

    def big_halves(w, g_pair, m, v, name):
        g, d, nm, nv = _adamw_halves(core_arr, w[0], g_pair[0], g_pair[1], m[0], v[0], name)
        return g[None], (d[None], nm[None], nv[None])

    g_w1_in, u_w1_in = big_halves(w_ffn1_in, reduced[0], m_w_ffn1_in, v_w_ffn1_in, "adamw_w_ffn1_in")
    g_w1_out, u_w1_out = big_halves(w_ffn1_out, reduced[1], m_w_ffn1_out, v_w_ffn1_out, "adamw_w_ffn1_out")
    g_w2_in, u_w2_in = big_halves(w_ffn2_in, reduced[2], m_w_ffn2_in, v_w_ffn2_in, "adamw_w_ffn2_in")
    g_w2_out, u_w2_out = big_halves(w_ffn2_out, reduced[3], m_w_ffn2_out, v_w_ffn2_out, "adamw_w_ffn2_out")
    g_wm_in, u_wm_in = big_halves(w_mix_in, reduced[4], m_w_mix_in, v_w_mix_in, "adamw_w_mix_in")
    g_wm_out, u_wm_out = big_halves(w_mix_out, reduced[5], m_w_mix_out, v_w_mix_out, "adamw_w_mix_out")

    smalls = [(b_ada, g_b_ada, m_b_ada, v_b_ada), (norm_g, g_norm_g, m_norm_g, v_norm_g), (hgrn_lb, g_hgrn_lb, m_hgrn_lb, v_hgrn_lb),
              (hgrn_norm_g, g_hgrn_norm_g, m_hgrn_norm_g, v_hgrn_norm_g), (qk_norm_g, g_qk_norm_g, m_qk_norm_g, v_qk_norm_g),
              (attn_sink, g_attn_sink, m_attn_sink, v_attn_sink), (rel_bias, g_rel_bias, m_rel_bias, v_rel_bias)]
    sizes = [t[0].size for t in smalls]
    total = sum(sizes)
    rows = -(-total // 128)
    rows = -(-rows // 8) * 8

    def pack(i):
        flat = jnp.concatenate([t[i].reshape(-1) for t in smalls])
        fill = 1.0 if i == 3 else 0.0
        return jnp.pad(flat, (0, rows * 128 - total), constant_values=fill).reshape(rows, 128)

    packed_out = _adamw(pack(0), pack(1), pack(2), pack(3), "adamw_small")

    def unpack(flat2d):
        flat = flat2d.reshape(-1)
        outs, off = [], 0
        for t, n in zip(smalls, sizes):
            outs.append(flat[off:off + n].reshape(t[0].shape))
            off += n
        return outs

    d_small, m_small, v_small = [unpack(t) for t in packed_out]

    upd = {
        "w_ada": big(w_ada, g_w_ada, m_w_ada, v_w_ada, "adamw_w_ada"),
        "w_ffn1_in": u_w1_in, "w_ffn1_out": u_w1_out, "w_ffn2_in": u_w2_in, "w_ffn2_out": u_w2_out,
        "w_mix_in": u_wm_in, "w_mix_out": u_wm_out,
    }
    small_names = ["b_ada", "norm_g", "hgrn_lb", "hgrn_norm_g", "qk_norm_g", "attn_sink", "rel_bias"]
    for i, nme in enumerate(small_names):
        upd[nme] = (d_small[i], m_small[i], v_small[i])
    grads = {
        "w_ada": g_w_ada, "b_ada": g_b_ada, "norm_g": g_norm_g, "w_ffn1_in": g_w1_in, "w_ffn1_out": g_w1_out,
        "w_ffn2_in": g_w2_in, "w_ffn2_out": g_w2_out, "w_mix_in": g_wm_in, "w_mix_out": g_wm_out, "hgrn_lb": g_hgrn_lb,
        "hgrn_norm_g": g_hgrn_norm_g, "qk_norm_g": g_qk_norm_g, "attn_sink": g_attn_sink, "rel_bias": g_rel_bias,
    }
    order = ["w_ada", "b_ada", "norm_g", "w_ffn1_in", "w_ffn1_out", "w_ffn2_in", "w_ffn2_out", "w_mix_in", "w_mix_out",
             "hgrn_lb", "hgrn_norm_g", "qk_norm_g", "attn_sink", "rel_bias"]
    return (loss, dx0[None], *[grads[k] for k in order], *[upd[k][0] for k in order], *[upd[k][1] for k in order],
            *[upd[k][2] for k in order])
```

```python
import functools
import math

import numpy as np
import jax
import jax.numpy as jnp
from jax import lax
from jax.experimental import pallas as pl
from jax.experimental.pallas import tpu as pltpu

F32, BF16 = jnp.float32, jnp.bfloat16

D_MODEL = 1024
D_FF = 2816
HG_HEADS, HG_DIM = 4, 128
HG_WIDTH = HG_HEADS * HG_DIM
ATT_Q_HEADS, ATT_KV_HEADS, ATT_HEAD_DIM = 8, 2, 64
ATT_GROUP = ATT_Q_HEADS // ATT_KV_HEADS
ATT_WIDTH = ATT_Q_HEADS * ATT_HEAD_DIM
KV_WIDTH = ATT_KV_HEADS * ATT_HEAD_DIM
WINDOW, BLOCK = 128, 128
NUM_BUCKETS, MAX_DISTANCE = 32, 128
N_MOD = 9
EPS = 1e-6
D_IN = 5 * HG_WIDTH + ATT_WIDTH + 2 * KV_WIDTH
ADAM_LR, ADAM_B1, ADAM_B2, ADAM_EPS, ADAM_WD, ADAM_STEP = 0.001, 0.9, 0.999, 1e-08, 0.01, 10

N_CHIPS = 4
FF_SHARD = 2 * D_FF // N_CHIPS
NEG = -1e30

VMEM_LIMIT_BYTES = 56 << 20
ROW_TILE = 512
HG_CHUNK = 16
HG_ROWS = 512

MESH = pl.DeviceIdType.MESH
ANY = pl.BlockSpec(memory_space=pl.ANY)


def _params(*sem):
    return pltpu.CompilerParams(dimension_semantics=sem, vmem_limit_bytes=VMEM_LIMIT_BYTES)


def _resident(shape, index_map):
    return pl.BlockSpec(shape, index_map, pipeline_mode=pl.Buffered(1))


def _dot(a, b, dims, precision=None):
    return lax.dot_general(a, b, (dims, ((), ())), precision=precision, preferred_element_type=F32)


def _nn(a, b, precision=None):
    return _dot(a, b, ((1,), (0,)), precision)


def _nt(a, b):
    return _dot(a, b, ((1,), (1,)))


def _tn(a, b):
    return _dot(a, b, ((0,), (0,)))


def _sigmoid(x):
    return jax.nn.sigmoid(x)


class _Exchange:
    def __init__(self, inputs, out_shapes, n_sems, plan, aliases=None, then=None):
        self.inputs, self.out_shapes, self.n_sems, self.plan, self.aliases = list(inputs), list(out_shapes), n_sems, plan, aliases or {}
        self.then = then

    def sem_shapes(self):
        return [pltpu.SemaphoreType.DMA((self.n_sems,)), pltpu.SemaphoreType.DMA((self.n_sems,))]

    @staticmethod
    def _copy(src, dst, i, to, send_sems, recv_sems):
        return pltpu.make_async_remote_copy(
            src_ref=src, dst_ref=dst, send_sem=send_sems.at[i], recv_sem=recv_sems.at[i], device_id=to, device_id_type=MESH)

    def _start(self, plan, in_refs, out_refs, send_sems, recv_sems):
        for src, dst, i, to in plan(in_refs, out_refs)[0]:
            self._copy(src, dst, i, to, send_sems, recv_sems).start()

    def _wait(self, plan, in_refs, out_refs, send_sems, recv_sems):
        sends, lands = plan(in_refs, out_refs)
        for zone, i in lands:
            self._copy(zone, zone, i, _place(), send_sems, recv_sems).wait_recv()
        for src, dst, i, to in sends:
            self._copy(src, dst, i, to, send_sems, recv_sems).wait_send()

    def start(self, *refs):
        self._start(self.plan, *refs)

    def switch(self, *refs):
        if self.then:
            self._wait(self.plan, *refs)
            self._start(self.then, *refs)

    def finish(self, *refs):
        self._wait(self.then or self.plan, *refs)


def _run_exchange(ex, name):
    n_in, n_out = len(ex.inputs), len(ex.out_shapes)

    def body(*refs):
        in_refs, out_refs, (send_sems, recv_sems) = refs[:n_in], refs[n_in:n_in + n_out], refs[n_in + n_out:]
        ex.start(in_refs, out_refs, send_sems, recv_sems)
        ex.switch(in_refs, out_refs, send_sems, recv_sems)
        ex.finish(in_refs, out_refs, send_sems, recv_sems)

    return pl.pallas_call(
        body, name=name, in_specs=[ANY] * n_in, out_specs=[ANY] * n_out, out_shape=ex.out_shapes,
        scratch_shapes=ex.sem_shapes(), input_output_aliases=dict(ex.aliases),
    )(*ex.inputs)


def _call(body, *, name, grid, in_specs, out_specs, out_shape, args, semantics, scratch_shapes=(), exchange=None):
    if exchange is None:
        return pl.pallas_call(
            body, name=name, grid=grid, in_specs=in_specs, out_specs=out_specs, out_shape=out_shape,
            scratch_shapes=list(scratch_shapes), compiler_params=_params(*semantics))(*args)
    exs = exchange if isinstance(exchange, (list, tuple)) else [exchange]
    n_in, n_out, n_scr = len(in_specs), len(out_specs), len(scratch_shapes)
    x_in, x_out = [len(ex.inputs) for ex in exs], [len(ex.out_shapes) for ex in exs]

    def take(refs, counts):
        groups = []
        for n in counts:
            groups.append(refs[:n])
            refs = refs[n:]
        return groups, refs

    def carrier(*refs):
        ins, refs = refs[:n_in], refs[n_in:]
        x_ins, refs = take(refs, x_in)
        outs, refs = refs[:n_out], refs[n_out:]
        x_outs, refs = take(refs, x_out)
        scr, refs = refs[:n_scr], refs[n_scr:]
        sems, _ = take(refs, [2] * len(exs))
        ids = [pl.program_id(a) for a in range(len(grid))]
        first = functools.reduce(jnp.logical_and, [i == 0 for i in ids])
        last = functools.reduce(jnp.logical_and, [i == g - 1 for i, g in zip(ids, grid)])
        step = functools.reduce(lambda acc, ig: acc * ig[1] + ig[0], zip(ids, grid), 0)

        @pl.when(first)
        def _():
            for ex, xi, xo, (send_sems, recv_sems) in zip(exs, x_ins, x_outs, sems):
                ex.start(xi, xo, send_sems, recv_sems)

        if any(ex.then for ex in exs):
            @pl.when(step == (3 * math.prod(grid)) // 4)
            def _():
                for ex, xi, xo, (send_sems, recv_sems) in zip(exs, x_ins, x_outs, sems):
                    ex.switch(xi, xo, send_sems, recv_sems)

        body(*ins, *outs, *scr)

        @pl.when(last)
        def _():
            for ex, xi, xo, (send_sems, recv_sems) in zip(exs, x_ins, x_outs, sems):
                ex.finish(xi, xo, send_sems, recv_sems)

    aliases, i0, o0 = {}, n_in, n_out
    for ex in exs:
        aliases.update({i0 + i: o0 + o for i, o in ex.aliases.items()})
        i0, o0 = i0 + len(ex.inputs), o0 + len(ex.out_shapes)
    res = pl.pallas_call(
        carrier, name=name, grid=grid, in_specs=list(in_specs) + [ANY] * sum(x_in),
        out_specs=list(out_specs) + [ANY] * sum(x_out),
        out_shape=list(out_shape) + [s for ex in exs for s in ex.out_shapes],
        scratch_shapes=list(scratch_shapes) + [s for ex in exs for s in ex.sem_shapes()],
        input_output_aliases=aliases, compiler_params=_params(*["arbitrary"] * len(grid)),
    )(*args, *[a for ex in exs for a in ex.inputs])
    x_res, _ = take(list(res[n_out:]), x_out)
    return list(res[:n_out]), (x_res if isinstance(exchange, (list, tuple)) else x_res[0])


def _rmsmod_fwd(x, g, shift, scale, name):
    S, D = x.shape
    tr = min(ROW_TILE, S)

    def body(x_ref, g_ref, sh_ref, sc_ref, h_ref):
        xv = x_ref[...]
        rstd = lax.rsqrt(jnp.mean(xv * xv, axis=-1, keepdims=True) + EPS)
        y = xv * rstd * g_ref[...]
        h_ref[...] = (y * (1.0 + sc_ref[...]) + sh_ref[...]).astype(h_ref.dtype)

    row = pl.BlockSpec((tr, D), lambda i: (i, 0))
    vec = pl.BlockSpec((1, D), lambda i: (0, 0))
    return pl.pallas_call(
        body, name=name, grid=(S // tr,), in_specs=[row, vec, vec, vec], out_specs=row,
        out_shape=jax.ShapeDtypeStruct((S, D), BF16), compiler_params=_params("parallel"),
    )(x, g, shift, scale)


class _NormBwd:
    def __init__(self, x, g, scale, dx_res, below=None):
        S, D = x.shape
        self.below, self.coef = below, (below[2] if below else None)
        self.inputs = [x, g, scale, dx_res] + ([below[0], below[1]] if below else [])
        vshape = jax.ShapeDtypeStruct((1, D), F32)
        self.out_shape = [jax.ShapeDtypeStruct((S, D), F32), vshape, vshape, vshape]
        if below:
            self.out_shape += [jax.ShapeDtypeStruct((S, D), BF16), vshape]

    def specs(self, tr, D):
        row = pl.BlockSpec((tr, D), lambda i: (i, 0))
        vec = pl.BlockSpec((1, D), lambda i: (0, 0))
        return ([row, vec, vec, row] + ([row, vec] if self.below else []),
                [row, vec, vec, vec] + ([row, vec] if self.below else []))

    def step(self, dhv, in_refs, out_refs):
        if self.below:
            x_ref, g_ref, sc_ref, dxr_ref, f_ref, gate_ref = in_refs
            dx_ref, dsh_ref, dsc_ref, dg_ref, df_ref, dgate_ref = out_refs
            sums = (dsh_ref, dsc_ref, dg_ref, dgate_ref)
        else:
            x_ref, g_ref, sc_ref, dxr_ref = in_refs
            dx_ref, dsh_ref, dsc_ref, dg_ref = out_refs
            sums = (dsh_ref, dsc_ref, dg_ref)

        @pl.when(pl.program_id(0) == 0)
        def _():
            for ref in sums:
                ref[...] = jnp.zeros_like(ref)

        xv, gv = x_ref[...], g_ref[...]
        one_sc = 1.0 + sc_ref[...]
        rstd = lax.rsqrt(jnp.mean(xv * xv, axis=-1, keepdims=True) + EPS)
        n = xv * rstd
        dsh_ref[...] += jnp.sum(dhv, axis=0, keepdims=True)
        dsc_ref[...] += jnp.sum(dhv * n, axis=0, keepdims=True) * gv
        dg_ref[...] += jnp.sum(dhv * n, axis=0, keepdims=True) * one_sc
        dn = dhv * (gv * one_sc)
        dx = dxr_ref[...] + rstd * (dn - n * jnp.mean(dn * n, axis=-1, keepdims=True))
        dx_ref[...] = dx
        if self.below:
            df_ref[...] = (self.coef * gate_ref[...] * dx).astype(df_ref.dtype)
            dgate_ref[...] += self.coef * jnp.sum(dx * f_ref[...].astype(F32), axis=0, keepdims=True)


def _rmsmod_bwd(dh, norm, name, exchange=None):
    S, D = dh.shape
    tr = min(ROW_TILE, S)
    n_in = len(norm.inputs)

    def body(dh_ref, *refs):
        norm.step(dh_ref[...], refs[:n_in], refs[n_in:])

    in_specs, out_specs = norm.specs(tr, D)
    return _call(body, name=name, grid=(S // tr,), in_specs=[pl.BlockSpec((tr, D), lambda i: (i, 0))] + in_specs,
                 out_specs=out_specs, out_shape=norm.out_shape, args=[dh] + norm.inputs, semantics=("arbitrary",),
                 exchange=exchange)


def _ffn_in_fwd(h, w4, name, exchange=None):
    S, D = h.shape
    tm = min(2 * ROW_TILE, S)
    n = w4.shape[2]

    def body(h_ref, wg_ref, wu_ref, zg_ref, zu_ref, a_ref):
        hv = h_ref[...]
        zg = _nn(hv, wg_ref[...])
        zu = _nn(hv, wu_ref[...])
        zg_ref[...] = zg.astype(zg_ref.dtype)
        zu_ref[...] = zu.astype(zu_ref.dtype)
        a_ref[...] = (zg * _sigmoid(zg) * zu).astype(a_ref.dtype)

    out = pl.BlockSpec((tm, n), lambda j, m: (m, j))
    oshape = jax.ShapeDtypeStruct((S, 2 * n), BF16)
    return _call(
        body, name=name, grid=(2, S // tm),
        in_specs=[pl.BlockSpec((tm, D), lambda j, m: (m, 0)),
                  pl.BlockSpec((None, D, n), lambda j, m: (j, 0, 0)),
                  pl.BlockSpec((None, D, n), lambda j, m: (j + 2, 0, 0))],
        out_specs=[out, out, out], out_shape=[oshape, oshape, oshape], args=(h, w4, w4),
        semantics=("parallel", "parallel"), exchange=exchange)


def _proj_out_fwd(lhs, w, x, gate, coef, name, exchange=None, next_norm=None):
    S, D = x.shape
    tm = min(ROW_TILE, S)
    ks = [a.shape[1] for a in lhs]

    def body(*refs):
        lhs_refs, refs = refs[:len(lhs)], refs[len(lhs):]
        if next_norm:
            w_ref, x_ref, gate_ref, g_ref, sh_ref, sc_ref, xn_ref, f_ref, h_ref = refs
        else:
            w_ref, x_ref, gate_ref, xn_ref, f_ref = refs
        acc, off = None, 0
        for a_ref, k in zip(lhs_refs, ks):
            part = _nn(a_ref[...], w_ref[off:off + k, :])
            acc = part if acc is None else acc + part
            off += k
        f_ref[...] = acc.astype(f_ref.dtype)
        xn = x_ref[...] + coef * gate_ref[...] * acc
        xn_ref[...] = xn
        if next_norm:
            rstd = lax.rsqrt(jnp.mean(xn * xn, axis=-1, keepdims=True) + EPS)
            h_ref[...] = (xn * rstd * g_ref[...] * (1.0 + sc_ref[...]) + sh_ref[...]).astype(h_ref.dtype)

    row = pl.BlockSpec((tm, D), lambda m: (m, 0))
    vec = pl.BlockSpec((1, D), lambda m: (0, 0))
    extra = list(next_norm) if next_norm else []
    return _call(
        body, name=name, grid=(S // tm,),
        in_specs=[pl.BlockSpec((tm, k), lambda m: (m, 0)) for k in ks]
        + [_resident(w.shape, lambda m: (0, 0)), row, vec] + [vec] * len(extra),
        out_specs=[row, row] + ([row] if next_norm else []),
        out_shape=[jax.ShapeDtypeStruct((S, D), F32), jax.ShapeDtypeStruct((S, D), BF16)]
        + ([jax.ShapeDtypeStruct((S, D), BF16)] if next_norm else []),
        args=(*lhs, w, x, gate, *extra), semantics=("parallel",), exchange=exchange)


def _proj_out_loss(lhs, w, x, gate, coef, target, name):
    S, D = x.shape
    tm = min(ROW_TILE, S)

    def body(a_ref, w_ref, x_ref, gate_ref, t_ref, dy_ref, df_ref, dgate_ref, sq_ref):
        @pl.when(pl.program_id(0) == 0)
        def _():
            dgate_ref[...] = jnp.zeros_like(dgate_ref)
            sq_ref[...] = jnp.zeros_like(sq_ref)

        f = _nn(a_ref[...], w_ref[...])
        gate = coef * gate_ref[...]
        err = x_ref[...] + gate * f - t_ref[...]
        sq_ref[...] += jnp.sum(err * err, axis=0, keepdims=True)
        dy = err * (1.0 / D)
        dy_ref[...] = dy
        df_ref[...] = (gate * dy).astype(df_ref.dtype)
        dgate_ref[...] += coef * jnp.sum(dy * f, axis=0, keepdims=True)

    row = pl.BlockSpec((tm, D), lambda m: (m, 0))
    vec = pl.BlockSpec((1, D), lambda m: (0, 0))
    vshape = jax.ShapeDtypeStruct((1, D), F32)
    return pl.pallas_call(
        body, name=name, grid=(S // tm,),
        in_specs=[pl.BlockSpec((tm, lhs.shape[1]), lambda m: (m, 0)), _resident(w.shape, lambda m: (0, 0)), row, vec, row],
        out_specs=[row, row, vec, vec],
        out_shape=[jax.ShapeDtypeStruct((S, D), F32), jax.ShapeDtypeStruct((S, D), BF16), vshape, vshape],
        compiler_params=_params("arbitrary"),
    )(lhs, w, x, gate, target)


def _matmul_nn(a, w, out_dtype, tm, name):
    S, K = a.shape
    N = w.shape[1]
    tm = min(tm, S)

    def body(a_ref, w_ref, o_ref):
        o_ref[...] = _nn(a_ref[...], w_ref[...]).astype(o_ref.dtype)

    return pl.pallas_call(
        body, name=name, grid=(S // tm,),
        in_specs=[pl.BlockSpec((tm, K), lambda m: (m, 0)), _resident((K, N), lambda m: (0, 0))],
        out_specs=pl.BlockSpec((tm, N), lambda m: (m, 0)), out_shape=jax.ShapeDtypeStruct((S, N), out_dtype),
        compiler_params=_params("parallel"),
    )(a, w)


def _dact_bwd(df, w_out, zg, zu, name, exchange=None):
    S, D = df.shape
    tm = min(ROW_TILE, S)
    n = w_out.shape[0] // 2

    def body(df_ref, w_ref, zg_ref, zu_ref, dzg_ref, dzu_ref):
        da = _nt(df_ref[...], w_ref[...]).astype(BF16)
        zg_v, zu_v = zg_ref[...], zu_ref[...]
        s = _sigmoid(zg_v)
        dzu_ref[...] = da * zg_v * s
        dzg_ref[...] = da * zu_v * (s * (1.0 + zg_v * (1.0 - s)))

    blk = pl.BlockSpec((tm, n), lambda j, m: (m, j))
    oshape = jax.ShapeDtypeStruct((S, 2 * n), BF16)
    return _call(
        body, name=name, grid=(2, S // tm),
        in_specs=[pl.BlockSpec((tm, D), lambda j, m: (m, 0)), pl.BlockSpec((n, D), lambda j, m: (j, 0)), blk, blk],
        out_specs=[blk, blk], out_shape=[oshape, oshape], args=(df, w_out, zg, zu), semantics=("parallel", "parallel"),
        exchange=exchange)


def _ffn_in_dgrad(dzg, dzu, w4, name, exchange=None, norm=None):
    S = dzg.shape[0]
    D, n = w4.shape[1], w4.shape[2]
    tm = min(ROW_TILE, S)
    n_norm = len(norm.inputs) if norm else 0

    def body(dzg_ref, dzu_ref, w_ref, *refs):
        acc = _nt(dzg_ref[:, 0:n], w_ref[0])
        acc += _nt(dzg_ref[:, n:2 * n], w_ref[1])
        acc += _nt(dzu_ref[:, 0:n], w_ref[2])
        acc += _nt(dzu_ref[:, n:2 * n], w_ref[3])
        if norm:
            norm.step(acc, refs[:n_norm], refs[n_norm:])
        else:
            refs[0][...] = acc

    blk = pl.BlockSpec((tm, 2 * n), lambda m: (m, 0))
    in_specs, args = [blk, blk, _resident(w4.shape, lambda m: (0, 0, 0))], [dzg, dzu, w4]
    out_specs, out_shape = [pl.BlockSpec((tm, D), lambda m: (m, 0))], [jax.ShapeDtypeStruct((S, D), F32)]
    if norm:
        norm_in, out_specs = norm.specs(tm, D)
        in_specs, args, out_shape = in_specs + norm_in, args + norm.inputs, norm.out_shape
    return _call(body, name=name, grid=(S // tm,), in_specs=in_specs, out_specs=out_specs, out_shape=out_shape, args=args,
                 semantics=("arbitrary",) if norm else ("parallel",), exchange=exchange)


def _matmul_nt(pieces, w, tm, name, exchange=None, norm=None):
    S = pieces[0].shape[0]
    ks = [p.shape[1] for p in pieces]
    N = w.shape[0]
    tm = min(tm, S)
    n_norm = len(norm.inputs) if norm else 0

    def body(*refs):
        p_refs, w_ref, refs = refs[:len(ks)], refs[len(ks)], refs[len(ks) + 1:]
        acc, off = None, 0
        for p_ref, k in zip(p_refs, ks):
            part = _nt(p_ref[...], w_ref[:, off:off + k])
            acc = part if acc is None else acc + part
            off += k
        if norm:
            norm.step(acc, refs[:n_norm], refs[n_norm:])
        else:
            refs[0][...] = acc

    in_specs = [pl.BlockSpec((tm, k), lambda m: (m, 0)) for k in ks] + [_resident(w.shape, lambda m: (0, 0))]
    args = list(pieces) + [w]
    out_specs, out_shape = [pl.BlockSpec((tm, N), lambda m: (m, 0))], [jax.ShapeDtypeStruct((S, N), F32)]
    if norm:
        norm_in, out_specs = norm.specs(tm, N)
        in_specs, args, out_shape = in_specs + norm_in, args + norm.inputs, norm.out_shape
    return _call(body, name=name, grid=(S // tm,), in_specs=in_specs, out_specs=out_specs, out_shape=out_shape, args=args,
                 semantics=("arbitrary",) if norm else ("parallel",), exchange=exchange)


def _wgrad(a, gs, tn, name, exchange=None):
    S, Ka = a.shape
    N = gs[0].shape[1]
    ts = min(ROW_TILE * (2 if Ka <= D_MODEL else 1), S)

    def body(a_ref, *refs):
        g_refs, o_ref = refs[:-1], refs[-1]

        @pl.when(pl.program_id(1) == 0)
        def _():
            o_ref[...] = jnp.zeros_like(o_ref)

        a_t = a_ref[...].T
        for i, g_ref in enumerate(g_refs):
            o_ref[i] += _nn(a_t, g_ref[...])

    return _call(
        body, name=name, grid=(N // tn, S // ts),
        in_specs=[pl.BlockSpec((ts, Ka), lambda j, s: (s, 0))] + [pl.BlockSpec((ts, tn), lambda j, s: (s, j))] * len(gs),
        out_specs=[pl.BlockSpec((len(gs), None, Ka, tn), lambda j, s: (0, j, 0, 0))],
        out_shape=[jax.ShapeDtypeStruct((len(gs), N // tn, Ka, tn), F32)], args=(a, *gs),
        semantics=("parallel", "arbitrary"), exchange=exchange)


def _wgrad_pieces(a, pieces, tn, name):
    S, Ka = a.shape
    ts = min(ROW_TILE, S)
    blocks = [(i, j) for i, p in enumerate(pieces) for j in range(p.shape[1] // tn)]

    def body(a_ref, *refs):
        g_refs, o_ref = refs[:-1], refs[-1]

        @pl.when(pl.program_id(0) == 0)
        def _():
            o_ref[...] = jnp.zeros_like(o_ref)

        a_t = a_ref[...].T
        for b, g_ref in enumerate(g_refs):
            o_ref[b] += _nn(a_t, g_ref[...])

    return pl.pallas_call(
        body, name=name, grid=(S // ts,),
        in_specs=[pl.BlockSpec((ts, Ka), lambda s: (s, 0))] + [pl.BlockSpec((ts, tn), lambda s, j=j: (s, j)) for _, j in blocks],
        out_specs=pl.BlockSpec((len(blocks), Ka, tn), lambda s: (0, 0, 0)),
        out_shape=jax.ShapeDtypeStruct((len(blocks), Ka, tn), F32), compiler_params=_params("arbitrary"),
    )(a, *[pieces[i] for i, _ in blocks])


def _wgrad_rows(lhs, g, name):
    S, Ka = lhs[0].shape
    N = g.shape[1]
    ts = min(ROW_TILE, S)

    def body(*refs):
        a_refs, g_ref, o_ref = refs[:-2], refs[-2], refs[-1]

        @pl.when(pl.program_id(0) == 0)
        def _():
            o_ref[...] = jnp.zeros_like(o_ref)

        gv = g_ref[...]
        for i, a_ref in enumerate(a_refs):
            o_ref[i] += _tn(a_ref[...], gv)

    return pl.pallas_call(
        body, name=name, grid=(S // ts,),
        in_specs=[pl.BlockSpec((ts, Ka), lambda s: (s, 0))] * len(lhs) + [pl.BlockSpec((ts, N), lambda s: (s, 0))],
        out_specs=pl.BlockSpec((len(lhs), Ka, N), lambda s: (0, 0, 0)),
        out_shape=jax.ShapeDtypeStruct((len(lhs), Ka, N), F32), compiler_params=_params("arbitrary"),
    )(*lhs, g)


def _hgrn_chunk_common(qr, fr, lb, oml, tri, last):
    sig_nf = _sigmoid(-fr)
    k = oml * sig_nf
    f_small = lb + oml * (jnp.exp(jnp.minimum(fr, 0.0)) * sig_nf)
    use_k = k < 0.5
    f = jnp.where(use_k, 1.0 - k, f_small)
    g = jnp.where(use_k, jnp.log1p(-k), jnp.log(f_small)) * math.log2(math.e)
    q = qr * _sigmoid(qr)
    G = _nn(tri, g, precision=lax.Precision.HIGHEST)
    Gl = G[last:last + 1]
    return q, k, f, G, Gl


def _hgrn_consts(reverse):
    C = HG_CHUNK
    r = lax.broadcasted_iota(jnp.int32, (C, C), 0)
    cc = lax.broadcasted_iota(jnp.int32, (C, C), 1)
    tri = ((cc >= r) if reverse else (cc <= r)).astype(F32)
    tri_t = ((cc <= r) if reverse else (cc >= r)).astype(F32)
    rid = lax.broadcasted_iota(jnp.int32, (C, HG_WIDTH), 0)
    return tri, tri_t, rid, (0 if reverse else C - 1)


def _head_slices():
    return [slice(h * HG_DIM, (h + 1) * HG_DIM) for h in range(HG_HEADS)]


def _per_head_lane_sum(x):
    C = x.shape[0]
    return jnp.concatenate(
        [jnp.broadcast_to(jnp.sum(x[:, sl], axis=-1, keepdims=True), (C, HG_DIM)) for sl in _head_slices()], axis=1)


HG_TILE = 8


def _pair_tiles(s, reverse):
    blk, r = divmod(s, HG_TILE)
    n_tiles = HG_CHUNK // HG_TILE
    others = range(0, blk) if reverse else range(blk + 1, n_tiles)
    return [(blk, r)] + [(t, None) for t in others]


def _pair_decay(G, s, tile, r, rid8, reverse, keys=False):
    rs = slice(tile * HG_TILE, (tile + 1) * HG_TILE)
    d = (G[s:s + 1] - G[rs]) if keys else (G[rs] - G[s:s + 1])
    if r is not None:
        d = jnp.where((rid8 <= r) if reverse else (rid8 >= r), d, NEG)
    return rs, jnp.exp2(d)


def _hgrn_fwd_both(z, lbs, name, exchange=None):
    S = z.shape[0]
    C, DK, W = HG_CHUNK, HG_DIM, HG_WIDTH
    tb = min(HG_ROWS, S)
    n_t, n_c = S // tb, tb // C
    dirs = (0, 1)

    def body(qf_ref, ff_ref, vf_ref, qb_ref, fb_ref, vb_ref, lbf_ref, lbb_ref, of_ref, stf_out, ob_ref, stb_out, st_ref):
        @pl.when(pl.program_id(0) == 0)
        def _():
            st_ref[...] = jnp.zeros_like(st_ref)

        q_refs, f_refs, v_refs, lb_refs = (qf_ref, qb_ref), (ff_ref, fb_ref), (vf_ref, vb_ref), (lbf_ref, lbb_ref)
        o_refs, st_outs = (of_ref, ob_ref), (stf_out, stb_out)
        consts = [_hgrn_consts(d == 1) for d in dirs]
        rid8 = lax.broadcasted_iota(jnp.int32, (HG_TILE, W), 0)

        def chunk(ci, carry):
            cidx = [ci, n_c - 1 - ci]
            rows = [pl.ds(pl.multiple_of(c * C, C), C) for c in cidx]
            v = [v_refs[d][rows[d], :] for d in dirs]
            com = [_hgrn_chunk_common(q_refs[d][rows[d], :], f_refs[d][rows[d], :], lb_refs[d][0:1, :], lb_refs[d][1:2, :],
                                      consts[d][0], consts[d][3]) for d in dirs]
            q, k, G, Gl = [c[0] for c in com], [c[1] for c in com], [c[3] for c in com], [c[4] for c in com]
            qd = [(q[d] * jnp.exp2(G[d])).astype(BF16) for d in dirs]
            kd = [(k[d] * jnp.exp2(Gl[d] - G[d])).astype(BF16) for d in dirs]
            e_gl = [jnp.exp2(Gl[d]) for d in dirs]
            v_b = [v[d].astype(BF16) for d in dirs]
            inter = [[], []]
            for h, sl in enumerate(_head_slices()):
                for d in dirs:
                    st0 = st_ref[d, h]
                    st_outs[d][h, cidx[d]] = st0
                    inter[d].append(_nt(qd[d][:, sl], st0.astype(BF16)))
                    st_ref[d, h] = st0 * e_gl[d][:, sl] + _tn(v_b[d][:, sl], kd[d][:, sl])
            o_t = [[jnp.concatenate(inter[d], axis=1)[t * HG_TILE:(t + 1) * HG_TILE] for t in range(C // HG_TILE)] for d in dirs]
            for s in range(C):
                for d in dirs:
                    k_s, v_s = k[d][s:s + 1], v[d][s:s + 1]
                    for tile, r in _pair_tiles(s, d == 1):
                        rs, e_s = _pair_decay(G[d], s, tile, r, rid8, d == 1)
                        o_t[d][tile] = o_t[d][tile] + _per_head_lane_sum(q[d][rs] * k_s * e_s) * v_s
            for d in dirs:
                o_refs[d][rows[d], :] = jnp.concatenate(o_t[d], axis=0)
            return carry

        lax.fori_loop(0, n_c, chunk, 0, unroll=4)

    def sec(j, back):
        return pl.BlockSpec((tb, W), (lambda i: (n_t - 1 - i, j)) if back else (lambda i: (i, j)))

    def st_spec(back):
        return pl.BlockSpec((HG_HEADS, n_c, DK, DK), (lambda i: (0, n_t - 1 - i, 0, 0)) if back else (lambda i: (0, i, 0, 0)))

    vec = pl.BlockSpec((2, W), lambda i: (0, 0))
    o_shape = jax.ShapeDtypeStruct((S, W), F32)
    st_shape = jax.ShapeDtypeStruct((HG_HEADS, S // C, DK, DK), F32)
    return _call(
        body, name=name, grid=(n_t,),
        in_specs=[sec(0, False), sec(1, False), sec(3, False), sec(0, True), sec(2, True), sec(3, True), vec, vec],
        out_specs=[sec(0, False), st_spec(False), sec(0, True), st_spec(True)],
        out_shape=[o_shape, st_shape, o_shape, st_shape],
        scratch_shapes=[pltpu.VMEM((2, HG_HEADS, DK, DK), F32)], args=(z, z, z, z, z, z, lbs[0], lbs[1]),
        semantics=("arbitrary",), exchange=exchange)


def _hgrn_bwd(z, lb, do, states, direction, name, acc=None, exchange=None):
    S = z.shape[0]
    C, DK, W = HG_CHUNK, HG_DIM, HG_WIDTH
    tb = min(HG_ROWS, S)
    n_t, n_c = S // tb, tb // C
    reverse = direction == 1
    tmap = (lambda i: i) if reverse else (lambda i: n_t - 1 - i)

    def body(*refs):
        if acc:
            q_ref, f_ref, v_ref, lb_ref, do_ref, st_in_ref, dqa_ref, dva_ref, dq_ref, df_ref, dv_ref, doml_ref, dst_ref = refs
        else:
            q_ref, f_ref, v_ref, lb_ref, do_ref, st_in_ref, dq_ref, df_ref, dv_ref, doml_ref, dst_ref = refs

        @pl.when(pl.program_id(0) == 0)
        def _():
            dst_ref[...] = jnp.zeros_like(dst_ref)
            doml_ref[...] = jnp.zeros_like(doml_ref)

        lbv, oml = lb_ref[0:1, :], lb_ref[1:2, :]
        tri, tri_t, rid, last = _hgrn_consts(reverse)
        rid8 = lax.broadcasted_iota(jnp.int32, (HG_TILE, W), 0)

        def chunk(ci, carry):
            cidx = ci if reverse else (n_c - 1 - ci)
            rows = pl.ds(pl.multiple_of(cidx * C, C), C)
            qr, fr, v, dov = q_ref[rows, :], f_ref[rows, :], v_ref[rows, :], do_ref[rows, :]
            q, k, f, G, Gl = _hgrn_chunk_common(qr, fr, lbv, oml, tri, last)
            e_g, e_gl, e_kd = jnp.exp2(G), jnp.exp2(Gl), jnp.exp2(Gl - G)
            qd, kd = q * e_g, k * e_kd
            do_b, v_b, qd_b, kd_b = dov.astype(BF16), v.astype(BF16), qd.astype(BF16), kd.astype(BF16)
            dqd, dkd, dv, state_dot = [], [], [], []
            for h, sl in enumerate(_head_slices()):
                st0, dst1 = st_in_ref[h, cidx], dst_ref[h]
                dst1_b = dst1.astype(BF16)
                dqd.append(_nn(do_b[:, sl], st0.astype(BF16)))
                dkd.append(_nn(v_b[:, sl], dst1_b))
                dv.append(_nt(kd_b[:, sl], dst1_b))
                state_dot.append(jnp.sum(st0 * dst1, axis=0, keepdims=True))
                dst_ref[h] = dst1 * e_gl[:, sl] + _tn(do_b[:, sl], qd_b[:, sl])
            dqd, dkd, dv = [jnp.concatenate(t, axis=1) for t in (dqd, dkd, dv)]
            d_gl = e_gl * jnp.concatenate(state_dot, axis=1) + jnp.sum(dkd * kd, axis=0, keepdims=True)
            dq, dk = dqd * e_g, dkd * e_kd
            n_tiles = C // HG_TILE
            dq_t, dk_t, dv_t = [[x[t * HG_TILE:(t + 1) * HG_TILE] for t in range(n_tiles)] for x in (dq, dk, dv)]
            for s in range(C):
                k_s, v_s = k[s:s + 1], v[s:s + 1]
                for tile, r in _pair_tiles(s, reverse):
                    rs, e_s = _pair_decay(G, s, tile, r, rid8, reverse)
                    dq_t[tile] = dq_t[tile] + _per_head_lane_sum(dov[rs] * v_s) * e_s * k_s
            for t in range(C):
                q_t, do_t = q[t:t + 1], dov[t:t + 1]
                for tile, r in _pair_tiles(t, not reverse):
                    rs, x_t = _pair_decay(G, t, tile, r, rid8, not reverse, keys=True)
                    qx = q_t * x_t
                    dv_t[tile] = dv_t[tile] + _per_head_lane_sum(k[rs] * qx) * do_t
                    dk_t[tile] = dk_t[tile] + _per_head_lane_sum(v[rs] * do_t) * qx
            dq, dk, dv = [jnp.concatenate(x, axis=0) for x in (dq_t, dk_t, dv_t)]
            d_big_g = dq * q - dk * k + jnp.where(rid == last, d_gl, 0.0)
            dg = _nn(tri_t, d_big_g, precision=lax.Precision.HIGHEST)
            dk_all = dk - dg / f
            sig_nf = _sigmoid(-fr)
            df_ref[rows, :] = (-dk_all * k * (1.0 - sig_nf)).astype(df_ref.dtype)
            doml_ref[...] += jnp.sum(dk_all * sig_nf, axis=0, keepdims=True)
            sq = _sigmoid(qr)
            dqr = dq * (sq * (1.0 + qr * (1.0 - sq)))
            if acc:
                dqr = dqr + dqa_ref[rows, :]
                dv = dv + dva_ref[rows, :]
            dq_ref[rows, :] = dqr.astype(dq_ref.dtype)
            dv_ref[rows, :] = dv.astype(dv_ref.dtype)
            return carry

        lax.fori_loop(0, n_c, chunk, 0, unroll=8)

    def sec(j):
        return pl.BlockSpec((tb, W), lambda i: (tmap(i), j))

    vec = pl.BlockSpec((1, W), lambda i: (0, 0))
    ins = [z, z, z, lb, do, states]
    in_specs = [sec(0), sec(1 + direction), sec(3), pl.BlockSpec((2, W), lambda i: (0, 0)), sec(0),
                pl.BlockSpec((HG_HEADS, n_c, DK, DK), lambda i: (0, tmap(i), 0, 0))]
    if acc:
        ins += list(acc)
        in_specs += [sec(0), sec(0)]
    final = jax.ShapeDtypeStruct((S, W), BF16)
    partial = final if acc else jax.ShapeDtypeStruct((S, W), F32)
    return _call(
        body, name=name, grid=(n_t,), in_specs=in_specs,
        out_specs=[sec(0), sec(0), sec(0), vec],
        out_shape=[partial, final, partial, jax.ShapeDtypeStruct((1, W), F32)],
        scratch_shapes=[pltpu.VMEM((HG_HEADS, DK, DK), F32)], args=ins, semantics=("arbitrary",), exchange=exchange)


def _hgrn_post_fwd(o_f, o_b, z, norm_g, name):
    S = z.shape[0]
    tr = min(ROW_TILE, S)

    def body(of_ref, ob_ref, gr_ref, ng_ref, y_ref):
        o = of_ref[...] + ob_ref[...]
        gr = gr_ref[...]
        gate = gr * _sigmoid(gr)
        ng = ng_ref[...]
        for h in range(HG_HEADS):
            sl = slice(h * HG_DIM, (h + 1) * HG_DIM)
            oh = o[:, sl]
            rstd = lax.rsqrt(jnp.mean(oh * oh, axis=-1, keepdims=True) + EPS)
            y_ref[:, sl] = (oh * rstd * ng[:, sl] * gate[:, sl]).astype(y_ref.dtype)

    row = pl.BlockSpec((tr, HG_WIDTH), lambda i: (i, 0))
    return pl.pallas_call(
        body, name=name, grid=(S // tr,),
        in_specs=[row, row, pl.BlockSpec((tr, HG_WIDTH), lambda i: (i, 4)), pl.BlockSpec((1, HG_WIDTH), lambda i: (0, 0))],
        out_specs=row, out_shape=jax.ShapeDtypeStruct((S, HG_WIDTH), BF16), compiler_params=_params("parallel"),
    )(o_f, o_b, z, norm_g)


def _hgrn_post_bwd(dy, o_f, o_b, z, norm_g, name):
    S = z.shape[0]
    tr = min(ROW_TILE, S)

    def body(dy_ref, of_ref, ob_ref, gr_ref, ng_ref, do_ref, dgr_ref, dng_ref):
        @pl.when(pl.program_id(0) == 0)
        def _():
            dng_ref[...] = jnp.zeros_like(dng_ref)

        o = of_ref[...] + ob_ref[...]
        gr, ng, dyv = gr_ref[...], ng_ref[...], dy_ref[...]
        sg = _sigmoid(gr)
        for h in range(HG_HEADS):
            sl = slice(h * HG_DIM, (h + 1) * HG_DIM)
            oh, dyh, grh, sgh, ngh = o[:, sl], dyv[:, sl], gr[:, sl], sg[:, sl], ng[:, sl]
            rstd = lax.rsqrt(jnp.mean(oh * oh, axis=-1, keepdims=True) + EPS)
            on = oh * rstd
            du = dyh * (grh * sgh)
            dgr_ref[:, sl] = (dyh * (on * ngh) * (sgh * (1.0 + grh * (1.0 - sgh)))).astype(dgr_ref.dtype)
            dng_ref[:, sl] += jnp.sum(du * on, axis=0, keepdims=True)
            don = du * ngh
            do_ref[:, sl] = rstd * (don - on * jnp.mean(don * on, axis=-1, keepdims=True))

    row = pl.BlockSpec((tr, HG_WIDTH), lambda i: (i, 0))
    vec = pl.BlockSpec((1, HG_WIDTH), lambda i: (0, 0))
    full = jax.ShapeDtypeStruct((S, HG_WIDTH), F32)
    return pl.pallas_call(
        body, name=name, grid=(S // tr,),
        in_specs=[row, row, row, pl.BlockSpec((tr, HG_WIDTH), lambda i: (i, 4)), vec],
        out_specs=[row, row, vec],
        out_shape=[full, jax.ShapeDtypeStruct((S, HG_WIDTH), BF16), jax.ShapeDtypeStruct((1, HG_WIDTH), F32)],
        compiler_params=_params("arbitrary"),
    )(dy, o_f, o_b, z, norm_g)


def _t5_bucket_table():
    rel = (np.arange(3 * BLOCK)[None, :] - BLOCK) - np.arange(BLOCK)[:, None]
    nb = NUM_BUCKETS // 2
    max_exact = nb // 2
    ret = (rel > 0).astype(np.int32) * nb
    n = np.abs(rel)
    ratio = np.log(np.maximum(n, 1).astype(np.float32) / np.float32(max_exact)) / np.float32(math.log(MAX_DISTANCE / max_exact))
    large = max_exact + (ratio.astype(np.float32) * np.float32(nb - max_exact)).astype(np.int32)
    large = np.minimum(large, nb - 1)
    bucket = ret + np.where(n < max_exact, n, large)
    return bucket.astype(np.int32), (n <= WINDOW)


def _bias_table(rel_bias, name):
    bucket, in_band = _t5_bucket_table()
    idx = jnp.asarray(np.where(in_band, bucket, -1))

    def body(rb_ref, idx_ref, o_ref):
        h = pl.program_id(0)
        iv = idx_ref[...]
        acc = jnp.where(iv < 0, NEG, 0.0).astype(F32)
        for b in range(NUM_BUCKETS):
            acc = acc + jnp.where(iv == b, rb_ref[b, h], 0.0)
        o_ref[...] = acc

    return pl.pallas_call(
        body, name=name, grid=(ATT_Q_HEADS,),
        in_specs=[pl.BlockSpec(memory_space=pltpu.SMEM), pl.BlockSpec((BLOCK, 3 * BLOCK), lambda h: (0, 0))],
        out_specs=pl.BlockSpec((None, BLOCK, 3 * BLOCK), lambda h: (h, 0, 0)),
        out_shape=jax.ShapeDtypeStruct((ATT_Q_HEADS, BLOCK, 3 * BLOCK), F32), compiler_params=_params("parallel"),
    )(rel_bias, idx)


def _bias_grad(ds_sum_t, name):
    bucket, in_band = _t5_bucket_table()
    idx_t = jnp.asarray(np.where(in_band, bucket, -1).T)

    def body(ds_ref, idx_ref, o_ref):
        iv, ds = idx_ref[...], ds_ref[...]
        for b in range(NUM_BUCKETS):
            o_ref[b:b + 1, :] = jnp.sum(jnp.where(iv == b, ds, 0.0), axis=0, keepdims=True)

    return pl.pallas_call(
        body, name=name, grid=(ATT_Q_HEADS,),
        in_specs=[pl.BlockSpec((None, 3 * BLOCK, BLOCK), lambda h: (h // ATT_GROUP, 0, h % ATT_GROUP)),
                  pl.BlockSpec((3 * BLOCK, BLOCK), lambda h: (0, 0))],
        out_specs=pl.BlockSpec((None, NUM_BUCKETS, BLOCK), lambda h: (h, 0, 0)),
        out_shape=jax.ShapeDtypeStruct((ATT_Q_HEADS, NUM_BUCKETS, BLOCK), F32), compiler_params=_params("parallel"),
    )(ds_sum_t, idx_t)


Q_COL = 5 * HG_WIDTH
KV_COL = Q_COL + ATT_WIDTH
GROUP_WIDTH = ATT_GROUP * ATT_HEAD_DIM


def _stack_heads(blk):
    dh = ATT_HEAD_DIM
    return jnp.concatenate([blk[:, g * dh:(g + 1) * dh] for g in range(ATT_GROUP)], axis=0)


def _unstack_heads(st):
    return jnp.concatenate([st[g * BLOCK:(g + 1) * BLOCK] for g in range(ATT_GROUP)], axis=1)


def _rms_rows(x):
    rstd = lax.rsqrt(jnp.mean(x * x, axis=-1, keepdims=True) + EPS)
    return x * rstd, rstd


def _edge_ok(n, nb):
    colid = lax.broadcasted_iota(jnp.int32, (ATT_GROUP * BLOCK, 3 * BLOCK), 1)
    return jnp.logical_and(jnp.logical_or(colid >= BLOCK, n > 0), jnp.logical_or(colid < 2 * BLOCK, n < nb - 1))


def _sink_column(sink_ref, j=0):
    heads = range(j * ATT_GROUP, (j + 1) * ATT_GROUP)
    return jnp.concatenate([jnp.broadcast_to(sink_ref[h][:, 0:1], (BLOCK, 1)) for h in heads], axis=0)


def _attn_fwd(z, q_g, k_g, sink, bias, name):
    S = z.shape[0]
    nb = S // BLOCK
    G, dh, KV = ATT_GROUP, ATT_HEAD_DIM, ATT_KV_HEADS
    scale = 1.0 / math.sqrt(dh)

    def body(q_ref, kv0, kv1, kv2, qg_ref, kg_ref, sink_ref, bias_ref, o_ref):
        n = pl.program_id(0)
        edge_ok = _edge_ok(n, nb)
        cat = jnp.concatenate([kv0[...], kv1[...], kv2[...]], axis=0)
        qblk = q_ref[...]
        kn = [(_rms_rows(cat[:, j * dh:(j + 1) * dh])[0] * kg_ref[...]).astype(BF16) for j in range(KV)]
        vb = [cat[:, (KV + j) * dh:(KV + j + 1) * dh].astype(BF16) for j in range(KV)]
        qn = [(_rms_rows(_stack_heads(qblk[:, j * GROUP_WIDTH:(j + 1) * GROUP_WIDTH]))[0] * (qg_ref[...] * scale)).astype(BF16)
              for j in range(KV)]
        s = [_nt(qn[j], kn[j]) + bias_ref[j * G:(j + 1) * G].reshape(G * BLOCK, 3 * BLOCK) for j in range(KV)]
        s = [jnp.where(edge_ok, sj, NEG) for sj in s]
        sinks = [_sink_column(sink_ref, j) for j in range(KV)]
        m = [jnp.maximum(jnp.max(s[j], axis=-1, keepdims=True), sinks[j]) for j in range(KV)]
        e = [jnp.exp(s[j] - m[j]) for j in range(KV)]
        den = [jnp.sum(e[j], axis=-1, keepdims=True) + jnp.exp(sinks[j] - m[j]) for j in range(KV)]
        o = [_nn(e[j].astype(BF16), vb[j]) * (1.0 / den[j]) for j in range(KV)]
        o_ref[...] = jnp.concatenate([_unstack_heads(oj) for oj in o], axis=1).astype(o_ref.dtype)

    def kv(shift):
        return pl.BlockSpec((BLOCK, 2 * KV_WIDTH), lambda n: (jnp.clip(n + shift, 0, nb - 1), KV_COL // (2 * KV_WIDTH)))

    gain = pl.BlockSpec((1, dh), lambda n: (0, 0))
    return pl.pallas_call(
        body, name=name, grid=(nb,),
        in_specs=[pl.BlockSpec((BLOCK, ATT_WIDTH), lambda n: (n, Q_COL // ATT_WIDTH)), kv(-1), kv(0), kv(1), gain, gain,
                  pl.BlockSpec((ATT_Q_HEADS, 1, BLOCK), lambda n: (0, 0, 0)),
                  pl.BlockSpec((ATT_Q_HEADS, BLOCK, 3 * BLOCK), lambda n: (0, 0, 0))],
        out_specs=pl.BlockSpec((BLOCK, ATT_WIDTH), lambda n: (n, 0)),
        out_shape=jax.ShapeDtypeStruct((S, ATT_WIDTH), BF16), compiler_params=_params("parallel"),
    )(z, z, z, z, q_g, k_g, sink, bias)


def _attn_bwd(z, q_g, k_g, sink, bias, do, name):
    S = z.shape[0]
    nb = S // BLOCK
    G, dh, KV = ATT_GROUP, ATT_HEAD_DIM, ATT_KV_HEADS
    scale = 1.0 / math.sqrt(dh)
    both = range(KV)
    bias_t = bias.reshape(KV, G, BLOCK, 3 * BLOCK).transpose(0, 3, 1, 2).reshape(KV, 3 * BLOCK, G * BLOCK)

    def body(q_ref, kv0, kv1, kv2, qg_ref, kg_ref, sink_ref, bias_ref, do_ref,
             dq_ref, dkw_ref, dvw_ref, ds_ref, dsink_ref, dqg_ref):
        n = pl.program_id(0)

        @pl.when(n == 0)
        def _():
            ds_ref[...] = jnp.zeros_like(ds_ref)
            dsink_ref[...] = jnp.zeros_like(dsink_ref)
            dqg_ref[...] = jnp.zeros_like(dqg_ref)

        rowid = lax.broadcasted_iota(jnp.int32, (3 * BLOCK, G * BLOCK), 0)
        edge_ok = jnp.logical_and(jnp.logical_or(rowid >= BLOCK, n > 0), jnp.logical_or(rowid < 2 * BLOCK, n < nb - 1))
        qg = qg_ref[...]
        cat = jnp.concatenate([kv0[...], kv1[...], kv2[...]], axis=0)
        qblk, doblk = q_ref[...], do_ref[...]
        kn = [(_rms_rows(cat[:, j * dh:(j + 1) * dh])[0] * kg_ref[...]).astype(BF16) for j in both]
        vb = [cat[:, (KV + j) * dh:(KV + j + 1) * dh].astype(BF16) for j in both]
        norm = [_rms_rows(_stack_heads(qblk[:, j * GROUP_WIDTH:(j + 1) * GROUP_WIDTH])) for j in both]
        qn = [(norm[j][0] * (qg * scale)).astype(BF16) for j in both]
        do_b = [_stack_heads(doblk[:, j * GROUP_WIDTH:(j + 1) * GROUP_WIDTH]).astype(BF16) for j in both]
        s = [_nt(kn[j], qn[j]) + bias_ref[j] for j in both]
        dp = [_nt(vb[j], do_b[j]) for j in both]
        s = [jnp.where(edge_ok, sj, NEG) for sj in s]
        sinks = [jnp.concatenate([sink_ref[j * G + g] for g in range(G)], axis=1) for j in both]
        m = [jnp.maximum(jnp.max(s[j], axis=0, keepdims=True), sinks[j]) for j in both]
        e = [jnp.exp(s[j] - m[j]) for j in both]
        e_sink = [jnp.exp(sinks[j] - m[j]) for j in both]
        inv = [1.0 / (jnp.sum(e[j], axis=0, keepdims=True) + e_sink[j]) for j in both]
        p = [e[j] * inv[j] for j in both]
        delta = [jnp.sum(p[j] * dp[j], axis=0, keepdims=True) for j in both]
        ds = [p[j] * (dp[j] - delta[j]) for j in both]
        ds_b = [dsj.astype(BF16) for dsj in ds]
        dqn = [_tn(kn[j], ds_b[j]).T * scale for j in both]
        for j in both:
            dvw_ref[j] = _nn(p[j].astype(BF16), do_b[j])
            dkw_ref[j] = _nn(ds_b[j], qn[j])
        for j in both:
            ds_ref[j] += ds[j]
            sink_term = e_sink[j] * inv[j] * delta[j]
            for g in range(G):
                dsink_ref[j * G + g] += (jnp.zeros((1, BLOCK), F32)
                                         - jnp.sum(sink_term[:, g * BLOCK:(g + 1) * BLOCK], axis=1, keepdims=True))
        dq = []
        for j in both:
            qhat, rstd = norm[j]
            dqg_ref[j] += jnp.sum(dqn[j] * qhat, axis=0, keepdims=True)
            dqh = dqn[j] * qg
            dq.append(_unstack_heads(rstd * (dqh - qhat * jnp.mean(dqh * qhat, axis=-1, keepdims=True))))
        dq_ref[...] = jnp.concatenate(dq, axis=1).astype(dq_ref.dtype)

    def kv(shift):
        return pl.BlockSpec((BLOCK, 2 * KV_WIDTH), lambda n: (jnp.clip(n + shift, 0, nb - 1), KV_COL // (2 * KV_WIDTH)))

    gain = pl.BlockSpec((1, dh), lambda n: (0, 0))
    sink_spec = pl.BlockSpec((ATT_Q_HEADS, 1, BLOCK), lambda n: (0, 0, 0))
    bias_spec = pl.BlockSpec((KV, 3 * BLOCK, G * BLOCK), lambda n: (0, 0, 0))
    win = pl.BlockSpec((KV, None, 3 * BLOCK, dh), lambda n: (0, n, 0, 0))
    wshape = jax.ShapeDtypeStruct((KV, nb, 3 * BLOCK, dh), F32)
    return pl.pallas_call(
        body, name=name, grid=(nb,),
        in_specs=[pl.BlockSpec((BLOCK, ATT_WIDTH), lambda n: (n, Q_COL // ATT_WIDTH)), kv(-1), kv(0), kv(1), gain, gain,
                  sink_spec, bias_spec, pl.BlockSpec((BLOCK, ATT_WIDTH), lambda n: (n, HG_WIDTH // ATT_WIDTH))],
        out_specs=[pl.BlockSpec((BLOCK, ATT_WIDTH), lambda n: (n, 0)), win, win, bias_spec, sink_spec,
                   pl.BlockSpec((KV, 1, dh), lambda n: (0, 0, 0))],
        out_shape=[jax.ShapeDtypeStruct((S, ATT_WIDTH), BF16), wshape, wshape,
                   jax.ShapeDtypeStruct((KV, 3 * BLOCK, G * BLOCK), F32),
                   jax.ShapeDtypeStruct((ATT_Q_HEADS, 1, BLOCK), F32),
                   jax.ShapeDtypeStruct((KV, 1, dh), F32)],
        compiler_params=_params("arbitrary"),
    )(z, z, z, z, q_g, k_g, sink, bias_t, do)


def _attn_kv_reduce(dkw, dvw, z, k_g, name):
    S = z.shape[0]
    nb = S // BLOCK
    dh = ATT_HEAD_DIM
    kb = min(8, nb)
    steps = nb // kb

    def body(a_lo, a, a_hi, b_lo, b, b_hi, kv_ref, kg_ref, dkv_ref, dkg_ref):
        n = pl.program_id(0)

        @pl.when(n == 0)
        def _():
            dkg_ref[...] = jnp.zeros_like(dkg_ref)

        lo = jnp.where(n > 0, 1.0, 0.0)
        hi = jnp.where(n < steps - 1, 1.0, 0.0)

        def overlap_add(w, w_lo, w_hi, j, i):
            before = lo * w_lo[j] if i == 0 else w[j, i - 1, 2 * BLOCK:3 * BLOCK, :]
            after = hi * w_hi[j] if i == kb - 1 else w[j, i + 1, 0:BLOCK, :]
            return w[j, i, BLOCK:2 * BLOCK, :] + before + after

        dkg = [jnp.zeros((1, dh), F32) for _ in range(ATT_KV_HEADS)]
        for i in range(kb):
            rows = slice(i * BLOCK, (i + 1) * BLOCK)
            dks, dvs = [], []
            for j in range(ATT_KV_HEADS):
                dkn = overlap_add(a, a_lo, a_hi, j, i)
                dvs.append(overlap_add(b, b_lo, b_hi, j, i))
                khat, rstd = _rms_rows(kv_ref[rows, j * dh:(j + 1) * dh])
                dkg[j] = dkg[j] + jnp.sum(dkn * khat, axis=0, keepdims=True)
                dkh = dkn * kg_ref[...]
                dks.append(rstd * (dkh - khat * jnp.mean(dkh * khat, axis=-1, keepdims=True)))
            dkv_ref[rows, :] = jnp.concatenate(dks + dvs, axis=1).astype(dkv_ref.dtype)
        for j in range(ATT_KV_HEADS):
            dkg_ref[j] += dkg[j]

    main = pl.BlockSpec((ATT_KV_HEADS, kb, 3 * BLOCK, dh), lambda n: (0, n, 0, 0))
    halo_lo = pl.BlockSpec((ATT_KV_HEADS, None, BLOCK, dh), lambda n: (0, jnp.maximum(n * kb - 1, 0), 2, 0))
    halo_hi = pl.BlockSpec((ATT_KV_HEADS, None, BLOCK, dh), lambda n: (0, jnp.minimum(n * kb + kb, nb - 1), 0, 0))
    return pl.pallas_call(
        body, name=name, grid=(steps,),
        in_specs=[halo_lo, main, halo_hi, halo_lo, main, halo_hi,
                  pl.BlockSpec((kb * BLOCK, 2 * KV_WIDTH), lambda n: (n, KV_COL // (2 * KV_WIDTH))),
                  pl.BlockSpec((1, dh), lambda n: (0, 0))],
        out_specs=[pl.BlockSpec((kb * BLOCK, 2 * KV_WIDTH), lambda n: (n, 0)),
                   pl.BlockSpec((ATT_KV_HEADS, 1, dh), lambda n: (0, 0, 0))],
        out_shape=[jax.ShapeDtypeStruct((S, 2 * KV_WIDTH), BF16), jax.ShapeDtypeStruct((ATT_KV_HEADS, 1, dh), F32)],
        compiler_params=_params("arbitrary"),
    )(dkw, dkw, dkw, dvw, dvw, dvw, z, k_g)


def _ada_wgrad(c_act_t, dm, name):
    D, nbatch = c_act_t.shape
    n = dm.shape[1]
    tr = 256

    def body(c_ref, dm_ref, o_ref):
        cv, dv = c_ref[...], dm_ref[...]
        acc = cv[:, 0:1] * dv[0:1, :]
        for b in range(1, nbatch):
            acc = acc + cv[:, b:b + 1] * dv[b:b + 1, :]
        o_ref[...] = acc

    return pl.pallas_call(
        body, name=name, grid=(D // tr,),
        in_specs=[pl.BlockSpec((tr, nbatch), lambda i: (i, 0)), pl.BlockSpec((nbatch, n), lambda i: (0, 0))],
        out_specs=pl.BlockSpec((tr, n), lambda i: (i, 0)), out_shape=jax.ShapeDtypeStruct((D, n), F32),
        compiler_params=_params("parallel"),
    )(c_act_t, dm)


def _to_bf16(w, name):
    R, Cn = w.shape
    tr = _row_tile(R)

    def body(w_ref, o_ref):
        o_ref[...] = w_ref[...].astype(BF16)

    blk = pl.BlockSpec((tr, Cn), lambda i: (i, 0))
    return pl.pallas_call(
        body, name=name, grid=(R // tr,), in_specs=[blk], out_specs=blk, out_shape=jax.ShapeDtypeStruct((R, Cn), BF16),
        compiler_params=_params("parallel"),
    )(w)


def _adamw(w, g, m, v, name):
    R, Cn = w.shape
    tr = R
    for cand in (256, 128, 64, 32, 16, 8):
        if R % cand == 0:
            tr = cand
            break

    def body(w_ref, g_ref, m_ref, v_ref, d_ref, nm_ref, nv_ref):
        gv = g_ref[...]
        m_new = ADAM_B1 * m_ref[...] + (1.0 - ADAM_B1) * gv
        v_new = ADAM_B2 * v_ref[...] + (1.0 - ADAM_B2) * (gv * gv)
        m_hat = m_new / (1.0 - ADAM_B1 ** ADAM_STEP)
        v_hat = v_new / (1.0 - ADAM_B2 ** ADAM_STEP)
        d_ref[...] = -ADAM_LR * (m_hat / (jnp.sqrt(v_hat) + ADAM_EPS) + ADAM_WD * w_ref[...])
        nm_ref[...] = m_new
        nv_ref[...] = v_new

    blk = pl.BlockSpec((tr, Cn), lambda i: (i, 0))
    shp = jax.ShapeDtypeStruct((R, Cn), F32)
    return pl.pallas_call(
        body, name=name, grid=(R // tr,), in_specs=[blk] * 4, out_specs=[blk] * 3, out_shape=[shp] * 3,
        compiler_params=_params("parallel"),
    )(w, g, m, v)


def _place():
    return lax.axis_index("x"), lax.axis_index("y"), lax.axis_index("c")


def _flip(place, k):
    x, y, c = place
    return (1 - x if k & 4 else x, 1 - y if k & 2 else y, 1 - c if k & 1 else c)


def _dev_index(place):
    x, y, c = place
    return 4 * x + 2 * y + c


def _chip_index(place):
    return 2 * place[0] + place[1]


def _gather8(x_ref, out_ref, send_sems, recv_sems, local_sem):
    me = _place()
    mine = pltpu.make_async_copy(x_ref, out_ref.at[_dev_index(me)], local_sem)
    mine.start()

    def copy(k, origin, to):
        return pltpu.make_async_remote_copy(
            src_ref=x_ref, dst_ref=out_ref.at[_dev_index(origin)], send_sem=send_sems.at[k - 1],
            recv_sem=recv_sems.at[k - 1], device_id=to, device_id_type=MESH)

    sends = [copy(k, me, _flip(me, k)) for k in range(1, 8)]
    for cp in sends:
        cp.start()
    for k in range(1, 8):
        copy(k, _flip(me, k), me).wait_recv()
    for cp in sends:
        cp.wait_send()
    mine.wait()


def _allgather8(x, name, reduce=False):
    R, Cn = x.shape

    def body(x_ref, *rest):
        if reduce:
            out_ref, sum_ref, send_sems, recv_sems, local_sem = rest
        else:
            out_ref, send_sems, recv_sems, local_sem = rest
        _gather8(x_ref, out_ref, send_sems, recv_sems, local_sem)
        if reduce:
            acc = out_ref[0]
            for i in range(1, 8):
                acc = acc + out_ref[i]
            sum_ref[...] = acc

    vm = pl.BlockSpec(memory_space=pltpu.VMEM)
    outs = [jax.ShapeDtypeStruct((8, R, Cn), F32)] + ([jax.ShapeDtypeStruct((R, Cn), F32)] if reduce else [])
    res = pl.pallas_call(
        body, name=name, in_specs=[vm], out_specs=[vm] * len(outs), out_shape=outs,
        scratch_shapes=[pltpu.SemaphoreType.DMA((7,)), pltpu.SemaphoreType.DMA((7,)), pltpu.SemaphoreType.DMA],
    )(x)
    return res if reduce else res[0]


def _prologue(small, w_ada, b_ada, w_shard, name):
    R, Cn = small.shape
    n_mod = w_ada.shape[1]
    big = _gather_in_one([w_shard])

    def body(small_ref, wada_ref, b_ref, shard_ref, small_all_ref, mods_all_ref, gathered_ref, mods_ref,
             send1, recv1, send2, recv2, local_sems, big_send, big_recv):
        big.start([shard_ref], [gathered_ref], big_send, big_recv)
        _gather8(small_ref, small_all_ref, send1, recv1, local_sems.at[0])
        c_all = jnp.concatenate([small_all_ref[d, 0:1, :] for d in range(8)], axis=0)
        c_act = c_all * _sigmoid(c_all)
        mods_ref[...] = _nn(c_act, wada_ref[...], precision=lax.Precision.HIGHEST) + b_ref[...]
        _gather8(mods_ref, mods_all_ref, send2, recv2, local_sems.at[1])
        big.switch([shard_ref], [gathered_ref], big_send, big_recv)
        big.finish([shard_ref], [gathered_ref], big_send, big_recv)

    vm = pl.BlockSpec(memory_space=pltpu.VMEM)
    seven = pltpu.SemaphoreType.DMA((7,))
    return pl.pallas_call(
        body, name=name, in_specs=[vm, vm, vm, ANY], out_specs=[vm, vm, ANY],
        out_shape=[jax.ShapeDtypeStruct((8, R, Cn), F32), jax.ShapeDtypeStruct((8, 8, n_mod), F32)] + big.out_shapes,
        scratch_shapes=[pltpu.VMEM((8, n_mod), F32), seven, seven, seven, seven, pltpu.SemaphoreType.DMA((2,))]
        + big.sem_shapes(),
        compiler_params=pltpu.CompilerParams(vmem_limit_bytes=VMEM_LIMIT_BYTES),
    )(small, w_ada, b_ada, w_shard)


def _symmetric_plan(copies):
    def plan(in_refs, out_refs):
        sends = [(src, dst, i, to) for i, (src, dst, to) in enumerate(copies(in_refs, out_refs))]
        return sends, [(dst, i) for _, dst, i, _ in sends]
    return plan


def _halves_exchange(grads):
    def copies(in_refs, out_refs):
        me = _place()
        return [(g.at[kk, 1 - me[2]], got.at[kk], _flip(me, 1)) for g, got in zip(in_refs, out_refs) for kk in range(N_CHIPS)]

    return _Exchange(grads, [jax.ShapeDtypeStruct((N_CHIPS,) + g.shape[2:], g.dtype) for g in grads],
                     N_CHIPS * len(grads), _symmetric_plan(copies))


def _chips_exchange(parts):
    def copies(in_refs, out_refs):
        me = _place()
        return [(p.at[_chip_index(_flip(me, 2 * j))], got.at[j - 1], _flip(me, 2 * j))
                for p, got in zip(in_refs, out_refs) for j in (1, 2, 3)]

    return _Exchange(parts, [jax.ShapeDtypeStruct((3,) + p.shape[1:], p.dtype) for p in parts], 3 * len(parts),
                     _symmetric_plan(copies))


def _siblings_exchange(halves):
    def copies(in_refs, out_refs):
        sibling = _flip(_place(), 1)
        return [(h, got, sibling) for h, got in zip(in_refs, out_refs)]

    return _Exchange(halves, [jax.ShapeDtypeStruct(h.shape, h.dtype) for h in halves], len(halves), _symmetric_plan(copies))


def _ici_gather_plan(n, base=0):
    def plan(in_refs, out_refs):
        me = _place()
        c = me[2]
        sends, lands = [], []
        for a, (w, out) in enumerate(zip(in_refs[:n], out_refs)):
            for j in (1, 2, 3):
                i = base + 3 * a + j - 1
                sends.append((w.at[c], out.at[_chip_index(me), c], i, _flip(me, 2 * j)))
                lands.append((out.at[_chip_index(_flip(me, 2 * j)), c], i))
        return sends, lands
    return plan


def _d2d_gather_plan(n, base=0):
    def plan(in_refs, out_refs):
        me = _place()
        c = me[2]
        sibling = _flip(me, 1)
        mine = _chip_index(me)
        sends, lands = [], []
        for a, (w, out) in enumerate(zip(in_refs[:n], out_refs)):
            moves = [(w.at[c], (mine, c)), (w.at[1 - c], (mine, 1 - c))]
            moves += [(out.at[_chip_index(_flip(me, 2 * j)), c], (_chip_index(_flip(me, 2 * j)), c)) for j in (1, 2, 3)]
            for k, (src, (chip, half)) in enumerate(moves):
                sends.append((src, out.at[chip, half], base + 5 * a + k, sibling))
            blocks = [(mine, 1 - c), (mine, c)] + [(_chip_index(_flip(me, 2 * j)), 1 - c) for j in (1, 2, 3)]
            lands += [(out.at[chip, half], base + 5 * a + k) for k, (chip, half) in enumerate(blocks)]
        return sends, lands
    return plan


def _gathered_shapes(shards):
    return [jax.ShapeDtypeStruct((N_CHIPS,) + s.shape, s.dtype) for s in shards]


def _gather_over_ici(shards):
    return _Exchange(shards, _gathered_shapes(shards), 3 * len(shards), _ici_gather_plan(len(shards)))


def _gather_over_d2d(shards, gathered):
    n = len(shards)
    return _Exchange(list(shards) + list(gathered), [jax.ShapeDtypeStruct(g.shape, g.dtype) for g in gathered], 5 * n,
                     _d2d_gather_plan(n), aliases={n + a: a for a in range(n)})


def _gather_in_one(shards):
    n = len(shards)
    return _Exchange(shards, _gathered_shapes(shards), 8 * n, _ici_gather_plan(n), then=_d2d_gather_plan(n, base=3 * n))


def _row_tile(rows):
    for cand in (256, 176, 128, 64, 32, 16, 8):
        if rows % cand == 0:
            return cand
    return rows


def _pair_sum(core, grad, theirs, name):
    N, _, R, Cn = grad.shape
    tr = R

    def body(core_ref, g_ref, t_ref, o_ref, ob_ref):
        s = g_ref[...] + t_ref[...]
        o_ref[...] = s
        ob_ref[...] = s.astype(BF16)

    out = pl.BlockSpec((None, tr, Cn), lambda k, i, core_ref: (k, i, 0))
    return pl.pallas_call(
        body, name=name,
        grid_spec=pltpu.PrefetchScalarGridSpec(
            num_scalar_prefetch=1, grid=(N, R // tr),
            in_specs=[pl.BlockSpec((None, None, tr, Cn), lambda k, i, core_ref: (k, core_ref[0], i, 0)),
                      pl.BlockSpec((None, tr, Cn), lambda k, i, core_ref: (k, i, 0))],
            out_specs=[out, out]),
        out_shape=[jax.ShapeDtypeStruct((N, R, Cn), F32), jax.ShapeDtypeStruct((N, R, Cn), BF16)],
        compiler_params=_params("parallel", "parallel"),
    )(core, grad, theirs)


def _chip_sum(chip, parts, landed, name):
    _, R, Cn = parts.shape
    tr = R

    def body(chip_ref, p_ref, l_ref, o_ref):
        o_ref[...] = ((p_ref[...] + l_ref[0].astype(F32)) + l_ref[1].astype(F32)) + l_ref[2].astype(F32)

    return pl.pallas_call(
        body, name=name,
        grid_spec=pltpu.PrefetchScalarGridSpec(
            num_scalar_prefetch=1, grid=(R // tr,),
            in_specs=[pl.BlockSpec((None, tr, Cn), lambda i, chip_ref: (chip_ref[0], i, 0)),
                      pl.BlockSpec((3, tr, Cn), lambda i, chip_ref: (0, i, 0))],
            out_specs=pl.BlockSpec((tr, Cn), lambda i, chip_ref: (i, 0))),
        out_shape=jax.ShapeDtypeStruct((R, Cn), F32), compiler_params=_params("parallel"),
    )(chip, parts, landed)


def _pair_sums(core, grads, theirs, tag):
    return [_pair_sum(core, g, t, f"{tag}_pair_sum_{i}") for i, (g, t) in enumerate(zip(grads, theirs))]


def _chip_sums(chip, parts, landed, tag):
    return [_chip_sum(chip, p[0], l, f"{tag}_chip_sum_{i}") for i, (p, l) in enumerate(zip(parts, landed))]


def _by_chip_rows(g):
    return g.reshape(N_CHIPS, 2, g.shape[0] // (2 * N_CHIPS), g.shape[1])


def _by_chip_cols(g):
    return g.reshape(N_CHIPS, 2, g.shape[1] // 2, g.shape[2])


def _adamw_halves(core, w, g_mine, g_theirs, m, v, name):
    R2, Cn = w.shape
    r = R2 // 2
    tr = _row_tile(r)
    nt = r // tr

    def body(core_ref, w_ref, gm_ref, gt_ref, m_ref, v_ref, g_ref, d_ref, nm_ref, nv_ref):
        gv = jnp.where(pl.program_id(0) == core_ref[0], gm_ref[...], gt_ref[...])
        g_ref[...] = gv
        m_new = ADAM_B1 * m_ref[...] + (1.0 - ADAM_B1) * gv
        v_new = ADAM_B2 * v_ref[...] + (1.0 - ADAM_B2) * (gv * gv)
        m_hat = m_new / (1.0 - ADAM_B1 ** ADAM_STEP)
        v_hat = v_new / (1.0 - ADAM_B2 ** ADAM_STEP)
        d_ref[...] = -ADAM_LR * (m_hat / (jnp.sqrt(v_hat) + ADAM_EPS) + ADAM_WD * w_ref[...])
        nm_ref[...] = m_new
        nv_ref[...] = v_new

    full = pl.BlockSpec((tr, Cn), lambda hf, i, core_ref: (hf * nt + i, 0))
    half = pl.BlockSpec((tr, Cn), lambda hf, i, core_ref: (i, 0))
    shp = jax.ShapeDtypeStruct((R2, Cn), F32)
    return pl.pallas_call(
        body, name=name,
        grid_spec=pltpu.PrefetchScalarGridSpec(
            num_scalar_prefetch=1, grid=(2, nt), in_specs=[full, half, half, full, full], out_specs=[full] * 4),
        out_shape=[shp] * 4, compiler_params=_params("parallel", "parallel"),
    )(core, w, g_mine, g_theirs, m, v)


def _pad_row(v, width):
    v = v.reshape(1, -1)
    return jnp.pad(v, ((0, 0), (0, width - v.shape[1])))


def _ffn1_forward(x, ng, shift, scale, gate, w_in4, w_out_shard, gather, next_norm):
    h = _rmsmod_fwd(x, ng, shift, scale, "ffn1_norm")
    (zg, zu, a), (partly, (w_out4,)) = _ffn_in_fwd(
        h, w_in4, "ffn1_in", exchange=[_gather_over_ici(gather), _gather_in_one([w_out_shard])])
    w_out = w_out4.reshape(D_FF, D_MODEL)
    (x_new, f, h_next), gathered = _proj_out_fwd([a], w_out, x, gate, 0.5, "ffn1_out", next_norm=next_norm,
                                                 exchange=_gather_over_d2d(gather, partly))
    return x_new, (h, zg, zu, a, f), w_out, gathered, h_next


def _ffn_backward(df, saved, w_in4, w_out, core, chip, tag, riding=None, norm=None):
    h, zg, zu, a = saved[:4]
    rode = None
    if riding:
        (dzg, dzu), rode = _dact_bwd(df, w_out, zg, zu, f"{tag}_dact", exchange=riding)
    else:
        dzg, dzu = _dact_bwd(df, w_out, zg, zu, f"{tag}_dact")
    g_out = [_by_chip_rows(_wgrad(a, [df], df.shape[1], f"{tag}_dw_out")[0].reshape(a.shape[1], df.shape[1]))]
    (dw_in,), theirs_out = _wgrad(h, [dzg, dzu], FF_SHARD, f"{tag}_dw_in", exchange=_halves_exchange(g_out))
    g_in = [_by_chip_cols(dw_in.reshape(N_CHIPS, h.shape[1], FF_SHARD))]
    parts_out = _pair_sums(core, g_out, theirs_out, f"{tag}_out")
    dh_outs, (theirs_in, landed_out) = _ffn_in_dgrad(
        dzg, dzu, w_in4, f"{tag}_dh", norm=norm, exchange=[_halves_exchange(g_in), _chips_exchange([parts_out[0][1]])])
    parts_in = _pair_sums(core, g_in, theirs_in, f"{tag}_in")
    return dh_outs, parts_in, _chip_sums(chip, parts_out, landed_out, f"{tag}_out"), rode


def kernel(x, c, w_ada, b_ada, norm_g, w_ffn1_in, w_ffn1_out, w_ffn2_in, w_ffn2_out, w_mix_in, w_mix_out, hgrn_lb, hgrn_norm_g, qk_norm_g, attn_sink, rel_bias, loss_target, m_w_ada, m_b_ada, m_norm_g, m_w_ffn1_in, m_w_ffn1_out, m_w_ffn2_in, m_w_ffn2_out, m_w_mix_in, m_w_mix_out, m_hgrn_lb, m_hgrn_norm_g, m_qk_norm_g, m_attn_sink, m_rel_bias, v_w_ada, v_b_ada, v_norm_g, v_w_ffn1_in, v_w_ffn1_out, v_w_ffn2_in, v_w_ffn2_out, v_w_mix_in, v_w_mix_out, v_hgrn_lb, v_hgrn_norm_g, v_qk_norm_g, v_attn_sink, v_rel_bias):
    D = D_MODEL
    S = x.shape[1]
    place = (lax.axis_index("x"), lax.axis_index("y"), lax.axis_index("c"))
    me, my_chip = _dev_index(place), _chip_index(place)
    x0 = x[0]
    target = loss_target[0]

    def halves(w, tag):
        return _to_bf16(w[0], f"{tag}_to_bf16").reshape(2, w.shape[1] // 2, w.shape[2])

    w1_out_shard = halves(w_ffn1_out, "w_ffn1_out")
    mix_shards = [halves(w_mix_in, "w_mix_in"), halves(w_mix_out, "w_mix_out")]
    ffn2_shards = [halves(w_ffn2_in, "w_ffn2_in"), halves(w_ffn2_out, "w_ffn2_out")]
    core_arr = jnp.reshape(place[2], (1,)).astype(jnp.int32)
    chip_arr = jnp.reshape(my_chip, (1,)).astype(jnp.int32)

    small = jnp.concatenate([_pad_row(c, D), _pad_row(norm_g, D), _pad_row(hgrn_lb, D), jnp.zeros((5, D), F32)], axis=0)
    n_ada = w_ada.shape[2]
    b_mine = lax.dynamic_slice_in_dim(b_ada, my_chip * n_ada, n_ada, axis=1)
    small_all, mods_parts, w1_in4 = _prologue(small, w_ada[0], b_mine, halves(w_ffn1_in, "w_ffn1_in"), "prologue")
    w1_in = w1_in4.reshape(N_CHIPS, D, FF_SHARD)
    c_all = small_all[:, 0, :]
    by_chip = small_all[0::2]
    norm_g_full = by_chip[:, 1, :3 * 256].reshape(N_CHIPS, 3, 256).transpose(1, 0, 2).reshape(3, D)
    lb_raw = by_chip[:, 2, :2 * 2 * 128].reshape(N_CHIPS, 2, 2, 128).transpose(1, 2, 0, 3).reshape(2, 2, HG_WIDTH)
    lb_logit = lb_raw[:, 0, :] - lb_raw[:, 1, :]
    lb = jax.nn.sigmoid(lb_logit)
    one_minus_lb = jax.nn.sigmoid(-lb_logit)
    lb_f = jnp.stack([lb[0], one_minus_lb[0]])
    lb_b = jnp.stack([lb[1], one_minus_lb[1]])

    c_act_all = c_all * jax.nn.sigmoid(c_all)
    mods_all = mods_parts[0::2].transpose(1, 0, 2).reshape(8, N_MOD * D)
    mods = lax.dynamic_slice_in_dim(mods_all, me, 1, axis=0)
    sh1, sc1, g1, sh2, sc2, g2, sh3, sc3, g3 = [mods[:, i * D:(i + 1) * D] for i in range(N_MOD)]

    x1, saved1, w1_out, gathered, h2 = _ffn1_forward(x0, norm_g_full[0:1], sh1, sc1, g1, w1_in, w1_out_shard, mix_shards,
                                                     (norm_g_full[1:2], sh2, sc2))
    wm_in = gathered[0].reshape(N_CHIPS, D, D_IN // N_CHIPS).transpose(1, 0, 2).reshape(D, D_IN)
    wm_out = gathered[1].reshape(D, D)

    z = _matmul_nn(h2, wm_in, F32, 256, "mix_in")
    (of, st_f, ob, st_b), gathered = _hgrn_fwd_both(z, (lb_f, lb_b), "hgrn_fwd", exchange=_gather_in_one(ffn2_shards))
    w2_in = gathered[0].reshape(N_CHIPS, D, FF_SHARD)
    w2_out = gathered[1].reshape(D_FF, D)
    o_h = _hgrn_post_fwd(of, ob, z, hgrn_norm_g, "hgrn_post")

    q_g, k_g = qk_norm_g[0, 0:1], qk_norm_g[0, 1:2]
    sink_b = jnp.broadcast_to(attn_sink.reshape(ATT_Q_HEADS, 1, 1), (ATT_Q_HEADS, 1, BLOCK))
    bias = _bias_table(rel_bias, "bias_table")
    o_a = _attn_fwd(z, q_g, k_g, sink_b, bias, "attn_fwd")
    x2, mixed, h3 = _proj_out_fwd([o_h, o_a], wm_out, x1, g2, 1.0, "mix_out", next_norm=(norm_g_full[2:3], sh3, sc3))

    zg3, zu3, a3 = _ffn_in_fwd(h3, w2_in, "ffn2_in")
    dx3, df3, dg3, sq_cols = _proj_out_loss(a3, w2_out, x2, g3, 0.5, target, "ffn2_out_loss")
    loss_mine = 0.5 * jnp.sum(sq_cols) / D

    (dx2, dsh3, dsc3, dng3, dmixed, dg2), parts2, mine2_out, _ = _ffn_backward(
        df3, (h3, zg3, zu3, a3), w2_in, w2_out, core_arr, chip_arr, "ffn2",
        norm=_NormBwd(x2, norm_g_full[2:3], sc3, dx3, below=(mixed, g2, 1.0)))

    (do_cat,) = _matmul_nt([dmixed], wm_out, ROW_TILE, "mix_out_dgrad")
    dwm_out = _wgrad_rows([o_h, o_a], dmixed, "mix_out_dw").reshape(D, D)

    do_sum, dgr, d_hnorm = _hgrn_post_bwd(do_cat, of, ob, z, hgrn_norm_g, "hgrn_post_bwd")
    (dq_f, dff, dv_f, doml_f), landed2 = _hgrn_bwd(z, lb_f, do_sum, st_f, 0, "hgrn_bwd_f",
                                                   exchange=_chips_exchange([p[1] for p in parts2]))
    mine2 = _chip_sums(chip_arr, parts2, landed2, "ffn2_in") + mine2_out
    (dhq, dfb, dhi, doml_b), theirs2 = _hgrn_bwd(z, lb_b, do_sum, st_b, 1, "hgrn_bwd_b", acc=(dq_f, dv_f),
                                                 exchange=_siblings_exchange(mine2))

    daq, dkw, dvw, ds_sum, dsink, dqg = _attn_bwd(z, q_g, k_g, sink_b, bias, do_cat, "attn_bwd")
    dkv, dkg = _attn_kv_reduce(dkw, dvw, z, k_g, "attn_kv_reduce")
    d_rel_bias = jnp.sum(_bias_grad(ds_sum, "bias_grad"), axis=-1).T
    dz = [dhq, dff, dfb, dhi, dgr, daq, dkv]
    dwm_in = _wgrad_pieces(h2, dz, 2 * KV_WIDTH, "mix_in_dw").transpose(1, 0, 2).reshape(D, D_IN)
    wide = D_IN // N_CHIPS
    grads_m = [_by_chip_cols(dwm_in.reshape(D, N_CHIPS, wide).transpose(1, 0, 2)), _by_chip_rows(dwm_out)]
    (dx1, dsh2, dsc2, dng2, df1, dg1), theirs_m = _matmul_nt(
        dz, wm_in, 256, "mix_in_dgrad", exchange=_halves_exchange(grads_m),
        norm=_NormBwd(x1, norm_g_full[1:2], sc2, dx2, below=(saved1[4], g1, 0.5)))
    parts_m = _pair_sums(core_arr, grads_m, theirs_m, "mix")

    (dh1,), parts1, mine1_out, landed_m = _ffn_backward(df1, saved1, w1_in, w1_out, core_arr, chip_arr, "ffn1",
                                                        riding=_chips_exchange([p[1] for p in parts_m]))
    mine_m = _chip_sums(chip_arr, parts_m, landed_m, "mix")
    (dx0, dsh1, dsc1, dng1), landed1 = _rmsmod_bwd(dh1, _NormBwd(x0, norm_g_full[0:1], sc1, dx1), "ffn1_norm_bwd",
                                                   exchange=_chips_exchange([p[1] for p in parts1]))
    mine1 = _chip_sums(chip_arr, parts1, landed1, "ffn1_in") + mine1_out
    theirs_1m = list(_run_exchange(_siblings_exchange(mine1 + mine_m), "siblings_exchange"))
    reduced = list(zip(mine1 + mine2 + mine_m, theirs_1m[:2] + list(theirs2) + theirs_1m[2:]))

    dlb = -jnp.concatenate([doml_f, doml_b], axis=0)
    dlb_raw = dlb * lb * one_minus_lb
    d_hgrn_lb = jnp.stack([dlb_raw, -dlb_raw], axis=1)
    d_qk = jnp.concatenate([jnp.sum(dqg, axis=0), jnp.sum(dkg, axis=0)], axis=0)
    dmods = jnp.concatenate([dsh1, dsc1, dg1, dsh2, dsc2, dg2, dsh3, dsc3, dg3], axis=0)
    packed = jnp.concatenate(
        [dmods, dng1, dng2, dng3, d_hgrn_lb.reshape(2, D), _pad_row(d_hnorm, D), _pad_row(d_qk, D),
         _pad_row(dsink[:, 0, 0], D), _pad_row(d_rel_bias, D), _pad_row(loss_mine, D)], axis=0)
    packed = jnp.pad(packed, ((0, 24 - packed.shape[0]), (0, 0)))
    packed_all, packed_sum = _allgather8(packed, "small_grads_allgather", reduce=True)
    dmods_all = packed_all[:, 0:N_MOD, :].reshape(8, N_MOD * D)
    g_b_ada = packed_sum[0:N_MOD].reshape(1, N_MOD * D)
    g_norm_full = packed_sum[9:12]
    g_norm_g = lax.dynamic_slice_in_dim(g_norm_full, my_chip * 256, 256, axis=1).reshape(1, 3, 256)
    g_hgrn_lb = lax.dynamic_slice_in_dim(packed_sum[12:14].reshape(2, 2, HG_WIDTH), my_chip * 128, 128, axis=2)
    g_hgrn_norm_g = packed_sum[14:15, :HG_WIDTH]
    g_qk_norm_g = packed_sum[15, :2 * ATT_HEAD_DIM].reshape(1, 2, ATT_HEAD_DIM)
    g_attn_sink = packed_sum[16:17, :ATT_Q_HEADS]
    g_rel_bias = packed_sum[17, :NUM_BUCKETS * ATT_Q_HEADS].reshape(NUM_BUCKETS, ATT_Q_HEADS)
    loss = packed_sum[18, 0]

    dm_mine = lax.dynamic_slice_in_dim(dmods_all, my_chip * n_ada, n_ada, axis=1)
    g_w_ada = _ada_wgrad(c_act_all.T, dm_mine, "ada_wgrad")[None]

    def big(w, g, m, v, name):
        d, nm, nv = _adamw(w[0], g[0], m[0], v[0], name)
        return d[None], nm[None], nv[None]

    def big_halves(w, g_pair, m, v, name):
        g, d, nm, nv = _adamw_halves(core_arr, w[0], g_pair[0], g_pair[1], m[0], v[0], name)
        return g[None], (d[None], nm[None], nv[None])

    g_w1_in, u_w1_in = big_halves(w_ffn1_in, reduced[0], m_w_ffn1_in, v_w_ffn1_in, "adamw_w_ffn1_in")
    g_w1_out, u_w1_out = big_halves(w_ffn1_out, reduced[1], m_w_ffn1_out, v_w_ffn1_out, "adamw_w_ffn1_out")
    g_w2_in, u_w2_in = big_halves(w_ffn2_in, reduced[2], m_w_ffn2_in, v_w_ffn2_in, "adamw_w_ffn2_in")
    g_w2_out, u_w2_out = big_halves(w_ffn2_out, reduced[3], m_w_ffn2_out, v_w_ffn2_out, "adamw_w_ffn2_out")
    g_wm_in, u_wm_in = big_halves(w_mix_in, reduced[4], m_w_mix_in, v_w_mix_in, "adamw_w_mix_in")
    g_wm_out, u_wm_out = big_halves(w_mix_out, reduced[5], m_w_mix_out, v_w_mix_out, "adamw_w_mix_out")

    smalls = [(b_ada, g_b_ada, m_b_ada, v_b_ada), (norm_g, g_norm_g, m_norm_g, v_norm_g), (hgrn_lb, g_hgrn_lb, m_hgrn_lb, v_hgrn_lb),
              (hgrn_norm_g, g_hgrn_norm_g, m_hgrn_norm_g, v_hgrn_norm_g), (qk_norm_g, g_qk_norm_g, m_qk_norm_g, v_qk_norm_g),
              (attn_sink, g_attn_sink, m_attn_sink, v_attn_sink), (rel_bias, g_rel_bias, m_rel_bias, v_rel_bias)]
    sizes = [t[0].size for t in smalls]
    total = sum(sizes)
    rows = -(-total // 128)
    rows = -(-rows // 8) * 8

    def pack(i):
        flat = jnp.concatenate([t[i].reshape(-1) for t in smalls])
        fill = 1.0 if i == 3 else 0.0
        return jnp.pad(flat, (0, rows * 128 - total), constant_values=fill).reshape(rows, 128)

    packed_out = _adamw(pack(0), pack(1), pack(2), pack(3), "adamw_small")

    def unpack(flat2d):
        flat = flat2d.reshape(-1)
        outs, off = [], 0
        for t, n in zip(smalls, sizes):
            outs.append(flat[off:off + n].reshape(t[0].shape))
            off += n
        return outs

    d_small, m_small, v_small = [unpack(t) for t in packed_out]

    upd = {
        "w_ada": big(w_ada, g_w_ada, m_w_ada, v_w_ada, "adamw_w_ada"),
        "w_ffn1_in": u_w1_in, "w_ffn1_out": u_w1_out, "w_ffn2_in": u_w2_in, "w_ffn2_out": u_w2_out,
        "w_mix_in": u_wm_in, "w_mix_out": u_wm_out,
    }
    small_names = ["b_ada", "norm_g", "hgrn_lb", "hgrn_norm_g", "qk_norm_g", "attn_sink", "rel_bias"]
    for i, nme in enumerate(small_names):
        upd[nme] = (d_small[i], m_small[i], v_small[i])
    grads = {
        "w_ada": g_w_ada, "b_ada": g_b_ada, "norm_g": g_norm_g, "w_ffn1_in": g_w1_in, "w_ffn1_out": g_w1_out,
        "w_ffn2_in": g_w2_in, "w_ffn2_out": g_w2_out, "w_mix_in": g_wm_in, "w_mix_out": g_wm_out, "hgrn_lb": g_hgrn_lb,
        "hgrn_norm_g": g_hgrn_norm_g, "qk_norm_g": g_qk_norm_g, "attn_sink": g_attn_sink, "rel_bias": g_rel_bias,
    }
    order = ["w_ada", "b_ada", "norm_g", "w_ffn1_in", "w_ffn1_out", "w_ffn2_in", "w_ffn2_out", "w_mix_in", "w_mix_out",
             "hgrn_lb", "hgrn_norm_g", "qk_norm_g", "attn_sink", "rel_bias"]
    return (loss, dx0[None], *[grads[k] for k in order], *[upd[k][0] for k in order], *[upd[k][1] for k in order],
            *[upd[k][2] for k in order])
```

```python
import functools
import math

import numpy as np
import jax
import jax.numpy as jnp
from jax import lax
from jax.experimental import pallas as pl
from jax.experimental.pallas import tpu as pltpu

F32, BF16 = jnp.float32, jnp.bfloat16

D_MODEL = 1024
D_FF = 2816
HG_HEADS, HG_DIM = 4, 128
HG_WIDTH = HG_HEADS * HG_DIM
ATT_Q_HEADS, ATT_KV_HEADS, ATT_HEAD_DIM = 8, 2, 64
ATT_GROUP = ATT_Q_HEADS // ATT_KV_HEADS
ATT_WIDTH = ATT_Q_HEADS * ATT_HEAD_DIM
KV_WIDTH = ATT_KV_HEADS * ATT_HEAD_DIM
WINDOW, BLOCK = 128, 128
NUM_BUCKETS, MAX_DISTANCE = 32, 128
N_MOD = 9
EPS = 1e-6
D_IN = 5 * HG_WIDTH + ATT_WIDTH + 2 * KV_WIDTH
ADAM_LR, ADAM_B1, ADAM_B2, ADAM_EPS, ADAM_WD, ADAM_STEP = 0.001, 0.9, 0.999, 1e-08, 0.01, 10

N_CHIPS = 4
FF_SHARD = 2 * D_FF // N_CHIPS
NEG = -1e30

VMEM_LIMIT_BYTES = 56 << 20
ROW_TILE = 512
HG_CHUNK = 16
HG_ROWS = 512

MESH = pl.DeviceIdType.MESH
ANY = pl.BlockSpec(memory_space=pl.ANY)


def _params(*sem):
    return pltpu.CompilerParams(dimension_semantics=sem, vmem_limit_bytes=VMEM_LIMIT_BYTES)


def _resident(shape, index_map):
    return pl.BlockSpec(shape, index_map, pipeline_mode=pl.Buffered(1))


def _dot(a, b, dims, precision=None):
    return lax.dot_general(a, b, (dims, ((), ())), precision=precision, preferred_element_type=F32)


def _nn(a, b, precision=None):
    return _dot(a, b, ((1,), (0,)), precision)


def _nt(a, b):
    return _dot(a, b, ((1,), (1,)))


def _tn(a, b):
    return _dot(a, b, ((0,), (0,)))


def _sigmoid(x):
    return jax.nn.sigmoid(x)


class _Exchange:
    def __init__(self, inputs, out_shapes, n_sems, plan, aliases=None, then=None):
        self.inputs, self.out_shapes, self.n_sems, self.plan, self.aliases = list(inputs), list(out_shapes), n_sems, plan, aliases or {}
        self.then = then

    def sem_shapes(self):
        return [pltpu.SemaphoreType.DMA((self.n_sems,)), pltpu.SemaphoreType.DMA((self.n_sems,))]

    @staticmethod
    def _copy(src, dst, i, to, send_sems, recv_sems):
        return pltpu.make_async_remote_copy(
            src_ref=src, dst_ref=dst, send_sem=send_sems.at[i], recv_sem=recv_sems.at[i], device_id=to, device_id_type=MESH)

    def _start(self, plan, in_refs, out_refs, send_sems, recv_sems):
        for src, dst, i, to in plan(in_refs, out_refs)[0]:
            self._copy(src, dst, i, to, send_sems, recv_sems).start()

    def _wait(self, plan, in_refs, out_refs, send_sems, recv_sems):
        sends, lands = plan(in_refs, out_refs)
        for zone, i in lands:
            self._copy(zone, zone, i, _place(), send_sems, recv_sems).wait_recv()
        for src, dst, i, to in sends:
            self._copy(src, dst, i, to, send_sems, recv_sems).wait_send()

    def start(self, *refs):
        self._start(self.plan, *refs)

    def switch(self, *refs):
        if self.then:
            self._wait(self.plan, *refs)
            self._start(self.then, *refs)

    def finish(self, *refs):
        self._wait(self.then or self.plan, *refs)


def _run_exchange(ex, name):
    n_in, n_out = len(ex.inputs), len(ex.out_shapes)

    def body(*refs):
        in_refs, out_refs, (send_sems, recv_sems) = refs[:n_in], refs[n_in:n_in + n_out], refs[n_in + n_out:]
        ex.start(in_refs, out_refs, send_sems, recv_sems)
        ex.switch(in_refs, out_refs, send_sems, recv_sems)
        ex.finish(in_refs, out_refs, send_sems, recv_sems)

    return pl.pallas_call(
        body, name=name, in_specs=[ANY] * n_in, out_specs=[ANY] * n_out, out_shape=ex.out_shapes,
        scratch_shapes=ex.sem_shapes(), input_output_aliases=dict(ex.aliases),
    )(*ex.inputs)


def _call(body, *, name, grid, in_specs, out_specs, out_shape, args, semantics, scratch_shapes=(), exchange=None):
    if exchange is None:
        return pl.pallas_call(
            body, name=name, grid=grid, in_specs=in_specs, out_specs=out_specs, out_shape=out_shape,
            scratch_shapes=list(scratch_shapes), compiler_params=_params(*semantics))(*args)
    exs = exchange if isinstance(exchange, (list, tuple)) else [exchange]
    n_in, n_out, n_scr = len(in_specs), len(out_specs), len(scratch_shapes)
    x_in, x_out = [len(ex.inputs) for ex in exs], [len(ex.out_shapes) for ex in exs]

    def take(refs, counts):
        groups = []
        for n in counts:
            groups.append(refs[:n])
            refs = refs[n:]
        return groups, refs

    def carrier(*refs):
        ins, refs = refs[:n_in], refs[n_in:]
        x_ins, refs = take(refs, x_in)
        outs, refs = refs[:n_out], refs[n_out:]
        x_outs, refs = take(refs, x_out)
        scr, refs = refs[:n_scr], refs[n_scr:]
        sems, _ = take(refs, [2] * len(exs))
        ids = [pl.program_id(a) for a in range(len(grid))]
        first = functools.reduce(jnp.logical_and, [i == 0 for i in ids])
        last = functools.reduce(jnp.logical_and, [i == g - 1 for i, g in zip(ids, grid)])
        step = functools.reduce(lambda acc, ig: acc * ig[1] + ig[0], zip(ids, grid), 0)

        @pl.when(first)
        def _():
            for ex, xi, xo, (send_sems, recv_sems) in zip(exs, x_ins, x_outs, sems):
                ex.start(xi, xo, send_sems, recv_sems)

        if any(ex.then for ex in exs):
            @pl.when(step == (3 * math.prod(grid)) // 4)
            def _():
                for ex, xi, xo, (send_sems, recv_sems) in zip(exs, x_ins, x_outs, sems):
                    ex.switch(xi, xo, send_sems, recv_sems)

        body(*ins, *outs, *scr)

        @pl.when(last)
        def _():
            for ex, xi, xo, (send_sems, recv_sems) in zip(exs, x_ins, x_outs, sems):
                ex.finish(xi, xo, send_sems, recv_sems)

    aliases, i0, o0 = {}, n_in, n_out
    for ex in exs:
        aliases.update({i0 + i: o0 + o for i, o in ex.aliases.items()})
        i0, o0 = i0 + len(ex.inputs), o0 + len(ex.out_shapes)
    res = pl.pallas_call(
        carrier, name=name, grid=grid, in_specs=list(in_specs) + [ANY] * sum(x_in),
        out_specs=list(out_specs) + [ANY] * sum(x_out),
        out_shape=list(out_shape) + [s for ex in exs for s in ex.out_shapes],
        scratch_shapes=list(scratch_shapes) + [s for ex in exs for s in ex.sem_shapes()],
        input_output_aliases=aliases, compiler_params=_params(*["arbitrary"] * len(grid)),
    )(*args, *[a for ex in exs for a in ex.inputs])
    x_res, _ = take(list(res[n_out:]), x_out)
    return list(res[:n_out]), (x_res if isinstance(exchange, (list, tuple)) else x_res[0])


def _rmsmod_fwd(x, g, shift, scale, name):
    S, D = x.shape
    tr = min(ROW_TILE, S)

    def body(x_ref, g_ref, sh_ref, sc_ref, h_ref):
        xv = x_ref[...]
        rstd = lax.rsqrt(jnp.mean(xv * xv, axis=-1, keepdims=True) + EPS)
        y = xv * rstd * g_ref[...]
        h_ref[...] = (y * (1.0 + sc_ref[...]) + sh_ref[...]).astype(h_ref.dtype)

    row = pl.BlockSpec((tr, D), lambda i: (i, 0))
    vec = pl.BlockSpec((1, D), lambda i: (0, 0))
    return pl.pallas_call(
        body, name=name, grid=(S // tr,), in_specs=[row, vec, vec, vec], out_specs=row,
        out_shape=jax.ShapeDtypeStruct((S, D), BF16), compiler_params=_params("parallel"),
    )(x, g, shift, scale)


class _NormBwd:
    def __init__(self, x, g, scale, dx_res, below=None):
        S, D = x.shape
        self.below, self.coef = below, (below[2] if below else None)
        self.inputs = [x, g, scale, dx_res] + ([below[0], below[1]] if below else [])
        vshape = jax.ShapeDtypeStruct((1, D), F32)
        self.out_shape = [jax.ShapeDtypeStruct((S, D), F32), vshape, vshape, vshape]
        if below:
            self.out_shape += [jax.ShapeDtypeStruct((S, D), BF16), vshape]

    def specs(self, tr, D):
        row = pl.BlockSpec((tr, D), lambda i: (i, 0))
        vec = pl.BlockSpec((1, D), lambda i: (0, 0))
        return ([row, vec, vec, row] + ([row, vec] if self.below else []),
                [row, vec, vec, vec] + ([row, vec] if self.below else []))

    def step(self, dhv, in_refs, out_refs):
        if self.below:
            x_ref, g_ref, sc_ref, dxr_ref, f_ref, gate_ref = in_refs
            dx_ref, dsh_ref, dsc_ref, dg_ref, df_ref, dgate_ref = out_refs
            sums = (dsh_ref, dsc_ref, dg_ref, dgate_ref)
        else:
            x_ref, g_ref, sc_ref, dxr_ref = in_refs
            dx_ref, dsh_ref, dsc_ref, dg_ref = out_refs
            sums = (dsh_ref, dsc_ref, dg_ref)

        @pl.when(pl.program_id(0) == 0)
        def _():
            for ref in sums:
                ref[...] = jnp.zeros_like(ref)

        xv, gv = x_ref[...], g_ref[...]
        one_sc = 1.0 + sc_ref[...]
        rstd = lax.rsqrt(jnp.mean(xv * xv, axis=-1, keepdims=True) + EPS)
        n = xv * rstd
        dsh_ref[...] += jnp.sum(dhv, axis=0, keepdims=True)
        dsc_ref[...] += jnp.sum(dhv * n, axis=0, keepdims=True) * gv
        dg_ref[...] += jnp.sum(dhv * n, axis=0, keepdims=True) * one_sc
        dn = dhv * (gv * one_sc)
        dx = dxr_ref[...] + rstd * (dn - n * jnp.mean(dn * n, axis=-1, keepdims=True))
        dx_ref[...] = dx
        if self.below:
            df_ref[...] = (self.coef * gate_ref[...] * dx).astype(df_ref.dtype)
            dgate_ref[...] += self.coef * jnp.sum(dx * f_ref[...].astype(F32), axis=0, keepdims=True)


def _rmsmod_bwd(dh, norm, name, exchange=None):
    S, D = dh.shape
    tr = min(ROW_TILE, S)
    n_in = len(norm.inputs)

    def body(dh_ref, *refs):
        norm.step(dh_ref[...], refs[:n_in], refs[n_in:])

    in_specs, out_specs = norm.specs(tr, D)
    return _call(body, name=name, grid=(S // tr,), in_specs=[pl.BlockSpec((tr, D), lambda i: (i, 0))] + in_specs,
                 out_specs=out_specs, out_shape=norm.out_shape, args=[dh] + norm.inputs, semantics=("arbitrary",),
                 exchange=exchange)


def _ffn_in_fwd(h, w4, name, exchange=None):
    S, D = h.shape
    tm = min(2 * ROW_TILE, S)
    n = w4.shape[2]

    def body(h_ref, wg_ref, wu_ref, zg_ref, zu_ref, a_ref):
        hv = h_ref[...]
        zg = _nn(hv, wg_ref[...])
        zu = _nn(hv, wu_ref[...])
        zg_ref[...] = zg.astype(zg_ref.dtype)
        zu_ref[...] = zu.astype(zu_ref.dtype)
        a_ref[...] = (zg * _sigmoid(zg) * zu).astype(a_ref.dtype)

    out = pl.BlockSpec((tm, n), lambda j, m: (m, j))
    oshape = jax.ShapeDtypeStruct((S, 2 * n), BF16)
    return _call(
        body, name=name, grid=(2, S // tm),
        in_specs=[pl.BlockSpec((tm, D), lambda j, m: (m, 0)),
                  pl.BlockSpec((None, D, n), lambda j, m: (j, 0, 0)),
                  pl.BlockSpec((None, D, n), lambda j, m: (j + 2, 0, 0))],
        out_specs=[out, out, out], out_shape=[oshape, oshape, oshape], args=(h, w4, w4),
        semantics=("parallel", "parallel"), exchange=exchange)


def _proj_out_fwd(lhs, w, x, gate, coef, name, exchange=None, next_norm=None):
    S, D = x.shape
    tm = min(ROW_TILE, S)
    ks = [a.shape[1] for a in lhs]

    def body(*refs):
        lhs_refs, refs = refs[:len(lhs)], refs[len(lhs):]
        if next_norm:
            w_ref, x_ref, gate_ref, g_ref, sh_ref, sc_ref, xn_ref, f_ref, h_ref = refs
        else:
            w_ref, x_ref, gate_ref, xn_ref, f_ref = refs
        acc, off = None, 0
        for a_ref, k in zip(lhs_refs, ks):
            part = _nn(a_ref[...], w_ref[off:off + k, :])
            acc = part if acc is None else acc + part
            off += k
        f_ref[...] = acc.astype(f_ref.dtype)
        xn = x_ref[...] + coef * gate_ref[...] * acc
        xn_ref[...] = xn
        if next_norm:
            rstd = lax.rsqrt(jnp.mean(xn * xn, axis=-1, keepdims=True) + EPS)
            h_ref[...] = (xn * rstd * g_ref[...] * (1.0 + sc_ref[...]) + sh_ref[...]).astype(h_ref.dtype)

    row = pl.BlockSpec((tm, D), lambda m: (m, 0))
    vec = pl.BlockSpec((1, D), lambda m: (0, 0))
    extra = list(next_norm) if next_norm else []
    return _call(
        body, name=name, grid=(S // tm,),
        in_specs=[pl.BlockSpec((tm, k), lambda m: (m, 0)) for k in ks]
        + [_resident(w.shape, lambda m: (0, 0)), row, vec] + [vec] * len(extra),
        out_specs=[row, row] + ([row] if next_norm else []),
        out_shape=[jax.ShapeDtypeStruct((S, D), F32), jax.ShapeDtypeStruct((S, D), BF16)]
        + ([jax.ShapeDtypeStruct((S, D), BF16)] if next_norm else []),
        args=(*lhs, w, x, gate, *extra), semantics=("parallel",), exchange=exchange)


def _proj_out_loss(lhs, w, x, gate, coef, target, name):
    S, D = x.shape
    tm = min(ROW_TILE, S)

    def body(a_ref, w_ref, x_ref, gate_ref, t_ref, dy_ref, df_ref, dgate_ref, sq_ref):
        @pl.when(pl.program_id(0) == 0)
        def _():
            dgate_ref[...] = jnp.zeros_like(dgate_ref)
            sq_ref[...] = jnp.zeros_like(sq_ref)

        f = _nn(a_ref[...], w_ref[...])
        gate = coef * gate_ref[...]
        err = x_ref[...] + gate * f - t_ref[...]
        sq_ref[...] += jnp.sum(err * err, axis=0, keepdims=True)
        dy = err * (1.0 / D)
        dy_ref[...] = dy
        df_ref[...] = (gate * dy).astype(df_ref.dtype)
        dgate_ref[...] += coef * jnp.sum(dy * f, axis=0, keepdims=True)

    row = pl.BlockSpec((tm, D), lambda m: (m, 0))
    vec = pl.BlockSpec((1, D), lambda m: (0, 0))
    vshape = jax.ShapeDtypeStruct((1, D), F32)
    return pl.pallas_call(
        body, name=name, grid=(S // tm,),
        in_specs=[pl.BlockSpec((tm, lhs.shape[1]), lambda m: (m, 0)), _resident(w.shape, lambda m: (0, 0)), row, vec, row],
        out_specs=[row, row, vec, vec],
        out_shape=[jax.ShapeDtypeStruct((S, D), F32), jax.ShapeDtypeStruct((S, D), BF16), vshape, vshape],
        compiler_params=_params("arbitrary"),
    )(lhs, w, x, gate, target)


def _matmul_nn(a, w, out_dtype, tm, name):
    S, K = a.shape
    N = w.shape[1]
    tm = min(tm, S)

    def body(a_ref, w_ref, o_ref):
        o_ref[...] = _nn(a_ref[...], w_ref[...]).astype(o_ref.dtype)

    return pl.pallas_call(
        body, name=name, grid=(S // tm,),
        in_specs=[pl.BlockSpec((tm, K), lambda m: (m, 0)), _resident((K, N), lambda m: (0, 0))],
        out_specs=pl.BlockSpec((tm, N), lambda m: (m, 0)), out_shape=jax.ShapeDtypeStruct((S, N), out_dtype),
        compiler_params=_params("parallel"),
    )(a, w)


def _dact_bwd(df, w_out, zg, zu, name, exchange=None):
    S, D = df.shape
    tm = min(ROW_TILE, S)
    n = w_out.shape[0] // 2

    def body(df_ref, w_ref, zg_ref, zu_ref, dzg_ref, dzu_ref):
        da = _nt(df_ref[...], w_ref[...]).astype(BF16)
        zg_v, zu_v = zg_ref[...], zu_ref[...]
        s = _sigmoid(zg_v)
        dzu_ref[...] = da * zg_v * s
        dzg_ref[...] = da * zu_v * (s * (1.0 + zg_v * (1.0 - s)))

    blk = pl.BlockSpec((tm, n), lambda j, m: (m, j))
    oshape = jax.ShapeDtypeStruct((S, 2 * n), BF16)
    return _call(
        body, name=name, grid=(2, S // tm),
        in_specs=[pl.BlockSpec((tm, D), lambda j, m: (m, 0)), pl.BlockSpec((n, D), lambda j, m: (j, 0)), blk, blk],
        out_specs=[blk, blk], out_shape=[oshape, oshape], args=(df, w_out, zg, zu), semantics=("parallel", "parallel"),
        exchange=exchange)


def _ffn_in_dgrad(dzg, dzu, w4, name, exchange=None, norm=None):
    S = dzg.shape[0]
    D, n = w4.shape[1], w4.shape[2]
    tm = min(ROW_TILE, S)
    n_norm = len(norm.inputs) if norm else 0

    def body(dzg_ref, dzu_ref, w_ref, *refs):
        acc = _nt(dzg_ref[:, 0:n], w_ref[0])
        acc += _nt(dzg_ref[:, n:2 * n], w_ref[1])
        acc += _nt(dzu_ref[:, 0:n], w_ref[2])
        acc += _nt(dzu_ref[:, n:2 * n], w_ref[3])
        if norm:
            norm.step(acc, refs[:n_norm], refs[n_norm:])
        else:
            refs[0][...] = acc

    blk = pl.BlockSpec((tm, 2 * n), lambda m: (m, 0))
    in_specs, args = [blk, blk, _resident(w4.shape, lambda m: (0, 0, 0))], [dzg, dzu, w4]
    out_specs, out_shape = [pl.BlockSpec((tm, D), lambda m: (m, 0))], [jax.ShapeDtypeStruct((S, D), F32)]
    if norm:
        norm_in, out_specs = norm.specs(tm, D)
        in_specs, args, out_shape = in_specs + norm_in, args + norm.inputs, norm.out_shape
    return _call(body, name=name, grid=(S // tm,), in_specs=in_specs, out_specs=out_specs, out_shape=out_shape, args=args,
                 semantics=("arbitrary",) if norm else ("parallel",), exchange=exchange)


def _matmul_nt(pieces, w, tm, name, exchange=None, norm=None):
    S = pieces[0].shape[0]
    ks = [p.shape[1] for p in pieces]
    N = w.shape[0]
    tm = min(tm, S)
    n_norm = len(norm.inputs) if norm else 0

    def body(*refs):
        p_refs, w_ref, refs = refs[:len(ks)], refs[len(ks)], refs[len(ks) + 1:]
        acc, off = None, 0
        for p_ref, k in zip(p_refs, ks):
            part = _nt(p_ref[...], w_ref[:, off:off + k])
            acc = part if acc is None else acc + part
            off += k
        if norm:
            norm.step(acc, refs[:n_norm], refs[n_norm:])
        else:
            refs[0][...] = acc

    in_specs = [pl.BlockSpec((tm, k), lambda m: (m, 0)) for k in ks] + [_resident(w.shape, lambda m: (0, 0))]
    args = list(pieces) + [w]
    out_specs, out_shape = [pl.BlockSpec((tm, N), lambda m: (m, 0))], [jax.ShapeDtypeStruct((S, N), F32)]
    if norm:
        norm_in, out_specs = norm.specs(tm, N)
        in_specs, args, out_shape = in_specs + norm_in, args + norm.inputs, norm.out_shape
    return _call(body, name=name, grid=(S // tm,), in_specs=in_specs, out_specs=out_specs, out_shape=out_shape, args=args,
                 semantics=("arbitrary",) if norm else ("parallel",), exchange=exchange)


def _wgrad(a, gs, tn, name, exchange=None):
    S, Ka = a.shape
    N = gs[0].shape[1]
    ts = min(ROW_TILE * (2 if Ka <= D_MODEL else 1), S)

    def body(a_ref, *refs):
        g_refs, o_ref = refs[:-1], refs[-1]

        @pl.when(pl.program_id(1) == 0)
        def _():
            o_ref[...] = jnp.zeros_like(o_ref)

        a_t = a_ref[...].T
        for i, g_ref in enumerate(g_refs):
            o_ref[i] += _nn(a_t, g_ref[...])

    return _call(
        body, name=name, grid=(N // tn, S // ts),
        in_specs=[pl.BlockSpec((ts, Ka), lambda j, s: (s, 0))] + [pl.BlockSpec((ts, tn), lambda j, s: (s, j))] * len(gs),
        out_specs=[pl.BlockSpec((len(gs), None, Ka, tn), lambda j, s: (0, j, 0, 0))],
        out_shape=[jax.ShapeDtypeStruct((len(gs), N // tn, Ka, tn), F32)], args=(a, *gs),
        semantics=("parallel", "arbitrary"), exchange=exchange)


def _wgrad_pieces(a, pieces, tn, name):
    S, Ka = a.shape
    ts = min(ROW_TILE, S)
    blocks = [(i, j) for i, p in enumerate(pieces) for j in range(p.shape[1] // tn)]

    def body(a_ref, *refs):
        g_refs, o_ref = refs[:-1], refs[-1]

        @pl.when(pl.program_id(0) == 0)
        def _():
            o_ref[...] = jnp.zeros_like(o_ref)

        a_t = a_ref[...].T
        for b, g_ref in enumerate(g_refs):
            o_ref[b] += _nn(a_t, g_ref[...])

    return pl.pallas_call(
        body, name=name, grid=(S // ts,),
        in_specs=[pl.BlockSpec((ts, Ka), lambda s: (s, 0))] + [pl.BlockSpec((ts, tn), lambda s, j=j: (s, j)) for _, j in blocks],
        out_specs=pl.BlockSpec((len(blocks), Ka, tn), lambda s: (0, 0, 0)),
        out_shape=jax.ShapeDtypeStruct((len(blocks), Ka, tn), F32), compiler_params=_params("arbitrary"),
    )(a, *[pieces[i] for i, _ in blocks])


def _wgrad_rows(lhs, g, name):
    S, Ka = lhs[0].shape
    N = g.shape[1]
    ts = min(ROW_TILE, S)

    def body(*refs):
        a_refs, g_ref, o_ref = refs[:-2], refs[-2], refs[-1]

        @pl.when(pl.program_id(0) == 0)
        def _():
            o_ref[...] = jnp.zeros_like(o_ref)

        gv = g_ref[...]
        for i, a_ref in enumerate(a_refs):
            o_ref[i] += _tn(a_ref[...], gv)

    return pl.pallas_call(
        body, name=name, grid=(S // ts,),
        in_specs=[pl.BlockSpec((ts, Ka), lambda s: (s, 0))] * len(lhs) + [pl.BlockSpec((ts, N), lambda s: (s, 0))],
        out_specs=pl.BlockSpec((len(lhs), Ka, N), lambda s: (0, 0, 0)),
        out_shape=jax.ShapeDtypeStruct((len(lhs), Ka, N), F32), compiler_params=_params("arbitrary"),
    )(*lhs, g)


def _hgrn_chunk_common(qr, fr, lb, oml, tri, last):
    sig_nf = _sigmoid(-fr)
    k = oml * sig_nf
    f_small = lb + oml * (jnp.exp(jnp.minimum(fr, 0.0)) * sig_nf)
    use_k = k < 0.5
    f = jnp.where(use_k, 1.0 - k, f_small)
    g = jnp.where(use_k, jnp.log1p(-k), jnp.log(f_small)) * math.log2(math.e)
    q = qr * _sigmoid(qr)
    G = _nn(tri, g, precision=lax.Precision.HIGHEST)
    Gl = G[last:last + 1]
    return q, k, f, G, Gl


def _hgrn_consts(reverse):
    C = HG_CHUNK
    r = lax.broadcasted_iota(jnp.int32, (C, C), 0)
    cc = lax.broadcasted_iota(jnp.int32, (C, C), 1)
    tri = ((cc >= r) if reverse else (cc <= r)).astype(F32)
    tri_t = ((cc <= r) if reverse else (cc >= r)).astype(F32)
    rid = lax.broadcasted_iota(jnp.int32, (C, HG_WIDTH), 0)
    return tri, tri_t, rid, (0 if reverse else C - 1)


def _head_slices():
    return [slice(h * HG_DIM, (h + 1) * HG_DIM) for h in range(HG_HEADS)]


def _per_head_lane_sum(x):
    C = x.shape[0]
    return jnp.concatenate(
        [jnp.broadcast_to(jnp.sum(x[:, sl], axis=-1, keepdims=True), (C, HG_DIM)) for sl in _head_slices()], axis=1)


HG_TILE = 8


def _pair_tiles(s, reverse):
    blk, r = divmod(s, HG_TILE)
    n_tiles = HG_CHUNK // HG_TILE
    others = range(0, blk) if reverse else range(blk + 1, n_tiles)
    return [(blk, r)] + [(t, None) for t in others]


def _pair_decay(G, s, tile, r, rid8, reverse, keys=False):
    rs = slice(tile * HG_TILE, (tile + 1) * HG_TILE)
    d = (G[s:s + 1] - G[rs]) if keys else (G[rs] - G[s:s + 1])
    if r is not None:
        d = jnp.where((rid8 <= r) if reverse else (rid8 >= r), d, NEG)
    return rs, jnp.exp2(d)


def _hgrn_fwd_both(z, lbs, name, exchange=None):
    S = z.shape[0]
    C, DK, W = HG_CHUNK, HG_DIM, HG_WIDTH
    tb = min(HG_ROWS, S)
    n_t, n_c = S // tb, tb // C
    dirs = (0, 1)

    def body(qf_ref, ff_ref, vf_ref, qb_ref, fb_ref, vb_ref, lbf_ref, lbb_ref, of_ref, stf_out, ob_ref, stb_out, st_ref):
        @pl.when(pl.program_id(0) == 0)
        def _():
            st_ref[...] = jnp.zeros_like(st_ref)

        q_refs, f_refs, v_refs, lb_refs = (qf_ref, qb_ref), (ff_ref, fb_ref), (vf_ref, vb_ref), (lbf_ref, lbb_ref)
        o_refs, st_outs = (of_ref, ob_ref), (stf_out, stb_out)
        consts = [_hgrn_consts(d == 1) for d in dirs]
        rid8 = lax.broadcasted_iota(jnp.int32, (HG_TILE, W), 0)

        def chunk(ci, carry):
            cidx = [ci, n_c - 1 - ci]
            rows = [pl.ds(pl.multiple_of(c * C, C), C) for c in cidx]
            v = [v_refs[d][rows[d], :] for d in dirs]
            com = [_hgrn_chunk_common(q_refs[d][rows[d], :], f_refs[d][rows[d], :], lb_refs[d][0:1, :], lb_refs[d][1:2, :],
                                      consts[d][0], consts[d][3]) for d in dirs]
            q, k, G, Gl = [c[0] for c in com], [c[1] for c in com], [c[3] for c in com], [c[4] for c in com]
            qd = [(q[d] * jnp.exp2(G[d])).astype(BF16) for d in dirs]
            kd = [(k[d] * jnp.exp2(Gl[d] - G[d])).astype(BF16) for d in dirs]
            e_gl = [jnp.exp2(Gl[d]) for d in dirs]
            v_b = [v[d].astype(BF16) for d in dirs]
            inter = [[], []]
            for h, sl in enumerate(_head_slices()):
                for d in dirs:
                    st0 = st_ref[d, h]
                    st_outs[d][h, cidx[d]] = st0
                    inter[d].append(_nt(qd[d][:, sl], st0.astype(BF16)))
                    st_ref[d, h] = st0 * e_gl[d][:, sl] + _tn(v_b[d][:, sl], kd[d][:, sl])
            o_t = [[jnp.concatenate(inter[d], axis=1)[t * HG_TILE:(t + 1) * HG_TILE] for t in range(C // HG_TILE)] for d in dirs]
            for s in range(C):
                for d in dirs:
                    k_s, v_s = k[d][s:s + 1], v[d][s:s + 1]
                    for tile, r in _pair_tiles(s, d == 1):
                        rs, e_s = _pair_decay(G[d], s, tile, r, rid8, d == 1)
                        o_t[d][tile] = o_t[d][tile] + _per_head_lane_sum(q[d][rs] * k_s * e_s) * v_s
            for d in dirs:
                o_refs[d][rows[d], :] = jnp.concatenate(o_t[d], axis=0)
            return carry

        lax.fori_loop(0, n_c, chunk, 0, unroll=4)

    def sec(j, back):
        return pl.BlockSpec((tb, W), (lambda i: (n_t - 1 - i, j)) if back else (lambda i: (i, j)))

    def st_spec(back):
        return pl.BlockSpec((HG_HEADS, n_c, DK, DK), (lambda i: (0, n_t - 1 - i, 0, 0)) if back else (lambda i: (0, i, 0, 0)))

    vec = pl.BlockSpec((2, W), lambda i: (0, 0))
    o_shape = jax.ShapeDtypeStruct((S, W), F32)
    st_shape = jax.ShapeDtypeStruct((HG_HEADS, S // C, DK, DK), F32)
    return _call(
        body, name=name, grid=(n_t,),
        in_specs=[sec(0, False), sec(1, False), sec(3, False), sec(0, True), sec(2, True), sec(3, True), vec, vec],
        out_specs=[sec(0, False), st_spec(False), sec(0, True), st_spec(True)],
        out_shape=[o_shape, st_shape, o_shape, st_shape],
        scratch_shapes=[pltpu.VMEM((2, HG_HEADS, DK, DK), F32)], args=(z, z, z, z, z, z, lbs[0], lbs[1]),
        semantics=("arbitrary",), exchange=exchange)


def _hgrn_bwd(z, lb, do, states, direction, name, acc=None, exchange=None):
    S = z.shape[0]
    C, DK, W = HG_CHUNK, HG_DIM, HG_WIDTH
    tb = min(HG_ROWS, S)
    n_t, n_c = S // tb, tb // C
    reverse = direction == 1
    tmap = (lambda i: i) if reverse else (lambda i: n_t - 1 - i)

    def body(*refs):
        if acc:
            q_ref, f_ref, v_ref, lb_ref, do_ref, st_in_ref, dqa_ref, dva_ref, dq_ref, df_ref, dv_ref, doml_ref, dst_ref = refs
        else:
            q_ref, f_ref, v_ref, lb_ref, do_ref, st_in_ref, dq_ref, df_ref, dv_ref, doml_ref, dst_ref = refs

        @pl.when(pl.program_id(0) == 0)
        def _():
            dst_ref[...] = jnp.zeros_like(dst_ref)
            doml_ref[...] = jnp.zeros_like(doml_ref)

        lbv, oml = lb_ref[0:1, :], lb_ref[1:2, :]
        tri, tri_t, rid, last = _hgrn_consts(reverse)
        rid8 = lax.broadcasted_iota(jnp.int32, (HG_TILE, W), 0)

        def chunk(ci, carry):
            cidx = ci if reverse else (n_c - 1 - ci)
            rows = pl.ds(pl.multiple_of(cidx * C, C), C)
            qr, fr, v, dov = q_ref[rows, :], f_ref[rows, :], v_ref[rows, :], do_ref[rows, :]
            q, k, f, G, Gl = _hgrn_chunk_common(qr, fr, lbv, oml, tri, last)
            e_g, e_gl, e_kd = jnp.exp2(G), jnp.exp2(Gl), jnp.exp2(Gl - G)
            qd, kd = q * e_g, k * e_kd
            do_b, v_b, qd_b, kd_b = dov.astype(BF16), v.astype(BF16), qd.astype(BF16), kd.astype(BF16)
            dqd, dkd, dv, state_dot = [], [], [], []
            for h, sl in enumerate(_head_slices()):
                st0, dst1 = st_in_ref[h, cidx], dst_ref[h]
                dst1_b = dst1.astype(BF16)
                dqd.append(_nn(do_b[:, sl], st0.astype(BF16)))
                dkd.append(_nn(v_b[:, sl], dst1_b))
                dv.append(_nt(kd_b[:, sl], dst1_b))
                state_dot.append(jnp.sum(st0 * dst1, axis=0, keepdims=True))
                dst_ref[h] = dst1 * e_gl[:, sl] + _tn(do_b[:, sl], qd_b[:, sl])
            dqd, dkd, dv = [jnp.concatenate(t, axis=1) for t in (dqd, dkd, dv)]
            d_gl = e_gl * jnp.concatenate(state_dot, axis=1) + jnp.sum(dkd * kd, axis=0, keepdims=True)
            dq, dk = dqd * e_g, dkd * e_kd
            n_tiles = C // HG_TILE
            dq_t, dk_t, dv_t = [[x[t * HG_TILE:(t + 1) * HG_TILE] for t in range(n_tiles)] for x in (dq, dk, dv)]
            for s in range(C):
                k_s, v_s = k[s:s + 1], v[s:s + 1]
                for tile, r in _pair_tiles(s, reverse):
                    rs, e_s = _pair_decay(G, s, tile, r, rid8, reverse)
                    dq_t[tile] = dq_t[tile] + _per_head_lane_sum(dov[rs] * v_s) * e_s * k_s
            for t in range(C):
                q_t, do_t = q[t:t + 1], dov[t:t + 1]
                for tile, r in _pair_tiles(t, not reverse):
                    rs, x_t = _pair_decay(G, t, tile, r, rid8, not reverse, keys=True)
                    qx = q_t * x_t
                    dv_t[tile] = dv_t[tile] + _per_head_lane_sum(k[rs] * qx) * do_t
                    dk_t[tile] = dk_t[tile] + _per_head_lane_sum(v[rs] * do_t) * qx
            dq, dk, dv = [jnp.concatenate(x, axis=0) for x in (dq_t, dk_t, dv_t)]
            d_big_g = dq * q - dk * k + jnp.where(rid == last, d_gl, 0.0)
            dg = _nn(tri_t, d_big_g, precision=lax.Precision.HIGHEST)
            dk_all = dk - dg / f
            sig_nf = _sigmoid(-fr)
            df_ref[rows, :] = (-dk_all * k * (1.0 - sig_nf)).astype(df_ref.dtype)
            doml_ref[...] += jnp.sum(dk_all * sig_nf, axis=0, keepdims=True)
            sq = _sigmoid(qr)
            dqr = dq * (sq * (1.0 + qr * (1.0 - sq)))
            if acc:
                dqr = dqr + dqa_ref[rows, :]
                dv = dv + dva_ref[rows, :]
            dq_ref[rows, :] = dqr.astype(dq_ref.dtype)
            dv_ref[rows, :] = dv.astype(dv_ref.dtype)
            return carry

        lax.fori_loop(0, n_c, chunk, 0, unroll=8)

    def sec(j):
        return pl.BlockSpec((tb, W), lambda i: (tmap(i), j))

    vec = pl.BlockSpec((1, W), lambda i: (0, 0))
    ins = [z, z, z, lb, do, states]
    in_specs = [sec(0), sec(1 + direction), sec(3), pl.BlockSpec((2, W), lambda i: (0, 0)), sec(0),
                pl.BlockSpec((HG_HEADS, n_c, DK, DK), lambda i: (0, tmap(i), 0, 0))]
    if acc:
        ins += list(acc)
        in_specs += [sec(0), sec(0)]
    final = jax.ShapeDtypeStruct((S, W), BF16)
    partial = final if acc else jax.ShapeDtypeStruct((S, W), F32)
    return _call(
        body, name=name, grid=(n_t,), in_specs=in_specs,
        out_specs=[sec(0), sec(0), sec(0), vec],
        out_shape=[partial, final, partial, jax.ShapeDtypeStruct((1, W), F32)],
        scratch_shapes=[pltpu.VMEM((HG_HEADS, DK, DK), F32)], args=ins, semantics=("arbitrary",), exchange=exchange)


def _hgrn_post_fwd(o_f, o_b, z, norm_g, name):
    S = z.shape[0]
    tr = min(ROW_TILE, S)

    def body(of_ref, ob_ref, gr_ref, ng_ref, y_ref):
        o = of_ref[...] + ob_ref[...]
        gr = gr_ref[...]
        gate = gr * _sigmoid(gr)
        ng = ng_ref[...]
        for h in range(HG_HEADS):
            sl = slice(h * HG_DIM, (h + 1) * HG_DIM)
            oh = o[:, sl]
            rstd = lax.rsqrt(jnp.mean(oh * oh, axis=-1, keepdims=True) + EPS)
            y_ref[:, sl] = (oh * rstd * ng[:, sl] * gate[:, sl]).astype(y_ref.dtype)

    row = pl.BlockSpec((tr, HG_WIDTH), lambda i: (i, 0))
    return pl.pallas_call(
        body, name=name, grid=(S // tr,),
        in_specs=[row, row, pl.BlockSpec((tr, HG_WIDTH), lambda i: (i, 4)), pl.BlockSpec((1, HG_WIDTH), lambda i: (0, 0))],
        out_specs=row, out_shape=jax.ShapeDtypeStruct((S, HG_WIDTH), BF16), compiler_params=_params("parallel"),
    )(o_f, o_b, z, norm_g)


def _hgrn_post_bwd(dy, o_f, o_b, z, norm_g, name):
    S = z.shape[0]
    tr = min(ROW_TILE, S)

    def body(dy_ref, of_ref, ob_ref, gr_ref, ng_ref, do_ref, dgr_ref, dng_ref):
        @pl.when(pl.program_id(0) == 0)
        def _():
            dng_ref[...] = jnp.zeros_like(dng_ref)

        o = of_ref[...] + ob_ref[...]
        gr, ng, dyv = gr_ref[...], ng_ref[...], dy_ref[...]
        sg = _sigmoid(gr)
        for h in range(HG_HEADS):
            sl = slice(h * HG_DIM, (h + 1) * HG_DIM)
            oh, dyh, grh, sgh, ngh = o[:, sl], dyv[:, sl], gr[:, sl], sg[:, sl], ng[:, sl]
            rstd = lax.rsqrt(jnp.mean(oh * oh, axis=-1, keepdims=True) + EPS)
            on = oh * rstd
            du = dyh * (grh * sgh)
            dgr_ref[:, sl] = (dyh * (on * ngh) * (sgh * (1.0 + grh * (1.0 - sgh)))).astype(dgr_ref.dtype)
            dng_ref[:, sl] += jnp.sum(du * on, axis=0, keepdims=True)
            don = du * ngh
            do_ref[:, sl] = rstd * (don - on * jnp.mean(don * on, axis=-1, keepdims=True))

    row = pl.BlockSpec((tr, HG_WIDTH), lambda i: (i, 0))
    vec = pl.BlockSpec((1, HG_WIDTH), lambda i: (0, 0))
    full = jax.ShapeDtypeStruct((S, HG_WIDTH), F32)
    return pl.pallas_call(
        body, name=name, grid=(S // tr,),
        in_specs=[row, row, row, pl.BlockSpec((tr, HG_WIDTH), lambda i: (i, 4)), vec],
        out_specs=[row, row, vec],
        out_shape=[full, jax.ShapeDtypeStruct((S, HG_WIDTH), BF16), jax.ShapeDtypeStruct((1, HG_WIDTH), F32)],
        compiler_params=_params("arbitrary"),
    )(dy, o_f, o_b, z, norm_g)


def _t5_bucket_table():
    rel = (np.arange(3 * BLOCK)[None, :] - BLOCK) - np.arange(BLOCK)[:, None]
    nb = NUM_BUCKETS // 2
    max_exact = nb // 2
    ret = (rel > 0).astype(np.int32) * nb
    n = np.abs(rel)
    ratio = np.log(np.maximum(n, 1).astype(np.float32) / np.float32(max_exact)) / np.float32(math.log(MAX_DISTANCE / max_exact))
    large = max_exact + (ratio.astype(np.float32) * np.float32(nb - max_exact)).astype(np.int32)
    large = np.minimum(large, nb - 1)
    bucket = ret + np.where(n < max_exact, n, large)
    return bucket.astype(np.int32), (n <= WINDOW)


def _bias_table(rel_bias, name):
    bucket, in_band = _t5_bucket_table()
    idx = jnp.asarray(np.where(in_band, bucket, -1))

    def body(rb_ref, idx_ref, o_ref):
        h = pl.program_id(0)
        iv = idx_ref[...]
        acc = jnp.where(iv < 0, NEG, 0.0).astype(F32)
        for b in range(NUM_BUCKETS):
            acc = acc + jnp.where(iv == b, rb_ref[b, h], 0.0)
        o_ref[...] = acc

    return pl.pallas_call(
        body, name=name, grid=(ATT_Q_HEADS,),
        in_specs=[pl.BlockSpec(memory_space=pltpu.SMEM), pl.BlockSpec((BLOCK, 3 * BLOCK), lambda h: (0, 0))],
        out_specs=pl.BlockSpec((None, BLOCK, 3 * BLOCK), lambda h: (h, 0, 0)),
        out_shape=jax.ShapeDtypeStruct((ATT_Q_HEADS, BLOCK, 3 * BLOCK), F32), compiler_params=_params("parallel"),
    )(rel_bias, idx)


def _bias_grad(ds_sum_t, name):
    bucket, in_band = _t5_bucket_table()
    idx_t = jnp.asarray(np.where(in_band, bucket, -1).T)

    def body(ds_ref, idx_ref, o_ref):
        iv, ds = idx_ref[...], ds_ref[...]
        for b in range(NUM_BUCKETS):
            o_ref[b:b + 1, :] = jnp.sum(jnp.where(iv == b, ds, 0.0), axis=0, keepdims=True)

    return pl.pallas_call(
        body, name=name, grid=(ATT_Q_HEADS,),
        in_specs=[pl.BlockSpec((None, 3 * BLOCK, BLOCK), lambda h: (h // ATT_GROUP, 0, h % ATT_GROUP)),
                  pl.BlockSpec((3 * BLOCK, BLOCK), lambda h: (0, 0))],
        out_specs=pl.BlockSpec((None, NUM_BUCKETS, BLOCK), lambda h: (h, 0, 0)),
        out_shape=jax.ShapeDtypeStruct((ATT_Q_HEADS, NUM_BUCKETS, BLOCK), F32), compiler_params=_params("parallel"),
    )(ds_sum_t, idx_t)


Q_COL = 5 * HG_WIDTH
KV_COL = Q_COL + ATT_WIDTH
GROUP_WIDTH = ATT_GROUP * ATT_HEAD_DIM


def _stack_heads(blk):
    dh = ATT_HEAD_DIM
    return jnp.concatenate([blk[:, g * dh:(g + 1) * dh] for g in range(ATT_GROUP)], axis=0)


def _unstack_heads(st):
    return jnp.concatenate([st[g * BLOCK:(g + 1) * BLOCK] for g in range(ATT_GROUP)], axis=1)


def _rms_rows(x):
    rstd = lax.rsqrt(jnp.mean(x * x, axis=-1, keepdims=True) + EPS)
    return x * rstd, rstd


def _edge_ok(n, nb):
    colid = lax.broadcasted_iota(jnp.int32, (ATT_GROUP * BLOCK, 3 * BLOCK), 1)
    return jnp.logical_and(jnp.logical_or(colid >= BLOCK, n > 0), jnp.logical_or(colid < 2 * BLOCK, n < nb - 1))


def _sink_column(sink_ref, j=0):
    heads = range(j * ATT_GROUP, (j + 1) * ATT_GROUP)
    return jnp.concatenate([jnp.broadcast_to(sink_ref[h][:, 0:1], (BLOCK, 1)) for h in heads], axis=0)


def _attn_fwd(z, q_g, k_g, sink, bias, name):
    S = z.shape[0]
    nb = S // BLOCK
    G, dh, KV = ATT_GROUP, ATT_HEAD_DIM, ATT_KV_HEADS
    scale = 1.0 / math.sqrt(dh)

    def body(q_ref, kv0, kv1, kv2, qg_ref, kg_ref, sink_ref, bias_ref, o_ref):
        n = pl.program_id(0)
        edge_ok = _edge_ok(n, nb)
        cat = jnp.concatenate([kv0[...], kv1[...], kv2[...]], axis=0)
        qblk = q_ref[...]
        kn = [(_rms_rows(cat[:, j * dh:(j + 1) * dh])[0] * kg_ref[...]).astype(BF16) for j in range(KV)]
        vb = [cat[:, (KV + j) * dh:(KV + j + 1) * dh].astype(BF16) for j in range(KV)]
        qn = [(_rms_rows(_stack_heads(qblk[:, j * GROUP_WIDTH:(j + 1) * GROUP_WIDTH]))[0] * (qg_ref[...] * scale)).astype(BF16)
              for j in range(KV)]
        s = [_nt(qn[j], kn[j]) + bias_ref[j * G:(j + 1) * G].reshape(G * BLOCK, 3 * BLOCK) for j in range(KV)]
        s = [jnp.where(edge_ok, sj, NEG) for sj in s]
        sinks = [_sink_column(sink_ref, j) for j in range(KV)]
        m = [jnp.maximum(jnp.max(s[j], axis=-1, keepdims=True), sinks[j]) for j in range(KV)]
        e = [jnp.exp(s[j] - m[j]) for j in range(KV)]
        den = [jnp.sum(e[j], axis=-1, keepdims=True) + jnp.exp(sinks[j] - m[j]) for j in range(KV)]
        o = [_nn(e[j].astype(BF16), vb[j]) * (1.0 / den[j]) for j in range(KV)]
        o_ref[...] = jnp.concatenate([_unstack_heads(oj) for oj in o], axis=1).astype(o_ref.dtype)

    def kv(shift):
        return pl.BlockSpec((BLOCK, 2 * KV_WIDTH), lambda n: (jnp.clip(n + shift, 0, nb - 1), KV_COL // (2 * KV_WIDTH)))

    gain = pl.BlockSpec((1, dh), lambda n: (0, 0))
    return pl.pallas_call(
        body, name=name, grid=(nb,),
        in_specs=[pl.BlockSpec((BLOCK, ATT_WIDTH), lambda n: (n, Q_COL // ATT_WIDTH)), kv(-1), kv(0), kv(1), gain, gain,
                  pl.BlockSpec((ATT_Q_HEADS, 1, BLOCK), lambda n: (0, 0, 0)),
                  pl.BlockSpec((ATT_Q_HEADS, BLOCK, 3 * BLOCK), lambda n: (0, 0, 0))],
        out_specs=pl.BlockSpec((BLOCK, ATT_WIDTH), lambda n: (n, 0)),
        out_shape=jax.ShapeDtypeStruct((S, ATT_WIDTH), BF16), compiler_params=_params("parallel"),
    )(z, z, z, z, q_g, k_g, sink, bias)


def _attn_bwd(z, q_g, k_g, sink, bias, do, name):
    S = z.shape[0]
    nb = S // BLOCK
    G, dh, KV = ATT_GROUP, ATT_HEAD_DIM, ATT_KV_HEADS
    scale = 1.0 / math.sqrt(dh)
    both = range(KV)
    bias_t = bias.reshape(KV, G, BLOCK, 3 * BLOCK).transpose(0, 3, 1, 2).reshape(KV, 3 * BLOCK, G * BLOCK)

    def body(q_ref, kv0, kv1, kv2, qg_ref, kg_ref, sink_ref, bias_ref, do_ref,
             dq_ref, dkw_ref, dvw_ref, ds_ref, dsink_ref, dqg_ref):
        n = pl.program_id(0)

        @pl.when(n == 0)
        def _():
            ds_ref[...] = jnp.zeros_like(ds_ref)
            dsink_ref[...] = jnp.zeros_like(dsink_ref)
            dqg_ref[...] = jnp.zeros_like(dqg_ref)

        rowid = lax.broadcasted_iota(jnp.int32, (3 * BLOCK, G * BLOCK), 0)
        edge_ok = jnp.logical_and(jnp.logical_or(rowid >= BLOCK, n > 0), jnp.logical_or(rowid < 2 * BLOCK, n < nb - 1))
        qg = qg_ref[...]
        cat = jnp.concatenate([kv0[...], kv1[...], kv2[...]], axis=0)
        qblk, doblk = q_ref[...], do_ref[...]
        kn = [(_rms_rows(cat[:, j * dh:(j + 1) * dh])[0] * kg_ref[...]).astype(BF16) for j in both]
        vb = [cat[:, (KV + j) * dh:(KV + j + 1) * dh].astype(BF16) for j in both]
        norm = [_rms_rows(_stack_heads(qblk[:, j * GROUP_WIDTH:(j + 1) * GROUP_WIDTH])) for j in both]
        qn = [(norm[j][0] * (qg * scale)).astype(BF16) for j in both]
        do_b = [_stack_heads(doblk[:, j * GROUP_WIDTH:(j + 1) * GROUP_WIDTH]).astype(BF16) for j in both]
        s = [_nt(kn[j], qn[j]) + bias_ref[j] for j in both]
        dp = [_nt(vb[j], do_b[j]) for j in both]
        s = [jnp.where(edge_ok, sj, NEG) for sj in s]
        sinks = [jnp.concatenate([sink_ref[j * G + g] for g in range(G)], axis=1) for j in both]
        m = [jnp.maximum(jnp.max(s[j], axis=0, keepdims=True), sinks[j]) for j in both]
        e = [jnp.exp(s[j] - m[j]) for j in both]
        e_sink = [jnp.exp(sinks[j] - m[j]) for j in both]
        inv = [1.0 / (jnp.sum(e[j], axis=0, keepdims=True) + e_sink[j]) for j in both]
        p = [e[j] * inv[j] for j in both]
        delta = [jnp.sum(p[j] * dp[j], axis=0, keepdims=True) for j in both]
        ds = [p[j] * (dp[j] - delta[j]) for j in both]
        ds_b = [dsj.astype(BF16) for dsj in ds]
        dqn = [_tn(kn[j], ds_b[j]).T * scale for j in both]
        for j in both:
            dvw_ref[j] = _nn(p[j].astype(BF16), do_b[j])
            dkw_ref[j] = _nn(ds_b[j], qn[j])
        for j in both:
            ds_ref[j] += ds[j]
            sink_term = e_sink[j] * inv[j] * delta[j]
            for g in range(G):
                dsink_ref[j * G + g] += (jnp.zeros((1, BLOCK), F32)
                                         - jnp.sum(sink_term[:, g * BLOCK:(g + 1) * BLOCK], axis=1, keepdims=True))
        dq = []
        for j in both:
            qhat, rstd = norm[j]
            dqg_ref[j] += jnp.sum(dqn[j] * qhat, axis=0, keepdims=True)
            dqh = dqn[j] * qg
            dq.append(_unstack_heads(rstd * (dqh - qhat * jnp.mean(dqh * qhat, axis=-1, keepdims=True))))
        dq_ref[...] = jnp.concatenate(dq, axis=1).astype(dq_ref.dtype)

    def kv(shift):
        return pl.BlockSpec((BLOCK, 2 * KV_WIDTH), lambda n: (jnp.clip(n + shift, 0, nb - 1), KV_COL // (2 * KV_WIDTH)))

    gain = pl.BlockSpec((1, dh), lambda n: (0, 0))
    sink_spec = pl.BlockSpec((ATT_Q_HEADS, 1, BLOCK), lambda n: (0, 0, 0))
    bias_spec = pl.BlockSpec((KV, 3 * BLOCK, G * BLOCK), lambda n: (0, 0, 0))
    win = pl.BlockSpec((KV, None, 3 * BLOCK, dh), lambda n: (0, n, 0, 0))
    wshape = jax.ShapeDtypeStruct((KV, nb, 3 * BLOCK, dh), F32)
    return pl.pallas_call(
        body, name=name, grid=(nb,),
        in_specs=[pl.BlockSpec((BLOCK, ATT_WIDTH), lambda n: (n, Q_COL // ATT_WIDTH)), kv(-1), kv(0), kv(1), gain, gain,
                  sink_spec, bias_spec, pl.BlockSpec((BLOCK, ATT_WIDTH), lambda n: (n, HG_WIDTH // ATT_WIDTH))],
        out_specs=[pl.BlockSpec((BLOCK, ATT_WIDTH), lambda n: (n, 0)), win, win, bias_spec, sink_spec,
                   pl.BlockSpec((KV, 1, dh), lambda n: (0, 0, 0))],
        out_shape=[jax.ShapeDtypeStruct((S, ATT_WIDTH), BF16), wshape, wshape,
                   jax.ShapeDtypeStruct((KV, 3 * BLOCK, G * BLOCK), F32),
                   jax.ShapeDtypeStruct((ATT_Q_HEADS, 1, BLOCK), F32),
                   jax.ShapeDtypeStruct((KV, 1, dh), F32)],
        compiler_params=_params("arbitrary"),
    )(z, z, z, z, q_g, k_g, sink, bias_t, do)


def _attn_kv_reduce(dkw, dvw, z, k_g, name):
    S = z.shape[0]
    nb = S // BLOCK
    dh = ATT_HEAD_DIM
    kb = min(8, nb)
    steps = nb // kb

    def body(a_lo, a, a_hi, b_lo, b, b_hi, kv_ref, kg_ref, dkv_ref, dkg_ref):
        n = pl.program_id(0)

        @pl.when(n == 0)
        def _():
            dkg_ref[...] = jnp.zeros_like(dkg_ref)

        lo = jnp.where(n > 0, 1.0, 0.0)
        hi = jnp.where(n < steps - 1, 1.0, 0.0)

        def overlap_add(w, w_lo, w_hi, j, i):
            before = lo * w_lo[j] if i == 0 else w[j, i - 1, 2 * BLOCK:3 * BLOCK, :]
            after = hi * w_hi[j] if i == kb - 1 else w[j, i + 1, 0:BLOCK, :]
            return w[j, i, BLOCK:2 * BLOCK, :] + before + after

        dkg = [jnp.zeros((1, dh), F32) for _ in range(ATT_KV_HEADS)]
        for i in range(kb):
            rows = slice(i * BLOCK, (i + 1) * BLOCK)
            dks, dvs = [], []
            for j in range(ATT_KV_HEADS):
                dkn = overlap_add(a, a_lo, a_hi, j, i)
                dvs.append(overlap_add(b, b_lo, b_hi, j, i))
                khat, rstd = _rms_rows(kv_ref[rows, j * dh:(j + 1) * dh])
                dkg[j] = dkg[j] + jnp.sum(dkn * khat, axis=0, keepdims=True)
                dkh = dkn * kg_ref[...]
                dks.append(rstd * (dkh - khat * jnp.mean(dkh * khat, axis=-1, keepdims=True)))
            dkv_ref[rows, :] = jnp.concatenate(dks + dvs, axis=1).astype(dkv_ref.dtype)
        for j in range(ATT_KV_HEADS):
            dkg_ref[j] += dkg[j]

    main = pl.BlockSpec((ATT_KV_HEADS, kb, 3 * BLOCK, dh), lambda n: (0, n, 0, 0))
    halo_lo = pl.BlockSpec((ATT_KV_HEADS, None, BLOCK, dh), lambda n: (0, jnp.maximum(n * kb - 1, 0), 2, 0))
    halo_hi = pl.BlockSpec((ATT_KV_HEADS, None, BLOCK, dh), lambda n: (0, jnp.minimum(n * kb + kb, nb - 1), 0, 0))
    return pl.pallas_call(
        body, name=name, grid=(steps,),
        in_specs=[halo_lo, main, halo_hi, halo_lo, main, halo_hi,
                  pl.BlockSpec((kb * BLOCK, 2 * KV_WIDTH), lambda n: (n, KV_COL // (2 * KV_WIDTH))),
                  pl.BlockSpec((1, dh), lambda n: (0, 0))],
        out_specs=[pl.BlockSpec((kb * BLOCK, 2 * KV_WIDTH), lambda n: (n, 0)),
                   pl.BlockSpec((ATT_KV_HEADS, 1, dh), lambda n: (0, 0, 0))],
        out_shape=[jax.ShapeDtypeStruct((S, 2 * KV_WIDTH), BF16), jax.ShapeDtypeStruct((ATT_KV_HEADS, 1, dh), F32)],
        compiler_params=_params("arbitrary"),
    )(dkw, dkw, dkw, dvw, dvw, dvw, z, k_g)


def _ada_wgrad(c_act_t, dm, name):
    D, nbatch = c_act_t.shape
    n = dm.shape[1]
    tr = 256

    def body(c_ref, dm_ref, o_ref):
        cv, dv = c_ref[...], dm_ref[...]
        acc = cv[:, 0:1] * dv[0:1, :]
        for b in range(1, nbatch):
            acc = acc + cv[:, b:b + 1] * dv[b:b + 1, :]
        o_ref[...] = acc

    return pl.pallas_call(
        body, name=name, grid=(D // tr,),
        in_specs=[pl.BlockSpec((tr, nbatch), lambda i: (i, 0)), pl.BlockSpec((nbatch, n), lambda i: (0, 0))],
        out_specs=pl.BlockSpec((tr, n), lambda i: (i, 0)), out_shape=jax.ShapeDtypeStruct((D, n), F32),
        compiler_params=_params("parallel"),
    )(c_act_t, dm)


def _to_bf16(w, name):
    R, Cn = w.shape
    tr = _row_tile(R)

    def body(w_ref, o_ref):
        o_ref[...] = w_ref[...].astype(BF16)

    blk = pl.BlockSpec((tr, Cn), lambda i: (i, 0))
    return pl.pallas_call(
        body, name=name, grid=(R // tr,), in_specs=[blk], out_specs=blk, out_shape=jax.ShapeDtypeStruct((R, Cn), BF16),
        compiler_params=_params("parallel"),
    )(w)


def _adamw(w, g, m, v, name):
    R, Cn = w.shape
    tr = R
    for cand in (256, 128, 64, 32, 16, 8):
        if R % cand == 0:
            tr = cand
            break

    def body(w_ref, g_ref, m_ref, v_ref, d_ref, nm_ref, nv_ref):
        gv = g_ref[...]
        m_new = ADAM_B1 * m_ref[...] + (1.0 - ADAM_B1) * gv
        v_new = ADAM_B2 * v_ref[...] + (1.0 - ADAM_B2) * (gv * gv)
        m_hat = m_new / (1.0 - ADAM_B1 ** ADAM_STEP)
        v_hat = v_new / (1.0 - ADAM_B2 ** ADAM_STEP)
        d_ref[...] = -ADAM_LR * (m_hat / (jnp.sqrt(v_hat) + ADAM_EPS) + ADAM_WD * w_ref[...])
        nm_ref[...] = m_new
        nv_ref[...] = v_new

    blk = pl.BlockSpec((tr, Cn), lambda i: (i, 0))
    shp = jax.ShapeDtypeStruct((R, Cn), F32)
    return pl.pallas_call(
        body, name=name, grid=(R // tr,), in_specs=[blk] * 4, out_specs=[blk] * 3, out_shape=[shp] * 3,
        compiler_params=_params("parallel"),
    )(w, g, m, v)


def _place():
    return lax.axis_index("x"), lax.axis_index("y"), lax.axis_index("c")


def _flip(place, k):
    x, y, c = place
    return (1 - x if k & 4 else x, 1 - y if k & 2 else y, 1 - c if k & 1 else c)


def _dev_index(place):
    x, y, c = place
    return 4 * x + 2 * y + c


def _chip_index(place):
    return 2 * place[0] + place[1]


def _gather8(x_ref, out_ref, send_sems, recv_sems, local_sem):
    me = _place()
    mine = pltpu.make_async_copy(x_ref, out_ref.at[_dev_index(me)], local_sem)
    mine.start()

    def copy(k, origin, to):
        return pltpu.make_async_remote_copy(
            src_ref=x_ref, dst_ref=out_ref.at[_dev_index(origin)], send_sem=send_sems.at[k - 1],
            recv_sem=recv_sems.at[k - 1], device_id=to, device_id_type=MESH)

    sends = [copy(k, me, _flip(me, k)) for k in range(1, 8)]
    for cp in sends:
        cp.start()
    for k in range(1, 8):
        copy(k, _flip(me, k), me).wait_recv()
    for cp in sends:
        cp.wait_send()
    mine.wait()


def _allgather8(x, name, reduce=False):
    R, Cn = x.shape

    def body(x_ref, *rest):
        if reduce:
            out_ref, sum_ref, send_sems, recv_sems, local_sem = rest
        else:
            out_ref, send_sems, recv_sems, local_sem = rest
        _gather8(x_ref, out_ref, send_sems, recv_sems, local_sem)
        if reduce:
            acc = out_ref[0]
            for i in range(1, 8):
                acc = acc + out_ref[i]
            sum_ref[...] = acc

    vm = pl.BlockSpec(memory_space=pltpu.VMEM)
    outs = [jax.ShapeDtypeStruct((8, R, Cn), F32)] + ([jax.ShapeDtypeStruct((R, Cn), F32)] if reduce else [])
    res = pl.pallas_call(
        body, name=name, in_specs=[vm], out_specs=[vm] * len(outs), out_shape=outs,
        scratch_shapes=[pltpu.SemaphoreType.DMA((7,)), pltpu.SemaphoreType.DMA((7,)), pltpu.SemaphoreType.DMA],
    )(x)
    return res if reduce else res[0]


def _prologue(small, w_ada, b_ada, w_shard, name):
    R, Cn = small.shape
    n_mod = w_ada.shape[1]
    big = _gather_in_one([w_shard])

    def body(small_ref, wada_ref, b_ref, shard_ref, small_all_ref, mods_all_ref, gathered_ref, mods_ref,
             send1, recv1, send2, recv2, local_sems, big_send, big_recv):
        big.start([shard_ref], [gathered_ref], big_send, big_recv)
        _gather8(small_ref, small_all_ref, send1, recv1, local_sems.at[0])
        c_all = jnp.concatenate([small_all_ref[d, 0:1, :] for d in range(8)], axis=0)
        c_act = c_all * _sigmoid(c_all)
        mods_ref[...] = _nn(c_act, wada_ref[...], precision=lax.Precision.HIGHEST) + b_ref[...]
        _gather8(mods_ref, mods_all_ref, send2, recv2, local_sems.at[1])
        big.switch([shard_ref], [gathered_ref], big_send, big_recv)
        big.finish([shard_ref], [gathered_ref], big_send, big_recv)

    vm = pl.BlockSpec(memory_space=pltpu.VMEM)
    seven = pltpu.SemaphoreType.DMA((7,))
    return pl.pallas_call(
        body, name=name, in_specs=[vm, vm, vm, ANY], out_specs=[vm, vm, ANY],
        out_shape=[jax.ShapeDtypeStruct((8, R, Cn), F32), jax.ShapeDtypeStruct((8, 8, n_mod), F32)] + big.out_shapes,
        scratch_shapes=[pltpu.VMEM((8, n_mod), F32), seven, seven, seven, seven, pltpu.SemaphoreType.DMA((2,))]
        + big.sem_shapes(),
        compiler_params=pltpu.CompilerParams(vmem_limit_bytes=VMEM_LIMIT_BYTES),
    )(small, w_ada, b_ada, w_shard)


def _symmetric_plan(copies):
    def plan(in_refs, out_refs):
        sends = [(src, dst, i, to) for i, (src, dst, to) in enumerate(copies(in_refs, out_refs))]
        return sends, [(dst, i) for _, dst, i, _ in sends]
    return plan


def _halves_exchange(grads):
    def copies(in_refs, out_refs):
        me = _place()
        return [(g.at[kk, 1 - me[2]], got.at[kk], _flip(me, 1)) for g, got in zip(in_refs, out_refs) for kk in range(N_CHIPS)]

    return _Exchange(grads, [jax.ShapeDtypeStruct((N_CHIPS,) + g.shape[2:], g.dtype) for g in grads],
                     N_CHIPS * len(grads), _symmetric_plan(copies))


def _chips_exchange(parts):
    def copies(in_refs, out_refs):
        me = _place()
        return [(p.at[_chip_index(_flip(me, 2 * j))], got.at[j - 1], _flip(me, 2 * j))
                for p, got in zip(in_refs, out_refs) for j in (1, 2, 3)]

    return _Exchange(parts, [jax.ShapeDtypeStruct((3,) + p.shape[1:], p.dtype) for p in parts], 3 * len(parts),
                     _symmetric_plan(copies))


def _siblings_exchange(halves):
    def copies(in_refs, out_refs):
        sibling = _flip(_place(), 1)
        return [(h, got, sibling) for h, got in zip(in_refs, out_refs)]

    return _Exchange(halves, [jax.ShapeDtypeStruct(h.shape, h.dtype) for h in halves], len(halves), _symmetric_plan(copies))


def _ici_gather_plan(n, base=0):
    def plan(in_refs, out_refs):
        me = _place()
        c = me[2]
        sends, lands = [], []
        for a, (w, out) in enumerate(zip(in_refs[:n], out_refs)):
            for j in (1, 2, 3):
                i = base + 3 * a + j - 1
                sends.append((w.at[c], out.at[_chip_index(me), c], i, _flip(me, 2 * j)))
                lands.append((out.at[_chip_index(_flip(me, 2 * j)), c], i))
        return sends, lands
    return plan


def _d2d_gather_plan(n, base=0):
    def plan(in_refs, out_refs):
        me = _place()
        c = me[2]
        sibling = _flip(me, 1)
        mine = _chip_index(me)
        sends, lands = [], []
        for a, (w, out) in enumerate(zip(in_refs[:n], out_refs)):
            moves = [(w.at[c], (mine, c)), (w.at[1 - c], (mine, 1 - c))]
            moves += [(out.at[_chip_index(_flip(me, 2 * j)), c], (_chip_index(_flip(me, 2 * j)), c)) for j in (1, 2, 3)]
            for k, (src, (chip, half)) in enumerate(moves):
                sends.append((src, out.at[chip, half], base + 5 * a + k, sibling))
            blocks = [(mine, 1 - c), (mine, c)] + [(_chip_index(_flip(me, 2 * j)), 1 - c) for j in (1, 2, 3)]
            lands += [(out.at[chip, half], base + 5 * a + k) for k, (chip, half) in enumerate(blocks)]
        return sends, lands
    return plan


def _gathered_shapes(shards):
    return [jax.ShapeDtypeStruct((N_CHIPS,) + s.shape, s.dtype) for s in shards]


def _gather_over_ici(shards):
    return _Exchange(shards, _gathered_shapes(shards), 3 * len(shards), _ici_gather_plan(len(shards)))


def _gather_over_d2d(shards, gathered):
    n = len(shards)
    return _Exchange(list(shards) + list(gathered), [jax.ShapeDtypeStruct(g.shape, g.dtype) for g in gathered], 5 * n,
                     _d2d_gather_plan(n), aliases={n + a: a for a in range(n)})


def _gather_in_one(shards):
    n = len(shards)
    return _Exchange(shards, _gathered_shapes(shards), 8 * n, _ici_gather_plan(n), then=_d2d_gather_plan(n, base=3 * n))


def _row_tile(rows):
    for cand in (256, 176, 128, 64, 32, 16, 8):
        if rows % cand == 0:
            return cand
    return rows


def _pair_sum(core, grad, theirs, name):
    N, _, R, Cn = grad.shape
    tr = R

    def body(core_ref, g_ref, t_ref, o_ref, ob_ref):
        s = g_ref[...] + t_ref[...]
        o_ref[...] = s
        ob_ref[...] = s.astype(BF16)

    out = pl.BlockSpec((None, tr, Cn), lambda k, i, core_ref: (k, i, 0))
    return pl.pallas_call(
        body, name=name,
        grid_spec=pltpu.PrefetchScalarGridSpec(
            num_scalar_prefetch=1, grid=(N, R // tr),
            in_specs=[pl.BlockSpec((None, None, tr, Cn), lambda k, i, core_ref: (k, core_ref[0], i, 0)),
                      pl.BlockSpec((None, tr, Cn), lambda k, i, core_ref: (k, i, 0))],
            out_specs=[out, out]),
        out_shape=[jax.ShapeDtypeStruct((N, R, Cn), F32), jax.ShapeDtypeStruct((N, R, Cn), BF16)],
        compiler_params=_params("parallel", "parallel"),
    )(core, grad, theirs)


def _chip_sum(chip, parts, landed, name):
    _, R, Cn = parts.shape
    tr = R

    def body(chip_ref, p_ref, l_ref, o_ref):
        o_ref[...] = ((p_ref[...] + l_ref[0].astype(F32)) + l_ref[1].astype(F32)) + l_ref[2].astype(F32)

    return pl.pallas_call(
        body, name=name,
        grid_spec=pltpu.PrefetchScalarGridSpec(
            num_scalar_prefetch=1, grid=(R // tr,),
            in_specs=[pl.BlockSpec((None, tr, Cn), lambda i, chip_ref: (chip_ref[0], i, 0)),
                      pl.BlockSpec((3, tr, Cn), lambda i, chip_ref: (0, i, 0))],
            out_specs=pl.BlockSpec((tr, Cn), lambda i, chip_ref: (i, 0))),
        out_shape=jax.ShapeDtypeStruct((R, Cn), F32), compiler_params=_params("parallel"),
    )(chip, parts, landed)


def _pair_sums(core, grads, theirs, tag):
    return [_pair_sum(core, g, t, f"{tag}_pair_sum_{i}") for i, (g, t) in enumerate(zip(grads, theirs))]


def _chip_sums(chip, parts, landed, tag):
    return [_chip_sum(chip, p[0], l, f"{tag}_chip_sum_{i}") for i, (p, l) in enumerate(zip(parts, landed))]


def _by_chip_rows(g):
    return g.reshape(N_CHIPS, 2, g.shape[0] // (2 * N_CHIPS), g.shape[1])


def _by_chip_cols(g):
    return g.reshape(N_CHIPS, 2, g.shape[1] // 2, g.shape[2])


def _adamw_halves(core, w, g_mine, g_theirs, m, v, name):
    R2, Cn = w.shape
    r = R2 // 2
    tr = _row_tile(r)
    nt = r // tr

    def body(core_ref, w_ref, gm_ref, gt_ref, m_ref, v_ref, g_ref, d_ref, nm_ref, nv_ref):
        gv = jnp.where(pl.program_id(0) == core_ref[0], gm_ref[...], gt_ref[...])
        g_ref[...] = gv
        m_new = ADAM_B1 * m_ref[...] + (1.0 - ADAM_B1) * gv
        v_new = ADAM_B2 * v_ref[...] + (1.0 - ADAM_B2) * (gv * gv)
        m_hat = m_new / (1.0 - ADAM_B1 ** ADAM_STEP)
        v_hat = v_new / (1.0 - ADAM_B2 ** ADAM_STEP)
        d_ref[...] = -ADAM_LR * (m_hat / (jnp.sqrt(v_hat) + ADAM_EPS) + ADAM_WD * w_ref[...])
        nm_ref[...] = m_new
        nv_ref[...] = v_new

    full = pl.BlockSpec((tr, Cn), lambda hf, i, core_ref: (hf * nt + i, 0))
    half = pl.BlockSpec((tr, Cn), lambda hf, i, core_ref: (i, 0))
    shp = jax.ShapeDtypeStruct((R2, Cn), F32)
    return pl.pallas_call(
        body, name=name,
        grid_spec=pltpu.PrefetchScalarGridSpec(
            num_scalar_prefetch=1, grid=(2, nt), in_specs=[full, half, half, full, full], out_specs=[full] * 4),
        out_shape=[shp] * 4, compiler_params=_params("parallel", "parallel"),
    )(core, w, g_mine, g_theirs, m, v)


def _pad_row(v, width):
    v = v.reshape(1, -1)
    return jnp.pad(v, ((0, 0), (0, width - v.shape[1])))


def _ffn1_forward(x, ng, shift, scale, gate, w_in4, w_out_shard, gather, next_norm):
    h = _rmsmod_fwd(x, ng, shift, scale, "ffn1_norm")
    (zg, zu, a), (partly, (w_out4,)) = _ffn_in_fwd(
        h, w_in4, "ffn1_in", exchange=[_gather_over_ici(gather), _gather_in_one([w_out_shard])])
    w_out = w_out4.reshape(D_FF, D_MODEL)
    (x_new, f, h_next), gathered = _proj_out_fwd([a], w_out, x, gate, 0.5, "ffn1_out", next_norm=next_norm,
                                                 exchange=_gather_over_d2d(gather, partly))
    return x_new, (h, zg, zu, a, f), w_out, gathered, h_next


def _ffn_backward(df, saved, w_in4, w_out, core, chip, tag, riding=None, norm=None, in_first=False):
    h, zg, zu, a = saved[:4]
    rode = None
    if riding:
        (dzg, dzu), rode = _dact_bwd(df, w_out, zg, zu, f"{tag}_dact", exchange=riding)
    else:
        dzg, dzu = _dact_bwd(df, w_out, zg, zu, f"{tag}_dact")

    def dw_out(exchange=None):
        outs = _wgrad(a, [df], df.shape[1], f"{tag}_dw_out", exchange=exchange)
        (dw,), landed = outs if exchange else (outs, None)
        return [_by_chip_rows(dw.reshape(a.shape[1], df.shape[1]))], landed

    def dw_in(exchange=None):
        outs = _wgrad(h, [dzg, dzu], FF_SHARD, f"{tag}_dw_in", exchange=exchange)
        (dw,), landed = outs if exchange else (outs, None)
        return [_by_chip_cols(dw.reshape(N_CHIPS, h.shape[1], FF_SHARD))], landed

    (first, tag_1), (second, tag_2) = ((dw_in, "in"), (dw_out, "out"))[::1 if in_first else -1]
    g_1, _ = first()
    g_2, theirs_1 = second(_halves_exchange(g_1))
    parts_1 = _pair_sums(core, g_1, theirs_1, f"{tag}_{tag_1}")
    dh_outs, (theirs_2, landed_1) = _ffn_in_dgrad(
        dzg, dzu, w_in4, f"{tag}_dh", norm=norm, exchange=[_halves_exchange(g_2), _chips_exchange([parts_1[0][1]])])
    parts_2 = _pair_sums(core, g_2, theirs_2, f"{tag}_{tag_2}")
    return dh_outs, parts_2, _chip_sums(chip, parts_1, landed_1, f"{tag}_{tag_1}"), rode


def kernel(x, c, w_ada, b_ada, norm_g, w_ffn1_in, w_ffn1_out, w_ffn2_in, w_ffn2_out, w_mix_in, w_mix_out, hgrn_lb, hgrn_norm_g, qk_norm_g, attn_sink, rel_bias, loss_target, m_w_ada, m_b_ada, m_norm_g, m_w_ffn1_in, m_w_ffn1_out, m_w_ffn2_in, m_w_ffn2_out, m_w_mix_in, m_w_mix_out, m_hgrn_lb, m_hgrn_norm_g, m_qk_norm_g, m_attn_sink, m_rel_bias, v_w_ada, v_b_ada, v_norm_g, v_w_ffn1_in, v_w_ffn1_out, v_w_ffn2_in, v_w_ffn2_out, v_w_mix_in, v_w_mix_out, v_hgrn_lb, v_hgrn_norm_g, v_qk_norm_g, v_attn_sink, v_rel_bias):
    D = D_MODEL
    S = x.shape[1]
    place = (lax.axis_index("x"), lax.axis_index("y"), lax.axis_index("c"))
    me, my_chip = _dev_index(place), _chip_index(place)
    x0 = x[0]
    target = loss_target[0]

    def halves(w, tag):
        return _to_bf16(w[0], f"{tag}_to_bf16").reshape(2, w.shape[1] // 2, w.shape[2])

    w1_out_shard = halves(w_ffn1_out, "w_ffn1_out")
    mix_shards = [halves(w_mix_in, "w_mix_in"), halves(w_mix_out, "w_mix_out")]
    ffn2_shards = [halves(w_ffn2_in, "w_ffn2_in"), halves(w_ffn2_out, "w_ffn2_out")]
    core_arr = jnp.reshape(place[2], (1,)).astype(jnp.int32)
    chip_arr = jnp.reshape(my_chip, (1,)).astype(jnp.int32)

    small = jnp.concatenate([_pad_row(c, D), _pad_row(norm_g, D), _pad_row(hgrn_lb, D), jnp.zeros((5, D), F32)], axis=0)
    n_ada = w_ada.shape[2]
    b_mine = lax.dynamic_slice_in_dim(b_ada, my_chip * n_ada, n_ada, axis=1)
    small_all, mods_parts, w1_in4 = _prologue(small, w_ada[0], b_mine, halves(w_ffn1_in, "w_ffn1_in"), "prologue")
    w1_in = w1_in4.reshape(N_CHIPS, D, FF_SHARD)
    c_all = small_all[:, 0, :]
    by_chip = small_all[0::2]
    norm_g_full = by_chip[:, 1, :3 * 256].reshape(N_CHIPS, 3, 256).transpose(1, 0, 2).reshape(3, D)
    lb_raw = by_chip[:, 2, :2 * 2 * 128].reshape(N_CHIPS, 2, 2, 128).transpose(1, 2, 0, 3).reshape(2, 2, HG_WIDTH)
    lb_logit = lb_raw[:, 0, :] - lb_raw[:, 1, :]
    lb = jax.nn.sigmoid(lb_logit)
    one_minus_lb = jax.nn.sigmoid(-lb_logit)
    lb_f = jnp.stack([lb[0], one_minus_lb[0]])
    lb_b = jnp.stack([lb[1], one_minus_lb[1]])

    c_act_all = c_all * jax.nn.sigmoid(c_all)
    mods_all = mods_parts[0::2].transpose(1, 0, 2).reshape(8, N_MOD * D)
    mods = lax.dynamic_slice_in_dim(mods_all, me, 1, axis=0)
    sh1, sc1, g1, sh2, sc2, g2, sh3, sc3, g3 = [mods[:, i * D:(i + 1) * D] for i in range(N_MOD)]

    x1, saved1, w1_out, gathered, h2 = _ffn1_forward(x0, norm_g_full[0:1], sh1, sc1, g1, w1_in, w1_out_shard, mix_shards,
                                                     (norm_g_full[1:2], sh2, sc2))
    wm_in = gathered[0].reshape(N_CHIPS, D, D_IN // N_CHIPS).transpose(1, 0, 2).reshape(D, D_IN)
    wm_out = gathered[1].reshape(D, D)

    z = _matmul_nn(h2, wm_in, F32, 256, "mix_in")
    (of, st_f, ob, st_b), gathered = _hgrn_fwd_both(z, (lb_f, lb_b), "hgrn_fwd", exchange=_gather_in_one(ffn2_shards))
    w2_in = gathered[0].reshape(N_CHIPS, D, FF_SHARD)
    w2_out = gathered[1].reshape(D_FF, D)
    o_h = _hgrn_post_fwd(of, ob, z, hgrn_norm_g, "hgrn_post")

    q_g, k_g = qk_norm_g[0, 0:1], qk_norm_g[0, 1:2]
    sink_b = jnp.broadcast_to(attn_sink.reshape(ATT_Q_HEADS, 1, 1), (ATT_Q_HEADS, 1, BLOCK))
    bias = _bias_table(rel_bias, "bias_table")
    o_a = _attn_fwd(z, q_g, k_g, sink_b, bias, "attn_fwd")
    x2, mixed, h3 = _proj_out_fwd([o_h, o_a], wm_out, x1, g2, 1.0, "mix_out", next_norm=(norm_g_full[2:3], sh3, sc3))

    zg3, zu3, a3 = _ffn_in_fwd(h3, w2_in, "ffn2_in")
    dx3, df3, dg3, sq_cols = _proj_out_loss(a3, w2_out, x2, g3, 0.5, target, "ffn2_out_loss")
    loss_mine = 0.5 * jnp.sum(sq_cols) / D

    (dx2, dsh3, dsc3, dng3, dmixed, dg2), parts2, mine2_out, _ = _ffn_backward(
        df3, (h3, zg3, zu3, a3), w2_in, w2_out, core_arr, chip_arr, "ffn2",
        norm=_NormBwd(x2, norm_g_full[2:3], sc3, dx3, below=(mixed, g2, 1.0)))

    (do_cat,) = _matmul_nt([dmixed], wm_out, ROW_TILE, "mix_out_dgrad")
    dwm_out = _wgrad_rows([o_h, o_a], dmixed, "mix_out_dw").reshape(D, D)

    do_sum, dgr, d_hnorm = _hgrn_post_bwd(do_cat, of, ob, z, hgrn_norm_g, "hgrn_post_bwd")
    (dq_f, dff, dv_f, doml_f), landed2 = _hgrn_bwd(z, lb_f, do_sum, st_f, 0, "hgrn_bwd_f",
                                                   exchange=_chips_exchange([p[1] for p in parts2]))
    mine2 = _chip_sums(chip_arr, parts2, landed2, "ffn2_in") + mine2_out
    (dhq, dfb, dhi, doml_b), theirs2 = _hgrn_bwd(z, lb_b, do_sum, st_b, 1, "hgrn_bwd_b", acc=(dq_f, dv_f),
                                                 exchange=_siblings_exchange(mine2))

    daq, dkw, dvw, ds_sum, dsink, dqg = _attn_bwd(z, q_g, k_g, sink_b, bias, do_cat, "attn_bwd")
    dkv, dkg = _attn_kv_reduce(dkw, dvw, z, k_g, "attn_kv_reduce")
    d_rel_bias = jnp.sum(_bias_grad(ds_sum, "bias_grad"), axis=-1).T
    dz = [dhq, dff, dfb, dhi, dgr, daq, dkv]
    dwm_in = _wgrad_pieces(h2, dz, 2 * KV_WIDTH, "mix_in_dw").transpose(1, 0, 2).reshape(D, D_IN)
    wide = D_IN // N_CHIPS
    grads_m = [_by_chip_cols(dwm_in.reshape(D, N_CHIPS, wide).transpose(1, 0, 2)), _by_chip_rows(dwm_out)]
    (dx1, dsh2, dsc2, dng2, df1, dg1), theirs_m = _matmul_nt(
        dz, wm_in, 256, "mix_in_dgrad", exchange=_halves_exchange(grads_m),
        norm=_NormBwd(x1, norm_g_full[1:2], sc2, dx2, below=(saved1[4], g1, 0.5)))
    parts_m = _pair_sums(core_arr, grads_m, theirs_m, "mix")

    (dh1,), parts1, mine1_in, landed_m = _ffn_backward(df1, saved1, w1_in, w1_out, core_arr, chip_arr, "ffn1",
                                                       riding=_chips_exchange([p[1] for p in parts_m]), in_first=True)
    mine_m = _chip_sums(chip_arr, parts_m, landed_m, "mix")
    (dx0, dsh1, dsc1, dng1), landed1 = _rmsmod_bwd(dh1, _NormBwd(x0, norm_g_full[0:1], sc1, dx1), "ffn1_norm_bwd",
                                                   exchange=_chips_exchange([p[1] for p in parts1]))
    mine1 = mine1_in + _chip_sums(chip_arr, parts1, landed1, "ffn1_out")
    theirs_1m = list(_run_exchange(_siblings_exchange(mine1 + mine_m), "siblings_exchange"))
    reduced = list(zip(mine1 + mine2 + mine_m, theirs_1m[:2] + list(theirs2) + theirs_1m[2:]))

    dlb = -jnp.concatenate([doml_f, doml_b], axis=0)
    dlb_raw = dlb * lb * one_minus_lb
    d_hgrn_lb = jnp.stack([dlb_raw, -dlb_raw], axis=1)
    d_qk = jnp.concatenate([jnp.sum(dqg, axis=0), jnp.sum(dkg, axis=0)], axis=0)
    dmods = jnp.concatenate([dsh1, dsc1, dg1, dsh2, dsc2, dg2, dsh3, dsc3, dg3], axis=0)
    packed = jnp.concatenate(
        [dmods, dng1, dng2, dng3, d_hgrn_lb.reshape(2, D), _pad_row(d_hnorm, D), _pad_row(d_qk, D),
         _pad_row(dsink[:, 0, 0], D), _pad_row(d_rel_bias, D), _pad_row(loss_mine, D)], axis=0)
    packed = jnp.pad(packed, ((0, 24 - packed.shape[0]), (0, 0)))
    packed_all, packed_sum = _allgather8(packed, "small_grads_allgather", reduce=True)
    dmods_all = packed_all[:, 0:N_MOD, :].reshape(8, N_MOD * D)
    g_b_ada = packed_sum[0:N_MOD].reshape(1, N_MOD * D)
    g_norm_full = packed_sum[9:12]
    g_norm_g = lax.dynamic_slice_in_dim(g_norm_full, my_chip * 256, 256, axis=1).reshape(1, 3, 256)
    g_hgrn_lb = lax.dynamic_slice_in_dim(packed_sum[12:14].reshape(2, 2, HG_WIDTH), my_chip * 128, 128, axis=2)
    g_hgrn_norm_g = packed_sum[14:15, :HG_WIDTH]
    g_qk_norm_g = packed_sum[15, :2 * ATT_HEAD_DIM].reshape(1, 2, ATT_HEAD_DIM)
    g_attn_sink = packed_sum[16:17, :ATT_Q_HEADS]
    g_rel_bias = packed_sum[17, :NUM_BUCKETS * ATT_Q_HEADS].reshape(NUM_BUCKETS, ATT_Q_HEADS)
    loss = packed_sum[18, 0]

    dm_mine = lax.dynamic_slice_in_dim(dmods_all, my_chip * n_ada, n_ada, axis=1)
    g_w_ada = _ada_wgrad(c_act_all.T, dm_mine, "ada_wgrad")[None]

    def big(w, g, m, v, name):
        d, nm, nv = _adamw(w[0], g[0], m[0], v[0], name)
        return d[None], nm[None], nv[None]

    def big_halves(w, g_pair, m, v, name):
        g, d, nm, nv = _adamw_halves(core_arr, w[0], g_pair[0], g_pair[1], m[0], v[0], name)
        return g[None], (d[None], nm[None], nv[None])

    g_w1_in, u_w1_in = big_halves(w_ffn1_in, reduced[0], m_w_ffn1_in, v_w_ffn1_in, "adamw_w_ffn1_in")
    g_w1_out, u_w1_out = big_halves(w_ffn1_out, reduced[1], m_w_ffn1_out, v_w_ffn1_out, "adamw_w_ffn1_out")
    g_w2_in, u_w2_in = big_halves(w_ffn2_in, reduced[2], m_w_ffn2_in, v_w_ffn2_in, "adamw_w_ffn2_in")
    g_w2_out, u_w2_out = big_halves(w_ffn2_out, reduced[3], m_w_ffn2_out, v_w_ffn2_out, "adamw_w_ffn2_out")
    g_wm_in, u_wm_in = big_halves(w_mix_in, reduced[4], m_w_mix_in, v_w_mix_in, "adamw_w_mix_in")
    g_wm_out, u_wm_out = big_halves(w_mix_out, reduced[5], m_w_mix_out, v_w_mix_out, "adamw_w_mix_out")

    smalls = [(b_ada, g_b_ada, m_b_ada, v_b_ada), (norm_g, g_norm_g, m_norm_g, v_norm_g), (hgrn_lb, g_hgrn_lb, m_hgrn_lb, v_hgrn_lb),
              (hgrn_norm_g, g_hgrn_norm_g, m_hgrn_norm_g, v_hgrn_norm_g), (qk_norm_g, g_qk_norm_g, m_qk_norm_g, v_qk_norm_g),
              (attn_sink, g_attn_sink, m_attn_sink, v_attn_sink), (rel_bias, g_rel_bias, m_rel_bias, v_rel_bias)]
    sizes = [t[0].size for t in smalls]
    total = sum(sizes)
    rows = -(-total // 128)
    rows = -(-rows // 8) * 8

    def pack(i):
        flat = jnp.concatenate([t[i].reshape(-1) for t in smalls])
        fill = 1.0 if i == 3 else 0.0
        return jnp.pad(flat, (0, rows * 128 - total), constant_values=fill).reshape(rows, 128)

    packed_out = _adamw(pack(0), pack(1), pack(2), pack(3), "adamw_small")

    def unpack(flat2d):
        flat = flat2d.reshape(-1)
        outs, off = [], 0
        for t, n in zip(smalls, sizes):
            outs.append(flat[off:off + n].reshape(t[0].shape))
            off += n
        return outs

    d_small, m_small, v_small = [unpack(t) for t in packed_out]

    upd = {
        "w_ada": big(w_ada, g_w_ada, m_w_ada, v_w_ada, "adamw_w_ada"),
        "w_ffn1_in": u_w1_in, "w_ffn1_out": u_w1_out, "w_ffn2_in": u_w2_in, "w_ffn2_out": u_w2_out,
        "w_mix_in": u_wm_in, "w_mix_out": u_wm_out,
    }
    small_names = ["b_ada", "norm_g", "hgrn_lb", "hgrn_norm_g", "qk_norm_g", "attn_sink", "rel_bias"]
    for i, nme in enumerate(small_names):
        upd[nme] = (d_small[i], m_small[i], v_small[i])
    grads = {
        "w_ada": g_w_ada, "b_ada": g_b_ada, "norm_g": g_norm_g, "w_ffn1_in": g_w1_in, "w_ffn1_out": g_w1_out,
        "w_ffn2_in": g_w2_in, "w_ffn2_out": g_w2_out, "w_mix_in": g_wm_in, "w_mix_out": g_wm_out, "hgrn_lb": g_hgrn_lb,
        "hgrn_norm_g": g_hgrn_norm_g, "qk_norm_g": g_qk_norm_g, "attn_sink": g_attn_sink, "rel_bias": g_rel_bias,
    }
    order = ["w_ada", "b_ada", "norm_g", "w_ffn1_in", "w_ffn1_out", "w_ffn2_in", "w_ffn2_out", "w_mix_in", "w_mix_out",
             "hgrn_lb", "hgrn_norm_g", "qk_norm_g", "attn_sink", "rel_bias"]
    return (loss, dx0[None], *[grads[k] for k in order], *[upd[k][0] for k in order], *[upd[k][1] for k in order],
            *[upd[k][2] for k in order])
```

```python
import functools
import math

import numpy as np
import jax
import jax.numpy as jnp
from jax import lax
from jax.experimental import pallas as pl
from jax.experimental.pallas import tpu as pltpu

F32, BF16 = jnp.float32, jnp.bfloat16

D_MODEL = 1024
D_FF = 2816
HG_HEADS, HG_DIM = 4, 128
HG_WIDTH = HG_HEADS * HG_DIM
ATT_Q_HEADS, ATT_KV_HEADS, ATT_HEAD_DIM = 8, 2, 64
ATT_GROUP = ATT_Q_HEADS // ATT_KV_HEADS
ATT_WIDTH = ATT_Q_HEADS * ATT_HEAD_DIM
KV_WIDTH = ATT_KV_HEADS * ATT_HEAD_DIM
WINDOW, BLOCK = 128, 128
NUM_BUCKETS, MAX_DISTANCE = 32, 128
N_MOD = 9
EPS = 1e-6
D_IN = 5 * HG_WIDTH + ATT_WIDTH + 2 * KV_WIDTH
ADAM_LR, ADAM_B1, ADAM_B2, ADAM_EPS, ADAM_WD, ADAM_STEP = 0.001, 0.9, 0.999, 1e-08, 0.01, 10

N_CHIPS = 4
FF_SHARD = 2 * D_FF // N_CHIPS
NEG = -1e30

VMEM_LIMIT_BYTES = 56 << 20
ROW_TILE = 512
HG_CHUNK = 16
HG_ROWS = 512

MESH = pl.DeviceIdType.MESH
ANY = pl.BlockSpec(memory_space=pl.ANY)


def _params(*sem):
    return pltpu.CompilerParams(dimension_semantics=sem, vmem_limit_bytes=VMEM_LIMIT_BYTES)


def _resident(shape, index_map):
    return pl.BlockSpec(shape, index_map, pipeline_mode=pl.Buffered(1))


def _dot(a, b, dims, precision=None):
    return lax.dot_general(a, b, (dims, ((), ())), precision=precision, preferred_element_type=F32)


def _nn(a, b, precision=None):
    return _dot(a, b, ((1,), (0,)), precision)


def _nt(a, b):
    return _dot(a, b, ((1,), (1,)))


def _tn(a, b):
    return _dot(a, b, ((0,), (0,)))


def _sigmoid(x):
    return jax.nn.sigmoid(x)


class _Exchange:
    def __init__(self, inputs, out_shapes, n_sems, plan, aliases=None, then=None):
        self.inputs, self.out_shapes, self.n_sems, self.plan, self.aliases = list(inputs), list(out_shapes), n_sems, plan, aliases or {}
        self.then = then

    def sem_shapes(self):
        return [pltpu.SemaphoreType.DMA((self.n_sems,)), pltpu.SemaphoreType.DMA((self.n_sems,))]

    @staticmethod
    def _copy(src, dst, i, to, send_sems, recv_sems):
        return pltpu.make_async_remote_copy(
            src_ref=src, dst_ref=dst, send_sem=send_sems.at[i], recv_sem=recv_sems.at[i], device_id=to, device_id_type=MESH)

    def _start(self, plan, in_refs, out_refs, send_sems, recv_sems):
        for src, dst, i, to in plan(in_refs, out_refs)[0]:
            self._copy(src, dst, i, to, send_sems, recv_sems).start()

    def _wait(self, plan, in_refs, out_refs, send_sems, recv_sems):
        sends, lands = plan(in_refs, out_refs)
        for zone, i in lands:
            self._copy(zone, zone, i, _place(), send_sems, recv_sems).wait_recv()
        for src, dst, i, to in sends:
            self._copy(src, dst, i, to, send_sems, recv_sems).wait_send()

    def start(self, *refs):
        self._start(self.plan, *refs)

    def switch(self, *refs):
        if self.then:
            self._wait(self.plan, *refs)
            self._start(self.then, *refs)

    def finish(self, *refs):
        self._wait(self.then or self.plan, *refs)


def _run_exchange(ex, name):
    n_in, n_out = len(ex.inputs), len(ex.out_shapes)

    def body(*refs):
        in_refs, out_refs, (send_sems, recv_sems) = refs[:n_in], refs[n_in:n_in + n_out], refs[n_in + n_out:]
        ex.start(in_refs, out_refs, send_sems, recv_sems)
        ex.switch(in_refs, out_refs, send_sems, recv_sems)
        ex.finish(in_refs, out_refs, send_sems, recv_sems)

    return pl.pallas_call(
        body, name=name, in_specs=[ANY] * n_in, out_specs=[ANY] * n_out, out_shape=ex.out_shapes,
        scratch_shapes=ex.sem_shapes(), input_output_aliases=dict(ex.aliases),
    )(*ex.inputs)


def _call(body, *, name, grid, in_specs, out_specs, out_shape, args, semantics, scratch_shapes=(), exchange=None):
    if exchange is None:
        return pl.pallas_call(
            body, name=name, grid=grid, in_specs=in_specs, out_specs=out_specs, out_shape=out_shape,
            scratch_shapes=list(scratch_shapes), compiler_params=_params(*semantics))(*args)
    exs = exchange if isinstance(exchange, (list, tuple)) else [exchange]
    n_in, n_out, n_scr = len(in_specs), len(out_specs), len(scratch_shapes)
    x_in, x_out = [len(ex.inputs) for ex in exs], [len(ex.out_shapes) for ex in exs]

    def take(refs, counts):
        groups = []
        for n in counts:
            groups.append(refs[:n])
            refs = refs[n:]
        return groups, refs

    def carrier(*refs):
        ins, refs = refs[:n_in], refs[n_in:]
        x_ins, refs = take(refs, x_in)
        outs, refs = refs[:n_out], refs[n_out:]
        x_outs, refs = take(refs, x_out)
        scr, refs = refs[:n_scr], refs[n_scr:]
        sems, _ = take(refs, [2] * len(exs))
        ids = [pl.program_id(a) for a in range(len(grid))]
        first = functools.reduce(jnp.logical_and, [i == 0 for i in ids])
        last = functools.reduce(jnp.logical_and, [i == g - 1 for i, g in zip(ids, grid)])
        step = functools.reduce(lambda acc, ig: acc * ig[1] + ig[0], zip(ids, grid), 0)

        @pl.when(first)
        def _():
            for ex, xi, xo, (send_sems, recv_sems) in zip(exs, x_ins, x_outs, sems):
                ex.start(xi, xo, send_sems, recv_sems)

        if any(ex.then for ex in exs):
            @pl.when(step == (3 * math.prod(grid)) // 4)
            def _():
                for ex, xi, xo, (send_sems, recv_sems) in zip(exs, x_ins, x_outs, sems):
                    ex.switch(xi, xo, send_sems, recv_sems)

        body(*ins, *outs, *scr)

        @pl.when(last)
        def _():
            for ex, xi, xo, (send_sems, recv_sems) in zip(exs, x_ins, x_outs, sems):
                ex.finish(xi, xo, send_sems, recv_sems)

    aliases, i0, o0 = {}, n_in, n_out
    for ex in exs:
        aliases.update({i0 + i: o0 + o for i, o in ex.aliases.items()})
        i0, o0 = i0 + len(ex.inputs), o0 + len(ex.out_shapes)
    res = pl.pallas_call(
        carrier, name=name, grid=grid, in_specs=list(in_specs) + [ANY] * sum(x_in),
        out_specs=list(out_specs) + [ANY] * sum(x_out),
        out_shape=list(out_shape) + [s for ex in exs for s in ex.out_shapes],
        scratch_shapes=list(scratch_shapes) + [s for ex in exs for s in ex.sem_shapes()],
        input_output_aliases=aliases, compiler_params=_params(*["arbitrary"] * len(grid)),
    )(*args, *[a for ex in exs for a in ex.inputs])
    x_res, _ = take(list(res[n_out:]), x_out)
    return list(res[:n_out]), (x_res if isinstance(exchange, (list, tuple)) else x_res[0])


def _rmsmod_fwd(x, g, shift, scale, name, exchange=None):
    S, D = x.shape
    tr = min(ROW_TILE, S)

    def body(x_ref, g_ref, sh_ref, sc_ref, h_ref):
        xv = x_ref[...]
        rstd = lax.rsqrt(jnp.mean(xv * xv, axis=-1, keepdims=True) + EPS)
        y = xv * rstd * g_ref[...]
        h_ref[...] = (y * (1.0 + sc_ref[...]) + sh_ref[...]).astype(h_ref.dtype)

    row = pl.BlockSpec((tr, D), lambda i: (i, 0))
    vec = pl.BlockSpec((1, D), lambda i: (0, 0))
    return _call(body, name=name, grid=(S // tr,), in_specs=[row, vec, vec, vec], out_specs=[row],
                 out_shape=[jax.ShapeDtypeStruct((S, D), BF16)], args=(x, g, shift, scale), semantics=("parallel",),
                 exchange=exchange)


class _NormBwd:
    def __init__(self, x, g, scale, dx_res, below=None):
        S, D = x.shape
        self.below, self.coef = below, (below[2] if below else None)
        self.inputs = [x, g, scale, dx_res] + ([below[0], below[1]] if below else [])
        vshape = jax.ShapeDtypeStruct((1, D), F32)
        self.out_shape = [jax.ShapeDtypeStruct((S, D), F32), vshape, vshape, vshape]
        if below:
            self.out_shape += [jax.ShapeDtypeStruct((S, D), BF16), vshape]

    def specs(self, tr, D):
        row = pl.BlockSpec((tr, D), lambda i: (i, 0))
        vec = pl.BlockSpec((1, D), lambda i: (0, 0))
        return ([row, vec, vec, row] + ([row, vec] if self.below else []),
                [row, vec, vec, vec] + ([row, vec] if self.below else []))

    def step(self, dhv, in_refs, out_refs):
        if self.below:
            x_ref, g_ref, sc_ref, dxr_ref, f_ref, gate_ref = in_refs
            dx_ref, dsh_ref, dsc_ref, dg_ref, df_ref, dgate_ref = out_refs
            sums = (dsh_ref, dsc_ref, dg_ref, dgate_ref)
        else:
            x_ref, g_ref, sc_ref, dxr_ref = in_refs
            dx_ref, dsh_ref, dsc_ref, dg_ref = out_refs
            sums = (dsh_ref, dsc_ref, dg_ref)

        @pl.when(pl.program_id(0) == 0)
        def _():
            for ref in sums:
                ref[...] = jnp.zeros_like(ref)

        xv, gv = x_ref[...], g_ref[...]
        one_sc = 1.0 + sc_ref[...]
        rstd = lax.rsqrt(jnp.mean(xv * xv, axis=-1, keepdims=True) + EPS)
        n = xv * rstd
        dsh_ref[...] += jnp.sum(dhv, axis=0, keepdims=True)
        dsc_ref[...] += jnp.sum(dhv * n, axis=0, keepdims=True) * gv
        dg_ref[...] += jnp.sum(dhv * n, axis=0, keepdims=True) * one_sc
        dn = dhv * (gv * one_sc)
        dx = dxr_ref[...] + rstd * (dn - n * jnp.mean(dn * n, axis=-1, keepdims=True))
        dx_ref[...] = dx
        if self.below:
            df_ref[...] = (self.coef * gate_ref[...] * dx).astype(df_ref.dtype)
            dgate_ref[...] += self.coef * jnp.sum(dx * f_ref[...].astype(F32), axis=0, keepdims=True)


def _rmsmod_bwd(dh, norm, name, exchange=None):
    S, D = dh.shape
    tr = min(ROW_TILE, S)
    n_in = len(norm.inputs)

    def body(dh_ref, *refs):
        norm.step(dh_ref[...], refs[:n_in], refs[n_in:])

    in_specs, out_specs = norm.specs(tr, D)
    return _call(body, name=name, grid=(S // tr,), in_specs=[pl.BlockSpec((tr, D), lambda i: (i, 0))] + in_specs,
                 out_specs=out_specs, out_shape=norm.out_shape, args=[dh] + norm.inputs, semantics=("arbitrary",),
                 exchange=exchange)


def _ffn_in_fwd(h, w4, name, exchange=None):
    S, D = h.shape
    tm = min(2 * ROW_TILE, S)
    n = w4.shape[2]

    def body(h_ref, wg_ref, wu_ref, zg_ref, zu_ref, a_ref):
        hv = h_ref[...]
        zg = _nn(hv, wg_ref[...])
        zu = _nn(hv, wu_ref[...])
        zg_ref[...] = zg.astype(zg_ref.dtype)
        zu_ref[...] = zu.astype(zu_ref.dtype)
        a_ref[...] = (zg * _sigmoid(zg) * zu).astype(a_ref.dtype)

    out = pl.BlockSpec((tm, n), lambda j, m: (m, j))
    oshape = jax.ShapeDtypeStruct((S, 2 * n), BF16)
    return _call(
        body, name=name, grid=(2, S // tm),
        in_specs=[pl.BlockSpec((tm, D), lambda j, m: (m, 0)),
                  pl.BlockSpec((None, D, n), lambda j, m: (j, 0, 0)),
                  pl.BlockSpec((None, D, n), lambda j, m: (j + 2, 0, 0))],
        out_specs=[out, out, out], out_shape=[oshape, oshape, oshape], args=(h, w4, w4),
        semantics=("parallel", "parallel"), exchange=exchange)


def _proj_out_fwd(lhs, w, x, gate, coef, name, exchange=None, next_norm=None):
    S, D = x.shape
    tm = min(ROW_TILE, S)
    ks = [a.shape[1] for a in lhs]

    def body(*refs):
        lhs_refs, refs = refs[:len(lhs)], refs[len(lhs):]
        if next_norm:
            w_ref, x_ref, gate_ref, g_ref, sh_ref, sc_ref, xn_ref, f_ref, h_ref = refs
        else:
            w_ref, x_ref, gate_ref, xn_ref, f_ref = refs
        acc, off = None, 0
        for a_ref, k in zip(lhs_refs, ks):
            part = _nn(a_ref[...], w_ref[off:off + k, :])
            acc = part if acc is None else acc + part
            off += k
        f_ref[...] = acc.astype(f_ref.dtype)
        xn = x_ref[...] + coef * gate_ref[...] * acc
        xn_ref[...] = xn
        if next_norm:
            rstd = lax.rsqrt(jnp.mean(xn * xn, axis=-1, keepdims=True) + EPS)
            h_ref[...] = (xn * rstd * g_ref[...] * (1.0 + sc_ref[...]) + sh_ref[...]).astype(h_ref.dtype)

    row = pl.BlockSpec((tm, D), lambda m: (m, 0))
    vec = pl.BlockSpec((1, D), lambda m: (0, 0))
    extra = list(next_norm) if next_norm else []
    return _call(
        body, name=name, grid=(S // tm,),
        in_specs=[pl.BlockSpec((tm, k), lambda m: (m, 0)) for k in ks]
        + [_resident(w.shape, lambda m: (0, 0)), row, vec] + [vec] * len(extra),
        out_specs=[row, row] + ([row] if next_norm else []),
        out_shape=[jax.ShapeDtypeStruct((S, D), F32), jax.ShapeDtypeStruct((S, D), BF16)]
        + ([jax.ShapeDtypeStruct((S, D), BF16)] if next_norm else []),
        args=(*lhs, w, x, gate, *extra), semantics=("parallel",), exchange=exchange)


def _proj_out_loss(lhs, w, x, gate, coef, target, name):
    S, D = x.shape
    tm = min(ROW_TILE, S)

    def body(a_ref, w_ref, x_ref, gate_ref, t_ref, dy_ref, df_ref, dgate_ref, sq_ref):
        @pl.when(pl.program_id(0) == 0)
        def _():
            dgate_ref[...] = jnp.zeros_like(dgate_ref)
            sq_ref[...] = jnp.zeros_like(sq_ref)

        f = _nn(a_ref[...], w_ref[...])
        gate = coef * gate_ref[...]
        err = x_ref[...] + gate * f - t_ref[...]
        sq_ref[...] += jnp.sum(err * err, axis=0, keepdims=True)
        dy = err * (1.0 / D)
        dy_ref[...] = dy
        df_ref[...] = (gate * dy).astype(df_ref.dtype)
        dgate_ref[...] += coef * jnp.sum(dy * f, axis=0, keepdims=True)

    row = pl.BlockSpec((tm, D), lambda m: (m, 0))
    vec = pl.BlockSpec((1, D), lambda m: (0, 0))
    vshape = jax.ShapeDtypeStruct((1, D), F32)
    return pl.pallas_call(
        body, name=name, grid=(S // tm,),
        in_specs=[pl.BlockSpec((tm, lhs.shape[1]), lambda m: (m, 0)), _resident(w.shape, lambda m: (0, 0)), row, vec, row],
        out_specs=[row, row, vec, vec],
        out_shape=[jax.ShapeDtypeStruct((S, D), F32), jax.ShapeDtypeStruct((S, D), BF16), vshape, vshape],
        compiler_params=_params("arbitrary"),
    )(lhs, w, x, gate, target)


def _matmul_nn(a, w, out_dtype, tm, name):
    S, K = a.shape
    N = w.shape[1]
    tm = min(tm, S)

    def body(a_ref, w_ref, o_ref):
        o_ref[...] = _nn(a_ref[...], w_ref[...]).astype(o_ref.dtype)

    return pl.pallas_call(
        body, name=name, grid=(S // tm,),
        in_specs=[pl.BlockSpec((tm, K), lambda m: (m, 0)), _resident((K, N), lambda m: (0, 0))],
        out_specs=pl.BlockSpec((tm, N), lambda m: (m, 0)), out_shape=jax.ShapeDtypeStruct((S, N), out_dtype),
        compiler_params=_params("parallel"),
    )(a, w)


def _dact_bwd(df, w_out, zg, zu, name, exchange=None):
    S, D = df.shape
    tm = min(ROW_TILE, S)
    n = w_out.shape[0] // 2

    def body(df_ref, w_ref, zg_ref, zu_ref, dzg_ref, dzu_ref):
        da = _nt(df_ref[...], w_ref[...]).astype(BF16)
        zg_v, zu_v = zg_ref[...], zu_ref[...]
        s = _sigmoid(zg_v)
        dzu_ref[...] = da * zg_v * s
        dzg_ref[...] = da * zu_v * (s * (1.0 + zg_v * (1.0 - s)))

    blk = pl.BlockSpec((tm, n), lambda j, m: (m, j))
    oshape = jax.ShapeDtypeStruct((S, 2 * n), BF16)
    return _call(
        body, name=name, grid=(2, S // tm),
        in_specs=[pl.BlockSpec((tm, D), lambda j, m: (m, 0)), pl.BlockSpec((n, D), lambda j, m: (j, 0)), blk, blk],
        out_specs=[blk, blk], out_shape=[oshape, oshape], args=(df, w_out, zg, zu), semantics=("parallel", "parallel"),
        exchange=exchange)


def _ffn_in_dgrad(dzg, dzu, w4, name, exchange=None, norm=None):
    S = dzg.shape[0]
    D, n = w4.shape[1], w4.shape[2]
    tm = min(ROW_TILE, S)
    n_norm = len(norm.inputs) if norm else 0

    def body(dzg_ref, dzu_ref, w_ref, *refs):
        acc = _nt(dzg_ref[:, 0:n], w_ref[0])
        acc += _nt(dzg_ref[:, n:2 * n], w_ref[1])
        acc += _nt(dzu_ref[:, 0:n], w_ref[2])
        acc += _nt(dzu_ref[:, n:2 * n], w_ref[3])
        if norm:
            norm.step(acc, refs[:n_norm], refs[n_norm:])
        else:
            refs[0][...] = acc

    blk = pl.BlockSpec((tm, 2 * n), lambda m: (m, 0))
    in_specs, args = [blk, blk, _resident(w4.shape, lambda m: (0, 0, 0))], [dzg, dzu, w4]
    out_specs, out_shape = [pl.BlockSpec((tm, D), lambda m: (m, 0))], [jax.ShapeDtypeStruct((S, D), F32)]
    if norm:
        norm_in, out_specs = norm.specs(tm, D)
        in_specs, args, out_shape = in_specs + norm_in, args + norm.inputs, norm.out_shape
    return _call(body, name=name, grid=(S // tm,), in_specs=in_specs, out_specs=out_specs, out_shape=out_shape, args=args,
                 semantics=("arbitrary",) if norm else ("parallel",), exchange=exchange)


def _matmul_nt(pieces, w, tm, name, exchange=None, norm=None):
    S = pieces[0].shape[0]
    ks = [p.shape[1] for p in pieces]
    N = w.shape[0]
    tm = min(tm, S)
    n_norm = len(norm.inputs) if norm else 0

    def body(*refs):
        p_refs, w_ref, refs = refs[:len(ks)], refs[len(ks)], refs[len(ks) + 1:]
        acc, off = None, 0
        for p_ref, k in zip(p_refs, ks):
            part = _nt(p_ref[...], w_ref[:, off:off + k])
            acc = part if acc is None else acc + part
            off += k
        if norm:
            norm.step(acc, refs[:n_norm], refs[n_norm:])
        else:
            refs[0][...] = acc

    in_specs = [pl.BlockSpec((tm, k), lambda m: (m, 0)) for k in ks] + [_resident(w.shape, lambda m: (0, 0))]
    args = list(pieces) + [w]
    out_specs, out_shape = [pl.BlockSpec((tm, N), lambda m: (m, 0))], [jax.ShapeDtypeStruct((S, N), F32)]
    if norm:
        norm_in, out_specs = norm.specs(tm, N)
        in_specs, args, out_shape = in_specs + norm_in, args + norm.inputs, norm.out_shape
    return _call(body, name=name, grid=(S // tm,), in_specs=in_specs, out_specs=out_specs, out_shape=out_shape, args=args,
                 semantics=("arbitrary",) if norm else ("parallel",), exchange=exchange)


def _wgrad(a, gs, tn, name, exchange=None):
    S, Ka = a.shape
    N = gs[0].shape[1]
    ts = min(ROW_TILE * (2 if Ka <= D_MODEL else 1), S)

    def body(a_ref, *refs):
        g_refs, o_ref = refs[:-1], refs[-1]

        @pl.when(pl.program_id(1) == 0)
        def _():
            o_ref[...] = jnp.zeros_like(o_ref)

        a_t = a_ref[...].T
        for i, g_ref in enumerate(g_refs):
            o_ref[i] += _nn(a_t, g_ref[...])

    return _call(
        body, name=name, grid=(N // tn, S // ts),
        in_specs=[pl.BlockSpec((ts, Ka), lambda j, s: (s, 0))] + [pl.BlockSpec((ts, tn), lambda j, s: (s, j))] * len(gs),
        out_specs=[pl.BlockSpec((len(gs), None, Ka, tn), lambda j, s: (0, j, 0, 0))],
        out_shape=[jax.ShapeDtypeStruct((len(gs), N // tn, Ka, tn), F32)], args=(a, *gs),
        semantics=("parallel", "arbitrary"), exchange=exchange)


def _wgrad_pieces(a, pieces, tn, name):
    S, Ka = a.shape
    ts = min(ROW_TILE, S)
    blocks = [(i, j) for i, p in enumerate(pieces) for j in range(p.shape[1] // tn)]

    def body(a_ref, *refs):
        g_refs, o_ref = refs[:-1], refs[-1]

        @pl.when(pl.program_id(0) == 0)
        def _():
            o_ref[...] = jnp.zeros_like(o_ref)

        a_t = a_ref[...].T
        for b, g_ref in enumerate(g_refs):
            o_ref[b] += _nn(a_t, g_ref[...])

    return pl.pallas_call(
        body, name=name, grid=(S // ts,),
        in_specs=[pl.BlockSpec((ts, Ka), lambda s: (s, 0))] + [pl.BlockSpec((ts, tn), lambda s, j=j: (s, j)) for _, j in blocks],
        out_specs=pl.BlockSpec((len(blocks), Ka, tn), lambda s: (0, 0, 0)),
        out_shape=jax.ShapeDtypeStruct((len(blocks), Ka, tn), F32), compiler_params=_params("arbitrary"),
    )(a, *[pieces[i] for i, _ in blocks])


def _wgrad_rows(lhs, g, name):
    S, Ka = lhs[0].shape
    N = g.shape[1]
    ts = min(ROW_TILE, S)

    def body(*refs):
        a_refs, g_ref, o_ref = refs[:-2], refs[-2], refs[-1]

        @pl.when(pl.program_id(0) == 0)
        def _():
            o_ref[...] = jnp.zeros_like(o_ref)

        gv = g_ref[...]
        for i, a_ref in enumerate(a_refs):
            o_ref[i] += _tn(a_ref[...], gv)

    return pl.pallas_call(
        body, name=name, grid=(S // ts,),
        in_specs=[pl.BlockSpec((ts, Ka), lambda s: (s, 0))] * len(lhs) + [pl.BlockSpec((ts, N), lambda s: (s, 0))],
        out_specs=pl.BlockSpec((len(lhs), Ka, N), lambda s: (0, 0, 0)),
        out_shape=jax.ShapeDtypeStruct((len(lhs), Ka, N), F32), compiler_params=_params("arbitrary"),
    )(*lhs, g)


def _hgrn_chunk_common(qr, fr, lb, oml, tri, last):
    sig_nf = _sigmoid(-fr)
    k = oml * sig_nf
    f_small = lb + oml * (jnp.exp(jnp.minimum(fr, 0.0)) * sig_nf)
    use_k = k < 0.5
    f = jnp.where(use_k, 1.0 - k, f_small)
    g = jnp.where(use_k, jnp.log1p(-k), jnp.log(f_small)) * math.log2(math.e)
    q = qr * _sigmoid(qr)
    G = _nn(tri, g, precision=lax.Precision.HIGHEST)
    Gl = G[last:last + 1]
    return q, k, f, G, Gl


def _hgrn_consts(reverse):
    C = HG_CHUNK
    r = lax.broadcasted_iota(jnp.int32, (C, C), 0)
    cc = lax.broadcasted_iota(jnp.int32, (C, C), 1)
    tri = ((cc >= r) if reverse else (cc <= r)).astype(F32)
    tri_t = ((cc <= r) if reverse else (cc >= r)).astype(F32)
    rid = lax.broadcasted_iota(jnp.int32, (C, HG_WIDTH), 0)
    return tri, tri_t, rid, (0 if reverse else C - 1)


def _head_slices():
    return [slice(h * HG_DIM, (h + 1) * HG_DIM) for h in range(HG_HEADS)]


def _per_head_lane_sum(x):
    C = x.shape[0]
    return jnp.concatenate(
        [jnp.broadcast_to(jnp.sum(x[:, sl], axis=-1, keepdims=True), (C, HG_DIM)) for sl in _head_slices()], axis=1)


HG_TILE = 8


def _pair_tiles(s, reverse):
    blk, r = divmod(s, HG_TILE)
    n_tiles = HG_CHUNK // HG_TILE
    others = range(0, blk) if reverse else range(blk + 1, n_tiles)
    return [(blk, r)] + [(t, None) for t in others]


def _pair_decay(G, s, tile, r, rid8, reverse, keys=False):
    rs = slice(tile * HG_TILE, (tile + 1) * HG_TILE)
    d = (G[s:s + 1] - G[rs]) if keys else (G[rs] - G[s:s + 1])
    if r is not None:
        d = jnp.where((rid8 <= r) if reverse else (rid8 >= r), d, NEG)
    return rs, jnp.exp2(d)


def _hgrn_fwd_both(z, lbs, name, exchange=None):
    S = z.shape[0]
    C, DK, W = HG_CHUNK, HG_DIM, HG_WIDTH
    tb = min(HG_ROWS, S)
    n_t, n_c = S // tb, tb // C
    dirs = (0, 1)

    def body(qf_ref, ff_ref, vf_ref, qb_ref, fb_ref, vb_ref, lbf_ref, lbb_ref, of_ref, stf_out, ob_ref, stb_out, st_ref):
        @pl.when(pl.program_id(0) == 0)
        def _():
            st_ref[...] = jnp.zeros_like(st_ref)

        q_refs, f_refs, v_refs, lb_refs = (qf_ref, qb_ref), (ff_ref, fb_ref), (vf_ref, vb_ref), (lbf_ref, lbb_ref)
        o_refs, st_outs = (of_ref, ob_ref), (stf_out, stb_out)
        consts = [_hgrn_consts(d == 1) for d in dirs]
        rid8 = lax.broadcasted_iota(jnp.int32, (HG_TILE, W), 0)

        def chunk(ci, carry):
            cidx = [ci, n_c - 1 - ci]
            rows = [pl.ds(pl.multiple_of(c * C, C), C) for c in cidx]
            v = [v_refs[d][rows[d], :] for d in dirs]
            com = [_hgrn_chunk_common(q_refs[d][rows[d], :], f_refs[d][rows[d], :], lb_refs[d][0:1, :], lb_refs[d][1:2, :],
                                      consts[d][0], consts[d][3]) for d in dirs]
            q, k, G, Gl = [c[0] for c in com], [c[1] for c in com], [c[3] for c in com], [c[4] for c in com]
            qd = [(q[d] * jnp.exp2(G[d])).astype(BF16) for d in dirs]
            kd = [(k[d] * jnp.exp2(Gl[d] - G[d])).astype(BF16) for d in dirs]
            e_gl = [jnp.exp2(Gl[d]) for d in dirs]
            v_b = [v[d].astype(BF16) for d in dirs]
            inter = [[], []]
            for h, sl in enumerate(_head_slices()):
                for d in dirs:
                    st0 = st_ref[d, h]
                    st_outs[d][h, cidx[d]] = st0
                    inter[d].append(_nt(qd[d][:, sl], st0.astype(BF16)))
                    st_ref[d, h] = st0 * e_gl[d][:, sl] + _tn(v_b[d][:, sl], kd[d][:, sl])
            o_t = [[jnp.concatenate(inter[d], axis=1)[t * HG_TILE:(t + 1) * HG_TILE] for t in range(C // HG_TILE)] for d in dirs]
            for s in range(C):
                for d in dirs:
                    k_s, v_s = k[d][s:s + 1], v[d][s:s + 1]
                    for tile, r in _pair_tiles(s, d == 1):
                        rs, e_s = _pair_decay(G[d], s, tile, r, rid8, d == 1)
                        o_t[d][tile] = o_t[d][tile] + _per_head_lane_sum(q[d][rs] * k_s * e_s) * v_s
            for d in dirs:
                o_refs[d][rows[d], :] = jnp.concatenate(o_t[d], axis=0)
            return carry

        lax.fori_loop(0, n_c, chunk, 0, unroll=4)

    def sec(j, back):
        return pl.BlockSpec((tb, W), (lambda i: (n_t - 1 - i, j)) if back else (lambda i: (i, j)))

    def st_spec(back):
        return pl.BlockSpec((HG_HEADS, n_c, DK, DK), (lambda i: (0, n_t - 1 - i, 0, 0)) if back else (lambda i: (0, i, 0, 0)))

    vec = pl.BlockSpec((2, W), lambda i: (0, 0))
    o_shape = jax.ShapeDtypeStruct((S, W), F32)
    st_shape = jax.ShapeDtypeStruct((HG_HEADS, S // C, DK, DK), F32)
    return _call(
        body, name=name, grid=(n_t,),
        in_specs=[sec(0, False), sec(1, False), sec(3, False), sec(0, True), sec(2, True), sec(3, True), vec, vec],
        out_specs=[sec(0, False), st_spec(False), sec(0, True), st_spec(True)],
        out_shape=[o_shape, st_shape, o_shape, st_shape],
        scratch_shapes=[pltpu.VMEM((2, HG_HEADS, DK, DK), F32)], args=(z, z, z, z, z, z, lbs[0], lbs[1]),
        semantics=("arbitrary",), exchange=exchange)


def _hgrn_bwd(z, lb, do, states, direction, name, acc=None, exchange=None):
    S = z.shape[0]
    C, DK, W = HG_CHUNK, HG_DIM, HG_WIDTH
    tb = min(HG_ROWS, S)
    n_t, n_c = S // tb, tb // C
    reverse = direction == 1
    tmap = (lambda i: i) if reverse else (lambda i: n_t - 1 - i)

    def body(*refs):
        if acc:
            q_ref, f_ref, v_ref, lb_ref, do_ref, st_in_ref, dqa_ref, dva_ref, dq_ref, df_ref, dv_ref, doml_ref, dst_ref = refs
        else:
            q_ref, f_ref, v_ref, lb_ref, do_ref, st_in_ref, dq_ref, df_ref, dv_ref, doml_ref, dst_ref = refs

        @pl.when(pl.program_id(0) == 0)
        def _():
            dst_ref[...] = jnp.zeros_like(dst_ref)
            doml_ref[...] = jnp.zeros_like(doml_ref)

        lbv, oml = lb_ref[0:1, :], lb_ref[1:2, :]
        tri, tri_t, rid, last = _hgrn_consts(reverse)
        rid8 = lax.broadcasted_iota(jnp.int32, (HG_TILE, W), 0)

        def chunk(ci, carry):
            cidx = ci if reverse else (n_c - 1 - ci)
            rows = pl.ds(pl.multiple_of(cidx * C, C), C)
            qr, fr, v, dov = q_ref[rows, :], f_ref[rows, :], v_ref[rows, :], do_ref[rows, :]
            q, k, f, G, Gl = _hgrn_chunk_common(qr, fr, lbv, oml, tri, last)
            e_g, e_gl, e_kd = jnp.exp2(G), jnp.exp2(Gl), jnp.exp2(Gl - G)
            qd, kd = q * e_g, k * e_kd
            do_b, v_b, qd_b, kd_b = dov.astype(BF16), v.astype(BF16), qd.astype(BF16), kd.astype(BF16)
            dqd, dkd, dv, state_dot = [], [], [], []
            for h, sl in enumerate(_head_slices()):
                st0, dst1 = st_in_ref[h, cidx], dst_ref[h]
                dst1_b = dst1.astype(BF16)
                dqd.append(_nn(do_b[:, sl], st0.astype(BF16)))
                dkd.append(_nn(v_b[:, sl], dst1_b))
                dv.append(_nt(kd_b[:, sl], dst1_b))
                state_dot.append(jnp.sum(st0 * dst1, axis=0, keepdims=True))
                dst_ref[h] = dst1 * e_gl[:, sl] + _tn(do_b[:, sl], qd_b[:, sl])
            dqd, dkd, dv = [jnp.concatenate(t, axis=1) for t in (dqd, dkd, dv)]
            d_gl = e_gl * jnp.concatenate(state_dot, axis=1) + jnp.sum(dkd * kd, axis=0, keepdims=True)
            dq, dk = dqd * e_g, dkd * e_kd
            n_tiles = C // HG_TILE
            dq_t, dk_t, dv_t = [[x[t * HG_TILE:(t + 1) * HG_TILE] for t in range(n_tiles)] for x in (dq, dk, dv)]
            for s in range(C):
                k_s, v_s = k[s:s + 1], v[s:s + 1]
                for tile, r in _pair_tiles(s, reverse):
                    rs, e_s = _pair_decay(G, s, tile, r, rid8, reverse)
                    dq_t[tile] = dq_t[tile] + _per_head_lane_sum(dov[rs] * v_s) * e_s * k_s
            for t in range(C):
                q_t, do_t = q[t:t + 1], dov[t:t + 1]
                for tile, r in _pair_tiles(t, not reverse):
                    rs, x_t = _pair_decay(G, t, tile, r, rid8, not reverse, keys=True)
                    qx = q_t * x_t
                    dv_t[tile] = dv_t[tile] + _per_head_lane_sum(k[rs] * qx) * do_t
                    dk_t[tile] = dk_t[tile] + _per_head_lane_sum(v[rs] * do_t) * qx
            dq, dk, dv = [jnp.concatenate(x, axis=0) for x in (dq_t, dk_t, dv_t)]
            d_big_g = dq * q - dk * k + jnp.where(rid == last, d_gl, 0.0)
            dg = _nn(tri_t, d_big_g, precision=lax.Precision.HIGHEST)
            dk_all = dk - dg / f
            sig_nf = _sigmoid(-fr)
            df_ref[rows, :] = (-dk_all * k * (1.0 - sig_nf)).astype(df_ref.dtype)
            doml_ref[...] += jnp.sum(dk_all * sig_nf, axis=0, keepdims=True)
            sq = _sigmoid(qr)
            dqr = dq * (sq * (1.0 + qr * (1.0 - sq)))
            if acc:
                dqr = dqr + dqa_ref[rows, :]
                dv = dv + dva_ref[rows, :]
            dq_ref[rows, :] = dqr.astype(dq_ref.dtype)
            dv_ref[rows, :] = dv.astype(dv_ref.dtype)
            return carry

        lax.fori_loop(0, n_c, chunk, 0, unroll=8)

    def sec(j):
        return pl.BlockSpec((tb, W), lambda i: (tmap(i), j))

    vec = pl.BlockSpec((1, W), lambda i: (0, 0))
    ins = [z, z, z, lb, do, states]
    in_specs = [sec(0), sec(1 + direction), sec(3), pl.BlockSpec((2, W), lambda i: (0, 0)), sec(0),
                pl.BlockSpec((HG_HEADS, n_c, DK, DK), lambda i: (0, tmap(i), 0, 0))]
    if acc:
        ins += list(acc)
        in_specs += [sec(0), sec(0)]
    final = jax.ShapeDtypeStruct((S, W), BF16)
    partial = final if acc else jax.ShapeDtypeStruct((S, W), F32)
    return _call(
        body, name=name, grid=(n_t,), in_specs=in_specs,
        out_specs=[sec(0), sec(0), sec(0), vec],
        out_shape=[partial, final, partial, jax.ShapeDtypeStruct((1, W), F32)],
        scratch_shapes=[pltpu.VMEM((HG_HEADS, DK, DK), F32)], args=ins, semantics=("arbitrary",), exchange=exchange)


def _hgrn_post_fwd(o_f, o_b, z, norm_g, name):
    S = z.shape[0]
    tr = min(ROW_TILE, S)

    def body(of_ref, ob_ref, gr_ref, ng_ref, y_ref):
        o = of_ref[...] + ob_ref[...]
        gr = gr_ref[...]
        gate = gr * _sigmoid(gr)
        ng = ng_ref[...]
        for h in range(HG_HEADS):
            sl = slice(h * HG_DIM, (h + 1) * HG_DIM)
            oh = o[:, sl]
            rstd = lax.rsqrt(jnp.mean(oh * oh, axis=-1, keepdims=True) + EPS)
            y_ref[:, sl] = (oh * rstd * ng[:, sl] * gate[:, sl]).astype(y_ref.dtype)

    row = pl.BlockSpec((tr, HG_WIDTH), lambda i: (i, 0))
    return pl.pallas_call(
        body, name=name, grid=(S // tr,),
        in_specs=[row, row, pl.BlockSpec((tr, HG_WIDTH), lambda i: (i, 4)), pl.BlockSpec((1, HG_WIDTH), lambda i: (0, 0))],
        out_specs=row, out_shape=jax.ShapeDtypeStruct((S, HG_WIDTH), BF16), compiler_params=_params("parallel"),
    )(o_f, o_b, z, norm_g)


def _hgrn_post_bwd(dy, o_f, o_b, z, norm_g, name):
    S = z.shape[0]
    tr = min(ROW_TILE, S)

    def body(dy_ref, of_ref, ob_ref, gr_ref, ng_ref, do_ref, dgr_ref, dng_ref):
        @pl.when(pl.program_id(0) == 0)
        def _():
            dng_ref[...] = jnp.zeros_like(dng_ref)

        o = of_ref[...] + ob_ref[...]
        gr, ng, dyv = gr_ref[...], ng_ref[...], dy_ref[...]
        sg = _sigmoid(gr)
        for h in range(HG_HEADS):
            sl = slice(h * HG_DIM, (h + 1) * HG_DIM)
            oh, dyh, grh, sgh, ngh = o[:, sl], dyv[:, sl], gr[:, sl], sg[:, sl], ng[:, sl]
            rstd = lax.rsqrt(jnp.mean(oh * oh, axis=-1, keepdims=True) + EPS)
            on = oh * rstd
            du = dyh * (grh * sgh)
            dgr_ref[:, sl] = (dyh * (on * ngh) * (sgh * (1.0 + grh * (1.0 - sgh)))).astype(dgr_ref.dtype)
            dng_ref[:, sl] += jnp.sum(du * on, axis=0, keepdims=True)
            don = du * ngh
            do_ref[:, sl] = rstd * (don - on * jnp.mean(don * on, axis=-1, keepdims=True))

    row = pl.BlockSpec((tr, HG_WIDTH), lambda i: (i, 0))
    vec = pl.BlockSpec((1, HG_WIDTH), lambda i: (0, 0))
    full = jax.ShapeDtypeStruct((S, HG_WIDTH), F32)
    return pl.pallas_call(
        body, name=name, grid=(S // tr,),
        in_specs=[row, row, row, pl.BlockSpec((tr, HG_WIDTH), lambda i: (i, 4)), vec],
        out_specs=[row, row, vec],
        out_shape=[full, jax.ShapeDtypeStruct((S, HG_WIDTH), BF16), jax.ShapeDtypeStruct((1, HG_WIDTH), F32)],
        compiler_params=_params("arbitrary"),
    )(dy, o_f, o_b, z, norm_g)


def _t5_bucket_table():
    rel = (np.arange(3 * BLOCK)[None, :] - BLOCK) - np.arange(BLOCK)[:, None]
    nb = NUM_BUCKETS // 2
    max_exact = nb // 2
    ret = (rel > 0).astype(np.int32) * nb
    n = np.abs(rel)
    ratio = np.log(np.maximum(n, 1).astype(np.float32) / np.float32(max_exact)) / np.float32(math.log(MAX_DISTANCE / max_exact))
    large = max_exact + (ratio.astype(np.float32) * np.float32(nb - max_exact)).astype(np.int32)
    large = np.minimum(large, nb - 1)
    bucket = ret + np.where(n < max_exact, n, large)
    return bucket.astype(np.int32), (n <= WINDOW)


def _bias_table(rel_bias, name):
    bucket, in_band = _t5_bucket_table()
    idx = jnp.asarray(np.where(in_band, bucket, -1))

    def body(rb_ref, idx_ref, o_ref):
        h = pl.program_id(0)
        iv = idx_ref[...]
        acc = jnp.where(iv < 0, NEG, 0.0).astype(F32)
        for b in range(NUM_BUCKETS):
            acc = acc + jnp.where(iv == b, rb_ref[b, h], 0.0)
        o_ref[...] = acc

    return pl.pallas_call(
        body, name=name, grid=(ATT_Q_HEADS,),
        in_specs=[pl.BlockSpec(memory_space=pltpu.SMEM), pl.BlockSpec((BLOCK, 3 * BLOCK), lambda h: (0, 0))],
        out_specs=pl.BlockSpec((None, BLOCK, 3 * BLOCK), lambda h: (h, 0, 0)),
        out_shape=jax.ShapeDtypeStruct((ATT_Q_HEADS, BLOCK, 3 * BLOCK), F32), compiler_params=_params("parallel"),
    )(rel_bias, idx)


def _bias_grad(ds_sum_t, name):
    bucket, in_band = _t5_bucket_table()
    idx_t = jnp.asarray(np.where(in_band, bucket, -1).T)

    def body(ds_ref, idx_ref, o_ref):
        iv, ds = idx_ref[...], ds_ref[...]
        for b in range(NUM_BUCKETS):
            o_ref[b:b + 1, :] = jnp.sum(jnp.where(iv == b, ds, 0.0), axis=0, keepdims=True)

    return pl.pallas_call(
        body, name=name, grid=(ATT_Q_HEADS,),
        in_specs=[pl.BlockSpec((None, 3 * BLOCK, BLOCK), lambda h: (h // ATT_GROUP, 0, h % ATT_GROUP)),
                  pl.BlockSpec((3 * BLOCK, BLOCK), lambda h: (0, 0))],
        out_specs=pl.BlockSpec((None, NUM_BUCKETS, BLOCK), lambda h: (h, 0, 0)),
        out_shape=jax.ShapeDtypeStruct((ATT_Q_HEADS, NUM_BUCKETS, BLOCK), F32), compiler_params=_params("parallel"),
    )(ds_sum_t, idx_t)


Q_COL = 5 * HG_WIDTH
KV_COL = Q_COL + ATT_WIDTH
GROUP_WIDTH = ATT_GROUP * ATT_HEAD_DIM


def _stack_heads(blk):
    dh = ATT_HEAD_DIM
    return jnp.concatenate([blk[:, g * dh:(g + 1) * dh] for g in range(ATT_GROUP)], axis=0)


def _unstack_heads(st):
    return jnp.concatenate([st[g * BLOCK:(g + 1) * BLOCK] for g in range(ATT_GROUP)], axis=1)


def _rms_rows(x):
    rstd = lax.rsqrt(jnp.mean(x * x, axis=-1, keepdims=True) + EPS)
    return x * rstd, rstd


def _edge_ok(n, nb):
    colid = lax.broadcasted_iota(jnp.int32, (ATT_GROUP * BLOCK, 3 * BLOCK), 1)
    return jnp.logical_and(jnp.logical_or(colid >= BLOCK, n > 0), jnp.logical_or(colid < 2 * BLOCK, n < nb - 1))


def _sink_column(sink_ref, j=0):
    heads = range(j * ATT_GROUP, (j + 1) * ATT_GROUP)
    return jnp.concatenate([jnp.broadcast_to(sink_ref[h][:, 0:1], (BLOCK, 1)) for h in heads], axis=0)


def _attn_fwd(z, q_g, k_g, sink, bias, name):
    S = z.shape[0]
    nb = S // BLOCK
    G, dh, KV = ATT_GROUP, ATT_HEAD_DIM, ATT_KV_HEADS
    scale = 1.0 / math.sqrt(dh)

    def body(q_ref, kv0, kv1, kv2, qg_ref, kg_ref, sink_ref, bias_ref, o_ref):
        n = pl.program_id(0)
        edge_ok = _edge_ok(n, nb)
        cat = jnp.concatenate([kv0[...], kv1[...], kv2[...]], axis=0)
        qblk = q_ref[...]
        kn = [(_rms_rows(cat[:, j * dh:(j + 1) * dh])[0] * kg_ref[...]).astype(BF16) for j in range(KV)]
        vb = [cat[:, (KV + j) * dh:(KV + j + 1) * dh].astype(BF16) for j in range(KV)]
        qn = [(_rms_rows(_stack_heads(qblk[:, j * GROUP_WIDTH:(j + 1) * GROUP_WIDTH]))[0] * (qg_ref[...] * scale)).astype(BF16)
              for j in range(KV)]
        s = [_nt(qn[j], kn[j]) + bias_ref[j * G:(j + 1) * G].reshape(G * BLOCK, 3 * BLOCK) for j in range(KV)]
        s = [jnp.where(edge_ok, sj, NEG) for sj in s]
        sinks = [_sink_column(sink_ref, j) for j in range(KV)]
        m = [jnp.maximum(jnp.max(s[j], axis=-1, keepdims=True), sinks[j]) for j in range(KV)]
        e = [jnp.exp(s[j] - m[j]) for j in range(KV)]
        den = [jnp.sum(e[j], axis=-1, keepdims=True) + jnp.exp(sinks[j] - m[j]) for j in range(KV)]
        o = [_nn(e[j].astype(BF16), vb[j]) * (1.0 / den[j]) for j in range(KV)]
        o_ref[...] = jnp.concatenate([_unstack_heads(oj) for oj in o], axis=1).astype(o_ref.dtype)

    def kv(shift):
        return pl.BlockSpec((BLOCK, 2 * KV_WIDTH), lambda n: (jnp.clip(n + shift, 0, nb - 1), KV_COL // (2 * KV_WIDTH)))

    gain = pl.BlockSpec((1, dh), lambda n: (0, 0))
    return pl.pallas_call(
        body, name=name, grid=(nb,),
        in_specs=[pl.BlockSpec((BLOCK, ATT_WIDTH), lambda n: (n, Q_COL // ATT_WIDTH)), kv(-1), kv(0), kv(1), gain, gain,
                  pl.BlockSpec((ATT_Q_HEADS, 1, BLOCK), lambda n: (0, 0, 0)),
                  pl.BlockSpec((ATT_Q_HEADS, BLOCK, 3 * BLOCK), lambda n: (0, 0, 0))],
        out_specs=pl.BlockSpec((BLOCK, ATT_WIDTH), lambda n: (n, 0)),
        out_shape=jax.ShapeDtypeStruct((S, ATT_WIDTH), BF16), compiler_params=_params("parallel"),
    )(z, z, z, z, q_g, k_g, sink, bias)


def _attn_bwd(z, q_g, k_g, sink, bias, do, name):
    S = z.shape[0]
    nb = S // BLOCK
    G, dh, KV = ATT_GROUP, ATT_HEAD_DIM, ATT_KV_HEADS
    scale = 1.0 / math.sqrt(dh)
    both = range(KV)
    bias_t = bias.reshape(KV, G, BLOCK, 3 * BLOCK).transpose(0, 3, 1, 2).reshape(KV, 3 * BLOCK, G * BLOCK)

    def body(q_ref, kv0, kv1, kv2, qg_ref, kg_ref, sink_ref, bias_ref, do_ref,
             dq_ref, dkw_ref, dvw_ref, ds_ref, dsink_ref, dqg_ref):
        n = pl.program_id(0)

        @pl.when(n == 0)
        def _():
            ds_ref[...] = jnp.zeros_like(ds_ref)
            dsink_ref[...] = jnp.zeros_like(dsink_ref)
            dqg_ref[...] = jnp.zeros_like(dqg_ref)

        rowid = lax.broadcasted_iota(jnp.int32, (3 * BLOCK, G * BLOCK), 0)
        edge_ok = jnp.logical_and(jnp.logical_or(rowid >= BLOCK, n > 0), jnp.logical_or(rowid < 2 * BLOCK, n < nb - 1))
        qg = qg_ref[...]
        cat = jnp.concatenate([kv0[...], kv1[...], kv2[...]], axis=0)
        qblk, doblk = q_ref[...], do_ref[...]
        kn = [(_rms_rows(cat[:, j * dh:(j + 1) * dh])[0] * kg_ref[...]).astype(BF16) for j in both]
        vb = [cat[:, (KV + j) * dh:(KV + j + 1) * dh].astype(BF16) for j in both]
        norm = [_rms_rows(_stack_heads(qblk[:, j * GROUP_WIDTH:(j + 1) * GROUP_WIDTH])) for j in both]
        qn = [(norm[j][0] * (qg * scale)).astype(BF16) for j in both]
        do_b = [_stack_heads(doblk[:, j * GROUP_WIDTH:(j + 1) * GROUP_WIDTH]).astype(BF16) for j in both]
        s = [_nt(kn[j], qn[j]) + bias_ref[j] for j in both]
        dp = [_nt(vb[j], do_b[j]) for j in both]
        s = [jnp.where(edge_ok, sj, NEG) for sj in s]
        sinks = [jnp.concatenate([sink_ref[j * G + g] for g in range(G)], axis=1) for j in both]
        m = [jnp.maximum(jnp.max(s[j], axis=0, keepdims=True), sinks[j]) for j in both]
        e = [jnp.exp(s[j] - m[j]) for j in both]
        e_sink = [jnp.exp(sinks[j] - m[j]) for j in both]
        inv = [1.0 / (jnp.sum(e[j], axis=0, keepdims=True) + e_sink[j]) for j in both]
        p = [e[j] * inv[j] for j in both]
        delta = [jnp.sum(p[j] * dp[j], axis=0, keepdims=True) for j in both]
        ds = [p[j] * (dp[j] - delta[j]) for j in both]
        ds_b = [dsj.astype(BF16) for dsj in ds]
        dqn = [_tn(kn[j], ds_b[j]).T * scale for j in both]
        for j in both:
            dvw_ref[j] = _nn(p[j].astype(BF16), do_b[j])
            dkw_ref[j] = _nn(ds_b[j], qn[j])
        for j in both:
            ds_ref[j] += ds[j]
            sink_term = e_sink[j] * inv[j] * delta[j]
            for g in range(G):
                dsink_ref[j * G + g] += (jnp.zeros((1, BLOCK), F32)
                                         - jnp.sum(sink_term[:, g * BLOCK:(g + 1) * BLOCK], axis=1, keepdims=True))
        dq = []
        for j in both:
            qhat, rstd = norm[j]
            dqg_ref[j] += jnp.sum(dqn[j] * qhat, axis=0, keepdims=True)
            dqh = dqn[j] * qg
            dq.append(_unstack_heads(rstd * (dqh - qhat * jnp.mean(dqh * qhat, axis=-1, keepdims=True))))
        dq_ref[...] = jnp.concatenate(dq, axis=1).astype(dq_ref.dtype)

    def kv(shift):
        return pl.BlockSpec((BLOCK, 2 * KV_WIDTH), lambda n: (jnp.clip(n + shift, 0, nb - 1), KV_COL // (2 * KV_WIDTH)))

    gain = pl.BlockSpec((1, dh), lambda n: (0, 0))
    sink_spec = pl.BlockSpec((ATT_Q_HEADS, 1, BLOCK), lambda n: (0, 0, 0))
    bias_spec = pl.BlockSpec((KV, 3 * BLOCK, G * BLOCK), lambda n: (0, 0, 0))
    win = pl.BlockSpec((KV, None, 3 * BLOCK, dh), lambda n: (0, n, 0, 0))
    wshape = jax.ShapeDtypeStruct((KV, nb, 3 * BLOCK, dh), F32)
    return pl.pallas_call(
        body, name=name, grid=(nb,),
        in_specs=[pl.BlockSpec((BLOCK, ATT_WIDTH), lambda n: (n, Q_COL // ATT_WIDTH)), kv(-1), kv(0), kv(1), gain, gain,
                  sink_spec, bias_spec, pl.BlockSpec((BLOCK, ATT_WIDTH), lambda n: (n, HG_WIDTH // ATT_WIDTH))],
        out_specs=[pl.BlockSpec((BLOCK, ATT_WIDTH), lambda n: (n, 0)), win, win, bias_spec, sink_spec,
                   pl.BlockSpec((KV, 1, dh), lambda n: (0, 0, 0))],
        out_shape=[jax.ShapeDtypeStruct((S, ATT_WIDTH), BF16), wshape, wshape,
                   jax.ShapeDtypeStruct((KV, 3 * BLOCK, G * BLOCK), F32),
                   jax.ShapeDtypeStruct((ATT_Q_HEADS, 1, BLOCK), F32),
                   jax.ShapeDtypeStruct((KV, 1, dh), F32)],
        compiler_params=_params("arbitrary"),
    )(z, z, z, z, q_g, k_g, sink, bias_t, do)


def _attn_kv_reduce(dkw, dvw, z, k_g, name):
    S = z.shape[0]
    nb = S // BLOCK
    dh = ATT_HEAD_DIM
    kb = min(8, nb)
    steps = nb // kb

    def body(a_lo, a, a_hi, b_lo, b, b_hi, kv_ref, kg_ref, dkv_ref, dkg_ref):
        n = pl.program_id(0)

        @pl.when(n == 0)
        def _():
            dkg_ref[...] = jnp.zeros_like(dkg_ref)

        lo = jnp.where(n > 0, 1.0, 0.0)
        hi = jnp.where(n < steps - 1, 1.0, 0.0)

        def overlap_add(w, w_lo, w_hi, j, i):
            before = lo * w_lo[j] if i == 0 else w[j, i - 1, 2 * BLOCK:3 * BLOCK, :]
            after = hi * w_hi[j] if i == kb - 1 else w[j, i + 1, 0:BLOCK, :]
            return w[j, i, BLOCK:2 * BLOCK, :] + before + after

        dkg = [jnp.zeros((1, dh), F32) for _ in range(ATT_KV_HEADS)]
        for i in range(kb):
            rows = slice(i * BLOCK, (i + 1) * BLOCK)
            dks, dvs = [], []
            for j in range(ATT_KV_HEADS):
                dkn = overlap_add(a, a_lo, a_hi, j, i)
                dvs.append(overlap_add(b, b_lo, b_hi, j, i))
                khat, rstd = _rms_rows(kv_ref[rows, j * dh:(j + 1) * dh])
                dkg[j] = dkg[j] + jnp.sum(dkn * khat, axis=0, keepdims=True)
                dkh = dkn * kg_ref[...]
                dks.append(rstd * (dkh - khat * jnp.mean(dkh * khat, axis=-1, keepdims=True)))
            dkv_ref[rows, :] = jnp.concatenate(dks + dvs, axis=1).astype(dkv_ref.dtype)
        for j in range(ATT_KV_HEADS):
            dkg_ref[j] += dkg[j]

    main = pl.BlockSpec((ATT_KV_HEADS, kb, 3 * BLOCK, dh), lambda n: (0, n, 0, 0))
    halo_lo = pl.BlockSpec((ATT_KV_HEADS, None, BLOCK, dh), lambda n: (0, jnp.maximum(n * kb - 1, 0), 2, 0))
    halo_hi = pl.BlockSpec((ATT_KV_HEADS, None, BLOCK, dh), lambda n: (0, jnp.minimum(n * kb + kb, nb - 1), 0, 0))
    return pl.pallas_call(
        body, name=name, grid=(steps,),
        in_specs=[halo_lo, main, halo_hi, halo_lo, main, halo_hi,
                  pl.BlockSpec((kb * BLOCK, 2 * KV_WIDTH), lambda n: (n, KV_COL // (2 * KV_WIDTH))),
                  pl.BlockSpec((1, dh), lambda n: (0, 0))],
        out_specs=[pl.BlockSpec((kb * BLOCK, 2 * KV_WIDTH), lambda n: (n, 0)),
                   pl.BlockSpec((ATT_KV_HEADS, 1, dh), lambda n: (0, 0, 0))],
        out_shape=[jax.ShapeDtypeStruct((S, 2 * KV_WIDTH), BF16), jax.ShapeDtypeStruct((ATT_KV_HEADS, 1, dh), F32)],
        compiler_params=_params("arbitrary"),
    )(dkw, dkw, dkw, dvw, dvw, dvw, z, k_g)


def _ada_wgrad(c_act_t, dm, name):
    D, nbatch = c_act_t.shape
    n = dm.shape[1]
    tr = 256

    def body(c_ref, dm_ref, o_ref):
        cv, dv = c_ref[...], dm_ref[...]
        acc = cv[:, 0:1] * dv[0:1, :]
        for b in range(1, nbatch):
            acc = acc + cv[:, b:b + 1] * dv[b:b + 1, :]
        o_ref[...] = acc

    return pl.pallas_call(
        body, name=name, grid=(D // tr,),
        in_specs=[pl.BlockSpec((tr, nbatch), lambda i: (i, 0)), pl.BlockSpec((nbatch, n), lambda i: (0, 0))],
        out_specs=pl.BlockSpec((tr, n), lambda i: (i, 0)), out_shape=jax.ShapeDtypeStruct((D, n), F32),
        compiler_params=_params("parallel"),
    )(c_act_t, dm)


def _to_bf16(w, name):
    R, Cn = w.shape
    tr = _row_tile(R)

    def body(w_ref, o_ref):
        o_ref[...] = w_ref[...].astype(BF16)

    blk = pl.BlockSpec((tr, Cn), lambda i: (i, 0))
    return pl.pallas_call(
        body, name=name, grid=(R // tr,), in_specs=[blk], out_specs=blk, out_shape=jax.ShapeDtypeStruct((R, Cn), BF16),
        compiler_params=_params("parallel"),
    )(w)


def _adamw(w, g, m, v, name):
    R, Cn = w.shape
    tr = R
    for cand in (256, 128, 64, 32, 16, 8):
        if R % cand == 0:
            tr = cand
            break

    def body(w_ref, g_ref, m_ref, v_ref, d_ref, nm_ref, nv_ref):
        gv = g_ref[...]
        m_new = ADAM_B1 * m_ref[...] + (1.0 - ADAM_B1) * gv
        v_new = ADAM_B2 * v_ref[...] + (1.0 - ADAM_B2) * (gv * gv)
        m_hat = m_new / (1.0 - ADAM_B1 ** ADAM_STEP)
        v_hat = v_new / (1.0 - ADAM_B2 ** ADAM_STEP)
        d_ref[...] = -ADAM_LR * (m_hat / (jnp.sqrt(v_hat) + ADAM_EPS) + ADAM_WD * w_ref[...])
        nm_ref[...] = m_new
        nv_ref[...] = v_new

    blk = pl.BlockSpec((tr, Cn), lambda i: (i, 0))
    shp = jax.ShapeDtypeStruct((R, Cn), F32)
    return pl.pallas_call(
        body, name=name, grid=(R // tr,), in_specs=[blk] * 4, out_specs=[blk] * 3, out_shape=[shp] * 3,
        compiler_params=_params("parallel"),
    )(w, g, m, v)


def _place():
    return lax.axis_index("x"), lax.axis_index("y"), lax.axis_index("c")


def _flip(place, k):
    x, y, c = place
    return (1 - x if k & 4 else x, 1 - y if k & 2 else y, 1 - c if k & 1 else c)


def _dev_index(place):
    x, y, c = place
    return 4 * x + 2 * y + c


def _chip_index(place):
    return 2 * place[0] + place[1]


def _gather8(x_ref, out_ref, send_sems, recv_sems, local_sem):
    me = _place()
    mine = pltpu.make_async_copy(x_ref, out_ref.at[_dev_index(me)], local_sem)
    mine.start()

    def copy(k, origin, to):
        return pltpu.make_async_remote_copy(
            src_ref=x_ref, dst_ref=out_ref.at[_dev_index(origin)], send_sem=send_sems.at[k - 1],
            recv_sem=recv_sems.at[k - 1], device_id=to, device_id_type=MESH)

    sends = [copy(k, me, _flip(me, k)) for k in range(1, 8)]
    for cp in sends:
        cp.start()
    for k in range(1, 8):
        copy(k, _flip(me, k), me).wait_recv()
    for cp in sends:
        cp.wait_send()
    mine.wait()


def _allgather8(x, name, reduce=False):
    R, Cn = x.shape

    def body(x_ref, *rest):
        if reduce:
            out_ref, sum_ref, send_sems, recv_sems, local_sem = rest
        else:
            out_ref, send_sems, recv_sems, local_sem = rest
        _gather8(x_ref, out_ref, send_sems, recv_sems, local_sem)
        if reduce:
            acc = out_ref[0]
            for i in range(1, 8):
                acc = acc + out_ref[i]
            sum_ref[...] = acc

    vm = pl.BlockSpec(memory_space=pltpu.VMEM)
    outs = [jax.ShapeDtypeStruct((8, R, Cn), F32)] + ([jax.ShapeDtypeStruct((R, Cn), F32)] if reduce else [])
    res = pl.pallas_call(
        body, name=name, in_specs=[vm], out_specs=[vm] * len(outs), out_shape=outs,
        scratch_shapes=[pltpu.SemaphoreType.DMA((7,)), pltpu.SemaphoreType.DMA((7,)), pltpu.SemaphoreType.DMA],
    )(x)
    return res if reduce else res[0]


def _prologue(small, w_ada, b_ada, w_shard, name):
    R, Cn = small.shape
    n_mod = w_ada.shape[1]
    big = _gather_over_ici([w_shard])

    def body(small_ref, wada_ref, b_ref, shard_ref, small_all_ref, mods_all_ref, gathered_ref, mods_ref,
             send1, recv1, send2, recv2, local_sems, big_send, big_recv):
        big.start([shard_ref], [gathered_ref], big_send, big_recv)
        _gather8(small_ref, small_all_ref, send1, recv1, local_sems.at[0])
        c_all = jnp.concatenate([small_all_ref[d, 0:1, :] for d in range(8)], axis=0)
        c_act = c_all * _sigmoid(c_all)
        mods_ref[...] = _nn(c_act, wada_ref[...], precision=lax.Precision.HIGHEST) + b_ref[...]
        _gather8(mods_ref, mods_all_ref, send2, recv2, local_sems.at[1])
        big.finish([shard_ref], [gathered_ref], big_send, big_recv)

    vm = pl.BlockSpec(memory_space=pltpu.VMEM)
    seven = pltpu.SemaphoreType.DMA((7,))
    return pl.pallas_call(
        body, name=name, in_specs=[vm, vm, vm, ANY], out_specs=[vm, vm, ANY],
        out_shape=[jax.ShapeDtypeStruct((8, R, Cn), F32), jax.ShapeDtypeStruct((8, 8, n_mod), F32)] + big.out_shapes,
        scratch_shapes=[pltpu.VMEM((8, n_mod), F32), seven, seven, seven, seven, pltpu.SemaphoreType.DMA((2,))]
        + big.sem_shapes(),
        compiler_params=pltpu.CompilerParams(vmem_limit_bytes=VMEM_LIMIT_BYTES),
    )(small, w_ada, b_ada, w_shard)


def _symmetric_plan(copies):
    def plan(in_refs, out_refs):
        sends = [(src, dst, i, to) for i, (src, dst, to) in enumerate(copies(in_refs, out_refs))]
        return sends, [(dst, i) for _, dst, i, _ in sends]
    return plan


def _halves_exchange(grads):
    def copies(in_refs, out_refs):
        me = _place()
        return [(g.at[kk, 1 - me[2]], got.at[kk], _flip(me, 1)) for g, got in zip(in_refs, out_refs) for kk in range(N_CHIPS)]

    return _Exchange(grads, [jax.ShapeDtypeStruct((N_CHIPS,) + g.shape[2:], g.dtype) for g in grads],
                     N_CHIPS * len(grads), _symmetric_plan(copies))


def _chips_exchange(parts):
    def copies(in_refs, out_refs):
        me = _place()
        return [(p.at[_chip_index(_flip(me, 2 * j))], got.at[j - 1], _flip(me, 2 * j))
                for p, got in zip(in_refs, out_refs) for j in (1, 2, 3)]

    return _Exchange(parts, [jax.ShapeDtypeStruct((3,) + p.shape[1:], p.dtype) for p in parts], 3 * len(parts),
                     _symmetric_plan(copies))


def _siblings_exchange(halves):
    def copies(in_refs, out_refs):
        sibling = _flip(_place(), 1)
        return [(h, got, sibling) for h, got in zip(in_refs, out_refs)]

    return _Exchange(halves, [jax.ShapeDtypeStruct(h.shape, h.dtype) for h in halves], len(halves), _symmetric_plan(copies))


def _ici_gather_plan(n, base=0):
    def plan(in_refs, out_refs):
        me = _place()
        c = me[2]
        sends, lands = [], []
        for a, (w, out) in enumerate(zip(in_refs[:n], out_refs)):
            for j in (1, 2, 3):
                i = base + 3 * a + j - 1
                sends.append((w.at[c], out.at[_chip_index(me), c], i, _flip(me, 2 * j)))
                lands.append((out.at[_chip_index(_flip(me, 2 * j)), c], i))
        return sends, lands
    return plan


def _d2d_gather_plan(n, base=0):
    def plan(in_refs, out_refs):
        me = _place()
        c = me[2]
        sibling = _flip(me, 1)
        mine = _chip_index(me)
        sends, lands = [], []
        for a, (w, out) in enumerate(zip(in_refs[:n], out_refs)):
            moves = [(w.at[c], (mine, c)), (w.at[1 - c], (mine, 1 - c))]
            moves += [(out.at[_chip_index(_flip(me, 2 * j)), c], (_chip_index(_flip(me, 2 * j)), c)) for j in (1, 2, 3)]
            for k, (src, (chip, half)) in enumerate(moves):
                sends.append((src, out.at[chip, half], base + 5 * a + k, sibling))
            blocks = [(mine, 1 - c), (mine, c)] + [(_chip_index(_flip(me, 2 * j)), 1 - c) for j in (1, 2, 3)]
            lands += [(out.at[chip, half], base + 5 * a + k) for k, (chip, half) in enumerate(blocks)]
        return sends, lands
    return plan


def _gathered_shapes(shards):
    return [jax.ShapeDtypeStruct((N_CHIPS,) + s.shape, s.dtype) for s in shards]


def _gather_over_ici(shards):
    return _Exchange(shards, _gathered_shapes(shards), 3 * len(shards), _ici_gather_plan(len(shards)))


def _gather_over_d2d(shards, gathered):
    n = len(shards)
    return _Exchange(list(shards) + list(gathered), [jax.ShapeDtypeStruct(g.shape, g.dtype) for g in gathered], 5 * n,
                     _d2d_gather_plan(n), aliases={n + a: a for a in range(n)})


def _gather_in_one(shards):
    n = len(shards)
    return _Exchange(shards, _gathered_shapes(shards), 8 * n, _ici_gather_plan(n), then=_d2d_gather_plan(n, base=3 * n))


def _row_tile(rows):
    for cand in (256, 176, 128, 64, 32, 16, 8):
        if rows % cand == 0:
            return cand
    return rows


def _pair_sum(core, grad, theirs, name):
    N, _, R, Cn = grad.shape
    tr = R

    def body(core_ref, g_ref, t_ref, o_ref, ob_ref):
        s = g_ref[...] + t_ref[...]
        o_ref[...] = s
        ob_ref[...] = s.astype(BF16)

    out = pl.BlockSpec((None, tr, Cn), lambda k, i, core_ref: (k, i, 0))
    return pl.pallas_call(
        body, name=name,
        grid_spec=pltpu.PrefetchScalarGridSpec(
            num_scalar_prefetch=1, grid=(N, R // tr),
            in_specs=[pl.BlockSpec((None, None, tr, Cn), lambda k, i, core_ref: (k, core_ref[0], i, 0)),
                      pl.BlockSpec((None, tr, Cn), lambda k, i, core_ref: (k, i, 0))],
            out_specs=[out, out]),
        out_shape=[jax.ShapeDtypeStruct((N, R, Cn), F32), jax.ShapeDtypeStruct((N, R, Cn), BF16)],
        compiler_params=_params("parallel", "parallel"),
    )(core, grad, theirs)


def _chip_sum(chip, parts, landed, name):
    _, R, Cn = parts.shape
    tr = R

    def body(chip_ref, p_ref, l_ref, o_ref):
        o_ref[...] = ((p_ref[...] + l_ref[0].astype(F32)) + l_ref[1].astype(F32)) + l_ref[2].astype(F32)

    return pl.pallas_call(
        body, name=name,
        grid_spec=pltpu.PrefetchScalarGridSpec(
            num_scalar_prefetch=1, grid=(R // tr,),
            in_specs=[pl.BlockSpec((None, tr, Cn), lambda i, chip_ref: (chip_ref[0], i, 0)),
                      pl.BlockSpec((3, tr, Cn), lambda i, chip_ref: (0, i, 0))],
            out_specs=pl.BlockSpec((tr, Cn), lambda i, chip_ref: (i, 0))),
        out_shape=jax.ShapeDtypeStruct((R, Cn), F32), compiler_params=_params("parallel"),
    )(chip, parts, landed)


def _pair_sums(core, grads, theirs, tag):
    return [_pair_sum(core, g, t, f"{tag}_pair_sum_{i}") for i, (g, t) in enumerate(zip(grads, theirs))]


def _chip_sums(chip, parts, landed, tag):
    return [_chip_sum(chip, p[0], l, f"{tag}_chip_sum_{i}") for i, (p, l) in enumerate(zip(parts, landed))]


def _by_chip_rows(g):
    return g.reshape(N_CHIPS, 2, g.shape[0] // (2 * N_CHIPS), g.shape[1])


def _by_chip_cols(g):
    return g.reshape(N_CHIPS, 2, g.shape[1] // 2, g.shape[2])


def _adamw_halves(core, w, g_mine, g_theirs, m, v, name):
    R2, Cn = w.shape
    r = R2 // 2
    tr = _row_tile(r)
    nt = r // tr

    def body(core_ref, w_ref, gm_ref, gt_ref, m_ref, v_ref, g_ref, d_ref, nm_ref, nv_ref):
        gv = jnp.where(pl.program_id(0) == core_ref[0], gm_ref[...], gt_ref[...])
        g_ref[...] = gv
        m_new = ADAM_B1 * m_ref[...] + (1.0 - ADAM_B1) * gv
        v_new = ADAM_B2 * v_ref[...] + (1.0 - ADAM_B2) * (gv * gv)
        m_hat = m_new / (1.0 - ADAM_B1 ** ADAM_STEP)
        v_hat = v_new / (1.0 - ADAM_B2 ** ADAM_STEP)
        d_ref[...] = -ADAM_LR * (m_hat / (jnp.sqrt(v_hat) + ADAM_EPS) + ADAM_WD * w_ref[...])
        nm_ref[...] = m_new
        nv_ref[...] = v_new

    full = pl.BlockSpec((tr, Cn), lambda hf, i, core_ref: (hf * nt + i, 0))
    half = pl.BlockSpec((tr, Cn), lambda hf, i, core_ref: (i, 0))
    shp = jax.ShapeDtypeStruct((R2, Cn), F32)
    return pl.pallas_call(
        body, name=name,
        grid_spec=pltpu.PrefetchScalarGridSpec(
            num_scalar_prefetch=1, grid=(2, nt), in_specs=[full, half, half, full, full], out_specs=[full] * 4),
        out_shape=[shp] * 4, compiler_params=_params("parallel", "parallel"),
    )(core, w, g_mine, g_theirs, m, v)


def _pad_row(v, width):
    v = v.reshape(1, -1)
    return jnp.pad(v, ((0, 0), (0, width - v.shape[1])))


def _ffn1_forward(x, ng, shift, scale, gate, w_in_shard, w_in_partly, w_out_shard, gather, next_norm):
    (h,), (w_in4,) = _rmsmod_fwd(x, ng, shift, scale, "ffn1_norm", exchange=_gather_over_d2d([w_in_shard], [w_in_partly]))
    w_in4 = w_in4.reshape(N_CHIPS, D_MODEL, FF_SHARD)
    (zg, zu, a), (partly, (w_out4,)) = _ffn_in_fwd(
        h, w_in4, "ffn1_in", exchange=[_gather_over_ici(gather), _gather_in_one([w_out_shard])])
    w_out = w_out4.reshape(D_FF, D_MODEL)
    (x_new, f, h_next), gathered = _proj_out_fwd([a], w_out, x, gate, 0.5, "ffn1_out", next_norm=next_norm,
                                                 exchange=_gather_over_d2d(gather, partly))
    return x_new, (h, zg, zu, a, f), w_in4, w_out, gathered, h_next


def _ffn_backward(df, saved, w_in4, w_out, core, chip, tag, riding=None, norm=None, in_first=False):
    h, zg, zu, a = saved[:4]
    rode = None
    if riding:
        (dzg, dzu), rode = _dact_bwd(df, w_out, zg, zu, f"{tag}_dact", exchange=riding)
    else:
        dzg, dzu = _dact_bwd(df, w_out, zg, zu, f"{tag}_dact")

    def dw_out(exchange=None):
        outs = _wgrad(a, [df], df.shape[1], f"{tag}_dw_out", exchange=exchange)
        (dw,), landed = outs if exchange else (outs, None)
        return [_by_chip_rows(dw.reshape(a.shape[1], df.shape[1]))], landed

    def dw_in(exchange=None):
        outs = _wgrad(h, [dzg, dzu], FF_SHARD, f"{tag}_dw_in", exchange=exchange)
        (dw,), landed = outs if exchange else (outs, None)
        return [_by_chip_cols(dw.reshape(N_CHIPS, h.shape[1], FF_SHARD))], landed

    (first, tag_1), (second, tag_2) = ((dw_in, "in"), (dw_out, "out"))[::1 if in_first else -1]
    g_1, _ = first()
    g_2, theirs_1 = second(_halves_exchange(g_1))
    parts_1 = _pair_sums(core, g_1, theirs_1, f"{tag}_{tag_1}")
    dh_outs, (theirs_2, landed_1) = _ffn_in_dgrad(
        dzg, dzu, w_in4, f"{tag}_dh", norm=norm, exchange=[_halves_exchange(g_2), _chips_exchange([parts_1[0][1]])])
    parts_2 = _pair_sums(core, g_2, theirs_2, f"{tag}_{tag_2}")
    return dh_outs, parts_2, _chip_sums(chip, parts_1, landed_1, f"{tag}_{tag_1}"), rode


def kernel(x, c, w_ada, b_ada, norm_g, w_ffn1_in, w_ffn1_out, w_ffn2_in, w_ffn2_out, w_mix_in, w_mix_out, hgrn_lb, hgrn_norm_g, qk_norm_g, attn_sink, rel_bias, loss_target, m_w_ada, m_b_ada, m_norm_g, m_w_ffn1_in, m_w_ffn1_out, m_w_ffn2_in, m_w_ffn2_out, m_w_mix_in, m_w_mix_out, m_hgrn_lb, m_hgrn_norm_g, m_qk_norm_g, m_attn_sink, m_rel_bias, v_w_ada, v_b_ada, v_norm_g, v_w_ffn1_in, v_w_ffn1_out, v_w_ffn2_in, v_w_ffn2_out, v_w_mix_in, v_w_mix_out, v_hgrn_lb, v_hgrn_norm_g, v_qk_norm_g, v_attn_sink, v_rel_bias):
    D = D_MODEL
    S = x.shape[1]
    place = (lax.axis_index("x"), lax.axis_index("y"), lax.axis_index("c"))
    me, my_chip = _dev_index(place), _chip_index(place)
    x0 = x[0]
    target = loss_target[0]

    def halves(w, tag):
        return _to_bf16(w[0], f"{tag}_to_bf16").reshape(2, w.shape[1] // 2, w.shape[2])

    w1_out_shard = halves(w_ffn1_out, "w_ffn1_out")
    mix_shards = [halves(w_mix_in, "w_mix_in"), halves(w_mix_out, "w_mix_out")]
    ffn2_shards = [halves(w_ffn2_in, "w_ffn2_in"), halves(w_ffn2_out, "w_ffn2_out")]
    core_arr = jnp.reshape(place[2], (1,)).astype(jnp.int32)
    chip_arr = jnp.reshape(my_chip, (1,)).astype(jnp.int32)

    small = jnp.concatenate([_pad_row(c, D), _pad_row(norm_g, D), _pad_row(hgrn_lb, D), jnp.zeros((5, D), F32)], axis=0)
    n_ada = w_ada.shape[2]
    b_mine = lax.dynamic_slice_in_dim(b_ada, my_chip * n_ada, n_ada, axis=1)
    w1_in_shard = halves(w_ffn1_in, "w_ffn1_in")
    small_all, mods_parts, w1_in_partly = _prologue(small, w_ada[0], b_mine, w1_in_shard, "prologue")
    c_all = small_all[:, 0, :]
    by_chip = small_all[0::2]
    norm_g_full = by_chip[:, 1, :3 * 256].reshape(N_CHIPS, 3, 256).transpose(1, 0, 2).reshape(3, D)
    lb_raw = by_chip[:, 2, :2 * 2 * 128].reshape(N_CHIPS, 2, 2, 128).transpose(1, 2, 0, 3).reshape(2, 2, HG_WIDTH)
    lb_logit = lb_raw[:, 0, :] - lb_raw[:, 1, :]
    lb = jax.nn.sigmoid(lb_logit)
    one_minus_lb = jax.nn.sigmoid(-lb_logit)
    lb_f = jnp.stack([lb[0], one_minus_lb[0]])
    lb_b = jnp.stack([lb[1], one_minus_lb[1]])

    c_act_all = c_all * jax.nn.sigmoid(c_all)
    mods_all = mods_parts[0::2].transpose(1, 0, 2).reshape(8, N_MOD * D)
    mods = lax.dynamic_slice_in_dim(mods_all, me, 1, axis=0)
    sh1, sc1, g1, sh2, sc2, g2, sh3, sc3, g3 = [mods[:, i * D:(i + 1) * D] for i in range(N_MOD)]

    x1, saved1, w1_in, w1_out, gathered, h2 = _ffn1_forward(
        x0, norm_g_full[0:1], sh1, sc1, g1, w1_in_shard, w1_in_partly, w1_out_shard, mix_shards, (norm_g_full[1:2], sh2, sc2))
    wm_in = gathered[0].reshape(N_CHIPS, D, D_IN // N_CHIPS).transpose(1, 0, 2).reshape(D, D_IN)
    wm_out = gathered[1].reshape(D, D)

    z = _matmul_nn(h2, wm_in, F32, 256, "mix_in")
    (of, st_f, ob, st_b), gathered = _hgrn_fwd_both(z, (lb_f, lb_b), "hgrn_fwd", exchange=_gather_in_one(ffn2_shards))
    w2_in = gathered[0].reshape(N_CHIPS, D, FF_SHARD)
    w2_out = gathered[1].reshape(D_FF, D)
    o_h = _hgrn_post_fwd(of, ob, z, hgrn_norm_g, "hgrn_post")

    q_g, k_g = qk_norm_g[0, 0:1], qk_norm_g[0, 1:2]
    sink_b = jnp.broadcast_to(attn_sink.reshape(ATT_Q_HEADS, 1, 1), (ATT_Q_HEADS, 1, BLOCK))
    bias = _bias_table(rel_bias, "bias_table")
    o_a = _attn_fwd(z, q_g, k_g, sink_b, bias, "attn_fwd")
    x2, mixed, h3 = _proj_out_fwd([o_h, o_a], wm_out, x1, g2, 1.0, "mix_out", next_norm=(norm_g_full[2:3], sh3, sc3))

    zg3, zu3, a3 = _ffn_in_fwd(h3, w2_in, "ffn2_in")
    dx3, df3, dg3, sq_cols = _proj_out_loss(a3, w2_out, x2, g3, 0.5, target, "ffn2_out_loss")
    loss_mine = 0.5 * jnp.sum(sq_cols) / D

    (dx2, dsh3, dsc3, dng3, dmixed, dg2), parts2, mine2_out, _ = _ffn_backward(
        df3, (h3, zg3, zu3, a3), w2_in, w2_out, core_arr, chip_arr, "ffn2",
        norm=_NormBwd(x2, norm_g_full[2:3], sc3, dx3, below=(mixed, g2, 1.0)))

    (do_cat,) = _matmul_nt([dmixed], wm_out, ROW_TILE, "mix_out_dgrad")
    dwm_out = _wgrad_rows([o_h, o_a], dmixed, "mix_out_dw").reshape(D, D)

    do_sum, dgr, d_hnorm = _hgrn_post_bwd(do_cat, of, ob, z, hgrn_norm_g, "hgrn_post_bwd")
    (dq_f, dff, dv_f, doml_f), landed2 = _hgrn_bwd(z, lb_f, do_sum, st_f, 0, "hgrn_bwd_f",
                                                   exchange=_chips_exchange([p[1] for p in parts2]))
    mine2 = _chip_sums(chip_arr, parts2, landed2, "ffn2_in") + mine2_out
    (dhq, dfb, dhi, doml_b), theirs2 = _hgrn_bwd(z, lb_b, do_sum, st_b, 1, "hgrn_bwd_b", acc=(dq_f, dv_f),
                                                 exchange=_siblings_exchange(mine2))

    daq, dkw, dvw, ds_sum, dsink, dqg = _attn_bwd(z, q_g, k_g, sink_b, bias, do_cat, "attn_bwd")
    dkv, dkg = _attn_kv_reduce(dkw, dvw, z, k_g, "attn_kv_reduce")
    d_rel_bias = jnp.sum(_bias_grad(ds_sum, "bias_grad"), axis=-1).T
    dz = [dhq, dff, dfb, dhi, dgr, daq, dkv]
    dwm_in = _wgrad_pieces(h2, dz, 2 * KV_WIDTH, "mix_in_dw").transpose(1, 0, 2).reshape(D, D_IN)
    wide = D_IN // N_CHIPS
    grads_m = [_by_chip_cols(dwm_in.reshape(D, N_CHIPS, wide).transpose(1, 0, 2)), _by_chip_rows(dwm_out)]
    (dx1, dsh2, dsc2, dng2, df1, dg1), theirs_m = _matmul_nt(
        dz, wm_in, 256, "mix_in_dgrad", exchange=_halves_exchange(grads_m),
        norm=_NormBwd(x1, norm_g_full[1:2], sc2, dx2, below=(saved1[4], g1, 0.5)))
    parts_m = _pair_sums(core_arr, grads_m, theirs_m, "mix")

    (dh1,), parts1, mine1_in, landed_m = _ffn_backward(df1, saved1, w1_in, w1_out, core_arr, chip_arr, "ffn1",
                                                       riding=_chips_exchange([p[1] for p in parts_m]), in_first=True)
    mine_m = _chip_sums(chip_arr, parts_m, landed_m, "mix")
    (dx0, dsh1, dsc1, dng1), landed1 = _rmsmod_bwd(dh1, _NormBwd(x0, norm_g_full[0:1], sc1, dx1), "ffn1_norm_bwd",
                                                   exchange=_chips_exchange([p[1] for p in parts1]))
    mine1 = mine1_in + _chip_sums(chip_arr, parts1, landed1, "ffn1_out")
    theirs_1m = list(_run_exchange(_siblings_exchange(mine1 + mine_m), "siblings_exchange"))
    reduced = list(zip(mine1 + mine2 + mine_m, theirs_1m[:2] + list(theirs2) + theirs_1m[2:]))

    dlb = -jnp.concatenate([doml_f, doml_b], axis=0)
    dlb_raw = dlb * lb * one_minus_lb
    d_hgrn_lb = jnp.stack([dlb_raw, -dlb_raw], axis=1)
    d_qk = jnp.concatenate([jnp.sum(dqg, axis=0), jnp.sum(dkg, axis=0)], axis=0)
    dmods = jnp.concatenate([dsh1, dsc1, dg1, dsh2, dsc2, dg2, dsh3, dsc3, dg3], axis=0)
    packed = jnp.concatenate(
        [dmods, dng1, dng2, dng3, d_hgrn_lb.reshape(2, D), _pad_row(d_hnorm, D), _pad_row(d_qk, D),
         _pad_row(dsink[:, 0, 0], D), _pad_row(d_rel_bias, D), _pad_row(loss_mine, D)], axis=0)
    packed = jnp.pad(packed, ((0, 24 - packed.shape[0]), (0, 0)))
    packed_all, packed_sum = _allgather8(packed, "small_grads_allgather", reduce=True)
    dmods_all = packed_all[:, 0:N_MOD, :].reshape(8, N_MOD * D)
    g_b_ada = packed_sum[0:N_MOD].reshape(1, N_MOD * D)
    g_norm_full = packed_sum[9:12]
    g_norm_g = lax.dynamic_slice_in_dim(g_norm_full, my_chip * 256, 256, axis=1).reshape(1, 3, 256)
    g_hgrn_lb = lax.dynamic_slice_in_dim(packed_sum[12:14].reshape(2, 2, HG_WIDTH), my_chip * 128, 128, axis=2)
    g_hgrn_norm_g = packed_sum[14:15, :HG_WIDTH]
    g_qk_norm_g = packed_sum[15, :2 * ATT_HEAD_DIM].reshape(1, 2, ATT_HEAD_DIM)
    g_attn_sink = packed_sum[16:17, :ATT_Q_HEADS]
    g_rel_bias = packed_sum[17, :NUM_BUCKETS * ATT_Q_HEADS].reshape(NUM_BUCKETS, ATT_Q_HEADS)
    loss = packed_sum[18, 0]

    dm_mine = lax.dynamic_slice_in_dim(dmods_all, my_chip * n_ada, n_ada, axis=1)
    g_w_ada = _ada_wgrad(c_act_all.T, dm_mine, "ada_wgrad")[None]

    def big(w, g, m, v, name):
        d, nm, nv = _adamw(w[0], g[0], m[0], v[0], name)
        return d[None], nm[None], nv[None]

    def big_halves(w, g_pair, m, v, name):
        g, d, nm, nv = _adamw_halves(core_arr, w[0], g_pair[0], g_pair[1], m[0], v[0], name)
        return g[None], (d[None], nm[None], nv[None])

    g_w1_in, u_w1_in = big_halves(w_ffn1_in, reduced[0], m_w_ffn1_in, v_w_ffn1_in, "adamw_w_ffn1_in")
    g_w1_out, u_w1_out = big_halves(w_ffn1_out, reduced[1], m_w_ffn1_out, v_w_ffn1_out, "adamw_w_ffn1_out")
    g_w2_in, u_w2_in = big_halves(w_ffn2_in, reduced[2], m_w_ffn2_in, v_w_ffn2_in, "adamw_w_ffn2_in")
    g_w2_out, u_w2_out = big_halves(w_ffn2_out, reduced[3], m_w_ffn2_out, v_w_ffn2_out, "adamw_w_ffn2_out")
    g_wm_in, u_wm_in = big_halves(w_mix_in, reduced[4], m_w_mix_in, v_w_mix_in, "adamw_w_mix_in")
    g_wm_out, u_wm_out = big_halves(w_mix_out, reduced[5], m_w_mix_out, v_w_mix_out, "adamw_w_mix_out")

    smalls = [(b_ada, g_b_ada, m_b_ada, v_b_ada), (norm_g, g_norm_g, m_norm_g, v_norm_g), (hgrn_lb, g_hgrn_lb, m_hgrn_lb, v_hgrn_lb),
              (hgrn_norm_g, g_hgrn_norm_g, m_hgrn_norm_g, v_hgrn_norm_g), (qk_norm_g, g_qk_norm_g, m_qk_norm_g, v_qk_norm_g),
              (attn_sink, g_attn_sink, m_attn_sink, v_attn_sink), (rel_bias, g_rel_bias, m_rel_bias, v_rel_bias)]
    sizes = [t[0].size for t in smalls]
    total = sum(sizes)
    rows = -(-total // 128)
    rows = -(-rows // 8) * 8

    def pack(i):
        flat = jnp.concatenate([t[i].reshape(-1) for t in smalls])
        fill = 1.0 if i == 3 else 0.0
        return jnp.pad(flat, (0, rows * 128 - total), constant_values=fill).reshape(rows, 128)

    packed_out = _adamw(pack(0), pack(1), pack(2), pack(3), "adamw_small")

    def unpack(flat2d):
        flat = flat2d.reshape(-1)
        outs, off = [], 0
        for t, n in zip(smalls, sizes):
            outs.append(flat[off:off + n].reshape(t[0].shape))
            off += n
        return outs

    d_small, m_small, v_small = [unpack(t) for t in packed_out]

    upd = {
        "w_ada": big(w_ada, g_w_ada, m_w_ada, v_w_ada, "adamw_w_ada"),
        "w_ffn1_in": u_w1_in, "w_ffn1_out": u_w1_out, "w_ffn2_in": u_w2_in, "w_ffn2_out": u_w2_out,
        "w_mix_in": u_wm_in, "w_mix_out": u_wm_out,
    }
    small_names = ["b_ada", "norm_g", "hgrn_lb", "hgrn_norm_g", "qk_norm_g", "attn_sink", "rel_bias"]
    for i, nme in enumerate(small_names):
        upd[nme] = (d_small[i], m_small[i], v_small[i])
    grads = {
        "w_ada": g_w_ada, "b_ada": g_b_ada, "norm_g": g_norm_g, "w_ffn1_in": g_w1_in, "w_ffn1_out": g_w1_out,
        "w_ffn2_in": g_w2_in, "w_ffn2_out": g_w2_out, "w_mix_in": g_wm_in, "w_mix_out": g_wm_out, "hgrn_lb": g_hgrn_lb,
        "hgrn_norm_g": g_hgrn_norm_g, "qk_norm_g": g_qk_norm_g, "attn_sink": g_attn_sink, "rel_bias": g_rel_bias,
    }
    order = ["w_ada", "b_ada", "norm_g", "w_ffn1_in", "w_ffn1_out", "w_ffn2_in", "w_ffn2_out", "w_mix_in", "w_mix_out",
             "hgrn_lb", "hgrn_norm_g", "qk_norm_g", "attn_sink", "rel_bias"]
    return (loss, dx0[None], *[grads[k] for k in order], *[upd[k][0] for k in order], *[upd[k][1] for k in order],
            *[upd[k][2] for k in order])
```

```python
import functools
import math

import numpy as np
import jax
import jax.numpy as jnp
from jax import lax
from jax.experimental import pallas as pl
from jax.experimental.pallas import tpu as pltpu

F32, BF16 = jnp.float32, jnp.bfloat16

D_MODEL = 1024
D_FF = 2816
HG_HEADS, HG_DIM = 4, 128
HG_WIDTH = HG_HEADS * HG_DIM
ATT_Q_HEADS, ATT_KV_HEADS, ATT_HEAD_DIM = 8, 2, 64
ATT_GROUP = ATT_Q_HEADS // ATT_KV_HEADS
ATT_WIDTH = ATT_Q_HEADS * ATT_HEAD_DIM
KV_WIDTH = ATT_KV_HEADS * ATT_HEAD_DIM
WINDOW, BLOCK = 128, 128
NUM_BUCKETS, MAX_DISTANCE = 32, 128
N_MOD = 9
EPS = 1e-6
D_IN = 5 * HG_WIDTH + ATT_WIDTH + 2 * KV_WIDTH
ADAM_LR, ADAM_B1, ADAM_B2, ADAM_EPS, ADAM_WD, ADAM_STEP = 0.001, 0.9, 0.999, 1e-08, 0.01, 10

N_CHIPS = 4
FF_SHARD = 2 * D_FF // N_CHIPS
NEG = -1e30

VMEM_LIMIT_BYTES = 56 << 20
ROW_TILE = 512
HG_CHUNK = 16
HG_ROWS = 512

MESH = pl.DeviceIdType.MESH
ANY = pl.BlockSpec(memory_space=pl.ANY)


def _params(*sem):
    return pltpu.CompilerParams(dimension_semantics=sem, vmem_limit_bytes=VMEM_LIMIT_BYTES)


def _resident(shape, index_map):
    return pl.BlockSpec(shape, index_map, pipeline_mode=pl.Buffered(1))


def _dot(a, b, dims, precision=None):
    return lax.dot_general(a, b, (dims, ((), ())), precision=precision, preferred_element_type=F32)


def _nn(a, b, precision=None):
    return _dot(a, b, ((1,), (0,)), precision)


def _nt(a, b):
    return _dot(a, b, ((1,), (1,)))


def _tn(a, b):
    return _dot(a, b, ((0,), (0,)))


def _sigmoid(x):
    return jax.nn.sigmoid(x)


class _Exchange:
    def __init__(self, inputs, out_shapes, n_sems, plan, aliases=None, then=None):
        self.inputs, self.out_shapes, self.n_sems, self.plan, self.aliases = list(inputs), list(out_shapes), n_sems, plan, aliases or {}
        self.then = then

    def sem_shapes(self):
        return [pltpu.SemaphoreType.DMA((self.n_sems,)), pltpu.SemaphoreType.DMA((self.n_sems,))]

    @staticmethod
    def _copy(src, dst, i, to, send_sems, recv_sems):
        return pltpu.make_async_remote_copy(
            src_ref=src, dst_ref=dst, send_sem=send_sems.at[i], recv_sem=recv_sems.at[i], device_id=to, device_id_type=MESH)

    def _start(self, plan, in_refs, out_refs, send_sems, recv_sems):
        for src, dst, i, to in plan(in_refs, out_refs)[0]:
            self._copy(src, dst, i, to, send_sems, recv_sems).start()

    def _wait(self, plan, in_refs, out_refs, send_sems, recv_sems):
        sends, lands = plan(in_refs, out_refs)
        for zone, i in lands:
            self._copy(zone, zone, i, _place(), send_sems, recv_sems).wait_recv()
        for src, dst, i, to in sends:
            self._copy(src, dst, i, to, send_sems, recv_sems).wait_send()

    def start(self, *refs):
        self._start(self.plan, *refs)

    def switch(self, *refs):
        if self.then:
            self._wait(self.plan, *refs)
            self._start(self.then, *refs)

    def finish(self, *refs):
        self._wait(self.then or self.plan, *refs)


def _run_exchange(ex, name):
    n_in, n_out = len(ex.inputs), len(ex.out_shapes)

    def body(*refs):
        in_refs, out_refs, (send_sems, recv_sems) = refs[:n_in], refs[n_in:n_in + n_out], refs[n_in + n_out:]
        ex.start(in_refs, out_refs, send_sems, recv_sems)
        ex.switch(in_refs, out_refs, send_sems, recv_sems)
        ex.finish(in_refs, out_refs, send_sems, recv_sems)

    return pl.pallas_call(
        body, name=name, in_specs=[ANY] * n_in, out_specs=[ANY] * n_out, out_shape=ex.out_shapes,
        scratch_shapes=ex.sem_shapes(), input_output_aliases=dict(ex.aliases),
    )(*ex.inputs)


def _call(body, *, name, grid, in_specs, out_specs, out_shape, args, semantics, scratch_shapes=(), exchange=None):
    if exchange is None:
        return pl.pallas_call(
            body, name=name, grid=grid, in_specs=in_specs, out_specs=out_specs, out_shape=out_shape,
            scratch_shapes=list(scratch_shapes), compiler_params=_params(*semantics))(*args)
    exs = exchange if isinstance(exchange, (list, tuple)) else [exchange]
    n_in, n_out, n_scr = len(in_specs), len(out_specs), len(scratch_shapes)
    x_in, x_out = [len(ex.inputs) for ex in exs], [len(ex.out_shapes) for ex in exs]

    def take(refs, counts):
        groups = []
        for n in counts:
            groups.append(refs[:n])
            refs = refs[n:]
        return groups, refs

    def carrier(*refs):
        ins, refs = refs[:n_in], refs[n_in:]
        x_ins, refs = take(refs, x_in)
        outs, refs = refs[:n_out], refs[n_out:]
        x_outs, refs = take(refs, x_out)
        scr, refs = refs[:n_scr], refs[n_scr:]
        sems, _ = take(refs, [2] * len(exs))
        ids = [pl.program_id(a) for a in range(len(grid))]
        first = functools.reduce(jnp.logical_and, [i == 0 for i in ids])
        last = functools.reduce(jnp.logical_and, [i == g - 1 for i, g in zip(ids, grid)])
        step = functools.reduce(lambda acc, ig: acc * ig[1] + ig[0], zip(ids, grid), 0)

        @pl.when(first)
        def _():
            for ex, xi, xo, (send_sems, recv_sems) in zip(exs, x_ins, x_outs, sems):
                ex.start(xi, xo, send_sems, recv_sems)

        if any(ex.then for ex in exs):
            @pl.when(step == (3 * math.prod(grid)) // 4)
            def _():
                for ex, xi, xo, (send_sems, recv_sems) in zip(exs, x_ins, x_outs, sems):
                    ex.switch(xi, xo, send_sems, recv_sems)

        body(*ins, *outs, *scr)

        @pl.when(last)
        def _():
            for ex, xi, xo, (send_sems, recv_sems) in zip(exs, x_ins, x_outs, sems):
                ex.finish(xi, xo, send_sems, recv_sems)

    aliases, i0, o0 = {}, n_in, n_out
    for ex in exs:
        aliases.update({i0 + i: o0 + o for i, o in ex.aliases.items()})
        i0, o0 = i0 + len(ex.inputs), o0 + len(ex.out_shapes)
    res = pl.pallas_call(
        carrier, name=name, grid=grid, in_specs=list(in_specs) + [ANY] * sum(x_in),
        out_specs=list(out_specs) + [ANY] * sum(x_out),
        out_shape=list(out_shape) + [s for ex in exs for s in ex.out_shapes],
        scratch_shapes=list(scratch_shapes) + [s for ex in exs for s in ex.sem_shapes()],
        input_output_aliases=aliases, compiler_params=_params(*["arbitrary"] * len(grid)),
    )(*args, *[a for ex in exs for a in ex.inputs])
    x_res, _ = take(list(res[n_out:]), x_out)
    return list(res[:n_out]), (x_res if isinstance(exchange, (list, tuple)) else x_res[0])


def _rmsmod_fwd(x, g, shift, scale, name, exchange=None):
    S, D = x.shape
    tr = min(ROW_TILE, S)

    def body(x_ref, g_ref, sh_ref, sc_ref, h_ref):
        xv = x_ref[...]
        rstd = lax.rsqrt(jnp.mean(xv * xv, axis=-1, keepdims=True) + EPS)
        y = xv * rstd * g_ref[...]
        h_ref[...] = (y * (1.0 + sc_ref[...]) + sh_ref[...]).astype(h_ref.dtype)

    row = pl.BlockSpec((tr, D), lambda i: (i, 0))
    vec = pl.BlockSpec((1, D), lambda i: (0, 0))
    return _call(body, name=name, grid=(S // tr,), in_specs=[row, vec, vec, vec], out_specs=[row],
                 out_shape=[jax.ShapeDtypeStruct((S, D), BF16)], args=(x, g, shift, scale), semantics=("parallel",),
                 exchange=exchange)


class _NormBwd:
    def __init__(self, x, g, scale, dx_res, below=None):
        S, D = x.shape
        self.below, self.coef = below, (below[2] if below else None)
        self.inputs = [x, g, scale, dx_res] + ([below[0], below[1]] if below else [])
        vshape = jax.ShapeDtypeStruct((1, D), F32)
        self.out_shape = [jax.ShapeDtypeStruct((S, D), F32), vshape, vshape, vshape]
        if below:
            self.out_shape += [jax.ShapeDtypeStruct((S, D), BF16), vshape]

    def specs(self, tr, D):
        row = pl.BlockSpec((tr, D), lambda i: (i, 0))
        vec = pl.BlockSpec((1, D), lambda i: (0, 0))
        return ([row, vec, vec, row] + ([row, vec] if self.below else []),
                [row, vec, vec, vec] + ([row, vec] if self.below else []))

    def step(self, dhv, in_refs, out_refs):
        if self.below:
            x_ref, g_ref, sc_ref, dxr_ref, f_ref, gate_ref = in_refs
            dx_ref, dsh_ref, dsc_ref, dg_ref, df_ref, dgate_ref = out_refs
            sums = (dsh_ref, dsc_ref, dg_ref, dgate_ref)
        else:
            x_ref, g_ref, sc_ref, dxr_ref = in_refs
            dx_ref, dsh_ref, dsc_ref, dg_ref = out_refs
            sums = (dsh_ref, dsc_ref, dg_ref)

        @pl.when(pl.program_id(0) == 0)
        def _():
            for ref in sums:
                ref[...] = jnp.zeros_like(ref)

        xv, gv = x_ref[...], g_ref[...]
        one_sc = 1.0 + sc_ref[...]
        rstd = lax.rsqrt(jnp.mean(xv * xv, axis=-1, keepdims=True) + EPS)
        n = xv * rstd
        dsh_ref[...] += jnp.sum(dhv, axis=0, keepdims=True)
        dsc_ref[...] += jnp.sum(dhv * n, axis=0, keepdims=True) * gv
        dg_ref[...] += jnp.sum(dhv * n, axis=0, keepdims=True) * one_sc
        dn = dhv * (gv * one_sc)
        dx = dxr_ref[...] + rstd * (dn - n * jnp.mean(dn * n, axis=-1, keepdims=True))
        dx_ref[...] = dx
        if self.below:
            df_ref[...] = (self.coef * gate_ref[...] * dx).astype(df_ref.dtype)
            dgate_ref[...] += self.coef * jnp.sum(dx * f_ref[...].astype(F32), axis=0, keepdims=True)


def _rmsmod_bwd(dh, norm, name, exchange=None):
    S, D = dh.shape
    tr = min(ROW_TILE, S)
    n_in = len(norm.inputs)

    def body(dh_ref, *refs):
        norm.step(dh_ref[...], refs[:n_in], refs[n_in:])

    in_specs, out_specs = norm.specs(tr, D)
    return _call(body, name=name, grid=(S // tr,), in_specs=[pl.BlockSpec((tr, D), lambda i: (i, 0))] + in_specs,
                 out_specs=out_specs, out_shape=norm.out_shape, args=[dh] + norm.inputs, semantics=("arbitrary",),
                 exchange=exchange)


def _ffn_in_fwd(h, w4, name, exchange=None):
    S, D = h.shape
    tm = min(2 * ROW_TILE, S)
    n = w4.shape[2]

    def body(h_ref, wg_ref, wu_ref, zg_ref, zu_ref, a_ref):
        hv = h_ref[...]
        zg = _nn(hv, wg_ref[...])
        zu = _nn(hv, wu_ref[...])
        zg_ref[...] = zg.astype(zg_ref.dtype)
        zu_ref[...] = zu.astype(zu_ref.dtype)
        a_ref[...] = (zg * _sigmoid(zg) * zu).astype(a_ref.dtype)

    out = pl.BlockSpec((tm, n), lambda j, m: (m, j))
    oshape = jax.ShapeDtypeStruct((S, 2 * n), BF16)
    return _call(
        body, name=name, grid=(2, S // tm),
        in_specs=[pl.BlockSpec((tm, D), lambda j, m: (m, 0)),
                  pl.BlockSpec((None, D, n), lambda j, m: (j, 0, 0)),
                  pl.BlockSpec((None, D, n), lambda j, m: (j + 2, 0, 0))],
        out_specs=[out, out, out], out_shape=[oshape, oshape, oshape], args=(h, w4, w4),
        semantics=("parallel", "parallel"), exchange=exchange)


def _proj_out_fwd(lhs, w, x, gate, coef, name, exchange=None, next_norm=None):
    S, D = x.shape
    tm = min(ROW_TILE, S)
    ks = [a.shape[1] for a in lhs]

    def body(*refs):
        lhs_refs, refs = refs[:len(lhs)], refs[len(lhs):]
        if next_norm:
            w_ref, x_ref, gate_ref, g_ref, sh_ref, sc_ref, xn_ref, f_ref, h_ref = refs
        else:
            w_ref, x_ref, gate_ref, xn_ref, f_ref = refs
        acc, off = None, 0
        for a_ref, k in zip(lhs_refs, ks):
            part = _nn(a_ref[...], w_ref[off:off + k, :])
            acc = part if acc is None else acc + part
            off += k
        f_ref[...] = acc.astype(f_ref.dtype)
        xn = x_ref[...] + coef * gate_ref[...] * acc
        xn_ref[...] = xn
        if next_norm:
            rstd = lax.rsqrt(jnp.mean(xn * xn, axis=-1, keepdims=True) + EPS)
            h_ref[...] = (xn * rstd * g_ref[...] * (1.0 + sc_ref[...]) + sh_ref[...]).astype(h_ref.dtype)

    row = pl.BlockSpec((tm, D), lambda m: (m, 0))
    vec = pl.BlockSpec((1, D), lambda m: (0, 0))
    extra = list(next_norm) if next_norm else []
    return _call(
        body, name=name, grid=(S // tm,),
        in_specs=[pl.BlockSpec((tm, k), lambda m: (m, 0)) for k in ks]
        + [_resident(w.shape, lambda m: (0, 0)), row, vec] + [vec] * len(extra),
        out_specs=[row, row] + ([row] if next_norm else []),
        out_shape=[jax.ShapeDtypeStruct((S, D), F32), jax.ShapeDtypeStruct((S, D), BF16)]
        + ([jax.ShapeDtypeStruct((S, D), BF16)] if next_norm else []),
        args=(*lhs, w, x, gate, *extra), semantics=("parallel",), exchange=exchange)


def _proj_out_loss(lhs, w, x, gate, coef, target, name):
    S, D = x.shape
    tm = min(ROW_TILE, S)

    def body(a_ref, w_ref, x_ref, gate_ref, t_ref, dy_ref, df_ref, dgate_ref, sq_ref):
        @pl.when(pl.program_id(0) == 0)
        def _():
            dgate_ref[...] = jnp.zeros_like(dgate_ref)
            sq_ref[...] = jnp.zeros_like(sq_ref)

        f = _nn(a_ref[...], w_ref[...])
        gate = coef * gate_ref[...]
        err = x_ref[...] + gate * f - t_ref[...]
        sq_ref[...] += jnp.sum(err * err, axis=0, keepdims=True)
        dy = err * (1.0 / D)
        dy_ref[...] = dy
        df_ref[...] = (gate * dy).astype(df_ref.dtype)
        dgate_ref[...] += coef * jnp.sum(dy * f, axis=0, keepdims=True)

    row = pl.BlockSpec((tm, D), lambda m: (m, 0))
    vec = pl.BlockSpec((1, D), lambda m: (0, 0))
    vshape = jax.ShapeDtypeStruct((1, D), F32)
    return pl.pallas_call(
        body, name=name, grid=(S // tm,),
        in_specs=[pl.BlockSpec((tm, lhs.shape[1]), lambda m: (m, 0)), _resident(w.shape, lambda m: (0, 0)), row, vec, row],
        out_specs=[row, row, vec, vec],
        out_shape=[jax.ShapeDtypeStruct((S, D), F32), jax.ShapeDtypeStruct((S, D), BF16), vshape, vshape],
        compiler_params=_params("arbitrary"),
    )(lhs, w, x, gate, target)


def _matmul_nn(a, w, out_dtype, tm, name):
    S, K = a.shape
    N = w.shape[1]
    tm = min(tm, S)

    def body(a_ref, w_ref, o_ref):
        o_ref[...] = _nn(a_ref[...], w_ref[...]).astype(o_ref.dtype)

    return pl.pallas_call(
        body, name=name, grid=(S // tm,),
        in_specs=[pl.BlockSpec((tm, K), lambda m: (m, 0)), _resident((K, N), lambda m: (0, 0))],
        out_specs=pl.BlockSpec((tm, N), lambda m: (m, 0)), out_shape=jax.ShapeDtypeStruct((S, N), out_dtype),
        compiler_params=_params("parallel"),
    )(a, w)


def _dact_bwd(df, w_out, zg, zu, name, exchange=None):
    S, D = df.shape
    tm = min(ROW_TILE, S)
    n = w_out.shape[0] // 2

    def body(df_ref, w_ref, zg_ref, zu_ref, dzg_ref, dzu_ref):
        da = _nt(df_ref[...], w_ref[...]).astype(BF16)
        zg_v, zu_v = zg_ref[...], zu_ref[...]
        s = _sigmoid(zg_v)
        dzu_ref[...] = da * zg_v * s
        dzg_ref[...] = da * zu_v * (s * (1.0 + zg_v * (1.0 - s)))

    blk = pl.BlockSpec((tm, n), lambda j, m: (m, j))
    oshape = jax.ShapeDtypeStruct((S, 2 * n), BF16)
    return _call(
        body, name=name, grid=(2, S // tm),
        in_specs=[pl.BlockSpec((tm, D), lambda j, m: (m, 0)), pl.BlockSpec((n, D), lambda j, m: (j, 0)), blk, blk],
        out_specs=[blk, blk], out_shape=[oshape, oshape], args=(df, w_out, zg, zu), semantics=("parallel", "parallel"),
        exchange=exchange)


def _ffn_in_dgrad(dzg, dzu, w4, name, exchange=None, norm=None):
    S = dzg.shape[0]
    D, n = w4.shape[1], w4.shape[2]
    tm = min(ROW_TILE, S)
    n_norm = len(norm.inputs) if norm else 0

    def body(dzg_ref, dzu_ref, w_ref, *refs):
        acc = _nt(dzg_ref[:, 0:n], w_ref[0])
        acc += _nt(dzg_ref[:, n:2 * n], w_ref[1])
        acc += _nt(dzu_ref[:, 0:n], w_ref[2])
        acc += _nt(dzu_ref[:, n:2 * n], w_ref[3])
        if norm:
            norm.step(acc, refs[:n_norm], refs[n_norm:])
        else:
            refs[0][...] = acc

    blk = pl.BlockSpec((tm, 2 * n), lambda m: (m, 0))
    in_specs, args = [blk, blk, _resident(w4.shape, lambda m: (0, 0, 0))], [dzg, dzu, w4]
    out_specs, out_shape = [pl.BlockSpec((tm, D), lambda m: (m, 0))], [jax.ShapeDtypeStruct((S, D), F32)]
    if norm:
        norm_in, out_specs = norm.specs(tm, D)
        in_specs, args, out_shape = in_specs + norm_in, args + norm.inputs, norm.out_shape
    return _call(body, name=name, grid=(S // tm,), in_specs=in_specs, out_specs=out_specs, out_shape=out_shape, args=args,
                 semantics=("arbitrary",) if norm else ("parallel",), exchange=exchange)


def _matmul_nt(pieces, w, tm, name, exchange=None, norm=None):
    S = pieces[0].shape[0]
    ks = [p.shape[1] for p in pieces]
    N = w.shape[0]
    tm = min(tm, S)
    n_norm = len(norm.inputs) if norm else 0

    def body(*refs):
        p_refs, w_ref, refs = refs[:len(ks)], refs[len(ks)], refs[len(ks) + 1:]
        acc, off = None, 0
        for p_ref, k in zip(p_refs, ks):
            part = _nt(p_ref[...], w_ref[:, off:off + k])
            acc = part if acc is None else acc + part
            off += k
        if norm:
            norm.step(acc, refs[:n_norm], refs[n_norm:])
        else:
            refs[0][...] = acc

    in_specs = [pl.BlockSpec((tm, k), lambda m: (m, 0)) for k in ks] + [_resident(w.shape, lambda m: (0, 0))]
    args = list(pieces) + [w]
    out_specs, out_shape = [pl.BlockSpec((tm, N), lambda m: (m, 0))], [jax.ShapeDtypeStruct((S, N), F32)]
    if norm:
        norm_in, out_specs = norm.specs(tm, N)
        in_specs, args, out_shape = in_specs + norm_in, args + norm.inputs, norm.out_shape
    return _call(body, name=name, grid=(S // tm,), in_specs=in_specs, out_specs=out_specs, out_shape=out_shape, args=args,
                 semantics=("arbitrary",) if norm else ("parallel",), exchange=exchange)


def _wgrad(a, gs, tn, name, exchange=None):
    S, Ka = a.shape
    N = gs[0].shape[1]
    ts = min(ROW_TILE * (2 if Ka <= D_MODEL else 1), S)

    def body(a_ref, *refs):
        g_refs, o_ref = refs[:-1], refs[-1]

        @pl.when(pl.program_id(1) == 0)
        def _():
            o_ref[...] = jnp.zeros_like(o_ref)

        a_t = a_ref[...].T
        for i, g_ref in enumerate(g_refs):
            o_ref[i] += _nn(a_t, g_ref[...])

    return _call(
        body, name=name, grid=(N // tn, S // ts),
        in_specs=[pl.BlockSpec((ts, Ka), lambda j, s: (s, 0))] + [pl.BlockSpec((ts, tn), lambda j, s: (s, j))] * len(gs),
        out_specs=[pl.BlockSpec((len(gs), None, Ka, tn), lambda j, s: (0, j, 0, 0))],
        out_shape=[jax.ShapeDtypeStruct((len(gs), N // tn, Ka, tn), F32)], args=(a, *gs),
        semantics=("parallel", "arbitrary"), exchange=exchange)


def _wgrad_pieces(a, pieces, tn, name):
    S, Ka = a.shape
    ts = min(ROW_TILE, S)
    blocks = [(i, j) for i, p in enumerate(pieces) for j in range(p.shape[1] // tn)]

    def body(a_ref, *refs):
        g_refs, o_ref = refs[:-1], refs[-1]

        @pl.when(pl.program_id(0) == 0)
        def _():
            o_ref[...] = jnp.zeros_like(o_ref)

        a_t = a_ref[...].T
        for b, g_ref in enumerate(g_refs):
            o_ref[b] += _nn(a_t, g_ref[...])

    return pl.pallas_call(
        body, name=name, grid=(S // ts,),
        in_specs=[pl.BlockSpec((ts, Ka), lambda s: (s, 0))] + [pl.BlockSpec((ts, tn), lambda s, j=j: (s, j)) for _, j in blocks],
        out_specs=pl.BlockSpec((len(blocks), Ka, tn), lambda s: (0, 0, 0)),
        out_shape=jax.ShapeDtypeStruct((len(blocks), Ka, tn), F32), compiler_params=_params("arbitrary"),
    )(a, *[pieces[i] for i, _ in blocks])


def _wgrad_rows(lhs, g, name):
    S, Ka = lhs[0].shape
    N = g.shape[1]
    ts = min(ROW_TILE, S)

    def body(*refs):
        a_refs, g_ref, o_ref = refs[:-2], refs[-2], refs[-1]

        @pl.when(pl.program_id(0) == 0)
        def _():
            o_ref[...] = jnp.zeros_like(o_ref)

        gv = g_ref[...]
        for i, a_ref in enumerate(a_refs):
            o_ref[i] += _tn(a_ref[...], gv)

    return pl.pallas_call(
        body, name=name, grid=(S // ts,),
        in_specs=[pl.BlockSpec((ts, Ka), lambda s: (s, 0))] * len(lhs) + [pl.BlockSpec((ts, N), lambda s: (s, 0))],
        out_specs=pl.BlockSpec((len(lhs), Ka, N), lambda s: (0, 0, 0)),
        out_shape=jax.ShapeDtypeStruct((len(lhs), Ka, N), F32), compiler_params=_params("arbitrary"),
    )(*lhs, g)


def _hgrn_chunk_common(qr, fr, lb, oml, tri, last):
    sig_nf = _sigmoid(-fr)
    k = oml * sig_nf
    f_small = lb + oml * (jnp.exp(jnp.minimum(fr, 0.0)) * sig_nf)
    use_k = k < 0.5
    f = jnp.where(use_k, 1.0 - k, f_small)
    g = jnp.where(use_k, jnp.log1p(-k), jnp.log(f_small)) * math.log2(math.e)
    q = qr * _sigmoid(qr)
    G = _nn(tri, g, precision=lax.Precision.HIGHEST)
    Gl = G[last:last + 1]
    return q, k, f, G, Gl


def _hgrn_consts(reverse):
    C = HG_CHUNK
    r = lax.broadcasted_iota(jnp.int32, (C, C), 0)
    cc = lax.broadcasted_iota(jnp.int32, (C, C), 1)
    tri = ((cc >= r) if reverse else (cc <= r)).astype(F32)
    tri_t = ((cc <= r) if reverse else (cc >= r)).astype(F32)
    rid = lax.broadcasted_iota(jnp.int32, (C, HG_WIDTH), 0)
    return tri, tri_t, rid, (0 if reverse else C - 1)


def _head_slices():
    return [slice(h * HG_DIM, (h + 1) * HG_DIM) for h in range(HG_HEADS)]


def _per_head_lane_sum(x):
    C = x.shape[0]
    return jnp.concatenate(
        [jnp.broadcast_to(jnp.sum(x[:, sl], axis=-1, keepdims=True), (C, HG_DIM)) for sl in _head_slices()], axis=1)


HG_TILE = 8


def _pair_tiles(s, reverse):
    blk, r = divmod(s, HG_TILE)
    n_tiles = HG_CHUNK // HG_TILE
    others = range(0, blk) if reverse else range(blk + 1, n_tiles)
    return [(blk, r)] + [(t, None) for t in others]


def _pair_decay(G, s, tile, r, rid8, reverse, keys=False):
    rs = slice(tile * HG_TILE, (tile + 1) * HG_TILE)
    d = (G[s:s + 1] - G[rs]) if keys else (G[rs] - G[s:s + 1])
    if r is not None:
        d = jnp.where((rid8 <= r) if reverse else (rid8 >= r), d, NEG)
    return rs, jnp.exp2(d)


def _hgrn_fwd_both(z, lbs, name, exchange=None):
    S = z.shape[0]
    C, DK, W = HG_CHUNK, HG_DIM, HG_WIDTH
    tb = min(HG_ROWS, S)
    n_t, n_c = S // tb, tb // C
    dirs = (0, 1)

    def body(qf_ref, ff_ref, vf_ref, qb_ref, fb_ref, vb_ref, lbf_ref, lbb_ref, of_ref, stf_out, ob_ref, stb_out, st_ref):
        @pl.when(pl.program_id(0) == 0)
        def _():
            st_ref[...] = jnp.zeros_like(st_ref)

        q_refs, f_refs, v_refs, lb_refs = (qf_ref, qb_ref), (ff_ref, fb_ref), (vf_ref, vb_ref), (lbf_ref, lbb_ref)
        o_refs, st_outs = (of_ref, ob_ref), (stf_out, stb_out)
        consts = [_hgrn_consts(d == 1) for d in dirs]
        rid8 = lax.broadcasted_iota(jnp.int32, (HG_TILE, W), 0)

        def chunk(ci, carry):
            cidx = [ci, n_c - 1 - ci]
            rows = [pl.ds(pl.multiple_of(c * C, C), C) for c in cidx]
            v = [v_refs[d][rows[d], :] for d in dirs]
            com = [_hgrn_chunk_common(q_refs[d][rows[d], :], f_refs[d][rows[d], :], lb_refs[d][0:1, :], lb_refs[d][1:2, :],
                                      consts[d][0], consts[d][3]) for d in dirs]
            q, k, G, Gl = [c[0] for c in com], [c[1] for c in com], [c[3] for c in com], [c[4] for c in com]
            qd = [(q[d] * jnp.exp2(G[d])).astype(BF16) for d in dirs]
            kd = [(k[d] * jnp.exp2(Gl[d] - G[d])).astype(BF16) for d in dirs]
            e_gl = [jnp.exp2(Gl[d]) for d in dirs]
            v_b = [v[d].astype(BF16) for d in dirs]
            inter = [[], []]
            for h, sl in enumerate(_head_slices()):
                for d in dirs:
                    st0 = st_ref[d, h]
                    st_outs[d][h, cidx[d]] = st0
                    inter[d].append(_nt(qd[d][:, sl], st0.astype(BF16)))
                    st_ref[d, h] = st0 * e_gl[d][:, sl] + _tn(v_b[d][:, sl], kd[d][:, sl])
            o_t = [[jnp.concatenate(inter[d], axis=1)[t * HG_TILE:(t + 1) * HG_TILE] for t in range(C // HG_TILE)] for d in dirs]
            for s in range(C):
                for d in dirs:
                    k_s, v_s = k[d][s:s + 1], v[d][s:s + 1]
                    for tile, r in _pair_tiles(s, d == 1):
                        rs, e_s = _pair_decay(G[d], s, tile, r, rid8, d == 1)
                        o_t[d][tile] = o_t[d][tile] + _per_head_lane_sum(q[d][rs] * k_s * e_s) * v_s
            for d in dirs:
                o_refs[d][rows[d], :] = jnp.concatenate(o_t[d], axis=0)
            return carry

        lax.fori_loop(0, n_c, chunk, 0, unroll=4)

    def sec(j, back):
        return pl.BlockSpec((tb, W), (lambda i: (n_t - 1 - i, j)) if back else (lambda i: (i, j)))

    def st_spec(back):
        return pl.BlockSpec((HG_HEADS, n_c, DK, DK), (lambda i: (0, n_t - 1 - i, 0, 0)) if back else (lambda i: (0, i, 0, 0)))

    vec = pl.BlockSpec((2, W), lambda i: (0, 0))
    o_shape = jax.ShapeDtypeStruct((S, W), F32)
    st_shape = jax.ShapeDtypeStruct((HG_HEADS, S // C, DK, DK), F32)
    return _call(
        body, name=name, grid=(n_t,),
        in_specs=[sec(0, False), sec(1, False), sec(3, False), sec(0, True), sec(2, True), sec(3, True), vec, vec],
        out_specs=[sec(0, False), st_spec(False), sec(0, True), st_spec(True)],
        out_shape=[o_shape, st_shape, o_shape, st_shape],
        scratch_shapes=[pltpu.VMEM((2, HG_HEADS, DK, DK), F32)], args=(z, z, z, z, z, z, lbs[0], lbs[1]),
        semantics=("arbitrary",), exchange=exchange)


def _hgrn_bwd(z, lb, do, states, direction, name, acc=None, exchange=None):
    S = z.shape[0]
    C, DK, W = HG_CHUNK, HG_DIM, HG_WIDTH
    tb = min(HG_ROWS, S)
    n_t, n_c = S // tb, tb // C
    reverse = direction == 1
    tmap = (lambda i: i) if reverse else (lambda i: n_t - 1 - i)

    def body(*refs):
        if acc:
            q_ref, f_ref, v_ref, lb_ref, do_ref, st_in_ref, dqa_ref, dva_ref, dq_ref, df_ref, dv_ref, doml_ref, dst_ref = refs
        else:
            q_ref, f_ref, v_ref, lb_ref, do_ref, st_in_ref, dq_ref, df_ref, dv_ref, doml_ref, dst_ref = refs

        @pl.when(pl.program_id(0) == 0)
        def _():
            dst_ref[...] = jnp.zeros_like(dst_ref)
            doml_ref[...] = jnp.zeros_like(doml_ref)

        lbv, oml = lb_ref[0:1, :], lb_ref[1:2, :]
        tri, tri_t, rid, last = _hgrn_consts(reverse)
        rid8 = lax.broadcasted_iota(jnp.int32, (HG_TILE, W), 0)

        def chunk(ci, carry):
            cidx = ci if reverse else (n_c - 1 - ci)
            rows = pl.ds(pl.multiple_of(cidx * C, C), C)
            qr, fr, v, dov = q_ref[rows, :], f_ref[rows, :], v_ref[rows, :], do_ref[rows, :]
            q, k, f, G, Gl = _hgrn_chunk_common(qr, fr, lbv, oml, tri, last)
            e_g, e_gl, e_kd = jnp.exp2(G), jnp.exp2(Gl), jnp.exp2(Gl - G)
            qd, kd = q * e_g, k * e_kd
            do_b, v_b, qd_b, kd_b = dov.astype(BF16), v.astype(BF16), qd.astype(BF16), kd.astype(BF16)
            dqd, dkd, dv, state_dot = [], [], [], []
            for h, sl in enumerate(_head_slices()):
                st0, dst1 = st_in_ref[h, cidx], dst_ref[h]
                dst1_b = dst1.astype(BF16)
                dqd.append(_nn(do_b[:, sl], st0.astype(BF16)))
                dkd.append(_nn(v_b[:, sl], dst1_b))
                dv.append(_nt(kd_b[:, sl], dst1_b))
                state_dot.append(jnp.sum(st0 * dst1, axis=0, keepdims=True))
                dst_ref[h] = dst1 * e_gl[:, sl] + _tn(do_b[:, sl], qd_b[:, sl])
            dqd, dkd, dv = [jnp.concatenate(t, axis=1) for t in (dqd, dkd, dv)]
            d_gl = e_gl * jnp.concatenate(state_dot, axis=1) + jnp.sum(dkd * kd, axis=0, keepdims=True)
            dq, dk = dqd * e_g, dkd * e_kd
            n_tiles = C // HG_TILE
            dq_t, dk_t, dv_t = [[x[t * HG_TILE:(t + 1) * HG_TILE] for t in range(n_tiles)] for x in (dq, dk, dv)]
            for s in range(C):
                k_s, v_s = k[s:s + 1], v[s:s + 1]
                for tile, r in _pair_tiles(s, reverse):
                    rs, e_s = _pair_decay(G, s, tile, r, rid8, reverse)
                    dq_t[tile] = dq_t[tile] + _per_head_lane_sum(dov[rs] * v_s) * e_s * k_s
            for t in range(C):
                q_t, do_t = q[t:t + 1], dov[t:t + 1]
                for tile, r in _pair_tiles(t, not reverse):
                    rs, x_t = _pair_decay(G, t, tile, r, rid8, not reverse, keys=True)
                    qx = q_t * x_t
                    dv_t[tile] = dv_t[tile] + _per_head_lane_sum(k[rs] * qx) * do_t
                    dk_t[tile] = dk_t[tile] + _per_head_lane_sum(v[rs] * do_t) * qx
            dq, dk, dv = [jnp.concatenate(x, axis=0) for x in (dq_t, dk_t, dv_t)]
            d_big_g = dq * q - dk * k + jnp.where(rid == last, d_gl, 0.0)
            dg = _nn(tri_t, d_big_g, precision=lax.Precision.HIGHEST)
            dk_all = dk - dg / f
            sig_nf = _sigmoid(-fr)
            df_ref[rows, :] = (-dk_all * k * (1.0 - sig_nf)).astype(df_ref.dtype)
            doml_ref[...] += jnp.sum(dk_all * sig_nf, axis=0, keepdims=True)
            sq = _sigmoid(qr)
            dqr = dq * (sq * (1.0 + qr * (1.0 - sq)))
            if acc:
                dqr = dqr + dqa_ref[rows, :]
                dv = dv + dva_ref[rows, :]
            dq_ref[rows, :] = dqr.astype(dq_ref.dtype)
            dv_ref[rows, :] = dv.astype(dv_ref.dtype)
            return carry

        lax.fori_loop(0, n_c, chunk, 0, unroll=8)

    def sec(j):
        return pl.BlockSpec((tb, W), lambda i: (tmap(i), j))

    vec = pl.BlockSpec((1, W), lambda i: (0, 0))
    ins = [z, z, z, lb, do, states]
    in_specs = [sec(0), sec(1 + direction), sec(3), pl.BlockSpec((2, W), lambda i: (0, 0)), sec(0),
                pl.BlockSpec((HG_HEADS, n_c, DK, DK), lambda i: (0, tmap(i), 0, 0))]
    if acc:
        ins += list(acc)
        in_specs += [sec(0), sec(0)]
    final = jax.ShapeDtypeStruct((S, W), BF16)
    partial = final if acc else jax.ShapeDtypeStruct((S, W), F32)
    return _call(
        body, name=name, grid=(n_t,), in_specs=in_specs,
        out_specs=[sec(0), sec(0), sec(0), vec],
        out_shape=[partial, final, partial, jax.ShapeDtypeStruct((1, W), F32)],
        scratch_shapes=[pltpu.VMEM((HG_HEADS, DK, DK), F32)], args=ins, semantics=("arbitrary",), exchange=exchange)


def _hgrn_post_fwd(o_f, o_b, z, norm_g, name):
    S = z.shape[0]
    tr = min(ROW_TILE, S)

    def body(of_ref, ob_ref, gr_ref, ng_ref, y_ref):
        o = of_ref[...] + ob_ref[...]
        gr = gr_ref[...]
        gate = gr * _sigmoid(gr)
        ng = ng_ref[...]
        for h in range(HG_HEADS):
            sl = slice(h * HG_DIM, (h + 1) * HG_DIM)
            oh = o[:, sl]
            rstd = lax.rsqrt(jnp.mean(oh * oh, axis=-1, keepdims=True) + EPS)
            y_ref[:, sl] = (oh * rstd * ng[:, sl] * gate[:, sl]).astype(y_ref.dtype)

    row = pl.BlockSpec((tr, HG_WIDTH), lambda i: (i, 0))
    return pl.pallas_call(
        body, name=name, grid=(S // tr,),
        in_specs=[row, row, pl.BlockSpec((tr, HG_WIDTH), lambda i: (i, 4)), pl.BlockSpec((1, HG_WIDTH), lambda i: (0, 0))],
        out_specs=row, out_shape=jax.ShapeDtypeStruct((S, HG_WIDTH), BF16), compiler_params=_params("parallel"),
    )(o_f, o_b, z, norm_g)


def _hgrn_post_bwd(dy, o_f, o_b, z, norm_g, name):
    S = z.shape[0]
    tr = min(ROW_TILE, S)

    def body(dy_ref, of_ref, ob_ref, gr_ref, ng_ref, do_ref, dgr_ref, dng_ref):
        @pl.when(pl.program_id(0) == 0)
        def _():
            dng_ref[...] = jnp.zeros_like(dng_ref)

        o = of_ref[...] + ob_ref[...]
        gr, ng, dyv = gr_ref[...], ng_ref[...], dy_ref[...]
        sg = _sigmoid(gr)
        for h in range(HG_HEADS):
            sl = slice(h * HG_DIM, (h + 1) * HG_DIM)
            oh, dyh, grh, sgh, ngh = o[:, sl], dyv[:, sl], gr[:, sl], sg[:, sl], ng[:, sl]
            rstd = lax.rsqrt(jnp.mean(oh * oh, axis=-1, keepdims=True) + EPS)
            on = oh * rstd
            du = dyh * (grh * sgh)
            dgr_ref[:, sl] = (dyh * (on * ngh) * (sgh * (1.0 + grh * (1.0 - sgh)))).astype(dgr_ref.dtype)
            dng_ref[:, sl] += jnp.sum(du * on, axis=0, keepdims=True)
            don = du * ngh
            do_ref[:, sl] = rstd * (don - on * jnp.mean(don * on, axis=-1, keepdims=True))

    row = pl.BlockSpec((tr, HG_WIDTH), lambda i: (i, 0))
    vec = pl.BlockSpec((1, HG_WIDTH), lambda i: (0, 0))
    full = jax.ShapeDtypeStruct((S, HG_WIDTH), F32)
    return pl.pallas_call(
        body, name=name, grid=(S // tr,),
        in_specs=[row, row, row, pl.BlockSpec((tr, HG_WIDTH), lambda i: (i, 4)), vec],
        out_specs=[row, row, vec],
        out_shape=[full, jax.ShapeDtypeStruct((S, HG_WIDTH), BF16), jax.ShapeDtypeStruct((1, HG_WIDTH), F32)],
        compiler_params=_params("arbitrary"),
    )(dy, o_f, o_b, z, norm_g)


def _t5_bucket_table():
    rel = (np.arange(3 * BLOCK)[None, :] - BLOCK) - np.arange(BLOCK)[:, None]
    nb = NUM_BUCKETS // 2
    max_exact = nb // 2
    ret = (rel > 0).astype(np.int32) * nb
    n = np.abs(rel)
    ratio = np.log(np.maximum(n, 1).astype(np.float32) / np.float32(max_exact)) / np.float32(math.log(MAX_DISTANCE / max_exact))
    large = max_exact + (ratio.astype(np.float32) * np.float32(nb - max_exact)).astype(np.int32)
    large = np.minimum(large, nb - 1)
    bucket = ret + np.where(n < max_exact, n, large)
    return bucket.astype(np.int32), (n <= WINDOW)


def _bias_table(rel_bias, name):
    bucket, in_band = _t5_bucket_table()
    idx = jnp.asarray(np.where(in_band, bucket, -1))

    def body(rb_ref, idx_ref, o_ref):
        h = pl.program_id(0)
        iv = idx_ref[...]
        acc = jnp.where(iv < 0, NEG, 0.0).astype(F32)
        for b in range(NUM_BUCKETS):
            acc = acc + jnp.where(iv == b, rb_ref[b, h], 0.0)
        o_ref[...] = acc

    return pl.pallas_call(
        body, name=name, grid=(ATT_Q_HEADS,),
        in_specs=[pl.BlockSpec(memory_space=pltpu.SMEM), pl.BlockSpec((BLOCK, 3 * BLOCK), lambda h: (0, 0))],
        out_specs=pl.BlockSpec((None, BLOCK, 3 * BLOCK), lambda h: (h, 0, 0)),
        out_shape=jax.ShapeDtypeStruct((ATT_Q_HEADS, BLOCK, 3 * BLOCK), F32), compiler_params=_params("parallel"),
    )(rel_bias, idx)


def _bias_grad(ds_sum_t, name):
    bucket, in_band = _t5_bucket_table()
    idx_t = jnp.asarray(np.where(in_band, bucket, -1).T)

    def body(ds_ref, idx_ref, o_ref):
        iv, ds = idx_ref[...], ds_ref[...]
        for b in range(NUM_BUCKETS):
            o_ref[b:b + 1, :] = jnp.sum(jnp.where(iv == b, ds, 0.0), axis=0, keepdims=True)

    return pl.pallas_call(
        body, name=name, grid=(ATT_Q_HEADS,),
        in_specs=[pl.BlockSpec((None, 3 * BLOCK, BLOCK), lambda h: (h // ATT_GROUP, 0, h % ATT_GROUP)),
                  pl.BlockSpec((3 * BLOCK, BLOCK), lambda h: (0, 0))],
        out_specs=pl.BlockSpec((None, NUM_BUCKETS, BLOCK), lambda h: (h, 0, 0)),
        out_shape=jax.ShapeDtypeStruct((ATT_Q_HEADS, NUM_BUCKETS, BLOCK), F32), compiler_params=_params("parallel"),
    )(ds_sum_t, idx_t)


Q_COL = 5 * HG_WIDTH
KV_COL = Q_COL + ATT_WIDTH
GROUP_WIDTH = ATT_GROUP * ATT_HEAD_DIM


def _stack_heads(blk):
    dh = ATT_HEAD_DIM
    return jnp.concatenate([blk[:, g * dh:(g + 1) * dh] for g in range(ATT_GROUP)], axis=0)


def _unstack_heads(st):
    return jnp.concatenate([st[g * BLOCK:(g + 1) * BLOCK] for g in range(ATT_GROUP)], axis=1)


def _rms_rows(x):
    rstd = lax.rsqrt(jnp.mean(x * x, axis=-1, keepdims=True) + EPS)
    return x * rstd, rstd


def _edge_ok(n, nb):
    colid = lax.broadcasted_iota(jnp.int32, (ATT_GROUP * BLOCK, 3 * BLOCK), 1)
    return jnp.logical_and(jnp.logical_or(colid >= BLOCK, n > 0), jnp.logical_or(colid < 2 * BLOCK, n < nb - 1))


def _sink_column(sink_ref, j=0):
    heads = range(j * ATT_GROUP, (j + 1) * ATT_GROUP)
    return jnp.concatenate([jnp.broadcast_to(sink_ref[h][:, 0:1], (BLOCK, 1)) for h in heads], axis=0)


def _attn_fwd(z, q_g, k_g, sink, bias, name):
    S = z.shape[0]
    nb = S // BLOCK
    G, dh, KV = ATT_GROUP, ATT_HEAD_DIM, ATT_KV_HEADS
    scale = 1.0 / math.sqrt(dh)

    def body(q_ref, kv0, kv1, kv2, qg_ref, kg_ref, sink_ref, bias_ref, o_ref):
        n = pl.program_id(0)
        edge_ok = _edge_ok(n, nb)
        cat = jnp.concatenate([kv0[...], kv1[...], kv2[...]], axis=0)
        qblk = q_ref[...]
        kn = [(_rms_rows(cat[:, j * dh:(j + 1) * dh])[0] * kg_ref[...]).astype(BF16) for j in range(KV)]
        vb = [cat[:, (KV + j) * dh:(KV + j + 1) * dh].astype(BF16) for j in range(KV)]
        qn = [(_rms_rows(_stack_heads(qblk[:, j * GROUP_WIDTH:(j + 1) * GROUP_WIDTH]))[0] * (qg_ref[...] * scale)).astype(BF16)
              for j in range(KV)]
        s = [_nt(qn[j], kn[j]) + bias_ref[j * G:(j + 1) * G].reshape(G * BLOCK, 3 * BLOCK) for j in range(KV)]
        s = [jnp.where(edge_ok, sj, NEG) for sj in s]
        sinks = [_sink_column(sink_ref, j) for j in range(KV)]
        m = [jnp.maximum(jnp.max(s[j], axis=-1, keepdims=True), sinks[j]) for j in range(KV)]
        e = [jnp.exp(s[j] - m[j]) for j in range(KV)]
        den = [jnp.sum(e[j], axis=-1, keepdims=True) + jnp.exp(sinks[j] - m[j]) for j in range(KV)]
        o = [_nn(e[j].astype(BF16), vb[j]) * (1.0 / den[j]) for j in range(KV)]
        o_ref[...] = jnp.concatenate([_unstack_heads(oj) for oj in o], axis=1).astype(o_ref.dtype)

    def kv(shift):
        return pl.BlockSpec((BLOCK, 2 * KV_WIDTH), lambda n: (jnp.clip(n + shift, 0, nb - 1), KV_COL // (2 * KV_WIDTH)))

    gain = pl.BlockSpec((1, dh), lambda n: (0, 0))
    return pl.pallas_call(
        body, name=name, grid=(nb,),
        in_specs=[pl.BlockSpec((BLOCK, ATT_WIDTH), lambda n: (n, Q_COL // ATT_WIDTH)), kv(-1), kv(0), kv(1), gain, gain,
                  pl.BlockSpec((ATT_Q_HEADS, 1, BLOCK), lambda n: (0, 0, 0)),
                  pl.BlockSpec((ATT_Q_HEADS, BLOCK, 3 * BLOCK), lambda n: (0, 0, 0))],
        out_specs=pl.BlockSpec((BLOCK, ATT_WIDTH), lambda n: (n, 0)),
        out_shape=jax.ShapeDtypeStruct((S, ATT_WIDTH), BF16), compiler_params=_params("parallel"),
    )(z, z, z, z, q_g, k_g, sink, bias)


def _attn_bwd(z, q_g, k_g, sink, bias, do, name):
    S = z.shape[0]
    nb = S // BLOCK
    G, dh, KV = ATT_GROUP, ATT_HEAD_DIM, ATT_KV_HEADS
    scale = 1.0 / math.sqrt(dh)
    both = range(KV)
    bias_t = bias.reshape(KV, G, BLOCK, 3 * BLOCK).transpose(0, 3, 1, 2).reshape(KV, 3 * BLOCK, G * BLOCK)

    def body(q_ref, kv0, kv1, kv2, qg_ref, kg_ref, sink_ref, bias_ref, do_ref,
             dq_ref, dkw_ref, dvw_ref, ds_ref, dsink_ref, dqg_ref):
        n = pl.program_id(0)

        @pl.when(n == 0)
        def _():
            ds_ref[...] = jnp.zeros_like(ds_ref)
            dsink_ref[...] = jnp.zeros_like(dsink_ref)
            dqg_ref[...] = jnp.zeros_like(dqg_ref)

        rowid = lax.broadcasted_iota(jnp.int32, (3 * BLOCK, G * BLOCK), 0)
        edge_ok = jnp.logical_and(jnp.logical_or(rowid >= BLOCK, n > 0), jnp.logical_or(rowid < 2 * BLOCK, n < nb - 1))
        qg = qg_ref[...]
        cat = jnp.concatenate([kv0[...], kv1[...], kv2[...]], axis=0)
        qblk, doblk = q_ref[...], do_ref[...]
        kn = [(_rms_rows(cat[:, j * dh:(j + 1) * dh])[0] * kg_ref[...]).astype(BF16) for j in both]
        vb = [cat[:, (KV + j) * dh:(KV + j + 1) * dh].astype(BF16) for j in both]
        norm = [_rms_rows(_stack_heads(qblk[:, j * GROUP_WIDTH:(j + 1) * GROUP_WIDTH])) for j in both]
        qn = [(norm[j][0] * (qg * scale)).astype(BF16) for j in both]
        do_b = [_stack_heads(doblk[:, j * GROUP_WIDTH:(j + 1) * GROUP_WIDTH]).astype(BF16) for j in both]
        s = [_nt(kn[j], qn[j]) + bias_ref[j] for j in both]
        dp = [_nt(vb[j], do_b[j]) for j in both]
        s = [jnp.where(edge_ok, sj, NEG) for sj in s]
        sinks = [jnp.concatenate([sink_ref[j * G + g] for g in range(G)], axis=1) for j in both]
        m = [jnp.maximum(jnp.max(s[j], axis=0, keepdims=True), sinks[j]) for j in both]
        e = [jnp.exp(s[j] - m[j]) for j in both]
        e_sink = [jnp.exp(sinks[j] - m[j]) for j in both]
        inv = [1.0 / (jnp.sum(e[j], axis=0, keepdims=True) + e_sink[j]) for j in both]
        p = [e[j] * inv[j] for j in both]
        delta = [jnp.sum(p[j] * dp[j], axis=0, keepdims=True) for j in both]
        ds = [p[j] * (dp[j] - delta[j]) for j in both]
        ds_b = [dsj.astype(BF16) for dsj in ds]
        dqn = [_tn(kn[j], ds_b[j]).T * scale for j in both]
        for j in both:
            dvw_ref[j] = _nn(p[j].astype(BF16), do_b[j])
            dkw_ref[j] = _nn(ds_b[j], qn[j])
        for j in both:
            ds_ref[j] += ds[j]
            sink_term = e_sink[j] * inv[j] * delta[j]
            for g in range(G):
                dsink_ref[j * G + g] += (jnp.zeros((1, BLOCK), F32)
                                         - jnp.sum(sink_term[:, g * BLOCK:(g + 1) * BLOCK], axis=1, keepdims=True))
        dq = []
        for j in both:
            qhat, rstd = norm[j]
            dqg_ref[j] += jnp.sum(dqn[j] * qhat, axis=0, keepdims=True)
            dqh = dqn[j] * qg
            dq.append(_unstack_heads(rstd * (dqh - qhat * jnp.mean(dqh * qhat, axis=-1, keepdims=True))))
        dq_ref[...] = jnp.concatenate(dq, axis=1).astype(dq_ref.dtype)

    def kv(shift):
        return pl.BlockSpec((BLOCK, 2 * KV_WIDTH), lambda n: (jnp.clip(n + shift, 0, nb - 1), KV_COL // (2 * KV_WIDTH)))

    gain = pl.BlockSpec((1, dh), lambda n: (0, 0))
    sink_spec = pl.BlockSpec((ATT_Q_HEADS, 1, BLOCK), lambda n: (0, 0, 0))
    bias_spec = pl.BlockSpec((KV, 3 * BLOCK, G * BLOCK), lambda n: (0, 0, 0))
    win = pl.BlockSpec((KV, None, 3 * BLOCK, dh), lambda n: (0, n, 0, 0))
    wshape = jax.ShapeDtypeStruct((KV, nb, 3 * BLOCK, dh), F32)
    return pl.pallas_call(
        body, name=name, grid=(nb,),
        in_specs=[pl.BlockSpec((BLOCK, ATT_WIDTH), lambda n: (n, Q_COL // ATT_WIDTH)), kv(-1), kv(0), kv(1), gain, gain,
                  sink_spec, bias_spec, pl.BlockSpec((BLOCK, ATT_WIDTH), lambda n: (n, HG_WIDTH // ATT_WIDTH))],
        out_specs=[pl.BlockSpec((BLOCK, ATT_WIDTH), lambda n: (n, 0)), win, win, bias_spec, sink_spec,
                   pl.BlockSpec((KV, 1, dh), lambda n: (0, 0, 0))],
        out_shape=[jax.ShapeDtypeStruct((S, ATT_WIDTH), BF16), wshape, wshape,
                   jax.ShapeDtypeStruct((KV, 3 * BLOCK, G * BLOCK), F32),
                   jax.ShapeDtypeStruct((ATT_Q_HEADS, 1, BLOCK), F32),
                   jax.ShapeDtypeStruct((KV, 1, dh), F32)],
        compiler_params=_params("arbitrary"),
    )(z, z, z, z, q_g, k_g, sink, bias_t, do)


def _attn_kv_reduce(dkw, dvw, z, k_g, name):
    S = z.shape[0]
    nb = S // BLOCK
    dh = ATT_HEAD_DIM
    kb = min(8, nb)
    steps = nb // kb

    def body(a_lo, a, a_hi, b_lo, b, b_hi, kv_ref, kg_ref, dkv_ref, dkg_ref):
        n = pl.program_id(0)

        @pl.when(n == 0)
        def _():
            dkg_ref[...] = jnp.zeros_like(dkg_ref)

        lo = jnp.where(n > 0, 1.0, 0.0)
        hi = jnp.where(n < steps - 1, 1.0, 0.0)

        def overlap_add(w, w_lo, w_hi, j, i):
            before = lo * w_lo[j] if i == 0 else w[j, i - 1, 2 * BLOCK:3 * BLOCK, :]
            after = hi * w_hi[j] if i == kb - 1 else w[j, i + 1, 0:BLOCK, :]
            return w[j, i, BLOCK:2 * BLOCK, :] + before + after

        dkg = [jnp.zeros((1, dh), F32) for _ in range(ATT_KV_HEADS)]
        for i in range(kb):
            rows = slice(i * BLOCK, (i + 1) * BLOCK)
            dks, dvs = [], []
            for j in range(ATT_KV_HEADS):
                dkn = overlap_add(a, a_lo, a_hi, j, i)
                dvs.append(overlap_add(b, b_lo, b_hi, j, i))
                khat, rstd = _rms_rows(kv_ref[rows, j * dh:(j + 1) * dh])
                dkg[j] = dkg[j] + jnp.sum(dkn * khat, axis=0, keepdims=True)
                dkh = dkn * kg_ref[...]
                dks.append(rstd * (dkh - khat * jnp.mean(dkh * khat, axis=-1, keepdims=True)))
            dkv_ref[rows, :] = jnp.concatenate(dks + dvs, axis=1).astype(dkv_ref.dtype)
        for j in range(ATT_KV_HEADS):
            dkg_ref[j] += dkg[j]

    main = pl.BlockSpec((ATT_KV_HEADS, kb, 3 * BLOCK, dh), lambda n: (0, n, 0, 0))
    halo_lo = pl.BlockSpec((ATT_KV_HEADS, None, BLOCK, dh), lambda n: (0, jnp.maximum(n * kb - 1, 0), 2, 0))
    halo_hi = pl.BlockSpec((ATT_KV_HEADS, None, BLOCK, dh), lambda n: (0, jnp.minimum(n * kb + kb, nb - 1), 0, 0))
    return pl.pallas_call(
        body, name=name, grid=(steps,),
        in_specs=[halo_lo, main, halo_hi, halo_lo, main, halo_hi,
                  pl.BlockSpec((kb * BLOCK, 2 * KV_WIDTH), lambda n: (n, KV_COL // (2 * KV_WIDTH))),
                  pl.BlockSpec((1, dh), lambda n: (0, 0))],
        out_specs=[pl.BlockSpec((kb * BLOCK, 2 * KV_WIDTH), lambda n: (n, 0)),
                   pl.BlockSpec((ATT_KV_HEADS, 1, dh), lambda n: (0, 0, 0))],
        out_shape=[jax.ShapeDtypeStruct((S, 2 * KV_WIDTH), BF16), jax.ShapeDtypeStruct((ATT_KV_HEADS, 1, dh), F32)],
        compiler_params=_params("arbitrary"),
    )(dkw, dkw, dkw, dvw, dvw, dvw, z, k_g)


def _ada_wgrad(c_act_t, dm, name):
    D, nbatch = c_act_t.shape
    n = dm.shape[1]
    tr = 256

    def body(c_ref, dm_ref, o_ref):
        cv, dv = c_ref[...], dm_ref[...]
        acc = cv[:, 0:1] * dv[0:1, :]
        for b in range(1, nbatch):
            acc = acc + cv[:, b:b + 1] * dv[b:b + 1, :]
        o_ref[...] = acc

    return pl.pallas_call(
        body, name=name, grid=(D // tr,),
        in_specs=[pl.BlockSpec((tr, nbatch), lambda i: (i, 0)), pl.BlockSpec((nbatch, n), lambda i: (0, 0))],
        out_specs=pl.BlockSpec((tr, n), lambda i: (i, 0)), out_shape=jax.ShapeDtypeStruct((D, n), F32),
        compiler_params=_params("parallel"),
    )(c_act_t, dm)


def _to_bf16(w, name):
    R, Cn = w.shape
    tr = _row_tile(R)

    def body(w_ref, o_ref):
        o_ref[...] = w_ref[...].astype(BF16)

    blk = pl.BlockSpec((tr, Cn), lambda i: (i, 0))
    return pl.pallas_call(
        body, name=name, grid=(R // tr,), in_specs=[blk], out_specs=blk, out_shape=jax.ShapeDtypeStruct((R, Cn), BF16),
        compiler_params=_params("parallel"),
    )(w)


def _adamw(w, g, m, v, name):
    R, Cn = w.shape
    tr = R
    for cand in (256, 128, 64, 32, 16, 8):
        if R % cand == 0:
            tr = cand
            break

    def body(w_ref, g_ref, m_ref, v_ref, d_ref, nm_ref, nv_ref):
        gv = g_ref[...]
        m_new = ADAM_B1 * m_ref[...] + (1.0 - ADAM_B1) * gv
        v_new = ADAM_B2 * v_ref[...] + (1.0 - ADAM_B2) * (gv * gv)
        m_hat = m_new / (1.0 - ADAM_B1 ** ADAM_STEP)
        v_hat = v_new / (1.0 - ADAM_B2 ** ADAM_STEP)
        d_ref[...] = -ADAM_LR * (m_hat / (jnp.sqrt(v_hat) + ADAM_EPS) + ADAM_WD * w_ref[...])
        nm_ref[...] = m_new
        nv_ref[...] = v_new

    blk = pl.BlockSpec((tr, Cn), lambda i: (i, 0))
    shp = jax.ShapeDtypeStruct((R, Cn), F32)
    return pl.pallas_call(
        body, name=name, grid=(R // tr,), in_specs=[blk] * 4, out_specs=[blk] * 3, out_shape=[shp] * 3,
        compiler_params=_params("parallel"),
    )(w, g, m, v)


def _place():
    return lax.axis_index("x"), lax.axis_index("y"), lax.axis_index("c")


def _flip(place, k):
    x, y, c = place
    return (1 - x if k & 4 else x, 1 - y if k & 2 else y, 1 - c if k & 1 else c)


def _dev_index(place):
    x, y, c = place
    return 4 * x + 2 * y + c


def _chip_index(place):
    return 2 * place[0] + place[1]


def _gather8(x_ref, out_ref, send_sems, recv_sems, local_sem):
    me = _place()
    mine = pltpu.make_async_copy(x_ref, out_ref.at[_dev_index(me)], local_sem)
    mine.start()

    def copy(k, origin, to):
        return pltpu.make_async_remote_copy(
            src_ref=x_ref, dst_ref=out_ref.at[_dev_index(origin)], send_sem=send_sems.at[k - 1],
            recv_sem=recv_sems.at[k - 1], device_id=to, device_id_type=MESH)

    sends = [copy(k, me, _flip(me, k)) for k in range(1, 8)]
    for cp in sends:
        cp.start()
    for k in range(1, 8):
        copy(k, _flip(me, k), me).wait_recv()
    for cp in sends:
        cp.wait_send()
    mine.wait()


def _allgather8(x, name, reduce=False):
    R, Cn = x.shape

    def body(x_ref, *rest):
        if reduce:
            out_ref, sum_ref, send_sems, recv_sems, local_sem = rest
        else:
            out_ref, send_sems, recv_sems, local_sem = rest
        _gather8(x_ref, out_ref, send_sems, recv_sems, local_sem)
        if reduce:
            acc = out_ref[0]
            for i in range(1, 8):
                acc = acc + out_ref[i]
            sum_ref[...] = acc

    vm = pl.BlockSpec(memory_space=pltpu.VMEM)
    outs = [jax.ShapeDtypeStruct((8, R, Cn), F32)] + ([jax.ShapeDtypeStruct((R, Cn), F32)] if reduce else [])
    res = pl.pallas_call(
        body, name=name, in_specs=[vm], out_specs=[vm] * len(outs), out_shape=outs,
        scratch_shapes=[pltpu.SemaphoreType.DMA((7,)), pltpu.SemaphoreType.DMA((7,)), pltpu.SemaphoreType.DMA],
    )(x)
    return res if reduce else res[0]


def _prologue(small, w_ada, b_ada, w_shard, to_cast, name):
    R, Cn = small.shape
    n_mod = w_ada.shape[1]
    n_w = len(to_cast)
    big = _gather_over_ici([w_shard])

    def body(*refs):
        (small_ref, wada_ref, b_ref, shard_ref), refs = refs[:4], refs[4:]
        wide_refs, refs = refs[:n_w], refs[n_w:]
        (small_all_ref, mods_all_ref, gathered_ref), refs = refs[:3], refs[3:]
        narrow_refs, refs = refs[:n_w], refs[n_w:]
        (mods_ref, send1, recv1, send2, recv2, local_sems), refs = refs[:6], refs[6:]
        wide_bufs, narrow_bufs, (load_sems, store_sems, big_send, big_recv) = refs[:n_w], refs[n_w:2 * n_w], refs[2 * n_w:]
        big.start([shard_ref], [gathered_ref], big_send, big_recv)
        loads = [pltpu.make_async_copy(w, buf, load_sems.at[i]) for i, (w, buf) in enumerate(zip(wide_refs, wide_bufs))]
        for cp in loads:
            cp.start()
        stores = []
        for i, cp in enumerate(loads):
            cp.wait()
            rows = wide_bufs[i].shape[0]
            tr = _row_tile(rows)

            def cast(j, carry, i=i, tr=tr):
                rs = pl.ds(pl.multiple_of(j * tr, tr), tr)
                narrow_bufs[i][rs, :] = wide_bufs[i][rs, :].astype(BF16)
                return carry

            lax.fori_loop(0, rows // tr, cast, 0)
            stores.append(pltpu.make_async_copy(narrow_bufs[i], narrow_refs[i], store_sems.at[i]))
            stores[-1].start()
        _gather8(small_ref, small_all_ref, send1, recv1, local_sems.at[0])
        c_all = jnp.concatenate([small_all_ref[d, 0:1, :] for d in range(8)], axis=0)
        c_act = c_all * _sigmoid(c_all)
        mods_ref[...] = _nn(c_act, wada_ref[...], precision=lax.Precision.HIGHEST) + b_ref[...]
        _gather8(mods_ref, mods_all_ref, send2, recv2, local_sems.at[1])
        for cp in stores:
            cp.wait()
        big.finish([shard_ref], [gathered_ref], big_send, big_recv)

    vm = pl.BlockSpec(memory_space=pltpu.VMEM)
    seven = pltpu.SemaphoreType.DMA((7,))
    res = pl.pallas_call(
        body, name=name, in_specs=[vm, vm, vm, ANY] + [ANY] * n_w, out_specs=[vm, vm, ANY] + [ANY] * n_w,
        out_shape=[jax.ShapeDtypeStruct((8, R, Cn), F32), jax.ShapeDtypeStruct((8, 8, n_mod), F32)] + big.out_shapes
        + [jax.ShapeDtypeStruct(w.shape, BF16) for w in to_cast],
        scratch_shapes=[pltpu.VMEM((8, n_mod), F32), seven, seven, seven, seven, pltpu.SemaphoreType.DMA((2,))]
        + [pltpu.VMEM(w.shape, F32) for w in to_cast] + [pltpu.VMEM(w.shape, BF16) for w in to_cast]
        + [pltpu.SemaphoreType.DMA((n_w,)), pltpu.SemaphoreType.DMA((n_w,))] + big.sem_shapes(),
        compiler_params=pltpu.CompilerParams(vmem_limit_bytes=VMEM_LIMIT_BYTES),
    )(small, w_ada, b_ada, w_shard, *to_cast)
    return res[0], res[1], res[2], list(res[3:])


def _symmetric_plan(copies):
    def plan(in_refs, out_refs):
        sends = [(src, dst, i, to) for i, (src, dst, to) in enumerate(copies(in_refs, out_refs))]
        return sends, [(dst, i) for _, dst, i, _ in sends]
    return plan


def _halves_exchange(grads):
    def copies(in_refs, out_refs):
        me = _place()
        return [(g.at[kk, 1 - me[2]], got.at[kk], _flip(me, 1)) for g, got in zip(in_refs, out_refs) for kk in range(N_CHIPS)]

    return _Exchange(grads, [jax.ShapeDtypeStruct((N_CHIPS,) + g.shape[2:], g.dtype) for g in grads],
                     N_CHIPS * len(grads), _symmetric_plan(copies))


def _chips_exchange(parts):
    def copies(in_refs, out_refs):
        me = _place()
        return [(p.at[_chip_index(_flip(me, 2 * j))], got.at[j - 1], _flip(me, 2 * j))
                for p, got in zip(in_refs, out_refs) for j in (1, 2, 3)]

    return _Exchange(parts, [jax.ShapeDtypeStruct((3,) + p.shape[1:], p.dtype) for p in parts], 3 * len(parts),
                     _symmetric_plan(copies))


def _siblings_exchange(halves):
    def copies(in_refs, out_refs):
        sibling = _flip(_place(), 1)
        return [(h, got, sibling) for h, got in zip(in_refs, out_refs)]

    return _Exchange(halves, [jax.ShapeDtypeStruct(h.shape, h.dtype) for h in halves], len(halves), _symmetric_plan(copies))


def _ici_gather_plan(n, base=0):
    def plan(in_refs, out_refs):
        me = _place()
        c = me[2]
        sends, lands = [], []
        for a, (w, out) in enumerate(zip(in_refs[:n], out_refs)):
            for j in (1, 2, 3):
                i = base + 3 * a + j - 1
                sends.append((w.at[c], out.at[_chip_index(me), c], i, _flip(me, 2 * j)))
                lands.append((out.at[_chip_index(_flip(me, 2 * j)), c], i))
        return sends, lands
    return plan


def _d2d_gather_plan(n, base=0):
    def plan(in_refs, out_refs):
        me = _place()
        c = me[2]
        sibling = _flip(me, 1)
        mine = _chip_index(me)
        sends, lands = [], []
        for a, (w, out) in enumerate(zip(in_refs[:n], out_refs)):
            moves = [(w.at[c], (mine, c)), (w.at[1 - c], (mine, 1 - c))]
            moves += [(out.at[_chip_index(_flip(me, 2 * j)), c], (_chip_index(_flip(me, 2 * j)), c)) for j in (1, 2, 3)]
            for k, (src, (chip, half)) in enumerate(moves):
                sends.append((src, out.at[chip, half], base + 5 * a + k, sibling))
            blocks = [(mine, 1 - c), (mine, c)] + [(_chip_index(_flip(me, 2 * j)), 1 - c) for j in (1, 2, 3)]
            lands += [(out.at[chip, half], base + 5 * a + k) for k, (chip, half) in enumerate(blocks)]
        return sends, lands
    return plan


def _gathered_shapes(shards):
    return [jax.ShapeDtypeStruct((N_CHIPS,) + s.shape, s.dtype) for s in shards]


def _gather_over_ici(shards):
    return _Exchange(shards, _gathered_shapes(shards), 3 * len(shards), _ici_gather_plan(len(shards)))


def _gather_over_d2d(shards, gathered):
    n = len(shards)
    return _Exchange(list(shards) + list(gathered), [jax.ShapeDtypeStruct(g.shape, g.dtype) for g in gathered], 5 * n,
                     _d2d_gather_plan(n), aliases={n + a: a for a in range(n)})


def _gather_in_one(shards):
    n = len(shards)
    return _Exchange(shards, _gathered_shapes(shards), 8 * n, _ici_gather_plan(n), then=_d2d_gather_plan(n, base=3 * n))


def _row_tile(rows):
    for cand in (256, 176, 128, 64, 32, 16, 8):
        if rows % cand == 0:
            return cand
    return rows


def _pair_sum(core, grad, theirs, name):
    N, _, R, Cn = grad.shape
    tr = R

    def body(core_ref, g_ref, t_ref, o_ref, ob_ref):
        s = g_ref[...] + t_ref[...]
        o_ref[...] = s
        ob_ref[...] = s.astype(BF16)

    out = pl.BlockSpec((None, tr, Cn), lambda k, i, core_ref: (k, i, 0))
    return pl.pallas_call(
        body, name=name,
        grid_spec=pltpu.PrefetchScalarGridSpec(
            num_scalar_prefetch=1, grid=(N, R // tr),
            in_specs=[pl.BlockSpec((None, None, tr, Cn), lambda k, i, core_ref: (k, core_ref[0], i, 0)),
                      pl.BlockSpec((None, tr, Cn), lambda k, i, core_ref: (k, i, 0))],
            out_specs=[out, out]),
        out_shape=[jax.ShapeDtypeStruct((N, R, Cn), F32), jax.ShapeDtypeStruct((N, R, Cn), BF16)],
        compiler_params=_params("parallel", "parallel"),
    )(core, grad, theirs)


def _chip_sum(chip, parts, landed, name):
    _, R, Cn = parts.shape
    tr = R

    def body(chip_ref, p_ref, l_ref, o_ref):
        o_ref[...] = ((p_ref[...] + l_ref[0].astype(F32)) + l_ref[1].astype(F32)) + l_ref[2].astype(F32)

    return pl.pallas_call(
        body, name=name,
        grid_spec=pltpu.PrefetchScalarGridSpec(
            num_scalar_prefetch=1, grid=(R // tr,),
            in_specs=[pl.BlockSpec((None, tr, Cn), lambda i, chip_ref: (chip_ref[0], i, 0)),
                      pl.BlockSpec((3, tr, Cn), lambda i, chip_ref: (0, i, 0))],
            out_specs=pl.BlockSpec((tr, Cn), lambda i, chip_ref: (i, 0))),
        out_shape=jax.ShapeDtypeStruct((R, Cn), F32), compiler_params=_params("parallel"),
    )(chip, parts, landed)


def _pair_sums(core, grads, theirs, tag):
    return [_pair_sum(core, g, t, f"{tag}_pair_sum_{i}") for i, (g, t) in enumerate(zip(grads, theirs))]


def _chip_sums(chip, parts, landed, tag):
    return [_chip_sum(chip, p[0], l, f"{tag}_chip_sum_{i}") for i, (p, l) in enumerate(zip(parts, landed))]


def _by_chip_rows(g):
    return g.reshape(N_CHIPS, 2, g.shape[0] // (2 * N_CHIPS), g.shape[1])


def _by_chip_cols(g):
    return g.reshape(N_CHIPS, 2, g.shape[1] // 2, g.shape[2])


def _adamw_halves(core, w, g_mine, g_theirs, m, v, name):
    R2, Cn = w.shape
    r = R2 // 2
    tr = _row_tile(r)
    nt = r // tr

    def body(core_ref, w_ref, gm_ref, gt_ref, m_ref, v_ref, g_ref, d_ref, nm_ref, nv_ref):
        gv = jnp.where(pl.program_id(0) == core_ref[0], gm_ref[...], gt_ref[...])
        g_ref[...] = gv
        m_new = ADAM_B1 * m_ref[...] + (1.0 - ADAM_B1) * gv
        v_new = ADAM_B2 * v_ref[...] + (1.0 - ADAM_B2) * (gv * gv)
        m_hat = m_new / (1.0 - ADAM_B1 ** ADAM_STEP)
        v_hat = v_new / (1.0 - ADAM_B2 ** ADAM_STEP)
        d_ref[...] = -ADAM_LR * (m_hat / (jnp.sqrt(v_hat) + ADAM_EPS) + ADAM_WD * w_ref[...])
        nm_ref[...] = m_new
        nv_ref[...] = v_new

    full = pl.BlockSpec((tr, Cn), lambda hf, i, core_ref: (hf * nt + i, 0))
    half = pl.BlockSpec((tr, Cn), lambda hf, i, core_ref: (i, 0))
    shp = jax.ShapeDtypeStruct((R2, Cn), F32)
    return pl.pallas_call(
        body, name=name,
        grid_spec=pltpu.PrefetchScalarGridSpec(
            num_scalar_prefetch=1, grid=(2, nt), in_specs=[full, half, half, full, full], out_specs=[full] * 4),
        out_shape=[shp] * 4, compiler_params=_params("parallel", "parallel"),
    )(core, w, g_mine, g_theirs, m, v)


def _pad_row(v, width):
    v = v.reshape(1, -1)
    return jnp.pad(v, ((0, 0), (0, width - v.shape[1])))


def _ffn1_forward(x, ng, shift, scale, gate, w_in_shard, w_in_partly, w_out_shard, gather, next_norm):
    (h,), (w_in4,) = _rmsmod_fwd(x, ng, shift, scale, "ffn1_norm", exchange=_gather_over_d2d([w_in_shard], [w_in_partly]))
    w_in4 = w_in4.reshape(N_CHIPS, D_MODEL, FF_SHARD)
    (zg, zu, a), (partly, (w_out4,)) = _ffn_in_fwd(
        h, w_in4, "ffn1_in", exchange=[_gather_over_ici(gather), _gather_in_one([w_out_shard])])
    w_out = w_out4.reshape(D_FF, D_MODEL)
    (x_new, f, h_next), gathered = _proj_out_fwd([a], w_out, x, gate, 0.5, "ffn1_out", next_norm=next_norm,
                                                 exchange=_gather_over_d2d(gather, partly))
    return x_new, (h, zg, zu, a, f), w_in4, w_out, gathered, h_next


def _ffn_backward(df, saved, w_in4, w_out, core, chip, tag, riding=None, norm=None, in_first=False):
    h, zg, zu, a = saved[:4]
    rode = None
    if riding:
        (dzg, dzu), rode = _dact_bwd(df, w_out, zg, zu, f"{tag}_dact", exchange=riding)
    else:
        dzg, dzu = _dact_bwd(df, w_out, zg, zu, f"{tag}_dact")

    def dw_out(exchange=None):
        outs = _wgrad(a, [df], df.shape[1], f"{tag}_dw_out", exchange=exchange)
        (dw,), landed = outs if exchange else (outs, None)
        return [_by_chip_rows(dw.reshape(a.shape[1], df.shape[1]))], landed

    def dw_in(exchange=None):
        outs = _wgrad(h, [dzg, dzu], FF_SHARD, f"{tag}_dw_in", exchange=exchange)
        (dw,), landed = outs if exchange else (outs, None)
        return [_by_chip_cols(dw.reshape(N_CHIPS, h.shape[1], FF_SHARD))], landed

    (first, tag_1), (second, tag_2) = ((dw_in, "in"), (dw_out, "out"))[::1 if in_first else -1]
    g_1, _ = first()
    g_2, theirs_1 = second(_halves_exchange(g_1))
    parts_1 = _pair_sums(core, g_1, theirs_1, f"{tag}_{tag_1}")
    dh_outs, (theirs_2, landed_1) = _ffn_in_dgrad(
        dzg, dzu, w_in4, f"{tag}_dh", norm=norm, exchange=[_halves_exchange(g_2), _chips_exchange([parts_1[0][1]])])
    parts_2 = _pair_sums(core, g_2, theirs_2, f"{tag}_{tag_2}")
    return dh_outs, parts_2, _chip_sums(chip, parts_1, landed_1, f"{tag}_{tag_1}"), rode


def kernel(x, c, w_ada, b_ada, norm_g, w_ffn1_in, w_ffn1_out, w_ffn2_in, w_ffn2_out, w_mix_in, w_mix_out, hgrn_lb, hgrn_norm_g, qk_norm_g, attn_sink, rel_bias, loss_target, m_w_ada, m_b_ada, m_norm_g, m_w_ffn1_in, m_w_ffn1_out, m_w_ffn2_in, m_w_ffn2_out, m_w_mix_in, m_w_mix_out, m_hgrn_lb, m_hgrn_norm_g, m_qk_norm_g, m_attn_sink, m_rel_bias, v_w_ada, v_b_ada, v_norm_g, v_w_ffn1_in, v_w_ffn1_out, v_w_ffn2_in, v_w_ffn2_out, v_w_mix_in, v_w_mix_out, v_hgrn_lb, v_hgrn_norm_g, v_qk_norm_g, v_attn_sink, v_rel_bias):
    D = D_MODEL
    S = x.shape[1]
    place = (lax.axis_index("x"), lax.axis_index("y"), lax.axis_index("c"))
    me, my_chip = _dev_index(place), _chip_index(place)
    x0 = x[0]
    target = loss_target[0]

    core_arr = jnp.reshape(place[2], (1,)).astype(jnp.int32)
    chip_arr = jnp.reshape(my_chip, (1,)).astype(jnp.int32)

    def halves(w):
        return w.reshape(2, w.shape[0] // 2, w.shape[1])

    small = jnp.concatenate([_pad_row(c, D), _pad_row(norm_g, D), _pad_row(hgrn_lb, D), jnp.zeros((5, D), F32)], axis=0)
    n_ada = w_ada.shape[2]
    b_mine = lax.dynamic_slice_in_dim(b_ada, my_chip * n_ada, n_ada, axis=1)
    w1_in_shard = halves(_to_bf16(w_ffn1_in[0], "w_ffn1_in_to_bf16"))
    small_all, mods_parts, w1_in_partly, shards = _prologue(
        small, w_ada[0], b_mine, w1_in_shard, [w_ffn1_out[0], w_mix_in[0], w_mix_out[0], w_ffn2_in[0], w_ffn2_out[0]], "prologue")
    w1_out_shard, mix_shards, ffn2_shards = halves(shards[0]), [halves(w) for w in shards[1:3]], [halves(w) for w in shards[3:5]]
    c_all = small_all[:, 0, :]
    by_chip = small_all[0::2]
    norm_g_full = by_chip[:, 1, :3 * 256].reshape(N_CHIPS, 3, 256).transpose(1, 0, 2).reshape(3, D)
    lb_raw = by_chip[:, 2, :2 * 2 * 128].reshape(N_CHIPS, 2, 2, 128).transpose(1, 2, 0, 3).reshape(2, 2, HG_WIDTH)
    lb_logit = lb_raw[:, 0, :] - lb_raw[:, 1, :]
    lb = jax.nn.sigmoid(lb_logit)
    one_minus_lb = jax.nn.sigmoid(-lb_logit)
    lb_f = jnp.stack([lb[0], one_minus_lb[0]])
    lb_b = jnp.stack([lb[1], one_minus_lb[1]])

    c_act_all = c_all * jax.nn.sigmoid(c_all)
    mods_all = mods_parts[0::2].transpose(1, 0, 2).reshape(8, N_MOD * D)
    mods = lax.dynamic_slice_in_dim(mods_all, me, 1, axis=0)
    sh1, sc1, g1, sh2, sc2, g2, sh3, sc3, g3 = [mods[:, i * D:(i + 1) * D] for i in range(N_MOD)]

    x1, saved1, w1_in, w1_out, gathered, h2 = _ffn1_forward(
        x0, norm_g_full[0:1], sh1, sc1, g1, w1_in_shard, w1_in_partly, w1_out_shard, mix_shards, (norm_g_full[1:2], sh2, sc2))
    wm_in = gathered[0].reshape(N_CHIPS, D, D_IN // N_CHIPS).transpose(1, 0, 2).reshape(D, D_IN)
    wm_out = gathered[1].reshape(D, D)

    z = _matmul_nn(h2, wm_in, F32, 256, "mix_in")
    (of, st_f, ob, st_b), gathered = _hgrn_fwd_both(z, (lb_f, lb_b), "hgrn_fwd", exchange=_gather_in_one(ffn2_shards))
    w2_in = gathered[0].reshape(N_CHIPS, D, FF_SHARD)
    w2_out = gathered[1].reshape(D_FF, D)
    o_h = _hgrn_post_fwd(of, ob, z, hgrn_norm_g, "hgrn_post")

    q_g, k_g = qk_norm_g[0, 0:1], qk_norm_g[0, 1:2]
    sink_b = jnp.broadcast_to(attn_sink.reshape(ATT_Q_HEADS, 1, 1), (ATT_Q_HEADS, 1, BLOCK))
    bias = _bias_table(rel_bias, "bias_table")
    o_a = _attn_fwd(z, q_g, k_g, sink_b, bias, "attn_fwd")
    x2, mixed, h3 = _proj_out_fwd([o_h, o_a], wm_out, x1, g2, 1.0, "mix_out", next_norm=(norm_g_full[2:3], sh3, sc3))

    zg3, zu3, a3 = _ffn_in_fwd(h3, w2_in, "ffn2_in")
    dx3, df3, dg3, sq_cols = _proj_out_loss(a3, w2_out, x2, g3, 0.5, target, "ffn2_out_loss")
    loss_mine = 0.5 * jnp.sum(sq_cols) / D

    (dx2, dsh3, dsc3, dng3, dmixed, dg2), parts2, mine2_out, _ = _ffn_backward(
        df3, (h3, zg3, zu3, a3), w2_in, w2_out, core_arr, chip_arr, "ffn2",
        norm=_NormBwd(x2, norm_g_full[2:3], sc3, dx3, below=(mixed, g2, 1.0)))

    (do_cat,) = _matmul_nt([dmixed], wm_out, ROW_TILE, "mix_out_dgrad")
    dwm_out = _wgrad_rows([o_h, o_a], dmixed, "mix_out_dw").reshape(D, D)

    do_sum, dgr, d_hnorm = _hgrn_post_bwd(do_cat, of, ob, z, hgrn_norm_g, "hgrn_post_bwd")
    (dq_f, dff, dv_f, doml_f), landed2 = _hgrn_bwd(z, lb_f, do_sum, st_f, 0, "hgrn_bwd_f",
                                                   exchange=_chips_exchange([p[1] for p in parts2]))
    mine2 = _chip_sums(chip_arr, parts2, landed2, "ffn2_in") + mine2_out
    (dhq, dfb, dhi, doml_b), theirs2 = _hgrn_bwd(z, lb_b, do_sum, st_b, 1, "hgrn_bwd_b", acc=(dq_f, dv_f),
                                                 exchange=_siblings_exchange(mine2))

    daq, dkw, dvw, ds_sum, dsink, dqg = _attn_bwd(z, q_g, k_g, sink_b, bias, do_cat, "attn_bwd")
    dkv, dkg = _attn_kv_reduce(dkw, dvw, z, k_g, "attn_kv_reduce")
    d_rel_bias = jnp.sum(_bias_grad(ds_sum, "bias_grad"), axis=-1).T
    dz = [dhq, dff, dfb, dhi, dgr, daq, dkv]
    dwm_in = _wgrad_pieces(h2, dz, 2 * KV_WIDTH, "mix_in_dw").transpose(1, 0, 2).reshape(D, D_IN)
    wide = D_IN // N_CHIPS
    grads_m = [_by_chip_cols(dwm_in.reshape(D, N_CHIPS, wide).transpose(1, 0, 2)), _by_chip_rows(dwm_out)]
    (dx1, dsh2, dsc2, dng2, df1, dg1), theirs_m = _matmul_nt(
        dz, wm_in, 256, "mix_in_dgrad", exchange=_halves_exchange(grads_m),
        norm=_NormBwd(x1, norm_g_full[1:2], sc2, dx2, below=(saved1[4], g1, 0.5)))
    parts_m = _pair_sums(core_arr, grads_m, theirs_m, "mix")

    (dh1,), parts1, mine1_in, landed_m = _ffn_backward(df1, saved1, w1_in, w1_out, core_arr, chip_arr, "ffn1",
                                                       riding=_chips_exchange([p[1] for p in parts_m]), in_first=True)
    mine_m = _chip_sums(chip_arr, parts_m, landed_m, "mix")
    (dx0, dsh1, dsc1, dng1), landed1 = _rmsmod_bwd(dh1, _NormBwd(x0, norm_g_full[0:1], sc1, dx1), "ffn1_norm_bwd",
                                                   exchange=_chips_exchange([p[1] for p in parts1]))
    mine1 = mine1_in + _chip_sums(chip_arr, parts1, landed1, "ffn1_out")
    theirs_1m = list(_run_exchange(_siblings_exchange(mine1 + mine_m), "siblings_exchange"))
    reduced = list(zip(mine1 + mine2 + mine_m, theirs_1m[:2] + list(theirs2) + theirs_1m[2:]))

    dlb = -jnp.concatenate([doml_f, doml_b], axis=0)
    dlb_raw = dlb * lb * one_minus_lb
    d_hgrn_lb = jnp.stack([dlb_raw, -dlb_raw], axis=1)
    d_qk = jnp.concatenate([jnp.sum(dqg, axis=0), jnp.sum(dkg, axis=0)], axis=0)
    dmods = jnp.concatenate([dsh1, dsc1, dg1, dsh2, dsc2, dg2, dsh3, dsc3, dg3], axis=0)
    packed = jnp.concatenate(
        [dmods, dng1, dng2, dng3, d_hgrn_lb.reshape(2, D), _pad_row(d_hnorm, D), _pad_row(d_qk, D),
         _pad_row(dsink[:, 0, 0], D), _pad_row(d_rel_bias, D), _pad_row(loss_mine, D)], axis=0)
    packed = jnp.pad(packed, ((0, 24 - packed.shape[0]), (0, 0)))
    packed_all, packed_sum = _allgather8(packed, "small_grads_allgather", reduce=True)
    dmods_all = packed_all[:, 0:N_MOD, :].reshape(8, N_MOD * D)
    g_b_ada = packed_sum[0:N_MOD].reshape(1, N_MOD * D)
    g_norm_full = packed_sum[9:12]
    g_norm_g = lax.dynamic_slice_in_dim(g_norm_full, my_chip * 256, 256, axis=1).reshape(1, 3, 256)
    g_hgrn_lb = lax.dynamic_slice_in_dim(packed_sum[12:14].reshape(2, 2, HG_WIDTH), my_chip * 128, 128, axis=2)
    g_hgrn_norm_g = packed_sum[14:15, :HG_WIDTH]
    g_qk_norm_g = packed_sum[15, :2 * ATT_HEAD_DIM].reshape(1, 2, ATT_HEAD_DIM)
    g_attn_sink = packed_sum[16:17, :ATT_Q_HEADS]
    g_rel_bias = packed_sum[17, :NUM_BUCKETS * ATT_Q_HEADS].reshape(NUM_BUCKETS, ATT_Q_HEADS)
    loss = packed_sum[18, 0]

    dm_mine = lax.dynamic_slice_in_dim(dmods_all, my_chip * n_ada, n_ada, axis=1)
    g_w_ada = _ada_wgrad(c_act_all.T, dm_mine, "ada_wgrad")[None]

    def big(w, g, m, v, name):
        d, nm, nv = _adamw(w[0], g[0], m[0], v[0], name)
        return d[None], nm[None], nv[None]

    def big_halves(w, g_pair, m, v, name):
        g, d, nm, nv = _adamw_halves(core_arr, w[0], g_pair[0], g_pair[1], m[0], v[0], name)
        return g[None], (d[None], nm[None], nv[None])

    g_w1_in, u_w1_in = big_halves(w_ffn1_in, reduced[0], m_w_ffn1_in, v_w_ffn1_in, "adamw_w_ffn1_in")
    g_w1_out, u_w1_out = big_halves(w_ffn1_out, reduced[1], m_w_ffn1_out, v_w_ffn1_out, "adamw_w_ffn1_out")
    g_w2_in, u_w2_in = big_halves(w_ffn2_in, reduced[2], m_w_ffn2_in, v_w_ffn2_in, "adamw_w_ffn2_in")
    g_w2_out, u_w2_out = big_halves(w_ffn2_out, reduced[3], m_w_ffn2_out, v_w_ffn2_out, "adamw_w_ffn2_out")
    g_wm_in, u_wm_in = big_halves(w_mix_in, reduced[4], m_w_mix_in, v_w_mix_in, "adamw_w_mix_in")
    g_wm_out, u_wm_out = big_halves(w_mix_out, reduced[5], m_w_mix_out, v_w_mix_out, "adamw_w_mix_out")

    smalls = [(b_ada, g_b_ada, m_b_ada, v_b_ada), (norm_g, g_norm_g, m_norm_g, v_norm_g), (hgrn_lb, g_hgrn_lb, m_hgrn_lb, v_hgrn_lb),
              (hgrn_norm_g, g_hgrn_norm_g, m_hgrn_norm_g, v_hgrn_norm_g), (qk_norm_g, g_qk_norm_g, m_qk_norm_g, v_qk_norm_g),
              (attn_sink, g_attn_sink, m_attn_sink, v_attn_sink), (rel_bias, g_rel_bias, m_rel_bias, v_rel_bias)]
    sizes = [t[0].size for t in smalls]
    total = sum(sizes)
    rows = -(-total // 128)
    rows = -(-rows // 8) * 8

    def pack(i):
        flat = jnp.concatenate([t[i].reshape(-1) for t in smalls])
        fill = 1.0 if i == 3 else 0.0
        return jnp.pad(flat, (0, rows * 128 - total), constant_values=fill).reshape(rows, 128)

    packed_out = _adamw(pack(0), pack(1), pack(2), pack(3), "adamw_small")

    def unpack(flat2d):
        flat = flat2d.reshape(-1)
        outs, off = [], 0
        for t, n in zip(smalls, sizes):
            outs.append(flat[off:off + n].reshape(t[0].shape))
            off += n
        return outs

    d_small, m_small, v_small = [unpack(t) for t in packed_out]

    upd = {
        "w_ada": big(w_ada, g_w_ada, m_w_ada, v_w_ada, "adamw_w_ada"),
        "w_ffn1_in": u_w1_in, "w_ffn1_out": u_w1_out, "w_ffn2_in": u_w2_in, "w_ffn2_out": u_w2_out,
        "w_mix_in": u_wm_in, "w_mix_out": u_wm_out,
    }
    small_names = ["b_ada", "norm_g", "hgrn_lb", "hgrn_norm_g", "qk_norm_g", "attn_sink", "rel_bias"]
    for i, nme in enumerate(small_names):
        upd[nme] = (d_small[i], m_small[i], v_small[i])
    grads = {
        "w_ada": g_w_ada, "b_ada": g_b_ada, "norm_g": g_norm_g, "w_ffn1_in": g_w1_in, "w_ffn1_out": g_w1_out,
        "w_ffn2_in": g_w2_in, "w_ffn2_out": g_w2_out, "w_mix_in": g_wm_in, "w_mix_out": g_wm_out, "hgrn_lb": g_hgrn_lb,
        "hgrn_norm_g": g_hgrn_norm_g, "qk_norm_g": g_qk_norm_g, "attn_sink": g_attn_sink, "rel_bias": g_rel_bias,
    }
    order = ["w_ada", "b_ada", "norm_g", "w_ffn1_in", "w_ffn1_out", "w_ffn2_in", "w_ffn2_out", "w_mix_in", "w_mix_out",
             "hgrn_lb", "hgrn_norm_g", "qk_norm_g", "attn_sink", "rel_bias"]
    return (loss, dx0[None], *[grads[k] for k in order], *[upd[k][0] for k in order], *[upd[k][1] for k in order],
            *[upd[k][2] for k in order])
```

```python
import functools
import math

import numpy as np
import jax
import jax.numpy as jnp
from jax import lax
from jax.experimental import pallas as pl
from jax.experimental.pallas import tpu as pltpu

F32, BF16 = jnp.float32, jnp.bfloat16

D_MODEL = 1024
D_FF = 2816
HG_HEADS, HG_DIM = 4, 128
HG_WIDTH = HG_HEADS * HG_DIM
ATT_Q_HEADS, ATT_KV_HEADS, ATT_HEAD_DIM = 8, 2, 64
ATT_GROUP = ATT_Q_HEADS // ATT_KV_HEADS
ATT_WIDTH = ATT_Q_HEADS * ATT_HEAD_DIM
KV_WIDTH = ATT_KV_HEADS * ATT_HEAD_DIM
WINDOW, BLOCK = 128, 128
NUM_BUCKETS, MAX_DISTANCE = 32, 128
N_MOD = 9
EPS = 1e-6
D_IN = 5 * HG_WIDTH + ATT_WIDTH + 2 * KV_WIDTH
ADAM_LR, ADAM_B1, ADAM_B2, ADAM_EPS, ADAM_WD, ADAM_STEP = 0.001, 0.9, 0.999, 1e-08, 0.01, 10

N_CHIPS = 4
FF_SHARD = 2 * D_FF // N_CHIPS
NEG = -1e30

VMEM_LIMIT_BYTES = 56 << 20
ROW_TILE = 512
HG_CHUNK = 16
HG_ROWS = 512

MESH = pl.DeviceIdType.MESH
ANY = pl.BlockSpec(memory_space=pl.ANY)


def _params(*sem):
    return pltpu.CompilerParams(dimension_semantics=sem, vmem_limit_bytes=VMEM_LIMIT_BYTES)


def _resident(shape, index_map):
    return pl.BlockSpec(shape, index_map, pipeline_mode=pl.Buffered(1))


def _dot(a, b, dims, precision=None):
    return lax.dot_general(a, b, (dims, ((), ())), precision=precision, preferred_element_type=F32)


def _nn(a, b, precision=None):
    return _dot(a, b, ((1,), (0,)), precision)


def _nt(a, b):
    return _dot(a, b, ((1,), (1,)))


def _tn(a, b):
    return _dot(a, b, ((0,), (0,)))


def _sigmoid(x):
    return jax.nn.sigmoid(x)


class _Exchange:
    def __init__(self, inputs, out_shapes, n_sems, plan, aliases=None, then=None):
        self.inputs, self.out_shapes, self.n_sems, self.plan, self.aliases = list(inputs), list(out_shapes), n_sems, plan, aliases or {}
        self.then = then

    def sem_shapes(self):
        return [pltpu.SemaphoreType.DMA((self.n_sems,)), pltpu.SemaphoreType.DMA((self.n_sems,))]

    @staticmethod
    def _copy(src, dst, i, to, send_sems, recv_sems):
        return pltpu.make_async_remote_copy(
            src_ref=src, dst_ref=dst, send_sem=send_sems.at[i], recv_sem=recv_sems.at[i], device_id=to, device_id_type=MESH)

    def _start(self, plan, in_refs, out_refs, send_sems, recv_sems):
        for src, dst, i, to in plan(in_refs, out_refs)[0]:
            self._copy(src, dst, i, to, send_sems, recv_sems).start()

    def _wait(self, plan, in_refs, out_refs, send_sems, recv_sems):
        sends, lands = plan(in_refs, out_refs)
        for zone, i in lands:
            self._copy(zone, zone, i, _place(), send_sems, recv_sems).wait_recv()
        for src, dst, i, to in sends:
            self._copy(src, dst, i, to, send_sems, recv_sems).wait_send()

    def start(self, *refs):
        self._start(self.plan, *refs)

    def switch(self, *refs):
        if self.then:
            self._wait(self.plan, *refs)
            self._start(self.then, *refs)

    def finish(self, *refs):
        self._wait(self.then or self.plan, *refs)


def _run_exchange(ex, name):
    n_in, n_out = len(ex.inputs), len(ex.out_shapes)

    def body(*refs):
        in_refs, out_refs, (send_sems, recv_sems) = refs[:n_in], refs[n_in:n_in + n_out], refs[n_in + n_out:]
        ex.start(in_refs, out_refs, send_sems, recv_sems)
        ex.switch(in_refs, out_refs, send_sems, recv_sems)
        ex.finish(in_refs, out_refs, send_sems, recv_sems)

    return pl.pallas_call(
        body, name=name, in_specs=[ANY] * n_in, out_specs=[ANY] * n_out, out_shape=ex.out_shapes,
        scratch_shapes=ex.sem_shapes(), input_output_aliases=dict(ex.aliases),
    )(*ex.inputs)


def _call(body, *, name, grid, in_specs, out_specs, out_shape, args, semantics, scratch_shapes=(), exchange=None):
    if exchange is None:
        return pl.pallas_call(
            body, name=name, grid=grid, in_specs=in_specs, out_specs=out_specs, out_shape=out_shape,
            scratch_shapes=list(scratch_shapes), compiler_params=_params(*semantics))(*args)
    exs = exchange if isinstance(exchange, (list, tuple)) else [exchange]
    n_in, n_out, n_scr = len(in_specs), len(out_specs), len(scratch_shapes)
    x_in, x_out = [len(ex.inputs) for ex in exs], [len(ex.out_shapes) for ex in exs]

    def take(refs, counts):
        groups = []
        for n in counts:
            groups.append(refs[:n])
            refs = refs[n:]
        return groups, refs

    def carrier(*refs):
        ins, refs = refs[:n_in], refs[n_in:]
        x_ins, refs = take(refs, x_in)
        outs, refs = refs[:n_out], refs[n_out:]
        x_outs, refs = take(refs, x_out)
        scr, refs = refs[:n_scr], refs[n_scr:]
        sems, _ = take(refs, [2] * len(exs))
        ids = [pl.program_id(a) for a in range(len(grid))]
        first = functools.reduce(jnp.logical_and, [i == 0 for i in ids])
        last = functools.reduce(jnp.logical_and, [i == g - 1 for i, g in zip(ids, grid)])
        step = functools.reduce(lambda acc, ig: acc * ig[1] + ig[0], zip(ids, grid), 0)

        @pl.when(first)
        def _():
            for ex, xi, xo, (send_sems, recv_sems) in zip(exs, x_ins, x_outs, sems):
                ex.start(xi, xo, send_sems, recv_sems)

        if any(ex.then for ex in exs):
            @pl.when(step == (3 * math.prod(grid)) // 4)
            def _():
                for ex, xi, xo, (send_sems, recv_sems) in zip(exs, x_ins, x_outs, sems):
                    ex.switch(xi, xo, send_sems, recv_sems)

        body(*ins, *outs, *scr)

        @pl.when(last)
        def _():
            for ex, xi, xo, (send_sems, recv_sems) in zip(exs, x_ins, x_outs, sems):
                ex.finish(xi, xo, send_sems, recv_sems)

    aliases, i0, o0 = {}, n_in, n_out
    for ex in exs:
        aliases.update({i0 + i: o0 + o for i, o in ex.aliases.items()})
        i0, o0 = i0 + len(ex.inputs), o0 + len(ex.out_shapes)
    res = pl.pallas_call(
        carrier, name=name, grid=grid, in_specs=list(in_specs) + [ANY] * sum(x_in),
        out_specs=list(out_specs) + [ANY] * sum(x_out),
        out_shape=list(out_shape) + [s for ex in exs for s in ex.out_shapes],
        scratch_shapes=list(scratch_shapes) + [s for ex in exs for s in ex.sem_shapes()],
        input_output_aliases=aliases, compiler_params=_params(*["arbitrary"] * len(grid)),
    )(*args, *[a for ex in exs for a in ex.inputs])
    x_res, _ = take(list(res[n_out:]), x_out)
    return list(res[:n_out]), (x_res if isinstance(exchange, (list, tuple)) else x_res[0])


def _rmsmod_fwd(x, g, shift, scale, name, exchange=None):
    S, D = x.shape
    tr = min(ROW_TILE, S)

    def body(x_ref, g_ref, sh_ref, sc_ref, h_ref):
        xv = x_ref[...]
        rstd = lax.rsqrt(jnp.mean(xv * xv, axis=-1, keepdims=True) + EPS)
        y = xv * rstd * g_ref[...]
        h_ref[...] = (y * (1.0 + sc_ref[...]) + sh_ref[...]).astype(h_ref.dtype)

    row = pl.BlockSpec((tr, D), lambda i: (i, 0))
    vec = pl.BlockSpec((1, D), lambda i: (0, 0))
    return _call(body, name=name, grid=(S // tr,), in_specs=[row, vec, vec, vec], out_specs=[row],
                 out_shape=[jax.ShapeDtypeStruct((S, D), BF16)], args=(x, g, shift, scale), semantics=("parallel",),
                 exchange=exchange)


class _NormBwd:
    def __init__(self, x, g, scale, dx_res, below=None):
        S, D = x.shape
        self.below, self.coef = below, (below[2] if below else None)
        self.inputs = [x, g, scale, dx_res] + ([below[0], below[1]] if below else [])
        vshape = jax.ShapeDtypeStruct((1, D), F32)
        self.out_shape = [jax.ShapeDtypeStruct((S, D), F32), vshape, vshape, vshape]
        if below:
            self.out_shape += [jax.ShapeDtypeStruct((S, D), BF16), vshape]

    def specs(self, tr, D):
        row = pl.BlockSpec((tr, D), lambda i: (i, 0))
        vec = pl.BlockSpec((1, D), lambda i: (0, 0))
        return ([row, vec, vec, row] + ([row, vec] if self.below else []),
                [row, vec, vec, vec] + ([row, vec] if self.below else []))

    def step(self, dhv, in_refs, out_refs):
        if self.below:
            x_ref, g_ref, sc_ref, dxr_ref, f_ref, gate_ref = in_refs
            dx_ref, dsh_ref, dsc_ref, dg_ref, df_ref, dgate_ref = out_refs
            sums = (dsh_ref, dsc_ref, dg_ref, dgate_ref)
        else:
            x_ref, g_ref, sc_ref, dxr_ref = in_refs
            dx_ref, dsh_ref, dsc_ref, dg_ref = out_refs
            sums = (dsh_ref, dsc_ref, dg_ref)

        @pl.when(pl.program_id(0) == 0)
        def _():
            for ref in sums:
                ref[...] = jnp.zeros_like(ref)

        xv, gv = x_ref[...], g_ref[...]
        one_sc = 1.0 + sc_ref[...]
        rstd = lax.rsqrt(jnp.mean(xv * xv, axis=-1, keepdims=True) + EPS)
        n = xv * rstd
        dsh_ref[...] += jnp.sum(dhv, axis=0, keepdims=True)
        dsc_ref[...] += jnp.sum(dhv * n, axis=0, keepdims=True) * gv
        dg_ref[...] += jnp.sum(dhv * n, axis=0, keepdims=True) * one_sc
        dn = dhv * (gv * one_sc)
        dx = dxr_ref[...] + rstd * (dn - n * jnp.mean(dn * n, axis=-1, keepdims=True))
        dx_ref[...] = dx
        if self.below:
            df_ref[...] = (self.coef * gate_ref[...] * dx).astype(df_ref.dtype)
            dgate_ref[...] += self.coef * jnp.sum(dx * f_ref[...].astype(F32), axis=0, keepdims=True)


def _rmsmod_bwd(dh, norm, name, exchange=None):
    S, D = dh.shape
    tr = min(ROW_TILE, S)
    n_in = len(norm.inputs)

    def body(dh_ref, *refs):
        norm.step(dh_ref[...], refs[:n_in], refs[n_in:])

    in_specs, out_specs = norm.specs(tr, D)
    return _call(body, name=name, grid=(S // tr,), in_specs=[pl.BlockSpec((tr, D), lambda i: (i, 0))] + in_specs,
                 out_specs=out_specs, out_shape=norm.out_shape, args=[dh] + norm.inputs, semantics=("arbitrary",),
                 exchange=exchange)


def _ffn_in_fwd(h, w4, name, exchange=None):
    S, D = h.shape
    tm = min(2 * ROW_TILE, S)
    n = w4.shape[2]

    def body(h_ref, wg_ref, wu_ref, zg_ref, zu_ref, a_ref):
        hv = h_ref[...]
        zg = _nn(hv, wg_ref[...])
        zu = _nn(hv, wu_ref[...])
        zg_ref[...] = zg.astype(zg_ref.dtype)
        zu_ref[...] = zu.astype(zu_ref.dtype)
        a_ref[...] = (zg * _sigmoid(zg) * zu).astype(a_ref.dtype)

    out = pl.BlockSpec((tm, n), lambda j, m: (m, j))
    oshape = jax.ShapeDtypeStruct((S, 2 * n), BF16)
    return _call(
        body, name=name, grid=(2, S // tm),
        in_specs=[pl.BlockSpec((tm, D), lambda j, m: (m, 0)),
                  pl.BlockSpec((None, D, n), lambda j, m: (j, 0, 0)),
                  pl.BlockSpec((None, D, n), lambda j, m: (j + 2, 0, 0))],
        out_specs=[out, out, out], out_shape=[oshape, oshape, oshape], args=(h, w4, w4),
        semantics=("parallel", "parallel"), exchange=exchange)


def _proj_out_fwd(lhs, w, x, gate, coef, name, exchange=None, next_norm=None):
    S, D = x.shape
    tm = min(ROW_TILE, S)
    ks = [a.shape[1] for a in lhs]

    def body(*refs):
        lhs_refs, refs = refs[:len(lhs)], refs[len(lhs):]
        if next_norm:
            w_ref, x_ref, gate_ref, g_ref, sh_ref, sc_ref, xn_ref, f_ref, h_ref = refs
        else:
            w_ref, x_ref, gate_ref, xn_ref, f_ref = refs
        acc, off = None, 0
        for a_ref, k in zip(lhs_refs, ks):
            part = _nn(a_ref[...], w_ref[off:off + k, :])
            acc = part if acc is None else acc + part
            off += k
        f_ref[...] = acc.astype(f_ref.dtype)
        xn = x_ref[...] + coef * gate_ref[...] * acc
        xn_ref[...] = xn
        if next_norm:
            rstd = lax.rsqrt(jnp.mean(xn * xn, axis=-1, keepdims=True) + EPS)
            h_ref[...] = (xn * rstd * g_ref[...] * (1.0 + sc_ref[...]) + sh_ref[...]).astype(h_ref.dtype)

    row = pl.BlockSpec((tm, D), lambda m: (m, 0))
    vec = pl.BlockSpec((1, D), lambda m: (0, 0))
    extra = list(next_norm) if next_norm else []
    return _call(
        body, name=name, grid=(S // tm,),
        in_specs=[pl.BlockSpec((tm, k), lambda m: (m, 0)) for k in ks]
        + [_resident(w.shape, lambda m: (0, 0)), row, vec] + [vec] * len(extra),
        out_specs=[row, row] + ([row] if next_norm else []),
        out_shape=[jax.ShapeDtypeStruct((S, D), F32), jax.ShapeDtypeStruct((S, D), BF16)]
        + ([jax.ShapeDtypeStruct((S, D), BF16)] if next_norm else []),
        args=(*lhs, w, x, gate, *extra), semantics=("parallel",), exchange=exchange)


def _proj_out_loss(lhs, w, x, gate, coef, target, name):
    S, D = x.shape
    tm = min(ROW_TILE, S)

    def body(a_ref, w_ref, x_ref, gate_ref, t_ref, dy_ref, df_ref, dgate_ref, sq_ref):
        @pl.when(pl.program_id(0) == 0)
        def _():
            dgate_ref[...] = jnp.zeros_like(dgate_ref)
            sq_ref[...] = jnp.zeros_like(sq_ref)

        f = _nn(a_ref[...], w_ref[...])
        gate = coef * gate_ref[...]
        err = x_ref[...] + gate * f - t_ref[...]
        sq_ref[...] += jnp.sum(err * err, axis=0, keepdims=True)
        dy = err * (1.0 / D)
        dy_ref[...] = dy
        df_ref[...] = (gate * dy).astype(df_ref.dtype)
        dgate_ref[...] += coef * jnp.sum(dy * f, axis=0, keepdims=True)

    row = pl.BlockSpec((tm, D), lambda m: (m, 0))
    vec = pl.BlockSpec((1, D), lambda m: (0, 0))
    vshape = jax.ShapeDtypeStruct((1, D), F32)
    return pl.pallas_call(
        body, name=name, grid=(S // tm,),
        in_specs=[pl.BlockSpec((tm, lhs.shape[1]), lambda m: (m, 0)), _resident(w.shape, lambda m: (0, 0)), row, vec, row],
        out_specs=[row, row, vec, vec],
        out_shape=[jax.ShapeDtypeStruct((S, D), F32), jax.ShapeDtypeStruct((S, D), BF16), vshape, vshape],
        compiler_params=_params("arbitrary"),
    )(lhs, w, x, gate, target)


def _matmul_nn(a, w, out_dtype, tm, name):
    S, K = a.shape
    N = w.shape[1]
    tm = min(tm, S)

    def body(a_ref, w_ref, o_ref):
        o_ref[...] = _nn(a_ref[...], w_ref[...]).astype(o_ref.dtype)

    return pl.pallas_call(
        body, name=name, grid=(S // tm,),
        in_specs=[pl.BlockSpec((tm, K), lambda m: (m, 0)), _resident((K, N), lambda m: (0, 0))],
        out_specs=pl.BlockSpec((tm, N), lambda m: (m, 0)), out_shape=jax.ShapeDtypeStruct((S, N), out_dtype),
        compiler_params=_params("parallel"),
    )(a, w)


def _dact_bwd(df, w_out, zg, zu, name, exchange=None):
    S, D = df.shape
    tm = min(ROW_TILE, S)
    n = w_out.shape[0] // 2

    def body(df_ref, w_ref, zg_ref, zu_ref, dzg_ref, dzu_ref):
        da = _nt(df_ref[...], w_ref[...]).astype(BF16)
        zg_v, zu_v = zg_ref[...], zu_ref[...]
        s = _sigmoid(zg_v)
        dzu_ref[...] = da * zg_v * s
        dzg_ref[...] = da * zu_v * (s * (1.0 + zg_v * (1.0 - s)))

    blk = pl.BlockSpec((tm, n), lambda j, m: (m, j))
    oshape = jax.ShapeDtypeStruct((S, 2 * n), BF16)
    return _call(
        body, name=name, grid=(2, S // tm),
        in_specs=[pl.BlockSpec((tm, D), lambda j, m: (m, 0)), pl.BlockSpec((n, D), lambda j, m: (j, 0)), blk, blk],
        out_specs=[blk, blk], out_shape=[oshape, oshape], args=(df, w_out, zg, zu), semantics=("parallel", "parallel"),
        exchange=exchange)


def _ffn_in_dgrad(dzg, dzu, w4, name, exchange=None, norm=None):
    S = dzg.shape[0]
    D, n = w4.shape[1], w4.shape[2]
    tm = min(ROW_TILE, S)
    n_norm = len(norm.inputs) if norm else 0

    def body(dzg_ref, dzu_ref, w_ref, *refs):
        acc = _nt(dzg_ref[:, 0:n], w_ref[0])
        acc += _nt(dzg_ref[:, n:2 * n], w_ref[1])
        acc += _nt(dzu_ref[:, 0:n], w_ref[2])
        acc += _nt(dzu_ref[:, n:2 * n], w_ref[3])
        if norm:
            norm.step(acc, refs[:n_norm], refs[n_norm:])
        else:
            refs[0][...] = acc

    blk = pl.BlockSpec((tm, 2 * n), lambda m: (m, 0))
    in_specs, args = [blk, blk, _resident(w4.shape, lambda m: (0, 0, 0))], [dzg, dzu, w4]
    out_specs, out_shape = [pl.BlockSpec((tm, D), lambda m: (m, 0))], [jax.ShapeDtypeStruct((S, D), F32)]
    if norm:
        norm_in, out_specs = norm.specs(tm, D)
        in_specs, args, out_shape = in_specs + norm_in, args + norm.inputs, norm.out_shape
    return _call(body, name=name, grid=(S // tm,), in_specs=in_specs, out_specs=out_specs, out_shape=out_shape, args=args,
                 semantics=("arbitrary",) if norm else ("parallel",), exchange=exchange)


def _matmul_nt(pieces, w, tm, name, exchange=None, norm=None):
    S = pieces[0].shape[0]
    ks = [p.shape[1] for p in pieces]
    N = w.shape[0]
    tm = min(tm, S)
    n_norm = len(norm.inputs) if norm else 0

    def body(*refs):
        p_refs, w_ref, refs = refs[:len(ks)], refs[len(ks)], refs[len(ks) + 1:]
        acc, off = None, 0
        for p_ref, k in zip(p_refs, ks):
            part = _nt(p_ref[...], w_ref[:, off:off + k])
            acc = part if acc is None else acc + part
            off += k
        if norm:
            norm.step(acc, refs[:n_norm], refs[n_norm:])
        else:
            refs[0][...] = acc

    in_specs = [pl.BlockSpec((tm, k), lambda m: (m, 0)) for k in ks] + [_resident(w.shape, lambda m: (0, 0))]
    args = list(pieces) + [w]
    out_specs, out_shape = [pl.BlockSpec((tm, N), lambda m: (m, 0))], [jax.ShapeDtypeStruct((S, N), F32)]
    if norm:
        norm_in, out_specs = norm.specs(tm, N)
        in_specs, args, out_shape = in_specs + norm_in, args + norm.inputs, norm.out_shape
    return _call(body, name=name, grid=(S // tm,), in_specs=in_specs, out_specs=out_specs, out_shape=out_shape, args=args,
                 semantics=("arbitrary",) if norm else ("parallel",), exchange=exchange)


def _wgrad(a, gs, tn, name, exchange=None):
    S, Ka = a.shape
    N = gs[0].shape[1]
    ts = min(ROW_TILE * (2 if Ka <= D_MODEL else 1), S)

    def body(a_ref, *refs):
        g_refs, o_ref = refs[:-1], refs[-1]

        @pl.when(pl.program_id(1) == 0)
        def _():
            o_ref[...] = jnp.zeros_like(o_ref)

        a_t = a_ref[...].T
        for i, g_ref in enumerate(g_refs):
            o_ref[i] += _nn(a_t, g_ref[...])

    return _call(
        body, name=name, grid=(N // tn, S // ts),
        in_specs=[pl.BlockSpec((ts, Ka), lambda j, s: (s, 0))] + [pl.BlockSpec((ts, tn), lambda j, s: (s, j))] * len(gs),
        out_specs=[pl.BlockSpec((len(gs), None, Ka, tn), lambda j, s: (0, j, 0, 0))],
        out_shape=[jax.ShapeDtypeStruct((len(gs), N // tn, Ka, tn), F32)], args=(a, *gs),
        semantics=("parallel", "arbitrary"), exchange=exchange)


def _wgrad_pieces(a, pieces, tn, name):
    S, Ka = a.shape
    ts = min(ROW_TILE, S)
    blocks = [(i, j) for i, p in enumerate(pieces) for j in range(p.shape[1] // tn)]

    def body(a_ref, *refs):
        g_refs, o_ref = refs[:-1], refs[-1]

        @pl.when(pl.program_id(0) == 0)
        def _():
            o_ref[...] = jnp.zeros_like(o_ref)

        a_t = a_ref[...].T
        for b, g_ref in enumerate(g_refs):
            o_ref[b] += _nn(a_t, g_ref[...])

    return pl.pallas_call(
        body, name=name, grid=(S // ts,),
        in_specs=[pl.BlockSpec((ts, Ka), lambda s: (s, 0))] + [pl.BlockSpec((ts, tn), lambda s, j=j: (s, j)) for _, j in blocks],
        out_specs=pl.BlockSpec((len(blocks), Ka, tn), lambda s: (0, 0, 0)),
        out_shape=jax.ShapeDtypeStruct((len(blocks), Ka, tn), F32), compiler_params=_params("arbitrary"),
    )(a, *[pieces[i] for i, _ in blocks])


def _wgrad_rows(lhs, g, name):
    S, Ka = lhs[0].shape
    N = g.shape[1]
    ts = min(ROW_TILE, S)

    def body(*refs):
        a_refs, g_ref, o_ref = refs[:-2], refs[-2], refs[-1]

        @pl.when(pl.program_id(0) == 0)
        def _():
            o_ref[...] = jnp.zeros_like(o_ref)

        gv = g_ref[...]
        for i, a_ref in enumerate(a_refs):
            o_ref[i] += _tn(a_ref[...], gv)

    return pl.pallas_call(
        body, name=name, grid=(S // ts,),
        in_specs=[pl.BlockSpec((ts, Ka), lambda s: (s, 0))] * len(lhs) + [pl.BlockSpec((ts, N), lambda s: (s, 0))],
        out_specs=pl.BlockSpec((len(lhs), Ka, N), lambda s: (0, 0, 0)),
        out_shape=jax.ShapeDtypeStruct((len(lhs), Ka, N), F32), compiler_params=_params("arbitrary"),
    )(*lhs, g)


def _hgrn_chunk_common(qr, fr, lb, oml, tri, last):
    sig_nf = _sigmoid(-fr)
    k = oml * sig_nf
    f_small = lb + oml * (jnp.exp(jnp.minimum(fr, 0.0)) * sig_nf)
    use_k = k < 0.5
    f = jnp.where(use_k, 1.0 - k, f_small)
    g = jnp.where(use_k, jnp.log1p(-k), jnp.log(f_small)) * math.log2(math.e)
    q = qr * _sigmoid(qr)
    G = _nn(tri, g, precision=lax.Precision.HIGHEST)
    Gl = G[last:last + 1]
    return q, k, f, G, Gl


def _hgrn_consts(reverse):
    C = HG_CHUNK
    r = lax.broadcasted_iota(jnp.int32, (C, C), 0)
    cc = lax.broadcasted_iota(jnp.int32, (C, C), 1)
    tri = ((cc >= r) if reverse else (cc <= r)).astype(F32)
    tri_t = ((cc <= r) if reverse else (cc >= r)).astype(F32)
    rid = lax.broadcasted_iota(jnp.int32, (C, HG_WIDTH), 0)
    return tri, tri_t, rid, (0 if reverse else C - 1)


def _head_slices():
    return [slice(h * HG_DIM, (h + 1) * HG_DIM) for h in range(HG_HEADS)]


def _per_head_lane_sum(x):
    C = x.shape[0]
    return jnp.concatenate(
        [jnp.broadcast_to(jnp.sum(x[:, sl], axis=-1, keepdims=True), (C, HG_DIM)) for sl in _head_slices()], axis=1)


HG_TILE = 8


def _pair_tiles(s, reverse):
    blk, r = divmod(s, HG_TILE)
    n_tiles = HG_CHUNK // HG_TILE
    others = range(0, blk) if reverse else range(blk + 1, n_tiles)
    return [(blk, r)] + [(t, None) for t in others]


def _pair_decay(G, s, tile, r, rid8, reverse, keys=False):
    rs = slice(tile * HG_TILE, (tile + 1) * HG_TILE)
    d = (G[s:s + 1] - G[rs]) if keys else (G[rs] - G[s:s + 1])
    if r is not None:
        d = jnp.where((rid8 <= r) if reverse else (rid8 >= r), d, NEG)
    return rs, jnp.exp2(d)


def _hgrn_fwd_both(z, lbs, name, exchange=None):
    S = z.shape[0]
    C, DK, W = HG_CHUNK, HG_DIM, HG_WIDTH
    tb = min(HG_ROWS, S)
    n_t, n_c = S // tb, tb // C
    dirs = (0, 1)

    def body(qf_ref, ff_ref, vf_ref, qb_ref, fb_ref, vb_ref, lbf_ref, lbb_ref, of_ref, stf_out, ob_ref, stb_out, st_ref):
        @pl.when(pl.program_id(0) == 0)
        def _():
            st_ref[...] = jnp.zeros_like(st_ref)

        q_refs, f_refs, v_refs, lb_refs = (qf_ref, qb_ref), (ff_ref, fb_ref), (vf_ref, vb_ref), (lbf_ref, lbb_ref)
        o_refs, st_outs = (of_ref, ob_ref), (stf_out, stb_out)
        consts = [_hgrn_consts(d == 1) for d in dirs]
        rid8 = lax.broadcasted_iota(jnp.int32, (HG_TILE, W), 0)

        def chunk(ci, carry):
            cidx = [ci, n_c - 1 - ci]
            rows = [pl.ds(pl.multiple_of(c * C, C), C) for c in cidx]
            v = [v_refs[d][rows[d], :] for d in dirs]
            com = [_hgrn_chunk_common(q_refs[d][rows[d], :], f_refs[d][rows[d], :], lb_refs[d][0:1, :], lb_refs[d][1:2, :],
                                      consts[d][0], consts[d][3]) for d in dirs]
            q, k, G, Gl = [c[0] for c in com], [c[1] for c in com], [c[3] for c in com], [c[4] for c in com]
            qd = [(q[d] * jnp.exp2(G[d])).astype(BF16) for d in dirs]
            kd = [(k[d] * jnp.exp2(Gl[d] - G[d])).astype(BF16) for d in dirs]
            e_gl = [jnp.exp2(Gl[d]) for d in dirs]
            v_b = [v[d].astype(BF16) for d in dirs]
            inter = [[], []]
            for h, sl in enumerate(_head_slices()):
                for d in dirs:
                    st0 = st_ref[d, h]
                    st_outs[d][h, cidx[d]] = st0
                    inter[d].append(_nt(qd[d][:, sl], st0.astype(BF16)))
                    st_ref[d, h] = st0 * e_gl[d][:, sl] + _tn(v_b[d][:, sl], kd[d][:, sl])
            o_t = [[jnp.concatenate(inter[d], axis=1)[t * HG_TILE:(t + 1) * HG_TILE] for t in range(C // HG_TILE)] for d in dirs]
            for s in range(C):
                for d in dirs:
                    k_s, v_s = k[d][s:s + 1], v[d][s:s + 1]
                    for tile, r in _pair_tiles(s, d == 1):
                        rs, e_s = _pair_decay(G[d], s, tile, r, rid8, d == 1)
                        o_t[d][tile] = o_t[d][tile] + _per_head_lane_sum(q[d][rs] * k_s * e_s) * v_s
            for d in dirs:
                o_refs[d][rows[d], :] = jnp.concatenate(o_t[d], axis=0)
            return carry

        lax.fori_loop(0, n_c, chunk, 0, unroll=8)

    def sec(j, back):
        return pl.BlockSpec((tb, W), (lambda i: (n_t - 1 - i, j)) if back else (lambda i: (i, j)))

    def st_spec(back):
        return pl.BlockSpec((HG_HEADS, n_c, DK, DK), (lambda i: (0, n_t - 1 - i, 0, 0)) if back else (lambda i: (0, i, 0, 0)))

    vec = pl.BlockSpec((2, W), lambda i: (0, 0))
    o_shape = jax.ShapeDtypeStruct((S, W), F32)
    st_shape = jax.ShapeDtypeStruct((HG_HEADS, S // C, DK, DK), F32)
    return _call(
        body, name=name, grid=(n_t,),
        in_specs=[sec(0, False), sec(1, False), sec(3, False), sec(0, True), sec(2, True), sec(3, True), vec, vec],
        out_specs=[sec(0, False), st_spec(False), sec(0, True), st_spec(True)],
        out_shape=[o_shape, st_shape, o_shape, st_shape],
        scratch_shapes=[pltpu.VMEM((2, HG_HEADS, DK, DK), F32)], args=(z, z, z, z, z, z, lbs[0], lbs[1]),
        semantics=("arbitrary",), exchange=exchange)


def _hgrn_bwd(z, lb, do, states, direction, name, acc=None, exchange=None):
    S = z.shape[0]
    C, DK, W = HG_CHUNK, HG_DIM, HG_WIDTH
    tb = min(HG_ROWS, S)
    n_t, n_c = S // tb, tb // C
    reverse = direction == 1
    tmap = (lambda i: i) if reverse else (lambda i: n_t - 1 - i)

    def body(*refs):
        if acc:
            q_ref, f_ref, v_ref, lb_ref, do_ref, st_in_ref, dqa_ref, dva_ref, dq_ref, df_ref, dv_ref, doml_ref, dst_ref = refs
        else:
            q_ref, f_ref, v_ref, lb_ref, do_ref, st_in_ref, dq_ref, df_ref, dv_ref, doml_ref, dst_ref = refs

        @pl.when(pl.program_id(0) == 0)
        def _():
            dst_ref[...] = jnp.zeros_like(dst_ref)
            doml_ref[...] = jnp.zeros_like(doml_ref)

        lbv, oml = lb_ref[0:1, :], lb_ref[1:2, :]
        tri, tri_t, rid, last = _hgrn_consts(reverse)
        rid8 = lax.broadcasted_iota(jnp.int32, (HG_TILE, W), 0)

        def chunk(ci, carry):
            cidx = ci if reverse else (n_c - 1 - ci)
            rows = pl.ds(pl.multiple_of(cidx * C, C), C)
            qr, fr, v, dov = q_ref[rows, :], f_ref[rows, :], v_ref[rows, :], do_ref[rows, :]
            q, k, f, G, Gl = _hgrn_chunk_common(qr, fr, lbv, oml, tri, last)
            e_g, e_gl, e_kd = jnp.exp2(G), jnp.exp2(Gl), jnp.exp2(Gl - G)
            qd, kd = q * e_g, k * e_kd
            do_b, v_b, qd_b, kd_b = dov.astype(BF16), v.astype(BF16), qd.astype(BF16), kd.astype(BF16)
            dqd, dkd, dv, state_dot = [], [], [], []
            for h, sl in enumerate(_head_slices()):
                st0, dst1 = st_in_ref[h, cidx], dst_ref[h]
                dst1_b = dst1.astype(BF16)
                dqd.append(_nn(do_b[:, sl], st0.astype(BF16)))
                dkd.append(_nn(v_b[:, sl], dst1_b))
                dv.append(_nt(kd_b[:, sl], dst1_b))
                state_dot.append(jnp.sum(st0 * dst1, axis=0, keepdims=True))
                dst_ref[h] = dst1 * e_gl[:, sl] + _tn(do_b[:, sl], qd_b[:, sl])
            dqd, dkd, dv = [jnp.concatenate(t, axis=1) for t in (dqd, dkd, dv)]
            d_gl = e_gl * jnp.concatenate(state_dot, axis=1) + jnp.sum(dkd * kd, axis=0, keepdims=True)
            dq, dk = dqd * e_g, dkd * e_kd
            n_tiles = C // HG_TILE
            dq_t, dk_t, dv_t = [[x[t * HG_TILE:(t + 1) * HG_TILE] for t in range(n_tiles)] for x in (dq, dk, dv)]
            for s in range(C):
                k_s, v_s = k[s:s + 1], v[s:s + 1]
                for tile, r in _pair_tiles(s, reverse):
                    rs, e_s = _pair_decay(G, s, tile, r, rid8, reverse)
                    dq_t[tile] = dq_t[tile] + _per_head_lane_sum(dov[rs] * v_s) * e_s * k_s
            for t in range(C):
                q_t, do_t = q[t:t + 1], dov[t:t + 1]
                for tile, r in _pair_tiles(t, not reverse):
                    rs, x_t = _pair_decay(G, t, tile, r, rid8, not reverse, keys=True)
                    qx = q_t * x_t
                    dv_t[tile] = dv_t[tile] + _per_head_lane_sum(k[rs] * qx) * do_t
                    dk_t[tile] = dk_t[tile] + _per_head_lane_sum(v[rs] * do_t) * qx
            dq, dk, dv = [jnp.concatenate(x, axis=0) for x in (dq_t, dk_t, dv_t)]
            d_big_g = dq * q - dk * k + jnp.where(rid == last, d_gl, 0.0)
            dg = _nn(tri_t, d_big_g, precision=lax.Precision.HIGHEST)
            dk_all = dk - dg / f
            sig_nf = _sigmoid(-fr)
            df_ref[rows, :] = (-dk_all * k * (1.0 - sig_nf)).astype(df_ref.dtype)
            doml_ref[...] += jnp.sum(dk_all * sig_nf, axis=0, keepdims=True)
            sq = _sigmoid(qr)
            dqr = dq * (sq * (1.0 + qr * (1.0 - sq)))
            if acc:
                dqr = dqr + dqa_ref[rows, :]
                dv = dv + dva_ref[rows, :]
            dq_ref[rows, :] = dqr.astype(dq_ref.dtype)
            dv_ref[rows, :] = dv.astype(dv_ref.dtype)
            return carry

        lax.fori_loop(0, n_c, chunk, 0, unroll=8)

    def sec(j):
        return pl.BlockSpec((tb, W), lambda i: (tmap(i), j))

    vec = pl.BlockSpec((1, W), lambda i: (0, 0))
    ins = [z, z, z, lb, do, states]
    in_specs = [sec(0), sec(1 + direction), sec(3), pl.BlockSpec((2, W), lambda i: (0, 0)), sec(0),
                pl.BlockSpec((HG_HEADS, n_c, DK, DK), lambda i: (0, tmap(i), 0, 0))]
    if acc:
        ins += list(acc)
        in_specs += [sec(0), sec(0)]
    final = jax.ShapeDtypeStruct((S, W), BF16)
    partial = final if acc else jax.ShapeDtypeStruct((S, W), F32)
    return _call(
        body, name=name, grid=(n_t,), in_specs=in_specs,
        out_specs=[sec(0), sec(0), sec(0), vec],
        out_shape=[partial, final, partial, jax.ShapeDtypeStruct((1, W), F32)],
        scratch_shapes=[pltpu.VMEM((HG_HEADS, DK, DK), F32)], args=ins, semantics=("arbitrary",), exchange=exchange)


def _hgrn_post_fwd(o_f, o_b, z, norm_g, name):
    S = z.shape[0]
    tr = min(ROW_TILE, S)

    def body(of_ref, ob_ref, gr_ref, ng_ref, y_ref):
        o = of_ref[...] + ob_ref[...]
        gr = gr_ref[...]
        gate = gr * _sigmoid(gr)
        ng = ng_ref[...]
        for h in range(HG_HEADS):
            sl = slice(h * HG_DIM, (h + 1) * HG_DIM)
            oh = o[:, sl]
            rstd = lax.rsqrt(jnp.mean(oh * oh, axis=-1, keepdims=True) + EPS)
            y_ref[:, sl] = (oh * rstd * ng[:, sl] * gate[:, sl]).astype(y_ref.dtype)

    row = pl.BlockSpec((tr, HG_WIDTH), lambda i: (i, 0))
    return pl.pallas_call(
        body, name=name, grid=(S // tr,),
        in_specs=[row, row, pl.BlockSpec((tr, HG_WIDTH), lambda i: (i, 4)), pl.BlockSpec((1, HG_WIDTH), lambda i: (0, 0))],
        out_specs=row, out_shape=jax.ShapeDtypeStruct((S, HG_WIDTH), BF16), compiler_params=_params("parallel"),
    )(o_f, o_b, z, norm_g)


def _hgrn_post_bwd(dy, o_f, o_b, z, norm_g, name):
    S = z.shape[0]
    tr = min(ROW_TILE, S)

    def body(dy_ref, of_ref, ob_ref, gr_ref, ng_ref, do_ref, dgr_ref, dng_ref):
        @pl.when(pl.program_id(0) == 0)
        def _():
            dng_ref[...] = jnp.zeros_like(dng_ref)

        o = of_ref[...] + ob_ref[...]
        gr, ng, dyv = gr_ref[...], ng_ref[...], dy_ref[...]
        sg = _sigmoid(gr)
        for h in range(HG_HEADS):
            sl = slice(h * HG_DIM, (h + 1) * HG_DIM)
            oh, dyh, grh, sgh, ngh = o[:, sl], dyv[:, sl], gr[:, sl], sg[:, sl], ng[:, sl]
            rstd = lax.rsqrt(jnp.mean(oh * oh, axis=-1, keepdims=True) + EPS)
            on = oh * rstd
            du = dyh * (grh * sgh)
            dgr_ref[:, sl] = (dyh * (on * ngh) * (sgh * (1.0 + grh * (1.0 - sgh)))).astype(dgr_ref.dtype)
            dng_ref[:, sl] += jnp.sum(du * on, axis=0, keepdims=True)
            don = du * ngh
            do_ref[:, sl] = rstd * (don - on * jnp.mean(don * on, axis=-1, keepdims=True))

    row = pl.BlockSpec((tr, HG_WIDTH), lambda i: (i, 0))
    vec = pl.BlockSpec((1, HG_WIDTH), lambda i: (0, 0))
    full = jax.ShapeDtypeStruct((S, HG_WIDTH), F32)
    return pl.pallas_call(
        body, name=name, grid=(S // tr,),
        in_specs=[row, row, row, pl.BlockSpec((tr, HG_WIDTH), lambda i: (i, 4)), vec],
        out_specs=[row, row, vec],
        out_shape=[full, jax.ShapeDtypeStruct((S, HG_WIDTH), BF16), jax.ShapeDtypeStruct((1, HG_WIDTH), F32)],
        compiler_params=_params("arbitrary"),
    )(dy, o_f, o_b, z, norm_g)


def _t5_bucket_table():
    rel = (np.arange(3 * BLOCK)[None, :] - BLOCK) - np.arange(BLOCK)[:, None]
    nb = NUM_BUCKETS // 2
    max_exact = nb // 2
    ret = (rel > 0).astype(np.int32) * nb
    n = np.abs(rel)
    ratio = np.log(np.maximum(n, 1).astype(np.float32) / np.float32(max_exact)) / np.float32(math.log(MAX_DISTANCE / max_exact))
    large = max_exact + (ratio.astype(np.float32) * np.float32(nb - max_exact)).astype(np.int32)
    large = np.minimum(large, nb - 1)
    bucket = ret + np.where(n < max_exact, n, large)
    return bucket.astype(np.int32), (n <= WINDOW)


def _bias_table(rel_bias, name):
    bucket, in_band = _t5_bucket_table()
    idx = jnp.asarray(np.where(in_band, bucket, -1))

    def body(rb_ref, idx_ref, o_ref):
        h = pl.program_id(0)
        iv = idx_ref[...]
        acc = jnp.where(iv < 0, NEG, 0.0).astype(F32)
        for b in range(NUM_BUCKETS):
            acc = acc + jnp.where(iv == b, rb_ref[b, h], 0.0)
        o_ref[...] = acc

    return pl.pallas_call(
        body, name=name, grid=(ATT_Q_HEADS,),
        in_specs=[pl.BlockSpec(memory_space=pltpu.SMEM), pl.BlockSpec((BLOCK, 3 * BLOCK), lambda h: (0, 0))],
        out_specs=pl.BlockSpec((None, BLOCK, 3 * BLOCK), lambda h: (h, 0, 0)),
        out_shape=jax.ShapeDtypeStruct((ATT_Q_HEADS, BLOCK, 3 * BLOCK), F32), compiler_params=_params("parallel"),
    )(rel_bias, idx)


def _bias_grad(ds_sum_t, name):
    bucket, in_band = _t5_bucket_table()
    idx_t = jnp.asarray(np.where(in_band, bucket, -1).T)

    def body(ds_ref, idx_ref, o_ref):
        iv, ds = idx_ref[...], ds_ref[...]
        for b in range(NUM_BUCKETS):
            o_ref[b:b + 1, :] = jnp.sum(jnp.where(iv == b, ds, 0.0), axis=0, keepdims=True)

    return pl.pallas_call(
        body, name=name, grid=(ATT_Q_HEADS,),
        in_specs=[pl.BlockSpec((None, 3 * BLOCK, BLOCK), lambda h: (h // ATT_GROUP, 0, h % ATT_GROUP)),
                  pl.BlockSpec((3 * BLOCK, BLOCK), lambda h: (0, 0))],
        out_specs=pl.BlockSpec((None, NUM_BUCKETS, BLOCK), lambda h: (h, 0, 0)),
        out_shape=jax.ShapeDtypeStruct((ATT_Q_HEADS, NUM_BUCKETS, BLOCK), F32), compiler_params=_params("parallel"),
    )(ds_sum_t, idx_t)


Q_COL = 5 * HG_WIDTH
KV_COL = Q_COL + ATT_WIDTH
GROUP_WIDTH = ATT_GROUP * ATT_HEAD_DIM


def _stack_heads(blk):
    dh = ATT_HEAD_DIM
    return jnp.concatenate([blk[:, g * dh:(g + 1) * dh] for g in range(ATT_GROUP)], axis=0)


def _unstack_heads(st):
    return jnp.concatenate([st[g * BLOCK:(g + 1) * BLOCK] for g in range(ATT_GROUP)], axis=1)


def _rms_rows(x):
    rstd = lax.rsqrt(jnp.mean(x * x, axis=-1, keepdims=True) + EPS)
    return x * rstd, rstd


def _edge_ok(n, nb):
    colid = lax.broadcasted_iota(jnp.int32, (ATT_GROUP * BLOCK, 3 * BLOCK), 1)
    return jnp.logical_and(jnp.logical_or(colid >= BLOCK, n > 0), jnp.logical_or(colid < 2 * BLOCK, n < nb - 1))


def _sink_column(sink_ref, j=0):
    heads = range(j * ATT_GROUP, (j + 1) * ATT_GROUP)
    return jnp.concatenate([jnp.broadcast_to(sink_ref[h][:, 0:1], (BLOCK, 1)) for h in heads], axis=0)


def _attn_fwd(z, q_g, k_g, sink, bias, name):
    S = z.shape[0]
    nb = S // BLOCK
    G, dh, KV = ATT_GROUP, ATT_HEAD_DIM, ATT_KV_HEADS
    scale = 1.0 / math.sqrt(dh)

    def body(q_ref, kv0, kv1, kv2, qg_ref, kg_ref, sink_ref, bias_ref, o_ref):
        n = pl.program_id(0)
        edge_ok = _edge_ok(n, nb)
        cat = jnp.concatenate([kv0[...], kv1[...], kv2[...]], axis=0)
        qblk = q_ref[...]
        kn = [(_rms_rows(cat[:, j * dh:(j + 1) * dh])[0] * kg_ref[...]).astype(BF16) for j in range(KV)]
        vb = [cat[:, (KV + j) * dh:(KV + j + 1) * dh].astype(BF16) for j in range(KV)]
        qn = [(_rms_rows(_stack_heads(qblk[:, j * GROUP_WIDTH:(j + 1) * GROUP_WIDTH]))[0] * (qg_ref[...] * scale)).astype(BF16)
              for j in range(KV)]
        s = [_nt(qn[j], kn[j]) + bias_ref[j * G:(j + 1) * G].reshape(G * BLOCK, 3 * BLOCK) for j in range(KV)]
        s = [jnp.where(edge_ok, sj, NEG) for sj in s]
        sinks = [_sink_column(sink_ref, j) for j in range(KV)]
        m = [jnp.maximum(jnp.max(s[j], axis=-1, keepdims=True), sinks[j]) for j in range(KV)]
        e = [jnp.exp(s[j] - m[j]) for j in range(KV)]
        den = [jnp.sum(e[j], axis=-1, keepdims=True) + jnp.exp(sinks[j] - m[j]) for j in range(KV)]
        o = [_nn(e[j].astype(BF16), vb[j]) * (1.0 / den[j]) for j in range(KV)]
        o_ref[...] = jnp.concatenate([_unstack_heads(oj) for oj in o], axis=1).astype(o_ref.dtype)

    def kv(shift):
        return pl.BlockSpec((BLOCK, 2 * KV_WIDTH), lambda n: (jnp.clip(n + shift, 0, nb - 1), KV_COL // (2 * KV_WIDTH)))

    gain = pl.BlockSpec((1, dh), lambda n: (0, 0))
    return pl.pallas_call(
        body, name=name, grid=(nb,),
        in_specs=[pl.BlockSpec((BLOCK, ATT_WIDTH), lambda n: (n, Q_COL // ATT_WIDTH)), kv(-1), kv(0), kv(1), gain, gain,
                  pl.BlockSpec((ATT_Q_HEADS, 1, BLOCK), lambda n: (0, 0, 0)),
                  pl.BlockSpec((ATT_Q_HEADS, BLOCK, 3 * BLOCK), lambda n: (0, 0, 0))],
        out_specs=pl.BlockSpec((BLOCK, ATT_WIDTH), lambda n: (n, 0)),
        out_shape=jax.ShapeDtypeStruct((S, ATT_WIDTH), BF16), compiler_params=_params("parallel"),
    )(z, z, z, z, q_g, k_g, sink, bias)


def _attn_bwd(z, q_g, k_g, sink, bias, do, name):
    S = z.shape[0]
    nb = S // BLOCK
    G, dh, KV = ATT_GROUP, ATT_HEAD_DIM, ATT_KV_HEADS
    scale = 1.0 / math.sqrt(dh)
    both = range(KV)
    bias_t = bias.reshape(KV, G, BLOCK, 3 * BLOCK).transpose(0, 3, 1, 2).reshape(KV, 3 * BLOCK, G * BLOCK)

    def body(q_ref, kv0, kv1, kv2, qg_ref, kg_ref, sink_ref, bias_ref, do_ref,
             dq_ref, dkw_ref, dvw_ref, ds_ref, dsink_ref, dqg_ref):
        n = pl.program_id(0)

        @pl.when(n == 0)
        def _():
            ds_ref[...] = jnp.zeros_like(ds_ref)
            dsink_ref[...] = jnp.zeros_like(dsink_ref)
            dqg_ref[...] = jnp.zeros_like(dqg_ref)

        rowid = lax.broadcasted_iota(jnp.int32, (3 * BLOCK, G * BLOCK), 0)
        edge_ok = jnp.logical_and(jnp.logical_or(rowid >= BLOCK, n > 0), jnp.logical_or(rowid < 2 * BLOCK, n < nb - 1))
        qg = qg_ref[...]
        cat = jnp.concatenate([kv0[...], kv1[...], kv2[...]], axis=0)
        qblk, doblk = q_ref[...], do_ref[...]
        kn = [(_rms_rows(cat[:, j * dh:(j + 1) * dh])[0] * kg_ref[...]).astype(BF16) for j in both]
        vb = [cat[:, (KV + j) * dh:(KV + j + 1) * dh].astype(BF16) for j in both]
        norm = [_rms_rows(_stack_heads(qblk[:, j * GROUP_WIDTH:(j + 1) * GROUP_WIDTH])) for j in both]
        qn = [(norm[j][0] * (qg * scale)).astype(BF16) for j in both]
        do_b = [_stack_heads(doblk[:, j * GROUP_WIDTH:(j + 1) * GROUP_WIDTH]).astype(BF16) for j in both]
        s = [_nt(kn[j], qn[j]) + bias_ref[j] for j in both]
        dp = [_nt(vb[j], do_b[j]) for j in both]
        s = [jnp.where(edge_ok, sj, NEG) for sj in s]
        sinks = [jnp.concatenate([sink_ref[j * G + g] for g in range(G)], axis=1) for j in both]
        m = [jnp.maximum(jnp.max(s[j], axis=0, keepdims=True), sinks[j]) for j in both]
        e = [jnp.exp(s[j] - m[j]) for j in both]
        e_sink = [jnp.exp(sinks[j] - m[j]) for j in both]
        inv = [1.0 / (jnp.sum(e[j], axis=0, keepdims=True) + e_sink[j]) for j in both]
        p = [e[j] * inv[j] for j in both]
        delta = [jnp.sum(p[j] * dp[j], axis=0, keepdims=True) for j in both]
        ds = [p[j] * (dp[j] - delta[j]) for j in both]
        ds_b = [dsj.astype(BF16) for dsj in ds]
        dqn = [_tn(kn[j], ds_b[j]).T * scale for j in both]
        for j in both:
            dvw_ref[j] = _nn(p[j].astype(BF16), do_b[j])
            dkw_ref[j] = _nn(ds_b[j], qn[j])
        for j in both:
            ds_ref[j] += ds[j]
            sink_term = e_sink[j] * inv[j] * delta[j]
            for g in range(G):
                dsink_ref[j * G + g] += (jnp.zeros((1, BLOCK), F32)
                                         - jnp.sum(sink_term[:, g * BLOCK:(g + 1) * BLOCK], axis=1, keepdims=True))
        dq = []
        for j in both:
            qhat, rstd = norm[j]
            dqg_ref[j] += jnp.sum(dqn[j] * qhat, axis=0, keepdims=True)
            dqh = dqn[j] * qg
            dq.append(_unstack_heads(rstd * (dqh - qhat * jnp.mean(dqh * qhat, axis=-1, keepdims=True))))
        dq_ref[...] = jnp.concatenate(dq, axis=1).astype(dq_ref.dtype)

    def kv(shift):
        return pl.BlockSpec((BLOCK, 2 * KV_WIDTH), lambda n: (jnp.clip(n + shift, 0, nb - 1), KV_COL // (2 * KV_WIDTH)))

    gain = pl.BlockSpec((1, dh), lambda n: (0, 0))
    sink_spec = pl.BlockSpec((ATT_Q_HEADS, 1, BLOCK), lambda n: (0, 0, 0))
    bias_spec = pl.BlockSpec((KV, 3 * BLOCK, G * BLOCK), lambda n: (0, 0, 0))
    win = pl.BlockSpec((KV, None, 3 * BLOCK, dh), lambda n: (0, n, 0, 0))
    wshape = jax.ShapeDtypeStruct((KV, nb, 3 * BLOCK, dh), F32)
    return pl.pallas_call(
        body, name=name, grid=(nb,),
        in_specs=[pl.BlockSpec((BLOCK, ATT_WIDTH), lambda n: (n, Q_COL // ATT_WIDTH)), kv(-1), kv(0), kv(1), gain, gain,
                  sink_spec, bias_spec, pl.BlockSpec((BLOCK, ATT_WIDTH), lambda n: (n, HG_WIDTH // ATT_WIDTH))],
        out_specs=[pl.BlockSpec((BLOCK, ATT_WIDTH), lambda n: (n, 0)), win, win, bias_spec, sink_spec,
                   pl.BlockSpec((KV, 1, dh), lambda n: (0, 0, 0))],
        out_shape=[jax.ShapeDtypeStruct((S, ATT_WIDTH), BF16), wshape, wshape,
                   jax.ShapeDtypeStruct((KV, 3 * BLOCK, G * BLOCK), F32),
                   jax.ShapeDtypeStruct((ATT_Q_HEADS, 1, BLOCK), F32),
                   jax.ShapeDtypeStruct((KV, 1, dh), F32)],
        compiler_params=_params("arbitrary"),
    )(z, z, z, z, q_g, k_g, sink, bias_t, do)


def _attn_kv_reduce(dkw, dvw, z, k_g, name):
    S = z.shape[0]
    nb = S // BLOCK
    dh = ATT_HEAD_DIM
    kb = min(8, nb)
    steps = nb // kb

    def body(a_lo, a, a_hi, b_lo, b, b_hi, kv_ref, kg_ref, dkv_ref, dkg_ref):
        n = pl.program_id(0)

        @pl.when(n == 0)
        def _():
            dkg_ref[...] = jnp.zeros_like(dkg_ref)

        lo = jnp.where(n > 0, 1.0, 0.0)
        hi = jnp.where(n < steps - 1, 1.0, 0.0)

        def overlap_add(w, w_lo, w_hi, j, i):
            before = lo * w_lo[j] if i == 0 else w[j, i - 1, 2 * BLOCK:3 * BLOCK, :]
            after = hi * w_hi[j] if i == kb - 1 else w[j, i + 1, 0:BLOCK, :]
            return w[j, i, BLOCK:2 * BLOCK, :] + before + after

        dkg = [jnp.zeros((1, dh), F32) for _ in range(ATT_KV_HEADS)]
        for i in range(kb):
            rows = slice(i * BLOCK, (i + 1) * BLOCK)
            dks, dvs = [], []
            for j in range(ATT_KV_HEADS):
                dkn = overlap_add(a, a_lo, a_hi, j, i)
                dvs.append(overlap_add(b, b_lo, b_hi, j, i))
                khat, rstd = _rms_rows(kv_ref[rows, j * dh:(j + 1) * dh])
                dkg[j] = dkg[j] + jnp.sum(dkn * khat, axis=0, keepdims=True)
                dkh = dkn * kg_ref[...]
                dks.append(rstd * (dkh - khat * jnp.mean(dkh * khat, axis=-1, keepdims=True)))
            dkv_ref[rows, :] = jnp.concatenate(dks + dvs, axis=1).astype(dkv_ref.dtype)
        for j in range(ATT_KV_HEADS):
            dkg_ref[j] += dkg[j]

    main = pl.BlockSpec((ATT_KV_HEADS, kb, 3 * BLOCK, dh), lambda n: (0, n, 0, 0))
    halo_lo = pl.BlockSpec((ATT_KV_HEADS, None, BLOCK, dh), lambda n: (0, jnp.maximum(n * kb - 1, 0), 2, 0))
    halo_hi = pl.BlockSpec((ATT_KV_HEADS, None, BLOCK, dh), lambda n: (0, jnp.minimum(n * kb + kb, nb - 1), 0, 0))
    return pl.pallas_call(
        body, name=name, grid=(steps,),
        in_specs=[halo_lo, main, halo_hi, halo_lo, main, halo_hi,
                  pl.BlockSpec((kb * BLOCK, 2 * KV_WIDTH), lambda n: (n, KV_COL // (2 * KV_WIDTH))),
                  pl.BlockSpec((1, dh), lambda n: (0, 0))],
        out_specs=[pl.BlockSpec((kb * BLOCK, 2 * KV_WIDTH), lambda n: (n, 0)),
                   pl.BlockSpec((ATT_KV_HEADS, 1, dh), lambda n: (0, 0, 0))],
        out_shape=[jax.ShapeDtypeStruct((S, 2 * KV_WIDTH), BF16), jax.ShapeDtypeStruct((ATT_KV_HEADS, 1, dh), F32)],
        compiler_params=_params("arbitrary"),
    )(dkw, dkw, dkw, dvw, dvw, dvw, z, k_g)


def _ada_wgrad(c_act_t, dm, name):
    D, nbatch = c_act_t.shape
    n = dm.shape[1]
    tr = 256

    def body(c_ref, dm_ref, o_ref):
        cv, dv = c_ref[...], dm_ref[...]
        acc = cv[:, 0:1] * dv[0:1, :]
        for b in range(1, nbatch):
            acc = acc + cv[:, b:b + 1] * dv[b:b + 1, :]
        o_ref[...] = acc

    return pl.pallas_call(
        body, name=name, grid=(D // tr,),
        in_specs=[pl.BlockSpec((tr, nbatch), lambda i: (i, 0)), pl.BlockSpec((nbatch, n), lambda i: (0, 0))],
        out_specs=pl.BlockSpec((tr, n), lambda i: (i, 0)), out_shape=jax.ShapeDtypeStruct((D, n), F32),
        compiler_params=_params("parallel"),
    )(c_act_t, dm)


def _to_bf16(w, name):
    R, Cn = w.shape
    tr = _row_tile(R)

    def body(w_ref, o_ref):
        o_ref[...] = w_ref[...].astype(BF16)

    blk = pl.BlockSpec((tr, Cn), lambda i: (i, 0))
    return pl.pallas_call(
        body, name=name, grid=(R // tr,), in_specs=[blk], out_specs=blk, out_shape=jax.ShapeDtypeStruct((R, Cn), BF16),
        compiler_params=_params("parallel"),
    )(w)


def _adamw(w, g, m, v, name):
    R, Cn = w.shape
    tr = R
    for cand in (256, 128, 64, 32, 16, 8):
        if R % cand == 0:
            tr = cand
            break

    def body(w_ref, g_ref, m_ref, v_ref, d_ref, nm_ref, nv_ref):
        gv = g_ref[...]
        m_new = ADAM_B1 * m_ref[...] + (1.0 - ADAM_B1) * gv
        v_new = ADAM_B2 * v_ref[...] + (1.0 - ADAM_B2) * (gv * gv)
        m_hat = m_new / (1.0 - ADAM_B1 ** ADAM_STEP)
        v_hat = v_new / (1.0 - ADAM_B2 ** ADAM_STEP)
        d_ref[...] = -ADAM_LR * (m_hat / (jnp.sqrt(v_hat) + ADAM_EPS) + ADAM_WD * w_ref[...])
        nm_ref[...] = m_new
        nv_ref[...] = v_new

    blk = pl.BlockSpec((tr, Cn), lambda i: (i, 0))
    shp = jax.ShapeDtypeStruct((R, Cn), F32)
    return pl.pallas_call(
        body, name=name, grid=(R // tr,), in_specs=[blk] * 4, out_specs=[blk] * 3, out_shape=[shp] * 3,
        compiler_params=_params("parallel"),
    )(w, g, m, v)


def _place():
    return lax.axis_index("x"), lax.axis_index("y"), lax.axis_index("c")


def _flip(place, k):
    x, y, c = place
    return (1 - x if k & 4 else x, 1 - y if k & 2 else y, 1 - c if k & 1 else c)


def _dev_index(place):
    x, y, c = place
    return 4 * x + 2 * y + c


def _chip_index(place):
    return 2 * place[0] + place[1]


def _gather8(x_ref, out_ref, send_sems, recv_sems, local_sem):
    me = _place()
    mine = pltpu.make_async_copy(x_ref, out_ref.at[_dev_index(me)], local_sem)
    mine.start()

    def copy(k, origin, to):
        return pltpu.make_async_remote_copy(
            src_ref=x_ref, dst_ref=out_ref.at[_dev_index(origin)], send_sem=send_sems.at[k - 1],
            recv_sem=recv_sems.at[k - 1], device_id=to, device_id_type=MESH)

    sends = [copy(k, me, _flip(me, k)) for k in range(1, 8)]
    for cp in sends:
        cp.start()
    for k in range(1, 8):
        copy(k, _flip(me, k), me).wait_recv()
    for cp in sends:
        cp.wait_send()
    mine.wait()


def _allgather8(x, name, reduce=False):
    R, Cn = x.shape

    def body(x_ref, *rest):
        if reduce:
            out_ref, sum_ref, send_sems, recv_sems, local_sem = rest
        else:
            out_ref, send_sems, recv_sems, local_sem = rest
        _gather8(x_ref, out_ref, send_sems, recv_sems, local_sem)
        if reduce:
            acc = out_ref[0]
            for i in range(1, 8):
                acc = acc + out_ref[i]
            sum_ref[...] = acc

    vm = pl.BlockSpec(memory_space=pltpu.VMEM)
    outs = [jax.ShapeDtypeStruct((8, R, Cn), F32)] + ([jax.ShapeDtypeStruct((R, Cn), F32)] if reduce else [])
    res = pl.pallas_call(
        body, name=name, in_specs=[vm], out_specs=[vm] * len(outs), out_shape=outs,
        scratch_shapes=[pltpu.SemaphoreType.DMA((7,)), pltpu.SemaphoreType.DMA((7,)), pltpu.SemaphoreType.DMA],
    )(x)
    return res if reduce else res[0]


def _prologue(small, w_ada, b_ada, w_shard, to_cast, name):
    R, Cn = small.shape
    n_mod = w_ada.shape[1]
    n_w = len(to_cast)
    big = _gather_over_ici([w_shard])

    def body(*refs):
        (small_ref, wada_ref, b_ref, shard_ref), refs = refs[:4], refs[4:]
        wide_refs, refs = refs[:n_w], refs[n_w:]
        (small_all_ref, mods_all_ref, gathered_ref), refs = refs[:3], refs[3:]
        narrow_refs, refs = refs[:n_w], refs[n_w:]
        (mods_ref, send1, recv1, send2, recv2, local_sems), refs = refs[:6], refs[6:]
        wide_bufs, narrow_bufs, (load_sems, store_sems, big_send, big_recv) = refs[:n_w], refs[n_w:2 * n_w], refs[2 * n_w:]
        big.start([shard_ref], [gathered_ref], big_send, big_recv)
        loads = [pltpu.make_async_copy(w, buf, load_sems.at[i]) for i, (w, buf) in enumerate(zip(wide_refs, wide_bufs))]
        for cp in loads:
            cp.start()
        stores = []
        for i, cp in enumerate(loads):
            cp.wait()
            rows = wide_bufs[i].shape[0]
            tr = _row_tile(rows)

            def cast(j, carry, i=i, tr=tr):
                rs = pl.ds(pl.multiple_of(j * tr, tr), tr)
                narrow_bufs[i][rs, :] = wide_bufs[i][rs, :].astype(BF16)
                return carry

            lax.fori_loop(0, rows // tr, cast, 0)
            stores.append(pltpu.make_async_copy(narrow_bufs[i], narrow_refs[i], store_sems.at[i]))
            stores[-1].start()
        _gather8(small_ref, small_all_ref, send1, recv1, local_sems.at[0])
        c_all = jnp.concatenate([small_all_ref[d, 0:1, :] for d in range(8)], axis=0)
        c_act = c_all * _sigmoid(c_all)
        mods_ref[...] = _nn(c_act, wada_ref[...], precision=lax.Precision.HIGHEST) + b_ref[...]
        _gather8(mods_ref, mods_all_ref, send2, recv2, local_sems.at[1])
        for cp in stores:
            cp.wait()
        big.finish([shard_ref], [gathered_ref], big_send, big_recv)

    vm = pl.BlockSpec(memory_space=pltpu.VMEM)
    seven = pltpu.SemaphoreType.DMA((7,))
    res = pl.pallas_call(
        body, name=name, in_specs=[vm, vm, vm, ANY] + [ANY] * n_w, out_specs=[vm, vm, ANY] + [ANY] * n_w,
        out_shape=[jax.ShapeDtypeStruct((8, R, Cn), F32), jax.ShapeDtypeStruct((8, 8, n_mod), F32)] + big.out_shapes
        + [jax.ShapeDtypeStruct(w.shape, BF16) for w in to_cast],
        scratch_shapes=[pltpu.VMEM((8, n_mod), F32), seven, seven, seven, seven, pltpu.SemaphoreType.DMA((2,))]
        + [pltpu.VMEM(w.shape, F32) for w in to_cast] + [pltpu.VMEM(w.shape, BF16) for w in to_cast]
        + [pltpu.SemaphoreType.DMA((n_w,)), pltpu.SemaphoreType.DMA((n_w,))] + big.sem_shapes(),
        compiler_params=pltpu.CompilerParams(vmem_limit_bytes=VMEM_LIMIT_BYTES),
    )(small, w_ada, b_ada, w_shard, *to_cast)
    return res[0], res[1], res[2], list(res[3:])


def _symmetric_plan(copies):
    def plan(in_refs, out_refs):
        sends = [(src, dst, i, to) for i, (src, dst, to) in enumerate(copies(in_refs, out_refs))]
        return sends, [(dst, i) for _, dst, i, _ in sends]
    return plan


def _halves_exchange(grads):
    def copies(in_refs, out_refs):
        me = _place()
        return [(g.at[kk, 1 - me[2]], got.at[kk], _flip(me, 1)) for g, got in zip(in_refs, out_refs) for kk in range(N_CHIPS)]

    return _Exchange(grads, [jax.ShapeDtypeStruct((N_CHIPS,) + g.shape[2:], g.dtype) for g in grads],
                     N_CHIPS * len(grads), _symmetric_plan(copies))


def _chips_exchange(parts):
    def copies(in_refs, out_refs):
        me = _place()
        return [(p.at[_chip_index(_flip(me, 2 * j))], got.at[j - 1], _flip(me, 2 * j))
                for p, got in zip(in_refs, out_refs) for j in (1, 2, 3)]

    return _Exchange(parts, [jax.ShapeDtypeStruct((3,) + p.shape[1:], p.dtype) for p in parts], 3 * len(parts),
                     _symmetric_plan(copies))


def _siblings_exchange(halves):
    def copies(in_refs, out_refs):
        sibling = _flip(_place(), 1)
        return [(h, got, sibling) for h, got in zip(in_refs, out_refs)]

    return _Exchange(halves, [jax.ShapeDtypeStruct(h.shape, h.dtype) for h in halves], len(halves), _symmetric_plan(copies))


def _ici_gather_plan(n, base=0):
    def plan(in_refs, out_refs):
        me = _place()
        c = me[2]
        sends, lands = [], []
        for a, (w, out) in enumerate(zip(in_refs[:n], out_refs)):
            for j in (1, 2, 3):
                i = base + 3 * a + j - 1
                sends.append((w.at[c], out.at[_chip_index(me), c], i, _flip(me, 2 * j)))
                lands.append((out.at[_chip_index(_flip(me, 2 * j)), c], i))
        return sends, lands
    return plan


def _d2d_gather_plan(n, base=0):
    def plan(in_refs, out_refs):
        me = _place()
        c = me[2]
        sibling = _flip(me, 1)
        mine = _chip_index(me)
        sends, lands = [], []
        for a, (w, out) in enumerate(zip(in_refs[:n], out_refs)):
            moves = [(w.at[c], (mine, c)), (w.at[1 - c], (mine, 1 - c))]
            moves += [(out.at[_chip_index(_flip(me, 2 * j)), c], (_chip_index(_flip(me, 2 * j)), c)) for j in (1, 2, 3)]
            for k, (src, (chip, half)) in enumerate(moves):
                sends.append((src, out.at[chip, half], base + 5 * a + k, sibling))
            blocks = [(mine, 1 - c), (mine, c)] + [(_chip_index(_flip(me, 2 * j)), 1 - c) for j in (1, 2, 3)]
            lands += [(out.at[chip, half], base + 5 * a + k) for k, (chip, half) in enumerate(blocks)]
        return sends, lands
    return plan


def _gathered_shapes(shards):
    return [jax.ShapeDtypeStruct((N_CHIPS,) + s.shape, s.dtype) for s in shards]


def _gather_over_ici(shards):
    return _Exchange(shards, _gathered_shapes(shards), 3 * len(shards), _ici_gather_plan(len(shards)))


def _gather_over_d2d(shards, gathered):
    n = len(shards)
    return _Exchange(list(shards) + list(gathered), [jax.ShapeDtypeStruct(g.shape, g.dtype) for g in gathered], 5 * n,
                     _d2d_gather_plan(n), aliases={n + a: a for a in range(n)})


def _gather_in_one(shards):
    n = len(shards)
    return _Exchange(shards, _gathered_shapes(shards), 8 * n, _ici_gather_plan(n), then=_d2d_gather_plan(n, base=3 * n))


def _row_tile(rows):
    for cand in (256, 176, 128, 64, 32, 16, 8):
        if rows % cand == 0:
            return cand
    return rows


def _pair_sum(core, grad, theirs, name):
    N, _, R, Cn = grad.shape
    tr = R

    def body(core_ref, g_ref, t_ref, o_ref, ob_ref):
        s = g_ref[...] + t_ref[...]
        o_ref[...] = s
        ob_ref[...] = s.astype(BF16)

    out = pl.BlockSpec((None, tr, Cn), lambda k, i, core_ref: (k, i, 0))
    return pl.pallas_call(
        body, name=name,
        grid_spec=pltpu.PrefetchScalarGridSpec(
            num_scalar_prefetch=1, grid=(N, R // tr),
            in_specs=[pl.BlockSpec((None, None, tr, Cn), lambda k, i, core_ref: (k, core_ref[0], i, 0)),
                      pl.BlockSpec((None, tr, Cn), lambda k, i, core_ref: (k, i, 0))],
            out_specs=[out, out]),
        out_shape=[jax.ShapeDtypeStruct((N, R, Cn), F32), jax.ShapeDtypeStruct((N, R, Cn), BF16)],
        compiler_params=_params("parallel", "parallel"),
    )(core, grad, theirs)


def _chip_sum(chip, parts, landed, name):
    _, R, Cn = parts.shape
    tr = R

    def body(chip_ref, p_ref, l_ref, o_ref):
        o_ref[...] = ((p_ref[...] + l_ref[0].astype(F32)) + l_ref[1].astype(F32)) + l_ref[2].astype(F32)

    return pl.pallas_call(
        body, name=name,
        grid_spec=pltpu.PrefetchScalarGridSpec(
            num_scalar_prefetch=1, grid=(R // tr,),
            in_specs=[pl.BlockSpec((None, tr, Cn), lambda i, chip_ref: (chip_ref[0], i, 0)),
                      pl.BlockSpec((3, tr, Cn), lambda i, chip_ref: (0, i, 0))],
            out_specs=pl.BlockSpec((tr, Cn), lambda i, chip_ref: (i, 0))),
        out_shape=jax.ShapeDtypeStruct((R, Cn), F32), compiler_params=_params("parallel"),
    )(chip, parts, landed)


def _pair_sums(core, grads, theirs, tag):
    return [_pair_sum(core, g, t, f"{tag}_pair_sum_{i}") for i, (g, t) in enumerate(zip(grads, theirs))]


def _chip_sums(chip, parts, landed, tag):
    return [_chip_sum(chip, p[0], l, f"{tag}_chip_sum_{i}") for i, (p, l) in enumerate(zip(parts, landed))]


def _by_chip_rows(g):
    return g.reshape(N_CHIPS, 2, g.shape[0] // (2 * N_CHIPS), g.shape[1])


def _by_chip_cols(g):
    return g.reshape(N_CHIPS, 2, g.shape[1] // 2, g.shape[2])


def _adamw_halves(core, w, g_mine, g_theirs, m, v, name):
    R2, Cn = w.shape
    r = R2 // 2
    tr = _row_tile(r)
    nt = r // tr

    def body(core_ref, w_ref, gm_ref, gt_ref, m_ref, v_ref, g_ref, d_ref, nm_ref, nv_ref):
        gv = jnp.where(pl.program_id(0) == core_ref[0], gm_ref[...], gt_ref[...])
        g_ref[...] = gv
        m_new = ADAM_B1 * m_ref[...] + (1.0 - ADAM_B1) * gv
        v_new = ADAM_B2 * v_ref[...] + (1.0 - ADAM_B2) * (gv * gv)
        m_hat = m_new / (1.0 - ADAM_B1 ** ADAM_STEP)
        v_hat = v_new / (1.0 - ADAM_B2 ** ADAM_STEP)
        d_ref[...] = -ADAM_LR * (m_hat / (jnp.sqrt(v_hat) + ADAM_EPS) + ADAM_WD * w_ref[...])
        nm_ref[...] = m_new
        nv_ref[...] = v_new

    full = pl.BlockSpec((tr, Cn), lambda hf, i, core_ref: (hf * nt + i, 0))
    half = pl.BlockSpec((tr, Cn), lambda hf, i, core_ref: (i, 0))
    shp = jax.ShapeDtypeStruct((R2, Cn), F32)
    return pl.pallas_call(
        body, name=name,
        grid_spec=pltpu.PrefetchScalarGridSpec(
            num_scalar_prefetch=1, grid=(2, nt), in_specs=[full, half, half, full, full], out_specs=[full] * 4),
        out_shape=[shp] * 4, compiler_params=_params("parallel", "parallel"),
    )(core, w, g_mine, g_theirs, m, v)


def _pad_row(v, width):
    v = v.reshape(1, -1)
    return jnp.pad(v, ((0, 0), (0, width - v.shape[1])))


def _ffn1_forward(x, ng, shift, scale, gate, w_in_shard, w_in_partly, w_out_shard, gather, next_norm):
    (h,), (w_in4,) = _rmsmod_fwd(x, ng, shift, scale, "ffn1_norm", exchange=_gather_over_d2d([w_in_shard], [w_in_partly]))
    w_in4 = w_in4.reshape(N_CHIPS, D_MODEL, FF_SHARD)
    (zg, zu, a), (partly, (w_out4,)) = _ffn_in_fwd(
        h, w_in4, "ffn1_in", exchange=[_gather_over_ici(gather), _gather_in_one([w_out_shard])])
    w_out = w_out4.reshape(D_FF, D_MODEL)
    (x_new, f, h_next), gathered = _proj_out_fwd([a], w_out, x, gate, 0.5, "ffn1_out", next_norm=next_norm,
                                                 exchange=_gather_over_d2d(gather, partly))
    return x_new, (h, zg, zu, a, f), w_in4, w_out, gathered, h_next


def _ffn_backward(df, saved, w_in4, w_out, core, chip, tag, riding=None, norm=None, in_first=False):
    h, zg, zu, a = saved[:4]
    rode = None
    if riding:
        (dzg, dzu), rode = _dact_bwd(df, w_out, zg, zu, f"{tag}_dact", exchange=riding)
    else:
        dzg, dzu = _dact_bwd(df, w_out, zg, zu, f"{tag}_dact")

    def dw_out(exchange=None):
        outs = _wgrad(a, [df], df.shape[1], f"{tag}_dw_out", exchange=exchange)
        (dw,), landed = outs if exchange else (outs, None)
        return [_by_chip_rows(dw.reshape(a.shape[1], df.shape[1]))], landed

    def dw_in(exchange=None):
        outs = _wgrad(h, [dzg, dzu], FF_SHARD, f"{tag}_dw_in", exchange=exchange)
        (dw,), landed = outs if exchange else (outs, None)
        return [_by_chip_cols(dw.reshape(N_CHIPS, h.shape[1], FF_SHARD))], landed

    (first, tag_1), (second, tag_2) = ((dw_in, "in"), (dw_out, "out"))[::1 if in_first else -1]
    g_1, _ = first()
    g_2, theirs_1 = second(_halves_exchange(g_1))
    parts_1 = _pair_sums(core, g_1, theirs_1, f"{tag}_{tag_1}")
    dh_outs, (theirs_2, landed_1) = _ffn_in_dgrad(
        dzg, dzu, w_in4, f"{tag}_dh", norm=norm, exchange=[_halves_exchange(g_2), _chips_exchange([parts_1[0][1]])])
    parts_2 = _pair_sums(core, g_2, theirs_2, f"{tag}_{tag_2}")
    return dh_outs, parts_2, _chip_sums(chip, parts_1, landed_1, f"{tag}_{tag_1}"), rode


def kernel(x, c, w_ada, b_ada, norm_g, w_ffn1_in, w_ffn1_out, w_ffn2_in, w_ffn2_out, w_mix_in, w_mix_out, hgrn_lb, hgrn_norm_g, qk_norm_g, attn_sink, rel_bias, loss_target, m_w_ada, m_b_ada, m_norm_g, m_w_ffn1_in, m_w_ffn1_out, m_w_ffn2_in, m_w_ffn2_out, m_w_mix_in, m_w_mix_out, m_hgrn_lb, m_hgrn_norm_g, m_qk_norm_g, m_attn_sink, m_rel_bias, v_w_ada, v_b_ada, v_norm_g, v_w_ffn1_in, v_w_ffn1_out, v_w_ffn2_in, v_w_ffn2_out, v_w_mix_in, v_w_mix_out, v_hgrn_lb, v_hgrn_norm_g, v_qk_norm_g, v_attn_sink, v_rel_bias):
    D = D_MODEL
    S = x.shape[1]
    place = (lax.axis_index("x"), lax.axis_index("y"), lax.axis_index("c"))
    me, my_chip = _dev_index(place), _chip_index(place)
    x0 = x[0]
    target = loss_target[0]

    core_arr = jnp.reshape(place[2], (1,)).astype(jnp.int32)
    chip_arr = jnp.reshape(my_chip, (1,)).astype(jnp.int32)

    def halves(w):
        return w.reshape(2, w.shape[0] // 2, w.shape[1])

    small = jnp.concatenate([_pad_row(c, D), _pad_row(norm_g, D), _pad_row(hgrn_lb, D), jnp.zeros((5, D), F32)], axis=0)
    n_ada = w_ada.shape[2]
    b_mine = lax.dynamic_slice_in_dim(b_ada, my_chip * n_ada, n_ada, axis=1)
    w1_in_shard = halves(_to_bf16(w_ffn1_in[0], "w_ffn1_in_to_bf16"))
    small_all, mods_parts, w1_in_partly, shards = _prologue(
        small, w_ada[0], b_mine, w1_in_shard, [w_ffn1_out[0], w_mix_in[0], w_mix_out[0], w_ffn2_in[0], w_ffn2_out[0]], "prologue")
    w1_out_shard, mix_shards, ffn2_shards = halves(shards[0]), [halves(w) for w in shards[1:3]], [halves(w) for w in shards[3:5]]
    c_all = small_all[:, 0, :]
    by_chip = small_all[0::2]
    norm_g_full = by_chip[:, 1, :3 * 256].reshape(N_CHIPS, 3, 256).transpose(1, 0, 2).reshape(3, D)
    lb_raw = by_chip[:, 2, :2 * 2 * 128].reshape(N_CHIPS, 2, 2, 128).transpose(1, 2, 0, 3).reshape(2, 2, HG_WIDTH)
    lb_logit = lb_raw[:, 0, :] - lb_raw[:, 1, :]
    lb = jax.nn.sigmoid(lb_logit)
    one_minus_lb = jax.nn.sigmoid(-lb_logit)
    lb_f = jnp.stack([lb[0], one_minus_lb[0]])
    lb_b = jnp.stack([lb[1], one_minus_lb[1]])

    c_act_all = c_all * jax.nn.sigmoid(c_all)
    mods_all = mods_parts[0::2].transpose(1, 0, 2).reshape(8, N_MOD * D)
    mods = lax.dynamic_slice_in_dim(mods_all, me, 1, axis=0)
    sh1, sc1, g1, sh2, sc2, g2, sh3, sc3, g3 = [mods[:, i * D:(i + 1) * D] for i in range(N_MOD)]

    x1, saved1, w1_in, w1_out, gathered, h2 = _ffn1_forward(
        x0, norm_g_full[0:1], sh1, sc1, g1, w1_in_shard, w1_in_partly, w1_out_shard, mix_shards, (norm_g_full[1:2], sh2, sc2))
    wm_in = gathered[0].reshape(N_CHIPS, D, D_IN // N_CHIPS).transpose(1, 0, 2).reshape(D, D_IN)
    wm_out = gathered[1].reshape(D, D)

    z = _matmul_nn(h2, wm_in, F32, 256, "mix_in")
    (of, st_f, ob, st_b), gathered = _hgrn_fwd_both(z, (lb_f, lb_b), "hgrn_fwd", exchange=_gather_in_one(ffn2_shards))
    w2_in = gathered[0].reshape(N_CHIPS, D, FF_SHARD)
    w2_out = gathered[1].reshape(D_FF, D)
    o_h = _hgrn_post_fwd(of, ob, z, hgrn_norm_g, "hgrn_post")

    q_g, k_g = qk_norm_g[0, 0:1], qk_norm_g[0, 1:2]
    sink_b = jnp.broadcast_to(attn_sink.reshape(ATT_Q_HEADS, 1, 1), (ATT_Q_HEADS, 1, BLOCK))
    bias = _bias_table(rel_bias, "bias_table")
    o_a = _attn_fwd(z, q_g, k_g, sink_b, bias, "attn_fwd")
    x2, mixed, h3 = _proj_out_fwd([o_h, o_a], wm_out, x1, g2, 1.0, "mix_out", next_norm=(norm_g_full[2:3], sh3, sc3))

    zg3, zu3, a3 = _ffn_in_fwd(h3, w2_in, "ffn2_in")
    dx3, df3, dg3, sq_cols = _proj_out_loss(a3, w2_out, x2, g3, 0.5, target, "ffn2_out_loss")
    loss_mine = 0.5 * jnp.sum(sq_cols) / D

    (dx2, dsh3, dsc3, dng3, dmixed, dg2), parts2, mine2_out, _ = _ffn_backward(
        df3, (h3, zg3, zu3, a3), w2_in, w2_out, core_arr, chip_arr, "ffn2",
        norm=_NormBwd(x2, norm_g_full[2:3], sc3, dx3, below=(mixed, g2, 1.0)))

    (do_cat,) = _matmul_nt([dmixed], wm_out, ROW_TILE, "mix_out_dgrad")
    dwm_out = _wgrad_rows([o_h, o_a], dmixed, "mix_out_dw").reshape(D, D)

    do_sum, dgr, d_hnorm = _hgrn_post_bwd(do_cat, of, ob, z, hgrn_norm_g, "hgrn_post_bwd")
    (dq_f, dff, dv_f, doml_f), landed2 = _hgrn_bwd(z, lb_f, do_sum, st_f, 0, "hgrn_bwd_f",
                                                   exchange=_chips_exchange([p[1] for p in parts2]))
    mine2 = _chip_sums(chip_arr, parts2, landed2, "ffn2_in") + mine2_out
    (dhq, dfb, dhi, doml_b), theirs2 = _hgrn_bwd(z, lb_b, do_sum, st_b, 1, "hgrn_bwd_b", acc=(dq_f, dv_f),
                                                 exchange=_siblings_exchange(mine2))

    daq, dkw, dvw, ds_sum, dsink, dqg = _attn_bwd(z, q_g, k_g, sink_b, bias, do_cat, "attn_bwd")
    dkv, dkg = _attn_kv_reduce(dkw, dvw, z, k_g, "attn_kv_reduce")
    d_rel_bias = jnp.sum(_bias_grad(ds_sum, "bias_grad"), axis=-1).T
    dz = [dhq, dff, dfb, dhi, dgr, daq, dkv]
    dwm_in = _wgrad_pieces(h2, dz, 2 * KV_WIDTH, "mix_in_dw").transpose(1, 0, 2).reshape(D, D_IN)
    wide = D_IN // N_CHIPS
    grads_m = [_by_chip_cols(dwm_in.reshape(D, N_CHIPS, wide).transpose(1, 0, 2)), _by_chip_rows(dwm_out)]
    (dx1, dsh2, dsc2, dng2, df1, dg1), theirs_m = _matmul_nt(
        dz, wm_in, 256, "mix_in_dgrad", exchange=_halves_exchange(grads_m),
        norm=_NormBwd(x1, norm_g_full[1:2], sc2, dx2, below=(saved1[4], g1, 0.5)))
    parts_m = _pair_sums(core_arr, grads_m, theirs_m, "mix")

    (dh1,), parts1, mine1_in, landed_m = _ffn_backward(df1, saved1, w1_in, w1_out, core_arr, chip_arr, "ffn1",
                                                       riding=_chips_exchange([p[1] for p in parts_m]), in_first=True)
    mine_m = _chip_sums(chip_arr, parts_m, landed_m, "mix")
    (dx0, dsh1, dsc1, dng1), landed1 = _rmsmod_bwd(dh1, _NormBwd(x0, norm_g_full[0:1], sc1, dx1), "ffn1_norm_bwd",
                                                   exchange=_chips_exchange([p[1] for p in parts1]))
    mine1 = mine1_in + _chip_sums(chip_arr, parts1, landed1, "ffn1_out")
    theirs_1m = list(_run_exchange(_siblings_exchange(mine1 + mine_m), "siblings_exchange"))
    reduced = list(zip(mine1 + mine2 + mine_m, theirs_1m[:2] + list(theirs2) + theirs_1m[2:]))

    dlb = -jnp.concatenate([doml_f, doml_b], axis=0)
    dlb_raw = dlb * lb * one_minus_lb
    d_hgrn_lb = jnp.stack([dlb_raw, -dlb_raw], axis=1)
    d_qk = jnp.concatenate([jnp.sum(dqg, axis=0), jnp.sum(dkg, axis=0)], axis=0)
    dmods = jnp.concatenate([dsh1, dsc1, dg1, dsh2, dsc2, dg2, dsh3, dsc3, dg3], axis=0)
    packed = jnp.concatenate(
        [dmods, dng1, dng2, dng3, d_hgrn_lb.reshape(2, D), _pad_row(d_hnorm, D), _pad_row(d_qk, D),
         _pad_row(dsink[:, 0, 0], D), _pad_row(d_rel_bias, D), _pad_row(loss_mine, D)], axis=0)
    packed = jnp.pad(packed, ((0, 24 - packed.shape[0]), (0, 0)))
    packed_all, packed_sum = _allgather8(packed, "small_grads_allgather", reduce=True)
    dmods_all = packed_all[:, 0:N_MOD, :].reshape(8, N_MOD * D)
    g_b_ada = packed_sum[0:N_MOD].reshape(1, N_MOD * D)
    g_norm_full = packed_sum[9:12]
    g_norm_g = lax.dynamic_slice_in_dim(g_norm_full, my_chip * 256, 256, axis=1).reshape(1, 3, 256)
    g_hgrn_lb = lax.dynamic_slice_in_dim(packed_sum[12:14].reshape(2, 2, HG_WIDTH), my_chip * 128, 128, axis=2)
    g_hgrn_norm_g = packed_sum[14:15, :HG_WIDTH]
    g_qk_norm_g = packed_sum[15, :2 * ATT_HEAD_DIM].reshape(1, 2, ATT_HEAD_DIM)
    g_attn_sink = packed_sum[16:17, :ATT_Q_HEADS]
    g_rel_bias = packed_sum[17, :NUM_BUCKETS * ATT_Q_HEADS].reshape(NUM_BUCKETS, ATT_Q_HEADS)
    loss = packed_sum[18, 0]

    dm_mine = lax.dynamic_slice_in_dim(dmods_all, my_chip * n_ada, n_ada, axis=1)
    g_w_ada = _ada_wgrad(c_act_all.T, dm_mine, "ada_wgrad")[None]

    def big(w, g, m, v, name):
        d, nm, nv = _adamw(w[0], g[0], m[0], v[0], name)
        return d[None], nm[None], nv[None]

    def big_halves(w, g_pair, m, v, name):
        g, d, nm, nv = _adamw_halves(core_arr, w[0], g_pair[0], g_pair[1], m[0], v[0], name)
        return g[None], (d[None], nm[None], nv[None])

    g_w1_in, u_w1_in = big_halves(w_ffn1_in, reduced[0], m_w_ffn1_in, v_w_ffn1_in, "adamw_w_ffn1_in")
    g_w1_out, u_w1_out = big_halves(w_ffn1_out, reduced[1], m_w_ffn1_out, v_w_ffn1_out, "adamw_w_ffn1_out")
    g_w2_in, u_w2_in = big_halves(w_ffn2_in, reduced[2], m_w_ffn2_in, v_w_ffn2_in, "adamw_w_ffn2_in")
    g_w2_out, u_w2_out = big_halves(w_ffn2_out, reduced[3], m_w_ffn2_out, v_w_ffn2_out, "adamw_w_ffn2_out")
    g_wm_in, u_wm_in = big_halves(w_mix_in, reduced[4], m_w_mix_in, v_w_mix_in, "adamw_w_mix_in")
    g_wm_out, u_wm_out = big_halves(w_mix_out, reduced[5], m_w_mix_out, v_w_mix_out, "adamw_w_mix_out")

    smalls = [(b_ada, g_b_ada, m_b_ada, v_b_ada), (norm_g, g_norm_g, m_norm_g, v_norm_g), (hgrn_lb, g_hgrn_lb, m_hgrn_lb, v_hgrn_lb),
              (hgrn_norm_g, g_hgrn_norm_g, m_hgrn_norm_g, v_hgrn_norm_g), (qk_norm_g, g_qk_norm_g, m_qk_norm_g, v_qk_norm_g),
              (attn_sink, g_attn_sink, m_attn_sink, v_attn_sink), (rel_bias, g_rel_bias, m_rel_bias, v_rel_bias)]
    sizes = [t[0].size for t in smalls]
    total = sum(sizes)
    rows = -(-total // 128)
    rows = -(-rows // 8) * 8

    def pack(i):
        flat = jnp.concatenate([t[i].reshape(-1) for t in smalls])
        fill = 1.0 if i == 3 else 0.0
        return jnp.pad(flat, (0, rows * 128 - total), constant_values=fill).reshape(rows, 128)

    packed_out = _adamw(pack(0), pack(1), pack(2), pack(3), "adamw_small")

    def unpack(flat2d):
        flat = flat2d.reshape(-1)
        outs, off = [], 0
        for t, n in zip(smalls, sizes):
            outs.append(flat[off:off + n].reshape(t[0].shape))
            off += n
        return outs

    d_small, m_small, v_small = [unpack(t) for t in packed_out]

    upd = {
        "w_ada": big(w_ada, g_w_ada, m_w_ada, v_w_ada, "adamw_w_ada"),
        "w_ffn1_in": u_w1_in, "w_ffn1_out": u_w1_out, "w_ffn2_in": u_w2_in, "w_ffn2_out": u_w2_out,
        "w_mix_in": u_wm_in, "w_mix_out": u_wm_out,
    }
    small_names = ["b_ada", "norm_g", "hgrn_lb", "hgrn_norm_g", "qk_norm_g", "attn_sink", "rel_bias"]
    for i, nme in enumerate(small_names):
        upd[nme] = (d_small[i], m_small[i], v_small[i])
    grads = {
        "w_ada": g_w_ada, "b_ada": g_b_ada, "norm_g": g_norm_g, "w_ffn1_in": g_w1_in, "w_ffn1_out": g_w1_out,
        "w_ffn2_in": g_w2_in, "w_ffn2_out": g_w2_out, "w_mix_in": g_wm_in, "w_mix_out": g_wm_out, "hgrn_lb": g_hgrn_lb,
        "hgrn_norm_g": g_hgrn_norm_g, "qk_norm_g": g_qk_norm_g, "attn_sink": g_attn_sink, "rel_bias": g_rel_bias,
    }
    order = ["w_ada", "b_ada", "norm_g", "w_ffn1_in", "w_ffn1_out", "w_ffn2_in", "w_ffn2_out", "w_mix_in", "w_mix_out",
             "hgrn_lb", "hgrn_norm_g", "qk_norm_g", "attn_sink", "rel_bias"]
    return (loss, dx0[None], *[grads[k] for k in order], *[upd[k][0] for k in order], *[upd[k][1] for k in order],
            *[upd[k][2] for k in order])
```

```python
import functools
import math

import numpy as np
import jax
import jax.numpy as jnp
from jax import lax
from jax.experimental import pallas as pl
from jax.experimental.pallas import tpu as pltpu

F32, BF16 = jnp.float32, jnp.bfloat16

D_MODEL = 1024
D_FF = 2816
HG_HEADS, HG_DIM = 4, 128
HG_WIDTH = HG_HEADS * HG_DIM
ATT_Q_HEADS, ATT_KV_HEADS, ATT_HEAD_DIM = 8, 2, 64
ATT_GROUP = ATT_Q_HEADS // ATT_KV_HEADS
ATT_WIDTH = ATT_Q_HEADS * ATT_HEAD_DIM
KV_WIDTH = ATT_KV_HEADS * ATT_HEAD_DIM
WINDOW, BLOCK = 128, 128
NUM_BUCKETS, MAX_DISTANCE = 32, 128
N_MOD = 9
EPS = 1e-6
D_IN = 5 * HG_WIDTH + ATT_WIDTH + 2 * KV_WIDTH
ADAM_LR, ADAM_B1, ADAM_B2, ADAM_EPS, ADAM_WD, ADAM_STEP = 0.001, 0.9, 0.999, 1e-08, 0.01, 10

N_CHIPS = 4
FF_SHARD = 2 * D_FF // N_CHIPS
NEG = -1e30

VMEM_LIMIT_BYTES = 56 << 20
ROW_TILE = 512
HG_CHUNK = 16
HG_ROWS = 512

MESH = pl.DeviceIdType.MESH
ANY = pl.BlockSpec(memory_space=pl.ANY)


def _params(*sem):
    return pltpu.CompilerParams(dimension_semantics=sem, vmem_limit_bytes=VMEM_LIMIT_BYTES)


def _resident(shape, index_map):
    return pl.BlockSpec(shape, index_map, pipeline_mode=pl.Buffered(1))


def _dot(a, b, dims, precision=None):
    return lax.dot_general(a, b, (dims, ((), ())), precision=precision, preferred_element_type=F32)


def _nn(a, b, precision=None):
    return _dot(a, b, ((1,), (0,)), precision)


def _nt(a, b):
    return _dot(a, b, ((1,), (1,)))


def _tn(a, b):
    return _dot(a, b, ((0,), (0,)))


def _sigmoid(x):
    return jax.nn.sigmoid(x)


class _Exchange:
    def __init__(self, inputs, out_shapes, n_sems, plan, aliases=None, then=None):
        self.inputs, self.out_shapes, self.n_sems, self.plan, self.aliases = list(inputs), list(out_shapes), n_sems, plan, aliases or {}
        self.then = then

    def sem_shapes(self):
        return [pltpu.SemaphoreType.DMA((self.n_sems,)), pltpu.SemaphoreType.DMA((self.n_sems,))]

    @staticmethod
    def _copy(src, dst, i, to, send_sems, recv_sems):
        return pltpu.make_async_remote_copy(
            src_ref=src, dst_ref=dst, send_sem=send_sems.at[i], recv_sem=recv_sems.at[i], device_id=to, device_id_type=MESH)

    def _start(self, plan, in_refs, out_refs, send_sems, recv_sems):
        for src, dst, i, to in plan(in_refs, out_refs)[0]:
            self._copy(src, dst, i, to, send_sems, recv_sems).start()

    def _wait(self, plan, in_refs, out_refs, send_sems, recv_sems):
        sends, lands = plan(in_refs, out_refs)
        for zone, i in lands:
            self._copy(zone, zone, i, _place(), send_sems, recv_sems).wait_recv()
        for src, dst, i, to in sends:
            self._copy(src, dst, i, to, send_sems, recv_sems).wait_send()

    def start(self, *refs):
        self._start(self.plan, *refs)

    def switch(self, *refs):
        if self.then:
            self._wait(self.plan, *refs)
            self._start(self.then, *refs)

    def finish(self, *refs):
        self._wait(self.then or self.plan, *refs)


def _run_exchange(ex, name):
    n_in, n_out = len(ex.inputs), len(ex.out_shapes)

    def body(*refs):
        in_refs, out_refs, (send_sems, recv_sems) = refs[:n_in], refs[n_in:n_in + n_out], refs[n_in + n_out:]
        ex.start(in_refs, out_refs, send_sems, recv_sems)
        ex.switch(in_refs, out_refs, send_sems, recv_sems)
        ex.finish(in_refs, out_refs, send_sems, recv_sems)

    return pl.pallas_call(
        body, name=name, in_specs=[ANY] * n_in, out_specs=[ANY] * n_out, out_shape=ex.out_shapes,
        scratch_shapes=ex.sem_shapes(), input_output_aliases=dict(ex.aliases),
    )(*ex.inputs)


def _call(body, *, name, grid, in_specs, out_specs, out_shape, args, semantics, scratch_shapes=(), exchange=None):
    if exchange is None:
        return pl.pallas_call(
            body, name=name, grid=grid, in_specs=in_specs, out_specs=out_specs, out_shape=out_shape,
            scratch_shapes=list(scratch_shapes), compiler_params=_params(*semantics))(*args)
    exs = exchange if isinstance(exchange, (list, tuple)) else [exchange]
    n_in, n_out, n_scr = len(in_specs), len(out_specs), len(scratch_shapes)
    x_in, x_out = [len(ex.inputs) for ex in exs], [len(ex.out_shapes) for ex in exs]

    def take(refs, counts):
        groups = []
        for n in counts:
            groups.append(refs[:n])
            refs = refs[n:]
        return groups, refs

    def carrier(*refs):
        ins, refs = refs[:n_in], refs[n_in:]
        x_ins, refs = take(refs, x_in)
        outs, refs = refs[:n_out], refs[n_out:]
        x_outs, refs = take(refs, x_out)
        scr, refs = refs[:n_scr], refs[n_scr:]
        sems, _ = take(refs, [2] * len(exs))
        ids = [pl.program_id(a) for a in range(len(grid))]
        first = functools.reduce(jnp.logical_and, [i == 0 for i in ids])
        last = functools.reduce(jnp.logical_and, [i == g - 1 for i, g in zip(ids, grid)])
        step = functools.reduce(lambda acc, ig: acc * ig[1] + ig[0], zip(ids, grid), 0)

        @pl.when(first)
        def _():
            for ex, xi, xo, (send_sems, recv_sems) in zip(exs, x_ins, x_outs, sems):
                ex.start(xi, xo, send_sems, recv_sems)

        if any(ex.then for ex in exs):
            @pl.when(step == (3 * math.prod(grid)) // 4)
            def _():
                for ex, xi, xo, (send_sems, recv_sems) in zip(exs, x_ins, x_outs, sems):
                    ex.switch(xi, xo, send_sems, recv_sems)

        body(*ins, *outs, *scr)

        @pl.when(last)
        def _():
            for ex, xi, xo, (send_sems, recv_sems) in zip(exs, x_ins, x_outs, sems):
                ex.finish(xi, xo, send_sems, recv_sems)

    aliases, i0, o0 = {}, n_in, n_out
    for ex in exs:
        aliases.update({i0 + i: o0 + o for i, o in ex.aliases.items()})
        i0, o0 = i0 + len(ex.inputs), o0 + len(ex.out_shapes)
    res = pl.pallas_call(
        carrier, name=name, grid=grid, in_specs=list(in_specs) + [ANY] * sum(x_in),
        out_specs=list(out_specs) + [ANY] * sum(x_out),
        out_shape=list(out_shape) + [s for ex in exs for s in ex.out_shapes],
        scratch_shapes=list(scratch_shapes) + [s for ex in exs for s in ex.sem_shapes()],
        input_output_aliases=aliases, compiler_params=_params(*["arbitrary"] * len(grid)),
    )(*args, *[a for ex in exs for a in ex.inputs])
    x_res, _ = take(list(res[n_out:]), x_out)
    return list(res[:n_out]), (x_res if isinstance(exchange, (list, tuple)) else x_res[0])


def _rmsmod_fwd(x, g, shift, scale, name, exchange=None):
    S, D = x.shape
    tr = min(ROW_TILE, S)

    def body(x_ref, g_ref, sh_ref, sc_ref, h_ref):
        xv = x_ref[...]
        rstd = lax.rsqrt(jnp.mean(xv * xv, axis=-1, keepdims=True) + EPS)
        y = xv * rstd * g_ref[...]
        h_ref[...] = (y * (1.0 + sc_ref[...]) + sh_ref[...]).astype(h_ref.dtype)

    row = pl.BlockSpec((tr, D), lambda i: (i, 0))
    vec = pl.BlockSpec((1, D), lambda i: (0, 0))
    return _call(body, name=name, grid=(S // tr,), in_specs=[row, vec, vec, vec], out_specs=[row],
                 out_shape=[jax.ShapeDtypeStruct((S, D), BF16)], args=(x, g, shift, scale), semantics=("parallel",),
                 exchange=exchange)


class _NormBwd:
    def __init__(self, x, g, scale, dx_res, below=None):
        S, D = x.shape
        self.below, self.coef = below, (below[2] if below else None)
        self.inputs = [x, g, scale, dx_res] + ([below[0], below[1]] if below else [])
        vshape = jax.ShapeDtypeStruct((1, D), F32)
        self.out_shape = [jax.ShapeDtypeStruct((S, D), F32), vshape, vshape, vshape]
        if below:
            self.out_shape += [jax.ShapeDtypeStruct((S, D), BF16), vshape]

    def specs(self, tr, D):
        row = pl.BlockSpec((tr, D), lambda i: (i, 0))
        vec = pl.BlockSpec((1, D), lambda i: (0, 0))
        return ([row, vec, vec, row] + ([row, vec] if self.below else []),
                [row, vec, vec, vec] + ([row, vec] if self.below else []))

    def step(self, dhv, in_refs, out_refs):
        if self.below:
            x_ref, g_ref, sc_ref, dxr_ref, f_ref, gate_ref = in_refs
            dx_ref, dsh_ref, dsc_ref, dg_ref, df_ref, dgate_ref = out_refs
            sums = (dsh_ref, dsc_ref, dg_ref, dgate_ref)
        else:
            x_ref, g_ref, sc_ref, dxr_ref = in_refs
            dx_ref, dsh_ref, dsc_ref, dg_ref = out_refs
            sums = (dsh_ref, dsc_ref, dg_ref)

        @pl.when(pl.program_id(0) == 0)
        def _():
            for ref in sums:
                ref[...] = jnp.zeros_like(ref)

        xv, gv = x_ref[...], g_ref[...]
        one_sc = 1.0 + sc_ref[...]
        rstd = lax.rsqrt(jnp.mean(xv * xv, axis=-1, keepdims=True) + EPS)
        n = xv * rstd
        dsh_ref[...] += jnp.sum(dhv, axis=0, keepdims=True)
        dsc_ref[...] += jnp.sum(dhv * n, axis=0, keepdims=True) * gv
        dg_ref[...] += jnp.sum(dhv * n, axis=0, keepdims=True) * one_sc
        dn = dhv * (gv * one_sc)
        dx = dxr_ref[...] + rstd * (dn - n * jnp.mean(dn * n, axis=-1, keepdims=True))
        dx_ref[...] = dx
        if self.below:
            df_ref[...] = (self.coef * gate_ref[...] * dx).astype(df_ref.dtype)
            dgate_ref[...] += self.coef * jnp.sum(dx * f_ref[...].astype(F32), axis=0, keepdims=True)


def _rmsmod_bwd(dh, norm, name, exchange=None):
    S, D = dh.shape
    tr = min(ROW_TILE, S)
    n_in = len(norm.inputs)

    def body(dh_ref, *refs):
        norm.step(dh_ref[...], refs[:n_in], refs[n_in:])

    in_specs, out_specs = norm.specs(tr, D)
    return _call(body, name=name, grid=(S // tr,), in_specs=[pl.BlockSpec((tr, D), lambda i: (i, 0))] + in_specs,
                 out_specs=out_specs, out_shape=norm.out_shape, args=[dh] + norm.inputs, semantics=("arbitrary",),
                 exchange=exchange)


def _ffn_in_fwd(h, w4, name, exchange=None):
    S, D = h.shape
    tm = min(2 * ROW_TILE, S)
    n = w4.shape[2]

    def body(h_ref, wg_ref, wu_ref, zg_ref, zu_ref, a_ref):
        hv = h_ref[...]
        zg = _nn(hv, wg_ref[...])
        zu = _nn(hv, wu_ref[...])
        zg_ref[...] = zg.astype(zg_ref.dtype)
        zu_ref[...] = zu.astype(zu_ref.dtype)
        a_ref[...] = (zg * _sigmoid(zg) * zu).astype(a_ref.dtype)

    out = pl.BlockSpec((tm, n), lambda j, m: (m, j))
    oshape = jax.ShapeDtypeStruct((S, 2 * n), BF16)
    return _call(
        body, name=name, grid=(2, S // tm),
        in_specs=[pl.BlockSpec((tm, D), lambda j, m: (m, 0)),
                  pl.BlockSpec((None, D, n), lambda j, m: (j, 0, 0)),
                  pl.BlockSpec((None, D, n), lambda j, m: (j + 2, 0, 0))],
        out_specs=[out, out, out], out_shape=[oshape, oshape, oshape], args=(h, w4, w4),
        semantics=("parallel", "parallel"), exchange=exchange)


def _proj_out_fwd(lhs, w, x, gate, coef, name, exchange=None, next_norm=None):
    S, D = x.shape
    tm = min(ROW_TILE, S)
    ks = [a.shape[1] for a in lhs]

    def body(*refs):
        lhs_refs, refs = refs[:len(lhs)], refs[len(lhs):]
        if next_norm:
            w_ref, x_ref, gate_ref, g_ref, sh_ref, sc_ref, xn_ref, f_ref, h_ref = refs
        else:
            w_ref, x_ref, gate_ref, xn_ref, f_ref = refs
        acc, off = None, 0
        for a_ref, k in zip(lhs_refs, ks):
            part = _nn(a_ref[...], w_ref[off:off + k, :])
            acc = part if acc is None else acc + part
            off += k
        f_ref[...] = acc.astype(f_ref.dtype)
        xn = x_ref[...] + coef * gate_ref[...] * acc
        xn_ref[...] = xn
        if next_norm:
            rstd = lax.rsqrt(jnp.mean(xn * xn, axis=-1, keepdims=True) + EPS)
            h_ref[...] = (xn * rstd * g_ref[...] * (1.0 + sc_ref[...]) + sh_ref[...]).astype(h_ref.dtype)

    row = pl.BlockSpec((tm, D), lambda m: (m, 0))
    vec = pl.BlockSpec((1, D), lambda m: (0, 0))
    extra = list(next_norm) if next_norm else []
    return _call(
        body, name=name, grid=(S // tm,),
        in_specs=[pl.BlockSpec((tm, k), lambda m: (m, 0)) for k in ks]
        + [_resident(w.shape, lambda m: (0, 0)), row, vec] + [vec] * len(extra),
        out_specs=[row, row] + ([row] if next_norm else []),
        out_shape=[jax.ShapeDtypeStruct((S, D), F32), jax.ShapeDtypeStruct((S, D), BF16)]
        + ([jax.ShapeDtypeStruct((S, D), BF16)] if next_norm else []),
        args=(*lhs, w, x, gate, *extra), semantics=("parallel",), exchange=exchange)


def _proj_out_loss(lhs, w, x, gate, coef, target, name):
    S, D = x.shape
    tm = min(ROW_TILE, S)

    def body(a_ref, w_ref, x_ref, gate_ref, t_ref, dy_ref, df_ref, dgate_ref, sq_ref):
        @pl.when(pl.program_id(0) == 0)
        def _():
            dgate_ref[...] = jnp.zeros_like(dgate_ref)
            sq_ref[...] = jnp.zeros_like(sq_ref)

        f = _nn(a_ref[...], w_ref[...])
        gate = coef * gate_ref[...]
        err = x_ref[...] + gate * f - t_ref[...]
        sq_ref[...] += jnp.sum(err * err, axis=0, keepdims=True)
        dy = err * (1.0 / D)
        dy_ref[...] = dy
        df_ref[...] = (gate * dy).astype(df_ref.dtype)
        dgate_ref[...] += coef * jnp.sum(dy * f, axis=0, keepdims=True)

    row = pl.BlockSpec((tm, D), lambda m: (m, 0))
    vec = pl.BlockSpec((1, D), lambda m: (0, 0))
    vshape = jax.ShapeDtypeStruct((1, D), F32)
    return pl.pallas_call(
        body, name=name, grid=(S // tm,),
        in_specs=[pl.BlockSpec((tm, lhs.shape[1]), lambda m: (m, 0)), _resident(w.shape, lambda m: (0, 0)), row, vec, row],
        out_specs=[row, row, vec, vec],
        out_shape=[jax.ShapeDtypeStruct((S, D), F32), jax.ShapeDtypeStruct((S, D), BF16), vshape, vshape],
        compiler_params=_params("arbitrary"),
    )(lhs, w, x, gate, target)


def _matmul_nn(a, w, out_dtype, tm, name):
    S, K = a.shape
    N = w.shape[1]
    tm = min(tm, S)

    def body(a_ref, w_ref, o_ref):
        o_ref[...] = _nn(a_ref[...], w_ref[...]).astype(o_ref.dtype)

    return pl.pallas_call(
        body, name=name, grid=(S // tm,),
        in_specs=[pl.BlockSpec((tm, K), lambda m: (m, 0)), _resident((K, N), lambda m: (0, 0))],
        out_specs=pl.BlockSpec((tm, N), lambda m: (m, 0)), out_shape=jax.ShapeDtypeStruct((S, N), out_dtype),
        compiler_params=_params("parallel"),
    )(a, w)


def _dact_bwd(df, w_out, zg, zu, name, exchange=None):
    S, D = df.shape
    tm = min(ROW_TILE, S)
    n = w_out.shape[0] // 2

    def body(df_ref, w_ref, zg_ref, zu_ref, dzg_ref, dzu_ref):
        da = _nt(df_ref[...], w_ref[...]).astype(BF16)
        zg_v, zu_v = zg_ref[...], zu_ref[...]
        s = _sigmoid(zg_v)
        dzu_ref[...] = da * zg_v * s
        dzg_ref[...] = da * zu_v * (s * (1.0 + zg_v * (1.0 - s)))

    blk = pl.BlockSpec((tm, n), lambda j, m: (m, j))
    oshape = jax.ShapeDtypeStruct((S, 2 * n), BF16)
    return _call(
        body, name=name, grid=(2, S // tm),
        in_specs=[pl.BlockSpec((tm, D), lambda j, m: (m, 0)), pl.BlockSpec((n, D), lambda j, m: (j, 0)), blk, blk],
        out_specs=[blk, blk], out_shape=[oshape, oshape], args=(df, w_out, zg, zu), semantics=("parallel", "parallel"),
        exchange=exchange)


def _ffn_in_dgrad(dzg, dzu, w4, name, exchange=None, norm=None):
    S = dzg.shape[0]
    D, n = w4.shape[1], w4.shape[2]
    tm = min(ROW_TILE, S)
    n_norm = len(norm.inputs) if norm else 0

    def body(dzg_ref, dzu_ref, w_ref, *refs):
        acc = _nt(dzg_ref[:, 0:n], w_ref[0])
        acc += _nt(dzg_ref[:, n:2 * n], w_ref[1])
        acc += _nt(dzu_ref[:, 0:n], w_ref[2])
        acc += _nt(dzu_ref[:, n:2 * n], w_ref[3])
        if norm:
            norm.step(acc, refs[:n_norm], refs[n_norm:])
        else:
            refs[0][...] = acc

    blk = pl.BlockSpec((tm, 2 * n), lambda m: (m, 0))
    in_specs, args = [blk, blk, _resident(w4.shape, lambda m: (0, 0, 0))], [dzg, dzu, w4]
    out_specs, out_shape = [pl.BlockSpec((tm, D), lambda m: (m, 0))], [jax.ShapeDtypeStruct((S, D), F32)]
    if norm:
        norm_in, out_specs = norm.specs(tm, D)
        in_specs, args, out_shape = in_specs + norm_in, args + norm.inputs, norm.out_shape
    return _call(body, name=name, grid=(S // tm,), in_specs=in_specs, out_specs=out_specs, out_shape=out_shape, args=args,
                 semantics=("arbitrary",) if norm else ("parallel",), exchange=exchange)


def _matmul_nt(pieces, w, tm, name, exchange=None, norm=None):
    S = pieces[0].shape[0]
    ks = [p.shape[1] for p in pieces]
    N = w.shape[0]
    tm = min(tm, S)
    n_norm = len(norm.inputs) if norm else 0

    def body(*refs):
        p_refs, w_ref, refs = refs[:len(ks)], refs[len(ks)], refs[len(ks) + 1:]
        acc, off = None, 0
        for p_ref, k in zip(p_refs, ks):
            part = _nt(p_ref[...], w_ref[:, off:off + k])
            acc = part if acc is None else acc + part
            off += k
        if norm:
            norm.step(acc, refs[:n_norm], refs[n_norm:])
        else:
            refs[0][...] = acc

    in_specs = [pl.BlockSpec((tm, k), lambda m: (m, 0)) for k in ks] + [_resident(w.shape, lambda m: (0, 0))]
    args = list(pieces) + [w]
    out_specs, out_shape = [pl.BlockSpec((tm, N), lambda m: (m, 0))], [jax.ShapeDtypeStruct((S, N), F32)]
    if norm:
        norm_in, out_specs = norm.specs(tm, N)
        in_specs, args, out_shape = in_specs + norm_in, args + norm.inputs, norm.out_shape
    return _call(body, name=name, grid=(S // tm,), in_specs=in_specs, out_specs=out_specs, out_shape=out_shape, args=args,
                 semantics=("arbitrary",) if norm else ("parallel",), exchange=exchange)


def _wgrad(a, gs, tn, name, exchange=None):
    S, Ka = a.shape
    N = gs[0].shape[1]
    ts = min(ROW_TILE * (2 if Ka <= D_MODEL else 1), S)

    def body(a_ref, *refs):
        g_refs, o_ref = refs[:-1], refs[-1]

        @pl.when(pl.program_id(1) == 0)
        def _():
            o_ref[...] = jnp.zeros_like(o_ref)

        a_t = a_ref[...].T
        for i, g_ref in enumerate(g_refs):
            o_ref[i] += _nn(a_t, g_ref[...])

    return _call(
        body, name=name, grid=(N // tn, S // ts),
        in_specs=[pl.BlockSpec((ts, Ka), lambda j, s: (s, 0))] + [pl.BlockSpec((ts, tn), lambda j, s: (s, j))] * len(gs),
        out_specs=[pl.BlockSpec((len(gs), None, Ka, tn), lambda j, s: (0, j, 0, 0))],
        out_shape=[jax.ShapeDtypeStruct((len(gs), N // tn, Ka, tn), F32)], args=(a, *gs),
        semantics=("parallel", "arbitrary"), exchange=exchange)


def _wgrad_pieces(a, pieces, tn, name):
    S, Ka = a.shape
    ts = min(ROW_TILE, S)
    blocks = [(i, j) for i, p in enumerate(pieces) for j in range(p.shape[1] // tn)]

    def body(a_ref, *refs):
        g_refs, o_ref = refs[:-1], refs[-1]

        @pl.when(pl.program_id(0) == 0)
        def _():
            o_ref[...] = jnp.zeros_like(o_ref)

        a_t = a_ref[...].T
        for b, g_ref in enumerate(g_refs):
            o_ref[b] += _nn(a_t, g_ref[...])

    return pl.pallas_call(
        body, name=name, grid=(S // ts,),
        in_specs=[pl.BlockSpec((ts, Ka), lambda s: (s, 0))] + [pl.BlockSpec((ts, tn), lambda s, j=j: (s, j)) for _, j in blocks],
        out_specs=pl.BlockSpec((len(blocks), Ka, tn), lambda s: (0, 0, 0)),
        out_shape=jax.ShapeDtypeStruct((len(blocks), Ka, tn), F32), compiler_params=_params("arbitrary"),
    )(a, *[pieces[i] for i, _ in blocks])


def _wgrad_rows(lhs, g, name):
    S, Ka = lhs[0].shape
    N = g.shape[1]
    ts = min(ROW_TILE, S)

    def body(*refs):
        a_refs, g_ref, o_ref = refs[:-2], refs[-2], refs[-1]

        @pl.when(pl.program_id(0) == 0)
        def _():
            o_ref[...] = jnp.zeros_like(o_ref)

        gv = g_ref[...]
        for i, a_ref in enumerate(a_refs):
            o_ref[i] += _tn(a_ref[...], gv)

    return pl.pallas_call(
        body, name=name, grid=(S // ts,),
        in_specs=[pl.BlockSpec((ts, Ka), lambda s: (s, 0))] * len(lhs) + [pl.BlockSpec((ts, N), lambda s: (s, 0))],
        out_specs=pl.BlockSpec((len(lhs), Ka, N), lambda s: (0, 0, 0)),
        out_shape=jax.ShapeDtypeStruct((len(lhs), Ka, N), F32), compiler_params=_params("arbitrary"),
    )(*lhs, g)


def _hgrn_chunk_common(qr, fr, lb, oml, tri, last):
    sig_nf = _sigmoid(-fr)
    k = oml * sig_nf
    f_small = lb + oml * (jnp.exp(jnp.minimum(fr, 0.0)) * sig_nf)
    use_k = k < 0.5
    f = jnp.where(use_k, 1.0 - k, f_small)
    g = jnp.where(use_k, jnp.log1p(-k), jnp.log(f_small)) * math.log2(math.e)
    q = qr * _sigmoid(qr)
    G = _nn(tri, g, precision=lax.Precision.HIGHEST)
    Gl = G[last:last + 1]
    return q, k, f, G, Gl


def _hgrn_consts(reverse):
    C = HG_CHUNK
    r = lax.broadcasted_iota(jnp.int32, (C, C), 0)
    cc = lax.broadcasted_iota(jnp.int32, (C, C), 1)
    tri = ((cc >= r) if reverse else (cc <= r)).astype(F32)
    tri_t = ((cc <= r) if reverse else (cc >= r)).astype(F32)
    rid = lax.broadcasted_iota(jnp.int32, (C, HG_WIDTH), 0)
    return tri, tri_t, rid, (0 if reverse else C - 1)


def _head_slices():
    return [slice(h * HG_DIM, (h + 1) * HG_DIM) for h in range(HG_HEADS)]


def _per_head_lane_sum(x):
    C = x.shape[0]
    return jnp.concatenate(
        [jnp.broadcast_to(jnp.sum(x[:, sl], axis=-1, keepdims=True), (C, HG_DIM)) for sl in _head_slices()], axis=1)


HG_TILE = 8


def _pair_tiles(s, reverse):
    blk, r = divmod(s, HG_TILE)
    n_tiles = HG_CHUNK // HG_TILE
    others = range(0, blk) if reverse else range(blk + 1, n_tiles)
    return [(blk, r)] + [(t, None) for t in others]


def _pair_decay(G, s, tile, r, rid8, reverse, keys=False):
    rs = slice(tile * HG_TILE, (tile + 1) * HG_TILE)
    d = (G[s:s + 1] - G[rs]) if keys else (G[rs] - G[s:s + 1])
    if r is not None:
        d = jnp.where((rid8 <= r) if reverse else (rid8 >= r), d, NEG)
    return rs, jnp.exp2(d)


def _hgrn_fwd_both(z, lbs, name, exchange=None):
    S = z.shape[0]
    C, DK, W = HG_CHUNK, HG_DIM, HG_WIDTH
    tb = min(HG_ROWS, S)
    n_t, n_c = S // tb, tb // C
    dirs = (0, 1)

    def body(qf_ref, ff_ref, vf_ref, qb_ref, fb_ref, vb_ref, lbf_ref, lbb_ref, of_ref, stf_out, ob_ref, stb_out, st_ref):
        @pl.when(pl.program_id(0) == 0)
        def _():
            st_ref[...] = jnp.zeros_like(st_ref)

        q_refs, f_refs, v_refs, lb_refs = (qf_ref, qb_ref), (ff_ref, fb_ref), (vf_ref, vb_ref), (lbf_ref, lbb_ref)
        o_refs, st_outs = (of_ref, ob_ref), (stf_out, stb_out)
        consts = [_hgrn_consts(d == 1) for d in dirs]
        rid8 = lax.broadcasted_iota(jnp.int32, (HG_TILE, W), 0)

        def chunk(ci, carry):
            cidx = [ci, n_c - 1 - ci]
            rows = [pl.ds(pl.multiple_of(c * C, C), C) for c in cidx]
            v = [v_refs[d][rows[d], :] for d in dirs]
            com = [_hgrn_chunk_common(q_refs[d][rows[d], :], f_refs[d][rows[d], :], lb_refs[d][0:1, :], lb_refs[d][1:2, :],
                                      consts[d][0], consts[d][3]) for d in dirs]
            q, k, G, Gl = [c[0] for c in com], [c[1] for c in com], [c[3] for c in com], [c[4] for c in com]
            qd = [(q[d] * jnp.exp2(G[d])).astype(BF16) for d in dirs]
            kd = [(k[d] * jnp.exp2(Gl[d] - G[d])).astype(BF16) for d in dirs]
            e_gl = [jnp.exp2(Gl[d]) for d in dirs]
            v_b = [v[d].astype(BF16) for d in dirs]
            inter = [[], []]
            for h, sl in enumerate(_head_slices()):
                for d in dirs:
                    st0 = st_ref[d, h]
                    st_outs[d][h, cidx[d]] = st0
                    inter[d].append(_nt(qd[d][:, sl], st0.astype(BF16)))
                    st_ref[d, h] = st0 * e_gl[d][:, sl] + _tn(v_b[d][:, sl], kd[d][:, sl])
            o_t = [[jnp.concatenate(inter[d], axis=1)[t * HG_TILE:(t + 1) * HG_TILE] for t in range(C // HG_TILE)] for d in dirs]
            for s in range(C):
                for d in dirs:
                    k_s, v_s = k[d][s:s + 1], v[d][s:s + 1]
                    for tile, r in _pair_tiles(s, d == 1):
                        rs, e_s = _pair_decay(G[d], s, tile, r, rid8, d == 1)
                        o_t[d][tile] = o_t[d][tile] + _per_head_lane_sum(q[d][rs] * k_s * e_s) * v_s
            for d in dirs:
                o_refs[d][rows[d], :] = jnp.concatenate(o_t[d], axis=0)
            return carry

        lax.fori_loop(0, n_c, chunk, 0, unroll=8)

    def sec(j, back):
        return pl.BlockSpec((tb, W), (lambda i: (n_t - 1 - i, j)) if back else (lambda i: (i, j)))

    def st_spec(back):
        return pl.BlockSpec((HG_HEADS, n_c, DK, DK), (lambda i: (0, n_t - 1 - i, 0, 0)) if back else (lambda i: (0, i, 0, 0)))

    vec = pl.BlockSpec((2, W), lambda i: (0, 0))
    o_shape = jax.ShapeDtypeStruct((S, W), F32)
    st_shape = jax.ShapeDtypeStruct((HG_HEADS, S // C, DK, DK), F32)
    return _call(
        body, name=name, grid=(n_t,),
        in_specs=[sec(0, False), sec(1, False), sec(3, False), sec(0, True), sec(2, True), sec(3, True), vec, vec],
        out_specs=[sec(0, False), st_spec(False), sec(0, True), st_spec(True)],
        out_shape=[o_shape, st_shape, o_shape, st_shape],
        scratch_shapes=[pltpu.VMEM((2, HG_HEADS, DK, DK), F32)], args=(z, z, z, z, z, z, lbs[0], lbs[1]),
        semantics=("arbitrary",), exchange=exchange)


def _hgrn_bwd(z, lb, do, states, direction, name, acc=None, exchange=None):
    S = z.shape[0]
    C, DK, W = HG_CHUNK, HG_DIM, HG_WIDTH
    tb = min(HG_ROWS, S)
    n_t, n_c = S // tb, tb // C
    reverse = direction == 1
    tmap = (lambda i: i) if reverse else (lambda i: n_t - 1 - i)

    def body(*refs):
        if acc:
            q_ref, f_ref, v_ref, lb_ref, do_ref, st_in_ref, dqa_ref, dva_ref, dq_ref, df_ref, dv_ref, doml_ref, dst_ref = refs
        else:
            q_ref, f_ref, v_ref, lb_ref, do_ref, st_in_ref, dq_ref, df_ref, dv_ref, doml_ref, dst_ref = refs

        @pl.when(pl.program_id(0) == 0)
        def _():
            dst_ref[...] = jnp.zeros_like(dst_ref)
            doml_ref[...] = jnp.zeros_like(doml_ref)

        lbv, oml = lb_ref[0:1, :], lb_ref[1:2, :]
        tri, tri_t, rid, last = _hgrn_consts(reverse)
        rid8 = lax.broadcasted_iota(jnp.int32, (HG_TILE, W), 0)

        def chunk(ci, carry):
            cidx = ci if reverse else (n_c - 1 - ci)
            rows = pl.ds(pl.multiple_of(cidx * C, C), C)
            qr, fr, v, dov = q_ref[rows, :], f_ref[rows, :], v_ref[rows, :], do_ref[rows, :]
            q, k, f, G, Gl = _hgrn_chunk_common(qr, fr, lbv, oml, tri, last)
            e_g, e_gl, e_kd = jnp.exp2(G), jnp.exp2(Gl), jnp.exp2(Gl - G)
            qd, kd = q * e_g, k * e_kd
            do_b, v_b, qd_b, kd_b = dov.astype(BF16), v.astype(BF16), qd.astype(BF16), kd.astype(BF16)
            dqd, dkd, dv, state_dot = [], [], [], []
            for h, sl in enumerate(_head_slices()):
                st0, dst1 = st_in_ref[h, cidx], dst_ref[h]
                dst1_b = dst1.astype(BF16)
                dqd.append(_nn(do_b[:, sl], st0.astype(BF16)))
                dkd.append(_nn(v_b[:, sl], dst1_b))
                dv.append(_nt(kd_b[:, sl], dst1_b))
                state_dot.append(jnp.sum(st0 * dst1, axis=0, keepdims=True))
                dst_ref[h] = dst1 * e_gl[:, sl] + _tn(do_b[:, sl], qd_b[:, sl])
            dqd, dkd, dv = [jnp.concatenate(t, axis=1) for t in (dqd, dkd, dv)]
            d_gl = e_gl * jnp.concatenate(state_dot, axis=1) + jnp.sum(dkd * kd, axis=0, keepdims=True)
            dq, dk = dqd * e_g, dkd * e_kd
            n_tiles = C // HG_TILE
            dq_t, dk_t, dv_t = [[x[t * HG_TILE:(t + 1) * HG_TILE] for t in range(n_tiles)] for x in (dq, dk, dv)]
            for s in range(C):
                k_s, v_s = k[s:s + 1], v[s:s + 1]
                for tile, r in _pair_tiles(s, reverse):
                    rs, e_s = _pair_decay(G, s, tile, r, rid8, reverse)
                    dq_t[tile] = dq_t[tile] + _per_head_lane_sum(dov[rs] * v_s) * e_s * k_s
            for t in range(C):
                q_t, do_t = q[t:t + 1], dov[t:t + 1]
                for tile, r in _pair_tiles(t, not reverse):
                    rs, x_t = _pair_decay(G, t, tile, r, rid8, not reverse, keys=True)
                    qx = q_t * x_t
                    dv_t[tile] = dv_t[tile] + _per_head_lane_sum(k[rs] * qx) * do_t
                    dk_t[tile] = dk_t[tile] + _per_head_lane_sum(v[rs] * do_t) * qx
            dq, dk, dv = [jnp.concatenate(x, axis=0) for x in (dq_t, dk_t, dv_t)]
            d_big_g = dq * q - dk * k + jnp.where(rid == last, d_gl, 0.0)
            dg = _nn(tri_t, d_big_g, precision=lax.Precision.HIGHEST)
            dk_all = dk - dg / f
            sig_nf = _sigmoid(-fr)
            df_ref[rows, :] = (-dk_all * k * (1.0 - sig_nf)).astype(df_ref.dtype)
            doml_ref[...] += jnp.sum(dk_all * sig_nf, axis=0, keepdims=True)
            sq = _sigmoid(qr)
            dqr = dq * (sq * (1.0 + qr * (1.0 - sq)))
            if acc:
                dqr = dqr + dqa_ref[rows, :]
                dv = dv + dva_ref[rows, :]
            dq_ref[rows, :] = dqr.astype(dq_ref.dtype)
            dv_ref[rows, :] = dv.astype(dv_ref.dtype)
            return carry

        lax.fori_loop(0, n_c, chunk, 0, unroll=8)

    def sec(j):
        return pl.BlockSpec((tb, W), lambda i: (tmap(i), j))

    vec = pl.BlockSpec((1, W), lambda i: (0, 0))
    ins = [z, z, z, lb, do, states]
    in_specs = [sec(0), sec(1 + direction), sec(3), pl.BlockSpec((2, W), lambda i: (0, 0)), sec(0),
                pl.BlockSpec((HG_HEADS, n_c, DK, DK), lambda i: (0, tmap(i), 0, 0))]
    if acc:
        ins += list(acc)
        in_specs += [sec(0), sec(0)]
    final = jax.ShapeDtypeStruct((S, W), BF16)
    partial = final if acc else jax.ShapeDtypeStruct((S, W), F32)
    return _call(
        body, name=name, grid=(n_t,), in_specs=in_specs,
        out_specs=[sec(0), sec(0), sec(0), vec],
        out_shape=[partial, final, partial, jax.ShapeDtypeStruct((1, W), F32)],
        scratch_shapes=[pltpu.VMEM((HG_HEADS, DK, DK), F32)], args=ins, semantics=("arbitrary",), exchange=exchange)


def _hgrn_post_fwd(o_f, o_b, z, norm_g, name):
    S = z.shape[0]
    tr = min(ROW_TILE, S)

    def body(of_ref, ob_ref, gr_ref, ng_ref, y_ref):
        o = of_ref[...] + ob_ref[...]
        gr = gr_ref[...]
        gate = gr * _sigmoid(gr)
        ng = ng_ref[...]
        for h in range(HG_HEADS):
            sl = slice(h * HG_DIM, (h + 1) * HG_DIM)
            oh = o[:, sl]
            rstd = lax.rsqrt(jnp.mean(oh * oh, axis=-1, keepdims=True) + EPS)
            y_ref[:, sl] = (oh * rstd * ng[:, sl] * gate[:, sl]).astype(y_ref.dtype)

    row = pl.BlockSpec((tr, HG_WIDTH), lambda i: (i, 0))
    return pl.pallas_call(
        body, name=name, grid=(S // tr,),
        in_specs=[row, row, pl.BlockSpec((tr, HG_WIDTH), lambda i: (i, 4)), pl.BlockSpec((1, HG_WIDTH), lambda i: (0, 0))],
        out_specs=row, out_shape=jax.ShapeDtypeStruct((S, HG_WIDTH), BF16), compiler_params=_params("parallel"),
    )(o_f, o_b, z, norm_g)


def _hgrn_post_bwd(dy, o_f, o_b, z, norm_g, name):
    S = z.shape[0]
    tr = min(ROW_TILE, S)

    def body(dy_ref, of_ref, ob_ref, gr_ref, ng_ref, do_ref, dgr_ref, dng_ref):
        @pl.when(pl.program_id(0) == 0)
        def _():
            dng_ref[...] = jnp.zeros_like(dng_ref)

        o = of_ref[...] + ob_ref[...]
        gr, ng, dyv = gr_ref[...], ng_ref[...], dy_ref[...]
        sg = _sigmoid(gr)
        for h in range(HG_HEADS):
            sl = slice(h * HG_DIM, (h + 1) * HG_DIM)
            oh, dyh, grh, sgh, ngh = o[:, sl], dyv[:, sl], gr[:, sl], sg[:, sl], ng[:, sl]
            rstd = lax.rsqrt(jnp.mean(oh * oh, axis=-1, keepdims=True) + EPS)
            on = oh * rstd
            du = dyh * (grh * sgh)
            dgr_ref[:, sl] = (dyh * (on * ngh) * (sgh * (1.0 + grh * (1.0 - sgh)))).astype(dgr_ref.dtype)
            dng_ref[:, sl] += jnp.sum(du * on, axis=0, keepdims=True)
            don = du * ngh
            do_ref[:, sl] = rstd * (don - on * jnp.mean(don * on, axis=-1, keepdims=True))

    row = pl.BlockSpec((tr, HG_WIDTH), lambda i: (i, 0))
    vec = pl.BlockSpec((1, HG_WIDTH), lambda i: (0, 0))
    full = jax.ShapeDtypeStruct((S, HG_WIDTH), F32)
    return pl.pallas_call(
        body, name=name, grid=(S // tr,),
        in_specs=[row, row, row, pl.BlockSpec((tr, HG_WIDTH), lambda i: (i, 4)), vec],
        out_specs=[row, row, vec],
        out_shape=[full, jax.ShapeDtypeStruct((S, HG_WIDTH), BF16), jax.ShapeDtypeStruct((1, HG_WIDTH), F32)],
        compiler_params=_params("arbitrary"),
    )(dy, o_f, o_b, z, norm_g)


def _t5_bucket_table():
    rel = (np.arange(3 * BLOCK)[None, :] - BLOCK) - np.arange(BLOCK)[:, None]
    nb = NUM_BUCKETS // 2
    max_exact = nb // 2
    ret = (rel > 0).astype(np.int32) * nb
    n = np.abs(rel)
    ratio = np.log(np.maximum(n, 1).astype(np.float32) / np.float32(max_exact)) / np.float32(math.log(MAX_DISTANCE / max_exact))
    large = max_exact + (ratio.astype(np.float32) * np.float32(nb - max_exact)).astype(np.int32)
    large = np.minimum(large, nb - 1)
    bucket = ret + np.where(n < max_exact, n, large)
    return bucket.astype(np.int32), (n <= WINDOW)


def _bias_table(rel_bias, name):
    bucket, in_band = _t5_bucket_table()
    idx = jnp.asarray(np.where(in_band, bucket, -1))

    def body(rb_ref, idx_ref, o_ref):
        h = pl.program_id(0)
        iv = idx_ref[...]
        acc = jnp.where(iv < 0, NEG, 0.0).astype(F32)
        for b in range(NUM_BUCKETS):
            acc = acc + jnp.where(iv == b, rb_ref[b, h], 0.0)
        o_ref[...] = acc

    return pl.pallas_call(
        body, name=name, grid=(ATT_Q_HEADS,),
        in_specs=[pl.BlockSpec(memory_space=pltpu.SMEM), pl.BlockSpec((BLOCK, 3 * BLOCK), lambda h: (0, 0))],
        out_specs=pl.BlockSpec((None, BLOCK, 3 * BLOCK), lambda h: (h, 0, 0)),
        out_shape=jax.ShapeDtypeStruct((ATT_Q_HEADS, BLOCK, 3 * BLOCK), F32), compiler_params=_params("parallel"),
    )(rel_bias, idx)


def _bias_grad(ds_sum_t, name):
    bucket, in_band = _t5_bucket_table()
    idx_t = jnp.asarray(np.where(in_band, bucket, -1).T)

    def body(ds_ref, idx_ref, o_ref):
        iv, ds = idx_ref[...], ds_ref[...]
        for b in range(NUM_BUCKETS):
            o_ref[b:b + 1, :] = jnp.sum(jnp.where(iv == b, ds, 0.0), axis=0, keepdims=True)

    return pl.pallas_call(
        body, name=name, grid=(ATT_Q_HEADS,),
        in_specs=[pl.BlockSpec((None, 3 * BLOCK, BLOCK), lambda h: (h // ATT_GROUP, 0, h % ATT_GROUP)),
                  pl.BlockSpec((3 * BLOCK, BLOCK), lambda h: (0, 0))],
        out_specs=pl.BlockSpec((None, NUM_BUCKETS, BLOCK), lambda h: (h, 0, 0)),
        out_shape=jax.ShapeDtypeStruct((ATT_Q_HEADS, NUM_BUCKETS, BLOCK), F32), compiler_params=_params("parallel"),
    )(ds_sum_t, idx_t)


Q_COL = 5 * HG_WIDTH
KV_COL = Q_COL + ATT_WIDTH
GROUP_WIDTH = ATT_GROUP * ATT_HEAD_DIM


def _stack_heads(blk):
    dh = ATT_HEAD_DIM
    return jnp.concatenate([blk[:, g * dh:(g + 1) * dh] for g in range(ATT_GROUP)], axis=0)


def _unstack_heads(st):
    return jnp.concatenate([st[g * BLOCK:(g + 1) * BLOCK] for g in range(ATT_GROUP)], axis=1)


def _rms_rows(x):
    rstd = lax.rsqrt(jnp.mean(x * x, axis=-1, keepdims=True) + EPS)
    return x * rstd, rstd


def _edge_ok(n, nb):
    colid = lax.broadcasted_iota(jnp.int32, (ATT_GROUP * BLOCK, 3 * BLOCK), 1)
    return jnp.logical_and(jnp.logical_or(colid >= BLOCK, n > 0), jnp.logical_or(colid < 2 * BLOCK, n < nb - 1))


def _sink_column(sink_ref, j=0):
    heads = range(j * ATT_GROUP, (j + 1) * ATT_GROUP)
    return jnp.concatenate([jnp.broadcast_to(sink_ref[h][:, 0:1], (BLOCK, 1)) for h in heads], axis=0)


def _attn_fwd(z, q_g, k_g, sink, bias, name):
    S = z.shape[0]
    nb = S // BLOCK
    G, dh, KV = ATT_GROUP, ATT_HEAD_DIM, ATT_KV_HEADS
    scale = 1.0 / math.sqrt(dh)

    def body(q_ref, kv0, kv1, kv2, qg_ref, kg_ref, sink_ref, bias_ref, o_ref):
        n = pl.program_id(0)
        edge_ok = _edge_ok(n, nb)
        cat = jnp.concatenate([kv0[...], kv1[...], kv2[...]], axis=0)
        qblk = q_ref[...]
        kn = [(_rms_rows(cat[:, j * dh:(j + 1) * dh])[0] * kg_ref[...]).astype(BF16) for j in range(KV)]
        vb = [cat[:, (KV + j) * dh:(KV + j + 1) * dh].astype(BF16) for j in range(KV)]
        qn = [(_rms_rows(_stack_heads(qblk[:, j * GROUP_WIDTH:(j + 1) * GROUP_WIDTH]))[0] * (qg_ref[...] * scale)).astype(BF16)
              for j in range(KV)]
        s = [_nt(qn[j], kn[j]) + bias_ref[j * G:(j + 1) * G].reshape(G * BLOCK, 3 * BLOCK) for j in range(KV)]
        s = [jnp.where(edge_ok, sj, NEG) for sj in s]
        sinks = [_sink_column(sink_ref, j) for j in range(KV)]
        m = [jnp.maximum(jnp.max(s[j], axis=-1, keepdims=True), sinks[j]) for j in range(KV)]
        e = [jnp.exp(s[j] - m[j]) for j in range(KV)]
        den = [jnp.sum(e[j], axis=-1, keepdims=True) + jnp.exp(sinks[j] - m[j]) for j in range(KV)]
        o = [_nn(e[j].astype(BF16), vb[j]) * (1.0 / den[j]) for j in range(KV)]
        o_ref[...] = jnp.concatenate([_unstack_heads(oj) for oj in o], axis=1).astype(o_ref.dtype)

    def kv(shift):
        return pl.BlockSpec((BLOCK, 2 * KV_WIDTH), lambda n: (jnp.clip(n + shift, 0, nb - 1), KV_COL // (2 * KV_WIDTH)))

    gain = pl.BlockSpec((1, dh), lambda n: (0, 0))
    return pl.pallas_call(
        body, name=name, grid=(nb,),
        in_specs=[pl.BlockSpec((BLOCK, ATT_WIDTH), lambda n: (n, Q_COL // ATT_WIDTH)), kv(-1), kv(0), kv(1), gain, gain,
                  pl.BlockSpec((ATT_Q_HEADS, 1, BLOCK), lambda n: (0, 0, 0)),
                  pl.BlockSpec((ATT_Q_HEADS, BLOCK, 3 * BLOCK), lambda n: (0, 0, 0))],
        out_specs=pl.BlockSpec((BLOCK, ATT_WIDTH), lambda n: (n, 0)),
        out_shape=jax.ShapeDtypeStruct((S, ATT_WIDTH), BF16), compiler_params=_params("parallel"),
    )(z, z, z, z, q_g, k_g, sink, bias)


def _attn_bwd(z, q_g, k_g, sink, bias, do, name):
    S = z.shape[0]
    nb = S // BLOCK
    G, dh, KV = ATT_GROUP, ATT_HEAD_DIM, ATT_KV_HEADS
    scale = 1.0 / math.sqrt(dh)
    both = range(KV)
    bias_t = bias.reshape(KV, G, BLOCK, 3 * BLOCK).transpose(0, 3, 1, 2).reshape(KV, 3 * BLOCK, G * BLOCK)

    def body(q_ref, kv0, kv1, kv2, qg_ref, kg_ref, sink_ref, bias_ref, do_ref,
             dq_ref, dkw_ref, dvw_ref, ds_ref, dsink_ref, dqg_ref):
        n = pl.program_id(0)

        @pl.when(n == 0)
        def _():
            ds_ref[...] = jnp.zeros_like(ds_ref)
            dsink_ref[...] = jnp.zeros_like(dsink_ref)
            dqg_ref[...] = jnp.zeros_like(dqg_ref)

        rowid = lax.broadcasted_iota(jnp.int32, (3 * BLOCK, G * BLOCK), 0)
        edge_ok = jnp.logical_and(jnp.logical_or(rowid >= BLOCK, n > 0), jnp.logical_or(rowid < 2 * BLOCK, n < nb - 1))
        qg = qg_ref[...]
        cat = jnp.concatenate([kv0[...], kv1[...], kv2[...]], axis=0)
        qblk, doblk = q_ref[...], do_ref[...]
        kn = [(_rms_rows(cat[:, j * dh:(j + 1) * dh])[0] * kg_ref[...]).astype(BF16) for j in both]
        vb = [cat[:, (KV + j) * dh:(KV + j + 1) * dh].astype(BF16) for j in both]
        norm = [_rms_rows(_stack_heads(qblk[:, j * GROUP_WIDTH:(j + 1) * GROUP_WIDTH])) for j in both]
        qn = [(norm[j][0] * (qg * scale)).astype(BF16) for j in both]
        do_b = [_stack_heads(doblk[:, j * GROUP_WIDTH:(j + 1) * GROUP_WIDTH]).astype(BF16) for j in both]
        s = [_nt(kn[j], qn[j]) + bias_ref[j] for j in both]
        dp = [_nt(vb[j], do_b[j]) for j in both]
        s = [jnp.where(edge_ok, sj, NEG) for sj in s]
        sinks = [jnp.concatenate([sink_ref[j * G + g] for g in range(G)], axis=1) for j in both]
        m = [jnp.maximum(jnp.max(s[j], axis=0, keepdims=True), sinks[j]) for j in both]
        e = [jnp.exp(s[j] - m[j]) for j in both]
        e_sink = [jnp.exp(sinks[j] - m[j]) for j in both]
        inv = [1.0 / (jnp.sum(e[j], axis=0, keepdims=True) + e_sink[j]) for j in both]
        p = [e[j] * inv[j] for j in both]
        delta = [jnp.sum(p[j] * dp[j], axis=0, keepdims=True) for j in both]
        ds = [p[j] * (dp[j] - delta[j]) for j in both]
        ds_b = [dsj.astype(BF16) for dsj in ds]
        dqn = [_tn(kn[j], ds_b[j]).T * scale for j in both]
        for j in both:
            dvw_ref[j] = _nn(p[j].astype(BF16), do_b[j])
            dkw_ref[j] = _nn(ds_b[j], qn[j])
        for j in both:
            ds_ref[j] += ds[j]
            sink_term = e_sink[j] * inv[j] * delta[j]
            for g in range(G):
                dsink_ref[j * G + g] += (jnp.zeros((1, BLOCK), F32)
                                         - jnp.sum(sink_term[:, g * BLOCK:(g + 1) * BLOCK], axis=1, keepdims=True))
        dq = []
        for j in both:
            qhat, rstd = norm[j]
            dqg_ref[j] += jnp.sum(dqn[j] * qhat, axis=0, keepdims=True)
            dqh = dqn[j] * qg
            dq.append(_unstack_heads(rstd * (dqh - qhat * jnp.mean(dqh * qhat, axis=-1, keepdims=True))))
        dq_ref[...] = jnp.concatenate(dq, axis=1).astype(dq_ref.dtype)

    def kv(shift):
        return pl.BlockSpec((BLOCK, 2 * KV_WIDTH), lambda n: (jnp.clip(n + shift, 0, nb - 1), KV_COL // (2 * KV_WIDTH)))

    gain = pl.BlockSpec((1, dh), lambda n: (0, 0))
    sink_spec = pl.BlockSpec((ATT_Q_HEADS, 1, BLOCK), lambda n: (0, 0, 0))
    bias_spec = pl.BlockSpec((KV, 3 * BLOCK, G * BLOCK), lambda n: (0, 0, 0))
    win = pl.BlockSpec((KV, None, 3 * BLOCK, dh), lambda n: (0, n, 0, 0))
    wshape = jax.ShapeDtypeStruct((KV, nb, 3 * BLOCK, dh), F32)
    return pl.pallas_call(
        body, name=name, grid=(nb,),
        in_specs=[pl.BlockSpec((BLOCK, ATT_WIDTH), lambda n: (n, Q_COL // ATT_WIDTH)), kv(-1), kv(0), kv(1), gain, gain,
                  sink_spec, bias_spec, pl.BlockSpec((BLOCK, ATT_WIDTH), lambda n: (n, HG_WIDTH // ATT_WIDTH))],
        out_specs=[pl.BlockSpec((BLOCK, ATT_WIDTH), lambda n: (n, 0)), win, win, bias_spec, sink_spec,
                   pl.BlockSpec((KV, 1, dh), lambda n: (0, 0, 0))],
        out_shape=[jax.ShapeDtypeStruct((S, ATT_WIDTH), BF16), wshape, wshape,
                   jax.ShapeDtypeStruct((KV, 3 * BLOCK, G * BLOCK), F32),
                   jax.ShapeDtypeStruct((ATT_Q_HEADS, 1, BLOCK), F32),
                   jax.ShapeDtypeStruct((KV, 1, dh), F32)],
        compiler_params=_params("arbitrary"),
    )(z, z, z, z, q_g, k_g, sink, bias_t, do)


def _attn_kv_reduce(dkw, dvw, z, k_g, name):
    S = z.shape[0]
    nb = S // BLOCK
    dh = ATT_HEAD_DIM
    kb = min(8, nb)
    steps = nb // kb

    def body(a_lo, a, a_hi, b_lo, b, b_hi, kv_ref, kg_ref, dkv_ref, dkg_ref):
        n = pl.program_id(0)

        @pl.when(n == 0)
        def _():
            dkg_ref[...] = jnp.zeros_like(dkg_ref)

        lo = jnp.where(n > 0, 1.0, 0.0)
        hi = jnp.where(n < steps - 1, 1.0, 0.0)

        def overlap_add(w, w_lo, w_hi, j, i):
            before = lo * w_lo[j] if i == 0 else w[j, i - 1, 2 * BLOCK:3 * BLOCK, :]
            after = hi * w_hi[j] if i == kb - 1 else w[j, i + 1, 0:BLOCK, :]
            return w[j, i, BLOCK:2 * BLOCK, :] + before + after

        dkg = [jnp.zeros((1, dh), F32) for _ in range(ATT_KV_HEADS)]
        for i in range(kb):
            rows = slice(i * BLOCK, (i + 1) * BLOCK)
            dks, dvs = [], []
            for j in range(ATT_KV_HEADS):
                dkn = overlap_add(a, a_lo, a_hi, j, i)
                dvs.append(overlap_add(b, b_lo, b_hi, j, i))
                khat, rstd = _rms_rows(kv_ref[rows, j * dh:(j + 1) * dh])
                dkg[j] = dkg[j] + jnp.sum(dkn * khat, axis=0, keepdims=True)
                dkh = dkn * kg_ref[...]
                dks.append(rstd * (dkh - khat * jnp.mean(dkh * khat, axis=-1, keepdims=True)))
            dkv_ref[rows, :] = jnp.concatenate(dks + dvs, axis=1).astype(dkv_ref.dtype)
        for j in range(ATT_KV_HEADS):
            dkg_ref[j] += dkg[j]

    main = pl.BlockSpec((ATT_KV_HEADS, kb, 3 * BLOCK, dh), lambda n: (0, n, 0, 0))
    halo_lo = pl.BlockSpec((ATT_KV_HEADS, None, BLOCK, dh), lambda n: (0, jnp.maximum(n * kb - 1, 0), 2, 0))
    halo_hi = pl.BlockSpec((ATT_KV_HEADS, None, BLOCK, dh), lambda n: (0, jnp.minimum(n * kb + kb, nb - 1), 0, 0))
    return pl.pallas_call(
        body, name=name, grid=(steps,),
        in_specs=[halo_lo, main, halo_hi, halo_lo, main, halo_hi,
                  pl.BlockSpec((kb * BLOCK, 2 * KV_WIDTH), lambda n: (n, KV_COL // (2 * KV_WIDTH))),
                  pl.BlockSpec((1, dh), lambda n: (0, 0))],
        out_specs=[pl.BlockSpec((kb * BLOCK, 2 * KV_WIDTH), lambda n: (n, 0)),
                   pl.BlockSpec((ATT_KV_HEADS, 1, dh), lambda n: (0, 0, 0))],
        out_shape=[jax.ShapeDtypeStruct((S, 2 * KV_WIDTH), BF16), jax.ShapeDtypeStruct((ATT_KV_HEADS, 1, dh), F32)],
        compiler_params=_params("arbitrary"),
    )(dkw, dkw, dkw, dvw, dvw, dvw, z, k_g)


def _ada_wgrad(c_act_t, dm, name):
    D, nbatch = c_act_t.shape
    n = dm.shape[1]
    tr = 256

    def body(c_ref, dm_ref, o_ref):
        cv, dv = c_ref[...], dm_ref[...]
        acc = cv[:, 0:1] * dv[0:1, :]
        for b in range(1, nbatch):
            acc = acc + cv[:, b:b + 1] * dv[b:b + 1, :]
        o_ref[...] = acc

    return pl.pallas_call(
        body, name=name, grid=(D // tr,),
        in_specs=[pl.BlockSpec((tr, nbatch), lambda i: (i, 0)), pl.BlockSpec((nbatch, n), lambda i: (0, 0))],
        out_specs=pl.BlockSpec((tr, n), lambda i: (i, 0)), out_shape=jax.ShapeDtypeStruct((D, n), F32),
        compiler_params=_params("parallel"),
    )(c_act_t, dm)


def _to_bf16(w, name):
    R, Cn = w.shape
    tr = _row_tile(R)

    def body(w_ref, o_ref):
        o_ref[...] = w_ref[...].astype(BF16)

    blk = pl.BlockSpec((tr, Cn), lambda i: (i, 0))
    return pl.pallas_call(
        body, name=name, grid=(R // tr,), in_specs=[blk], out_specs=blk, out_shape=jax.ShapeDtypeStruct((R, Cn), BF16),
        compiler_params=_params("parallel"),
    )(w)


def _adamw(w, g, m, v, name):
    R, Cn = w.shape
    tr = R
    for cand in (256, 128, 64, 32, 16, 8):
        if R % cand == 0:
            tr = cand
            break

    def body(w_ref, g_ref, m_ref, v_ref, d_ref, nm_ref, nv_ref):
        gv = g_ref[...]
        m_new = ADAM_B1 * m_ref[...] + (1.0 - ADAM_B1) * gv
        v_new = ADAM_B2 * v_ref[...] + (1.0 - ADAM_B2) * (gv * gv)
        m_hat = m_new / (1.0 - ADAM_B1 ** ADAM_STEP)
        v_hat = v_new / (1.0 - ADAM_B2 ** ADAM_STEP)
        d_ref[...] = -ADAM_LR * (m_hat / (jnp.sqrt(v_hat) + ADAM_EPS) + ADAM_WD * w_ref[...])
        nm_ref[...] = m_new
        nv_ref[...] = v_new

    blk = pl.BlockSpec((tr, Cn), lambda i: (i, 0))
    shp = jax.ShapeDtypeStruct((R, Cn), F32)
    return pl.pallas_call(
        body, name=name, grid=(R // tr,), in_specs=[blk] * 4, out_specs=[blk] * 3, out_shape=[shp] * 3,
        compiler_params=_params("parallel"),
    )(w, g, m, v)


def _place():
    return lax.axis_index("x"), lax.axis_index("y"), lax.axis_index("c")


def _flip(place, k):
    x, y, c = place
    return (1 - x if k & 4 else x, 1 - y if k & 2 else y, 1 - c if k & 1 else c)


def _dev_index(place):
    x, y, c = place
    return 4 * x + 2 * y + c


def _chip_index(place):
    return 2 * place[0] + place[1]


def _gather8(x_ref, out_ref, send_sems, recv_sems, local_sem):
    me = _place()
    mine = pltpu.make_async_copy(x_ref, out_ref.at[_dev_index(me)], local_sem)
    mine.start()

    def copy(k, origin, to):
        return pltpu.make_async_remote_copy(
            src_ref=x_ref, dst_ref=out_ref.at[_dev_index(origin)], send_sem=send_sems.at[k - 1],
            recv_sem=recv_sems.at[k - 1], device_id=to, device_id_type=MESH)

    sends = [copy(k, me, _flip(me, k)) for k in range(1, 8)]
    for cp in sends:
        cp.start()
    for k in range(1, 8):
        copy(k, _flip(me, k), me).wait_recv()
    for cp in sends:
        cp.wait_send()
    mine.wait()


def _allgather8(x, name, reduce=False):
    R, Cn = x.shape

    def body(x_ref, *rest):
        if reduce:
            out_ref, sum_ref, send_sems, recv_sems, local_sem = rest
        else:
            out_ref, send_sems, recv_sems, local_sem = rest
        _gather8(x_ref, out_ref, send_sems, recv_sems, local_sem)
        if reduce:
            acc = out_ref[0]
            for i in range(1, 8):
                acc = acc + out_ref[i]
            sum_ref[...] = acc

    vm = pl.BlockSpec(memory_space=pltpu.VMEM)
    outs = [jax.ShapeDtypeStruct((8, R, Cn), F32)] + ([jax.ShapeDtypeStruct((R, Cn), F32)] if reduce else [])
    res = pl.pallas_call(
        body, name=name, in_specs=[vm], out_specs=[vm] * len(outs), out_shape=outs,
        scratch_shapes=[pltpu.SemaphoreType.DMA((7,)), pltpu.SemaphoreType.DMA((7,)), pltpu.SemaphoreType.DMA],
    )(x)
    return res if reduce else res[0]


def _prologue(small, w_ada, b_ada, w_shard, to_cast, name):
    R, Cn = small.shape
    n_mod = w_ada.shape[1]
    n_w = len(to_cast)
    big = _gather_over_ici([w_shard])

    def body(*refs):
        (small_ref, wada_ref, b_ref, shard_ref), refs = refs[:4], refs[4:]
        wide_refs, refs = refs[:n_w], refs[n_w:]
        (small_all_ref, mods_all_ref, gathered_ref), refs = refs[:3], refs[3:]
        narrow_refs, refs = refs[:n_w], refs[n_w:]
        (mods_ref, send1, recv1, send2, recv2, local_sems), refs = refs[:6], refs[6:]
        wide_bufs, narrow_bufs, (load_sems, store_sems, big_send, big_recv) = refs[:n_w], refs[n_w:2 * n_w], refs[2 * n_w:]
        big.start([shard_ref], [gathered_ref], big_send, big_recv)
        loads = [pltpu.make_async_copy(w, buf, load_sems.at[i]) for i, (w, buf) in enumerate(zip(wide_refs, wide_bufs))]
        for cp in loads:
            cp.start()
        stores = []
        for i, cp in enumerate(loads):
            cp.wait()
            rows = wide_bufs[i].shape[0]
            tr = _row_tile(rows)

            def cast(j, carry, i=i, tr=tr):
                rs = pl.ds(pl.multiple_of(j * tr, tr), tr)
                narrow_bufs[i][rs, :] = wide_bufs[i][rs, :].astype(BF16)
                return carry

            lax.fori_loop(0, rows // tr, cast, 0)
            stores.append(pltpu.make_async_copy(narrow_bufs[i], narrow_refs[i], store_sems.at[i]))
            stores[-1].start()
        _gather8(small_ref, small_all_ref, send1, recv1, local_sems.at[0])
        c_all = jnp.concatenate([small_all_ref[d, 0:1, :] for d in range(8)], axis=0)
        c_act = c_all * _sigmoid(c_all)
        mods_ref[...] = _nn(c_act, wada_ref[...], precision=lax.Precision.HIGHEST) + b_ref[...]
        _gather8(mods_ref, mods_all_ref, send2, recv2, local_sems.at[1])
        for cp in stores:
            cp.wait()
        big.finish([shard_ref], [gathered_ref], big_send, big_recv)

    vm = pl.BlockSpec(memory_space=pltpu.VMEM)
    seven = pltpu.SemaphoreType.DMA((7,))
    res = pl.pallas_call(
        body, name=name, in_specs=[vm, vm, vm, ANY] + [ANY] * n_w, out_specs=[vm, vm, ANY] + [ANY] * n_w,
        out_shape=[jax.ShapeDtypeStruct((8, R, Cn), F32), jax.ShapeDtypeStruct((8, 8, n_mod), F32)] + big.out_shapes
        + [jax.ShapeDtypeStruct(w.shape, BF16) for w in to_cast],
        scratch_shapes=[pltpu.VMEM((8, n_mod), F32), seven, seven, seven, seven, pltpu.SemaphoreType.DMA((2,))]
        + [pltpu.VMEM(w.shape, F32) for w in to_cast] + [pltpu.VMEM(w.shape, BF16) for w in to_cast]
        + [pltpu.SemaphoreType.DMA((n_w,)), pltpu.SemaphoreType.DMA((n_w,))] + big.sem_shapes(),
        compiler_params=pltpu.CompilerParams(vmem_limit_bytes=VMEM_LIMIT_BYTES),
    )(small, w_ada, b_ada, w_shard, *to_cast)
    return res[0], res[1], res[2], list(res[3:])


def _symmetric_plan(copies):
    def plan(in_refs, out_refs):
        sends = [(src, dst, i, to) for i, (src, dst, to) in enumerate(copies(in_refs, out_refs))]
        return sends, [(dst, i) for _, dst, i, _ in sends]
    return plan


def _halves_exchange(grads):
    def copies(in_refs, out_refs):
        me = _place()
        return [(g.at[kk, 1 - me[2]], got.at[kk], _flip(me, 1)) for g, got in zip(in_refs, out_refs) for kk in range(N_CHIPS)]

    return _Exchange(grads, [jax.ShapeDtypeStruct((N_CHIPS,) + g.shape[2:], g.dtype) for g in grads],
                     N_CHIPS * len(grads), _symmetric_plan(copies))


def _chips_exchange(parts):
    def copies(in_refs, out_refs):
        me = _place()
        return [(p.at[_chip_index(_flip(me, 2 * j))], got.at[j - 1], _flip(me, 2 * j))
                for p, got in zip(in_refs, out_refs) for j in (1, 2, 3)]

    return _Exchange(parts, [jax.ShapeDtypeStruct((3,) + p.shape[1:], p.dtype) for p in parts], 3 * len(parts),
                     _symmetric_plan(copies))


def _siblings_exchange(halves):
    def copies(in_refs, out_refs):
        sibling = _flip(_place(), 1)
        return [(h, got, sibling) for h, got in zip(in_refs, out_refs)]

    return _Exchange(halves, [jax.ShapeDtypeStruct(h.shape, h.dtype) for h in halves], len(halves), _symmetric_plan(copies))


def _ici_gather_plan(n, base=0):
    def plan(in_refs, out_refs):
        me = _place()
        c = me[2]
        sends, lands = [], []
        for a, (w, out) in enumerate(zip(in_refs[:n], out_refs)):
            for j in (1, 2, 3):
                i = base + 3 * a + j - 1
                sends.append((w.at[c], out.at[_chip_index(me), c], i, _flip(me, 2 * j)))
                lands.append((out.at[_chip_index(_flip(me, 2 * j)), c], i))
        return sends, lands
    return plan


def _d2d_gather_plan(n, base=0):
    def plan(in_refs, out_refs):
        me = _place()
        c = me[2]
        sibling = _flip(me, 1)
        mine = _chip_index(me)
        sends, lands = [], []
        for a, (w, out) in enumerate(zip(in_refs[:n], out_refs)):
            moves = [(w.at[c], (mine, c)), (w.at[1 - c], (mine, 1 - c))]
            moves += [(out.at[_chip_index(_flip(me, 2 * j)), c], (_chip_index(_flip(me, 2 * j)), c)) for j in (1, 2, 3)]
            for k, (src, (chip, half)) in enumerate(moves):
                sends.append((src, out.at[chip, half], base + 5 * a + k, sibling))
            blocks = [(mine, 1 - c), (mine, c)] + [(_chip_index(_flip(me, 2 * j)), 1 - c) for j in (1, 2, 3)]
            lands += [(out.at[chip, half], base + 5 * a + k) for k, (chip, half) in enumerate(blocks)]
        return sends, lands
    return plan


def _gathered_shapes(shards):
    return [jax.ShapeDtypeStruct((N_CHIPS,) + s.shape, s.dtype) for s in shards]


def _gather_over_ici(shards):
    return _Exchange(shards, _gathered_shapes(shards), 3 * len(shards), _ici_gather_plan(len(shards)))


def _gather_over_d2d(shards, gathered):
    n = len(shards)
    return _Exchange(list(shards) + list(gathered), [jax.ShapeDtypeStruct(g.shape, g.dtype) for g in gathered], 5 * n,
                     _d2d_gather_plan(n), aliases={n + a: a for a in range(n)})


def _gather_in_one(shards):
    n = len(shards)
    return _Exchange(shards, _gathered_shapes(shards), 8 * n, _ici_gather_plan(n), then=_d2d_gather_plan(n, base=3 * n))


def _row_tile(rows):
    for cand in (256, 176, 128, 64, 32, 16, 8):
        if rows % cand == 0:
            return cand
    return rows


def _pair_sum(core, grad, theirs, name):
    N, _, R, Cn = grad.shape
    tr = R

    def body(core_ref, g_ref, t_ref, o_ref, ob_ref):
        s = g_ref[...] + t_ref[...]
        o_ref[...] = s
        ob_ref[...] = s.astype(BF16)

    out = pl.BlockSpec((None, tr, Cn), lambda k, i, core_ref: (k, i, 0))
    return pl.pallas_call(
        body, name=name,
        grid_spec=pltpu.PrefetchScalarGridSpec(
            num_scalar_prefetch=1, grid=(N, R // tr),
            in_specs=[pl.BlockSpec((None, None, tr, Cn), lambda k, i, core_ref: (k, core_ref[0], i, 0)),
                      pl.BlockSpec((None, tr, Cn), lambda k, i, core_ref: (k, i, 0))],
            out_specs=[out, out]),
        out_shape=[jax.ShapeDtypeStruct((N, R, Cn), F32), jax.ShapeDtypeStruct((N, R, Cn), BF16)],
        compiler_params=_params("parallel", "parallel"),
    )(core, grad, theirs)


def _chip_sum(chip, parts, landed, name):
    _, R, Cn = parts.shape
    tr = R

    def body(chip_ref, p_ref, l_ref, o_ref):
        o_ref[...] = ((p_ref[...] + l_ref[0].astype(F32)) + l_ref[1].astype(F32)) + l_ref[2].astype(F32)

    return pl.pallas_call(
        body, name=name,
        grid_spec=pltpu.PrefetchScalarGridSpec(
            num_scalar_prefetch=1, grid=(R // tr,),
            in_specs=[pl.BlockSpec((None, tr, Cn), lambda i, chip_ref: (chip_ref[0], i, 0)),
                      pl.BlockSpec((3, tr, Cn), lambda i, chip_ref: (0, i, 0))],
            out_specs=pl.BlockSpec((tr, Cn), lambda i, chip_ref: (i, 0))),
        out_shape=jax.ShapeDtypeStruct((R, Cn), F32), compiler_params=_params("parallel"),
    )(chip, parts, landed)


def _pair_sums(core, grads, theirs, tag):
    return [_pair_sum(core, g, t, f"{tag}_pair_sum_{i}") for i, (g, t) in enumerate(zip(grads, theirs))]


def _chip_sums(chip, parts, landed, tag):
    return [_chip_sum(chip, p[0], l, f"{tag}_chip_sum_{i}") for i, (p, l) in enumerate(zip(parts, landed))]


def _by_chip_rows(g):
    return g.reshape(N_CHIPS, 2, g.shape[0] // (2 * N_CHIPS), g.shape[1])


def _by_chip_cols(g):
    return g.reshape(N_CHIPS, 2, g.shape[1] // 2, g.shape[2])


def _adamw_halves(core, w, g_mine, g_theirs, m, v, name):
    R2, Cn = w.shape
    r = R2 // 2
    tr = _row_tile(r)
    nt = r // tr

    def body(core_ref, w_ref, gm_ref, gt_ref, m_ref, v_ref, g_ref, d_ref, nm_ref, nv_ref):
        gv = jnp.where(pl.program_id(0) == core_ref[0], gm_ref[...], gt_ref[...])
        g_ref[...] = gv
        m_new = ADAM_B1 * m_ref[...] + (1.0 - ADAM_B1) * gv
        v_new = ADAM_B2 * v_ref[...] + (1.0 - ADAM_B2) * (gv * gv)
        m_hat = m_new / (1.0 - ADAM_B1 ** ADAM_STEP)
        v_hat = v_new / (1.0 - ADAM_B2 ** ADAM_STEP)
        d_ref[...] = -ADAM_LR * (m_hat / (jnp.sqrt(v_hat) + ADAM_EPS) + ADAM_WD * w_ref[...])
        nm_ref[...] = m_new
        nv_ref[...] = v_new

    full = pl.BlockSpec((tr, Cn), lambda hf, i, core_ref: (hf * nt + i, 0))
    half = pl.BlockSpec((tr, Cn), lambda hf, i, core_ref: (i, 0))
    shp = jax.ShapeDtypeStruct((R2, Cn), F32)
    return pl.pallas_call(
        body, name=name,
        grid_spec=pltpu.PrefetchScalarGridSpec(
            num_scalar_prefetch=1, grid=(2, nt), in_specs=[full, half, half, full, full], out_specs=[full] * 4),
        out_shape=[shp] * 4, compiler_params=_params("parallel", "parallel"),
    )(core, w, g_mine, g_theirs, m, v)


def _pad_row(v, width):
    v = v.reshape(1, -1)
    return jnp.pad(v, ((0, 0), (0, width - v.shape[1])))


def _ffn1_forward(x, ng, shift, scale, gate, w_in_shard, w_in_partly, w_out_shard, gather, next_norm):
    (h,), (w_in4,) = _rmsmod_fwd(x, ng, shift, scale, "ffn1_norm", exchange=_gather_over_d2d([w_in_shard], [w_in_partly]))
    w_in4 = w_in4.reshape(N_CHIPS, D_MODEL, FF_SHARD)
    (zg, zu, a), (partly, (w_out4,)) = _ffn_in_fwd(
        h, w_in4, "ffn1_in", exchange=[_gather_over_ici(gather), _gather_in_one([w_out_shard])])
    w_out = w_out4.reshape(D_FF, D_MODEL)
    (x_new, f, h_next), gathered = _proj_out_fwd([a], w_out, x, gate, 0.5, "ffn1_out", next_norm=next_norm,
                                                 exchange=_gather_over_d2d(gather, partly))
    return x_new, (h, zg, zu, a, f), w_in4, w_out, gathered, h_next


def _ffn_backward(df, saved, w_in4, w_out, core, chip, tag, riding=None, norm=None, in_first=False):
    h, zg, zu, a = saved[:4]
    rode = None
    if riding:
        (dzg, dzu), rode = _dact_bwd(df, w_out, zg, zu, f"{tag}_dact", exchange=riding)
    else:
        dzg, dzu = _dact_bwd(df, w_out, zg, zu, f"{tag}_dact")

    def dw_out(exchange=None):
        outs = _wgrad(a, [df], df.shape[1], f"{tag}_dw_out", exchange=exchange)
        (dw,), landed = outs if exchange else (outs, None)
        return [_by_chip_rows(dw.reshape(a.shape[1], df.shape[1]))], landed

    def dw_in(exchange=None):
        outs = _wgrad(h, [dzg, dzu], FF_SHARD, f"{tag}_dw_in", exchange=exchange)
        (dw,), landed = outs if exchange else (outs, None)
        return [_by_chip_cols(dw.reshape(N_CHIPS, h.shape[1], FF_SHARD))], landed

    (first, tag_1), (second, tag_2) = ((dw_in, "in"), (dw_out, "out"))[::1 if in_first else -1]
    g_1, _ = first()
    g_2, theirs_1 = second(_halves_exchange(g_1))
    parts_1 = _pair_sums(core, g_1, theirs_1, f"{tag}_{tag_1}")
    dh_outs, (theirs_2, landed_1) = _ffn_in_dgrad(
        dzg, dzu, w_in4, f"{tag}_dh", norm=norm, exchange=[_halves_exchange(g_2), _chips_exchange([parts_1[0][1]])])
    parts_2 = _pair_sums(core, g_2, theirs_2, f"{tag}_{tag_2}")
    return dh_outs, parts_2, _chip_sums(chip, parts_1, landed_1, f"{tag}_{tag_1}"), rode


def kernel(x, c, w_ada, b_ada, norm_g, w_ffn1_in, w_ffn1_out, w_ffn2_in, w_ffn2_out, w_mix_in, w_mix_out, hgrn_lb, hgrn_norm_g, qk_norm_g, attn_sink, rel_bias, loss_target, m_w_ada, m_b_ada, m_norm_g, m_w_ffn1_in, m_w_ffn1_out, m_w_ffn2_in, m_w_ffn2_out, m_w_mix_in, m_w_mix_out, m_hgrn_lb, m_hgrn_norm_g, m_qk_norm_g, m_attn_sink, m_rel_bias, v_w_ada, v_b_ada, v_norm_g, v_w_ffn1_in, v_w_ffn1_out, v_w_ffn2_in, v_w_ffn2_out, v_w_mix_in, v_w_mix_out, v_hgrn_lb, v_hgrn_norm_g, v_qk_norm_g, v_attn_sink, v_rel_bias):
    D = D_MODEL
    S = x.shape[1]
    place = (lax.axis_index("x"), lax.axis_index("y"), lax.axis_index("c"))
    me, my_chip = _dev_index(place), _chip_index(place)
    x0 = x[0]
    target = loss_target[0]

    core_arr = jnp.reshape(place[2], (1,)).astype(jnp.int32)
    chip_arr = jnp.reshape(my_chip, (1,)).astype(jnp.int32)

    def halves(w):
        return w.reshape(2, w.shape[0] // 2, w.shape[1])

    small = jnp.concatenate([_pad_row(c, D), _pad_row(norm_g, D), _pad_row(hgrn_lb, D), jnp.zeros((5, D), F32)], axis=0)
    n_ada = w_ada.shape[2]
    b_mine = lax.dynamic_slice_in_dim(b_ada, my_chip * n_ada, n_ada, axis=1)
    w1_in_shard = halves(_to_bf16(w_ffn1_in[0], "w_ffn1_in_to_bf16"))
    small_all, mods_parts, w1_in_partly, shards = _prologue(
        small, w_ada[0], b_mine, w1_in_shard, [w_ffn1_out[0], w_mix_in[0], w_mix_out[0], w_ffn2_in[0], w_ffn2_out[0]], "prologue")
    w1_out_shard, mix_shards, ffn2_shards = halves(shards[0]), [halves(w) for w in shards[1:3]], [halves(w) for w in shards[3:5]]
    c_all = small_all[:, 0, :]
    by_chip = small_all[0::2]
    norm_g_full = by_chip[:, 1, :3 * 256].reshape(N_CHIPS, 3, 256).transpose(1, 0, 2).reshape(3, D)
    lb_raw = by_chip[:, 2, :2 * 2 * 128].reshape(N_CHIPS, 2, 2, 128).transpose(1, 2, 0, 3).reshape(2, 2, HG_WIDTH)
    lb_logit = lb_raw[:, 0, :] - lb_raw[:, 1, :]
    lb = jax.nn.sigmoid(lb_logit)
    one_minus_lb = jax.nn.sigmoid(-lb_logit)
    lb_f = jnp.stack([lb[0], one_minus_lb[0]])
    lb_b = jnp.stack([lb[1], one_minus_lb[1]])

    c_act_all = c_all * jax.nn.sigmoid(c_all)
    mods_all = mods_parts[0::2].transpose(1, 0, 2).reshape(8, N_MOD * D)
    mods = lax.dynamic_slice_in_dim(mods_all, me, 1, axis=0)
    sh1, sc1, g1, sh2, sc2, g2, sh3, sc3, g3 = [mods[:, i * D:(i + 1) * D] for i in range(N_MOD)]

    x1, saved1, w1_in, w1_out, gathered, h2 = _ffn1_forward(
        x0, norm_g_full[0:1], sh1, sc1, g1, w1_in_shard, w1_in_partly, w1_out_shard, mix_shards, (norm_g_full[1:2], sh2, sc2))
    wm_in = gathered[0].reshape(N_CHIPS, D, D_IN // N_CHIPS).transpose(1, 0, 2).reshape(D, D_IN)
    wm_out = gathered[1].reshape(D, D)

    z = _matmul_nn(h2, wm_in, F32, 256, "mix_in")
    (of, st_f, ob, st_b), gathered = _hgrn_fwd_both(z, (lb_f, lb_b), "hgrn_fwd", exchange=_gather_in_one(ffn2_shards))
    w2_in = gathered[0].reshape(N_CHIPS, D, FF_SHARD)
    w2_out = gathered[1].reshape(D_FF, D)
    o_h = _hgrn_post_fwd(of, ob, z, hgrn_norm_g, "hgrn_post")

    q_g, k_g = qk_norm_g[0, 0:1], qk_norm_g[0, 1:2]
    sink_b = jnp.broadcast_to(attn_sink.reshape(ATT_Q_HEADS, 1, 1), (ATT_Q_HEADS, 1, BLOCK))
    bias = _bias_table(rel_bias, "bias_table")
    o_a = _attn_fwd(z, q_g, k_g, sink_b, bias, "attn_fwd")
    x2, mixed, h3 = _proj_out_fwd([o_h, o_a], wm_out, x1, g2, 1.0, "mix_out", next_norm=(norm_g_full[2:3], sh3, sc3))

    zg3, zu3, a3 = _ffn_in_fwd(h3, w2_in, "ffn2_in")
    dx3, df3, dg3, sq_cols = _proj_out_loss(a3, w2_out, x2, g3, 0.5, target, "ffn2_out_loss")
    loss_mine = 0.5 * jnp.sum(sq_cols) / D

    (dx2, dsh3, dsc3, dng3, dmixed, dg2), parts2, mine2_out, _ = _ffn_backward(
        df3, (h3, zg3, zu3, a3), w2_in, w2_out, core_arr, chip_arr, "ffn2",
        norm=_NormBwd(x2, norm_g_full[2:3], sc3, dx3, below=(mixed, g2, 1.0)))

    (do_cat,) = _matmul_nt([dmixed], wm_out, ROW_TILE, "mix_out_dgrad")
    dwm_out = _wgrad_rows([o_h, o_a], dmixed, "mix_out_dw").reshape(D, D)

    do_sum, dgr, d_hnorm = _hgrn_post_bwd(do_cat, of, ob, z, hgrn_norm_g, "hgrn_post_bwd")
    (dq_f, dff, dv_f, doml_f), landed2 = _hgrn_bwd(z, lb_f, do_sum, st_f, 0, "hgrn_bwd_f",
                                                   exchange=_chips_exchange([p[1] for p in parts2]))
    mine2 = _chip_sums(chip_arr, parts2, landed2, "ffn2_in") + mine2_out
    (dhq, dfb, dhi, doml_b), theirs2 = _hgrn_bwd(z, lb_b, do_sum, st_b, 1, "hgrn_bwd_b", acc=(dq_f, dv_f),
                                                 exchange=_siblings_exchange(mine2))

    daq, dkw, dvw, ds_sum, dsink, dqg = _attn_bwd(z, q_g, k_g, sink_b, bias, do_cat, "attn_bwd")
    dkv, dkg = _attn_kv_reduce(dkw, dvw, z, k_g, "attn_kv_reduce")
    d_rel_bias = jnp.sum(_bias_grad(ds_sum, "bias_grad"), axis=-1).T
    dz = [dhq, dff, dfb, dhi, dgr, daq, dkv]
    dwm_in = _wgrad_pieces(h2, dz, 2 * KV_WIDTH, "mix_in_dw").transpose(1, 0, 2).reshape(D, D_IN)
    wide = D_IN // N_CHIPS
    grads_m = [_by_chip_cols(dwm_in.reshape(D, N_CHIPS, wide).transpose(1, 0, 2)), _by_chip_rows(dwm_out)]
    (dx1, dsh2, dsc2, dng2, df1, dg1), theirs_m = _matmul_nt(
        dz, wm_in, ROW_TILE, "mix_in_dgrad", exchange=_halves_exchange(grads_m),
        norm=_NormBwd(x1, norm_g_full[1:2], sc2, dx2, below=(saved1[4], g1, 0.5)))
    parts_m = _pair_sums(core_arr, grads_m, theirs_m, "mix")

    (dh1,), parts1, mine1_in, landed_m = _ffn_backward(df1, saved1, w1_in, w1_out, core_arr, chip_arr, "ffn1",
                                                       riding=_chips_exchange([p[1] for p in parts_m]), in_first=True)
    mine_m = _chip_sums(chip_arr, parts_m, landed_m, "mix")
    (dx0, dsh1, dsc1, dng1), landed1 = _rmsmod_bwd(dh1, _NormBwd(x0, norm_g_full[0:1], sc1, dx1), "ffn1_norm_bwd",
                                                   exchange=_chips_exchange([p[1] for p in parts1]))
    mine1 = mine1_in + _chip_sums(chip_arr, parts1, landed1, "ffn1_out")
    theirs_1m = list(_run_exchange(_siblings_exchange(mine1 + mine_m), "siblings_exchange"))
    reduced = list(zip(mine1 + mine2 + mine_m, theirs_1m[:2] + list(theirs2) + theirs_1m[2:]))

    dlb = -jnp.concatenate([doml_f, doml_b], axis=0)
    dlb_raw = dlb * lb * one_minus_lb
    d_hgrn_lb = jnp.stack([dlb_raw, -dlb_raw], axis=1)
    d_qk = jnp.concatenate([jnp.sum(dqg, axis=0), jnp.sum(dkg, axis=0)], axis=0)
    dmods = jnp.concatenate([dsh1, dsc1, dg1, dsh2, dsc2, dg2, dsh3, dsc3, dg3], axis=0)
    packed = jnp.concatenate(
        [dmods, dng1, dng2, dng3, d_hgrn_lb.reshape(2, D), _pad_row(d_hnorm, D), _pad_row(d_qk, D),
         _pad_row(dsink[:, 0, 0], D), _pad_row(d_rel_bias, D), _pad_row(loss_mine, D)], axis=0)
    packed = jnp.pad(packed, ((0, 24 - packed.shape[0]), (0, 0)))
    packed_all, packed_sum = _allgather8(packed, "small_grads_allgather", reduce=True)
    dmods_all = packed_all[:, 0:N_MOD, :].reshape(8, N_MOD * D)
    g_b_ada = packed_sum[0:N_MOD].reshape(1, N_MOD * D)
    g_norm_full = packed_sum[9:12]
    g_norm_g = lax.dynamic_slice_in_dim(g_norm_full, my_chip * 256, 256, axis=1).reshape(1, 3, 256)
    g_hgrn_lb = lax.dynamic_slice_in_dim(packed_sum[12:14].reshape(2, 2, HG_WIDTH), my_chip * 128, 128, axis=2)
    g_hgrn_norm_g = packed_sum[14:15, :HG_WIDTH]
    g_qk_norm_g = packed_sum[15, :2 * ATT_HEAD_DIM].reshape(1, 2, ATT_HEAD_DIM)
    g_attn_sink = packed_sum[16:17, :ATT_Q_HEADS]
    g_rel_bias = packed_sum[17, :NUM_BUCKETS * ATT_Q_HEADS].reshape(NUM_BUCKETS, ATT_Q_HEADS)
    loss = packed_sum[18, 0]

    dm_mine = lax.dynamic_slice_in_dim(dmods_all, my_chip * n_ada, n_ada, axis=1)
    g_w_ada = _ada_wgrad(c_act_all.T, dm_mine, "ada_wgrad")[None]

    def big(w, g, m, v, name):
        d, nm, nv = _adamw(w[0], g[0], m[0], v[0], name)
        return d[None], nm[None], nv[None]

    def big_halves(w, g_pair, m, v, name):
        g, d, nm, nv = _adamw_halves(core_arr, w[0], g_pair[0], g_pair[1], m[0], v[0], name)
        return g[None], (d[None], nm[None], nv[None])

    g_w1_in, u_w1_in = big_halves(w_ffn1_in, reduced[0], m_w_ffn1_in, v_w_ffn1_in, "adamw_w_ffn1_in")
    g_w1_out, u_w1_out = big_halves(w_ffn1_out, reduced[1], m_w_ffn1_out, v_w_ffn1_out, "adamw_w_ffn1_out")
    g_w2_in, u_w2_in = big_halves(w_ffn2_in, reduced[2], m_w_ffn2_in, v_w_ffn2_in, "adamw_w_ffn2_in")
    g_w2_out, u_w2_out = big_halves(w_ffn2_out, reduced[3], m_w_ffn2_out, v_w_ffn2_out, "adamw_w_ffn2_out")
    g_wm_in, u_wm_in = big_halves(w_mix_in, reduced[4], m_w_mix_in, v_w_mix_in, "adamw_w_mix_in")
    g_wm_out, u_wm_out = big_halves(w_mix_out, reduced[5], m_w_mix_out, v_w_mix_out, "adamw_w_mix_out")

    smalls = [(b_ada, g_b_ada, m_b_ada, v_b_ada), (norm_g, g_norm_g, m_norm_g, v_norm_g), (hgrn_lb, g_hgrn_lb, m_hgrn_lb, v_hgrn_lb),
              (hgrn_norm_g, g_hgrn_norm_g, m_hgrn_norm_g, v_hgrn_norm_g), (qk_norm_g, g_qk_norm_g, m_qk_norm_g, v_qk_norm_g),
              (attn_sink, g_attn_sink, m_attn_sink, v_attn_sink), (rel_bias, g_rel_bias, m_rel_bias, v_rel_bias)]
    sizes = [t[0].size for t in smalls]
    total = sum(sizes)
    rows = -(-total // 128)
    rows = -(-rows // 8) * 8

    def pack(i):
        flat = jnp.concatenate([t[i].reshape(-1) for t in smalls])
        fill = 1.0 if i == 3 else 0.0
        return jnp.pad(flat, (0, rows * 128 - total), constant_values=fill).reshape(rows, 128)

    packed_out = _adamw(pack(0), pack(1), pack(2), pack(3), "adamw_small")

    def unpack(flat2d):
        flat = flat2d.reshape(-1)
        outs, off = [], 0
        for t, n in zip(smalls, sizes):
            outs.append(flat[off:off + n].reshape(t[0].shape))
            off += n
        return outs

    d_small, m_small, v_small = [unpack(t) for t in packed_out]

    upd = {
        "w_ada": big(w_ada, g_w_ada, m_w_ada, v_w_ada, "adamw_w_ada"),
        "w_ffn1_in": u_w1_in, "w_ffn1_out": u_w1_out, "w_ffn2_in": u_w2_in, "w_ffn2_out": u_w2_out,
        "w_mix_in": u_wm_in, "w_mix_out": u_wm_out,
    }
    small_names = ["b_ada", "norm_g", "hgrn_lb", "hgrn_norm_g", "qk_norm_g", "attn_sink", "rel_bias"]
    for i, nme in enumerate(small_names):
        upd[nme] = (d_small[i], m_small[i], v_small[i])
    grads = {
        "w_ada": g_w_ada, "b_ada": g_b_ada, "norm_g": g_norm_g, "w_ffn1_in": g_w1_in, "w_ffn1_out": g_w1_out,
        "w_ffn2_in": g_w2_in, "w_ffn2_out": g_w2_out, "w_mix_in": g_wm_in, "w_mix_out": g_wm_out, "hgrn_lb": g_hgrn_lb,
        "hgrn_norm_g": g_hgrn_norm_g, "qk_norm_g": g_qk_norm_g, "attn_sink": g_attn_sink, "rel_bias": g_rel_bias,
    }
    order = ["w_ada", "b_ada", "norm_g", "w_ffn1_in", "w_ffn1_out", "w_ffn2_in", "w_ffn2_out", "w_mix_in", "w_mix_out",
             "hgrn_lb", "hgrn_norm_g", "qk_norm_g", "attn_sink", "rel_bias"]
    return (loss, dx0[None], *[grads[k] for k in order], *[upd[k][0] for k in order], *[upd[k][1] for k in order],
            *[upd[k][2] for k in order])
```

```python
import functools
import math

import numpy as np
import jax
import jax.numpy as jnp
from jax import lax
from jax.experimental import pallas as pl
from jax.experimental.pallas import tpu as pltpu

F32, BF16 = jnp.float32, jnp.bfloat16

D_MODEL = 1024
D_FF = 2816
HG_HEADS, HG_DIM = 4, 128
HG_WIDTH = HG_HEADS * HG_DIM
ATT_Q_HEADS, ATT_KV_HEADS, ATT_HEAD_DIM = 8, 2, 64
ATT_GROUP = ATT_Q_HEADS // ATT_KV_HEADS
ATT_WIDTH = ATT_Q_HEADS * ATT_HEAD_DIM
KV_WIDTH = ATT_KV_HEADS * ATT_HEAD_DIM
WINDOW, BLOCK = 128, 128
NUM_BUCKETS, MAX_DISTANCE = 32, 128
N_MOD = 9
EPS = 1e-6
D_IN = 5 * HG_WIDTH + ATT_WIDTH + 2 * KV_WIDTH
ADAM_LR, ADAM_B1, ADAM_B2, ADAM_EPS, ADAM_WD, ADAM_STEP = 0.001, 0.9, 0.999, 1e-08, 0.01, 10

N_CHIPS = 4
FF_SHARD = 2 * D_FF // N_CHIPS
NEG = -1e30

VMEM_LIMIT_BYTES = 56 << 20
ROW_TILE = 512
HG_CHUNK = 16
HG_ROWS = 512

MESH = pl.DeviceIdType.MESH
ANY = pl.BlockSpec(memory_space=pl.ANY)


def _params(*sem):
    return pltpu.CompilerParams(dimension_semantics=sem, vmem_limit_bytes=VMEM_LIMIT_BYTES)


def _resident(shape, index_map):
    return pl.BlockSpec(shape, index_map, pipeline_mode=pl.Buffered(1))


def _dot(a, b, dims, precision=None):
    return lax.dot_general(a, b, (dims, ((), ())), precision=precision, preferred_element_type=F32)


def _nn(a, b, precision=None):
    return _dot(a, b, ((1,), (0,)), precision)


def _nt(a, b):
    return _dot(a, b, ((1,), (1,)))


def _tn(a, b):
    return _dot(a, b, ((0,), (0,)))


def _sigmoid(x):
    return jax.nn.sigmoid(x)


class _Exchange:
    def __init__(self, inputs, out_shapes, n_sems, plan, aliases=None, then=None):
        self.inputs, self.out_shapes, self.n_sems, self.plan, self.aliases = list(inputs), list(out_shapes), n_sems, plan, aliases or {}
        self.then = then

    def sem_shapes(self):
        return [pltpu.SemaphoreType.DMA((self.n_sems,)), pltpu.SemaphoreType.DMA((self.n_sems,))]

    @staticmethod
    def _copy(src, dst, i, to, send_sems, recv_sems):
        return pltpu.make_async_remote_copy(
            src_ref=src, dst_ref=dst, send_sem=send_sems.at[i], recv_sem=recv_sems.at[i], device_id=to, device_id_type=MESH)

    def _start(self, plan, in_refs, out_refs, send_sems, recv_sems):
        for src, dst, i, to in plan(in_refs, out_refs)[0]:
            self._copy(src, dst, i, to, send_sems, recv_sems).start()

    def _wait(self, plan, in_refs, out_refs, send_sems, recv_sems):
        sends, lands = plan(in_refs, out_refs)
        for zone, i in lands:
            self._copy(zone, zone, i, _place(), send_sems, recv_sems).wait_recv()
        for src, dst, i, to in sends:
            self._copy(src, dst, i, to, send_sems, recv_sems).wait_send()

    def start(self, *refs):
        self._start(self.plan, *refs)

    def switch(self, *refs):
        if self.then:
            self._wait(self.plan, *refs)
            self._start(self.then, *refs)

    def finish(self, *refs):
        self._wait(self.then or self.plan, *refs)


def _run_exchange(ex, name):
    n_in, n_out = len(ex.inputs), len(ex.out_shapes)

    def body(*refs):
        in_refs, out_refs, (send_sems, recv_sems) = refs[:n_in], refs[n_in:n_in + n_out], refs[n_in + n_out:]
        ex.start(in_refs, out_refs, send_sems, recv_sems)
        ex.switch(in_refs, out_refs, send_sems, recv_sems)
        ex.finish(in_refs, out_refs, send_sems, recv_sems)

    return pl.pallas_call(
        body, name=name, in_specs=[ANY] * n_in, out_specs=[ANY] * n_out, out_shape=ex.out_shapes,
        scratch_shapes=ex.sem_shapes(), input_output_aliases=dict(ex.aliases),
    )(*ex.inputs)


def _call(body, *, name, grid, in_specs, out_specs, out_shape, args, semantics, scratch_shapes=(), exchange=None):
    if exchange is None:
        return pl.pallas_call(
            body, name=name, grid=grid, in_specs=in_specs, out_specs=out_specs, out_shape=out_shape,
            scratch_shapes=list(scratch_shapes), compiler_params=_params(*semantics))(*args)
    exs = exchange if isinstance(exchange, (list, tuple)) else [exchange]
    n_in, n_out, n_scr = len(in_specs), len(out_specs), len(scratch_shapes)
    x_in, x_out = [len(ex.inputs) for ex in exs], [len(ex.out_shapes) for ex in exs]

    def take(refs, counts):
        groups = []
        for n in counts:
            groups.append(refs[:n])
            refs = refs[n:]
        return groups, refs

    def carrier(*refs):
        ins, refs = refs[:n_in], refs[n_in:]
        x_ins, refs = take(refs, x_in)
        outs, refs = refs[:n_out], refs[n_out:]
        x_outs, refs = take(refs, x_out)
        scr, refs = refs[:n_scr], refs[n_scr:]
        sems, _ = take(refs, [2] * len(exs))
        ids = [pl.program_id(a) for a in range(len(grid))]
        first = functools.reduce(jnp.logical_and, [i == 0 for i in ids])
        last = functools.reduce(jnp.logical_and, [i == g - 1 for i, g in zip(ids, grid)])
        step = functools.reduce(lambda acc, ig: acc * ig[1] + ig[0], zip(ids, grid), 0)

        @pl.when(first)
        def _():
            for ex, xi, xo, (send_sems, recv_sems) in zip(exs, x_ins, x_outs, sems):
                ex.start(xi, xo, send_sems, recv_sems)

        if any(ex.then for ex in exs):
            @pl.when(step == (3 * math.prod(grid)) // 4)
            def _():
                for ex, xi, xo, (send_sems, recv_sems) in zip(exs, x_ins, x_outs, sems):
                    ex.switch(xi, xo, send_sems, recv_sems)

        body(*ins, *outs, *scr)

        @pl.when(last)
        def _():
            for ex, xi, xo, (send_sems, recv_sems) in zip(exs, x_ins, x_outs, sems):
                ex.finish(xi, xo, send_sems, recv_sems)

    aliases, i0, o0 = {}, n_in, n_out
    for ex in exs:
        aliases.update({i0 + i: o0 + o for i, o in ex.aliases.items()})
        i0, o0 = i0 + len(ex.inputs), o0 + len(ex.out_shapes)
    res = pl.pallas_call(
        carrier, name=name, grid=grid, in_specs=list(in_specs) + [ANY] * sum(x_in),
        out_specs=list(out_specs) + [ANY] * sum(x_out),
        out_shape=list(out_shape) + [s for ex in exs for s in ex.out_shapes],
        scratch_shapes=list(scratch_shapes) + [s for ex in exs for s in ex.sem_shapes()],
        input_output_aliases=aliases, compiler_params=_params(*["arbitrary"] * len(grid)),
    )(*args, *[a for ex in exs for a in ex.inputs])
    x_res, _ = take(list(res[n_out:]), x_out)
    return list(res[:n_out]), (x_res if isinstance(exchange, (list, tuple)) else x_res[0])


def _rmsmod_fwd(x, g, shift, scale, name, exchange=None):
    S, D = x.shape
    tr = min(ROW_TILE, S)

    def body(x_ref, g_ref, sh_ref, sc_ref, h_ref):
        xv = x_ref[...]
        rstd = lax.rsqrt(jnp.mean(xv * xv, axis=-1, keepdims=True) + EPS)
        y = xv * rstd * g_ref[...]
        h_ref[...] = (y * (1.0 + sc_ref[...]) + sh_ref[...]).astype(h_ref.dtype)

    row = pl.BlockSpec((tr, D), lambda i: (i, 0))
    vec = pl.BlockSpec((1, D), lambda i: (0, 0))
    return _call(body, name=name, grid=(S // tr,), in_specs=[row, vec, vec, vec], out_specs=[row],
                 out_shape=[jax.ShapeDtypeStruct((S, D), BF16)], args=(x, g, shift, scale), semantics=("parallel",),
                 exchange=exchange)


class _NormBwd:
    def __init__(self, x, g, scale, dx_res, below=None):
        S, D = x.shape
        self.below, self.coef = below, (below[2] if below else None)
        self.inputs = [x, g, scale, dx_res] + ([below[0], below[1]] if below else [])
        vshape = jax.ShapeDtypeStruct((1, D), F32)
        self.out_shape = [jax.ShapeDtypeStruct((S, D), F32), vshape, vshape, vshape]
        if below:
            self.out_shape += [jax.ShapeDtypeStruct((S, D), BF16), vshape]

    def specs(self, tr, D):
        row = pl.BlockSpec((tr, D), lambda i: (i, 0))
        vec = pl.BlockSpec((1, D), lambda i: (0, 0))
        return ([row, vec, vec, row] + ([row, vec] if self.below else []),
                [row, vec, vec, vec] + ([row, vec] if self.below else []))

    def step(self, dhv, in_refs, out_refs):
        if self.below:
            x_ref, g_ref, sc_ref, dxr_ref, f_ref, gate_ref = in_refs
            dx_ref, dsh_ref, dsc_ref, dg_ref, df_ref, dgate_ref = out_refs
            sums = (dsh_ref, dsc_ref, dg_ref, dgate_ref)
        else:
            x_ref, g_ref, sc_ref, dxr_ref = in_refs
            dx_ref, dsh_ref, dsc_ref, dg_ref = out_refs
            sums = (dsh_ref, dsc_ref, dg_ref)

        @pl.when(pl.program_id(0) == 0)
        def _():
            for ref in sums:
                ref[...] = jnp.zeros_like(ref)

        xv, gv = x_ref[...], g_ref[...]
        one_sc = 1.0 + sc_ref[...]
        rstd = lax.rsqrt(jnp.mean(xv * xv, axis=-1, keepdims=True) + EPS)
        n = xv * rstd
        dsh_ref[...] += jnp.sum(dhv, axis=0, keepdims=True)
        dsc_ref[...] += jnp.sum(dhv * n, axis=0, keepdims=True) * gv
        dg_ref[...] += jnp.sum(dhv * n, axis=0, keepdims=True) * one_sc
        dn = dhv * (gv * one_sc)
        dx = dxr_ref[...] + rstd * (dn - n * jnp.mean(dn * n, axis=-1, keepdims=True))
        dx_ref[...] = dx
        if self.below:
            df_ref[...] = (self.coef * gate_ref[...] * dx).astype(df_ref.dtype)
            dgate_ref[...] += self.coef * jnp.sum(dx * f_ref[...].astype(F32), axis=0, keepdims=True)


def _rmsmod_bwd(dh, norm, name, exchange=None):
    S, D = dh.shape
    tr = min(ROW_TILE, S)
    n_in = len(norm.inputs)

    def body(dh_ref, *refs):
        norm.step(dh_ref[...], refs[:n_in], refs[n_in:])

    in_specs, out_specs = norm.specs(tr, D)
    return _call(body, name=name, grid=(S // tr,), in_specs=[pl.BlockSpec((tr, D), lambda i: (i, 0))] + in_specs,
                 out_specs=out_specs, out_shape=norm.out_shape, args=[dh] + norm.inputs, semantics=("arbitrary",),
                 exchange=exchange)


def _ffn_in_fwd(h, w4, name, exchange=None):
    S, D = h.shape
    tm = min(2 * ROW_TILE, S)
    n = w4.shape[2]

    def body(h_ref, wg_ref, wu_ref, zg_ref, zu_ref, a_ref):
        hv = h_ref[...]
        zg = _nn(hv, wg_ref[...])
        zu = _nn(hv, wu_ref[...])
        zg_ref[...] = zg.astype(zg_ref.dtype)
        zu_ref[...] = zu.astype(zu_ref.dtype)
        a_ref[...] = (zg * _sigmoid(zg) * zu).astype(a_ref.dtype)

    out = pl.BlockSpec((tm, n), lambda j, m: (m, j))
    oshape = jax.ShapeDtypeStruct((S, 2 * n), BF16)
    return _call(
        body, name=name, grid=(2, S // tm),
        in_specs=[pl.BlockSpec((tm, D), lambda j, m: (m, 0)),
                  pl.BlockSpec((None, D, n), lambda j, m: (j, 0, 0)),
                  pl.BlockSpec((None, D, n), lambda j, m: (j + 2, 0, 0))],
        out_specs=[out, out, out], out_shape=[oshape, oshape, oshape], args=(h, w4, w4),
        semantics=("parallel", "parallel"), exchange=exchange)


def _proj_out_fwd(lhs, w, x, gate, coef, name, exchange=None, next_norm=None):
    S, D = x.shape
    tm = min(ROW_TILE, S)
    ks = [a.shape[1] for a in lhs]

    def body(*refs):
        lhs_refs, refs = refs[:len(lhs)], refs[len(lhs):]
        if next_norm:
            w_ref, x_ref, gate_ref, g_ref, sh_ref, sc_ref, xn_ref, f_ref, h_ref = refs
        else:
            w_ref, x_ref, gate_ref, xn_ref, f_ref = refs
        acc, off = None, 0
        for a_ref, k in zip(lhs_refs, ks):
            part = _nn(a_ref[...], w_ref[off:off + k, :])
            acc = part if acc is None else acc + part
            off += k
        f_ref[...] = acc.astype(f_ref.dtype)
        xn = x_ref[...] + coef * gate_ref[...] * acc
        xn_ref[...] = xn
        if next_norm:
            rstd = lax.rsqrt(jnp.mean(xn * xn, axis=-1, keepdims=True) + EPS)
            h_ref[...] = (xn * rstd * g_ref[...] * (1.0 + sc_ref[...]) + sh_ref[...]).astype(h_ref.dtype)

    row = pl.BlockSpec((tm, D), lambda m: (m, 0))
    vec = pl.BlockSpec((1, D), lambda m: (0, 0))
    extra = list(next_norm) if next_norm else []
    return _call(
        body, name=name, grid=(S // tm,),
        in_specs=[pl.BlockSpec((tm, k), lambda m: (m, 0)) for k in ks]
        + [_resident(w.shape, lambda m: (0, 0)), row, vec] + [vec] * len(extra),
        out_specs=[row, row] + ([row] if next_norm else []),
        out_shape=[jax.ShapeDtypeStruct((S, D), F32), jax.ShapeDtypeStruct((S, D), BF16)]
        + ([jax.ShapeDtypeStruct((S, D), BF16)] if next_norm else []),
        args=(*lhs, w, x, gate, *extra), semantics=("parallel",), exchange=exchange)


def _proj_out_loss(lhs, w, x, gate, coef, target, name):
    S, D = x.shape
    tm = min(ROW_TILE, S)

    def body(a_ref, w_ref, x_ref, gate_ref, t_ref, dy_ref, df_ref, dgate_ref, sq_ref):
        @pl.when(pl.program_id(0) == 0)
        def _():
            dgate_ref[...] = jnp.zeros_like(dgate_ref)
            sq_ref[...] = jnp.zeros_like(sq_ref)

        f = _nn(a_ref[...], w_ref[...])
        gate = coef * gate_ref[...]
        err = x_ref[...] + gate * f - t_ref[...]
        sq_ref[...] += jnp.sum(err * err, axis=0, keepdims=True)
        dy = err * (1.0 / D)
        dy_ref[...] = dy
        df_ref[...] = (gate * dy).astype(df_ref.dtype)
        dgate_ref[...] += coef * jnp.sum(dy * f, axis=0, keepdims=True)

    row = pl.BlockSpec((tm, D), lambda m: (m, 0))
    vec = pl.BlockSpec((1, D), lambda m: (0, 0))
    vshape = jax.ShapeDtypeStruct((1, D), F32)
    return pl.pallas_call(
        body, name=name, grid=(S // tm,),
        in_specs=[pl.BlockSpec((tm, lhs.shape[1]), lambda m: (m, 0)), _resident(w.shape, lambda m: (0, 0)), row, vec, row],
        out_specs=[row, row, vec, vec],
        out_shape=[jax.ShapeDtypeStruct((S, D), F32), jax.ShapeDtypeStruct((S, D), BF16), vshape, vshape],
        compiler_params=_params("arbitrary"),
    )(lhs, w, x, gate, target)


def _matmul_nn(a, w, out_dtype, tm, name):
    S, K = a.shape
    N = w.shape[1]
    tm = min(tm, S)

    def body(a_ref, w_ref, o_ref):
        o_ref[...] = _nn(a_ref[...], w_ref[...]).astype(o_ref.dtype)

    return pl.pallas_call(
        body, name=name, grid=(S // tm,),
        in_specs=[pl.BlockSpec((tm, K), lambda m: (m, 0)), _resident((K, N), lambda m: (0, 0))],
        out_specs=pl.BlockSpec((tm, N), lambda m: (m, 0)), out_shape=jax.ShapeDtypeStruct((S, N), out_dtype),
        compiler_params=_params("parallel"),
    )(a, w)


def _dact_bwd(df, w_out, zg, zu, name, exchange=None):
    S, D = df.shape
    tm = min(ROW_TILE, S)
    n = w_out.shape[0] // 2

    def body(df_ref, w_ref, zg_ref, zu_ref, dzg_ref, dzu_ref):
        da = _nt(df_ref[...], w_ref[...]).astype(BF16)
        zg_v, zu_v = zg_ref[...], zu_ref[...]
        s = _sigmoid(zg_v)
        dzu_ref[...] = da * zg_v * s
        dzg_ref[...] = da * zu_v * (s * (1.0 + zg_v * (1.0 - s)))

    blk = pl.BlockSpec((tm, n), lambda j, m: (m, j))
    oshape = jax.ShapeDtypeStruct((S, 2 * n), BF16)
    return _call(
        body, name=name, grid=(2, S // tm),
        in_specs=[pl.BlockSpec((tm, D), lambda j, m: (m, 0)), pl.BlockSpec((n, D), lambda j, m: (j, 0)), blk, blk],
        out_specs=[blk, blk], out_shape=[oshape, oshape], args=(df, w_out, zg, zu), semantics=("parallel", "parallel"),
        exchange=exchange)


def _ffn_in_dgrad(dzg, dzu, w4, name, exchange=None, norm=None):
    S = dzg.shape[0]
    D, n = w4.shape[1], w4.shape[2]
    tm = min(ROW_TILE, S)
    n_norm = len(norm.inputs) if norm else 0

    def body(dzg_ref, dzu_ref, w_ref, *refs):
        acc = _nt(dzg_ref[:, 0:n], w_ref[0])
        acc += _nt(dzg_ref[:, n:2 * n], w_ref[1])
        acc += _nt(dzu_ref[:, 0:n], w_ref[2])
        acc += _nt(dzu_ref[:, n:2 * n], w_ref[3])
        if norm:
            norm.step(acc, refs[:n_norm], refs[n_norm:])
        else:
            refs[0][...] = acc

    blk = pl.BlockSpec((tm, 2 * n), lambda m: (m, 0))
    in_specs, args = [blk, blk, _resident(w4.shape, lambda m: (0, 0, 0))], [dzg, dzu, w4]
    out_specs, out_shape = [pl.BlockSpec((tm, D), lambda m: (m, 0))], [jax.ShapeDtypeStruct((S, D), F32)]
    if norm:
        norm_in, out_specs = norm.specs(tm, D)
        in_specs, args, out_shape = in_specs + norm_in, args + norm.inputs, norm.out_shape
    return _call(body, name=name, grid=(S // tm,), in_specs=in_specs, out_specs=out_specs, out_shape=out_shape, args=args,
                 semantics=("arbitrary",) if norm else ("parallel",), exchange=exchange)


def _matmul_nt(pieces, w, tm, name, exchange=None, norm=None):
    S = pieces[0].shape[0]
    ks = [p.shape[1] for p in pieces]
    N = w.shape[0]
    tm = min(tm, S)
    n_norm = len(norm.inputs) if norm else 0

    def body(*refs):
        p_refs, w_ref, refs = refs[:len(ks)], refs[len(ks)], refs[len(ks) + 1:]
        acc, off = None, 0
        for p_ref, k in zip(p_refs, ks):
            part = _nt(p_ref[...], w_ref[:, off:off + k])
            acc = part if acc is None else acc + part
            off += k
        if norm:
            norm.step(acc, refs[:n_norm], refs[n_norm:])
        else:
            refs[0][...] = acc

    in_specs = [pl.BlockSpec((tm, k), lambda m: (m, 0)) for k in ks] + [_resident(w.shape, lambda m: (0, 0))]
    args = list(pieces) + [w]
    out_specs, out_shape = [pl.BlockSpec((tm, N), lambda m: (m, 0))], [jax.ShapeDtypeStruct((S, N), F32)]
    if norm:
        norm_in, out_specs = norm.specs(tm, N)
        in_specs, args, out_shape = in_specs + norm_in, args + norm.inputs, norm.out_shape
    return _call(body, name=name, grid=(S // tm,), in_specs=in_specs, out_specs=out_specs, out_shape=out_shape, args=args,
                 semantics=("arbitrary",) if norm else ("parallel",), exchange=exchange)


def _wgrad(a, gs, tn, name, exchange=None):
    S, Ka = a.shape
    N = gs[0].shape[1]
    ts = min(ROW_TILE * (2 if Ka <= D_MODEL else 1), S)

    def body(a_ref, *refs):
        g_refs, o_ref = refs[:-1], refs[-1]

        @pl.when(pl.program_id(1) == 0)
        def _():
            o_ref[...] = jnp.zeros_like(o_ref)

        a_t = a_ref[...].T
        for i, g_ref in enumerate(g_refs):
            o_ref[i] += _nn(a_t, g_ref[...])

    return _call(
        body, name=name, grid=(N // tn, S // ts),
        in_specs=[pl.BlockSpec((ts, Ka), lambda j, s: (s, 0))] + [pl.BlockSpec((ts, tn), lambda j, s: (s, j))] * len(gs),
        out_specs=[pl.BlockSpec((len(gs), None, Ka, tn), lambda j, s: (0, j, 0, 0))],
        out_shape=[jax.ShapeDtypeStruct((len(gs), N // tn, Ka, tn), F32)], args=(a, *gs),
        semantics=("parallel", "arbitrary"), exchange=exchange)


def _wgrad_pieces(a, pieces, tn, name):
    S, Ka = a.shape
    ts = min(ROW_TILE, S)
    blocks = [(i, j) for i, p in enumerate(pieces) for j in range(p.shape[1] // tn)]

    def body(a_ref, *refs):
        g_refs, o_ref = refs[:-1], refs[-1]

        @pl.when(pl.program_id(0) == 0)
        def _():
            o_ref[...] = jnp.zeros_like(o_ref)

        a_t = a_ref[...].T
        for b, g_ref in enumerate(g_refs):
            o_ref[b] += _nn(a_t, g_ref[...])

    return pl.pallas_call(
        body, name=name, grid=(S // ts,),
        in_specs=[pl.BlockSpec((ts, Ka), lambda s: (s, 0))] + [pl.BlockSpec((ts, tn), lambda s, j=j: (s, j)) for _, j in blocks],
        out_specs=pl.BlockSpec((len(blocks), Ka, tn), lambda s: (0, 0, 0)),
        out_shape=jax.ShapeDtypeStruct((len(blocks), Ka, tn), F32), compiler_params=_params("arbitrary"),
    )(a, *[pieces[i] for i, _ in blocks])


def _wgrad_rows(lhs, g, name):
    S, Ka = lhs[0].shape
    N = g.shape[1]
    ts = min(ROW_TILE, S)

    def body(*refs):
        a_refs, g_ref, o_ref = refs[:-2], refs[-2], refs[-1]

        @pl.when(pl.program_id(0) == 0)
        def _():
            o_ref[...] = jnp.zeros_like(o_ref)

        gv = g_ref[...]
        for i, a_ref in enumerate(a_refs):
            o_ref[i] += _tn(a_ref[...], gv)

    return pl.pallas_call(
        body, name=name, grid=(S // ts,),
        in_specs=[pl.BlockSpec((ts, Ka), lambda s: (s, 0))] * len(lhs) + [pl.BlockSpec((ts, N), lambda s: (s, 0))],
        out_specs=pl.BlockSpec((len(lhs), Ka, N), lambda s: (0, 0, 0)),
        out_shape=jax.ShapeDtypeStruct((len(lhs), Ka, N), F32), compiler_params=_params("arbitrary"),
    )(*lhs, g)


def _hgrn_chunk_common(qr, fr, lb, oml, tri, last):
    sig_nf = _sigmoid(-fr)
    k = oml * sig_nf
    f_small = lb + oml * (jnp.exp(jnp.minimum(fr, 0.0)) * sig_nf)
    use_k = k < 0.5
    f = jnp.where(use_k, 1.0 - k, f_small)
    g = jnp.where(use_k, jnp.log1p(-k), jnp.log(f_small)) * math.log2(math.e)
    q = qr * _sigmoid(qr)
    G = _nn(tri, g, precision=lax.Precision.HIGHEST)
    Gl = G[last:last + 1]
    return q, k, f, G, Gl


def _hgrn_consts(reverse):
    C = HG_CHUNK
    r = lax.broadcasted_iota(jnp.int32, (C, C), 0)
    cc = lax.broadcasted_iota(jnp.int32, (C, C), 1)
    tri = ((cc >= r) if reverse else (cc <= r)).astype(F32)
    tri_t = ((cc <= r) if reverse else (cc >= r)).astype(F32)
    rid = lax.broadcasted_iota(jnp.int32, (C, HG_WIDTH), 0)
    return tri, tri_t, rid, (0 if reverse else C - 1)


def _head_slices():
    return [slice(h * HG_DIM, (h + 1) * HG_DIM) for h in range(HG_HEADS)]


def _per_head_lane_sum(x):
    C = x.shape[0]
    return jnp.concatenate(
        [jnp.broadcast_to(jnp.sum(x[:, sl], axis=-1, keepdims=True), (C, HG_DIM)) for sl in _head_slices()], axis=1)


HG_TILE = 8


def _pair_tiles(s, reverse):
    blk, r = divmod(s, HG_TILE)
    n_tiles = HG_CHUNK // HG_TILE
    others = range(0, blk) if reverse else range(blk + 1, n_tiles)
    return [(blk, r)] + [(t, None) for t in others]


def _pair_decay(G, s, tile, r, rid8, reverse, keys=False):
    rs = slice(tile * HG_TILE, (tile + 1) * HG_TILE)
    d = (G[s:s + 1] - G[rs]) if keys else (G[rs] - G[s:s + 1])
    if r is not None:
        d = jnp.where((rid8 <= r) if reverse else (rid8 >= r), d, NEG)
    return rs, jnp.exp2(d)


def _hgrn_fwd_both(z, lbs, name, exchange=None):
    S = z.shape[0]
    C, DK, W = HG_CHUNK, HG_DIM, HG_WIDTH
    tb = min(HG_ROWS, S)
    n_t, n_c = S // tb, tb // C
    dirs = (0, 1)

    def body(qf_ref, ff_ref, vf_ref, qb_ref, fb_ref, vb_ref, lbf_ref, lbb_ref, of_ref, stf_out, ob_ref, stb_out, st_ref):
        @pl.when(pl.program_id(0) == 0)
        def _():
            st_ref[...] = jnp.zeros_like(st_ref)

        q_refs, f_refs, v_refs, lb_refs = (qf_ref, qb_ref), (ff_ref, fb_ref), (vf_ref, vb_ref), (lbf_ref, lbb_ref)
        o_refs, st_outs = (of_ref, ob_ref), (stf_out, stb_out)
        consts = [_hgrn_consts(d == 1) for d in dirs]
        rid8 = lax.broadcasted_iota(jnp.int32, (HG_TILE, W), 0)

        def chunk(ci, carry):
            cidx = [ci, n_c - 1 - ci]
            rows = [pl.ds(pl.multiple_of(c * C, C), C) for c in cidx]
            v = [v_refs[d][rows[d], :] for d in dirs]
            com = [_hgrn_chunk_common(q_refs[d][rows[d], :], f_refs[d][rows[d], :], lb_refs[d][0:1, :], lb_refs[d][1:2, :],
                                      consts[d][0], consts[d][3]) for d in dirs]
            q, k, G, Gl = [c[0] for c in com], [c[1] for c in com], [c[3] for c in com], [c[4] for c in com]
            qd = [(q[d] * jnp.exp2(G[d])).astype(BF16) for d in dirs]
            kd = [(k[d] * jnp.exp2(Gl[d] - G[d])).astype(BF16) for d in dirs]
            e_gl = [jnp.exp2(Gl[d]) for d in dirs]
            v_b = [v[d].astype(BF16) for d in dirs]
            inter = [[], []]
            for h, sl in enumerate(_head_slices()):
                for d in dirs:
                    st0 = st_ref[d, h]
                    st_outs[d][h, cidx[d]] = st0
                    inter[d].append(_nt(qd[d][:, sl], st0.astype(BF16)))
                    st_ref[d, h] = st0 * e_gl[d][:, sl] + _tn(v_b[d][:, sl], kd[d][:, sl])
            o_t = [[jnp.concatenate(inter[d], axis=1)[t * HG_TILE:(t + 1) * HG_TILE] for t in range(C // HG_TILE)] for d in dirs]
            for s in range(C):
                for d in dirs:
                    k_s, v_s = k[d][s:s + 1], v[d][s:s + 1]
                    for tile, r in _pair_tiles(s, d == 1):
                        rs, e_s = _pair_decay(G[d], s, tile, r, rid8, d == 1)
                        o_t[d][tile] = o_t[d][tile] + _per_head_lane_sum(q[d][rs] * k_s * e_s) * v_s
            for d in dirs:
                o_refs[d][rows[d], :] = jnp.concatenate(o_t[d], axis=0)
            return carry

        lax.fori_loop(0, n_c, chunk, 0, unroll=8)

    def sec(j, back):
        return pl.BlockSpec((tb, W), (lambda i: (n_t - 1 - i, j)) if back else (lambda i: (i, j)))

    def st_spec(back):
        return pl.BlockSpec((HG_HEADS, n_c, DK, DK), (lambda i: (0, n_t - 1 - i, 0, 0)) if back else (lambda i: (0, i, 0, 0)))

    vec = pl.BlockSpec((2, W), lambda i: (0, 0))
    o_shape = jax.ShapeDtypeStruct((S, W), F32)
    st_shape = jax.ShapeDtypeStruct((HG_HEADS, S // C, DK, DK), F32)
    return _call(
        body, name=name, grid=(n_t,),
        in_specs=[sec(0, False), sec(1, False), sec(3, False), sec(0, True), sec(2, True), sec(3, True), vec, vec],
        out_specs=[sec(0, False), st_spec(False), sec(0, True), st_spec(True)],
        out_shape=[o_shape, st_shape, o_shape, st_shape],
        scratch_shapes=[pltpu.VMEM((2, HG_HEADS, DK, DK), F32)], args=(z, z, z, z, z, z, lbs[0], lbs[1]),
        semantics=("arbitrary",), exchange=exchange)


def _hgrn_bwd(z, lb, do, states, direction, name, acc=None, exchange=None):
    S = z.shape[0]
    C, DK, W = HG_CHUNK, HG_DIM, HG_WIDTH
    tb = min(HG_ROWS, S)
    n_t, n_c = S // tb, tb // C
    reverse = direction == 1
    tmap = (lambda i: i) if reverse else (lambda i: n_t - 1 - i)

    def body(*refs):
        if acc:
            q_ref, f_ref, v_ref, lb_ref, do_ref, st_in_ref, dqa_ref, dva_ref, dq_ref, df_ref, dv_ref, doml_ref, dst_ref = refs
        else:
            q_ref, f_ref, v_ref, lb_ref, do_ref, st_in_ref, dq_ref, df_ref, dv_ref, doml_ref, dst_ref = refs

        @pl.when(pl.program_id(0) == 0)
        def _():
            dst_ref[...] = jnp.zeros_like(dst_ref)
            doml_ref[...] = jnp.zeros_like(doml_ref)

        lbv, oml = lb_ref[0:1, :], lb_ref[1:2, :]
        tri, tri_t, rid, last = _hgrn_consts(reverse)
        rid8 = lax.broadcasted_iota(jnp.int32, (HG_TILE, W), 0)

        def chunk(ci, carry):
            cidx = ci if reverse else (n_c - 1 - ci)
            rows = pl.ds(pl.multiple_of(cidx * C, C), C)
            qr, fr, v, dov = q_ref[rows, :], f_ref[rows, :], v_ref[rows, :], do_ref[rows, :]
            q, k, f, G, Gl = _hgrn_chunk_common(qr, fr, lbv, oml, tri, last)
            e_g, e_gl, e_kd = jnp.exp2(G), jnp.exp2(Gl), jnp.exp2(Gl - G)
            qd, kd = q * e_g, k * e_kd
            do_b, v_b, qd_b, kd_b = dov.astype(BF16), v.astype(BF16), qd.astype(BF16), kd.astype(BF16)
            dqd, dkd, dv, state_dot = [], [], [], []
            for h, sl in enumerate(_head_slices()):
                st0, dst1 = st_in_ref[h, cidx], dst_ref[h]
                dst1_b = dst1.astype(BF16)
                dqd.append(_nn(do_b[:, sl], st0.astype(BF16)))
                dkd.append(_nn(v_b[:, sl], dst1_b))
                dv.append(_nt(kd_b[:, sl], dst1_b))
                state_dot.append(jnp.sum(st0 * dst1, axis=0, keepdims=True))
                dst_ref[h] = dst1 * e_gl[:, sl] + _tn(do_b[:, sl], qd_b[:, sl])
            dqd, dkd, dv = [jnp.concatenate(t, axis=1) for t in (dqd, dkd, dv)]
            d_gl = e_gl * jnp.concatenate(state_dot, axis=1) + jnp.sum(dkd * kd, axis=0, keepdims=True)
            dq, dk = dqd * e_g, dkd * e_kd
            n_tiles = C // HG_TILE
            dq_t, dk_t, dv_t = [[x[t * HG_TILE:(t + 1) * HG_TILE] for t in range(n_tiles)] for x in (dq, dk, dv)]
            for s in range(C):
                k_s, v_s = k[s:s + 1], v[s:s + 1]
                for tile, r in _pair_tiles(s, reverse):
                    rs, e_s = _pair_decay(G, s, tile, r, rid8, reverse)
                    dq_t[tile] = dq_t[tile] + _per_head_lane_sum(dov[rs] * v_s) * e_s * k_s
            for t in range(C):
                q_t, do_t = q[t:t + 1], dov[t:t + 1]
                for tile, r in _pair_tiles(t, not reverse):
                    rs, x_t = _pair_decay(G, t, tile, r, rid8, not reverse, keys=True)
                    qx = q_t * x_t
                    dv_t[tile] = dv_t[tile] + _per_head_lane_sum(k[rs] * qx) * do_t
                    dk_t[tile] = dk_t[tile] + _per_head_lane_sum(v[rs] * do_t) * qx
            dq, dk, dv = [jnp.concatenate(x, axis=0) for x in (dq_t, dk_t, dv_t)]
            d_big_g = dq * q - dk * k + jnp.where(rid == last, d_gl, 0.0)
            dg = _nn(tri_t, d_big_g, precision=lax.Precision.HIGHEST)
            dk_all = dk - dg / f
            sig_nf = _sigmoid(-fr)
            df_ref[rows, :] = (-dk_all * k * (1.0 - sig_nf)).astype(df_ref.dtype)
            doml_ref[...] += jnp.sum(dk_all * sig_nf, axis=0, keepdims=True)
            sq = _sigmoid(qr)
            dqr = dq * (sq * (1.0 + qr * (1.0 - sq)))
            if acc:
                dqr = dqr + dqa_ref[rows, :]
                dv = dv + dva_ref[rows, :]
            dq_ref[rows, :] = dqr.astype(dq_ref.dtype)
            dv_ref[rows, :] = dv.astype(dv_ref.dtype)
            return carry

        lax.fori_loop(0, n_c, chunk, 0, unroll=16)

    def sec(j):
        return pl.BlockSpec((tb, W), lambda i: (tmap(i), j))

    vec = pl.BlockSpec((1, W), lambda i: (0, 0))
    ins = [z, z, z, lb, do, states]
    in_specs = [sec(0), sec(1 + direction), sec(3), pl.BlockSpec((2, W), lambda i: (0, 0)), sec(0),
                pl.BlockSpec((HG_HEADS, n_c, DK, DK), lambda i: (0, tmap(i), 0, 0))]
    if acc:
        ins += list(acc)
        in_specs += [sec(0), sec(0)]
    final = jax.ShapeDtypeStruct((S, W), BF16)
    partial = final if acc else jax.ShapeDtypeStruct((S, W), F32)
    return _call(
        body, name=name, grid=(n_t,), in_specs=in_specs,
        out_specs=[sec(0), sec(0), sec(0), vec],
        out_shape=[partial, final, partial, jax.ShapeDtypeStruct((1, W), F32)],
        scratch_shapes=[pltpu.VMEM((HG_HEADS, DK, DK), F32)], args=ins, semantics=("arbitrary",), exchange=exchange)


def _hgrn_post_fwd(o_f, o_b, z, norm_g, name):
    S = z.shape[0]
    tr = min(ROW_TILE, S)

    def body(of_ref, ob_ref, gr_ref, ng_ref, y_ref):
        o = of_ref[...] + ob_ref[...]
        gr = gr_ref[...]
        gate = gr * _sigmoid(gr)
        ng = ng_ref[...]
        for h in range(HG_HEADS):
            sl = slice(h * HG_DIM, (h + 1) * HG_DIM)
            oh = o[:, sl]
            rstd = lax.rsqrt(jnp.mean(oh * oh, axis=-1, keepdims=True) + EPS)
            y_ref[:, sl] = (oh * rstd * ng[:, sl] * gate[:, sl]).astype(y_ref.dtype)

    row = pl.BlockSpec((tr, HG_WIDTH), lambda i: (i, 0))
    return pl.pallas_call(
        body, name=name, grid=(S // tr,),
        in_specs=[row, row, pl.BlockSpec((tr, HG_WIDTH), lambda i: (i, 4)), pl.BlockSpec((1, HG_WIDTH), lambda i: (0, 0))],
        out_specs=row, out_shape=jax.ShapeDtypeStruct((S, HG_WIDTH), BF16), compiler_params=_params("parallel"),
    )(o_f, o_b, z, norm_g)


def _hgrn_post_bwd(dy, o_f, o_b, z, norm_g, name):
    S = z.shape[0]
    tr = min(ROW_TILE, S)

    def body(dy_ref, of_ref, ob_ref, gr_ref, ng_ref, do_ref, dgr_ref, dng_ref):
        @pl.when(pl.program_id(0) == 0)
        def _():
            dng_ref[...] = jnp.zeros_like(dng_ref)

        o = of_ref[...] + ob_ref[...]
        gr, ng, dyv = gr_ref[...], ng_ref[...], dy_ref[...]
        sg = _sigmoid(gr)
        for h in range(HG_HEADS):
            sl = slice(h * HG_DIM, (h + 1) * HG_DIM)
            oh, dyh, grh, sgh, ngh = o[:, sl], dyv[:, sl], gr[:, sl], sg[:, sl], ng[:, sl]
            rstd = lax.rsqrt(jnp.mean(oh * oh, axis=-1, keepdims=True) + EPS)
            on = oh * rstd
            du = dyh * (grh * sgh)
            dgr_ref[:, sl] = (dyh * (on * ngh) * (sgh * (1.0 + grh * (1.0 - sgh)))).astype(dgr_ref.dtype)
            dng_ref[:, sl] += jnp.sum(du * on, axis=0, keepdims=True)
            don = du * ngh
            do_ref[:, sl] = rstd * (don - on * jnp.mean(don * on, axis=-1, keepdims=True))

    row = pl.BlockSpec((tr, HG_WIDTH), lambda i: (i, 0))
    vec = pl.BlockSpec((1, HG_WIDTH), lambda i: (0, 0))
    full = jax.ShapeDtypeStruct((S, HG_WIDTH), F32)
    return pl.pallas_call(
        body, name=name, grid=(S // tr,),
        in_specs=[row, row, row, pl.BlockSpec((tr, HG_WIDTH), lambda i: (i, 4)), vec],
        out_specs=[row, row, vec],
        out_shape=[full, jax.ShapeDtypeStruct((S, HG_WIDTH), BF16), jax.ShapeDtypeStruct((1, HG_WIDTH), F32)],
        compiler_params=_params("arbitrary"),
    )(dy, o_f, o_b, z, norm_g)


def _t5_bucket_table():
    rel = (np.arange(3 * BLOCK)[None, :] - BLOCK) - np.arange(BLOCK)[:, None]
    nb = NUM_BUCKETS // 2
    max_exact = nb // 2
    ret = (rel > 0).astype(np.int32) * nb
    n = np.abs(rel)
    ratio = np.log(np.maximum(n, 1).astype(np.float32) / np.float32(max_exact)) / np.float32(math.log(MAX_DISTANCE / max_exact))
    large = max_exact + (ratio.astype(np.float32) * np.float32(nb - max_exact)).astype(np.int32)
    large = np.minimum(large, nb - 1)
    bucket = ret + np.where(n < max_exact, n, large)
    return bucket.astype(np.int32), (n <= WINDOW)


def _bias_table(rel_bias, name):
    bucket, in_band = _t5_bucket_table()
    idx = jnp.asarray(np.where(in_band, bucket, -1))

    def body(rb_ref, idx_ref, o_ref):
        h = pl.program_id(0)
        iv = idx_ref[...]
        acc = jnp.where(iv < 0, NEG, 0.0).astype(F32)
        for b in range(NUM_BUCKETS):
            acc = acc + jnp.where(iv == b, rb_ref[b, h], 0.0)
        o_ref[...] = acc

    return pl.pallas_call(
        body, name=name, grid=(ATT_Q_HEADS,),
        in_specs=[pl.BlockSpec(memory_space=pltpu.SMEM), pl.BlockSpec((BLOCK, 3 * BLOCK), lambda h: (0, 0))],
        out_specs=pl.BlockSpec((None, BLOCK, 3 * BLOCK), lambda h: (h, 0, 0)),
        out_shape=jax.ShapeDtypeStruct((ATT_Q_HEADS, BLOCK, 3 * BLOCK), F32), compiler_params=_params("parallel"),
    )(rel_bias, idx)


def _bias_grad(ds_sum_t, name):
    bucket, in_band = _t5_bucket_table()
    idx_t = jnp.asarray(np.where(in_band, bucket, -1).T)

    def body(ds_ref, idx_ref, o_ref):
        iv, ds = idx_ref[...], ds_ref[...]
        for b in range(NUM_BUCKETS):
            o_ref[b:b + 1, :] = jnp.sum(jnp.where(iv == b, ds, 0.0), axis=0, keepdims=True)

    return pl.pallas_call(
        body, name=name, grid=(ATT_Q_HEADS,),
        in_specs=[pl.BlockSpec((None, 3 * BLOCK, BLOCK), lambda h: (h // ATT_GROUP, 0, h % ATT_GROUP)),
                  pl.BlockSpec((3 * BLOCK, BLOCK), lambda h: (0, 0))],
        out_specs=pl.BlockSpec((None, NUM_BUCKETS, BLOCK), lambda h: (h, 0, 0)),
        out_shape=jax.ShapeDtypeStruct((ATT_Q_HEADS, NUM_BUCKETS, BLOCK), F32), compiler_params=_params("parallel"),
    )(ds_sum_t, idx_t)


Q_COL = 5 * HG_WIDTH
KV_COL = Q_COL + ATT_WIDTH
GROUP_WIDTH = ATT_GROUP * ATT_HEAD_DIM


def _stack_heads(blk):
    dh = ATT_HEAD_DIM
    return jnp.concatenate([blk[:, g * dh:(g + 1) * dh] for g in range(ATT_GROUP)], axis=0)


def _unstack_heads(st):
    return jnp.concatenate([st[g * BLOCK:(g + 1) * BLOCK] for g in range(ATT_GROUP)], axis=1)


def _rms_rows(x):
    rstd = lax.rsqrt(jnp.mean(x * x, axis=-1, keepdims=True) + EPS)
    return x * rstd, rstd


def _edge_ok(n, nb):
    colid = lax.broadcasted_iota(jnp.int32, (ATT_GROUP * BLOCK, 3 * BLOCK), 1)
    return jnp.logical_and(jnp.logical_or(colid >= BLOCK, n > 0), jnp.logical_or(colid < 2 * BLOCK, n < nb - 1))


def _sink_column(sink_ref, j=0):
    heads = range(j * ATT_GROUP, (j + 1) * ATT_GROUP)
    return jnp.concatenate([jnp.broadcast_to(sink_ref[h][:, 0:1], (BLOCK, 1)) for h in heads], axis=0)


def _attn_fwd(z, q_g, k_g, sink, bias, name):
    S = z.shape[0]
    nb = S // BLOCK
    G, dh, KV = ATT_GROUP, ATT_HEAD_DIM, ATT_KV_HEADS
    scale = 1.0 / math.sqrt(dh)

    def body(q_ref, kv0, kv1, kv2, qg_ref, kg_ref, sink_ref, bias_ref, o_ref):
        n = pl.program_id(0)
        edge_ok = _edge_ok(n, nb)
        cat = jnp.concatenate([kv0[...], kv1[...], kv2[...]], axis=0)
        qblk = q_ref[...]
        kn = [(_rms_rows(cat[:, j * dh:(j + 1) * dh])[0] * kg_ref[...]).astype(BF16) for j in range(KV)]
        vb = [cat[:, (KV + j) * dh:(KV + j + 1) * dh].astype(BF16) for j in range(KV)]
        qn = [(_rms_rows(_stack_heads(qblk[:, j * GROUP_WIDTH:(j + 1) * GROUP_WIDTH]))[0] * (qg_ref[...] * scale)).astype(BF16)
              for j in range(KV)]
        s = [_nt(qn[j], kn[j]) + bias_ref[j * G:(j + 1) * G].reshape(G * BLOCK, 3 * BLOCK) for j in range(KV)]
        s = [jnp.where(edge_ok, sj, NEG) for sj in s]
        sinks = [_sink_column(sink_ref, j) for j in range(KV)]
        m = [jnp.maximum(jnp.max(s[j], axis=-1, keepdims=True), sinks[j]) for j in range(KV)]
        e = [jnp.exp(s[j] - m[j]) for j in range(KV)]
        den = [jnp.sum(e[j], axis=-1, keepdims=True) + jnp.exp(sinks[j] - m[j]) for j in range(KV)]
        o = [_nn(e[j].astype(BF16), vb[j]) * (1.0 / den[j]) for j in range(KV)]
        o_ref[...] = jnp.concatenate([_unstack_heads(oj) for oj in o], axis=1).astype(o_ref.dtype)

    def kv(shift):
        return pl.BlockSpec((BLOCK, 2 * KV_WIDTH), lambda n: (jnp.clip(n + shift, 0, nb - 1), KV_COL // (2 * KV_WIDTH)))

    gain = pl.BlockSpec((1, dh), lambda n: (0, 0))
    return pl.pallas_call(
        body, name=name, grid=(nb,),
        in_specs=[pl.BlockSpec((BLOCK, ATT_WIDTH), lambda n: (n, Q_COL // ATT_WIDTH)), kv(-1), kv(0), kv(1), gain, gain,
                  pl.BlockSpec((ATT_Q_HEADS, 1, BLOCK), lambda n: (0, 0, 0)),
                  pl.BlockSpec((ATT_Q_HEADS, BLOCK, 3 * BLOCK), lambda n: (0, 0, 0))],
        out_specs=pl.BlockSpec((BLOCK, ATT_WIDTH), lambda n: (n, 0)),
        out_shape=jax.ShapeDtypeStruct((S, ATT_WIDTH), BF16), compiler_params=_params("parallel"),
    )(z, z, z, z, q_g, k_g, sink, bias)


def _attn_bwd(z, q_g, k_g, sink, bias, do, name):
    S = z.shape[0]
    nb = S // BLOCK
    G, dh, KV = ATT_GROUP, ATT_HEAD_DIM, ATT_KV_HEADS
    scale = 1.0 / math.sqrt(dh)
    both = range(KV)
    bias_t = bias.reshape(KV, G, BLOCK, 3 * BLOCK).transpose(0, 3, 1, 2).reshape(KV, 3 * BLOCK, G * BLOCK)

    def body(q_ref, kv0, kv1, kv2, qg_ref, kg_ref, sink_ref, bias_ref, do_ref,
             dq_ref, dkw_ref, dvw_ref, ds_ref, dsink_ref, dqg_ref):
        n = pl.program_id(0)

        @pl.when(n == 0)
        def _():
            ds_ref[...] = jnp.zeros_like(ds_ref)
            dsink_ref[...] = jnp.zeros_like(dsink_ref)
            dqg_ref[...] = jnp.zeros_like(dqg_ref)

        rowid = lax.broadcasted_iota(jnp.int32, (3 * BLOCK, G * BLOCK), 0)
        edge_ok = jnp.logical_and(jnp.logical_or(rowid >= BLOCK, n > 0), jnp.logical_or(rowid < 2 * BLOCK, n < nb - 1))
        qg = qg_ref[...]
        cat = jnp.concatenate([kv0[...], kv1[...], kv2[...]], axis=0)
        qblk, doblk = q_ref[...], do_ref[...]
        kn = [(_rms_rows(cat[:, j * dh:(j + 1) * dh])[0] * kg_ref[...]).astype(BF16) for j in both]
        vb = [cat[:, (KV + j) * dh:(KV + j + 1) * dh].astype(BF16) for j in both]
        norm = [_rms_rows(_stack_heads(qblk[:, j * GROUP_WIDTH:(j + 1) * GROUP_WIDTH])) for j in both]
        qn = [(norm[j][0] * (qg * scale)).astype(BF16) for j in both]
        do_b = [_stack_heads(doblk[:, j * GROUP_WIDTH:(j + 1) * GROUP_WIDTH]).astype(BF16) for j in both]
        s = [_nt(kn[j], qn[j]) + bias_ref[j] for j in both]
        dp = [_nt(vb[j], do_b[j]) for j in both]
        s = [jnp.where(edge_ok, sj, NEG) for sj in s]
        sinks = [jnp.concatenate([sink_ref[j * G + g] for g in range(G)], axis=1) for j in both]
        m = [jnp.maximum(jnp.max(s[j], axis=0, keepdims=True), sinks[j]) for j in both]
        e = [jnp.exp(s[j] - m[j]) for j in both]
        e_sink = [jnp.exp(sinks[j] - m[j]) for j in both]
        inv = [1.0 / (jnp.sum(e[j], axis=0, keepdims=True) + e_sink[j]) for j in both]
        p = [e[j] * inv[j] for j in both]
        delta = [jnp.sum(p[j] * dp[j], axis=0, keepdims=True) for j in both]
        ds = [p[j] * (dp[j] - delta[j]) for j in both]
        ds_b = [dsj.astype(BF16) for dsj in ds]
        dqn = [_tn(kn[j], ds_b[j]).T * scale for j in both]
        for j in both:
            dvw_ref[j] = _nn(p[j].astype(BF16), do_b[j])
            dkw_ref[j] = _nn(ds_b[j], qn[j])
        for j in both:
            ds_ref[j] += ds[j]
            sink_term = e_sink[j] * inv[j] * delta[j]
            for g in range(G):
                dsink_ref[j * G + g] += (jnp.zeros((1, BLOCK), F32)
                                         - jnp.sum(sink_term[:, g * BLOCK:(g + 1) * BLOCK], axis=1, keepdims=True))
        dq = []
        for j in both:
            qhat, rstd = norm[j]
            dqg_ref[j] += jnp.sum(dqn[j] * qhat, axis=0, keepdims=True)
            dqh = dqn[j] * qg
            dq.append(_unstack_heads(rstd * (dqh - qhat * jnp.mean(dqh * qhat, axis=-1, keepdims=True))))
        dq_ref[...] = jnp.concatenate(dq, axis=1).astype(dq_ref.dtype)

    def kv(shift):
        return pl.BlockSpec((BLOCK, 2 * KV_WIDTH), lambda n: (jnp.clip(n + shift, 0, nb - 1), KV_COL // (2 * KV_WIDTH)))

    gain = pl.BlockSpec((1, dh), lambda n: (0, 0))
    sink_spec = pl.BlockSpec((ATT_Q_HEADS, 1, BLOCK), lambda n: (0, 0, 0))
    bias_spec = pl.BlockSpec((KV, 3 * BLOCK, G * BLOCK), lambda n: (0, 0, 0))
    win = pl.BlockSpec((KV, None, 3 * BLOCK, dh), lambda n: (0, n, 0, 0))
    wshape = jax.ShapeDtypeStruct((KV, nb, 3 * BLOCK, dh), F32)
    return pl.pallas_call(
        body, name=name, grid=(nb,),
        in_specs=[pl.BlockSpec((BLOCK, ATT_WIDTH), lambda n: (n, Q_COL // ATT_WIDTH)), kv(-1), kv(0), kv(1), gain, gain,
                  sink_spec, bias_spec, pl.BlockSpec((BLOCK, ATT_WIDTH), lambda n: (n, HG_WIDTH // ATT_WIDTH))],
        out_specs=[pl.BlockSpec((BLOCK, ATT_WIDTH), lambda n: (n, 0)), win, win, bias_spec, sink_spec,
                   pl.BlockSpec((KV, 1, dh), lambda n: (0, 0, 0))],
        out_shape=[jax.ShapeDtypeStruct((S, ATT_WIDTH), BF16), wshape, wshape,
                   jax.ShapeDtypeStruct((KV, 3 * BLOCK, G * BLOCK), F32),
                   jax.ShapeDtypeStruct((ATT_Q_HEADS, 1, BLOCK), F32),
                   jax.ShapeDtypeStruct((KV, 1, dh), F32)],
        compiler_params=_params("arbitrary"),
    )(z, z, z, z, q_g, k_g, sink, bias_t, do)


def _attn_kv_reduce(dkw, dvw, z, k_g, name):
    S = z.shape[0]
    nb = S // BLOCK
    dh = ATT_HEAD_DIM
    kb = min(8, nb)
    steps = nb // kb

    def body(a_lo, a, a_hi, b_lo, b, b_hi, kv_ref, kg_ref, dkv_ref, dkg_ref):
        n = pl.program_id(0)

        @pl.when(n == 0)
        def _():
            dkg_ref[...] = jnp.zeros_like(dkg_ref)

        lo = jnp.where(n > 0, 1.0, 0.0)
        hi = jnp.where(n < steps - 1, 1.0, 0.0)

        def overlap_add(w, w_lo, w_hi, j, i):
            before = lo * w_lo[j] if i == 0 else w[j, i - 1, 2 * BLOCK:3 * BLOCK, :]
            after = hi * w_hi[j] if i == kb - 1 else w[j, i + 1, 0:BLOCK, :]
            return w[j, i, BLOCK:2 * BLOCK, :] + before + after

        dkg = [jnp.zeros((1, dh), F32) for _ in range(ATT_KV_HEADS)]
        for i in range(kb):
            rows = slice(i * BLOCK, (i + 1) * BLOCK)
            dks, dvs = [], []
            for j in range(ATT_KV_HEADS):
                dkn = overlap_add(a, a_lo, a_hi, j, i)
                dvs.append(overlap_add(b, b_lo, b_hi, j, i))
                khat, rstd = _rms_rows(kv_ref[rows, j * dh:(j + 1) * dh])
                dkg[j] = dkg[j] + jnp.sum(dkn * khat, axis=0, keepdims=True)
                dkh = dkn * kg_ref[...]
                dks.append(rstd * (dkh - khat * jnp.mean(dkh * khat, axis=-1, keepdims=True)))
            dkv_ref[rows, :] = jnp.concatenate(dks + dvs, axis=1).astype(dkv_ref.dtype)
        for j in range(ATT_KV_HEADS):
            dkg_ref[j] += dkg[j]

    main = pl.BlockSpec((ATT_KV_HEADS, kb, 3 * BLOCK, dh), lambda n: (0, n, 0, 0))
    halo_lo = pl.BlockSpec((ATT_KV_HEADS, None, BLOCK, dh), lambda n: (0, jnp.maximum(n * kb - 1, 0), 2, 0))
    halo_hi = pl.BlockSpec((ATT_KV_HEADS, None, BLOCK, dh), lambda n: (0, jnp.minimum(n * kb + kb, nb - 1), 0, 0))
    return pl.pallas_call(
        body, name=name, grid=(steps,),
        in_specs=[halo_lo, main, halo_hi, halo_lo, main, halo_hi,
                  pl.BlockSpec((kb * BLOCK, 2 * KV_WIDTH), lambda n: (n, KV_COL // (2 * KV_WIDTH))),
                  pl.BlockSpec((1, dh), lambda n: (0, 0))],
        out_specs=[pl.BlockSpec((kb * BLOCK, 2 * KV_WIDTH), lambda n: (n, 0)),
                   pl.BlockSpec((ATT_KV_HEADS, 1, dh), lambda n: (0, 0, 0))],
        out_shape=[jax.ShapeDtypeStruct((S, 2 * KV_WIDTH), BF16), jax.ShapeDtypeStruct((ATT_KV_HEADS, 1, dh), F32)],
        compiler_params=_params("arbitrary"),
    )(dkw, dkw, dkw, dvw, dvw, dvw, z, k_g)


def _ada_wgrad(c_act_t, dm, name):
    D, nbatch = c_act_t.shape
    n = dm.shape[1]
    tr = 256

    def body(c_ref, dm_ref, o_ref):
        cv, dv = c_ref[...], dm_ref[...]
        acc = cv[:, 0:1] * dv[0:1, :]
        for b in range(1, nbatch):
            acc = acc + cv[:, b:b + 1] * dv[b:b + 1, :]
        o_ref[...] = acc

    return pl.pallas_call(
        body, name=name, grid=(D // tr,),
        in_specs=[pl.BlockSpec((tr, nbatch), lambda i: (i, 0)), pl.BlockSpec((nbatch, n), lambda i: (0, 0))],
        out_specs=pl.BlockSpec((tr, n), lambda i: (i, 0)), out_shape=jax.ShapeDtypeStruct((D, n), F32),
        compiler_params=_params("parallel"),
    )(c_act_t, dm)


def _to_bf16(w, name):
    R, Cn = w.shape
    tr = _row_tile(R)

    def body(w_ref, o_ref):
        o_ref[...] = w_ref[...].astype(BF16)

    blk = pl.BlockSpec((tr, Cn), lambda i: (i, 0))
    return pl.pallas_call(
        body, name=name, grid=(R // tr,), in_specs=[blk], out_specs=blk, out_shape=jax.ShapeDtypeStruct((R, Cn), BF16),
        compiler_params=_params("parallel"),
    )(w)


def _adamw(w, g, m, v, name):
    R, Cn = w.shape
    tr = R
    for cand in (256, 128, 64, 32, 16, 8):
        if R % cand == 0:
            tr = cand
            break

    def body(w_ref, g_ref, m_ref, v_ref, d_ref, nm_ref, nv_ref):
        gv = g_ref[...]
        m_new = ADAM_B1 * m_ref[...] + (1.0 - ADAM_B1) * gv
        v_new = ADAM_B2 * v_ref[...] + (1.0 - ADAM_B2) * (gv * gv)
        m_hat = m_new / (1.0 - ADAM_B1 ** ADAM_STEP)
        v_hat = v_new / (1.0 - ADAM_B2 ** ADAM_STEP)
        d_ref[...] = -ADAM_LR * (m_hat / (jnp.sqrt(v_hat) + ADAM_EPS) + ADAM_WD * w_ref[...])
        nm_ref[...] = m_new
        nv_ref[...] = v_new

    blk = pl.BlockSpec((tr, Cn), lambda i: (i, 0))
    shp = jax.ShapeDtypeStruct((R, Cn), F32)
    return pl.pallas_call(
        body, name=name, grid=(R // tr,), in_specs=[blk] * 4, out_specs=[blk] * 3, out_shape=[shp] * 3,
        compiler_params=_params("parallel"),
    )(w, g, m, v)


def _place():
    return lax.axis_index("x"), lax.axis_index("y"), lax.axis_index("c")


def _flip(place, k):
    x, y, c = place
    return (1 - x if k & 4 else x, 1 - y if k & 2 else y, 1 - c if k & 1 else c)


def _dev_index(place):
    x, y, c = place
    return 4 * x + 2 * y + c


def _chip_index(place):
    return 2 * place[0] + place[1]


def _gather8(x_ref, out_ref, send_sems, recv_sems, local_sem):
    me = _place()
    mine = pltpu.make_async_copy(x_ref, out_ref.at[_dev_index(me)], local_sem)
    mine.start()

    def copy(k, origin, to):
        return pltpu.make_async_remote_copy(
            src_ref=x_ref, dst_ref=out_ref.at[_dev_index(origin)], send_sem=send_sems.at[k - 1],
            recv_sem=recv_sems.at[k - 1], device_id=to, device_id_type=MESH)

    sends = [copy(k, me, _flip(me, k)) for k in range(1, 8)]
    for cp in sends:
        cp.start()
    for k in range(1, 8):
        copy(k, _flip(me, k), me).wait_recv()
    for cp in sends:
        cp.wait_send()
    mine.wait()


def _allgather8(x, name, reduce=False):
    R, Cn = x.shape

    def body(x_ref, *rest):
        if reduce:
            out_ref, sum_ref, send_sems, recv_sems, local_sem = rest
        else:
            out_ref, send_sems, recv_sems, local_sem = rest
        _gather8(x_ref, out_ref, send_sems, recv_sems, local_sem)
        if reduce:
            acc = out_ref[0]
            for i in range(1, 8):
                acc = acc + out_ref[i]
            sum_ref[...] = acc

    vm = pl.BlockSpec(memory_space=pltpu.VMEM)
    outs = [jax.ShapeDtypeStruct((8, R, Cn), F32)] + ([jax.ShapeDtypeStruct((R, Cn), F32)] if reduce else [])
    res = pl.pallas_call(
        body, name=name, in_specs=[vm], out_specs=[vm] * len(outs), out_shape=outs,
        scratch_shapes=[pltpu.SemaphoreType.DMA((7,)), pltpu.SemaphoreType.DMA((7,)), pltpu.SemaphoreType.DMA],
    )(x)
    return res if reduce else res[0]


def _prologue(small, w_ada, b_ada, w_shard, to_cast, name):
    R, Cn = small.shape
    n_mod = w_ada.shape[1]
    n_w = len(to_cast)
    big = _gather_over_ici([w_shard])

    def body(*refs):
        (small_ref, wada_ref, b_ref, shard_ref), refs = refs[:4], refs[4:]
        wide_refs, refs = refs[:n_w], refs[n_w:]
        (small_all_ref, mods_all_ref, gathered_ref), refs = refs[:3], refs[3:]
        narrow_refs, refs = refs[:n_w], refs[n_w:]
        (mods_ref, send1, recv1, send2, recv2, local_sems), refs = refs[:6], refs[6:]
        wide_bufs, narrow_bufs, (load_sems, store_sems, big_send, big_recv) = refs[:n_w], refs[n_w:2 * n_w], refs[2 * n_w:]
        big.start([shard_ref], [gathered_ref], big_send, big_recv)
        loads = [pltpu.make_async_copy(w, buf, load_sems.at[i]) for i, (w, buf) in enumerate(zip(wide_refs, wide_bufs))]
        for cp in loads:
            cp.start()
        stores = []
        for i, cp in enumerate(loads):
            cp.wait()
            rows = wide_bufs[i].shape[0]
            tr = _row_tile(rows)

            def cast(j, carry, i=i, tr=tr):
                rs = pl.ds(pl.multiple_of(j * tr, tr), tr)
                narrow_bufs[i][rs, :] = wide_bufs[i][rs, :].astype(BF16)
                return carry

            lax.fori_loop(0, rows // tr, cast, 0)
            stores.append(pltpu.make_async_copy(narrow_bufs[i], narrow_refs[i], store_sems.at[i]))
            stores[-1].start()
        _gather8(small_ref, small_all_ref, send1, recv1, local_sems.at[0])
        c_all = jnp.concatenate([small_all_ref[d, 0:1, :] for d in range(8)], axis=0)
        c_act = c_all * _sigmoid(c_all)
        mods_ref[...] = _nn(c_act, wada_ref[...], precision=lax.Precision.HIGHEST) + b_ref[...]
        _gather8(mods_ref, mods_all_ref, send2, recv2, local_sems.at[1])
        for cp in stores:
            cp.wait()
        big.finish([shard_ref], [gathered_ref], big_send, big_recv)

    vm = pl.BlockSpec(memory_space=pltpu.VMEM)
    seven = pltpu.SemaphoreType.DMA((7,))
    res = pl.pallas_call(
        body, name=name, in_specs=[vm, vm, vm, ANY] + [ANY] * n_w, out_specs=[vm, vm, ANY] + [ANY] * n_w,
        out_shape=[jax.ShapeDtypeStruct((8, R, Cn), F32), jax.ShapeDtypeStruct((8, 8, n_mod), F32)] + big.out_shapes
        + [jax.ShapeDtypeStruct(w.shape, BF16) for w in to_cast],
        scratch_shapes=[pltpu.VMEM((8, n_mod), F32), seven, seven, seven, seven, pltpu.SemaphoreType.DMA((2,))]
        + [pltpu.VMEM(w.shape, F32) for w in to_cast] + [pltpu.VMEM(w.shape, BF16) for w in to_cast]
        + [pltpu.SemaphoreType.DMA((n_w,)), pltpu.SemaphoreType.DMA((n_w,))] + big.sem_shapes(),
        compiler_params=pltpu.CompilerParams(vmem_limit_bytes=VMEM_LIMIT_BYTES),
    )(small, w_ada, b_ada, w_shard, *to_cast)
    return res[0], res[1], res[2], list(res[3:])


def _symmetric_plan(copies):
    def plan(in_refs, out_refs):
        sends = [(src, dst, i, to) for i, (src, dst, to) in enumerate(copies(in_refs, out_refs))]
        return sends, [(dst, i) for _, dst, i, _ in sends]
    return plan


def _halves_exchange(grads):
    def copies(in_refs, out_refs):
        me = _place()
        return [(g.at[kk, 1 - me[2]], got.at[kk], _flip(me, 1)) for g, got in zip(in_refs, out_refs) for kk in range(N_CHIPS)]

    return _Exchange(grads, [jax.ShapeDtypeStruct((N_CHIPS,) + g.shape[2:], g.dtype) for g in grads],
                     N_CHIPS * len(grads), _symmetric_plan(copies))


def _chips_exchange(parts):
    def copies(in_refs, out_refs):
        me = _place()
        return [(p.at[_chip_index(_flip(me, 2 * j))], got.at[j - 1], _flip(me, 2 * j))
                for p, got in zip(in_refs, out_refs) for j in (1, 2, 3)]

    return _Exchange(parts, [jax.ShapeDtypeStruct((3,) + p.shape[1:], p.dtype) for p in parts], 3 * len(parts),
                     _symmetric_plan(copies))


def _siblings_exchange(halves):
    def copies(in_refs, out_refs):
        sibling = _flip(_place(), 1)
        return [(h, got, sibling) for h, got in zip(in_refs, out_refs)]

    return _Exchange(halves, [jax.ShapeDtypeStruct(h.shape, h.dtype) for h in halves], len(halves), _symmetric_plan(copies))


def _ici_gather_plan(n, base=0):
    def plan(in_refs, out_refs):
        me = _place()
        c = me[2]
        sends, lands = [], []
        for a, (w, out) in enumerate(zip(in_refs[:n], out_refs)):
            for j in (1, 2, 3):
                i = base + 3 * a + j - 1
                sends.append((w.at[c], out.at[_chip_index(me), c], i, _flip(me, 2 * j)))
                lands.append((out.at[_chip_index(_flip(me, 2 * j)), c], i))
        return sends, lands
    return plan


def _d2d_gather_plan(n, base=0):
    def plan(in_refs, out_refs):
        me = _place()
        c = me[2]
        sibling = _flip(me, 1)
        mine = _chip_index(me)
        sends, lands = [], []
        for a, (w, out) in enumerate(zip(in_refs[:n], out_refs)):
            moves = [(w.at[c], (mine, c)), (w.at[1 - c], (mine, 1 - c))]
            moves += [(out.at[_chip_index(_flip(me, 2 * j)), c], (_chip_index(_flip(me, 2 * j)), c)) for j in (1, 2, 3)]
            for k, (src, (chip, half)) in enumerate(moves):
                sends.append((src, out.at[chip, half], base + 5 * a + k, sibling))
            blocks = [(mine, 1 - c), (mine, c)] + [(_chip_index(_flip(me, 2 * j)), 1 - c) for j in (1, 2, 3)]
            lands += [(out.at[chip, half], base + 5 * a + k) for k, (chip, half) in enumerate(blocks)]
        return sends, lands
    return plan


def _gathered_shapes(shards):
    return [jax.ShapeDtypeStruct((N_CHIPS,) + s.shape, s.dtype) for s in shards]


def _gather_over_ici(shards):
    return _Exchange(shards, _gathered_shapes(shards), 3 * len(shards), _ici_gather_plan(len(shards)))


def _gather_over_d2d(shards, gathered):
    n = len(shards)
    return _Exchange(list(shards) + list(gathered), [jax.ShapeDtypeStruct(g.shape, g.dtype) for g in gathered], 5 * n,
                     _d2d_gather_plan(n), aliases={n + a: a for a in range(n)})


def _gather_in_one(shards):
    n = len(shards)
    return _Exchange(shards, _gathered_shapes(shards), 8 * n, _ici_gather_plan(n), then=_d2d_gather_plan(n, base=3 * n))


def _row_tile(rows):
    for cand in (256, 176, 128, 64, 32, 16, 8):
        if rows % cand == 0:
            return cand
    return rows


def _pair_sum(core, grad, theirs, name):
    N, _, R, Cn = grad.shape
    tr = R

    def body(core_ref, g_ref, t_ref, o_ref, ob_ref):
        s = g_ref[...] + t_ref[...]
        o_ref[...] = s
        ob_ref[...] = s.astype(BF16)

    out = pl.BlockSpec((None, tr, Cn), lambda k, i, core_ref: (k, i, 0))
    return pl.pallas_call(
        body, name=name,
        grid_spec=pltpu.PrefetchScalarGridSpec(
            num_scalar_prefetch=1, grid=(N, R // tr),
            in_specs=[pl.BlockSpec((None, None, tr, Cn), lambda k, i, core_ref: (k, core_ref[0], i, 0)),
                      pl.BlockSpec((None, tr, Cn), lambda k, i, core_ref: (k, i, 0))],
            out_specs=[out, out]),
        out_shape=[jax.ShapeDtypeStruct((N, R, Cn), F32), jax.ShapeDtypeStruct((N, R, Cn), BF16)],
        compiler_params=_params("parallel", "parallel"),
    )(core, grad, theirs)


def _chip_sum(chip, parts, landed, name):
    _, R, Cn = parts.shape
    tr = R

    def body(chip_ref, p_ref, l_ref, o_ref):
        o_ref[...] = ((p_ref[...] + l_ref[0].astype(F32)) + l_ref[1].astype(F32)) + l_ref[2].astype(F32)

    return pl.pallas_call(
        body, name=name,
        grid_spec=pltpu.PrefetchScalarGridSpec(
            num_scalar_prefetch=1, grid=(R // tr,),
            in_specs=[pl.BlockSpec((None, tr, Cn), lambda i, chip_ref: (chip_ref[0], i, 0)),
                      pl.BlockSpec((3, tr, Cn), lambda i, chip_ref: (0, i, 0))],
            out_specs=pl.BlockSpec((tr, Cn), lambda i, chip_ref: (i, 0))),
        out_shape=jax.ShapeDtypeStruct((R, Cn), F32), compiler_params=_params("parallel"),
    )(chip, parts, landed)


def _pair_sums(core, grads, theirs, tag):
    return [_pair_sum(core, g, t, f"{tag}_pair_sum_{i}") for i, (g, t) in enumerate(zip(grads, theirs))]


def _chip_sums(chip, parts, landed, tag):
    return [_chip_sum(chip, p[0], l, f"{tag}_chip_sum_{i}") for i, (p, l) in enumerate(zip(parts, landed))]


def _by_chip_rows(g):
    return g.reshape(N_CHIPS, 2, g.shape[0] // (2 * N_CHIPS), g.shape[1])


def _by_chip_cols(g):
    return g.reshape(N_CHIPS, 2, g.shape[1] // 2, g.shape[2])


def _adamw_halves(core, w, g_mine, g_theirs, m, v, name):
    R2, Cn = w.shape
    r = R2 // 2
    tr = _row_tile(r)
    nt = r // tr

    def body(core_ref, w_ref, gm_ref, gt_ref, m_ref, v_ref, g_ref, d_ref, nm_ref, nv_ref):
        gv = jnp.where(pl.program_id(0) == core_ref[0], gm_ref[...], gt_ref[...])
        g_ref[...] = gv
        m_new = ADAM_B1 * m_ref[...] + (1.0 - ADAM_B1) * gv
        v_new = ADAM_B2 * v_ref[...] + (1.0 - ADAM_B2) * (gv * gv)
        m_hat = m_new / (1.0 - ADAM_B1 ** ADAM_STEP)
        v_hat = v_new / (1.0 - ADAM_B2 ** ADAM_STEP)
        d_ref[...] = -ADAM_LR * (m_hat / (jnp.sqrt(v_hat) + ADAM_EPS) + ADAM_WD * w_ref[...])
        nm_ref[...] = m_new
        nv_ref[...] = v_new

    full = pl.BlockSpec((tr, Cn), lambda hf, i, core_ref: (hf * nt + i, 0))
    half = pl.BlockSpec((tr, Cn), lambda hf, i, core_ref: (i, 0))
    shp = jax.ShapeDtypeStruct((R2, Cn), F32)
    return pl.pallas_call(
        body, name=name,
        grid_spec=pltpu.PrefetchScalarGridSpec(
            num_scalar_prefetch=1, grid=(2, nt), in_specs=[full, half, half, full, full], out_specs=[full] * 4),
        out_shape=[shp] * 4, compiler_params=_params("parallel", "parallel"),
    )(core, w, g_mine, g_theirs, m, v)


def _pad_row(v, width):
    v = v.reshape(1, -1)
    return jnp.pad(v, ((0, 0), (0, width - v.shape[1])))


def _ffn1_forward(x, ng, shift, scale, gate, w_in_shard, w_in_partly, w_out_shard, gather, next_norm):
    (h,), (w_in4,) = _rmsmod_fwd(x, ng, shift, scale, "ffn1_norm", exchange=_gather_over_d2d([w_in_shard], [w_in_partly]))
    w_in4 = w_in4.reshape(N_CHIPS, D_MODEL, FF_SHARD)
    (zg, zu, a), (partly, (w_out4,)) = _ffn_in_fwd(
        h, w_in4, "ffn1_in", exchange=[_gather_over_ici(gather), _gather_in_one([w_out_shard])])
    w_out = w_out4.reshape(D_FF, D_MODEL)
    (x_new, f, h_next), gathered = _proj_out_fwd([a], w_out, x, gate, 0.5, "ffn1_out", next_norm=next_norm,
                                                 exchange=_gather_over_d2d(gather, partly))
    return x_new, (h, zg, zu, a, f), w_in4, w_out, gathered, h_next


def _ffn_backward(df, saved, w_in4, w_out, core, chip, tag, riding=None, norm=None, in_first=False):
    h, zg, zu, a = saved[:4]
    rode = None
    if riding:
        (dzg, dzu), rode = _dact_bwd(df, w_out, zg, zu, f"{tag}_dact", exchange=riding)
    else:
        dzg, dzu = _dact_bwd(df, w_out, zg, zu, f"{tag}_dact")

    def dw_out(exchange=None):
        outs = _wgrad(a, [df], df.shape[1], f"{tag}_dw_out", exchange=exchange)
        (dw,), landed = outs if exchange else (outs, None)
        return [_by_chip_rows(dw.reshape(a.shape[1], df.shape[1]))], landed

    def dw_in(exchange=None):
        outs = _wgrad(h, [dzg, dzu], FF_SHARD, f"{tag}_dw_in", exchange=exchange)
        (dw,), landed = outs if exchange else (outs, None)
        return [_by_chip_cols(dw.reshape(N_CHIPS, h.shape[1], FF_SHARD))], landed

    (first, tag_1), (second, tag_2) = ((dw_in, "in"), (dw_out, "out"))[::1 if in_first else -1]
    g_1, _ = first()
    g_2, theirs_1 = second(_halves_exchange(g_1))
    parts_1 = _pair_sums(core, g_1, theirs_1, f"{tag}_{tag_1}")
    dh_outs, (theirs_2, landed_1) = _ffn_in_dgrad(
        dzg, dzu, w_in4, f"{tag}_dh", norm=norm, exchange=[_halves_exchange(g_2), _chips_exchange([parts_1[0][1]])])
    parts_2 = _pair_sums(core, g_2, theirs_2, f"{tag}_{tag_2}")
    return dh_outs, parts_2, _chip_sums(chip, parts_1, landed_1, f"{tag}_{tag_1}"), rode


def kernel(x, c, w_ada, b_ada, norm_g, w_ffn1_in, w_ffn1_out, w_ffn2_in, w_ffn2_out, w_mix_in, w_mix_out, hgrn_lb, hgrn_norm_g, qk_norm_g, attn_sink, rel_bias, loss_target, m_w_ada, m_b_ada, m_norm_g, m_w_ffn1_in, m_w_ffn1_out, m_w_ffn2_in, m_w_ffn2_out, m_w_mix_in, m_w_mix_out, m_hgrn_lb, m_hgrn_norm_g, m_qk_norm_g, m_attn_sink, m_rel_bias, v_w_ada, v_b_ada, v_norm_g, v_w_ffn1_in, v_w_ffn1_out, v_w_ffn2_in, v_w_ffn2_out, v_w_mix_in, v_w_mix_out, v_hgrn_lb, v_hgrn_norm_g, v_qk_norm_g, v_attn_sink, v_rel_bias):
    D = D_MODEL
    S = x.shape[1]
    place = (lax.axis_index("x"), lax.axis_index("y"), lax.axis_index("c"))
    me, my_chip = _dev_index(place), _chip_index(place)
    x0 = x[0]
    target = loss_target[0]

    core_arr = jnp.reshape(place[2], (1,)).astype(jnp.int32)
    chip_arr = jnp.reshape(my_chip, (1,)).astype(jnp.int32)

    def halves(w):
        return w.reshape(2, w.shape[0] // 2, w.shape[1])

    small = jnp.concatenate([_pad_row(c, D), _pad_row(norm_g, D), _pad_row(hgrn_lb, D), jnp.zeros((5, D), F32)], axis=0)
    n_ada = w_ada.shape[2]
    b_mine = lax.dynamic_slice_in_dim(b_ada, my_chip * n_ada, n_ada, axis=1)
    w1_in_shard = halves(_to_bf16(w_ffn1_in[0], "w_ffn1_in_to_bf16"))
    small_all, mods_parts, w1_in_partly, shards = _prologue(
        small, w_ada[0], b_mine, w1_in_shard, [w_ffn1_out[0], w_mix_in[0], w_mix_out[0], w_ffn2_in[0], w_ffn2_out[0]], "prologue")
    w1_out_shard, mix_shards, ffn2_shards = halves(shards[0]), [halves(w) for w in shards[1:3]], [halves(w) for w in shards[3:5]]
    c_all = small_all[:, 0, :]
    by_chip = small_all[0::2]
    norm_g_full = by_chip[:, 1, :3 * 256].reshape(N_CHIPS, 3, 256).transpose(1, 0, 2).reshape(3, D)
    lb_raw = by_chip[:, 2, :2 * 2 * 128].reshape(N_CHIPS, 2, 2, 128).transpose(1, 2, 0, 3).reshape(2, 2, HG_WIDTH)
    lb_logit = lb_raw[:, 0, :] - lb_raw[:, 1, :]
    lb = jax.nn.sigmoid(lb_logit)
    one_minus_lb = jax.nn.sigmoid(-lb_logit)
    lb_f = jnp.stack([lb[0], one_minus_lb[0]])
    lb_b = jnp.stack([lb[1], one_minus_lb[1]])

    c_act_all = c_all * jax.nn.sigmoid(c_all)
    mods_all = mods_parts[0::2].transpose(1, 0, 2).reshape(8, N_MOD * D)
    mods = lax.dynamic_slice_in_dim(mods_all, me, 1, axis=0)
    sh1, sc1, g1, sh2, sc2, g2, sh3, sc3, g3 = [mods[:, i * D:(i + 1) * D] for i in range(N_MOD)]

    x1, saved1, w1_in, w1_out, gathered, h2 = _ffn1_forward(
        x0, norm_g_full[0:1], sh1, sc1, g1, w1_in_shard, w1_in_partly, w1_out_shard, mix_shards, (norm_g_full[1:2], sh2, sc2))
    wm_in = gathered[0].reshape(N_CHIPS, D, D_IN // N_CHIPS).transpose(1, 0, 2).reshape(D, D_IN)
    wm_out = gathered[1].reshape(D, D)

    z = _matmul_nn(h2, wm_in, F32, 256, "mix_in")
    (of, st_f, ob, st_b), gathered = _hgrn_fwd_both(z, (lb_f, lb_b), "hgrn_fwd", exchange=_gather_in_one(ffn2_shards))
    w2_in = gathered[0].reshape(N_CHIPS, D, FF_SHARD)
    w2_out = gathered[1].reshape(D_FF, D)
    o_h = _hgrn_post_fwd(of, ob, z, hgrn_norm_g, "hgrn_post")

    q_g, k_g = qk_norm_g[0, 0:1], qk_norm_g[0, 1:2]
    sink_b = jnp.broadcast_to(attn_sink.reshape(ATT_Q_HEADS, 1, 1), (ATT_Q_HEADS, 1, BLOCK))
    bias = _bias_table(rel_bias, "bias_table")
    o_a = _attn_fwd(z, q_g, k_g, sink_b, bias, "attn_fwd")
    x2, mixed, h3 = _proj_out_fwd([o_h, o_a], wm_out, x1, g2, 1.0, "mix_out", next_norm=(norm_g_full[2:3], sh3, sc3))

    zg3, zu3, a3 = _ffn_in_fwd(h3, w2_in, "ffn2_in")
    dx3, df3, dg3, sq_cols = _proj_out_loss(a3, w2_out, x2, g3, 0.5, target, "ffn2_out_loss")
    loss_mine = 0.5 * jnp.sum(sq_cols) / D

    (dx2, dsh3, dsc3, dng3, dmixed, dg2), parts2, mine2_out, _ = _ffn_backward(
        df3, (h3, zg3, zu3, a3), w2_in, w2_out, core_arr, chip_arr, "ffn2",
        norm=_NormBwd(x2, norm_g_full[2:3], sc3, dx3, below=(mixed, g2, 1.0)))

    (do_cat,) = _matmul_nt([dmixed], wm_out, ROW_TILE, "mix_out_dgrad")
    dwm_out = _wgrad_rows([o_h, o_a], dmixed, "mix_out_dw").reshape(D, D)

    do_sum, dgr, d_hnorm = _hgrn_post_bwd(do_cat, of, ob, z, hgrn_norm_g, "hgrn_post_bwd")
    (dq_f, dff, dv_f, doml_f), landed2 = _hgrn_bwd(z, lb_f, do_sum, st_f, 0, "hgrn_bwd_f",
                                                   exchange=_chips_exchange([p[1] for p in parts2]))
    mine2 = _chip_sums(chip_arr, parts2, landed2, "ffn2_in") + mine2_out
    (dhq, dfb, dhi, doml_b), theirs2 = _hgrn_bwd(z, lb_b, do_sum, st_b, 1, "hgrn_bwd_b", acc=(dq_f, dv_f),
                                                 exchange=_siblings_exchange(mine2))

    daq, dkw, dvw, ds_sum, dsink, dqg = _attn_bwd(z, q_g, k_g, sink_b, bias, do_cat, "attn_bwd")
    dkv, dkg = _attn_kv_reduce(dkw, dvw, z, k_g, "attn_kv_reduce")
    d_rel_bias = jnp.sum(_bias_grad(ds_sum, "bias_grad"), axis=-1).T
    dz = [dhq, dff, dfb, dhi, dgr, daq, dkv]
    dwm_in = _wgrad_pieces(h2, dz, 2 * KV_WIDTH, "mix_in_dw").transpose(1, 0, 2).reshape(D, D_IN)
    wide = D_IN // N_CHIPS
    grads_m = [_by_chip_cols(dwm_in.reshape(D, N_CHIPS, wide).transpose(1, 0, 2)), _by_chip_rows(dwm_out)]
    (dx1, dsh2, dsc2, dng2, df1, dg1), theirs_m = _matmul_nt(
        dz, wm_in, 256, "mix_in_dgrad", exchange=_halves_exchange(grads_m),
        norm=_NormBwd(x1, norm_g_full[1:2], sc2, dx2, below=(saved1[4], g1, 0.5)))
    parts_m = _pair_sums(core_arr, grads_m, theirs_m, "mix")

    (dh1,), parts1, mine1_in, landed_m = _ffn_backward(df1, saved1, w1_in, w1_out, core_arr, chip_arr, "ffn1",
                                                       riding=_chips_exchange([p[1] for p in parts_m]), in_first=True)
    mine_m = _chip_sums(chip_arr, parts_m, landed_m, "mix")
    (dx0, dsh1, dsc1, dng1), landed1 = _rmsmod_bwd(dh1, _NormBwd(x0, norm_g_full[0:1], sc1, dx1), "ffn1_norm_bwd",
                                                   exchange=_chips_exchange([p[1] for p in parts1]))
    mine1 = mine1_in + _chip_sums(chip_arr, parts1, landed1, "ffn1_out")
    theirs_1m = list(_run_exchange(_siblings_exchange(mine1 + mine_m), "siblings_exchange"))
    reduced = list(zip(mine1 + mine2 + mine_m, theirs_1m[:2] + list(theirs2) + theirs_1m[2:]))

    dlb = -jnp.concatenate([doml_f, doml_b], axis=0)
    dlb_raw = dlb * lb * one_minus_lb
    d_hgrn_lb = jnp.stack([dlb_raw, -dlb_raw], axis=1)
    d_qk = jnp.concatenate([jnp.sum(dqg, axis=0), jnp.sum(dkg, axis=0)], axis=0)
    dmods = jnp.concatenate([dsh1, dsc1, dg1, dsh2, dsc2, dg2, dsh3, dsc3, dg3], axis=0)
    packed = jnp.concatenate(
        [dmods, dng1, dng2, dng3, d_hgrn_lb.reshape(2, D), _pad_row(d_hnorm, D), _pad_row(d_qk, D),
         _pad_row(dsink[:, 0, 0], D), _pad_row(d_rel_bias, D), _pad_row(loss_mine, D)], axis=0)
    packed = jnp.pad(packed, ((0, 24 - packed.shape[0]), (0, 0)))
    packed_all, packed_sum = _allgather8(packed, "small_grads_allgather", reduce=True)
    dmods_all = packed_all[:, 0:N_MOD, :].reshape(8, N_MOD * D)
    g_b_ada = packed_sum[0:N_MOD].reshape(1, N_MOD * D)
    g_norm_full = packed_sum[9:12]
    g_norm_g = lax.dynamic_slice_in_dim(g_norm_full, my_chip * 256, 256, axis=1).reshape(1, 3, 256)
    g_hgrn_lb = lax.dynamic_slice_in_dim(packed_sum[12:14].reshape(2, 2, HG_WIDTH), my_chip * 128, 128, axis=2)
    g_hgrn_norm_g = packed_sum[14:15, :HG_WIDTH]
    g_qk_norm_g = packed_sum[15, :2 * ATT_HEAD_DIM].reshape(1, 2, ATT_HEAD_DIM)
    g_attn_sink = packed_sum[16:17, :ATT_Q_HEADS]
    g_rel_bias = packed_sum[17, :NUM_BUCKETS * ATT_Q_HEADS].reshape(NUM_BUCKETS, ATT_Q_HEADS)
    loss = packed_sum[18, 0]

    dm_mine = lax.dynamic_slice_in_dim(dmods_all, my_chip * n_ada, n_ada, axis=1)
    g_w_ada = _ada_wgrad(c_act_all.T, dm_mine, "ada_wgrad")[None]

    def big(w, g, m, v, name):
        d, nm, nv = _adamw(w[0], g[0], m[0], v[0], name)
        return d[None], nm[None], nv[None]

    def big_halves(w, g_pair, m, v, name):
        g, d, nm, nv = _adamw_halves(core_arr, w[0], g_pair[0], g_pair[1], m[0], v[0], name)
        return g[None], (d[None], nm[None], nv[None])

    g_w1_in, u_w1_in = big_halves(w_ffn1_in, reduced[0], m_w_ffn1_in, v_w_ffn1_in, "adamw_w_ffn1_in")
    g_w1_out, u_w1_out = big_halves(w_ffn1_out, reduced[1], m_w_ffn1_out, v_w_ffn1_out, "adamw_w_ffn1_out")
    g_w2_in, u_w2_in = big_halves(w_ffn2_in, reduced[2], m_w_ffn2_in, v_w_ffn2_in, "adamw_w_ffn2_in")
    g_w2_out, u_w2_out = big_halves(w_ffn2_out, reduced[3], m_w_ffn2_out, v_w_ffn2_out, "adamw_w_ffn2_out")
    g_wm_in, u_wm_in = big_halves(w_mix_in, reduced[4], m_w_mix_in, v_w_mix_in, "adamw_w_mix_in")
    g_wm_out, u_wm_out = big_halves(w_mix_out, reduced[5], m_w_mix_out, v_w_mix_out, "adamw_w_mix_out")

    smalls = [(b_ada, g_b_ada, m_b_ada, v_b_ada), (norm_g, g_norm_g, m_norm_g, v_norm_g), (hgrn_lb, g_hgrn_lb, m_hgrn_lb, v_hgrn_lb),
              (hgrn_norm_g, g_hgrn_norm_g, m_hgrn_norm_g, v_hgrn_norm_g), (qk_norm_g, g_qk_norm_g, m_qk_norm_g, v_qk_norm_g),
              (attn_sink, g_attn_sink, m_attn_sink, v_attn_sink), (rel_bias, g_rel_bias, m_rel_bias, v_rel_bias)]
    sizes = [t[0].size for t in smalls]
    total = sum(sizes)
    rows = -(-total // 128)
    rows = -(-rows // 8) * 8

    def pack(i):
        flat = jnp.concatenate([t[i].reshape(-1) for t in smalls])
        fill = 1.0 if i == 3 else 0.0
        return jnp.pad(flat, (0, rows * 128 - total), constant_values=fill).reshape(rows, 128)

    packed_out = _adamw(pack(0), pack(1), pack(2), pack(3), "adamw_small")

    def unpack(flat2d):
        flat = flat2d.reshape(-1)
        outs, off = [], 0
        for t, n in zip(smalls, sizes):
            outs.append(flat[off:off + n].reshape(t[0].shape))
            off += n
        return outs

    d_small, m_small, v_small = [unpack(t) for t in packed_out]

    upd = {
        "w_ada": big(w_ada, g_w_ada, m_w_ada, v_w_ada, "adamw_w_ada"),
        "w_ffn1_in": u_w1_in, "w_ffn1_out": u_w1_out, "w_ffn2_in": u_w2_in, "w_ffn2_out": u_w2_out,
        "w_mix_in": u_wm_in, "w_mix_out": u_wm_out,
    }
    small_names = ["b_ada", "norm_g", "hgrn_lb", "hgrn_norm_g", "qk_norm_g", "attn_sink", "rel_bias"]
    for i, nme in enumerate(small_names):
        upd[nme] = (d_small[i], m_small[i], v_small[i])
    grads = {
        "w_ada": g_w_ada, "b_ada": g_b_ada, "norm_g": g_norm_g, "w_ffn1_in": g_w1_in, "w_ffn1_out": g_w1_out,
        "w_ffn2_in": g_w2_in, "w_ffn2_out": g_w2_out, "w_mix_in": g_wm_in, "w_mix_out": g_wm_out, "hgrn_lb": g_hgrn_lb,
        "hgrn_norm_g": g_hgrn_norm_g, "qk_norm_g": g_qk_norm_g, "attn_sink": g_attn_sink, "rel_bias": g_rel_bias,
    }
    order = ["w_ada", "b_ada", "norm_g", "w_ffn1_in", "w_ffn1_out", "w_ffn2_in", "w_ffn2_out", "w_mix_in", "w_mix_out",
             "hgrn_lb", "hgrn_norm_g", "qk_norm_g", "attn_sink", "rel_bias"]
    return (loss, dx0[None], *[grads[k] for k in order], *[upd[k][0] for k in order], *[upd[k][1] for k in order],
            *[upd[k][2] for k in order])
```
